```python
import jax, jax.numpy as jnp
from jax import lax
import numpy as np

D_MODEL = 2048
BATCH = 8
SEQ = 4096
DEPTH = 2

N_A_LAYERS = DEPTH // 2
N_B_LAYERS = DEPTH - N_A_LAYERS

SSD_EXPAND = 2
SSD_D_INNER = SSD_EXPAND * D_MODEL
SSD_HEAD_DIM = 64
SSD_N_HEADS = SSD_D_INNER // SSD_HEAD_DIM
SSD_N_GROUPS = 8
SSD_HEADS_PER_GROUP = SSD_N_HEADS // SSD_N_GROUPS
SSD_D_STATE = 128
SSD_CONV_W = 4
SSD_CHUNK = 256
SSD_BC_DIM = SSD_N_GROUPS * SSD_D_STATE
SSD_CONV_DIM = SSD_D_INNER + 2 * SSD_BC_DIM
SSD_IN_DIM = SSD_D_INNER + SSD_CONV_DIM + SSD_N_HEADS

DIL_PATTERNS = ((128, 1), (512, 4), (2048, 16))
DIL_N_GROUPS = len(DIL_PATTERNS)
DIL_HEADS = 8
DIL_HEAD_DIM = 128
DIL_Q_WIDTH = DIL_N_GROUPS * DIL_HEADS * DIL_HEAD_DIM
DIL_OUT_WIDTH = DIL_HEADS * DIL_HEAD_DIM
DIL_IN_DIM = DIL_Q_WIDTH + DIL_OUT_WIDTH
DIL_KV_DIM = 2 * DIL_Q_WIDTH
DIL_BLOCK = 128

DEEPNORM_ALPHA = (2 * DEPTH) ** 0.25
DEEPNORM_BETA = (8 * DEPTH) ** -0.25
LN_EPS = 1e-5
RMS_EPS = 1e-5

kernel_name = "hybrid_yoco_ssd_dilated_alibi_deepnorm"


def _layer_norm(x, g, b):
    xf = x.astype(jnp.float32)
    mu = jnp.mean(xf, -1, keepdims=True)
    var = jnp.mean(jnp.square(xf - mu), -1, keepdims=True)
    return ((xf - mu) * lax.rsqrt(var + LN_EPS)).astype(x.dtype) * g + b


def _adaln(c, w, b):
    mod = jax.nn.silu(c) @ w + b
    shift, scale, gate = jnp.split(mod, 3, axis=-1)
    return shift[:, None, :], scale[:, None, :], gate[:, None, :]


def _causal_depthwise_conv(x, w, b):
    y = lax.conv_general_dilated(
        x, w[:, None, :], window_strides=(1,), padding=[(SSD_CONV_W - 1, 0)],
        dimension_numbers=("NWC", "WIO", "NWC"), feature_group_count=x.shape[-1])
    return y + b


def _ssd_chunked(xdt, dtA, Bm, Cm):
    f32 = jnp.float32
    bsz, L = xdt.shape[:2]
    G, K, P, N = SSD_N_GROUPS, SSD_HEADS_PER_GROUP, SSD_HEAD_DIM, SSD_D_STATE
    Lp = -(-L // SSD_CHUNK) * SSD_CHUNK
    nc = Lp // SSD_CHUNK

    def chunks(a):
        a = jnp.pad(a, [(0, 0), (0, Lp - L)] + [(0, 0)] * (a.ndim - 2))
        a = a.reshape((bsz, nc, SSD_CHUNK) + a.shape[2:])
        return jnp.moveaxis(a, 1, 0)

    xs = chunks(xdt.reshape(bsz, L, G, K, P).astype(f32))
    As = chunks(dtA.reshape(bsz, L, G, K).astype(f32))
    Bs = chunks(Bm.astype(f32))
    Cs = chunks(Cm.astype(f32))
    causal = jnp.tril(jnp.ones((SSD_CHUNK, SSD_CHUNK), bool))[None, :, :, None, None]

    def step(state, inp):
        xc, ac, bc, cc = inp
        acum = jnp.cumsum(ac, axis=1)
        seg = acum[:, :, None] - acum[:, None, :]
        decay = jnp.exp(jnp.where(causal, seg, -jnp.inf))
        cb = jnp.einsum("blgn,bsgn->blsg", cc, bc)
        y_diag = jnp.einsum("blsgk,bsgkp->blgkp", cb[..., None] * decay, xc)
        y_off = jnp.einsum("blgn,bgkpn->blgkp", cc, state) * jnp.exp(acum)[..., None]
        tail = jnp.exp(acum[:, -1:] - acum)
        new_state = (state * jnp.exp(acum[:, -1])[..., None, None]
                     + jnp.einsum("bsgn,bsgkp->bgkpn", bc, xc * tail[..., None]))
        return new_state, y_diag + y_off

    state0 = jnp.zeros((bsz, G, K, P, N), f32)
    _, ys = lax.scan(step, state0, (xs, As, Bs, Cs))
    ys = jnp.moveaxis(ys, 0, 1).reshape(bsz, Lp, SSD_N_HEADS * P)
    return ys[:, :L]


def _ssd_mixer(h, in_w, conv_w, conv_b, dt_bias, A_log, D_skip, norm_g, out_w):
    bsz, L, _ = h.shape
    proj = h @ in_w
    z, xBC, dt = jnp.split(proj, [SSD_D_INNER, SSD_D_INNER + SSD_CONV_DIM], axis=-1)
    xBC = jax.nn.silu(_causal_depthwise_conv(xBC, conv_w, conv_b))
    xs, Bm, Cm = jnp.split(xBC, [SSD_D_INNER, SSD_D_INNER + SSD_BC_DIM], axis=-1)
    xs = xs.reshape(bsz, L, SSD_N_HEADS, SSD_HEAD_DIM)
    Bm = Bm.reshape(bsz, L, SSD_N_GROUPS, SSD_D_STATE)
    Cm = Cm.reshape(bsz, L, SSD_N_GROUPS, SSD_D_STATE)
    dt = jax.nn.softplus((dt + dt_bias).astype(jnp.float32))
    A = -jnp.exp(A_log.astype(jnp.float32))
    y = _ssd_chunked(xs * dt[..., None], dt * A, Bm, Cm)
    y = y + (xs * D_skip[:, None]).reshape(bsz, L, SSD_D_INNER)
    y = y * jax.nn.silu(z.astype(jnp.float32))
    yg = y.reshape(bsz, L, SSD_N_GROUPS, -1)
    yg = yg * lax.rsqrt(jnp.mean(jnp.square(yg), -1, keepdims=True) + RMS_EPS)
    y = yg.reshape(bsz, L, SSD_D_INNER).astype(h.dtype) * norm_g
    return y @ out_w


def _alibi_slopes():
    n = DIL_N_GROUPS * DIL_HEADS
    s = 2.0 ** (-8.0 * np.arange(1, n + 1) / n)
    return jnp.asarray(s.reshape(DIL_N_GROUPS, DIL_HEADS), dtype=jnp.float32)


def _dilated_window_attention(q, k, v, window, dilation, slopes):
    f32 = jnp.float32
    bsz, L, H, E = q.shape
    span = window // dilation
    M = -(-L // (dilation * DIL_BLOCK)) * DIL_BLOCK
    nb = M // DIL_BLOCK
    pad = M * dilation - L

    def to_blocks(a):
        a = jnp.pad(a, [(0, 0), (0, pad), (0, 0), (0, 0)])
        a = a.reshape(bsz, M, dilation, H, E).transpose(0, 2, 1, 3, 4)
        return a.reshape(bsz, dilation, nb, DIL_BLOCK, H, E)

    def with_prev(a):
        prev = jnp.pad(a[:, :, :-1], [(0, 0), (0, 0), (1, 0), (0, 0), (0, 0), (0, 0)])
        return jnp.concatenate([prev, a], axis=3)

    qb = to_blocks(q)
    kb = with_prev(to_blocks(k))
    vb = with_prev(to_blocks(v))
    s = jnp.einsum("brnqhe,brnkhe->brnhqk", qb, kb,
                   preferred_element_type=f32) * (E ** -0.5)
    qi = jnp.arange(DIL_BLOCK)[:, None]
    kj = jnp.arange(2 * DIL_BLOCK)[None, :]
    delta = qi + DIL_BLOCK - kj
    valid = (delta >= 0) & (delta <= span)
    first = (jnp.arange(nb) == 0)[:, None, None]
    valid = valid[None] & ~(first & (kj < DIL_BLOCK)[None])
    alibi = -slopes[:, None, None] * (delta * dilation).astype(f32)[None]
    s = jnp.where(valid[None, None, :, None], s + alibi[None, None, None], -jnp.inf)
    m = jnp.max(s, -1, keepdims=True)
    p = jnp.exp(s - m)
    den = jnp.sum(p, -1)
    o = jnp.einsum("brnhqk,brnkhe->brnqhe", p, vb.astype(f32))
    o = o / jnp.moveaxis(den, 3, 4)[..., None]
    lse = jnp.moveaxis(m[..., 0] + jnp.log(den), 3, 4)

    def from_blocks(a):
        a = a.reshape((bsz, dilation, M) + a.shape[4:])
        a = jnp.moveaxis(a, 1, 2).reshape((bsz, M * dilation) + a.shape[3:])
        return a[:, :L]

    return from_blocks(o), from_blocks(lse)


def _shared_kv(x, kv_w):
    bsz, L, _ = x.shape
    k, v = jnp.split(x @ kv_w, 2, axis=-1)
    shp = (bsz, L, DIL_N_GROUPS, DIL_HEADS, DIL_HEAD_DIM)
    return k.reshape(shp), v.reshape(shp)


def _dilated_mixer(h, k_sh, v_sh, in_w, out_w):
    bsz, L, _ = h.shape
    q, z = jnp.split(h @ in_w, [DIL_Q_WIDTH], axis=-1)
    q = q.reshape(bsz, L, DIL_N_GROUPS, DIL_HEADS, DIL_HEAD_DIM)
    slopes = _alibi_slopes()
    outs, lses = [], []
    for g, (window, dilation) in enumerate(DIL_PATTERNS):
        o, lse = _dilated_window_attention(q[:, :, g], k_sh[:, :, g], v_sh[:, :, g],
                                           window, dilation, slopes[g])
        outs.append(o)
        lses.append(lse)
    o = jnp.stack(outs, 2)
    wts = jax.nn.softmax(jnp.stack(lses, 2), axis=2)
    o = jnp.einsum("blghe,blgh->blhe", o, wts).reshape(bsz, L, DIL_OUT_WIDTH)
    o = o.astype(h.dtype) * jax.nn.silu(z)
    return o @ out_w


def _fwd_setup_inputs(seed: int = 0) -> dict:
    key = jax.random.key(seed)
    ks = jax.random.split(key, 20)
    f32 = jnp.float32
    nA, nB, D = N_A_LAYERS, N_B_LAYERS, D_MODEL
    nrm = lambda k, shp, sc: jax.random.normal(k, shp, f32) * sc
    dt0 = jnp.exp(jax.random.uniform(ks[8], (nA, SSD_N_HEADS), f32,
                                     np.log(1e-3), np.log(1e-1)))
    return {
        "x": nrm(ks[0], (BATCH, SEQ, D), 1.0),
        "c": nrm(ks[1], (BATCH, D), 1.0),
        "ada_w": nrm(ks[2], (DEPTH, D, 3 * D), 0.1 * D ** -0.5),
        "ada_b": nrm(ks[3], (DEPTH, 3 * D), 0.01),
        "ln_g": 1.0 + nrm(ks[4], (DEPTH, D), 0.01),
        "ln_b": nrm(ks[5], (DEPTH, D), 0.01),
        "a_in_w": nrm(ks[6], (nA, D, SSD_IN_DIM), D ** -0.5),
        "a_conv_w": nrm(ks[7], (nA, SSD_CONV_W, SSD_CONV_DIM), SSD_CONV_W ** -0.5),
        "a_conv_b": nrm(ks[9], (nA, SSD_CONV_DIM), 0.01),
        "a_dt_bias": dt0 + jnp.log(-jnp.expm1(-dt0)),
        "a_A_log": jnp.log(jax.random.uniform(ks[10], (nA, SSD_N_HEADS), f32, 1.0, 16.0)),
        "a_D": 1.0 + nrm(ks[11], (nA, SSD_N_HEADS), 0.01),
        "a_norm_g": 1.0 + nrm(ks[12], (nA, SSD_D_INNER), 0.01),
        "a_out_w": nrm(ks[13], (nA, SSD_D_INNER, D), DEEPNORM_BETA * SSD_D_INNER ** -0.5),
        "kv_w": nrm(ks[14], (D, DIL_KV_DIM), D ** -0.5),
        "b_in_w": nrm(ks[15], (nB, D, DIL_IN_DIM), D ** -0.5),
        "b_out_w": nrm(ks[16], (nB, DIL_OUT_WIDTH, D), DEEPNORM_BETA * DIL_OUT_WIDTH ** -0.5),
    }


def _fwd_reference(x, c, ada_w, ada_b, ln_g, ln_b, a_in_w, a_conv_w, a_conv_b, a_dt_bias,
              a_A_log, a_D, a_norm_g, a_out_w, kv_w, b_in_w, b_out_w):
    k_sh, v_sh = None, None
    for layer in range(DEPTH):
        shift, scale, gate = _adaln(c, ada_w[layer], ada_b[layer])
        h = x * (1.0 + scale) + shift
        if layer < N_A_LAYERS:
            i = layer
            y = _ssd_mixer(h, a_in_w[i], a_conv_w[i], a_conv_b[i], a_dt_bias[i],
                           a_A_log[i], a_D[i], a_norm_g[i], a_out_w[i])
        else:
            i = layer - N_A_LAYERS
            y = _dilated_mixer(h, k_sh, v_sh, b_in_w[i], b_out_w[i])
        x = _layer_norm(DEEPNORM_ALPHA * x + (1.0 + gate) * y, ln_g[layer], ln_b[layer])
        if layer == N_A_LAYERS - 1:
            k_sh, v_sh = _shared_kv(x, kv_w)
    return x


import jax as _jax
import jax.numpy as _jnp

TWIN_FORMAT = 'train_step'
FWD_PARAMS = ['x', 'c', 'ada_w', 'ada_b', 'ln_g', 'ln_b', 'a_in_w', 'a_conv_w', 'a_conv_b', 'a_dt_bias', 'a_A_log', 'a_D', 'a_norm_g', 'a_out_w', 'kv_w', 'b_in_w', 'b_out_w']
TWIN_WEIGHTS = ['ada_w', 'ada_b', 'ln_g', 'ln_b', 'a_in_w', 'a_conv_w', 'a_conv_b', 'a_dt_bias', 'a_A_log', 'a_D', 'a_norm_g', 'a_out_w', 'kv_w', 'b_in_w', 'b_out_w']
TWIN_DIFF_INPUT = 'x'
TWIN_INPUTS = ['x', 'c', 'ada_w', 'ada_b', 'ln_g', 'ln_b', 'a_in_w', 'a_conv_w', 'a_conv_b', 'a_dt_bias', 'a_A_log', 'a_D', 'a_norm_g', 'a_out_w', 'kv_w', 'b_in_w', 'b_out_w', 'loss_target', 'm_ada_w', 'm_ada_b', 'm_ln_g', 'm_ln_b', 'm_a_in_w', 'm_a_conv_w', 'm_a_conv_b', 'm_a_dt_bias', 'm_a_A_log', 'm_a_D', 'm_a_norm_g', 'm_a_out_w', 'm_kv_w', 'm_b_in_w', 'm_b_out_w', 'v_ada_w', 'v_ada_b', 'v_ln_g', 'v_ln_b', 'v_a_in_w', 'v_a_conv_w', 'v_a_conv_b', 'v_a_dt_bias', 'v_a_A_log', 'v_a_D', 'v_a_norm_g', 'v_a_out_w', 'v_kv_w', 'v_b_in_w', 'v_b_out_w']
TWIN_OUTPUTS = ['loss', 'grad_x', 'grad_ada_w', 'grad_ada_b', 'grad_ln_g', 'grad_ln_b', 'grad_a_in_w', 'grad_a_conv_w', 'grad_a_conv_b', 'grad_a_dt_bias', 'grad_a_A_log', 'grad_a_D', 'grad_a_norm_g', 'grad_a_out_w', 'grad_kv_w', 'grad_b_in_w', 'grad_b_out_w', 'delta_ada_w', 'delta_ada_b', 'delta_ln_g', 'delta_ln_b', 'delta_a_in_w', 'delta_a_conv_w', 'delta_a_conv_b', 'delta_a_dt_bias', 'delta_a_A_log', 'delta_a_D', 'delta_a_norm_g', 'delta_a_out_w', 'delta_kv_w', 'delta_b_in_w', 'delta_b_out_w', 'new_m_ada_w', 'new_m_ada_b', 'new_m_ln_g', 'new_m_ln_b', 'new_m_a_in_w', 'new_m_a_conv_w', 'new_m_a_conv_b', 'new_m_a_dt_bias', 'new_m_a_A_log', 'new_m_a_D', 'new_m_a_norm_g', 'new_m_a_out_w', 'new_m_kv_w', 'new_m_b_in_w', 'new_m_b_out_w', 'new_v_ada_w', 'new_v_ada_b', 'new_v_ln_g', 'new_v_ln_b', 'new_v_a_in_w', 'new_v_a_conv_w', 'new_v_a_conv_b', 'new_v_a_dt_bias', 'new_v_a_A_log', 'new_v_a_D', 'new_v_a_norm_g', 'new_v_a_out_w', 'new_v_kv_w', 'new_v_b_in_w', 'new_v_b_out_w']
TWIN_LEAF_KINDS = {'loss': 'loss', 'grad_x': 'grad_x', 'grad_ada_w': 'grad_w', 'grad_ada_b': 'grad_w', 'grad_ln_g': 'grad_w', 'grad_ln_b': 'grad_w', 'grad_a_in_w': 'grad_w', 'grad_a_conv_w': 'grad_w', 'grad_a_conv_b': 'grad_w', 'grad_a_dt_bias': 'grad_w', 'grad_a_A_log': 'grad_w', 'grad_a_D': 'grad_w', 'grad_a_norm_g': 'grad_w', 'grad_a_out_w': 'grad_w', 'grad_kv_w': 'grad_w', 'grad_b_in_w': 'grad_w', 'grad_b_out_w': 'grad_w', 'delta_ada_w': 'delta_w', 'delta_ada_b': 'delta_w', 'delta_ln_g': 'delta_w', 'delta_ln_b': 'delta_w', 'delta_a_in_w': 'delta_w', 'delta_a_conv_w': 'delta_w', 'delta_a_conv_b': 'delta_w', 'delta_a_dt_bias': 'delta_w', 'delta_a_A_log': 'delta_w', 'delta_a_D': 'delta_w', 'delta_a_norm_g': 'delta_w', 'delta_a_out_w': 'delta_w', 'delta_kv_w': 'delta_w', 'delta_b_in_w': 'delta_w', 'delta_b_out_w': 'delta_w', 'new_m_ada_w': 'new_m', 'new_m_ada_b': 'new_m', 'new_m_ln_g': 'new_m', 'new_m_ln_b': 'new_m', 'new_m_a_in_w': 'new_m', 'new_m_a_conv_w': 'new_m', 'new_m_a_conv_b': 'new_m', 'new_m_a_dt_bias': 'new_m', 'new_m_a_A_log': 'new_m', 'new_m_a_D': 'new_m', 'new_m_a_norm_g': 'new_m', 'new_m_a_out_w': 'new_m', 'new_m_kv_w': 'new_m', 'new_m_b_in_w': 'new_m', 'new_m_b_out_w': 'new_m', 'new_v_ada_w': 'new_v', 'new_v_ada_b': 'new_v', 'new_v_ln_g': 'new_v', 'new_v_ln_b': 'new_v', 'new_v_a_in_w': 'new_v', 'new_v_a_conv_w': 'new_v', 'new_v_a_conv_b': 'new_v', 'new_v_a_dt_bias': 'new_v', 'new_v_a_A_log': 'new_v', 'new_v_a_D': 'new_v', 'new_v_a_norm_g': 'new_v', 'new_v_a_out_w': 'new_v', 'new_v_kv_w': 'new_v', 'new_v_b_in_w': 'new_v', 'new_v_b_out_w': 'new_v'}


def _forward(args):
    return _fwd_reference(*[args[k] for k in FWD_PARAMS])


def _output_shape():
    def fwd():
        inp = _fwd_setup_inputs(0)
        return _fwd_reference(*[inp[k] for k in FWD_PARAMS])
    out = _jax.eval_shape(fwd)
    return out.shape, out.dtype

N_MICROBATCH = 1
ADAM_LR = 0.001
ADAM_B1 = 0.9
ADAM_B2 = 0.999
ADAM_EPS = 1e-08
ADAM_WD = 0.01
ADAM_STEP = 10
PER_EXAMPLE_BATCH_AXIS = {'x': 0, 'c': 0, 'loss_target': 0}
SHARED_INPUTS = []
_WEIGHT_DTYPES = {'ada_w': _jnp.float32, 'ada_b': _jnp.float32, 'ln_g': _jnp.float32, 'ln_b': _jnp.float32, 'a_in_w': _jnp.float32, 'a_conv_w': _jnp.float32, 'a_conv_b': _jnp.float32, 'a_dt_bias': _jnp.float32, 'a_A_log': _jnp.float32, 'a_D': _jnp.float32, 'a_norm_g': _jnp.float32, 'a_out_w': _jnp.float32, 'kv_w': _jnp.float32, 'b_in_w': _jnp.float32, 'b_out_w': _jnp.float32}
MOMENT_SCALE = {'ada_w': 1.926844e-02, 'ada_b': 3.460211e-02, 'ln_g': 1.130489e+01, 'ln_b': 7.183403e-01, 'a_in_w': 1.971026e-02, 'a_conv_w': 1.830726e-02, 'a_conv_b': 2.874263e-02, 'a_dt_bias': 3.846289e-02, 'a_A_log': 7.402253e-02, 'a_D': 1.143452e-01, 'a_norm_g': 2.248121e-02, 'a_out_w': 6.120548e-02, 'kv_w': 4.763243e-03, 'b_in_w': 5.956312e-03, 'b_out_w': 1.275172e-02}


def _to_microbatches(a, axis):
    t = _jnp.moveaxis(a, axis, 0)
    t = t.reshape((N_MICROBATCH, t.shape[0] // N_MICROBATCH) + t.shape[1:])
    return _jnp.moveaxis(t, 1, axis + 1)


def setup_inputs(seed: int = 0) -> dict:
    inp = _fwd_setup_inputs(seed)
    key = _jax.random.fold_in(_jax.random.key(seed), 7919)
    shape, _ = _output_shape()
    out = dict(inp)
    out["loss_target"] = _jax.random.normal(_jax.random.fold_in(key, 0), shape, _jnp.float32)
    for i, name in enumerate(TWIN_WEIGHTS):
        w = inp[name].astype(_jnp.float32)
        if MOMENT_SCALE is None:
            s = _jnp.sqrt(_jnp.mean(_jnp.square(w)) + 1e-30)
        else:
            s = MOMENT_SCALE[name]
        km, kv = _jax.random.split(_jax.random.fold_in(key, i + 1))
        out[name] = w
        out["m_" + name] = s * _jax.random.normal(km, w.shape, _jnp.float32)
        out["v_" + name] = (s * s) * _jax.random.uniform(kv, w.shape, _jnp.float32, 0.5, 1.5)
    if N_MICROBATCH > 1:
        for name, axis in PER_EXAMPLE_BATCH_AXIS.items():
            out[name] = _to_microbatches(out[name], axis)
    return {'x': out['x'], 'c': out['c'], 'ada_w': out['ada_w'], 'ada_b': out['ada_b'], 'ln_g': out['ln_g'], 'ln_b': out['ln_b'], 'a_in_w': out['a_in_w'], 'a_conv_w': out['a_conv_w'], 'a_conv_b': out['a_conv_b'], 'a_dt_bias': out['a_dt_bias'], 'a_A_log': out['a_A_log'], 'a_D': out['a_D'], 'a_norm_g': out['a_norm_g'], 'a_out_w': out['a_out_w'], 'kv_w': out['kv_w'], 'b_in_w': out['b_in_w'], 'b_out_w': out['b_out_w'], 'loss_target': out['loss_target'], 'm_ada_w': out['m_ada_w'], 'm_ada_b': out['m_ada_b'], 'm_ln_g': out['m_ln_g'], 'm_ln_b': out['m_ln_b'], 'm_a_in_w': out['m_a_in_w'], 'm_a_conv_w': out['m_a_conv_w'], 'm_a_conv_b': out['m_a_conv_b'], 'm_a_dt_bias': out['m_a_dt_bias'], 'm_a_A_log': out['m_a_A_log'], 'm_a_D': out['m_a_D'], 'm_a_norm_g': out['m_a_norm_g'], 'm_a_out_w': out['m_a_out_w'], 'm_kv_w': out['m_kv_w'], 'm_b_in_w': out['m_b_in_w'], 'm_b_out_w': out['m_b_out_w'], 'v_ada_w': out['v_ada_w'], 'v_ada_b': out['v_ada_b'], 'v_ln_g': out['v_ln_g'], 'v_ln_b': out['v_ln_b'], 'v_a_in_w': out['v_a_in_w'], 'v_a_conv_w': out['v_a_conv_w'], 'v_a_conv_b': out['v_a_conv_b'], 'v_a_dt_bias': out['v_a_dt_bias'], 'v_a_A_log': out['v_a_A_log'], 'v_a_D': out['v_a_D'], 'v_a_norm_g': out['v_a_norm_g'], 'v_a_out_w': out['v_a_out_w'], 'v_kv_w': out['v_kv_w'], 'v_b_in_w': out['v_b_in_w'], 'v_b_out_w': out['v_b_out_w']}


def _loss(weights, diff, rest, loss_target):
    with _jax.named_scope("forward"):
        args = {**rest, TWIN_DIFF_INPUT: diff, **{k: w.astype(_WEIGHT_DTYPES[k]) for k, w in weights.items()}}
        y = _forward(args)
    with _jax.named_scope("loss_head"):
        err = _jnp.square(y.astype(_jnp.float32) - loss_target)
        return 0.5 * _jnp.sum(_jnp.mean(err, axis=-1)) if err.ndim else 0.5 * err


def _adamw(w, g, m, v):
    m = ADAM_B1 * m + (1.0 - ADAM_B1) * g
    v = ADAM_B2 * v + (1.0 - ADAM_B2) * _jnp.square(g)
    m_hat = m / (1.0 - ADAM_B1 ** ADAM_STEP)
    v_hat = v / (1.0 - ADAM_B2 ** ADAM_STEP)
    delta = -ADAM_LR * (m_hat / (_jnp.sqrt(v_hat) + ADAM_EPS) + ADAM_WD * w)
    return delta, m, v


def reference(x, c, ada_w, ada_b, ln_g, ln_b, a_in_w, a_conv_w, a_conv_b, a_dt_bias, a_A_log, a_D, a_norm_g, a_out_w, kv_w, b_in_w, b_out_w, loss_target, m_ada_w, m_ada_b, m_ln_g, m_ln_b, m_a_in_w, m_a_conv_w, m_a_conv_b, m_a_dt_bias, m_a_A_log, m_a_D, m_a_norm_g, m_a_out_w, m_kv_w, m_b_in_w, m_b_out_w, v_ada_w, v_ada_b, v_ln_g, v_ln_b, v_a_in_w, v_a_conv_w, v_a_conv_b, v_a_dt_bias, v_a_A_log, v_a_D, v_a_norm_g, v_a_out_w, v_kv_w, v_b_in_w, v_b_out_w):
    given = dict(x=x, c=c, ada_w=ada_w, ada_b=ada_b, ln_g=ln_g, ln_b=ln_b, a_in_w=a_in_w, a_conv_w=a_conv_w, a_conv_b=a_conv_b, a_dt_bias=a_dt_bias, a_A_log=a_A_log, a_D=a_D, a_norm_g=a_norm_g, a_out_w=a_out_w, kv_w=kv_w, b_in_w=b_in_w, b_out_w=b_out_w, loss_target=loss_target, m_ada_w=m_ada_w, m_ada_b=m_ada_b, m_ln_g=m_ln_g, m_ln_b=m_ln_b, m_a_in_w=m_a_in_w, m_a_conv_w=m_a_conv_w, m_a_conv_b=m_a_conv_b, m_a_dt_bias=m_a_dt_bias, m_a_A_log=m_a_A_log, m_a_D=m_a_D, m_a_norm_g=m_a_norm_g, m_a_out_w=m_a_out_w, m_kv_w=m_kv_w, m_b_in_w=m_b_in_w, m_b_out_w=m_b_out_w, v_ada_w=v_ada_w, v_ada_b=v_ada_b, v_ln_g=v_ln_g, v_ln_b=v_ln_b, v_a_in_w=v_a_in_w, v_a_conv_w=v_a_conv_w, v_a_conv_b=v_a_conv_b, v_a_dt_bias=v_a_dt_bias, v_a_A_log=v_a_A_log, v_a_D=v_a_D, v_a_norm_g=v_a_norm_g, v_a_out_w=v_a_out_w, v_kv_w=v_kv_w, v_b_in_w=v_b_in_w, v_b_out_w=v_b_out_w)
    weights = {n: given[n] for n in TWIN_WEIGHTS}
    shared = {n: given[n] for n in SHARED_INPUTS}
    per_example = {n: given[n] for n in ['x', 'c']}
    grad_fn = _jax.value_and_grad(_loss, argnums=(0, 1))

    def one_microbatch(ex, loss_target):
        ex = dict(ex)
        diff = ex.pop(TWIN_DIFF_INPUT)
        return grad_fn(weights, diff, {**shared, **ex}, loss_target)

    if N_MICROBATCH == 1:
        loss, (grad_w, grad_x) = one_microbatch(per_example, given["loss_target"])
    else:
        def body(carry, xs):
            loss_sum, grad_sum = carry
            l_k, (gw_k, gx_k) = one_microbatch(xs[0], xs[1])
            with _jax.named_scope("update"):
                return (loss_sum + l_k, _jax.tree.map(_jnp.add, grad_sum, gw_k)), gx_k

        init = (_jnp.zeros((), _jnp.float32), _jax.tree.map(_jnp.zeros_like, weights))
        (loss, grad_w), grad_x = _jax.lax.scan(body, init, (per_example, given["loss_target"]))
    with _jax.named_scope("update"):
        delta_w, new_m, new_v = {}, {}, {}
        for n in TWIN_WEIGHTS:
            delta_w[n], new_m[n], new_v[n] = _adamw(weights[n], grad_w[n], given["m_" + n], given["v_" + n])
    return (loss, grad_x, *[grad_w[n] for n in TWIN_WEIGHTS], *[delta_w[n] for n in TWIN_WEIGHTS],
            *[new_m[n] for n in TWIN_WEIGHTS], *[new_v[n] for n in TWIN_WEIGHTS])
```

```python
import functools

import jax
import jax.numpy as jnp
import numpy as np
from jax import lax
from jax.experimental import pallas as pl
from jax.experimental.pallas import tpu as pltpu

F32 = jnp.float32
BF16 = jnp.bfloat16
MESH = pl.DeviceIdType.MESH

DEPTH = 2
ALPHA = (2 * DEPTH) ** 0.25
LN_EPS = 1e-5
RMS_EPS = 1e-5
SSD_P = 64
SSD_N = 128
SSD_Q = 256
SSD_G = 8
CONV_W = 4
DIL_PATTERNS = ((128, 1), (512, 4), (2048, 16))
DIL_H = 8
DIL_E = 128
DIL_BLK = 128
ADAM_LR, ADAM_B1, ADAM_B2, ADAM_EPS, ADAM_WD, ADAM_STEP = 0.001, 0.9, 0.999, 1e-08, 0.01, 10

VMEM_LIMIT = 56 * 1024 * 1024
N_CHIPS = 4
N_DEV = 8


def _tile(dim, target, mult=128):
    if dim <= target:
        return dim
    t = (target // mult) * mult
    while t >= mult:
        if dim % t == 0:
            return t
        t -= mult
    return dim


def _cp(sem):
    return pltpu.CompilerParams(dimension_semantics=sem, vmem_limit_bytes=VMEM_LIMIT)


def _sigmoid(x):
    return 1.0 / (1.0 + jnp.exp(-x))


def _silu(x):
    return x * _sigmoid(x)


def _dsilu(x):
    s = _sigmoid(x)
    return s * (1.0 + x * (1.0 - s))


def _softplus(x):
    return jnp.maximum(x, 0.0) + jnp.log(1.0 + jnp.exp(-jnp.abs(x)))


def _mm_call(a, b, out_shape, grid, a_spec, b_spec, o_spec, acc_shape, dims, name):
    nk = grid[2]

    def body(a_ref, b_ref, o_ref, acc_ref):
        k = pl.program_id(2)

        @pl.when(k == 0)
        def _():
            acc_ref[...] = jnp.zeros(acc_ref.shape, F32)

        acc_ref[...] += lax.dot_general(a_ref[...].astype(BF16), b_ref[...].astype(BF16), (dims, ((), ())),
                                        preferred_element_type=F32)

        @pl.when(k == nk - 1)
        def _():
            o_ref[...] = acc_ref[...].astype(o_ref.dtype)

    return pl.pallas_call(
        body, grid=grid, in_specs=[a_spec, b_spec], out_specs=o_spec, out_shape=out_shape,
        scratch_shapes=[pltpu.VMEM(acc_shape, F32)],
        compiler_params=_cp(("parallel", "parallel", "arbitrary")), name=name)(a, b)


def mm_nn(a, b, out_dtype, name, stack=None, tm=1024, tn=1024, tk=512):
    M, K = a.shape
    if stack is None:
        N = b.shape[1]
        tn, tk = _tile(N, tn), _tile(K, tk)
        b_spec = pl.BlockSpec((tk, tn), lambda i, j, k: (k, j))
    elif stack == "col":
        S, _, Ns = b.shape
        N = S * Ns
        tn, tk = _tile(Ns, tn), _tile(K, tk)
        npb = Ns // tn
        b_spec = pl.BlockSpec((None, tk, tn), lambda i, j, k: (j // npb, k, j % npb))
    else:
        S, Ks, N = b.shape
        tn, tk = _tile(N, tn), _tile(Ks, tk)
        kpb = Ks // tk
        b_spec = pl.BlockSpec((None, tk, tn), lambda i, j, k: (k // kpb, k % kpb, j))
    tm = _tile(M, tm)
    return _mm_call(a, b, jax.ShapeDtypeStruct((M, N), out_dtype), (M // tm, N // tn, K // tk),
                    pl.BlockSpec((tm, tk), lambda i, j, k: (i, k)), b_spec,
                    pl.BlockSpec((tm, tn), lambda i, j, k: (i, j)), (tm, tn), ((1,), (0,)), name)


def mm_nt(a, b, out_dtype, name, stack=None, tm=1024, tn=1024, tk=512):
    M, C = a.shape
    if stack is None:
        Kw = b.shape[0]
        tn, tk = _tile(Kw, tn), _tile(C, tk)
        b_spec = pl.BlockSpec((tn, tk), lambda i, j, k: (j, k))
    elif stack == "col":
        S, Kw, Cs = b.shape
        tn, tk = _tile(Kw, tn), _tile(Cs, tk)
        cpb = Cs // tk
        b_spec = pl.BlockSpec((None, tn, tk), lambda i, j, k: (k // cpb, j, k % cpb))
    else:
        S, Ks, _ = b.shape
        Kw = S * Ks
        tn, tk = _tile(Ks, tn), _tile(C, tk)
        jpb = Ks // tn
        b_spec = pl.BlockSpec((None, tn, tk), lambda i, j, k: (j // jpb, j % jpb, k))
    tm = _tile(M, tm)
    return _mm_call(a, b, jax.ShapeDtypeStruct((M, Kw), out_dtype), (M // tm, Kw // tn, C // tk),
                    pl.BlockSpec((tm, tk), lambda i, j, k: (i, k)), b_spec,
                    pl.BlockSpec((tm, tn), lambda i, j, k: (i, j)), (tm, tn), ((1,), (1,)), name)


def mm_tn(a, b, out_dtype, name, stack=None, n_stack=N_CHIPS, tm=1024, tn=1024, tk=512):
    L, M = a.shape
    N = b.shape[1]
    tk = _tile(L, tk)
    if stack is None:
        tm, tn = _tile(M, tm), _tile(N, tn)
        o_spec = pl.BlockSpec((tm, tn), lambda i, j, k: (i, j))
        out_shape = (M, N)
    elif stack == "col":
        Ns = N // n_stack
        tm, tn = _tile(M, tm), _tile(Ns, tn)
        npb = Ns // tn
        o_spec = pl.BlockSpec((None, tm, tn), lambda i, j, k: (j // npb, i, j % npb))
        out_shape = (n_stack, M, Ns)
    else:
        Ms = M // n_stack
        tm, tn = _tile(Ms, tm), _tile(N, tn)
        mpb = Ms // tm
        o_spec = pl.BlockSpec((None, tm, tn), lambda i, j, k: (i // mpb, i % mpb, j))
        out_shape = (n_stack, Ms, N)
    return _mm_call(a, b, jax.ShapeDtypeStruct(out_shape, out_dtype), (M // tm, N // tn, L // tk),
                    pl.BlockSpec((tk, tm), lambda i, j, k: (k, i)), pl.BlockSpec((tk, tn), lambda i, j, k: (k, j)),
                    o_spec, (tm, tn), ((0,), (0,)), name)


def _row_specs(tr, widths):
    return [pl.BlockSpec((tr, w), lambda i: (i, 0)) for w in widths]


def _vec_spec(w):
    return pl.BlockSpec((1, w), lambda i: (0, 0))


def _acc_rows(ref, val, i):
    s = jnp.sum(val, axis=0, keepdims=True)

    @pl.when(i == 0)
    def _():
        ref[...] = s

    @pl.when(i > 0)
    def _():
        ref[...] += s


def modulate(x, scale, shift, name):
    L, D = x.shape
    tr = _tile(L, 512, 16)

    def body(x_ref, sc_ref, sh_ref, h_ref):
        h_ref[...] = (x_ref[...] * (1.0 + sc_ref[...]) + sh_ref[...]).astype(BF16)

    return pl.pallas_call(
        body, grid=(L // tr,), in_specs=_row_specs(tr, [D]) + [_vec_spec(D)] * 2, out_specs=_row_specs(tr, [D])[0],
        out_shape=jax.ShapeDtypeStruct((L, D), BF16), compiler_params=_cp(("parallel",)), name=name)(x, scale, shift)


def _ln_core(x, y, gate, g, b):
    u = ALPHA * x + (1.0 + gate) * y
    mu = jnp.mean(u, axis=-1, keepdims=True)
    d = u - mu
    var = jnp.mean(d * d, axis=-1, keepdims=True)
    rstd = lax.rsqrt(var + LN_EPS)
    xhat = d * rstd
    return xhat * g + b, xhat, rstd


def ln_mid(x, y, gate, g, b, scale, shift):
    L, D = x.shape
    tr = _tile(L, 256, 16)

    def body(x_ref, y_ref, gate_ref, g_ref, b_ref, sc_ref, sh_ref, x1_ref, x1b_ref, h_ref):
        x1, _, _ = _ln_core(x_ref[...], y_ref[...], gate_ref[...], g_ref[...], b_ref[...])
        x1_ref[...] = x1
        x1b_ref[...] = x1.astype(BF16)
        h_ref[...] = (x1 * (1.0 + sc_ref[...]) + sh_ref[...]).astype(BF16)

    return pl.pallas_call(
        body, grid=(L // tr,), in_specs=_row_specs(tr, [D, D]) + [_vec_spec(D)] * 5,
        out_specs=_row_specs(tr, [D, D, D]),
        out_shape=[jax.ShapeDtypeStruct((L, D), F32), jax.ShapeDtypeStruct((L, D), BF16),
                   jax.ShapeDtypeStruct((L, D), BF16)],
        compiler_params=_cp(("parallel",)), name="ln_mid")(x, y, gate, g, b, scale, shift)


def ln_final(x, y, gate, g, b, target):
    L, D = x.shape
    tr = _tile(L, 256, 16)

    def body(x_ref, y_ref, gate_ref, g_ref, b_ref, t_ref, dout_ref, sq_ref):
        out, _, _ = _ln_core(x_ref[...], y_ref[...], gate_ref[...], g_ref[...], b_ref[...])
        err = out - t_ref[...]
        dout_ref[...] = err * (1.0 / D)
        _acc_rows(sq_ref, err * err, pl.program_id(0))

    return pl.pallas_call(
        body, grid=(L // tr,), in_specs=_row_specs(tr, [D, D]) + [_vec_spec(D)] * 3 + _row_specs(tr, [D]),
        out_specs=[_row_specs(tr, [D])[0], _vec_spec(D)],
        out_shape=[jax.ShapeDtypeStruct((L, D), F32), jax.ShapeDtypeStruct((1, D), F32)],
        compiler_params=_cp(("arbitrary",)), name="ln_final")(x, y, gate, g, b, target)


def ln_bwd(dout, x, y, gate, g, name):
    L, D = x.shape
    tr = _tile(L, 256, 16)

    def body(do_ref, x_ref, y_ref, gate_ref, g_ref, dres_ref, dy_ref, dg_ref, db_ref, dgate_ref):
        i = pl.program_id(0)
        yv = y_ref[...]
        dout_v = do_ref[...]
        _, xhat, rstd = _ln_core(x_ref[...], yv, gate_ref[...], g_ref[...], 0.0)
        dxh = dout_v * g_ref[...]
        m1 = jnp.mean(dxh, axis=-1, keepdims=True)
        m2 = jnp.mean(dxh * xhat, axis=-1, keepdims=True)
        du = rstd * (dxh - m1 - xhat * m2)
        dres_ref[...] = ALPHA * du
        dy_ref[...] = ((1.0 + gate_ref[...]) * du).astype(BF16)
        _acc_rows(dg_ref, dout_v * xhat, i)
        _acc_rows(db_ref, dout_v, i)
        _acc_rows(dgate_ref, du * yv, i)

    return pl.pallas_call(
        body, grid=(L // tr,), in_specs=_row_specs(tr, [D, D, D]) + [_vec_spec(D)] * 2,
        out_specs=_row_specs(tr, [D, D]) + [_vec_spec(D)] * 3,
        out_shape=[jax.ShapeDtypeStruct((L, D), F32), jax.ShapeDtypeStruct((L, D), BF16)]
        + [jax.ShapeDtypeStruct((1, D), F32)] * 3,
        compiler_params=_cp(("arbitrary",)), name=name)(dout, x, y, gate, g)


def mod_bwd(dres, dh, dh2, xin, scale, name, through_mod):
    L, D = xin.shape
    tr = _tile(L, 256, 16)

    def body(dres_ref, dh_ref, dh2_ref, x_ref, sc_ref, dx_ref, dsc_ref, dsh_ref):
        i = pl.program_id(0)
        dh_v = dh_ref[...]
        tot = dres_ref[...]
        if through_mod:
            dh_v = dh_v + dh2_ref[...]
        else:
            tot = tot + dh2_ref[...]
        dx_ref[...] = tot + dh_v * (1.0 + sc_ref[...])
        _acc_rows(dsc_ref, dh_v * x_ref[...], i)
        _acc_rows(dsh_ref, dh_v, i)

    return pl.pallas_call(
        body, grid=(L // tr,), in_specs=_row_specs(tr, [D, D, D, D]) + [_vec_spec(D)],
        out_specs=_row_specs(tr, [D]) + [_vec_spec(D)] * 2,
        out_shape=[jax.ShapeDtypeStruct((L, D), F32)] + [jax.ShapeDtypeStruct((1, D), F32)] * 2,
        compiler_params=_cp(("arbitrary",)), name=name)(dres, dh, dh2, xin, scale)


CONV_HALO = 16


def _conv_rows(x_ref, i, tr, L):
    nblk = L // tr
    s = pl.multiple_of(i * tr, CONV_HALO)
    cur = x_ref[pl.ds(s, tr), :].astype(F32)
    sp = pl.multiple_of(jnp.maximum(i * tr - CONV_HALO, 0), CONV_HALO)
    sn = pl.multiple_of(jnp.minimum(i * tr + tr, L - CONV_HALO), CONV_HALO)
    prev = x_ref[pl.ds(sp, CONV_HALO), :].astype(F32) * (i > 0).astype(F32)
    nxt = x_ref[pl.ds(sn, CONV_HALO), :].astype(F32) * (i < nblk - 1).astype(F32)
    return jnp.concatenate([prev, cur, nxt], axis=0)


def _shift_rows(v, j):
    n = v.shape[0]
    return v if j % n == 0 else pltpu.roll(v, j % n, 0)


def _conv_eval(xe, w_ref, b_ref):
    c = b_ref[...] + w_ref[CONV_W - 1:CONV_W, :] * xe
    for k in range(CONV_W - 1):
        c = c + w_ref[k:k + 1, :] * _shift_rows(xe, CONV_W - 1 - k)
    return c


def conv_fwd(zx, col0, conv_w, conv_b):
    L = zx.shape[0]
    C = conv_w.shape[1]
    tc = _tile(C, 512)
    tr = _tile(L, 512, CONV_HALO)
    off = col0 // tc

    def body(x_ref, w_ref, b_ref, o_ref):
        i = pl.program_id(1)
        xe = _conv_rows(x_ref, i, tr, L)
        c = _conv_eval(xe, w_ref, b_ref)[CONV_HALO:CONV_HALO + tr]
        o_ref[...] = _silu(c).astype(BF16)

    return pl.pallas_call(
        body, grid=(C // tc, L // tr),
        in_specs=[pl.BlockSpec((L, tc), lambda j, i: (0, off + j)), pl.BlockSpec((CONV_W, tc), lambda j, i: (0, j)),
                  pl.BlockSpec((1, tc), lambda j, i: (0, j))],
        out_specs=pl.BlockSpec((tr, tc), lambda j, i: (i, j)),
        out_shape=jax.ShapeDtypeStruct((L, C), BF16), compiler_params=_cp(("parallel", "arbitrary")),
        name="conv_fwd")(zx, conv_w, conv_b)


def conv_bwd(zx, col0, conv_w, conv_b, dxbc):
    L = zx.shape[0]
    C = conv_w.shape[1]
    tc = _tile(C, 512)
    tr = _tile(L, 512, CONV_HALO)
    off = col0 // tc
    H = CONV_HALO

    def body(x_ref, g_ref, w_ref, b_ref, dx_ref, dw_ref, db_ref):
        i = pl.program_id(1)
        xe = _conv_rows(x_ref, i, tr, L)
        ge = _conv_rows(g_ref, i, tr, L)
        dc = ge * _dsilu(_conv_eval(xe, w_ref, b_ref))
        dx = w_ref[CONV_W - 1:CONV_W, :] * dc
        for k in range(CONV_W - 1):
            dx = dx + w_ref[k:k + 1, :] * _shift_rows(dc, -(CONV_W - 1 - k))
        dx_ref[...] = dx[H:H + tr].astype(BF16)
        dcc = dc[H:H + tr]
        rows = [jnp.sum(dcc * _shift_rows(xe, CONV_W - 1 - k)[H:H + tr], axis=0, keepdims=True) for k in range(CONV_W)]
        dwv = jnp.concatenate(rows + [jnp.zeros((8 - CONV_W, tc), F32)], axis=0)
        dbv = jnp.sum(dcc, axis=0, keepdims=True)

        @pl.when(i == 0)
        def _():
            dw_ref[...] = dwv
            db_ref[...] = dbv

        @pl.when(i > 0)
        def _():
            dw_ref[...] += dwv
            db_ref[...] += dbv

    dx, dw, db = pl.pallas_call(
        body, grid=(C // tc, L // tr),
        in_specs=[pl.BlockSpec((L, tc), lambda j, i: (0, off + j)), pl.BlockSpec((L, tc), lambda j, i: (0, j)),
                  pl.BlockSpec((CONV_W, tc), lambda j, i: (0, j)), pl.BlockSpec((1, tc), lambda j, i: (0, j))],
        out_specs=[pl.BlockSpec((tr, tc), lambda j, i: (i, j)), pl.BlockSpec((8, tc), lambda j, i: (0, j)),
                   pl.BlockSpec((1, tc), lambda j, i: (0, j))],
        out_shape=[jax.ShapeDtypeStruct((L, C), BF16), jax.ShapeDtypeStruct((8, C), F32),
                   jax.ShapeDtypeStruct((1, C), F32)],
        compiler_params=_cp(("parallel", "arbitrary")), name="conv_bwd")(zx, dxbc, conv_w, conv_b)
    return dx, dw[:CONV_W], db


def _ssd_common(dtp_ref, dtpT_ref, bias_ref, biasT_ref, alog_ref, alogT_ref, b_ref, c_ref):
    Q = SSD_Q
    dt = _softplus(dtp_ref[...] + bias_ref[...])
    dtT = _softplus(dtpT_ref[...] + biasT_ref[...])
    A = -jnp.exp(alog_ref[...])
    AT = -jnp.exp(alogT_ref[...])
    row = lax.broadcasted_iota(jnp.int32, (Q, Q), 0)
    col = lax.broadcasted_iota(jnp.int32, (Q, Q), 1)
    causal = row >= col
    tril = causal.astype(F32)
    acum = jnp.dot(tril, dt * A, precision=lax.Precision.HIGHEST, preferred_element_type=F32)
    acumT = lax.dot_general(dtT * AT, tril, (((1,), (1,)), ((), ())), precision=lax.Precision.HIGHEST,
                            preferred_element_type=F32)
    Bm = b_ref[...]
    Cm = c_ref[...]
    cb = lax.dot_general(Cm, Bm, (((1,), (1,)), ((), ())), preferred_element_type=F32)
    return dt, A, causal, row, col, acum, acumT, Bm, Cm, cb


def _ssd_in_specs(Q, GP, N, Kh, DI, cmap):
    nb0 = DI // N
    vec = pl.BlockSpec((None, 1, Kh), lambda g, c: (g, 0, 0))
    vecT = pl.BlockSpec((None, Kh, 1), lambda g, c: (g, 0, 0))
    return [pl.BlockSpec((Q, GP), lambda g, c: (cmap(c), g)),
            pl.BlockSpec((Q, N), lambda g, c: (cmap(c), nb0 + g)),
            pl.BlockSpec((Q, N), lambda g, c: (cmap(c), nb0 + SSD_G + g)),
            pl.BlockSpec((None, Q, Kh), lambda g, c: (g, cmap(c), 0)),
            pl.BlockSpec((None, Kh, Q), lambda g, c: (g, 0, cmap(c))),
            vec, vecT, vec, vecT, vec]


def ssd_fwd(xbc, dtp_g, dtp_gT, bias_g, bias_gT, alog_g, alog_gT, dsk_g, DI):
    L = xbc.shape[0]
    Q, P, N, G = SSD_Q, SSD_P, SSD_N, SSD_G
    GP = DI // G
    Kh = GP // P
    nc = L // Q

    def body(xs_ref, b_ref, c_ref, dtp_ref, dtpT_ref, bias_ref, biasT_ref, alog_ref, alogT_ref, dsk_ref,
             y_ref, st_ref, state):
        @pl.when(pl.program_id(1) == 0)
        def _():
            state[...] = jnp.zeros(state.shape, F32)

        st_ref[...] = state[...]
        dt, A, causal, row, col, acum, acumT, Bm, Cm, cb = _ssd_common(
            dtp_ref, dtpT_ref, bias_ref, biasT_ref, alog_ref, alogT_ref, b_ref, c_ref)
        for k in range(Kh):
            hs = slice(k * P, (k + 1) * P)
            a_col = acum[:, k:k + 1]
            decay = jnp.exp(jnp.where(causal, a_col - acumT[k:k + 1, :], -jnp.inf))
            xk = xs_ref[:, hs].astype(F32)
            xdt = xk * dt[:, k:k + 1]
            y = jnp.dot((cb * decay).astype(BF16), xdt.astype(BF16), preferred_element_type=F32)
            S = state[hs, :]
            y = y + lax.dot_general(Cm, S.astype(BF16), (((1,), (1,)), ((), ())),
                                    preferred_element_type=F32) * jnp.exp(a_col)
            y = y + xk * dsk_ref[:, k:k + 1]
            y_ref[:, hs] = y.astype(BF16)
            a_last = acum[Q - 1:Q, k:k + 1]
            tail = jnp.exp(a_last - a_col)
            state[hs, :] = S * jnp.exp(a_last) + lax.dot_general(
                (xdt * tail).astype(BF16), Bm, (((0,), (0,)), ((), ())), preferred_element_type=F32)

    return pl.pallas_call(
        body, grid=(G, nc), in_specs=_ssd_in_specs(Q, GP, N, Kh, DI, lambda c: c),
        out_specs=[pl.BlockSpec((Q, GP), lambda g, c: (c, g)),
                   pl.BlockSpec((None, None, GP, N), lambda g, c: (c, g, 0, 0))],
        out_shape=[jax.ShapeDtypeStruct((L, DI), BF16), jax.ShapeDtypeStruct((nc, G, GP, N), F32)],
        scratch_shapes=[pltpu.VMEM((GP, N), F32)], compiler_params=_cp(("parallel", "arbitrary")),
        name="ssd_fwd")(xbc, xbc, xbc, dtp_g, dtp_gT, bias_g, bias_gT, alog_g, alog_gT, dsk_g)


def ssd_bwd(xbc, dtp_g, dtp_gT, bias_g, bias_gT, alog_g, alog_gT, dsk_g, states, dy, DI):
    L = xbc.shape[0]
    Q, P, N, G = SSD_Q, SSD_P, SSD_N, SSD_G
    GP = DI // G
    Kh = GP // P
    nc = L // Q
    rev = lambda c: nc - 1 - c

    def body(xs_ref, b_ref, c_ref, dtp_ref, dtpT_ref, bias_ref, biasT_ref, alog_ref, alogT_ref, dsk_ref,
             st_ref, dy_ref, dxs_ref, dB_ref, dC_ref, ddtp_ref, dbias_ref, dalog_ref, dD_ref, dstate):
        ci = pl.program_id(1)

        @pl.when(ci == 0)
        def _():
            dstate[...] = jnp.zeros(dstate.shape, F32)

        dt, A, causal, row, col, acum, acumT, Bm, Cm, cb = _ssd_common(
            dtp_ref, dtpT_ref, bias_ref, biasT_ref, alog_ref, alogT_ref, b_ref, c_ref)
        Bf = Bm.astype(F32)
        Cf = Cm.astype(F32)
        lane = lax.broadcasted_iota(jnp.int32, (Q, Kh), 1)
        lane1 = lax.broadcasted_iota(jnp.int32, (1, Kh), 1)
        rowi = lax.broadcasted_iota(jnp.int32, (Q, 1), 0)
        dC_acc = jnp.zeros((Q, N), F32)
        dB_acc = jnp.zeros((Q, N), F32)
        da_cols = jnp.zeros((Q, Kh), F32)
        ddt_x = jnp.zeros((Q, Kh), F32)
        dD_row = jnp.zeros((1, Kh), F32)
        tn = (((0,), (0,)), ((), ()))
        nt = (((1,), (1,)), ((), ()))
        for k in range(Kh):
            hs = slice(k * P, (k + 1) * P)
            a_col = acum[:, k:k + 1]
            decay = jnp.exp(jnp.where(causal, a_col - acumT[k:k + 1, :], -jnp.inf))
            xk = xs_ref[:, hs].astype(F32)
            dtk = dt[:, k:k + 1]
            xdt = xk * dtk
            xdt_b = xdt.astype(BF16)
            dyk_b = dy_ref[:, hs]
            dyk = dyk_b.astype(F32)
            S = st_ref[hs, :]
            S_b = S.astype(BF16)
            dS = dstate[hs, :]
            dS_b = dS.astype(BF16)
            Gm = lax.dot_general(dyk_b, xdt_b, nt, preferred_element_type=F32)
            MG = (decay * Gm).astype(BF16)
            dCd = jnp.dot(MG, Bm, preferred_element_type=F32)
            dBd = lax.dot_general(MG, Cm, tn, preferred_element_type=F32)
            dxdt = lax.dot_general((decay * cb).astype(BF16), dyk_b, tn, preferred_element_type=F32)
            ea = jnp.exp(a_col)
            CS = lax.dot_general(Cm, S_b, nt, preferred_element_type=F32)
            dyE_b = (dyk * ea).astype(BF16)
            dCo = jnp.dot(dyE_b, S_b, preferred_element_type=F32)
            dS_y = lax.dot_general(dyE_b, Cm, tn, preferred_element_type=F32)
            a_last = acum[Q - 1:Q, k:k + 1]
            eL = jnp.exp(a_last)
            tail = jnp.exp(a_last - a_col)
            BdS = lax.dot_general(Bm, dS_b, nt, preferred_element_type=F32)
            dxdt = dxdt + tail * BdS
            dtail = jnp.sum(xdt * BdS, axis=1, keepdims=True)
            dBo = jnp.dot((xdt * tail).astype(BF16), dS_b, preferred_element_type=F32)
            da = (jnp.sum(Cf * dCd, axis=1, keepdims=True) - jnp.sum(Bf * dBd, axis=1, keepdims=True)
                  + jnp.sum(dyk * CS, axis=1, keepdims=True) * ea - dtail * tail)
            da_last = jnp.sum(dS * S, keepdims=True) * eL + jnp.sum(dtail * tail, keepdims=True)
            da = da + jnp.where(rowi == Q - 1, da_last, 0.0)
            dstate[hs, :] = eL * dS + dS_y
            dC_acc = dC_acc + dCd + dCo
            dB_acc = dB_acc + dBd + dBo
            dxs_ref[:, hs] = (dxdt * dtk + dyk * dsk_ref[:, k:k + 1]).astype(BF16)
            da_cols = jnp.where(lane == k, da, da_cols)
            ddt_x = jnp.where(lane == k, jnp.sum(dxdt * xk, axis=1, keepdims=True), ddt_x)
            dD_row = jnp.where(lane1 == k, jnp.sum(dyk * xk, keepdims=True), dD_row)
        dB_ref[...] = dB_acc.astype(BF16)
        dC_ref[...] = dC_acc.astype(BF16)
        triu = (row <= col).astype(F32)
        ddtA = jnp.dot(triu, da_cols, precision=lax.Precision.HIGHEST, preferred_element_type=F32)
        ddt = ddt_x + ddtA * A
        dpre = ddt * _sigmoid(dtp_ref[...] + bias_ref[...])
        ddtp_ref[...] = dpre
        dbias_v = jnp.sum(dpre, axis=0, keepdims=True)
        dalog_v = jnp.sum(ddtA * dt, axis=0, keepdims=True) * A

        @pl.when(ci == 0)
        def _():
            dbias_ref[...] = dbias_v
            dalog_ref[...] = dalog_v
            dD_ref[...] = dD_row

        @pl.when(ci > 0)
        def _():
            dbias_ref[...] += dbias_v
            dalog_ref[...] += dalog_v
            dD_ref[...] += dD_row

    vec_o = pl.BlockSpec((None, 1, Kh), lambda g, c: (g, 0, 0))
    return pl.pallas_call(
        body, grid=(G, nc),
        in_specs=_ssd_in_specs(Q, GP, N, Kh, DI, rev)
        + [pl.BlockSpec((None, None, GP, N), lambda g, c: (rev(c), g, 0, 0)),
           pl.BlockSpec((Q, GP), lambda g, c: (rev(c), g))],
        out_specs=[pl.BlockSpec((Q, GP), lambda g, c: (rev(c), g)), pl.BlockSpec((Q, N), lambda g, c: (rev(c), g)),
                   pl.BlockSpec((Q, N), lambda g, c: (rev(c), g)),
                   pl.BlockSpec((None, Q, Kh), lambda g, c: (g, rev(c), 0)), vec_o, vec_o, vec_o],
        out_shape=[jax.ShapeDtypeStruct((L, DI), BF16), jax.ShapeDtypeStruct((L, G * N), BF16),
                   jax.ShapeDtypeStruct((L, G * N), BF16), jax.ShapeDtypeStruct((G, L, Kh), F32)]
        + [jax.ShapeDtypeStruct((G, 1, Kh), F32)] * 3,
        scratch_shapes=[pltpu.VMEM((GP, N), F32)], compiler_params=_cp(("parallel", "arbitrary")),
        name="ssd_bwd")(xbc, xbc, xbc, dtp_g, dtp_gT, bias_g, bias_gT, alog_g, alog_gT, dsk_g, states, dy)


def _rms_groups(y2, ng_ref, DI):
    S = DI // SSD_G
    for g in range(SSD_G):
        gs = slice(g * S, (g + 1) * S)
        seg = y2[:, gs]
        r = lax.rsqrt(jnp.mean(seg * seg, axis=-1, keepdims=True) + RMS_EPS)
        yield gs, seg * r, r, ng_ref[:, gs]


def rms_gate_fwd(y, zx, norm_g):
    L, DI = y.shape
    tr = _tile(L, 256, 16)

    def body(y_ref, z_ref, ng_ref, o_ref):
        y2 = y_ref[...].astype(F32) * _silu(z_ref[...].astype(F32))
        for gs, yh, _, ng in _rms_groups(y2, ng_ref, DI):
            o_ref[:, gs] = (yh * ng).astype(BF16)

    return pl.pallas_call(
        body, grid=(L // tr,), in_specs=_row_specs(tr, [DI, DI]) + [_vec_spec(DI)], out_specs=_row_specs(tr, [DI])[0],
        out_shape=jax.ShapeDtypeStruct((L, DI), BF16), compiler_params=_cp(("parallel",)),
        name="rms_gate_fwd")(y, zx, norm_g)


def rms_gate_bwd(dyn, y, zx, norm_g):
    L, DI = y.shape
    tr = _tile(L, 256, 16)

    def body(dyn_ref, y_ref, z_ref, ng_ref, dy_ref, dz_ref, dng_ref):
        i = pl.program_id(0)
        yv = y_ref[...].astype(F32)
        zv = z_ref[...].astype(F32)
        sz = _silu(zv)
        dsz = _dsilu(zv)
        dynv = dyn_ref[...].astype(F32)
        for gs, yh, r, ng in _rms_groups(yv * sz, ng_ref, DI):
            dyh = dynv[:, gs] * ng
            dy2 = r * (dyh - yh * jnp.mean(dyh * yh, axis=-1, keepdims=True))
            dy_ref[:, gs] = (dy2 * sz[:, gs]).astype(BF16)
            dz_ref[:, gs] = (dy2 * yv[:, gs] * dsz[:, gs]).astype(BF16)
            s = jnp.sum(dynv[:, gs] * yh, axis=0, keepdims=True)

            @pl.when(i == 0)
            def _():
                dng_ref[:, gs] = s

            @pl.when(i > 0)
            def _():
                dng_ref[:, gs] += s

    return pl.pallas_call(
        body, grid=(L // tr,), in_specs=_row_specs(tr, [DI, DI, DI]) + [_vec_spec(DI)],
        out_specs=_row_specs(tr, [DI, DI]) + [_vec_spec(DI)],
        out_shape=[jax.ShapeDtypeStruct((L, DI), BF16)] * 2 + [jax.ShapeDtypeStruct((1, DI), F32)],
        compiler_params=_cp(("arbitrary",)), name="rms_gate_bwd")(dyn, y, zx, norm_g)


def _alibi_slope(gi, h):
    n = len(DIL_PATTERNS) * DIL_H
    return float(2.0 ** (-8.0 * (gi * DIL_H + h + 1) / n))


def _attn_masks():
    qi = lax.broadcasted_iota(jnp.int32, (DIL_BLK, DIL_BLK), 0)
    kj = lax.broadcasted_iota(jnp.int32, (DIL_BLK, DIL_BLK), 1)
    dcur = (qi - kj).astype(F32)
    return dcur, qi >= kj, dcur + float(DIL_BLK), kj >= qi


def attn_fwd(qz, kv, gi):
    window, d = DIL_PATTERNS[gi]
    assert window // d == DIL_BLK
    L, QZ = qz.shape
    KV = kv.shape[1]
    HW = DIL_H * DIL_E
    M = L // d
    nb = M // DIL_BLK
    nq, nkv = QZ // HW, KV // HW
    scale = DIL_E ** -0.5
    nt = (((1,), (1,)), ((), ()))

    def body(q_ref, kp_ref, kc_ref, vp_ref, vc_ref, o_ref, lse_ref):
        n = pl.program_id(1)
        dcur, vcur, dprev, vprev0 = _attn_masks()
        vprev = jnp.logical_and(vprev0, n > 0)
        lane = lax.broadcasted_iota(jnp.int32, (DIL_BLK, 128), 1)
        lse_acc = jnp.zeros((DIL_BLK, 128), F32)
        for h in range(DIL_H):
            hs = slice(h * DIL_E, (h + 1) * DIL_E)
            sl = _alibi_slope(gi, h) * d
            q = q_ref[:, hs]
            s_c = lax.dot_general(q, kc_ref[:, hs], nt, preferred_element_type=F32) * scale - sl * dcur
            s_p = lax.dot_general(q, kp_ref[:, hs], nt, preferred_element_type=F32) * scale - sl * dprev
            s_c = jnp.where(vcur, s_c, -jnp.inf)
            s_p = jnp.where(vprev, s_p, -jnp.inf)
            m = jnp.maximum(jnp.max(s_c, axis=-1, keepdims=True), jnp.max(s_p, axis=-1, keepdims=True))
            p_c = jnp.exp(s_c - m)
            p_p = jnp.exp(s_p - m)
            den = jnp.sum(p_c, axis=-1, keepdims=True) + jnp.sum(p_p, axis=-1, keepdims=True)
            o = (jnp.dot(p_c.astype(BF16), vc_ref[:, hs], preferred_element_type=F32)
                 + jnp.dot(p_p.astype(BF16), vp_ref[:, hs], preferred_element_type=F32)) / den
            o_ref[:, hs] = o.astype(BF16)
            lse_acc = jnp.where(lane == h, m + jnp.log(den), lse_acc)
        lse_ref[...] = lse_acc

    blk = (DIL_BLK, HW)
    prev = lambda n: jnp.maximum(n - 1, 0)
    o, lse = pl.pallas_call(
        body, grid=(d, nb),
        in_specs=[pl.BlockSpec(blk, lambda r, n: (n, r * nq + gi)),
                  pl.BlockSpec(blk, lambda r, n: (prev(n), r * nkv + gi)),
                  pl.BlockSpec(blk, lambda r, n: (n, r * nkv + gi)),
                  pl.BlockSpec(blk, lambda r, n: (prev(n), r * nkv + nkv // 2 + gi)),
                  pl.BlockSpec(blk, lambda r, n: (n, r * nkv + nkv // 2 + gi))],
        out_specs=[pl.BlockSpec(blk, lambda r, n: (n, r)), pl.BlockSpec((DIL_BLK, 128), lambda r, n: (n, r))],
        out_shape=[jax.ShapeDtypeStruct((M, d * HW), BF16), jax.ShapeDtypeStruct((M, d * 128), F32)],
        compiler_params=_cp(("parallel", "parallel")), name=f"attn_fwd_{gi}")(
            qz.reshape(M, d * QZ), *([kv.reshape(M, d * KV)] * 4))
    return o.reshape(L, HW), lse.reshape(L, 128)


def attn_bwd(qz, kv, do, lse, dpr, gi):
    window, d = DIL_PATTERNS[gi]
    L, QZ = qz.shape
    KV = kv.shape[1]
    HW = DIL_H * DIL_E
    M = L // d
    nb = M // DIL_BLK
    nq, nkv = QZ // HW, KV // HW
    scale = DIL_E ** -0.5
    nt = (((1,), (1,)), ((), ()))
    tn = (((0,), (0,)), ((), ()))

    def body(q0_ref, q1_ref, k_ref, v_ref, do0_ref, do1_ref, l0_ref, l1_ref, r0_ref, r1_ref,
             dq_ref, dk_ref, dv_ref, carry):
        n = pl.program_id(1)

        @pl.when(n == 0)
        def _():
            carry[...] = jnp.zeros(carry.shape, F32)

        dcur, vcur, dprev, vprev0 = _attn_masks()
        vprev = jnp.logical_and(vprev0, n < nb - 1)
        for h in range(DIL_H):
            hs = slice(h * DIL_E, (h + 1) * DIL_E)
            sl = _alibi_slope(gi, h) * d
            kh = k_ref[:, hs]
            vh = v_ref[:, hs]
            q0, q1 = q0_ref[:, hs], q1_ref[:, hs]
            do0, do1 = do0_ref[:, hs], do1_ref[:, hs]
            s0 = lax.dot_general(q0, kh, nt, preferred_element_type=F32) * scale - sl * dcur
            p0 = jnp.exp(jnp.where(vcur, s0 - l0_ref[:, h:h + 1], -jnp.inf))
            ds0 = p0 * (lax.dot_general(do0, vh, nt, preferred_element_type=F32) - r0_ref[:, h:h + 1])
            s1 = lax.dot_general(q1, kh, nt, preferred_element_type=F32) * scale - sl * dprev
            p1 = jnp.exp(jnp.where(vprev, s1 - l1_ref[:, h:h + 1], -jnp.inf))
            ds1 = p1 * (lax.dot_general(do1, vh, nt, preferred_element_type=F32) - r1_ref[:, h:h + 1])
            ds0_b = (ds0 * scale).astype(BF16)
            ds1_b = (ds1 * scale).astype(BF16)
            dv = (lax.dot_general(p0.astype(BF16), do0, tn, preferred_element_type=F32)
                  + lax.dot_general(p1.astype(BF16), do1, tn, preferred_element_type=F32))
            dk = (lax.dot_general(ds0_b, q0, tn, preferred_element_type=F32)
                  + lax.dot_general(ds1_b, q1, tn, preferred_element_type=F32))
            dv_ref[:, hs] = dv.astype(BF16)
            dk_ref[:, hs] = dk.astype(BF16)
            dq_ref[:, hs] = (carry[:, hs] + jnp.dot(ds0_b, kh, preferred_element_type=F32)).astype(BF16)
            carry[:, hs] = jnp.dot(ds1_b, kh, preferred_element_type=F32)

    blk = (DIL_BLK, HW)
    sblk = (DIL_BLK, 128)
    nxt = lambda n: jnp.minimum(n + 1, nb - 1)
    qv = qz.reshape(M, d * QZ)
    kvv = kv.reshape(M, d * KV)
    dov = do.reshape(M, d * HW)
    lv = lse.reshape(M, d * 128)
    rv = dpr.reshape(M, d * 128)
    outs = pl.pallas_call(
        body, grid=(d, nb),
        in_specs=[pl.BlockSpec(blk, lambda r, n: (n, r * nq + gi)), pl.BlockSpec(blk, lambda r, n: (nxt(n), r * nq + gi)),
                  pl.BlockSpec(blk, lambda r, n: (n, r * nkv + gi)),
                  pl.BlockSpec(blk, lambda r, n: (n, r * nkv + nkv // 2 + gi)),
                  pl.BlockSpec(blk, lambda r, n: (n, r)), pl.BlockSpec(blk, lambda r, n: (nxt(n), r)),
                  pl.BlockSpec(sblk, lambda r, n: (n, r)), pl.BlockSpec(sblk, lambda r, n: (nxt(n), r)),
                  pl.BlockSpec(sblk, lambda r, n: (n, r)), pl.BlockSpec(sblk, lambda r, n: (nxt(n), r))],
        out_specs=[pl.BlockSpec(blk, lambda r, n: (n, r))] * 3,
        out_shape=[jax.ShapeDtypeStruct((M, d * HW), BF16)] * 3,
        scratch_shapes=[pltpu.VMEM(blk, F32)], compiler_params=_cp(("parallel", "arbitrary")),
        name=f"attn_bwd_{gi}")(qv, qv, kvv, kvv, dov, dov, lv, lv, rv, rv)
    return [t.reshape(L, HW) for t in outs]


def _merge_weights(l_refs, h):
    ls = [r[:, h:h + 1] for r in l_refs]
    mx = functools.reduce(jnp.maximum, ls)
    es = [jnp.exp(l - mx) for l in ls]
    den = functools.reduce(lambda a, b: a + b, es)
    return [e / den for e in es]


def merge_fwd(os_, lses, qz):
    L, HW = os_[0].shape
    tr = _tile(L, 256, 16)
    ng = len(os_)
    zblk = qz.shape[1] // HW - 1

    def body(*refs):
        o_refs, l_refs, z_ref, out_ref = refs[:ng], refs[ng:2 * ng], refs[2 * ng], refs[2 * ng + 1]
        for h in range(DIL_H):
            hs = slice(h * DIL_E, (h + 1) * DIL_E)
            ws = _merge_weights(l_refs, h)
            om = functools.reduce(lambda a, b: a + b, [w * o[:, hs].astype(F32) for w, o in zip(ws, o_refs)])
            out_ref[:, hs] = (om * _silu(z_ref[:, hs].astype(F32))).astype(BF16)

    return pl.pallas_call(
        body, grid=(L // tr,),
        in_specs=_row_specs(tr, [HW] * ng + [128] * ng) + [pl.BlockSpec((tr, HW), lambda i: (i, zblk))],
        out_specs=_row_specs(tr, [HW])[0], out_shape=jax.ShapeDtypeStruct((L, HW), BF16),
        compiler_params=_cp(("parallel",)), name="merge_fwd")(*os_, *lses, qz)


def merge_bwd(dgated, os_, lses, qz):
    L, HW = os_[0].shape
    tr = _tile(L, 256, 16)
    ng = len(os_)
    zblk = qz.shape[1] // HW - 1

    def body(*refs):
        dg_ref = refs[0]
        o_refs, l_refs, z_ref = refs[1:1 + ng], refs[1 + ng:1 + 2 * ng], refs[1 + 2 * ng]
        outs = refs[2 + 2 * ng:]
        do_refs, dpr_refs, dz_ref = outs[:ng], outs[ng:2 * ng], outs[2 * ng]
        lane = lax.broadcasted_iota(jnp.int32, (tr, 128), 1)
        accs = [jnp.zeros((tr, 128), F32) for _ in range(ng)]
        for h in range(DIL_H):
            hs = slice(h * DIL_E, (h + 1) * DIL_E)
            ws = _merge_weights(l_refs, h)
            ov = [o[:, hs].astype(F32) for o in o_refs]
            om = functools.reduce(lambda a, b: a + b, [w * o for w, o in zip(ws, ov)])
            zv = z_ref[:, hs].astype(F32)
            dgv = dg_ref[:, hs].astype(F32)
            dom = dgv * _silu(zv)
            dz_ref[:, hs] = (dgv * om * _dsilu(zv)).astype(BF16)
            dws = [jnp.sum(dom * o, axis=-1, keepdims=True) for o in ov]
            dwbar = functools.reduce(lambda a, b: a + b, [w * dw for w, dw in zip(ws, dws)])
            for g in range(ng):
                do_refs[g][:, hs] = (ws[g] * dom).astype(BF16)
                accs[g] = jnp.where(lane == h, ws[g] * dwbar, accs[g])
        for g in range(ng):
            dpr_refs[g][...] = accs[g]

    outs = pl.pallas_call(
        body, grid=(L // tr,),
        in_specs=_row_specs(tr, [HW] * (1 + ng) + [128] * ng) + [pl.BlockSpec((tr, HW), lambda i: (i, zblk))],
        out_specs=_row_specs(tr, [HW] * ng + [128] * ng + [HW]),
        out_shape=[jax.ShapeDtypeStruct((L, HW), BF16)] * ng + [jax.ShapeDtypeStruct((L, 128), F32)] * ng
        + [jax.ShapeDtypeStruct((L, HW), BF16)],
        compiler_params=_cp(("parallel",)), name="merge_bwd")(dgated, *os_, *lses, qz)
    return outs[:ng], outs[ng:2 * ng], outs[2 * ng]


def ada_fwd(c8, ada_w):
    nl, D, Ws = ada_w.shape
    tn = _tile(Ws, 512)

    def body(c_ref, w_ref, o_ref):
        o_ref[...] = jnp.dot(_silu(c_ref[...]), w_ref[...], precision=lax.Precision.HIGHEST,
                             preferred_element_type=F32)

    return pl.pallas_call(
        body, grid=(nl, Ws // tn),
        in_specs=[pl.BlockSpec((N_DEV, D), lambda l, j: (0, 0)), pl.BlockSpec((None, D, tn), lambda l, j: (l, 0, j))],
        out_specs=pl.BlockSpec((None, N_DEV, tn), lambda l, j: (l, 0, j)),
        out_shape=jax.ShapeDtypeStruct((nl, N_DEV, Ws), F32), compiler_params=_cp(("parallel", "parallel")),
        name="ada_fwd")(c8, ada_w)


def ada_wgrad(c8t, dmod):
    nl, _, Ws = dmod.shape
    D = c8t.shape[0]
    tm = _tile(D, 512, 8)

    def body(c_ref, d_ref, o_ref):
        sc = _silu(c_ref[...])
        acc = sc[:, 0:1] * d_ref[0:1, :]
        for e in range(1, N_DEV):
            acc = acc + sc[:, e:e + 1] * d_ref[e:e + 1, :]
        o_ref[...] = acc

    return pl.pallas_call(
        body, grid=(nl, D // tm),
        in_specs=[pl.BlockSpec((tm, N_DEV), lambda l, i: (i, 0)), pl.BlockSpec((None, N_DEV, Ws), lambda l, i: (l, 0, 0))],
        out_specs=pl.BlockSpec((None, tm, Ws), lambda l, i: (l, i, 0)),
        out_shape=jax.ShapeDtypeStruct((nl, D, Ws), F32), compiler_params=_cp(("parallel", "parallel")),
        name="ada_wgrad")(c8t, dmod)


def adamw(w, g, m, v, name):
    R, C = w.shape
    tr = _tile(R, 256, 8)
    c1 = 1.0 - ADAM_B1 ** ADAM_STEP
    c2 = 1.0 - ADAM_B2 ** ADAM_STEP

    def body(w_ref, g_ref, m_ref, v_ref, d_ref, nm_ref, nv_ref):
        gv = g_ref[...]
        nm = ADAM_B1 * m_ref[...] + (1.0 - ADAM_B1) * gv
        nv = ADAM_B2 * v_ref[...] + (1.0 - ADAM_B2) * (gv * gv)
        nm_ref[...] = nm
        nv_ref[...] = nv
        d_ref[...] = -ADAM_LR * ((nm / c1) / (jnp.sqrt(nv / c2) + ADAM_EPS) + ADAM_WD * w_ref[...])

    return pl.pallas_call(
        body, grid=(R // tr,), in_specs=_row_specs(tr, [C] * 4), out_specs=_row_specs(tr, [C] * 3),
        out_shape=[jax.ShapeDtypeStruct((R, C), F32)] * 3, compiler_params=_cp(("parallel",)), name=name)(w, g, m, v)


def sum_leading(t, name, out_dtype=F32):
    S, R, C = t.shape
    tr = _tile(R, 256, 16)

    def body(t_ref, o_ref):
        acc = t_ref[0].astype(F32)
        for s in range(1, S):
            acc = acc + t_ref[s].astype(F32)
        o_ref[...] = acc.astype(out_dtype)

    return pl.pallas_call(
        body, grid=(R // tr,), in_specs=[pl.BlockSpec((S, tr, C), lambda i: (0, i, 0))],
        out_specs=pl.BlockSpec((tr, C), lambda i: (i, 0)), out_shape=jax.ShapeDtypeStruct((R, C), out_dtype),
        compiler_params=_cp(("parallel",)), name=name)(t)


def add_half(g, a, core, name):
    S, R, C = g.shape
    h = R // 2
    tr = _tile(h, 256, 16)
    nb = h // tr

    def body(core_ref, g_ref, a_ref, o_ref):
        o_ref[...] = (g_ref[...].astype(F32) + a_ref[...].astype(F32)).astype(BF16)

    return pl.pallas_call(
        body,
        grid_spec=pltpu.PrefetchScalarGridSpec(
            num_scalar_prefetch=1, grid=(S, nb),
            in_specs=[pl.BlockSpec((None, tr, C), lambda s, i, core_ref: (s, core_ref[0] * nb + i, 0)),
                      pl.BlockSpec((None, tr, C), lambda s, i, core_ref: (s, i, 0))],
            out_specs=pl.BlockSpec((None, tr, C), lambda s, i, core_ref: (s, i, 0))),
        out_shape=jax.ShapeDtypeStruct((S, h, C), BF16), compiler_params=_cp(("parallel", "parallel")),
        name=name)(core, g, a)


_ANY = pl.BlockSpec(memory_space=pl.ANY)


def _place():
    x, y, c = lax.axis_index("x"), lax.axis_index("y"), lax.axis_index("c")
    chips = [(1 - x, y), (x, 1 - y), (1 - x, 1 - y)]
    return x, y, c, chips


def allgather_small(v, name):
    R, W = v.shape

    def body(x_ref, out_ref, send_sems, recv_sems, local_sem):
        x, y, c, chips = _place()
        me, sibling = (x, y, c), (x, y, 1 - c)

        def rows(px, py, pc):
            return out_ref.at[pl.ds((4 * px + 2 * py + pc) * R, R), :]

        def copy(k, block, to, src=None):
            return pltpu.make_async_remote_copy(
                src_ref=rows(*block) if src is None else src, dst_ref=rows(*block),
                send_sem=send_sems.at[k], recv_sem=recv_sems.at[k], device_id=to, device_id_type=MESH)

        mine = pltpu.make_async_copy(x_ref, rows(*me), local_sem)
        mine.start()
        first = [copy(0, me, sibling, src=x_ref)]
        first += [copy(1 + j, me, (*chip, c), src=x_ref) for j, chip in enumerate(chips)]
        for cp in first:
            cp.start()
        passed = [copy(4 + j, (*chip, c), sibling) for j, chip in enumerate(chips)]
        for j, chip in enumerate(chips):
            copy(1 + j, (*chip, c), me).wait_recv()
            passed[j].start()
        copy(0, sibling, me).wait_recv()
        for j, chip in enumerate(chips):
            copy(4 + j, (*chip, 1 - c), me).wait_recv()
        for cp in first + passed:
            cp.wait_send()
        mine.wait()

    return pl.pallas_call(
        body, out_shape=jax.ShapeDtypeStruct((N_DEV * R, W), v.dtype),
        in_specs=[pl.BlockSpec(memory_space=pltpu.VMEM)], out_specs=pl.BlockSpec(memory_space=pltpu.VMEM),
        scratch_shapes=[pltpu.SemaphoreType.DMA((7,)), pltpu.SemaphoreType.DMA((7,)), pltpu.SemaphoreType.DMA],
        name=name)(v)


def allgather_weights(shards):
    n = len(shards)

    def body(*refs):
        ins, outs = refs[:n], refs[n:2 * n]
        send_sems, recv_sems, local_sems = refs[2 * n:]
        x, y, c, chips = _place()
        p = 2 * x + y
        sibling = (x, y, 1 - c)
        locals_ = [pltpu.make_async_copy(ins[i], outs[i].at[p], local_sems.at[i]) for i in range(n)]
        for cp in locals_:
            cp.start()

        def half(i, chip_id, core, ref=None):
            h = ins[i].shape[0] // 2
            r = outs[i].at[chip_id] if ref is None else ref
            return r.at[pl.ds(core * h, h), :]

        def copy(i, k, chip_id, core, to, src=None):
            return pltpu.make_async_remote_copy(
                src_ref=half(i, chip_id, core) if src is None else src, dst_ref=half(i, chip_id, core),
                send_sem=send_sems.at[6 * i + k], recv_sem=recv_sems.at[6 * i + k], device_id=to, device_id_type=MESH)

        first = [copy(i, j, p, c, (*chip, c), src=half(i, p, c, ref=ins[i]))
                 for i in range(n) for j, chip in enumerate(chips)]
        for cp in first:
            cp.start()
        passed = []
        for i in range(n):
            for j, (cx, cy) in enumerate(chips):
                copy(i, j, 2 * cx + cy, c, sibling).wait_recv()
                fw = copy(i, 3 + j, 2 * cx + cy, c, sibling)
                fw.start()
                passed.append(fw)
        for i in range(n):
            for j, (cx, cy) in enumerate(chips):
                copy(i, 3 + j, 2 * cx + cy, 1 - c, sibling).wait_recv()
        for cp in first + passed:
            cp.wait_send()
        for cp in locals_:
            cp.wait()

    return pl.pallas_call(
        body, out_shape=[jax.ShapeDtypeStruct((N_CHIPS,) + s.shape, s.dtype) for s in shards],
        in_specs=[_ANY] * n, out_specs=[_ANY] * n,
        scratch_shapes=[pltpu.SemaphoreType.DMA((6 * n,)), pltpu.SemaphoreType.DMA((6 * n,)),
                        pltpu.SemaphoreType.DMA((n,))],
        name="allgather_weights")(*shards)


def exchange_halves_to_sibling(gs):
    n = len(gs)

    def body(*refs):
        ins, outs = refs[:n], refs[n:2 * n]
        send_sems, recv_sems = refs[2 * n:]
        x, y, c, _ = _place()
        cps = []
        for i in range(n):
            h = ins[i].shape[1] // 2
            cps.append(pltpu.make_async_remote_copy(
                src_ref=ins[i].at[:, pl.ds((1 - c) * h, h), :], dst_ref=outs[i],
                send_sem=send_sems.at[i], recv_sem=recv_sems.at[i], device_id=(x, y, 1 - c), device_id_type=MESH))
        for cp in cps:
            cp.start()
        for cp in cps:
            cp.wait()

    return pl.pallas_call(
        body, out_shape=[jax.ShapeDtypeStruct((g.shape[0], g.shape[1] // 2, g.shape[2]), g.dtype) for g in gs],
        in_specs=[_ANY] * n, out_specs=[_ANY] * n,
        scratch_shapes=[pltpu.SemaphoreType.DMA((n,)), pltpu.SemaphoreType.DMA((n,))],
        name="rs_sibling_exchange")(*gs)


def scatter_to_chips(ps):
    n = len(ps)

    def body(*refs):
        ins, outs = refs[:n], refs[n:2 * n]
        send_sems, recv_sems, local_sems = refs[2 * n:]
        x, y, c, chips = _place()
        p = 2 * x + y
        locals_ = [pltpu.make_async_copy(ins[i].at[p], outs[i].at[p], local_sems.at[i]) for i in range(n)]
        for cp in locals_:
            cp.start()
        sends, recvs = [], []
        for i in range(n):
            for j, (cx, cy) in enumerate(chips):
                q = 2 * cx + cy
                sends.append(pltpu.make_async_remote_copy(
                    src_ref=ins[i].at[q], dst_ref=outs[i].at[p], send_sem=send_sems.at[3 * i + j],
                    recv_sem=recv_sems.at[3 * i + j], device_id=(cx, cy, c), device_id_type=MESH))
                recvs.append(pltpu.make_async_remote_copy(
                    src_ref=ins[i].at[q], dst_ref=outs[i].at[q], send_sem=send_sems.at[3 * i + j],
                    recv_sem=recv_sems.at[3 * i + j], device_id=(cx, cy, c), device_id_type=MESH))
        for cp in sends:
            cp.start()
        for cp in recvs:
            cp.wait_recv()
        for cp in sends:
            cp.wait_send()
        for cp in locals_:
            cp.wait()

    return pl.pallas_call(
        body, out_shape=[jax.ShapeDtypeStruct(t.shape, t.dtype) for t in ps], in_specs=[_ANY] * n, out_specs=[_ANY] * n,
        scratch_shapes=[pltpu.SemaphoreType.DMA((3 * n,)), pltpu.SemaphoreType.DMA((3 * n,)),
                        pltpu.SemaphoreType.DMA((n,))],
        name="rs_chip_scatter")(*ps)


def join_halves(rs):
    n = len(rs)

    def body(*refs):
        ins, outs = refs[:n], refs[n:2 * n]
        send_sems, recv_sems, local_sems = refs[2 * n:]
        x, y, c, _ = _place()
        locals_, sends, recvs = [], [], []
        for i in range(n):
            h = ins[i].shape[0]
            mine = outs[i].at[pl.ds(c * h, h), :]
            theirs = outs[i].at[pl.ds((1 - c) * h, h), :]
            locals_.append(pltpu.make_async_copy(ins[i], mine, local_sems.at[i]))
            sends.append(pltpu.make_async_remote_copy(
                src_ref=ins[i], dst_ref=mine, send_sem=send_sems.at[i], recv_sem=recv_sems.at[i],
                device_id=(x, y, 1 - c), device_id_type=MESH))
            recvs.append(pltpu.make_async_remote_copy(
                src_ref=ins[i], dst_ref=theirs, send_sem=send_sems.at[i], recv_sem=recv_sems.at[i],
                device_id=(x, y, 1 - c), device_id_type=MESH))
        for cp in locals_ + sends:
            cp.start()
        for cp in recvs:
            cp.wait_recv()
        for cp in sends:
            cp.wait_send()
        for cp in locals_:
            cp.wait()

    return pl.pallas_call(
        body, out_shape=[jax.ShapeDtypeStruct((2 * r.shape[0], r.shape[1]), r.dtype) for r in rs],
        in_specs=[_ANY] * n, out_specs=[_ANY] * n,
        scratch_shapes=[pltpu.SemaphoreType.DMA((n,)), pltpu.SemaphoreType.DMA((n,)), pltpu.SemaphoreType.DMA((n,))],
        name="rs_join_halves")(*rs)


def _pack(parts, row_mult=8):
    flat = jnp.concatenate([p.reshape(-1).astype(F32) for p in parts])
    unit = row_mult * 128
    n = -(-flat.shape[0] // unit) * unit
    return jnp.pad(flat, (0, n - flat.shape[0])).reshape(n // 128, 128)


def _unpack(flat, shapes):
    out, off = [], 0
    for s in shapes:
        n = int(np.prod(s))
        out.append(flat[off:off + n].reshape(s))
        off += n
    return out


def _gather_packed(parts, name):
    packed = _pack(parts)
    g = allgather_small(packed, name).reshape(N_DEV, -1)
    return _unpack_rows(g, [p.shape for p in parts])


def _unpack_rows(g, shapes):
    out, off = [], 0
    for s in shapes:
        n = int(np.prod(s))
        out.append(g[:, off:off + n].reshape((g.shape[0],) + tuple(s)))
        off += n
    return out


def _by_chip(t, axis):
    return jnp.concatenate([t[2 * p] for p in range(N_CHIPS)], axis=axis)


def kernel(x, c, ada_w, ada_b, ln_g, ln_b, a_in_w, a_conv_w, a_conv_b, a_dt_bias, a_A_log, a_D, a_norm_g, a_out_w, kv_w, b_in_w, b_out_w, loss_target, m_ada_w, m_ada_b, m_ln_g, m_ln_b, m_a_in_w, m_a_conv_w, m_a_conv_b, m_a_dt_bias, m_a_A_log, m_a_D, m_a_norm_g, m_a_out_w, m_kv_w, m_b_in_w, m_b_out_w, v_ada_w, v_ada_b, v_ln_g, v_ln_b, v_a_in_w, v_a_conv_w, v_a_conv_b, v_a_dt_bias, v_a_A_log, v_a_D, v_a_norm_g, v_a_out_w, v_kv_w, v_b_in_w, v_b_out_w):
    ax, ay, ac = lax.axis_index("x"), lax.axis_index("y"), lax.axis_index("c")
    chip = 2 * ax + ay
    dev = 4 * ax + 2 * ay + ac
    xin = x[0]
    tgt = loss_target[0]
    L, D = xin.shape
    G, P = SSD_G, SSD_P
    H = a_dt_bias.shape[1]
    Kh = H // G
    DI = H * P
    CONVD = a_conv_b.shape[1] * N_CHIPS
    HW = DIL_H * DIL_E
    Ws = ada_w.shape[2]

    w_in_g, w_out_g, w_kv_g, w_bin_g, w_bout_g = allgather_weights(
        [a_in_w[0].astype(BF16), a_out_w[0].astype(BF16), kv_w.astype(BF16), b_in_w[0].astype(BF16),
         b_out_w[0].astype(BF16)])
    w_in = jnp.transpose(w_in_g, (1, 0, 2)).reshape(D, -1)
    w_zx = w_in[:, :DI + CONVD]
    w_dt = jnp.pad(w_in[:, DI + CONVD:], ((0, 0), (0, 128 - H)))

    c8, cw8, cb8, ng8 = _gather_packed([c[0], a_conv_w[0], a_conv_b[0], a_norm_g[0]], "allgather_small_params")
    conv_w = _by_chip(cw8, 1)
    conv_b = _by_chip(cb8, 0).reshape(1, CONVD)
    norm_g = _by_chip(ng8, 0).reshape(1, DI)

    mod_s = ada_fwd(c8, ada_w)
    (mod8,) = _gather_packed([mod_s], "allgather_small_mod")
    mods = _by_chip(mod8, 2)
    mod = lax.dynamic_index_in_dim(mods, dev, axis=1, keepdims=False) + ada_b
    shift = [mod[l:l + 1, :D] for l in range(DEPTH)]
    scale = [mod[l:l + 1, D:2 * D] for l in range(DEPTH)]
    gate = [mod[l:l + 1, 2 * D:] for l in range(DEPTH)]
    lg = [ln_g[l:l + 1] for l in range(DEPTH)]
    lb = [ln_b[l:l + 1] for l in range(DEPTH)]

    h0 = modulate(xin, scale[0], shift[0], "modulate0")
    zx = mm_nn(h0, w_zx, BF16, "mm_in_zx")
    dtp = mm_nn(h0, w_dt, F32, "mm_in_dt")
    xbc = conv_fwd(zx, DI, conv_w, conv_b)
    dtp_g = jnp.transpose(dtp[:, :H].reshape(L, G, Kh), (1, 0, 2))
    dtp_gT = jnp.transpose(dtp_g, (0, 2, 1))
    vecs = [a_dt_bias.reshape(G, 1, Kh), a_dt_bias.reshape(G, Kh, 1), a_A_log.reshape(G, 1, Kh),
            a_A_log.reshape(G, Kh, 1), a_D.reshape(G, 1, Kh)]
    y_ssd, states = ssd_fwd(xbc, dtp_g, dtp_gT, *vecs, DI)
    yn = rms_gate_fwd(y_ssd, zx, norm_g)
    ymix0 = mm_nn(yn, w_out_g, F32, "mm_out_a", stack="row")
    x1, x1b, h1 = ln_mid(xin, ymix0, gate[0], lg[0], lb[0], scale[1], shift[1])

    kvp = mm_nn(x1b, w_kv_g, BF16, "mm_kv", stack="col")
    qz = mm_nn(h1, w_bin_g, BF16, "mm_in_b", stack="col")
    os_, lses = [], []
    for gi in range(len(DIL_PATTERNS)):
        o, lse = attn_fwd(qz, kvp, gi)
        os_.append(o)
        lses.append(lse)
    om = merge_fwd(os_, lses, qz)
    ymix1 = mm_nn(om, w_bout_g, F32, "mm_out_b", stack="col")
    dx2, sq = ln_final(x1, ymix1, gate[1], lg[1], lb[1], tgt)
    loss_part = 0.5 * jnp.sum(sq) / D

    dres2, dy2, dg1, db1, dgate1 = ln_bwd(dx2, x1, ymix1, gate[1], lg[1], "ln_bwd1")
    g_bout = mm_tn(om, dy2, BF16, "mm_gw_out_b", stack="col")
    dgated = mm_nt(dy2, w_bout_g, BF16, "mm_gx_out_b", stack="col")
    dos, dprs, dz_b = merge_bwd(dgated, os_, lses, qz)
    dqs, dks, dvs = [], [], []
    for gi in range(len(DIL_PATTERNS)):
        dq, dk, dv = attn_bwd(qz, kvp, dos[gi], lses[gi], dprs[gi], gi)
        dqs.append(dq)
        dks.append(dk)
        dvs.append(dv)
    dqz = jnp.concatenate(dqs + [dz_b], axis=1)
    dkv = jnp.concatenate(dks + dvs, axis=1)
    g_bin = mm_tn(h1, dqz, BF16, "mm_gw_in_b", stack="col")
    dh1 = mm_nt(dqz, w_bin_g, F32, "mm_gx_in_b", stack="col")
    g_kv = mm_tn(x1b, dkv, BF16, "mm_gw_kv", stack="col")
    dx1_kv = mm_nt(dkv, w_kv_g, F32, "mm_gx_kv", stack="col")
    dx1, dscale1, dshift1 = mod_bwd(dres2, dh1, dx1_kv, x1, scale[1], "mod_bwd1", through_mod=False)

    dres1, dy1, dg0, db0, dgate0 = ln_bwd(dx1, xin, ymix0, gate[0], lg[0], "ln_bwd0")
    g_out = mm_tn(yn, dy1, BF16, "mm_gw_out_a", stack="row")
    dyn = mm_nt(dy1, w_out_g, BF16, "mm_gx_out_a", stack="row")
    dy_ssd, dz_a, dnorm_g = rms_gate_bwd(dyn, y_ssd, zx, norm_g)
    dxs, dB, dC, ddtp_g, dbias_g, dalog_g, dD_g = ssd_bwd(xbc, dtp_g, dtp_gT, *vecs, states, dy_ssd, DI)
    dxbc = jnp.concatenate([dxs, dB, dC], axis=1)
    dxbc_pre, dconv_w, dconv_b = conv_bwd(zx, DI, conv_w, conv_b, dxbc)
    dzx = jnp.concatenate([dz_a, dxbc_pre], axis=1)
    ddtp = jnp.pad(jnp.transpose(ddtp_g, (1, 0, 2)).reshape(L, H), ((0, 0), (0, 128 - H)))
    g_zx = mm_tn(h0, dzx, BF16, "mm_gw_in_zx")
    g_dt = mm_tn(h0, ddtp, BF16, "mm_gw_in_dt")
    dh0 = mm_nt(dzx, w_zx, F32, "mm_gx_in_zx")
    dh0_dt = mm_nt(ddtp, w_dt, F32, "mm_gx_in_dt")
    grad_x, dscale0, dshift0 = mod_bwd(dres1, dh0, dh0_dt, xin, scale[0], "mod_bwd0", through_mod=True)
    g_in = jnp.concatenate([g_zx, g_dt[:, :H]], axis=1)
    g_in = jnp.transpose(g_in.reshape(D, N_CHIPS, -1), (1, 0, 2))

    gs = [g_in, g_out, g_kv, g_bin, g_bout]
    names = ["in_a", "out_a", "kv", "in_b", "out_b"]
    core = ac.astype(jnp.int32).reshape(1)
    sib = exchange_halves_to_sibling(gs)
    parts = [add_half(g, a, core, "rs_add_" + nm) for g, a, nm in zip(gs, sib, names)]
    landed = scatter_to_chips(parts)
    halves = [sum_leading(t, "rs_sum_" + nm) for t, nm in zip(landed, names)]
    g_in_s, g_out_s, g_kv_s, g_bin_s, g_bout_s = join_halves(halves)

    dmod = jnp.concatenate([jnp.concatenate([dshift0, dscale0, dgate0], axis=1),
                            jnp.concatenate([dshift1, dscale1, dgate1], axis=1)], axis=0)
    small_parts = [jnp.concatenate([dg0, dg1], axis=0), jnp.concatenate([db0, db1], axis=0),
                   dbias_g.reshape(1, H), dalog_g.reshape(1, H), dD_g.reshape(1, H),
                   dconv_w, dconv_b, dnorm_g, loss_part.reshape(1, 1)]
    small_shapes = [p.shape for p in small_parts]
    packed = jnp.concatenate([_pack([dmod]), _pack(small_parts)], axis=0)
    n_mod_rows = _pack([dmod]).shape[0]
    gathered = allgather_small(packed, "allgather_small_grads").reshape(N_DEV, -1, 128)
    dmod8 = gathered[:, :n_mod_rows].reshape(N_DEV, -1)[:, :2 * 3 * D].reshape(N_DEV, DEPTH, 3 * D)
    summed = sum_leading(gathered, "sum_small")
    g_ada_b = summed[:n_mod_rows].reshape(-1)[:2 * 3 * D].reshape(DEPTH, 3 * D)
    (g_ln_g, g_ln_b, g_dt_bias, g_a_log, g_dsk, g_conv_w, g_conv_b, g_norm_g, loss_all) = _unpack(
        summed[n_mod_rows:].reshape(-1), small_shapes)
    loss = loss_all.reshape(())
    Cs = CONVD // N_CHIPS
    g_conv_w_s = lax.dynamic_slice_in_dim(g_conv_w, chip * Cs, Cs, axis=1)
    g_conv_b_s = lax.dynamic_slice_in_dim(g_conv_b, chip * Cs, Cs, axis=1)
    g_norm_g_s = lax.dynamic_slice_in_dim(g_norm_g, chip * (DI // N_CHIPS), DI // N_CHIPS, axis=1)
    dmod_s = jnp.transpose(lax.dynamic_slice_in_dim(dmod8, chip * Ws, Ws, axis=2), (1, 0, 2))
    g_ada_w = ada_wgrad(jnp.transpose(c8), dmod_s)

    def step2d(w, g, m, v, nm):
        shp = w.shape
        d_, m_, v_ = adamw(w.reshape(-1, shp[-1]), g.reshape(-1, shp[-1]), m.reshape(-1, shp[-1]),
                           v.reshape(-1, shp[-1]), "adamw_" + nm)
        return g.reshape(shp), d_.reshape(shp), m_.reshape(shp), v_.reshape(shp)

    big = {
        "ada_w": step2d(ada_w, g_ada_w, m_ada_w, v_ada_w, "ada_w"),
        "a_in_w": step2d(a_in_w, g_in_s, m_a_in_w, v_a_in_w, "in_a"),
        "a_out_w": step2d(a_out_w, g_out_s, m_a_out_w, v_a_out_w, "out_a"),
        "kv_w": step2d(kv_w, g_kv_s, m_kv_w, v_kv_w, "kv"),
        "b_in_w": step2d(b_in_w, g_bin_s, m_b_in_w, v_b_in_w, "in_b"),
        "b_out_w": step2d(b_out_w, g_bout_s, m_b_out_w, v_b_out_w, "out_b"),
    }
    small_names = ["ada_b", "ln_g", "ln_b", "a_conv_w", "a_conv_b", "a_dt_bias", "a_A_log", "a_D", "a_norm_g"]
    small_w = [ada_b, ln_g, ln_b, a_conv_w, a_conv_b, a_dt_bias, a_A_log, a_D, a_norm_g]
    small_m = [m_ada_b, m_ln_g, m_ln_b, m_a_conv_w, m_a_conv_b, m_a_dt_bias, m_a_A_log, m_a_D, m_a_norm_g]
    small_v = [v_ada_b, v_ln_g, v_ln_b, v_a_conv_w, v_a_conv_b, v_a_dt_bias, v_a_A_log, v_a_D, v_a_norm_g]
    small_g = [g_ada_b, g_ln_g, g_ln_b, g_conv_w_s, g_conv_b_s, g_dt_bias, g_a_log, g_dsk, g_norm_g_s]
    shapes = [w.shape for w in small_w]
    small_g = [g.reshape(s) for g, s in zip(small_g, shapes)]
    d_p, m_p, v_p = adamw(_pack(small_w), _pack(small_g), _pack(small_m), _pack(small_v), "adamw_small")
    small = {}
    for nm, g, d_, m_, v_ in zip(small_names, small_g, _unpack(d_p.reshape(-1), shapes), _unpack(m_p.reshape(-1), shapes),
                                 _unpack(v_p.reshape(-1), shapes)):
        small[nm] = (g, d_, m_, v_)
    allw = {**big, **small}
    order = ["ada_w", "ada_b", "ln_g", "ln_b", "a_in_w", "a_conv_w", "a_conv_b", "a_dt_bias", "a_A_log", "a_D",
             "a_norm_g", "a_out_w", "kv_w", "b_in_w", "b_out_w"]
    outs = [loss, grad_x.reshape(x.shape)]
    for k in range(4):
        outs += [allw[n][k] for n in order]
    return tuple(outs)
```

```python
import functools

import jax
import jax.numpy as jnp
import numpy as np
from jax import lax
from jax.experimental import pallas as pl
from jax.experimental.pallas import tpu as pltpu

F32 = jnp.float32
BF16 = jnp.bfloat16
MESH = pl.DeviceIdType.MESH

DEPTH = 2
ALPHA = (2 * DEPTH) ** 0.25
LN_EPS = 1e-5
RMS_EPS = 1e-5
SSD_P = 64
SSD_N = 128
SSD_Q = 256
SSD_G = 8
CONV_W = 4
DIL_PATTERNS = ((128, 1), (512, 4), (2048, 16))
DIL_H = 8
DIL_E = 128
DIL_BLK = 128
ADAM_LR, ADAM_B1, ADAM_B2, ADAM_EPS, ADAM_WD, ADAM_STEP = 0.001, 0.9, 0.999, 1e-08, 0.01, 10

VMEM_LIMIT = 56 * 1024 * 1024
N_CHIPS = 4
N_DEV = 8


def _tile(dim, target, mult=128):
    if dim <= target:
        return dim
    t = (target // mult) * mult
    while t >= mult:
        if dim % t == 0:
            return t
        t -= mult
    return dim


def _cp(sem):
    return pltpu.CompilerParams(dimension_semantics=sem, vmem_limit_bytes=VMEM_LIMIT)


def _sigmoid(x):
    return 1.0 / (1.0 + jnp.exp(-x))


def _silu(x):
    return x * _sigmoid(x)


def _dsilu(x):
    s = _sigmoid(x)
    return s * (1.0 + x * (1.0 - s))


def _softplus(x):
    return jnp.maximum(x, 0.0) + jnp.log(1.0 + jnp.exp(-jnp.abs(x)))


def _mm_call(a, b, out_shape, grid, a_spec, b_spec, o_spec, acc_shape, dims, name):
    nk = grid[2]

    def body(a_ref, b_ref, o_ref, acc_ref):
        k = pl.program_id(2)

        @pl.when(k == 0)
        def _():
            acc_ref[...] = jnp.zeros(acc_ref.shape, F32)

        acc_ref[...] += lax.dot_general(a_ref[...].astype(BF16), b_ref[...].astype(BF16), (dims, ((), ())),
                                        preferred_element_type=F32)

        @pl.when(k == nk - 1)
        def _():
            o_ref[...] = acc_ref[...].astype(o_ref.dtype)

    return pl.pallas_call(
        body, grid=grid, in_specs=[a_spec, b_spec], out_specs=o_spec, out_shape=out_shape,
        scratch_shapes=[pltpu.VMEM(acc_shape, F32)],
        compiler_params=_cp(("parallel", "parallel", "arbitrary")), name=name)(a, b)


def mm_nn(a, b, out_dtype, name, stack=None, tm=1024, tn=1024, tk=512):
    M, K = a.shape
    if stack is None:
        N = b.shape[1]
        tn, tk = _tile(N, tn), _tile(K, tk)
        b_spec = pl.BlockSpec((tk, tn), lambda i, j, k: (k, j))
    elif stack == "col":
        S, _, Ns = b.shape
        N = S * Ns
        tn, tk = _tile(Ns, tn), _tile(K, tk)
        npb = Ns // tn
        b_spec = pl.BlockSpec((None, tk, tn), lambda i, j, k: (j // npb, k, j % npb))
    else:
        S, Ks, N = b.shape
        tn, tk = _tile(N, tn), _tile(Ks, tk)
        kpb = Ks // tk
        b_spec = pl.BlockSpec((None, tk, tn), lambda i, j, k: (k // kpb, k % kpb, j))
    tm = _tile(M, tm)
    return _mm_call(a, b, jax.ShapeDtypeStruct((M, N), out_dtype), (M // tm, N // tn, K // tk),
                    pl.BlockSpec((tm, tk), lambda i, j, k: (i, k)), b_spec,
                    pl.BlockSpec((tm, tn), lambda i, j, k: (i, j)), (tm, tn), ((1,), (0,)), name)


def mm_nt(a, b, out_dtype, name, stack=None, tm=1024, tn=1024, tk=512):
    M, C = a.shape
    if stack is None:
        Kw = b.shape[0]
        tn, tk = _tile(Kw, tn), _tile(C, tk)
        b_spec = pl.BlockSpec((tn, tk), lambda i, j, k: (j, k))
    elif stack == "col":
        S, Kw, Cs = b.shape
        tn, tk = _tile(Kw, tn), _tile(Cs, tk)
        cpb = Cs // tk
        b_spec = pl.BlockSpec((None, tn, tk), lambda i, j, k: (k // cpb, j, k % cpb))
    else:
        S, Ks, _ = b.shape
        Kw = S * Ks
        tn, tk = _tile(Ks, tn), _tile(C, tk)
        jpb = Ks // tn
        b_spec = pl.BlockSpec((None, tn, tk), lambda i, j, k: (j // jpb, j % jpb, k))
    tm = _tile(M, tm)
    return _mm_call(a, b, jax.ShapeDtypeStruct((M, Kw), out_dtype), (M // tm, Kw // tn, C // tk),
                    pl.BlockSpec((tm, tk), lambda i, j, k: (i, k)), b_spec,
                    pl.BlockSpec((tm, tn), lambda i, j, k: (i, j)), (tm, tn), ((1,), (1,)), name)


def mm_tn(a, b, out_dtype, name, stack=None, n_stack=N_CHIPS, tm=1024, tn=1024, tk=512):
    L, M = a.shape
    N = b.shape[1]
    tk = _tile(L, tk)
    if stack is None:
        tm, tn = _tile(M, tm), _tile(N, tn)
        o_spec = pl.BlockSpec((tm, tn), lambda i, j, k: (i, j))
        out_shape = (M, N)
    elif stack == "col":
        Ns = N // n_stack
        tm, tn = _tile(M, tm), _tile(Ns, tn)
        npb = Ns // tn
        o_spec = pl.BlockSpec((None, tm, tn), lambda i, j, k: (j // npb, i, j % npb))
        out_shape = (n_stack, M, Ns)
    else:
        Ms = M // n_stack
        tm, tn = _tile(Ms, tm), _tile(N, tn)
        mpb = Ms // tm
        o_spec = pl.BlockSpec((None, tm, tn), lambda i, j, k: (i // mpb, i % mpb, j))
        out_shape = (n_stack, Ms, N)
    return _mm_call(a, b, jax.ShapeDtypeStruct(out_shape, out_dtype), (M // tm, N // tn, L // tk),
                    pl.BlockSpec((tk, tm), lambda i, j, k: (k, i)), pl.BlockSpec((tk, tn), lambda i, j, k: (k, j)),
                    o_spec, (tm, tn), ((0,), (0,)), name)


def _row_specs(tr, widths):
    return [pl.BlockSpec((tr, w), lambda i: (i, 0)) for w in widths]


def _vec_spec(w):
    return pl.BlockSpec((1, w), lambda i: (0, 0))


def _acc_rows(ref, val, i):
    s = jnp.sum(val, axis=0, keepdims=True)

    @pl.when(i == 0)
    def _():
        ref[...] = s

    @pl.when(i > 0)
    def _():
        ref[...] += s


def modulate(x, scale, shift, name):
    L, D = x.shape
    tr = _tile(L, 512, 16)

    def body(x_ref, sc_ref, sh_ref, h_ref):
        h_ref[...] = (x_ref[...] * (1.0 + sc_ref[...]) + sh_ref[...]).astype(BF16)

    return pl.pallas_call(
        body, grid=(L // tr,), in_specs=_row_specs(tr, [D]) + [_vec_spec(D)] * 2, out_specs=_row_specs(tr, [D])[0],
        out_shape=jax.ShapeDtypeStruct((L, D), BF16), compiler_params=_cp(("parallel",)), name=name)(x, scale, shift)


def _ln_core(x, y, gate, g, b):
    u = ALPHA * x + (1.0 + gate) * y
    mu = jnp.mean(u, axis=-1, keepdims=True)
    d = u - mu
    var = jnp.mean(d * d, axis=-1, keepdims=True)
    rstd = lax.rsqrt(var + LN_EPS)
    xhat = d * rstd
    return xhat * g + b, xhat, rstd


def ln_mid(x, y, gate, g, b, scale, shift):
    L, D = x.shape
    tr = _tile(L, 256, 16)

    def body(x_ref, y_ref, gate_ref, g_ref, b_ref, sc_ref, sh_ref, x1_ref, x1b_ref, h_ref):
        x1, _, _ = _ln_core(x_ref[...], y_ref[...], gate_ref[...], g_ref[...], b_ref[...])
        x1_ref[...] = x1
        x1b_ref[...] = x1.astype(BF16)
        h_ref[...] = (x1 * (1.0 + sc_ref[...]) + sh_ref[...]).astype(BF16)

    return pl.pallas_call(
        body, grid=(L // tr,), in_specs=_row_specs(tr, [D, D]) + [_vec_spec(D)] * 5,
        out_specs=_row_specs(tr, [D, D, D]),
        out_shape=[jax.ShapeDtypeStruct((L, D), F32), jax.ShapeDtypeStruct((L, D), BF16),
                   jax.ShapeDtypeStruct((L, D), BF16)],
        compiler_params=_cp(("parallel",)), name="ln_mid")(x, y, gate, g, b, scale, shift)


def ln_final(x, y, gate, g, b, target):
    L, D = x.shape
    tr = _tile(L, 256, 16)

    def body(x_ref, y_ref, gate_ref, g_ref, b_ref, t_ref, dout_ref, sq_ref):
        out, _, _ = _ln_core(x_ref[...], y_ref[...], gate_ref[...], g_ref[...], b_ref[...])
        err = out - t_ref[...]
        dout_ref[...] = err * (1.0 / D)
        _acc_rows(sq_ref, err * err, pl.program_id(0))

    return pl.pallas_call(
        body, grid=(L // tr,), in_specs=_row_specs(tr, [D, D]) + [_vec_spec(D)] * 3 + _row_specs(tr, [D]),
        out_specs=[_row_specs(tr, [D])[0], _vec_spec(D)],
        out_shape=[jax.ShapeDtypeStruct((L, D), F32), jax.ShapeDtypeStruct((1, D), F32)],
        compiler_params=_cp(("arbitrary",)), name="ln_final")(x, y, gate, g, b, target)


def ln_bwd(dout, x, y, gate, g, name):
    L, D = x.shape
    tr = _tile(L, 256, 16)

    def body(do_ref, x_ref, y_ref, gate_ref, g_ref, dres_ref, dy_ref, dg_ref, db_ref, dgate_ref):
        i = pl.program_id(0)
        yv = y_ref[...]
        dout_v = do_ref[...]
        _, xhat, rstd = _ln_core(x_ref[...], yv, gate_ref[...], g_ref[...], 0.0)
        dxh = dout_v * g_ref[...]
        m1 = jnp.mean(dxh, axis=-1, keepdims=True)
        m2 = jnp.mean(dxh * xhat, axis=-1, keepdims=True)
        du = rstd * (dxh - m1 - xhat * m2)
        dres_ref[...] = ALPHA * du
        dy_ref[...] = ((1.0 + gate_ref[...]) * du).astype(BF16)
        _acc_rows(dg_ref, dout_v * xhat, i)
        _acc_rows(db_ref, dout_v, i)
        _acc_rows(dgate_ref, du * yv, i)

    return pl.pallas_call(
        body, grid=(L // tr,), in_specs=_row_specs(tr, [D, D, D]) + [_vec_spec(D)] * 2,
        out_specs=_row_specs(tr, [D, D]) + [_vec_spec(D)] * 3,
        out_shape=[jax.ShapeDtypeStruct((L, D), F32), jax.ShapeDtypeStruct((L, D), BF16)]
        + [jax.ShapeDtypeStruct((1, D), F32)] * 3,
        compiler_params=_cp(("arbitrary",)), name=name)(dout, x, y, gate, g)


def mod_bwd(dres, dh, dh2, xin, scale, name, through_mod):
    L, D = xin.shape
    tr = _tile(L, 256, 16)

    def body(dres_ref, dh_ref, dh2_ref, x_ref, sc_ref, dx_ref, dsc_ref, dsh_ref):
        i = pl.program_id(0)
        dh_v = dh_ref[...]
        tot = dres_ref[...]
        if through_mod:
            dh_v = dh_v + dh2_ref[...]
        else:
            tot = tot + dh2_ref[...]
        dx_ref[...] = tot + dh_v * (1.0 + sc_ref[...])
        _acc_rows(dsc_ref, dh_v * x_ref[...], i)
        _acc_rows(dsh_ref, dh_v, i)

    return pl.pallas_call(
        body, grid=(L // tr,), in_specs=_row_specs(tr, [D, D, D, D]) + [_vec_spec(D)],
        out_specs=_row_specs(tr, [D]) + [_vec_spec(D)] * 2,
        out_shape=[jax.ShapeDtypeStruct((L, D), F32)] + [jax.ShapeDtypeStruct((1, D), F32)] * 2,
        compiler_params=_cp(("arbitrary",)), name=name)(dres, dh, dh2, xin, scale)


CONV_HALO = 16


def _conv_rows(x_ref, i, tr, L):
    nblk = L // tr
    s = pl.multiple_of(i * tr, CONV_HALO)
    cur = x_ref[pl.ds(s, tr), :].astype(F32)
    sp = pl.multiple_of(jnp.maximum(i * tr - CONV_HALO, 0), CONV_HALO)
    sn = pl.multiple_of(jnp.minimum(i * tr + tr, L - CONV_HALO), CONV_HALO)
    prev = x_ref[pl.ds(sp, CONV_HALO), :].astype(F32) * (i > 0).astype(F32)
    nxt = x_ref[pl.ds(sn, CONV_HALO), :].astype(F32) * (i < nblk - 1).astype(F32)
    return jnp.concatenate([prev, cur, nxt], axis=0)


def _shift_rows(v, j):
    n = v.shape[0]
    return v if j % n == 0 else pltpu.roll(v, j % n, 0)


def _conv_eval(xe, w_ref, b_ref):
    c = b_ref[...] + w_ref[CONV_W - 1:CONV_W, :] * xe
    for k in range(CONV_W - 1):
        c = c + w_ref[k:k + 1, :] * _shift_rows(xe, CONV_W - 1 - k)
    return c


def conv_fwd(zx, col0, conv_w, conv_b):
    L = zx.shape[0]
    C = conv_w.shape[1]
    tc = _tile(C, 512)
    tr = _tile(L, 512, CONV_HALO)
    off = col0 // tc

    def body(x_ref, w_ref, b_ref, o_ref):
        i = pl.program_id(1)
        xe = _conv_rows(x_ref, i, tr, L)
        c = _conv_eval(xe, w_ref, b_ref)[CONV_HALO:CONV_HALO + tr]
        o_ref[...] = _silu(c).astype(BF16)

    return pl.pallas_call(
        body, grid=(C // tc, L // tr),
        in_specs=[pl.BlockSpec((L, tc), lambda j, i: (0, off + j)), pl.BlockSpec((CONV_W, tc), lambda j, i: (0, j)),
                  pl.BlockSpec((1, tc), lambda j, i: (0, j))],
        out_specs=pl.BlockSpec((tr, tc), lambda j, i: (i, j)),
        out_shape=jax.ShapeDtypeStruct((L, C), BF16), compiler_params=_cp(("parallel", "arbitrary")),
        name="conv_fwd")(zx, conv_w, conv_b)


def conv_bwd(zx, col0, conv_w, conv_b, dxbc):
    L = zx.shape[0]
    C = conv_w.shape[1]
    tc = _tile(C, 512)
    tr = _tile(L, 512, CONV_HALO)
    off = col0 // tc
    H = CONV_HALO

    def body(x_ref, g_ref, w_ref, b_ref, dx_ref, dw_ref, db_ref):
        i = pl.program_id(1)
        xe = _conv_rows(x_ref, i, tr, L)
        ge = _conv_rows(g_ref, i, tr, L)
        dc = ge * _dsilu(_conv_eval(xe, w_ref, b_ref))
        dx = w_ref[CONV_W - 1:CONV_W, :] * dc
        for k in range(CONV_W - 1):
            dx = dx + w_ref[k:k + 1, :] * _shift_rows(dc, -(CONV_W - 1 - k))
        dx_ref[...] = dx[H:H + tr].astype(BF16)
        dcc = dc[H:H + tr]
        rows = [jnp.sum(dcc * _shift_rows(xe, CONV_W - 1 - k)[H:H + tr], axis=0, keepdims=True) for k in range(CONV_W)]
        dwv = jnp.concatenate(rows + [jnp.zeros((8 - CONV_W, tc), F32)], axis=0)
        dbv = jnp.sum(dcc, axis=0, keepdims=True)

        @pl.when(i == 0)
        def _():
            dw_ref[...] = dwv
            db_ref[...] = dbv

        @pl.when(i > 0)
        def _():
            dw_ref[...] += dwv
            db_ref[...] += dbv

    dx, dw, db = pl.pallas_call(
        body, grid=(C // tc, L // tr),
        in_specs=[pl.BlockSpec((L, tc), lambda j, i: (0, off + j)), pl.BlockSpec((L, tc), lambda j, i: (0, j)),
                  pl.BlockSpec((CONV_W, tc), lambda j, i: (0, j)), pl.BlockSpec((1, tc), lambda j, i: (0, j))],
        out_specs=[pl.BlockSpec((tr, tc), lambda j, i: (i, j)), pl.BlockSpec((8, tc), lambda j, i: (0, j)),
                   pl.BlockSpec((1, tc), lambda j, i: (0, j))],
        out_shape=[jax.ShapeDtypeStruct((L, C), BF16), jax.ShapeDtypeStruct((8, C), F32),
                   jax.ShapeDtypeStruct((1, C), F32)],
        compiler_params=_cp(("parallel", "arbitrary")), name="conv_bwd")(zx, dxbc, conv_w, conv_b)
    return dx, dw[:CONV_W], db


_NN = (((1,), (0,)), ((), ()))


def _pieces(x, n):
    out, r = [], x
    for _ in range(n):
        p = r.astype(BF16)
        out.append(p)
        r = r - p.astype(F32)
    return out


def _dot01(a, b01, n, dims=_NN):
    b = b01.astype(BF16)
    return functools.reduce(lambda u, v: u + v,
                            [lax.dot_general(p, b, dims, preferred_element_type=F32) for p in _pieces(a, n)])


def _dot01_left(a01, b, n, dims=_NN):
    a = a01.astype(BF16)
    return functools.reduce(lambda u, v: u + v,
                            [lax.dot_general(a, p, dims, preferred_element_type=F32) for p in _pieces(b, n)])


def _ssd_common(dtp_ref, dtpT_ref, bias_ref, biasT_ref, alog_ref, alogT_ref, b_ref, c_ref):
    Q = SSD_Q
    dt = _softplus(dtp_ref[...] + bias_ref[...])
    A = -jnp.exp(alog_ref[...])
    row = lax.broadcasted_iota(jnp.int32, (Q, Q), 0)
    col = lax.broadcasted_iota(jnp.int32, (Q, Q), 1)
    causal = row >= col
    tril = causal.astype(F32)
    Kh = dt.shape[1]
    acum = _dot01_left(tril, dt * A, 3)
    eye = (lax.broadcasted_iota(jnp.int32, (Kh, Kh), 0) == lax.broadcasted_iota(jnp.int32, (Kh, Kh), 1)).astype(F32)
    acumT = _dot01_left(eye, acum, 3, dims=(((1,), (1,)), ((), ())))
    Bm = b_ref[...]
    Cm = c_ref[...]
    cb = lax.dot_general(Cm, Bm, (((1,), (1,)), ((), ())), preferred_element_type=F32)
    return dt, A, causal, row, col, acum, acumT, Bm, Cm, cb


def _ssd_in_specs(Q, GP, N, Kh, DI, cmap):
    nb0 = DI // N
    vec = pl.BlockSpec((None, 1, Kh), lambda g, c: (g, 0, 0))
    vecT = pl.BlockSpec((None, Kh, 1), lambda g, c: (g, 0, 0))
    return [pl.BlockSpec((Q, GP), lambda g, c: (cmap(c), g)),
            pl.BlockSpec((Q, N), lambda g, c: (cmap(c), nb0 + g)),
            pl.BlockSpec((Q, N), lambda g, c: (cmap(c), nb0 + SSD_G + g)),
            pl.BlockSpec((None, Q, Kh), lambda g, c: (g, cmap(c), 0)),
            pl.BlockSpec((None, Kh, Q), lambda g, c: (g, 0, cmap(c))),
            vec, vecT, vec, vecT, vec, vecT]


def _hi(a, b01):
    return _dot01(a, b01, 2)


def _ssd_heads(dskT_ref, acum, acumT, dt, Kh):
    Q, P, N = SSD_Q, SSD_P, SSD_N
    GP = Kh * P
    sh_p = P.bit_length() - 1
    seg = lambda shape, dim: lax.shift_right_logical(lax.broadcasted_iota(jnp.int32, shape, dim), sh_p)
    E = (seg((Kh, GP), 1) == lax.broadcasted_iota(jnp.int32, (Kh, GP), 0)).astype(F32)
    ET = (seg((GP, Kh), 0) == lax.broadcasted_iota(jnp.int32, (GP, Kh), 1)).astype(F32)
    a_last = acum[Q - 1:Q, :]
    tail = jnp.exp(a_last - acum)
    eLT = jnp.exp(acumT[:, Q - 1:Q])
    rowseg = seg((GP, N), 0)
    eL_b = jnp.zeros((GP, N), F32)
    for k in range(Kh):
        eL_b = jnp.where(rowseg == k, eLT[k:k + 1, :], eL_b)
    return dict(
        E=E, ET=ET, a_last=a_last, tail=tail, eL_b=eL_b,
        dt_all=_hi(dt, E), ea_all=_hi(jnp.exp(acum), E), tail_all=_hi(tail, E),
        dsk_all=jnp.sum(E * dskT_ref[...], axis=0, keepdims=True))


def _head_chunks(GP):
    CW = min(GP, 128)
    return CW, CW // SSD_P, GP // CW


def _head_mask(Q, CW, kk):
    lane = lax.broadcasted_iota(jnp.int32, (Q, CW), 1)
    return jnp.logical_and(lane >= kk * SSD_P, lane < (kk + 1) * SSD_P)


def ssd_fwd(xbc, dtp_g, dtp_gT, bias_g, bias_gT, alog_g, alog_gT, dsk_g, dsk_gT, DI):
    L = xbc.shape[0]
    Q, P, N, G = SSD_Q, SSD_P, SSD_N, SSD_G
    GP = DI // G
    Kh = GP // P
    nc = L // Q

    CW, hpc, nch = _head_chunks(GP)
    nt = (((1,), (1,)), ((), ()))
    tn = (((0,), (0,)), ((), ()))

    def body(xs_ref, b_ref, c_ref, dtp_ref, dtpT_ref, bias_ref, biasT_ref, alog_ref, alogT_ref, dsk_ref, dskT_ref,
             y_ref, st_ref, state):
        @pl.when(pl.program_id(1) == 0)
        def _():
            state[...] = jnp.zeros(state.shape, F32)

        st_ref[...] = state[...]
        dt, A, causal, row, col, acum, acumT, Bm, Cm, cb = _ssd_common(
            dtp_ref, dtpT_ref, bias_ref, biasT_ref, alog_ref, alogT_ref, b_ref, c_ref)
        hd = _ssd_heads(dskT_ref, acum, acumT, dt, Kh)
        xs = xs_ref[...].astype(F32)
        xdt_all = xs * hd["dt_all"]
        S_all = state[...]
        y_all = (lax.dot_general(Cm, S_all.astype(BF16), nt, preferred_element_type=F32) * hd["ea_all"]
                 + xs * hd["dsk_all"])
        state[...] = S_all * hd["eL_b"] + lax.dot_general(
            (xdt_all * hd["tail_all"]).astype(BF16), Bm, tn, preferred_element_type=F32)
        for ch in range(nch):
            cs = slice(ch * CW, (ch + 1) * CW)
            xc = xdt_all[:, cs]
            acc = y_all[:, cs]
            for kk in range(hpc):
                k = ch * hpc + kk
                decay = jnp.exp(jnp.where(causal, acum[:, k:k + 1] - acumT[k:k + 1, :], -jnp.inf))
                xk = xc if hpc == 1 else jnp.where(_head_mask(Q, CW, kk), xc, 0.0)
                acc = acc + jnp.dot((cb * decay).astype(BF16), xk.astype(BF16), preferred_element_type=F32)
            y_ref[:, cs] = acc.astype(BF16)

    return pl.pallas_call(
        body, grid=(G, nc), in_specs=_ssd_in_specs(Q, GP, N, Kh, DI, lambda c: c),
        out_specs=[pl.BlockSpec((Q, GP), lambda g, c: (c, g)),
                   pl.BlockSpec((None, None, GP, N), lambda g, c: (c, g, 0, 0))],
        out_shape=[jax.ShapeDtypeStruct((L, DI), BF16), jax.ShapeDtypeStruct((nc, G, GP, N), F32)],
        scratch_shapes=[pltpu.VMEM((GP, N), F32)], compiler_params=_cp(("parallel", "arbitrary")),
        name="ssd_fwd")(xbc, xbc, xbc, dtp_g, dtp_gT, bias_g, bias_gT, alog_g, alog_gT, dsk_g, dsk_gT)


def ssd_bwd(xbc, dtp_g, dtp_gT, bias_g, bias_gT, alog_g, alog_gT, dsk_g, dsk_gT, states, dy, DI):
    L = xbc.shape[0]
    Q, P, N, G = SSD_Q, SSD_P, SSD_N, SSD_G
    GP = DI // G
    Kh = GP // P
    nc = L // Q
    rev = lambda c: nc - 1 - c

    CW, hpc, nch = _head_chunks(GP)

    def body(xs_ref, b_ref, c_ref, dtp_ref, dtpT_ref, bias_ref, biasT_ref, alog_ref, alogT_ref, dsk_ref, dskT_ref,
             st_ref, dy_ref, dxs_ref, dB_ref, dC_ref, ddtp_ref, dbias_ref, dalog_ref, dD_ref, dstate):
        ci = pl.program_id(1)

        @pl.when(ci == 0)
        def _():
            dstate[...] = jnp.zeros(dstate.shape, F32)

        dt, A, causal, row, col, acum, acumT, Bm, Cm, cb = _ssd_common(
            dtp_ref, dtpT_ref, bias_ref, biasT_ref, alog_ref, alogT_ref, b_ref, c_ref)
        tn = (((0,), (0,)), ((), ()))
        nt = (((1,), (1,)), ((), ()))
        hd = _ssd_heads(dskT_ref, acum, acumT, dt, Kh)
        ET, tail = hd["ET"], hd["tail"]
        cbT = lax.dot_general(Bm, Cm, nt, preferred_element_type=F32)
        causalT = row <= col
        xs = xs_ref[...].astype(F32)
        xdt_all = xs * hd["dt_all"]
        dyb = dy_ref[...]
        dy_all = dyb.astype(F32)
        S_all = st_ref[...]
        S_b = S_all.astype(BF16)
        dS_all = dstate[...]
        dS_b = dS_all.astype(BF16)
        CS_all = lax.dot_general(Cm, S_b, nt, preferred_element_type=F32)
        dyE_b = (dy_all * hd["ea_all"]).astype(BF16)
        dC_acc = jnp.dot(dyE_b, S_b, preferred_element_type=F32)
        dS_y = lax.dot_general(dyE_b, Cm, tn, preferred_element_type=F32)
        BdS_all = lax.dot_general(Bm, dS_b, nt, preferred_element_type=F32)
        dB_acc = jnp.dot((xdt_all * hd["tail_all"]).astype(BF16), dS_b, preferred_element_type=F32)
        dtail = _hi(xdt_all * BdS_all, ET)
        da_cols = _hi(dy_all * CS_all * hd["ea_all"], ET) - dtail * tail
        dss = _dot01_left(jnp.ones((8, N), F32), _dot01_left(hd["E"], dS_all * S_all, 2), 2, dims=nt)
        da_last = dss[0:1] * jnp.exp(hd["a_last"]) + jnp.sum(dtail * tail, axis=0, keepdims=True)
        rowi = lax.broadcasted_iota(jnp.int32, (Q, Kh), 0)
        da_cols = da_cols + jnp.where(rowi == Q - 1, da_last, 0.0)
        dstate[...] = hd["eL_b"] * dS_all + dS_y
        sum_mg = jnp.zeros((Q, Q), F32)
        sum_mgt = jnp.zeros((Q, Q), F32)
        dacc = jnp.zeros((Q, 128), F32)
        ddt_x = jnp.zeros((Q, Kh), F32)
        lane128 = lax.broadcasted_iota(jnp.int32, (Q, 128), 1)
        for ch in range(nch):
            cs = slice(ch * CW, (ch + 1) * CW)
            dyc = dyb[:, cs]
            xc_b = xdt_all[:, cs].astype(BF16)
            acc = hd["tail_all"][:, cs] * BdS_all[:, cs]
            for kk in range(hpc):
                k = ch * hpc + kk
                a_b = jnp.broadcast_to(acum[:, k:k + 1], (Q, Q))
                a_r = acumT[k:k + 1, :]
                decay = jnp.exp(jnp.where(causal, a_b - a_r, -jnp.inf))
                decayT = jnp.exp(jnp.where(causalT, a_r - a_b, -jnp.inf))
                dyk = dyc if hpc == 1 else jnp.where(_head_mask(Q, CW, kk), dyc, jnp.zeros_like(dyc))
                mg = decay * lax.dot_general(dyk, xc_b, nt, preferred_element_type=F32)
                mgt = decayT * lax.dot_general(xc_b, dyk, nt, preferred_element_type=F32)
                sum_mg = sum_mg + mg
                sum_mgt = sum_mgt + mgt
                onek = jnp.where(lane128 == k, 1.0, 0.0).astype(BF16)
                dk = mg * cb - mgt * cbT
                dk_hi = dk.astype(BF16)
                dk_lo = (dk - dk_hi.astype(F32)).astype(BF16)
                dacc = dacc + (jnp.dot(dk_hi, onek, preferred_element_type=F32)
                               + jnp.dot(dk_lo, onek, preferred_element_type=F32))
                acc = acc + jnp.dot((decayT * cbT).astype(BF16), dyk, preferred_element_type=F32)
            dxs_ref[:, cs] = (acc * hd["dt_all"][:, cs] + dy_all[:, cs] * hd["dsk_all"][:, cs]).astype(BF16)
            ddt_x = ddt_x + _hi(acc * xs[:, cs], ET[cs, :])
        da_cols = da_cols + dacc[:, :Kh]
        dD_row = jnp.sum(_hi(dy_all * xs, ET), axis=0, keepdims=True)
        dB_ref[...] = (dB_acc + jnp.dot(sum_mgt.astype(BF16), Cm, preferred_element_type=F32)).astype(BF16)
        dC_ref[...] = (dC_acc + jnp.dot(sum_mg.astype(BF16), Bm, preferred_element_type=F32)).astype(BF16)
        triu = (row <= col).astype(F32)
        ddtA = _dot01_left(triu, da_cols, 3)
        ddt = ddt_x + ddtA * A
        dpre = ddt * _sigmoid(dtp_ref[...] + bias_ref[...])
        ddtp_ref[...] = dpre
        dbias_v = jnp.sum(dpre, axis=0, keepdims=True)
        dalog_v = jnp.sum(ddtA * dt, axis=0, keepdims=True) * A

        @pl.when(ci == 0)
        def _():
            dbias_ref[...] = dbias_v
            dalog_ref[...] = dalog_v
            dD_ref[...] = dD_row

        @pl.when(ci > 0)
        def _():
            dbias_ref[...] += dbias_v
            dalog_ref[...] += dalog_v
            dD_ref[...] += dD_row

    vec_o = pl.BlockSpec((None, 1, Kh), lambda g, c: (g, 0, 0))
    return pl.pallas_call(
        body, grid=(G, nc),
        in_specs=_ssd_in_specs(Q, GP, N, Kh, DI, rev)
        + [pl.BlockSpec((None, None, GP, N), lambda g, c: (rev(c), g, 0, 0)),
           pl.BlockSpec((Q, GP), lambda g, c: (rev(c), g))],
        out_specs=[pl.BlockSpec((Q, GP), lambda g, c: (rev(c), g)), pl.BlockSpec((Q, N), lambda g, c: (rev(c), g)),
                   pl.BlockSpec((Q, N), lambda g, c: (rev(c), g)),
                   pl.BlockSpec((None, Q, Kh), lambda g, c: (g, rev(c), 0)), vec_o, vec_o, vec_o],
        out_shape=[jax.ShapeDtypeStruct((L, DI), BF16), jax.ShapeDtypeStruct((L, G * N), BF16),
                   jax.ShapeDtypeStruct((L, G * N), BF16), jax.ShapeDtypeStruct((G, L, Kh), F32)]
        + [jax.ShapeDtypeStruct((G, 1, Kh), F32)] * 3,
        scratch_shapes=[pltpu.VMEM((GP, N), F32)], compiler_params=_cp(("parallel", "arbitrary")),
        name="ssd_bwd")(xbc, xbc, xbc, dtp_g, dtp_gT, bias_g, bias_gT, alog_g, alog_gT, dsk_g, dsk_gT, states, dy)


def _rms_groups(y2, ng_ref, DI):
    S = DI // SSD_G
    for g in range(SSD_G):
        gs = slice(g * S, (g + 1) * S)
        seg = y2[:, gs]
        r = lax.rsqrt(jnp.mean(seg * seg, axis=-1, keepdims=True) + RMS_EPS)
        yield gs, seg * r, r, ng_ref[:, gs]


def rms_gate_fwd(y, zx, norm_g):
    L, DI = y.shape
    tr = _tile(L, 256, 16)

    def body(y_ref, z_ref, ng_ref, o_ref):
        y2 = y_ref[...].astype(F32) * _silu(z_ref[...].astype(F32))
        for gs, yh, _, ng in _rms_groups(y2, ng_ref, DI):
            o_ref[:, gs] = (yh * ng).astype(BF16)

    return pl.pallas_call(
        body, grid=(L // tr,), in_specs=_row_specs(tr, [DI, DI]) + [_vec_spec(DI)], out_specs=_row_specs(tr, [DI])[0],
        out_shape=jax.ShapeDtypeStruct((L, DI), BF16), compiler_params=_cp(("parallel",)),
        name="rms_gate_fwd")(y, zx, norm_g)


def rms_gate_bwd(dyn, y, zx, norm_g):
    L, DI = y.shape
    tr = _tile(L, 256, 16)

    def body(dyn_ref, y_ref, z_ref, ng_ref, dy_ref, dz_ref, dng_ref):
        i = pl.program_id(0)
        yv = y_ref[...].astype(F32)
        zv = z_ref[...].astype(F32)
        sz = _silu(zv)
        dsz = _dsilu(zv)
        dynv = dyn_ref[...].astype(F32)
        for gs, yh, r, ng in _rms_groups(yv * sz, ng_ref, DI):
            dyh = dynv[:, gs] * ng
            dy2 = r * (dyh - yh * jnp.mean(dyh * yh, axis=-1, keepdims=True))
            dy_ref[:, gs] = (dy2 * sz[:, gs]).astype(BF16)
            dz_ref[:, gs] = (dy2 * yv[:, gs] * dsz[:, gs]).astype(BF16)
            s = jnp.sum(dynv[:, gs] * yh, axis=0, keepdims=True)

            @pl.when(i == 0)
            def _():
                dng_ref[:, gs] = s

            @pl.when(i > 0)
            def _():
                dng_ref[:, gs] += s

    return pl.pallas_call(
        body, grid=(L // tr,), in_specs=_row_specs(tr, [DI, DI, DI]) + [_vec_spec(DI)],
        out_specs=_row_specs(tr, [DI, DI]) + [_vec_spec(DI)],
        out_shape=[jax.ShapeDtypeStruct((L, DI), BF16)] * 2 + [jax.ShapeDtypeStruct((1, DI), F32)],
        compiler_params=_cp(("arbitrary",)), name="rms_gate_bwd")(dyn, y, zx, norm_g)


def _alibi_slope(gi, h):
    n = len(DIL_PATTERNS) * DIL_H
    return float(2.0 ** (-8.0 * (gi * DIL_H + h + 1) / n))


def _attn_masks():
    qi = lax.broadcasted_iota(jnp.int32, (DIL_BLK, DIL_BLK), 0)
    kj = lax.broadcasted_iota(jnp.int32, (DIL_BLK, DIL_BLK), 1)
    dcur = (qi - kj).astype(F32)
    return dcur, qi >= kj, dcur + float(DIL_BLK), kj >= qi


def _dil_cols(arr, col0, d):
    HW = DIL_H * DIL_E
    if d == 1:
        return arr, arr.shape[1] // HW, col0 // HW
    return arr[:, col0:col0 + HW].reshape(arr.shape[0] // d, d * HW), 1, 0


def attn_fwd(qz, kv, gi):
    window, d = DIL_PATTERNS[gi]
    assert window // d == DIL_BLK
    L, QZ = qz.shape
    KV = kv.shape[1]
    HW = DIL_H * DIL_E
    M = L // d
    nb = M // DIL_BLK
    nq, nkv = QZ // HW, KV // HW
    scale = DIL_E ** -0.5
    nt = (((1,), (1,)), ((), ()))

    def body(q_ref, kp_ref, kc_ref, vp_ref, vc_ref, o_ref, lse_ref):
        n = pl.program_id(1)
        dcur, vcur, dprev, vprev0 = _attn_masks()
        vprev = jnp.logical_and(vprev0, n > 0)
        lane = lax.broadcasted_iota(jnp.int32, (DIL_BLK, 128), 1)
        lse_acc = jnp.zeros((DIL_BLK, 128), F32)
        for h in range(DIL_H):
            hs = slice(h * DIL_E, (h + 1) * DIL_E)
            sl = _alibi_slope(gi, h) * d
            q = q_ref[:, hs]
            s_c = lax.dot_general(q, kc_ref[:, hs], nt, preferred_element_type=F32) * scale - sl * dcur
            s_p = lax.dot_general(q, kp_ref[:, hs], nt, preferred_element_type=F32) * scale - sl * dprev
            s_c = jnp.where(vcur, s_c, -jnp.inf)
            s_p = jnp.where(vprev, s_p, -jnp.inf)
            m = jnp.maximum(jnp.max(s_c, axis=-1, keepdims=True), jnp.max(s_p, axis=-1, keepdims=True))
            p_c = jnp.exp(s_c - m)
            p_p = jnp.exp(s_p - m)
            den = jnp.sum(p_c, axis=-1, keepdims=True) + jnp.sum(p_p, axis=-1, keepdims=True)
            o = (jnp.dot(p_c.astype(BF16), vc_ref[:, hs], preferred_element_type=F32)
                 + jnp.dot(p_p.astype(BF16), vp_ref[:, hs], preferred_element_type=F32)) / den
            o_ref[:, hs] = o.astype(BF16)
            lse_acc = jnp.where(lane == h, m + jnp.log(den), lse_acc)
        lse_ref[...] = lse_acc

    blk = (DIL_BLK, HW)
    prev = lambda n: jnp.maximum(n - 1, 0)
    qv, qn, qo = _dil_cols(qz, gi * HW, d)
    kv_, kn, ko = _dil_cols(kv, gi * HW, d)
    vv, vn, vo = _dil_cols(kv, (nkv // 2 + gi) * HW, d)
    o, lse = pl.pallas_call(
        body, grid=(d, nb),
        in_specs=[pl.BlockSpec(blk, lambda r, n: (n, r * qn + qo)),
                  pl.BlockSpec(blk, lambda r, n: (prev(n), r * kn + ko)),
                  pl.BlockSpec(blk, lambda r, n: (n, r * kn + ko)),
                  pl.BlockSpec(blk, lambda r, n: (prev(n), r * vn + vo)),
                  pl.BlockSpec(blk, lambda r, n: (n, r * vn + vo))],
        out_specs=[pl.BlockSpec(blk, lambda r, n: (n, r)), pl.BlockSpec((DIL_BLK, 128), lambda r, n: (n, r))],
        out_shape=[jax.ShapeDtypeStruct((M, d * HW), BF16), jax.ShapeDtypeStruct((M, d * 128), F32)],
        compiler_params=_cp(("parallel", "parallel")), name=f"attn_fwd_{gi}")(qv, kv_, kv_, vv, vv)
    return o.reshape(L, HW), lse.reshape(L, 128)


def attn_bwd(qz, kv, do, lse, dpr, gi):
    window, d = DIL_PATTERNS[gi]
    L, QZ = qz.shape
    KV = kv.shape[1]
    HW = DIL_H * DIL_E
    M = L // d
    nb = M // DIL_BLK
    nq, nkv = QZ // HW, KV // HW
    scale = DIL_E ** -0.5
    nt = (((1,), (1,)), ((), ()))
    tn = (((0,), (0,)), ((), ()))

    def body(q0_ref, q1_ref, k_ref, v_ref, do0_ref, do1_ref, l0_ref, l1_ref, r0_ref, r1_ref,
             dq_ref, dk_ref, dv_ref, carry):
        n = pl.program_id(1)

        @pl.when(n == 0)
        def _():
            carry[...] = jnp.zeros(carry.shape, F32)

        dcur, vcur, dprev, vprev0 = _attn_masks()
        vprev = jnp.logical_and(vprev0, n < nb - 1)
        for h in range(DIL_H):
            hs = slice(h * DIL_E, (h + 1) * DIL_E)
            sl = _alibi_slope(gi, h) * d
            kh = k_ref[:, hs]
            vh = v_ref[:, hs]
            q0, q1 = q0_ref[:, hs], q1_ref[:, hs]
            do0, do1 = do0_ref[:, hs], do1_ref[:, hs]
            s0 = lax.dot_general(q0, kh, nt, preferred_element_type=F32) * scale - sl * dcur
            p0 = jnp.exp(jnp.where(vcur, s0 - l0_ref[:, h:h + 1], -jnp.inf))
            ds0 = p0 * (lax.dot_general(do0, vh, nt, preferred_element_type=F32) - r0_ref[:, h:h + 1])
            s1 = lax.dot_general(q1, kh, nt, preferred_element_type=F32) * scale - sl * dprev
            p1 = jnp.exp(jnp.where(vprev, s1 - l1_ref[:, h:h + 1], -jnp.inf))
            ds1 = p1 * (lax.dot_general(do1, vh, nt, preferred_element_type=F32) - r1_ref[:, h:h + 1])
            ds0_b = (ds0 * scale).astype(BF16)
            ds1_b = (ds1 * scale).astype(BF16)
            dv = (lax.dot_general(p0.astype(BF16), do0, tn, preferred_element_type=F32)
                  + lax.dot_general(p1.astype(BF16), do1, tn, preferred_element_type=F32))
            dk = (lax.dot_general(ds0_b, q0, tn, preferred_element_type=F32)
                  + lax.dot_general(ds1_b, q1, tn, preferred_element_type=F32))
            dv_ref[:, hs] = dv.astype(BF16)
            dk_ref[:, hs] = dk.astype(BF16)
            dq_ref[:, hs] = (carry[:, hs] + jnp.dot(ds0_b, kh, preferred_element_type=F32)).astype(BF16)
            carry[:, hs] = jnp.dot(ds1_b, kh, preferred_element_type=F32)

    blk = (DIL_BLK, HW)
    sblk = (DIL_BLK, 128)
    nxt = lambda n: jnp.minimum(n + 1, nb - 1)
    qv, qn, qo = _dil_cols(qz, gi * HW, d)
    kv_, kn, ko = _dil_cols(kv, gi * HW, d)
    vv, vn, vo = _dil_cols(kv, (nkv // 2 + gi) * HW, d)
    dov = do.reshape(M, d * HW)
    lv = lse.reshape(M, d * 128)
    rv = dpr.reshape(M, d * 128)
    outs = pl.pallas_call(
        body, grid=(d, nb),
        in_specs=[pl.BlockSpec(blk, lambda r, n: (n, r * qn + qo)), pl.BlockSpec(blk, lambda r, n: (nxt(n), r * qn + qo)),
                  pl.BlockSpec(blk, lambda r, n: (n, r * kn + ko)),
                  pl.BlockSpec(blk, lambda r, n: (n, r * vn + vo)),
                  pl.BlockSpec(blk, lambda r, n: (n, r)), pl.BlockSpec(blk, lambda r, n: (nxt(n), r)),
                  pl.BlockSpec(sblk, lambda r, n: (n, r)), pl.BlockSpec(sblk, lambda r, n: (nxt(n), r)),
                  pl.BlockSpec(sblk, lambda r, n: (n, r)), pl.BlockSpec(sblk, lambda r, n: (nxt(n), r))],
        out_specs=[pl.BlockSpec(blk, lambda r, n: (n, r))] * 3,
        out_shape=[jax.ShapeDtypeStruct((M, d * HW), BF16)] * 3,
        scratch_shapes=[pltpu.VMEM(blk, F32)], compiler_params=_cp(("parallel", "arbitrary")),
        name=f"attn_bwd_{gi}")(qv, qv, kv_, vv, dov, dov, lv, lv, rv, rv)
    return [t.reshape(L, HW) for t in outs]


def _merge_weights(l_refs, h):
    ls = [r[:, h:h + 1] for r in l_refs]
    mx = functools.reduce(jnp.maximum, ls)
    es = [jnp.exp(l - mx) for l in ls]
    den = functools.reduce(lambda a, b: a + b, es)
    return [e / den for e in es]


def merge_fwd(os_, lses, qz):
    L, HW = os_[0].shape
    tr = _tile(L, 256, 16)
    ng = len(os_)
    zblk = qz.shape[1] // HW - 1

    def body(*refs):
        o_refs, l_refs, z_ref, out_ref = refs[:ng], refs[ng:2 * ng], refs[2 * ng], refs[2 * ng + 1]
        for h in range(DIL_H):
            hs = slice(h * DIL_E, (h + 1) * DIL_E)
            ws = _merge_weights(l_refs, h)
            om = functools.reduce(lambda a, b: a + b, [w * o[:, hs].astype(F32) for w, o in zip(ws, o_refs)])
            out_ref[:, hs] = (om * _silu(z_ref[:, hs].astype(F32))).astype(BF16)

    return pl.pallas_call(
        body, grid=(L // tr,),
        in_specs=_row_specs(tr, [HW] * ng + [128] * ng) + [pl.BlockSpec((tr, HW), lambda i: (i, zblk))],
        out_specs=_row_specs(tr, [HW])[0], out_shape=jax.ShapeDtypeStruct((L, HW), BF16),
        compiler_params=_cp(("parallel",)), name="merge_fwd")(*os_, *lses, qz)


def merge_bwd(dgated, os_, lses, qz):
    L, HW = os_[0].shape
    tr = _tile(L, 256, 16)
    ng = len(os_)
    zblk = qz.shape[1] // HW - 1

    def body(*refs):
        dg_ref = refs[0]
        o_refs, l_refs, z_ref = refs[1:1 + ng], refs[1 + ng:1 + 2 * ng], refs[1 + 2 * ng]
        outs = refs[2 + 2 * ng:]
        do_refs, dpr_refs, dz_ref = outs[:ng], outs[ng:2 * ng], outs[2 * ng]
        lane = lax.broadcasted_iota(jnp.int32, (tr, 128), 1)
        accs = [jnp.zeros((tr, 128), F32) for _ in range(ng)]
        for h in range(DIL_H):
            hs = slice(h * DIL_E, (h + 1) * DIL_E)
            ws = _merge_weights(l_refs, h)
            ov = [o[:, hs].astype(F32) for o in o_refs]
            om = functools.reduce(lambda a, b: a + b, [w * o for w, o in zip(ws, ov)])
            zv = z_ref[:, hs].astype(F32)
            dgv = dg_ref[:, hs].astype(F32)
            dom = dgv * _silu(zv)
            dz_ref[:, hs] = (dgv * om * _dsilu(zv)).astype(BF16)
            dws = [jnp.sum(dom * o, axis=-1, keepdims=True) for o in ov]
            dwbar = functools.reduce(lambda a, b: a + b, [w * dw for w, dw in zip(ws, dws)])
            for g in range(ng):
                do_refs[g][:, hs] = (ws[g] * dom).astype(BF16)
                accs[g] = jnp.where(lane == h, ws[g] * dwbar, accs[g])
        for g in range(ng):
            dpr_refs[g][...] = accs[g]

    outs = pl.pallas_call(
        body, grid=(L // tr,),
        in_specs=_row_specs(tr, [HW] * (1 + ng) + [128] * ng) + [pl.BlockSpec((tr, HW), lambda i: (i, zblk))],
        out_specs=_row_specs(tr, [HW] * ng + [128] * ng + [HW]),
        out_shape=[jax.ShapeDtypeStruct((L, HW), BF16)] * ng + [jax.ShapeDtypeStruct((L, 128), F32)] * ng
        + [jax.ShapeDtypeStruct((L, HW), BF16)],
        compiler_params=_cp(("parallel",)), name="merge_bwd")(dgated, *os_, *lses, qz)
    return outs[:ng], outs[ng:2 * ng], outs[2 * ng]


def ada_fwd(c8, ada_w):
    nl, D, Ws = ada_w.shape
    tn = _tile(Ws, 512)

    def body(c_ref, w_ref, o_ref):
        o_ref[...] = jnp.dot(_silu(c_ref[...]), w_ref[...], precision=lax.Precision.HIGHEST,
                             preferred_element_type=F32)

    return pl.pallas_call(
        body, grid=(nl, Ws // tn),
        in_specs=[pl.BlockSpec((N_DEV, D), lambda l, j: (0, 0)), pl.BlockSpec((None, D, tn), lambda l, j: (l, 0, j))],
        out_specs=pl.BlockSpec((None, N_DEV, tn), lambda l, j: (l, 0, j)),
        out_shape=jax.ShapeDtypeStruct((nl, N_DEV, Ws), F32), compiler_params=_cp(("parallel", "parallel")),
        name="ada_fwd")(c8, ada_w)


def ada_wgrad(c8t, dmod):
    nl, _, Ws = dmod.shape
    D = c8t.shape[0]
    tm = _tile(D, 512, 8)

    def body(c_ref, d_ref, o_ref):
        sc = _silu(c_ref[...])
        acc = sc[:, 0:1] * d_ref[0:1, :]
        for e in range(1, N_DEV):
            acc = acc + sc[:, e:e + 1] * d_ref[e:e + 1, :]
        o_ref[...] = acc

    return pl.pallas_call(
        body, grid=(nl, D // tm),
        in_specs=[pl.BlockSpec((tm, N_DEV), lambda l, i: (i, 0)), pl.BlockSpec((None, N_DEV, Ws), lambda l, i: (l, 0, 0))],
        out_specs=pl.BlockSpec((None, tm, Ws), lambda l, i: (l, i, 0)),
        out_shape=jax.ShapeDtypeStruct((nl, D, Ws), F32), compiler_params=_cp(("parallel", "parallel")),
        name="ada_wgrad")(c8t, dmod)


def adamw(w, g, m, v, name):
    R, C = w.shape
    tr = _tile(R, 256, 8)
    c1 = 1.0 - ADAM_B1 ** ADAM_STEP
    c2 = 1.0 - ADAM_B2 ** ADAM_STEP

    def body(w_ref, g_ref, m_ref, v_ref, d_ref, nm_ref, nv_ref):
        gv = g_ref[...]
        nm = ADAM_B1 * m_ref[...] + (1.0 - ADAM_B1) * gv
        nv = ADAM_B2 * v_ref[...] + (1.0 - ADAM_B2) * (gv * gv)
        nm_ref[...] = nm
        nv_ref[...] = nv
        d_ref[...] = -ADAM_LR * ((nm / c1) / (jnp.sqrt(nv / c2) + ADAM_EPS) + ADAM_WD * w_ref[...])

    return pl.pallas_call(
        body, grid=(R // tr,), in_specs=_row_specs(tr, [C] * 4), out_specs=_row_specs(tr, [C] * 3),
        out_shape=[jax.ShapeDtypeStruct((R, C), F32)] * 3, compiler_params=_cp(("parallel",)), name=name)(w, g, m, v)


def sum_leading(t, name, out_dtype=F32):
    S, R, C = t.shape
    tr = _tile(R, 256, 16)

    def body(t_ref, o_ref):
        acc = t_ref[0].astype(F32)
        for s in range(1, S):
            acc = acc + t_ref[s].astype(F32)
        o_ref[...] = acc.astype(out_dtype)

    return pl.pallas_call(
        body, grid=(R // tr,), in_specs=[pl.BlockSpec((S, tr, C), lambda i: (0, i, 0))],
        out_specs=pl.BlockSpec((tr, C), lambda i: (i, 0)), out_shape=jax.ShapeDtypeStruct((R, C), out_dtype),
        compiler_params=_cp(("parallel",)), name=name)(t)


def add_half(g, a, core, name):
    S, R, C = g.shape
    h = R // 2
    tr = _tile(h, 256, 16)
    nb = h // tr

    def body(core_ref, g_ref, a_ref, o_ref):
        o_ref[...] = (g_ref[...].astype(F32) + a_ref[...].astype(F32)).astype(BF16)

    return pl.pallas_call(
        body,
        grid_spec=pltpu.PrefetchScalarGridSpec(
            num_scalar_prefetch=1, grid=(S, nb),
            in_specs=[pl.BlockSpec((None, tr, C), lambda s, i, core_ref: (s, core_ref[0] * nb + i, 0)),
                      pl.BlockSpec((None, tr, C), lambda s, i, core_ref: (s, i, 0))],
            out_specs=pl.BlockSpec((None, tr, C), lambda s, i, core_ref: (s, i, 0))),
        out_shape=jax.ShapeDtypeStruct((S, h, C), BF16), compiler_params=_cp(("parallel", "parallel")),
        name=name)(core, g, a)


_ANY = pl.BlockSpec(memory_space=pl.ANY)


def _place():
    x, y, c = lax.axis_index("x"), lax.axis_index("y"), lax.axis_index("c")
    chips = [(1 - x, y), (x, 1 - y), (1 - x, 1 - y)]
    return x, y, c, chips


def allgather_small(v, name):
    R, W = v.shape

    def body(x_ref, out_ref, send_sems, recv_sems, local_sem):
        x, y, c, chips = _place()
        me, sibling = (x, y, c), (x, y, 1 - c)

        def rows(px, py, pc):
            return out_ref.at[pl.ds((4 * px + 2 * py + pc) * R, R), :]

        def copy(k, block, to, src=None):
            return pltpu.make_async_remote_copy(
                src_ref=rows(*block) if src is None else src, dst_ref=rows(*block),
                send_sem=send_sems.at[k], recv_sem=recv_sems.at[k], device_id=to, device_id_type=MESH)

        mine = pltpu.make_async_copy(x_ref, rows(*me), local_sem)
        mine.start()
        first = [copy(0, me, sibling, src=x_ref)]
        first += [copy(1 + j, me, (*chip, c), src=x_ref) for j, chip in enumerate(chips)]
        for cp in first:
            cp.start()
        passed = [copy(4 + j, (*chip, c), sibling) for j, chip in enumerate(chips)]
        for j, chip in enumerate(chips):
            copy(1 + j, (*chip, c), me).wait_recv()
            passed[j].start()
        copy(0, sibling, me).wait_recv()
        for j, chip in enumerate(chips):
            copy(4 + j, (*chip, 1 - c), me).wait_recv()
        for cp in first + passed:
            cp.wait_send()
        mine.wait()

    return pl.pallas_call(
        body, out_shape=jax.ShapeDtypeStruct((N_DEV * R, W), v.dtype),
        in_specs=[pl.BlockSpec(memory_space=pltpu.VMEM)], out_specs=pl.BlockSpec(memory_space=pltpu.VMEM),
        scratch_shapes=[pltpu.SemaphoreType.DMA((7,)), pltpu.SemaphoreType.DMA((7,)), pltpu.SemaphoreType.DMA],
        name=name)(v)


def allgather_weights(shards):
    n = len(shards)

    def body(*refs):
        ins, outs = refs[:n], refs[n:2 * n]
        send_sems, recv_sems, local_sems = refs[2 * n:]
        x, y, c, chips = _place()
        p = 2 * x + y
        sibling = (x, y, 1 - c)
        locals_ = [pltpu.make_async_copy(ins[i], outs[i].at[p], local_sems.at[i]) for i in range(n)]
        for cp in locals_:
            cp.start()

        def half(i, chip_id, core, ref=None):
            r = outs[i].at[chip_id] if ref is None else ref
            return r.at[core]

        def copy(i, k, chip_id, core, to, src=None):
            return pltpu.make_async_remote_copy(
                src_ref=half(i, chip_id, core) if src is None else src, dst_ref=half(i, chip_id, core),
                send_sem=send_sems.at[6 * i + k], recv_sem=recv_sems.at[6 * i + k], device_id=to, device_id_type=MESH)

        first = [copy(i, j, p, c, (*chip, c), src=half(i, p, c, ref=ins[i]))
                 for i in range(n) for j, chip in enumerate(chips)]
        for cp in first:
            cp.start()
        passed = []
        for i in range(n):
            for j, (cx, cy) in enumerate(chips):
                copy(i, j, 2 * cx + cy, c, sibling).wait_recv()
                fw = copy(i, 3 + j, 2 * cx + cy, c, sibling)
                fw.start()
                passed.append(fw)
        for i in range(n):
            for j, (cx, cy) in enumerate(chips):
                copy(i, 3 + j, 2 * cx + cy, 1 - c, sibling).wait_recv()
        for cp in first + passed:
            cp.wait_send()
        for cp in locals_:
            cp.wait()

    split = [s.reshape(2, s.shape[0] // 2, s.shape[1]) for s in shards]
    outs = pl.pallas_call(
        body, out_shape=[jax.ShapeDtypeStruct((N_CHIPS,) + s.shape, s.dtype) for s in split],
        in_specs=[_ANY] * n, out_specs=[_ANY] * n,
        scratch_shapes=[pltpu.SemaphoreType.DMA((6 * n,)), pltpu.SemaphoreType.DMA((6 * n,)),
                        pltpu.SemaphoreType.DMA((n,))],
        name="allgather_weights")(*split)
    return [o.reshape((N_CHIPS,) + s.shape) for o, s in zip(outs, shards)]


def exchange_halves_to_sibling(gs):
    n = len(gs)

    def body(*refs):
        ins, outs = refs[:n], refs[n:2 * n]
        send_sems, recv_sems = refs[2 * n:]
        x, y, c, _ = _place()
        cps = []
        for i in range(n):
            h = ins[i].shape[1] // 2
            cps.append(pltpu.make_async_remote_copy(
                src_ref=ins[i].at[:, pl.ds((1 - c) * h, h), :], dst_ref=outs[i],
                send_sem=send_sems.at[i], recv_sem=recv_sems.at[i], device_id=(x, y, 1 - c), device_id_type=MESH))
        for cp in cps:
            cp.start()
        for cp in cps:
            cp.wait()

    return pl.pallas_call(
        body, out_shape=[jax.ShapeDtypeStruct((g.shape[0], g.shape[1] // 2, g.shape[2]), g.dtype) for g in gs],
        in_specs=[_ANY] * n, out_specs=[_ANY] * n,
        scratch_shapes=[pltpu.SemaphoreType.DMA((n,)), pltpu.SemaphoreType.DMA((n,))],
        name="rs_sibling_exchange")(*gs)


def scatter_to_chips(ps):
    n = len(ps)

    def body(*refs):
        ins, outs = refs[:n], refs[n:2 * n]
        send_sems, recv_sems, local_sems = refs[2 * n:]
        x, y, c, chips = _place()
        p = 2 * x + y
        locals_ = [pltpu.make_async_copy(ins[i].at[p], outs[i].at[p], local_sems.at[i]) for i in range(n)]
        for cp in locals_:
            cp.start()
        sends, recvs = [], []
        for i in range(n):
            for j, (cx, cy) in enumerate(chips):
                q = 2 * cx + cy
                sends.append(pltpu.make_async_remote_copy(
                    src_ref=ins[i].at[q], dst_ref=outs[i].at[p], send_sem=send_sems.at[3 * i + j],
                    recv_sem=recv_sems.at[3 * i + j], device_id=(cx, cy, c), device_id_type=MESH))
                recvs.append(pltpu.make_async_remote_copy(
                    src_ref=ins[i].at[q], dst_ref=outs[i].at[q], send_sem=send_sems.at[3 * i + j],
                    recv_sem=recv_sems.at[3 * i + j], device_id=(cx, cy, c), device_id_type=MESH))
        for cp in sends:
            cp.start()
        for cp in recvs:
            cp.wait_recv()
        for cp in sends:
            cp.wait_send()
        for cp in locals_:
            cp.wait()

    return pl.pallas_call(
        body, out_shape=[jax.ShapeDtypeStruct(t.shape, t.dtype) for t in ps], in_specs=[_ANY] * n, out_specs=[_ANY] * n,
        scratch_shapes=[pltpu.SemaphoreType.DMA((3 * n,)), pltpu.SemaphoreType.DMA((3 * n,)),
                        pltpu.SemaphoreType.DMA((n,))],
        name="rs_chip_scatter")(*ps)


def join_halves(rs):
    n = len(rs)

    def body(*refs):
        ins, outs = refs[:n], refs[n:2 * n]
        send_sems, recv_sems, local_sems = refs[2 * n:]
        x, y, c, _ = _place()
        locals_, sends, recvs = [], [], []
        for i in range(n):
            mine = outs[i].at[c]
            theirs = outs[i].at[1 - c]
            locals_.append(pltpu.make_async_copy(ins[i], mine, local_sems.at[i]))
            sends.append(pltpu.make_async_remote_copy(
                src_ref=ins[i], dst_ref=mine, send_sem=send_sems.at[i], recv_sem=recv_sems.at[i],
                device_id=(x, y, 1 - c), device_id_type=MESH))
            recvs.append(pltpu.make_async_remote_copy(
                src_ref=ins[i], dst_ref=theirs, send_sem=send_sems.at[i], recv_sem=recv_sems.at[i],
                device_id=(x, y, 1 - c), device_id_type=MESH))
        for cp in locals_ + sends:
            cp.start()
        for cp in recvs:
            cp.wait_recv()
        for cp in sends:
            cp.wait_send()
        for cp in locals_:
            cp.wait()

    outs = pl.pallas_call(
        body, out_shape=[jax.ShapeDtypeStruct((2,) + r.shape, r.dtype) for r in rs],
        in_specs=[_ANY] * n, out_specs=[_ANY] * n,
        scratch_shapes=[pltpu.SemaphoreType.DMA((n,)), pltpu.SemaphoreType.DMA((n,)), pltpu.SemaphoreType.DMA((n,))],
        name="rs_join_halves")(*rs)
    return [o.reshape(2 * r.shape[0], r.shape[1]) for o, r in zip(outs, rs)]


def _pack(parts, row_mult=8):
    flat = jnp.concatenate([p.reshape(-1).astype(F32) for p in parts])
    unit = row_mult * 128
    n = -(-flat.shape[0] // unit) * unit
    return jnp.pad(flat, (0, n - flat.shape[0])).reshape(n // 128, 128)


def _unpack(flat, shapes):
    out, off = [], 0
    for s in shapes:
        n = int(np.prod(s))
        out.append(flat[off:off + n].reshape(s))
        off += n
    return out


def _gather_packed(parts, name):
    packed = _pack(parts)
    g = allgather_small(packed, name).reshape(N_DEV, -1)
    return _unpack_rows(g, [p.shape for p in parts])


def _unpack_rows(g, shapes):
    out, off = [], 0
    for s in shapes:
        n = int(np.prod(s))
        out.append(g[:, off:off + n].reshape((g.shape[0],) + tuple(s)))
        off += n
    return out


def _by_chip(t, axis):
    return jnp.concatenate([t[2 * p] for p in range(N_CHIPS)], axis=axis)


def kernel(x, c, ada_w, ada_b, ln_g, ln_b, a_in_w, a_conv_w, a_conv_b, a_dt_bias, a_A_log, a_D, a_norm_g, a_out_w, kv_w, b_in_w, b_out_w, loss_target, m_ada_w, m_ada_b, m_ln_g, m_ln_b, m_a_in_w, m_a_conv_w, m_a_conv_b, m_a_dt_bias, m_a_A_log, m_a_D, m_a_norm_g, m_a_out_w, m_kv_w, m_b_in_w, m_b_out_w, v_ada_w, v_ada_b, v_ln_g, v_ln_b, v_a_in_w, v_a_conv_w, v_a_conv_b, v_a_dt_bias, v_a_A_log, v_a_D, v_a_norm_g, v_a_out_w, v_kv_w, v_b_in_w, v_b_out_w):
    ax, ay, ac = lax.axis_index("x"), lax.axis_index("y"), lax.axis_index("c")
    chip = 2 * ax + ay
    dev = 4 * ax + 2 * ay + ac
    xin = x[0]
    tgt = loss_target[0]
    L, D = xin.shape
    G, P = SSD_G, SSD_P
    H = a_dt_bias.shape[1]
    Kh = H // G
    DI = H * P
    CONVD = a_conv_b.shape[1] * N_CHIPS
    HW = DIL_H * DIL_E
    Ws = ada_w.shape[2]

    w_in_g, w_out_g, w_kv_g, w_bin_g, w_bout_g = allgather_weights(
        [a_in_w[0].astype(BF16), a_out_w[0].astype(BF16), kv_w.astype(BF16), b_in_w[0].astype(BF16),
         b_out_w[0].astype(BF16)])
    w_in = jnp.transpose(w_in_g, (1, 0, 2)).reshape(D, -1)
    w_zx = w_in[:, :DI + CONVD]
    w_dt = jnp.pad(w_in[:, DI + CONVD:], ((0, 0), (0, 128 - H)))

    c8, cw8, cb8, ng8 = _gather_packed([c[0], a_conv_w[0], a_conv_b[0], a_norm_g[0]], "allgather_small_params")
    conv_w = _by_chip(cw8, 1)
    conv_b = _by_chip(cb8, 0).reshape(1, CONVD)
    norm_g = _by_chip(ng8, 0).reshape(1, DI)

    mod_s = ada_fwd(c8, ada_w)
    (mod8,) = _gather_packed([mod_s], "allgather_small_mod")
    mods = _by_chip(mod8, 2)
    mod = lax.dynamic_index_in_dim(mods, dev, axis=1, keepdims=False) + ada_b
    shift = [mod[l:l + 1, :D] for l in range(DEPTH)]
    scale = [mod[l:l + 1, D:2 * D] for l in range(DEPTH)]
    gate = [mod[l:l + 1, 2 * D:] for l in range(DEPTH)]
    lg = [ln_g[l:l + 1] for l in range(DEPTH)]
    lb = [ln_b[l:l + 1] for l in range(DEPTH)]

    h0 = modulate(xin, scale[0], shift[0], "modulate0")
    zx = mm_nn(h0, w_zx, BF16, "mm_in_zx")
    dtp = mm_nn(h0, w_dt, F32, "mm_in_dt")
    xbc = conv_fwd(zx, DI, conv_w, conv_b)
    dtp_g = jnp.transpose(dtp[:, :H].reshape(L, G, Kh), (1, 0, 2))
    dtp_gT = jnp.transpose(dtp_g, (0, 2, 1))
    vecs = [a_dt_bias.reshape(G, 1, Kh), a_dt_bias.reshape(G, Kh, 1), a_A_log.reshape(G, 1, Kh),
            a_A_log.reshape(G, Kh, 1), a_D.reshape(G, 1, Kh), a_D.reshape(G, Kh, 1)]
    y_ssd, states = ssd_fwd(xbc, dtp_g, dtp_gT, *vecs, DI)
    yn = rms_gate_fwd(y_ssd, zx, norm_g)
    ymix0 = mm_nn(yn, w_out_g, F32, "mm_out_a", stack="row")
    x1, x1b, h1 = ln_mid(xin, ymix0, gate[0], lg[0], lb[0], scale[1], shift[1])

    kvp = mm_nn(x1b, w_kv_g, BF16, "mm_kv", stack="col")
    qz = mm_nn(h1, w_bin_g, BF16, "mm_in_b", stack="col")
    os_, lses = [], []
    for gi in range(len(DIL_PATTERNS)):
        o, lse = attn_fwd(qz, kvp, gi)
        os_.append(o)
        lses.append(lse)
    om = merge_fwd(os_, lses, qz)
    ymix1 = mm_nn(om, w_bout_g, F32, "mm_out_b", stack="col")
    dx2, sq = ln_final(x1, ymix1, gate[1], lg[1], lb[1], tgt)
    loss_part = 0.5 * jnp.sum(sq) / D

    dres2, dy2, dg1, db1, dgate1 = ln_bwd(dx2, x1, ymix1, gate[1], lg[1], "ln_bwd1")
    g_bout = mm_tn(om, dy2, BF16, "mm_gw_out_b", stack="col")
    dgated = mm_nt(dy2, w_bout_g, BF16, "mm_gx_out_b", stack="col")
    dos, dprs, dz_b = merge_bwd(dgated, os_, lses, qz)
    dqs, dks, dvs = [], [], []
    for gi in range(len(DIL_PATTERNS)):
        dq, dk, dv = attn_bwd(qz, kvp, dos[gi], lses[gi], dprs[gi], gi)
        dqs.append(dq)
        dks.append(dk)
        dvs.append(dv)
    dqz = jnp.concatenate(dqs + [dz_b], axis=1)
    dkv = jnp.concatenate(dks + dvs, axis=1)
    g_bin = mm_tn(h1, dqz, BF16, "mm_gw_in_b", stack="col")
    dh1 = mm_nt(dqz, w_bin_g, F32, "mm_gx_in_b", stack="col")
    g_kv = mm_tn(x1b, dkv, BF16, "mm_gw_kv", stack="col")
    dx1_kv = mm_nt(dkv, w_kv_g, F32, "mm_gx_kv", stack="col")
    dx1, dscale1, dshift1 = mod_bwd(dres2, dh1, dx1_kv, x1, scale[1], "mod_bwd1", through_mod=False)

    dres1, dy1, dg0, db0, dgate0 = ln_bwd(dx1, xin, ymix0, gate[0], lg[0], "ln_bwd0")
    g_out = mm_tn(yn, dy1, BF16, "mm_gw_out_a", stack="row")
    dyn = mm_nt(dy1, w_out_g, BF16, "mm_gx_out_a", stack="row")
    dy_ssd, dz_a, dnorm_g = rms_gate_bwd(dyn, y_ssd, zx, norm_g)
    dxs, dB, dC, ddtp_g, dbias_g, dalog_g, dD_g = ssd_bwd(xbc, dtp_g, dtp_gT, *vecs, states, dy_ssd, DI)
    dxbc = jnp.concatenate([dxs, dB, dC], axis=1)
    dxbc_pre, dconv_w, dconv_b = conv_bwd(zx, DI, conv_w, conv_b, dxbc)
    dzx = jnp.concatenate([dz_a, dxbc_pre], axis=1)
    ddtp = jnp.pad(jnp.transpose(ddtp_g, (1, 0, 2)).reshape(L, H), ((0, 0), (0, 128 - H)))
    g_zx = mm_tn(h0, dzx, BF16, "mm_gw_in_zx")
    g_dt = mm_tn(h0, ddtp, BF16, "mm_gw_in_dt")
    dh0 = mm_nt(dzx, w_zx, F32, "mm_gx_in_zx")
    dh0_dt = mm_nt(ddtp, w_dt, F32, "mm_gx_in_dt")
    grad_x, dscale0, dshift0 = mod_bwd(dres1, dh0, dh0_dt, xin, scale[0], "mod_bwd0", through_mod=True)
    g_in = jnp.concatenate([g_zx, g_dt[:, :H]], axis=1)
    g_in = jnp.transpose(g_in.reshape(D, N_CHIPS, -1), (1, 0, 2))

    gs = [g_in, g_out, g_kv, g_bin, g_bout]
    names = ["in_a", "out_a", "kv", "in_b", "out_b"]
    core = ac.astype(jnp.int32).reshape(1)
    sib = exchange_halves_to_sibling(gs)
    parts = [add_half(g, a, core, "rs_add_" + nm) for g, a, nm in zip(gs, sib, names)]
    landed = scatter_to_chips(parts)
    halves = [sum_leading(t, "rs_sum_" + nm) for t, nm in zip(landed, names)]
    g_in_s, g_out_s, g_kv_s, g_bin_s, g_bout_s = join_halves(halves)

    dmod = jnp.concatenate([jnp.concatenate([dshift0, dscale0, dgate0], axis=1),
                            jnp.concatenate([dshift1, dscale1, dgate1], axis=1)], axis=0)
    small_parts = [jnp.concatenate([dg0, dg1], axis=0), jnp.concatenate([db0, db1], axis=0),
                   dbias_g.reshape(1, H), dalog_g.reshape(1, H), dD_g.reshape(1, H),
                   dconv_w, dconv_b, dnorm_g, loss_part.reshape(1, 1)]
    small_shapes = [p.shape for p in small_parts]
    packed = jnp.concatenate([_pack([dmod]), _pack(small_parts)], axis=0)
    n_mod_rows = _pack([dmod]).shape[0]
    gathered = allgather_small(packed, "allgather_small_grads").reshape(N_DEV, -1, 128)
    dmod8 = gathered[:, :n_mod_rows].reshape(N_DEV, -1)[:, :2 * 3 * D].reshape(N_DEV, DEPTH, 3 * D)
    summed = sum_leading(gathered, "sum_small")
    g_ada_b = summed[:n_mod_rows].reshape(-1)[:2 * 3 * D].reshape(DEPTH, 3 * D)
    (g_ln_g, g_ln_b, g_dt_bias, g_a_log, g_dsk, g_conv_w, g_conv_b, g_norm_g, loss_all) = _unpack(
        summed[n_mod_rows:].reshape(-1), small_shapes)
    loss = loss_all.reshape(())
    Cs = CONVD // N_CHIPS
    g_conv_w_s = lax.dynamic_slice_in_dim(g_conv_w, chip * Cs, Cs, axis=1)
    g_conv_b_s = lax.dynamic_slice_in_dim(g_conv_b, chip * Cs, Cs, axis=1)
    g_norm_g_s = lax.dynamic_slice_in_dim(g_norm_g, chip * (DI // N_CHIPS), DI // N_CHIPS, axis=1)
    dmod_s = jnp.transpose(lax.dynamic_slice_in_dim(dmod8, chip * Ws, Ws, axis=2), (1, 0, 2))
    g_ada_w = ada_wgrad(jnp.transpose(c8), dmod_s)

    def step2d(w, g, m, v, nm):
        shp = w.shape
        d_, m_, v_ = adamw(w.reshape(-1, shp[-1]), g.reshape(-1, shp[-1]), m.reshape(-1, shp[-1]),
                           v.reshape(-1, shp[-1]), "adamw_" + nm)
        return g.reshape(shp), d_.reshape(shp), m_.reshape(shp), v_.reshape(shp)

    big = {
        "ada_w": step2d(ada_w, g_ada_w, m_ada_w, v_ada_w, "ada_w"),
        "a_in_w": step2d(a_in_w, g_in_s, m_a_in_w, v_a_in_w, "in_a"),
        "a_out_w": step2d(a_out_w, g_out_s, m_a_out_w, v_a_out_w, "out_a"),
        "kv_w": step2d(kv_w, g_kv_s, m_kv_w, v_kv_w, "kv"),
        "b_in_w": step2d(b_in_w, g_bin_s, m_b_in_w, v_b_in_w, "in_b"),
        "b_out_w": step2d(b_out_w, g_bout_s, m_b_out_w, v_b_out_w, "out_b"),
    }
    small_names = ["ada_b", "ln_g", "ln_b", "a_conv_w", "a_conv_b", "a_dt_bias", "a_A_log", "a_D", "a_norm_g"]
    small_w = [ada_b, ln_g, ln_b, a_conv_w, a_conv_b, a_dt_bias, a_A_log, a_D, a_norm_g]
    small_m = [m_ada_b, m_ln_g, m_ln_b, m_a_conv_w, m_a_conv_b, m_a_dt_bias, m_a_A_log, m_a_D, m_a_norm_g]
    small_v = [v_ada_b, v_ln_g, v_ln_b, v_a_conv_w, v_a_conv_b, v_a_dt_bias, v_a_A_log, v_a_D, v_a_norm_g]
    small_g = [g_ada_b, g_ln_g, g_ln_b, g_conv_w_s, g_conv_b_s, g_dt_bias, g_a_log, g_dsk, g_norm_g_s]
    shapes = [w.shape for w in small_w]
    small_g = [g.reshape(s) for g, s in zip(small_g, shapes)]
    d_p, m_p, v_p = adamw(_pack(small_w), _pack(small_g), _pack(small_m), _pack(small_v), "adamw_small")
    small = {}
    for nm, g, d_, m_, v_ in zip(small_names, small_g, _unpack(d_p.reshape(-1), shapes), _unpack(m_p.reshape(-1), shapes),
                                 _unpack(v_p.reshape(-1), shapes)):
        small[nm] = (g, d_, m_, v_)
    allw = {**big, **small}
    order = ["ada_w", "ada_b", "ln_g", "ln_b", "a_in_w", "a_conv_w", "a_conv_b", "a_dt_bias", "a_A_log", "a_D",
             "a_norm_g", "a_out_w", "kv_w", "b_in_w", "b_out_w"]
    outs = [loss, grad_x.reshape(x.shape)]
    for k in range(4):
        outs += [allw[n][k] for n in order]
    return tuple(outs)
```

```python
import functools

import jax
import jax.numpy as jnp
import numpy as np
from jax import lax
from jax.experimental import pallas as pl
from jax.experimental.pallas import tpu as pltpu

F32 = jnp.float32
BF16 = jnp.bfloat16
MESH = pl.DeviceIdType.MESH

DEPTH = 2
ALPHA = (2 * DEPTH) ** 0.25
LN_EPS = 1e-5
RMS_EPS = 1e-5
SSD_P = 64
SSD_N = 128
SSD_Q = 256
SSD_G = 8
CONV_W = 4
DIL_PATTERNS = ((128, 1), (512, 4), (2048, 16))
DIL_H = 8
DIL_E = 128
DIL_BLK = 128
ADAM_LR, ADAM_B1, ADAM_B2, ADAM_EPS, ADAM_WD, ADAM_STEP = 0.001, 0.9, 0.999, 1e-08, 0.01, 10

VMEM_LIMIT = 56 * 1024 * 1024
N_CHIPS = 4
N_DEV = 8


def _tile(dim, target, mult=128):
    if dim <= target:
        return dim
    t = (target // mult) * mult
    while t >= mult:
        if dim % t == 0:
            return t
        t -= mult
    return dim


def _cp(sem):
    return pltpu.CompilerParams(dimension_semantics=sem, vmem_limit_bytes=VMEM_LIMIT)


def _sigmoid(x):
    return 1.0 / (1.0 + jnp.exp(-x))


def _silu(x):
    return x * _sigmoid(x)


def _dsilu(x):
    s = _sigmoid(x)
    return s * (1.0 + x * (1.0 - s))


def _softplus(x):
    return jnp.maximum(x, 0.0) + jnp.log(1.0 + jnp.exp(-jnp.abs(x)))


def _mm_call(a, b, out_shape, grid, a_spec, b_spec, o_spec, acc_shape, dims, name):
    nk = grid[2]

    def body(a_ref, b_ref, o_ref, acc_ref):
        k = pl.program_id(2)

        @pl.when(k == 0)
        def _():
            acc_ref[...] = jnp.zeros(acc_ref.shape, F32)

        acc_ref[...] += lax.dot_general(a_ref[...].astype(BF16), b_ref[...].astype(BF16), (dims, ((), ())),
                                        preferred_element_type=F32)

        @pl.when(k == nk - 1)
        def _():
            o_ref[...] = acc_ref[...].astype(o_ref.dtype)

    return pl.pallas_call(
        body, grid=grid, in_specs=[a_spec, b_spec], out_specs=o_spec, out_shape=out_shape,
        scratch_shapes=[pltpu.VMEM(acc_shape, F32)],
        compiler_params=_cp(("parallel", "parallel", "arbitrary")), name=name)(a, b)


def mm_nn(a, b, out_dtype, name, stack=None, tm=1024, tn=1024, tk=512):
    M, K = a.shape
    if stack is None:
        N = b.shape[1]
        tn, tk = _tile(N, tn), _tile(K, tk)
        b_spec = pl.BlockSpec((tk, tn), lambda i, j, k: (k, j))
    elif stack == "col":
        S, _, Ns = b.shape
        N = S * Ns
        tn, tk = _tile(Ns, tn), _tile(K, tk)
        npb = Ns // tn
        b_spec = pl.BlockSpec((None, tk, tn), lambda i, j, k: (j // npb, k, j % npb))
    else:
        S, Ks, N = b.shape
        tn, tk = _tile(N, tn), _tile(Ks, tk)
        kpb = Ks // tk
        b_spec = pl.BlockSpec((None, tk, tn), lambda i, j, k: (k // kpb, k % kpb, j))
    tm = _tile(M, tm)
    return _mm_call(a, b, jax.ShapeDtypeStruct((M, N), out_dtype), (M // tm, N // tn, K // tk),
                    pl.BlockSpec((tm, tk), lambda i, j, k: (i, k)), b_spec,
                    pl.BlockSpec((tm, tn), lambda i, j, k: (i, j)), (tm, tn), ((1,), (0,)), name)


def mm_nt(a, b, out_dtype, name, stack=None, tm=1024, tn=1024, tk=512):
    M, C = a.shape
    if stack is None:
        Kw = b.shape[0]
        tn, tk = _tile(Kw, tn), _tile(C, tk)
        b_spec = pl.BlockSpec((tn, tk), lambda i, j, k: (j, k))
    elif stack == "col":
        S, Kw, Cs = b.shape
        tn, tk = _tile(Kw, tn), _tile(Cs, tk)
        cpb = Cs // tk
        b_spec = pl.BlockSpec((None, tn, tk), lambda i, j, k: (k // cpb, j, k % cpb))
    else:
        S, Ks, _ = b.shape
        Kw = S * Ks
        tn, tk = _tile(Ks, tn), _tile(C, tk)
        jpb = Ks // tn
        b_spec = pl.BlockSpec((None, tn, tk), lambda i, j, k: (j // jpb, j % jpb, k))
    tm = _tile(M, tm)
    return _mm_call(a, b, jax.ShapeDtypeStruct((M, Kw), out_dtype), (M // tm, Kw // tn, C // tk),
                    pl.BlockSpec((tm, tk), lambda i, j, k: (i, k)), b_spec,
                    pl.BlockSpec((tm, tn), lambda i, j, k: (i, j)), (tm, tn), ((1,), (1,)), name)


def mm_tn(a, b, out_dtype, name, stack=None, n_stack=N_CHIPS, tm=1024, tn=1024, tk=512):
    L, M = a.shape
    N = b.shape[1]
    tk = _tile(L, tk)
    if stack is None:
        tm, tn = _tile(M, tm), _tile(N, tn)
        o_spec = pl.BlockSpec((tm, tn), lambda i, j, k: (i, j))
        out_shape = (M, N)
    elif stack == "col":
        Ns = N // n_stack
        tm, tn = _tile(M, tm), _tile(Ns, tn)
        npb = Ns // tn
        o_spec = pl.BlockSpec((None, tm, tn), lambda i, j, k: (j // npb, i, j % npb))
        out_shape = (n_stack, M, Ns)
    else:
        Ms = M // n_stack
        tm, tn = _tile(Ms, tm), _tile(N, tn)
        mpb = Ms // tm
        o_spec = pl.BlockSpec((None, tm, tn), lambda i, j, k: (i // mpb, i % mpb, j))
        out_shape = (n_stack, Ms, N)
    return _mm_call(a, b, jax.ShapeDtypeStruct(out_shape, out_dtype), (M // tm, N // tn, L // tk),
                    pl.BlockSpec((tk, tm), lambda i, j, k: (k, i)), pl.BlockSpec((tk, tn), lambda i, j, k: (k, j)),
                    o_spec, (tm, tn), ((0,), (0,)), name)


def _row_specs(tr, widths):
    return [pl.BlockSpec((tr, w), lambda i: (i, 0)) for w in widths]


def _vec_spec(w):
    return pl.BlockSpec((1, w), lambda i: (0, 0))


def _acc_rows(ref, val, i):
    s = jnp.sum(val, axis=0, keepdims=True)

    @pl.when(i == 0)
    def _():
        ref[...] = s

    @pl.when(i > 0)
    def _():
        ref[...] += s


def modulate(x, scale, shift, name):
    L, D = x.shape
    tr = _tile(L, 512, 16)

    def body(x_ref, sc_ref, sh_ref, h_ref):
        h_ref[...] = (x_ref[...] * (1.0 + sc_ref[...]) + sh_ref[...]).astype(BF16)

    return pl.pallas_call(
        body, grid=(L // tr,), in_specs=_row_specs(tr, [D]) + [_vec_spec(D)] * 2, out_specs=_row_specs(tr, [D])[0],
        out_shape=jax.ShapeDtypeStruct((L, D), BF16), compiler_params=_cp(("parallel",)), name=name)(x, scale, shift)


def _ln_core(x, y, gate, g, b):
    u = ALPHA * x + (1.0 + gate) * y
    mu = jnp.mean(u, axis=-1, keepdims=True)
    d = u - mu
    var = jnp.mean(d * d, axis=-1, keepdims=True)
    rstd = lax.rsqrt(var + LN_EPS)
    xhat = d * rstd
    return xhat * g + b, xhat, rstd


def ln_mid(x, y, gate, g, b, scale, shift):
    L, D = x.shape
    tr = _tile(L, 256, 16)

    def body(x_ref, y_ref, gate_ref, g_ref, b_ref, sc_ref, sh_ref, x1_ref, x1b_ref, h_ref):
        x1, _, _ = _ln_core(x_ref[...], y_ref[...], gate_ref[...], g_ref[...], b_ref[...])
        x1_ref[...] = x1
        x1b_ref[...] = x1.astype(BF16)
        h_ref[...] = (x1 * (1.0 + sc_ref[...]) + sh_ref[...]).astype(BF16)

    return pl.pallas_call(
        body, grid=(L // tr,), in_specs=_row_specs(tr, [D, D]) + [_vec_spec(D)] * 5,
        out_specs=_row_specs(tr, [D, D, D]),
        out_shape=[jax.ShapeDtypeStruct((L, D), F32), jax.ShapeDtypeStruct((L, D), BF16),
                   jax.ShapeDtypeStruct((L, D), BF16)],
        compiler_params=_cp(("parallel",)), name="ln_mid")(x, y, gate, g, b, scale, shift)


def ln_final(x, y, gate, g, b, target):
    L, D = x.shape
    tr = _tile(L, 256, 16)

    def body(x_ref, y_ref, gate_ref, g_ref, b_ref, t_ref, dout_ref, sq_ref):
        out, _, _ = _ln_core(x_ref[...], y_ref[...], gate_ref[...], g_ref[...], b_ref[...])
        err = out - t_ref[...]
        dout_ref[...] = err * (1.0 / D)
        _acc_rows(sq_ref, err * err, pl.program_id(0))

    return pl.pallas_call(
        body, grid=(L // tr,), in_specs=_row_specs(tr, [D, D]) + [_vec_spec(D)] * 3 + _row_specs(tr, [D]),
        out_specs=[_row_specs(tr, [D])[0], _vec_spec(D)],
        out_shape=[jax.ShapeDtypeStruct((L, D), F32), jax.ShapeDtypeStruct((1, D), F32)],
        compiler_params=_cp(("arbitrary",)), name="ln_final")(x, y, gate, g, b, target)


def ln_bwd(dout, x, y, gate, g, name):
    L, D = x.shape
    tr = _tile(L, 256, 16)

    def body(do_ref, x_ref, y_ref, gate_ref, g_ref, dres_ref, dy_ref, dg_ref, db_ref, dgate_ref):
        i = pl.program_id(0)
        yv = y_ref[...]
        dout_v = do_ref[...]
        _, xhat, rstd = _ln_core(x_ref[...], yv, gate_ref[...], g_ref[...], 0.0)
        dxh = dout_v * g_ref[...]
        m1 = jnp.mean(dxh, axis=-1, keepdims=True)
        m2 = jnp.mean(dxh * xhat, axis=-1, keepdims=True)
        du = rstd * (dxh - m1 - xhat * m2)
        dres_ref[...] = ALPHA * du
        dy_ref[...] = ((1.0 + gate_ref[...]) * du).astype(BF16)
        _acc_rows(dg_ref, dout_v * xhat, i)
        _acc_rows(db_ref, dout_v, i)
        _acc_rows(dgate_ref, du * yv, i)

    return pl.pallas_call(
        body, grid=(L // tr,), in_specs=_row_specs(tr, [D, D, D]) + [_vec_spec(D)] * 2,
        out_specs=_row_specs(tr, [D, D]) + [_vec_spec(D)] * 3,
        out_shape=[jax.ShapeDtypeStruct((L, D), F32), jax.ShapeDtypeStruct((L, D), BF16)]
        + [jax.ShapeDtypeStruct((1, D), F32)] * 3,
        compiler_params=_cp(("arbitrary",)), name=name)(dout, x, y, gate, g)


def mod_bwd(dres, dh, dh2, xin, scale, name, through_mod):
    L, D = xin.shape
    tr = _tile(L, 256, 16)

    def body(dres_ref, dh_ref, dh2_ref, x_ref, sc_ref, dx_ref, dsc_ref, dsh_ref):
        i = pl.program_id(0)
        dh_v = dh_ref[...]
        tot = dres_ref[...]
        if through_mod:
            dh_v = dh_v + dh2_ref[...]
        else:
            tot = tot + dh2_ref[...]
        dx_ref[...] = tot + dh_v * (1.0 + sc_ref[...])
        _acc_rows(dsc_ref, dh_v * x_ref[...], i)
        _acc_rows(dsh_ref, dh_v, i)

    return pl.pallas_call(
        body, grid=(L // tr,), in_specs=_row_specs(tr, [D, D, D, D]) + [_vec_spec(D)],
        out_specs=_row_specs(tr, [D]) + [_vec_spec(D)] * 2,
        out_shape=[jax.ShapeDtypeStruct((L, D), F32)] + [jax.ShapeDtypeStruct((1, D), F32)] * 2,
        compiler_params=_cp(("arbitrary",)), name=name)(dres, dh, dh2, xin, scale)


CONV_HALO = 16


def _conv_rows(x_ref, i, tr, L):
    nblk = L // tr
    s = pl.multiple_of(i * tr, CONV_HALO)
    cur = x_ref[pl.ds(s, tr), :].astype(F32)
    sp = pl.multiple_of(jnp.maximum(i * tr - CONV_HALO, 0), CONV_HALO)
    sn = pl.multiple_of(jnp.minimum(i * tr + tr, L - CONV_HALO), CONV_HALO)
    prev = x_ref[pl.ds(sp, CONV_HALO), :].astype(F32) * (i > 0).astype(F32)
    nxt = x_ref[pl.ds(sn, CONV_HALO), :].astype(F32) * (i < nblk - 1).astype(F32)
    return jnp.concatenate([prev, cur, nxt], axis=0)


def _shift_rows(v, j):
    n = v.shape[0]
    return v if j % n == 0 else pltpu.roll(v, j % n, 0)


def _conv_eval(xe, w_ref, b_ref):
    c = b_ref[...] + w_ref[CONV_W - 1:CONV_W, :] * xe
    for k in range(CONV_W - 1):
        c = c + w_ref[k:k + 1, :] * _shift_rows(xe, CONV_W - 1 - k)
    return c


def conv_fwd(zx, col0, conv_w, conv_b):
    L = zx.shape[0]
    C = conv_w.shape[1]
    tc = _tile(C, 512)
    tr = _tile(L, 512, CONV_HALO)
    off = col0 // tc

    def body(x_ref, w_ref, b_ref, o_ref):
        i = pl.program_id(1)
        xe = _conv_rows(x_ref, i, tr, L)
        c = _conv_eval(xe, w_ref, b_ref)[CONV_HALO:CONV_HALO + tr]
        o_ref[...] = _silu(c).astype(BF16)

    return pl.pallas_call(
        body, grid=(C // tc, L // tr),
        in_specs=[pl.BlockSpec((L, tc), lambda j, i: (0, off + j)), pl.BlockSpec((CONV_W, tc), lambda j, i: (0, j)),
                  pl.BlockSpec((1, tc), lambda j, i: (0, j))],
        out_specs=pl.BlockSpec((tr, tc), lambda j, i: (i, j)),
        out_shape=jax.ShapeDtypeStruct((L, C), BF16), compiler_params=_cp(("parallel", "arbitrary")),
        name="conv_fwd")(zx, conv_w, conv_b)


def conv_bwd(zx, col0, conv_w, conv_b, dxbc):
    L = zx.shape[0]
    C = conv_w.shape[1]
    tc = _tile(C, 512)
    tr = _tile(L, 512, CONV_HALO)
    off = col0 // tc
    H = CONV_HALO

    def body(x_ref, g_ref, w_ref, b_ref, dx_ref, dw_ref, db_ref):
        i = pl.program_id(1)
        xe = _conv_rows(x_ref, i, tr, L)
        ge = _conv_rows(g_ref, i, tr, L)
        dc = ge * _dsilu(_conv_eval(xe, w_ref, b_ref))
        dx = w_ref[CONV_W - 1:CONV_W, :] * dc
        for k in range(CONV_W - 1):
            dx = dx + w_ref[k:k + 1, :] * _shift_rows(dc, -(CONV_W - 1 - k))
        dx_ref[...] = dx[H:H + tr].astype(BF16)
        dcc = dc[H:H + tr]
        rows = [jnp.sum(dcc * _shift_rows(xe, CONV_W - 1 - k)[H:H + tr], axis=0, keepdims=True) for k in range(CONV_W)]
        dwv = jnp.concatenate(rows + [jnp.zeros((8 - CONV_W, tc), F32)], axis=0)
        dbv = jnp.sum(dcc, axis=0, keepdims=True)

        @pl.when(i == 0)
        def _():
            dw_ref[...] = dwv
            db_ref[...] = dbv

        @pl.when(i > 0)
        def _():
            dw_ref[...] += dwv
            db_ref[...] += dbv

    dx, dw, db = pl.pallas_call(
        body, grid=(C // tc, L // tr),
        in_specs=[pl.BlockSpec((L, tc), lambda j, i: (0, off + j)), pl.BlockSpec((L, tc), lambda j, i: (0, j)),
                  pl.BlockSpec((CONV_W, tc), lambda j, i: (0, j)), pl.BlockSpec((1, tc), lambda j, i: (0, j))],
        out_specs=[pl.BlockSpec((tr, tc), lambda j, i: (i, j)), pl.BlockSpec((8, tc), lambda j, i: (0, j)),
                   pl.BlockSpec((1, tc), lambda j, i: (0, j))],
        out_shape=[jax.ShapeDtypeStruct((L, C), BF16), jax.ShapeDtypeStruct((8, C), F32),
                   jax.ShapeDtypeStruct((1, C), F32)],
        compiler_params=_cp(("parallel", "arbitrary")), name="conv_bwd")(zx, dxbc, conv_w, conv_b)
    return dx, dw[:CONV_W], db


_NN = (((1,), (0,)), ((), ()))


def _pieces(x, n):
    out, r = [], x
    for _ in range(n):
        p = r.astype(BF16)
        out.append(p)
        r = r - p.astype(F32)
    return out


def _dot01(a, b01, n, dims=_NN):
    b = b01.astype(BF16)
    return functools.reduce(lambda u, v: u + v,
                            [lax.dot_general(p, b, dims, preferred_element_type=F32) for p in _pieces(a, n)])


def _dot01_left(a01, b, n, dims=_NN):
    a = a01.astype(BF16)
    return functools.reduce(lambda u, v: u + v,
                            [lax.dot_general(a, p, dims, preferred_element_type=F32) for p in _pieces(b, n)])


def _ssd_common(dtp_ref, dtpT_ref, bias_ref, biasT_ref, alog_ref, alogT_ref, b_ref, c_ref):
    Q = SSD_Q
    dt = _softplus(dtp_ref[...] + bias_ref[...])
    A = -jnp.exp(alog_ref[...])
    row = lax.broadcasted_iota(jnp.int32, (Q, Q), 0)
    col = lax.broadcasted_iota(jnp.int32, (Q, Q), 1)
    causal = row >= col
    tril = causal.astype(F32)
    Kh = dt.shape[1]
    acum = _dot01_left(tril, dt * A, 3)
    eye = (lax.broadcasted_iota(jnp.int32, (Kh, Kh), 0) == lax.broadcasted_iota(jnp.int32, (Kh, Kh), 1)).astype(F32)
    acumT = _dot01_left(eye, acum, 3, dims=(((1,), (1,)), ((), ())))
    Bm = b_ref[...]
    Cm = c_ref[...]
    cb = lax.dot_general(Cm, Bm, (((1,), (1,)), ((), ())), preferred_element_type=F32)
    return dt, A, causal, row, col, acum, acumT, Bm, Cm, cb


def _ssd_in_specs(Q, GP, N, Kh, DI, cmap):
    nb0 = DI // N
    vec = pl.BlockSpec((None, 1, Kh), lambda g, c: (g, 0, 0))
    vecT = pl.BlockSpec((None, Kh, 1), lambda g, c: (g, 0, 0))
    return [pl.BlockSpec((Q, GP), lambda g, c: (cmap(c), g)),
            pl.BlockSpec((Q, N), lambda g, c: (cmap(c), nb0 + g)),
            pl.BlockSpec((Q, N), lambda g, c: (cmap(c), nb0 + SSD_G + g)),
            pl.BlockSpec((None, Q, Kh), lambda g, c: (g, cmap(c), 0)),
            pl.BlockSpec((None, Kh, Q), lambda g, c: (g, 0, cmap(c))),
            vec, vecT, vec, vecT, vec, vecT]


def _hi(a, b01):
    return _dot01(a, b01, 2)


def _ssd_heads(dskT_ref, acum, acumT, dt, Kh):
    Q, P, N = SSD_Q, SSD_P, SSD_N
    GP = Kh * P
    sh_p = P.bit_length() - 1
    seg = lambda shape, dim: lax.shift_right_logical(lax.broadcasted_iota(jnp.int32, shape, dim), sh_p)
    E = (seg((Kh, GP), 1) == lax.broadcasted_iota(jnp.int32, (Kh, GP), 0)).astype(F32)
    ET = (seg((GP, Kh), 0) == lax.broadcasted_iota(jnp.int32, (GP, Kh), 1)).astype(F32)
    a_last = acum[Q - 1:Q, :]
    tail = jnp.exp(a_last - acum)
    eLT = jnp.exp(acumT[:, Q - 1:Q])
    rowseg = seg((GP, N), 0)
    eL_b = jnp.zeros((GP, N), F32)
    for k in range(Kh):
        eL_b = jnp.where(rowseg == k, eLT[k:k + 1, :], eL_b)
    return dict(
        E=E, ET=ET, a_last=a_last, tail=tail, eL_b=eL_b,
        dt_all=_hi(dt, E), ea_all=_hi(jnp.exp(acum), E), tail_all=_hi(tail, E),
        dsk_all=jnp.sum(E * dskT_ref[...], axis=0, keepdims=True))


def _head_chunks(GP):
    CW = min(GP, 128)
    return CW, CW // SSD_P, GP // CW


def _head_mask(Q, CW, kk):
    lane = lax.broadcasted_iota(jnp.int32, (Q, CW), 1)
    return jnp.logical_and(lane >= kk * SSD_P, lane < (kk + 1) * SSD_P)


def ssd_fwd(xbc, dtp_g, dtp_gT, bias_g, bias_gT, alog_g, alog_gT, dsk_g, dsk_gT, DI):
    L = xbc.shape[0]
    Q, P, N, G = SSD_Q, SSD_P, SSD_N, SSD_G
    GP = DI // G
    Kh = GP // P
    nc = L // Q

    CW, hpc, nch = _head_chunks(GP)
    nt = (((1,), (1,)), ((), ()))
    tn = (((0,), (0,)), ((), ()))

    def body(xs_ref, b_ref, c_ref, dtp_ref, dtpT_ref, bias_ref, biasT_ref, alog_ref, alogT_ref, dsk_ref, dskT_ref,
             y_ref, st_ref, state):
        @pl.when(pl.program_id(1) == 0)
        def _():
            state[...] = jnp.zeros(state.shape, F32)

        st_ref[...] = state[...]
        dt, A, causal, row, col, acum, acumT, Bm, Cm, cb = _ssd_common(
            dtp_ref, dtpT_ref, bias_ref, biasT_ref, alog_ref, alogT_ref, b_ref, c_ref)
        hd = _ssd_heads(dskT_ref, acum, acumT, dt, Kh)
        xs = xs_ref[...].astype(F32)
        xdt_all = xs * hd["dt_all"]
        S_all = state[...]
        y_all = (lax.dot_general(Cm, S_all.astype(BF16), nt, preferred_element_type=F32) * hd["ea_all"]
                 + xs * hd["dsk_all"])
        state[...] = S_all * hd["eL_b"] + lax.dot_general(
            (xdt_all * hd["tail_all"]).astype(BF16), Bm, tn, preferred_element_type=F32)
        for ch in range(nch):
            cs = slice(ch * CW, (ch + 1) * CW)
            xc = xdt_all[:, cs]
            acc = y_all[:, cs]
            for kk in range(hpc):
                k = ch * hpc + kk
                decay = jnp.exp(jnp.where(causal, acum[:, k:k + 1] - acumT[k:k + 1, :], -jnp.inf))
                xk = xc if hpc == 1 else jnp.where(_head_mask(Q, CW, kk), xc, 0.0)
                acc = acc + jnp.dot((cb * decay).astype(BF16), xk.astype(BF16), preferred_element_type=F32)
            y_ref[:, cs] = acc.astype(BF16)

    return pl.pallas_call(
        body, grid=(G, nc), in_specs=_ssd_in_specs(Q, GP, N, Kh, DI, lambda c: c),
        out_specs=[pl.BlockSpec((Q, GP), lambda g, c: (c, g)),
                   pl.BlockSpec((None, None, GP, N), lambda g, c: (c, g, 0, 0))],
        out_shape=[jax.ShapeDtypeStruct((L, DI), BF16), jax.ShapeDtypeStruct((nc, G, GP, N), F32)],
        scratch_shapes=[pltpu.VMEM((GP, N), F32)], compiler_params=_cp(("parallel", "arbitrary")),
        name="ssd_fwd")(xbc, xbc, xbc, dtp_g, dtp_gT, bias_g, bias_gT, alog_g, alog_gT, dsk_g, dsk_gT)


def ssd_bwd(xbc, dtp_g, dtp_gT, bias_g, bias_gT, alog_g, alog_gT, dsk_g, dsk_gT, states, dy, DI):
    L = xbc.shape[0]
    Q, P, N, G = SSD_Q, SSD_P, SSD_N, SSD_G
    GP = DI // G
    Kh = GP // P
    nc = L // Q
    rev = lambda c: nc - 1 - c

    CW, hpc, nch = _head_chunks(GP)

    def body(xs_ref, b_ref, c_ref, dtp_ref, dtpT_ref, bias_ref, biasT_ref, alog_ref, alogT_ref, dsk_ref, dskT_ref,
             st_ref, dy_ref, dxs_ref, dB_ref, dC_ref, ddtp_ref, dbias_ref, dalog_ref, dD_ref, dstate):
        ci = pl.program_id(1)

        @pl.when(ci == 0)
        def _():
            dstate[...] = jnp.zeros(dstate.shape, F32)

        dt, A, causal, row, col, acum, acumT, Bm, Cm, cb = _ssd_common(
            dtp_ref, dtpT_ref, bias_ref, biasT_ref, alog_ref, alogT_ref, b_ref, c_ref)
        tn = (((0,), (0,)), ((), ()))
        nt = (((1,), (1,)), ((), ()))
        hd = _ssd_heads(dskT_ref, acum, acumT, dt, Kh)
        ET, tail = hd["ET"], hd["tail"]
        cbT = lax.dot_general(Bm, Cm, nt, preferred_element_type=F32)
        causalT = row <= col
        xs = xs_ref[...].astype(F32)
        xdt_all = xs * hd["dt_all"]
        dyb = dy_ref[...]
        dy_all = dyb.astype(F32)
        S_all = st_ref[...]
        S_b = S_all.astype(BF16)
        dS_all = dstate[...]
        dS_b = dS_all.astype(BF16)
        CS_all = lax.dot_general(Cm, S_b, nt, preferred_element_type=F32)
        dyE_b = (dy_all * hd["ea_all"]).astype(BF16)
        dC_acc = jnp.dot(dyE_b, S_b, preferred_element_type=F32)
        dS_y = lax.dot_general(dyE_b, Cm, tn, preferred_element_type=F32)
        BdS_all = lax.dot_general(Bm, dS_b, nt, preferred_element_type=F32)
        dB_acc = jnp.dot((xdt_all * hd["tail_all"]).astype(BF16), dS_b, preferred_element_type=F32)
        dtail = _hi(xdt_all * BdS_all, ET)
        da_cols = _hi(dy_all * CS_all * hd["ea_all"], ET) - dtail * tail
        dss = _dot01_left(jnp.ones((8, N), F32), _dot01_left(hd["E"], dS_all * S_all, 2), 2, dims=nt)
        da_last = dss[0:1] * jnp.exp(hd["a_last"]) + jnp.sum(dtail * tail, axis=0, keepdims=True)
        rowi = lax.broadcasted_iota(jnp.int32, (Q, Kh), 0)
        da_cols = da_cols + jnp.where(rowi == Q - 1, da_last, 0.0)
        dstate[...] = hd["eL_b"] * dS_all + dS_y
        sum_mg = jnp.zeros((Q, Q), F32)
        sum_mgt = jnp.zeros((Q, Q), F32)
        dacc = jnp.zeros((Q, 128), F32)
        ddt_x = jnp.zeros((Q, Kh), F32)
        lane128 = lax.broadcasted_iota(jnp.int32, (Q, 128), 1)
        for ch in range(nch):
            cs = slice(ch * CW, (ch + 1) * CW)
            dyc = dyb[:, cs]
            xc_b = xdt_all[:, cs].astype(BF16)
            acc = hd["tail_all"][:, cs] * BdS_all[:, cs]
            for kk in range(hpc):
                k = ch * hpc + kk
                a_b = jnp.broadcast_to(acum[:, k:k + 1], (Q, Q))
                a_r = acumT[k:k + 1, :]
                decay = jnp.exp(jnp.where(causal, a_b - a_r, -jnp.inf))
                decayT = jnp.exp(jnp.where(causalT, a_r - a_b, -jnp.inf))
                dyk = dyc if hpc == 1 else jnp.where(_head_mask(Q, CW, kk), dyc, jnp.zeros_like(dyc))
                mg = decay * lax.dot_general(dyk, xc_b, nt, preferred_element_type=F32)
                mgt = decayT * lax.dot_general(xc_b, dyk, nt, preferred_element_type=F32)
                sum_mg = sum_mg + mg
                sum_mgt = sum_mgt + mgt
                onek = jnp.where(lane128 == k, 1.0, 0.0).astype(BF16)
                dk = mg * cb - mgt * cbT
                dk_hi = dk.astype(BF16)
                dk_lo = (dk - dk_hi.astype(F32)).astype(BF16)
                dacc = dacc + (jnp.dot(dk_hi, onek, preferred_element_type=F32)
                               + jnp.dot(dk_lo, onek, preferred_element_type=F32))
                acc = acc + jnp.dot((decayT * cbT).astype(BF16), dyk, preferred_element_type=F32)
            dxs_ref[:, cs] = (acc * hd["dt_all"][:, cs] + dy_all[:, cs] * hd["dsk_all"][:, cs]).astype(BF16)
            ddt_x = ddt_x + _hi(acc * xs[:, cs], ET[cs, :])
        da_cols = da_cols + dacc[:, :Kh]
        dD_row = jnp.sum(_hi(dy_all * xs, ET), axis=0, keepdims=True)
        dB_ref[...] = (dB_acc + jnp.dot(sum_mgt.astype(BF16), Cm, preferred_element_type=F32)).astype(BF16)
        dC_ref[...] = (dC_acc + jnp.dot(sum_mg.astype(BF16), Bm, preferred_element_type=F32)).astype(BF16)
        triu = (row <= col).astype(F32)
        ddtA = _dot01_left(triu, da_cols, 3)
        ddt = ddt_x + ddtA * A
        dpre = ddt * _sigmoid(dtp_ref[...] + bias_ref[...])
        ddtp_ref[...] = dpre
        dbias_v = jnp.sum(dpre, axis=0, keepdims=True)
        dalog_v = jnp.sum(ddtA * dt, axis=0, keepdims=True) * A

        @pl.when(ci == 0)
        def _():
            dbias_ref[...] = dbias_v
            dalog_ref[...] = dalog_v
            dD_ref[...] = dD_row

        @pl.when(ci > 0)
        def _():
            dbias_ref[...] += dbias_v
            dalog_ref[...] += dalog_v
            dD_ref[...] += dD_row

    vec_o = pl.BlockSpec((None, 1, Kh), lambda g, c: (g, 0, 0))
    return pl.pallas_call(
        body, grid=(G, nc),
        in_specs=_ssd_in_specs(Q, GP, N, Kh, DI, rev)
        + [pl.BlockSpec((None, None, GP, N), lambda g, c: (rev(c), g, 0, 0)),
           pl.BlockSpec((Q, GP), lambda g, c: (rev(c), g))],
        out_specs=[pl.BlockSpec((Q, GP), lambda g, c: (rev(c), g)), pl.BlockSpec((Q, N), lambda g, c: (rev(c), g)),
                   pl.BlockSpec((Q, N), lambda g, c: (rev(c), g)),
                   pl.BlockSpec((None, Q, Kh), lambda g, c: (g, rev(c), 0)), vec_o, vec_o, vec_o],
        out_shape=[jax.ShapeDtypeStruct((L, DI), BF16), jax.ShapeDtypeStruct((L, G * N), BF16),
                   jax.ShapeDtypeStruct((L, G * N), BF16), jax.ShapeDtypeStruct((G, L, Kh), F32)]
        + [jax.ShapeDtypeStruct((G, 1, Kh), F32)] * 3,
        scratch_shapes=[pltpu.VMEM((GP, N), F32)], compiler_params=_cp(("parallel", "arbitrary")),
        name="ssd_bwd")(xbc, xbc, xbc, dtp_g, dtp_gT, bias_g, bias_gT, alog_g, alog_gT, dsk_g, dsk_gT, states, dy)


def _rms_groups(y2, ng_ref, DI):
    S = DI // SSD_G
    for g in range(SSD_G):
        gs = slice(g * S, (g + 1) * S)
        seg = y2[:, gs]
        r = lax.rsqrt(jnp.mean(seg * seg, axis=-1, keepdims=True) + RMS_EPS)
        yield gs, seg * r, r, ng_ref[:, gs]


def rms_gate_fwd(y, zx, norm_g):
    L, DI = y.shape
    tr = _tile(L, 256, 16)

    def body(y_ref, z_ref, ng_ref, o_ref):
        y2 = y_ref[...].astype(F32) * _silu(z_ref[...].astype(F32))
        for gs, yh, _, ng in _rms_groups(y2, ng_ref, DI):
            o_ref[:, gs] = (yh * ng).astype(BF16)

    return pl.pallas_call(
        body, grid=(L // tr,), in_specs=_row_specs(tr, [DI, DI]) + [_vec_spec(DI)], out_specs=_row_specs(tr, [DI])[0],
        out_shape=jax.ShapeDtypeStruct((L, DI), BF16), compiler_params=_cp(("parallel",)),
        name="rms_gate_fwd")(y, zx, norm_g)


def rms_gate_bwd(dyn, y, zx, norm_g):
    L, DI = y.shape
    tr = _tile(L, 256, 16)

    def body(dyn_ref, y_ref, z_ref, ng_ref, dy_ref, dz_ref, dng_ref):
        i = pl.program_id(0)
        yv = y_ref[...].astype(F32)
        zv = z_ref[...].astype(F32)
        sz = _silu(zv)
        dsz = _dsilu(zv)
        dynv = dyn_ref[...].astype(F32)
        for gs, yh, r, ng in _rms_groups(yv * sz, ng_ref, DI):
            dyh = dynv[:, gs] * ng
            dy2 = r * (dyh - yh * jnp.mean(dyh * yh, axis=-1, keepdims=True))
            dy_ref[:, gs] = (dy2 * sz[:, gs]).astype(BF16)
            dz_ref[:, gs] = (dy2 * yv[:, gs] * dsz[:, gs]).astype(BF16)
            s = jnp.sum(dynv[:, gs] * yh, axis=0, keepdims=True)

            @pl.when(i == 0)
            def _():
                dng_ref[:, gs] = s

            @pl.when(i > 0)
            def _():
                dng_ref[:, gs] += s

    return pl.pallas_call(
        body, grid=(L // tr,), in_specs=_row_specs(tr, [DI, DI, DI]) + [_vec_spec(DI)],
        out_specs=_row_specs(tr, [DI, DI]) + [_vec_spec(DI)],
        out_shape=[jax.ShapeDtypeStruct((L, DI), BF16)] * 2 + [jax.ShapeDtypeStruct((1, DI), F32)],
        compiler_params=_cp(("arbitrary",)), name="rms_gate_bwd")(dyn, y, zx, norm_g)


def _alibi_slope(gi, h):
    n = len(DIL_PATTERNS) * DIL_H
    return float(2.0 ** (-8.0 * (gi * DIL_H + h + 1) / n))


def _attn_masks():
    qi = lax.broadcasted_iota(jnp.int32, (DIL_BLK, DIL_BLK), 0)
    kj = lax.broadcasted_iota(jnp.int32, (DIL_BLK, DIL_BLK), 1)
    dcur = (qi - kj).astype(F32)
    return dcur, qi >= kj, dcur + float(DIL_BLK), kj >= qi


def _dil_cols(arr, col0, d):
    HW = DIL_H * DIL_E
    if d == 1:
        return arr, arr.shape[1] // HW, col0 // HW
    return arr[:, col0:col0 + HW].reshape(arr.shape[0] // d, d * HW), 1, 0


def attn_fwd(qz, kv, gi):
    window, d = DIL_PATTERNS[gi]
    assert window // d == DIL_BLK
    L, QZ = qz.shape
    KV = kv.shape[1]
    HW = DIL_H * DIL_E
    M = L // d
    nb = M // DIL_BLK
    nq, nkv = QZ // HW, KV // HW
    scale = DIL_E ** -0.5
    nt = (((1,), (1,)), ((), ()))

    def body(q_ref, kp_ref, kc_ref, vp_ref, vc_ref, o_ref, lse_ref):
        n = pl.program_id(1)
        dcur, vcur, dprev, vprev0 = _attn_masks()
        vprev = jnp.logical_and(vprev0, n > 0)
        lane = lax.broadcasted_iota(jnp.int32, (DIL_BLK, 128), 1)
        lse_acc = jnp.zeros((DIL_BLK, 128), F32)
        for h in range(DIL_H):
            hs = slice(h * DIL_E, (h + 1) * DIL_E)
            sl = _alibi_slope(gi, h) * d
            q = q_ref[:, hs]
            s_c = lax.dot_general(q, kc_ref[:, hs], nt, preferred_element_type=F32) * scale - sl * dcur
            s_p = lax.dot_general(q, kp_ref[:, hs], nt, preferred_element_type=F32) * scale - sl * dprev
            s_c = jnp.where(vcur, s_c, -jnp.inf)
            s_p = jnp.where(vprev, s_p, -jnp.inf)
            m = jnp.maximum(jnp.max(s_c, axis=-1, keepdims=True), jnp.max(s_p, axis=-1, keepdims=True))
            p_c = jnp.exp(s_c - m)
            p_p = jnp.exp(s_p - m)
            den = jnp.sum(p_c, axis=-1, keepdims=True) + jnp.sum(p_p, axis=-1, keepdims=True)
            o = (jnp.dot(p_c.astype(BF16), vc_ref[:, hs], preferred_element_type=F32)
                 + jnp.dot(p_p.astype(BF16), vp_ref[:, hs], preferred_element_type=F32)) / den
            o_ref[:, hs] = o.astype(BF16)
            lse_acc = jnp.where(lane == h, m + jnp.log(den), lse_acc)
        lse_ref[...] = lse_acc

    blk = (DIL_BLK, HW)
    prev = lambda n: jnp.maximum(n - 1, 0)
    qv, qn, qo = _dil_cols(qz, gi * HW, d)
    kv_, kn, ko = _dil_cols(kv, gi * HW, d)
    vv, vn, vo = _dil_cols(kv, (nkv // 2 + gi) * HW, d)
    o, lse = pl.pallas_call(
        body, grid=(d, nb),
        in_specs=[pl.BlockSpec(blk, lambda r, n: (n, r * qn + qo)),
                  pl.BlockSpec(blk, lambda r, n: (prev(n), r * kn + ko)),
                  pl.BlockSpec(blk, lambda r, n: (n, r * kn + ko)),
                  pl.BlockSpec(blk, lambda r, n: (prev(n), r * vn + vo)),
                  pl.BlockSpec(blk, lambda r, n: (n, r * vn + vo))],
        out_specs=[pl.BlockSpec(blk, lambda r, n: (n, r)), pl.BlockSpec((DIL_BLK, 128), lambda r, n: (n, r))],
        out_shape=[jax.ShapeDtypeStruct((M, d * HW), BF16), jax.ShapeDtypeStruct((M, d * 128), F32)],
        compiler_params=_cp(("parallel", "parallel")), name=f"attn_fwd_{gi}")(qv, kv_, kv_, vv, vv)
    return o.reshape(L, HW), lse.reshape(L, 128)


def attn_bwd(qz, kv, do, lse, dpr, gi):
    window, d = DIL_PATTERNS[gi]
    L, QZ = qz.shape
    KV = kv.shape[1]
    HW = DIL_H * DIL_E
    M = L // d
    nb = M // DIL_BLK
    nq, nkv = QZ // HW, KV // HW
    scale = DIL_E ** -0.5
    nt = (((1,), (1,)), ((), ()))
    tn = (((0,), (0,)), ((), ()))

    def body(q0_ref, q1_ref, k_ref, v_ref, do0_ref, do1_ref, l0_ref, l1_ref, r0_ref, r1_ref,
             dq_ref, dk_ref, dv_ref, carry):
        n = pl.program_id(1)

        @pl.when(n == 0)
        def _():
            carry[...] = jnp.zeros(carry.shape, F32)

        dcur, vcur, dprev, vprev0 = _attn_masks()
        vprev = jnp.logical_and(vprev0, n < nb - 1)
        for h in range(DIL_H):
            hs = slice(h * DIL_E, (h + 1) * DIL_E)
            sl = _alibi_slope(gi, h) * d
            kh = k_ref[:, hs]
            vh = v_ref[:, hs]
            q0, q1 = q0_ref[:, hs], q1_ref[:, hs]
            do0, do1 = do0_ref[:, hs], do1_ref[:, hs]
            s0 = lax.dot_general(q0, kh, nt, preferred_element_type=F32) * scale - sl * dcur
            p0 = jnp.exp(jnp.where(vcur, s0 - l0_ref[:, h:h + 1], -jnp.inf))
            ds0 = p0 * (lax.dot_general(do0, vh, nt, preferred_element_type=F32) - r0_ref[:, h:h + 1])
            s1 = lax.dot_general(q1, kh, nt, preferred_element_type=F32) * scale - sl * dprev
            p1 = jnp.exp(jnp.where(vprev, s1 - l1_ref[:, h:h + 1], -jnp.inf))
            ds1 = p1 * (lax.dot_general(do1, vh, nt, preferred_element_type=F32) - r1_ref[:, h:h + 1])
            ds0_b = (ds0 * scale).astype(BF16)
            ds1_b = (ds1 * scale).astype(BF16)
            dv = (lax.dot_general(p0.astype(BF16), do0, tn, preferred_element_type=F32)
                  + lax.dot_general(p1.astype(BF16), do1, tn, preferred_element_type=F32))
            dk = (lax.dot_general(ds0_b, q0, tn, preferred_element_type=F32)
                  + lax.dot_general(ds1_b, q1, tn, preferred_element_type=F32))
            dv_ref[:, hs] = dv.astype(BF16)
            dk_ref[:, hs] = dk.astype(BF16)
            dq_ref[:, hs] = (carry[:, hs] + jnp.dot(ds0_b, kh, preferred_element_type=F32)).astype(BF16)
            carry[:, hs] = jnp.dot(ds1_b, kh, preferred_element_type=F32)

    blk = (DIL_BLK, HW)
    sblk = (DIL_BLK, 128)
    nxt = lambda n: jnp.minimum(n + 1, nb - 1)
    qv, qn, qo = _dil_cols(qz, gi * HW, d)
    kv_, kn, ko = _dil_cols(kv, gi * HW, d)
    vv, vn, vo = _dil_cols(kv, (nkv // 2 + gi) * HW, d)
    dov = do.reshape(M, d * HW)
    lv = lse.reshape(M, d * 128)
    rv = dpr.reshape(M, d * 128)
    outs = pl.pallas_call(
        body, grid=(d, nb),
        in_specs=[pl.BlockSpec(blk, lambda r, n: (n, r * qn + qo)), pl.BlockSpec(blk, lambda r, n: (nxt(n), r * qn + qo)),
                  pl.BlockSpec(blk, lambda r, n: (n, r * kn + ko)),
                  pl.BlockSpec(blk, lambda r, n: (n, r * vn + vo)),
                  pl.BlockSpec(blk, lambda r, n: (n, r)), pl.BlockSpec(blk, lambda r, n: (nxt(n), r)),
                  pl.BlockSpec(sblk, lambda r, n: (n, r)), pl.BlockSpec(sblk, lambda r, n: (nxt(n), r)),
                  pl.BlockSpec(sblk, lambda r, n: (n, r)), pl.BlockSpec(sblk, lambda r, n: (nxt(n), r))],
        out_specs=[pl.BlockSpec(blk, lambda r, n: (n, r))] * 3,
        out_shape=[jax.ShapeDtypeStruct((M, d * HW), BF16)] * 3,
        scratch_shapes=[pltpu.VMEM(blk, F32)], compiler_params=_cp(("parallel", "arbitrary")),
        name=f"attn_bwd_{gi}")(qv, qv, kv_, vv, dov, dov, lv, lv, rv, rv)
    return [t.reshape(L, HW) for t in outs]


def _merge_weights(l_refs, h):
    ls = [r[:, h:h + 1] for r in l_refs]
    mx = functools.reduce(jnp.maximum, ls)
    es = [jnp.exp(l - mx) for l in ls]
    den = functools.reduce(lambda a, b: a + b, es)
    return [e / den for e in es]


def merge_fwd(os_, lses, qz):
    L, HW = os_[0].shape
    tr = _tile(L, 256, 16)
    ng = len(os_)
    zblk = qz.shape[1] // HW - 1

    def body(*refs):
        o_refs, l_refs, z_ref, out_ref = refs[:ng], refs[ng:2 * ng], refs[2 * ng], refs[2 * ng + 1]
        for h in range(DIL_H):
            hs = slice(h * DIL_E, (h + 1) * DIL_E)
            ws = _merge_weights(l_refs, h)
            om = functools.reduce(lambda a, b: a + b, [w * o[:, hs].astype(F32) for w, o in zip(ws, o_refs)])
            out_ref[:, hs] = (om * _silu(z_ref[:, hs].astype(F32))).astype(BF16)

    return pl.pallas_call(
        body, grid=(L // tr,),
        in_specs=_row_specs(tr, [HW] * ng + [128] * ng) + [pl.BlockSpec((tr, HW), lambda i: (i, zblk))],
        out_specs=_row_specs(tr, [HW])[0], out_shape=jax.ShapeDtypeStruct((L, HW), BF16),
        compiler_params=_cp(("parallel",)), name="merge_fwd")(*os_, *lses, qz)


def merge_bwd(dgated, os_, lses, qz):
    L, HW = os_[0].shape
    tr = _tile(L, 256, 16)
    ng = len(os_)
    zblk = qz.shape[1] // HW - 1

    def body(*refs):
        dg_ref = refs[0]
        o_refs, l_refs, z_ref = refs[1:1 + ng], refs[1 + ng:1 + 2 * ng], refs[1 + 2 * ng]
        outs = refs[2 + 2 * ng:]
        do_refs, dpr_refs, dz_ref = outs[:ng], outs[ng:2 * ng], outs[2 * ng]
        lane = lax.broadcasted_iota(jnp.int32, (tr, 128), 1)
        accs = [jnp.zeros((tr, 128), F32) for _ in range(ng)]
        for h in range(DIL_H):
            hs = slice(h * DIL_E, (h + 1) * DIL_E)
            ws = _merge_weights(l_refs, h)
            ov = [o[:, hs].astype(F32) for o in o_refs]
            om = functools.reduce(lambda a, b: a + b, [w * o for w, o in zip(ws, ov)])
            zv = z_ref[:, hs].astype(F32)
            dgv = dg_ref[:, hs].astype(F32)
            dom = dgv * _silu(zv)
            dz_ref[:, hs] = (dgv * om * _dsilu(zv)).astype(BF16)
            dws = [jnp.sum(dom * o, axis=-1, keepdims=True) for o in ov]
            dwbar = functools.reduce(lambda a, b: a + b, [w * dw for w, dw in zip(ws, dws)])
            for g in range(ng):
                do_refs[g][:, hs] = (ws[g] * dom).astype(BF16)
                accs[g] = jnp.where(lane == h, ws[g] * dwbar, accs[g])
        for g in range(ng):
            dpr_refs[g][...] = accs[g]

    outs = pl.pallas_call(
        body, grid=(L // tr,),
        in_specs=_row_specs(tr, [HW] * (1 + ng) + [128] * ng) + [pl.BlockSpec((tr, HW), lambda i: (i, zblk))],
        out_specs=_row_specs(tr, [HW] * ng + [128] * ng + [HW]),
        out_shape=[jax.ShapeDtypeStruct((L, HW), BF16)] * ng + [jax.ShapeDtypeStruct((L, 128), F32)] * ng
        + [jax.ShapeDtypeStruct((L, HW), BF16)],
        compiler_params=_cp(("parallel",)), name="merge_bwd")(dgated, *os_, *lses, qz)
    return outs[:ng], outs[ng:2 * ng], outs[2 * ng]


def ada_fwd(c8, ada_w):
    nl, D, Ws = ada_w.shape
    tn = _tile(Ws, 512)

    def body(c_ref, w_ref, o_ref):
        o_ref[...] = jnp.dot(_silu(c_ref[...]), w_ref[...], precision=lax.Precision.HIGHEST,
                             preferred_element_type=F32)

    return pl.pallas_call(
        body, grid=(nl, Ws // tn),
        in_specs=[pl.BlockSpec((N_DEV, D), lambda l, j: (0, 0)), pl.BlockSpec((None, D, tn), lambda l, j: (l, 0, j))],
        out_specs=pl.BlockSpec((None, N_DEV, tn), lambda l, j: (l, 0, j)),
        out_shape=jax.ShapeDtypeStruct((nl, N_DEV, Ws), F32), compiler_params=_cp(("parallel", "parallel")),
        name="ada_fwd")(c8, ada_w)


def ada_wgrad(c8t, dmod):
    nl, _, Ws = dmod.shape
    D = c8t.shape[0]
    tm = _tile(D, 512, 8)

    def body(c_ref, d_ref, o_ref):
        sc = _silu(c_ref[...])
        acc = sc[:, 0:1] * d_ref[0:1, :]
        for e in range(1, N_DEV):
            acc = acc + sc[:, e:e + 1] * d_ref[e:e + 1, :]
        o_ref[...] = acc

    return pl.pallas_call(
        body, grid=(nl, D // tm),
        in_specs=[pl.BlockSpec((tm, N_DEV), lambda l, i: (i, 0)), pl.BlockSpec((None, N_DEV, Ws), lambda l, i: (l, 0, 0))],
        out_specs=pl.BlockSpec((None, tm, Ws), lambda l, i: (l, i, 0)),
        out_shape=jax.ShapeDtypeStruct((nl, D, Ws), F32), compiler_params=_cp(("parallel", "parallel")),
        name="ada_wgrad")(c8t, dmod)


def adamw(w, g, m, v, name):
    R, C = w.shape
    tr = _tile(R, 256, 8)
    c1 = 1.0 - ADAM_B1 ** ADAM_STEP
    c2 = 1.0 - ADAM_B2 ** ADAM_STEP

    def body(w_ref, g_ref, m_ref, v_ref, d_ref, nm_ref, nv_ref):
        gv = g_ref[...]
        nm = ADAM_B1 * m_ref[...] + (1.0 - ADAM_B1) * gv
        nv = ADAM_B2 * v_ref[...] + (1.0 - ADAM_B2) * (gv * gv)
        nm_ref[...] = nm
        nv_ref[...] = nv
        d_ref[...] = -ADAM_LR * ((nm / c1) / (jnp.sqrt(nv / c2) + ADAM_EPS) + ADAM_WD * w_ref[...])

    return pl.pallas_call(
        body, grid=(R // tr,), in_specs=_row_specs(tr, [C] * 4), out_specs=_row_specs(tr, [C] * 3),
        out_shape=[jax.ShapeDtypeStruct((R, C), F32)] * 3, compiler_params=_cp(("parallel",)), name=name)(w, g, m, v)


def sum_leading(t, name, out_dtype=F32):
    S, R, C = t.shape
    tr = _tile(R, 256, 16)

    def body(t_ref, o_ref):
        acc = t_ref[0].astype(F32)
        for s in range(1, S):
            acc = acc + t_ref[s].astype(F32)
        o_ref[...] = acc.astype(out_dtype)

    return pl.pallas_call(
        body, grid=(R // tr,), in_specs=[pl.BlockSpec((S, tr, C), lambda i: (0, i, 0))],
        out_specs=pl.BlockSpec((tr, C), lambda i: (i, 0)), out_shape=jax.ShapeDtypeStruct((R, C), out_dtype),
        compiler_params=_cp(("parallel",)), name=name)(t)


def add_half(g, a, core, name):
    S, R, C = g.shape
    h = R // 2
    tr = _tile(h, 256, 16)
    nb = h // tr

    def body(core_ref, g_ref, a_ref, o_ref):
        o_ref[...] = (g_ref[...].astype(F32) + a_ref[...].astype(F32)).astype(BF16)

    return pl.pallas_call(
        body,
        grid_spec=pltpu.PrefetchScalarGridSpec(
            num_scalar_prefetch=1, grid=(S, nb),
            in_specs=[pl.BlockSpec((None, tr, C), lambda s, i, core_ref: (s, core_ref[0] * nb + i, 0)),
                      pl.BlockSpec((None, tr, C), lambda s, i, core_ref: (s, i, 0))],
            out_specs=pl.BlockSpec((None, tr, C), lambda s, i, core_ref: (s, i, 0))),
        out_shape=jax.ShapeDtypeStruct((S, h, C), BF16), compiler_params=_cp(("parallel", "parallel")),
        name=name)(core, g, a)


def sum_partials(own, landed, chip, name):
    _, h, C = own.shape
    tr = _tile(h, 256, 16)

    def body(chip_ref, own_ref, l_ref, o_ref):
        acc = own_ref[...].astype(F32)
        for j in range(3):
            acc = acc + l_ref[j].astype(F32)
        o_ref[...] = acc

    return pl.pallas_call(
        body,
        grid_spec=pltpu.PrefetchScalarGridSpec(
            num_scalar_prefetch=1, grid=(h // tr,),
            in_specs=[pl.BlockSpec((None, tr, C), lambda i, chip_ref: (chip_ref[0], i, 0)),
                      pl.BlockSpec((3, tr, C), lambda i, chip_ref: (0, i, 0))],
            out_specs=pl.BlockSpec((tr, C), lambda i, chip_ref: (i, 0))),
        out_shape=jax.ShapeDtypeStruct((h, C), F32), compiler_params=_cp(("parallel",)), name=name)(chip, own, landed)


def adamw_halves(w, g_mine, g_theirs, m, v, core, name):
    R, C = w.shape
    h = R // 2
    tr = _tile(h, 256, 8)
    nbh = h // tr
    c1 = 1.0 - ADAM_B1 ** ADAM_STEP
    c2 = 1.0 - ADAM_B2 ** ADAM_STEP

    def body(core_ref, w_ref, gm_ref, gt_ref, m_ref, v_ref, g_ref, d_ref, nm_ref, nv_ref):
        mine = (pl.program_id(0) // nbh) == core_ref[0]
        gv = jnp.where(mine, gm_ref[...], gt_ref[...])
        g_ref[...] = gv
        nm = ADAM_B1 * m_ref[...] + (1.0 - ADAM_B1) * gv
        nv = ADAM_B2 * v_ref[...] + (1.0 - ADAM_B2) * (gv * gv)
        nm_ref[...] = nm
        nv_ref[...] = nv
        d_ref[...] = -ADAM_LR * ((nm / c1) / (jnp.sqrt(nv / c2) + ADAM_EPS) + ADAM_WD * w_ref[...])

    full = pl.BlockSpec((tr, C), lambda i, core_ref: (i, 0))
    halfspec = pl.BlockSpec((tr, C), lambda i, core_ref: (i % nbh, 0))
    return pl.pallas_call(
        body,
        grid_spec=pltpu.PrefetchScalarGridSpec(
            num_scalar_prefetch=1, grid=(2 * nbh,), in_specs=[full, halfspec, halfspec, full, full],
            out_specs=[full] * 4),
        out_shape=[jax.ShapeDtypeStruct((R, C), F32)] * 4, compiler_params=_cp(("parallel",)),
        name=name)(core, w, g_mine, g_theirs, m, v)


_ANY = pl.BlockSpec(memory_space=pl.ANY)


def _place():
    x, y, c = lax.axis_index("x"), lax.axis_index("y"), lax.axis_index("c")
    chips = [(1 - x, y), (x, 1 - y), (1 - x, 1 - y)]
    return x, y, c, chips


def allgather_small(v, name):
    R, W = v.shape

    def body(x_ref, out_ref, send_sems, recv_sems, local_sem):
        x, y, c, chips = _place()
        me, sibling = (x, y, c), (x, y, 1 - c)

        def rows(px, py, pc):
            return out_ref.at[pl.ds((4 * px + 2 * py + pc) * R, R), :]

        def copy(k, block, to, src=None):
            return pltpu.make_async_remote_copy(
                src_ref=rows(*block) if src is None else src, dst_ref=rows(*block),
                send_sem=send_sems.at[k], recv_sem=recv_sems.at[k], device_id=to, device_id_type=MESH)

        mine = pltpu.make_async_copy(x_ref, rows(*me), local_sem)
        mine.start()
        first = [copy(0, me, sibling, src=x_ref)]
        first += [copy(1 + j, me, (*chip, c), src=x_ref) for j, chip in enumerate(chips)]
        for cp in first:
            cp.start()
        passed = [copy(4 + j, (*chip, c), sibling) for j, chip in enumerate(chips)]
        for j, chip in enumerate(chips):
            copy(1 + j, (*chip, c), me).wait_recv()
            passed[j].start()
        copy(0, sibling, me).wait_recv()
        for j, chip in enumerate(chips):
            copy(4 + j, (*chip, 1 - c), me).wait_recv()
        for cp in first + passed:
            cp.wait_send()
        mine.wait()

    return pl.pallas_call(
        body, out_shape=jax.ShapeDtypeStruct((N_DEV * R, W), v.dtype),
        in_specs=[pl.BlockSpec(memory_space=pltpu.VMEM)], out_specs=pl.BlockSpec(memory_space=pltpu.VMEM),
        scratch_shapes=[pltpu.SemaphoreType.DMA((7,)), pltpu.SemaphoreType.DMA((7,)), pltpu.SemaphoreType.DMA],
        name=name)(v)


def allgather_weights(shards):
    n = len(shards)

    def body(*refs):
        ins, outs = refs[:n], refs[n:2 * n]
        send_sems, recv_sems = refs[2 * n:]
        x, y, c, chips = _place()
        p = 2 * x + y
        sibling = (x, y, 1 - c)

        def half(i, chip_id, core, ref=None):
            r = outs[i].at[chip_id] if ref is None else ref
            return r.at[core]

        def copy(i, k, chip_id, core, to, src=None):
            return pltpu.make_async_remote_copy(
                src_ref=half(i, chip_id, core) if src is None else src, dst_ref=half(i, chip_id, core),
                send_sem=send_sems.at[6 * i + k], recv_sem=recv_sems.at[6 * i + k], device_id=to, device_id_type=MESH)

        first = [copy(i, j, p, c, (*chip, c), src=half(i, p, c, ref=ins[i]))
                 for i in range(n) for j, chip in enumerate(chips)]
        for cp in first:
            cp.start()
        passed = []
        for i in range(n):
            for j, (cx, cy) in enumerate(chips):
                copy(i, j, 2 * cx + cy, c, sibling).wait_recv()
                fw = copy(i, 3 + j, 2 * cx + cy, c, sibling)
                fw.start()
                passed.append(fw)
        for i in range(n):
            for j, (cx, cy) in enumerate(chips):
                copy(i, 3 + j, 2 * cx + cy, 1 - c, sibling).wait_recv()
        for cp in first + passed:
            cp.wait_send()

    split = [s.reshape(2, s.shape[0] // 2, s.shape[1]) for s in shards]
    outs = pl.pallas_call(
        body, out_shape=[jax.ShapeDtypeStruct((N_CHIPS,) + s.shape, s.dtype) for s in split],
        in_specs=[_ANY] * n, out_specs=[_ANY] * n,
        scratch_shapes=[pltpu.SemaphoreType.DMA((6 * n,)), pltpu.SemaphoreType.DMA((6 * n,))],
        name="allgather_weights")(*split)
    chip = 2 * lax.axis_index("x") + lax.axis_index("y")
    return [lax.dynamic_update_index_in_dim(o, s, chip, 0).reshape((N_CHIPS,) + sh.shape)
            for o, s, sh in zip(outs, split, shards)]


def exchange_halves_to_sibling(gs):
    n = len(gs)

    def body(*refs):
        ins, outs = refs[:n], refs[n:2 * n]
        send_sems, recv_sems = refs[2 * n:]
        x, y, c, _ = _place()
        cps = []
        for i in range(n):
            h = ins[i].shape[1] // 2
            cps.append(pltpu.make_async_remote_copy(
                src_ref=ins[i].at[:, pl.ds((1 - c) * h, h), :], dst_ref=outs[i],
                send_sem=send_sems.at[i], recv_sem=recv_sems.at[i], device_id=(x, y, 1 - c), device_id_type=MESH))
        for cp in cps:
            cp.start()
        for cp in cps:
            cp.wait()

    return pl.pallas_call(
        body, out_shape=[jax.ShapeDtypeStruct((g.shape[0], g.shape[1] // 2, g.shape[2]), g.dtype) for g in gs],
        in_specs=[_ANY] * n, out_specs=[_ANY] * n,
        scratch_shapes=[pltpu.SemaphoreType.DMA((n,)), pltpu.SemaphoreType.DMA((n,))],
        name="rs_sibling_exchange")(*gs)


def scatter_to_chips(ps):
    n = len(ps)

    def body(*refs):
        ins, outs = refs[:n], refs[n:2 * n]
        send_sems, recv_sems = refs[2 * n:]
        x, y, c, chips = _place()
        cps = []
        for i in range(n):
            for j, (cx, cy) in enumerate(chips):
                cps.append(pltpu.make_async_remote_copy(
                    src_ref=ins[i].at[2 * cx + cy], dst_ref=outs[i].at[j], send_sem=send_sems.at[3 * i + j],
                    recv_sem=recv_sems.at[3 * i + j], device_id=(cx, cy, c), device_id_type=MESH))
        for cp in cps:
            cp.start()
        for cp in cps:
            cp.wait()

    return pl.pallas_call(
        body, out_shape=[jax.ShapeDtypeStruct((3,) + t.shape[1:], t.dtype) for t in ps],
        in_specs=[_ANY] * n, out_specs=[_ANY] * n,
        scratch_shapes=[pltpu.SemaphoreType.DMA((3 * n,)), pltpu.SemaphoreType.DMA((3 * n,))],
        name="rs_chip_scatter")(*ps)


def join_halves(rs):
    n = len(rs)

    def body(*refs):
        ins, outs = refs[:n], refs[n:2 * n]
        send_sems, recv_sems = refs[2 * n:]
        x, y, c, _ = _place()
        cps = [pltpu.make_async_remote_copy(
            src_ref=ins[i], dst_ref=outs[i], send_sem=send_sems.at[i], recv_sem=recv_sems.at[i],
            device_id=(x, y, 1 - c), device_id_type=MESH) for i in range(n)]
        for cp in cps:
            cp.start()
        for cp in cps:
            cp.wait()

    return pl.pallas_call(
        body, out_shape=[jax.ShapeDtypeStruct(r.shape, r.dtype) for r in rs],
        in_specs=[_ANY] * n, out_specs=[_ANY] * n,
        scratch_shapes=[pltpu.SemaphoreType.DMA((n,)), pltpu.SemaphoreType.DMA((n,))],
        name="rs_join_halves")(*rs)


def _pack(parts, row_mult=8):
    flat = jnp.concatenate([p.reshape(-1).astype(F32) for p in parts])
    unit = row_mult * 128
    n = -(-flat.shape[0] // unit) * unit
    return jnp.pad(flat, (0, n - flat.shape[0])).reshape(n // 128, 128)


def _unpack(flat, shapes):
    out, off = [], 0
    for s in shapes:
        n = int(np.prod(s))
        out.append(flat[off:off + n].reshape(s))
        off += n
    return out


def _gather_packed(parts, name):
    packed = _pack(parts)
    g = allgather_small(packed, name).reshape(N_DEV, -1)
    return _unpack_rows(g, [p.shape for p in parts])


def _unpack_rows(g, shapes):
    out, off = [], 0
    for s in shapes:
        n = int(np.prod(s))
        out.append(g[:, off:off + n].reshape((g.shape[0],) + tuple(s)))
        off += n
    return out


def _by_chip(t, axis):
    return jnp.concatenate([t[2 * p] for p in range(N_CHIPS)], axis=axis)


def kernel(x, c, ada_w, ada_b, ln_g, ln_b, a_in_w, a_conv_w, a_conv_b, a_dt_bias, a_A_log, a_D, a_norm_g, a_out_w, kv_w, b_in_w, b_out_w, loss_target, m_ada_w, m_ada_b, m_ln_g, m_ln_b, m_a_in_w, m_a_conv_w, m_a_conv_b, m_a_dt_bias, m_a_A_log, m_a_D, m_a_norm_g, m_a_out_w, m_kv_w, m_b_in_w, m_b_out_w, v_ada_w, v_ada_b, v_ln_g, v_ln_b, v_a_in_w, v_a_conv_w, v_a_conv_b, v_a_dt_bias, v_a_A_log, v_a_D, v_a_norm_g, v_a_out_w, v_kv_w, v_b_in_w, v_b_out_w):
    ax, ay, ac = lax.axis_index("x"), lax.axis_index("y"), lax.axis_index("c")
    chip = 2 * ax + ay
    dev = 4 * ax + 2 * ay + ac
    xin = x[0]
    tgt = loss_target[0]
    L, D = xin.shape
    G, P = SSD_G, SSD_P
    H = a_dt_bias.shape[1]
    Kh = H // G
    DI = H * P
    CONVD = a_conv_b.shape[1] * N_CHIPS
    HW = DIL_H * DIL_E
    Ws = ada_w.shape[2]

    w_in_g, w_out_g, w_kv_g, w_bin_g, w_bout_g = allgather_weights(
        [a_in_w[0].astype(BF16), a_out_w[0].astype(BF16), kv_w.astype(BF16), b_in_w[0].astype(BF16),
         b_out_w[0].astype(BF16)])
    w_in = jnp.transpose(w_in_g, (1, 0, 2)).reshape(D, -1)
    w_zx = w_in[:, :DI + CONVD]
    w_dt = jnp.pad(w_in[:, DI + CONVD:], ((0, 0), (0, 128 - H)))

    c8, cw8, cb8, ng8 = _gather_packed([c[0], a_conv_w[0], a_conv_b[0], a_norm_g[0]], "allgather_small_params")
    conv_w = _by_chip(cw8, 1)
    conv_b = _by_chip(cb8, 0).reshape(1, CONVD)
    norm_g = _by_chip(ng8, 0).reshape(1, DI)

    mod_s = ada_fwd(c8, ada_w)
    (mod8,) = _gather_packed([mod_s], "allgather_small_mod")
    mods = _by_chip(mod8, 2)
    mod = lax.dynamic_index_in_dim(mods, dev, axis=1, keepdims=False) + ada_b
    shift = [mod[l:l + 1, :D] for l in range(DEPTH)]
    scale = [mod[l:l + 1, D:2 * D] for l in range(DEPTH)]
    gate = [mod[l:l + 1, 2 * D:] for l in range(DEPTH)]
    lg = [ln_g[l:l + 1] for l in range(DEPTH)]
    lb = [ln_b[l:l + 1] for l in range(DEPTH)]

    h0 = modulate(xin, scale[0], shift[0], "modulate0")
    zx = mm_nn(h0, w_zx, BF16, "mm_in_zx")
    dtp = mm_nn(h0, w_dt, F32, "mm_in_dt")
    xbc = conv_fwd(zx, DI, conv_w, conv_b)
    dtp_g = jnp.transpose(dtp[:, :H].reshape(L, G, Kh), (1, 0, 2))
    dtp_gT = jnp.transpose(dtp_g, (0, 2, 1))
    vecs = [a_dt_bias.reshape(G, 1, Kh), a_dt_bias.reshape(G, Kh, 1), a_A_log.reshape(G, 1, Kh),
            a_A_log.reshape(G, Kh, 1), a_D.reshape(G, 1, Kh), a_D.reshape(G, Kh, 1)]
    y_ssd, states = ssd_fwd(xbc, dtp_g, dtp_gT, *vecs, DI)
    yn = rms_gate_fwd(y_ssd, zx, norm_g)
    ymix0 = mm_nn(yn, w_out_g, F32, "mm_out_a", stack="row")
    x1, x1b, h1 = ln_mid(xin, ymix0, gate[0], lg[0], lb[0], scale[1], shift[1])

    kvp = mm_nn(x1b, w_kv_g, BF16, "mm_kv", stack="col")
    qz = mm_nn(h1, w_bin_g, BF16, "mm_in_b", stack="col")
    os_, lses = [], []
    for gi in range(len(DIL_PATTERNS)):
        o, lse = attn_fwd(qz, kvp, gi)
        os_.append(o)
        lses.append(lse)
    om = merge_fwd(os_, lses, qz)
    ymix1 = mm_nn(om, w_bout_g, F32, "mm_out_b", stack="col")
    dx2, sq = ln_final(x1, ymix1, gate[1], lg[1], lb[1], tgt)
    loss_part = 0.5 * jnp.sum(sq) / D

    dres2, dy2, dg1, db1, dgate1 = ln_bwd(dx2, x1, ymix1, gate[1], lg[1], "ln_bwd1")
    g_bout = mm_tn(om, dy2, BF16, "mm_gw_out_b", stack="col")
    dgated = mm_nt(dy2, w_bout_g, BF16, "mm_gx_out_b", stack="col")
    dos, dprs, dz_b = merge_bwd(dgated, os_, lses, qz)
    dqs, dks, dvs = [], [], []
    for gi in range(len(DIL_PATTERNS)):
        dq, dk, dv = attn_bwd(qz, kvp, dos[gi], lses[gi], dprs[gi], gi)
        dqs.append(dq)
        dks.append(dk)
        dvs.append(dv)
    dqz = jnp.concatenate(dqs + [dz_b], axis=1)
    dkv = jnp.concatenate(dks + dvs, axis=1)
    g_bin = mm_tn(h1, dqz, BF16, "mm_gw_in_b", stack="col")
    dh1 = mm_nt(dqz, w_bin_g, F32, "mm_gx_in_b", stack="col")
    g_kv = mm_tn(x1b, dkv, BF16, "mm_gw_kv", stack="col")
    dx1_kv = mm_nt(dkv, w_kv_g, F32, "mm_gx_kv", stack="col")
    dx1, dscale1, dshift1 = mod_bwd(dres2, dh1, dx1_kv, x1, scale[1], "mod_bwd1", through_mod=False)

    dres1, dy1, dg0, db0, dgate0 = ln_bwd(dx1, xin, ymix0, gate[0], lg[0], "ln_bwd0")
    g_out = mm_tn(yn, dy1, BF16, "mm_gw_out_a", stack="row")
    dyn = mm_nt(dy1, w_out_g, BF16, "mm_gx_out_a", stack="row")
    dy_ssd, dz_a, dnorm_g = rms_gate_bwd(dyn, y_ssd, zx, norm_g)
    dxs, dB, dC, ddtp_g, dbias_g, dalog_g, dD_g = ssd_bwd(xbc, dtp_g, dtp_gT, *vecs, states, dy_ssd, DI)
    dxbc = jnp.concatenate([dxs, dB, dC], axis=1)
    dxbc_pre, dconv_w, dconv_b = conv_bwd(zx, DI, conv_w, conv_b, dxbc)
    dzx = jnp.concatenate([dz_a, dxbc_pre], axis=1)
    ddtp = jnp.pad(jnp.transpose(ddtp_g, (1, 0, 2)).reshape(L, H), ((0, 0), (0, 128 - H)))
    g_zx = mm_tn(h0, dzx, BF16, "mm_gw_in_zx")
    g_dt = mm_tn(h0, ddtp, BF16, "mm_gw_in_dt")
    dh0 = mm_nt(dzx, w_zx, F32, "mm_gx_in_zx")
    dh0_dt = mm_nt(ddtp, w_dt, F32, "mm_gx_in_dt")
    grad_x, dscale0, dshift0 = mod_bwd(dres1, dh0, dh0_dt, xin, scale[0], "mod_bwd0", through_mod=True)
    g_in = jnp.concatenate([g_zx, g_dt[:, :H]], axis=1)
    g_in = jnp.transpose(g_in.reshape(D, N_CHIPS, -1), (1, 0, 2))

    gs = [g_in, g_out, g_kv, g_bin, g_bout]
    names = ["in_a", "out_a", "kv", "in_b", "out_b"]
    core = ac.astype(jnp.int32).reshape(1)
    sib = exchange_halves_to_sibling(gs)
    parts = [add_half(g, a, core, "rs_add_" + nm) for g, a, nm in zip(gs, sib, names)]
    landed = scatter_to_chips(parts)
    chip_i = chip.astype(jnp.int32).reshape(1)
    halves = [sum_partials(own, t, chip_i, "rs_sum_" + nm) for own, t, nm in zip(parts, landed, names)]
    theirs = join_halves(halves)
    g_halves = dict(zip(names, zip(halves, theirs)))

    dmod = jnp.concatenate([jnp.concatenate([dshift0, dscale0, dgate0], axis=1),
                            jnp.concatenate([dshift1, dscale1, dgate1], axis=1)], axis=0)
    small_parts = [jnp.concatenate([dg0, dg1], axis=0), jnp.concatenate([db0, db1], axis=0),
                   dbias_g.reshape(1, H), dalog_g.reshape(1, H), dD_g.reshape(1, H),
                   dconv_w, dconv_b, dnorm_g, loss_part.reshape(1, 1)]
    small_shapes = [p.shape for p in small_parts]
    packed = jnp.concatenate([_pack([dmod]), _pack(small_parts)], axis=0)
    n_mod_rows = _pack([dmod]).shape[0]
    gathered = allgather_small(packed, "allgather_small_grads").reshape(N_DEV, -1, 128)
    dmod8 = gathered[:, :n_mod_rows].reshape(N_DEV, -1)[:, :2 * 3 * D].reshape(N_DEV, DEPTH, 3 * D)
    summed = sum_leading(gathered, "sum_small")
    g_ada_b = summed[:n_mod_rows].reshape(-1)[:2 * 3 * D].reshape(DEPTH, 3 * D)
    (g_ln_g, g_ln_b, g_dt_bias, g_a_log, g_dsk, g_conv_w, g_conv_b, g_norm_g, loss_all) = _unpack(
        summed[n_mod_rows:].reshape(-1), small_shapes)
    loss = loss_all.reshape(())
    Cs = CONVD // N_CHIPS
    g_conv_w_s = lax.dynamic_slice_in_dim(g_conv_w, chip * Cs, Cs, axis=1)
    g_conv_b_s = lax.dynamic_slice_in_dim(g_conv_b, chip * Cs, Cs, axis=1)
    g_norm_g_s = lax.dynamic_slice_in_dim(g_norm_g, chip * (DI // N_CHIPS), DI // N_CHIPS, axis=1)
    dmod_s = jnp.transpose(lax.dynamic_slice_in_dim(dmod8, chip * Ws, Ws, axis=2), (1, 0, 2))
    g_ada_w = ada_wgrad(jnp.transpose(c8), dmod_s)

    def step2d(w, g, m, v, nm):
        shp = w.shape
        d_, m_, v_ = adamw(w.reshape(-1, shp[-1]), g.reshape(-1, shp[-1]), m.reshape(-1, shp[-1]),
                           v.reshape(-1, shp[-1]), "adamw_" + nm)
        return g.reshape(shp), d_.reshape(shp), m_.reshape(shp), v_.reshape(shp)

    def step_halves(w, m, v, nm):
        shp = w.shape
        mine, theirs_ = g_halves[nm]
        outs4 = adamw_halves(w.reshape(-1, shp[-1]), mine, theirs_, m.reshape(-1, shp[-1]), v.reshape(-1, shp[-1]),
                             core, "adamw_" + nm)
        return tuple(t.reshape(shp) for t in outs4)

    big = {
        "ada_w": step2d(ada_w, g_ada_w, m_ada_w, v_ada_w, "ada_w"),
        "a_in_w": step_halves(a_in_w, m_a_in_w, v_a_in_w, "in_a"),
        "a_out_w": step_halves(a_out_w, m_a_out_w, v_a_out_w, "out_a"),
        "kv_w": step_halves(kv_w, m_kv_w, v_kv_w, "kv"),
        "b_in_w": step_halves(b_in_w, m_b_in_w, v_b_in_w, "in_b"),
        "b_out_w": step_halves(b_out_w, m_b_out_w, v_b_out_w, "out_b"),
    }
    small_names = ["ada_b", "ln_g", "ln_b", "a_conv_w", "a_conv_b", "a_dt_bias", "a_A_log", "a_D", "a_norm_g"]
    small_w = [ada_b, ln_g, ln_b, a_conv_w, a_conv_b, a_dt_bias, a_A_log, a_D, a_norm_g]
    small_m = [m_ada_b, m_ln_g, m_ln_b, m_a_conv_w, m_a_conv_b, m_a_dt_bias, m_a_A_log, m_a_D, m_a_norm_g]
    small_v = [v_ada_b, v_ln_g, v_ln_b, v_a_conv_w, v_a_conv_b, v_a_dt_bias, v_a_A_log, v_a_D, v_a_norm_g]
    small_g = [g_ada_b, g_ln_g, g_ln_b, g_conv_w_s, g_conv_b_s, g_dt_bias, g_a_log, g_dsk, g_norm_g_s]
    shapes = [w.shape for w in small_w]
    small_g = [g.reshape(s) for g, s in zip(small_g, shapes)]
    d_p, m_p, v_p = adamw(_pack(small_w), _pack(small_g), _pack(small_m), _pack(small_v), "adamw_small")
    small = {}
    for nm, g, d_, m_, v_ in zip(small_names, small_g, _unpack(d_p.reshape(-1), shapes), _unpack(m_p.reshape(-1), shapes),
                                 _unpack(v_p.reshape(-1), shapes)):
        small[nm] = (g, d_, m_, v_)
    allw = {**big, **small}
    order = ["ada_w", "ada_b", "ln_g", "ln_b", "a_in_w", "a_conv_w", "a_conv_b", "a_dt_bias", "a_A_log", "a_D",
             "a_norm_g", "a_out_w", "kv_w", "b_in_w", "b_out_w"]
    outs = [loss, grad_x.reshape(x.shape)]
    for k in range(4):
        outs += [allw[n][k] for n in order]
    return tuple(outs)
```

```python
import functools

import jax
import jax.numpy as jnp
import numpy as np
from jax import lax
from jax.experimental import pallas as pl
from jax.experimental.pallas import tpu as pltpu

F32 = jnp.float32
BF16 = jnp.bfloat16
MESH = pl.DeviceIdType.MESH

DEPTH = 2
ALPHA = (2 * DEPTH) ** 0.25
LN_EPS = 1e-5
RMS_EPS = 1e-5
SSD_P = 64
SSD_N = 128
SSD_Q = 256
SSD_G = 8
CONV_W = 4
DIL_PATTERNS = ((128, 1), (512, 4), (2048, 16))
DIL_H = 8
DIL_E = 128
DIL_BLK = 128
ADAM_LR, ADAM_B1, ADAM_B2, ADAM_EPS, ADAM_WD, ADAM_STEP = 0.001, 0.9, 0.999, 1e-08, 0.01, 10

VMEM_LIMIT = 56 * 1024 * 1024
N_CHIPS = 4
N_DEV = 8


def _tile(dim, target, mult=128):
    if dim <= target:
        return dim
    t = (target // mult) * mult
    while t >= mult:
        if dim % t == 0:
            return t
        t -= mult
    return dim


def _cp(sem):
    return pltpu.CompilerParams(dimension_semantics=sem, vmem_limit_bytes=VMEM_LIMIT)


def _sigmoid(x):
    return 1.0 / (1.0 + jnp.exp(-x))


def _silu(x):
    return x * _sigmoid(x)


def _dsilu(x):
    s = _sigmoid(x)
    return s * (1.0 + x * (1.0 - s))


def _softplus(x):
    return jnp.maximum(x, 0.0) + jnp.log(1.0 + jnp.exp(-jnp.abs(x)))


def _mm_call(a, b, out_shape, grid, a_spec, b_spec, o_spec, acc_shape, dims, name):
    nk = grid[2]

    def prod(a_ref, b_ref):
        return lax.dot_general(a_ref[...].astype(BF16), b_ref[...].astype(BF16), (dims, ((), ())),
                               preferred_element_type=F32)

    def body_single(a_ref, b_ref, o_ref):
        o_ref[...] = prod(a_ref, b_ref).astype(o_ref.dtype)

    def body_multi(a_ref, b_ref, o_ref, acc_ref):
        k = pl.program_id(2)

        @pl.when(k == 0)
        def _():
            acc_ref[...] = prod(a_ref, b_ref)

        @pl.when(jnp.logical_and(k > 0, k < nk - 1))
        def _():
            acc_ref[...] += prod(a_ref, b_ref)

        @pl.when(k == nk - 1)
        def _():
            o_ref[...] = (acc_ref[...] + prod(a_ref, b_ref)).astype(o_ref.dtype)

    return pl.pallas_call(
        body_single if nk == 1 else body_multi, grid=grid, in_specs=[a_spec, b_spec], out_specs=o_spec,
        out_shape=out_shape, scratch_shapes=[] if nk == 1 else [pltpu.VMEM(acc_shape, F32)],
        compiler_params=_cp(("parallel", "parallel", "arbitrary")), name=name)(a, b)


def mm_nn(a, b, out_dtype, name, stack=None, tm=1024, tn=1024, tk=2048):
    M, K = a.shape
    if stack is None:
        N = b.shape[1]
        tn, tk = _tile(N, tn), _tile(K, tk)
        b_spec = pl.BlockSpec((tk, tn), lambda i, j, k: (k, j))
    elif stack == "col":
        S, _, Ns = b.shape
        N = S * Ns
        tn, tk = _tile(Ns, tn), _tile(K, tk)
        npb = Ns // tn
        b_spec = pl.BlockSpec((None, tk, tn), lambda i, j, k: (j // npb, k, j % npb))
    else:
        S, Ks, N = b.shape
        tn, tk = _tile(N, tn), _tile(Ks, tk)
        kpb = Ks // tk
        b_spec = pl.BlockSpec((None, tk, tn), lambda i, j, k: (k // kpb, k % kpb, j))
    tm = _tile(M, tm)
    return _mm_call(a, b, jax.ShapeDtypeStruct((M, N), out_dtype), (M // tm, N // tn, K // tk),
                    pl.BlockSpec((tm, tk), lambda i, j, k: (i, k)), b_spec,
                    pl.BlockSpec((tm, tn), lambda i, j, k: (i, j)), (tm, tn), ((1,), (0,)), name)


def mm_nt(a, b, out_dtype, name, stack=None, tm=1024, tn=1024, tk=2048):
    M, C = a.shape
    if stack is None:
        Kw = b.shape[0]
        tn, tk = _tile(Kw, tn), _tile(C, tk)
        b_spec = pl.BlockSpec((tn, tk), lambda i, j, k: (j, k))
    elif stack == "col":
        S, Kw, Cs = b.shape
        tn, tk = _tile(Kw, tn), _tile(Cs, tk)
        cpb = Cs // tk
        b_spec = pl.BlockSpec((None, tn, tk), lambda i, j, k: (k // cpb, j, k % cpb))
    else:
        S, Ks, _ = b.shape
        Kw = S * Ks
        tn, tk = _tile(Ks, tn), _tile(C, tk)
        jpb = Ks // tn
        b_spec = pl.BlockSpec((None, tn, tk), lambda i, j, k: (j // jpb, j % jpb, k))
    tm = _tile(M, tm)
    return _mm_call(a, b, jax.ShapeDtypeStruct((M, Kw), out_dtype), (M // tm, Kw // tn, C // tk),
                    pl.BlockSpec((tm, tk), lambda i, j, k: (i, k)), b_spec,
                    pl.BlockSpec((tm, tn), lambda i, j, k: (i, j)), (tm, tn), ((1,), (1,)), name)


def mm_tn(a, b, out_dtype, name, stack=None, n_stack=N_CHIPS, tm=1024, tn=1024, tk=2048):
    L, M = a.shape
    N = b.shape[1]
    tk = _tile(L, tk)
    if stack is None:
        tm, tn = _tile(M, tm), _tile(N, tn)
        o_spec = pl.BlockSpec((tm, tn), lambda i, j, k: (i, j))
        out_shape = (M, N)
    elif stack == "col":
        Ns = N // n_stack
        tm, tn = _tile(M, tm), _tile(Ns, tn)
        npb = Ns // tn
        o_spec = pl.BlockSpec((None, tm, tn), lambda i, j, k: (j // npb, i, j % npb))
        out_shape = (n_stack, M, Ns)
    else:
        Ms = M // n_stack
        tm, tn = _tile(Ms, tm), _tile(N, tn)
        mpb = Ms // tm
        o_spec = pl.BlockSpec((None, tm, tn), lambda i, j, k: (i // mpb, i % mpb, j))
        out_shape = (n_stack, Ms, N)
    return _mm_call(a, b, jax.ShapeDtypeStruct(out_shape, out_dtype), (M // tm, N // tn, L // tk),
                    pl.BlockSpec((tk, tm), lambda i, j, k: (k, i)), pl.BlockSpec((tk, tn), lambda i, j, k: (k, j)),
                    o_spec, (tm, tn), ((0,), (0,)), name)


def _row_specs(tr, widths):
    return [pl.BlockSpec((tr, w), lambda i: (i, 0)) for w in widths]


def _vec_spec(w):
    return pl.BlockSpec((1, w), lambda i: (0, 0))


def _acc_rows(ref, val, i):
    s = jnp.sum(val, axis=0, keepdims=True)

    @pl.when(i == 0)
    def _():
        ref[...] = s

    @pl.when(i > 0)
    def _():
        ref[...] += s


def modulate(x, scale, shift, name):
    L, D = x.shape
    tr = _tile(L, 512, 16)

    def body(x_ref, sc_ref, sh_ref, h_ref):
        h_ref[...] = (x_ref[...] * (1.0 + sc_ref[...]) + sh_ref[...]).astype(BF16)

    return pl.pallas_call(
        body, grid=(L // tr,), in_specs=_row_specs(tr, [D]) + [_vec_spec(D)] * 2, out_specs=_row_specs(tr, [D])[0],
        out_shape=jax.ShapeDtypeStruct((L, D), BF16), compiler_params=_cp(("parallel",)), name=name)(x, scale, shift)


def _ln_core(x, y, gate, g, b):
    u = ALPHA * x + (1.0 + gate) * y
    mu = jnp.mean(u, axis=-1, keepdims=True)
    d = u - mu
    var = jnp.mean(d * d, axis=-1, keepdims=True)
    rstd = lax.rsqrt(var + LN_EPS)
    xhat = d * rstd
    return xhat * g + b, xhat, rstd


def ln_mid(x, y, gate, g, b, scale, shift):
    L, D = x.shape
    tr = _tile(L, 256, 16)

    def body(x_ref, y_ref, gate_ref, g_ref, b_ref, sc_ref, sh_ref, x1_ref, x1b_ref, h_ref):
        x1, _, _ = _ln_core(x_ref[...], y_ref[...], gate_ref[...], g_ref[...], b_ref[...])
        x1_ref[...] = x1
        x1b_ref[...] = x1.astype(BF16)
        h_ref[...] = (x1 * (1.0 + sc_ref[...]) + sh_ref[...]).astype(BF16)

    return pl.pallas_call(
        body, grid=(L // tr,), in_specs=_row_specs(tr, [D, D]) + [_vec_spec(D)] * 5,
        out_specs=_row_specs(tr, [D, D, D]),
        out_shape=[jax.ShapeDtypeStruct((L, D), F32), jax.ShapeDtypeStruct((L, D), BF16),
                   jax.ShapeDtypeStruct((L, D), BF16)],
        compiler_params=_cp(("parallel",)), name="ln_mid")(x, y, gate, g, b, scale, shift)


def ln_final(x, y, gate, g, b, target):
    L, D = x.shape
    tr = _tile(L, 256, 16)

    def body(x_ref, y_ref, gate_ref, g_ref, b_ref, t_ref, dout_ref, sq_ref):
        out, _, _ = _ln_core(x_ref[...], y_ref[...], gate_ref[...], g_ref[...], b_ref[...])
        err = out - t_ref[...]
        dout_ref[...] = err * (1.0 / D)
        _acc_rows(sq_ref, err * err, pl.program_id(0))

    return pl.pallas_call(
        body, grid=(L // tr,), in_specs=_row_specs(tr, [D, D]) + [_vec_spec(D)] * 3 + _row_specs(tr, [D]),
        out_specs=[_row_specs(tr, [D])[0], _vec_spec(D)],
        out_shape=[jax.ShapeDtypeStruct((L, D), F32), jax.ShapeDtypeStruct((1, D), F32)],
        compiler_params=_cp(("arbitrary",)), name="ln_final")(x, y, gate, g, b, target)


def ln_bwd(dout, x, y, gate, g, name):
    L, D = x.shape
    tr = _tile(L, 256, 16)

    def body(do_ref, x_ref, y_ref, gate_ref, g_ref, dres_ref, dy_ref, dg_ref, db_ref, dgate_ref):
        i = pl.program_id(0)
        yv = y_ref[...]
        dout_v = do_ref[...]
        _, xhat, rstd = _ln_core(x_ref[...], yv, gate_ref[...], g_ref[...], 0.0)
        dxh = dout_v * g_ref[...]
        m1 = jnp.mean(dxh, axis=-1, keepdims=True)
        m2 = jnp.mean(dxh * xhat, axis=-1, keepdims=True)
        du = rstd * (dxh - m1 - xhat * m2)
        dres_ref[...] = ALPHA * du
        dy_ref[...] = ((1.0 + gate_ref[...]) * du).astype(BF16)
        _acc_rows(dg_ref, dout_v * xhat, i)
        _acc_rows(db_ref, dout_v, i)
        _acc_rows(dgate_ref, du * yv, i)

    return pl.pallas_call(
        body, grid=(L // tr,), in_specs=_row_specs(tr, [D, D, D]) + [_vec_spec(D)] * 2,
        out_specs=_row_specs(tr, [D, D]) + [_vec_spec(D)] * 3,
        out_shape=[jax.ShapeDtypeStruct((L, D), F32), jax.ShapeDtypeStruct((L, D), BF16)]
        + [jax.ShapeDtypeStruct((1, D), F32)] * 3,
        compiler_params=_cp(("arbitrary",)), name=name)(dout, x, y, gate, g)


def mod_bwd(dres, dh, dh2, xin, scale, name, through_mod):
    L, D = xin.shape
    tr = _tile(L, 256, 16)

    def body(dres_ref, dh_ref, dh2_ref, x_ref, sc_ref, dx_ref, dsc_ref, dsh_ref):
        i = pl.program_id(0)
        dh_v = dh_ref[...]
        tot = dres_ref[...]
        if through_mod:
            dh_v = dh_v + dh2_ref[...]
        else:
            tot = tot + dh2_ref[...]
        dx_ref[...] = tot + dh_v * (1.0 + sc_ref[...])
        _acc_rows(dsc_ref, dh_v * x_ref[...], i)
        _acc_rows(dsh_ref, dh_v, i)

    return pl.pallas_call(
        body, grid=(L // tr,), in_specs=_row_specs(tr, [D, D, D, D]) + [_vec_spec(D)],
        out_specs=_row_specs(tr, [D]) + [_vec_spec(D)] * 2,
        out_shape=[jax.ShapeDtypeStruct((L, D), F32)] + [jax.ShapeDtypeStruct((1, D), F32)] * 2,
        compiler_params=_cp(("arbitrary",)), name=name)(dres, dh, dh2, xin, scale)


CONV_HALO = 16


def _conv_rows(x_ref, i, tr, L):
    nblk = L // tr
    s = pl.multiple_of(i * tr, CONV_HALO)
    cur = x_ref[pl.ds(s, tr), :].astype(F32)
    sp = pl.multiple_of(jnp.maximum(i * tr - CONV_HALO, 0), CONV_HALO)
    sn = pl.multiple_of(jnp.minimum(i * tr + tr, L - CONV_HALO), CONV_HALO)
    prev = x_ref[pl.ds(sp, CONV_HALO), :].astype(F32) * (i > 0).astype(F32)
    nxt = x_ref[pl.ds(sn, CONV_HALO), :].astype(F32) * (i < nblk - 1).astype(F32)
    return jnp.concatenate([prev, cur, nxt], axis=0)


def _shift_rows(v, j):
    n = v.shape[0]
    return v if j % n == 0 else pltpu.roll(v, j % n, 0)


def _conv_eval(xe, w_ref, b_ref):
    c = b_ref[...] + w_ref[CONV_W - 1:CONV_W, :] * xe
    for k in range(CONV_W - 1):
        c = c + w_ref[k:k + 1, :] * _shift_rows(xe, CONV_W - 1 - k)
    return c


def conv_fwd(zx, col0, conv_w, conv_b):
    L = zx.shape[0]
    C = conv_w.shape[1]
    tc = _tile(C, 512)
    tr = _tile(L, 512, CONV_HALO)
    off = col0 // tc

    def body(x_ref, w_ref, b_ref, o_ref):
        i = pl.program_id(1)
        xe = _conv_rows(x_ref, i, tr, L)
        c = _conv_eval(xe, w_ref, b_ref)[CONV_HALO:CONV_HALO + tr]
        o_ref[...] = _silu(c).astype(BF16)

    return pl.pallas_call(
        body, grid=(C // tc, L // tr),
        in_specs=[pl.BlockSpec((L, tc), lambda j, i: (0, off + j)), pl.BlockSpec((CONV_W, tc), lambda j, i: (0, j)),
                  pl.BlockSpec((1, tc), lambda j, i: (0, j))],
        out_specs=pl.BlockSpec((tr, tc), lambda j, i: (i, j)),
        out_shape=jax.ShapeDtypeStruct((L, C), BF16), compiler_params=_cp(("parallel", "arbitrary")),
        name="conv_fwd")(zx, conv_w, conv_b)


def conv_bwd(zx, col0, conv_w, conv_b, dxbc):
    L = zx.shape[0]
    C = conv_w.shape[1]
    tc = _tile(C, 512)
    tr = _tile(L, 512, CONV_HALO)
    off = col0 // tc
    H = CONV_HALO

    def body(x_ref, g_ref, w_ref, b_ref, dx_ref, dw_ref, db_ref):
        i = pl.program_id(1)
        xe = _conv_rows(x_ref, i, tr, L)
        ge = _conv_rows(g_ref, i, tr, L)
        dc = ge * _dsilu(_conv_eval(xe, w_ref, b_ref))
        dx = w_ref[CONV_W - 1:CONV_W, :] * dc
        for k in range(CONV_W - 1):
            dx = dx + w_ref[k:k + 1, :] * _shift_rows(dc, -(CONV_W - 1 - k))
        dx_ref[...] = dx[H:H + tr].astype(BF16)
        dcc = dc[H:H + tr]
        rows = [jnp.sum(dcc * _shift_rows(xe, CONV_W - 1 - k)[H:H + tr], axis=0, keepdims=True) for k in range(CONV_W)]
        dwv = jnp.concatenate(rows + [jnp.zeros((8 - CONV_W, tc), F32)], axis=0)
        dbv = jnp.sum(dcc, axis=0, keepdims=True)

        @pl.when(i == 0)
        def _():
            dw_ref[...] = dwv
            db_ref[...] = dbv

        @pl.when(i > 0)
        def _():
            dw_ref[...] += dwv
            db_ref[...] += dbv

    dx, dw, db = pl.pallas_call(
        body, grid=(C // tc, L // tr),
        in_specs=[pl.BlockSpec((L, tc), lambda j, i: (0, off + j)), pl.BlockSpec((L, tc), lambda j, i: (0, j)),
                  pl.BlockSpec((CONV_W, tc), lambda j, i: (0, j)), pl.BlockSpec((1, tc), lambda j, i: (0, j))],
        out_specs=[pl.BlockSpec((tr, tc), lambda j, i: (i, j)), pl.BlockSpec((8, tc), lambda j, i: (0, j)),
                   pl.BlockSpec((1, tc), lambda j, i: (0, j))],
        out_shape=[jax.ShapeDtypeStruct((L, C), BF16), jax.ShapeDtypeStruct((8, C), F32),
                   jax.ShapeDtypeStruct((1, C), F32)],
        compiler_params=_cp(("parallel", "arbitrary")), name="conv_bwd")(zx, dxbc, conv_w, conv_b)
    return dx, dw[:CONV_W], db


_NN = (((1,), (0,)), ((), ()))


def _pieces(x, n):
    out, r = [], x
    for _ in range(n):
        p = r.astype(BF16)
        out.append(p)
        r = r - p.astype(F32)
    return out


def _dot01(a, b01, n, dims=_NN):
    b = b01.astype(BF16)
    return functools.reduce(lambda u, v: u + v,
                            [lax.dot_general(p, b, dims, preferred_element_type=F32) for p in _pieces(a, n)])


def _dot01_left(a01, b, n, dims=_NN):
    a = a01.astype(BF16)
    return functools.reduce(lambda u, v: u + v,
                            [lax.dot_general(a, p, dims, preferred_element_type=F32) for p in _pieces(b, n)])


def _ssd_common(dtp_ref, dtpT_ref, bias_ref, biasT_ref, alog_ref, alogT_ref, b_ref, c_ref):
    Q = SSD_Q
    dt = _softplus(dtp_ref[...] + bias_ref[...])
    A = -jnp.exp(alog_ref[...])
    row = lax.broadcasted_iota(jnp.int32, (Q, Q), 0)
    col = lax.broadcasted_iota(jnp.int32, (Q, Q), 1)
    causal = row >= col
    tril = causal.astype(F32)
    Kh = dt.shape[1]
    acum = _dot01_left(tril, dt * A, 3)
    eye = (lax.broadcasted_iota(jnp.int32, (Kh, Kh), 0) == lax.broadcasted_iota(jnp.int32, (Kh, Kh), 1)).astype(F32)
    acumT = _dot01_left(eye, acum, 3, dims=(((1,), (1,)), ((), ())))
    Bm = b_ref[...]
    Cm = c_ref[...]
    cb = lax.dot_general(Cm, Bm, (((1,), (1,)), ((), ())), preferred_element_type=F32)
    return dt, A, causal, row, col, acum, acumT, Bm, Cm, cb


def _ssd_in_specs(Q, GP, N, Kh, DI, cmap):
    nb0 = DI // N
    vec = pl.BlockSpec((None, 1, Kh), lambda g, c: (g, 0, 0))
    vecT = pl.BlockSpec((None, Kh, 1), lambda g, c: (g, 0, 0))
    return [pl.BlockSpec((Q, GP), lambda g, c: (cmap(c), g)),
            pl.BlockSpec((Q, N), lambda g, c: (cmap(c), nb0 + g)),
            pl.BlockSpec((Q, N), lambda g, c: (cmap(c), nb0 + SSD_G + g)),
            pl.BlockSpec((None, Q, Kh), lambda g, c: (g, cmap(c), 0)),
            pl.BlockSpec((None, Kh, Q), lambda g, c: (g, 0, cmap(c))),
            vec, vecT, vec, vecT, vec, vecT]


def _hi(a, b01):
    return _dot01(a, b01, 2)


def _ssd_heads(dskT_ref, acum, acumT, dt, Kh):
    Q, P, N = SSD_Q, SSD_P, SSD_N
    GP = Kh * P
    sh_p = P.bit_length() - 1
    seg = lambda shape, dim: lax.shift_right_logical(lax.broadcasted_iota(jnp.int32, shape, dim), sh_p)
    E = (seg((Kh, GP), 1) == lax.broadcasted_iota(jnp.int32, (Kh, GP), 0)).astype(F32)
    ET = (seg((GP, Kh), 0) == lax.broadcasted_iota(jnp.int32, (GP, Kh), 1)).astype(F32)
    a_last = acum[Q - 1:Q, :]
    tail = jnp.exp(a_last - acum)
    eLT = jnp.exp(acumT[:, Q - 1:Q])
    rowseg = seg((GP, N), 0)
    eL_b = jnp.zeros((GP, N), F32)
    for k in range(Kh):
        eL_b = jnp.where(rowseg == k, eLT[k:k + 1, :], eL_b)
    return dict(
        E=E, ET=ET, a_last=a_last, tail=tail, eL_b=eL_b,
        dt_all=_hi(dt, E), ea_all=_hi(jnp.exp(acum), E), tail_all=_hi(tail, E),
        dsk_all=jnp.sum(E * dskT_ref[...], axis=0, keepdims=True))


def _head_chunks(GP):
    CW = min(GP, 128)
    return CW, CW // SSD_P, GP // CW


def _head_mask(Q, CW, kk):
    lane = lax.broadcasted_iota(jnp.int32, (Q, CW), 1)
    return jnp.logical_and(lane >= kk * SSD_P, lane < (kk + 1) * SSD_P)


def ssd_fwd(xbc, dtp_g, dtp_gT, bias_g, bias_gT, alog_g, alog_gT, dsk_g, dsk_gT, DI):
    L = xbc.shape[0]
    Q, P, N, G = SSD_Q, SSD_P, SSD_N, SSD_G
    GP = DI // G
    Kh = GP // P
    nc = L // Q

    CW, hpc, nch = _head_chunks(GP)
    nt = (((1,), (1,)), ((), ()))
    tn = (((0,), (0,)), ((), ()))

    def body(xs_ref, b_ref, c_ref, dtp_ref, dtpT_ref, bias_ref, biasT_ref, alog_ref, alogT_ref, dsk_ref, dskT_ref,
             y_ref, st_ref, state):
        @pl.when(pl.program_id(1) == 0)
        def _():
            state[...] = jnp.zeros(state.shape, F32)

        st_ref[...] = state[...]
        dt, A, causal, row, col, acum, acumT, Bm, Cm, cb = _ssd_common(
            dtp_ref, dtpT_ref, bias_ref, biasT_ref, alog_ref, alogT_ref, b_ref, c_ref)
        hd = _ssd_heads(dskT_ref, acum, acumT, dt, Kh)
        xs = xs_ref[...].astype(F32)
        xdt_all = xs * hd["dt_all"]
        S_all = state[...]
        y_all = (lax.dot_general(Cm, S_all.astype(BF16), nt, preferred_element_type=F32) * hd["ea_all"]
                 + xs * hd["dsk_all"])
        state[...] = S_all * hd["eL_b"] + lax.dot_general(
            (xdt_all * hd["tail_all"]).astype(BF16), Bm, tn, preferred_element_type=F32)
        for ch in range(nch):
            cs = slice(ch * CW, (ch + 1) * CW)
            xc = xdt_all[:, cs]
            acc = y_all[:, cs]
            for kk in range(hpc):
                k = ch * hpc + kk
                decay = jnp.exp(jnp.where(causal, acum[:, k:k + 1] - acumT[k:k + 1, :], -jnp.inf))
                xk = xc if hpc == 1 else jnp.where(_head_mask(Q, CW, kk), xc, 0.0)
                acc = acc + jnp.dot((cb * decay).astype(BF16), xk.astype(BF16), preferred_element_type=F32)
            y_ref[:, cs] = acc.astype(BF16)

    return pl.pallas_call(
        body, grid=(G, nc), in_specs=_ssd_in_specs(Q, GP, N, Kh, DI, lambda c: c),
        out_specs=[pl.BlockSpec((Q, GP), lambda g, c: (c, g)),
                   pl.BlockSpec((None, None, GP, N), lambda g, c: (c, g, 0, 0))],
        out_shape=[jax.ShapeDtypeStruct((L, DI), BF16), jax.ShapeDtypeStruct((nc, G, GP, N), F32)],
        scratch_shapes=[pltpu.VMEM((GP, N), F32)], compiler_params=_cp(("parallel", "arbitrary")),
        name="ssd_fwd")(xbc, xbc, xbc, dtp_g, dtp_gT, bias_g, bias_gT, alog_g, alog_gT, dsk_g, dsk_gT)


def ssd_bwd(xbc, dtp_g, dtp_gT, bias_g, bias_gT, alog_g, alog_gT, dsk_g, dsk_gT, states, dy, DI):
    L = xbc.shape[0]
    Q, P, N, G = SSD_Q, SSD_P, SSD_N, SSD_G
    GP = DI // G
    Kh = GP // P
    nc = L // Q
    rev = lambda c: nc - 1 - c

    CW, hpc, nch = _head_chunks(GP)

    def body(xs_ref, b_ref, c_ref, dtp_ref, dtpT_ref, bias_ref, biasT_ref, alog_ref, alogT_ref, dsk_ref, dskT_ref,
             st_ref, dy_ref, dxs_ref, dB_ref, dC_ref, ddtp_ref, dbias_ref, dalog_ref, dD_ref, dstate):
        ci = pl.program_id(1)

        @pl.when(ci == 0)
        def _():
            dstate[...] = jnp.zeros(dstate.shape, F32)

        dt, A, causal, row, col, acum, acumT, Bm, Cm, cb = _ssd_common(
            dtp_ref, dtpT_ref, bias_ref, biasT_ref, alog_ref, alogT_ref, b_ref, c_ref)
        tn = (((0,), (0,)), ((), ()))
        nt = (((1,), (1,)), ((), ()))
        hd = _ssd_heads(dskT_ref, acum, acumT, dt, Kh)
        ET, tail = hd["ET"], hd["tail"]
        cbT = lax.dot_general(Bm, Cm, nt, preferred_element_type=F32)
        causalT = row <= col
        xs = xs_ref[...].astype(F32)
        xdt_all = xs * hd["dt_all"]
        dyb = dy_ref[...]
        dy_all = dyb.astype(F32)
        S_all = st_ref[...]
        S_b = S_all.astype(BF16)
        dS_all = dstate[...]
        dS_b = dS_all.astype(BF16)
        CS_all = lax.dot_general(Cm, S_b, nt, preferred_element_type=F32)
        dyE_b = (dy_all * hd["ea_all"]).astype(BF16)
        dC_acc = jnp.dot(dyE_b, S_b, preferred_element_type=F32)
        dS_y = lax.dot_general(dyE_b, Cm, tn, preferred_element_type=F32)
        BdS_all = lax.dot_general(Bm, dS_b, nt, preferred_element_type=F32)
        dB_acc = jnp.dot((xdt_all * hd["tail_all"]).astype(BF16), dS_b, preferred_element_type=F32)
        dtail = _hi(xdt_all * BdS_all, ET)
        da_cols = _hi(dy_all * CS_all * hd["ea_all"], ET) - dtail * tail
        dss = _dot01_left(jnp.ones((8, N), F32), _dot01_left(hd["E"], dS_all * S_all, 2), 2, dims=nt)
        da_last = dss[0:1] * jnp.exp(hd["a_last"]) + jnp.sum(dtail * tail, axis=0, keepdims=True)
        rowi = lax.broadcasted_iota(jnp.int32, (Q, Kh), 0)
        da_cols = da_cols + jnp.where(rowi == Q - 1, da_last, 0.0)
        dstate[...] = hd["eL_b"] * dS_all + dS_y
        sum_mg = jnp.zeros((Q, Q), F32)
        sum_mgt = jnp.zeros((Q, Q), F32)
        dacc = jnp.zeros((Q, 128), F32)
        ddt_x = jnp.zeros((Q, Kh), F32)
        lane128 = lax.broadcasted_iota(jnp.int32, (Q, 128), 1)
        for ch in range(nch):
            cs = slice(ch * CW, (ch + 1) * CW)
            dyc = dyb[:, cs]
            xc_b = xdt_all[:, cs].astype(BF16)
            acc = hd["tail_all"][:, cs] * BdS_all[:, cs]
            for kk in range(hpc):
                k = ch * hpc + kk
                a_b = jnp.broadcast_to(acum[:, k:k + 1], (Q, Q))
                a_r = acumT[k:k + 1, :]
                decay = jnp.exp(jnp.where(causal, a_b - a_r, -jnp.inf))
                decayT = jnp.exp(jnp.where(causalT, a_r - a_b, -jnp.inf))
                dyk = dyc if hpc == 1 else jnp.where(_head_mask(Q, CW, kk), dyc, jnp.zeros_like(dyc))
                mg = decay * lax.dot_general(dyk, xc_b, nt, preferred_element_type=F32)
                mgt = decayT * lax.dot_general(xc_b, dyk, nt, preferred_element_type=F32)
                sum_mg = sum_mg + mg
                sum_mgt = sum_mgt + mgt
                onek = jnp.where(lane128 == k, 1.0, 0.0).astype(BF16)
                dk = mg * cb - mgt * cbT
                dk_hi = dk.astype(BF16)
                dk_lo = (dk - dk_hi.astype(F32)).astype(BF16)
                dacc = dacc + (jnp.dot(dk_hi, onek, preferred_element_type=F32)
                               + jnp.dot(dk_lo, onek, preferred_element_type=F32))
                acc = acc + jnp.dot((decayT * cbT).astype(BF16), dyk, preferred_element_type=F32)
            dxs_ref[:, cs] = (acc * hd["dt_all"][:, cs] + dy_all[:, cs] * hd["dsk_all"][:, cs]).astype(BF16)
            ddt_x = ddt_x + _hi(acc * xs[:, cs], ET[cs, :])
        da_cols = da_cols + dacc[:, :Kh]
        dD_row = jnp.sum(_hi(dy_all * xs, ET), axis=0, keepdims=True)
        dB_ref[...] = (dB_acc + jnp.dot(sum_mgt.astype(BF16), Cm, preferred_element_type=F32)).astype(BF16)
        dC_ref[...] = (dC_acc + jnp.dot(sum_mg.astype(BF16), Bm, preferred_element_type=F32)).astype(BF16)
        triu = (row <= col).astype(F32)
        ddtA = _dot01_left(triu, da_cols, 3)
        ddt = ddt_x + ddtA * A
        dpre = ddt * _sigmoid(dtp_ref[...] + bias_ref[...])
        ddtp_ref[...] = dpre
        dbias_v = jnp.sum(dpre, axis=0, keepdims=True)
        dalog_v = jnp.sum(ddtA * dt, axis=0, keepdims=True) * A

        @pl.when(ci == 0)
        def _():
            dbias_ref[...] = dbias_v
            dalog_ref[...] = dalog_v
            dD_ref[...] = dD_row

        @pl.when(ci > 0)
        def _():
            dbias_ref[...] += dbias_v
            dalog_ref[...] += dalog_v
            dD_ref[...] += dD_row

    vec_o = pl.BlockSpec((None, 1, Kh), lambda g, c: (g, 0, 0))
    return pl.pallas_call(
        body, grid=(G, nc),
        in_specs=_ssd_in_specs(Q, GP, N, Kh, DI, rev)
        + [pl.BlockSpec((None, None, GP, N), lambda g, c: (rev(c), g, 0, 0)),
           pl.BlockSpec((Q, GP), lambda g, c: (rev(c), g))],
        out_specs=[pl.BlockSpec((Q, GP), lambda g, c: (rev(c), g)), pl.BlockSpec((Q, N), lambda g, c: (rev(c), g)),
                   pl.BlockSpec((Q, N), lambda g, c: (rev(c), g)),
                   pl.BlockSpec((None, Q, Kh), lambda g, c: (g, rev(c), 0)), vec_o, vec_o, vec_o],
        out_shape=[jax.ShapeDtypeStruct((L, DI), BF16), jax.ShapeDtypeStruct((L, G * N), BF16),
                   jax.ShapeDtypeStruct((L, G * N), BF16), jax.ShapeDtypeStruct((G, L, Kh), F32)]
        + [jax.ShapeDtypeStruct((G, 1, Kh), F32)] * 3,
        scratch_shapes=[pltpu.VMEM((GP, N), F32)], compiler_params=_cp(("parallel", "arbitrary")),
        name="ssd_bwd")(xbc, xbc, xbc, dtp_g, dtp_gT, bias_g, bias_gT, alog_g, alog_gT, dsk_g, dsk_gT, states, dy)


def _rms_groups(y2, ng_ref, DI):
    S = DI // SSD_G
    for g in range(SSD_G):
        gs = slice(g * S, (g + 1) * S)
        seg = y2[:, gs]
        r = lax.rsqrt(jnp.mean(seg * seg, axis=-1, keepdims=True) + RMS_EPS)
        yield gs, seg * r, r, ng_ref[:, gs]


def rms_gate_fwd(y, zx, norm_g):
    L, DI = y.shape
    tr = _tile(L, 256, 16)

    def body(y_ref, z_ref, ng_ref, o_ref):
        y2 = y_ref[...].astype(F32) * _silu(z_ref[...].astype(F32))
        for gs, yh, _, ng in _rms_groups(y2, ng_ref, DI):
            o_ref[:, gs] = (yh * ng).astype(BF16)

    return pl.pallas_call(
        body, grid=(L // tr,), in_specs=_row_specs(tr, [DI, DI]) + [_vec_spec(DI)], out_specs=_row_specs(tr, [DI])[0],
        out_shape=jax.ShapeDtypeStruct((L, DI), BF16), compiler_params=_cp(("parallel",)),
        name="rms_gate_fwd")(y, zx, norm_g)


def rms_gate_bwd(dyn, y, zx, norm_g):
    L, DI = y.shape
    tr = _tile(L, 256, 16)

    def body(dyn_ref, y_ref, z_ref, ng_ref, dy_ref, dz_ref, dng_ref):
        i = pl.program_id(0)
        yv = y_ref[...].astype(F32)
        zv = z_ref[...].astype(F32)
        sz = _silu(zv)
        dsz = _dsilu(zv)
        dynv = dyn_ref[...].astype(F32)
        for gs, yh, r, ng in _rms_groups(yv * sz, ng_ref, DI):
            dyh = dynv[:, gs] * ng
            dy2 = r * (dyh - yh * jnp.mean(dyh * yh, axis=-1, keepdims=True))
            dy_ref[:, gs] = (dy2 * sz[:, gs]).astype(BF16)
            dz_ref[:, gs] = (dy2 * yv[:, gs] * dsz[:, gs]).astype(BF16)
            s = jnp.sum(dynv[:, gs] * yh, axis=0, keepdims=True)

            @pl.when(i == 0)
            def _():
                dng_ref[:, gs] = s

            @pl.when(i > 0)
            def _():
                dng_ref[:, gs] += s

    return pl.pallas_call(
        body, grid=(L // tr,), in_specs=_row_specs(tr, [DI, DI, DI]) + [_vec_spec(DI)],
        out_specs=_row_specs(tr, [DI, DI]) + [_vec_spec(DI)],
        out_shape=[jax.ShapeDtypeStruct((L, DI), BF16)] * 2 + [jax.ShapeDtypeStruct((1, DI), F32)],
        compiler_params=_cp(("arbitrary",)), name="rms_gate_bwd")(dyn, y, zx, norm_g)


def _alibi_slope(gi, h):
    n = len(DIL_PATTERNS) * DIL_H
    return float(2.0 ** (-8.0 * (gi * DIL_H + h + 1) / n))


def _attn_masks():
    qi = lax.broadcasted_iota(jnp.int32, (DIL_BLK, DIL_BLK), 0)
    kj = lax.broadcasted_iota(jnp.int32, (DIL_BLK, DIL_BLK), 1)
    dcur = (qi - kj).astype(F32)
    return dcur, qi >= kj, dcur + float(DIL_BLK), kj >= qi


def _dil_cols(arr, col0, d):
    HW = DIL_H * DIL_E
    if d == 1:
        return arr, arr.shape[1] // HW, col0 // HW
    return arr[:, col0:col0 + HW].reshape(arr.shape[0] // d, d * HW), 1, 0


def attn_fwd(qz, kv, gi):
    window, d = DIL_PATTERNS[gi]
    assert window // d == DIL_BLK
    L, QZ = qz.shape
    KV = kv.shape[1]
    HW = DIL_H * DIL_E
    M = L // d
    nb = M // DIL_BLK
    nq, nkv = QZ // HW, KV // HW
    scale = DIL_E ** -0.5
    nt = (((1,), (1,)), ((), ()))

    def body(q_ref, kp_ref, kc_ref, vp_ref, vc_ref, o_ref, lse_ref):
        n = pl.program_id(1)
        dcur, vcur, dprev, vprev0 = _attn_masks()
        vprev = jnp.logical_and(vprev0, n > 0)
        lane = lax.broadcasted_iota(jnp.int32, (DIL_BLK, 128), 1)
        lse_acc = jnp.zeros((DIL_BLK, 128), F32)
        for h in range(DIL_H):
            hs = slice(h * DIL_E, (h + 1) * DIL_E)
            sl = _alibi_slope(gi, h) * d
            q = q_ref[:, hs]
            s_c = lax.dot_general(q, kc_ref[:, hs], nt, preferred_element_type=F32) * scale - sl * dcur
            s_p = lax.dot_general(q, kp_ref[:, hs], nt, preferred_element_type=F32) * scale - sl * dprev
            s_c = jnp.where(vcur, s_c, -jnp.inf)
            s_p = jnp.where(vprev, s_p, -jnp.inf)
            m = jnp.maximum(jnp.max(s_c, axis=-1, keepdims=True), jnp.max(s_p, axis=-1, keepdims=True))
            p_c = jnp.exp(s_c - m)
            p_p = jnp.exp(s_p - m)
            den = jnp.sum(p_c, axis=-1, keepdims=True) + jnp.sum(p_p, axis=-1, keepdims=True)
            o = (jnp.dot(p_c.astype(BF16), vc_ref[:, hs], preferred_element_type=F32)
                 + jnp.dot(p_p.astype(BF16), vp_ref[:, hs], preferred_element_type=F32)) / den
            o_ref[:, hs] = o.astype(BF16)
            lse_acc = jnp.where(lane == h, m + jnp.log(den), lse_acc)
        lse_ref[...] = lse_acc

    blk = (DIL_BLK, HW)
    prev = lambda n: jnp.maximum(n - 1, 0)
    qv, qn, qo = _dil_cols(qz, gi * HW, d)
    kv_, kn, ko = _dil_cols(kv, gi * HW, d)
    vv, vn, vo = _dil_cols(kv, (nkv // 2 + gi) * HW, d)
    o, lse = pl.pallas_call(
        body, grid=(d, nb),
        in_specs=[pl.BlockSpec(blk, lambda r, n: (n, r * qn + qo)),
                  pl.BlockSpec(blk, lambda r, n: (prev(n), r * kn + ko)),
                  pl.BlockSpec(blk, lambda r, n: (n, r * kn + ko)),
                  pl.BlockSpec(blk, lambda r, n: (prev(n), r * vn + vo)),
                  pl.BlockSpec(blk, lambda r, n: (n, r * vn + vo))],
        out_specs=[pl.BlockSpec(blk, lambda r, n: (n, r)), pl.BlockSpec((DIL_BLK, 128), lambda r, n: (n, r))],
        out_shape=[jax.ShapeDtypeStruct((M, d * HW), BF16), jax.ShapeDtypeStruct((M, d * 128), F32)],
        compiler_params=_cp(("parallel", "parallel")), name=f"attn_fwd_{gi}")(qv, kv_, kv_, vv, vv)
    return o.reshape(L, HW), lse.reshape(L, 128)


def attn_bwd(qz, kv, do, lse, dpr, gi):
    window, d = DIL_PATTERNS[gi]
    L, QZ = qz.shape
    KV = kv.shape[1]
    HW = DIL_H * DIL_E
    M = L // d
    nb = M // DIL_BLK
    nq, nkv = QZ // HW, KV // HW
    scale = DIL_E ** -0.5
    nt = (((1,), (1,)), ((), ()))
    tn = (((0,), (0,)), ((), ()))

    def body(q0_ref, q1_ref, k_ref, v_ref, do0_ref, do1_ref, l0_ref, l1_ref, r0_ref, r1_ref,
             dq_ref, dk_ref, dv_ref, carry):
        n = pl.program_id(1)

        @pl.when(n == 0)
        def _():
            carry[...] = jnp.zeros(carry.shape, F32)

        dcur, vcur, dprev, vprev0 = _attn_masks()
        vprev = jnp.logical_and(vprev0, n < nb - 1)
        for h in range(DIL_H):
            hs = slice(h * DIL_E, (h + 1) * DIL_E)
            sl = _alibi_slope(gi, h) * d
            kh = k_ref[:, hs]
            vh = v_ref[:, hs]
            q0, q1 = q0_ref[:, hs], q1_ref[:, hs]
            do0, do1 = do0_ref[:, hs], do1_ref[:, hs]
            s0 = lax.dot_general(q0, kh, nt, preferred_element_type=F32) * scale - sl * dcur
            p0 = jnp.exp(jnp.where(vcur, s0 - l0_ref[:, h:h + 1], -jnp.inf))
            ds0 = p0 * (lax.dot_general(do0, vh, nt, preferred_element_type=F32) - r0_ref[:, h:h + 1])
            s1 = lax.dot_general(q1, kh, nt, preferred_element_type=F32) * scale - sl * dprev
            p1 = jnp.exp(jnp.where(vprev, s1 - l1_ref[:, h:h + 1], -jnp.inf))
            ds1 = p1 * (lax.dot_general(do1, vh, nt, preferred_element_type=F32) - r1_ref[:, h:h + 1])
            ds0_b = (ds0 * scale).astype(BF16)
            ds1_b = (ds1 * scale).astype(BF16)
            dv = (lax.dot_general(p0.astype(BF16), do0, tn, preferred_element_type=F32)
                  + lax.dot_general(p1.astype(BF16), do1, tn, preferred_element_type=F32))
            dk = (lax.dot_general(ds0_b, q0, tn, preferred_element_type=F32)
                  + lax.dot_general(ds1_b, q1, tn, preferred_element_type=F32))
            dv_ref[:, hs] = dv.astype(BF16)
            dk_ref[:, hs] = dk.astype(BF16)
            dq_ref[:, hs] = (carry[:, hs] + jnp.dot(ds0_b, kh, preferred_element_type=F32)).astype(BF16)
            carry[:, hs] = jnp.dot(ds1_b, kh, preferred_element_type=F32)

    blk = (DIL_BLK, HW)
    sblk = (DIL_BLK, 128)
    nxt = lambda n: jnp.minimum(n + 1, nb - 1)
    qv, qn, qo = _dil_cols(qz, gi * HW, d)
    kv_, kn, ko = _dil_cols(kv, gi * HW, d)
    vv, vn, vo = _dil_cols(kv, (nkv // 2 + gi) * HW, d)
    dov = do.reshape(M, d * HW)
    lv = lse.reshape(M, d * 128)
    rv = dpr.reshape(M, d * 128)
    outs = pl.pallas_call(
        body, grid=(d, nb),
        in_specs=[pl.BlockSpec(blk, lambda r, n: (n, r * qn + qo)), pl.BlockSpec(blk, lambda r, n: (nxt(n), r * qn + qo)),
                  pl.BlockSpec(blk, lambda r, n: (n, r * kn + ko)),
                  pl.BlockSpec(blk, lambda r, n: (n, r * vn + vo)),
                  pl.BlockSpec(blk, lambda r, n: (n, r)), pl.BlockSpec(blk, lambda r, n: (nxt(n), r)),
                  pl.BlockSpec(sblk, lambda r, n: (n, r)), pl.BlockSpec(sblk, lambda r, n: (nxt(n), r)),
                  pl.BlockSpec(sblk, lambda r, n: (n, r)), pl.BlockSpec(sblk, lambda r, n: (nxt(n), r))],
        out_specs=[pl.BlockSpec(blk, lambda r, n: (n, r))] * 3,
        out_shape=[jax.ShapeDtypeStruct((M, d * HW), BF16)] * 3,
        scratch_shapes=[pltpu.VMEM(blk, F32)], compiler_params=_cp(("parallel", "arbitrary")),
        name=f"attn_bwd_{gi}")(qv, qv, kv_, vv, dov, dov, lv, lv, rv, rv)
    return [t.reshape(L, HW) for t in outs]


def _merge_weights(l_refs, h):
    ls = [r[:, h:h + 1] for r in l_refs]
    mx = functools.reduce(jnp.maximum, ls)
    es = [jnp.exp(l - mx) for l in ls]
    den = functools.reduce(lambda a, b: a + b, es)
    return [e / den for e in es]


def merge_fwd(os_, lses, qz):
    L, HW = os_[0].shape
    tr = _tile(L, 256, 16)
    ng = len(os_)
    zblk = qz.shape[1] // HW - 1

    def body(*refs):
        o_refs, l_refs, z_ref, out_ref = refs[:ng], refs[ng:2 * ng], refs[2 * ng], refs[2 * ng + 1]
        for h in range(DIL_H):
            hs = slice(h * DIL_E, (h + 1) * DIL_E)
            ws = _merge_weights(l_refs, h)
            om = functools.reduce(lambda a, b: a + b, [w * o[:, hs].astype(F32) for w, o in zip(ws, o_refs)])
            out_ref[:, hs] = (om * _silu(z_ref[:, hs].astype(F32))).astype(BF16)

    return pl.pallas_call(
        body, grid=(L // tr,),
        in_specs=_row_specs(tr, [HW] * ng + [128] * ng) + [pl.BlockSpec((tr, HW), lambda i: (i, zblk))],
        out_specs=_row_specs(tr, [HW])[0], out_shape=jax.ShapeDtypeStruct((L, HW), BF16),
        compiler_params=_cp(("parallel",)), name="merge_fwd")(*os_, *lses, qz)


def merge_bwd(dgated, os_, lses, qz):
    L, HW = os_[0].shape
    tr = _tile(L, 256, 16)
    ng = len(os_)
    zblk = qz.shape[1] // HW - 1

    def body(*refs):
        dg_ref = refs[0]
        o_refs, l_refs, z_ref = refs[1:1 + ng], refs[1 + ng:1 + 2 * ng], refs[1 + 2 * ng]
        outs = refs[2 + 2 * ng:]
        do_refs, dpr_refs, dz_ref = outs[:ng], outs[ng:2 * ng], outs[2 * ng]
        lane = lax.broadcasted_iota(jnp.int32, (tr, 128), 1)
        accs = [jnp.zeros((tr, 128), F32) for _ in range(ng)]
        for h in range(DIL_H):
            hs = slice(h * DIL_E, (h + 1) * DIL_E)
            ws = _merge_weights(l_refs, h)
            ov = [o[:, hs].astype(F32) for o in o_refs]
            om = functools.reduce(lambda a, b: a + b, [w * o for w, o in zip(ws, ov)])
            zv = z_ref[:, hs].astype(F32)
            dgv = dg_ref[:, hs].astype(F32)
            dom = dgv * _silu(zv)
            dz_ref[:, hs] = (dgv * om * _dsilu(zv)).astype(BF16)
            dws = [jnp.sum(dom * o, axis=-1, keepdims=True) for o in ov]
            dwbar = functools.reduce(lambda a, b: a + b, [w * dw for w, dw in zip(ws, dws)])
            for g in range(ng):
                do_refs[g][:, hs] = (ws[g] * dom).astype(BF16)
                accs[g] = jnp.where(lane == h, ws[g] * dwbar, accs[g])
        for g in range(ng):
            dpr_refs[g][...] = accs[g]

    outs = pl.pallas_call(
        body, grid=(L // tr,),
        in_specs=_row_specs(tr, [HW] * (1 + ng) + [128] * ng) + [pl.BlockSpec((tr, HW), lambda i: (i, zblk))],
        out_specs=_row_specs(tr, [HW] * ng + [128] * ng + [HW]),
        out_shape=[jax.ShapeDtypeStruct((L, HW), BF16)] * ng + [jax.ShapeDtypeStruct((L, 128), F32)] * ng
        + [jax.ShapeDtypeStruct((L, HW), BF16)],
        compiler_params=_cp(("parallel",)), name="merge_bwd")(dgated, *os_, *lses, qz)
    return outs[:ng], outs[ng:2 * ng], outs[2 * ng]


def ada_fwd(c8, ada_w):
    nl, D, Ws = ada_w.shape
    tn = _tile(Ws, 512)

    def body(c_ref, w_ref, o_ref):
        o_ref[...] = jnp.dot(_silu(c_ref[...]), w_ref[...], precision=lax.Precision.HIGHEST,
                             preferred_element_type=F32)

    return pl.pallas_call(
        body, grid=(nl, Ws // tn),
        in_specs=[pl.BlockSpec((N_DEV, D), lambda l, j: (0, 0)), pl.BlockSpec((None, D, tn), lambda l, j: (l, 0, j))],
        out_specs=pl.BlockSpec((None, N_DEV, tn), lambda l, j: (l, 0, j)),
        out_shape=jax.ShapeDtypeStruct((nl, N_DEV, Ws), F32), compiler_params=_cp(("parallel", "parallel")),
        name="ada_fwd")(c8, ada_w)


def ada_wgrad(c8t, dmod):
    nl, _, Ws = dmod.shape
    D = c8t.shape[0]
    tm = _tile(D, 512, 8)

    def body(c_ref, d_ref, o_ref):
        sc = _silu(c_ref[...])
        acc = sc[:, 0:1] * d_ref[0:1, :]
        for e in range(1, N_DEV):
            acc = acc + sc[:, e:e + 1] * d_ref[e:e + 1, :]
        o_ref[...] = acc

    return pl.pallas_call(
        body, grid=(nl, D // tm),
        in_specs=[pl.BlockSpec((tm, N_DEV), lambda l, i: (i, 0)), pl.BlockSpec((None, N_DEV, Ws), lambda l, i: (l, 0, 0))],
        out_specs=pl.BlockSpec((None, tm, Ws), lambda l, i: (l, i, 0)),
        out_shape=jax.ShapeDtypeStruct((nl, D, Ws), F32), compiler_params=_cp(("parallel", "parallel")),
        name="ada_wgrad")(c8t, dmod)


def adamw(w, g, m, v, name):
    R, C = w.shape
    tr = _tile(R, 256, 8)
    c1 = 1.0 - ADAM_B1 ** ADAM_STEP
    c2 = 1.0 - ADAM_B2 ** ADAM_STEP

    def body(w_ref, g_ref, m_ref, v_ref, d_ref, nm_ref, nv_ref):
        gv = g_ref[...]
        nm = ADAM_B1 * m_ref[...] + (1.0 - ADAM_B1) * gv
        nv = ADAM_B2 * v_ref[...] + (1.0 - ADAM_B2) * (gv * gv)
        nm_ref[...] = nm
        nv_ref[...] = nv
        d_ref[...] = -ADAM_LR * ((nm / c1) / (jnp.sqrt(nv / c2) + ADAM_EPS) + ADAM_WD * w_ref[...])

    return pl.pallas_call(
        body, grid=(R // tr,), in_specs=_row_specs(tr, [C] * 4), out_specs=_row_specs(tr, [C] * 3),
        out_shape=[jax.ShapeDtypeStruct((R, C), F32)] * 3, compiler_params=_cp(("parallel",)), name=name)(w, g, m, v)


def sum_leading(t, name, out_dtype=F32):
    S, R, C = t.shape
    tr = _tile(R, 256, 16)

    def body(t_ref, o_ref):
        acc = t_ref[0].astype(F32)
        for s in range(1, S):
            acc = acc + t_ref[s].astype(F32)
        o_ref[...] = acc.astype(out_dtype)

    return pl.pallas_call(
        body, grid=(R // tr,), in_specs=[pl.BlockSpec((S, tr, C), lambda i: (0, i, 0))],
        out_specs=pl.BlockSpec((tr, C), lambda i: (i, 0)), out_shape=jax.ShapeDtypeStruct((R, C), out_dtype),
        compiler_params=_cp(("parallel",)), name=name)(t)


def add_half(g, a, core, name):
    S, R, C = g.shape
    h = R // 2
    tr = _tile(h, 256, 16)
    nb = h // tr

    def body(core_ref, g_ref, a_ref, o_ref):
        o_ref[...] = (g_ref[...].astype(F32) + a_ref[...].astype(F32)).astype(BF16)

    return pl.pallas_call(
        body,
        grid_spec=pltpu.PrefetchScalarGridSpec(
            num_scalar_prefetch=1, grid=(S, nb),
            in_specs=[pl.BlockSpec((None, tr, C), lambda s, i, core_ref: (s, core_ref[0] * nb + i, 0)),
                      pl.BlockSpec((None, tr, C), lambda s, i, core_ref: (s, i, 0))],
            out_specs=pl.BlockSpec((None, tr, C), lambda s, i, core_ref: (s, i, 0))),
        out_shape=jax.ShapeDtypeStruct((S, h, C), BF16), compiler_params=_cp(("parallel", "parallel")),
        name=name)(core, g, a)


def sum_partials(own, landed, chip, name):
    _, h, C = own.shape
    tr = _tile(h, 256, 16)

    def body(chip_ref, own_ref, l_ref, o_ref):
        acc = own_ref[...].astype(F32)
        for j in range(3):
            acc = acc + l_ref[j].astype(F32)
        o_ref[...] = acc

    return pl.pallas_call(
        body,
        grid_spec=pltpu.PrefetchScalarGridSpec(
            num_scalar_prefetch=1, grid=(h // tr,),
            in_specs=[pl.BlockSpec((None, tr, C), lambda i, chip_ref: (chip_ref[0], i, 0)),
                      pl.BlockSpec((3, tr, C), lambda i, chip_ref: (0, i, 0))],
            out_specs=pl.BlockSpec((tr, C), lambda i, chip_ref: (i, 0))),
        out_shape=jax.ShapeDtypeStruct((h, C), F32), compiler_params=_cp(("parallel",)), name=name)(chip, own, landed)


def adamw_halves(w, g_mine, g_theirs, m, v, core, name):
    R, C = w.shape
    h = R // 2
    tr = _tile(h, 256, 8)
    nbh = h // tr
    c1 = 1.0 - ADAM_B1 ** ADAM_STEP
    c2 = 1.0 - ADAM_B2 ** ADAM_STEP

    def body(core_ref, w_ref, gm_ref, gt_ref, m_ref, v_ref, g_ref, d_ref, nm_ref, nv_ref):
        mine = (pl.program_id(0) // nbh) == core_ref[0]
        gv = jnp.where(mine, gm_ref[...], gt_ref[...])
        g_ref[...] = gv
        nm = ADAM_B1 * m_ref[...] + (1.0 - ADAM_B1) * gv
        nv = ADAM_B2 * v_ref[...] + (1.0 - ADAM_B2) * (gv * gv)
        nm_ref[...] = nm
        nv_ref[...] = nv
        d_ref[...] = -ADAM_LR * ((nm / c1) / (jnp.sqrt(nv / c2) + ADAM_EPS) + ADAM_WD * w_ref[...])

    full = pl.BlockSpec((tr, C), lambda i, core_ref: (i, 0))
    halfspec = pl.BlockSpec((tr, C), lambda i, core_ref: (i % nbh, 0))
    return pl.pallas_call(
        body,
        grid_spec=pltpu.PrefetchScalarGridSpec(
            num_scalar_prefetch=1, grid=(2 * nbh,), in_specs=[full, halfspec, halfspec, full, full],
            out_specs=[full] * 4),
        out_shape=[jax.ShapeDtypeStruct((R, C), F32)] * 4, compiler_params=_cp(("parallel",)),
        name=name)(core, w, g_mine, g_theirs, m, v)


_ANY = pl.BlockSpec(memory_space=pl.ANY)


def _place():
    x, y, c = lax.axis_index("x"), lax.axis_index("y"), lax.axis_index("c")
    chips = [(1 - x, y), (x, 1 - y), (1 - x, 1 - y)]
    return x, y, c, chips


def allgather_small(v, name):
    R, W = v.shape

    def body(x_ref, out_ref, send_sems, recv_sems, local_sem):
        x, y, c, chips = _place()
        me, sibling = (x, y, c), (x, y, 1 - c)

        def rows(px, py, pc):
            return out_ref.at[pl.ds((4 * px + 2 * py + pc) * R, R), :]

        def copy(k, block, to, src=None):
            return pltpu.make_async_remote_copy(
                src_ref=rows(*block) if src is None else src, dst_ref=rows(*block),
                send_sem=send_sems.at[k], recv_sem=recv_sems.at[k], device_id=to, device_id_type=MESH)

        mine = pltpu.make_async_copy(x_ref, rows(*me), local_sem)
        mine.start()
        first = [copy(0, me, sibling, src=x_ref)]
        first += [copy(1 + j, me, (*chip, c), src=x_ref) for j, chip in enumerate(chips)]
        for cp in first:
            cp.start()
        passed = [copy(4 + j, (*chip, c), sibling) for j, chip in enumerate(chips)]
        for j, chip in enumerate(chips):
            copy(1 + j, (*chip, c), me).wait_recv()
            passed[j].start()
        copy(0, sibling, me).wait_recv()
        for j, chip in enumerate(chips):
            copy(4 + j, (*chip, 1 - c), me).wait_recv()
        for cp in first + passed:
            cp.wait_send()
        mine.wait()

    return pl.pallas_call(
        body, out_shape=jax.ShapeDtypeStruct((N_DEV * R, W), v.dtype),
        in_specs=[pl.BlockSpec(memory_space=pltpu.VMEM)], out_specs=pl.BlockSpec(memory_space=pltpu.VMEM),
        scratch_shapes=[pltpu.SemaphoreType.DMA((7,)), pltpu.SemaphoreType.DMA((7,)), pltpu.SemaphoreType.DMA],
        name=name)(v)


def allgather_weights(shards):
    n = len(shards)

    def body(*refs):
        ins, outs = refs[:n], refs[n:2 * n]
        send_sems, recv_sems = refs[2 * n:]
        x, y, c, chips = _place()
        p = 2 * x + y
        sibling = (x, y, 1 - c)

        def half(i, chip_id, core, ref=None):
            r = outs[i].at[chip_id] if ref is None else ref
            return r.at[core]

        def copy(i, k, chip_id, core, to, src=None):
            return pltpu.make_async_remote_copy(
                src_ref=half(i, chip_id, core) if src is None else src, dst_ref=half(i, chip_id, core),
                send_sem=send_sems.at[6 * i + k], recv_sem=recv_sems.at[6 * i + k], device_id=to, device_id_type=MESH)

        first = [copy(i, j, p, c, (*chip, c), src=half(i, p, c, ref=ins[i]))
                 for i in range(n) for j, chip in enumerate(chips)]
        for cp in first:
            cp.start()
        passed = []
        for i in range(n):
            for j, (cx, cy) in enumerate(chips):
                copy(i, j, 2 * cx + cy, c, sibling).wait_recv()
                fw = copy(i, 3 + j, 2 * cx + cy, c, sibling)
                fw.start()
                passed.append(fw)
        for i in range(n):
            for j, (cx, cy) in enumerate(chips):
                copy(i, 3 + j, 2 * cx + cy, 1 - c, sibling).wait_recv()
        for cp in first + passed:
            cp.wait_send()

    split = [s.reshape(2, s.shape[0] // 2, s.shape[1]) for s in shards]
    outs = pl.pallas_call(
        body, out_shape=[jax.ShapeDtypeStruct((N_CHIPS,) + s.shape, s.dtype) for s in split],
        in_specs=[_ANY] * n, out_specs=[_ANY] * n,
        scratch_shapes=[pltpu.SemaphoreType.DMA((6 * n,)), pltpu.SemaphoreType.DMA((6 * n,))],
        name="allgather_weights")(*split)
    chip = 2 * lax.axis_index("x") + lax.axis_index("y")
    return [lax.dynamic_update_index_in_dim(o, s, chip, 0).reshape((N_CHIPS,) + sh.shape)
            for o, s, sh in zip(outs, split, shards)]


def exchange_halves_to_sibling(gs):
    n = len(gs)

    def body(*refs):
        ins, outs = refs[:n], refs[n:2 * n]
        send_sems, recv_sems = refs[2 * n:]
        x, y, c, _ = _place()
        cps = []
        for i in range(n):
            h = ins[i].shape[1] // 2
            cps.append(pltpu.make_async_remote_copy(
                src_ref=ins[i].at[:, pl.ds((1 - c) * h, h), :], dst_ref=outs[i],
                send_sem=send_sems.at[i], recv_sem=recv_sems.at[i], device_id=(x, y, 1 - c), device_id_type=MESH))
        for cp in cps:
            cp.start()
        for cp in cps:
            cp.wait()

    return pl.pallas_call(
        body, out_shape=[jax.ShapeDtypeStruct((g.shape[0], g.shape[1] // 2, g.shape[2]), g.dtype) for g in gs],
        in_specs=[_ANY] * n, out_specs=[_ANY] * n,
        scratch_shapes=[pltpu.SemaphoreType.DMA((n,)), pltpu.SemaphoreType.DMA((n,))],
        name="rs_sibling_exchange")(*gs)


def scatter_to_chips(ps):
    n = len(ps)

    def body(*refs):
        ins, outs = refs[:n], refs[n:2 * n]
        send_sems, recv_sems = refs[2 * n:]
        x, y, c, chips = _place()
        cps = []
        for i in range(n):
            for j, (cx, cy) in enumerate(chips):
                cps.append(pltpu.make_async_remote_copy(
                    src_ref=ins[i].at[2 * cx + cy], dst_ref=outs[i].at[j], send_sem=send_sems.at[3 * i + j],
                    recv_sem=recv_sems.at[3 * i + j], device_id=(cx, cy, c), device_id_type=MESH))
        for cp in cps:
            cp.start()
        for cp in cps:
            cp.wait()

    return pl.pallas_call(
        body, out_shape=[jax.ShapeDtypeStruct((3,) + t.shape[1:], t.dtype) for t in ps],
        in_specs=[_ANY] * n, out_specs=[_ANY] * n,
        scratch_shapes=[pltpu.SemaphoreType.DMA((3 * n,)), pltpu.SemaphoreType.DMA((3 * n,))],
        name="rs_chip_scatter")(*ps)


def join_halves(rs):
    n = len(rs)

    def body(*refs):
        ins, outs = refs[:n], refs[n:2 * n]
        send_sems, recv_sems = refs[2 * n:]
        x, y, c, _ = _place()
        cps = [pltpu.make_async_remote_copy(
            src_ref=ins[i], dst_ref=outs[i], send_sem=send_sems.at[i], recv_sem=recv_sems.at[i],
            device_id=(x, y, 1 - c), device_id_type=MESH) for i in range(n)]
        for cp in cps:
            cp.start()
        for cp in cps:
            cp.wait()

    return pl.pallas_call(
        body, out_shape=[jax.ShapeDtypeStruct(r.shape, r.dtype) for r in rs],
        in_specs=[_ANY] * n, out_specs=[_ANY] * n,
        scratch_shapes=[pltpu.SemaphoreType.DMA((n,)), pltpu.SemaphoreType.DMA((n,))],
        name="rs_join_halves")(*rs)


def _pack(parts, row_mult=8):
    flat = jnp.concatenate([p.reshape(-1).astype(F32) for p in parts])
    unit = row_mult * 128
    n = -(-flat.shape[0] // unit) * unit
    return jnp.pad(flat, (0, n - flat.shape[0])).reshape(n // 128, 128)


def _unpack(flat, shapes):
    out, off = [], 0
    for s in shapes:
        n = int(np.prod(s))
        out.append(flat[off:off + n].reshape(s))
        off += n
    return out


def _gather_packed(parts, name):
    packed = _pack(parts)
    g = allgather_small(packed, name).reshape(N_DEV, -1)
    return _unpack_rows(g, [p.shape for p in parts])


def _unpack_rows(g, shapes):
    out, off = [], 0
    for s in shapes:
        n = int(np.prod(s))
        out.append(g[:, off:off + n].reshape((g.shape[0],) + tuple(s)))
        off += n
    return out


def _by_chip(t, axis):
    return jnp.concatenate([t[2 * p] for p in range(N_CHIPS)], axis=axis)


def kernel(x, c, ada_w, ada_b, ln_g, ln_b, a_in_w, a_conv_w, a_conv_b, a_dt_bias, a_A_log, a_D, a_norm_g, a_out_w, kv_w, b_in_w, b_out_w, loss_target, m_ada_w, m_ada_b, m_ln_g, m_ln_b, m_a_in_w, m_a_conv_w, m_a_conv_b, m_a_dt_bias, m_a_A_log, m_a_D, m_a_norm_g, m_a_out_w, m_kv_w, m_b_in_w, m_b_out_w, v_ada_w, v_ada_b, v_ln_g, v_ln_b, v_a_in_w, v_a_conv_w, v_a_conv_b, v_a_dt_bias, v_a_A_log, v_a_D, v_a_norm_g, v_a_out_w, v_kv_w, v_b_in_w, v_b_out_w):
    ax, ay, ac = lax.axis_index("x"), lax.axis_index("y"), lax.axis_index("c")
    chip = 2 * ax + ay
    dev = 4 * ax + 2 * ay + ac
    xin = x[0]
    tgt = loss_target[0]
    L, D = xin.shape
    G, P = SSD_G, SSD_P
    H = a_dt_bias.shape[1]
    Kh = H // G
    DI = H * P
    CONVD = a_conv_b.shape[1] * N_CHIPS
    HW = DIL_H * DIL_E
    Ws = ada_w.shape[2]

    w_in_g, w_out_g, w_kv_g, w_bin_g, w_bout_g = allgather_weights(
        [a_in_w[0].astype(BF16), a_out_w[0].astype(BF16), kv_w.astype(BF16), b_in_w[0].astype(BF16),
         b_out_w[0].astype(BF16)])
    w_in = jnp.transpose(w_in_g, (1, 0, 2)).reshape(D, -1)
    w_zx = w_in[:, :DI + CONVD]
    w_dt = jnp.pad(w_in[:, DI + CONVD:], ((0, 0), (0, 128 - H)))

    c8, cw8, cb8, ng8 = _gather_packed([c[0], a_conv_w[0], a_conv_b[0], a_norm_g[0]], "allgather_small_params")
    conv_w = _by_chip(cw8, 1)
    conv_b = _by_chip(cb8, 0).reshape(1, CONVD)
    norm_g = _by_chip(ng8, 0).reshape(1, DI)

    mod_s = ada_fwd(c8, ada_w)
    (mod8,) = _gather_packed([mod_s], "allgather_small_mod")
    mods = _by_chip(mod8, 2)
    mod = lax.dynamic_index_in_dim(mods, dev, axis=1, keepdims=False) + ada_b
    shift = [mod[l:l + 1, :D] for l in range(DEPTH)]
    scale = [mod[l:l + 1, D:2 * D] for l in range(DEPTH)]
    gate = [mod[l:l + 1, 2 * D:] for l in range(DEPTH)]
    lg = [ln_g[l:l + 1] for l in range(DEPTH)]
    lb = [ln_b[l:l + 1] for l in range(DEPTH)]

    h0 = modulate(xin, scale[0], shift[0], "modulate0")
    zx = mm_nn(h0, w_zx, BF16, "mm_in_zx")
    dtp = mm_nn(h0, w_dt, F32, "mm_in_dt")
    xbc = conv_fwd(zx, DI, conv_w, conv_b)
    dtp_g = jnp.transpose(dtp[:, :H].reshape(L, G, Kh), (1, 0, 2))
    dtp_gT = jnp.transpose(dtp_g, (0, 2, 1))
    vecs = [a_dt_bias.reshape(G, 1, Kh), a_dt_bias.reshape(G, Kh, 1), a_A_log.reshape(G, 1, Kh),
            a_A_log.reshape(G, Kh, 1), a_D.reshape(G, 1, Kh), a_D.reshape(G, Kh, 1)]
    y_ssd, states = ssd_fwd(xbc, dtp_g, dtp_gT, *vecs, DI)
    yn = rms_gate_fwd(y_ssd, zx, norm_g)
    ymix0 = mm_nn(yn, w_out_g, F32, "mm_out_a", stack="row")
    x1, x1b, h1 = ln_mid(xin, ymix0, gate[0], lg[0], lb[0], scale[1], shift[1])

    kvp = mm_nn(x1b, w_kv_g, BF16, "mm_kv", stack="col")
    qz = mm_nn(h1, w_bin_g, BF16, "mm_in_b", stack="col")
    os_, lses = [], []
    for gi in range(len(DIL_PATTERNS)):
        o, lse = attn_fwd(qz, kvp, gi)
        os_.append(o)
        lses.append(lse)
    om = merge_fwd(os_, lses, qz)
    ymix1 = mm_nn(om, w_bout_g, F32, "mm_out_b", stack="col")
    dx2, sq = ln_final(x1, ymix1, gate[1], lg[1], lb[1], tgt)
    loss_part = 0.5 * jnp.sum(sq) / D

    dres2, dy2, dg1, db1, dgate1 = ln_bwd(dx2, x1, ymix1, gate[1], lg[1], "ln_bwd1")
    g_bout = mm_tn(om, dy2, BF16, "mm_gw_out_b", stack="col")
    dgated = mm_nt(dy2, w_bout_g, BF16, "mm_gx_out_b", stack="col")
    dos, dprs, dz_b = merge_bwd(dgated, os_, lses, qz)
    dqs, dks, dvs = [], [], []
    for gi in range(len(DIL_PATTERNS)):
        dq, dk, dv = attn_bwd(qz, kvp, dos[gi], lses[gi], dprs[gi], gi)
        dqs.append(dq)
        dks.append(dk)
        dvs.append(dv)
    dqz = jnp.concatenate(dqs + [dz_b], axis=1)
    dkv = jnp.concatenate(dks + dvs, axis=1)
    g_bin = mm_tn(h1, dqz, BF16, "mm_gw_in_b", stack="col")
    dh1 = mm_nt(dqz, w_bin_g, F32, "mm_gx_in_b", stack="col")
    g_kv = mm_tn(x1b, dkv, BF16, "mm_gw_kv", stack="col")
    dx1_kv = mm_nt(dkv, w_kv_g, F32, "mm_gx_kv", stack="col")
    dx1, dscale1, dshift1 = mod_bwd(dres2, dh1, dx1_kv, x1, scale[1], "mod_bwd1", through_mod=False)

    dres1, dy1, dg0, db0, dgate0 = ln_bwd(dx1, xin, ymix0, gate[0], lg[0], "ln_bwd0")
    g_out = mm_tn(yn, dy1, BF16, "mm_gw_out_a", stack="row")
    dyn = mm_nt(dy1, w_out_g, BF16, "mm_gx_out_a", stack="row")
    dy_ssd, dz_a, dnorm_g = rms_gate_bwd(dyn, y_ssd, zx, norm_g)
    dxs, dB, dC, ddtp_g, dbias_g, dalog_g, dD_g = ssd_bwd(xbc, dtp_g, dtp_gT, *vecs, states, dy_ssd, DI)
    dxbc = jnp.concatenate([dxs, dB, dC], axis=1)
    dxbc_pre, dconv_w, dconv_b = conv_bwd(zx, DI, conv_w, conv_b, dxbc)
    dzx = jnp.concatenate([dz_a, dxbc_pre], axis=1)
    ddtp = jnp.pad(jnp.transpose(ddtp_g, (1, 0, 2)).reshape(L, H), ((0, 0), (0, 128 - H)))
    g_zx = mm_tn(h0, dzx, BF16, "mm_gw_in_zx")
    g_dt = mm_tn(h0, ddtp, BF16, "mm_gw_in_dt")
    dh0 = mm_nt(dzx, w_zx, F32, "mm_gx_in_zx")
    dh0_dt = mm_nt(ddtp, w_dt, F32, "mm_gx_in_dt")
    grad_x, dscale0, dshift0 = mod_bwd(dres1, dh0, dh0_dt, xin, scale[0], "mod_bwd0", through_mod=True)
    g_in = jnp.concatenate([g_zx, g_dt[:, :H]], axis=1)
    g_in = jnp.transpose(g_in.reshape(D, N_CHIPS, -1), (1, 0, 2))

    gs = [g_in, g_out, g_kv, g_bin, g_bout]
    names = ["in_a", "out_a", "kv", "in_b", "out_b"]
    core = ac.astype(jnp.int32).reshape(1)
    sib = exchange_halves_to_sibling(gs)
    parts = [add_half(g, a, core, "rs_add_" + nm) for g, a, nm in zip(gs, sib, names)]
    landed = scatter_to_chips(parts)
    chip_i = chip.astype(jnp.int32).reshape(1)
    halves = [sum_partials(own, t, chip_i, "rs_sum_" + nm) for own, t, nm in zip(parts, landed, names)]
    theirs = join_halves(halves)
    g_halves = dict(zip(names, zip(halves, theirs)))

    dmod = jnp.concatenate([jnp.concatenate([dshift0, dscale0, dgate0], axis=1),
                            jnp.concatenate([dshift1, dscale1, dgate1], axis=1)], axis=0)
    small_parts = [jnp.concatenate([dg0, dg1], axis=0), jnp.concatenate([db0, db1], axis=0),
                   dbias_g.reshape(1, H), dalog_g.reshape(1, H), dD_g.reshape(1, H),
                   dconv_w, dconv_b, dnorm_g, loss_part.reshape(1, 1)]
    small_shapes = [p.shape for p in small_parts]
    packed = jnp.concatenate([_pack([dmod]), _pack(small_parts)], axis=0)
    n_mod_rows = _pack([dmod]).shape[0]
    gathered = allgather_small(packed, "allgather_small_grads").reshape(N_DEV, -1, 128)
    dmod8 = gathered[:, :n_mod_rows].reshape(N_DEV, -1)[:, :2 * 3 * D].reshape(N_DEV, DEPTH, 3 * D)
    summed = sum_leading(gathered, "sum_small")
    g_ada_b = summed[:n_mod_rows].reshape(-1)[:2 * 3 * D].reshape(DEPTH, 3 * D)
    (g_ln_g, g_ln_b, g_dt_bias, g_a_log, g_dsk, g_conv_w, g_conv_b, g_norm_g, loss_all) = _unpack(
        summed[n_mod_rows:].reshape(-1), small_shapes)
    loss = loss_all.reshape(())
    Cs = CONVD // N_CHIPS
    g_conv_w_s = lax.dynamic_slice_in_dim(g_conv_w, chip * Cs, Cs, axis=1)
    g_conv_b_s = lax.dynamic_slice_in_dim(g_conv_b, chip * Cs, Cs, axis=1)
    g_norm_g_s = lax.dynamic_slice_in_dim(g_norm_g, chip * (DI // N_CHIPS), DI // N_CHIPS, axis=1)
    dmod_s = jnp.transpose(lax.dynamic_slice_in_dim(dmod8, chip * Ws, Ws, axis=2), (1, 0, 2))
    g_ada_w = ada_wgrad(jnp.transpose(c8), dmod_s)

    def step2d(w, g, m, v, nm):
        shp = w.shape
        d_, m_, v_ = adamw(w.reshape(-1, shp[-1]), g.reshape(-1, shp[-1]), m.reshape(-1, shp[-1]),
                           v.reshape(-1, shp[-1]), "adamw_" + nm)
        return g.reshape(shp), d_.reshape(shp), m_.reshape(shp), v_.reshape(shp)

    def step_halves(w, m, v, nm):
        shp = w.shape
        mine, theirs_ = g_halves[nm]
        outs4 = adamw_halves(w.reshape(-1, shp[-1]), mine, theirs_, m.reshape(-1, shp[-1]), v.reshape(-1, shp[-1]),
                             core, "adamw_" + nm)
        return tuple(t.reshape(shp) for t in outs4)

    big = {
        "ada_w": step2d(ada_w, g_ada_w, m_ada_w, v_ada_w, "ada_w"),
        "a_in_w": step_halves(a_in_w, m_a_in_w, v_a_in_w, "in_a"),
        "a_out_w": step_halves(a_out_w, m_a_out_w, v_a_out_w, "out_a"),
        "kv_w": step_halves(kv_w, m_kv_w, v_kv_w, "kv"),
        "b_in_w": step_halves(b_in_w, m_b_in_w, v_b_in_w, "in_b"),
        "b_out_w": step_halves(b_out_w, m_b_out_w, v_b_out_w, "out_b"),
    }
    small_names = ["ada_b", "ln_g", "ln_b", "a_conv_w", "a_conv_b", "a_dt_bias", "a_A_log", "a_D", "a_norm_g"]
    small_w = [ada_b, ln_g, ln_b, a_conv_w, a_conv_b, a_dt_bias, a_A_log, a_D, a_norm_g]
    small_m = [m_ada_b, m_ln_g, m_ln_b, m_a_conv_w, m_a_conv_b, m_a_dt_bias, m_a_A_log, m_a_D, m_a_norm_g]
    small_v = [v_ada_b, v_ln_g, v_ln_b, v_a_conv_w, v_a_conv_b, v_a_dt_bias, v_a_A_log, v_a_D, v_a_norm_g]
    small_g = [g_ada_b, g_ln_g, g_ln_b, g_conv_w_s, g_conv_b_s, g_dt_bias, g_a_log, g_dsk, g_norm_g_s]
    shapes = [w.shape for w in small_w]
    small_g = [g.reshape(s) for g, s in zip(small_g, shapes)]
    d_p, m_p, v_p = adamw(_pack(small_w), _pack(small_g), _pack(small_m), _pack(small_v), "adamw_small")
    small = {}
    for nm, g, d_, m_, v_ in zip(small_names, small_g, _unpack(d_p.reshape(-1), shapes), _unpack(m_p.reshape(-1), shapes),
                                 _unpack(v_p.reshape(-1), shapes)):
        small[nm] = (g, d_, m_, v_)
    allw = {**big, **small}
    order = ["ada_w", "ada_b", "ln_g", "ln_b", "a_in_w", "a_conv_w", "a_conv_b", "a_dt_bias", "a_A_log", "a_D",
             "a_norm_g", "a_out_w", "kv_w", "b_in_w", "b_out_w"]
    outs = [loss, grad_x.reshape(x.shape)]
    for k in range(4):
        outs += [allw[n][k] for n in order]
    return tuple(outs)
```

```python
import functools

import jax
import jax.numpy as jnp
import numpy as np
from jax import lax
from jax.experimental import pallas as pl
from jax.experimental.pallas import tpu as pltpu

F32 = jnp.float32
BF16 = jnp.bfloat16
MESH = pl.DeviceIdType.MESH

DEPTH = 2
ALPHA = (2 * DEPTH) ** 0.25
LN_EPS = 1e-5
RMS_EPS = 1e-5
SSD_P = 64
SSD_N = 128
SSD_Q = 256
SSD_G = 8
CONV_W = 4
DIL_PATTERNS = ((128, 1), (512, 4), (2048, 16))
DIL_H = 8
DIL_E = 128
DIL_BLK = 128
ADAM_LR, ADAM_B1, ADAM_B2, ADAM_EPS, ADAM_WD, ADAM_STEP = 0.001, 0.9, 0.999, 1e-08, 0.01, 10

VMEM_LIMIT = 56 * 1024 * 1024
N_CHIPS = 4
N_DEV = 8


def _tile(dim, target, mult=128):
    if dim <= target:
        return dim
    t = (target // mult) * mult
    while t >= mult:
        if dim % t == 0:
            return t
        t -= mult
    return dim


def _cp(sem):
    return pltpu.CompilerParams(dimension_semantics=sem, vmem_limit_bytes=VMEM_LIMIT)


def _sigmoid(x):
    return 1.0 / (1.0 + jnp.exp(-x))


def _silu(x):
    return x * _sigmoid(x)


def _dsilu(x):
    s = _sigmoid(x)
    return s * (1.0 + x * (1.0 - s))


def _softplus(x):
    return jnp.maximum(x, 0.0) + jnp.log(1.0 + jnp.exp(-jnp.abs(x)))


def _mm_call(a, b, out_shape, grid, a_spec, b_spec, o_spec, acc_shape, dims, name):
    nk = grid[2]

    def prod(a_ref, b_ref):
        return lax.dot_general(a_ref[...].astype(BF16), b_ref[...].astype(BF16), (dims, ((), ())),
                               preferred_element_type=F32)

    def body_single(a_ref, b_ref, o_ref):
        o_ref[...] = prod(a_ref, b_ref).astype(o_ref.dtype)

    def body_multi(a_ref, b_ref, o_ref, acc_ref):
        k = pl.program_id(2)

        @pl.when(k == 0)
        def _():
            acc_ref[...] = prod(a_ref, b_ref)

        @pl.when(jnp.logical_and(k > 0, k < nk - 1))
        def _():
            acc_ref[...] += prod(a_ref, b_ref)

        @pl.when(k == nk - 1)
        def _():
            o_ref[...] = (acc_ref[...] + prod(a_ref, b_ref)).astype(o_ref.dtype)

    return pl.pallas_call(
        body_single if nk == 1 else body_multi, grid=grid, in_specs=[a_spec, b_spec], out_specs=o_spec,
        out_shape=out_shape, scratch_shapes=[] if nk == 1 else [pltpu.VMEM(acc_shape, F32)],
        compiler_params=_cp(("parallel", "parallel", "arbitrary")), name=name)(a, b)


def mm_nn(a, b, out_dtype, name, stack=None, tm=1024, tn=1024, tk=2048):
    M, K = a.shape
    if stack is None:
        N = b.shape[1]
        tn, tk = _tile(N, tn), _tile(K, tk)
        b_spec = pl.BlockSpec((tk, tn), lambda i, j, k: (k, j))
    elif stack == "col":
        S, _, Ns = b.shape
        N = S * Ns
        tn, tk = _tile(Ns, tn), _tile(K, tk)
        npb = Ns // tn
        b_spec = pl.BlockSpec((None, tk, tn), lambda i, j, k: (j // npb, k, j % npb))
    else:
        S, Ks, N = b.shape
        tn, tk = _tile(N, tn), _tile(Ks, tk)
        kpb = Ks // tk
        b_spec = pl.BlockSpec((None, tk, tn), lambda i, j, k: (k // kpb, k % kpb, j))
    tm = _tile(M, tm)
    return _mm_call(a, b, jax.ShapeDtypeStruct((M, N), out_dtype), (M // tm, N // tn, K // tk),
                    pl.BlockSpec((tm, tk), lambda i, j, k: (i, k)), b_spec,
                    pl.BlockSpec((tm, tn), lambda i, j, k: (i, j)), (tm, tn), ((1,), (0,)), name)


def mm_nt(a, b, out_dtype, name, stack=None, tm=1024, tn=1024, tk=2048):
    M, C = a.shape
    if stack is None:
        Kw = b.shape[0]
        tn, tk = _tile(Kw, tn), _tile(C, tk)
        b_spec = pl.BlockSpec((tn, tk), lambda i, j, k: (j, k))
    elif stack == "col":
        S, Kw, Cs = b.shape
        tn, tk = _tile(Kw, tn), _tile(Cs, tk)
        cpb = Cs // tk
        b_spec = pl.BlockSpec((None, tn, tk), lambda i, j, k: (k // cpb, j, k % cpb))
    else:
        S, Ks, _ = b.shape
        Kw = S * Ks
        tn, tk = _tile(Ks, tn), _tile(C, tk)
        jpb = Ks // tn
        b_spec = pl.BlockSpec((None, tn, tk), lambda i, j, k: (j // jpb, j % jpb, k))
    tm = _tile(M, tm)
    return _mm_call(a, b, jax.ShapeDtypeStruct((M, Kw), out_dtype), (M // tm, Kw // tn, C // tk),
                    pl.BlockSpec((tm, tk), lambda i, j, k: (i, k)), b_spec,
                    pl.BlockSpec((tm, tn), lambda i, j, k: (i, j)), (tm, tn), ((1,), (1,)), name)


def mm_tn(a, b, out_dtype, name, stack=None, n_stack=N_CHIPS, tm=1024, tn=1024, tk=2048):
    L, M = a.shape
    N = b.shape[1]
    tk = _tile(L, tk)
    if stack is None:
        tm, tn = _tile(M, tm), _tile(N, tn)
        o_spec = pl.BlockSpec((tm, tn), lambda i, j, k: (i, j))
        out_shape = (M, N)
    elif stack == "col":
        Ns = N // n_stack
        tm, tn = _tile(M, tm), _tile(Ns, tn)
        npb = Ns // tn
        o_spec = pl.BlockSpec((None, tm, tn), lambda i, j, k: (j // npb, i, j % npb))
        out_shape = (n_stack, M, Ns)
    else:
        Ms = M // n_stack
        tm, tn = _tile(Ms, tm), _tile(N, tn)
        mpb = Ms // tm
        o_spec = pl.BlockSpec((None, tm, tn), lambda i, j, k: (i // mpb, i % mpb, j))
        out_shape = (n_stack, Ms, N)
    return _mm_call(a, b, jax.ShapeDtypeStruct(out_shape, out_dtype), (M // tm, N // tn, L // tk),
                    pl.BlockSpec((tk, tm), lambda i, j, k: (k, i)), pl.BlockSpec((tk, tn), lambda i, j, k: (k, j)),
                    o_spec, (tm, tn), ((0,), (0,)), name)


def _row_specs(tr, widths):
    return [pl.BlockSpec((tr, w), lambda i: (i, 0)) for w in widths]


def _vec_spec(w):
    return pl.BlockSpec((1, w), lambda i: (0, 0))


def _acc_rows(ref, val, i):
    s = jnp.sum(val, axis=0, keepdims=True)

    @pl.when(i == 0)
    def _():
        ref[...] = s

    @pl.when(i > 0)
    def _():
        ref[...] += s


def modulate(x, scale, shift, name):
    L, D = x.shape
    tr = _tile(L, 512, 16)

    def body(x_ref, sc_ref, sh_ref, h_ref):
        h_ref[...] = (x_ref[...] * (1.0 + sc_ref[...]) + sh_ref[...]).astype(BF16)

    return pl.pallas_call(
        body, grid=(L // tr,), in_specs=_row_specs(tr, [D]) + [_vec_spec(D)] * 2, out_specs=_row_specs(tr, [D])[0],
        out_shape=jax.ShapeDtypeStruct((L, D), BF16), compiler_params=_cp(("parallel",)), name=name)(x, scale, shift)


def _ln_core(x, y, gate, g, b):
    u = ALPHA * x + (1.0 + gate) * y
    mu = jnp.mean(u, axis=-1, keepdims=True)
    d = u - mu
    var = jnp.mean(d * d, axis=-1, keepdims=True)
    rstd = lax.rsqrt(var + LN_EPS)
    xhat = d * rstd
    return xhat * g + b, xhat, rstd


def ln_mid(x, y, gate, g, b, scale, shift):
    L, D = x.shape
    tr = _tile(L, 256, 16)

    def body(x_ref, y_ref, gate_ref, g_ref, b_ref, sc_ref, sh_ref, x1_ref, x1b_ref, h_ref):
        x1, _, _ = _ln_core(x_ref[...], y_ref[...], gate_ref[...], g_ref[...], b_ref[...])
        x1_ref[...] = x1
        x1b_ref[...] = x1.astype(BF16)
        h_ref[...] = (x1 * (1.0 + sc_ref[...]) + sh_ref[...]).astype(BF16)

    return pl.pallas_call(
        body, grid=(L // tr,), in_specs=_row_specs(tr, [D, D]) + [_vec_spec(D)] * 5,
        out_specs=_row_specs(tr, [D, D, D]),
        out_shape=[jax.ShapeDtypeStruct((L, D), F32), jax.ShapeDtypeStruct((L, D), BF16),
                   jax.ShapeDtypeStruct((L, D), BF16)],
        compiler_params=_cp(("parallel",)), name="ln_mid")(x, y, gate, g, b, scale, shift)


def ln_final(x, y, gate, g, b, target):
    L, D = x.shape
    tr = _tile(L, 256, 16)

    def body(x_ref, y_ref, gate_ref, g_ref, b_ref, t_ref, dout_ref, sq_ref):
        out, _, _ = _ln_core(x_ref[...], y_ref[...], gate_ref[...], g_ref[...], b_ref[...])
        err = out - t_ref[...]
        dout_ref[...] = err * (1.0 / D)
        _acc_rows(sq_ref, err * err, pl.program_id(0))

    return pl.pallas_call(
        body, grid=(L // tr,), in_specs=_row_specs(tr, [D, D]) + [_vec_spec(D)] * 3 + _row_specs(tr, [D]),
        out_specs=[_row_specs(tr, [D])[0], _vec_spec(D)],
        out_shape=[jax.ShapeDtypeStruct((L, D), F32), jax.ShapeDtypeStruct((1, D), F32)],
        compiler_params=_cp(("arbitrary",)), name="ln_final")(x, y, gate, g, b, target)


def ln_bwd(dout, x, y, gate, g, name):
    L, D = x.shape
    tr = _tile(L, 256, 16)

    def body(do_ref, x_ref, y_ref, gate_ref, g_ref, dres_ref, dy_ref, dg_ref, db_ref, dgate_ref):
        i = pl.program_id(0)
        yv = y_ref[...]
        dout_v = do_ref[...]
        _, xhat, rstd = _ln_core(x_ref[...], yv, gate_ref[...], g_ref[...], 0.0)
        dxh = dout_v * g_ref[...]
        m1 = jnp.mean(dxh, axis=-1, keepdims=True)
        m2 = jnp.mean(dxh * xhat, axis=-1, keepdims=True)
        du = rstd * (dxh - m1 - xhat * m2)
        dres_ref[...] = ALPHA * du
        dy_ref[...] = ((1.0 + gate_ref[...]) * du).astype(BF16)
        _acc_rows(dg_ref, dout_v * xhat, i)
        _acc_rows(db_ref, dout_v, i)
        _acc_rows(dgate_ref, du * yv, i)

    return pl.pallas_call(
        body, grid=(L // tr,), in_specs=_row_specs(tr, [D, D, D]) + [_vec_spec(D)] * 2,
        out_specs=_row_specs(tr, [D, D]) + [_vec_spec(D)] * 3,
        out_shape=[jax.ShapeDtypeStruct((L, D), F32), jax.ShapeDtypeStruct((L, D), BF16)]
        + [jax.ShapeDtypeStruct((1, D), F32)] * 3,
        compiler_params=_cp(("arbitrary",)), name=name)(dout, x, y, gate, g)


def mod_bwd(dres, dh, dh2, xin, scale, name, through_mod):
    L, D = xin.shape
    tr = _tile(L, 256, 16)

    def body(dres_ref, dh_ref, dh2_ref, x_ref, sc_ref, dx_ref, dsc_ref, dsh_ref):
        i = pl.program_id(0)
        dh_v = dh_ref[...]
        tot = dres_ref[...]
        if through_mod:
            dh_v = dh_v + dh2_ref[...]
        else:
            tot = tot + dh2_ref[...]
        dx_ref[...] = tot + dh_v * (1.0 + sc_ref[...])
        _acc_rows(dsc_ref, dh_v * x_ref[...], i)
        _acc_rows(dsh_ref, dh_v, i)

    return pl.pallas_call(
        body, grid=(L // tr,), in_specs=_row_specs(tr, [D, D, D, D]) + [_vec_spec(D)],
        out_specs=_row_specs(tr, [D]) + [_vec_spec(D)] * 2,
        out_shape=[jax.ShapeDtypeStruct((L, D), F32)] + [jax.ShapeDtypeStruct((1, D), F32)] * 2,
        compiler_params=_cp(("arbitrary",)), name=name)(dres, dh, dh2, xin, scale)


CONV_HALO = 16


def _conv_rows(x_ref, i, tr, L):
    nblk = L // tr
    s = pl.multiple_of(i * tr, CONV_HALO)
    cur = x_ref[pl.ds(s, tr), :].astype(F32)
    sp = pl.multiple_of(jnp.maximum(i * tr - CONV_HALO, 0), CONV_HALO)
    sn = pl.multiple_of(jnp.minimum(i * tr + tr, L - CONV_HALO), CONV_HALO)
    prev = x_ref[pl.ds(sp, CONV_HALO), :].astype(F32) * (i > 0).astype(F32)
    nxt = x_ref[pl.ds(sn, CONV_HALO), :].astype(F32) * (i < nblk - 1).astype(F32)
    return jnp.concatenate([prev, cur, nxt], axis=0)


def _shift_rows(v, j):
    n = v.shape[0]
    return v if j % n == 0 else pltpu.roll(v, j % n, 0)


def _conv_eval(xe, w_ref, b_ref):
    c = b_ref[...] + w_ref[CONV_W - 1:CONV_W, :] * xe
    for k in range(CONV_W - 1):
        c = c + w_ref[k:k + 1, :] * _shift_rows(xe, CONV_W - 1 - k)
    return c


def conv_fwd(zx, col0, conv_w, conv_b):
    L = zx.shape[0]
    C = conv_w.shape[1]
    tc = _tile(C, 512)
    tr = _tile(L, 512, CONV_HALO)
    off = col0 // tc

    def body(x_ref, w_ref, b_ref, o_ref):
        i = pl.program_id(1)
        xe = _conv_rows(x_ref, i, tr, L)
        c = _conv_eval(xe, w_ref, b_ref)[CONV_HALO:CONV_HALO + tr]
        o_ref[...] = _silu(c).astype(BF16)

    return pl.pallas_call(
        body, grid=(C // tc, L // tr),
        in_specs=[pl.BlockSpec((L, tc), lambda j, i: (0, off + j)), pl.BlockSpec((CONV_W, tc), lambda j, i: (0, j)),
                  pl.BlockSpec((1, tc), lambda j, i: (0, j))],
        out_specs=pl.BlockSpec((tr, tc), lambda j, i: (i, j)),
        out_shape=jax.ShapeDtypeStruct((L, C), BF16), compiler_params=_cp(("parallel", "arbitrary")),
        name="conv_fwd")(zx, conv_w, conv_b)


def conv_bwd(zx, col0, conv_w, conv_b, dxbc):
    L = zx.shape[0]
    C = conv_w.shape[1]
    tc = _tile(C, 512)
    tr = _tile(L, 512, CONV_HALO)
    off = col0 // tc
    H = CONV_HALO

    def body(x_ref, g_ref, w_ref, b_ref, dx_ref, dw_ref, db_ref):
        i = pl.program_id(1)
        xe = _conv_rows(x_ref, i, tr, L)
        ge = _conv_rows(g_ref, i, tr, L)
        dc = ge * _dsilu(_conv_eval(xe, w_ref, b_ref))
        dx = w_ref[CONV_W - 1:CONV_W, :] * dc
        for k in range(CONV_W - 1):
            dx = dx + w_ref[k:k + 1, :] * _shift_rows(dc, -(CONV_W - 1 - k))
        dx_ref[...] = dx[H:H + tr].astype(BF16)
        dcc = dc[H:H + tr]
        rows = [jnp.sum(dcc * _shift_rows(xe, CONV_W - 1 - k)[H:H + tr], axis=0, keepdims=True) for k in range(CONV_W)]
        dwv = jnp.concatenate(rows + [jnp.zeros((8 - CONV_W, tc), F32)], axis=0)
        dbv = jnp.sum(dcc, axis=0, keepdims=True)

        @pl.when(i == 0)
        def _():
            dw_ref[...] = dwv
            db_ref[...] = dbv

        @pl.when(i > 0)
        def _():
            dw_ref[...] += dwv
            db_ref[...] += dbv

    dx, dw, db = pl.pallas_call(
        body, grid=(C // tc, L // tr),
        in_specs=[pl.BlockSpec((L, tc), lambda j, i: (0, off + j)), pl.BlockSpec((L, tc), lambda j, i: (0, j)),
                  pl.BlockSpec((CONV_W, tc), lambda j, i: (0, j)), pl.BlockSpec((1, tc), lambda j, i: (0, j))],
        out_specs=[pl.BlockSpec((tr, tc), lambda j, i: (i, j)), pl.BlockSpec((8, tc), lambda j, i: (0, j)),
                   pl.BlockSpec((1, tc), lambda j, i: (0, j))],
        out_shape=[jax.ShapeDtypeStruct((L, C), BF16), jax.ShapeDtypeStruct((8, C), F32),
                   jax.ShapeDtypeStruct((1, C), F32)],
        compiler_params=_cp(("parallel", "arbitrary")), name="conv_bwd")(zx, dxbc, conv_w, conv_b)
    return dx, dw[:CONV_W], db


_NN = (((1,), (0,)), ((), ()))


def _pieces(x, n):
    out, r = [], x
    for _ in range(n):
        p = r.astype(BF16)
        out.append(p)
        r = r - p.astype(F32)
    return out


def _dot01(a, b01, n, dims=_NN):
    b = b01.astype(BF16)
    return functools.reduce(lambda u, v: u + v,
                            [lax.dot_general(p, b, dims, preferred_element_type=F32) for p in _pieces(a, n)])


def _dot01_left(a01, b, n, dims=_NN):
    a = a01.astype(BF16)
    return functools.reduce(lambda u, v: u + v,
                            [lax.dot_general(a, p, dims, preferred_element_type=F32) for p in _pieces(b, n)])


def _ssd_common(dtp_ref, dtpT_ref, bias_ref, biasT_ref, alog_ref, alogT_ref, b_ref, c_ref):
    Q = SSD_Q
    dt = _softplus(dtp_ref[...] + bias_ref[...])
    A = -jnp.exp(alog_ref[...])
    row = lax.broadcasted_iota(jnp.int32, (Q, Q), 0)
    col = lax.broadcasted_iota(jnp.int32, (Q, Q), 1)
    causal = row >= col
    tril = causal.astype(F32)
    Kh = dt.shape[1]
    acum = _dot01_left(tril, dt * A, 3)
    eye = (lax.broadcasted_iota(jnp.int32, (Kh, Kh), 0) == lax.broadcasted_iota(jnp.int32, (Kh, Kh), 1)).astype(F32)
    acumT = _dot01_left(eye, acum, 3, dims=(((1,), (1,)), ((), ())))
    Bm = b_ref[...]
    Cm = c_ref[...]
    cb = lax.dot_general(Cm, Bm, (((1,), (1,)), ((), ())), preferred_element_type=F32)
    return dt, A, causal, row, col, acum, acumT, Bm, Cm, cb


def _ssd_in_specs(Q, GP, N, Kh, DI, cmap):
    nb0 = DI // N
    vec = pl.BlockSpec((None, 1, Kh), lambda g, c: (g, 0, 0))
    vecT = pl.BlockSpec((None, Kh, 1), lambda g, c: (g, 0, 0))
    return [pl.BlockSpec((Q, GP), lambda g, c: (cmap(c), g)),
            pl.BlockSpec((Q, N), lambda g, c: (cmap(c), nb0 + g)),
            pl.BlockSpec((Q, N), lambda g, c: (cmap(c), nb0 + SSD_G + g)),
            pl.BlockSpec((None, Q, Kh), lambda g, c: (g, cmap(c), 0)),
            pl.BlockSpec((None, Kh, Q), lambda g, c: (g, 0, cmap(c))),
            vec, vecT, vec, vecT, vec, vecT]


def _hi(a, b01):
    return _dot01(a, b01, 2)


def _ssd_heads(dskT_ref, acum, acumT, dt, Kh):
    Q, P, N = SSD_Q, SSD_P, SSD_N
    GP = Kh * P
    sh_p = P.bit_length() - 1
    seg = lambda shape, dim: lax.shift_right_logical(lax.broadcasted_iota(jnp.int32, shape, dim), sh_p)
    E = (seg((Kh, GP), 1) == lax.broadcasted_iota(jnp.int32, (Kh, GP), 0)).astype(F32)
    ET = (seg((GP, Kh), 0) == lax.broadcasted_iota(jnp.int32, (GP, Kh), 1)).astype(F32)
    a_last = acum[Q - 1:Q, :]
    tail = jnp.exp(a_last - acum)
    eLT = jnp.exp(acumT[:, Q - 1:Q])
    rowseg = seg((GP, N), 0)
    eL_b = jnp.zeros((GP, N), F32)
    for k in range(Kh):
        eL_b = jnp.where(rowseg == k, eLT[k:k + 1, :], eL_b)
    return dict(
        E=E, ET=ET, a_last=a_last, tail=tail, eL_b=eL_b,
        dt_all=_hi(dt, E), ea_all=_hi(jnp.exp(acum), E), tail_all=_hi(tail, E),
        dsk_all=jnp.sum(E * dskT_ref[...], axis=0, keepdims=True))


def _head_chunks(GP):
    CW = min(GP, 128)
    return CW, CW // SSD_P, GP // CW


def _head_mask(Q, CW, kk):
    lane = lax.broadcasted_iota(jnp.int32, (Q, CW), 1)
    return jnp.logical_and(lane >= kk * SSD_P, lane < (kk + 1) * SSD_P)


def ssd_fwd(xbc, dtp_g, dtp_gT, bias_g, bias_gT, alog_g, alog_gT, dsk_g, dsk_gT, DI):
    L = xbc.shape[0]
    Q, P, N, G = SSD_Q, SSD_P, SSD_N, SSD_G
    GP = DI // G
    Kh = GP // P
    nc = L // Q

    CW, hpc, nch = _head_chunks(GP)
    nt = (((1,), (1,)), ((), ()))
    tn = (((0,), (0,)), ((), ()))

    def body(xs_ref, b_ref, c_ref, dtp_ref, dtpT_ref, bias_ref, biasT_ref, alog_ref, alogT_ref, dsk_ref, dskT_ref,
             y_ref, st_ref, state):
        @pl.when(pl.program_id(1) == 0)
        def _():
            state[...] = jnp.zeros(state.shape, F32)

        st_ref[...] = state[...]
        dt, A, causal, row, col, acum, acumT, Bm, Cm, cb = _ssd_common(
            dtp_ref, dtpT_ref, bias_ref, biasT_ref, alog_ref, alogT_ref, b_ref, c_ref)
        hd = _ssd_heads(dskT_ref, acum, acumT, dt, Kh)
        xs = xs_ref[...].astype(F32)
        xdt_all = xs * hd["dt_all"]
        S_all = state[...]
        y_all = (lax.dot_general(Cm, S_all.astype(BF16), nt, preferred_element_type=F32) * hd["ea_all"]
                 + xs * hd["dsk_all"])
        state[...] = S_all * hd["eL_b"] + lax.dot_general(
            (xdt_all * hd["tail_all"]).astype(BF16), Bm, tn, preferred_element_type=F32)
        for ch in range(nch):
            cs = slice(ch * CW, (ch + 1) * CW)
            xc = xdt_all[:, cs]
            acc = y_all[:, cs]
            for kk in range(hpc):
                k = ch * hpc + kk
                decay = jnp.exp(jnp.where(causal, acum[:, k:k + 1] - acumT[k:k + 1, :], -jnp.inf))
                xk = xc if hpc == 1 else jnp.where(_head_mask(Q, CW, kk), xc, 0.0)
                acc = acc + jnp.dot((cb * decay).astype(BF16), xk.astype(BF16), preferred_element_type=F32)
            y_ref[:, cs] = acc.astype(BF16)

    return pl.pallas_call(
        body, grid=(G, nc), in_specs=_ssd_in_specs(Q, GP, N, Kh, DI, lambda c: c),
        out_specs=[pl.BlockSpec((Q, GP), lambda g, c: (c, g)),
                   pl.BlockSpec((None, None, GP, N), lambda g, c: (c, g, 0, 0))],
        out_shape=[jax.ShapeDtypeStruct((L, DI), BF16), jax.ShapeDtypeStruct((nc, G, GP, N), F32)],
        scratch_shapes=[pltpu.VMEM((GP, N), F32)], compiler_params=_cp(("parallel", "arbitrary")),
        name="ssd_fwd")(xbc, xbc, xbc, dtp_g, dtp_gT, bias_g, bias_gT, alog_g, alog_gT, dsk_g, dsk_gT)


def ssd_bwd(xbc, dtp_g, dtp_gT, bias_g, bias_gT, alog_g, alog_gT, dsk_g, dsk_gT, states, dy, DI):
    L = xbc.shape[0]
    Q, P, N, G = SSD_Q, SSD_P, SSD_N, SSD_G
    GP = DI // G
    Kh = GP // P
    nc = L // Q
    rev = lambda c: nc - 1 - c

    CW, hpc, nch = _head_chunks(GP)

    def body(xs_ref, b_ref, c_ref, dtp_ref, dtpT_ref, bias_ref, biasT_ref, alog_ref, alogT_ref, dsk_ref, dskT_ref,
             st_ref, dy_ref, dxs_ref, dB_ref, dC_ref, ddtp_ref, dbias_ref, dalog_ref, dD_ref, dstate):
        ci = pl.program_id(1)

        @pl.when(ci == 0)
        def _():
            dstate[...] = jnp.zeros(dstate.shape, F32)

        dt, A, causal, row, col, acum, acumT, Bm, Cm, cb = _ssd_common(
            dtp_ref, dtpT_ref, bias_ref, biasT_ref, alog_ref, alogT_ref, b_ref, c_ref)
        tn = (((0,), (0,)), ((), ()))
        nt = (((1,), (1,)), ((), ()))
        hd = _ssd_heads(dskT_ref, acum, acumT, dt, Kh)
        ET, tail = hd["ET"], hd["tail"]
        cbT = lax.dot_general(Bm, Cm, nt, preferred_element_type=F32)
        causalT = row <= col
        xs = xs_ref[...].astype(F32)
        xdt_all = xs * hd["dt_all"]
        dyb = dy_ref[...]
        dy_all = dyb.astype(F32)
        S_all = st_ref[...]
        S_b = S_all.astype(BF16)
        dS_all = dstate[...]
        dS_b = dS_all.astype(BF16)
        CS_all = lax.dot_general(Cm, S_b, nt, preferred_element_type=F32)
        dyE_b = (dy_all * hd["ea_all"]).astype(BF16)
        dC_acc = jnp.dot(dyE_b, S_b, preferred_element_type=F32)
        dS_y = lax.dot_general(dyE_b, Cm, tn, preferred_element_type=F32)
        BdS_all = lax.dot_general(Bm, dS_b, nt, preferred_element_type=F32)
        dB_acc = jnp.dot((xdt_all * hd["tail_all"]).astype(BF16), dS_b, preferred_element_type=F32)
        dtail = _hi(xdt_all * BdS_all, ET)
        da_cols = _hi(dy_all * CS_all * hd["ea_all"], ET) - dtail * tail
        dss = _dot01_left(jnp.ones((8, N), F32), _dot01_left(hd["E"], dS_all * S_all, 2), 2, dims=nt)
        da_last = dss[0:1] * jnp.exp(hd["a_last"]) + jnp.sum(dtail * tail, axis=0, keepdims=True)
        rowi = lax.broadcasted_iota(jnp.int32, (Q, Kh), 0)
        da_cols = da_cols + jnp.where(rowi == Q - 1, da_last, 0.0)
        dstate[...] = hd["eL_b"] * dS_all + dS_y
        sum_mg = jnp.zeros((Q, Q), F32)
        sum_mgt = jnp.zeros((Q, Q), F32)
        dacc = jnp.zeros((Q, 128), F32)
        ddt_x = jnp.zeros((Q, Kh), F32)
        lane128 = lax.broadcasted_iota(jnp.int32, (Q, 128), 1)
        for ch in range(nch):
            cs = slice(ch * CW, (ch + 1) * CW)
            dyc = dyb[:, cs]
            xc_b = xdt_all[:, cs].astype(BF16)
            acc = hd["tail_all"][:, cs] * BdS_all[:, cs]
            for kk in range(hpc):
                k = ch * hpc + kk
                a_b = jnp.broadcast_to(acum[:, k:k + 1], (Q, Q))
                a_r = acumT[k:k + 1, :]
                decay = jnp.exp(jnp.where(causal, a_b - a_r, -jnp.inf))
                decayT = jnp.exp(jnp.where(causalT, a_r - a_b, -jnp.inf))
                dyk = dyc if hpc == 1 else jnp.where(_head_mask(Q, CW, kk), dyc, jnp.zeros_like(dyc))
                mg = decay * lax.dot_general(dyk, xc_b, nt, preferred_element_type=F32)
                mgt = decayT * lax.dot_general(xc_b, dyk, nt, preferred_element_type=F32)
                sum_mg = sum_mg + mg
                sum_mgt = sum_mgt + mgt
                onek = jnp.where(lane128 == k, 1.0, 0.0).astype(BF16)
                dk = mg * cb - mgt * cbT
                dk_hi = dk.astype(BF16)
                dk_lo = (dk - dk_hi.astype(F32)).astype(BF16)
                dacc = dacc + (jnp.dot(dk_hi, onek, preferred_element_type=F32)
                               + jnp.dot(dk_lo, onek, preferred_element_type=F32))
                acc = acc + jnp.dot((decayT * cbT).astype(BF16), dyk, preferred_element_type=F32)
            dxs_ref[:, cs] = (acc * hd["dt_all"][:, cs] + dy_all[:, cs] * hd["dsk_all"][:, cs]).astype(BF16)
            ddt_x = ddt_x + _hi(acc * xs[:, cs], ET[cs, :])
        da_cols = da_cols + dacc[:, :Kh]
        dD_row = jnp.sum(_hi(dy_all * xs, ET), axis=0, keepdims=True)
        dB_ref[...] = (dB_acc + jnp.dot(sum_mgt.astype(BF16), Cm, preferred_element_type=F32)).astype(BF16)
        dC_ref[...] = (dC_acc + jnp.dot(sum_mg.astype(BF16), Bm, preferred_element_type=F32)).astype(BF16)
        triu = (row <= col).astype(F32)
        ddtA = _dot01_left(triu, da_cols, 3)
        ddt = ddt_x + ddtA * A
        dpre = ddt * _sigmoid(dtp_ref[...] + bias_ref[...])
        ddtp_ref[...] = dpre
        dbias_v = jnp.sum(dpre, axis=0, keepdims=True)
        dalog_v = jnp.sum(ddtA * dt, axis=0, keepdims=True) * A

        @pl.when(ci == 0)
        def _():
            dbias_ref[...] = dbias_v
            dalog_ref[...] = dalog_v
            dD_ref[...] = dD_row

        @pl.when(ci > 0)
        def _():
            dbias_ref[...] += dbias_v
            dalog_ref[...] += dalog_v
            dD_ref[...] += dD_row

    vec_o = pl.BlockSpec((None, 1, Kh), lambda g, c: (g, 0, 0))
    return pl.pallas_call(
        body, grid=(G, nc),
        in_specs=_ssd_in_specs(Q, GP, N, Kh, DI, rev)
        + [pl.BlockSpec((None, None, GP, N), lambda g, c: (rev(c), g, 0, 0)),
           pl.BlockSpec((Q, GP), lambda g, c: (rev(c), g))],
        out_specs=[pl.BlockSpec((Q, GP), lambda g, c: (rev(c), g)), pl.BlockSpec((Q, N), lambda g, c: (rev(c), g)),
                   pl.BlockSpec((Q, N), lambda g, c: (rev(c), g)),
                   pl.BlockSpec((None, Q, Kh), lambda g, c: (g, rev(c), 0)), vec_o, vec_o, vec_o],
        out_shape=[jax.ShapeDtypeStruct((L, DI), BF16), jax.ShapeDtypeStruct((L, G * N), BF16),
                   jax.ShapeDtypeStruct((L, G * N), BF16), jax.ShapeDtypeStruct((G, L, Kh), F32)]
        + [jax.ShapeDtypeStruct((G, 1, Kh), F32)] * 3,
        scratch_shapes=[pltpu.VMEM((GP, N), F32)], compiler_params=_cp(("parallel", "arbitrary")),
        name="ssd_bwd")(xbc, xbc, xbc, dtp_g, dtp_gT, bias_g, bias_gT, alog_g, alog_gT, dsk_g, dsk_gT, states, dy)


def _rms_groups(y2, ng_ref, DI):
    S = DI // SSD_G
    for g in range(SSD_G):
        gs = slice(g * S, (g + 1) * S)
        seg = y2[:, gs]
        r = lax.rsqrt(jnp.mean(seg * seg, axis=-1, keepdims=True) + RMS_EPS)
        yield gs, seg * r, r, ng_ref[:, gs]


def rms_gate_fwd(y, zx, norm_g):
    L, DI = y.shape
    tr = _tile(L, 256, 16)

    def body(y_ref, z_ref, ng_ref, o_ref):
        y2 = y_ref[...].astype(F32) * _silu(z_ref[...].astype(F32))
        for gs, yh, _, ng in _rms_groups(y2, ng_ref, DI):
            o_ref[:, gs] = (yh * ng).astype(BF16)

    return pl.pallas_call(
        body, grid=(L // tr,), in_specs=_row_specs(tr, [DI, DI]) + [_vec_spec(DI)], out_specs=_row_specs(tr, [DI])[0],
        out_shape=jax.ShapeDtypeStruct((L, DI), BF16), compiler_params=_cp(("parallel",)),
        name="rms_gate_fwd")(y, zx, norm_g)


def rms_gate_bwd(dyn, y, zx, norm_g):
    L, DI = y.shape
    tr = _tile(L, 256, 16)

    def body(dyn_ref, y_ref, z_ref, ng_ref, dy_ref, dz_ref, dng_ref):
        i = pl.program_id(0)
        yv = y_ref[...].astype(F32)
        zv = z_ref[...].astype(F32)
        sz = _silu(zv)
        dsz = _dsilu(zv)
        dynv = dyn_ref[...].astype(F32)
        for gs, yh, r, ng in _rms_groups(yv * sz, ng_ref, DI):
            dyh = dynv[:, gs] * ng
            dy2 = r * (dyh - yh * jnp.mean(dyh * yh, axis=-1, keepdims=True))
            dy_ref[:, gs] = (dy2 * sz[:, gs]).astype(BF16)
            dz_ref[:, gs] = (dy2 * yv[:, gs] * dsz[:, gs]).astype(BF16)
            s = jnp.sum(dynv[:, gs] * yh, axis=0, keepdims=True)

            @pl.when(i == 0)
            def _():
                dng_ref[:, gs] = s

            @pl.when(i > 0)
            def _():
                dng_ref[:, gs] += s

    return pl.pallas_call(
        body, grid=(L // tr,), in_specs=_row_specs(tr, [DI, DI, DI]) + [_vec_spec(DI)],
        out_specs=_row_specs(tr, [DI, DI]) + [_vec_spec(DI)],
        out_shape=[jax.ShapeDtypeStruct((L, DI), BF16)] * 2 + [jax.ShapeDtypeStruct((1, DI), F32)],
        compiler_params=_cp(("arbitrary",)), name="rms_gate_bwd")(dyn, y, zx, norm_g)


def _alibi_slope(gi, h):
    n = len(DIL_PATTERNS) * DIL_H
    return float(2.0 ** (-8.0 * (gi * DIL_H + h + 1) / n))


def _attn_masks():
    qi = lax.broadcasted_iota(jnp.int32, (DIL_BLK, DIL_BLK), 0)
    kj = lax.broadcasted_iota(jnp.int32, (DIL_BLK, DIL_BLK), 1)
    dcur = (qi - kj).astype(F32)
    return dcur, qi >= kj, dcur + float(DIL_BLK), kj >= qi


def _dil_cols(arr, col0, d):
    HW = DIL_H * DIL_E
    if d == 1:
        return arr, arr.shape[1] // HW, col0 // HW
    return arr[:, col0:col0 + HW].reshape(arr.shape[0] // d, d * HW), 1, 0


def attn_fwd(qz, kv, gi):
    window, d = DIL_PATTERNS[gi]
    assert window // d == DIL_BLK
    L, QZ = qz.shape
    KV = kv.shape[1]
    HW = DIL_H * DIL_E
    M = L // d
    nb = M // DIL_BLK
    nq, nkv = QZ // HW, KV // HW
    scale = DIL_E ** -0.5
    nt = (((1,), (1,)), ((), ()))

    def body(q_ref, kp_ref, kc_ref, vp_ref, vc_ref, o_ref, lse_ref):
        n = pl.program_id(1)
        dcur, vcur, dprev, vprev0 = _attn_masks()
        vprev = jnp.logical_and(vprev0, n > 0)
        lane = lax.broadcasted_iota(jnp.int32, (DIL_BLK, 128), 1)
        lse_acc = jnp.zeros((DIL_BLK, 128), F32)
        for h in range(DIL_H):
            hs = slice(h * DIL_E, (h + 1) * DIL_E)
            sl = _alibi_slope(gi, h) * d
            q = q_ref[:, hs]
            s_c = lax.dot_general(q, kc_ref[:, hs], nt, preferred_element_type=F32) * scale - sl * dcur
            s_p = lax.dot_general(q, kp_ref[:, hs], nt, preferred_element_type=F32) * scale - sl * dprev
            s_c = jnp.where(vcur, s_c, -jnp.inf)
            s_p = jnp.where(vprev, s_p, -jnp.inf)
            m = jnp.maximum(jnp.max(s_c, axis=-1, keepdims=True), jnp.max(s_p, axis=-1, keepdims=True))
            p_c = jnp.exp(s_c - m)
            p_p = jnp.exp(s_p - m)
            den = jnp.sum(p_c, axis=-1, keepdims=True) + jnp.sum(p_p, axis=-1, keepdims=True)
            o = (jnp.dot(p_c.astype(BF16), vc_ref[:, hs], preferred_element_type=F32)
                 + jnp.dot(p_p.astype(BF16), vp_ref[:, hs], preferred_element_type=F32)) / den
            o_ref[:, hs] = o.astype(BF16)
            lse_acc = jnp.where(lane == h, m + jnp.log(den), lse_acc)
        lse_ref[...] = lse_acc

    blk = (DIL_BLK, HW)
    prev = lambda n: jnp.maximum(n - 1, 0)
    qv, qn, qo = _dil_cols(qz, gi * HW, d)
    kv_, kn, ko = _dil_cols(kv, gi * HW, d)
    vv, vn, vo = _dil_cols(kv, (nkv // 2 + gi) * HW, d)
    o, lse = pl.pallas_call(
        body, grid=(d, nb),
        in_specs=[pl.BlockSpec(blk, lambda r, n: (n, r * qn + qo)),
                  pl.BlockSpec(blk, lambda r, n: (prev(n), r * kn + ko)),
                  pl.BlockSpec(blk, lambda r, n: (n, r * kn + ko)),
                  pl.BlockSpec(blk, lambda r, n: (prev(n), r * vn + vo)),
                  pl.BlockSpec(blk, lambda r, n: (n, r * vn + vo))],
        out_specs=[pl.BlockSpec(blk, lambda r, n: (n, r)), pl.BlockSpec((DIL_BLK, 128), lambda r, n: (n, r))],
        out_shape=[jax.ShapeDtypeStruct((M, d * HW), BF16), jax.ShapeDtypeStruct((M, d * 128), F32)],
        compiler_params=_cp(("parallel", "parallel")), name=f"attn_fwd_{gi}")(qv, kv_, kv_, vv, vv)
    return o.reshape(L, HW), lse.reshape(L, 128)


def attn_bwd(qz, kv, do, lse, dpr, gi):
    window, d = DIL_PATTERNS[gi]
    L, QZ = qz.shape
    KV = kv.shape[1]
    HW = DIL_H * DIL_E
    M = L // d
    nb = M // DIL_BLK
    nq, nkv = QZ // HW, KV // HW
    scale = DIL_E ** -0.5
    nt = (((1,), (1,)), ((), ()))
    tn = (((0,), (0,)), ((), ()))

    def body(q0_ref, q1_ref, k_ref, v_ref, do0_ref, do1_ref, l0_ref, l1_ref, r0_ref, r1_ref,
             dq_ref, dk_ref, dv_ref, carry):
        n = pl.program_id(1)

        @pl.when(n == 0)
        def _():
            carry[...] = jnp.zeros(carry.shape, F32)

        dcur, vcur, dprev, vprev0 = _attn_masks()
        vprev = jnp.logical_and(vprev0, n < nb - 1)
        for h in range(DIL_H):
            hs = slice(h * DIL_E, (h + 1) * DIL_E)
            sl = _alibi_slope(gi, h) * d
            kh = k_ref[:, hs]
            vh = v_ref[:, hs]
            q0, q1 = q0_ref[:, hs], q1_ref[:, hs]
            do0, do1 = do0_ref[:, hs], do1_ref[:, hs]
            s0 = lax.dot_general(q0, kh, nt, preferred_element_type=F32) * scale - sl * dcur
            p0 = jnp.exp(jnp.where(vcur, s0 - l0_ref[:, h:h + 1], -jnp.inf))
            ds0 = p0 * (lax.dot_general(do0, vh, nt, preferred_element_type=F32) - r0_ref[:, h:h + 1])
            s1 = lax.dot_general(q1, kh, nt, preferred_element_type=F32) * scale - sl * dprev
            p1 = jnp.exp(jnp.where(vprev, s1 - l1_ref[:, h:h + 1], -jnp.inf))
            ds1 = p1 * (lax.dot_general(do1, vh, nt, preferred_element_type=F32) - r1_ref[:, h:h + 1])
            ds0_b = (ds0 * scale).astype(BF16)
            ds1_b = (ds1 * scale).astype(BF16)
            dv = (lax.dot_general(p0.astype(BF16), do0, tn, preferred_element_type=F32)
                  + lax.dot_general(p1.astype(BF16), do1, tn, preferred_element_type=F32))
            dk = (lax.dot_general(ds0_b, q0, tn, preferred_element_type=F32)
                  + lax.dot_general(ds1_b, q1, tn, preferred_element_type=F32))
            dv_ref[:, hs] = dv.astype(BF16)
            dk_ref[:, hs] = dk.astype(BF16)
            dq_ref[:, hs] = (carry[:, hs] + jnp.dot(ds0_b, kh, preferred_element_type=F32)).astype(BF16)
            carry[:, hs] = jnp.dot(ds1_b, kh, preferred_element_type=F32)

    blk = (DIL_BLK, HW)
    sblk = (DIL_BLK, 128)
    nxt = lambda n: jnp.minimum(n + 1, nb - 1)
    qv, qn, qo = _dil_cols(qz, gi * HW, d)
    kv_, kn, ko = _dil_cols(kv, gi * HW, d)
    vv, vn, vo = _dil_cols(kv, (nkv // 2 + gi) * HW, d)
    dov = do.reshape(M, d * HW)
    lv = lse.reshape(M, d * 128)
    rv = dpr.reshape(M, d * 128)
    outs = pl.pallas_call(
        body, grid=(d, nb),
        in_specs=[pl.BlockSpec(blk, lambda r, n: (n, r * qn + qo)), pl.BlockSpec(blk, lambda r, n: (nxt(n), r * qn + qo)),
                  pl.BlockSpec(blk, lambda r, n: (n, r * kn + ko)),
                  pl.BlockSpec(blk, lambda r, n: (n, r * vn + vo)),
                  pl.BlockSpec(blk, lambda r, n: (n, r)), pl.BlockSpec(blk, lambda r, n: (nxt(n), r)),
                  pl.BlockSpec(sblk, lambda r, n: (n, r)), pl.BlockSpec(sblk, lambda r, n: (nxt(n), r)),
                  pl.BlockSpec(sblk, lambda r, n: (n, r)), pl.BlockSpec(sblk, lambda r, n: (nxt(n), r))],
        out_specs=[pl.BlockSpec(blk, lambda r, n: (n, r))] * 3,
        out_shape=[jax.ShapeDtypeStruct((M, d * HW), BF16)] * 3,
        scratch_shapes=[pltpu.VMEM(blk, F32)], compiler_params=_cp(("parallel", "arbitrary")),
        name=f"attn_bwd_{gi}")(qv, qv, kv_, vv, dov, dov, lv, lv, rv, rv)
    return [t.reshape(L, HW) for t in outs]


def _merge_weights(l_refs, h):
    ls = [r[:, h:h + 1] for r in l_refs]
    mx = functools.reduce(jnp.maximum, ls)
    es = [jnp.exp(l - mx) for l in ls]
    den = functools.reduce(lambda a, b: a + b, es)
    return [e / den for e in es]


def merge_fwd(os_, lses, qz):
    L, HW = os_[0].shape
    tr = _tile(L, 256, 16)
    ng = len(os_)
    zblk = qz.shape[1] // HW - 1

    def body(*refs):
        o_refs, l_refs, z_ref, out_ref = refs[:ng], refs[ng:2 * ng], refs[2 * ng], refs[2 * ng + 1]
        for h in range(DIL_H):
            hs = slice(h * DIL_E, (h + 1) * DIL_E)
            ws = _merge_weights(l_refs, h)
            om = functools.reduce(lambda a, b: a + b, [w * o[:, hs].astype(F32) for w, o in zip(ws, o_refs)])
            out_ref[:, hs] = (om * _silu(z_ref[:, hs].astype(F32))).astype(BF16)

    return pl.pallas_call(
        body, grid=(L // tr,),
        in_specs=_row_specs(tr, [HW] * ng + [128] * ng) + [pl.BlockSpec((tr, HW), lambda i: (i, zblk))],
        out_specs=_row_specs(tr, [HW])[0], out_shape=jax.ShapeDtypeStruct((L, HW), BF16),
        compiler_params=_cp(("parallel",)), name="merge_fwd")(*os_, *lses, qz)


def merge_bwd(dgated, os_, lses, qz):
    L, HW = os_[0].shape
    tr = _tile(L, 256, 16)
    ng = len(os_)
    zblk = qz.shape[1] // HW - 1

    def body(*refs):
        dg_ref = refs[0]
        o_refs, l_refs, z_ref = refs[1:1 + ng], refs[1 + ng:1 + 2 * ng], refs[1 + 2 * ng]
        outs = refs[2 + 2 * ng:]
        do_refs, dpr_refs, dz_ref = outs[:ng], outs[ng:2 * ng], outs[2 * ng]
        lane = lax.broadcasted_iota(jnp.int32, (tr, 128), 1)
        accs = [jnp.zeros((tr, 128), F32) for _ in range(ng)]
        for h in range(DIL_H):
            hs = slice(h * DIL_E, (h + 1) * DIL_E)
            ws = _merge_weights(l_refs, h)
            ov = [o[:, hs].astype(F32) for o in o_refs]
            om = functools.reduce(lambda a, b: a + b, [w * o for w, o in zip(ws, ov)])
            zv = z_ref[:, hs].astype(F32)
            dgv = dg_ref[:, hs].astype(F32)
            dom = dgv * _silu(zv)
            dz_ref[:, hs] = (dgv * om * _dsilu(zv)).astype(BF16)
            dws = [jnp.sum(dom * o, axis=-1, keepdims=True) for o in ov]
            dwbar = functools.reduce(lambda a, b: a + b, [w * dw for w, dw in zip(ws, dws)])
            for g in range(ng):
                do_refs[g][:, hs] = (ws[g] * dom).astype(BF16)
                accs[g] = jnp.where(lane == h, ws[g] * dwbar, accs[g])
        for g in range(ng):
            dpr_refs[g][...] = accs[g]

    outs = pl.pallas_call(
        body, grid=(L // tr,),
        in_specs=_row_specs(tr, [HW] * (1 + ng) + [128] * ng) + [pl.BlockSpec((tr, HW), lambda i: (i, zblk))],
        out_specs=_row_specs(tr, [HW] * ng + [128] * ng + [HW]),
        out_shape=[jax.ShapeDtypeStruct((L, HW), BF16)] * ng + [jax.ShapeDtypeStruct((L, 128), F32)] * ng
        + [jax.ShapeDtypeStruct((L, HW), BF16)],
        compiler_params=_cp(("parallel",)), name="merge_bwd")(dgated, *os_, *lses, qz)
    return outs[:ng], outs[ng:2 * ng], outs[2 * ng]


def ada_fwd(c8, ada_w):
    nl, D, Ws = ada_w.shape
    tn = _tile(Ws, 512)

    def body(c_ref, w_ref, o_ref):
        o_ref[...] = jnp.dot(_silu(c_ref[...]), w_ref[...], precision=lax.Precision.HIGHEST,
                             preferred_element_type=F32)

    return pl.pallas_call(
        body, grid=(nl, Ws // tn),
        in_specs=[pl.BlockSpec((N_DEV, D), lambda l, j: (0, 0)), pl.BlockSpec((None, D, tn), lambda l, j: (l, 0, j))],
        out_specs=pl.BlockSpec((None, N_DEV, tn), lambda l, j: (l, 0, j)),
        out_shape=jax.ShapeDtypeStruct((nl, N_DEV, Ws), F32), compiler_params=_cp(("parallel", "parallel")),
        name="ada_fwd")(c8, ada_w)


def ada_wgrad(c8t, dmod):
    nl, _, Ws = dmod.shape
    D = c8t.shape[0]
    tm = _tile(D, 512, 8)

    def body(c_ref, d_ref, o_ref):
        sc = _silu(c_ref[...])
        acc = sc[:, 0:1] * d_ref[0:1, :]
        for e in range(1, N_DEV):
            acc = acc + sc[:, e:e + 1] * d_ref[e:e + 1, :]
        o_ref[...] = acc

    return pl.pallas_call(
        body, grid=(nl, D // tm),
        in_specs=[pl.BlockSpec((tm, N_DEV), lambda l, i: (i, 0)), pl.BlockSpec((None, N_DEV, Ws), lambda l, i: (l, 0, 0))],
        out_specs=pl.BlockSpec((None, tm, Ws), lambda l, i: (l, i, 0)),
        out_shape=jax.ShapeDtypeStruct((nl, D, Ws), F32), compiler_params=_cp(("parallel", "parallel")),
        name="ada_wgrad")(c8t, dmod)


def adamw(w, g, m, v, name):
    R, C = w.shape
    tr = _tile(R, 256, 8)
    c1 = 1.0 - ADAM_B1 ** ADAM_STEP
    c2 = 1.0 - ADAM_B2 ** ADAM_STEP

    def body(w_ref, g_ref, m_ref, v_ref, d_ref, nm_ref, nv_ref):
        gv = g_ref[...]
        nm = ADAM_B1 * m_ref[...] + (1.0 - ADAM_B1) * gv
        nv = ADAM_B2 * v_ref[...] + (1.0 - ADAM_B2) * (gv * gv)
        nm_ref[...] = nm
        nv_ref[...] = nv
        d_ref[...] = -ADAM_LR * ((nm / c1) / (jnp.sqrt(nv / c2) + ADAM_EPS) + ADAM_WD * w_ref[...])

    return pl.pallas_call(
        body, grid=(R // tr,), in_specs=_row_specs(tr, [C] * 4), out_specs=_row_specs(tr, [C] * 3),
        out_shape=[jax.ShapeDtypeStruct((R, C), F32)] * 3, compiler_params=_cp(("parallel",)), name=name)(w, g, m, v)


def sum_leading(t, name, out_dtype=F32):
    S, R, C = t.shape
    tr = _tile(R, 256, 16)

    def body(t_ref, o_ref):
        acc = t_ref[0].astype(F32)
        for s in range(1, S):
            acc = acc + t_ref[s].astype(F32)
        o_ref[...] = acc.astype(out_dtype)

    return pl.pallas_call(
        body, grid=(R // tr,), in_specs=[pl.BlockSpec((S, tr, C), lambda i: (0, i, 0))],
        out_specs=pl.BlockSpec((tr, C), lambda i: (i, 0)), out_shape=jax.ShapeDtypeStruct((R, C), out_dtype),
        compiler_params=_cp(("parallel",)), name=name)(t)


def add_half(g, a, core, name):
    S, R, C = g.shape
    h = R // 2
    tr = _tile(h, 256, 16)
    nb = h // tr

    def body(core_ref, g_ref, a_ref, o_ref):
        o_ref[...] = (g_ref[...].astype(F32) + a_ref[...].astype(F32)).astype(BF16)

    return pl.pallas_call(
        body,
        grid_spec=pltpu.PrefetchScalarGridSpec(
            num_scalar_prefetch=1, grid=(S, nb),
            in_specs=[pl.BlockSpec((None, tr, C), lambda s, i, core_ref: (s, core_ref[0] * nb + i, 0)),
                      pl.BlockSpec((None, tr, C), lambda s, i, core_ref: (s, i, 0))],
            out_specs=pl.BlockSpec((None, tr, C), lambda s, i, core_ref: (s, i, 0))),
        out_shape=jax.ShapeDtypeStruct((S, h, C), BF16), compiler_params=_cp(("parallel", "parallel")),
        name=name)(core, g, a)


def sum_partials(own, landed, chip, name):
    _, h, C = own.shape
    tr = _tile(h, 256, 16)

    def body(chip_ref, own_ref, l_ref, o_ref):
        acc = own_ref[...].astype(F32)
        for j in range(3):
            acc = acc + l_ref[j].astype(F32)
        o_ref[...] = acc

    return pl.pallas_call(
        body,
        grid_spec=pltpu.PrefetchScalarGridSpec(
            num_scalar_prefetch=1, grid=(h // tr,),
            in_specs=[pl.BlockSpec((None, tr, C), lambda i, chip_ref: (chip_ref[0], i, 0)),
                      pl.BlockSpec((3, tr, C), lambda i, chip_ref: (0, i, 0))],
            out_specs=pl.BlockSpec((tr, C), lambda i, chip_ref: (i, 0))),
        out_shape=jax.ShapeDtypeStruct((h, C), F32), compiler_params=_cp(("parallel",)), name=name)(chip, own, landed)


def adamw_halves(w, g_mine, g_theirs, m, v, core, name):
    R, C = w.shape
    h = R // 2
    tr = _tile(h, 256, 8)
    nbh = h // tr
    c1 = 1.0 - ADAM_B1 ** ADAM_STEP
    c2 = 1.0 - ADAM_B2 ** ADAM_STEP

    def body(core_ref, w_ref, gm_ref, gt_ref, m_ref, v_ref, g_ref, d_ref, nm_ref, nv_ref):
        mine = (pl.program_id(0) // nbh) == core_ref[0]
        gv = jnp.where(mine, gm_ref[...], gt_ref[...])
        g_ref[...] = gv
        nm = ADAM_B1 * m_ref[...] + (1.0 - ADAM_B1) * gv
        nv = ADAM_B2 * v_ref[...] + (1.0 - ADAM_B2) * (gv * gv)
        nm_ref[...] = nm
        nv_ref[...] = nv
        d_ref[...] = -ADAM_LR * ((nm / c1) / (jnp.sqrt(nv / c2) + ADAM_EPS) + ADAM_WD * w_ref[...])

    full = pl.BlockSpec((tr, C), lambda i, core_ref: (i, 0))
    halfspec = pl.BlockSpec((tr, C), lambda i, core_ref: (i % nbh, 0))
    return pl.pallas_call(
        body,
        grid_spec=pltpu.PrefetchScalarGridSpec(
            num_scalar_prefetch=1, grid=(2 * nbh,), in_specs=[full, halfspec, halfspec, full, full],
            out_specs=[full] * 4),
        out_shape=[jax.ShapeDtypeStruct((R, C), F32)] * 4, compiler_params=_cp(("parallel",)),
        name=name)(core, w, g_mine, g_theirs, m, v)


_ANY = pl.BlockSpec(memory_space=pl.ANY)


def _place():
    x, y, c = lax.axis_index("x"), lax.axis_index("y"), lax.axis_index("c")
    chips = [(1 - x, y), (x, 1 - y), (1 - x, 1 - y)]
    return x, y, c, chips


def allgather_small(v, name):
    R, W = v.shape

    def body(x_ref, out_ref, send_sems, recv_sems, local_sem):
        x, y, c, chips = _place()
        me, sibling = (x, y, c), (x, y, 1 - c)

        def rows(px, py, pc):
            return out_ref.at[pl.ds((4 * px + 2 * py + pc) * R, R), :]

        def copy(k, block, to, src=None):
            return pltpu.make_async_remote_copy(
                src_ref=rows(*block) if src is None else src, dst_ref=rows(*block),
                send_sem=send_sems.at[k], recv_sem=recv_sems.at[k], device_id=to, device_id_type=MESH)

        mine = pltpu.make_async_copy(x_ref, rows(*me), local_sem)
        mine.start()
        first = [copy(0, me, sibling, src=x_ref)]
        first += [copy(1 + j, me, (*chip, c), src=x_ref) for j, chip in enumerate(chips)]
        for cp in first:
            cp.start()
        passed = [copy(4 + j, (*chip, c), sibling) for j, chip in enumerate(chips)]
        for j, chip in enumerate(chips):
            copy(1 + j, (*chip, c), me).wait_recv()
            passed[j].start()
        copy(0, sibling, me).wait_recv()
        for j, chip in enumerate(chips):
            copy(4 + j, (*chip, 1 - c), me).wait_recv()
        for cp in first + passed:
            cp.wait_send()
        mine.wait()

    return pl.pallas_call(
        body, out_shape=jax.ShapeDtypeStruct((N_DEV * R, W), v.dtype),
        in_specs=[pl.BlockSpec(memory_space=pltpu.VMEM)], out_specs=pl.BlockSpec(memory_space=pltpu.VMEM),
        scratch_shapes=[pltpu.SemaphoreType.DMA((7,)), pltpu.SemaphoreType.DMA((7,)), pltpu.SemaphoreType.DMA],
        name=name)(v)


def allgather_weights(shards, name="allgather_weights"):
    n = len(shards)

    def body(*refs):
        ins, outs = refs[:n], refs[n:2 * n]
        send_sems, recv_sems = refs[2 * n:]
        x, y, c, chips = _place()
        p = 2 * x + y
        sibling = (x, y, 1 - c)

        def half(i, chip_id, core, ref=None):
            r = outs[i].at[chip_id] if ref is None else ref
            return r.at[core]

        def copy(i, k, chip_id, core, to, src=None):
            return pltpu.make_async_remote_copy(
                src_ref=half(i, chip_id, core) if src is None else src, dst_ref=half(i, chip_id, core),
                send_sem=send_sems.at[6 * i + k], recv_sem=recv_sems.at[6 * i + k], device_id=to, device_id_type=MESH)

        first = [copy(i, j, p, c, (*chip, c), src=half(i, p, c, ref=ins[i]))
                 for i in range(n) for j, chip in enumerate(chips)]
        for cp in first:
            cp.start()
        passed = []
        for i in range(n):
            for j, (cx, cy) in enumerate(chips):
                copy(i, j, 2 * cx + cy, c, sibling).wait_recv()
                fw = copy(i, 3 + j, 2 * cx + cy, c, sibling)
                fw.start()
                passed.append(fw)
        for i in range(n):
            for j, (cx, cy) in enumerate(chips):
                copy(i, 3 + j, 2 * cx + cy, 1 - c, sibling).wait_recv()
        for cp in first + passed:
            cp.wait_send()

    split = [s.reshape(2, s.shape[0] // 2, s.shape[1]) for s in shards]
    outs = pl.pallas_call(
        body, out_shape=[jax.ShapeDtypeStruct((N_CHIPS,) + s.shape, s.dtype) for s in split],
        in_specs=[_ANY] * n, out_specs=[_ANY] * n,
        scratch_shapes=[pltpu.SemaphoreType.DMA((6 * n,)), pltpu.SemaphoreType.DMA((6 * n,))],
        name=name)(*split)
    chip = 2 * lax.axis_index("x") + lax.axis_index("y")
    return [lax.dynamic_update_index_in_dim(o, s, chip, 0).reshape((N_CHIPS,) + sh.shape)
            for o, s, sh in zip(outs, split, shards)]


_HBM = pl.BlockSpec(memory_space=pltpu.HBM)
_SEM = pl.BlockSpec(memory_space=pltpu.SEMAPHORE)
_EFFECT = pltpu.SideEffectType.DATAFLOW_SIDE_EFFECTING


def _chip_copies(kind, srcs, lands, send_sems, recv_sems):
    x, y, c, chips = _place()
    p = 2 * x + y
    cps = []
    for i in range(len(srcs)):
        for j, (cx, cy) in enumerate(chips):
            if kind == "gather":
                src, dst = srcs[i].at[c], lands[i].at[p, c]
            else:
                src, dst = srcs[i].at[2 * cx + cy], lands[i].at[j]
            cps.append(pltpu.make_async_remote_copy(
                src_ref=src, dst_ref=dst, send_sem=send_sems.at[3 * i + j], recv_sem=recv_sems.at[3 * i + j],
                device_id=(cx, cy, c), device_id_type=MESH))
    return cps


def split_start(kind, srcs, land_shapes, after, name):
    n = len(srcs)

    def body(*refs):
        src_refs, land_refs = refs[:n], refs[n:2 * n]
        send_sems, recv_sems = refs[2 * n + 1], refs[2 * n + 2]
        token = refs[-1]
        for cp in _chip_copies(kind, src_refs, land_refs, send_sems, recv_sems):
            cp.start()
        token[...] = jnp.zeros_like(token)

    lands = [pltpu.with_memory_space_constraint(lax.empty(s, BF16), pltpu.HBM) for s in land_shapes]
    outs = pl.pallas_call(
        body, name=name,
        out_shape=(pltpu.SemaphoreType.DMA((3 * n,)), pltpu.SemaphoreType.DMA((3 * n,)),
                   *[pltpu.HBM(s.shape, s.dtype) for s in srcs], *[pltpu.HBM(s, BF16) for s in land_shapes],
                   jax.ShapeDtypeStruct((8, 128), F32)),
        in_specs=[_HBM] * (2 * n) + [_ANY],
        out_specs=(_SEM, _SEM, *([_HBM] * (2 * n)), pl.BlockSpec(memory_space=pltpu.VMEM)),
        input_output_aliases={i: 2 + i for i in range(2 * n)},
        compiler_params=pltpu.CompilerParams(has_side_effects=_EFFECT),
    )(*[pltpu.with_memory_space_constraint(s, pltpu.HBM) for s in srcs], *lands, after)
    return outs[0], outs[1], outs[2:2 + n], outs[2 + n:2 + 2 * n], outs[-1]


def split_wait(kind, send_sems, recv_sems, srcs, lands, after, name):
    n = len(srcs)

    def body(*refs):
        src_refs, land_refs = refs[:n], refs[n:2 * n]
        ssem, rsem = refs[2 * n], refs[2 * n + 1]
        for cp in _chip_copies(kind, src_refs, land_refs, ssem, rsem):
            cp.wait_send()
            cp.wait_recv()

    outs = pl.pallas_call(
        body, name=name,
        out_shape=[pltpu.HBM(s.shape, s.dtype) for s in srcs] + [pltpu.HBM(s.shape, s.dtype) for s in lands],
        in_specs=[_HBM] * (2 * n) + [_SEM, _SEM, _ANY], out_specs=[_HBM] * (2 * n),
        input_output_aliases={i: i for i in range(2 * n)},
        compiler_params=pltpu.CompilerParams(has_side_effects=_EFFECT),
    )(*srcs, *lands, send_sems, recv_sems, after)
    return outs[:n], outs[n:]


def pass_to_sibling(lands):
    n = len(lands)

    def body(*refs):
        ins, outs = refs[:n], refs[n:2 * n]
        send_sems, recv_sems = refs[2 * n:]
        x, y, c, chips = _place()
        cps = []
        for i in range(n):
            for j, (cx, cy) in enumerate(chips):
                blk = outs[i].at[2 * cx + cy, c]
                cps.append(pltpu.make_async_remote_copy(
                    src_ref=ins[i].at[2 * cx + cy, c], dst_ref=blk, send_sem=send_sems.at[3 * i + j],
                    recv_sem=recv_sems.at[3 * i + j], device_id=(x, y, 1 - c), device_id_type=MESH))
        for cp in cps:
            cp.start()
        for cp in cps:
            cp.wait()

    return pl.pallas_call(
        body, out_shape=[jax.ShapeDtypeStruct(t.shape, t.dtype) for t in lands], in_specs=[_ANY] * n,
        out_specs=[_ANY] * n, input_output_aliases={i: i for i in range(n)},
        scratch_shapes=[pltpu.SemaphoreType.DMA((3 * n,)), pltpu.SemaphoreType.DMA((3 * n,))],
        name="ag_pass_to_sibling")(*lands)


def exchange_halves_to_sibling(gs, name):
    n = len(gs)

    def body(*refs):
        ins, outs = refs[:n], refs[n:2 * n]
        send_sems, recv_sems = refs[2 * n:]
        x, y, c, _ = _place()
        cps = []
        for i in range(n):
            h = ins[i].shape[1] // 2
            cps.append(pltpu.make_async_remote_copy(
                src_ref=ins[i].at[:, pl.ds((1 - c) * h, h), :], dst_ref=outs[i],
                send_sem=send_sems.at[i], recv_sem=recv_sems.at[i], device_id=(x, y, 1 - c), device_id_type=MESH))
        for cp in cps:
            cp.start()
        for cp in cps:
            cp.wait()

    return pl.pallas_call(
        body, out_shape=[jax.ShapeDtypeStruct((g.shape[0], g.shape[1] // 2, g.shape[2]), g.dtype) for g in gs],
        in_specs=[_ANY] * n, out_specs=[_ANY] * n,
        scratch_shapes=[pltpu.SemaphoreType.DMA((n,)), pltpu.SemaphoreType.DMA((n,))],
        name=name)(*gs)


def scatter_to_chips(ps, name):
    n = len(ps)

    def body(*refs):
        ins, outs = refs[:n], refs[n:2 * n]
        send_sems, recv_sems = refs[2 * n:]
        x, y, c, chips = _place()
        cps = []
        for i in range(n):
            for j, (cx, cy) in enumerate(chips):
                cps.append(pltpu.make_async_remote_copy(
                    src_ref=ins[i].at[2 * cx + cy], dst_ref=outs[i].at[j], send_sem=send_sems.at[3 * i + j],
                    recv_sem=recv_sems.at[3 * i + j], device_id=(cx, cy, c), device_id_type=MESH))
        for cp in cps:
            cp.start()
        for cp in cps:
            cp.wait()

    return pl.pallas_call(
        body, out_shape=[jax.ShapeDtypeStruct((3,) + t.shape[1:], t.dtype) for t in ps],
        in_specs=[_ANY] * n, out_specs=[_ANY] * n,
        scratch_shapes=[pltpu.SemaphoreType.DMA((3 * n,)), pltpu.SemaphoreType.DMA((3 * n,))],
        name=name)(*ps)


def join_halves(rs, name):
    n = len(rs)

    def body(*refs):
        ins, outs = refs[:n], refs[n:2 * n]
        send_sems, recv_sems = refs[2 * n:]
        x, y, c, _ = _place()
        cps = [pltpu.make_async_remote_copy(
            src_ref=ins[i], dst_ref=outs[i], send_sem=send_sems.at[i], recv_sem=recv_sems.at[i],
            device_id=(x, y, 1 - c), device_id_type=MESH) for i in range(n)]
        for cp in cps:
            cp.start()
        for cp in cps:
            cp.wait()

    return pl.pallas_call(
        body, out_shape=[jax.ShapeDtypeStruct(r.shape, r.dtype) for r in rs],
        in_specs=[_ANY] * n, out_specs=[_ANY] * n,
        scratch_shapes=[pltpu.SemaphoreType.DMA((n,)), pltpu.SemaphoreType.DMA((n,))],
        name=name)(*rs)


def _pack(parts, row_mult=8):
    flat = jnp.concatenate([p.reshape(-1).astype(F32) for p in parts])
    unit = row_mult * 128
    n = -(-flat.shape[0] // unit) * unit
    return jnp.pad(flat, (0, n - flat.shape[0])).reshape(n // 128, 128)


def _unpack(flat, shapes):
    out, off = [], 0
    for s in shapes:
        n = int(np.prod(s))
        out.append(flat[off:off + n].reshape(s))
        off += n
    return out


def _gather_packed(parts, name):
    packed = _pack(parts)
    g = allgather_small(packed, name).reshape(N_DEV, -1)
    return _unpack_rows(g, [p.shape for p in parts])


def _unpack_rows(g, shapes):
    out, off = [], 0
    for s in shapes:
        n = int(np.prod(s))
        out.append(g[:, off:off + n].reshape((g.shape[0],) + tuple(s)))
        off += n
    return out


def _by_chip(t, axis):
    return jnp.concatenate([t[2 * p] for p in range(N_CHIPS)], axis=axis)


def kernel(x, c, ada_w, ada_b, ln_g, ln_b, a_in_w, a_conv_w, a_conv_b, a_dt_bias, a_A_log, a_D, a_norm_g, a_out_w, kv_w, b_in_w, b_out_w, loss_target, m_ada_w, m_ada_b, m_ln_g, m_ln_b, m_a_in_w, m_a_conv_w, m_a_conv_b, m_a_dt_bias, m_a_A_log, m_a_D, m_a_norm_g, m_a_out_w, m_kv_w, m_b_in_w, m_b_out_w, v_ada_w, v_ada_b, v_ln_g, v_ln_b, v_a_in_w, v_a_conv_w, v_a_conv_b, v_a_dt_bias, v_a_A_log, v_a_D, v_a_norm_g, v_a_out_w, v_kv_w, v_b_in_w, v_b_out_w):
    ax, ay, ac = lax.axis_index("x"), lax.axis_index("y"), lax.axis_index("c")
    chip = 2 * ax + ay
    dev = 4 * ax + 2 * ay + ac
    xin = x[0]
    tgt = loss_target[0]
    L, D = xin.shape
    G, P = SSD_G, SSD_P
    H = a_dt_bias.shape[1]
    Kh = H // G
    DI = H * P
    CONVD = a_conv_b.shape[1] * N_CHIPS
    HW = DIL_H * DIL_E
    Ws = ada_w.shape[2]

    (w_in_g,) = allgather_weights([a_in_w[0].astype(BF16)], "allgather_w_in")
    later = [a_out_w[0].astype(BF16), kv_w.astype(BF16), b_in_w[0].astype(BF16), b_out_w[0].astype(BF16)]
    later_split = [s.reshape(2, s.shape[0] // 2, s.shape[1]) for s in later]
    ag_ssem, ag_rsem, ag_srcs, ag_lands, ag_token = split_start(
        "gather", later_split, [(N_CHIPS,) + s.shape for s in later_split], w_in_g, "ag_later_start")
    w_in = jnp.transpose(w_in_g, (1, 0, 2)).reshape(D, -1)
    w_zx = w_in[:, :DI + CONVD]
    w_dt = jnp.pad(w_in[:, DI + CONVD:], ((0, 0), (0, 128 - H)))

    c8, cw8, cb8, ng8 = _gather_packed([c[0], a_conv_w[0], a_conv_b[0], a_norm_g[0]], "allgather_small_params")
    conv_w = _by_chip(cw8, 1)
    conv_b = _by_chip(cb8, 0).reshape(1, CONVD)
    norm_g = _by_chip(ng8, 0).reshape(1, DI)

    mod_s = ada_fwd(c8, ada_w)
    (mod8,) = _gather_packed([mod_s], "allgather_small_mod")
    mods = _by_chip(mod8, 2)
    mod = lax.dynamic_index_in_dim(mods, dev, axis=1, keepdims=False) + ada_b
    shift = [mod[l:l + 1, :D] for l in range(DEPTH)]
    scale = [mod[l:l + 1, D:2 * D] for l in range(DEPTH)]
    gate = [mod[l:l + 1, 2 * D:] for l in range(DEPTH)]
    lg = [ln_g[l:l + 1] for l in range(DEPTH)]
    lb = [ln_b[l:l + 1] for l in range(DEPTH)]

    h0 = modulate(xin, scale[0] + ag_token[0:1, 0:1], shift[0], "modulate0")
    zx = mm_nn(h0, w_zx, BF16, "mm_in_zx")
    dtp = mm_nn(h0, w_dt, F32, "mm_in_dt")
    xbc = conv_fwd(zx, DI, conv_w, conv_b)
    dtp_g = jnp.transpose(dtp[:, :H].reshape(L, G, Kh), (1, 0, 2))
    dtp_gT = jnp.transpose(dtp_g, (0, 2, 1))
    vecs = [a_dt_bias.reshape(G, 1, Kh), a_dt_bias.reshape(G, Kh, 1), a_A_log.reshape(G, 1, Kh),
            a_A_log.reshape(G, Kh, 1), a_D.reshape(G, 1, Kh), a_D.reshape(G, Kh, 1)]
    y_ssd, states = ssd_fwd(xbc, dtp_g, dtp_gT, *vecs, DI)
    yn = rms_gate_fwd(y_ssd, zx, norm_g)
    later_split, ag_lands = split_wait("gather", ag_ssem, ag_rsem, ag_srcs, ag_lands, yn, "ag_later_wait")
    ag_lands = pass_to_sibling(ag_lands)
    w_out_g, w_kv_g, w_bin_g, w_bout_g = [
        lax.dynamic_update_index_in_dim(o, s, chip, 0).reshape((N_CHIPS,) + full.shape)
        for o, s, full in zip(ag_lands, later_split, later)]
    ymix0 = mm_nn(yn, w_out_g, F32, "mm_out_a", stack="row")
    x1, x1b, h1 = ln_mid(xin, ymix0, gate[0], lg[0], lb[0], scale[1], shift[1])

    kvp = mm_nn(x1b, w_kv_g, BF16, "mm_kv", stack="col")
    qz = mm_nn(h1, w_bin_g, BF16, "mm_in_b", stack="col")
    os_, lses = [], []
    for gi in range(len(DIL_PATTERNS)):
        o, lse = attn_fwd(qz, kvp, gi)
        os_.append(o)
        lses.append(lse)
    om = merge_fwd(os_, lses, qz)
    ymix1 = mm_nn(om, w_bout_g, F32, "mm_out_b", stack="col")
    dx2, sq = ln_final(x1, ymix1, gate[1], lg[1], lb[1], tgt)
    loss_part = 0.5 * jnp.sum(sq) / D

    dres2, dy2, dg1, db1, dgate1 = ln_bwd(dx2, x1, ymix1, gate[1], lg[1], "ln_bwd1")
    g_bout = mm_tn(om, dy2, BF16, "mm_gw_out_b", stack="col")
    dgated = mm_nt(dy2, w_bout_g, BF16, "mm_gx_out_b", stack="col")
    dos, dprs, dz_b = merge_bwd(dgated, os_, lses, qz)
    dqs, dks, dvs = [], [], []
    for gi in range(len(DIL_PATTERNS)):
        dq, dk, dv = attn_bwd(qz, kvp, dos[gi], lses[gi], dprs[gi], gi)
        dqs.append(dq)
        dks.append(dk)
        dvs.append(dv)
    dqz = jnp.concatenate(dqs + [dz_b], axis=1)
    dkv = jnp.concatenate(dks + dvs, axis=1)
    g_bin = mm_tn(h1, dqz, BF16, "mm_gw_in_b", stack="col")
    dh1 = mm_nt(dqz, w_bin_g, F32, "mm_gx_in_b", stack="col")
    g_kv = mm_tn(x1b, dkv, BF16, "mm_gw_kv", stack="col")
    dx1_kv = mm_nt(dkv, w_kv_g, F32, "mm_gx_kv", stack="col")
    dx1, dscale1, dshift1 = mod_bwd(dres2, dh1, dx1_kv, x1, scale[1], "mod_bwd1", through_mod=False)

    core = ac.astype(jnp.int32).reshape(1)
    names_b = ["kv", "in_b", "out_b"]
    gs_b = [g_kv, g_bin, g_bout]
    sib_b = exchange_halves_to_sibling(gs_b, "rs_sibling_exchange_b")
    parts_b = [add_half(g, a, core, "rs_add_" + nm) for g, a, nm in zip(gs_b, sib_b, names_b)]
    rs_ssem, rs_rsem, rs_srcs, rs_lands, rs_token = split_start(
        "scatter", parts_b, [(3,) + t.shape[1:] for t in parts_b], parts_b[0], "rs_b_start")

    dres1, dy1, dg0, db0, dgate0 = ln_bwd(dx1, xin, ymix0, gate[0] + rs_token[0:1, 0:1], lg[0], "ln_bwd0")
    g_out = mm_tn(yn, dy1, BF16, "mm_gw_out_a", stack="row")
    dyn = mm_nt(dy1, w_out_g, BF16, "mm_gx_out_a", stack="row")
    dy_ssd, dz_a, dnorm_g = rms_gate_bwd(dyn, y_ssd, zx, norm_g)
    dxs, dB, dC, ddtp_g, dbias_g, dalog_g, dD_g = ssd_bwd(xbc, dtp_g, dtp_gT, *vecs, states, dy_ssd, DI)
    dxbc = jnp.concatenate([dxs, dB, dC], axis=1)
    dxbc_pre, dconv_w, dconv_b = conv_bwd(zx, DI, conv_w, conv_b, dxbc)
    dzx = jnp.concatenate([dz_a, dxbc_pre], axis=1)
    ddtp = jnp.pad(jnp.transpose(ddtp_g, (1, 0, 2)).reshape(L, H), ((0, 0), (0, 128 - H)))
    g_zx = mm_tn(h0, dzx, BF16, "mm_gw_in_zx")
    g_dt = mm_tn(h0, ddtp, BF16, "mm_gw_in_dt")
    dh0 = mm_nt(dzx, w_zx, F32, "mm_gx_in_zx")
    dh0_dt = mm_nt(ddtp, w_dt, F32, "mm_gx_in_dt")
    grad_x, dscale0, dshift0 = mod_bwd(dres1, dh0, dh0_dt, xin, scale[0], "mod_bwd0", through_mod=True)
    g_in = jnp.concatenate([g_zx, g_dt[:, :H]], axis=1)
    g_in = jnp.transpose(g_in.reshape(D, N_CHIPS, -1), (1, 0, 2))

    names_a = ["in_a", "out_a"]
    gs_a = [g_in, g_out]
    sib_a = exchange_halves_to_sibling(gs_a, "rs_sibling_exchange_a")
    parts_a = [add_half(g, a, core, "rs_add_" + nm) for g, a, nm in zip(gs_a, sib_a, names_a)]
    landed_a = scatter_to_chips(parts_a, "rs_chip_scatter_a")
    parts_b, landed_b = split_wait("scatter", rs_ssem, rs_rsem, rs_srcs, rs_lands, parts_a[0], "rs_b_wait")
    names = names_a + names_b
    chip_i = chip.astype(jnp.int32).reshape(1)
    halves = [sum_partials(own, t, chip_i, "rs_sum_" + nm)
              for own, t, nm in zip(parts_a + list(parts_b), list(landed_a) + list(landed_b), names)]
    theirs = join_halves(halves, "rs_join_halves")
    g_halves = dict(zip(names, zip(halves, theirs)))

    dmod = jnp.concatenate([jnp.concatenate([dshift0, dscale0, dgate0], axis=1),
                            jnp.concatenate([dshift1, dscale1, dgate1], axis=1)], axis=0)
    small_parts = [jnp.concatenate([dg0, dg1], axis=0), jnp.concatenate([db0, db1], axis=0),
                   dbias_g.reshape(1, H), dalog_g.reshape(1, H), dD_g.reshape(1, H),
                   dconv_w, dconv_b, dnorm_g, loss_part.reshape(1, 1)]
    small_shapes = [p.shape for p in small_parts]
    packed = jnp.concatenate([_pack([dmod]), _pack(small_parts)], axis=0)
    n_mod_rows = _pack([dmod]).shape[0]
    gathered = allgather_small(packed, "allgather_small_grads").reshape(N_DEV, -1, 128)
    dmod8 = gathered[:, :n_mod_rows].reshape(N_DEV, -1)[:, :2 * 3 * D].reshape(N_DEV, DEPTH, 3 * D)
    summed = sum_leading(gathered, "sum_small")
    g_ada_b = summed[:n_mod_rows].reshape(-1)[:2 * 3 * D].reshape(DEPTH, 3 * D)
    (g_ln_g, g_ln_b, g_dt_bias, g_a_log, g_dsk, g_conv_w, g_conv_b, g_norm_g, loss_all) = _unpack(
        summed[n_mod_rows:].reshape(-1), small_shapes)
    loss = loss_all.reshape(())
    Cs = CONVD // N_CHIPS
    g_conv_w_s = lax.dynamic_slice_in_dim(g_conv_w, chip * Cs, Cs, axis=1)
    g_conv_b_s = lax.dynamic_slice_in_dim(g_conv_b, chip * Cs, Cs, axis=1)
    g_norm_g_s = lax.dynamic_slice_in_dim(g_norm_g, chip * (DI // N_CHIPS), DI // N_CHIPS, axis=1)
    dmod_s = jnp.transpose(lax.dynamic_slice_in_dim(dmod8, chip * Ws, Ws, axis=2), (1, 0, 2))
    g_ada_w = ada_wgrad(jnp.transpose(c8), dmod_s)

    def step2d(w, g, m, v, nm):
        shp = w.shape
        d_, m_, v_ = adamw(w.reshape(-1, shp[-1]), g.reshape(-1, shp[-1]), m.reshape(-1, shp[-1]),
                           v.reshape(-1, shp[-1]), "adamw_" + nm)
        return g.reshape(shp), d_.reshape(shp), m_.reshape(shp), v_.reshape(shp)

    def step_halves(w, m, v, nm):
        shp = w.shape
        mine, theirs_ = g_halves[nm]
        outs4 = adamw_halves(w.reshape(-1, shp[-1]), mine, theirs_, m.reshape(-1, shp[-1]), v.reshape(-1, shp[-1]),
                             core, "adamw_" + nm)
        return tuple(t.reshape(shp) for t in outs4)

    big = {
        "ada_w": step2d(ada_w, g_ada_w, m_ada_w, v_ada_w, "ada_w"),
        "a_in_w": step_halves(a_in_w, m_a_in_w, v_a_in_w, "in_a"),
        "a_out_w": step_halves(a_out_w, m_a_out_w, v_a_out_w, "out_a"),
        "kv_w": step_halves(kv_w, m_kv_w, v_kv_w, "kv"),
        "b_in_w": step_halves(b_in_w, m_b_in_w, v_b_in_w, "in_b"),
        "b_out_w": step_halves(b_out_w, m_b_out_w, v_b_out_w, "out_b"),
    }
    small_names = ["ada_b", "ln_g", "ln_b", "a_conv_w", "a_conv_b", "a_dt_bias", "a_A_log", "a_D", "a_norm_g"]
    small_w = [ada_b, ln_g, ln_b, a_conv_w, a_conv_b, a_dt_bias, a_A_log, a_D, a_norm_g]
    small_m = [m_ada_b, m_ln_g, m_ln_b, m_a_conv_w, m_a_conv_b, m_a_dt_bias, m_a_A_log, m_a_D, m_a_norm_g]
    small_v = [v_ada_b, v_ln_g, v_ln_b, v_a_conv_w, v_a_conv_b, v_a_dt_bias, v_a_A_log, v_a_D, v_a_norm_g]
    small_g = [g_ada_b, g_ln_g, g_ln_b, g_conv_w_s, g_conv_b_s, g_dt_bias, g_a_log, g_dsk, g_norm_g_s]
    shapes = [w.shape for w in small_w]
    small_g = [g.reshape(s) for g, s in zip(small_g, shapes)]
    d_p, m_p, v_p = adamw(_pack(small_w), _pack(small_g), _pack(small_m), _pack(small_v), "adamw_small")
    small = {}
    for nm, g, d_, m_, v_ in zip(small_names, small_g, _unpack(d_p.reshape(-1), shapes), _unpack(m_p.reshape(-1), shapes),
                                 _unpack(v_p.reshape(-1), shapes)):
        small[nm] = (g, d_, m_, v_)
    allw = {**big, **small}
    order = ["ada_w", "ada_b", "ln_g", "ln_b", "a_in_w", "a_conv_w", "a_conv_b", "a_dt_bias", "a_A_log", "a_D",
             "a_norm_g", "a_out_w", "kv_w", "b_in_w", "b_out_w"]
    outs = [loss, grad_x.reshape(x.shape)]
    for k in range(4):
        outs += [allw[n][k] for n in order]
    return tuple(outs)
```

```python
import functools

import jax
import jax.numpy as jnp
import numpy as np
from jax import lax
from jax.experimental import pallas as pl
from jax.experimental.pallas import tpu as pltpu

F32 = jnp.float32
BF16 = jnp.bfloat16
MESH = pl.DeviceIdType.MESH

DEPTH = 2
ALPHA = (2 * DEPTH) ** 0.25
LN_EPS = 1e-5
RMS_EPS = 1e-5
SSD_P = 64
SSD_N = 128
SSD_Q = 256
SSD_G = 8
CONV_W = 4
DIL_PATTERNS = ((128, 1), (512, 4), (2048, 16))
DIL_H = 8
DIL_E = 128
DIL_BLK = 128
ADAM_LR, ADAM_B1, ADAM_B2, ADAM_EPS, ADAM_WD, ADAM_STEP = 0.001, 0.9, 0.999, 1e-08, 0.01, 10

VMEM_LIMIT = 56 * 1024 * 1024
N_CHIPS = 4
N_DEV = 8


def _tile(dim, target, mult=128):
    if dim <= target:
        return dim
    t = (target // mult) * mult
    while t >= mult:
        if dim % t == 0:
            return t
        t -= mult
    return dim


def _cp(sem):
    return pltpu.CompilerParams(dimension_semantics=sem, vmem_limit_bytes=VMEM_LIMIT)


def _sigmoid(x):
    return 1.0 / (1.0 + jnp.exp(-x))


def _silu(x):
    return x * _sigmoid(x)


def _dsilu(x):
    s = _sigmoid(x)
    return s * (1.0 + x * (1.0 - s))


def _softplus(x):
    return jnp.maximum(x, 0.0) + jnp.log(1.0 + jnp.exp(-jnp.abs(x)))


def _mm_call(a, b, out_shape, grid, a_spec, b_spec, o_spec, acc_shape, dims, name):
    nk = grid[2]

    def prod(a_ref, b_ref):
        return lax.dot_general(a_ref[...].astype(BF16), b_ref[...].astype(BF16), (dims, ((), ())),
                               preferred_element_type=F32)

    def body_single(a_ref, b_ref, o_ref):
        o_ref[...] = prod(a_ref, b_ref).astype(o_ref.dtype)

    def body_multi(a_ref, b_ref, o_ref, acc_ref):
        k = pl.program_id(2)

        @pl.when(k == 0)
        def _():
            acc_ref[...] = prod(a_ref, b_ref)

        @pl.when(jnp.logical_and(k > 0, k < nk - 1))
        def _():
            acc_ref[...] += prod(a_ref, b_ref)

        @pl.when(k == nk - 1)
        def _():
            o_ref[...] = (acc_ref[...] + prod(a_ref, b_ref)).astype(o_ref.dtype)

    return pl.pallas_call(
        body_single if nk == 1 else body_multi, grid=grid, in_specs=[a_spec, b_spec], out_specs=o_spec,
        out_shape=out_shape, scratch_shapes=[] if nk == 1 else [pltpu.VMEM(acc_shape, F32)],
        compiler_params=_cp(("parallel", "parallel", "arbitrary")), name=name)(a, b)


def mm_nn(a, b, out_dtype, name, stack=None, tm=1024, tn=1024, tk=2048):
    M, K = a.shape
    if stack is None:
        N = b.shape[1]
        tn, tk = _tile(N, tn), _tile(K, tk)
        b_spec = pl.BlockSpec((tk, tn), lambda i, j, k: (k, j))
    elif stack == "col":
        S, _, Ns = b.shape
        N = S * Ns
        tn, tk = _tile(Ns, tn), _tile(K, tk)
        npb = Ns // tn
        b_spec = pl.BlockSpec((None, tk, tn), lambda i, j, k: (j // npb, k, j % npb))
    else:
        S, Ks, N = b.shape
        tn, tk = _tile(N, tn), _tile(Ks, tk)
        kpb = Ks // tk
        b_spec = pl.BlockSpec((None, tk, tn), lambda i, j, k: (k // kpb, k % kpb, j))
    tm = _tile(M, tm)
    return _mm_call(a, b, jax.ShapeDtypeStruct((M, N), out_dtype), (M // tm, N // tn, K // tk),
                    pl.BlockSpec((tm, tk), lambda i, j, k: (i, k)), b_spec,
                    pl.BlockSpec((tm, tn), lambda i, j, k: (i, j)), (tm, tn), ((1,), (0,)), name)


def mm_nt(a, b, out_dtype, name, stack=None, tm=1024, tn=1024, tk=2048):
    M, C = a.shape
    if stack is None:
        Kw = b.shape[0]
        tn, tk = _tile(Kw, tn), _tile(C, tk)
        b_spec = pl.BlockSpec((tn, tk), lambda i, j, k: (j, k))
    elif stack == "col":
        S, Kw, Cs = b.shape
        tn, tk = _tile(Kw, tn), _tile(Cs, tk)
        cpb = Cs // tk
        b_spec = pl.BlockSpec((None, tn, tk), lambda i, j, k: (k // cpb, j, k % cpb))
    else:
        S, Ks, _ = b.shape
        Kw = S * Ks
        tn, tk = _tile(Ks, tn), _tile(C, tk)
        jpb = Ks // tn
        b_spec = pl.BlockSpec((None, tn, tk), lambda i, j, k: (j // jpb, j % jpb, k))
    tm = _tile(M, tm)
    return _mm_call(a, b, jax.ShapeDtypeStruct((M, Kw), out_dtype), (M // tm, Kw // tn, C // tk),
                    pl.BlockSpec((tm, tk), lambda i, j, k: (i, k)), b_spec,
                    pl.BlockSpec((tm, tn), lambda i, j, k: (i, j)), (tm, tn), ((1,), (1,)), name)


def mm_tn(a, b, out_dtype, name, stack=None, n_stack=N_CHIPS, tm=1024, tn=1024, tk=2048):
    L, M = a.shape
    N = b.shape[1]
    tk = _tile(L, tk)
    if stack is None:
        tm, tn = _tile(M, tm), _tile(N, tn)
        o_spec = pl.BlockSpec((tm, tn), lambda i, j, k: (i, j))
        out_shape = (M, N)
    elif stack == "col":
        Ns = N // n_stack
        tm, tn = _tile(M, tm), _tile(Ns, tn)
        npb = Ns // tn
        o_spec = pl.BlockSpec((None, tm, tn), lambda i, j, k: (j // npb, i, j % npb))
        out_shape = (n_stack, M, Ns)
    else:
        Ms = M // n_stack
        tm, tn = _tile(Ms, tm), _tile(N, tn)
        mpb = Ms // tm
        o_spec = pl.BlockSpec((None, tm, tn), lambda i, j, k: (i // mpb, i % mpb, j))
        out_shape = (n_stack, Ms, N)
    return _mm_call(a, b, jax.ShapeDtypeStruct(out_shape, out_dtype), (M // tm, N // tn, L // tk),
                    pl.BlockSpec((tk, tm), lambda i, j, k: (k, i)), pl.BlockSpec((tk, tn), lambda i, j, k: (k, j)),
                    o_spec, (tm, tn), ((0,), (0,)), name)


def _row_specs(tr, widths):
    return [pl.BlockSpec((tr, w), lambda i: (i, 0)) for w in widths]


def _vec_spec(w):
    return pl.BlockSpec((1, w), lambda i: (0, 0))


def _acc_rows(ref, val, i):
    s = jnp.sum(val, axis=0, keepdims=True)

    @pl.when(i == 0)
    def _():
        ref[...] = s

    @pl.when(i > 0)
    def _():
        ref[...] += s


def modulate(x, scale, shift, name):
    L, D = x.shape
    tr = _tile(L, 512, 16)

    def body(x_ref, sc_ref, sh_ref, h_ref):
        h_ref[...] = (x_ref[...] * (1.0 + sc_ref[...]) + sh_ref[...]).astype(BF16)

    return pl.pallas_call(
        body, grid=(L // tr,), in_specs=_row_specs(tr, [D]) + [_vec_spec(D)] * 2, out_specs=_row_specs(tr, [D])[0],
        out_shape=jax.ShapeDtypeStruct((L, D), BF16), compiler_params=_cp(("parallel",)), name=name)(x, scale, shift)


def _ln_core(x, y, gate, g, b):
    u = ALPHA * x + (1.0 + gate) * y
    mu = jnp.mean(u, axis=-1, keepdims=True)
    d = u - mu
    var = jnp.mean(d * d, axis=-1, keepdims=True)
    rstd = lax.rsqrt(var + LN_EPS)
    xhat = d * rstd
    return xhat * g + b, xhat, rstd


def ln_mid(x, y, gate, g, b, scale, shift):
    L, D = x.shape
    tr = _tile(L, 256, 16)

    def body(x_ref, y_ref, gate_ref, g_ref, b_ref, sc_ref, sh_ref, x1_ref, x1b_ref, h_ref):
        x1, _, _ = _ln_core(x_ref[...], y_ref[...], gate_ref[...], g_ref[...], b_ref[...])
        x1_ref[...] = x1
        x1b_ref[...] = x1.astype(BF16)
        h_ref[...] = (x1 * (1.0 + sc_ref[...]) + sh_ref[...]).astype(BF16)

    return pl.pallas_call(
        body, grid=(L // tr,), in_specs=_row_specs(tr, [D, D]) + [_vec_spec(D)] * 5,
        out_specs=_row_specs(tr, [D, D, D]),
        out_shape=[jax.ShapeDtypeStruct((L, D), F32), jax.ShapeDtypeStruct((L, D), BF16),
                   jax.ShapeDtypeStruct((L, D), BF16)],
        compiler_params=_cp(("parallel",)), name="ln_mid")(x, y, gate, g, b, scale, shift)


def ln_final(x, y, gate, g, b, target):
    L, D = x.shape
    tr = _tile(L, 256, 16)

    def body(x_ref, y_ref, gate_ref, g_ref, b_ref, t_ref, dout_ref, sq_ref):
        out, _, _ = _ln_core(x_ref[...], y_ref[...], gate_ref[...], g_ref[...], b_ref[...])
        err = out - t_ref[...]
        dout_ref[...] = err * (1.0 / D)
        _acc_rows(sq_ref, err * err, pl.program_id(0))

    return pl.pallas_call(
        body, grid=(L // tr,), in_specs=_row_specs(tr, [D, D]) + [_vec_spec(D)] * 3 + _row_specs(tr, [D]),
        out_specs=[_row_specs(tr, [D])[0], _vec_spec(D)],
        out_shape=[jax.ShapeDtypeStruct((L, D), F32), jax.ShapeDtypeStruct((1, D), F32)],
        compiler_params=_cp(("arbitrary",)), name="ln_final")(x, y, gate, g, b, target)


def ln_bwd(dout, x, y, gate, g, name):
    L, D = x.shape
    tr = _tile(L, 256, 16)

    def body(do_ref, x_ref, y_ref, gate_ref, g_ref, dres_ref, dy_ref, dg_ref, db_ref, dgate_ref):
        i = pl.program_id(0)
        yv = y_ref[...]
        dout_v = do_ref[...]
        _, xhat, rstd = _ln_core(x_ref[...], yv, gate_ref[...], g_ref[...], 0.0)
        dxh = dout_v * g_ref[...]
        m1 = jnp.mean(dxh, axis=-1, keepdims=True)
        m2 = jnp.mean(dxh * xhat, axis=-1, keepdims=True)
        du = rstd * (dxh - m1 - xhat * m2)
        dres_ref[...] = ALPHA * du
        dy_ref[...] = ((1.0 + gate_ref[...]) * du).astype(BF16)
        _acc_rows(dg_ref, dout_v * xhat, i)
        _acc_rows(db_ref, dout_v, i)
        _acc_rows(dgate_ref, du * yv, i)

    return pl.pallas_call(
        body, grid=(L // tr,), in_specs=_row_specs(tr, [D, D, D]) + [_vec_spec(D)] * 2,
        out_specs=_row_specs(tr, [D, D]) + [_vec_spec(D)] * 3,
        out_shape=[jax.ShapeDtypeStruct((L, D), F32), jax.ShapeDtypeStruct((L, D), BF16)]
        + [jax.ShapeDtypeStruct((1, D), F32)] * 3,
        compiler_params=_cp(("arbitrary",)), name=name)(dout, x, y, gate, g)


def mod_bwd(dres, dh, dh2, xin, scale, name, through_mod):
    L, D = xin.shape
    tr = _tile(L, 256, 16)

    def body(dres_ref, dh_ref, dh2_ref, x_ref, sc_ref, dx_ref, dsc_ref, dsh_ref):
        i = pl.program_id(0)
        dh_v = dh_ref[...]
        tot = dres_ref[...]
        if through_mod:
            dh_v = dh_v + dh2_ref[...]
        else:
            tot = tot + dh2_ref[...]
        dx_ref[...] = tot + dh_v * (1.0 + sc_ref[...])
        _acc_rows(dsc_ref, dh_v * x_ref[...], i)
        _acc_rows(dsh_ref, dh_v, i)

    return pl.pallas_call(
        body, grid=(L // tr,), in_specs=_row_specs(tr, [D, D, D, D]) + [_vec_spec(D)],
        out_specs=_row_specs(tr, [D]) + [_vec_spec(D)] * 2,
        out_shape=[jax.ShapeDtypeStruct((L, D), F32)] + [jax.ShapeDtypeStruct((1, D), F32)] * 2,
        compiler_params=_cp(("arbitrary",)), name=name)(dres, dh, dh2, xin, scale)


CONV_HALO = 16


def _conv_rows(x_ref, i, tr, L):
    nblk = L // tr
    s = pl.multiple_of(i * tr, CONV_HALO)
    cur = x_ref[pl.ds(s, tr), :].astype(F32)
    sp = pl.multiple_of(jnp.maximum(i * tr - CONV_HALO, 0), CONV_HALO)
    sn = pl.multiple_of(jnp.minimum(i * tr + tr, L - CONV_HALO), CONV_HALO)
    prev = x_ref[pl.ds(sp, CONV_HALO), :].astype(F32) * (i > 0).astype(F32)
    nxt = x_ref[pl.ds(sn, CONV_HALO), :].astype(F32) * (i < nblk - 1).astype(F32)
    return jnp.concatenate([prev, cur, nxt], axis=0)


def _shift_rows(v, j):
    n = v.shape[0]
    return v if j % n == 0 else pltpu.roll(v, j % n, 0)


def _conv_eval(xe, w_ref, b_ref):
    c = b_ref[...] + w_ref[CONV_W - 1:CONV_W, :] * xe
    for k in range(CONV_W - 1):
        c = c + w_ref[k:k + 1, :] * _shift_rows(xe, CONV_W - 1 - k)
    return c


def conv_fwd(zx, col0, conv_w, conv_b):
    L = zx.shape[0]
    C = conv_w.shape[1]
    tc = _tile(C, 512)
    tr = _tile(L, 512, CONV_HALO)
    off = col0 // tc

    def body(x_ref, w_ref, b_ref, o_ref):
        i = pl.program_id(1)
        xe = _conv_rows(x_ref, i, tr, L)
        c = _conv_eval(xe, w_ref, b_ref)[CONV_HALO:CONV_HALO + tr]
        o_ref[...] = _silu(c).astype(BF16)

    return pl.pallas_call(
        body, grid=(C // tc, L // tr),
        in_specs=[pl.BlockSpec((L, tc), lambda j, i: (0, off + j)), pl.BlockSpec((CONV_W, tc), lambda j, i: (0, j)),
                  pl.BlockSpec((1, tc), lambda j, i: (0, j))],
        out_specs=pl.BlockSpec((tr, tc), lambda j, i: (i, j)),
        out_shape=jax.ShapeDtypeStruct((L, C), BF16), compiler_params=_cp(("parallel", "arbitrary")),
        name="conv_fwd")(zx, conv_w, conv_b)


def conv_bwd(zx, col0, conv_w, conv_b, dxbc):
    L = zx.shape[0]
    C = conv_w.shape[1]
    tc = _tile(C, 512)
    tr = _tile(L, 512, CONV_HALO)
    off = col0 // tc
    H = CONV_HALO

    def body(x_ref, g_ref, w_ref, b_ref, dx_ref, dw_ref, db_ref):
        i = pl.program_id(1)
        xe = _conv_rows(x_ref, i, tr, L)
        ge = _conv_rows(g_ref, i, tr, L)
        dc = ge * _dsilu(_conv_eval(xe, w_ref, b_ref))
        dx = w_ref[CONV_W - 1:CONV_W, :] * dc
        for k in range(CONV_W - 1):
            dx = dx + w_ref[k:k + 1, :] * _shift_rows(dc, -(CONV_W - 1 - k))
        dx_ref[...] = dx[H:H + tr].astype(BF16)
        dcc = dc[H:H + tr]
        rows = [jnp.sum(dcc * _shift_rows(xe, CONV_W - 1 - k)[H:H + tr], axis=0, keepdims=True) for k in range(CONV_W)]
        dwv = jnp.concatenate(rows + [jnp.zeros((8 - CONV_W, tc), F32)], axis=0)
        dbv = jnp.sum(dcc, axis=0, keepdims=True)

        @pl.when(i == 0)
        def _():
            dw_ref[...] = dwv
            db_ref[...] = dbv

        @pl.when(i > 0)
        def _():
            dw_ref[...] += dwv
            db_ref[...] += dbv

    dx, dw, db = pl.pallas_call(
        body, grid=(C // tc, L // tr),
        in_specs=[pl.BlockSpec((L, tc), lambda j, i: (0, off + j)), pl.BlockSpec((L, tc), lambda j, i: (0, j)),
                  pl.BlockSpec((CONV_W, tc), lambda j, i: (0, j)), pl.BlockSpec((1, tc), lambda j, i: (0, j))],
        out_specs=[pl.BlockSpec((tr, tc), lambda j, i: (i, j)), pl.BlockSpec((8, tc), lambda j, i: (0, j)),
                   pl.BlockSpec((1, tc), lambda j, i: (0, j))],
        out_shape=[jax.ShapeDtypeStruct((L, C), BF16), jax.ShapeDtypeStruct((8, C), F32),
                   jax.ShapeDtypeStruct((1, C), F32)],
        compiler_params=_cp(("parallel", "arbitrary")), name="conv_bwd")(zx, dxbc, conv_w, conv_b)
    return dx, dw[:CONV_W], db


_NN = (((1,), (0,)), ((), ()))


def _pieces(x, n):
    out, r = [], x
    for _ in range(n):
        p = r.astype(BF16)
        out.append(p)
        r = r - p.astype(F32)
    return out


def _dot01(a, b01, n, dims=_NN):
    b = b01.astype(BF16)
    return functools.reduce(lambda u, v: u + v,
                            [lax.dot_general(p, b, dims, preferred_element_type=F32) for p in _pieces(a, n)])


def _dot01_left(a01, b, n, dims=_NN):
    a = a01.astype(BF16)
    return functools.reduce(lambda u, v: u + v,
                            [lax.dot_general(a, p, dims, preferred_element_type=F32) for p in _pieces(b, n)])


def _ssd_common(dtp_ref, dtpT_ref, bias_ref, biasT_ref, alog_ref, alogT_ref, b_ref, c_ref):
    Q = SSD_Q
    dt = _softplus(dtp_ref[...] + bias_ref[...])
    A = -jnp.exp(alog_ref[...])
    row = lax.broadcasted_iota(jnp.int32, (Q, Q), 0)
    col = lax.broadcasted_iota(jnp.int32, (Q, Q), 1)
    causal = row >= col
    tril = causal.astype(F32)
    Kh = dt.shape[1]
    acum = _dot01_left(tril, dt * A, 3)
    eye = (lax.broadcasted_iota(jnp.int32, (Kh, Kh), 0) == lax.broadcasted_iota(jnp.int32, (Kh, Kh), 1)).astype(F32)
    acumT = _dot01_left(eye, acum, 3, dims=(((1,), (1,)), ((), ())))
    Bm = b_ref[...]
    Cm = c_ref[...]
    cb = lax.dot_general(Cm, Bm, (((1,), (1,)), ((), ())), preferred_element_type=F32)
    return dt, A, causal, row, col, acum, acumT, Bm, Cm, cb


def _ssd_in_specs(Q, GP, N, Kh, DI, cmap):
    nb0 = DI // N
    vec = pl.BlockSpec((None, 1, Kh), lambda g, c: (g, 0, 0))
    vecT = pl.BlockSpec((None, Kh, 1), lambda g, c: (g, 0, 0))
    return [pl.BlockSpec((Q, GP), lambda g, c: (cmap(c), g)),
            pl.BlockSpec((Q, N), lambda g, c: (cmap(c), nb0 + g)),
            pl.BlockSpec((Q, N), lambda g, c: (cmap(c), nb0 + SSD_G + g)),
            pl.BlockSpec((None, Q, Kh), lambda g, c: (g, cmap(c), 0)),
            pl.BlockSpec((None, Kh, Q), lambda g, c: (g, 0, cmap(c))),
            vec, vecT, vec, vecT, vec, vecT]


def _hi(a, b01):
    return _dot01(a, b01, 2)


def _ssd_heads(dskT_ref, acum, acumT, dt, Kh):
    Q, P, N = SSD_Q, SSD_P, SSD_N
    GP = Kh * P
    sh_p = P.bit_length() - 1
    seg = lambda shape, dim: lax.shift_right_logical(lax.broadcasted_iota(jnp.int32, shape, dim), sh_p)
    E = (seg((Kh, GP), 1) == lax.broadcasted_iota(jnp.int32, (Kh, GP), 0)).astype(F32)
    ET = (seg((GP, Kh), 0) == lax.broadcasted_iota(jnp.int32, (GP, Kh), 1)).astype(F32)
    a_last = acum[Q - 1:Q, :]
    tail = jnp.exp(a_last - acum)
    eLT = jnp.exp(acumT[:, Q - 1:Q])
    rowseg = seg((GP, N), 0)
    eL_b = jnp.zeros((GP, N), F32)
    for k in range(Kh):
        eL_b = jnp.where(rowseg == k, eLT[k:k + 1, :], eL_b)
    return dict(
        E=E, ET=ET, a_last=a_last, tail=tail, eL_b=eL_b,
        dt_all=_hi(dt, E), ea_all=_hi(jnp.exp(acum), E), tail_all=_hi(tail, E),
        dsk_all=jnp.sum(E * dskT_ref[...], axis=0, keepdims=True))


def _head_chunks(GP):
    CW = min(GP, 128)
    return CW, CW // SSD_P, GP // CW


def _head_mask(Q, CW, kk):
    lane = lax.broadcasted_iota(jnp.int32, (Q, CW), 1)
    return jnp.logical_and(lane >= kk * SSD_P, lane < (kk + 1) * SSD_P)


def ssd_fwd(xbc, dtp_g, dtp_gT, bias_g, bias_gT, alog_g, alog_gT, dsk_g, dsk_gT, DI):
    L = xbc.shape[0]
    Q, P, N, G = SSD_Q, SSD_P, SSD_N, SSD_G
    GP = DI // G
    Kh = GP // P
    nc = L // Q

    CW, hpc, nch = _head_chunks(GP)
    nt = (((1,), (1,)), ((), ()))
    tn = (((0,), (0,)), ((), ()))

    def body(xs_ref, b_ref, c_ref, dtp_ref, dtpT_ref, bias_ref, biasT_ref, alog_ref, alogT_ref, dsk_ref, dskT_ref,
             y_ref, st_ref, state):
        @pl.when(pl.program_id(1) == 0)
        def _():
            state[...] = jnp.zeros(state.shape, F32)

        st_ref[...] = state[...]
        dt, A, causal, row, col, acum, acumT, Bm, Cm, cb = _ssd_common(
            dtp_ref, dtpT_ref, bias_ref, biasT_ref, alog_ref, alogT_ref, b_ref, c_ref)
        hd = _ssd_heads(dskT_ref, acum, acumT, dt, Kh)
        xs = xs_ref[...].astype(F32)
        xdt_all = xs * hd["dt_all"]
        S_all = state[...]
        y_all = (lax.dot_general(Cm, S_all.astype(BF16), nt, preferred_element_type=F32) * hd["ea_all"]
                 + xs * hd["dsk_all"])
        state[...] = S_all * hd["eL_b"] + lax.dot_general(
            (xdt_all * hd["tail_all"]).astype(BF16), Bm, tn, preferred_element_type=F32)
        for ch in range(nch):
            cs = slice(ch * CW, (ch + 1) * CW)
            xc = xdt_all[:, cs]
            acc = y_all[:, cs]
            for kk in range(hpc):
                k = ch * hpc + kk
                decay = jnp.exp(jnp.where(causal, acum[:, k:k + 1] - acumT[k:k + 1, :], -jnp.inf))
                xk = xc if hpc == 1 else jnp.where(_head_mask(Q, CW, kk), xc, 0.0)
                acc = acc + jnp.dot((cb * decay).astype(BF16), xk.astype(BF16), preferred_element_type=F32)
            y_ref[:, cs] = acc.astype(BF16)

    return pl.pallas_call(
        body, grid=(G, nc), in_specs=_ssd_in_specs(Q, GP, N, Kh, DI, lambda c: c),
        out_specs=[pl.BlockSpec((Q, GP), lambda g, c: (c, g)),
                   pl.BlockSpec((None, None, GP, N), lambda g, c: (c, g, 0, 0))],
        out_shape=[jax.ShapeDtypeStruct((L, DI), BF16), jax.ShapeDtypeStruct((nc, G, GP, N), F32)],
        scratch_shapes=[pltpu.VMEM((GP, N), F32)], compiler_params=_cp(("parallel", "arbitrary")),
        name="ssd_fwd")(xbc, xbc, xbc, dtp_g, dtp_gT, bias_g, bias_gT, alog_g, alog_gT, dsk_g, dsk_gT)


def ssd_bwd(xbc, dtp_g, dtp_gT, bias_g, bias_gT, alog_g, alog_gT, dsk_g, dsk_gT, states, dy, DI):
    L = xbc.shape[0]
    Q, P, N, G = SSD_Q, SSD_P, SSD_N, SSD_G
    GP = DI // G
    Kh = GP // P
    nc = L // Q
    rev = lambda c: nc - 1 - c

    CW, hpc, nch = _head_chunks(GP)

    def body(xs_ref, b_ref, c_ref, dtp_ref, dtpT_ref, bias_ref, biasT_ref, alog_ref, alogT_ref, dsk_ref, dskT_ref,
             st_ref, dy_ref, dxs_ref, dB_ref, dC_ref, ddtp_ref, dbias_ref, dalog_ref, dD_ref, dstate):
        ci = pl.program_id(1)

        @pl.when(ci == 0)
        def _():
            dstate[...] = jnp.zeros(dstate.shape, F32)

        dt, A, causal, row, col, acum, acumT, Bm, Cm, cb = _ssd_common(
            dtp_ref, dtpT_ref, bias_ref, biasT_ref, alog_ref, alogT_ref, b_ref, c_ref)
        tn = (((0,), (0,)), ((), ()))
        nt = (((1,), (1,)), ((), ()))
        hd = _ssd_heads(dskT_ref, acum, acumT, dt, Kh)
        ET, tail = hd["ET"], hd["tail"]
        cbT = lax.dot_general(Bm, Cm, nt, preferred_element_type=F32)
        causalT = row <= col
        xs = xs_ref[...].astype(F32)
        xdt_all = xs * hd["dt_all"]
        dyb = dy_ref[...]
        dy_all = dyb.astype(F32)
        S_all = st_ref[...]
        S_b = S_all.astype(BF16)
        dS_all = dstate[...]
        dS_b = dS_all.astype(BF16)
        CS_all = lax.dot_general(Cm, S_b, nt, preferred_element_type=F32)
        dyE_b = (dy_all * hd["ea_all"]).astype(BF16)
        dC_acc = jnp.dot(dyE_b, S_b, preferred_element_type=F32)
        dS_y = lax.dot_general(dyE_b, Cm, tn, preferred_element_type=F32)
        BdS_all = lax.dot_general(Bm, dS_b, nt, preferred_element_type=F32)
        dB_acc = jnp.dot((xdt_all * hd["tail_all"]).astype(BF16), dS_b, preferred_element_type=F32)
        dtail = _hi(xdt_all * BdS_all, ET)
        da_cols = _hi(dy_all * CS_all * hd["ea_all"], ET) - dtail * tail
        dss = _dot01_left(jnp.ones((8, N), F32), _dot01_left(hd["E"], dS_all * S_all, 2), 2, dims=nt)
        da_last = dss[0:1] * jnp.exp(hd["a_last"]) + jnp.sum(dtail * tail, axis=0, keepdims=True)
        rowi = lax.broadcasted_iota(jnp.int32, (Q, Kh), 0)
        da_cols = da_cols + jnp.where(rowi == Q - 1, da_last, 0.0)
        dstate[...] = hd["eL_b"] * dS_all + dS_y
        sum_mg = jnp.zeros((Q, Q), F32)
        sum_mgt = jnp.zeros((Q, Q), F32)
        dacc = jnp.zeros((Q, 128), F32)
        ddt_x = jnp.zeros((Q, Kh), F32)
        lane128 = lax.broadcasted_iota(jnp.int32, (Q, 128), 1)
        for ch in range(nch):
            cs = slice(ch * CW, (ch + 1) * CW)
            dyc = dyb[:, cs]
            xc_b = xdt_all[:, cs].astype(BF16)
            acc = hd["tail_all"][:, cs] * BdS_all[:, cs]
            for kk in range(hpc):
                k = ch * hpc + kk
                a_b = jnp.broadcast_to(acum[:, k:k + 1], (Q, Q))
                a_r = acumT[k:k + 1, :]
                decay = jnp.exp(jnp.where(causal, a_b - a_r, -jnp.inf))
                decayT = jnp.exp(jnp.where(causalT, a_r - a_b, -jnp.inf))
                dyk = dyc if hpc == 1 else jnp.where(_head_mask(Q, CW, kk), dyc, jnp.zeros_like(dyc))
                mg = decay * lax.dot_general(dyk, xc_b, nt, preferred_element_type=F32)
                mgt = decayT * lax.dot_general(xc_b, dyk, nt, preferred_element_type=F32)
                sum_mg = sum_mg + mg
                sum_mgt = sum_mgt + mgt
                onek = jnp.where(lane128 == k, 1.0, 0.0).astype(BF16)
                dk = mg * cb - mgt * cbT
                dk_hi = dk.astype(BF16)
                dk_lo = (dk - dk_hi.astype(F32)).astype(BF16)
                dacc = dacc + (jnp.dot(dk_hi, onek, preferred_element_type=F32)
                               + jnp.dot(dk_lo, onek, preferred_element_type=F32))
                acc = acc + jnp.dot((decayT * cbT).astype(BF16), dyk, preferred_element_type=F32)
            dxs_ref[:, cs] = (acc * hd["dt_all"][:, cs] + dy_all[:, cs] * hd["dsk_all"][:, cs]).astype(BF16)
            ddt_x = ddt_x + _hi(acc * xs[:, cs], ET[cs, :])
        da_cols = da_cols + dacc[:, :Kh]
        dD_row = jnp.sum(_hi(dy_all * xs, ET), axis=0, keepdims=True)
        dB_ref[...] = (dB_acc + jnp.dot(sum_mgt.astype(BF16), Cm, preferred_element_type=F32)).astype(BF16)
        dC_ref[...] = (dC_acc + jnp.dot(sum_mg.astype(BF16), Bm, preferred_element_type=F32)).astype(BF16)
        triu = (row <= col).astype(F32)
        ddtA = _dot01_left(triu, da_cols, 3)
        ddt = ddt_x + ddtA * A
        dpre = ddt * _sigmoid(dtp_ref[...] + bias_ref[...])
        ddtp_ref[...] = dpre
        dbias_v = jnp.sum(dpre, axis=0, keepdims=True)
        dalog_v = jnp.sum(ddtA * dt, axis=0, keepdims=True) * A

        @pl.when(ci == 0)
        def _():
            dbias_ref[...] = dbias_v
            dalog_ref[...] = dalog_v
            dD_ref[...] = dD_row

        @pl.when(ci > 0)
        def _():
            dbias_ref[...] += dbias_v
            dalog_ref[...] += dalog_v
            dD_ref[...] += dD_row

    vec_o = pl.BlockSpec((None, 1, Kh), lambda g, c: (g, 0, 0))
    return pl.pallas_call(
        body, grid=(G, nc),
        in_specs=_ssd_in_specs(Q, GP, N, Kh, DI, rev)
        + [pl.BlockSpec((None, None, GP, N), lambda g, c: (rev(c), g, 0, 0)),
           pl.BlockSpec((Q, GP), lambda g, c: (rev(c), g))],
        out_specs=[pl.BlockSpec((Q, GP), lambda g, c: (rev(c), g)), pl.BlockSpec((Q, N), lambda g, c: (rev(c), g)),
                   pl.BlockSpec((Q, N), lambda g, c: (rev(c), g)),
                   pl.BlockSpec((None, Q, Kh), lambda g, c: (g, rev(c), 0)), vec_o, vec_o, vec_o],
        out_shape=[jax.ShapeDtypeStruct((L, DI), BF16), jax.ShapeDtypeStruct((L, G * N), BF16),
                   jax.ShapeDtypeStruct((L, G * N), BF16), jax.ShapeDtypeStruct((G, L, Kh), F32)]
        + [jax.ShapeDtypeStruct((G, 1, Kh), F32)] * 3,
        scratch_shapes=[pltpu.VMEM((GP, N), F32)], compiler_params=_cp(("parallel", "arbitrary")),
        name="ssd_bwd")(xbc, xbc, xbc, dtp_g, dtp_gT, bias_g, bias_gT, alog_g, alog_gT, dsk_g, dsk_gT, states, dy)


def _rms_groups(y2, ng_ref, DI):
    S = DI // SSD_G
    for g in range(SSD_G):
        gs = slice(g * S, (g + 1) * S)
        seg = y2[:, gs]
        r = lax.rsqrt(jnp.mean(seg * seg, axis=-1, keepdims=True) + RMS_EPS)
        yield gs, seg * r, r, ng_ref[:, gs]


def rms_gate_fwd(y, zx, norm_g):
    L, DI = y.shape
    tr = _tile(L, 256, 16)

    def body(y_ref, z_ref, ng_ref, o_ref):
        y2 = y_ref[...].astype(F32) * _silu(z_ref[...].astype(F32))
        for gs, yh, _, ng in _rms_groups(y2, ng_ref, DI):
            o_ref[:, gs] = (yh * ng).astype(BF16)

    return pl.pallas_call(
        body, grid=(L // tr,), in_specs=_row_specs(tr, [DI, DI]) + [_vec_spec(DI)], out_specs=_row_specs(tr, [DI])[0],
        out_shape=jax.ShapeDtypeStruct((L, DI), BF16), compiler_params=_cp(("parallel",)),
        name="rms_gate_fwd")(y, zx, norm_g)


def rms_gate_bwd(dyn, y, zx, norm_g):
    L, DI = y.shape
    tr = _tile(L, 256, 16)

    def body(dyn_ref, y_ref, z_ref, ng_ref, dy_ref, dz_ref, dng_ref):
        i = pl.program_id(0)
        yv = y_ref[...].astype(F32)
        zv = z_ref[...].astype(F32)
        sz = _silu(zv)
        dsz = _dsilu(zv)
        dynv = dyn_ref[...].astype(F32)
        for gs, yh, r, ng in _rms_groups(yv * sz, ng_ref, DI):
            dyh = dynv[:, gs] * ng
            dy2 = r * (dyh - yh * jnp.mean(dyh * yh, axis=-1, keepdims=True))
            dy_ref[:, gs] = (dy2 * sz[:, gs]).astype(BF16)
            dz_ref[:, gs] = (dy2 * yv[:, gs] * dsz[:, gs]).astype(BF16)
            s = jnp.sum(dynv[:, gs] * yh, axis=0, keepdims=True)

            @pl.when(i == 0)
            def _():
                dng_ref[:, gs] = s

            @pl.when(i > 0)
            def _():
                dng_ref[:, gs] += s

    return pl.pallas_call(
        body, grid=(L // tr,), in_specs=_row_specs(tr, [DI, DI, DI]) + [_vec_spec(DI)],
        out_specs=_row_specs(tr, [DI, DI]) + [_vec_spec(DI)],
        out_shape=[jax.ShapeDtypeStruct((L, DI), BF16)] * 2 + [jax.ShapeDtypeStruct((1, DI), F32)],
        compiler_params=_cp(("arbitrary",)), name="rms_gate_bwd")(dyn, y, zx, norm_g)


def _alibi_slope(gi, h):
    n = len(DIL_PATTERNS) * DIL_H
    return float(2.0 ** (-8.0 * (gi * DIL_H + h + 1) / n))


def _attn_masks():
    qi = lax.broadcasted_iota(jnp.int32, (DIL_BLK, DIL_BLK), 0)
    kj = lax.broadcasted_iota(jnp.int32, (DIL_BLK, DIL_BLK), 1)
    dcur = (qi - kj).astype(F32)
    return dcur, qi >= kj, dcur + float(DIL_BLK), kj >= qi


def _dil_cols(arr, col0, d):
    HW = DIL_H * DIL_E
    if d == 1:
        return arr, arr.shape[1] // HW, col0 // HW
    return arr[:, col0:col0 + HW].reshape(arr.shape[0] // d, d * HW), 1, 0


def attn_fwd(qz, kv, gi):
    window, d = DIL_PATTERNS[gi]
    assert window // d == DIL_BLK
    L, QZ = qz.shape
    KV = kv.shape[1]
    HW = DIL_H * DIL_E
    M = L // d
    nb = M // DIL_BLK
    nq, nkv = QZ // HW, KV // HW
    scale = DIL_E ** -0.5
    nt = (((1,), (1,)), ((), ()))

    def body(q_ref, kp_ref, kc_ref, vp_ref, vc_ref, o_ref, lse_ref):
        n = pl.program_id(1)
        dcur, vcur, dprev, vprev0 = _attn_masks()
        vprev = jnp.logical_and(vprev0, n > 0)
        lane = lax.broadcasted_iota(jnp.int32, (DIL_BLK, 128), 1)
        lse_acc = jnp.zeros((DIL_BLK, 128), F32)
        for h in range(DIL_H):
            hs = slice(h * DIL_E, (h + 1) * DIL_E)
            sl = _alibi_slope(gi, h) * d
            q = q_ref[:, hs]
            s_c = lax.dot_general(q, kc_ref[:, hs], nt, preferred_element_type=F32) * scale - sl * dcur
            s_p = lax.dot_general(q, kp_ref[:, hs], nt, preferred_element_type=F32) * scale - sl * dprev
            s_c = jnp.where(vcur, s_c, -jnp.inf)
            s_p = jnp.where(vprev, s_p, -jnp.inf)
            m = jnp.maximum(jnp.max(s_c, axis=-1, keepdims=True), jnp.max(s_p, axis=-1, keepdims=True))
            p_c = jnp.exp(s_c - m)
            p_p = jnp.exp(s_p - m)
            den = jnp.sum(p_c, axis=-1, keepdims=True) + jnp.sum(p_p, axis=-1, keepdims=True)
            o = (jnp.dot(p_c.astype(BF16), vc_ref[:, hs], preferred_element_type=F32)
                 + jnp.dot(p_p.astype(BF16), vp_ref[:, hs], preferred_element_type=F32)) / den
            o_ref[:, hs] = o.astype(BF16)
            lse_acc = jnp.where(lane == h, m + jnp.log(den), lse_acc)
        lse_ref[...] = lse_acc

    blk = (DIL_BLK, HW)
    prev = lambda n: jnp.maximum(n - 1, 0)
    qv, qn, qo = _dil_cols(qz, gi * HW, d)
    kv_, kn, ko = _dil_cols(kv, gi * HW, d)
    vv, vn, vo = _dil_cols(kv, (nkv // 2 + gi) * HW, d)
    o, lse = pl.pallas_call(
        body, grid=(d, nb),
        in_specs=[pl.BlockSpec(blk, lambda r, n: (n, r * qn + qo)),
                  pl.BlockSpec(blk, lambda r, n: (prev(n), r * kn + ko)),
                  pl.BlockSpec(blk, lambda r, n: (n, r * kn + ko)),
                  pl.BlockSpec(blk, lambda r, n: (prev(n), r * vn + vo)),
                  pl.BlockSpec(blk, lambda r, n: (n, r * vn + vo))],
        out_specs=[pl.BlockSpec(blk, lambda r, n: (n, r)), pl.BlockSpec((DIL_BLK, 128), lambda r, n: (n, r))],
        out_shape=[jax.ShapeDtypeStruct((M, d * HW), BF16), jax.ShapeDtypeStruct((M, d * 128), F32)],
        compiler_params=_cp(("parallel", "parallel")), name=f"attn_fwd_{gi}")(qv, kv_, kv_, vv, vv)
    return o.reshape(L, HW), lse.reshape(L, 128)


def attn_bwd(qz, kv, do, lse, dpr, gi):
    window, d = DIL_PATTERNS[gi]
    L, QZ = qz.shape
    KV = kv.shape[1]
    HW = DIL_H * DIL_E
    M = L // d
    nb = M // DIL_BLK
    nq, nkv = QZ // HW, KV // HW
    scale = DIL_E ** -0.5
    nt = (((1,), (1,)), ((), ()))
    tn = (((0,), (0,)), ((), ()))

    def body(q0_ref, q1_ref, k_ref, v_ref, do0_ref, do1_ref, l0_ref, l1_ref, r0_ref, r1_ref,
             dq_ref, dk_ref, dv_ref, carry):
        n = pl.program_id(1)

        @pl.when(n == 0)
        def _():
            carry[...] = jnp.zeros(carry.shape, F32)

        dcur, vcur, dprev, vprev0 = _attn_masks()
        vprev = jnp.logical_and(vprev0, n < nb - 1)
        for h in range(DIL_H):
            hs = slice(h * DIL_E, (h + 1) * DIL_E)
            sl = _alibi_slope(gi, h) * d
            kh = k_ref[:, hs]
            vh = v_ref[:, hs]
            q0, q1 = q0_ref[:, hs], q1_ref[:, hs]
            do0, do1 = do0_ref[:, hs], do1_ref[:, hs]
            s0 = lax.dot_general(q0, kh, nt, preferred_element_type=F32) * scale - sl * dcur
            p0 = jnp.exp(jnp.where(vcur, s0 - l0_ref[:, h:h + 1], -jnp.inf))
            ds0 = p0 * (lax.dot_general(do0, vh, nt, preferred_element_type=F32) - r0_ref[:, h:h + 1])
            s1 = lax.dot_general(q1, kh, nt, preferred_element_type=F32) * scale - sl * dprev
            p1 = jnp.exp(jnp.where(vprev, s1 - l1_ref[:, h:h + 1], -jnp.inf))
            ds1 = p1 * (lax.dot_general(do1, vh, nt, preferred_element_type=F32) - r1_ref[:, h:h + 1])
            ds0_b = (ds0 * scale).astype(BF16)
            ds1_b = (ds1 * scale).astype(BF16)
            dv = (lax.dot_general(p0.astype(BF16), do0, tn, preferred_element_type=F32)
                  + lax.dot_general(p1.astype(BF16), do1, tn, preferred_element_type=F32))
            dk = (lax.dot_general(ds0_b, q0, tn, preferred_element_type=F32)
                  + lax.dot_general(ds1_b, q1, tn, preferred_element_type=F32))
            dv_ref[:, hs] = dv.astype(BF16)
            dk_ref[:, hs] = dk.astype(BF16)
            dq_ref[:, hs] = (carry[:, hs] + jnp.dot(ds0_b, kh, preferred_element_type=F32)).astype(BF16)
            carry[:, hs] = jnp.dot(ds1_b, kh, preferred_element_type=F32)

    blk = (DIL_BLK, HW)
    sblk = (DIL_BLK, 128)
    nxt = lambda n: jnp.minimum(n + 1, nb - 1)
    qv, qn, qo = _dil_cols(qz, gi * HW, d)
    kv_, kn, ko = _dil_cols(kv, gi * HW, d)
    vv, vn, vo = _dil_cols(kv, (nkv // 2 + gi) * HW, d)
    dov = do.reshape(M, d * HW)
    lv = lse.reshape(M, d * 128)
    rv = dpr.reshape(M, d * 128)
    outs = pl.pallas_call(
        body, grid=(d, nb),
        in_specs=[pl.BlockSpec(blk, lambda r, n: (n, r * qn + qo)), pl.BlockSpec(blk, lambda r, n: (nxt(n), r * qn + qo)),
                  pl.BlockSpec(blk, lambda r, n: (n, r * kn + ko)),
                  pl.BlockSpec(blk, lambda r, n: (n, r * vn + vo)),
                  pl.BlockSpec(blk, lambda r, n: (n, r)), pl.BlockSpec(blk, lambda r, n: (nxt(n), r)),
                  pl.BlockSpec(sblk, lambda r, n: (n, r)), pl.BlockSpec(sblk, lambda r, n: (nxt(n), r)),
                  pl.BlockSpec(sblk, lambda r, n: (n, r)), pl.BlockSpec(sblk, lambda r, n: (nxt(n), r))],
        out_specs=[pl.BlockSpec(blk, lambda r, n: (n, r))] * 3,
        out_shape=[jax.ShapeDtypeStruct((M, d * HW), BF16)] * 3,
        scratch_shapes=[pltpu.VMEM(blk, F32)], compiler_params=_cp(("parallel", "arbitrary")),
        name=f"attn_bwd_{gi}")(qv, qv, kv_, vv, dov, dov, lv, lv, rv, rv)
    return [t.reshape(L, HW) for t in outs]


def _merge_weights(l_refs, h):
    ls = [r[:, h:h + 1] for r in l_refs]
    mx = functools.reduce(jnp.maximum, ls)
    es = [jnp.exp(l - mx) for l in ls]
    den = functools.reduce(lambda a, b: a + b, es)
    return [e / den for e in es]


def merge_fwd(os_, lses, qz):
    L, HW = os_[0].shape
    tr = _tile(L, 256, 16)
    ng = len(os_)
    zblk = qz.shape[1] // HW - 1

    def body(*refs):
        o_refs, l_refs, z_ref, out_ref = refs[:ng], refs[ng:2 * ng], refs[2 * ng], refs[2 * ng + 1]
        for h in range(DIL_H):
            hs = slice(h * DIL_E, (h + 1) * DIL_E)
            ws = _merge_weights(l_refs, h)
            om = functools.reduce(lambda a, b: a + b, [w * o[:, hs].astype(F32) for w, o in zip(ws, o_refs)])
            out_ref[:, hs] = (om * _silu(z_ref[:, hs].astype(F32))).astype(BF16)

    return pl.pallas_call(
        body, grid=(L // tr,),
        in_specs=_row_specs(tr, [HW] * ng + [128] * ng) + [pl.BlockSpec((tr, HW), lambda i: (i, zblk))],
        out_specs=_row_specs(tr, [HW])[0], out_shape=jax.ShapeDtypeStruct((L, HW), BF16),
        compiler_params=_cp(("parallel",)), name="merge_fwd")(*os_, *lses, qz)


def merge_bwd(dgated, os_, lses, qz):
    L, HW = os_[0].shape
    tr = _tile(L, 256, 16)
    ng = len(os_)
    zblk = qz.shape[1] // HW - 1

    def body(*refs):
        dg_ref = refs[0]
        o_refs, l_refs, z_ref = refs[1:1 + ng], refs[1 + ng:1 + 2 * ng], refs[1 + 2 * ng]
        outs = refs[2 + 2 * ng:]
        do_refs, dpr_refs, dz_ref = outs[:ng], outs[ng:2 * ng], outs[2 * ng]
        lane = lax.broadcasted_iota(jnp.int32, (tr, 128), 1)
        accs = [jnp.zeros((tr, 128), F32) for _ in range(ng)]
        for h in range(DIL_H):
            hs = slice(h * DIL_E, (h + 1) * DIL_E)
            ws = _merge_weights(l_refs, h)
            ov = [o[:, hs].astype(F32) for o in o_refs]
            om = functools.reduce(lambda a, b: a + b, [w * o for w, o in zip(ws, ov)])
            zv = z_ref[:, hs].astype(F32)
            dgv = dg_ref[:, hs].astype(F32)
            dom = dgv * _silu(zv)
            dz_ref[:, hs] = (dgv * om * _dsilu(zv)).astype(BF16)
            dws = [jnp.sum(dom * o, axis=-1, keepdims=True) for o in ov]
            dwbar = functools.reduce(lambda a, b: a + b, [w * dw for w, dw in zip(ws, dws)])
            for g in range(ng):
                do_refs[g][:, hs] = (ws[g] * dom).astype(BF16)
                accs[g] = jnp.where(lane == h, ws[g] * dwbar, accs[g])
        for g in range(ng):
            dpr_refs[g][...] = accs[g]

    outs = pl.pallas_call(
        body, grid=(L // tr,),
        in_specs=_row_specs(tr, [HW] * (1 + ng) + [128] * ng) + [pl.BlockSpec((tr, HW), lambda i: (i, zblk))],
        out_specs=_row_specs(tr, [HW] * ng + [128] * ng + [HW]),
        out_shape=[jax.ShapeDtypeStruct((L, HW), BF16)] * ng + [jax.ShapeDtypeStruct((L, 128), F32)] * ng
        + [jax.ShapeDtypeStruct((L, HW), BF16)],
        compiler_params=_cp(("parallel",)), name="merge_bwd")(dgated, *os_, *lses, qz)
    return outs[:ng], outs[ng:2 * ng], outs[2 * ng]


def ada_fwd(c8, ada_w):
    nl, D, Ws = ada_w.shape
    tn = _tile(Ws, 512)

    def body(c_ref, w_ref, o_ref):
        o_ref[...] = jnp.dot(_silu(c_ref[...]), w_ref[...], precision=lax.Precision.HIGHEST,
                             preferred_element_type=F32)

    return pl.pallas_call(
        body, grid=(nl, Ws // tn),
        in_specs=[pl.BlockSpec((N_DEV, D), lambda l, j: (0, 0)), pl.BlockSpec((None, D, tn), lambda l, j: (l, 0, j))],
        out_specs=pl.BlockSpec((None, N_DEV, tn), lambda l, j: (l, 0, j)),
        out_shape=jax.ShapeDtypeStruct((nl, N_DEV, Ws), F32), compiler_params=_cp(("parallel", "parallel")),
        name="ada_fwd")(c8, ada_w)


def ada_wgrad(c8t, dmod):
    nl, _, Ws = dmod.shape
    D = c8t.shape[0]
    tm = _tile(D, 512, 8)

    def body(c_ref, d_ref, o_ref):
        sc = _silu(c_ref[...])
        acc = sc[:, 0:1] * d_ref[0:1, :]
        for e in range(1, N_DEV):
            acc = acc + sc[:, e:e + 1] * d_ref[e:e + 1, :]
        o_ref[...] = acc

    return pl.pallas_call(
        body, grid=(nl, D // tm),
        in_specs=[pl.BlockSpec((tm, N_DEV), lambda l, i: (i, 0)), pl.BlockSpec((None, N_DEV, Ws), lambda l, i: (l, 0, 0))],
        out_specs=pl.BlockSpec((None, tm, Ws), lambda l, i: (l, i, 0)),
        out_shape=jax.ShapeDtypeStruct((nl, D, Ws), F32), compiler_params=_cp(("parallel", "parallel")),
        name="ada_wgrad")(c8t, dmod)


def adamw(w, g, m, v, name):
    R, C = w.shape
    tr = _tile(R, 256, 8)
    c1 = 1.0 - ADAM_B1 ** ADAM_STEP
    c2 = 1.0 - ADAM_B2 ** ADAM_STEP

    def body(w_ref, g_ref, m_ref, v_ref, d_ref, nm_ref, nv_ref):
        gv = g_ref[...]
        nm = ADAM_B1 * m_ref[...] + (1.0 - ADAM_B1) * gv
        nv = ADAM_B2 * v_ref[...] + (1.0 - ADAM_B2) * (gv * gv)
        nm_ref[...] = nm
        nv_ref[...] = nv
        d_ref[...] = -ADAM_LR * ((nm / c1) / (jnp.sqrt(nv / c2) + ADAM_EPS) + ADAM_WD * w_ref[...])

    return pl.pallas_call(
        body, grid=(R // tr,), in_specs=_row_specs(tr, [C] * 4), out_specs=_row_specs(tr, [C] * 3),
        out_shape=[jax.ShapeDtypeStruct((R, C), F32)] * 3, compiler_params=_cp(("parallel",)), name=name)(w, g, m, v)


def sum_leading(t, name, out_dtype=F32):
    S, R, C = t.shape
    tr = _tile(R, 256, 16)

    def body(t_ref, o_ref):
        acc = t_ref[0].astype(F32)
        for s in range(1, S):
            acc = acc + t_ref[s].astype(F32)
        o_ref[...] = acc.astype(out_dtype)

    return pl.pallas_call(
        body, grid=(R // tr,), in_specs=[pl.BlockSpec((S, tr, C), lambda i: (0, i, 0))],
        out_specs=pl.BlockSpec((tr, C), lambda i: (i, 0)), out_shape=jax.ShapeDtypeStruct((R, C), out_dtype),
        compiler_params=_cp(("parallel",)), name=name)(t)


def add_half(g, a, core, name):
    S, R, C = g.shape
    h = R // 2
    tr = _tile(h, 256, 16)
    nb = h // tr

    def body(core_ref, g_ref, a_ref, o_ref):
        o_ref[...] = (g_ref[...].astype(F32) + a_ref[...].astype(F32)).astype(BF16)

    return pl.pallas_call(
        body,
        grid_spec=pltpu.PrefetchScalarGridSpec(
            num_scalar_prefetch=1, grid=(S, nb),
            in_specs=[pl.BlockSpec((None, tr, C), lambda s, i, core_ref: (s, core_ref[0] * nb + i, 0)),
                      pl.BlockSpec((None, tr, C), lambda s, i, core_ref: (s, i, 0))],
            out_specs=pl.BlockSpec((None, tr, C), lambda s, i, core_ref: (s, i, 0))),
        out_shape=jax.ShapeDtypeStruct((S, h, C), BF16), compiler_params=_cp(("parallel", "parallel")),
        name=name)(core, g, a)


def sum_partials(own, landed, chip, name):
    _, h, C = own.shape
    tr = _tile(h, 256, 16)

    def body(chip_ref, own_ref, l_ref, o_ref):
        acc = own_ref[...].astype(F32)
        for j in range(3):
            acc = acc + l_ref[j].astype(F32)
        o_ref[...] = acc

    return pl.pallas_call(
        body,
        grid_spec=pltpu.PrefetchScalarGridSpec(
            num_scalar_prefetch=1, grid=(h // tr,),
            in_specs=[pl.BlockSpec((None, tr, C), lambda i, chip_ref: (chip_ref[0], i, 0)),
                      pl.BlockSpec((3, tr, C), lambda i, chip_ref: (0, i, 0))],
            out_specs=pl.BlockSpec((tr, C), lambda i, chip_ref: (i, 0))),
        out_shape=jax.ShapeDtypeStruct((h, C), F32), compiler_params=_cp(("parallel",)), name=name)(chip, own, landed)


def adamw_halves(w, g_mine, g_theirs, m, v, core, name):
    R, C = w.shape
    h = R // 2
    tr = _tile(h, 256, 8)
    nbh = h // tr
    c1 = 1.0 - ADAM_B1 ** ADAM_STEP
    c2 = 1.0 - ADAM_B2 ** ADAM_STEP

    def body(core_ref, w_ref, gm_ref, gt_ref, m_ref, v_ref, g_ref, d_ref, nm_ref, nv_ref):
        mine = (pl.program_id(0) // nbh) == core_ref[0]
        gv = jnp.where(mine, gm_ref[...], gt_ref[...])
        g_ref[...] = gv
        nm = ADAM_B1 * m_ref[...] + (1.0 - ADAM_B1) * gv
        nv = ADAM_B2 * v_ref[...] + (1.0 - ADAM_B2) * (gv * gv)
        nm_ref[...] = nm
        nv_ref[...] = nv
        d_ref[...] = -ADAM_LR * ((nm / c1) / (jnp.sqrt(nv / c2) + ADAM_EPS) + ADAM_WD * w_ref[...])

    full = pl.BlockSpec((tr, C), lambda i, core_ref: (i, 0))
    halfspec = pl.BlockSpec((tr, C), lambda i, core_ref: (i % nbh, 0))
    return pl.pallas_call(
        body,
        grid_spec=pltpu.PrefetchScalarGridSpec(
            num_scalar_prefetch=1, grid=(2 * nbh,), in_specs=[full, halfspec, halfspec, full, full],
            out_specs=[full] * 4),
        out_shape=[jax.ShapeDtypeStruct((R, C), F32)] * 4, compiler_params=_cp(("parallel",)),
        name=name)(core, w, g_mine, g_theirs, m, v)


_ANY = pl.BlockSpec(memory_space=pl.ANY)


def _place():
    x, y, c = lax.axis_index("x"), lax.axis_index("y"), lax.axis_index("c")
    chips = [(1 - x, y), (x, 1 - y), (1 - x, 1 - y)]
    return x, y, c, chips


def allgather_small(v, name):
    R, W = v.shape

    def body(x_ref, out_ref, send_sems, recv_sems, local_sem):
        x, y, c, chips = _place()
        me, sibling = (x, y, c), (x, y, 1 - c)

        def rows(px, py, pc):
            return out_ref.at[pl.ds((4 * px + 2 * py + pc) * R, R), :]

        def copy(k, block, to, src=None):
            return pltpu.make_async_remote_copy(
                src_ref=rows(*block) if src is None else src, dst_ref=rows(*block),
                send_sem=send_sems.at[k], recv_sem=recv_sems.at[k], device_id=to, device_id_type=MESH)

        mine = pltpu.make_async_copy(x_ref, rows(*me), local_sem)
        mine.start()
        first = [copy(0, me, sibling, src=x_ref)]
        first += [copy(1 + j, me, (*chip, c), src=x_ref) for j, chip in enumerate(chips)]
        for cp in first:
            cp.start()
        passed = [copy(4 + j, (*chip, c), sibling) for j, chip in enumerate(chips)]
        for j, chip in enumerate(chips):
            copy(1 + j, (*chip, c), me).wait_recv()
            passed[j].start()
        copy(0, sibling, me).wait_recv()
        for j, chip in enumerate(chips):
            copy(4 + j, (*chip, 1 - c), me).wait_recv()
        for cp in first + passed:
            cp.wait_send()
        mine.wait()

    return pl.pallas_call(
        body, out_shape=jax.ShapeDtypeStruct((N_DEV * R, W), v.dtype),
        in_specs=[pl.BlockSpec(memory_space=pltpu.VMEM)], out_specs=pl.BlockSpec(memory_space=pltpu.VMEM),
        scratch_shapes=[pltpu.SemaphoreType.DMA((7,)), pltpu.SemaphoreType.DMA((7,)), pltpu.SemaphoreType.DMA],
        name=name)(v)


def allgather_weights(shards, name="allgather_weights"):
    n = len(shards)

    def body(*refs):
        ins, outs = refs[:n], refs[n:2 * n]
        send_sems, recv_sems = refs[2 * n:]
        x, y, c, chips = _place()
        p = 2 * x + y
        sibling = (x, y, 1 - c)

        def half(i, chip_id, core, ref=None):
            r = outs[i].at[chip_id] if ref is None else ref
            return r.at[core]

        def copy(i, k, chip_id, core, to, src=None):
            return pltpu.make_async_remote_copy(
                src_ref=half(i, chip_id, core) if src is None else src, dst_ref=half(i, chip_id, core),
                send_sem=send_sems.at[6 * i + k], recv_sem=recv_sems.at[6 * i + k], device_id=to, device_id_type=MESH)

        first = [copy(i, j, p, c, (*chip, c), src=half(i, p, c, ref=ins[i]))
                 for i in range(n) for j, chip in enumerate(chips)]
        for cp in first:
            cp.start()
        passed = []
        for i in range(n):
            for j, (cx, cy) in enumerate(chips):
                copy(i, j, 2 * cx + cy, c, sibling).wait_recv()
                fw = copy(i, 3 + j, 2 * cx + cy, c, sibling)
                fw.start()
                passed.append(fw)
        for i in range(n):
            for j, (cx, cy) in enumerate(chips):
                copy(i, 3 + j, 2 * cx + cy, 1 - c, sibling).wait_recv()
        for cp in first + passed:
            cp.wait_send()

    split = [s.reshape(2, s.shape[0] // 2, s.shape[1]) for s in shards]
    outs = pl.pallas_call(
        body, out_shape=[jax.ShapeDtypeStruct((N_CHIPS,) + s.shape, s.dtype) for s in split],
        in_specs=[_ANY] * n, out_specs=[_ANY] * n,
        scratch_shapes=[pltpu.SemaphoreType.DMA((6 * n,)), pltpu.SemaphoreType.DMA((6 * n,))],
        name=name)(*split)
    chip = 2 * lax.axis_index("x") + lax.axis_index("y")
    return [lax.dynamic_update_index_in_dim(o, s, chip, 0).reshape((N_CHIPS,) + sh.shape)
            for o, s, sh in zip(outs, split, shards)]


_HBM = pl.BlockSpec(memory_space=pltpu.HBM)
_SEM = pl.BlockSpec(memory_space=pltpu.SEMAPHORE)
_EFFECT = pltpu.SideEffectType.DATAFLOW_SIDE_EFFECTING


def _chip_copies(kind, srcs, lands, send_sems, recv_sems):
    x, y, c, chips = _place()
    p = 2 * x + y
    cps = []
    for i in range(len(srcs)):
        for j, (cx, cy) in enumerate(chips):
            if kind == "gather":
                src, dst = srcs[i].at[c], lands[i].at[p, c]
            else:
                src, dst = srcs[i].at[2 * cx + cy], lands[i].at[j]
            cps.append(pltpu.make_async_remote_copy(
                src_ref=src, dst_ref=dst, send_sem=send_sems.at[3 * i + j], recv_sem=recv_sems.at[3 * i + j],
                device_id=(cx, cy, c), device_id_type=MESH))
    return cps


def split_start(kind, srcs, land_shapes, after, name):
    n = len(srcs)

    def body(*refs):
        src_refs, land_refs = refs[:n], refs[n:2 * n]
        send_sems, recv_sems = refs[2 * n + 1], refs[2 * n + 2]
        token = refs[-1]
        for cp in _chip_copies(kind, src_refs, land_refs, send_sems, recv_sems):
            cp.start()
        token[...] = jnp.zeros_like(token)

    lands = [pltpu.with_memory_space_constraint(lax.empty(s, BF16), pltpu.HBM) for s in land_shapes]
    outs = pl.pallas_call(
        body, name=name,
        out_shape=(pltpu.SemaphoreType.DMA((3 * n,)), pltpu.SemaphoreType.DMA((3 * n,)),
                   *[pltpu.HBM(s.shape, s.dtype) for s in srcs], *[pltpu.HBM(s, BF16) for s in land_shapes],
                   jax.ShapeDtypeStruct((8, 128), F32)),
        in_specs=[_HBM] * (2 * n) + [_ANY],
        out_specs=(_SEM, _SEM, *([_HBM] * (2 * n)), pl.BlockSpec(memory_space=pltpu.VMEM)),
        input_output_aliases={i: 2 + i for i in range(2 * n)},
        compiler_params=pltpu.CompilerParams(has_side_effects=_EFFECT),
    )(*[pltpu.with_memory_space_constraint(s, pltpu.HBM) for s in srcs], *lands, after)
    return outs[0], outs[1], outs[2:2 + n], outs[2 + n:2 + 2 * n], outs[-1]


def split_wait(kind, send_sems, recv_sems, srcs, lands, after, name):
    n = len(srcs)

    def body(*refs):
        src_refs, land_refs = refs[:n], refs[n:2 * n]
        ssem, rsem = refs[2 * n], refs[2 * n + 1]
        for cp in _chip_copies(kind, src_refs, land_refs, ssem, rsem):
            cp.wait_send()
            cp.wait_recv()

    outs = pl.pallas_call(
        body, name=name,
        out_shape=[pltpu.HBM(s.shape, s.dtype) for s in srcs] + [pltpu.HBM(s.shape, s.dtype) for s in lands],
        in_specs=[_HBM] * (2 * n) + [_SEM, _SEM, _ANY], out_specs=[_HBM] * (2 * n),
        input_output_aliases={i: i for i in range(2 * n)},
        compiler_params=pltpu.CompilerParams(has_side_effects=_EFFECT),
    )(*srcs, *lands, send_sems, recv_sems, after)
    return outs[:n], outs[n:]


def pass_to_sibling(lands):
    n = len(lands)

    def body(*refs):
        ins, outs = refs[:n], refs[n:2 * n]
        send_sems, recv_sems = refs[2 * n:]
        x, y, c, chips = _place()
        cps = []
        for i in range(n):
            for j, (cx, cy) in enumerate(chips):
                blk = outs[i].at[2 * cx + cy, c]
                cps.append(pltpu.make_async_remote_copy(
                    src_ref=ins[i].at[2 * cx + cy, c], dst_ref=blk, send_sem=send_sems.at[3 * i + j],
                    recv_sem=recv_sems.at[3 * i + j], device_id=(x, y, 1 - c), device_id_type=MESH))
        for cp in cps:
            cp.start()
        for cp in cps:
            cp.wait()

    return pl.pallas_call(
        body, out_shape=[jax.ShapeDtypeStruct(t.shape, t.dtype) for t in lands], in_specs=[_ANY] * n,
        out_specs=[_ANY] * n, input_output_aliases={i: i for i in range(n)},
        scratch_shapes=[pltpu.SemaphoreType.DMA((3 * n,)), pltpu.SemaphoreType.DMA((3 * n,))],
        name="ag_pass_to_sibling")(*lands)


def exchange_halves_to_sibling(gs, name):
    n = len(gs)

    def body(*refs):
        ins, outs = refs[:n], refs[n:2 * n]
        send_sems, recv_sems = refs[2 * n:]
        x, y, c, _ = _place()
        cps = []
        for i in range(n):
            h = ins[i].shape[1] // 2
            cps.append(pltpu.make_async_remote_copy(
                src_ref=ins[i].at[:, pl.ds((1 - c) * h, h), :], dst_ref=outs[i],
                send_sem=send_sems.at[i], recv_sem=recv_sems.at[i], device_id=(x, y, 1 - c), device_id_type=MESH))
        for cp in cps:
            cp.start()
        for cp in cps:
            cp.wait()

    return pl.pallas_call(
        body, out_shape=[jax.ShapeDtypeStruct((g.shape[0], g.shape[1] // 2, g.shape[2]), g.dtype) for g in gs],
        in_specs=[_ANY] * n, out_specs=[_ANY] * n,
        scratch_shapes=[pltpu.SemaphoreType.DMA((n,)), pltpu.SemaphoreType.DMA((n,))],
        name=name)(*gs)


def scatter_to_chips(ps, name):
    n = len(ps)

    def body(*refs):
        ins, outs = refs[:n], refs[n:2 * n]
        send_sems, recv_sems = refs[2 * n:]
        x, y, c, chips = _place()
        cps = []
        for i in range(n):
            for j, (cx, cy) in enumerate(chips):
                cps.append(pltpu.make_async_remote_copy(
                    src_ref=ins[i].at[2 * cx + cy], dst_ref=outs[i].at[j], send_sem=send_sems.at[3 * i + j],
                    recv_sem=recv_sems.at[3 * i + j], device_id=(cx, cy, c), device_id_type=MESH))
        for cp in cps:
            cp.start()
        for cp in cps:
            cp.wait()

    return pl.pallas_call(
        body, out_shape=[jax.ShapeDtypeStruct((3,) + t.shape[1:], t.dtype) for t in ps],
        in_specs=[_ANY] * n, out_specs=[_ANY] * n,
        scratch_shapes=[pltpu.SemaphoreType.DMA((3 * n,)), pltpu.SemaphoreType.DMA((3 * n,))],
        name=name)(*ps)


def join_halves(rs, name):
    n = len(rs)

    def body(*refs):
        ins, outs = refs[:n], refs[n:2 * n]
        send_sems, recv_sems = refs[2 * n:]
        x, y, c, _ = _place()
        cps = [pltpu.make_async_remote_copy(
            src_ref=ins[i], dst_ref=outs[i], send_sem=send_sems.at[i], recv_sem=recv_sems.at[i],
            device_id=(x, y, 1 - c), device_id_type=MESH) for i in range(n)]
        for cp in cps:
            cp.start()
        for cp in cps:
            cp.wait()

    return pl.pallas_call(
        body, out_shape=[jax.ShapeDtypeStruct(r.shape, r.dtype) for r in rs],
        in_specs=[_ANY] * n, out_specs=[_ANY] * n,
        scratch_shapes=[pltpu.SemaphoreType.DMA((n,)), pltpu.SemaphoreType.DMA((n,))],
        name=name)(*rs)


def _pack(parts, row_mult=8):
    flat = jnp.concatenate([p.reshape(-1).astype(F32) for p in parts])
    unit = row_mult * 128
    n = -(-flat.shape[0] // unit) * unit
    return jnp.pad(flat, (0, n - flat.shape[0])).reshape(n // 128, 128)


def _unpack(flat, shapes):
    out, off = [], 0
    for s in shapes:
        n = int(np.prod(s))
        out.append(flat[off:off + n].reshape(s))
        off += n
    return out


def _gather_packed(parts, name):
    packed = _pack(parts)
    g = allgather_small(packed, name).reshape(N_DEV, -1)
    return _unpack_rows(g, [p.shape for p in parts])


def _unpack_rows(g, shapes):
    out, off = [], 0
    for s in shapes:
        n = int(np.prod(s))
        out.append(g[:, off:off + n].reshape((g.shape[0],) + tuple(s)))
        off += n
    return out


def _by_chip(t, axis):
    return jnp.concatenate([t[2 * p] for p in range(N_CHIPS)], axis=axis)


def kernel(x, c, ada_w, ada_b, ln_g, ln_b, a_in_w, a_conv_w, a_conv_b, a_dt_bias, a_A_log, a_D, a_norm_g, a_out_w, kv_w, b_in_w, b_out_w, loss_target, m_ada_w, m_ada_b, m_ln_g, m_ln_b, m_a_in_w, m_a_conv_w, m_a_conv_b, m_a_dt_bias, m_a_A_log, m_a_D, m_a_norm_g, m_a_out_w, m_kv_w, m_b_in_w, m_b_out_w, v_ada_w, v_ada_b, v_ln_g, v_ln_b, v_a_in_w, v_a_conv_w, v_a_conv_b, v_a_dt_bias, v_a_A_log, v_a_D, v_a_norm_g, v_a_out_w, v_kv_w, v_b_in_w, v_b_out_w):
    ax, ay, ac = lax.axis_index("x"), lax.axis_index("y"), lax.axis_index("c")
    chip = 2 * ax + ay
    dev = 4 * ax + 2 * ay + ac
    xin = x[0]
    tgt = loss_target[0]
    L, D = xin.shape
    G, P = SSD_G, SSD_P
    H = a_dt_bias.shape[1]
    Kh = H // G
    DI = H * P
    CONVD = a_conv_b.shape[1] * N_CHIPS
    HW = DIL_H * DIL_E
    Ws = ada_w.shape[2]

    (w_in_g,) = allgather_weights([a_in_w[0].astype(BF16)], "allgather_w_in")
    later = [a_out_w[0].astype(BF16), kv_w.astype(BF16), b_in_w[0].astype(BF16), b_out_w[0].astype(BF16)]
    later_split = [s.reshape(2, s.shape[0] // 2, s.shape[1]) for s in later]
    ag_ssem, ag_rsem, ag_srcs, ag_lands, ag_token = split_start(
        "gather", later_split, [(N_CHIPS,) + s.shape for s in later_split], w_in_g, "ag_later_start")
    w_in = jnp.transpose(w_in_g, (1, 0, 2)).reshape(D, -1)
    w_zx = w_in[:, :DI + CONVD]
    w_dt = jnp.pad(w_in[:, DI + CONVD:], ((0, 0), (0, 128 - H)))

    c8, cw8, cb8, ng8 = _gather_packed([c[0], a_conv_w[0], a_conv_b[0], a_norm_g[0]], "allgather_small_params")
    conv_w = _by_chip(cw8, 1)
    conv_b = _by_chip(cb8, 0).reshape(1, CONVD)
    norm_g = _by_chip(ng8, 0).reshape(1, DI)

    mod_s = ada_fwd(c8, ada_w)
    (mod8,) = _gather_packed([mod_s], "allgather_small_mod")
    mods = _by_chip(mod8, 2)
    mod = lax.dynamic_index_in_dim(mods, dev, axis=1, keepdims=False) + ada_b
    shift = [mod[l:l + 1, :D] for l in range(DEPTH)]
    scale = [mod[l:l + 1, D:2 * D] for l in range(DEPTH)]
    gate = [mod[l:l + 1, 2 * D:] for l in range(DEPTH)]
    lg = [ln_g[l:l + 1] for l in range(DEPTH)]
    lb = [ln_b[l:l + 1] for l in range(DEPTH)]

    h0 = modulate(xin, scale[0] + ag_token[0:1, 0:1], shift[0], "modulate0")
    zx = mm_nn(h0, w_zx, BF16, "mm_in_zx")
    dtp = mm_nn(h0, w_dt, F32, "mm_in_dt")
    xbc = conv_fwd(zx, DI, conv_w, conv_b)
    dtp_g = jnp.transpose(dtp[:, :H].reshape(L, G, Kh), (1, 0, 2))
    dtp_gT = jnp.transpose(dtp_g, (0, 2, 1))
    vecs = [a_dt_bias.reshape(G, 1, Kh), a_dt_bias.reshape(G, Kh, 1), a_A_log.reshape(G, 1, Kh),
            a_A_log.reshape(G, Kh, 1), a_D.reshape(G, 1, Kh), a_D.reshape(G, Kh, 1)]
    y_ssd, states = ssd_fwd(xbc, dtp_g, dtp_gT, *vecs, DI)
    yn = rms_gate_fwd(y_ssd, zx, norm_g)
    later_split, ag_lands = split_wait("gather", ag_ssem, ag_rsem, ag_srcs, ag_lands, yn, "ag_later_wait")
    ag_lands = pass_to_sibling(ag_lands)
    w_out_g, w_kv_g, w_bin_g, w_bout_g = [
        lax.dynamic_update_index_in_dim(o, s, chip, 0).reshape((N_CHIPS,) + full.shape)
        for o, s, full in zip(ag_lands, later_split, later)]
    ymix0 = mm_nn(yn, w_out_g, F32, "mm_out_a", stack="row")
    x1, x1b, h1 = ln_mid(xin, ymix0, gate[0], lg[0], lb[0], scale[1], shift[1])

    kvp = mm_nn(x1b, w_kv_g, BF16, "mm_kv", stack="col")
    qz = mm_nn(h1, w_bin_g, BF16, "mm_in_b", stack="col")
    os_, lses = [], []
    for gi in range(len(DIL_PATTERNS)):
        o, lse = attn_fwd(qz, kvp, gi)
        os_.append(o)
        lses.append(lse)
    om = merge_fwd(os_, lses, qz)
    ymix1 = mm_nn(om, w_bout_g, F32, "mm_out_b", stack="col")
    dx2, sq = ln_final(x1, ymix1, gate[1], lg[1], lb[1], tgt)
    loss_part = 0.5 * jnp.sum(sq) / D

    dres2, dy2, dg1, db1, dgate1 = ln_bwd(dx2, x1, ymix1, gate[1], lg[1], "ln_bwd1")
    g_bout = mm_tn(om, dy2, BF16, "mm_gw_out_b", stack="col")
    dgated = mm_nt(dy2, w_bout_g, BF16, "mm_gx_out_b", stack="col")
    dos, dprs, dz_b = merge_bwd(dgated, os_, lses, qz)
    dqs, dks, dvs = [], [], []
    for gi in range(len(DIL_PATTERNS)):
        dq, dk, dv = attn_bwd(qz, kvp, dos[gi], lses[gi], dprs[gi], gi)
        dqs.append(dq)
        dks.append(dk)
        dvs.append(dv)
    dqz = jnp.concatenate(dqs + [dz_b], axis=1)
    dkv = jnp.concatenate(dks + dvs, axis=1)
    g_bin = mm_tn(h1, dqz, BF16, "mm_gw_in_b", stack="col")
    dh1 = mm_nt(dqz, w_bin_g, F32, "mm_gx_in_b", stack="col")
    g_kv = mm_tn(x1b, dkv, BF16, "mm_gw_kv", stack="col")
    dx1_kv = mm_nt(dkv, w_kv_g, F32, "mm_gx_kv", stack="col")
    dx1, dscale1, dshift1 = mod_bwd(dres2, dh1, dx1_kv, x1, scale[1], "mod_bwd1", through_mod=False)

    core = ac.astype(jnp.int32).reshape(1)
    chip_i = chip.astype(jnp.int32).reshape(1)

    def begin_scatter(gs, nms, tag):
        sib = exchange_halves_to_sibling(gs, "rs_sibling_exchange_" + tag)
        parts = [add_half(g, a, core, "rs_add_" + nm) for g, a, nm in zip(gs, sib, nms)]
        return split_start("scatter", parts, [(3,) + t.shape[1:] for t in parts], parts[0], "rs_%s_start" % tag)

    def finish_scatter(handles, after, tag):
        nms, owns, landed = [], [], []
        for k, (handle, hn) in enumerate(handles):
            parts, lands = split_wait("scatter", handle[0], handle[1], handle[2], handle[3], after,
                                      "rs_%s%d_wait" % (tag, k))
            nms += hn
            owns += list(parts)
            landed += list(lands)
        halves = [sum_partials(own, t, chip_i, "rs_sum_" + nm) for own, t, nm in zip(owns, landed, nms)]
        theirs = join_halves(halves, "rs_join_halves_" + tag)
        return dict(zip(nms, zip(halves, theirs)))

    names_b = ["kv", "in_b", "out_b"]
    rs_b = begin_scatter([g_kv, g_bin, g_bout], names_b, "b")

    dres1, dy1, dg0, db0, dgate0 = ln_bwd(dx1, xin, ymix0, gate[0] + rs_b[4][0:1, 0:1], lg[0], "ln_bwd0")
    g_out = mm_tn(yn, dy1, BF16, "mm_gw_out_a", stack="row")
    rs_a1 = begin_scatter([g_out], ["out_a"], "a1")
    dyn = mm_nt(dy1, w_out_g, BF16, "mm_gx_out_a", stack="row")
    dy_ssd, dz_a, dnorm_g = rms_gate_bwd(dyn, y_ssd, zx, norm_g + rs_a1[4][0:1, 0:1])
    dxs, dB, dC, ddtp_g, dbias_g, dalog_g, dD_g = ssd_bwd(xbc, dtp_g, dtp_gT, *vecs, states, dy_ssd, DI)
    dxbc = jnp.concatenate([dxs, dB, dC], axis=1)
    dxbc_pre, dconv_w, dconv_b = conv_bwd(zx, DI, conv_w, conv_b, dxbc)
    dzx = jnp.concatenate([dz_a, dxbc_pre], axis=1)
    ddtp = jnp.pad(jnp.transpose(ddtp_g, (1, 0, 2)).reshape(L, H), ((0, 0), (0, 128 - H)))
    g_zx = mm_tn(h0, dzx, BF16, "mm_gw_in_zx")
    g_dt = mm_tn(h0, ddtp, BF16, "mm_gw_in_dt")
    g_in = jnp.concatenate([g_zx, g_dt[:, :H]], axis=1)
    g_in = jnp.transpose(g_in.reshape(D, N_CHIPS, -1), (1, 0, 2))
    rs_a2 = begin_scatter([g_in], ["in_a"], "a2")
    dh0 = mm_nt(dzx, w_zx, F32, "mm_gx_in_zx")
    dh0_dt = mm_nt(ddtp, w_dt, F32, "mm_gx_in_dt")
    grad_x, dscale0, dshift0 = mod_bwd(dres1, dh0, dh0_dt, xin, scale[0] + rs_a2[4][0:1, 0:1], "mod_bwd0",
                                       through_mod=True)
    g_halves = finish_scatter([(rs_b, names_b)], grad_x, "b")

    dmod = jnp.concatenate([jnp.concatenate([dshift0, dscale0, dgate0], axis=1),
                            jnp.concatenate([dshift1, dscale1, dgate1], axis=1)], axis=0)
    small_parts = [jnp.concatenate([dg0, dg1], axis=0), jnp.concatenate([db0, db1], axis=0),
                   dbias_g.reshape(1, H), dalog_g.reshape(1, H), dD_g.reshape(1, H),
                   dconv_w, dconv_b, dnorm_g, loss_part.reshape(1, 1)]
    small_shapes = [p.shape for p in small_parts]
    packed = jnp.concatenate([_pack([dmod]), _pack(small_parts)], axis=0)
    n_mod_rows = _pack([dmod]).shape[0]
    gathered = allgather_small(packed, "allgather_small_grads").reshape(N_DEV, -1, 128)
    dmod8 = gathered[:, :n_mod_rows].reshape(N_DEV, -1)[:, :2 * 3 * D].reshape(N_DEV, DEPTH, 3 * D)
    summed = sum_leading(gathered, "sum_small")
    g_ada_b = summed[:n_mod_rows].reshape(-1)[:2 * 3 * D].reshape(DEPTH, 3 * D)
    (g_ln_g, g_ln_b, g_dt_bias, g_a_log, g_dsk, g_conv_w, g_conv_b, g_norm_g, loss_all) = _unpack(
        summed[n_mod_rows:].reshape(-1), small_shapes)
    loss = loss_all.reshape(())
    Cs = CONVD // N_CHIPS
    g_conv_w_s = lax.dynamic_slice_in_dim(g_conv_w, chip * Cs, Cs, axis=1)
    g_conv_b_s = lax.dynamic_slice_in_dim(g_conv_b, chip * Cs, Cs, axis=1)
    g_norm_g_s = lax.dynamic_slice_in_dim(g_norm_g, chip * (DI // N_CHIPS), DI // N_CHIPS, axis=1)
    dmod_s = jnp.transpose(lax.dynamic_slice_in_dim(dmod8, chip * Ws, Ws, axis=2), (1, 0, 2))
    g_ada_w = ada_wgrad(jnp.transpose(c8), dmod_s)

    def step2d(w, g, m, v, nm):
        shp = w.shape
        d_, m_, v_ = adamw(w.reshape(-1, shp[-1]), g.reshape(-1, shp[-1]), m.reshape(-1, shp[-1]),
                           v.reshape(-1, shp[-1]), "adamw_" + nm)
        return g.reshape(shp), d_.reshape(shp), m_.reshape(shp), v_.reshape(shp)

    def step_halves(w, m, v, nm):
        shp = w.shape
        mine, theirs_ = g_halves[nm]
        outs4 = adamw_halves(w.reshape(-1, shp[-1]), mine, theirs_, m.reshape(-1, shp[-1]), v.reshape(-1, shp[-1]),
                             core, "adamw_" + nm)
        return tuple(t.reshape(shp) for t in outs4)

    big = {
        "ada_w": step2d(ada_w, g_ada_w, m_ada_w, v_ada_w, "ada_w"),
        "kv_w": step_halves(kv_w, m_kv_w, v_kv_w, "kv"),
        "b_in_w": step_halves(b_in_w, m_b_in_w, v_b_in_w, "in_b"),
        "b_out_w": step_halves(b_out_w, m_b_out_w, v_b_out_w, "out_b"),
    }
    g_halves.update(finish_scatter([(rs_a1, ["out_a"]), (rs_a2, ["in_a"])], big["kv_w"][1], "a"))
    big["a_in_w"] = step_halves(a_in_w, m_a_in_w, v_a_in_w, "in_a")
    big["a_out_w"] = step_halves(a_out_w, m_a_out_w, v_a_out_w, "out_a")
    small_names = ["ada_b", "ln_g", "ln_b", "a_conv_w", "a_conv_b", "a_dt_bias", "a_A_log", "a_D", "a_norm_g"]
    small_w = [ada_b, ln_g, ln_b, a_conv_w, a_conv_b, a_dt_bias, a_A_log, a_D, a_norm_g]
    small_m = [m_ada_b, m_ln_g, m_ln_b, m_a_conv_w, m_a_conv_b, m_a_dt_bias, m_a_A_log, m_a_D, m_a_norm_g]
    small_v = [v_ada_b, v_ln_g, v_ln_b, v_a_conv_w, v_a_conv_b, v_a_dt_bias, v_a_A_log, v_a_D, v_a_norm_g]
    small_g = [g_ada_b, g_ln_g, g_ln_b, g_conv_w_s, g_conv_b_s, g_dt_bias, g_a_log, g_dsk, g_norm_g_s]
    shapes = [w.shape for w in small_w]
    small_g = [g.reshape(s) for g, s in zip(small_g, shapes)]
    d_p, m_p, v_p = adamw(_pack(small_w), _pack(small_g), _pack(small_m), _pack(small_v), "adamw_small")
    small = {}
    for nm, g, d_, m_, v_ in zip(small_names, small_g, _unpack(d_p.reshape(-1), shapes), _unpack(m_p.reshape(-1), shapes),
                                 _unpack(v_p.reshape(-1), shapes)):
        small[nm] = (g, d_, m_, v_)
    allw = {**big, **small}
    order = ["ada_w", "ada_b", "ln_g", "ln_b", "a_in_w", "a_conv_w", "a_conv_b", "a_dt_bias", "a_A_log", "a_D",
             "a_norm_g", "a_out_w", "kv_w", "b_in_w", "b_out_w"]
    outs = [loss, grad_x.reshape(x.shape)]
    for k in range(4):
        outs += [allw[n][k] for n in order]
    return tuple(outs)
```

```python
import functools

import jax
import jax.numpy as jnp
import numpy as np
from jax import lax
from jax.experimental import pallas as pl
from jax.experimental.pallas import tpu as pltpu

F32 = jnp.float32
BF16 = jnp.bfloat16
MESH = pl.DeviceIdType.MESH

DEPTH = 2
ALPHA = (2 * DEPTH) ** 0.25
LN_EPS = 1e-5
RMS_EPS = 1e-5
SSD_P = 64
SSD_N = 128
SSD_Q = 256
SSD_G = 8
CONV_W = 4
DIL_PATTERNS = ((128, 1), (512, 4), (2048, 16))
DIL_H = 8
DIL_E = 128
DIL_BLK = 128
ADAM_LR, ADAM_B1, ADAM_B2, ADAM_EPS, ADAM_WD, ADAM_STEP = 0.001, 0.9, 0.999, 1e-08, 0.01, 10

VMEM_LIMIT = 56 * 1024 * 1024
N_CHIPS = 4
N_DEV = 8


def _tile(dim, target, mult=128):
    if dim <= target:
        return dim
    t = (target // mult) * mult
    while t >= mult:
        if dim % t == 0:
            return t
        t -= mult
    return dim


def _cp(sem):
    return pltpu.CompilerParams(dimension_semantics=sem, vmem_limit_bytes=VMEM_LIMIT)


def _sigmoid(x):
    return 1.0 / (1.0 + jnp.exp(-x))


def _silu(x):
    return x * _sigmoid(x)


def _dsilu(x):
    s = _sigmoid(x)
    return s * (1.0 + x * (1.0 - s))


def _softplus(x):
    return jnp.maximum(x, 0.0) + jnp.log(1.0 + jnp.exp(-jnp.abs(x)))


def _mm_call(a, b, out_shape, grid, a_spec, b_spec, o_spec, acc_shape, dims, name):
    nk = grid[2]

    def prod(a_ref, b_ref):
        return lax.dot_general(a_ref[...].astype(BF16), b_ref[...].astype(BF16), (dims, ((), ())),
                               preferred_element_type=F32)

    def body_single(a_ref, b_ref, o_ref):
        o_ref[...] = prod(a_ref, b_ref).astype(o_ref.dtype)

    def body_multi(a_ref, b_ref, o_ref, acc_ref):
        k = pl.program_id(2)

        @pl.when(k == 0)
        def _():
            acc_ref[...] = prod(a_ref, b_ref)

        @pl.when(jnp.logical_and(k > 0, k < nk - 1))
        def _():
            acc_ref[...] += prod(a_ref, b_ref)

        @pl.when(k == nk - 1)
        def _():
            o_ref[...] = (acc_ref[...] + prod(a_ref, b_ref)).astype(o_ref.dtype)

    return pl.pallas_call(
        body_single if nk == 1 else body_multi, grid=grid, in_specs=[a_spec, b_spec], out_specs=o_spec,
        out_shape=out_shape, scratch_shapes=[] if nk == 1 else [pltpu.VMEM(acc_shape, F32)],
        compiler_params=_cp(("parallel", "parallel", "arbitrary")), name=name)(a, b)


def mm_nn(a, b, out_dtype, name, stack=None, tm=1024, tn=1024, tk=2048):
    M, K = a.shape
    if stack is None:
        N = b.shape[1]
        tn, tk = _tile(N, tn), _tile(K, tk)
        b_spec = pl.BlockSpec((tk, tn), lambda i, j, k: (k, j))
    elif stack == "col":
        S, _, Ns = b.shape
        N = S * Ns
        tn, tk = _tile(Ns, tn), _tile(K, tk)
        npb = Ns // tn
        b_spec = pl.BlockSpec((None, tk, tn), lambda i, j, k: (j // npb, k, j % npb))
    else:
        S, Ks, N = b.shape
        tn, tk = _tile(N, tn), _tile(Ks, tk)
        kpb = Ks // tk
        b_spec = pl.BlockSpec((None, tk, tn), lambda i, j, k: (k // kpb, k % kpb, j))
    tm = _tile(M, tm)
    return _mm_call(a, b, jax.ShapeDtypeStruct((M, N), out_dtype), (M // tm, N // tn, K // tk),
                    pl.BlockSpec((tm, tk), lambda i, j, k: (i, k)), b_spec,
                    pl.BlockSpec((tm, tn), lambda i, j, k: (i, j)), (tm, tn), ((1,), (0,)), name)


def mm_nt(a, b, out_dtype, name, stack=None, tm=1024, tn=1024, tk=2048):
    M, C = a.shape
    if stack is None:
        Kw = b.shape[0]
        tn, tk = _tile(Kw, tn), _tile(C, tk)
        b_spec = pl.BlockSpec((tn, tk), lambda i, j, k: (j, k))
    elif stack == "col":
        S, Kw, Cs = b.shape
        tn, tk = _tile(Kw, tn), _tile(Cs, tk)
        cpb = Cs // tk
        b_spec = pl.BlockSpec((None, tn, tk), lambda i, j, k: (k // cpb, j, k % cpb))
    else:
        S, Ks, _ = b.shape
        Kw = S * Ks
        tn, tk = _tile(Ks, tn), _tile(C, tk)
        jpb = Ks // tn
        b_spec = pl.BlockSpec((None, tn, tk), lambda i, j, k: (j // jpb, j % jpb, k))
    tm = _tile(M, tm)
    return _mm_call(a, b, jax.ShapeDtypeStruct((M, Kw), out_dtype), (M // tm, Kw // tn, C // tk),
                    pl.BlockSpec((tm, tk), lambda i, j, k: (i, k)), b_spec,
                    pl.BlockSpec((tm, tn), lambda i, j, k: (i, j)), (tm, tn), ((1,), (1,)), name)


def mm_tn(a, b, out_dtype, name, stack=None, n_stack=N_CHIPS, tm=1024, tn=1024, tk=2048):
    L, M = a.shape
    N = b.shape[1]
    tk = _tile(L, tk)
    if stack is None:
        tm, tn = _tile(M, tm), _tile(N, tn)
        o_spec = pl.BlockSpec((tm, tn), lambda i, j, k: (i, j))
        out_shape = (M, N)
    elif stack == "col":
        Ns = N // n_stack
        tm, tn = _tile(M, tm), _tile(Ns, tn)
        npb = Ns // tn
        o_spec = pl.BlockSpec((None, tm, tn), lambda i, j, k: (j // npb, i, j % npb))
        out_shape = (n_stack, M, Ns)
    else:
        Ms = M // n_stack
        tm, tn = _tile(Ms, tm), _tile(N, tn)
        mpb = Ms // tm
        o_spec = pl.BlockSpec((None, tm, tn), lambda i, j, k: (i // mpb, i % mpb, j))
        out_shape = (n_stack, Ms, N)
    return _mm_call(a, b, jax.ShapeDtypeStruct(out_shape, out_dtype), (M // tm, N // tn, L // tk),
                    pl.BlockSpec((tk, tm), lambda i, j, k: (k, i)), pl.BlockSpec((tk, tn), lambda i, j, k: (k, j)),
                    o_spec, (tm, tn), ((0,), (0,)), name)


def _row_specs(tr, widths):
    return [pl.BlockSpec((tr, w), lambda i: (i, 0)) for w in widths]


def _vec_spec(w):
    return pl.BlockSpec((1, w), lambda i: (0, 0))


def _acc_rows(ref, val, i):
    s = jnp.sum(val, axis=0, keepdims=True)

    @pl.when(i == 0)
    def _():
        ref[...] = s

    @pl.when(i > 0)
    def _():
        ref[...] += s


def modulate(x, scale, shift, name):
    L, D = x.shape
    tr = _tile(L, 512, 16)

    def body(x_ref, sc_ref, sh_ref, h_ref):
        h_ref[...] = (x_ref[...] * (1.0 + sc_ref[...]) + sh_ref[...]).astype(BF16)

    return pl.pallas_call(
        body, grid=(L // tr,), in_specs=_row_specs(tr, [D]) + [_vec_spec(D)] * 2, out_specs=_row_specs(tr, [D])[0],
        out_shape=jax.ShapeDtypeStruct((L, D), BF16), compiler_params=_cp(("parallel",)), name=name)(x, scale, shift)


def _ln_core(x, y, gate, g, b):
    u = ALPHA * x + (1.0 + gate) * y
    mu = jnp.mean(u, axis=-1, keepdims=True)
    d = u - mu
    var = jnp.mean(d * d, axis=-1, keepdims=True)
    rstd = lax.rsqrt(var + LN_EPS)
    xhat = d * rstd
    return xhat * g + b, xhat, rstd


def ln_mid(x, y, gate, g, b, scale, shift):
    L, D = x.shape
    tr = _tile(L, 256, 16)

    def body(x_ref, y_ref, gate_ref, g_ref, b_ref, sc_ref, sh_ref, x1_ref, x1b_ref, h_ref):
        x1, _, _ = _ln_core(x_ref[...], y_ref[...], gate_ref[...], g_ref[...], b_ref[...])
        x1_ref[...] = x1
        x1b_ref[...] = x1.astype(BF16)
        h_ref[...] = (x1 * (1.0 + sc_ref[...]) + sh_ref[...]).astype(BF16)

    return pl.pallas_call(
        body, grid=(L // tr,), in_specs=_row_specs(tr, [D, D]) + [_vec_spec(D)] * 5,
        out_specs=_row_specs(tr, [D, D, D]),
        out_shape=[jax.ShapeDtypeStruct((L, D), F32), jax.ShapeDtypeStruct((L, D), BF16),
                   jax.ShapeDtypeStruct((L, D), BF16)],
        compiler_params=_cp(("parallel",)), name="ln_mid")(x, y, gate, g, b, scale, shift)


def ln_final(x, y, gate, g, b, target):
    L, D = x.shape
    tr = _tile(L, 256, 16)

    def body(x_ref, y_ref, gate_ref, g_ref, b_ref, t_ref, dout_ref, sq_ref):
        out, _, _ = _ln_core(x_ref[...], y_ref[...], gate_ref[...], g_ref[...], b_ref[...])
        err = out - t_ref[...]
        dout_ref[...] = err * (1.0 / D)
        _acc_rows(sq_ref, err * err, pl.program_id(0))

    return pl.pallas_call(
        body, grid=(L // tr,), in_specs=_row_specs(tr, [D, D]) + [_vec_spec(D)] * 3 + _row_specs(tr, [D]),
        out_specs=[_row_specs(tr, [D])[0], _vec_spec(D)],
        out_shape=[jax.ShapeDtypeStruct((L, D), F32), jax.ShapeDtypeStruct((1, D), F32)],
        compiler_params=_cp(("arbitrary",)), name="ln_final")(x, y, gate, g, b, target)


def ln_bwd(dout, x, y, gate, g, name):
    L, D = x.shape
    tr = _tile(L, 256, 16)

    def body(do_ref, x_ref, y_ref, gate_ref, g_ref, dres_ref, dy_ref, dg_ref, db_ref, dgate_ref):
        i = pl.program_id(0)
        yv = y_ref[...]
        dout_v = do_ref[...]
        _, xhat, rstd = _ln_core(x_ref[...], yv, gate_ref[...], g_ref[...], 0.0)
        dxh = dout_v * g_ref[...]
        m1 = jnp.mean(dxh, axis=-1, keepdims=True)
        m2 = jnp.mean(dxh * xhat, axis=-1, keepdims=True)
        du = rstd * (dxh - m1 - xhat * m2)
        dres_ref[...] = ALPHA * du
        dy_ref[...] = ((1.0 + gate_ref[...]) * du).astype(BF16)
        _acc_rows(dg_ref, dout_v * xhat, i)
        _acc_rows(db_ref, dout_v, i)
        _acc_rows(dgate_ref, du * yv, i)

    return pl.pallas_call(
        body, grid=(L // tr,), in_specs=_row_specs(tr, [D, D, D]) + [_vec_spec(D)] * 2,
        out_specs=_row_specs(tr, [D, D]) + [_vec_spec(D)] * 3,
        out_shape=[jax.ShapeDtypeStruct((L, D), F32), jax.ShapeDtypeStruct((L, D), BF16)]
        + [jax.ShapeDtypeStruct((1, D), F32)] * 3,
        compiler_params=_cp(("arbitrary",)), name=name)(dout, x, y, gate, g)


def mod_bwd(dres, dh, dh2, xin, scale, name, through_mod):
    L, D = xin.shape
    tr = _tile(L, 256, 16)

    def body(dres_ref, dh_ref, dh2_ref, x_ref, sc_ref, dx_ref, dsc_ref, dsh_ref):
        i = pl.program_id(0)
        dh_v = dh_ref[...]
        tot = dres_ref[...]
        if through_mod:
            dh_v = dh_v + dh2_ref[...]
        else:
            tot = tot + dh2_ref[...]
        dx_ref[...] = tot + dh_v * (1.0 + sc_ref[...])
        _acc_rows(dsc_ref, dh_v * x_ref[...], i)
        _acc_rows(dsh_ref, dh_v, i)

    return pl.pallas_call(
        body, grid=(L // tr,), in_specs=_row_specs(tr, [D, D, D, D]) + [_vec_spec(D)],
        out_specs=_row_specs(tr, [D]) + [_vec_spec(D)] * 2,
        out_shape=[jax.ShapeDtypeStruct((L, D), F32)] + [jax.ShapeDtypeStruct((1, D), F32)] * 2,
        compiler_params=_cp(("arbitrary",)), name=name)(dres, dh, dh2, xin, scale)


CONV_HALO = 16


def _conv_rows(x_ref, i, tr, L):
    nblk = L // tr
    s = pl.multiple_of(i * tr, CONV_HALO)
    cur = x_ref[pl.ds(s, tr), :].astype(F32)
    sp = pl.multiple_of(jnp.maximum(i * tr - CONV_HALO, 0), CONV_HALO)
    sn = pl.multiple_of(jnp.minimum(i * tr + tr, L - CONV_HALO), CONV_HALO)
    prev = x_ref[pl.ds(sp, CONV_HALO), :].astype(F32) * (i > 0).astype(F32)
    nxt = x_ref[pl.ds(sn, CONV_HALO), :].astype(F32) * (i < nblk - 1).astype(F32)
    return jnp.concatenate([prev, cur, nxt], axis=0)


def _shift_rows(v, j):
    n = v.shape[0]
    return v if j % n == 0 else pltpu.roll(v, j % n, 0)


def _conv_eval(xe, w_ref, b_ref):
    c = b_ref[...] + w_ref[CONV_W - 1:CONV_W, :] * xe
    for k in range(CONV_W - 1):
        c = c + w_ref[k:k + 1, :] * _shift_rows(xe, CONV_W - 1 - k)
    return c


def conv_fwd(zx, col0, conv_w, conv_b):
    L = zx.shape[0]
    C = conv_w.shape[1]
    tc = _tile(C, 512)
    tr = _tile(L, 512, CONV_HALO)
    off = col0 // tc

    def body(x_ref, w_ref, b_ref, o_ref):
        i = pl.program_id(1)
        xe = _conv_rows(x_ref, i, tr, L)
        c = _conv_eval(xe, w_ref, b_ref)[CONV_HALO:CONV_HALO + tr]
        o_ref[...] = _silu(c).astype(BF16)

    return pl.pallas_call(
        body, grid=(C // tc, L // tr),
        in_specs=[pl.BlockSpec((L, tc), lambda j, i: (0, off + j)), pl.BlockSpec((CONV_W, tc), lambda j, i: (0, j)),
                  pl.BlockSpec((1, tc), lambda j, i: (0, j))],
        out_specs=pl.BlockSpec((tr, tc), lambda j, i: (i, j)),
        out_shape=jax.ShapeDtypeStruct((L, C), BF16), compiler_params=_cp(("parallel", "arbitrary")),
        name="conv_fwd")(zx, conv_w, conv_b)


def conv_bwd(zx, col0, conv_w, conv_b, dxbc):
    L = zx.shape[0]
    C = conv_w.shape[1]
    tc = _tile(C, 512)
    tr = _tile(L, 512, CONV_HALO)
    off = col0 // tc
    H = CONV_HALO

    def body(x_ref, g_ref, w_ref, b_ref, dx_ref, dw_ref, db_ref):
        i = pl.program_id(1)
        xe = _conv_rows(x_ref, i, tr, L)
        ge = _conv_rows(g_ref, i, tr, L)
        dc = ge * _dsilu(_conv_eval(xe, w_ref, b_ref))
        dx = w_ref[CONV_W - 1:CONV_W, :] * dc
        for k in range(CONV_W - 1):
            dx = dx + w_ref[k:k + 1, :] * _shift_rows(dc, -(CONV_W - 1 - k))
        dx_ref[...] = dx[H:H + tr].astype(BF16)
        dcc = dc[H:H + tr]
        rows = [jnp.sum(dcc * _shift_rows(xe, CONV_W - 1 - k)[H:H + tr], axis=0, keepdims=True) for k in range(CONV_W)]
        dwv = jnp.concatenate(rows + [jnp.zeros((8 - CONV_W, tc), F32)], axis=0)
        dbv = jnp.sum(dcc, axis=0, keepdims=True)

        @pl.when(i == 0)
        def _():
            dw_ref[...] = dwv
            db_ref[...] = dbv

        @pl.when(i > 0)
        def _():
            dw_ref[...] += dwv
            db_ref[...] += dbv

    dx, dw, db = pl.pallas_call(
        body, grid=(C // tc, L // tr),
        in_specs=[pl.BlockSpec((L, tc), lambda j, i: (0, off + j)), pl.BlockSpec((L, tc), lambda j, i: (0, j)),
                  pl.BlockSpec((CONV_W, tc), lambda j, i: (0, j)), pl.BlockSpec((1, tc), lambda j, i: (0, j))],
        out_specs=[pl.BlockSpec((tr, tc), lambda j, i: (i, j)), pl.BlockSpec((8, tc), lambda j, i: (0, j)),
                   pl.BlockSpec((1, tc), lambda j, i: (0, j))],
        out_shape=[jax.ShapeDtypeStruct((L, C), BF16), jax.ShapeDtypeStruct((8, C), F32),
                   jax.ShapeDtypeStruct((1, C), F32)],
        compiler_params=_cp(("parallel", "arbitrary")), name="conv_bwd")(zx, dxbc, conv_w, conv_b)
    return dx, dw[:CONV_W], db


_NN = (((1,), (0,)), ((), ()))


def _pieces(x, n):
    out, r = [], x
    for _ in range(n):
        p = r.astype(BF16)
        out.append(p)
        r = r - p.astype(F32)
    return out


def _dot01(a, b01, n, dims=_NN):
    b = b01.astype(BF16)
    return functools.reduce(lambda u, v: u + v,
                            [lax.dot_general(p, b, dims, preferred_element_type=F32) for p in _pieces(a, n)])


def _dot01_left(a01, b, n, dims=_NN):
    a = a01.astype(BF16)
    return functools.reduce(lambda u, v: u + v,
                            [lax.dot_general(a, p, dims, preferred_element_type=F32) for p in _pieces(b, n)])


def _ssd_common(dtp_ref, dtpT_ref, bias_ref, biasT_ref, alog_ref, alogT_ref, b_ref, c_ref):
    Q = SSD_Q
    dt = _softplus(dtp_ref[...] + bias_ref[...])
    A = -jnp.exp(alog_ref[...])
    row = lax.broadcasted_iota(jnp.int32, (Q, Q), 0)
    col = lax.broadcasted_iota(jnp.int32, (Q, Q), 1)
    causal = row >= col
    tril = causal.astype(F32)
    Kh = dt.shape[1]
    acum = _dot01_left(tril, dt * A, 3)
    eye = (lax.broadcasted_iota(jnp.int32, (Kh, Kh), 0) == lax.broadcasted_iota(jnp.int32, (Kh, Kh), 1)).astype(F32)
    acumT = _dot01_left(eye, acum, 3, dims=(((1,), (1,)), ((), ())))
    Bm = b_ref[...]
    Cm = c_ref[...]
    cb = lax.dot_general(Cm, Bm, (((1,), (1,)), ((), ())), preferred_element_type=F32)
    return dt, A, causal, row, col, acum, acumT, Bm, Cm, cb


def _ssd_in_specs(Q, GP, N, Kh, DI, cmap):
    nb0 = DI // N
    vec = pl.BlockSpec((None, 1, Kh), lambda g, c: (g, 0, 0))
    vecT = pl.BlockSpec((None, Kh, 1), lambda g, c: (g, 0, 0))
    return [pl.BlockSpec((Q, GP), lambda g, c: (cmap(c), g)),
            pl.BlockSpec((Q, N), lambda g, c: (cmap(c), nb0 + g)),
            pl.BlockSpec((Q, N), lambda g, c: (cmap(c), nb0 + SSD_G + g)),
            pl.BlockSpec((None, Q, Kh), lambda g, c: (g, cmap(c), 0)),
            pl.BlockSpec((None, Kh, Q), lambda g, c: (g, 0, cmap(c))),
            vec, vecT, vec, vecT, vec, vecT]


def _hi(a, b01):
    return _dot01(a, b01, 2)


def _ssd_heads(dskT_ref, acum, acumT, dt, Kh):
    Q, P, N = SSD_Q, SSD_P, SSD_N
    GP = Kh * P
    sh_p = P.bit_length() - 1
    seg = lambda shape, dim: lax.shift_right_logical(lax.broadcasted_iota(jnp.int32, shape, dim), sh_p)
    E = (seg((Kh, GP), 1) == lax.broadcasted_iota(jnp.int32, (Kh, GP), 0)).astype(F32)
    ET = (seg((GP, Kh), 0) == lax.broadcasted_iota(jnp.int32, (GP, Kh), 1)).astype(F32)
    a_last = acum[Q - 1:Q, :]
    tail = jnp.exp(a_last - acum)
    eLT = jnp.exp(acumT[:, Q - 1:Q])
    rowseg = seg((GP, N), 0)
    eL_b = jnp.zeros((GP, N), F32)
    for k in range(Kh):
        eL_b = jnp.where(rowseg == k, eLT[k:k + 1, :], eL_b)
    return dict(
        E=E, ET=ET, a_last=a_last, tail=tail, eL_b=eL_b,
        dt_all=_hi(dt, E), ea_all=_hi(jnp.exp(acum), E), tail_all=_hi(tail, E),
        dsk_all=jnp.sum(E * dskT_ref[...], axis=0, keepdims=True))


def _head_chunks(GP):
    CW = min(GP, 128)
    return CW, CW // SSD_P, GP // CW


def _head_mask(Q, CW, kk):
    lane = lax.broadcasted_iota(jnp.int32, (Q, CW), 1)
    return jnp.logical_and(lane >= kk * SSD_P, lane < (kk + 1) * SSD_P)


def ssd_fwd(xbc, dtp_g, dtp_gT, bias_g, bias_gT, alog_g, alog_gT, dsk_g, dsk_gT, DI):
    L = xbc.shape[0]
    Q, P, N, G = SSD_Q, SSD_P, SSD_N, SSD_G
    GP = DI // G
    Kh = GP // P
    nc = L // Q

    CW, hpc, nch = _head_chunks(GP)
    nt = (((1,), (1,)), ((), ()))
    tn = (((0,), (0,)), ((), ()))

    def body(xs_ref, b_ref, c_ref, dtp_ref, dtpT_ref, bias_ref, biasT_ref, alog_ref, alogT_ref, dsk_ref, dskT_ref,
             y_ref, st_ref, state):
        @pl.when(pl.program_id(1) == 0)
        def _():
            state[...] = jnp.zeros(state.shape, F32)

        st_ref[...] = state[...]
        dt, A, causal, row, col, acum, acumT, Bm, Cm, cb = _ssd_common(
            dtp_ref, dtpT_ref, bias_ref, biasT_ref, alog_ref, alogT_ref, b_ref, c_ref)
        hd = _ssd_heads(dskT_ref, acum, acumT, dt, Kh)
        xs = xs_ref[...].astype(F32)
        xdt_all = xs * hd["dt_all"]
        S_all = state[...]
        y_all = (lax.dot_general(Cm, S_all.astype(BF16), nt, preferred_element_type=F32) * hd["ea_all"]
                 + xs * hd["dsk_all"])
        state[...] = S_all * hd["eL_b"] + lax.dot_general(
            (xdt_all * hd["tail_all"]).astype(BF16), Bm, tn, preferred_element_type=F32)
        for ch in range(nch):
            cs = slice(ch * CW, (ch + 1) * CW)
            xc = xdt_all[:, cs]
            acc = y_all[:, cs]
            for kk in range(hpc):
                k = ch * hpc + kk
                decay = jnp.exp(jnp.where(causal, acum[:, k:k + 1] - acumT[k:k + 1, :], -jnp.inf))
                xk = xc if hpc == 1 else jnp.where(_head_mask(Q, CW, kk), xc, 0.0)
                acc = acc + jnp.dot((cb * decay).astype(BF16), xk.astype(BF16), preferred_element_type=F32)
            y_ref[:, cs] = acc.astype(BF16)

    return pl.pallas_call(
        body, grid=(G, nc), in_specs=_ssd_in_specs(Q, GP, N, Kh, DI, lambda c: c),
        out_specs=[pl.BlockSpec((Q, GP), lambda g, c: (c, g)),
                   pl.BlockSpec((None, None, GP, N), lambda g, c: (c, g, 0, 0))],
        out_shape=[jax.ShapeDtypeStruct((L, DI), BF16), jax.ShapeDtypeStruct((nc, G, GP, N), F32)],
        scratch_shapes=[pltpu.VMEM((GP, N), F32)], compiler_params=_cp(("parallel", "arbitrary")),
        name="ssd_fwd")(xbc, xbc, xbc, dtp_g, dtp_gT, bias_g, bias_gT, alog_g, alog_gT, dsk_g, dsk_gT)


def ssd_bwd(xbc, dtp_g, dtp_gT, bias_g, bias_gT, alog_g, alog_gT, dsk_g, dsk_gT, states, dy, DI):
    L = xbc.shape[0]
    Q, P, N, G = SSD_Q, SSD_P, SSD_N, SSD_G
    GP = DI // G
    Kh = GP // P
    nc = L // Q
    rev = lambda c: nc - 1 - c

    CW, hpc, nch = _head_chunks(GP)

    def body(xs_ref, b_ref, c_ref, dtp_ref, dtpT_ref, bias_ref, biasT_ref, alog_ref, alogT_ref, dsk_ref, dskT_ref,
             st_ref, dy_ref, dxs_ref, dB_ref, dC_ref, ddtp_ref, dbias_ref, dalog_ref, dD_ref, dstate):
        ci = pl.program_id(1)

        @pl.when(ci == 0)
        def _():
            dstate[...] = jnp.zeros(dstate.shape, F32)

        dt, A, causal, row, col, acum, acumT, Bm, Cm, cb = _ssd_common(
            dtp_ref, dtpT_ref, bias_ref, biasT_ref, alog_ref, alogT_ref, b_ref, c_ref)
        tn = (((0,), (0,)), ((), ()))
        nt = (((1,), (1,)), ((), ()))
        hd = _ssd_heads(dskT_ref, acum, acumT, dt, Kh)
        ET, tail = hd["ET"], hd["tail"]
        cbT = lax.dot_general(Bm, Cm, nt, preferred_element_type=F32)
        causalT = row <= col
        xs = xs_ref[...].astype(F32)
        xdt_all = xs * hd["dt_all"]
        dyb = dy_ref[...]
        dy_all = dyb.astype(F32)
        S_all = st_ref[...]
        S_b = S_all.astype(BF16)
        dS_all = dstate[...]
        dS_b = dS_all.astype(BF16)
        CS_all = lax.dot_general(Cm, S_b, nt, preferred_element_type=F32)
        dyE_b = (dy_all * hd["ea_all"]).astype(BF16)
        dC_acc = jnp.dot(dyE_b, S_b, preferred_element_type=F32)
        dS_y = lax.dot_general(dyE_b, Cm, tn, preferred_element_type=F32)
        BdS_all = lax.dot_general(Bm, dS_b, nt, preferred_element_type=F32)
        dB_acc = jnp.dot((xdt_all * hd["tail_all"]).astype(BF16), dS_b, preferred_element_type=F32)
        dtail = _hi(xdt_all * BdS_all, ET)
        da_cols = _hi(dy_all * CS_all * hd["ea_all"], ET) - dtail * tail
        dss = _dot01_left(jnp.ones((8, N), F32), _dot01_left(hd["E"], dS_all * S_all, 2), 2, dims=nt)
        da_last = dss[0:1] * jnp.exp(hd["a_last"]) + jnp.sum(dtail * tail, axis=0, keepdims=True)
        rowi = lax.broadcasted_iota(jnp.int32, (Q, Kh), 0)
        da_cols = da_cols + jnp.where(rowi == Q - 1, da_last, 0.0)
        dstate[...] = hd["eL_b"] * dS_all + dS_y
        sum_mg = jnp.zeros((Q, Q), F32)
        sum_mgt = jnp.zeros((Q, Q), F32)
        dacc = jnp.zeros((Q, 128), F32)
        ddt_x = jnp.zeros((Q, Kh), F32)
        lane128 = lax.broadcasted_iota(jnp.int32, (Q, 128), 1)
        for ch in range(nch):
            cs = slice(ch * CW, (ch + 1) * CW)
            dyc = dyb[:, cs]
            xc_b = xdt_all[:, cs].astype(BF16)
            acc = hd["tail_all"][:, cs] * BdS_all[:, cs]
            for kk in range(hpc):
                k = ch * hpc + kk
                a_b = jnp.broadcast_to(acum[:, k:k + 1], (Q, Q))
                a_r = acumT[k:k + 1, :]
                decay = jnp.exp(jnp.where(causal, a_b - a_r, -jnp.inf))
                decayT = jnp.exp(jnp.where(causalT, a_r - a_b, -jnp.inf))
                dyk = dyc if hpc == 1 else jnp.where(_head_mask(Q, CW, kk), dyc, jnp.zeros_like(dyc))
                mg = decay * lax.dot_general(dyk, xc_b, nt, preferred_element_type=F32)
                mgt = decayT * lax.dot_general(xc_b, dyk, nt, preferred_element_type=F32)
                sum_mg = sum_mg + mg
                sum_mgt = sum_mgt + mgt
                onek = jnp.where(lane128 == k, 1.0, 0.0).astype(BF16)
                dk = mg * cb - mgt * cbT
                dk_hi = dk.astype(BF16)
                dk_lo = (dk - dk_hi.astype(F32)).astype(BF16)
                dacc = dacc + (jnp.dot(dk_hi, onek, preferred_element_type=F32)
                               + jnp.dot(dk_lo, onek, preferred_element_type=F32))
                acc = acc + jnp.dot((decayT * cbT).astype(BF16), dyk, preferred_element_type=F32)
            dxs_ref[:, cs] = (acc * hd["dt_all"][:, cs] + dy_all[:, cs] * hd["dsk_all"][:, cs]).astype(BF16)
            ddt_x = ddt_x + _hi(acc * xs[:, cs], ET[cs, :])
        da_cols = da_cols + dacc[:, :Kh]
        dD_row = jnp.sum(_hi(dy_all * xs, ET), axis=0, keepdims=True)
        dB_ref[...] = (dB_acc + jnp.dot(sum_mgt.astype(BF16), Cm, preferred_element_type=F32)).astype(BF16)
        dC_ref[...] = (dC_acc + jnp.dot(sum_mg.astype(BF16), Bm, preferred_element_type=F32)).astype(BF16)
        triu = (row <= col).astype(F32)
        ddtA = _dot01_left(triu, da_cols, 3)
        ddt = ddt_x + ddtA * A
        dpre = ddt * _sigmoid(dtp_ref[...] + bias_ref[...])
        ddtp_ref[...] = dpre
        dbias_v = jnp.sum(dpre, axis=0, keepdims=True)
        dalog_v = jnp.sum(ddtA * dt, axis=0, keepdims=True) * A

        @pl.when(ci == 0)
        def _():
            dbias_ref[...] = dbias_v
            dalog_ref[...] = dalog_v
            dD_ref[...] = dD_row

        @pl.when(ci > 0)
        def _():
            dbias_ref[...] += dbias_v
            dalog_ref[...] += dalog_v
            dD_ref[...] += dD_row

    vec_o = pl.BlockSpec((None, 1, Kh), lambda g, c: (g, 0, 0))
    return pl.pallas_call(
        body, grid=(G, nc),
        in_specs=_ssd_in_specs(Q, GP, N, Kh, DI, rev)
        + [pl.BlockSpec((None, None, GP, N), lambda g, c: (rev(c), g, 0, 0)),
           pl.BlockSpec((Q, GP), lambda g, c: (rev(c), g))],
        out_specs=[pl.BlockSpec((Q, GP), lambda g, c: (rev(c), g)), pl.BlockSpec((Q, N), lambda g, c: (rev(c), g)),
                   pl.BlockSpec((Q, N), lambda g, c: (rev(c), g)),
                   pl.BlockSpec((None, Q, Kh), lambda g, c: (g, rev(c), 0)), vec_o, vec_o, vec_o],
        out_shape=[jax.ShapeDtypeStruct((L, DI), BF16), jax.ShapeDtypeStruct((L, G * N), BF16),
                   jax.ShapeDtypeStruct((L, G * N), BF16), jax.ShapeDtypeStruct((G, L, Kh), F32)]
        + [jax.ShapeDtypeStruct((G, 1, Kh), F32)] * 3,
        scratch_shapes=[pltpu.VMEM((GP, N), F32)], compiler_params=_cp(("parallel", "arbitrary")),
        name="ssd_bwd")(xbc, xbc, xbc, dtp_g, dtp_gT, bias_g, bias_gT, alog_g, alog_gT, dsk_g, dsk_gT, states, dy)


def _rms_groups(y2, ng_ref, DI):
    S = DI // SSD_G
    for g in range(SSD_G):
        gs = slice(g * S, (g + 1) * S)
        seg = y2[:, gs]
        r = lax.rsqrt(jnp.mean(seg * seg, axis=-1, keepdims=True) + RMS_EPS)
        yield gs, seg * r, r, ng_ref[:, gs]


def rms_gate_fwd(y, zx, norm_g):
    L, DI = y.shape
    tr = _tile(L, 256, 16)

    def body(y_ref, z_ref, ng_ref, o_ref):
        y2 = y_ref[...].astype(F32) * _silu(z_ref[...].astype(F32))
        for gs, yh, _, ng in _rms_groups(y2, ng_ref, DI):
            o_ref[:, gs] = (yh * ng).astype(BF16)

    return pl.pallas_call(
        body, grid=(L // tr,), in_specs=_row_specs(tr, [DI, DI]) + [_vec_spec(DI)], out_specs=_row_specs(tr, [DI])[0],
        out_shape=jax.ShapeDtypeStruct((L, DI), BF16), compiler_params=_cp(("parallel",)),
        name="rms_gate_fwd")(y, zx, norm_g)


def rms_gate_bwd(dyn, y, zx, norm_g):
    L, DI = y.shape
    tr = _tile(L, 256, 16)

    def body(dyn_ref, y_ref, z_ref, ng_ref, dy_ref, dz_ref, dng_ref):
        i = pl.program_id(0)
        yv = y_ref[...].astype(F32)
        zv = z_ref[...].astype(F32)
        sz = _silu(zv)
        dsz = _dsilu(zv)
        dynv = dyn_ref[...].astype(F32)
        for gs, yh, r, ng in _rms_groups(yv * sz, ng_ref, DI):
            dyh = dynv[:, gs] * ng
            dy2 = r * (dyh - yh * jnp.mean(dyh * yh, axis=-1, keepdims=True))
            dy_ref[:, gs] = (dy2 * sz[:, gs]).astype(BF16)
            dz_ref[:, gs] = (dy2 * yv[:, gs] * dsz[:, gs]).astype(BF16)
            s = jnp.sum(dynv[:, gs] * yh, axis=0, keepdims=True)

            @pl.when(i == 0)
            def _():
                dng_ref[:, gs] = s

            @pl.when(i > 0)
            def _():
                dng_ref[:, gs] += s

    return pl.pallas_call(
        body, grid=(L // tr,), in_specs=_row_specs(tr, [DI, DI, DI]) + [_vec_spec(DI)],
        out_specs=_row_specs(tr, [DI, DI]) + [_vec_spec(DI)],
        out_shape=[jax.ShapeDtypeStruct((L, DI), BF16)] * 2 + [jax.ShapeDtypeStruct((1, DI), F32)],
        compiler_params=_cp(("arbitrary",)), name="rms_gate_bwd")(dyn, y, zx, norm_g)


def _alibi_slope(gi, h):
    n = len(DIL_PATTERNS) * DIL_H
    return float(2.0 ** (-8.0 * (gi * DIL_H + h + 1) / n))


def _attn_masks():
    qi = lax.broadcasted_iota(jnp.int32, (DIL_BLK, DIL_BLK), 0)
    kj = lax.broadcasted_iota(jnp.int32, (DIL_BLK, DIL_BLK), 1)
    dcur = (qi - kj).astype(F32)
    return dcur, qi >= kj, dcur + float(DIL_BLK), kj >= qi


def _dil_cols(arr, col0, d):
    HW = DIL_H * DIL_E
    if d == 1:
        return arr, arr.shape[1] // HW, col0 // HW
    return arr[:, col0:col0 + HW].reshape(arr.shape[0] // d, d * HW), 1, 0


def attn_fwd(qz, kv, gi):
    window, d = DIL_PATTERNS[gi]
    assert window // d == DIL_BLK
    L, QZ = qz.shape
    KV = kv.shape[1]
    HW = DIL_H * DIL_E
    M = L // d
    nb = M // DIL_BLK
    nq, nkv = QZ // HW, KV // HW
    scale = DIL_E ** -0.5
    nt = (((1,), (1,)), ((), ()))

    def body(q_ref, kp_ref, kc_ref, vp_ref, vc_ref, o_ref, lse_ref):
        n = pl.program_id(1)
        dcur, vcur, dprev, vprev0 = _attn_masks()
        vprev = jnp.logical_and(vprev0, n > 0)
        lane = lax.broadcasted_iota(jnp.int32, (DIL_BLK, 128), 1)
        lse_acc = jnp.zeros((DIL_BLK, 128), F32)
        for h in range(DIL_H):
            hs = slice(h * DIL_E, (h + 1) * DIL_E)
            sl = _alibi_slope(gi, h) * d
            q = q_ref[:, hs]
            s_c = lax.dot_general(q, kc_ref[:, hs], nt, preferred_element_type=F32) * scale - sl * dcur
            s_p = lax.dot_general(q, kp_ref[:, hs], nt, preferred_element_type=F32) * scale - sl * dprev
            s_c = jnp.where(vcur, s_c, -jnp.inf)
            s_p = jnp.where(vprev, s_p, -jnp.inf)
            m = jnp.maximum(jnp.max(s_c, axis=-1, keepdims=True), jnp.max(s_p, axis=-1, keepdims=True))
            p_c = jnp.exp(s_c - m)
            p_p = jnp.exp(s_p - m)
            den = jnp.sum(p_c, axis=-1, keepdims=True) + jnp.sum(p_p, axis=-1, keepdims=True)
            o = (jnp.dot(p_c.astype(BF16), vc_ref[:, hs], preferred_element_type=F32)
                 + jnp.dot(p_p.astype(BF16), vp_ref[:, hs], preferred_element_type=F32)) / den
            o_ref[:, hs] = o.astype(BF16)
            lse_acc = jnp.where(lane == h, m + jnp.log(den), lse_acc)
        lse_ref[...] = lse_acc

    blk = (DIL_BLK, HW)
    prev = lambda n: jnp.maximum(n - 1, 0)
    qv, qn, qo = _dil_cols(qz, gi * HW, d)
    kv_, kn, ko = _dil_cols(kv, gi * HW, d)
    vv, vn, vo = _dil_cols(kv, (nkv // 2 + gi) * HW, d)
    o, lse = pl.pallas_call(
        body, grid=(d, nb),
        in_specs=[pl.BlockSpec(blk, lambda r, n: (n, r * qn + qo)),
                  pl.BlockSpec(blk, lambda r, n: (prev(n), r * kn + ko)),
                  pl.BlockSpec(blk, lambda r, n: (n, r * kn + ko)),
                  pl.BlockSpec(blk, lambda r, n: (prev(n), r * vn + vo)),
                  pl.BlockSpec(blk, lambda r, n: (n, r * vn + vo))],
        out_specs=[pl.BlockSpec(blk, lambda r, n: (n, r)), pl.BlockSpec((DIL_BLK, 128), lambda r, n: (n, r))],
        out_shape=[jax.ShapeDtypeStruct((M, d * HW), BF16), jax.ShapeDtypeStruct((M, d * 128), F32)],
        compiler_params=_cp(("parallel", "parallel")), name=f"attn_fwd_{gi}")(qv, kv_, kv_, vv, vv)
    return o.reshape(L, HW), lse.reshape(L, 128)


def attn_bwd(qz, kv, do, lse, dpr, gi):
    window, d = DIL_PATTERNS[gi]
    L, QZ = qz.shape
    KV = kv.shape[1]
    HW = DIL_H * DIL_E
    M = L // d
    nb = M // DIL_BLK
    nq, nkv = QZ // HW, KV // HW
    scale = DIL_E ** -0.5
    nt = (((1,), (1,)), ((), ()))
    tn = (((0,), (0,)), ((), ()))

    def body(q0_ref, q1_ref, k_ref, v_ref, do0_ref, do1_ref, l0_ref, l1_ref, r0_ref, r1_ref,
             dq_ref, dk_ref, dv_ref, carry):
        n = pl.program_id(1)

        @pl.when(n == 0)
        def _():
            carry[...] = jnp.zeros(carry.shape, F32)

        dcur, vcur, dprev, vprev0 = _attn_masks()
        vprev = jnp.logical_and(vprev0, n < nb - 1)
        for h in range(DIL_H):
            hs = slice(h * DIL_E, (h + 1) * DIL_E)
            sl = _alibi_slope(gi, h) * d
            kh = k_ref[:, hs]
            vh = v_ref[:, hs]
            q0, q1 = q0_ref[:, hs], q1_ref[:, hs]
            do0, do1 = do0_ref[:, hs], do1_ref[:, hs]
            s0 = lax.dot_general(q0, kh, nt, preferred_element_type=F32) * scale - sl * dcur
            p0 = jnp.exp(jnp.where(vcur, s0 - l0_ref[:, h:h + 1], -jnp.inf))
            ds0 = p0 * (lax.dot_general(do0, vh, nt, preferred_element_type=F32) - r0_ref[:, h:h + 1])
            s1 = lax.dot_general(q1, kh, nt, preferred_element_type=F32) * scale - sl * dprev
            p1 = jnp.exp(jnp.where(vprev, s1 - l1_ref[:, h:h + 1], -jnp.inf))
            ds1 = p1 * (lax.dot_general(do1, vh, nt, preferred_element_type=F32) - r1_ref[:, h:h + 1])
            ds0_b = (ds0 * scale).astype(BF16)
            ds1_b = (ds1 * scale).astype(BF16)
            dv = (lax.dot_general(p0.astype(BF16), do0, tn, preferred_element_type=F32)
                  + lax.dot_general(p1.astype(BF16), do1, tn, preferred_element_type=F32))
            dk = (lax.dot_general(ds0_b, q0, tn, preferred_element_type=F32)
                  + lax.dot_general(ds1_b, q1, tn, preferred_element_type=F32))
            dv_ref[:, hs] = dv.astype(BF16)
            dk_ref[:, hs] = dk.astype(BF16)
            dq_ref[:, hs] = (carry[:, hs] + jnp.dot(ds0_b, kh, preferred_element_type=F32)).astype(BF16)
            carry[:, hs] = jnp.dot(ds1_b, kh, preferred_element_type=F32)

    blk = (DIL_BLK, HW)
    sblk = (DIL_BLK, 128)
    nxt = lambda n: jnp.minimum(n + 1, nb - 1)
    qv, qn, qo = _dil_cols(qz, gi * HW, d)
    kv_, kn, ko = _dil_cols(kv, gi * HW, d)
    vv, vn, vo = _dil_cols(kv, (nkv // 2 + gi) * HW, d)
    dov = do.reshape(M, d * HW)
    lv = lse.reshape(M, d * 128)
    rv = dpr.reshape(M, d * 128)
    outs = pl.pallas_call(
        body, grid=(d, nb),
        in_specs=[pl.BlockSpec(blk, lambda r, n: (n, r * qn + qo)), pl.BlockSpec(blk, lambda r, n: (nxt(n), r * qn + qo)),
                  pl.BlockSpec(blk, lambda r, n: (n, r * kn + ko)),
                  pl.BlockSpec(blk, lambda r, n: (n, r * vn + vo)),
                  pl.BlockSpec(blk, lambda r, n: (n, r)), pl.BlockSpec(blk, lambda r, n: (nxt(n), r)),
                  pl.BlockSpec(sblk, lambda r, n: (n, r)), pl.BlockSpec(sblk, lambda r, n: (nxt(n), r)),
                  pl.BlockSpec(sblk, lambda r, n: (n, r)), pl.BlockSpec(sblk, lambda r, n: (nxt(n), r))],
        out_specs=[pl.BlockSpec(blk, lambda r, n: (n, r))] * 3,
        out_shape=[jax.ShapeDtypeStruct((M, d * HW), BF16)] * 3,
        scratch_shapes=[pltpu.VMEM(blk, F32)], compiler_params=_cp(("parallel", "arbitrary")),
        name=f"attn_bwd_{gi}")(qv, qv, kv_, vv, dov, dov, lv, lv, rv, rv)
    return [t.reshape(L, HW) for t in outs]


def _merge_weights(l_refs, h):
    ls = [r[:, h:h + 1] for r in l_refs]
    mx = functools.reduce(jnp.maximum, ls)
    es = [jnp.exp(l - mx) for l in ls]
    den = functools.reduce(lambda a, b: a + b, es)
    return [e / den for e in es]


def merge_fwd(os_, lses, qz):
    L, HW = os_[0].shape
    tr = _tile(L, 256, 16)
    ng = len(os_)
    zblk = qz.shape[1] // HW - 1

    def body(*refs):
        o_refs, l_refs, z_ref, out_ref = refs[:ng], refs[ng:2 * ng], refs[2 * ng], refs[2 * ng + 1]
        for h in range(DIL_H):
            hs = slice(h * DIL_E, (h + 1) * DIL_E)
            ws = _merge_weights(l_refs, h)
            om = functools.reduce(lambda a, b: a + b, [w * o[:, hs].astype(F32) for w, o in zip(ws, o_refs)])
            out_ref[:, hs] = (om * _silu(z_ref[:, hs].astype(F32))).astype(BF16)

    return pl.pallas_call(
        body, grid=(L // tr,),
        in_specs=_row_specs(tr, [HW] * ng + [128] * ng) + [pl.BlockSpec((tr, HW), lambda i: (i, zblk))],
        out_specs=_row_specs(tr, [HW])[0], out_shape=jax.ShapeDtypeStruct((L, HW), BF16),
        compiler_params=_cp(("parallel",)), name="merge_fwd")(*os_, *lses, qz)


def merge_bwd(dgated, os_, lses, qz):
    L, HW = os_[0].shape
    tr = _tile(L, 256, 16)
    ng = len(os_)
    zblk = qz.shape[1] // HW - 1

    def body(*refs):
        dg_ref = refs[0]
        o_refs, l_refs, z_ref = refs[1:1 + ng], refs[1 + ng:1 + 2 * ng], refs[1 + 2 * ng]
        outs = refs[2 + 2 * ng:]
        do_refs, dpr_refs, dz_ref = outs[:ng], outs[ng:2 * ng], outs[2 * ng]
        lane = lax.broadcasted_iota(jnp.int32, (tr, 128), 1)
        accs = [jnp.zeros((tr, 128), F32) for _ in range(ng)]
        for h in range(DIL_H):
            hs = slice(h * DIL_E, (h + 1) * DIL_E)
            ws = _merge_weights(l_refs, h)
            ov = [o[:, hs].astype(F32) for o in o_refs]
            om = functools.reduce(lambda a, b: a + b, [w * o for w, o in zip(ws, ov)])
            zv = z_ref[:, hs].astype(F32)
            dgv = dg_ref[:, hs].astype(F32)
            dom = dgv * _silu(zv)
            dz_ref[:, hs] = (dgv * om * _dsilu(zv)).astype(BF16)
            dws = [jnp.sum(dom * o, axis=-1, keepdims=True) for o in ov]
            dwbar = functools.reduce(lambda a, b: a + b, [w * dw for w, dw in zip(ws, dws)])
            for g in range(ng):
                do_refs[g][:, hs] = (ws[g] * dom).astype(BF16)
                accs[g] = jnp.where(lane == h, ws[g] * dwbar, accs[g])
        for g in range(ng):
            dpr_refs[g][...] = accs[g]

    outs = pl.pallas_call(
        body, grid=(L // tr,),
        in_specs=_row_specs(tr, [HW] * (1 + ng) + [128] * ng) + [pl.BlockSpec((tr, HW), lambda i: (i, zblk))],
        out_specs=_row_specs(tr, [HW] * ng + [128] * ng + [HW]),
        out_shape=[jax.ShapeDtypeStruct((L, HW), BF16)] * ng + [jax.ShapeDtypeStruct((L, 128), F32)] * ng
        + [jax.ShapeDtypeStruct((L, HW), BF16)],
        compiler_params=_cp(("parallel",)), name="merge_bwd")(dgated, *os_, *lses, qz)
    return outs[:ng], outs[ng:2 * ng], outs[2 * ng]


def ada_fwd(c8, ada_w):
    nl, D, Ws = ada_w.shape
    tn = _tile(Ws, 512)

    def body(c_ref, w_ref, o_ref):
        o_ref[...] = jnp.dot(_silu(c_ref[...]), w_ref[...], precision=lax.Precision.HIGHEST,
                             preferred_element_type=F32)

    return pl.pallas_call(
        body, grid=(nl, Ws // tn),
        in_specs=[pl.BlockSpec((N_DEV, D), lambda l, j: (0, 0)), pl.BlockSpec((None, D, tn), lambda l, j: (l, 0, j))],
        out_specs=pl.BlockSpec((None, N_DEV, tn), lambda l, j: (l, 0, j)),
        out_shape=jax.ShapeDtypeStruct((nl, N_DEV, Ws), F32), compiler_params=_cp(("parallel", "parallel")),
        name="ada_fwd")(c8, ada_w)


def ada_wgrad(c8t, dmod):
    nl, _, Ws = dmod.shape
    D = c8t.shape[0]
    tm = _tile(D, 512, 8)

    def body(c_ref, d_ref, o_ref):
        sc = _silu(c_ref[...])
        acc = sc[:, 0:1] * d_ref[0:1, :]
        for e in range(1, N_DEV):
            acc = acc + sc[:, e:e + 1] * d_ref[e:e + 1, :]
        o_ref[...] = acc

    return pl.pallas_call(
        body, grid=(nl, D // tm),
        in_specs=[pl.BlockSpec((tm, N_DEV), lambda l, i: (i, 0)), pl.BlockSpec((None, N_DEV, Ws), lambda l, i: (l, 0, 0))],
        out_specs=pl.BlockSpec((None, tm, Ws), lambda l, i: (l, i, 0)),
        out_shape=jax.ShapeDtypeStruct((nl, D, Ws), F32), compiler_params=_cp(("parallel", "parallel")),
        name="ada_wgrad")(c8t, dmod)


def adamw(w, g, m, v, name):
    R, C = w.shape
    tr = _tile(R, 256, 8)
    c1 = 1.0 - ADAM_B1 ** ADAM_STEP
    c2 = 1.0 - ADAM_B2 ** ADAM_STEP

    def body(w_ref, g_ref, m_ref, v_ref, d_ref, nm_ref, nv_ref):
        gv = g_ref[...]
        nm = ADAM_B1 * m_ref[...] + (1.0 - ADAM_B1) * gv
        nv = ADAM_B2 * v_ref[...] + (1.0 - ADAM_B2) * (gv * gv)
        nm_ref[...] = nm
        nv_ref[...] = nv
        d_ref[...] = -ADAM_LR * ((nm / c1) / (jnp.sqrt(nv / c2) + ADAM_EPS) + ADAM_WD * w_ref[...])

    return pl.pallas_call(
        body, grid=(R // tr,), in_specs=_row_specs(tr, [C] * 4), out_specs=_row_specs(tr, [C] * 3),
        out_shape=[jax.ShapeDtypeStruct((R, C), F32)] * 3, compiler_params=_cp(("parallel",)), name=name)(w, g, m, v)


def sum_leading(t, name, out_dtype=F32):
    S, R, C = t.shape
    tr = _tile(R, 256, 16)

    def body(t_ref, o_ref):
        acc = t_ref[0].astype(F32)
        for s in range(1, S):
            acc = acc + t_ref[s].astype(F32)
        o_ref[...] = acc.astype(out_dtype)

    return pl.pallas_call(
        body, grid=(R // tr,), in_specs=[pl.BlockSpec((S, tr, C), lambda i: (0, i, 0))],
        out_specs=pl.BlockSpec((tr, C), lambda i: (i, 0)), out_shape=jax.ShapeDtypeStruct((R, C), out_dtype),
        compiler_params=_cp(("parallel",)), name=name)(t)


def add_half(g, a, core, name):
    S, R, C = g.shape
    h = R // 2
    tr = _tile(h, 256, 16)
    nb = h // tr

    def body(core_ref, g_ref, a_ref, o_ref):
        o_ref[...] = (g_ref[...].astype(F32) + a_ref[...].astype(F32)).astype(BF16)

    return pl.pallas_call(
        body,
        grid_spec=pltpu.PrefetchScalarGridSpec(
            num_scalar_prefetch=1, grid=(S, nb),
            in_specs=[pl.BlockSpec((None, tr, C), lambda s, i, core_ref: (s, core_ref[0] * nb + i, 0)),
                      pl.BlockSpec((None, tr, C), lambda s, i, core_ref: (s, i, 0))],
            out_specs=pl.BlockSpec((None, tr, C), lambda s, i, core_ref: (s, i, 0))),
        out_shape=jax.ShapeDtypeStruct((S, h, C), BF16), compiler_params=_cp(("parallel", "parallel")),
        name=name)(core, g, a)


def sum_partials(own, landed, chip, name):
    _, h, C = own.shape
    tr = _tile(h, 256, 16)

    def body(chip_ref, own_ref, l_ref, o_ref):
        acc = own_ref[...].astype(F32)
        for j in range(3):
            acc = acc + l_ref[j].astype(F32)
        o_ref[...] = acc

    return pl.pallas_call(
        body,
        grid_spec=pltpu.PrefetchScalarGridSpec(
            num_scalar_prefetch=1, grid=(h // tr,),
            in_specs=[pl.BlockSpec((None, tr, C), lambda i, chip_ref: (chip_ref[0], i, 0)),
                      pl.BlockSpec((3, tr, C), lambda i, chip_ref: (0, i, 0))],
            out_specs=pl.BlockSpec((tr, C), lambda i, chip_ref: (i, 0))),
        out_shape=jax.ShapeDtypeStruct((h, C), F32), compiler_params=_cp(("parallel",)), name=name)(chip, own, landed)


def adamw_halves(w, g_mine, g_theirs, m, v, core, name):
    R, C = w.shape
    h = R // 2
    tr = _tile(h, 256, 8)
    nbh = h // tr
    c1 = 1.0 - ADAM_B1 ** ADAM_STEP
    c2 = 1.0 - ADAM_B2 ** ADAM_STEP

    def body(core_ref, w_ref, gm_ref, gt_ref, m_ref, v_ref, g_ref, d_ref, nm_ref, nv_ref):
        mine = (pl.program_id(0) // nbh) == core_ref[0]
        gv = jnp.where(mine, gm_ref[...], gt_ref[...])
        g_ref[...] = gv
        nm = ADAM_B1 * m_ref[...] + (1.0 - ADAM_B1) * gv
        nv = ADAM_B2 * v_ref[...] + (1.0 - ADAM_B2) * (gv * gv)
        nm_ref[...] = nm
        nv_ref[...] = nv
        d_ref[...] = -ADAM_LR * ((nm / c1) / (jnp.sqrt(nv / c2) + ADAM_EPS) + ADAM_WD * w_ref[...])

    full = pl.BlockSpec((tr, C), lambda i, core_ref: (i, 0))
    halfspec = pl.BlockSpec((tr, C), lambda i, core_ref: (i % nbh, 0))
    return pl.pallas_call(
        body,
        grid_spec=pltpu.PrefetchScalarGridSpec(
            num_scalar_prefetch=1, grid=(2 * nbh,), in_specs=[full, halfspec, halfspec, full, full],
            out_specs=[full] * 4),
        out_shape=[jax.ShapeDtypeStruct((R, C), F32)] * 4, compiler_params=_cp(("parallel",)),
        name=name)(core, w, g_mine, g_theirs, m, v)


_ANY = pl.BlockSpec(memory_space=pl.ANY)


def _place():
    x, y, c = lax.axis_index("x"), lax.axis_index("y"), lax.axis_index("c")
    chips = [(1 - x, y), (x, 1 - y), (1 - x, 1 - y)]
    return x, y, c, chips


def allgather_small(v, name, after=None):
    R, W = v.shape
    extra = [] if after is None else [after]

    def body(x_ref, *rest):
        out_ref, send_sems, recv_sems, local_sem = rest[len(extra):]
        x, y, c, chips = _place()
        me, sibling = (x, y, c), (x, y, 1 - c)

        def rows(px, py, pc):
            return out_ref.at[pl.ds((4 * px + 2 * py + pc) * R, R), :]

        def copy(k, block, to, src=None):
            return pltpu.make_async_remote_copy(
                src_ref=rows(*block) if src is None else src, dst_ref=rows(*block),
                send_sem=send_sems.at[k], recv_sem=recv_sems.at[k], device_id=to, device_id_type=MESH)

        mine = pltpu.make_async_copy(x_ref, rows(*me), local_sem)
        mine.start()
        first = [copy(0, me, sibling, src=x_ref)]
        first += [copy(1 + j, me, (*chip, c), src=x_ref) for j, chip in enumerate(chips)]
        for cp in first:
            cp.start()
        passed = [copy(4 + j, (*chip, c), sibling) for j, chip in enumerate(chips)]
        for j, chip in enumerate(chips):
            copy(1 + j, (*chip, c), me).wait_recv()
            passed[j].start()
        copy(0, sibling, me).wait_recv()
        for j, chip in enumerate(chips):
            copy(4 + j, (*chip, 1 - c), me).wait_recv()
        for cp in first + passed:
            cp.wait_send()
        mine.wait()

    return pl.pallas_call(
        body, out_shape=jax.ShapeDtypeStruct((N_DEV * R, W), v.dtype),
        in_specs=[pl.BlockSpec(memory_space=pltpu.VMEM)] + [_ANY] * len(extra),
        out_specs=pl.BlockSpec(memory_space=pltpu.VMEM),
        scratch_shapes=[pltpu.SemaphoreType.DMA((7,)), pltpu.SemaphoreType.DMA((7,)), pltpu.SemaphoreType.DMA],
        name=name)(v, *extra)


def allgather_weights(shards, name="allgather_weights"):
    n = len(shards)

    def body(*refs):
        ins, outs = refs[:n], refs[n:2 * n]
        send_sems, recv_sems = refs[2 * n:]
        x, y, c, chips = _place()
        p = 2 * x + y
        sibling = (x, y, 1 - c)

        def half(i, chip_id, core, ref=None):
            r = outs[i].at[chip_id] if ref is None else ref
            return r.at[core]

        def copy(i, k, chip_id, core, to, src=None):
            return pltpu.make_async_remote_copy(
                src_ref=half(i, chip_id, core) if src is None else src, dst_ref=half(i, chip_id, core),
                send_sem=send_sems.at[6 * i + k], recv_sem=recv_sems.at[6 * i + k], device_id=to, device_id_type=MESH)

        first = [copy(i, j, p, c, (*chip, c), src=half(i, p, c, ref=ins[i]))
                 for i in range(n) for j, chip in enumerate(chips)]
        for cp in first:
            cp.start()
        passed = []
        for i in range(n):
            for j, (cx, cy) in enumerate(chips):
                copy(i, j, 2 * cx + cy, c, sibling).wait_recv()
                fw = copy(i, 3 + j, 2 * cx + cy, c, sibling)
                fw.start()
                passed.append(fw)
        for i in range(n):
            for j, (cx, cy) in enumerate(chips):
                copy(i, 3 + j, 2 * cx + cy, 1 - c, sibling).wait_recv()
        for cp in first + passed:
            cp.wait_send()

    split = [s.reshape(2, s.shape[0] // 2, s.shape[1]) for s in shards]
    outs = pl.pallas_call(
        body, out_shape=[jax.ShapeDtypeStruct((N_CHIPS,) + s.shape, s.dtype) for s in split],
        in_specs=[_ANY] * n, out_specs=[_ANY] * n,
        scratch_shapes=[pltpu.SemaphoreType.DMA((6 * n,)), pltpu.SemaphoreType.DMA((6 * n,))],
        name=name)(*split)
    chip = 2 * lax.axis_index("x") + lax.axis_index("y")
    return [lax.dynamic_update_index_in_dim(o, s, chip, 0).reshape((N_CHIPS,) + sh.shape)
            for o, s, sh in zip(outs, split, shards)]


_HBM = pl.BlockSpec(memory_space=pltpu.HBM)
_SEM = pl.BlockSpec(memory_space=pltpu.SEMAPHORE)
_EFFECT = pltpu.SideEffectType.DATAFLOW_SIDE_EFFECTING


def _chip_copies(kind, srcs, lands, send_sems, recv_sems):
    x, y, c, chips = _place()
    p = 2 * x + y
    cps = []
    for i in range(len(srcs)):
        for j, (cx, cy) in enumerate(chips):
            if kind == "gather":
                src, dst = srcs[i].at[c], lands[i].at[p, c]
            else:
                src, dst = srcs[i].at[2 * cx + cy], lands[i].at[j]
            cps.append(pltpu.make_async_remote_copy(
                src_ref=src, dst_ref=dst, send_sem=send_sems.at[3 * i + j], recv_sem=recv_sems.at[3 * i + j],
                device_id=(cx, cy, c), device_id_type=MESH))
    return cps


def split_start(kind, srcs, land_shapes, after, name):
    n = len(srcs)

    def body(*refs):
        src_refs, land_refs = refs[:n], refs[n:2 * n]
        send_sems, recv_sems = refs[2 * n + 1], refs[2 * n + 2]
        token = refs[-1]
        for cp in _chip_copies(kind, src_refs, land_refs, send_sems, recv_sems):
            cp.start()
        token[...] = jnp.zeros_like(token)

    lands = [pltpu.with_memory_space_constraint(lax.empty(s, BF16), pltpu.HBM) for s in land_shapes]
    outs = pl.pallas_call(
        body, name=name,
        out_shape=(pltpu.SemaphoreType.DMA((3 * n,)), pltpu.SemaphoreType.DMA((3 * n,)),
                   *[pltpu.HBM(s.shape, s.dtype) for s in srcs], *[pltpu.HBM(s, BF16) for s in land_shapes],
                   jax.ShapeDtypeStruct((8, 128), F32)),
        in_specs=[_HBM] * (2 * n) + [_ANY],
        out_specs=(_SEM, _SEM, *([_HBM] * (2 * n)), pl.BlockSpec(memory_space=pltpu.VMEM)),
        input_output_aliases={i: 2 + i for i in range(2 * n)},
        compiler_params=pltpu.CompilerParams(has_side_effects=_EFFECT),
    )(*[pltpu.with_memory_space_constraint(s, pltpu.HBM) for s in srcs], *lands, after)
    return outs[0], outs[1], outs[2:2 + n], outs[2 + n:2 + 2 * n], outs[-1]


def split_wait(kind, send_sems, recv_sems, srcs, lands, after, name):
    n = len(srcs)

    def body(*refs):
        src_refs, land_refs = refs[:n], refs[n:2 * n]
        ssem, rsem = refs[2 * n], refs[2 * n + 1]
        for cp in _chip_copies(kind, src_refs, land_refs, ssem, rsem):
            cp.wait_send()
            cp.wait_recv()

    outs = pl.pallas_call(
        body, name=name,
        out_shape=[pltpu.HBM(s.shape, s.dtype) for s in srcs] + [pltpu.HBM(s.shape, s.dtype) for s in lands],
        in_specs=[_HBM] * (2 * n) + [_SEM, _SEM, _ANY], out_specs=[_HBM] * (2 * n),
        input_output_aliases={i: i for i in range(2 * n)},
        compiler_params=pltpu.CompilerParams(has_side_effects=_EFFECT),
    )(*srcs, *lands, send_sems, recv_sems, after)
    return outs[:n], outs[n:]


def pass_to_sibling(lands):
    n = len(lands)

    def body(*refs):
        ins, outs = refs[:n], refs[n:2 * n]
        send_sems, recv_sems = refs[2 * n:]
        x, y, c, chips = _place()
        cps = []
        for i in range(n):
            for j, (cx, cy) in enumerate(chips):
                blk = outs[i].at[2 * cx + cy, c]
                cps.append(pltpu.make_async_remote_copy(
                    src_ref=ins[i].at[2 * cx + cy, c], dst_ref=blk, send_sem=send_sems.at[3 * i + j],
                    recv_sem=recv_sems.at[3 * i + j], device_id=(x, y, 1 - c), device_id_type=MESH))
        for cp in cps:
            cp.start()
        for cp in cps:
            cp.wait()

    return pl.pallas_call(
        body, out_shape=[jax.ShapeDtypeStruct(t.shape, t.dtype) for t in lands], in_specs=[_ANY] * n,
        out_specs=[_ANY] * n, input_output_aliases={i: i for i in range(n)},
        scratch_shapes=[pltpu.SemaphoreType.DMA((3 * n,)), pltpu.SemaphoreType.DMA((3 * n,))],
        name="ag_pass_to_sibling")(*lands)


def exchange_halves_to_sibling(gs, name):
    n = len(gs)

    def body(*refs):
        ins, outs = refs[:n], refs[n:2 * n]
        send_sems, recv_sems = refs[2 * n:]
        x, y, c, _ = _place()
        cps = []
        for i in range(n):
            h = ins[i].shape[1] // 2
            cps.append(pltpu.make_async_remote_copy(
                src_ref=ins[i].at[:, pl.ds((1 - c) * h, h), :], dst_ref=outs[i],
                send_sem=send_sems.at[i], recv_sem=recv_sems.at[i], device_id=(x, y, 1 - c), device_id_type=MESH))
        for cp in cps:
            cp.start()
        for cp in cps:
            cp.wait()

    return pl.pallas_call(
        body, out_shape=[jax.ShapeDtypeStruct((g.shape[0], g.shape[1] // 2, g.shape[2]), g.dtype) for g in gs],
        in_specs=[_ANY] * n, out_specs=[_ANY] * n,
        scratch_shapes=[pltpu.SemaphoreType.DMA((n,)), pltpu.SemaphoreType.DMA((n,))],
        name=name)(*gs)


def scatter_to_chips(ps, name):
    n = len(ps)

    def body(*refs):
        ins, outs = refs[:n], refs[n:2 * n]
        send_sems, recv_sems = refs[2 * n:]
        x, y, c, chips = _place()
        cps = []
        for i in range(n):
            for j, (cx, cy) in enumerate(chips):
                cps.append(pltpu.make_async_remote_copy(
                    src_ref=ins[i].at[2 * cx + cy], dst_ref=outs[i].at[j], send_sem=send_sems.at[3 * i + j],
                    recv_sem=recv_sems.at[3 * i + j], device_id=(cx, cy, c), device_id_type=MESH))
        for cp in cps:
            cp.start()
        for cp in cps:
            cp.wait()

    return pl.pallas_call(
        body, out_shape=[jax.ShapeDtypeStruct((3,) + t.shape[1:], t.dtype) for t in ps],
        in_specs=[_ANY] * n, out_specs=[_ANY] * n,
        scratch_shapes=[pltpu.SemaphoreType.DMA((3 * n,)), pltpu.SemaphoreType.DMA((3 * n,))],
        name=name)(*ps)


def join_halves(rs, name):
    n = len(rs)

    def body(*refs):
        ins, outs = refs[:n], refs[n:2 * n]
        send_sems, recv_sems = refs[2 * n:]
        x, y, c, _ = _place()
        cps = [pltpu.make_async_remote_copy(
            src_ref=ins[i], dst_ref=outs[i], send_sem=send_sems.at[i], recv_sem=recv_sems.at[i],
            device_id=(x, y, 1 - c), device_id_type=MESH) for i in range(n)]
        for cp in cps:
            cp.start()
        for cp in cps:
            cp.wait()

    return pl.pallas_call(
        body, out_shape=[jax.ShapeDtypeStruct(r.shape, r.dtype) for r in rs],
        in_specs=[_ANY] * n, out_specs=[_ANY] * n,
        scratch_shapes=[pltpu.SemaphoreType.DMA((n,)), pltpu.SemaphoreType.DMA((n,))],
        name=name)(*rs)


def _pack(parts, row_mult=8):
    flat = jnp.concatenate([p.reshape(-1).astype(F32) for p in parts])
    unit = row_mult * 128
    n = -(-flat.shape[0] // unit) * unit
    return jnp.pad(flat, (0, n - flat.shape[0])).reshape(n // 128, 128)


def _unpack(flat, shapes):
    out, off = [], 0
    for s in shapes:
        n = int(np.prod(s))
        out.append(flat[off:off + n].reshape(s))
        off += n
    return out


def _gather_packed(parts, name):
    packed = _pack(parts)
    g = allgather_small(packed, name).reshape(N_DEV, -1)
    return _unpack_rows(g, [p.shape for p in parts])


def _unpack_rows(g, shapes):
    out, off = [], 0
    for s in shapes:
        n = int(np.prod(s))
        out.append(g[:, off:off + n].reshape((g.shape[0],) + tuple(s)))
        off += n
    return out


def _by_chip(t, axis):
    return jnp.concatenate([t[2 * p] for p in range(N_CHIPS)], axis=axis)


def kernel(x, c, ada_w, ada_b, ln_g, ln_b, a_in_w, a_conv_w, a_conv_b, a_dt_bias, a_A_log, a_D, a_norm_g, a_out_w, kv_w, b_in_w, b_out_w, loss_target, m_ada_w, m_ada_b, m_ln_g, m_ln_b, m_a_in_w, m_a_conv_w, m_a_conv_b, m_a_dt_bias, m_a_A_log, m_a_D, m_a_norm_g, m_a_out_w, m_kv_w, m_b_in_w, m_b_out_w, v_ada_w, v_ada_b, v_ln_g, v_ln_b, v_a_in_w, v_a_conv_w, v_a_conv_b, v_a_dt_bias, v_a_A_log, v_a_D, v_a_norm_g, v_a_out_w, v_kv_w, v_b_in_w, v_b_out_w):
    ax, ay, ac = lax.axis_index("x"), lax.axis_index("y"), lax.axis_index("c")
    chip = 2 * ax + ay
    dev = 4 * ax + 2 * ay + ac
    xin = x[0]
    tgt = loss_target[0]
    L, D = xin.shape
    G, P = SSD_G, SSD_P
    H = a_dt_bias.shape[1]
    Kh = H // G
    DI = H * P
    CONVD = a_conv_b.shape[1] * N_CHIPS
    HW = DIL_H * DIL_E
    Ws = ada_w.shape[2]

    (w_in_g,) = allgather_weights([a_in_w[0].astype(BF16)], "allgather_w_in")
    later = [a_out_w[0].astype(BF16), kv_w.astype(BF16), b_in_w[0].astype(BF16), b_out_w[0].astype(BF16)]
    later_split = [s.reshape(2, s.shape[0] // 2, s.shape[1]) for s in later]
    ag_ssem, ag_rsem, ag_srcs, ag_lands, ag_token = split_start(
        "gather", later_split, [(N_CHIPS,) + s.shape for s in later_split], w_in_g, "ag_later_start")
    w_in = jnp.transpose(w_in_g, (1, 0, 2)).reshape(D, -1)
    w_zx = w_in[:, :DI + CONVD]
    w_dt = jnp.pad(w_in[:, DI + CONVD:], ((0, 0), (0, 128 - H)))

    c8, cw8, cb8, ng8 = _gather_packed([c[0], a_conv_w[0], a_conv_b[0], a_norm_g[0]], "allgather_small_params")
    conv_w = _by_chip(cw8, 1)
    conv_b = _by_chip(cb8, 0).reshape(1, CONVD)
    norm_g = _by_chip(ng8, 0).reshape(1, DI)

    mod_s = ada_fwd(c8, ada_w)
    (mod8,) = _gather_packed([mod_s], "allgather_small_mod")
    mods = _by_chip(mod8, 2)
    mod = lax.dynamic_index_in_dim(mods, dev, axis=1, keepdims=False) + ada_b
    shift = [mod[l:l + 1, :D] for l in range(DEPTH)]
    scale = [mod[l:l + 1, D:2 * D] for l in range(DEPTH)]
    gate = [mod[l:l + 1, 2 * D:] for l in range(DEPTH)]
    lg = [ln_g[l:l + 1] for l in range(DEPTH)]
    lb = [ln_b[l:l + 1] for l in range(DEPTH)]

    h0 = modulate(xin, scale[0] + ag_token[0:1, 0:1], shift[0], "modulate0")
    zx = mm_nn(h0, w_zx, BF16, "mm_in_zx")
    dtp = mm_nn(h0, w_dt, F32, "mm_in_dt")
    xbc = conv_fwd(zx, DI, conv_w, conv_b)
    dtp_g = jnp.transpose(dtp[:, :H].reshape(L, G, Kh), (1, 0, 2))
    dtp_gT = jnp.transpose(dtp_g, (0, 2, 1))
    vecs = [a_dt_bias.reshape(G, 1, Kh), a_dt_bias.reshape(G, Kh, 1), a_A_log.reshape(G, 1, Kh),
            a_A_log.reshape(G, Kh, 1), a_D.reshape(G, 1, Kh), a_D.reshape(G, Kh, 1)]
    y_ssd, states = ssd_fwd(xbc, dtp_g, dtp_gT, *vecs, DI)
    yn = rms_gate_fwd(y_ssd, zx, norm_g)
    later_split, ag_lands = split_wait("gather", ag_ssem, ag_rsem, ag_srcs, ag_lands, yn, "ag_later_wait")
    ag_lands = pass_to_sibling(ag_lands)
    w_out_g, w_kv_g, w_bin_g, w_bout_g = [
        lax.dynamic_update_index_in_dim(o, s, chip, 0).reshape((N_CHIPS,) + full.shape)
        for o, s, full in zip(ag_lands, later_split, later)]
    ymix0 = mm_nn(yn, w_out_g, F32, "mm_out_a", stack="row")
    x1, x1b, h1 = ln_mid(xin, ymix0, gate[0], lg[0], lb[0], scale[1], shift[1])

    kvp = mm_nn(x1b, w_kv_g, BF16, "mm_kv", stack="col")
    qz = mm_nn(h1, w_bin_g, BF16, "mm_in_b", stack="col")
    os_, lses = [], []
    for gi in range(len(DIL_PATTERNS)):
        o, lse = attn_fwd(qz, kvp, gi)
        os_.append(o)
        lses.append(lse)
    om = merge_fwd(os_, lses, qz)
    ymix1 = mm_nn(om, w_bout_g, F32, "mm_out_b", stack="col")
    dx2, sq = ln_final(x1, ymix1, gate[1], lg[1], lb[1], tgt)
    loss_part = 0.5 * jnp.sum(sq) / D

    dres2, dy2, dg1, db1, dgate1 = ln_bwd(dx2, x1, ymix1, gate[1], lg[1], "ln_bwd1")
    g_bout = mm_tn(om, dy2, BF16, "mm_gw_out_b", stack="col")
    dgated = mm_nt(dy2, w_bout_g, BF16, "mm_gx_out_b", stack="col")
    dos, dprs, dz_b = merge_bwd(dgated, os_, lses, qz)
    dqs, dks, dvs = [], [], []
    for gi in range(len(DIL_PATTERNS)):
        dq, dk, dv = attn_bwd(qz, kvp, dos[gi], lses[gi], dprs[gi], gi)
        dqs.append(dq)
        dks.append(dk)
        dvs.append(dv)
    dqz = jnp.concatenate(dqs + [dz_b], axis=1)
    dkv = jnp.concatenate(dks + dvs, axis=1)
    g_bin = mm_tn(h1, dqz, BF16, "mm_gw_in_b", stack="col")
    dh1 = mm_nt(dqz, w_bin_g, F32, "mm_gx_in_b", stack="col")
    g_kv = mm_tn(x1b, dkv, BF16, "mm_gw_kv", stack="col")
    dx1_kv = mm_nt(dkv, w_kv_g, F32, "mm_gx_kv", stack="col")
    dx1, dscale1, dshift1 = mod_bwd(dres2, dh1, dx1_kv, x1, scale[1], "mod_bwd1", through_mod=False)

    core = ac.astype(jnp.int32).reshape(1)
    chip_i = chip.astype(jnp.int32).reshape(1)

    def begin_scatter(gs, nms, tag):
        sib = exchange_halves_to_sibling(gs, "rs_sibling_exchange_" + tag)
        parts = [add_half(g, a, core, "rs_add_" + nm) for g, a, nm in zip(gs, sib, nms)]
        return split_start("scatter", parts, [(3,) + t.shape[1:] for t in parts], parts[0], "rs_%s_start" % tag)

    def finish_scatter(handles, after, tag):
        nms, owns, landed = [], [], []
        for k, (handle, hn) in enumerate(handles):
            parts, lands = split_wait("scatter", handle[0], handle[1], handle[2], handle[3], after,
                                      "rs_%s%d_wait" % (tag, k))
            nms += hn
            owns += list(parts)
            landed += list(lands)
        halves = [sum_partials(own, t, chip_i, "rs_sum_" + nm) for own, t, nm in zip(owns, landed, nms)]
        theirs = join_halves(halves, "rs_join_halves_" + tag)
        return dict(zip(nms, zip(halves, theirs)))

    names_b = ["kv", "in_b", "out_b"]
    rs_b = begin_scatter([g_kv, g_bin, g_bout], names_b, "b")

    dres1, dy1, dg0, db0, dgate0 = ln_bwd(dx1, xin, ymix0, gate[0] + rs_b[4][0:1, 0:1], lg[0], "ln_bwd0")
    g_out = mm_tn(yn, dy1, BF16, "mm_gw_out_a", stack="row")
    rs_a1 = begin_scatter([g_out], ["out_a"], "a1")
    dyn = mm_nt(dy1, w_out_g, BF16, "mm_gx_out_a", stack="row")
    dy_ssd, dz_a, dnorm_g = rms_gate_bwd(dyn, y_ssd, zx, norm_g + rs_a1[4][0:1, 0:1])
    dxs, dB, dC, ddtp_g, dbias_g, dalog_g, dD_g = ssd_bwd(xbc, dtp_g, dtp_gT, *vecs, states, dy_ssd, DI)
    dxbc = jnp.concatenate([dxs, dB, dC], axis=1)
    dxbc_pre, dconv_w, dconv_b = conv_bwd(zx, DI, conv_w, conv_b, dxbc)
    dzx = jnp.concatenate([dz_a, dxbc_pre], axis=1)
    ddtp = jnp.pad(jnp.transpose(ddtp_g, (1, 0, 2)).reshape(L, H), ((0, 0), (0, 128 - H)))
    g_zx = mm_tn(h0, dzx, BF16, "mm_gw_in_zx")
    g_dt = mm_tn(h0, ddtp, BF16, "mm_gw_in_dt")
    g_in = jnp.concatenate([g_zx, g_dt[:, :H]], axis=1)
    g_in = jnp.transpose(g_in.reshape(D, N_CHIPS, -1), (1, 0, 2))
    rs_a2 = begin_scatter([g_in], ["in_a"], "a2")
    dh0 = mm_nt(dzx, w_zx, F32, "mm_gx_in_zx")
    dh0_dt = mm_nt(ddtp, w_dt, F32, "mm_gx_in_dt")
    grad_x, dscale0, dshift0 = mod_bwd(dres1, dh0, dh0_dt, xin, scale[0] + rs_a2[4][0:1, 0:1], "mod_bwd0",
                                       through_mod=True)
    g_halves = finish_scatter([(rs_b, names_b)], grad_x, "b")

    def step_halves(w, m, v, nm):
        shp = w.shape
        mine, theirs_ = g_halves[nm]
        outs4 = adamw_halves(w.reshape(-1, shp[-1]), mine, theirs_, m.reshape(-1, shp[-1]), v.reshape(-1, shp[-1]),
                             core, "adamw_" + nm)
        return tuple(t.reshape(shp) for t in outs4)

    big = {
        "kv_w": step_halves(kv_w, m_kv_w, v_kv_w, "kv"),
        "b_in_w": step_halves(b_in_w, m_b_in_w, v_b_in_w, "in_b"),
        "b_out_w": step_halves(b_out_w, m_b_out_w, v_b_out_w, "out_b"),
    }
    g_halves.update(finish_scatter([(rs_a1, ["out_a"]), (rs_a2, ["in_a"])], big["kv_w"][1], "a"))
    big["a_in_w"] = step_halves(a_in_w, m_a_in_w, v_a_in_w, "in_a")
    big["a_out_w"] = step_halves(a_out_w, m_a_out_w, v_a_out_w, "out_a")

    dmod = jnp.concatenate([jnp.concatenate([dshift0, dscale0, dgate0], axis=1),
                            jnp.concatenate([dshift1, dscale1, dgate1], axis=1)], axis=0)
    small_parts = [jnp.concatenate([dg0, dg1], axis=0), jnp.concatenate([db0, db1], axis=0),
                   dbias_g.reshape(1, H), dalog_g.reshape(1, H), dD_g.reshape(1, H),
                   dconv_w, dconv_b, dnorm_g, loss_part.reshape(1, 1)]
    small_shapes = [p.shape for p in small_parts]
    packed = jnp.concatenate([_pack([dmod]), _pack(small_parts)], axis=0)
    n_mod_rows = _pack([dmod]).shape[0]
    gathered = allgather_small(packed, "allgather_small_grads", after=g_halves["in_a"][1]).reshape(N_DEV, -1, 128)
    dmod8 = gathered[:, :n_mod_rows].reshape(N_DEV, -1)[:, :2 * 3 * D].reshape(N_DEV, DEPTH, 3 * D)
    summed = sum_leading(gathered, "sum_small")
    g_ada_b = summed[:n_mod_rows].reshape(-1)[:2 * 3 * D].reshape(DEPTH, 3 * D)
    (g_ln_g, g_ln_b, g_dt_bias, g_a_log, g_dsk, g_conv_w, g_conv_b, g_norm_g, loss_all) = _unpack(
        summed[n_mod_rows:].reshape(-1), small_shapes)
    loss = loss_all.reshape(())
    Cs = CONVD // N_CHIPS
    g_conv_w_s = lax.dynamic_slice_in_dim(g_conv_w, chip * Cs, Cs, axis=1)
    g_conv_b_s = lax.dynamic_slice_in_dim(g_conv_b, chip * Cs, Cs, axis=1)
    g_norm_g_s = lax.dynamic_slice_in_dim(g_norm_g, chip * (DI // N_CHIPS), DI // N_CHIPS, axis=1)
    dmod_s = jnp.transpose(lax.dynamic_slice_in_dim(dmod8, chip * Ws, Ws, axis=2), (1, 0, 2))
    g_ada_w = ada_wgrad(jnp.transpose(c8), dmod_s)

    def step2d(w, g, m, v, nm):
        shp = w.shape
        d_, m_, v_ = adamw(w.reshape(-1, shp[-1]), g.reshape(-1, shp[-1]), m.reshape(-1, shp[-1]),
                           v.reshape(-1, shp[-1]), "adamw_" + nm)
        return g.reshape(shp), d_.reshape(shp), m_.reshape(shp), v_.reshape(shp)

    big["ada_w"] = step2d(ada_w, g_ada_w, m_ada_w, v_ada_w, "ada_w")
    small_names = ["ada_b", "ln_g", "ln_b", "a_conv_w", "a_conv_b", "a_dt_bias", "a_A_log", "a_D", "a_norm_g"]
    small_w = [ada_b, ln_g, ln_b, a_conv_w, a_conv_b, a_dt_bias, a_A_log, a_D, a_norm_g]
    small_m = [m_ada_b, m_ln_g, m_ln_b, m_a_conv_w, m_a_conv_b, m_a_dt_bias, m_a_A_log, m_a_D, m_a_norm_g]
    small_v = [v_ada_b, v_ln_g, v_ln_b, v_a_conv_w, v_a_conv_b, v_a_dt_bias, v_a_A_log, v_a_D, v_a_norm_g]
    small_g = [g_ada_b, g_ln_g, g_ln_b, g_conv_w_s, g_conv_b_s, g_dt_bias, g_a_log, g_dsk, g_norm_g_s]
    shapes = [w.shape for w in small_w]
    small_g = [g.reshape(s) for g, s in zip(small_g, shapes)]
    d_p, m_p, v_p = adamw(_pack(small_w), _pack(small_g), _pack(small_m), _pack(small_v), "adamw_small")
    small = {}
    for nm, g, d_, m_, v_ in zip(small_names, small_g, _unpack(d_p.reshape(-1), shapes), _unpack(m_p.reshape(-1), shapes),
                                 _unpack(v_p.reshape(-1), shapes)):
        small[nm] = (g, d_, m_, v_)
    allw = {**big, **small}
    order = ["ada_w", "ada_b", "ln_g", "ln_b", "a_in_w", "a_conv_w", "a_conv_b", "a_dt_bias", "a_A_log", "a_D",
             "a_norm_g", "a_out_w", "kv_w", "b_in_w", "b_out_w"]
    outs = [loss, grad_x.reshape(x.shape)]
    for k in range(4):
        outs += [allw[n][k] for n in order]
    return tuple(outs)
```

```python
import functools

import jax
import jax.numpy as jnp
import numpy as np
from jax import lax
from jax.experimental import pallas as pl
from jax.experimental.pallas import tpu as pltpu

F32 = jnp.float32
BF16 = jnp.bfloat16
MESH = pl.DeviceIdType.MESH

DEPTH = 2
ALPHA = (2 * DEPTH) ** 0.25
LN_EPS = 1e-5
RMS_EPS = 1e-5
SSD_P = 64
SSD_N = 128
SSD_Q = 256
SSD_G = 8
CONV_W = 4
DIL_PATTERNS = ((128, 1), (512, 4), (2048, 16))
DIL_H = 8
DIL_E = 128
DIL_BLK = 128
ADAM_LR, ADAM_B1, ADAM_B2, ADAM_EPS, ADAM_WD, ADAM_STEP = 0.001, 0.9, 0.999, 1e-08, 0.01, 10

VMEM_LIMIT = 56 * 1024 * 1024
N_CHIPS = 4
N_DEV = 8


def _tile(dim, target, mult=128):
    if dim <= target:
        return dim
    t = (target // mult) * mult
    while t >= mult:
        if dim % t == 0:
            return t
        t -= mult
    return dim


def _cp(sem):
    return pltpu.CompilerParams(dimension_semantics=sem, vmem_limit_bytes=VMEM_LIMIT)


def _sigmoid(x):
    return 1.0 / (1.0 + jnp.exp(-x))


def _silu(x):
    return x * _sigmoid(x)


def _dsilu(x):
    s = _sigmoid(x)
    return s * (1.0 + x * (1.0 - s))


def _softplus(x):
    return jnp.maximum(x, 0.0) + jnp.log(1.0 + jnp.exp(-jnp.abs(x)))


def _mm_call(a, b, out_shape, grid, a_spec, b_spec, o_spec, acc_shape, dims, name, after=None):
    nk = grid[2]
    extra = [] if after is None else [after]

    def prod(a_ref, b_ref):
        return lax.dot_general(a_ref[...].astype(BF16), b_ref[...].astype(BF16), (dims, ((), ())),
                               preferred_element_type=F32)

    def body_single(a_ref, b_ref, *rest):
        o_ref = rest[len(extra)]
        o_ref[...] = prod(a_ref, b_ref).astype(o_ref.dtype)

    def body_multi(a_ref, b_ref, *rest):
        o_ref, acc_ref = rest[len(extra):]
        k = pl.program_id(2)

        @pl.when(k == 0)
        def _():
            acc_ref[...] = prod(a_ref, b_ref)

        @pl.when(jnp.logical_and(k > 0, k < nk - 1))
        def _():
            acc_ref[...] += prod(a_ref, b_ref)

        @pl.when(k == nk - 1)
        def _():
            o_ref[...] = (acc_ref[...] + prod(a_ref, b_ref)).astype(o_ref.dtype)

    return pl.pallas_call(
        body_single if nk == 1 else body_multi, grid=grid, in_specs=[a_spec, b_spec] + [_ANY] * len(extra),
        out_specs=o_spec, out_shape=out_shape, scratch_shapes=[] if nk == 1 else [pltpu.VMEM(acc_shape, F32)],
        compiler_params=_cp(("parallel", "parallel", "arbitrary")), name=name)(a, b, *extra)


def mm_nn(a, b, out_dtype, name, stack=None, tm=1024, tn=1024, tk=2048):
    M, K = a.shape
    if stack is None:
        N = b.shape[1]
        tn, tk = _tile(N, tn), _tile(K, tk)
        b_spec = pl.BlockSpec((tk, tn), lambda i, j, k: (k, j))
    elif stack == "col":
        S, _, Ns = b.shape
        N = S * Ns
        tn, tk = _tile(Ns, tn), _tile(K, tk)
        npb = Ns // tn
        b_spec = pl.BlockSpec((None, tk, tn), lambda i, j, k: (j // npb, k, j % npb))
    else:
        S, Ks, N = b.shape
        tn, tk = _tile(N, tn), _tile(Ks, tk)
        kpb = Ks // tk
        b_spec = pl.BlockSpec((None, tk, tn), lambda i, j, k: (k // kpb, k % kpb, j))
    tm = _tile(M, tm)
    return _mm_call(a, b, jax.ShapeDtypeStruct((M, N), out_dtype), (M // tm, N // tn, K // tk),
                    pl.BlockSpec((tm, tk), lambda i, j, k: (i, k)), b_spec,
                    pl.BlockSpec((tm, tn), lambda i, j, k: (i, j)), (tm, tn), ((1,), (0,)), name)


def mm_nt(a, b, out_dtype, name, stack=None, tm=1024, tn=1024, tk=2048, after=None):
    M, C = a.shape
    if stack is None:
        Kw = b.shape[0]
        tn, tk = _tile(Kw, tn), _tile(C, tk)
        b_spec = pl.BlockSpec((tn, tk), lambda i, j, k: (j, k))
    elif stack == "col":
        S, Kw, Cs = b.shape
        tn, tk = _tile(Kw, tn), _tile(Cs, tk)
        cpb = Cs // tk
        b_spec = pl.BlockSpec((None, tn, tk), lambda i, j, k: (k // cpb, j, k % cpb))
    else:
        S, Ks, _ = b.shape
        Kw = S * Ks
        tn, tk = _tile(Ks, tn), _tile(C, tk)
        jpb = Ks // tn
        b_spec = pl.BlockSpec((None, tn, tk), lambda i, j, k: (j // jpb, j % jpb, k))
    tm = _tile(M, tm)
    return _mm_call(a, b, jax.ShapeDtypeStruct((M, Kw), out_dtype), (M // tm, Kw // tn, C // tk),
                    pl.BlockSpec((tm, tk), lambda i, j, k: (i, k)), b_spec,
                    pl.BlockSpec((tm, tn), lambda i, j, k: (i, j)), (tm, tn), ((1,), (1,)), name, after=after)


def mm_tn(a, b, out_dtype, name, stack=None, n_stack=N_CHIPS, tm=1024, tn=1024, tk=2048):
    L, M = a.shape
    N = b.shape[1]
    tk = _tile(L, tk)
    if stack is None:
        tm, tn = _tile(M, tm), _tile(N, tn)
        o_spec = pl.BlockSpec((tm, tn), lambda i, j, k: (i, j))
        out_shape = (M, N)
    elif stack == "col":
        Ns = N // n_stack
        tm, tn = _tile(M, tm), _tile(Ns, tn)
        npb = Ns // tn
        o_spec = pl.BlockSpec((None, tm, tn), lambda i, j, k: (j // npb, i, j % npb))
        out_shape = (n_stack, M, Ns)
    else:
        Ms = M // n_stack
        tm, tn = _tile(Ms, tm), _tile(N, tn)
        mpb = Ms // tm
        o_spec = pl.BlockSpec((None, tm, tn), lambda i, j, k: (i // mpb, i % mpb, j))
        out_shape = (n_stack, Ms, N)
    return _mm_call(a, b, jax.ShapeDtypeStruct(out_shape, out_dtype), (M // tm, N // tn, L // tk),
                    pl.BlockSpec((tk, tm), lambda i, j, k: (k, i)), pl.BlockSpec((tk, tn), lambda i, j, k: (k, j)),
                    o_spec, (tm, tn), ((0,), (0,)), name)


def _row_specs(tr, widths):
    return [pl.BlockSpec((tr, w), lambda i: (i, 0)) for w in widths]


def _vec_spec(w):
    return pl.BlockSpec((1, w), lambda i: (0, 0))


def _acc_rows(ref, val, i):
    s = jnp.sum(val, axis=0, keepdims=True)

    @pl.when(i == 0)
    def _():
        ref[...] = s

    @pl.when(i > 0)
    def _():
        ref[...] += s


def modulate(x, scale, shift, name):
    L, D = x.shape
    tr = _tile(L, 512, 16)

    def body(x_ref, sc_ref, sh_ref, h_ref):
        h_ref[...] = (x_ref[...] * (1.0 + sc_ref[...]) + sh_ref[...]).astype(BF16)

    return pl.pallas_call(
        body, grid=(L // tr,), in_specs=_row_specs(tr, [D]) + [_vec_spec(D)] * 2, out_specs=_row_specs(tr, [D])[0],
        out_shape=jax.ShapeDtypeStruct((L, D), BF16), compiler_params=_cp(("parallel",)), name=name)(x, scale, shift)


def _ln_core(x, y, gate, g, b):
    u = ALPHA * x + (1.0 + gate) * y
    mu = jnp.mean(u, axis=-1, keepdims=True)
    d = u - mu
    var = jnp.mean(d * d, axis=-1, keepdims=True)
    rstd = lax.rsqrt(var + LN_EPS)
    xhat = d * rstd
    return xhat * g + b, xhat, rstd


def ln_mid(x, y, gate, g, b, scale, shift):
    L, D = x.shape
    tr = _tile(L, 256, 16)

    def body(x_ref, y_ref, gate_ref, g_ref, b_ref, sc_ref, sh_ref, x1_ref, x1b_ref, h_ref):
        x1, _, _ = _ln_core(x_ref[...], y_ref[...], gate_ref[...], g_ref[...], b_ref[...])
        x1_ref[...] = x1
        x1b_ref[...] = x1.astype(BF16)
        h_ref[...] = (x1 * (1.0 + sc_ref[...]) + sh_ref[...]).astype(BF16)

    return pl.pallas_call(
        body, grid=(L // tr,), in_specs=_row_specs(tr, [D, D]) + [_vec_spec(D)] * 5,
        out_specs=_row_specs(tr, [D, D, D]),
        out_shape=[jax.ShapeDtypeStruct((L, D), F32), jax.ShapeDtypeStruct((L, D), BF16),
                   jax.ShapeDtypeStruct((L, D), BF16)],
        compiler_params=_cp(("parallel",)), name="ln_mid")(x, y, gate, g, b, scale, shift)


def ln_final(x, y, gate, g, b, target):
    L, D = x.shape
    tr = _tile(L, 256, 16)

    def body(x_ref, y_ref, gate_ref, g_ref, b_ref, t_ref, dout_ref, sq_ref):
        out, _, _ = _ln_core(x_ref[...], y_ref[...], gate_ref[...], g_ref[...], b_ref[...])
        err = out - t_ref[...]
        dout_ref[...] = err * (1.0 / D)
        _acc_rows(sq_ref, err * err, pl.program_id(0))

    return pl.pallas_call(
        body, grid=(L // tr,), in_specs=_row_specs(tr, [D, D]) + [_vec_spec(D)] * 3 + _row_specs(tr, [D]),
        out_specs=[_row_specs(tr, [D])[0], _vec_spec(D)],
        out_shape=[jax.ShapeDtypeStruct((L, D), F32), jax.ShapeDtypeStruct((1, D), F32)],
        compiler_params=_cp(("arbitrary",)), name="ln_final")(x, y, gate, g, b, target)


def ln_bwd(dout, x, y, gate, g, name):
    L, D = x.shape
    tr = _tile(L, 256, 16)

    def body(do_ref, x_ref, y_ref, gate_ref, g_ref, dres_ref, dy_ref, dg_ref, db_ref, dgate_ref):
        i = pl.program_id(0)
        yv = y_ref[...]
        dout_v = do_ref[...]
        _, xhat, rstd = _ln_core(x_ref[...], yv, gate_ref[...], g_ref[...], 0.0)
        dxh = dout_v * g_ref[...]
        m1 = jnp.mean(dxh, axis=-1, keepdims=True)
        m2 = jnp.mean(dxh * xhat, axis=-1, keepdims=True)
        du = rstd * (dxh - m1 - xhat * m2)
        dres_ref[...] = ALPHA * du
        dy_ref[...] = ((1.0 + gate_ref[...]) * du).astype(BF16)
        _acc_rows(dg_ref, dout_v * xhat, i)
        _acc_rows(db_ref, dout_v, i)
        _acc_rows(dgate_ref, du * yv, i)

    return pl.pallas_call(
        body, grid=(L // tr,), in_specs=_row_specs(tr, [D, D, D]) + [_vec_spec(D)] * 2,
        out_specs=_row_specs(tr, [D, D]) + [_vec_spec(D)] * 3,
        out_shape=[jax.ShapeDtypeStruct((L, D), F32), jax.ShapeDtypeStruct((L, D), BF16)]
        + [jax.ShapeDtypeStruct((1, D), F32)] * 3,
        compiler_params=_cp(("arbitrary",)), name=name)(dout, x, y, gate, g)


def mod_bwd(dres, dh, dh2, xin, scale, name, through_mod):
    L, D = xin.shape
    tr = _tile(L, 256, 16)

    def body(dres_ref, dh_ref, dh2_ref, x_ref, sc_ref, dx_ref, dsc_ref, dsh_ref):
        i = pl.program_id(0)
        dh_v = dh_ref[...]
        tot = dres_ref[...]
        if through_mod:
            dh_v = dh_v + dh2_ref[...]
        else:
            tot = tot + dh2_ref[...]
        dx_ref[...] = tot + dh_v * (1.0 + sc_ref[...])
        _acc_rows(dsc_ref, dh_v * x_ref[...], i)
        _acc_rows(dsh_ref, dh_v, i)

    return pl.pallas_call(
        body, grid=(L // tr,), in_specs=_row_specs(tr, [D, D, D, D]) + [_vec_spec(D)],
        out_specs=_row_specs(tr, [D]) + [_vec_spec(D)] * 2,
        out_shape=[jax.ShapeDtypeStruct((L, D), F32)] + [jax.ShapeDtypeStruct((1, D), F32)] * 2,
        compiler_params=_cp(("arbitrary",)), name=name)(dres, dh, dh2, xin, scale)


CONV_HALO = 16


def _conv_rows(x_ref, i, tr, L):
    nblk = L // tr
    s = pl.multiple_of(i * tr, CONV_HALO)
    cur = x_ref[pl.ds(s, tr), :].astype(F32)
    sp = pl.multiple_of(jnp.maximum(i * tr - CONV_HALO, 0), CONV_HALO)
    sn = pl.multiple_of(jnp.minimum(i * tr + tr, L - CONV_HALO), CONV_HALO)
    prev = x_ref[pl.ds(sp, CONV_HALO), :].astype(F32) * (i > 0).astype(F32)
    nxt = x_ref[pl.ds(sn, CONV_HALO), :].astype(F32) * (i < nblk - 1).astype(F32)
    return jnp.concatenate([prev, cur, nxt], axis=0)


def _shift_rows(v, j):
    n = v.shape[0]
    return v if j % n == 0 else pltpu.roll(v, j % n, 0)


def _conv_eval(xe, w_ref, b_ref):
    c = b_ref[...] + w_ref[CONV_W - 1:CONV_W, :] * xe
    for k in range(CONV_W - 1):
        c = c + w_ref[k:k + 1, :] * _shift_rows(xe, CONV_W - 1 - k)
    return c


def conv_fwd(zx, col0, conv_w, conv_b):
    L = zx.shape[0]
    C = conv_w.shape[1]
    tc = _tile(C, 512)
    tr = _tile(L, 512, CONV_HALO)
    off = col0 // tc

    def body(x_ref, w_ref, b_ref, o_ref):
        i = pl.program_id(1)
        xe = _conv_rows(x_ref, i, tr, L)
        c = _conv_eval(xe, w_ref, b_ref)[CONV_HALO:CONV_HALO + tr]
        o_ref[...] = _silu(c).astype(BF16)

    return pl.pallas_call(
        body, grid=(C // tc, L // tr),
        in_specs=[pl.BlockSpec((L, tc), lambda j, i: (0, off + j)), pl.BlockSpec((CONV_W, tc), lambda j, i: (0, j)),
                  pl.BlockSpec((1, tc), lambda j, i: (0, j))],
        out_specs=pl.BlockSpec((tr, tc), lambda j, i: (i, j)),
        out_shape=jax.ShapeDtypeStruct((L, C), BF16), compiler_params=_cp(("parallel", "arbitrary")),
        name="conv_fwd")(zx, conv_w, conv_b)


def conv_bwd(zx, col0, conv_w, conv_b, dxbc):
    L = zx.shape[0]
    C = conv_w.shape[1]
    tc = _tile(C, 512)
    tr = _tile(L, 512, CONV_HALO)
    off = col0 // tc
    H = CONV_HALO

    def body(x_ref, g_ref, w_ref, b_ref, dx_ref, dw_ref, db_ref):
        i = pl.program_id(1)
        xe = _conv_rows(x_ref, i, tr, L)
        ge = _conv_rows(g_ref, i, tr, L)
        dc = ge * _dsilu(_conv_eval(xe, w_ref, b_ref))
        dx = w_ref[CONV_W - 1:CONV_W, :] * dc
        for k in range(CONV_W - 1):
            dx = dx + w_ref[k:k + 1, :] * _shift_rows(dc, -(CONV_W - 1 - k))
        dx_ref[...] = dx[H:H + tr].astype(BF16)
        dcc = dc[H:H + tr]
        rows = [jnp.sum(dcc * _shift_rows(xe, CONV_W - 1 - k)[H:H + tr], axis=0, keepdims=True) for k in range(CONV_W)]
        dwv = jnp.concatenate(rows + [jnp.zeros((8 - CONV_W, tc), F32)], axis=0)
        dbv = jnp.sum(dcc, axis=0, keepdims=True)

        @pl.when(i == 0)
        def _():
            dw_ref[...] = dwv
            db_ref[...] = dbv

        @pl.when(i > 0)
        def _():
            dw_ref[...] += dwv
            db_ref[...] += dbv

    dx, dw, db = pl.pallas_call(
        body, grid=(C // tc, L // tr),
        in_specs=[pl.BlockSpec((L, tc), lambda j, i: (0, off + j)), pl.BlockSpec((L, tc), lambda j, i: (0, j)),
                  pl.BlockSpec((CONV_W, tc), lambda j, i: (0, j)), pl.BlockSpec((1, tc), lambda j, i: (0, j))],
        out_specs=[pl.BlockSpec((tr, tc), lambda j, i: (i, j)), pl.BlockSpec((8, tc), lambda j, i: (0, j)),
                   pl.BlockSpec((1, tc), lambda j, i: (0, j))],
        out_shape=[jax.ShapeDtypeStruct((L, C), BF16), jax.ShapeDtypeStruct((8, C), F32),
                   jax.ShapeDtypeStruct((1, C), F32)],
        compiler_params=_cp(("parallel", "arbitrary")), name="conv_bwd")(zx, dxbc, conv_w, conv_b)
    return dx, dw[:CONV_W], db


_NN = (((1,), (0,)), ((), ()))


def _pieces(x, n):
    out, r = [], x
    for _ in range(n):
        p = r.astype(BF16)
        out.append(p)
        r = r - p.astype(F32)
    return out


def _dot01(a, b01, n, dims=_NN):
    b = b01.astype(BF16)
    return functools.reduce(lambda u, v: u + v,
                            [lax.dot_general(p, b, dims, preferred_element_type=F32) for p in _pieces(a, n)])


def _dot01_left(a01, b, n, dims=_NN):
    a = a01.astype(BF16)
    return functools.reduce(lambda u, v: u + v,
                            [lax.dot_general(a, p, dims, preferred_element_type=F32) for p in _pieces(b, n)])


def _ssd_common(dtp_ref, dtpT_ref, bias_ref, biasT_ref, alog_ref, alogT_ref, b_ref, c_ref):
    Q = SSD_Q
    dt = _softplus(dtp_ref[...] + bias_ref[...])
    A = -jnp.exp(alog_ref[...])
    row = lax.broadcasted_iota(jnp.int32, (Q, Q), 0)
    col = lax.broadcasted_iota(jnp.int32, (Q, Q), 1)
    causal = row >= col
    tril = causal.astype(F32)
    Kh = dt.shape[1]
    acum = _dot01_left(tril, dt * A, 3)
    eye = (lax.broadcasted_iota(jnp.int32, (Kh, Kh), 0) == lax.broadcasted_iota(jnp.int32, (Kh, Kh), 1)).astype(F32)
    acumT = _dot01_left(eye, acum, 3, dims=(((1,), (1,)), ((), ())))
    Bm = b_ref[...]
    Cm = c_ref[...]
    cb = lax.dot_general(Cm, Bm, (((1,), (1,)), ((), ())), preferred_element_type=F32)
    return dt, A, causal, row, col, acum, acumT, Bm, Cm, cb


def _ssd_in_specs(Q, GP, N, Kh, DI, cmap):
    nb0 = DI // N
    vec = pl.BlockSpec((None, 1, Kh), lambda g, c: (g, 0, 0))
    vecT = pl.BlockSpec((None, Kh, 1), lambda g, c: (g, 0, 0))
    return [pl.BlockSpec((Q, GP), lambda g, c: (cmap(c), g)),
            pl.BlockSpec((Q, N), lambda g, c: (cmap(c), nb0 + g)),
            pl.BlockSpec((Q, N), lambda g, c: (cmap(c), nb0 + SSD_G + g)),
            pl.BlockSpec((None, Q, Kh), lambda g, c: (g, cmap(c), 0)),
            pl.BlockSpec((None, Kh, Q), lambda g, c: (g, 0, cmap(c))),
            vec, vecT, vec, vecT, vec, vecT]


def _hi(a, b01):
    return _dot01(a, b01, 2)


def _ssd_heads(dskT_ref, acum, acumT, dt, Kh):
    Q, P, N = SSD_Q, SSD_P, SSD_N
    GP = Kh * P
    sh_p = P.bit_length() - 1
    seg = lambda shape, dim: lax.shift_right_logical(lax.broadcasted_iota(jnp.int32, shape, dim), sh_p)
    E = (seg((Kh, GP), 1) == lax.broadcasted_iota(jnp.int32, (Kh, GP), 0)).astype(F32)
    ET = (seg((GP, Kh), 0) == lax.broadcasted_iota(jnp.int32, (GP, Kh), 1)).astype(F32)
    a_last = acum[Q - 1:Q, :]
    tail = jnp.exp(a_last - acum)
    eLT = jnp.exp(acumT[:, Q - 1:Q])
    rowseg = seg((GP, N), 0)
    eL_b = jnp.zeros((GP, N), F32)
    for k in range(Kh):
        eL_b = jnp.where(rowseg == k, eLT[k:k + 1, :], eL_b)
    return dict(
        E=E, ET=ET, a_last=a_last, tail=tail, eL_b=eL_b,
        dt_all=_hi(dt, E), ea_all=_hi(jnp.exp(acum), E), tail_all=_hi(tail, E),
        dsk_all=jnp.sum(E * dskT_ref[...], axis=0, keepdims=True))


def _head_chunks(GP):
    CW = min(GP, 128)
    return CW, CW // SSD_P, GP // CW


def _head_mask(Q, CW, kk):
    lane = lax.broadcasted_iota(jnp.int32, (Q, CW), 1)
    return jnp.logical_and(lane >= kk * SSD_P, lane < (kk + 1) * SSD_P)


def ssd_fwd(xbc, dtp_g, dtp_gT, bias_g, bias_gT, alog_g, alog_gT, dsk_g, dsk_gT, DI):
    L = xbc.shape[0]
    Q, P, N, G = SSD_Q, SSD_P, SSD_N, SSD_G
    GP = DI // G
    Kh = GP // P
    nc = L // Q

    CW, hpc, nch = _head_chunks(GP)
    nt = (((1,), (1,)), ((), ()))
    tn = (((0,), (0,)), ((), ()))

    def body(xs_ref, b_ref, c_ref, dtp_ref, dtpT_ref, bias_ref, biasT_ref, alog_ref, alogT_ref, dsk_ref, dskT_ref,
             y_ref, st_ref, state):
        @pl.when(pl.program_id(1) == 0)
        def _():
            state[...] = jnp.zeros(state.shape, F32)

        st_ref[...] = state[...]
        dt, A, causal, row, col, acum, acumT, Bm, Cm, cb = _ssd_common(
            dtp_ref, dtpT_ref, bias_ref, biasT_ref, alog_ref, alogT_ref, b_ref, c_ref)
        hd = _ssd_heads(dskT_ref, acum, acumT, dt, Kh)
        xs = xs_ref[...].astype(F32)
        xdt_all = xs * hd["dt_all"]
        S_all = state[...]
        y_all = (lax.dot_general(Cm, S_all.astype(BF16), nt, preferred_element_type=F32) * hd["ea_all"]
                 + xs * hd["dsk_all"])
        state[...] = S_all * hd["eL_b"] + lax.dot_general(
            (xdt_all * hd["tail_all"]).astype(BF16), Bm, tn, preferred_element_type=F32)
        for ch in range(nch):
            cs = slice(ch * CW, (ch + 1) * CW)
            xc = xdt_all[:, cs]
            acc = y_all[:, cs]
            for kk in range(hpc):
                k = ch * hpc + kk
                decay = jnp.exp(jnp.where(causal, acum[:, k:k + 1] - acumT[k:k + 1, :], -jnp.inf))
                xk = xc if hpc == 1 else jnp.where(_head_mask(Q, CW, kk), xc, 0.0)
                acc = acc + jnp.dot((cb * decay).astype(BF16), xk.astype(BF16), preferred_element_type=F32)
            y_ref[:, cs] = acc.astype(BF16)

    return pl.pallas_call(
        body, grid=(G, nc), in_specs=_ssd_in_specs(Q, GP, N, Kh, DI, lambda c: c),
        out_specs=[pl.BlockSpec((Q, GP), lambda g, c: (c, g)),
                   pl.BlockSpec((None, None, GP, N), lambda g, c: (c, g, 0, 0))],
        out_shape=[jax.ShapeDtypeStruct((L, DI), BF16), jax.ShapeDtypeStruct((nc, G, GP, N), F32)],
        scratch_shapes=[pltpu.VMEM((GP, N), F32)], compiler_params=_cp(("parallel", "arbitrary")),
        name="ssd_fwd")(xbc, xbc, xbc, dtp_g, dtp_gT, bias_g, bias_gT, alog_g, alog_gT, dsk_g, dsk_gT)


def ssd_bwd(xbc, dtp_g, dtp_gT, bias_g, bias_gT, alog_g, alog_gT, dsk_g, dsk_gT, states, dy, DI):
    L = xbc.shape[0]
    Q, P, N, G = SSD_Q, SSD_P, SSD_N, SSD_G
    GP = DI // G
    Kh = GP // P
    nc = L // Q
    rev = lambda c: nc - 1 - c

    CW, hpc, nch = _head_chunks(GP)

    def body(xs_ref, b_ref, c_ref, dtp_ref, dtpT_ref, bias_ref, biasT_ref, alog_ref, alogT_ref, dsk_ref, dskT_ref,
             st_ref, dy_ref, dxs_ref, dB_ref, dC_ref, ddtp_ref, dbias_ref, dalog_ref, dD_ref, dstate):
        ci = pl.program_id(1)

        @pl.when(ci == 0)
        def _():
            dstate[...] = jnp.zeros(dstate.shape, F32)

        dt, A, causal, row, col, acum, acumT, Bm, Cm, cb = _ssd_common(
            dtp_ref, dtpT_ref, bias_ref, biasT_ref, alog_ref, alogT_ref, b_ref, c_ref)
        tn = (((0,), (0,)), ((), ()))
        nt = (((1,), (1,)), ((), ()))
        hd = _ssd_heads(dskT_ref, acum, acumT, dt, Kh)
        ET, tail = hd["ET"], hd["tail"]
        cbT = lax.dot_general(Bm, Cm, nt, preferred_element_type=F32)
        causalT = row <= col
        xs = xs_ref[...].astype(F32)
        xdt_all = xs * hd["dt_all"]
        dyb = dy_ref[...]
        dy_all = dyb.astype(F32)
        S_all = st_ref[...]
        S_b = S_all.astype(BF16)
        dS_all = dstate[...]
        dS_b = dS_all.astype(BF16)
        CS_all = lax.dot_general(Cm, S_b, nt, preferred_element_type=F32)
        dyE_b = (dy_all * hd["ea_all"]).astype(BF16)
        dC_acc = jnp.dot(dyE_b, S_b, preferred_element_type=F32)
        dS_y = lax.dot_general(dyE_b, Cm, tn, preferred_element_type=F32)
        BdS_all = lax.dot_general(Bm, dS_b, nt, preferred_element_type=F32)
        dB_acc = jnp.dot((xdt_all * hd["tail_all"]).astype(BF16), dS_b, preferred_element_type=F32)
        dtail = _hi(xdt_all * BdS_all, ET)
        da_cols = _hi(dy_all * CS_all * hd["ea_all"], ET) - dtail * tail
        dss = _dot01_left(jnp.ones((8, N), F32), _dot01_left(hd["E"], dS_all * S_all, 2), 2, dims=nt)
        da_last = dss[0:1] * jnp.exp(hd["a_last"]) + jnp.sum(dtail * tail, axis=0, keepdims=True)
        rowi = lax.broadcasted_iota(jnp.int32, (Q, Kh), 0)
        da_cols = da_cols + jnp.where(rowi == Q - 1, da_last, 0.0)
        dstate[...] = hd["eL_b"] * dS_all + dS_y
        sum_mg = jnp.zeros((Q, Q), F32)
        ddt_x = jnp.zeros((Q, Kh), F32)
        da_rows = jnp.zeros((Kh, Q), F32)
        lane_k = lax.broadcasted_iota(jnp.int32, (Q, Kh), 1)
        sub_k = lax.broadcasted_iota(jnp.int32, (Kh, Q), 0)
        for ch in range(nch):
            cs = slice(ch * CW, (ch + 1) * CW)
            dyc = dyb[:, cs]
            xc_b = xdt_all[:, cs].astype(BF16)
            acc = hd["tail_all"][:, cs] * BdS_all[:, cs]
            for kk in range(hpc):
                k = ch * hpc + kk
                a_b = jnp.broadcast_to(acum[:, k:k + 1], (Q, Q))
                a_r = acumT[k:k + 1, :]
                decay = jnp.exp(jnp.where(causal, a_b - a_r, -jnp.inf))
                decayT = jnp.exp(jnp.where(causalT, a_r - a_b, -jnp.inf))
                dyk = dyc if hpc == 1 else jnp.where(_head_mask(Q, CW, kk), dyc, jnp.zeros_like(dyc))
                mg = decay * lax.dot_general(dyk, xc_b, nt, preferred_element_type=F32)
                sum_mg = sum_mg + mg
                w = mg * cb
                da_cols = da_cols + jnp.where(lane_k == k, jnp.sum(w, axis=1, keepdims=True), 0.0)
                da_rows = da_rows + jnp.where(sub_k == k, jnp.sum(w, axis=0, keepdims=True), 0.0)
                acc = acc + jnp.dot((decayT * cbT).astype(BF16), dyk, preferred_element_type=F32)
            dxs_ref[:, cs] = (acc * hd["dt_all"][:, cs] + dy_all[:, cs] * hd["dsk_all"][:, cs]).astype(BF16)
            ddt_x = ddt_x + _hi(acc * xs[:, cs], ET[cs, :])
        eye_q = (row == col).astype(F32)
        da_cols = da_cols - _dot01_left(eye_q, da_rows, 3, dims=nt)
        dD_row = jnp.sum(_hi(dy_all * xs, ET), axis=0, keepdims=True)
        sum_mg_b = sum_mg.astype(BF16)
        dB_ref[...] = (dB_acc + lax.dot_general(sum_mg_b, Cm, tn, preferred_element_type=F32)).astype(BF16)
        dC_ref[...] = (dC_acc + jnp.dot(sum_mg_b, Bm, preferred_element_type=F32)).astype(BF16)
        triu = (row <= col).astype(F32)
        ddtA = _dot01_left(triu, da_cols, 3)
        ddt = ddt_x + ddtA * A
        dpre = ddt * _sigmoid(dtp_ref[...] + bias_ref[...])
        ddtp_ref[...] = dpre
        dbias_v = jnp.sum(dpre, axis=0, keepdims=True)
        dalog_v = jnp.sum(ddtA * dt, axis=0, keepdims=True) * A

        @pl.when(ci == 0)
        def _():
            dbias_ref[...] = dbias_v
            dalog_ref[...] = dalog_v
            dD_ref[...] = dD_row

        @pl.when(ci > 0)
        def _():
            dbias_ref[...] += dbias_v
            dalog_ref[...] += dalog_v
            dD_ref[...] += dD_row

    vec_o = pl.BlockSpec((None, 1, Kh), lambda g, c: (g, 0, 0))
    return pl.pallas_call(
        body, grid=(G, nc),
        in_specs=_ssd_in_specs(Q, GP, N, Kh, DI, rev)
        + [pl.BlockSpec((None, None, GP, N), lambda g, c: (rev(c), g, 0, 0)),
           pl.BlockSpec((Q, GP), lambda g, c: (rev(c), g))],
        out_specs=[pl.BlockSpec((Q, GP), lambda g, c: (rev(c), g)), pl.BlockSpec((Q, N), lambda g, c: (rev(c), g)),
                   pl.BlockSpec((Q, N), lambda g, c: (rev(c), g)),
                   pl.BlockSpec((None, Q, Kh), lambda g, c: (g, rev(c), 0)), vec_o, vec_o, vec_o],
        out_shape=[jax.ShapeDtypeStruct((L, DI), BF16), jax.ShapeDtypeStruct((L, G * N), BF16),
                   jax.ShapeDtypeStruct((L, G * N), BF16), jax.ShapeDtypeStruct((G, L, Kh), F32)]
        + [jax.ShapeDtypeStruct((G, 1, Kh), F32)] * 3,
        scratch_shapes=[pltpu.VMEM((GP, N), F32)], compiler_params=_cp(("parallel", "arbitrary")),
        name="ssd_bwd")(xbc, xbc, xbc, dtp_g, dtp_gT, bias_g, bias_gT, alog_g, alog_gT, dsk_g, dsk_gT, states, dy)


def _rms_groups(y2, ng_ref, DI):
    S = DI // SSD_G
    for g in range(SSD_G):
        gs = slice(g * S, (g + 1) * S)
        seg = y2[:, gs]
        r = lax.rsqrt(jnp.mean(seg * seg, axis=-1, keepdims=True) + RMS_EPS)
        yield gs, seg * r, r, ng_ref[:, gs]


def rms_gate_fwd(y, zx, norm_g):
    L, DI = y.shape
    tr = _tile(L, 256, 16)

    def body(y_ref, z_ref, ng_ref, o_ref):
        y2 = y_ref[...].astype(F32) * _silu(z_ref[...].astype(F32))
        for gs, yh, _, ng in _rms_groups(y2, ng_ref, DI):
            o_ref[:, gs] = (yh * ng).astype(BF16)

    return pl.pallas_call(
        body, grid=(L // tr,), in_specs=_row_specs(tr, [DI, DI]) + [_vec_spec(DI)], out_specs=_row_specs(tr, [DI])[0],
        out_shape=jax.ShapeDtypeStruct((L, DI), BF16), compiler_params=_cp(("parallel",)),
        name="rms_gate_fwd")(y, zx, norm_g)


def rms_gate_bwd(dyn, y, zx, norm_g):
    L, DI = y.shape
    tr = _tile(L, 256, 16)

    def body(dyn_ref, y_ref, z_ref, ng_ref, dy_ref, dz_ref, dng_ref):
        i = pl.program_id(0)
        yv = y_ref[...].astype(F32)
        zv = z_ref[...].astype(F32)
        sz = _silu(zv)
        dsz = _dsilu(zv)
        dynv = dyn_ref[...].astype(F32)
        for gs, yh, r, ng in _rms_groups(yv * sz, ng_ref, DI):
            dyh = dynv[:, gs] * ng
            dy2 = r * (dyh - yh * jnp.mean(dyh * yh, axis=-1, keepdims=True))
            dy_ref[:, gs] = (dy2 * sz[:, gs]).astype(BF16)
            dz_ref[:, gs] = (dy2 * yv[:, gs] * dsz[:, gs]).astype(BF16)
            s = jnp.sum(dynv[:, gs] * yh, axis=0, keepdims=True)

            @pl.when(i == 0)
            def _():
                dng_ref[:, gs] = s

            @pl.when(i > 0)
            def _():
                dng_ref[:, gs] += s

    return pl.pallas_call(
        body, grid=(L // tr,), in_specs=_row_specs(tr, [DI, DI, DI]) + [_vec_spec(DI)],
        out_specs=_row_specs(tr, [DI, DI]) + [_vec_spec(DI)],
        out_shape=[jax.ShapeDtypeStruct((L, DI), BF16)] * 2 + [jax.ShapeDtypeStruct((1, DI), F32)],
        compiler_params=_cp(("arbitrary",)), name="rms_gate_bwd")(dyn, y, zx, norm_g)


def _alibi_slope(gi, h):
    n = len(DIL_PATTERNS) * DIL_H
    return float(2.0 ** (-8.0 * (gi * DIL_H + h + 1) / n))


def _attn_masks():
    qi = lax.broadcasted_iota(jnp.int32, (DIL_BLK, DIL_BLK), 0)
    kj = lax.broadcasted_iota(jnp.int32, (DIL_BLK, DIL_BLK), 1)
    dcur = (qi - kj).astype(F32)
    return dcur, qi >= kj, dcur + float(DIL_BLK), kj >= qi


def _dil_cols(arr, col0, d):
    HW = DIL_H * DIL_E
    if d == 1:
        return arr, arr.shape[1] // HW, col0 // HW
    return arr[:, col0:col0 + HW].reshape(arr.shape[0] // d, d * HW), 1, 0


def attn_fwd(qz, kv, gi):
    window, d = DIL_PATTERNS[gi]
    assert window // d == DIL_BLK
    L, QZ = qz.shape
    KV = kv.shape[1]
    HW = DIL_H * DIL_E
    M = L // d
    nb = M // DIL_BLK
    nq, nkv = QZ // HW, KV // HW
    scale = DIL_E ** -0.5
    nt = (((1,), (1,)), ((), ()))

    def body(q_ref, kp_ref, kc_ref, vp_ref, vc_ref, o_ref, lse_ref):
        n = pl.program_id(1)
        dcur, vcur, dprev, vprev0 = _attn_masks()
        vprev = jnp.logical_and(vprev0, n > 0)
        lane = lax.broadcasted_iota(jnp.int32, (DIL_BLK, 128), 1)
        lse_acc = jnp.zeros((DIL_BLK, 128), F32)
        for h in range(DIL_H):
            hs = slice(h * DIL_E, (h + 1) * DIL_E)
            sl = _alibi_slope(gi, h) * d
            q = q_ref[:, hs]
            s_c = lax.dot_general(q, kc_ref[:, hs], nt, preferred_element_type=F32) * scale - sl * dcur
            s_p = lax.dot_general(q, kp_ref[:, hs], nt, preferred_element_type=F32) * scale - sl * dprev
            s_c = jnp.where(vcur, s_c, -jnp.inf)
            s_p = jnp.where(vprev, s_p, -jnp.inf)
            m = jnp.maximum(jnp.max(s_c, axis=-1, keepdims=True), jnp.max(s_p, axis=-1, keepdims=True))
            p_c = jnp.exp(s_c - m)
            p_p = jnp.exp(s_p - m)
            den = jnp.sum(p_c, axis=-1, keepdims=True) + jnp.sum(p_p, axis=-1, keepdims=True)
            o = (jnp.dot(p_c.astype(BF16), vc_ref[:, hs], preferred_element_type=F32)
                 + jnp.dot(p_p.astype(BF16), vp_ref[:, hs], preferred_element_type=F32)) / den
            o_ref[:, hs] = o.astype(BF16)
            lse_acc = jnp.where(lane == h, m + jnp.log(den), lse_acc)
        lse_ref[...] = lse_acc

    blk = (DIL_BLK, HW)
    prev = lambda n: jnp.maximum(n - 1, 0)
    qv, qn, qo = _dil_cols(qz, gi * HW, d)
    kv_, kn, ko = _dil_cols(kv, gi * HW, d)
    vv, vn, vo = _dil_cols(kv, (nkv // 2 + gi) * HW, d)
    o, lse = pl.pallas_call(
        body, grid=(d, nb),
        in_specs=[pl.BlockSpec(blk, lambda r, n: (n, r * qn + qo)),
                  pl.BlockSpec(blk, lambda r, n: (prev(n), r * kn + ko)),
                  pl.BlockSpec(blk, lambda r, n: (n, r * kn + ko)),
                  pl.BlockSpec(blk, lambda r, n: (prev(n), r * vn + vo)),
                  pl.BlockSpec(blk, lambda r, n: (n, r * vn + vo))],
        out_specs=[pl.BlockSpec(blk, lambda r, n: (n, r)), pl.BlockSpec((DIL_BLK, 128), lambda r, n: (n, r))],
        out_shape=[jax.ShapeDtypeStruct((M, d * HW), BF16), jax.ShapeDtypeStruct((M, d * 128), F32)],
        compiler_params=_cp(("parallel", "parallel")), name=f"attn_fwd_{gi}")(qv, kv_, kv_, vv, vv)
    return o.reshape(L, HW), lse.reshape(L, 128)


def attn_bwd(qz, kv, do, lse, dpr, gi):
    window, d = DIL_PATTERNS[gi]
    L, QZ = qz.shape
    KV = kv.shape[1]
    HW = DIL_H * DIL_E
    M = L // d
    nb = M // DIL_BLK
    nq, nkv = QZ // HW, KV // HW
    scale = DIL_E ** -0.5
    nt = (((1,), (1,)), ((), ()))
    tn = (((0,), (0,)), ((), ()))

    def body(q0_ref, q1_ref, k_ref, v_ref, do0_ref, do1_ref, l0_ref, l1_ref, r0_ref, r1_ref,
             dq_ref, dk_ref, dv_ref, carry):
        n = pl.program_id(1)

        @pl.when(n == 0)
        def _():
            carry[...] = jnp.zeros(carry.shape, F32)

        dcur, vcur, dprev, vprev0 = _attn_masks()
        vprev = jnp.logical_and(vprev0, n < nb - 1)
        for h in range(DIL_H):
            hs = slice(h * DIL_E, (h + 1) * DIL_E)
            sl = _alibi_slope(gi, h) * d
            kh = k_ref[:, hs]
            vh = v_ref[:, hs]
            q0, q1 = q0_ref[:, hs], q1_ref[:, hs]
            do0, do1 = do0_ref[:, hs], do1_ref[:, hs]
            s0 = lax.dot_general(q0, kh, nt, preferred_element_type=F32) * scale - sl * dcur
            p0 = jnp.exp(jnp.where(vcur, s0 - l0_ref[:, h:h + 1], -jnp.inf))
            ds0 = p0 * (lax.dot_general(do0, vh, nt, preferred_element_type=F32) - r0_ref[:, h:h + 1])
            s1 = lax.dot_general(q1, kh, nt, preferred_element_type=F32) * scale - sl * dprev
            p1 = jnp.exp(jnp.where(vprev, s1 - l1_ref[:, h:h + 1], -jnp.inf))
            ds1 = p1 * (lax.dot_general(do1, vh, nt, preferred_element_type=F32) - r1_ref[:, h:h + 1])
            ds0_b = (ds0 * scale).astype(BF16)
            ds1_b = (ds1 * scale).astype(BF16)
            dv = (lax.dot_general(p0.astype(BF16), do0, tn, preferred_element_type=F32)
                  + lax.dot_general(p1.astype(BF16), do1, tn, preferred_element_type=F32))
            dk = (lax.dot_general(ds0_b, q0, tn, preferred_element_type=F32)
                  + lax.dot_general(ds1_b, q1, tn, preferred_element_type=F32))
            dv_ref[:, hs] = dv.astype(BF16)
            dk_ref[:, hs] = dk.astype(BF16)
            dq_ref[:, hs] = (carry[:, hs] + jnp.dot(ds0_b, kh, preferred_element_type=F32)).astype(BF16)
            carry[:, hs] = jnp.dot(ds1_b, kh, preferred_element_type=F32)

    blk = (DIL_BLK, HW)
    sblk = (DIL_BLK, 128)
    nxt = lambda n: jnp.minimum(n + 1, nb - 1)
    qv, qn, qo = _dil_cols(qz, gi * HW, d)
    kv_, kn, ko = _dil_cols(kv, gi * HW, d)
    vv, vn, vo = _dil_cols(kv, (nkv // 2 + gi) * HW, d)
    dov = do.reshape(M, d * HW)
    lv = lse.reshape(M, d * 128)
    rv = dpr.reshape(M, d * 128)
    outs = pl.pallas_call(
        body, grid=(d, nb),
        in_specs=[pl.BlockSpec(blk, lambda r, n: (n, r * qn + qo)), pl.BlockSpec(blk, lambda r, n: (nxt(n), r * qn + qo)),
                  pl.BlockSpec(blk, lambda r, n: (n, r * kn + ko)),
                  pl.BlockSpec(blk, lambda r, n: (n, r * vn + vo)),
                  pl.BlockSpec(blk, lambda r, n: (n, r)), pl.BlockSpec(blk, lambda r, n: (nxt(n), r)),
                  pl.BlockSpec(sblk, lambda r, n: (n, r)), pl.BlockSpec(sblk, lambda r, n: (nxt(n), r)),
                  pl.BlockSpec(sblk, lambda r, n: (n, r)), pl.BlockSpec(sblk, lambda r, n: (nxt(n), r))],
        out_specs=[pl.BlockSpec(blk, lambda r, n: (n, r))] * 3,
        out_shape=[jax.ShapeDtypeStruct((M, d * HW), BF16)] * 3,
        scratch_shapes=[pltpu.VMEM(blk, F32)], compiler_params=_cp(("parallel", "arbitrary")),
        name=f"attn_bwd_{gi}")(qv, qv, kv_, vv, dov, dov, lv, lv, rv, rv)
    return [t.reshape(L, HW) for t in outs]


def _merge_weights(l_refs, h):
    ls = [r[:, h:h + 1] for r in l_refs]
    mx = functools.reduce(jnp.maximum, ls)
    es = [jnp.exp(l - mx) for l in ls]
    den = functools.reduce(lambda a, b: a + b, es)
    return [e / den for e in es]


def merge_fwd(os_, lses, qz):
    L, HW = os_[0].shape
    tr = _tile(L, 256, 16)
    ng = len(os_)
    zblk = qz.shape[1] // HW - 1

    def body(*refs):
        o_refs, l_refs, z_ref, out_ref = refs[:ng], refs[ng:2 * ng], refs[2 * ng], refs[2 * ng + 1]
        for h in range(DIL_H):
            hs = slice(h * DIL_E, (h + 1) * DIL_E)
            ws = _merge_weights(l_refs, h)
            om = functools.reduce(lambda a, b: a + b, [w * o[:, hs].astype(F32) for w, o in zip(ws, o_refs)])
            out_ref[:, hs] = (om * _silu(z_ref[:, hs].astype(F32))).astype(BF16)

    return pl.pallas_call(
        body, grid=(L // tr,),
        in_specs=_row_specs(tr, [HW] * ng + [128] * ng) + [pl.BlockSpec((tr, HW), lambda i: (i, zblk))],
        out_specs=_row_specs(tr, [HW])[0], out_shape=jax.ShapeDtypeStruct((L, HW), BF16),
        compiler_params=_cp(("parallel",)), name="merge_fwd")(*os_, *lses, qz)


def merge_bwd(dgated, os_, lses, qz):
    L, HW = os_[0].shape
    tr = _tile(L, 256, 16)
    ng = len(os_)
    zblk = qz.shape[1] // HW - 1

    def body(*refs):
        dg_ref = refs[0]
        o_refs, l_refs, z_ref = refs[1:1 + ng], refs[1 + ng:1 + 2 * ng], refs[1 + 2 * ng]
        outs = refs[2 + 2 * ng:]
        do_refs, dpr_refs, dz_ref = outs[:ng], outs[ng:2 * ng], outs[2 * ng]
        lane = lax.broadcasted_iota(jnp.int32, (tr, 128), 1)
        accs = [jnp.zeros((tr, 128), F32) for _ in range(ng)]
        for h in range(DIL_H):
            hs = slice(h * DIL_E, (h + 1) * DIL_E)
            ws = _merge_weights(l_refs, h)
            ov = [o[:, hs].astype(F32) for o in o_refs]
            om = functools.reduce(lambda a, b: a + b, [w * o for w, o in zip(ws, ov)])
            zv = z_ref[:, hs].astype(F32)
            dgv = dg_ref[:, hs].astype(F32)
            dom = dgv * _silu(zv)
            dz_ref[:, hs] = (dgv * om * _dsilu(zv)).astype(BF16)
            dws = [jnp.sum(dom * o, axis=-1, keepdims=True) for o in ov]
            dwbar = functools.reduce(lambda a, b: a + b, [w * dw for w, dw in zip(ws, dws)])
            for g in range(ng):
                do_refs[g][:, hs] = (ws[g] * dom).astype(BF16)
                accs[g] = jnp.where(lane == h, ws[g] * dwbar, accs[g])
        for g in range(ng):
            dpr_refs[g][...] = accs[g]

    outs = pl.pallas_call(
        body, grid=(L // tr,),
        in_specs=_row_specs(tr, [HW] * (1 + ng) + [128] * ng) + [pl.BlockSpec((tr, HW), lambda i: (i, zblk))],
        out_specs=_row_specs(tr, [HW] * ng + [128] * ng + [HW]),
        out_shape=[jax.ShapeDtypeStruct((L, HW), BF16)] * ng + [jax.ShapeDtypeStruct((L, 128), F32)] * ng
        + [jax.ShapeDtypeStruct((L, HW), BF16)],
        compiler_params=_cp(("parallel",)), name="merge_bwd")(dgated, *os_, *lses, qz)
    return outs[:ng], outs[ng:2 * ng], outs[2 * ng]


def ada_fwd(c8, ada_w):
    nl, D, Ws = ada_w.shape
    tn = _tile(Ws, 512)

    def body(c_ref, w_ref, o_ref):
        o_ref[...] = jnp.dot(_silu(c_ref[...]), w_ref[...], precision=lax.Precision.HIGHEST,
                             preferred_element_type=F32)

    return pl.pallas_call(
        body, grid=(nl, Ws // tn),
        in_specs=[pl.BlockSpec((N_DEV, D), lambda l, j: (0, 0)), pl.BlockSpec((None, D, tn), lambda l, j: (l, 0, j))],
        out_specs=pl.BlockSpec((None, N_DEV, tn), lambda l, j: (l, 0, j)),
        out_shape=jax.ShapeDtypeStruct((nl, N_DEV, Ws), F32), compiler_params=_cp(("parallel", "parallel")),
        name="ada_fwd")(c8, ada_w)


def ada_wgrad(c8t, dmod):
    nl, _, Ws = dmod.shape
    D = c8t.shape[0]
    tm = _tile(D, 512, 8)

    def body(c_ref, d_ref, o_ref):
        sc = _silu(c_ref[...])
        acc = sc[:, 0:1] * d_ref[0:1, :]
        for e in range(1, N_DEV):
            acc = acc + sc[:, e:e + 1] * d_ref[e:e + 1, :]
        o_ref[...] = acc

    return pl.pallas_call(
        body, grid=(nl, D // tm),
        in_specs=[pl.BlockSpec((tm, N_DEV), lambda l, i: (i, 0)), pl.BlockSpec((None, N_DEV, Ws), lambda l, i: (l, 0, 0))],
        out_specs=pl.BlockSpec((None, tm, Ws), lambda l, i: (l, i, 0)),
        out_shape=jax.ShapeDtypeStruct((nl, D, Ws), F32), compiler_params=_cp(("parallel", "parallel")),
        name="ada_wgrad")(c8t, dmod)


def adamw(w, g, m, v, name):
    R, C = w.shape
    tr = _tile(R, 256, 8)
    c1 = 1.0 - ADAM_B1 ** ADAM_STEP
    c2 = 1.0 - ADAM_B2 ** ADAM_STEP

    def body(w_ref, g_ref, m_ref, v_ref, d_ref, nm_ref, nv_ref):
        gv = g_ref[...]
        nm = ADAM_B1 * m_ref[...] + (1.0 - ADAM_B1) * gv
        nv = ADAM_B2 * v_ref[...] + (1.0 - ADAM_B2) * (gv * gv)
        nm_ref[...] = nm
        nv_ref[...] = nv
        d_ref[...] = -ADAM_LR * ((nm / c1) / (jnp.sqrt(nv / c2) + ADAM_EPS) + ADAM_WD * w_ref[...])

    return pl.pallas_call(
        body, grid=(R // tr,), in_specs=_row_specs(tr, [C] * 4), out_specs=_row_specs(tr, [C] * 3),
        out_shape=[jax.ShapeDtypeStruct((R, C), F32)] * 3, compiler_params=_cp(("parallel",)), name=name)(w, g, m, v)


def sum_leading(t, name, out_dtype=F32):
    S, R, C = t.shape
    tr = _tile(R, 256, 16)

    def body(t_ref, o_ref):
        acc = t_ref[0].astype(F32)
        for s in range(1, S):
            acc = acc + t_ref[s].astype(F32)
        o_ref[...] = acc.astype(out_dtype)

    return pl.pallas_call(
        body, grid=(R // tr,), in_specs=[pl.BlockSpec((S, tr, C), lambda i: (0, i, 0))],
        out_specs=pl.BlockSpec((tr, C), lambda i: (i, 0)), out_shape=jax.ShapeDtypeStruct((R, C), out_dtype),
        compiler_params=_cp(("parallel",)), name=name)(t)


def add_half(g, a, core, name):
    S, R, C = g.shape
    h = R // 2
    tr = _tile(h, 256, 16)
    nb = h // tr

    def body(core_ref, g_ref, a_ref, o_ref):
        o_ref[...] = (g_ref[...].astype(F32) + a_ref[...].astype(F32)).astype(BF16)

    return pl.pallas_call(
        body,
        grid_spec=pltpu.PrefetchScalarGridSpec(
            num_scalar_prefetch=1, grid=(S, nb),
            in_specs=[pl.BlockSpec((None, tr, C), lambda s, i, core_ref: (s, core_ref[0] * nb + i, 0)),
                      pl.BlockSpec((None, tr, C), lambda s, i, core_ref: (s, i, 0))],
            out_specs=pl.BlockSpec((None, tr, C), lambda s, i, core_ref: (s, i, 0))),
        out_shape=jax.ShapeDtypeStruct((S, h, C), BF16), compiler_params=_cp(("parallel", "parallel")),
        name=name)(core, g, a)


def sum_partials(own, landed, chip, name):
    _, h, C = own.shape
    tr = _tile(h, 256, 16)

    def body(chip_ref, own_ref, l_ref, o_ref):
        acc = own_ref[...].astype(F32)
        for j in range(3):
            acc = acc + l_ref[j].astype(F32)
        o_ref[...] = acc

    return pl.pallas_call(
        body,
        grid_spec=pltpu.PrefetchScalarGridSpec(
            num_scalar_prefetch=1, grid=(h // tr,),
            in_specs=[pl.BlockSpec((None, tr, C), lambda i, chip_ref: (chip_ref[0], i, 0)),
                      pl.BlockSpec((3, tr, C), lambda i, chip_ref: (0, i, 0))],
            out_specs=pl.BlockSpec((tr, C), lambda i, chip_ref: (i, 0))),
        out_shape=jax.ShapeDtypeStruct((h, C), F32), compiler_params=_cp(("parallel",)), name=name)(chip, own, landed)


def adamw_halves(w, g_mine, g_theirs, m, v, core, name):
    R, C = w.shape
    h = R // 2
    tr = _tile(h, 256, 8)
    nbh = h // tr
    c1 = 1.0 - ADAM_B1 ** ADAM_STEP
    c2 = 1.0 - ADAM_B2 ** ADAM_STEP

    def body(core_ref, w_ref, gm_ref, gt_ref, m_ref, v_ref, g_ref, d_ref, nm_ref, nv_ref):
        mine = (pl.program_id(0) // nbh) == core_ref[0]
        gv = jnp.where(mine, gm_ref[...], gt_ref[...])
        g_ref[...] = gv
        nm = ADAM_B1 * m_ref[...] + (1.0 - ADAM_B1) * gv
        nv = ADAM_B2 * v_ref[...] + (1.0 - ADAM_B2) * (gv * gv)
        nm_ref[...] = nm
        nv_ref[...] = nv
        d_ref[...] = -ADAM_LR * ((nm / c1) / (jnp.sqrt(nv / c2) + ADAM_EPS) + ADAM_WD * w_ref[...])

    full = pl.BlockSpec((tr, C), lambda i, core_ref: (i, 0))
    halfspec = pl.BlockSpec((tr, C), lambda i, core_ref: (i % nbh, 0))
    return pl.pallas_call(
        body,
        grid_spec=pltpu.PrefetchScalarGridSpec(
            num_scalar_prefetch=1, grid=(2 * nbh,), in_specs=[full, halfspec, halfspec, full, full],
            out_specs=[full] * 4),
        out_shape=[jax.ShapeDtypeStruct((R, C), F32)] * 4, compiler_params=_cp(("parallel",)),
        name=name)(core, w, g_mine, g_theirs, m, v)


_ANY = pl.BlockSpec(memory_space=pl.ANY)


def _place():
    x, y, c = lax.axis_index("x"), lax.axis_index("y"), lax.axis_index("c")
    chips = [(1 - x, y), (x, 1 - y), (1 - x, 1 - y)]
    return x, y, c, chips


def allgather_small(v, name, after=None):
    R, W = v.shape
    extra = [] if after is None else [after]

    def body(x_ref, *rest):
        out_ref, send_sems, recv_sems, local_sem = rest[len(extra):]
        x, y, c, chips = _place()
        me, sibling = (x, y, c), (x, y, 1 - c)

        def rows(px, py, pc):
            return out_ref.at[pl.ds((4 * px + 2 * py + pc) * R, R), :]

        def copy(k, block, to, src=None):
            return pltpu.make_async_remote_copy(
                src_ref=rows(*block) if src is None else src, dst_ref=rows(*block),
                send_sem=send_sems.at[k], recv_sem=recv_sems.at[k], device_id=to, device_id_type=MESH)

        mine = pltpu.make_async_copy(x_ref, rows(*me), local_sem)
        mine.start()
        first = [copy(0, me, sibling, src=x_ref)]
        first += [copy(1 + j, me, (*chip, c), src=x_ref) for j, chip in enumerate(chips)]
        for cp in first:
            cp.start()
        passed = [copy(4 + j, (*chip, c), sibling) for j, chip in enumerate(chips)]
        for j, chip in enumerate(chips):
            copy(1 + j, (*chip, c), me).wait_recv()
            passed[j].start()
        copy(0, sibling, me).wait_recv()
        for j, chip in enumerate(chips):
            copy(4 + j, (*chip, 1 - c), me).wait_recv()
        for cp in first + passed:
            cp.wait_send()
        mine.wait()

    return pl.pallas_call(
        body, out_shape=jax.ShapeDtypeStruct((N_DEV * R, W), v.dtype),
        in_specs=[pl.BlockSpec(memory_space=pltpu.VMEM)] + [_ANY] * len(extra),
        out_specs=pl.BlockSpec(memory_space=pltpu.VMEM),
        scratch_shapes=[pltpu.SemaphoreType.DMA((7,)), pltpu.SemaphoreType.DMA((7,)), pltpu.SemaphoreType.DMA],
        name=name)(v, *extra)


def allgather_weights(shards, name="allgather_weights"):
    n = len(shards)

    def body(*refs):
        ins, outs = refs[:n], refs[n:2 * n]
        send_sems, recv_sems = refs[2 * n:]
        x, y, c, chips = _place()
        p = 2 * x + y
        sibling = (x, y, 1 - c)

        def half(i, chip_id, core, ref=None):
            r = outs[i].at[chip_id] if ref is None else ref
            return r.at[core]

        def copy(i, k, chip_id, core, to, src=None):
            return pltpu.make_async_remote_copy(
                src_ref=half(i, chip_id, core) if src is None else src, dst_ref=half(i, chip_id, core),
                send_sem=send_sems.at[6 * i + k], recv_sem=recv_sems.at[6 * i + k], device_id=to, device_id_type=MESH)

        first = [copy(i, j, p, c, (*chip, c), src=half(i, p, c, ref=ins[i]))
                 for i in range(n) for j, chip in enumerate(chips)]
        for cp in first:
            cp.start()
        passed = []
        for i in range(n):
            for j, (cx, cy) in enumerate(chips):
                copy(i, j, 2 * cx + cy, c, sibling).wait_recv()
                fw = copy(i, 3 + j, 2 * cx + cy, c, sibling)
                fw.start()
                passed.append(fw)
        for i in range(n):
            for j, (cx, cy) in enumerate(chips):
                copy(i, 3 + j, 2 * cx + cy, 1 - c, sibling).wait_recv()
        for cp in first + passed:
            cp.wait_send()

    split = [s.reshape(2, s.shape[0] // 2, s.shape[1]) for s in shards]
    outs = pl.pallas_call(
        body, out_shape=[jax.ShapeDtypeStruct((N_CHIPS,) + s.shape, s.dtype) for s in split],
        in_specs=[_ANY] * n, out_specs=[_ANY] * n,
        scratch_shapes=[pltpu.SemaphoreType.DMA((6 * n,)), pltpu.SemaphoreType.DMA((6 * n,))],
        name=name)(*split)
    chip = 2 * lax.axis_index("x") + lax.axis_index("y")
    return [lax.dynamic_update_index_in_dim(o, s, chip, 0).reshape((N_CHIPS,) + sh.shape)
            for o, s, sh in zip(outs, split, shards)]


_HBM = pl.BlockSpec(memory_space=pltpu.HBM)
_SEM = pl.BlockSpec(memory_space=pltpu.SEMAPHORE)
_EFFECT = pltpu.SideEffectType.DATAFLOW_SIDE_EFFECTING


def _chip_copies(kind, srcs, lands, send_sems, recv_sems):
    x, y, c, chips = _place()
    p = 2 * x + y
    cps = []
    for i in range(len(srcs)):
        for j, (cx, cy) in enumerate(chips):
            if kind == "gather":
                src, dst = srcs[i].at[c], lands[i].at[p, c]
            else:
                src, dst = srcs[i].at[2 * cx + cy], lands[i].at[j]
            cps.append(pltpu.make_async_remote_copy(
                src_ref=src, dst_ref=dst, send_sem=send_sems.at[3 * i + j], recv_sem=recv_sems.at[3 * i + j],
                device_id=(cx, cy, c), device_id_type=MESH))
    return cps


def split_start(kind, srcs, land_shapes, after, name):
    n = len(srcs)

    def body(*refs):
        src_refs, land_refs = refs[:n], refs[n:2 * n]
        send_sems, recv_sems = refs[2 * n + 1], refs[2 * n + 2]
        token = refs[-1]
        for cp in _chip_copies(kind, src_refs, land_refs, send_sems, recv_sems):
            cp.start()
        token[...] = jnp.zeros_like(token)

    lands = [pltpu.with_memory_space_constraint(lax.empty(s, BF16), pltpu.HBM) for s in land_shapes]
    outs = pl.pallas_call(
        body, name=name,
        out_shape=(pltpu.SemaphoreType.DMA((3 * n,)), pltpu.SemaphoreType.DMA((3 * n,)),
                   *[pltpu.HBM(s.shape, s.dtype) for s in srcs], *[pltpu.HBM(s, BF16) for s in land_shapes],
                   jax.ShapeDtypeStruct((8, 128), F32)),
        in_specs=[_HBM] * (2 * n) + [_ANY],
        out_specs=(_SEM, _SEM, *([_HBM] * (2 * n)), pl.BlockSpec(memory_space=pltpu.VMEM)),
        input_output_aliases={i: 2 + i for i in range(2 * n)},
        compiler_params=pltpu.CompilerParams(has_side_effects=_EFFECT),
    )(*[pltpu.with_memory_space_constraint(s, pltpu.HBM) for s in srcs], *lands, after)
    return outs[0], outs[1], outs[2:2 + n], outs[2 + n:2 + 2 * n], outs[-1]


def split_wait(kind, send_sems, recv_sems, srcs, lands, after, name):
    n = len(srcs)

    def body(*refs):
        src_refs, land_refs = refs[:n], refs[n:2 * n]
        ssem, rsem = refs[2 * n], refs[2 * n + 1]
        for cp in _chip_copies(kind, src_refs, land_refs, ssem, rsem):
            cp.wait_send()
            cp.wait_recv()

    outs = pl.pallas_call(
        body, name=name,
        out_shape=[pltpu.HBM(s.shape, s.dtype) for s in srcs] + [pltpu.HBM(s.shape, s.dtype) for s in lands],
        in_specs=[_HBM] * (2 * n) + [_SEM, _SEM, _ANY], out_specs=[_HBM] * (2 * n),
        input_output_aliases={i: i for i in range(2 * n)},
        compiler_params=pltpu.CompilerParams(has_side_effects=_EFFECT),
    )(*srcs, *lands, send_sems, recv_sems, after)
    return outs[:n], outs[n:]


def pass_to_sibling(lands):
    n = len(lands)

    def body(*refs):
        ins, outs = refs[:n], refs[n:2 * n]
        send_sems, recv_sems = refs[2 * n:]
        x, y, c, chips = _place()
        cps = []
        for i in range(n):
            for j, (cx, cy) in enumerate(chips):
                blk = outs[i].at[2 * cx + cy, c]
                cps.append(pltpu.make_async_remote_copy(
                    src_ref=ins[i].at[2 * cx + cy, c], dst_ref=blk, send_sem=send_sems.at[3 * i + j],
                    recv_sem=recv_sems.at[3 * i + j], device_id=(x, y, 1 - c), device_id_type=MESH))
        for cp in cps:
            cp.start()
        for cp in cps:
            cp.wait()

    return pl.pallas_call(
        body, out_shape=[jax.ShapeDtypeStruct(t.shape, t.dtype) for t in lands], in_specs=[_ANY] * n,
        out_specs=[_ANY] * n, input_output_aliases={i: i for i in range(n)},
        scratch_shapes=[pltpu.SemaphoreType.DMA((3 * n,)), pltpu.SemaphoreType.DMA((3 * n,))],
        name="ag_pass_to_sibling")(*lands)


def exchange_halves_to_sibling(gs, name):
    n = len(gs)

    def body(*refs):
        ins, outs = refs[:n], refs[n:2 * n]
        send_sems, recv_sems = refs[2 * n:]
        x, y, c, _ = _place()
        cps = []
        for i in range(n):
            h = ins[i].shape[1] // 2
            cps.append(pltpu.make_async_remote_copy(
                src_ref=ins[i].at[:, pl.ds((1 - c) * h, h), :], dst_ref=outs[i],
                send_sem=send_sems.at[i], recv_sem=recv_sems.at[i], device_id=(x, y, 1 - c), device_id_type=MESH))
        for cp in cps:
            cp.start()
        for cp in cps:
            cp.wait()

    return pl.pallas_call(
        body, out_shape=[jax.ShapeDtypeStruct((g.shape[0], g.shape[1] // 2, g.shape[2]), g.dtype) for g in gs],
        in_specs=[_ANY] * n, out_specs=[_ANY] * n,
        scratch_shapes=[pltpu.SemaphoreType.DMA((n,)), pltpu.SemaphoreType.DMA((n,))],
        name=name)(*gs)


def scatter_to_chips(ps, name):
    n = len(ps)

    def body(*refs):
        ins, outs = refs[:n], refs[n:2 * n]
        send_sems, recv_sems = refs[2 * n:]
        x, y, c, chips = _place()
        cps = []
        for i in range(n):
            for j, (cx, cy) in enumerate(chips):
                cps.append(pltpu.make_async_remote_copy(
                    src_ref=ins[i].at[2 * cx + cy], dst_ref=outs[i].at[j], send_sem=send_sems.at[3 * i + j],
                    recv_sem=recv_sems.at[3 * i + j], device_id=(cx, cy, c), device_id_type=MESH))
        for cp in cps:
            cp.start()
        for cp in cps:
            cp.wait()

    return pl.pallas_call(
        body, out_shape=[jax.ShapeDtypeStruct((3,) + t.shape[1:], t.dtype) for t in ps],
        in_specs=[_ANY] * n, out_specs=[_ANY] * n,
        scratch_shapes=[pltpu.SemaphoreType.DMA((3 * n,)), pltpu.SemaphoreType.DMA((3 * n,))],
        name=name)(*ps)


def join_halves(rs, name):
    n = len(rs)

    def body(*refs):
        ins, outs = refs[:n], refs[n:2 * n]
        send_sems, recv_sems = refs[2 * n:]
        x, y, c, _ = _place()
        cps = [pltpu.make_async_remote_copy(
            src_ref=ins[i], dst_ref=outs[i], send_sem=send_sems.at[i], recv_sem=recv_sems.at[i],
            device_id=(x, y, 1 - c), device_id_type=MESH) for i in range(n)]
        for cp in cps:
            cp.start()
        for cp in cps:
            cp.wait()

    return pl.pallas_call(
        body, out_shape=[jax.ShapeDtypeStruct(r.shape, r.dtype) for r in rs],
        in_specs=[_ANY] * n, out_specs=[_ANY] * n,
        scratch_shapes=[pltpu.SemaphoreType.DMA((n,)), pltpu.SemaphoreType.DMA((n,))],
        name=name)(*rs)


def _pack(parts, row_mult=8):
    flat = jnp.concatenate([p.reshape(-1).astype(F32) for p in parts])
    unit = row_mult * 128
    n = -(-flat.shape[0] // unit) * unit
    return jnp.pad(flat, (0, n - flat.shape[0])).reshape(n // 128, 128)


def _unpack(flat, shapes):
    out, off = [], 0
    for s in shapes:
        n = int(np.prod(s))
        out.append(flat[off:off + n].reshape(s))
        off += n
    return out


def _gather_packed(parts, name):
    packed = _pack(parts)
    g = allgather_small(packed, name).reshape(N_DEV, -1)
    return _unpack_rows(g, [p.shape for p in parts])


def _unpack_rows(g, shapes):
    out, off = [], 0
    for s in shapes:
        n = int(np.prod(s))
        out.append(g[:, off:off + n].reshape((g.shape[0],) + tuple(s)))
        off += n
    return out


def _by_chip(t, axis):
    return jnp.concatenate([t[2 * p] for p in range(N_CHIPS)], axis=axis)


def kernel(x, c, ada_w, ada_b, ln_g, ln_b, a_in_w, a_conv_w, a_conv_b, a_dt_bias, a_A_log, a_D, a_norm_g, a_out_w, kv_w, b_in_w, b_out_w, loss_target, m_ada_w, m_ada_b, m_ln_g, m_ln_b, m_a_in_w, m_a_conv_w, m_a_conv_b, m_a_dt_bias, m_a_A_log, m_a_D, m_a_norm_g, m_a_out_w, m_kv_w, m_b_in_w, m_b_out_w, v_ada_w, v_ada_b, v_ln_g, v_ln_b, v_a_in_w, v_a_conv_w, v_a_conv_b, v_a_dt_bias, v_a_A_log, v_a_D, v_a_norm_g, v_a_out_w, v_kv_w, v_b_in_w, v_b_out_w):
    ax, ay, ac = lax.axis_index("x"), lax.axis_index("y"), lax.axis_index("c")
    chip = 2 * ax + ay
    dev = 4 * ax + 2 * ay + ac
    xin = x[0]
    tgt = loss_target[0]
    L, D = xin.shape
    G, P = SSD_G, SSD_P
    H = a_dt_bias.shape[1]
    Kh = H // G
    DI = H * P
    CONVD = a_conv_b.shape[1] * N_CHIPS
    HW = DIL_H * DIL_E
    Ws = ada_w.shape[2]

    (w_in_g,) = allgather_weights([a_in_w[0].astype(BF16)], "allgather_w_in")
    later = [a_out_w[0].astype(BF16), kv_w.astype(BF16), b_in_w[0].astype(BF16), b_out_w[0].astype(BF16)]
    later_split = [s.reshape(2, s.shape[0] // 2, s.shape[1]) for s in later]
    ag_ssem, ag_rsem, ag_srcs, ag_lands, ag_token = split_start(
        "gather", later_split, [(N_CHIPS,) + s.shape for s in later_split], w_in_g, "ag_later_start")
    w_in = jnp.transpose(w_in_g, (1, 0, 2)).reshape(D, -1)
    w_zx = w_in[:, :DI + CONVD]
    w_dt = jnp.pad(w_in[:, DI + CONVD:], ((0, 0), (0, 128 - H)))

    c8, cw8, cb8, ng8 = _gather_packed([c[0], a_conv_w[0], a_conv_b[0], a_norm_g[0]], "allgather_small_params")
    conv_w = _by_chip(cw8, 1)
    conv_b = _by_chip(cb8, 0).reshape(1, CONVD)
    norm_g = _by_chip(ng8, 0).reshape(1, DI)

    mod_s = ada_fwd(c8, ada_w)
    (mod8,) = _gather_packed([mod_s], "allgather_small_mod")
    mods = _by_chip(mod8, 2)
    mod = lax.dynamic_index_in_dim(mods, dev, axis=1, keepdims=False) + ada_b
    shift = [mod[l:l + 1, :D] for l in range(DEPTH)]
    scale = [mod[l:l + 1, D:2 * D] for l in range(DEPTH)]
    gate = [mod[l:l + 1, 2 * D:] for l in range(DEPTH)]
    lg = [ln_g[l:l + 1] for l in range(DEPTH)]
    lb = [ln_b[l:l + 1] for l in range(DEPTH)]

    h0 = modulate(xin, scale[0] + ag_token[0:1, 0:1], shift[0], "modulate0")
    zx = mm_nn(h0, w_zx, BF16, "mm_in_zx")
    dtp = mm_nn(h0, w_dt, F32, "mm_in_dt")
    xbc = conv_fwd(zx, DI, conv_w, conv_b)
    dtp_g = jnp.transpose(dtp[:, :H].reshape(L, G, Kh), (1, 0, 2))
    dtp_gT = jnp.transpose(dtp_g, (0, 2, 1))
    vecs = [a_dt_bias.reshape(G, 1, Kh), a_dt_bias.reshape(G, Kh, 1), a_A_log.reshape(G, 1, Kh),
            a_A_log.reshape(G, Kh, 1), a_D.reshape(G, 1, Kh), a_D.reshape(G, Kh, 1)]
    y_ssd, states = ssd_fwd(xbc, dtp_g, dtp_gT, *vecs, DI)
    yn = rms_gate_fwd(y_ssd, zx, norm_g)
    later_split, ag_lands = split_wait("gather", ag_ssem, ag_rsem, ag_srcs, ag_lands, yn, "ag_later_wait")
    ag_lands = pass_to_sibling(ag_lands)
    w_out_g, w_kv_g, w_bin_g, w_bout_g = [
        lax.dynamic_update_index_in_dim(o, s, chip, 0).reshape((N_CHIPS,) + full.shape)
        for o, s, full in zip(ag_lands, later_split, later)]
    ymix0 = mm_nn(yn, w_out_g, F32, "mm_out_a", stack="row")
    x1, x1b, h1 = ln_mid(xin, ymix0, gate[0], lg[0], lb[0], scale[1], shift[1])

    kvp = mm_nn(x1b, w_kv_g, BF16, "mm_kv", stack="col")
    qz = mm_nn(h1, w_bin_g, BF16, "mm_in_b", stack="col")
    os_, lses = [], []
    for gi in range(len(DIL_PATTERNS)):
        o, lse = attn_fwd(qz, kvp, gi)
        os_.append(o)
        lses.append(lse)
    om = merge_fwd(os_, lses, qz)
    ymix1 = mm_nn(om, w_bout_g, F32, "mm_out_b", stack="col")
    dx2, sq = ln_final(x1, ymix1, gate[1], lg[1], lb[1], tgt)
    loss_part = 0.5 * jnp.sum(sq) / D

    dres2, dy2, dg1, db1, dgate1 = ln_bwd(dx2, x1, ymix1, gate[1], lg[1], "ln_bwd1")
    g_bout = mm_tn(om, dy2, BF16, "mm_gw_out_b", stack="col")
    dgated = mm_nt(dy2, w_bout_g, BF16, "mm_gx_out_b", stack="col")
    dos, dprs, dz_b = merge_bwd(dgated, os_, lses, qz)
    dqs, dks, dvs = [], [], []
    for gi in range(len(DIL_PATTERNS)):
        dq, dk, dv = attn_bwd(qz, kvp, dos[gi], lses[gi], dprs[gi], gi)
        dqs.append(dq)
        dks.append(dk)
        dvs.append(dv)
    dqz = jnp.concatenate(dqs + [dz_b], axis=1)
    dkv = jnp.concatenate(dks + dvs, axis=1)
    g_bin = mm_tn(h1, dqz, BF16, "mm_gw_in_b", stack="col")
    dh1 = mm_nt(dqz, w_bin_g, F32, "mm_gx_in_b", stack="col")
    g_kv = mm_tn(x1b, dkv, BF16, "mm_gw_kv", stack="col")
    dx1_kv = mm_nt(dkv, w_kv_g, F32, "mm_gx_kv", stack="col")
    dx1, dscale1, dshift1 = mod_bwd(dres2, dh1, dx1_kv, x1, scale[1], "mod_bwd1", through_mod=False)

    core = ac.astype(jnp.int32).reshape(1)
    chip_i = chip.astype(jnp.int32).reshape(1)

    def begin_scatter(gs, nms, tag):
        sib = exchange_halves_to_sibling(gs, "rs_sibling_exchange_" + tag)
        parts = [add_half(g, a, core, "rs_add_" + nm) for g, a, nm in zip(gs, sib, nms)]
        return split_start("scatter", parts, [(3,) + t.shape[1:] for t in parts], parts[0], "rs_%s_start" % tag)

    def finish_scatter(handles, after, tag):
        nms, owns, landed = [], [], []
        for k, (handle, hn) in enumerate(handles):
            parts, lands = split_wait("scatter", handle[0], handle[1], handle[2], handle[3], after,
                                      "rs_%s%d_wait" % (tag, k))
            nms += hn
            owns += list(parts)
            landed += list(lands)
        halves = [sum_partials(own, t, chip_i, "rs_sum_" + nm) for own, t, nm in zip(owns, landed, nms)]
        theirs = join_halves(halves, "rs_join_halves_" + tag)
        return dict(zip(nms, zip(halves, theirs)))

    names_b = ["kv", "in_b", "out_b"]
    rs_b = begin_scatter([g_kv, g_bin, g_bout], names_b, "b")

    dres1, dy1, dg0, db0, dgate0 = ln_bwd(dx1, xin, ymix0, gate[0] + rs_b[4][0:1, 0:1], lg[0], "ln_bwd0")
    g_out = mm_tn(yn, dy1, BF16, "mm_gw_out_a", stack="row")
    rs_a1 = begin_scatter([g_out], ["out_a"], "a1")
    dyn = mm_nt(dy1, w_out_g, BF16, "mm_gx_out_a", stack="row")
    dy_ssd, dz_a, dnorm_g = rms_gate_bwd(dyn, y_ssd, zx, norm_g + rs_a1[4][0:1, 0:1])
    dxs, dB, dC, ddtp_g, dbias_g, dalog_g, dD_g = ssd_bwd(xbc, dtp_g, dtp_gT, *vecs, states, dy_ssd, DI)
    dxbc = jnp.concatenate([dxs, dB, dC], axis=1)
    dxbc_pre, dconv_w, dconv_b = conv_bwd(zx, DI, conv_w, conv_b, dxbc)
    dzx = jnp.concatenate([dz_a, dxbc_pre], axis=1)
    ddtp = jnp.pad(jnp.transpose(ddtp_g, (1, 0, 2)).reshape(L, H), ((0, 0), (0, 128 - H)))
    g_zx = mm_tn(h0, dzx, BF16, "mm_gw_in_zx")
    g_dt = mm_tn(h0, ddtp, BF16, "mm_gw_in_dt")
    g_in = jnp.concatenate([g_zx, g_dt[:, :H]], axis=1)
    g_in = jnp.transpose(g_in.reshape(D, N_CHIPS, -1), (1, 0, 2))
    rs_a2 = begin_scatter([g_in], ["in_a"], "a2")
    dh0 = mm_nt(dzx, w_zx, F32, "mm_gx_in_zx", after=rs_a2[4])
    dh0_dt = mm_nt(ddtp, w_dt, F32, "mm_gx_in_dt")
    grad_x, dscale0, dshift0 = mod_bwd(dres1, dh0, dh0_dt, xin, scale[0] + rs_a2[4][0:1, 0:1], "mod_bwd0",
                                       through_mod=True)
    g_halves = finish_scatter([(rs_b, names_b)], grad_x, "b")

    def step_halves(w, m, v, nm):
        shp = w.shape
        mine, theirs_ = g_halves[nm]
        outs4 = adamw_halves(w.reshape(-1, shp[-1]), mine, theirs_, m.reshape(-1, shp[-1]), v.reshape(-1, shp[-1]),
                             core, "adamw_" + nm)
        return tuple(t.reshape(shp) for t in outs4)

    big = {
        "kv_w": step_halves(kv_w, m_kv_w, v_kv_w, "kv"),
        "b_in_w": step_halves(b_in_w, m_b_in_w, v_b_in_w, "in_b"),
        "b_out_w": step_halves(b_out_w, m_b_out_w, v_b_out_w, "out_b"),
    }
    g_halves.update(finish_scatter([(rs_a1, ["out_a"]), (rs_a2, ["in_a"])], big["kv_w"][1], "a"))
    big["a_in_w"] = step_halves(a_in_w, m_a_in_w, v_a_in_w, "in_a")
    big["a_out_w"] = step_halves(a_out_w, m_a_out_w, v_a_out_w, "out_a")

    dmod = jnp.concatenate([jnp.concatenate([dshift0, dscale0, dgate0], axis=1),
                            jnp.concatenate([dshift1, dscale1, dgate1], axis=1)], axis=0)
    small_parts = [jnp.concatenate([dg0, dg1], axis=0), jnp.concatenate([db0, db1], axis=0),
                   dbias_g.reshape(1, H), dalog_g.reshape(1, H), dD_g.reshape(1, H),
                   dconv_w, dconv_b, dnorm_g, loss_part.reshape(1, 1)]
    small_shapes = [p.shape for p in small_parts]
    packed = jnp.concatenate([_pack([dmod]), _pack(small_parts)], axis=0)
    n_mod_rows = _pack([dmod]).shape[0]
    gathered = allgather_small(packed, "allgather_small_grads", after=g_halves["in_a"][1]).reshape(N_DEV, -1, 128)
    dmod8 = gathered[:, :n_mod_rows].reshape(N_DEV, -1)[:, :2 * 3 * D].reshape(N_DEV, DEPTH, 3 * D)
    summed = sum_leading(gathered, "sum_small")
    g_ada_b = summed[:n_mod_rows].reshape(-1)[:2 * 3 * D].reshape(DEPTH, 3 * D)
    (g_ln_g, g_ln_b, g_dt_bias, g_a_log, g_dsk, g_conv_w, g_conv_b, g_norm_g, loss_all) = _unpack(
        summed[n_mod_rows:].reshape(-1), small_shapes)
    loss = loss_all.reshape(())
    Cs = CONVD // N_CHIPS
    g_conv_w_s = lax.dynamic_slice_in_dim(g_conv_w, chip * Cs, Cs, axis=1)
    g_conv_b_s = lax.dynamic_slice_in_dim(g_conv_b, chip * Cs, Cs, axis=1)
    g_norm_g_s = lax.dynamic_slice_in_dim(g_norm_g, chip * (DI // N_CHIPS), DI // N_CHIPS, axis=1)
    dmod_s = jnp.transpose(lax.dynamic_slice_in_dim(dmod8, chip * Ws, Ws, axis=2), (1, 0, 2))
    g_ada_w = ada_wgrad(jnp.transpose(c8), dmod_s)

    def step2d(w, g, m, v, nm):
        shp = w.shape
        d_, m_, v_ = adamw(w.reshape(-1, shp[-1]), g.reshape(-1, shp[-1]), m.reshape(-1, shp[-1]),
                           v.reshape(-1, shp[-1]), "adamw_" + nm)
        return g.reshape(shp), d_.reshape(shp), m_.reshape(shp), v_.reshape(shp)

    big["ada_w"] = step2d(ada_w, g_ada_w, m_ada_w, v_ada_w, "ada_w")
    small_names = ["ada_b", "ln_g", "ln_b", "a_conv_w", "a_conv_b", "a_dt_bias", "a_A_log", "a_D", "a_norm_g"]
    small_w = [ada_b, ln_g, ln_b, a_conv_w, a_conv_b, a_dt_bias, a_A_log, a_D, a_norm_g]
    small_m = [m_ada_b, m_ln_g, m_ln_b, m_a_conv_w, m_a_conv_b, m_a_dt_bias, m_a_A_log, m_a_D, m_a_norm_g]
    small_v = [v_ada_b, v_ln_g, v_ln_b, v_a_conv_w, v_a_conv_b, v_a_dt_bias, v_a_A_log, v_a_D, v_a_norm_g]
    small_g = [g_ada_b, g_ln_g, g_ln_b, g_conv_w_s, g_conv_b_s, g_dt_bias, g_a_log, g_dsk, g_norm_g_s]
    shapes = [w.shape for w in small_w]
    small_g = [g.reshape(s) for g, s in zip(small_g, shapes)]
    d_p, m_p, v_p = adamw(_pack(small_w), _pack(small_g), _pack(small_m), _pack(small_v), "adamw_small")
    small = {}
    for nm, g, d_, m_, v_ in zip(small_names, small_g, _unpack(d_p.reshape(-1), shapes), _unpack(m_p.reshape(-1), shapes),
                                 _unpack(v_p.reshape(-1), shapes)):
        small[nm] = (g, d_, m_, v_)
    allw = {**big, **small}
    order = ["ada_w", "ada_b", "ln_g", "ln_b", "a_in_w", "a_conv_w", "a_conv_b", "a_dt_bias", "a_A_log", "a_D",
             "a_norm_g", "a_out_w", "kv_w", "b_in_w", "b_out_w"]
    outs = [loss, grad_x.reshape(x.shape)]
    for k in range(4):
        outs += [allw[n][k] for n in order]
    return tuple(outs)
```

```python
import functools

import jax
import jax.numpy as jnp
import numpy as np
from jax import lax
from jax.experimental import pallas as pl
from jax.experimental.pallas import tpu as pltpu

F32 = jnp.float32
BF16 = jnp.bfloat16
MESH = pl.DeviceIdType.MESH

DEPTH = 2
ALPHA = (2 * DEPTH) ** 0.25
LN_EPS = 1e-5
RMS_EPS = 1e-5
SSD_P = 64
SSD_N = 128
SSD_Q = 256
SSD_G = 8
CONV_W = 4
DIL_PATTERNS = ((128, 1), (512, 4), (2048, 16))
DIL_H = 8
DIL_E = 128
DIL_BLK = 128
ADAM_LR, ADAM_B1, ADAM_B2, ADAM_EPS, ADAM_WD, ADAM_STEP = 0.001, 0.9, 0.999, 1e-08, 0.01, 10

VMEM_LIMIT = 56 * 1024 * 1024
N_CHIPS = 4
N_DEV = 8


def _tile(dim, target, mult=128):
    if dim <= target:
        return dim
    t = (target // mult) * mult
    while t >= mult:
        if dim % t == 0:
            return t
        t -= mult
    return dim


def _cp(sem):
    return pltpu.CompilerParams(dimension_semantics=sem, vmem_limit_bytes=VMEM_LIMIT)


def _sigmoid(x):
    return 1.0 / (1.0 + jnp.exp(-x))


def _silu(x):
    return x * _sigmoid(x)


def _dsilu(x):
    s = _sigmoid(x)
    return s * (1.0 + x * (1.0 - s))


def _softplus(x):
    return jnp.maximum(x, 0.0) + jnp.log(1.0 + jnp.exp(-jnp.abs(x)))


def _mm_call(a, b, out_shape, grid, a_spec, b_spec, o_spec, acc_shape, dims, name, after=None):
    nk = grid[2]
    extra = [] if after is None else [after]

    def prod(a_ref, b_ref):
        return lax.dot_general(a_ref[...].astype(BF16), b_ref[...].astype(BF16), (dims, ((), ())),
                               preferred_element_type=F32)

    def body_single(a_ref, b_ref, *rest):
        o_ref = rest[len(extra)]
        o_ref[...] = prod(a_ref, b_ref).astype(o_ref.dtype)

    def body_multi(a_ref, b_ref, *rest):
        o_ref, acc_ref = rest[len(extra):]
        k = pl.program_id(2)

        @pl.when(k == 0)
        def _():
            acc_ref[...] = prod(a_ref, b_ref)

        @pl.when(jnp.logical_and(k > 0, k < nk - 1))
        def _():
            acc_ref[...] += prod(a_ref, b_ref)

        @pl.when(k == nk - 1)
        def _():
            o_ref[...] = (acc_ref[...] + prod(a_ref, b_ref)).astype(o_ref.dtype)

    return pl.pallas_call(
        body_single if nk == 1 else body_multi, grid=grid, in_specs=[a_spec, b_spec] + [_ANY] * len(extra),
        out_specs=o_spec, out_shape=out_shape, scratch_shapes=[] if nk == 1 else [pltpu.VMEM(acc_shape, F32)],
        compiler_params=_cp(("parallel", "parallel", "arbitrary")), name=name)(a, b, *extra)


def mm_nn(a, b, out_dtype, name, stack=None, tm=1024, tn=1024, tk=2048, n_cols=None):
    M, K = a.shape
    if stack is None:
        N = b.shape[1] if n_cols is None else n_cols
        tn, tk = _tile(N, tn), _tile(K, tk)
        b_spec = pl.BlockSpec((tk, tn), lambda i, j, k: (k, j))
    elif stack == "col":
        S, _, Ns = b.shape
        N = S * Ns
        tn, tk = _tile(Ns, tn), _tile(K, tk)
        npb = Ns // tn
        b_spec = pl.BlockSpec((None, tk, tn), lambda i, j, k: (j // npb, k, j % npb))
    else:
        S, Ks, N = b.shape
        tn, tk = _tile(N, tn), _tile(Ks, tk)
        kpb = Ks // tk
        b_spec = pl.BlockSpec((None, tk, tn), lambda i, j, k: (k // kpb, k % kpb, j))
    tm = _tile(M, tm)
    return _mm_call(a, b, jax.ShapeDtypeStruct((M, N), out_dtype), (M // tm, N // tn, K // tk),
                    pl.BlockSpec((tm, tk), lambda i, j, k: (i, k)), b_spec,
                    pl.BlockSpec((tm, tn), lambda i, j, k: (i, j)), (tm, tn), ((1,), (0,)), name)


def mm_nt(a, b, out_dtype, name, stack=None, tm=1024, tn=1024, tk=2048, after=None):
    M, C = a.shape
    if stack is None:
        Kw = b.shape[0]
        tn, tk = _tile(Kw, tn), _tile(C, tk)
        b_spec = pl.BlockSpec((tn, tk), lambda i, j, k: (j, k))
    elif stack == "col":
        S, Kw, Cs = b.shape
        tn, tk = _tile(Kw, tn), _tile(Cs, tk)
        cpb = Cs // tk
        b_spec = pl.BlockSpec((None, tn, tk), lambda i, j, k: (k // cpb, j, k % cpb))
    else:
        S, Ks, _ = b.shape
        Kw = S * Ks
        tn, tk = _tile(Ks, tn), _tile(C, tk)
        jpb = Ks // tn
        b_spec = pl.BlockSpec((None, tn, tk), lambda i, j, k: (j // jpb, j % jpb, k))
    tm = _tile(M, tm)
    return _mm_call(a, b, jax.ShapeDtypeStruct((M, Kw), out_dtype), (M // tm, Kw // tn, C // tk),
                    pl.BlockSpec((tm, tk), lambda i, j, k: (i, k)), b_spec,
                    pl.BlockSpec((tm, tn), lambda i, j, k: (i, j)), (tm, tn), ((1,), (1,)), name, after=after)


def mm_tn(a, b, out_dtype, name, stack=None, n_stack=N_CHIPS, tm=1024, tn=1024, tk=2048):
    L, M = a.shape
    N = b.shape[1]
    tk = _tile(L, tk)
    if stack is None:
        tm, tn = _tile(M, tm), _tile(N, tn)
        o_spec = pl.BlockSpec((tm, tn), lambda i, j, k: (i, j))
        out_shape = (M, N)
    elif stack == "col":
        Ns = N // n_stack
        tm, tn = _tile(M, tm), _tile(Ns, tn)
        npb = Ns // tn
        o_spec = pl.BlockSpec((None, tm, tn), lambda i, j, k: (j // npb, i, j % npb))
        out_shape = (n_stack, M, Ns)
    else:
        Ms = M // n_stack
        tm, tn = _tile(Ms, tm), _tile(N, tn)
        mpb = Ms // tm
        o_spec = pl.BlockSpec((None, tm, tn), lambda i, j, k: (i // mpb, i % mpb, j))
        out_shape = (n_stack, Ms, N)
    return _mm_call(a, b, jax.ShapeDtypeStruct(out_shape, out_dtype), (M // tm, N // tn, L // tk),
                    pl.BlockSpec((tk, tm), lambda i, j, k: (k, i)), pl.BlockSpec((tk, tn), lambda i, j, k: (k, j)),
                    o_spec, (tm, tn), ((0,), (0,)), name)


def _row_specs(tr, widths):
    return [pl.BlockSpec((tr, w), lambda i: (i, 0)) for w in widths]


def _vec_spec(w):
    return pl.BlockSpec((1, w), lambda i: (0, 0))


def _acc_rows(ref, val, i):
    s = jnp.sum(val, axis=0, keepdims=True)

    @pl.when(i == 0)
    def _():
        ref[...] = s

    @pl.when(i > 0)
    def _():
        ref[...] += s


def modulate(x, scale, shift, name):
    L, D = x.shape
    tr = _tile(L, 512, 16)

    def body(x_ref, sc_ref, sh_ref, h_ref):
        h_ref[...] = (x_ref[...] * (1.0 + sc_ref[...]) + sh_ref[...]).astype(BF16)

    return pl.pallas_call(
        body, grid=(L // tr,), in_specs=_row_specs(tr, [D]) + [_vec_spec(D)] * 2, out_specs=_row_specs(tr, [D])[0],
        out_shape=jax.ShapeDtypeStruct((L, D), BF16), compiler_params=_cp(("parallel",)), name=name)(x, scale, shift)


def _ln_core(x, y, gate, g, b):
    u = ALPHA * x + (1.0 + gate) * y
    mu = jnp.mean(u, axis=-1, keepdims=True)
    d = u - mu
    var = jnp.mean(d * d, axis=-1, keepdims=True)
    rstd = lax.rsqrt(var + LN_EPS)
    xhat = d * rstd
    return xhat * g + b, xhat, rstd


def ln_mid(x, y, gate, g, b, scale, shift):
    L, D = x.shape
    tr = _tile(L, 256, 16)

    def body(x_ref, y_ref, gate_ref, g_ref, b_ref, sc_ref, sh_ref, x1_ref, x1b_ref, h_ref):
        x1, _, _ = _ln_core(x_ref[...], y_ref[...], gate_ref[...], g_ref[...], b_ref[...])
        x1_ref[...] = x1
        x1b_ref[...] = x1.astype(BF16)
        h_ref[...] = (x1 * (1.0 + sc_ref[...]) + sh_ref[...]).astype(BF16)

    return pl.pallas_call(
        body, grid=(L // tr,), in_specs=_row_specs(tr, [D, D]) + [_vec_spec(D)] * 5,
        out_specs=_row_specs(tr, [D, D, D]),
        out_shape=[jax.ShapeDtypeStruct((L, D), F32), jax.ShapeDtypeStruct((L, D), BF16),
                   jax.ShapeDtypeStruct((L, D), BF16)],
        compiler_params=_cp(("parallel",)), name="ln_mid")(x, y, gate, g, b, scale, shift)


def _ln_bwd_rows(dout_v, xhat, rstd, g):
    dxh = dout_v * g
    m1 = jnp.mean(dxh, axis=-1, keepdims=True)
    m2 = jnp.mean(dxh * xhat, axis=-1, keepdims=True)
    return rstd * (dxh - m1 - xhat * m2)


def ln_final_fwd_bwd(x, y, gate, g, b, target):
    L, D = x.shape
    tr = _tile(L, 256, 16)

    def body(x_ref, y_ref, gate_ref, g_ref, b_ref, t_ref, dres_ref, dy_ref, dg_ref, db_ref, dgate_ref, sq_ref):
        i = pl.program_id(0)
        yv = y_ref[...]
        out, xhat, rstd = _ln_core(x_ref[...], yv, gate_ref[...], g_ref[...], b_ref[...])
        err = out - t_ref[...]
        dout_v = err * (1.0 / D)
        du = _ln_bwd_rows(dout_v, xhat, rstd, g_ref[...])
        dres_ref[...] = ALPHA * du
        dy_ref[...] = ((1.0 + gate_ref[...]) * du).astype(BF16)
        _acc_rows(dg_ref, dout_v * xhat, i)
        _acc_rows(db_ref, dout_v, i)
        _acc_rows(dgate_ref, du * yv, i)
        _acc_rows(sq_ref, err * err, i)

    return pl.pallas_call(
        body, grid=(L // tr,), in_specs=_row_specs(tr, [D, D]) + [_vec_spec(D)] * 3 + _row_specs(tr, [D]),
        out_specs=_row_specs(tr, [D, D]) + [_vec_spec(D)] * 4,
        out_shape=[jax.ShapeDtypeStruct((L, D), F32), jax.ShapeDtypeStruct((L, D), BF16)]
        + [jax.ShapeDtypeStruct((1, D), F32)] * 4,
        compiler_params=_cp(("arbitrary",)), name="ln_final_fwd_bwd")(x, y, gate, g, b, target)


def mod_ln_bwd(dres_in, dh, dskip, xmid, scale, x, y, gate, g):
    L, D = x.shape
    tr = _tile(L, 256, 16)

    def body(dres_ref, dh_ref, dskip_ref, xm_ref, sc_ref, x_ref, y_ref, gate_ref, g_ref,
             dres_out, dy_ref, dg_ref, db_ref, dgate_ref, dsc_ref, dsh_ref):
        i = pl.program_id(0)
        dh_v = dh_ref[...]
        dout_v = dres_ref[...] + dskip_ref[...] + dh_v * (1.0 + sc_ref[...])
        _acc_rows(dsc_ref, dh_v * xm_ref[...], i)
        _acc_rows(dsh_ref, dh_v, i)
        yv = y_ref[...]
        _, xhat, rstd = _ln_core(x_ref[...], yv, gate_ref[...], g_ref[...], 0.0)
        du = _ln_bwd_rows(dout_v, xhat, rstd, g_ref[...])
        dres_out[...] = ALPHA * du
        dy_ref[...] = ((1.0 + gate_ref[...]) * du).astype(BF16)
        _acc_rows(dg_ref, dout_v * xhat, i)
        _acc_rows(db_ref, dout_v, i)
        _acc_rows(dgate_ref, du * yv, i)

    return pl.pallas_call(
        body, grid=(L // tr,),
        in_specs=_row_specs(tr, [D] * 4) + [_vec_spec(D)] + _row_specs(tr, [D, D]) + [_vec_spec(D)] * 2,
        out_specs=_row_specs(tr, [D, D]) + [_vec_spec(D)] * 5,
        out_shape=[jax.ShapeDtypeStruct((L, D), F32), jax.ShapeDtypeStruct((L, D), BF16)]
        + [jax.ShapeDtypeStruct((1, D), F32)] * 5,
        compiler_params=_cp(("arbitrary",)), name="mod_ln_bwd")(dres_in, dh, dskip, xmid, scale, x, y, gate, g)


def ln_bwd(dout, x, y, gate, g, name):
    L, D = x.shape
    tr = _tile(L, 256, 16)

    def body(do_ref, x_ref, y_ref, gate_ref, g_ref, dres_ref, dy_ref, dg_ref, db_ref, dgate_ref):
        i = pl.program_id(0)
        yv = y_ref[...]
        dout_v = do_ref[...]
        _, xhat, rstd = _ln_core(x_ref[...], yv, gate_ref[...], g_ref[...], 0.0)
        dxh = dout_v * g_ref[...]
        m1 = jnp.mean(dxh, axis=-1, keepdims=True)
        m2 = jnp.mean(dxh * xhat, axis=-1, keepdims=True)
        du = rstd * (dxh - m1 - xhat * m2)
        dres_ref[...] = ALPHA * du
        dy_ref[...] = ((1.0 + gate_ref[...]) * du).astype(BF16)
        _acc_rows(dg_ref, dout_v * xhat, i)
        _acc_rows(db_ref, dout_v, i)
        _acc_rows(dgate_ref, du * yv, i)

    return pl.pallas_call(
        body, grid=(L // tr,), in_specs=_row_specs(tr, [D, D, D]) + [_vec_spec(D)] * 2,
        out_specs=_row_specs(tr, [D, D]) + [_vec_spec(D)] * 3,
        out_shape=[jax.ShapeDtypeStruct((L, D), F32), jax.ShapeDtypeStruct((L, D), BF16)]
        + [jax.ShapeDtypeStruct((1, D), F32)] * 3,
        compiler_params=_cp(("arbitrary",)), name=name)(dout, x, y, gate, g)


def mod_bwd(dres, dh, dh2, xin, scale, name, through_mod):
    L, D = xin.shape
    tr = _tile(L, 256, 16)

    def body(dres_ref, dh_ref, dh2_ref, x_ref, sc_ref, dx_ref, dsc_ref, dsh_ref):
        i = pl.program_id(0)
        dh_v = dh_ref[...]
        tot = dres_ref[...]
        if through_mod:
            dh_v = dh_v + dh2_ref[...]
        else:
            tot = tot + dh2_ref[...]
        dx_ref[...] = tot + dh_v * (1.0 + sc_ref[...])
        _acc_rows(dsc_ref, dh_v * x_ref[...], i)
        _acc_rows(dsh_ref, dh_v, i)

    return pl.pallas_call(
        body, grid=(L // tr,), in_specs=_row_specs(tr, [D, D, D, D]) + [_vec_spec(D)],
        out_specs=_row_specs(tr, [D]) + [_vec_spec(D)] * 2,
        out_shape=[jax.ShapeDtypeStruct((L, D), F32)] + [jax.ShapeDtypeStruct((1, D), F32)] * 2,
        compiler_params=_cp(("arbitrary",)), name=name)(dres, dh, dh2, xin, scale)


CONV_HALO = 16


def _conv_rows(x_ref, i, tr, L):
    nblk = L // tr
    s = pl.multiple_of(i * tr, CONV_HALO)
    cur = x_ref[pl.ds(s, tr), :].astype(F32)
    sp = pl.multiple_of(jnp.maximum(i * tr - CONV_HALO, 0), CONV_HALO)
    sn = pl.multiple_of(jnp.minimum(i * tr + tr, L - CONV_HALO), CONV_HALO)
    prev = x_ref[pl.ds(sp, CONV_HALO), :].astype(F32) * (i > 0).astype(F32)
    nxt = x_ref[pl.ds(sn, CONV_HALO), :].astype(F32) * (i < nblk - 1).astype(F32)
    return jnp.concatenate([prev, cur, nxt], axis=0)


def _shift_rows(v, j):
    n = v.shape[0]
    return v if j % n == 0 else pltpu.roll(v, j % n, 0)


def _conv_eval(xe, w_ref, b_ref):
    c = b_ref[...] + w_ref[CONV_W - 1:CONV_W, :] * xe
    for k in range(CONV_W - 1):
        c = c + w_ref[k:k + 1, :] * _shift_rows(xe, CONV_W - 1 - k)
    return c


def conv_fwd(zx, col0, conv_w, conv_b):
    L = zx.shape[0]
    C = conv_w.shape[1]
    tc = _tile(C, 512)
    tr = _tile(L, 512, CONV_HALO)
    off = col0 // tc

    def body(x_ref, w_ref, b_ref, o_ref):
        i = pl.program_id(1)
        xe = _conv_rows(x_ref, i, tr, L)
        c = _conv_eval(xe, w_ref, b_ref)[CONV_HALO:CONV_HALO + tr]
        o_ref[...] = _silu(c).astype(BF16)

    return pl.pallas_call(
        body, grid=(C // tc, L // tr),
        in_specs=[pl.BlockSpec((L, tc), lambda j, i: (0, off + j)), pl.BlockSpec((CONV_W, tc), lambda j, i: (0, j)),
                  pl.BlockSpec((1, tc), lambda j, i: (0, j))],
        out_specs=pl.BlockSpec((tr, tc), lambda j, i: (i, j)),
        out_shape=jax.ShapeDtypeStruct((L, C), BF16), compiler_params=_cp(("parallel", "arbitrary")),
        name="conv_fwd")(zx, conv_w, conv_b)


def conv_bwd(zx, col0, conv_w, conv_b, dxbc):
    L = zx.shape[0]
    C = conv_w.shape[1]
    tc = _tile(C, 512)
    tr = _tile(L, 512, CONV_HALO)
    off = col0 // tc
    H = CONV_HALO

    def body(x_ref, g_ref, w_ref, b_ref, dx_ref, dw_ref, db_ref):
        i = pl.program_id(1)
        xe = _conv_rows(x_ref, i, tr, L)
        ge = _conv_rows(g_ref, i, tr, L)
        dc = ge * _dsilu(_conv_eval(xe, w_ref, b_ref))
        dx = w_ref[CONV_W - 1:CONV_W, :] * dc
        for k in range(CONV_W - 1):
            dx = dx + w_ref[k:k + 1, :] * _shift_rows(dc, -(CONV_W - 1 - k))
        dx_ref[...] = dx[H:H + tr].astype(BF16)
        dcc = dc[H:H + tr]
        rows = [jnp.sum(dcc * _shift_rows(xe, CONV_W - 1 - k)[H:H + tr], axis=0, keepdims=True) for k in range(CONV_W)]
        dwv = jnp.concatenate(rows + [jnp.zeros((8 - CONV_W, tc), F32)], axis=0)
        dbv = jnp.sum(dcc, axis=0, keepdims=True)

        @pl.when(i == 0)
        def _():
            dw_ref[...] = dwv
            db_ref[...] = dbv

        @pl.when(i > 0)
        def _():
            dw_ref[...] += dwv
            db_ref[...] += dbv

    dx, dw, db = pl.pallas_call(
        body, grid=(C // tc, L // tr),
        in_specs=[pl.BlockSpec((L, tc), lambda j, i: (0, off + j)), pl.BlockSpec((L, tc), lambda j, i: (0, j)),
                  pl.BlockSpec((CONV_W, tc), lambda j, i: (0, j)), pl.BlockSpec((1, tc), lambda j, i: (0, j))],
        out_specs=[pl.BlockSpec((tr, tc), lambda j, i: (i, j)), pl.BlockSpec((8, tc), lambda j, i: (0, j)),
                   pl.BlockSpec((1, tc), lambda j, i: (0, j))],
        out_shape=[jax.ShapeDtypeStruct((L, C), BF16), jax.ShapeDtypeStruct((8, C), F32),
                   jax.ShapeDtypeStruct((1, C), F32)],
        compiler_params=_cp(("parallel", "arbitrary")), name="conv_bwd")(zx, dxbc, conv_w, conv_b)
    return dx, dw[:CONV_W], db


_NN = (((1,), (0,)), ((), ()))


def _pieces(x, n):
    out, r = [], x
    for _ in range(n):
        p = r.astype(BF16)
        out.append(p)
        r = r - p.astype(F32)
    return out


def _dot01(a, b01, n, dims=_NN):
    b = b01.astype(BF16)
    return functools.reduce(lambda u, v: u + v,
                            [lax.dot_general(p, b, dims, preferred_element_type=F32) for p in _pieces(a, n)])


def _dot01_left(a01, b, n, dims=_NN):
    a = a01.astype(BF16)
    return functools.reduce(lambda u, v: u + v,
                            [lax.dot_general(a, p, dims, preferred_element_type=F32) for p in _pieces(b, n)])


def _ssd_common(dtp_ref, dtpT_ref, bias_ref, biasT_ref, alog_ref, alogT_ref, b_ref, c_ref):
    Q = SSD_Q
    dt = _softplus(dtp_ref[...] + bias_ref[...])
    A = -jnp.exp(alog_ref[...])
    row = lax.broadcasted_iota(jnp.int32, (Q, Q), 0)
    col = lax.broadcasted_iota(jnp.int32, (Q, Q), 1)
    causal = row >= col
    tril = causal.astype(F32)
    Kh = dt.shape[1]
    acum = _dot01_left(tril, dt * A, 3)
    eye = (lax.broadcasted_iota(jnp.int32, (Kh, Kh), 0) == lax.broadcasted_iota(jnp.int32, (Kh, Kh), 1)).astype(F32)
    acumT = _dot01_left(eye, acum, 3, dims=(((1,), (1,)), ((), ())))
    Bm = b_ref[...]
    Cm = c_ref[...]
    cb = lax.dot_general(Cm, Bm, (((1,), (1,)), ((), ())), preferred_element_type=F32)
    return dt, A, causal, row, col, acum, acumT, Bm, Cm, cb


def _ssd_in_specs(Q, GP, N, Kh, DI, cmap):
    nb0 = DI // N
    vec = pl.BlockSpec((None, 1, Kh), lambda g, c: (g, 0, 0))
    vecT = pl.BlockSpec((None, Kh, 1), lambda g, c: (g, 0, 0))
    return [pl.BlockSpec((Q, GP), lambda g, c: (cmap(c), g)),
            pl.BlockSpec((Q, N), lambda g, c: (cmap(c), nb0 + g)),
            pl.BlockSpec((Q, N), lambda g, c: (cmap(c), nb0 + SSD_G + g)),
            pl.BlockSpec((None, Q, Kh), lambda g, c: (g, cmap(c), 0)),
            pl.BlockSpec((None, Kh, Q), lambda g, c: (g, 0, cmap(c))),
            vec, vecT, vec, vecT, vec, vecT]


def _hi(a, b01):
    return _dot01(a, b01, 2)


def _ssd_heads(dskT_ref, acum, acumT, dt, Kh):
    Q, P, N = SSD_Q, SSD_P, SSD_N
    GP = Kh * P
    sh_p = P.bit_length() - 1
    seg = lambda shape, dim: lax.shift_right_logical(lax.broadcasted_iota(jnp.int32, shape, dim), sh_p)
    E = (seg((Kh, GP), 1) == lax.broadcasted_iota(jnp.int32, (Kh, GP), 0)).astype(F32)
    ET = (seg((GP, Kh), 0) == lax.broadcasted_iota(jnp.int32, (GP, Kh), 1)).astype(F32)
    a_last = acum[Q - 1:Q, :]
    tail = jnp.exp(a_last - acum)
    eLT = jnp.exp(acumT[:, Q - 1:Q])
    rowseg = seg((GP, N), 0)
    eL_b = jnp.zeros((GP, N), F32)
    for k in range(Kh):
        eL_b = jnp.where(rowseg == k, eLT[k:k + 1, :], eL_b)
    return dict(
        E=E, ET=ET, a_last=a_last, tail=tail, eL_b=eL_b,
        dt_all=_hi(dt, E), ea_all=_hi(jnp.exp(acum), E), tail_all=_hi(tail, E),
        dsk_all=jnp.sum(E * dskT_ref[...], axis=0, keepdims=True))


def _head_chunks(GP):
    CW = min(GP, 128)
    return CW, CW // SSD_P, GP // CW


def _head_mask(Q, CW, kk):
    lane = lax.broadcasted_iota(jnp.int32, (Q, CW), 1)
    return jnp.logical_and(lane >= kk * SSD_P, lane < (kk + 1) * SSD_P)


def ssd_fwd(xbc, dtp_g, dtp_gT, bias_g, bias_gT, alog_g, alog_gT, dsk_g, dsk_gT, DI):
    L = xbc.shape[0]
    Q, P, N, G = SSD_Q, SSD_P, SSD_N, SSD_G
    GP = DI // G
    Kh = GP // P
    nc = L // Q

    CW, hpc, nch = _head_chunks(GP)
    nt = (((1,), (1,)), ((), ()))
    tn = (((0,), (0,)), ((), ()))

    def body(xs_ref, b_ref, c_ref, dtp_ref, dtpT_ref, bias_ref, biasT_ref, alog_ref, alogT_ref, dsk_ref, dskT_ref,
             y_ref, st_ref, state):
        @pl.when(pl.program_id(1) == 0)
        def _():
            state[...] = jnp.zeros(state.shape, F32)

        st_ref[...] = state[...]
        dt, A, causal, row, col, acum, acumT, Bm, Cm, cb = _ssd_common(
            dtp_ref, dtpT_ref, bias_ref, biasT_ref, alog_ref, alogT_ref, b_ref, c_ref)
        hd = _ssd_heads(dskT_ref, acum, acumT, dt, Kh)
        xs = xs_ref[...].astype(F32)
        xdt_all = xs * hd["dt_all"]
        S_all = state[...]
        y_all = (lax.dot_general(Cm, S_all.astype(BF16), nt, preferred_element_type=F32) * hd["ea_all"]
                 + xs * hd["dsk_all"])
        state[...] = S_all * hd["eL_b"] + lax.dot_general(
            (xdt_all * hd["tail_all"]).astype(BF16), Bm, tn, preferred_element_type=F32)
        for ch in range(nch):
            cs = slice(ch * CW, (ch + 1) * CW)
            xc = xdt_all[:, cs]
            acc = y_all[:, cs]
            for kk in range(hpc):
                k = ch * hpc + kk
                decay = jnp.exp(jnp.where(causal, acum[:, k:k + 1] - acumT[k:k + 1, :], -jnp.inf))
                xk = xc if hpc == 1 else jnp.where(_head_mask(Q, CW, kk), xc, 0.0)
                acc = acc + jnp.dot((cb * decay).astype(BF16), xk.astype(BF16), preferred_element_type=F32)
            y_ref[:, cs] = acc.astype(BF16)

    return pl.pallas_call(
        body, grid=(G, nc), in_specs=_ssd_in_specs(Q, GP, N, Kh, DI, lambda c: c),
        out_specs=[pl.BlockSpec((Q, GP), lambda g, c: (c, g)),
                   pl.BlockSpec((None, None, GP, N), lambda g, c: (c, g, 0, 0))],
        out_shape=[jax.ShapeDtypeStruct((L, DI), BF16), jax.ShapeDtypeStruct((nc, G, GP, N), F32)],
        scratch_shapes=[pltpu.VMEM((GP, N), F32)], compiler_params=_cp(("parallel", "arbitrary")),
        name="ssd_fwd")(xbc, xbc, xbc, dtp_g, dtp_gT, bias_g, bias_gT, alog_g, alog_gT, dsk_g, dsk_gT)


def ssd_bwd(xbc, dtp_g, dtp_gT, bias_g, bias_gT, alog_g, alog_gT, dsk_g, dsk_gT, states, dy, DI):
    L = xbc.shape[0]
    Q, P, N, G = SSD_Q, SSD_P, SSD_N, SSD_G
    GP = DI // G
    Kh = GP // P
    nc = L // Q
    rev = lambda c: nc - 1 - c

    CW, hpc, nch = _head_chunks(GP)

    def body(xs_ref, b_ref, c_ref, dtp_ref, dtpT_ref, bias_ref, biasT_ref, alog_ref, alogT_ref, dsk_ref, dskT_ref,
             st_ref, dy_ref, dxs_ref, dB_ref, dC_ref, ddtp_ref, dbias_ref, dalog_ref, dD_ref, dstate):
        ci = pl.program_id(1)

        @pl.when(ci == 0)
        def _():
            dstate[...] = jnp.zeros(dstate.shape, F32)

        dt, A, causal, row, col, acum, acumT, Bm, Cm, cb = _ssd_common(
            dtp_ref, dtpT_ref, bias_ref, biasT_ref, alog_ref, alogT_ref, b_ref, c_ref)
        tn = (((0,), (0,)), ((), ()))
        nt = (((1,), (1,)), ((), ()))
        hd = _ssd_heads(dskT_ref, acum, acumT, dt, Kh)
        ET, tail = hd["ET"], hd["tail"]
        cbT = lax.dot_general(Bm, Cm, nt, preferred_element_type=F32)
        causalT = row <= col
        xs = xs_ref[...].astype(F32)
        xdt_all = xs * hd["dt_all"]
        dyb = dy_ref[...]
        dy_all = dyb.astype(F32)
        S_all = st_ref[...]
        S_b = S_all.astype(BF16)
        dS_all = dstate[...]
        dS_b = dS_all.astype(BF16)
        CS_all = lax.dot_general(Cm, S_b, nt, preferred_element_type=F32)
        dyE_b = (dy_all * hd["ea_all"]).astype(BF16)
        dC_acc = jnp.dot(dyE_b, S_b, preferred_element_type=F32)
        dS_y = lax.dot_general(dyE_b, Cm, tn, preferred_element_type=F32)
        BdS_all = lax.dot_general(Bm, dS_b, nt, preferred_element_type=F32)
        dB_acc = jnp.dot((xdt_all * hd["tail_all"]).astype(BF16), dS_b, preferred_element_type=F32)
        dtail = _hi(xdt_all * BdS_all, ET)
        da_cols = _hi(dy_all * CS_all * hd["ea_all"], ET) - dtail * tail
        dss = _dot01_left(jnp.ones((8, N), F32), _dot01_left(hd["E"], dS_all * S_all, 2), 2, dims=nt)
        da_last = dss[0:1] * jnp.exp(hd["a_last"]) + jnp.sum(dtail * tail, axis=0, keepdims=True)
        rowi = lax.broadcasted_iota(jnp.int32, (Q, Kh), 0)
        da_cols = da_cols + jnp.where(rowi == Q - 1, da_last, 0.0)
        dstate[...] = hd["eL_b"] * dS_all + dS_y
        sum_mg = jnp.zeros((Q, Q), F32)
        ddt_x = jnp.zeros((Q, Kh), F32)
        da_rows = jnp.zeros((Kh, Q), F32)
        lane_k = lax.broadcasted_iota(jnp.int32, (Q, Kh), 1)
        sub_k = lax.broadcasted_iota(jnp.int32, (Kh, Q), 0)
        for ch in range(nch):
            cs = slice(ch * CW, (ch + 1) * CW)
            dyc = dyb[:, cs]
            xc_b = xdt_all[:, cs].astype(BF16)
            acc = hd["tail_all"][:, cs] * BdS_all[:, cs]
            for kk in range(hpc):
                k = ch * hpc + kk
                a_b = jnp.broadcast_to(acum[:, k:k + 1], (Q, Q))
                a_r = acumT[k:k + 1, :]
                decay = jnp.exp(jnp.where(causal, a_b - a_r, -jnp.inf))
                decayT = jnp.exp(jnp.where(causalT, a_r - a_b, -jnp.inf))
                dyk = dyc if hpc == 1 else jnp.where(_head_mask(Q, CW, kk), dyc, jnp.zeros_like(dyc))
                mg = decay * lax.dot_general(dyk, xc_b, nt, preferred_element_type=F32)
                sum_mg = sum_mg + mg
                w = mg * cb
                da_cols = da_cols + jnp.where(lane_k == k, jnp.sum(w, axis=1, keepdims=True), 0.0)
                da_rows = da_rows + jnp.where(sub_k == k, jnp.sum(w, axis=0, keepdims=True), 0.0)
                acc = acc + jnp.dot((decayT * cbT).astype(BF16), dyk, preferred_element_type=F32)
            dxs_ref[:, cs] = (acc * hd["dt_all"][:, cs] + dy_all[:, cs] * hd["dsk_all"][:, cs]).astype(BF16)
            ddt_x = ddt_x + _hi(acc * xs[:, cs], ET[cs, :])
        eye_q = (row == col).astype(F32)
        da_cols = da_cols - _dot01_left(eye_q, da_rows, 3, dims=nt)
        dD_row = jnp.sum(_hi(dy_all * xs, ET), axis=0, keepdims=True)
        sum_mg_b = sum_mg.astype(BF16)
        dB_ref[...] = (dB_acc + lax.dot_general(sum_mg_b, Cm, tn, preferred_element_type=F32)).astype(BF16)
        dC_ref[...] = (dC_acc + jnp.dot(sum_mg_b, Bm, preferred_element_type=F32)).astype(BF16)
        triu = (row <= col).astype(F32)
        ddtA = _dot01_left(triu, da_cols, 3)
        ddt = ddt_x + ddtA * A
        dpre = ddt * _sigmoid(dtp_ref[...] + bias_ref[...])
        ddtp_ref[...] = dpre
        dbias_v = jnp.sum(dpre, axis=0, keepdims=True)
        dalog_v = jnp.sum(ddtA * dt, axis=0, keepdims=True) * A

        @pl.when(ci == 0)
        def _():
            dbias_ref[...] = dbias_v
            dalog_ref[...] = dalog_v
            dD_ref[...] = dD_row

        @pl.when(ci > 0)
        def _():
            dbias_ref[...] += dbias_v
            dalog_ref[...] += dalog_v
            dD_ref[...] += dD_row

    vec_o = pl.BlockSpec((None, 1, Kh), lambda g, c: (g, 0, 0))
    return pl.pallas_call(
        body, grid=(G, nc),
        in_specs=_ssd_in_specs(Q, GP, N, Kh, DI, rev)
        + [pl.BlockSpec((None, None, GP, N), lambda g, c: (rev(c), g, 0, 0)),
           pl.BlockSpec((Q, GP), lambda g, c: (rev(c), g))],
        out_specs=[pl.BlockSpec((Q, GP), lambda g, c: (rev(c), g)), pl.BlockSpec((Q, N), lambda g, c: (rev(c), g)),
                   pl.BlockSpec((Q, N), lambda g, c: (rev(c), g)),
                   pl.BlockSpec((None, Q, Kh), lambda g, c: (g, rev(c), 0)), vec_o, vec_o, vec_o],
        out_shape=[jax.ShapeDtypeStruct((L, DI), BF16), jax.ShapeDtypeStruct((L, G * N), BF16),
                   jax.ShapeDtypeStruct((L, G * N), BF16), jax.ShapeDtypeStruct((G, L, Kh), F32)]
        + [jax.ShapeDtypeStruct((G, 1, Kh), F32)] * 3,
        scratch_shapes=[pltpu.VMEM((GP, N), F32)], compiler_params=_cp(("parallel", "arbitrary")),
        name="ssd_bwd")(xbc, xbc, xbc, dtp_g, dtp_gT, bias_g, bias_gT, alog_g, alog_gT, dsk_g, dsk_gT, states, dy)


def _rms_groups(y2, ng_ref, DI):
    S = DI // SSD_G
    for g in range(SSD_G):
        gs = slice(g * S, (g + 1) * S)
        seg = y2[:, gs]
        r = lax.rsqrt(jnp.mean(seg * seg, axis=-1, keepdims=True) + RMS_EPS)
        yield gs, seg * r, r, ng_ref[:, gs]


def rms_gate_fwd(y, zx, norm_g):
    L, DI = y.shape
    tr = _tile(L, 256, 16)

    def body(y_ref, z_ref, ng_ref, o_ref):
        y2 = y_ref[...].astype(F32) * _silu(z_ref[...].astype(F32))
        for gs, yh, _, ng in _rms_groups(y2, ng_ref, DI):
            o_ref[:, gs] = (yh * ng).astype(BF16)

    return pl.pallas_call(
        body, grid=(L // tr,), in_specs=_row_specs(tr, [DI, DI]) + [_vec_spec(DI)], out_specs=_row_specs(tr, [DI])[0],
        out_shape=jax.ShapeDtypeStruct((L, DI), BF16), compiler_params=_cp(("parallel",)),
        name="rms_gate_fwd")(y, zx, norm_g)


def rms_gate_bwd(dyn, y, zx, norm_g):
    L, DI = y.shape
    tr = _tile(L, 256, 16)

    def body(dyn_ref, y_ref, z_ref, ng_ref, dy_ref, dz_ref, dng_ref):
        i = pl.program_id(0)
        yv = y_ref[...].astype(F32)
        zv = z_ref[...].astype(F32)
        sz = _silu(zv)
        dsz = _dsilu(zv)
        dynv = dyn_ref[...].astype(F32)
        for gs, yh, r, ng in _rms_groups(yv * sz, ng_ref, DI):
            dyh = dynv[:, gs] * ng
            dy2 = r * (dyh - yh * jnp.mean(dyh * yh, axis=-1, keepdims=True))
            dy_ref[:, gs] = (dy2 * sz[:, gs]).astype(BF16)
            dz_ref[:, gs] = (dy2 * yv[:, gs] * dsz[:, gs]).astype(BF16)
            s = jnp.sum(dynv[:, gs] * yh, axis=0, keepdims=True)

            @pl.when(i == 0)
            def _():
                dng_ref[:, gs] = s

            @pl.when(i > 0)
            def _():
                dng_ref[:, gs] += s

    return pl.pallas_call(
        body, grid=(L // tr,), in_specs=_row_specs(tr, [DI, DI, DI]) + [_vec_spec(DI)],
        out_specs=_row_specs(tr, [DI, DI]) + [_vec_spec(DI)],
        out_shape=[jax.ShapeDtypeStruct((L, DI), BF16)] * 2 + [jax.ShapeDtypeStruct((1, DI), F32)],
        compiler_params=_cp(("arbitrary",)), name="rms_gate_bwd")(dyn, y, zx, norm_g)


def _alibi_slope(gi, h):
    n = len(DIL_PATTERNS) * DIL_H
    return float(2.0 ** (-8.0 * (gi * DIL_H + h + 1) / n))


def _attn_masks():
    qi = lax.broadcasted_iota(jnp.int32, (DIL_BLK, DIL_BLK), 0)
    kj = lax.broadcasted_iota(jnp.int32, (DIL_BLK, DIL_BLK), 1)
    dcur = (qi - kj).astype(F32)
    return dcur, qi >= kj, dcur + float(DIL_BLK), kj >= qi


def _dil_cols(arr, col0, d):
    HW = DIL_H * DIL_E
    if d == 1:
        return arr, arr.shape[1] // HW, col0 // HW
    return arr[:, col0:col0 + HW].reshape(arr.shape[0] // d, d * HW), 1, 0


def attn_fwd(qz, kv, gi):
    window, d = DIL_PATTERNS[gi]
    assert window // d == DIL_BLK
    L, QZ = qz.shape
    KV = kv.shape[1]
    HW = DIL_H * DIL_E
    M = L // d
    nb = M // DIL_BLK
    nq, nkv = QZ // HW, KV // HW
    scale = DIL_E ** -0.5
    nt = (((1,), (1,)), ((), ()))

    def body(q_ref, kp_ref, kc_ref, vp_ref, vc_ref, o_ref, lse_ref):
        n = pl.program_id(1)
        dcur, vcur, dprev, vprev0 = _attn_masks()
        dist = jnp.concatenate([dprev, dcur], axis=1)
        valid = jnp.concatenate([jnp.logical_and(vprev0, n > 0), vcur], axis=1)
        lane = lax.broadcasted_iota(jnp.int32, (DIL_BLK, 128), 1)
        lse_acc = jnp.zeros((DIL_BLK, 128), F32)
        for h in range(DIL_H):
            hs = slice(h * DIL_E, (h + 1) * DIL_E)
            sl = _alibi_slope(gi, h) * d
            kcat = jnp.concatenate([kp_ref[:, hs], kc_ref[:, hs]], axis=0)
            vcat = jnp.concatenate([vp_ref[:, hs], vc_ref[:, hs]], axis=0)
            s = lax.dot_general(q_ref[:, hs], kcat, nt, preferred_element_type=F32) * scale - sl * dist
            s = jnp.where(valid, s, -jnp.inf)
            m = jnp.max(s, axis=-1, keepdims=True)
            p = jnp.exp(s - m)
            den = jnp.sum(p, axis=-1, keepdims=True)
            o = jnp.dot(p.astype(BF16), vcat, preferred_element_type=F32) / den
            o_ref[:, hs] = o.astype(BF16)
            lse_acc = jnp.where(lane == h, m + jnp.log(den), lse_acc)
        lse_ref[...] = lse_acc

    blk = (DIL_BLK, HW)
    prev = lambda n: jnp.maximum(n - 1, 0)
    qv, qn, qo = _dil_cols(qz, gi * HW, d)
    kv_, kn, ko = _dil_cols(kv, gi * HW, d)
    vv, vn, vo = _dil_cols(kv, (nkv // 2 + gi) * HW, d)
    o, lse = pl.pallas_call(
        body, grid=(d, nb),
        in_specs=[pl.BlockSpec(blk, lambda r, n: (n, r * qn + qo)),
                  pl.BlockSpec(blk, lambda r, n: (prev(n), r * kn + ko)),
                  pl.BlockSpec(blk, lambda r, n: (n, r * kn + ko)),
                  pl.BlockSpec(blk, lambda r, n: (prev(n), r * vn + vo)),
                  pl.BlockSpec(blk, lambda r, n: (n, r * vn + vo))],
        out_specs=[pl.BlockSpec(blk, lambda r, n: (n, r)), pl.BlockSpec((DIL_BLK, 128), lambda r, n: (n, r))],
        out_shape=[jax.ShapeDtypeStruct((M, d * HW), BF16), jax.ShapeDtypeStruct((M, d * 128), F32)],
        compiler_params=_cp(("parallel", "parallel")), name=f"attn_fwd_{gi}")(qv, kv_, kv_, vv, vv)
    return o.reshape(L, HW), lse.reshape(L, 128)


def attn_bwd(qz, kv, do, lse, dpr, gi):
    window, d = DIL_PATTERNS[gi]
    L, QZ = qz.shape
    KV = kv.shape[1]
    HW = DIL_H * DIL_E
    M = L // d
    nb = M // DIL_BLK
    nq, nkv = QZ // HW, KV // HW
    scale = DIL_E ** -0.5
    nt = (((1,), (1,)), ((), ()))
    tn = (((0,), (0,)), ((), ()))

    def body(q0_ref, q1_ref, k_ref, v_ref, do0_ref, do1_ref, l0_ref, l1_ref, r0_ref, r1_ref,
             dq_ref, dk_ref, dv_ref, carry):
        n = pl.program_id(1)

        @pl.when(n == 0)
        def _():
            carry[...] = jnp.zeros(carry.shape, F32)

        dcur, vcur, dprev, vprev0 = _attn_masks()
        dist = jnp.concatenate([dcur, dprev], axis=0)
        valid = jnp.concatenate([vcur, jnp.logical_and(vprev0, n < nb - 1)], axis=0)
        B = DIL_BLK
        for h in range(DIL_H):
            hs = slice(h * DIL_E, (h + 1) * DIL_E)
            sl = _alibi_slope(gi, h) * d
            kh = k_ref[:, hs]
            vh = v_ref[:, hs]
            qcat = jnp.concatenate([q0_ref[:, hs], q1_ref[:, hs]], axis=0)
            docat = jnp.concatenate([do0_ref[:, hs], do1_ref[:, hs]], axis=0)
            lcat = jnp.concatenate([l0_ref[:, h:h + 1], l1_ref[:, h:h + 1]], axis=0)
            rcat = jnp.concatenate([r0_ref[:, h:h + 1], r1_ref[:, h:h + 1]], axis=0)
            s = lax.dot_general(qcat, kh, nt, preferred_element_type=F32) * scale - sl * dist
            p = jnp.exp(jnp.where(valid, s - lcat, -jnp.inf))
            ds = p * (lax.dot_general(docat, vh, nt, preferred_element_type=F32) - rcat)
            ds_b = (ds * scale).astype(BF16)
            dv_ref[:, hs] = lax.dot_general(p.astype(BF16), docat, tn, preferred_element_type=F32).astype(BF16)
            dk_ref[:, hs] = lax.dot_general(ds_b, qcat, tn, preferred_element_type=F32).astype(BF16)
            dqc = jnp.dot(ds_b, kh, preferred_element_type=F32)
            dq_ref[:, hs] = (carry[:, hs] + dqc[:B]).astype(BF16)
            carry[:, hs] = dqc[B:]

    blk = (DIL_BLK, HW)
    sblk = (DIL_BLK, 128)
    nxt = lambda n: jnp.minimum(n + 1, nb - 1)
    qv, qn, qo = _dil_cols(qz, gi * HW, d)
    kv_, kn, ko = _dil_cols(kv, gi * HW, d)
    vv, vn, vo = _dil_cols(kv, (nkv // 2 + gi) * HW, d)
    dov = do.reshape(M, d * HW)
    lv = lse.reshape(M, d * 128)
    rv = dpr.reshape(M, d * 128)
    outs = pl.pallas_call(
        body, grid=(d, nb),
        in_specs=[pl.BlockSpec(blk, lambda r, n: (n, r * qn + qo)), pl.BlockSpec(blk, lambda r, n: (nxt(n), r * qn + qo)),
                  pl.BlockSpec(blk, lambda r, n: (n, r * kn + ko)),
                  pl.BlockSpec(blk, lambda r, n: (n, r * vn + vo)),
                  pl.BlockSpec(blk, lambda r, n: (n, r)), pl.BlockSpec(blk, lambda r, n: (nxt(n), r)),
                  pl.BlockSpec(sblk, lambda r, n: (n, r)), pl.BlockSpec(sblk, lambda r, n: (nxt(n), r)),
                  pl.BlockSpec(sblk, lambda r, n: (n, r)), pl.BlockSpec(sblk, lambda r, n: (nxt(n), r))],
        out_specs=[pl.BlockSpec(blk, lambda r, n: (n, r))] * 3,
        out_shape=[jax.ShapeDtypeStruct((M, d * HW), BF16)] * 3,
        scratch_shapes=[pltpu.VMEM(blk, F32)], compiler_params=_cp(("parallel", "arbitrary")),
        name=f"attn_bwd_{gi}")(qv, qv, kv_, vv, dov, dov, lv, lv, rv, rv)
    return [t.reshape(L, HW) for t in outs]


def _merge_weights(l_refs, h):
    ls = [r[:, h:h + 1] for r in l_refs]
    mx = functools.reduce(jnp.maximum, ls)
    es = [jnp.exp(l - mx) for l in ls]
    den = functools.reduce(lambda a, b: a + b, es)
    return [e / den for e in es]


def merge_fwd(os_, lses, qz):
    L, HW = os_[0].shape
    tr = _tile(L, 256, 16)
    ng = len(os_)
    zblk = qz.shape[1] // HW - 1

    def body(*refs):
        o_refs, l_refs, z_ref, out_ref = refs[:ng], refs[ng:2 * ng], refs[2 * ng], refs[2 * ng + 1]
        for h in range(DIL_H):
            hs = slice(h * DIL_E, (h + 1) * DIL_E)
            ws = _merge_weights(l_refs, h)
            om = functools.reduce(lambda a, b: a + b, [w * o[:, hs].astype(F32) for w, o in zip(ws, o_refs)])
            out_ref[:, hs] = (om * _silu(z_ref[:, hs].astype(F32))).astype(BF16)

    return pl.pallas_call(
        body, grid=(L // tr,),
        in_specs=_row_specs(tr, [HW] * ng + [128] * ng) + [pl.BlockSpec((tr, HW), lambda i: (i, zblk))],
        out_specs=_row_specs(tr, [HW])[0], out_shape=jax.ShapeDtypeStruct((L, HW), BF16),
        compiler_params=_cp(("parallel",)), name="merge_fwd")(*os_, *lses, qz)


def merge_bwd(dgated, os_, lses, qz):
    L, HW = os_[0].shape
    tr = _tile(L, 256, 16)
    ng = len(os_)
    zblk = qz.shape[1] // HW - 1

    def body(*refs):
        dg_ref = refs[0]
        o_refs, l_refs, z_ref = refs[1:1 + ng], refs[1 + ng:1 + 2 * ng], refs[1 + 2 * ng]
        outs = refs[2 + 2 * ng:]
        do_refs, dpr_refs, dz_ref = outs[:ng], outs[ng:2 * ng], outs[2 * ng]
        lane = lax.broadcasted_iota(jnp.int32, (tr, 128), 1)
        accs = [jnp.zeros((tr, 128), F32) for _ in range(ng)]
        for h in range(DIL_H):
            hs = slice(h * DIL_E, (h + 1) * DIL_E)
            ws = _merge_weights(l_refs, h)
            ov = [o[:, hs].astype(F32) for o in o_refs]
            om = functools.reduce(lambda a, b: a + b, [w * o for w, o in zip(ws, ov)])
            zv = z_ref[:, hs].astype(F32)
            dgv = dg_ref[:, hs].astype(F32)
            dom = dgv * _silu(zv)
            dz_ref[:, hs] = (dgv * om * _dsilu(zv)).astype(BF16)
            dws = [jnp.sum(dom * o, axis=-1, keepdims=True) for o in ov]
            dwbar = functools.reduce(lambda a, b: a + b, [w * dw for w, dw in zip(ws, dws)])
            for g in range(ng):
                do_refs[g][:, hs] = (ws[g] * dom).astype(BF16)
                accs[g] = jnp.where(lane == h, ws[g] * dwbar, accs[g])
        for g in range(ng):
            dpr_refs[g][...] = accs[g]

    outs = pl.pallas_call(
        body, grid=(L // tr,),
        in_specs=_row_specs(tr, [HW] * (1 + ng) + [128] * ng) + [pl.BlockSpec((tr, HW), lambda i: (i, zblk))],
        out_specs=_row_specs(tr, [HW] * ng + [128] * ng + [HW]),
        out_shape=[jax.ShapeDtypeStruct((L, HW), BF16)] * ng + [jax.ShapeDtypeStruct((L, 128), F32)] * ng
        + [jax.ShapeDtypeStruct((L, HW), BF16)],
        compiler_params=_cp(("parallel",)), name="merge_bwd")(dgated, *os_, *lses, qz)
    return outs[:ng], outs[ng:2 * ng], outs[2 * ng]


def ada_fwd(c8, ada_w):
    nl, D, Ws = ada_w.shape
    tn = _tile(Ws, 512)

    def body(c_ref, w_ref, o_ref):
        o_ref[...] = jnp.dot(_silu(c_ref[...]), w_ref[...], precision=lax.Precision.HIGHEST,
                             preferred_element_type=F32)

    return pl.pallas_call(
        body, grid=(nl, Ws // tn),
        in_specs=[pl.BlockSpec((N_DEV, D), lambda l, j: (0, 0)), pl.BlockSpec((None, D, tn), lambda l, j: (l, 0, j))],
        out_specs=pl.BlockSpec((None, N_DEV, tn), lambda l, j: (l, 0, j)),
        out_shape=jax.ShapeDtypeStruct((nl, N_DEV, Ws), F32), compiler_params=_cp(("parallel", "parallel")),
        name="ada_fwd")(c8, ada_w)


def ada_wgrad(c8t, dmod):
    nl, _, Ws = dmod.shape
    D = c8t.shape[0]
    tm = _tile(D, 512, 8)

    def body(c_ref, d_ref, o_ref):
        sc = _silu(c_ref[...])
        acc = sc[:, 0:1] * d_ref[0:1, :]
        for e in range(1, N_DEV):
            acc = acc + sc[:, e:e + 1] * d_ref[e:e + 1, :]
        o_ref[...] = acc

    return pl.pallas_call(
        body, grid=(nl, D // tm),
        in_specs=[pl.BlockSpec((tm, N_DEV), lambda l, i: (i, 0)), pl.BlockSpec((None, N_DEV, Ws), lambda l, i: (l, 0, 0))],
        out_specs=pl.BlockSpec((None, tm, Ws), lambda l, i: (l, i, 0)),
        out_shape=jax.ShapeDtypeStruct((nl, D, Ws), F32), compiler_params=_cp(("parallel", "parallel")),
        name="ada_wgrad")(c8t, dmod)


def adamw(w, g, m, v, name):
    R, C = w.shape
    tr = _tile(R, 256, 8)
    c1 = 1.0 - ADAM_B1 ** ADAM_STEP
    c2 = 1.0 - ADAM_B2 ** ADAM_STEP

    def body(w_ref, g_ref, m_ref, v_ref, d_ref, nm_ref, nv_ref):
        gv = g_ref[...]
        nm = ADAM_B1 * m_ref[...] + (1.0 - ADAM_B1) * gv
        nv = ADAM_B2 * v_ref[...] + (1.0 - ADAM_B2) * (gv * gv)
        nm_ref[...] = nm
        nv_ref[...] = nv
        d_ref[...] = -ADAM_LR * ((nm / c1) / (jnp.sqrt(nv / c2) + ADAM_EPS) + ADAM_WD * w_ref[...])

    return pl.pallas_call(
        body, grid=(R // tr,), in_specs=_row_specs(tr, [C] * 4), out_specs=_row_specs(tr, [C] * 3),
        out_shape=[jax.ShapeDtypeStruct((R, C), F32)] * 3, compiler_params=_cp(("parallel",)), name=name)(w, g, m, v)


def sum_leading(t, name, out_dtype=F32):
    S, R, C = t.shape
    tr = _tile(R, 256, 16)

    def body(t_ref, o_ref):
        acc = t_ref[0].astype(F32)
        for s in range(1, S):
            acc = acc + t_ref[s].astype(F32)
        o_ref[...] = acc.astype(out_dtype)

    return pl.pallas_call(
        body, grid=(R // tr,), in_specs=[pl.BlockSpec((S, tr, C), lambda i: (0, i, 0))],
        out_specs=pl.BlockSpec((tr, C), lambda i: (i, 0)), out_shape=jax.ShapeDtypeStruct((R, C), out_dtype),
        compiler_params=_cp(("parallel",)), name=name)(t)


def add_half(g, a, core, name):
    S, R, C = g.shape
    h = R // 2
    tr = _tile(h, 256, 16)
    nb = h // tr

    def body(core_ref, g_ref, a_ref, o_ref):
        o_ref[...] = (g_ref[...].astype(F32) + a_ref[...].astype(F32)).astype(BF16)

    return pl.pallas_call(
        body,
        grid_spec=pltpu.PrefetchScalarGridSpec(
            num_scalar_prefetch=1, grid=(S, nb),
            in_specs=[pl.BlockSpec((None, tr, C), lambda s, i, core_ref: (s, core_ref[0] * nb + i, 0)),
                      pl.BlockSpec((None, tr, C), lambda s, i, core_ref: (s, i, 0))],
            out_specs=pl.BlockSpec((None, tr, C), lambda s, i, core_ref: (s, i, 0))),
        out_shape=jax.ShapeDtypeStruct((S, h, C), BF16), compiler_params=_cp(("parallel", "parallel")),
        name=name)(core, g, a)


def sum_partials(own, landed, chip, name):
    _, h, C = own.shape
    tr = _tile(h, 256, 16)

    def body(chip_ref, own_ref, l_ref, o_ref):
        acc = own_ref[...].astype(F32)
        for j in range(3):
            acc = acc + l_ref[j].astype(F32)
        o_ref[...] = acc

    return pl.pallas_call(
        body,
        grid_spec=pltpu.PrefetchScalarGridSpec(
            num_scalar_prefetch=1, grid=(h // tr,),
            in_specs=[pl.BlockSpec((None, tr, C), lambda i, chip_ref: (chip_ref[0], i, 0)),
                      pl.BlockSpec((3, tr, C), lambda i, chip_ref: (0, i, 0))],
            out_specs=pl.BlockSpec((tr, C), lambda i, chip_ref: (i, 0))),
        out_shape=jax.ShapeDtypeStruct((h, C), F32), compiler_params=_cp(("parallel",)), name=name)(chip, own, landed)


def adamw_halves(w, g_mine, g_theirs, m, v, core, name):
    R, C = w.shape
    h = R // 2
    tr = _tile(h, 256, 8)
    nbh = h // tr
    c1 = 1.0 - ADAM_B1 ** ADAM_STEP
    c2 = 1.0 - ADAM_B2 ** ADAM_STEP

    def body(core_ref, w_ref, gm_ref, gt_ref, m_ref, v_ref, g_ref, d_ref, nm_ref, nv_ref):
        mine = (pl.program_id(0) // nbh) == core_ref[0]
        gv = jnp.where(mine, gm_ref[...], gt_ref[...])
        g_ref[...] = gv
        nm = ADAM_B1 * m_ref[...] + (1.0 - ADAM_B1) * gv
        nv = ADAM_B2 * v_ref[...] + (1.0 - ADAM_B2) * (gv * gv)
        nm_ref[...] = nm
        nv_ref[...] = nv
        d_ref[...] = -ADAM_LR * ((nm / c1) / (jnp.sqrt(nv / c2) + ADAM_EPS) + ADAM_WD * w_ref[...])

    full = pl.BlockSpec((tr, C), lambda i, core_ref: (i, 0))
    halfspec = pl.BlockSpec((tr, C), lambda i, core_ref: (i % nbh, 0))
    return pl.pallas_call(
        body,
        grid_spec=pltpu.PrefetchScalarGridSpec(
            num_scalar_prefetch=1, grid=(2 * nbh,), in_specs=[full, halfspec, halfspec, full, full],
            out_specs=[full] * 4),
        out_shape=[jax.ShapeDtypeStruct((R, C), F32)] * 4, compiler_params=_cp(("parallel",)),
        name=name)(core, w, g_mine, g_theirs, m, v)


_ANY = pl.BlockSpec(memory_space=pl.ANY)


def _place():
    x, y, c = lax.axis_index("x"), lax.axis_index("y"), lax.axis_index("c")
    chips = [(1 - x, y), (x, 1 - y), (1 - x, 1 - y)]
    return x, y, c, chips


def allgather_small(v, name, after=None):
    R, W = v.shape
    extra = [] if after is None else [after]

    def body(x_ref, *rest):
        out_ref, send_sems, recv_sems, local_sem = rest[len(extra):]
        x, y, c, chips = _place()
        me, sibling = (x, y, c), (x, y, 1 - c)

        def rows(px, py, pc):
            return out_ref.at[pl.ds((4 * px + 2 * py + pc) * R, R), :]

        def copy(k, block, to, src=None):
            return pltpu.make_async_remote_copy(
                src_ref=rows(*block) if src is None else src, dst_ref=rows(*block),
                send_sem=send_sems.at[k], recv_sem=recv_sems.at[k], device_id=to, device_id_type=MESH)

        mine = pltpu.make_async_copy(x_ref, rows(*me), local_sem)
        mine.start()
        first = [copy(0, me, sibling, src=x_ref)]
        first += [copy(1 + j, me, (*chip, c), src=x_ref) for j, chip in enumerate(chips)]
        for cp in first:
            cp.start()
        passed = [copy(4 + j, (*chip, c), sibling) for j, chip in enumerate(chips)]
        for j, chip in enumerate(chips):
            copy(1 + j, (*chip, c), me).wait_recv()
            passed[j].start()
        copy(0, sibling, me).wait_recv()
        for j, chip in enumerate(chips):
            copy(4 + j, (*chip, 1 - c), me).wait_recv()
        for cp in first + passed:
            cp.wait_send()
        mine.wait()

    return pl.pallas_call(
        body, out_shape=jax.ShapeDtypeStruct((N_DEV * R, W), v.dtype),
        in_specs=[pl.BlockSpec(memory_space=pltpu.VMEM)] + [_ANY] * len(extra),
        out_specs=pl.BlockSpec(memory_space=pltpu.VMEM),
        scratch_shapes=[pltpu.SemaphoreType.DMA((7,)), pltpu.SemaphoreType.DMA((7,)), pltpu.SemaphoreType.DMA],
        name=name)(v, *extra)


def allgather_weights(shards, name="allgather_weights"):
    n = len(shards)

    def body(*refs):
        ins, outs = refs[:n], refs[n:2 * n]
        send_sems, recv_sems = refs[2 * n:]
        x, y, c, chips = _place()
        p = 2 * x + y
        sibling = (x, y, 1 - c)

        def half(i, chip_id, core, ref=None):
            r = outs[i].at[chip_id] if ref is None else ref
            return r.at[core]

        def copy(i, k, chip_id, core, to, src=None):
            return pltpu.make_async_remote_copy(
                src_ref=half(i, chip_id, core) if src is None else src, dst_ref=half(i, chip_id, core),
                send_sem=send_sems.at[6 * i + k], recv_sem=recv_sems.at[6 * i + k], device_id=to, device_id_type=MESH)

        first = [copy(i, j, p, c, (*chip, c), src=half(i, p, c, ref=ins[i]))
                 for i in range(n) for j, chip in enumerate(chips)]
        for cp in first:
            cp.start()
        passed = []
        for i in range(n):
            for j, (cx, cy) in enumerate(chips):
                copy(i, j, 2 * cx + cy, c, sibling).wait_recv()
                fw = copy(i, 3 + j, 2 * cx + cy, c, sibling)
                fw.start()
                passed.append(fw)
        for i in range(n):
            for j, (cx, cy) in enumerate(chips):
                copy(i, 3 + j, 2 * cx + cy, 1 - c, sibling).wait_recv()
        for cp in first + passed:
            cp.wait_send()

    split = [s.reshape(2, s.shape[0] // 2, s.shape[1]) for s in shards]
    outs = pl.pallas_call(
        body, out_shape=[jax.ShapeDtypeStruct((N_CHIPS,) + s.shape, s.dtype) for s in split],
        in_specs=[_ANY] * n, out_specs=[_ANY] * n,
        scratch_shapes=[pltpu.SemaphoreType.DMA((6 * n,)), pltpu.SemaphoreType.DMA((6 * n,))],
        name=name)(*split)
    chip = 2 * lax.axis_index("x") + lax.axis_index("y")
    return [lax.dynamic_update_index_in_dim(o, s, chip, 0).reshape((N_CHIPS,) + sh.shape)
            for o, s, sh in zip(outs, split, shards)]


_HBM = pl.BlockSpec(memory_space=pltpu.HBM)
_SEM = pl.BlockSpec(memory_space=pltpu.SEMAPHORE)
_EFFECT = pltpu.SideEffectType.DATAFLOW_SIDE_EFFECTING


def _chip_copies(kind, srcs, lands, send_sems, recv_sems):
    x, y, c, chips = _place()
    p = 2 * x + y
    cps = []
    for i in range(len(srcs)):
        for j, (cx, cy) in enumerate(chips):
            if kind == "gather":
                src, dst = srcs[i].at[c], lands[i].at[p, c]
            else:
                src, dst = srcs[i].at[2 * cx + cy], lands[i].at[j]
            cps.append(pltpu.make_async_remote_copy(
                src_ref=src, dst_ref=dst, send_sem=send_sems.at[3 * i + j], recv_sem=recv_sems.at[3 * i + j],
                device_id=(cx, cy, c), device_id_type=MESH))
    return cps


def split_start(kind, srcs, land_shapes, after, name):
    n = len(srcs)

    def body(*refs):
        src_refs, land_refs = refs[:n], refs[n:2 * n]
        send_sems, recv_sems = refs[2 * n + 1], refs[2 * n + 2]
        token = refs[-1]
        for cp in _chip_copies(kind, src_refs, land_refs, send_sems, recv_sems):
            cp.start()
        token[...] = jnp.zeros_like(token)

    lands = [pltpu.with_memory_space_constraint(lax.empty(s, BF16), pltpu.HBM) for s in land_shapes]
    outs = pl.pallas_call(
        body, name=name,
        out_shape=(pltpu.SemaphoreType.DMA((3 * n,)), pltpu.SemaphoreType.DMA((3 * n,)),
                   *[pltpu.HBM(s.shape, s.dtype) for s in srcs], *[pltpu.HBM(s, BF16) for s in land_shapes],
                   jax.ShapeDtypeStruct((8, 128), F32)),
        in_specs=[_HBM] * (2 * n) + [_ANY],
        out_specs=(_SEM, _SEM, *([_HBM] * (2 * n)), pl.BlockSpec(memory_space=pltpu.VMEM)),
        input_output_aliases={i: 2 + i for i in range(2 * n)},
        compiler_params=pltpu.CompilerParams(has_side_effects=_EFFECT),
    )(*[pltpu.with_memory_space_constraint(s, pltpu.HBM) for s in srcs], *lands, after)
    return outs[0], outs[1], outs[2:2 + n], outs[2 + n:2 + 2 * n], outs[-1]


def split_wait(kind, send_sems, recv_sems, srcs, lands, after, name):
    n = len(srcs)

    def body(*refs):
        src_refs, land_refs = refs[:n], refs[n:2 * n]
        ssem, rsem = refs[2 * n], refs[2 * n + 1]
        for cp in _chip_copies(kind, src_refs, land_refs, ssem, rsem):
            cp.wait_send()
            cp.wait_recv()

    outs = pl.pallas_call(
        body, name=name,
        out_shape=[pltpu.HBM(s.shape, s.dtype) for s in srcs] + [pltpu.HBM(s.shape, s.dtype) for s in lands],
        in_specs=[_HBM] * (2 * n) + [_SEM, _SEM, _ANY], out_specs=[_HBM] * (2 * n),
        input_output_aliases={i: i for i in range(2 * n)},
        compiler_params=pltpu.CompilerParams(has_side_effects=_EFFECT),
    )(*srcs, *lands, send_sems, recv_sems, after)
    return outs[:n], outs[n:]


def pass_to_sibling(lands):
    n = len(lands)

    def body(*refs):
        ins, outs = refs[:n], refs[n:2 * n]
        send_sems, recv_sems = refs[2 * n:]
        x, y, c, chips = _place()
        cps = []
        for i in range(n):
            for j, (cx, cy) in enumerate(chips):
                blk = outs[i].at[2 * cx + cy, c]
                cps.append(pltpu.make_async_remote_copy(
                    src_ref=ins[i].at[2 * cx + cy, c], dst_ref=blk, send_sem=send_sems.at[3 * i + j],
                    recv_sem=recv_sems.at[3 * i + j], device_id=(x, y, 1 - c), device_id_type=MESH))
        for cp in cps:
            cp.start()
        for cp in cps:
            cp.wait()

    return pl.pallas_call(
        body, out_shape=[jax.ShapeDtypeStruct(t.shape, t.dtype) for t in lands], in_specs=[_ANY] * n,
        out_specs=[_ANY] * n, input_output_aliases={i: i for i in range(n)},
        scratch_shapes=[pltpu.SemaphoreType.DMA((3 * n,)), pltpu.SemaphoreType.DMA((3 * n,))],
        name="ag_pass_to_sibling")(*lands)


def exchange_halves_to_sibling(gs, name):
    n = len(gs)

    def body(*refs):
        ins, outs = refs[:n], refs[n:2 * n]
        send_sems, recv_sems = refs[2 * n:]
        x, y, c, _ = _place()
        cps = []
        for i in range(n):
            h = ins[i].shape[1] // 2
            cps.append(pltpu.make_async_remote_copy(
                src_ref=ins[i].at[:, pl.ds((1 - c) * h, h), :], dst_ref=outs[i],
                send_sem=send_sems.at[i], recv_sem=recv_sems.at[i], device_id=(x, y, 1 - c), device_id_type=MESH))
        for cp in cps:
            cp.start()
        for cp in cps:
            cp.wait()

    return pl.pallas_call(
        body, out_shape=[jax.ShapeDtypeStruct((g.shape[0], g.shape[1] // 2, g.shape[2]), g.dtype) for g in gs],
        in_specs=[_ANY] * n, out_specs=[_ANY] * n,
        scratch_shapes=[pltpu.SemaphoreType.DMA((n,)), pltpu.SemaphoreType.DMA((n,))],
        name=name)(*gs)


def scatter_to_chips(ps, name):
    n = len(ps)

    def body(*refs):
        ins, outs = refs[:n], refs[n:2 * n]
        send_sems, recv_sems = refs[2 * n:]
        x, y, c, chips = _place()
        cps = []
        for i in range(n):
            for j, (cx, cy) in enumerate(chips):
                cps.append(pltpu.make_async_remote_copy(
                    src_ref=ins[i].at[2 * cx + cy], dst_ref=outs[i].at[j], send_sem=send_sems.at[3 * i + j],
                    recv_sem=recv_sems.at[3 * i + j], device_id=(cx, cy, c), device_id_type=MESH))
        for cp in cps:
            cp.start()
        for cp in cps:
            cp.wait()

    return pl.pallas_call(
        body, out_shape=[jax.ShapeDtypeStruct((3,) + t.shape[1:], t.dtype) for t in ps],
        in_specs=[_ANY] * n, out_specs=[_ANY] * n,
        scratch_shapes=[pltpu.SemaphoreType.DMA((3 * n,)), pltpu.SemaphoreType.DMA((3 * n,))],
        name=name)(*ps)


def join_halves(rs, name):
    n = len(rs)

    def body(*refs):
        ins, outs = refs[:n], refs[n:2 * n]
        send_sems, recv_sems = refs[2 * n:]
        x, y, c, _ = _place()
        cps = [pltpu.make_async_remote_copy(
            src_ref=ins[i], dst_ref=outs[i], send_sem=send_sems.at[i], recv_sem=recv_sems.at[i],
            device_id=(x, y, 1 - c), device_id_type=MESH) for i in range(n)]
        for cp in cps:
            cp.start()
        for cp in cps:
            cp.wait()

    return pl.pallas_call(
        body, out_shape=[jax.ShapeDtypeStruct(r.shape, r.dtype) for r in rs],
        in_specs=[_ANY] * n, out_specs=[_ANY] * n,
        scratch_shapes=[pltpu.SemaphoreType.DMA((n,)), pltpu.SemaphoreType.DMA((n,))],
        name=name)(*rs)


def _pack(parts, row_mult=8):
    flat = jnp.concatenate([p.reshape(-1).astype(F32) for p in parts])
    unit = row_mult * 128
    n = -(-flat.shape[0] // unit) * unit
    return jnp.pad(flat, (0, n - flat.shape[0])).reshape(n // 128, 128)


def _unpack(flat, shapes):
    out, off = [], 0
    for s in shapes:
        n = int(np.prod(s))
        out.append(flat[off:off + n].reshape(s))
        off += n
    return out


def _gather_packed(parts, name):
    packed = _pack(parts)
    g = allgather_small(packed, name).reshape(N_DEV, -1)
    return _unpack_rows(g, [p.shape for p in parts])


def _unpack_rows(g, shapes):
    out, off = [], 0
    for s in shapes:
        n = int(np.prod(s))
        out.append(g[:, off:off + n].reshape((g.shape[0],) + tuple(s)))
        off += n
    return out


def _by_chip(t, axis):
    return jnp.concatenate([t[2 * p] for p in range(N_CHIPS)], axis=axis)


def kernel(x, c, ada_w, ada_b, ln_g, ln_b, a_in_w, a_conv_w, a_conv_b, a_dt_bias, a_A_log, a_D, a_norm_g, a_out_w, kv_w, b_in_w, b_out_w, loss_target, m_ada_w, m_ada_b, m_ln_g, m_ln_b, m_a_in_w, m_a_conv_w, m_a_conv_b, m_a_dt_bias, m_a_A_log, m_a_D, m_a_norm_g, m_a_out_w, m_kv_w, m_b_in_w, m_b_out_w, v_ada_w, v_ada_b, v_ln_g, v_ln_b, v_a_in_w, v_a_conv_w, v_a_conv_b, v_a_dt_bias, v_a_A_log, v_a_D, v_a_norm_g, v_a_out_w, v_kv_w, v_b_in_w, v_b_out_w):
    ax, ay, ac = lax.axis_index("x"), lax.axis_index("y"), lax.axis_index("c")
    chip = 2 * ax + ay
    dev = 4 * ax + 2 * ay + ac
    xin = x[0]
    tgt = loss_target[0]
    L, D = xin.shape
    G, P = SSD_G, SSD_P
    H = a_dt_bias.shape[1]
    Kh = H // G
    DI = H * P
    CONVD = a_conv_b.shape[1] * N_CHIPS
    HW = DIL_H * DIL_E
    Ws = ada_w.shape[2]

    (w_in_g,) = allgather_weights([a_in_w[0].astype(BF16)], "allgather_w_in")
    later = [a_out_w[0].astype(BF16), kv_w.astype(BF16), b_in_w[0].astype(BF16), b_out_w[0].astype(BF16)]
    later_split = [s.reshape(2, s.shape[0] // 2, s.shape[1]) for s in later]
    ag_ssem, ag_rsem, ag_srcs, ag_lands, ag_token = split_start(
        "gather", later_split, [(N_CHIPS,) + s.shape for s in later_split], w_in_g, "ag_later_start")
    w_in = jnp.transpose(w_in_g, (1, 0, 2)).reshape(D, -1)
    w_zx = w_in
    w_dt = jnp.pad(w_in[:, DI + CONVD:], ((0, 0), (0, 128 - H)))

    c8, cw8, cb8, ng8 = _gather_packed([c[0], a_conv_w[0], a_conv_b[0], a_norm_g[0]], "allgather_small_params")
    conv_w = _by_chip(cw8, 1)
    conv_b = _by_chip(cb8, 0).reshape(1, CONVD)
    norm_g = _by_chip(ng8, 0).reshape(1, DI)

    mod_s = ada_fwd(c8, ada_w)
    (mod8,) = _gather_packed([mod_s], "allgather_small_mod")
    mods = _by_chip(mod8, 2)
    mod = lax.dynamic_index_in_dim(mods, dev, axis=1, keepdims=False) + ada_b
    shift = [mod[l:l + 1, :D] for l in range(DEPTH)]
    scale = [mod[l:l + 1, D:2 * D] for l in range(DEPTH)]
    gate = [mod[l:l + 1, 2 * D:] for l in range(DEPTH)]
    lg = [ln_g[l:l + 1] for l in range(DEPTH)]
    lb = [ln_b[l:l + 1] for l in range(DEPTH)]

    h0 = modulate(xin, scale[0] + ag_token[0:1, 0:1], shift[0], "modulate0")
    zx = mm_nn(h0, w_zx, BF16, "mm_in_zx", n_cols=DI + CONVD)
    dtp = mm_nn(h0, w_dt, F32, "mm_in_dt")
    xbc = conv_fwd(zx, DI, conv_w, conv_b)
    dtp_g = jnp.transpose(dtp[:, :H].reshape(L, G, Kh), (1, 0, 2))
    dtp_gT = jnp.transpose(dtp_g, (0, 2, 1))
    vecs = [a_dt_bias.reshape(G, 1, Kh), a_dt_bias.reshape(G, Kh, 1), a_A_log.reshape(G, 1, Kh),
            a_A_log.reshape(G, Kh, 1), a_D.reshape(G, 1, Kh), a_D.reshape(G, Kh, 1)]
    y_ssd, states = ssd_fwd(xbc, dtp_g, dtp_gT, *vecs, DI)
    yn = rms_gate_fwd(y_ssd, zx, norm_g)
    later_split, ag_lands = split_wait("gather", ag_ssem, ag_rsem, ag_srcs, ag_lands, yn, "ag_later_wait")
    ag_lands = pass_to_sibling(ag_lands)
    w_out_g, w_kv_g, w_bin_g, w_bout_g = [
        lax.dynamic_update_index_in_dim(o, s, chip, 0).reshape((N_CHIPS,) + full.shape)
        for o, s, full in zip(ag_lands, later_split, later)]
    ymix0 = mm_nn(yn, w_out_g, F32, "mm_out_a", stack="row")
    x1, x1b, h1 = ln_mid(xin, ymix0, gate[0], lg[0], lb[0], scale[1], shift[1])

    kvp = mm_nn(x1b, w_kv_g, BF16, "mm_kv", stack="col")
    qz = mm_nn(h1, w_bin_g, BF16, "mm_in_b", stack="col")
    os_, lses = [], []
    for gi in range(len(DIL_PATTERNS)):
        o, lse = attn_fwd(qz, kvp, gi)
        os_.append(o)
        lses.append(lse)
    om = merge_fwd(os_, lses, qz)
    ymix1 = mm_nn(om, w_bout_g, F32, "mm_out_b", stack="col")
    dres2, dy2, dg1, db1, dgate1, sq = ln_final_fwd_bwd(x1, ymix1, gate[1], lg[1], lb[1], tgt)
    loss_part = 0.5 * jnp.sum(sq) / D

    g_bout = mm_tn(om, dy2, BF16, "mm_gw_out_b", stack="col")
    dgated = mm_nt(dy2, w_bout_g, BF16, "mm_gx_out_b", stack="col")
    dos, dprs, dz_b = merge_bwd(dgated, os_, lses, qz)
    dqs, dks, dvs = [], [], []
    for gi in range(len(DIL_PATTERNS)):
        dq, dk, dv = attn_bwd(qz, kvp, dos[gi], lses[gi], dprs[gi], gi)
        dqs.append(dq)
        dks.append(dk)
        dvs.append(dv)
    dqz = jnp.concatenate(dqs + [dz_b], axis=1)
    dkv = jnp.concatenate(dks + dvs, axis=1)
    g_bin = mm_tn(h1, dqz, BF16, "mm_gw_in_b", stack="col")
    dh1 = mm_nt(dqz, w_bin_g, F32, "mm_gx_in_b", stack="col")
    g_kv = mm_tn(x1b, dkv, BF16, "mm_gw_kv", stack="col")
    dx1_kv = mm_nt(dkv, w_kv_g, F32, "mm_gx_kv", stack="col")

    core = ac.astype(jnp.int32).reshape(1)
    chip_i = chip.astype(jnp.int32).reshape(1)

    def begin_scatter(gs, nms, tag):
        sib = exchange_halves_to_sibling(gs, "rs_sibling_exchange_" + tag)
        parts = [add_half(g, a, core, "rs_add_" + nm) for g, a, nm in zip(gs, sib, nms)]
        return split_start("scatter", parts, [(3,) + t.shape[1:] for t in parts], parts[0], "rs_%s_start" % tag)

    def finish_scatter(handles, after, tag):
        nms, owns, landed = [], [], []
        for k, (handle, hn) in enumerate(handles):
            parts, lands = split_wait("scatter", handle[0], handle[1], handle[2], handle[3], after,
                                      "rs_%s%d_wait" % (tag, k))
            nms += hn
            owns += list(parts)
            landed += list(lands)
        halves = [sum_partials(own, t, chip_i, "rs_sum_" + nm) for own, t, nm in zip(owns, landed, nms)]
        theirs = join_halves(halves, "rs_join_halves_" + tag)
        return dict(zip(nms, zip(halves, theirs)))

    names_b = ["kv", "in_b", "out_b"]
    rs_b = begin_scatter([g_kv, g_bin, g_bout], names_b, "b")

    dres1, dy1, dg0, db0, dgate0, dscale1, dshift1 = mod_ln_bwd(
        dres2, dh1, dx1_kv, x1, scale[1], xin, ymix0, gate[0] + rs_b[4][0:1, 0:1], lg[0])
    g_out = mm_tn(yn, dy1, BF16, "mm_gw_out_a", stack="row")
    rs_a1 = begin_scatter([g_out], ["out_a"], "a1")
    dyn = mm_nt(dy1, w_out_g, BF16, "mm_gx_out_a", stack="row")
    dy_ssd, dz_a, dnorm_g = rms_gate_bwd(dyn, y_ssd, zx, norm_g + rs_a1[4][0:1, 0:1])
    dxs, dB, dC, ddtp_g, dbias_g, dalog_g, dD_g = ssd_bwd(xbc, dtp_g, dtp_gT, *vecs, states, dy_ssd, DI)
    dxbc = jnp.concatenate([dxs, dB, dC], axis=1)
    dxbc_pre, dconv_w, dconv_b = conv_bwd(zx, DI, conv_w, conv_b, dxbc)
    dzx = jnp.concatenate([dz_a, dxbc_pre], axis=1)
    ddtp = jnp.pad(jnp.transpose(ddtp_g, (1, 0, 2)).reshape(L, H), ((0, 0), (0, 128 - H)))
    g_zx = mm_tn(h0, dzx, BF16, "mm_gw_in_zx")
    g_dt = mm_tn(h0, ddtp, BF16, "mm_gw_in_dt")
    g_in = jnp.concatenate([g_zx, g_dt[:, :H]], axis=1)
    g_in = jnp.transpose(g_in.reshape(D, N_CHIPS, -1), (1, 0, 2))
    rs_a2 = begin_scatter([g_in], ["in_a"], "a2")
    dh0 = mm_nt(dzx, w_zx, F32, "mm_gx_in_zx", after=rs_a2[4])
    dh0_dt = mm_nt(ddtp, w_dt, F32, "mm_gx_in_dt")
    grad_x, dscale0, dshift0 = mod_bwd(dres1, dh0, dh0_dt, xin, scale[0] + rs_a2[4][0:1, 0:1], "mod_bwd0",
                                       through_mod=True)
    g_halves = finish_scatter([(rs_b, names_b)], grad_x, "b")

    def step_halves(w, m, v, nm):
        shp = w.shape
        mine, theirs_ = g_halves[nm]
        outs4 = adamw_halves(w.reshape(-1, shp[-1]), mine, theirs_, m.reshape(-1, shp[-1]), v.reshape(-1, shp[-1]),
                             core, "adamw_" + nm)
        return tuple(t.reshape(shp) for t in outs4)

    big = {
        "kv_w": step_halves(kv_w, m_kv_w, v_kv_w, "kv"),
        "b_in_w": step_halves(b_in_w, m_b_in_w, v_b_in_w, "in_b"),
        "b_out_w": step_halves(b_out_w, m_b_out_w, v_b_out_w, "out_b"),
    }
    g_halves.update(finish_scatter([(rs_a1, ["out_a"]), (rs_a2, ["in_a"])], big["kv_w"][1], "a"))
    big["a_in_w"] = step_halves(a_in_w, m_a_in_w, v_a_in_w, "in_a")
    big["a_out_w"] = step_halves(a_out_w, m_a_out_w, v_a_out_w, "out_a")

    dmod = jnp.concatenate([jnp.concatenate([dshift0, dscale0, dgate0], axis=1),
                            jnp.concatenate([dshift1, dscale1, dgate1], axis=1)], axis=0)
    small_parts = [jnp.concatenate([dg0, dg1], axis=0), jnp.concatenate([db0, db1], axis=0),
                   dbias_g.reshape(1, H), dalog_g.reshape(1, H), dD_g.reshape(1, H),
                   dconv_w, dconv_b, dnorm_g, loss_part.reshape(1, 1)]
    small_shapes = [p.shape for p in small_parts]
    packed = jnp.concatenate([_pack([dmod]), _pack(small_parts)], axis=0)
    n_mod_rows = _pack([dmod]).shape[0]
    gathered = allgather_small(packed, "allgather_small_grads", after=g_halves["in_a"][1]).reshape(N_DEV, -1, 128)
    dmod8 = gathered[:, :n_mod_rows].reshape(N_DEV, -1)[:, :2 * 3 * D].reshape(N_DEV, DEPTH, 3 * D)
    summed = sum_leading(gathered, "sum_small")
    g_ada_b = summed[:n_mod_rows].reshape(-1)[:2 * 3 * D].reshape(DEPTH, 3 * D)
    (g_ln_g, g_ln_b, g_dt_bias, g_a_log, g_dsk, g_conv_w, g_conv_b, g_norm_g, loss_all) = _unpack(
        summed[n_mod_rows:].reshape(-1), small_shapes)
    loss = loss_all.reshape(())
    Cs = CONVD // N_CHIPS
    g_conv_w_s = lax.dynamic_slice_in_dim(g_conv_w, chip * Cs, Cs, axis=1)
    g_conv_b_s = lax.dynamic_slice_in_dim(g_conv_b, chip * Cs, Cs, axis=1)
    g_norm_g_s = lax.dynamic_slice_in_dim(g_norm_g, chip * (DI // N_CHIPS), DI // N_CHIPS, axis=1)
    dmod_s = jnp.transpose(lax.dynamic_slice_in_dim(dmod8, chip * Ws, Ws, axis=2), (1, 0, 2))
    g_ada_w = ada_wgrad(jnp.transpose(c8), dmod_s)

    def step2d(w, g, m, v, nm):
        shp = w.shape
        d_, m_, v_ = adamw(w.reshape(-1, shp[-1]), g.reshape(-1, shp[-1]), m.reshape(-1, shp[-1]),
                           v.reshape(-1, shp[-1]), "adamw_" + nm)
        return g.reshape(shp), d_.reshape(shp), m_.reshape(shp), v_.reshape(shp)

    big["ada_w"] = step2d(ada_w, g_ada_w, m_ada_w, v_ada_w, "ada_w")
    small_names = ["ada_b", "ln_g", "ln_b", "a_conv_w", "a_conv_b", "a_dt_bias", "a_A_log", "a_D", "a_norm_g"]
    small_w = [ada_b, ln_g, ln_b, a_conv_w, a_conv_b, a_dt_bias, a_A_log, a_D, a_norm_g]
    small_m = [m_ada_b, m_ln_g, m_ln_b, m_a_conv_w, m_a_conv_b, m_a_dt_bias, m_a_A_log, m_a_D, m_a_norm_g]
    small_v = [v_ada_b, v_ln_g, v_ln_b, v_a_conv_w, v_a_conv_b, v_a_dt_bias, v_a_A_log, v_a_D, v_a_norm_g]
    small_g = [g_ada_b, g_ln_g, g_ln_b, g_conv_w_s, g_conv_b_s, g_dt_bias, g_a_log, g_dsk, g_norm_g_s]
    shapes = [w.shape for w in small_w]
    small_g = [g.reshape(s) for g, s in zip(small_g, shapes)]
    d_p, m_p, v_p = adamw(_pack(small_w), _pack(small_g), _pack(small_m), _pack(small_v), "adamw_small")
    small = {}
    for nm, g, d_, m_, v_ in zip(small_names, small_g, _unpack(d_p.reshape(-1), shapes), _unpack(m_p.reshape(-1), shapes),
                                 _unpack(v_p.reshape(-1), shapes)):
        small[nm] = (g, d_, m_, v_)
    allw = {**big, **small}
    order = ["ada_w", "ada_b", "ln_g", "ln_b", "a_in_w", "a_conv_w", "a_conv_b", "a_dt_bias", "a_A_log", "a_D",
             "a_norm_g", "a_out_w", "kv_w", "b_in_w", "b_out_w"]
    outs = [loss, grad_x.reshape(x.shape)]
    for k in range(4):
        outs += [allw[n][k] for n in order]
    return tuple(outs)
```

```python
import functools

import jax
import jax.numpy as jnp
import numpy as np
from jax import lax
from jax.experimental import pallas as pl
from jax.experimental.pallas import tpu as pltpu

F32 = jnp.float32
BF16 = jnp.bfloat16
MESH = pl.DeviceIdType.MESH

DEPTH = 2
ALPHA = (2 * DEPTH) ** 0.25
LN_EPS = 1e-5
RMS_EPS = 1e-5
SSD_P = 64
SSD_N = 128
SSD_Q = 256
SSD_G = 8
CONV_W = 4
DIL_PATTERNS = ((128, 1), (512, 4), (2048, 16))
DIL_H = 8
DIL_E = 128
DIL_BLK = 128
ADAM_LR, ADAM_B1, ADAM_B2, ADAM_EPS, ADAM_WD, ADAM_STEP = 0.001, 0.9, 0.999, 1e-08, 0.01, 10

VMEM_LIMIT = 56 * 1024 * 1024
N_CHIPS = 4
N_DEV = 8


def _tile(dim, target, mult=128):
    if dim <= target:
        return dim
    t = (target // mult) * mult
    while t >= mult:
        if dim % t == 0:
            return t
        t -= mult
    return dim


def _cp(sem):
    return pltpu.CompilerParams(dimension_semantics=sem, vmem_limit_bytes=VMEM_LIMIT)


def _sigmoid(x):
    return 1.0 / (1.0 + jnp.exp(-x))


def _silu(x):
    return x * _sigmoid(x)


def _dsilu(x):
    s = _sigmoid(x)
    return s * (1.0 + x * (1.0 - s))


def _softplus(x):
    return jnp.maximum(x, 0.0) + jnp.log(1.0 + jnp.exp(-jnp.abs(x)))


def _mm_call(a, b, out_shape, grid, a_spec, b_spec, o_spec, acc_shape, dims, name, after=None):
    nk = grid[2]
    extra = [] if after is None else [after]

    def prod(a_ref, b_ref):
        return lax.dot_general(a_ref[...].astype(BF16), b_ref[...].astype(BF16), (dims, ((), ())),
                               preferred_element_type=F32)

    def body_single(a_ref, b_ref, *rest):
        o_ref = rest[len(extra)]
        o_ref[...] = prod(a_ref, b_ref).astype(o_ref.dtype)

    def body_multi(a_ref, b_ref, *rest):
        o_ref, acc_ref = rest[len(extra):]
        k = pl.program_id(2)

        @pl.when(k == 0)
        def _():
            acc_ref[...] = prod(a_ref, b_ref)

        @pl.when(jnp.logical_and(k > 0, k < nk - 1))
        def _():
            acc_ref[...] += prod(a_ref, b_ref)

        @pl.when(k == nk - 1)
        def _():
            o_ref[...] = (acc_ref[...] + prod(a_ref, b_ref)).astype(o_ref.dtype)

    return pl.pallas_call(
        body_single if nk == 1 else body_multi, grid=grid, in_specs=[a_spec, b_spec] + [_ANY] * len(extra),
        out_specs=o_spec, out_shape=out_shape, scratch_shapes=[] if nk == 1 else [pltpu.VMEM(acc_shape, F32)],
        compiler_params=_cp(("parallel", "parallel", "arbitrary")), name=name)(a, b, *extra)


def mm_nn(a, b, out_dtype, name, stack=None, tm=1024, tn=1024, tk=2048, n_cols=None):
    M, K = a.shape
    if stack is None:
        N = b.shape[1] if n_cols is None else n_cols
        tn, tk = _tile(N, tn), _tile(K, tk)
        b_spec = pl.BlockSpec((tk, tn), lambda i, j, k: (k, j))
    elif stack == "col":
        S, _, Ns = b.shape
        N = S * Ns
        tn, tk = _tile(Ns, tn), _tile(K, tk)
        npb = Ns // tn
        b_spec = pl.BlockSpec((None, tk, tn), lambda i, j, k: (j // npb, k, j % npb))
    else:
        S, Ks, N = b.shape
        tn, tk = _tile(N, tn), _tile(Ks, tk)
        kpb = Ks // tk
        b_spec = pl.BlockSpec((None, tk, tn), lambda i, j, k: (k // kpb, k % kpb, j))
    tm = _tile(M, tm)
    return _mm_call(a, b, jax.ShapeDtypeStruct((M, N), out_dtype), (M // tm, N // tn, K // tk),
                    pl.BlockSpec((tm, tk), lambda i, j, k: (i, k)), b_spec,
                    pl.BlockSpec((tm, tn), lambda i, j, k: (i, j)), (tm, tn), ((1,), (0,)), name)


def mm_nt(a, b, out_dtype, name, stack=None, tm=1024, tn=1024, tk=2048, after=None):
    M, C = a.shape
    if stack is None:
        Kw = b.shape[0]
        tn, tk = _tile(Kw, tn), _tile(C, tk)
        b_spec = pl.BlockSpec((tn, tk), lambda i, j, k: (j, k))
    elif stack == "col":
        S, Kw, Cs = b.shape
        tn, tk = _tile(Kw, tn), _tile(Cs, tk)
        cpb = Cs // tk
        b_spec = pl.BlockSpec((None, tn, tk), lambda i, j, k: (k // cpb, j, k % cpb))
    else:
        S, Ks, _ = b.shape
        Kw = S * Ks
        tn, tk = _tile(Ks, tn), _tile(C, tk)
        jpb = Ks // tn
        b_spec = pl.BlockSpec((None, tn, tk), lambda i, j, k: (j // jpb, j % jpb, k))
    tm = _tile(M, tm)
    return _mm_call(a, b, jax.ShapeDtypeStruct((M, Kw), out_dtype), (M // tm, Kw // tn, C // tk),
                    pl.BlockSpec((tm, tk), lambda i, j, k: (i, k)), b_spec,
                    pl.BlockSpec((tm, tn), lambda i, j, k: (i, j)), (tm, tn), ((1,), (1,)), name, after=after)


def mm_tn(a, b, out_dtype, name, stack=None, n_stack=N_CHIPS, tm=1024, tn=1024, tk=2048):
    L, M = a.shape
    N = b.shape[1]
    tk = _tile(L, tk)
    if stack is None:
        tm, tn = _tile(M, tm), _tile(N, tn)
        o_spec = pl.BlockSpec((tm, tn), lambda i, j, k: (i, j))
        out_shape = (M, N)
    elif stack == "col":
        Ns = N // n_stack
        tm, tn = _tile(M, tm), _tile(Ns, tn)
        npb = Ns // tn
        o_spec = pl.BlockSpec((None, tm, tn), lambda i, j, k: (j // npb, i, j % npb))
        out_shape = (n_stack, M, Ns)
    else:
        Ms = M // n_stack
        tm, tn = _tile(Ms, tm), _tile(N, tn)
        mpb = Ms // tm
        o_spec = pl.BlockSpec((None, tm, tn), lambda i, j, k: (i // mpb, i % mpb, j))
        out_shape = (n_stack, Ms, N)
    return _mm_call(a, b, jax.ShapeDtypeStruct(out_shape, out_dtype), (M // tm, N // tn, L // tk),
                    pl.BlockSpec((tk, tm), lambda i, j, k: (k, i)), pl.BlockSpec((tk, tn), lambda i, j, k: (k, j)),
                    o_spec, (tm, tn), ((0,), (0,)), name)


def _row_specs(tr, widths):
    return [pl.BlockSpec((tr, w), lambda i: (i, 0)) for w in widths]


def _vec_spec(w):
    return pl.BlockSpec((1, w), lambda i: (0, 0))


def _acc_rows(ref, val, i):
    s = jnp.sum(val, axis=0, keepdims=True)

    @pl.when(i == 0)
    def _():
        ref[...] = s

    @pl.when(i > 0)
    def _():
        ref[...] += s


def modulate(x, scale, shift, name):
    L, D = x.shape
    tr = _tile(L, 512, 16)

    def body(x_ref, sc_ref, sh_ref, h_ref):
        h_ref[...] = (x_ref[...] * (1.0 + sc_ref[...]) + sh_ref[...]).astype(BF16)

    return pl.pallas_call(
        body, grid=(L // tr,), in_specs=_row_specs(tr, [D]) + [_vec_spec(D)] * 2, out_specs=_row_specs(tr, [D])[0],
        out_shape=jax.ShapeDtypeStruct((L, D), BF16), compiler_params=_cp(("parallel",)), name=name)(x, scale, shift)


def _ln_core(x, y, gate, g, b):
    u = ALPHA * x + (1.0 + gate) * y
    mu = jnp.mean(u, axis=-1, keepdims=True)
    d = u - mu
    var = jnp.mean(d * d, axis=-1, keepdims=True)
    rstd = lax.rsqrt(var + LN_EPS)
    xhat = d * rstd
    return xhat * g + b, xhat, rstd


def ln_mid(x, y, gate, g, b, scale, shift):
    L, D = x.shape
    tr = _tile(L, 256, 16)

    def body(x_ref, y_ref, gate_ref, g_ref, b_ref, sc_ref, sh_ref, x1_ref, x1b_ref, h_ref):
        x1, _, _ = _ln_core(x_ref[...], y_ref[...], gate_ref[...], g_ref[...], b_ref[...])
        x1_ref[...] = x1
        x1b_ref[...] = x1.astype(BF16)
        h_ref[...] = (x1 * (1.0 + sc_ref[...]) + sh_ref[...]).astype(BF16)

    return pl.pallas_call(
        body, grid=(L // tr,), in_specs=_row_specs(tr, [D, D]) + [_vec_spec(D)] * 5,
        out_specs=_row_specs(tr, [D, D, D]),
        out_shape=[jax.ShapeDtypeStruct((L, D), F32), jax.ShapeDtypeStruct((L, D), BF16),
                   jax.ShapeDtypeStruct((L, D), BF16)],
        compiler_params=_cp(("parallel",)), name="ln_mid")(x, y, gate, g, b, scale, shift)


def _ln_bwd_rows(dout_v, xhat, rstd, g):
    dxh = dout_v * g
    m1 = jnp.mean(dxh, axis=-1, keepdims=True)
    m2 = jnp.mean(dxh * xhat, axis=-1, keepdims=True)
    return rstd * (dxh - m1 - xhat * m2)


def ln_final_fwd_bwd(x, y, gate, g, b, target):
    L, D = x.shape
    tr = _tile(L, 256, 16)

    def body(x_ref, y_ref, gate_ref, g_ref, b_ref, t_ref, dres_ref, dy_ref, dg_ref, db_ref, dgate_ref, sq_ref):
        i = pl.program_id(0)
        yv = y_ref[...]
        out, xhat, rstd = _ln_core(x_ref[...], yv, gate_ref[...], g_ref[...], b_ref[...])
        err = out - t_ref[...]
        dout_v = err * (1.0 / D)
        du = _ln_bwd_rows(dout_v, xhat, rstd, g_ref[...])
        dres_ref[...] = ALPHA * du
        dy_ref[...] = ((1.0 + gate_ref[...]) * du).astype(BF16)
        _acc_rows(dg_ref, dout_v * xhat, i)
        _acc_rows(db_ref, dout_v, i)
        _acc_rows(dgate_ref, du * yv, i)
        _acc_rows(sq_ref, err * err, i)

    return pl.pallas_call(
        body, grid=(L // tr,), in_specs=_row_specs(tr, [D, D]) + [_vec_spec(D)] * 3 + _row_specs(tr, [D]),
        out_specs=_row_specs(tr, [D, D]) + [_vec_spec(D)] * 4,
        out_shape=[jax.ShapeDtypeStruct((L, D), F32), jax.ShapeDtypeStruct((L, D), BF16)]
        + [jax.ShapeDtypeStruct((1, D), F32)] * 4,
        compiler_params=_cp(("arbitrary",)), name="ln_final_fwd_bwd")(x, y, gate, g, b, target)


def mod_ln_bwd(dres_in, dh, dskip, xmid, scale, x, y, gate, g):
    L, D = x.shape
    tr = _tile(L, 256, 16)

    def body(dres_ref, dh_ref, dskip_ref, xm_ref, sc_ref, x_ref, y_ref, gate_ref, g_ref,
             dres_out, dy_ref, dg_ref, db_ref, dgate_ref, dsc_ref, dsh_ref):
        i = pl.program_id(0)
        dh_v = dh_ref[...]
        dout_v = dres_ref[...] + dskip_ref[...] + dh_v * (1.0 + sc_ref[...])
        _acc_rows(dsc_ref, dh_v * xm_ref[...], i)
        _acc_rows(dsh_ref, dh_v, i)
        yv = y_ref[...]
        _, xhat, rstd = _ln_core(x_ref[...], yv, gate_ref[...], g_ref[...], 0.0)
        du = _ln_bwd_rows(dout_v, xhat, rstd, g_ref[...])
        dres_out[...] = ALPHA * du
        dy_ref[...] = ((1.0 + gate_ref[...]) * du).astype(BF16)
        _acc_rows(dg_ref, dout_v * xhat, i)
        _acc_rows(db_ref, dout_v, i)
        _acc_rows(dgate_ref, du * yv, i)

    return pl.pallas_call(
        body, grid=(L // tr,),
        in_specs=_row_specs(tr, [D] * 4) + [_vec_spec(D)] + _row_specs(tr, [D, D]) + [_vec_spec(D)] * 2,
        out_specs=_row_specs(tr, [D, D]) + [_vec_spec(D)] * 5,
        out_shape=[jax.ShapeDtypeStruct((L, D), F32), jax.ShapeDtypeStruct((L, D), BF16)]
        + [jax.ShapeDtypeStruct((1, D), F32)] * 5,
        compiler_params=_cp(("arbitrary",)), name="mod_ln_bwd")(dres_in, dh, dskip, xmid, scale, x, y, gate, g)


def ln_bwd(dout, x, y, gate, g, name):
    L, D = x.shape
    tr = _tile(L, 256, 16)

    def body(do_ref, x_ref, y_ref, gate_ref, g_ref, dres_ref, dy_ref, dg_ref, db_ref, dgate_ref):
        i = pl.program_id(0)
        yv = y_ref[...]
        dout_v = do_ref[...]
        _, xhat, rstd = _ln_core(x_ref[...], yv, gate_ref[...], g_ref[...], 0.0)
        dxh = dout_v * g_ref[...]
        m1 = jnp.mean(dxh, axis=-1, keepdims=True)
        m2 = jnp.mean(dxh * xhat, axis=-1, keepdims=True)
        du = rstd * (dxh - m1 - xhat * m2)
        dres_ref[...] = ALPHA * du
        dy_ref[...] = ((1.0 + gate_ref[...]) * du).astype(BF16)
        _acc_rows(dg_ref, dout_v * xhat, i)
        _acc_rows(db_ref, dout_v, i)
        _acc_rows(dgate_ref, du * yv, i)

    return pl.pallas_call(
        body, grid=(L // tr,), in_specs=_row_specs(tr, [D, D, D]) + [_vec_spec(D)] * 2,
        out_specs=_row_specs(tr, [D, D]) + [_vec_spec(D)] * 3,
        out_shape=[jax.ShapeDtypeStruct((L, D), F32), jax.ShapeDtypeStruct((L, D), BF16)]
        + [jax.ShapeDtypeStruct((1, D), F32)] * 3,
        compiler_params=_cp(("arbitrary",)), name=name)(dout, x, y, gate, g)


def mod_bwd(dres, dh, dh2, xin, scale, name, through_mod):
    L, D = xin.shape
    tr = _tile(L, 256, 16)

    def body(dres_ref, dh_ref, dh2_ref, x_ref, sc_ref, dx_ref, dsc_ref, dsh_ref):
        i = pl.program_id(0)
        dh_v = dh_ref[...]
        tot = dres_ref[...]
        if through_mod:
            dh_v = dh_v + dh2_ref[...]
        else:
            tot = tot + dh2_ref[...]
        dx_ref[...] = tot + dh_v * (1.0 + sc_ref[...])
        _acc_rows(dsc_ref, dh_v * x_ref[...], i)
        _acc_rows(dsh_ref, dh_v, i)

    return pl.pallas_call(
        body, grid=(L // tr,), in_specs=_row_specs(tr, [D, D, D, D]) + [_vec_spec(D)],
        out_specs=_row_specs(tr, [D]) + [_vec_spec(D)] * 2,
        out_shape=[jax.ShapeDtypeStruct((L, D), F32)] + [jax.ShapeDtypeStruct((1, D), F32)] * 2,
        compiler_params=_cp(("arbitrary",)), name=name)(dres, dh, dh2, xin, scale)


CONV_HALO = 16


def _conv_rows(x_ref, i, tr, L):
    nblk = L // tr
    s = pl.multiple_of(i * tr, CONV_HALO)
    cur = x_ref[pl.ds(s, tr), :].astype(F32)
    sp = pl.multiple_of(jnp.maximum(i * tr - CONV_HALO, 0), CONV_HALO)
    sn = pl.multiple_of(jnp.minimum(i * tr + tr, L - CONV_HALO), CONV_HALO)
    prev = x_ref[pl.ds(sp, CONV_HALO), :].astype(F32) * (i > 0).astype(F32)
    nxt = x_ref[pl.ds(sn, CONV_HALO), :].astype(F32) * (i < nblk - 1).astype(F32)
    return jnp.concatenate([prev, cur, nxt], axis=0)


def _shift_rows(v, j):
    n = v.shape[0]
    return v if j % n == 0 else pltpu.roll(v, j % n, 0)


def _conv_taps(xe):
    return [_shift_rows(xe, CONV_W - 1 - k) for k in range(CONV_W)]


def _conv_eval(taps, w_ref, b_ref):
    c = b_ref[...] + w_ref[0:1, :] * taps[0]
    for k in range(1, CONV_W):
        c = c + w_ref[k:k + 1, :] * taps[k]
    return c


def conv_fwd(zx, col0, conv_w, conv_b):
    L = zx.shape[0]
    C = conv_w.shape[1]
    tc = _tile(C, 512)
    tr = _tile(L, 512, CONV_HALO)
    off = col0 // tc

    def body(x_ref, w_ref, b_ref, o_ref):
        i = pl.program_id(1)
        xe = _conv_rows(x_ref, i, tr, L)
        c = _conv_eval(_conv_taps(xe), w_ref, b_ref)[CONV_HALO:CONV_HALO + tr]
        o_ref[...] = _silu(c).astype(BF16)

    return pl.pallas_call(
        body, grid=(C // tc, L // tr),
        in_specs=[pl.BlockSpec((L, tc), lambda j, i: (0, off + j)), pl.BlockSpec((CONV_W, tc), lambda j, i: (0, j)),
                  pl.BlockSpec((1, tc), lambda j, i: (0, j))],
        out_specs=pl.BlockSpec((tr, tc), lambda j, i: (i, j)),
        out_shape=jax.ShapeDtypeStruct((L, C), BF16), compiler_params=_cp(("parallel", "arbitrary")),
        name="conv_fwd")(zx, conv_w, conv_b)


def conv_bwd(zx, col0, conv_w, conv_b, dxbc):
    L = zx.shape[0]
    C = conv_w.shape[1]
    tc = _tile(C, 512)
    tr = _tile(L, 512, CONV_HALO)
    off = col0 // tc
    H = CONV_HALO

    def body(x_ref, g_ref, w_ref, b_ref, dx_ref, dw_ref, db_ref):
        i = pl.program_id(1)
        xe = _conv_rows(x_ref, i, tr, L)
        ge = _conv_rows(g_ref, i, tr, L)
        taps = _conv_taps(xe)
        dc = ge * _dsilu(_conv_eval(taps, w_ref, b_ref))
        dx = w_ref[CONV_W - 1:CONV_W, :] * dc
        for k in range(CONV_W - 1):
            dx = dx + w_ref[k:k + 1, :] * _shift_rows(dc, -(CONV_W - 1 - k))
        dx_ref[...] = dx[H:H + tr].astype(BF16)
        dcc = dc[H:H + tr]
        rows = [jnp.sum(dcc * taps[k][H:H + tr], axis=0, keepdims=True) for k in range(CONV_W)]
        dwv = jnp.concatenate(rows + [jnp.zeros((8 - CONV_W, tc), F32)], axis=0)
        dbv = jnp.sum(dcc, axis=0, keepdims=True)

        @pl.when(i == 0)
        def _():
            dw_ref[...] = dwv
            db_ref[...] = dbv

        @pl.when(i > 0)
        def _():
            dw_ref[...] += dwv
            db_ref[...] += dbv

    dx, dw, db = pl.pallas_call(
        body, grid=(C // tc, L // tr),
        in_specs=[pl.BlockSpec((L, tc), lambda j, i: (0, off + j)), pl.BlockSpec((L, tc), lambda j, i: (0, j)),
                  pl.BlockSpec((CONV_W, tc), lambda j, i: (0, j)), pl.BlockSpec((1, tc), lambda j, i: (0, j))],
        out_specs=[pl.BlockSpec((tr, tc), lambda j, i: (i, j)), pl.BlockSpec((8, tc), lambda j, i: (0, j)),
                   pl.BlockSpec((1, tc), lambda j, i: (0, j))],
        out_shape=[jax.ShapeDtypeStruct((L, C), BF16), jax.ShapeDtypeStruct((8, C), F32),
                   jax.ShapeDtypeStruct((1, C), F32)],
        compiler_params=_cp(("parallel", "arbitrary")), name="conv_bwd")(zx, dxbc, conv_w, conv_b)
    return dx, dw[:CONV_W], db


_NN = (((1,), (0,)), ((), ()))


def _pieces(x, n):
    out, r = [], x
    for _ in range(n):
        p = r.astype(BF16)
        out.append(p)
        r = r - p.astype(F32)
    return out


def _dot01(a, b01, n, dims=_NN):
    b = b01.astype(BF16)
    return functools.reduce(lambda u, v: u + v,
                            [lax.dot_general(p, b, dims, preferred_element_type=F32) for p in _pieces(a, n)])


def _dot01_left(a01, b, n, dims=_NN):
    a = a01.astype(BF16)
    return functools.reduce(lambda u, v: u + v,
                            [lax.dot_general(a, p, dims, preferred_element_type=F32) for p in _pieces(b, n)])


def _ssd_common(dtp_ref, dtpT_ref, bias_ref, biasT_ref, alog_ref, alogT_ref, b_ref, c_ref):
    Q = SSD_Q
    dt = _softplus(dtp_ref[...] + bias_ref[...])
    A = -jnp.exp(alog_ref[...])
    row = lax.broadcasted_iota(jnp.int32, (Q, Q), 0)
    col = lax.broadcasted_iota(jnp.int32, (Q, Q), 1)
    causal = row >= col
    tril = causal.astype(F32)
    Kh = dt.shape[1]
    acum = _dot01_left(tril, dt * A, 3)
    eye = (lax.broadcasted_iota(jnp.int32, (Kh, Kh), 0) == lax.broadcasted_iota(jnp.int32, (Kh, Kh), 1)).astype(F32)
    acumT = _dot01_left(eye, acum, 3, dims=(((1,), (1,)), ((), ())))
    Bm = b_ref[...]
    Cm = c_ref[...]
    cb = lax.dot_general(Cm, Bm, (((1,), (1,)), ((), ())), preferred_element_type=F32)
    return dt, A, causal, row, col, acum, acumT, Bm, Cm, cb


def _ssd_in_specs(Q, GP, N, Kh, DI, cmap):
    nb0 = DI // N
    vec = pl.BlockSpec((None, 1, Kh), lambda g, c: (g, 0, 0))
    vecT = pl.BlockSpec((None, Kh, 1), lambda g, c: (g, 0, 0))
    return [pl.BlockSpec((Q, GP), lambda g, c: (cmap(c), g)),
            pl.BlockSpec((Q, N), lambda g, c: (cmap(c), nb0 + g)),
            pl.BlockSpec((Q, N), lambda g, c: (cmap(c), nb0 + SSD_G + g)),
            pl.BlockSpec((None, Q, Kh), lambda g, c: (g, cmap(c), 0)),
            pl.BlockSpec((None, Kh, Q), lambda g, c: (g, 0, cmap(c))),
            vec, vecT, vec, vecT, vec, vecT]


def _hi(a, b01):
    return _dot01(a, b01, 2)


def _headsum(a, b01):
    return _dot01(a, b01, 1)


def _ssd_heads(dskT_ref, acum, acumT, dt, Kh):
    Q, P, N = SSD_Q, SSD_P, SSD_N
    GP = Kh * P
    sh_p = P.bit_length() - 1
    seg = lambda shape, dim: lax.shift_right_logical(lax.broadcasted_iota(jnp.int32, shape, dim), sh_p)
    E = (seg((Kh, GP), 1) == lax.broadcasted_iota(jnp.int32, (Kh, GP), 0)).astype(F32)
    ET = (seg((GP, Kh), 0) == lax.broadcasted_iota(jnp.int32, (GP, Kh), 1)).astype(F32)
    a_last = acum[Q - 1:Q, :]
    tail = jnp.exp(a_last - acum)
    eLT = jnp.exp(acumT[:, Q - 1:Q])
    rowseg = seg((GP, N), 0)
    eL_b = jnp.zeros((GP, N), F32)
    for k in range(Kh):
        eL_b = jnp.where(rowseg == k, eLT[k:k + 1, :], eL_b)
    return dict(
        E=E, ET=ET, a_last=a_last, tail=tail, eL_b=eL_b,
        dt_all=_hi(dt, E), ea_all=_hi(jnp.exp(acum), E), tail_all=_hi(tail, E),
        dsk_all=jnp.sum(E * dskT_ref[...], axis=0, keepdims=True))


def _head_chunks(GP):
    CW = min(GP, 128)
    return CW, CW // SSD_P, GP // CW


def _head_mask(Q, CW, kk):
    lane = lax.broadcasted_iota(jnp.int32, (Q, CW), 1)
    return jnp.logical_and(lane >= kk * SSD_P, lane < (kk + 1) * SSD_P)


def ssd_fwd(xbc, dtp_g, dtp_gT, bias_g, bias_gT, alog_g, alog_gT, dsk_g, dsk_gT, DI):
    L = xbc.shape[0]
    Q, P, N, G = SSD_Q, SSD_P, SSD_N, SSD_G
    GP = DI // G
    Kh = GP // P
    nc = L // Q

    CW, hpc, nch = _head_chunks(GP)
    nt = (((1,), (1,)), ((), ()))
    tn = (((0,), (0,)), ((), ()))

    def body(xs_ref, b_ref, c_ref, dtp_ref, dtpT_ref, bias_ref, biasT_ref, alog_ref, alogT_ref, dsk_ref, dskT_ref,
             y_ref, st_ref, state):
        @pl.when(pl.program_id(1) == 0)
        def _():
            state[...] = jnp.zeros(state.shape, F32)

        st_ref[...] = state[...]
        dt, A, causal, row, col, acum, acumT, Bm, Cm, cb = _ssd_common(
            dtp_ref, dtpT_ref, bias_ref, biasT_ref, alog_ref, alogT_ref, b_ref, c_ref)
        hd = _ssd_heads(dskT_ref, acum, acumT, dt, Kh)
        xs = xs_ref[...].astype(F32)
        xdt_all = xs * hd["dt_all"]
        S_all = state[...]
        y_all = (lax.dot_general(Cm, S_all.astype(BF16), nt, preferred_element_type=F32) * hd["ea_all"]
                 + xs * hd["dsk_all"])
        state[...] = S_all * hd["eL_b"] + lax.dot_general(
            (xdt_all * hd["tail_all"]).astype(BF16), Bm, tn, preferred_element_type=F32)
        for ch in range(nch):
            cs = slice(ch * CW, (ch + 1) * CW)
            xc = xdt_all[:, cs]
            acc = y_all[:, cs]
            for kk in range(hpc):
                k = ch * hpc + kk
                decay = jnp.exp(jnp.where(causal, acum[:, k:k + 1] - acumT[k:k + 1, :], -jnp.inf))
                xk = xc if hpc == 1 else jnp.where(_head_mask(Q, CW, kk), xc, 0.0)
                acc = acc + jnp.dot((cb * decay).astype(BF16), xk.astype(BF16), preferred_element_type=F32)
            y_ref[:, cs] = acc.astype(BF16)

    return pl.pallas_call(
        body, grid=(G, nc), in_specs=_ssd_in_specs(Q, GP, N, Kh, DI, lambda c: c),
        out_specs=[pl.BlockSpec((Q, GP), lambda g, c: (c, g)),
                   pl.BlockSpec((None, None, GP, N), lambda g, c: (c, g, 0, 0))],
        out_shape=[jax.ShapeDtypeStruct((L, DI), BF16), jax.ShapeDtypeStruct((nc, G, GP, N), F32)],
        scratch_shapes=[pltpu.VMEM((GP, N), F32)], compiler_params=_cp(("parallel", "arbitrary")),
        name="ssd_fwd")(xbc, xbc, xbc, dtp_g, dtp_gT, bias_g, bias_gT, alog_g, alog_gT, dsk_g, dsk_gT)


def ssd_bwd(xbc, dtp_g, dtp_gT, bias_g, bias_gT, alog_g, alog_gT, dsk_g, dsk_gT, states, dy, DI):
    L = xbc.shape[0]
    Q, P, N, G = SSD_Q, SSD_P, SSD_N, SSD_G
    GP = DI // G
    Kh = GP // P
    nc = L // Q
    rev = lambda c: nc - 1 - c

    CW, hpc, nch = _head_chunks(GP)

    def body(xs_ref, b_ref, c_ref, dtp_ref, dtpT_ref, bias_ref, biasT_ref, alog_ref, alogT_ref, dsk_ref, dskT_ref,
             st_ref, dy_ref, dxs_ref, dB_ref, dC_ref, ddtp_ref, dbias_ref, dalog_ref, dD_ref, dstate):
        ci = pl.program_id(1)

        @pl.when(ci == 0)
        def _():
            dstate[...] = jnp.zeros(dstate.shape, F32)

        dt, A, causal, row, col, acum, acumT, Bm, Cm, cb = _ssd_common(
            dtp_ref, dtpT_ref, bias_ref, biasT_ref, alog_ref, alogT_ref, b_ref, c_ref)
        tn = (((0,), (0,)), ((), ()))
        nt = (((1,), (1,)), ((), ()))
        hd = _ssd_heads(dskT_ref, acum, acumT, dt, Kh)
        ET, tail = hd["ET"], hd["tail"]
        cbT = lax.dot_general(Bm, Cm, nt, preferred_element_type=F32)
        causalT = row <= col
        xs = xs_ref[...].astype(F32)
        xdt_all = xs * hd["dt_all"]
        dyb = dy_ref[...]
        dy_all = dyb.astype(F32)
        S_all = st_ref[...]
        S_b = S_all.astype(BF16)
        dS_all = dstate[...]
        dS_b = dS_all.astype(BF16)
        CS_all = lax.dot_general(Cm, S_b, nt, preferred_element_type=F32)
        dyE_b = (dy_all * hd["ea_all"]).astype(BF16)
        dC_acc = jnp.dot(dyE_b, S_b, preferred_element_type=F32)
        dS_y = lax.dot_general(dyE_b, Cm, tn, preferred_element_type=F32)
        BdS_all = lax.dot_general(Bm, dS_b, nt, preferred_element_type=F32)
        dB_acc = jnp.dot((xdt_all * hd["tail_all"]).astype(BF16), dS_b, preferred_element_type=F32)
        dtail = _headsum(xdt_all * BdS_all, ET)
        da_cols = _headsum(dy_all * CS_all * hd["ea_all"], ET) - dtail * tail
        dss = _dot01_left(jnp.ones((8, N), F32), _dot01_left(hd["E"], dS_all * S_all, 2), 2, dims=nt)
        da_last = dss[0:1] * jnp.exp(hd["a_last"]) + jnp.sum(dtail * tail, axis=0, keepdims=True)
        rowi = lax.broadcasted_iota(jnp.int32, (Q, Kh), 0)
        da_cols = da_cols + jnp.where(rowi == Q - 1, da_last, 0.0)
        dstate[...] = hd["eL_b"] * dS_all + dS_y
        sum_mg = jnp.zeros((Q, Q), F32)
        ddt_x = jnp.zeros((Q, Kh), F32)
        da_rows = jnp.zeros((Kh, Q), F32)
        lane_k = lax.broadcasted_iota(jnp.int32, (Q, Kh), 1)
        sub_k = lax.broadcasted_iota(jnp.int32, (Kh, Q), 0)
        for ch in range(nch):
            cs = slice(ch * CW, (ch + 1) * CW)
            dyc = dyb[:, cs]
            xc_b = xdt_all[:, cs].astype(BF16)
            acc = hd["tail_all"][:, cs] * BdS_all[:, cs]
            for kk in range(hpc):
                k = ch * hpc + kk
                a_b = jnp.broadcast_to(acum[:, k:k + 1], (Q, Q))
                a_r = acumT[k:k + 1, :]
                decay = jnp.exp(jnp.where(causal, a_b - a_r, -jnp.inf))
                decayT = jnp.exp(jnp.where(causalT, a_r - a_b, -jnp.inf))
                dyk = dyc if hpc == 1 else jnp.where(_head_mask(Q, CW, kk), dyc, jnp.zeros_like(dyc))
                mg = decay * lax.dot_general(dyk, xc_b, nt, preferred_element_type=F32)
                sum_mg = sum_mg + mg
                w = mg * cb
                da_cols = da_cols + jnp.where(lane_k == k, jnp.sum(w, axis=1, keepdims=True), 0.0)
                da_rows = da_rows + jnp.where(sub_k == k, jnp.sum(w, axis=0, keepdims=True), 0.0)
                acc = acc + jnp.dot((decayT * cbT).astype(BF16), dyk, preferred_element_type=F32)
            dxs_ref[:, cs] = (acc * hd["dt_all"][:, cs] + dy_all[:, cs] * hd["dsk_all"][:, cs]).astype(BF16)
            ddt_x = ddt_x + _headsum(acc * xs[:, cs], ET[cs, :])
        eye_q = (row == col).astype(F32)
        da_cols = da_cols - _dot01_left(eye_q, da_rows, 3, dims=nt)
        dD_row = jnp.sum(_headsum(dy_all * xs, ET), axis=0, keepdims=True)
        sum_mg_b = sum_mg.astype(BF16)
        dB_ref[...] = (dB_acc + lax.dot_general(sum_mg_b, Cm, tn, preferred_element_type=F32)).astype(BF16)
        dC_ref[...] = (dC_acc + jnp.dot(sum_mg_b, Bm, preferred_element_type=F32)).astype(BF16)
        triu = (row <= col).astype(F32)
        ddtA = _dot01_left(triu, da_cols, 3)
        ddt = ddt_x + ddtA * A
        dpre = ddt * _sigmoid(dtp_ref[...] + bias_ref[...])
        ddtp_ref[...] = dpre
        dbias_v = jnp.sum(dpre, axis=0, keepdims=True)
        dalog_v = jnp.sum(ddtA * dt, axis=0, keepdims=True) * A

        @pl.when(ci == 0)
        def _():
            dbias_ref[...] = dbias_v
            dalog_ref[...] = dalog_v
            dD_ref[...] = dD_row

        @pl.when(ci > 0)
        def _():
            dbias_ref[...] += dbias_v
            dalog_ref[...] += dalog_v
            dD_ref[...] += dD_row

    vec_o = pl.BlockSpec((None, 1, Kh), lambda g, c: (g, 0, 0))
    return pl.pallas_call(
        body, grid=(G, nc),
        in_specs=_ssd_in_specs(Q, GP, N, Kh, DI, rev)
        + [pl.BlockSpec((None, None, GP, N), lambda g, c: (rev(c), g, 0, 0)),
           pl.BlockSpec((Q, GP), lambda g, c: (rev(c), g))],
        out_specs=[pl.BlockSpec((Q, GP), lambda g, c: (rev(c), g)), pl.BlockSpec((Q, N), lambda g, c: (rev(c), g)),
                   pl.BlockSpec((Q, N), lambda g, c: (rev(c), g)),
                   pl.BlockSpec((None, Q, Kh), lambda g, c: (g, rev(c), 0)), vec_o, vec_o, vec_o],
        out_shape=[jax.ShapeDtypeStruct((L, DI), BF16), jax.ShapeDtypeStruct((L, G * N), BF16),
                   jax.ShapeDtypeStruct((L, G * N), BF16), jax.ShapeDtypeStruct((G, L, Kh), F32)]
        + [jax.ShapeDtypeStruct((G, 1, Kh), F32)] * 3,
        scratch_shapes=[pltpu.VMEM((GP, N), F32)], compiler_params=_cp(("parallel", "arbitrary")),
        name="ssd_bwd")(xbc, xbc, xbc, dtp_g, dtp_gT, bias_g, bias_gT, alog_g, alog_gT, dsk_g, dsk_gT, states, dy)


def _rms_groups(y2, ng_ref, DI):
    S = DI // SSD_G
    for g in range(SSD_G):
        gs = slice(g * S, (g + 1) * S)
        seg = y2[:, gs]
        r = lax.rsqrt(jnp.mean(seg * seg, axis=-1, keepdims=True) + RMS_EPS)
        yield gs, seg * r, r, ng_ref[:, gs]


def rms_gate_fwd(y, zx, norm_g):
    L, DI = y.shape
    tr = _tile(L, 256, 16)

    def body(y_ref, z_ref, ng_ref, o_ref):
        y2 = y_ref[...].astype(F32) * _silu(z_ref[...].astype(F32))
        for gs, yh, _, ng in _rms_groups(y2, ng_ref, DI):
            o_ref[:, gs] = (yh * ng).astype(BF16)

    return pl.pallas_call(
        body, grid=(L // tr,), in_specs=_row_specs(tr, [DI, DI]) + [_vec_spec(DI)], out_specs=_row_specs(tr, [DI])[0],
        out_shape=jax.ShapeDtypeStruct((L, DI), BF16), compiler_params=_cp(("parallel",)),
        name="rms_gate_fwd")(y, zx, norm_g)


def rms_gate_bwd(dyn, y, zx, norm_g):
    L, DI = y.shape
    tr = _tile(L, 256, 16)

    def body(dyn_ref, y_ref, z_ref, ng_ref, dy_ref, dz_ref, dng_ref):
        i = pl.program_id(0)
        yv = y_ref[...].astype(F32)
        zv = z_ref[...].astype(F32)
        sz = _silu(zv)
        dsz = _dsilu(zv)
        dynv = dyn_ref[...].astype(F32)
        for gs, yh, r, ng in _rms_groups(yv * sz, ng_ref, DI):
            dyh = dynv[:, gs] * ng
            dy2 = r * (dyh - yh * jnp.mean(dyh * yh, axis=-1, keepdims=True))
            dy_ref[:, gs] = (dy2 * sz[:, gs]).astype(BF16)
            dz_ref[:, gs] = (dy2 * yv[:, gs] * dsz[:, gs]).astype(BF16)
            s = jnp.sum(dynv[:, gs] * yh, axis=0, keepdims=True)

            @pl.when(i == 0)
            def _():
                dng_ref[:, gs] = s

            @pl.when(i > 0)
            def _():
                dng_ref[:, gs] += s

    return pl.pallas_call(
        body, grid=(L // tr,), in_specs=_row_specs(tr, [DI, DI, DI]) + [_vec_spec(DI)],
        out_specs=_row_specs(tr, [DI, DI]) + [_vec_spec(DI)],
        out_shape=[jax.ShapeDtypeStruct((L, DI), BF16)] * 2 + [jax.ShapeDtypeStruct((1, DI), F32)],
        compiler_params=_cp(("arbitrary",)), name="rms_gate_bwd")(dyn, y, zx, norm_g)


def _alibi_slope(gi, h):
    n = len(DIL_PATTERNS) * DIL_H
    return float(2.0 ** (-8.0 * (gi * DIL_H + h + 1) / n))


def _attn_masks():
    qi = lax.broadcasted_iota(jnp.int32, (DIL_BLK, DIL_BLK), 0)
    kj = lax.broadcasted_iota(jnp.int32, (DIL_BLK, DIL_BLK), 1)
    dcur = (qi - kj).astype(F32)
    return dcur, qi >= kj, dcur + float(DIL_BLK), kj >= qi


def _dil_cols(arr, col0, d):
    HW = DIL_H * DIL_E
    if d == 1:
        return arr, arr.shape[1] // HW, col0 // HW
    return arr[:, col0:col0 + HW].reshape(arr.shape[0] // d, d * HW), 1, 0


def attn_fwd(qz, kv, gi):
    window, d = DIL_PATTERNS[gi]
    assert window // d == DIL_BLK
    L, QZ = qz.shape
    KV = kv.shape[1]
    HW = DIL_H * DIL_E
    M = L // d
    nb = M // DIL_BLK
    nq, nkv = QZ // HW, KV // HW
    scale = DIL_E ** -0.5
    nt = (((1,), (1,)), ((), ()))

    def body(q_ref, kp_ref, kc_ref, vp_ref, vc_ref, o_ref, lse_ref):
        n = pl.program_id(1)
        dcur, vcur, dprev, vprev0 = _attn_masks()
        dist = jnp.concatenate([dprev, dcur], axis=1)
        valid = jnp.concatenate([jnp.logical_and(vprev0, n > 0), vcur], axis=1)
        lane = lax.broadcasted_iota(jnp.int32, (DIL_BLK, 128), 1)
        lse_acc = jnp.zeros((DIL_BLK, 128), F32)
        for h in range(DIL_H):
            hs = slice(h * DIL_E, (h + 1) * DIL_E)
            sl = _alibi_slope(gi, h) * d
            kcat = jnp.concatenate([kp_ref[:, hs], kc_ref[:, hs]], axis=0)
            vcat = jnp.concatenate([vp_ref[:, hs], vc_ref[:, hs]], axis=0)
            s = lax.dot_general(q_ref[:, hs], kcat, nt, preferred_element_type=F32) * scale - sl * dist
            s = jnp.where(valid, s, -jnp.inf)
            m = jnp.max(s, axis=-1, keepdims=True)
            p = jnp.exp(s - m)
            den = jnp.sum(p, axis=-1, keepdims=True)
            o = jnp.dot(p.astype(BF16), vcat, preferred_element_type=F32) / den
            o_ref[:, hs] = o.astype(BF16)
            lse_acc = jnp.where(lane == h, m + jnp.log(den), lse_acc)
        lse_ref[...] = lse_acc

    blk = (DIL_BLK, HW)
    prev = lambda n: jnp.maximum(n - 1, 0)
    qv, qn, qo = _dil_cols(qz, gi * HW, d)
    kv_, kn, ko = _dil_cols(kv, gi * HW, d)
    vv, vn, vo = _dil_cols(kv, (nkv // 2 + gi) * HW, d)
    o, lse = pl.pallas_call(
        body, grid=(d, nb),
        in_specs=[pl.BlockSpec(blk, lambda r, n: (n, r * qn + qo)),
                  pl.BlockSpec(blk, lambda r, n: (prev(n), r * kn + ko)),
                  pl.BlockSpec(blk, lambda r, n: (n, r * kn + ko)),
                  pl.BlockSpec(blk, lambda r, n: (prev(n), r * vn + vo)),
                  pl.BlockSpec(blk, lambda r, n: (n, r * vn + vo))],
        out_specs=[pl.BlockSpec(blk, lambda r, n: (n, r)), pl.BlockSpec((DIL_BLK, 128), lambda r, n: (n, r))],
        out_shape=[jax.ShapeDtypeStruct((M, d * HW), BF16), jax.ShapeDtypeStruct((M, d * 128), F32)],
        compiler_params=_cp(("parallel", "parallel")), name=f"attn_fwd_{gi}")(qv, kv_, kv_, vv, vv)
    return o.reshape(L, HW), lse.reshape(L, 128)


def attn_bwd(qz, kv, do, lse, dpr, gi):
    window, d = DIL_PATTERNS[gi]
    L, QZ = qz.shape
    KV = kv.shape[1]
    HW = DIL_H * DIL_E
    M = L // d
    nb = M // DIL_BLK
    nq, nkv = QZ // HW, KV // HW
    scale = DIL_E ** -0.5
    nt = (((1,), (1,)), ((), ()))
    tn = (((0,), (0,)), ((), ()))

    def body(q0_ref, q1_ref, k_ref, v_ref, do0_ref, do1_ref, l0_ref, l1_ref, r0_ref, r1_ref,
             dq_ref, dk_ref, dv_ref, carry):
        n = pl.program_id(1)

        @pl.when(n == 0)
        def _():
            carry[...] = jnp.zeros(carry.shape, F32)

        dcur, vcur, dprev, vprev0 = _attn_masks()
        dist = jnp.concatenate([dcur, dprev], axis=0)
        valid = jnp.concatenate([vcur, jnp.logical_and(vprev0, n < nb - 1)], axis=0)
        B = DIL_BLK
        for h in range(DIL_H):
            hs = slice(h * DIL_E, (h + 1) * DIL_E)
            sl = _alibi_slope(gi, h) * d
            kh = k_ref[:, hs]
            vh = v_ref[:, hs]
            qcat = jnp.concatenate([q0_ref[:, hs], q1_ref[:, hs]], axis=0)
            docat = jnp.concatenate([do0_ref[:, hs], do1_ref[:, hs]], axis=0)
            lcat = jnp.concatenate([l0_ref[:, h:h + 1], l1_ref[:, h:h + 1]], axis=0)
            rcat = jnp.concatenate([r0_ref[:, h:h + 1], r1_ref[:, h:h + 1]], axis=0)
            s = lax.dot_general(qcat, kh, nt, preferred_element_type=F32) * scale - sl * dist
            p = jnp.exp(jnp.where(valid, s - lcat, -jnp.inf))
            ds = p * (lax.dot_general(docat, vh, nt, preferred_element_type=F32) - rcat)
            ds_b = (ds * scale).astype(BF16)
            dv_ref[:, hs] = lax.dot_general(p.astype(BF16), docat, tn, preferred_element_type=F32).astype(BF16)
            dk_ref[:, hs] = lax.dot_general(ds_b, qcat, tn, preferred_element_type=F32).astype(BF16)
            dqc = jnp.dot(ds_b, kh, preferred_element_type=F32)
            dq_ref[:, hs] = (carry[:, hs] + dqc[:B]).astype(BF16)
            carry[:, hs] = dqc[B:]

    blk = (DIL_BLK, HW)
    sblk = (DIL_BLK, 128)
    nxt = lambda n: jnp.minimum(n + 1, nb - 1)
    qv, qn, qo = _dil_cols(qz, gi * HW, d)
    kv_, kn, ko = _dil_cols(kv, gi * HW, d)
    vv, vn, vo = _dil_cols(kv, (nkv // 2 + gi) * HW, d)
    dov = do.reshape(M, d * HW)
    lv = lse.reshape(M, d * 128)
    rv = dpr.reshape(M, d * 128)
    outs = pl.pallas_call(
        body, grid=(d, nb),
        in_specs=[pl.BlockSpec(blk, lambda r, n: (n, r * qn + qo)), pl.BlockSpec(blk, lambda r, n: (nxt(n), r * qn + qo)),
                  pl.BlockSpec(blk, lambda r, n: (n, r * kn + ko)),
                  pl.BlockSpec(blk, lambda r, n: (n, r * vn + vo)),
                  pl.BlockSpec(blk, lambda r, n: (n, r)), pl.BlockSpec(blk, lambda r, n: (nxt(n), r)),
                  pl.BlockSpec(sblk, lambda r, n: (n, r)), pl.BlockSpec(sblk, lambda r, n: (nxt(n), r)),
                  pl.BlockSpec(sblk, lambda r, n: (n, r)), pl.BlockSpec(sblk, lambda r, n: (nxt(n), r))],
        out_specs=[pl.BlockSpec(blk, lambda r, n: (n, r))] * 3,
        out_shape=[jax.ShapeDtypeStruct((M, d * HW), BF16)] * 3,
        scratch_shapes=[pltpu.VMEM(blk, F32)], compiler_params=_cp(("parallel", "arbitrary")),
        name=f"attn_bwd_{gi}")(qv, qv, kv_, vv, dov, dov, lv, lv, rv, rv)
    return [t.reshape(L, HW) for t in outs]


def _merge_weights(l_refs, h):
    ls = [r[:, h:h + 1] for r in l_refs]
    mx = functools.reduce(jnp.maximum, ls)
    es = [jnp.exp(l - mx) for l in ls]
    den = functools.reduce(lambda a, b: a + b, es)
    return [e / den for e in es]


def merge_fwd(os_, lses, qz):
    L, HW = os_[0].shape
    tr = _tile(L, 256, 16)
    ng = len(os_)
    zblk = qz.shape[1] // HW - 1

    def body(*refs):
        o_refs, l_refs, z_ref, out_ref = refs[:ng], refs[ng:2 * ng], refs[2 * ng], refs[2 * ng + 1]
        for h in range(DIL_H):
            hs = slice(h * DIL_E, (h + 1) * DIL_E)
            ws = _merge_weights(l_refs, h)
            om = functools.reduce(lambda a, b: a + b, [w * o[:, hs].astype(F32) for w, o in zip(ws, o_refs)])
            out_ref[:, hs] = (om * _silu(z_ref[:, hs].astype(F32))).astype(BF16)

    return pl.pallas_call(
        body, grid=(L // tr,),
        in_specs=_row_specs(tr, [HW] * ng + [128] * ng) + [pl.BlockSpec((tr, HW), lambda i: (i, zblk))],
        out_specs=_row_specs(tr, [HW])[0], out_shape=jax.ShapeDtypeStruct((L, HW), BF16),
        compiler_params=_cp(("parallel",)), name="merge_fwd")(*os_, *lses, qz)


def merge_bwd(dgated, os_, lses, qz):
    L, HW = os_[0].shape
    tr = _tile(L, 256, 16)
    ng = len(os_)
    zblk = qz.shape[1] // HW - 1

    def body(*refs):
        dg_ref = refs[0]
        o_refs, l_refs, z_ref = refs[1:1 + ng], refs[1 + ng:1 + 2 * ng], refs[1 + 2 * ng]
        outs = refs[2 + 2 * ng:]
        do_refs, dpr_refs, dz_ref = outs[:ng], outs[ng:2 * ng], outs[2 * ng]
        lane = lax.broadcasted_iota(jnp.int32, (tr, 128), 1)
        accs = [jnp.zeros((tr, 128), F32) for _ in range(ng)]
        for h in range(DIL_H):
            hs = slice(h * DIL_E, (h + 1) * DIL_E)
            ws = _merge_weights(l_refs, h)
            ov = [o[:, hs].astype(F32) for o in o_refs]
            om = functools.reduce(lambda a, b: a + b, [w * o for w, o in zip(ws, ov)])
            zv = z_ref[:, hs].astype(F32)
            dgv = dg_ref[:, hs].astype(F32)
            dom = dgv * _silu(zv)
            dz_ref[:, hs] = (dgv * om * _dsilu(zv)).astype(BF16)
            dws = [jnp.sum(dom * o, axis=-1, keepdims=True) for o in ov]
            dwbar = functools.reduce(lambda a, b: a + b, [w * dw for w, dw in zip(ws, dws)])
            for g in range(ng):
                do_refs[g][:, hs] = (ws[g] * dom).astype(BF16)
                accs[g] = jnp.where(lane == h, ws[g] * dwbar, accs[g])
        for g in range(ng):
            dpr_refs[g][...] = accs[g]

    outs = pl.pallas_call(
        body, grid=(L // tr,),
        in_specs=_row_specs(tr, [HW] * (1 + ng) + [128] * ng) + [pl.BlockSpec((tr, HW), lambda i: (i, zblk))],
        out_specs=_row_specs(tr, [HW] * ng + [128] * ng + [HW]),
        out_shape=[jax.ShapeDtypeStruct((L, HW), BF16)] * ng + [jax.ShapeDtypeStruct((L, 128), F32)] * ng
        + [jax.ShapeDtypeStruct((L, HW), BF16)],
        compiler_params=_cp(("parallel",)), name="merge_bwd")(dgated, *os_, *lses, qz)
    return outs[:ng], outs[ng:2 * ng], outs[2 * ng]


def ada_fwd(c8, ada_w):
    nl, D, Ws = ada_w.shape
    tn = _tile(Ws, 512)

    def body(c_ref, w_ref, o_ref):
        o_ref[...] = jnp.dot(_silu(c_ref[...]), w_ref[...], precision=lax.Precision.HIGHEST,
                             preferred_element_type=F32)

    return pl.pallas_call(
        body, grid=(nl, Ws // tn),
        in_specs=[pl.BlockSpec((N_DEV, D), lambda l, j: (0, 0)), pl.BlockSpec((None, D, tn), lambda l, j: (l, 0, j))],
        out_specs=pl.BlockSpec((None, N_DEV, tn), lambda l, j: (l, 0, j)),
        out_shape=jax.ShapeDtypeStruct((nl, N_DEV, Ws), F32), compiler_params=_cp(("parallel", "parallel")),
        name="ada_fwd")(c8, ada_w)


def ada_wgrad(c8t, dmod):
    nl, _, Ws = dmod.shape
    D = c8t.shape[0]
    tm = _tile(D, 512, 8)

    def body(c_ref, d_ref, o_ref):
        sc = _silu(c_ref[...])
        acc = sc[:, 0:1] * d_ref[0:1, :]
        for e in range(1, N_DEV):
            acc = acc + sc[:, e:e + 1] * d_ref[e:e + 1, :]
        o_ref[...] = acc

    return pl.pallas_call(
        body, grid=(nl, D // tm),
        in_specs=[pl.BlockSpec((tm, N_DEV), lambda l, i: (i, 0)), pl.BlockSpec((None, N_DEV, Ws), lambda l, i: (l, 0, 0))],
        out_specs=pl.BlockSpec((None, tm, Ws), lambda l, i: (l, i, 0)),
        out_shape=jax.ShapeDtypeStruct((nl, D, Ws), F32), compiler_params=_cp(("parallel", "parallel")),
        name="ada_wgrad")(c8t, dmod)


def adamw(w, g, m, v, name):
    R, C = w.shape
    tr = _tile(R, 256, 8)
    c1 = 1.0 - ADAM_B1 ** ADAM_STEP
    c2 = 1.0 - ADAM_B2 ** ADAM_STEP

    def body(w_ref, g_ref, m_ref, v_ref, d_ref, nm_ref, nv_ref):
        gv = g_ref[...]
        nm = ADAM_B1 * m_ref[...] + (1.0 - ADAM_B1) * gv
        nv = ADAM_B2 * v_ref[...] + (1.0 - ADAM_B2) * (gv * gv)
        nm_ref[...] = nm
        nv_ref[...] = nv
        d_ref[...] = -ADAM_LR * ((nm / c1) / (jnp.sqrt(nv / c2) + ADAM_EPS) + ADAM_WD * w_ref[...])

    return pl.pallas_call(
        body, grid=(R // tr,), in_specs=_row_specs(tr, [C] * 4), out_specs=_row_specs(tr, [C] * 3),
        out_shape=[jax.ShapeDtypeStruct((R, C), F32)] * 3, compiler_params=_cp(("parallel",)), name=name)(w, g, m, v)


def sum_leading(t, name, out_dtype=F32):
    S, R, C = t.shape
    tr = _tile(R, 256, 16)

    def body(t_ref, o_ref):
        acc = t_ref[0].astype(F32)
        for s in range(1, S):
            acc = acc + t_ref[s].astype(F32)
        o_ref[...] = acc.astype(out_dtype)

    return pl.pallas_call(
        body, grid=(R // tr,), in_specs=[pl.BlockSpec((S, tr, C), lambda i: (0, i, 0))],
        out_specs=pl.BlockSpec((tr, C), lambda i: (i, 0)), out_shape=jax.ShapeDtypeStruct((R, C), out_dtype),
        compiler_params=_cp(("parallel",)), name=name)(t)


def add_half(g, a, core, name):
    S, R, C = g.shape
    h = R // 2
    tr = _tile(h, 256, 16)
    nb = h // tr

    def body(core_ref, g_ref, a_ref, o_ref):
        o_ref[...] = (g_ref[...].astype(F32) + a_ref[...].astype(F32)).astype(BF16)

    return pl.pallas_call(
        body,
        grid_spec=pltpu.PrefetchScalarGridSpec(
            num_scalar_prefetch=1, grid=(S, nb),
            in_specs=[pl.BlockSpec((None, tr, C), lambda s, i, core_ref: (s, core_ref[0] * nb + i, 0)),
                      pl.BlockSpec((None, tr, C), lambda s, i, core_ref: (s, i, 0))],
            out_specs=pl.BlockSpec((None, tr, C), lambda s, i, core_ref: (s, i, 0))),
        out_shape=jax.ShapeDtypeStruct((S, h, C), BF16), compiler_params=_cp(("parallel", "parallel")),
        name=name)(core, g, a)


def sum_partials(own, landed, chip, name):
    _, h, C = own.shape
    tr = _tile(h, 256, 16)

    def body(chip_ref, own_ref, l_ref, o_ref):
        acc = own_ref[...].astype(F32)
        for j in range(3):
            acc = acc + l_ref[j].astype(F32)
        o_ref[...] = acc

    return pl.pallas_call(
        body,
        grid_spec=pltpu.PrefetchScalarGridSpec(
            num_scalar_prefetch=1, grid=(h // tr,),
            in_specs=[pl.BlockSpec((None, tr, C), lambda i, chip_ref: (chip_ref[0], i, 0)),
                      pl.BlockSpec((3, tr, C), lambda i, chip_ref: (0, i, 0))],
            out_specs=pl.BlockSpec((tr, C), lambda i, chip_ref: (i, 0))),
        out_shape=jax.ShapeDtypeStruct((h, C), F32), compiler_params=_cp(("parallel",)), name=name)(chip, own, landed)


def adamw_halves(w, g_mine, g_theirs, m, v, core, name):
    R, C = w.shape
    h = R // 2
    tr = _tile(h, 256, 8)
    nbh = h // tr
    c1 = 1.0 - ADAM_B1 ** ADAM_STEP
    c2 = 1.0 - ADAM_B2 ** ADAM_STEP

    def body(core_ref, w_ref, gm_ref, gt_ref, m_ref, v_ref, g_ref, d_ref, nm_ref, nv_ref):
        mine = (pl.program_id(0) // nbh) == core_ref[0]
        gv = jnp.where(mine, gm_ref[...], gt_ref[...])
        g_ref[...] = gv
        nm = ADAM_B1 * m_ref[...] + (1.0 - ADAM_B1) * gv
        nv = ADAM_B2 * v_ref[...] + (1.0 - ADAM_B2) * (gv * gv)
        nm_ref[...] = nm
        nv_ref[...] = nv
        d_ref[...] = -ADAM_LR * ((nm / c1) / (jnp.sqrt(nv / c2) + ADAM_EPS) + ADAM_WD * w_ref[...])

    full = pl.BlockSpec((tr, C), lambda i, core_ref: (i, 0))
    halfspec = pl.BlockSpec((tr, C), lambda i, core_ref: (i % nbh, 0))
    return pl.pallas_call(
        body,
        grid_spec=pltpu.PrefetchScalarGridSpec(
            num_scalar_prefetch=1, grid=(2 * nbh,), in_specs=[full, halfspec, halfspec, full, full],
            out_specs=[full] * 4),
        out_shape=[jax.ShapeDtypeStruct((R, C), F32)] * 4, compiler_params=_cp(("parallel",)),
        name=name)(core, w, g_mine, g_theirs, m, v)


_ANY = pl.BlockSpec(memory_space=pl.ANY)


def _place():
    x, y, c = lax.axis_index("x"), lax.axis_index("y"), lax.axis_index("c")
    chips = [(1 - x, y), (x, 1 - y), (1 - x, 1 - y)]
    return x, y, c, chips


def allgather_small(v, name, after=None):
    R, W = v.shape
    extra = [] if after is None else [after]

    def body(x_ref, *rest):
        out_ref, send_sems, recv_sems, local_sem = rest[len(extra):]
        x, y, c, chips = _place()
        me, sibling = (x, y, c), (x, y, 1 - c)

        def rows(px, py, pc):
            return out_ref.at[pl.ds((4 * px + 2 * py + pc) * R, R), :]

        def copy(k, block, to, src=None):
            return pltpu.make_async_remote_copy(
                src_ref=rows(*block) if src is None else src, dst_ref=rows(*block),
                send_sem=send_sems.at[k], recv_sem=recv_sems.at[k], device_id=to, device_id_type=MESH)

        mine = pltpu.make_async_copy(x_ref, rows(*me), local_sem)
        mine.start()
        first = [copy(0, me, sibling, src=x_ref)]
        first += [copy(1 + j, me, (*chip, c), src=x_ref) for j, chip in enumerate(chips)]
        for cp in first:
            cp.start()
        passed = [copy(4 + j, (*chip, c), sibling) for j, chip in enumerate(chips)]
        for j, chip in enumerate(chips):
            copy(1 + j, (*chip, c), me).wait_recv()
            passed[j].start()
        copy(0, sibling, me).wait_recv()
        for j, chip in enumerate(chips):
            copy(4 + j, (*chip, 1 - c), me).wait_recv()
        for cp in first + passed:
            cp.wait_send()
        mine.wait()

    return pl.pallas_call(
        body, out_shape=jax.ShapeDtypeStruct((N_DEV * R, W), v.dtype),
        in_specs=[pl.BlockSpec(memory_space=pltpu.VMEM)] + [_ANY] * len(extra),
        out_specs=pl.BlockSpec(memory_space=pltpu.VMEM),
        scratch_shapes=[pltpu.SemaphoreType.DMA((7,)), pltpu.SemaphoreType.DMA((7,)), pltpu.SemaphoreType.DMA],
        name=name)(v, *extra)


def allgather_weights(shards, name="allgather_weights"):
    n = len(shards)

    def body(*refs):
        ins, outs = refs[:n], refs[n:2 * n]
        send_sems, recv_sems = refs[2 * n:]
        x, y, c, chips = _place()
        p = 2 * x + y
        sibling = (x, y, 1 - c)

        def half(i, chip_id, core, ref=None):
            r = outs[i].at[chip_id] if ref is None else ref
            return r.at[core]

        def copy(i, k, chip_id, core, to, src=None):
            return pltpu.make_async_remote_copy(
                src_ref=half(i, chip_id, core) if src is None else src, dst_ref=half(i, chip_id, core),
                send_sem=send_sems.at[6 * i + k], recv_sem=recv_sems.at[6 * i + k], device_id=to, device_id_type=MESH)

        first = [copy(i, j, p, c, (*chip, c), src=half(i, p, c, ref=ins[i]))
                 for i in range(n) for j, chip in enumerate(chips)]
        for cp in first:
            cp.start()
        passed = []
        for i in range(n):
            for j, (cx, cy) in enumerate(chips):
                copy(i, j, 2 * cx + cy, c, sibling).wait_recv()
                fw = copy(i, 3 + j, 2 * cx + cy, c, sibling)
                fw.start()
                passed.append(fw)
        for i in range(n):
            for j, (cx, cy) in enumerate(chips):
                copy(i, 3 + j, 2 * cx + cy, 1 - c, sibling).wait_recv()
        for cp in first + passed:
            cp.wait_send()

    split = [s.reshape(2, s.shape[0] // 2, s.shape[1]) for s in shards]
    outs = pl.pallas_call(
        body, out_shape=[jax.ShapeDtypeStruct((N_CHIPS,) + s.shape, s.dtype) for s in split],
        in_specs=[_ANY] * n, out_specs=[_ANY] * n,
        scratch_shapes=[pltpu.SemaphoreType.DMA((6 * n,)), pltpu.SemaphoreType.DMA((6 * n,))],
        name=name)(*split)
    chip = 2 * lax.axis_index("x") + lax.axis_index("y")
    return [lax.dynamic_update_index_in_dim(o, s, chip, 0).reshape((N_CHIPS,) + sh.shape)
            for o, s, sh in zip(outs, split, shards)]


_HBM = pl.BlockSpec(memory_space=pltpu.HBM)
_SEM = pl.BlockSpec(memory_space=pltpu.SEMAPHORE)
_EFFECT = pltpu.SideEffectType.DATAFLOW_SIDE_EFFECTING


def _chip_copies(kind, srcs, lands, send_sems, recv_sems):
    x, y, c, chips = _place()
    p = 2 * x + y
    cps = []
    for i in range(len(srcs)):
        for j, (cx, cy) in enumerate(chips):
            if kind == "gather":
                src, dst = srcs[i].at[c], lands[i].at[p, c]
            else:
                src, dst = srcs[i].at[2 * cx + cy], lands[i].at[j]
            cps.append(pltpu.make_async_remote_copy(
                src_ref=src, dst_ref=dst, send_sem=send_sems.at[3 * i + j], recv_sem=recv_sems.at[3 * i + j],
                device_id=(cx, cy, c), device_id_type=MESH))
    return cps


def split_start(kind, srcs, land_shapes, after, name):
    n = len(srcs)

    def body(*refs):
        src_refs, land_refs = refs[:n], refs[n:2 * n]
        send_sems, recv_sems = refs[2 * n + 1], refs[2 * n + 2]
        token = refs[-1]
        for cp in _chip_copies(kind, src_refs, land_refs, send_sems, recv_sems):
            cp.start()
        token[...] = jnp.zeros_like(token)

    lands = [pltpu.with_memory_space_constraint(lax.empty(s, BF16), pltpu.HBM) for s in land_shapes]
    outs = pl.pallas_call(
        body, name=name,
        out_shape=(pltpu.SemaphoreType.DMA((3 * n,)), pltpu.SemaphoreType.DMA((3 * n,)),
                   *[pltpu.HBM(s.shape, s.dtype) for s in srcs], *[pltpu.HBM(s, BF16) for s in land_shapes],
                   jax.ShapeDtypeStruct((8, 128), F32)),
        in_specs=[_HBM] * (2 * n) + [_ANY],
        out_specs=(_SEM, _SEM, *([_HBM] * (2 * n)), pl.BlockSpec(memory_space=pltpu.VMEM)),
        input_output_aliases={i: 2 + i for i in range(2 * n)},
        compiler_params=pltpu.CompilerParams(has_side_effects=_EFFECT),
    )(*[pltpu.with_memory_space_constraint(s, pltpu.HBM) for s in srcs], *lands, after)
    return outs[0], outs[1], outs[2:2 + n], outs[2 + n:2 + 2 * n], outs[-1]


def split_wait(kind, send_sems, recv_sems, srcs, lands, after, name):
    n = len(srcs)

    def body(*refs):
        src_refs, land_refs = refs[:n], refs[n:2 * n]
        ssem, rsem = refs[2 * n], refs[2 * n + 1]
        for cp in _chip_copies(kind, src_refs, land_refs, ssem, rsem):
            cp.wait_send()
            cp.wait_recv()

    outs = pl.pallas_call(
        body, name=name,
        out_shape=[pltpu.HBM(s.shape, s.dtype) for s in srcs] + [pltpu.HBM(s.shape, s.dtype) for s in lands],
        in_specs=[_HBM] * (2 * n) + [_SEM, _SEM, _ANY], out_specs=[_HBM] * (2 * n),
        input_output_aliases={i: i for i in range(2 * n)},
        compiler_params=pltpu.CompilerParams(has_side_effects=_EFFECT),
    )(*srcs, *lands, send_sems, recv_sems, after)
    return outs[:n], outs[n:]


def pass_to_sibling(lands):
    n = len(lands)

    def body(*refs):
        ins, outs = refs[:n], refs[n:2 * n]
        send_sems, recv_sems = refs[2 * n:]
        x, y, c, chips = _place()
        cps = []
        for i in range(n):
            for j, (cx, cy) in enumerate(chips):
                blk = outs[i].at[2 * cx + cy, c]
                cps.append(pltpu.make_async_remote_copy(
                    src_ref=ins[i].at[2 * cx + cy, c], dst_ref=blk, send_sem=send_sems.at[3 * i + j],
                    recv_sem=recv_sems.at[3 * i + j], device_id=(x, y, 1 - c), device_id_type=MESH))
        for cp in cps:
            cp.start()
        for cp in cps:
            cp.wait()

    return pl.pallas_call(
        body, out_shape=[jax.ShapeDtypeStruct(t.shape, t.dtype) for t in lands], in_specs=[_ANY] * n,
        out_specs=[_ANY] * n, input_output_aliases={i: i for i in range(n)},
        scratch_shapes=[pltpu.SemaphoreType.DMA((3 * n,)), pltpu.SemaphoreType.DMA((3 * n,))],
        name="ag_pass_to_sibling")(*lands)


def exchange_halves_to_sibling(gs, name):
    n = len(gs)

    def body(*refs):
        ins, outs = refs[:n], refs[n:2 * n]
        send_sems, recv_sems = refs[2 * n:]
        x, y, c, _ = _place()
        cps = []
        for i in range(n):
            h = ins[i].shape[1] // 2
            cps.append(pltpu.make_async_remote_copy(
                src_ref=ins[i].at[:, pl.ds((1 - c) * h, h), :], dst_ref=outs[i],
                send_sem=send_sems.at[i], recv_sem=recv_sems.at[i], device_id=(x, y, 1 - c), device_id_type=MESH))
        for cp in cps:
            cp.start()
        for cp in cps:
            cp.wait()

    return pl.pallas_call(
        body, out_shape=[jax.ShapeDtypeStruct((g.shape[0], g.shape[1] // 2, g.shape[2]), g.dtype) for g in gs],
        in_specs=[_ANY] * n, out_specs=[_ANY] * n,
        scratch_shapes=[pltpu.SemaphoreType.DMA((n,)), pltpu.SemaphoreType.DMA((n,))],
        name=name)(*gs)


def scatter_to_chips(ps, name):
    n = len(ps)

    def body(*refs):
        ins, outs = refs[:n], refs[n:2 * n]
        send_sems, recv_sems = refs[2 * n:]
        x, y, c, chips = _place()
        cps = []
        for i in range(n):
            for j, (cx, cy) in enumerate(chips):
                cps.append(pltpu.make_async_remote_copy(
                    src_ref=ins[i].at[2 * cx + cy], dst_ref=outs[i].at[j], send_sem=send_sems.at[3 * i + j],
                    recv_sem=recv_sems.at[3 * i + j], device_id=(cx, cy, c), device_id_type=MESH))
        for cp in cps:
            cp.start()
        for cp in cps:
            cp.wait()

    return pl.pallas_call(
        body, out_shape=[jax.ShapeDtypeStruct((3,) + t.shape[1:], t.dtype) for t in ps],
        in_specs=[_ANY] * n, out_specs=[_ANY] * n,
        scratch_shapes=[pltpu.SemaphoreType.DMA((3 * n,)), pltpu.SemaphoreType.DMA((3 * n,))],
        name=name)(*ps)


def join_halves(rs, name):
    n = len(rs)

    def body(*refs):
        ins, outs = refs[:n], refs[n:2 * n]
        send_sems, recv_sems = refs[2 * n:]
        x, y, c, _ = _place()
        cps = [pltpu.make_async_remote_copy(
            src_ref=ins[i], dst_ref=outs[i], send_sem=send_sems.at[i], recv_sem=recv_sems.at[i],
            device_id=(x, y, 1 - c), device_id_type=MESH) for i in range(n)]
        for cp in cps:
            cp.start()
        for cp in cps:
            cp.wait()

    return pl.pallas_call(
        body, out_shape=[jax.ShapeDtypeStruct(r.shape, r.dtype) for r in rs],
        in_specs=[_ANY] * n, out_specs=[_ANY] * n,
        scratch_shapes=[pltpu.SemaphoreType.DMA((n,)), pltpu.SemaphoreType.DMA((n,))],
        name=name)(*rs)


def _pack(parts, row_mult=8):
    flat = jnp.concatenate([p.reshape(-1).astype(F32) for p in parts])
    unit = row_mult * 128
    n = -(-flat.shape[0] // unit) * unit
    return jnp.pad(flat, (0, n - flat.shape[0])).reshape(n // 128, 128)


def _unpack(flat, shapes):
    out, off = [], 0
    for s in shapes:
        n = int(np.prod(s))
        out.append(flat[off:off + n].reshape(s))
        off += n
    return out


def _gather_packed(parts, name):
    packed = _pack(parts)
    g = allgather_small(packed, name).reshape(N_DEV, -1)
    return _unpack_rows(g, [p.shape for p in parts])


def _unpack_rows(g, shapes):
    out, off = [], 0
    for s in shapes:
        n = int(np.prod(s))
        out.append(g[:, off:off + n].reshape((g.shape[0],) + tuple(s)))
        off += n
    return out


def _by_chip(t, axis):
    return jnp.concatenate([t[2 * p] for p in range(N_CHIPS)], axis=axis)


def kernel(x, c, ada_w, ada_b, ln_g, ln_b, a_in_w, a_conv_w, a_conv_b, a_dt_bias, a_A_log, a_D, a_norm_g, a_out_w, kv_w, b_in_w, b_out_w, loss_target, m_ada_w, m_ada_b, m_ln_g, m_ln_b, m_a_in_w, m_a_conv_w, m_a_conv_b, m_a_dt_bias, m_a_A_log, m_a_D, m_a_norm_g, m_a_out_w, m_kv_w, m_b_in_w, m_b_out_w, v_ada_w, v_ada_b, v_ln_g, v_ln_b, v_a_in_w, v_a_conv_w, v_a_conv_b, v_a_dt_bias, v_a_A_log, v_a_D, v_a_norm_g, v_a_out_w, v_kv_w, v_b_in_w, v_b_out_w):
    ax, ay, ac = lax.axis_index("x"), lax.axis_index("y"), lax.axis_index("c")
    chip = 2 * ax + ay
    dev = 4 * ax + 2 * ay + ac
    xin = x[0]
    tgt = loss_target[0]
    L, D = xin.shape
    G, P = SSD_G, SSD_P
    H = a_dt_bias.shape[1]
    Kh = H // G
    DI = H * P
    CONVD = a_conv_b.shape[1] * N_CHIPS
    HW = DIL_H * DIL_E
    Ws = ada_w.shape[2]

    (w_in_g,) = allgather_weights([a_in_w[0].astype(BF16)], "allgather_w_in")
    later = [a_out_w[0].astype(BF16), kv_w.astype(BF16), b_in_w[0].astype(BF16), b_out_w[0].astype(BF16)]
    later_split = [s.reshape(2, s.shape[0] // 2, s.shape[1]) for s in later]
    ag_ssem, ag_rsem, ag_srcs, ag_lands, ag_token = split_start(
        "gather", later_split, [(N_CHIPS,) + s.shape for s in later_split], w_in_g, "ag_later_start")
    w_in = jnp.transpose(w_in_g, (1, 0, 2)).reshape(D, -1)
    w_zx = w_in
    w_dt = jnp.pad(w_in[:, DI + CONVD:], ((0, 0), (0, 128 - H)))

    c8, cw8, cb8, ng8 = _gather_packed([c[0], a_conv_w[0], a_conv_b[0], a_norm_g[0]], "allgather_small_params")
    conv_w = _by_chip(cw8, 1)
    conv_b = _by_chip(cb8, 0).reshape(1, CONVD)
    norm_g = _by_chip(ng8, 0).reshape(1, DI)

    mod_s = ada_fwd(c8, ada_w)
    (mod8,) = _gather_packed([mod_s], "allgather_small_mod")
    mods = _by_chip(mod8, 2)
    mod = lax.dynamic_index_in_dim(mods, dev, axis=1, keepdims=False) + ada_b
    shift = [mod[l:l + 1, :D] for l in range(DEPTH)]
    scale = [mod[l:l + 1, D:2 * D] for l in range(DEPTH)]
    gate = [mod[l:l + 1, 2 * D:] for l in range(DEPTH)]
    lg = [ln_g[l:l + 1] for l in range(DEPTH)]
    lb = [ln_b[l:l + 1] for l in range(DEPTH)]

    h0 = modulate(xin, scale[0] + ag_token[0:1, 0:1], shift[0], "modulate0")
    zx = mm_nn(h0, w_zx, BF16, "mm_in_zx", n_cols=DI + CONVD)
    dtp = mm_nn(h0, w_dt, F32, "mm_in_dt")
    xbc = conv_fwd(zx, DI, conv_w, conv_b)
    dtp_g = jnp.transpose(dtp[:, :H].reshape(L, G, Kh), (1, 0, 2))
    dtp_gT = jnp.transpose(dtp_g, (0, 2, 1))
    vecs = [a_dt_bias.reshape(G, 1, Kh), a_dt_bias.reshape(G, Kh, 1), a_A_log.reshape(G, 1, Kh),
            a_A_log.reshape(G, Kh, 1), a_D.reshape(G, 1, Kh), a_D.reshape(G, Kh, 1)]
    y_ssd, states = ssd_fwd(xbc, dtp_g, dtp_gT, *vecs, DI)
    yn = rms_gate_fwd(y_ssd, zx, norm_g)
    later_split, ag_lands = split_wait("gather", ag_ssem, ag_rsem, ag_srcs, ag_lands, yn, "ag_later_wait")
    ag_lands = pass_to_sibling(ag_lands)
    w_out_g, w_kv_g, w_bin_g, w_bout_g = [
        lax.dynamic_update_index_in_dim(o, s, chip, 0).reshape((N_CHIPS,) + full.shape)
        for o, s, full in zip(ag_lands, later_split, later)]
    ymix0 = mm_nn(yn, w_out_g, F32, "mm_out_a", stack="row")
    x1, x1b, h1 = ln_mid(xin, ymix0, gate[0], lg[0], lb[0], scale[1], shift[1])

    kvp = mm_nn(x1b, w_kv_g, BF16, "mm_kv", stack="col")
    qz = mm_nn(h1, w_bin_g, BF16, "mm_in_b", stack="col")
    os_, lses = [], []
    for gi in range(len(DIL_PATTERNS)):
        o, lse = attn_fwd(qz, kvp, gi)
        os_.append(o)
        lses.append(lse)
    om = merge_fwd(os_, lses, qz)
    ymix1 = mm_nn(om, w_bout_g, F32, "mm_out_b", stack="col")
    dres2, dy2, dg1, db1, dgate1, sq = ln_final_fwd_bwd(x1, ymix1, gate[1], lg[1], lb[1], tgt)
    loss_part = 0.5 * jnp.sum(sq) / D

    g_bout = mm_tn(om, dy2, BF16, "mm_gw_out_b", stack="col")
    dgated = mm_nt(dy2, w_bout_g, BF16, "mm_gx_out_b", stack="col")
    dos, dprs, dz_b = merge_bwd(dgated, os_, lses, qz)
    dqs, dks, dvs = [], [], []
    for gi in range(len(DIL_PATTERNS)):
        dq, dk, dv = attn_bwd(qz, kvp, dos[gi], lses[gi], dprs[gi], gi)
        dqs.append(dq)
        dks.append(dk)
        dvs.append(dv)
    dqz = jnp.concatenate(dqs + [dz_b], axis=1)
    dkv = jnp.concatenate(dks + dvs, axis=1)
    g_bin = mm_tn(h1, dqz, BF16, "mm_gw_in_b", stack="col")
    dh1 = mm_nt(dqz, w_bin_g, F32, "mm_gx_in_b", stack="col")
    g_kv = mm_tn(x1b, dkv, BF16, "mm_gw_kv", stack="col")
    dx1_kv = mm_nt(dkv, w_kv_g, F32, "mm_gx_kv", stack="col")

    core = ac.astype(jnp.int32).reshape(1)
    chip_i = chip.astype(jnp.int32).reshape(1)

    def begin_scatter(gs, nms, tag):
        sib = exchange_halves_to_sibling(gs, "rs_sibling_exchange_" + tag)
        parts = [add_half(g, a, core, "rs_add_" + nm) for g, a, nm in zip(gs, sib, nms)]
        return split_start("scatter", parts, [(3,) + t.shape[1:] for t in parts], parts[0], "rs_%s_start" % tag)

    def finish_scatter(handles, after, tag):
        nms, owns, landed = [], [], []
        for k, (handle, hn) in enumerate(handles):
            parts, lands = split_wait("scatter", handle[0], handle[1], handle[2], handle[3], after,
                                      "rs_%s%d_wait" % (tag, k))
            nms += hn
            owns += list(parts)
            landed += list(lands)
        halves = [sum_partials(own, t, chip_i, "rs_sum_" + nm) for own, t, nm in zip(owns, landed, nms)]
        theirs = join_halves(halves, "rs_join_halves_" + tag)
        return dict(zip(nms, zip(halves, theirs)))

    names_b = ["kv", "in_b", "out_b"]
    rs_b = begin_scatter([g_kv, g_bin, g_bout], names_b, "b")

    dres1, dy1, dg0, db0, dgate0, dscale1, dshift1 = mod_ln_bwd(
        dres2, dh1, dx1_kv, x1, scale[1], xin, ymix0, gate[0] + rs_b[4][0:1, 0:1], lg[0])
    g_out = mm_tn(yn, dy1, BF16, "mm_gw_out_a", stack="row")
    rs_a1 = begin_scatter([g_out], ["out_a"], "a1")
    dyn = mm_nt(dy1, w_out_g, BF16, "mm_gx_out_a", stack="row")
    dy_ssd, dz_a, dnorm_g = rms_gate_bwd(dyn, y_ssd, zx, norm_g + rs_a1[4][0:1, 0:1])
    dxs, dB, dC, ddtp_g, dbias_g, dalog_g, dD_g = ssd_bwd(xbc, dtp_g, dtp_gT, *vecs, states, dy_ssd, DI)
    dxbc = jnp.concatenate([dxs, dB, dC], axis=1)
    dxbc_pre, dconv_w, dconv_b = conv_bwd(zx, DI, conv_w, conv_b, dxbc)
    dzx = jnp.concatenate([dz_a, dxbc_pre], axis=1)
    ddtp = jnp.pad(jnp.transpose(ddtp_g, (1, 0, 2)).reshape(L, H), ((0, 0), (0, 128 - H)))
    g_zx = mm_tn(h0, dzx, BF16, "mm_gw_in_zx")
    g_dt = mm_tn(h0, ddtp, BF16, "mm_gw_in_dt")
    g_in = jnp.concatenate([g_zx, g_dt[:, :H]], axis=1)
    cs_in = g_in.shape[1] // N_CHIPS
    g_in = jnp.stack([g_in[:, s * cs_in:(s + 1) * cs_in] for s in range(N_CHIPS)])
    rs_a2 = begin_scatter([g_in], ["in_a"], "a2")
    dh0 = mm_nt(dzx, w_zx, F32, "mm_gx_in_zx", after=rs_a2[4])
    dh0_dt = mm_nt(ddtp, w_dt, F32, "mm_gx_in_dt")
    grad_x, dscale0, dshift0 = mod_bwd(dres1, dh0, dh0_dt, xin, scale[0] + rs_a2[4][0:1, 0:1], "mod_bwd0",
                                       through_mod=True)
    g_halves = finish_scatter([(rs_b, names_b)], grad_x, "b")

    def step_halves(w, m, v, nm):
        shp = w.shape
        mine, theirs_ = g_halves[nm]
        outs4 = adamw_halves(w.reshape(-1, shp[-1]), mine, theirs_, m.reshape(-1, shp[-1]), v.reshape(-1, shp[-1]),
                             core, "adamw_" + nm)
        return tuple(t.reshape(shp) for t in outs4)

    big = {
        "kv_w": step_halves(kv_w, m_kv_w, v_kv_w, "kv"),
        "b_in_w": step_halves(b_in_w, m_b_in_w, v_b_in_w, "in_b"),
        "b_out_w": step_halves(b_out_w, m_b_out_w, v_b_out_w, "out_b"),
    }
    g_halves.update(finish_scatter([(rs_a1, ["out_a"]), (rs_a2, ["in_a"])], big["kv_w"][1], "a"))
    big["a_in_w"] = step_halves(a_in_w, m_a_in_w, v_a_in_w, "in_a")
    big["a_out_w"] = step_halves(a_out_w, m_a_out_w, v_a_out_w, "out_a")

    dmod = jnp.concatenate([jnp.concatenate([dshift0, dscale0, dgate0], axis=1),
                            jnp.concatenate([dshift1, dscale1, dgate1], axis=1)], axis=0)
    small_parts = [jnp.concatenate([dg0, dg1], axis=0), jnp.concatenate([db0, db1], axis=0),
                   dbias_g.reshape(1, H), dalog_g.reshape(1, H), dD_g.reshape(1, H),
                   dconv_w, dconv_b, dnorm_g, loss_part.reshape(1, 1)]
    small_shapes = [p.shape for p in small_parts]
    packed = jnp.concatenate([_pack([dmod]), _pack(small_parts)], axis=0)
    n_mod_rows = _pack([dmod]).shape[0]
    gathered = allgather_small(packed, "allgather_small_grads", after=g_halves["in_a"][1]).reshape(N_DEV, -1, 128)
    dmod8 = gathered[:, :n_mod_rows].reshape(N_DEV, -1)[:, :2 * 3 * D].reshape(N_DEV, DEPTH, 3 * D)
    summed = sum_leading(gathered, "sum_small")
    g_ada_b = summed[:n_mod_rows].reshape(-1)[:2 * 3 * D].reshape(DEPTH, 3 * D)
    (g_ln_g, g_ln_b, g_dt_bias, g_a_log, g_dsk, g_conv_w, g_conv_b, g_norm_g, loss_all) = _unpack(
        summed[n_mod_rows:].reshape(-1), small_shapes)
    loss = loss_all.reshape(())
    Cs = CONVD // N_CHIPS
    g_conv_w_s = lax.dynamic_slice_in_dim(g_conv_w, chip * Cs, Cs, axis=1)
    g_conv_b_s = lax.dynamic_slice_in_dim(g_conv_b, chip * Cs, Cs, axis=1)
    g_norm_g_s = lax.dynamic_slice_in_dim(g_norm_g, chip * (DI // N_CHIPS), DI // N_CHIPS, axis=1)
    dmod_s = jnp.transpose(lax.dynamic_slice_in_dim(dmod8, chip * Ws, Ws, axis=2), (1, 0, 2))
    g_ada_w = ada_wgrad(jnp.transpose(c8), dmod_s)

    def step2d(w, g, m, v, nm):
        shp = w.shape
        d_, m_, v_ = adamw(w.reshape(-1, shp[-1]), g.reshape(-1, shp[-1]), m.reshape(-1, shp[-1]),
                           v.reshape(-1, shp[-1]), "adamw_" + nm)
        return g.reshape(shp), d_.reshape(shp), m_.reshape(shp), v_.reshape(shp)

    big["ada_w"] = step2d(ada_w, g_ada_w, m_ada_w, v_ada_w, "ada_w")
    small_names = ["ada_b", "ln_g", "ln_b", "a_conv_w", "a_conv_b", "a_dt_bias", "a_A_log", "a_D", "a_norm_g"]
    small_w = [ada_b, ln_g, ln_b, a_conv_w, a_conv_b, a_dt_bias, a_A_log, a_D, a_norm_g]
    small_m = [m_ada_b, m_ln_g, m_ln_b, m_a_conv_w, m_a_conv_b, m_a_dt_bias, m_a_A_log, m_a_D, m_a_norm_g]
    small_v = [v_ada_b, v_ln_g, v_ln_b, v_a_conv_w, v_a_conv_b, v_a_dt_bias, v_a_A_log, v_a_D, v_a_norm_g]
    small_g = [g_ada_b, g_ln_g, g_ln_b, g_conv_w_s, g_conv_b_s, g_dt_bias, g_a_log, g_dsk, g_norm_g_s]
    shapes = [w.shape for w in small_w]
    small_g = [g.reshape(s) for g, s in zip(small_g, shapes)]
    d_p, m_p, v_p = adamw(_pack(small_w), _pack(small_g), _pack(small_m), _pack(small_v), "adamw_small")
    small = {}
    for nm, g, d_, m_, v_ in zip(small_names, small_g, _unpack(d_p.reshape(-1), shapes), _unpack(m_p.reshape(-1), shapes),
                                 _unpack(v_p.reshape(-1), shapes)):
        small[nm] = (g, d_, m_, v_)
    allw = {**big, **small}
    order = ["ada_w", "ada_b", "ln_g", "ln_b", "a_in_w", "a_conv_w", "a_conv_b", "a_dt_bias", "a_A_log", "a_D",
             "a_norm_g", "a_out_w", "kv_w", "b_in_w", "b_out_w"]
    outs = [loss, grad_x.reshape(x.shape)]
    for k in range(4):
        outs += [allw[n][k] for n in order]
    return tuple(outs)
```

```python
import functools

import jax
import jax.numpy as jnp
import numpy as np
from jax import lax
from jax.experimental import pallas as pl
from jax.experimental.pallas import tpu as pltpu

F32 = jnp.float32
BF16 = jnp.bfloat16
MESH = pl.DeviceIdType.MESH

DEPTH = 2
ALPHA = (2 * DEPTH) ** 0.25
LN_EPS = 1e-5
RMS_EPS = 1e-5
SSD_P = 64
SSD_N = 128
SSD_Q = 256
SSD_G = 8
CONV_W = 4
DIL_PATTERNS = ((128, 1), (512, 4), (2048, 16))
DIL_H = 8
DIL_E = 128
DIL_BLK = 128
ADAM_LR, ADAM_B1, ADAM_B2, ADAM_EPS, ADAM_WD, ADAM_STEP = 0.001, 0.9, 0.999, 1e-08, 0.01, 10

VMEM_LIMIT = 56 * 1024 * 1024
N_CHIPS = 4
N_DEV = 8


def _tile(dim, target, mult=128):
    if dim <= target:
        return dim
    t = (target // mult) * mult
    while t >= mult:
        if dim % t == 0:
            return t
        t -= mult
    return dim


def _cp(sem):
    return pltpu.CompilerParams(dimension_semantics=sem, vmem_limit_bytes=VMEM_LIMIT)


def _sigmoid(x):
    return 1.0 / (1.0 + jnp.exp(-x))


def _silu(x):
    return x * _sigmoid(x)


def _dsilu(x):
    s = _sigmoid(x)
    return s * (1.0 + x * (1.0 - s))


def _softplus(x):
    return jnp.maximum(x, 0.0) + jnp.log(1.0 + jnp.exp(-jnp.abs(x)))


def _mm_call(a, b, out_shape, grid, a_spec, b_spec, o_spec, acc_shape, dims, name, after=None):
    nk = grid[2]
    extra = [] if after is None else [after]

    def prod(a_ref, b_ref):
        return lax.dot_general(a_ref[...].astype(BF16), b_ref[...].astype(BF16), (dims, ((), ())),
                               preferred_element_type=F32)

    def body_single(a_ref, b_ref, *rest):
        o_ref = rest[len(extra)]
        o_ref[...] = prod(a_ref, b_ref).astype(o_ref.dtype)

    def body_multi(a_ref, b_ref, *rest):
        o_ref, acc_ref = rest[len(extra):]
        k = pl.program_id(2)

        @pl.when(k == 0)
        def _():
            acc_ref[...] = prod(a_ref, b_ref)

        @pl.when(jnp.logical_and(k > 0, k < nk - 1))
        def _():
            acc_ref[...] += prod(a_ref, b_ref)

        @pl.when(k == nk - 1)
        def _():
            o_ref[...] = (acc_ref[...] + prod(a_ref, b_ref)).astype(o_ref.dtype)

    return pl.pallas_call(
        body_single if nk == 1 else body_multi, grid=grid, in_specs=[a_spec, b_spec] + [_ANY] * len(extra),
        out_specs=o_spec, out_shape=out_shape, scratch_shapes=[] if nk == 1 else [pltpu.VMEM(acc_shape, F32)],
        compiler_params=_cp(("parallel", "parallel", "arbitrary")), name=name)(a, b, *extra)


def mm_nn(a, b, out_dtype, name, stack=None, tm=1024, tn=1024, tk=2048, n_cols=None):
    M, K = a.shape
    if stack is None:
        N = b.shape[1] if n_cols is None else n_cols
        tn, tk = _tile(N, tn), _tile(K, tk)
        b_spec = pl.BlockSpec((tk, tn), lambda i, j, k: (k, j))
    elif stack == "col":
        S, _, Ns = b.shape
        N = S * Ns
        tn, tk = _tile(Ns, tn), _tile(K, tk)
        npb = Ns // tn
        b_spec = pl.BlockSpec((None, tk, tn), lambda i, j, k: (j // npb, k, j % npb))
    else:
        S, Ks, N = b.shape
        tn, tk = _tile(N, tn), _tile(Ks, tk)
        kpb = Ks // tk
        b_spec = pl.BlockSpec((None, tk, tn), lambda i, j, k: (k // kpb, k % kpb, j))
    tm = _tile(M, tm)
    return _mm_call(a, b, jax.ShapeDtypeStruct((M, N), out_dtype), (M // tm, N // tn, K // tk),
                    pl.BlockSpec((tm, tk), lambda i, j, k: (i, k)), b_spec,
                    pl.BlockSpec((tm, tn), lambda i, j, k: (i, j)), (tm, tn), ((1,), (0,)), name)


def mm_cols_dilated(a, b, gcols, d, name, tm=1024, tn=512):
    L, K = a.shape
    S, _, Ns = b.shape
    tm, tn = _tile(L, tm), _tile(Ns, tn)
    npb = Ns // tn
    nj = len(gcols)
    rows = tm // d

    def body(cols_ref, a_ref, b_ref, o_ref, *scr):
        prod = jnp.dot(a_ref[...], b_ref[...], preferred_element_type=F32)
        if d == 1:
            o_ref[0] = prod.astype(BF16)
        else:
            for c in range(tn // 128):
                scr[0][c] = prod[:, c * 128:(c + 1) * 128]
            for r in range(d):
                for c in range(tn // 128):
                    o_ref[r, :, c * 128:(c + 1) * 128] = scr[0].at[c][pl.ds(r, rows, stride=d), :].astype(BF16)

    return pl.pallas_call(
        body,
        grid_spec=pltpu.PrefetchScalarGridSpec(
            num_scalar_prefetch=1, grid=(L // tm, nj),
            in_specs=[pl.BlockSpec((tm, K), lambda i, j, c: (i, 0)),
                      pl.BlockSpec((None, K, tn), lambda i, j, c: (c[j] // npb, 0, c[j] % npb))],
            out_specs=pl.BlockSpec((d, rows, tn), lambda i, j, c: (0, i, j)),
            scratch_shapes=[] if d == 1 else [pltpu.VMEM((tn // 128, tm, 128), F32)]),
        out_shape=jax.ShapeDtypeStruct((d, L // d, nj * tn), BF16),
        compiler_params=_cp(("parallel", "arbitrary")), name=name)(jnp.asarray(gcols, jnp.int32), a, b)


def mm_nt(a, b, out_dtype, name, stack=None, tm=1024, tn=1024, tk=2048, after=None):
    M, C = a.shape
    if stack is None:
        Kw = b.shape[0]
        tn, tk = _tile(Kw, tn), _tile(C, tk)
        b_spec = pl.BlockSpec((tn, tk), lambda i, j, k: (j, k))
    elif stack == "col":
        S, Kw, Cs = b.shape
        tn, tk = _tile(Kw, tn), _tile(Cs, tk)
        cpb = Cs // tk
        b_spec = pl.BlockSpec((None, tn, tk), lambda i, j, k: (k // cpb, j, k % cpb))
    else:
        S, Ks, _ = b.shape
        Kw = S * Ks
        tn, tk = _tile(Ks, tn), _tile(C, tk)
        jpb = Ks // tn
        b_spec = pl.BlockSpec((None, tn, tk), lambda i, j, k: (j // jpb, j % jpb, k))
    tm = _tile(M, tm)
    return _mm_call(a, b, jax.ShapeDtypeStruct((M, Kw), out_dtype), (M // tm, Kw // tn, C // tk),
                    pl.BlockSpec((tm, tk), lambda i, j, k: (i, k)), b_spec,
                    pl.BlockSpec((tm, tn), lambda i, j, k: (i, j)), (tm, tn), ((1,), (1,)), name, after=after)


def mm_tn(a, b, out_dtype, name, stack=None, n_stack=N_CHIPS, tm=1024, tn=1024, tk=2048):
    L, M = a.shape
    N = b.shape[1]
    tk = _tile(L, tk)
    if stack is None:
        tm, tn = _tile(M, tm), _tile(N, tn)
        o_spec = pl.BlockSpec((tm, tn), lambda i, j, k: (i, j))
        out_shape = (M, N)
    elif stack == "col":
        Ns = N // n_stack
        tm, tn = _tile(M, tm), _tile(Ns, tn)
        npb = Ns // tn
        o_spec = pl.BlockSpec((None, tm, tn), lambda i, j, k: (j // npb, i, j % npb))
        out_shape = (n_stack, M, Ns)
    else:
        Ms = M // n_stack
        tm, tn = _tile(Ms, tm), _tile(N, tn)
        mpb = Ms // tm
        o_spec = pl.BlockSpec((None, tm, tn), lambda i, j, k: (i // mpb, i % mpb, j))
        out_shape = (n_stack, Ms, N)
    return _mm_call(a, b, jax.ShapeDtypeStruct(out_shape, out_dtype), (M // tm, N // tn, L // tk),
                    pl.BlockSpec((tk, tm), lambda i, j, k: (k, i)), pl.BlockSpec((tk, tn), lambda i, j, k: (k, j)),
                    o_spec, (tm, tn), ((0,), (0,)), name)


def _row_specs(tr, widths):
    return [pl.BlockSpec((tr, w), lambda i: (i, 0)) for w in widths]


def _vec_spec(w):
    return pl.BlockSpec((1, w), lambda i: (0, 0))


def _acc_rows(ref, val, i):
    s = jnp.sum(val, axis=0, keepdims=True)

    @pl.when(i == 0)
    def _():
        ref[...] = s

    @pl.when(i > 0)
    def _():
        ref[...] += s


def modulate(x, scale, shift, name):
    L, D = x.shape
    tr = _tile(L, 512, 16)

    def body(x_ref, sc_ref, sh_ref, h_ref):
        h_ref[...] = (x_ref[...] * (1.0 + sc_ref[...]) + sh_ref[...]).astype(BF16)

    return pl.pallas_call(
        body, grid=(L // tr,), in_specs=_row_specs(tr, [D]) + [_vec_spec(D)] * 2, out_specs=_row_specs(tr, [D])[0],
        out_shape=jax.ShapeDtypeStruct((L, D), BF16), compiler_params=_cp(("parallel",)), name=name)(x, scale, shift)


def _ln_core(x, y, gate, g, b):
    u = ALPHA * x + (1.0 + gate) * y
    mu = jnp.mean(u, axis=-1, keepdims=True)
    d = u - mu
    var = jnp.mean(d * d, axis=-1, keepdims=True)
    rstd = lax.rsqrt(var + LN_EPS)
    xhat = d * rstd
    return xhat * g + b, xhat, rstd


def ln_mid(x, y, gate, g, b, scale, shift):
    L, D = x.shape
    tr = _tile(L, 256, 16)

    def body(x_ref, y_ref, gate_ref, g_ref, b_ref, sc_ref, sh_ref, x1_ref, x1b_ref, h_ref):
        x1, _, _ = _ln_core(x_ref[...], y_ref[...], gate_ref[...], g_ref[...], b_ref[...])
        x1_ref[...] = x1
        x1b_ref[...] = x1.astype(BF16)
        h_ref[...] = (x1 * (1.0 + sc_ref[...]) + sh_ref[...]).astype(BF16)

    return pl.pallas_call(
        body, grid=(L // tr,), in_specs=_row_specs(tr, [D, D]) + [_vec_spec(D)] * 5,
        out_specs=_row_specs(tr, [D, D, D]),
        out_shape=[jax.ShapeDtypeStruct((L, D), F32), jax.ShapeDtypeStruct((L, D), BF16),
                   jax.ShapeDtypeStruct((L, D), BF16)],
        compiler_params=_cp(("parallel",)), name="ln_mid")(x, y, gate, g, b, scale, shift)


def _ln_bwd_rows(dout_v, xhat, rstd, g):
    dxh = dout_v * g
    m1 = jnp.mean(dxh, axis=-1, keepdims=True)
    m2 = jnp.mean(dxh * xhat, axis=-1, keepdims=True)
    return rstd * (dxh - m1 - xhat * m2)


def ln_final_fwd_bwd(x, y, gate, g, b, target):
    L, D = x.shape
    tr = _tile(L, 256, 16)

    def body(x_ref, y_ref, gate_ref, g_ref, b_ref, t_ref, dres_ref, dy_ref, dg_ref, db_ref, dgate_ref, sq_ref):
        i = pl.program_id(0)
        yv = y_ref[...]
        out, xhat, rstd = _ln_core(x_ref[...], yv, gate_ref[...], g_ref[...], b_ref[...])
        err = out - t_ref[...]
        dout_v = err * (1.0 / D)
        du = _ln_bwd_rows(dout_v, xhat, rstd, g_ref[...])
        dres_ref[...] = ALPHA * du
        dy_ref[...] = ((1.0 + gate_ref[...]) * du).astype(BF16)
        _acc_rows(dg_ref, dout_v * xhat, i)
        _acc_rows(db_ref, dout_v, i)
        _acc_rows(dgate_ref, du * yv, i)
        _acc_rows(sq_ref, err * err, i)

    return pl.pallas_call(
        body, grid=(L // tr,), in_specs=_row_specs(tr, [D, D]) + [_vec_spec(D)] * 3 + _row_specs(tr, [D]),
        out_specs=_row_specs(tr, [D, D]) + [_vec_spec(D)] * 4,
        out_shape=[jax.ShapeDtypeStruct((L, D), F32), jax.ShapeDtypeStruct((L, D), BF16)]
        + [jax.ShapeDtypeStruct((1, D), F32)] * 4,
        compiler_params=_cp(("arbitrary",)), name="ln_final_fwd_bwd")(x, y, gate, g, b, target)


def mod_ln_bwd(dres_in, dh, dskip, xmid, scale, x, y, gate, g):
    L, D = x.shape
    tr = _tile(L, 256, 16)

    def body(dres_ref, dh_ref, dskip_ref, xm_ref, sc_ref, x_ref, y_ref, gate_ref, g_ref,
             dres_out, dy_ref, dg_ref, db_ref, dgate_ref, dsc_ref, dsh_ref):
        i = pl.program_id(0)
        dh_v = dh_ref[...]
        dout_v = dres_ref[...] + dskip_ref[...] + dh_v * (1.0 + sc_ref[...])
        _acc_rows(dsc_ref, dh_v * xm_ref[...], i)
        _acc_rows(dsh_ref, dh_v, i)
        yv = y_ref[...]
        _, xhat, rstd = _ln_core(x_ref[...], yv, gate_ref[...], g_ref[...], 0.0)
        du = _ln_bwd_rows(dout_v, xhat, rstd, g_ref[...])
        dres_out[...] = ALPHA * du
        dy_ref[...] = ((1.0 + gate_ref[...]) * du).astype(BF16)
        _acc_rows(dg_ref, dout_v * xhat, i)
        _acc_rows(db_ref, dout_v, i)
        _acc_rows(dgate_ref, du * yv, i)

    return pl.pallas_call(
        body, grid=(L // tr,),
        in_specs=_row_specs(tr, [D] * 4) + [_vec_spec(D)] + _row_specs(tr, [D, D]) + [_vec_spec(D)] * 2,
        out_specs=_row_specs(tr, [D, D]) + [_vec_spec(D)] * 5,
        out_shape=[jax.ShapeDtypeStruct((L, D), F32), jax.ShapeDtypeStruct((L, D), BF16)]
        + [jax.ShapeDtypeStruct((1, D), F32)] * 5,
        compiler_params=_cp(("arbitrary",)), name="mod_ln_bwd")(dres_in, dh, dskip, xmid, scale, x, y, gate, g)


def ln_bwd(dout, x, y, gate, g, name):
    L, D = x.shape
    tr = _tile(L, 256, 16)

    def body(do_ref, x_ref, y_ref, gate_ref, g_ref, dres_ref, dy_ref, dg_ref, db_ref, dgate_ref):
        i = pl.program_id(0)
        yv = y_ref[...]
        dout_v = do_ref[...]
        _, xhat, rstd = _ln_core(x_ref[...], yv, gate_ref[...], g_ref[...], 0.0)
        dxh = dout_v * g_ref[...]
        m1 = jnp.mean(dxh, axis=-1, keepdims=True)
        m2 = jnp.mean(dxh * xhat, axis=-1, keepdims=True)
        du = rstd * (dxh - m1 - xhat * m2)
        dres_ref[...] = ALPHA * du
        dy_ref[...] = ((1.0 + gate_ref[...]) * du).astype(BF16)
        _acc_rows(dg_ref, dout_v * xhat, i)
        _acc_rows(db_ref, dout_v, i)
        _acc_rows(dgate_ref, du * yv, i)

    return pl.pallas_call(
        body, grid=(L // tr,), in_specs=_row_specs(tr, [D, D, D]) + [_vec_spec(D)] * 2,
        out_specs=_row_specs(tr, [D, D]) + [_vec_spec(D)] * 3,
        out_shape=[jax.ShapeDtypeStruct((L, D), F32), jax.ShapeDtypeStruct((L, D), BF16)]
        + [jax.ShapeDtypeStruct((1, D), F32)] * 3,
        compiler_params=_cp(("arbitrary",)), name=name)(dout, x, y, gate, g)


def mod_bwd(dres, dh, dh2, xin, scale, name, through_mod):
    L, D = xin.shape
    tr = _tile(L, 256, 16)

    def body(dres_ref, dh_ref, dh2_ref, x_ref, sc_ref, dx_ref, dsc_ref, dsh_ref):
        i = pl.program_id(0)
        dh_v = dh_ref[...]
        tot = dres_ref[...]
        if through_mod:
            dh_v = dh_v + dh2_ref[...]
        else:
            tot = tot + dh2_ref[...]
        dx_ref[...] = tot + dh_v * (1.0 + sc_ref[...])
        _acc_rows(dsc_ref, dh_v * x_ref[...], i)
        _acc_rows(dsh_ref, dh_v, i)

    return pl.pallas_call(
        body, grid=(L // tr,), in_specs=_row_specs(tr, [D, D, D, D]) + [_vec_spec(D)],
        out_specs=_row_specs(tr, [D]) + [_vec_spec(D)] * 2,
        out_shape=[jax.ShapeDtypeStruct((L, D), F32)] + [jax.ShapeDtypeStruct((1, D), F32)] * 2,
        compiler_params=_cp(("arbitrary",)), name=name)(dres, dh, dh2, xin, scale)


CONV_HALO = 16


def _conv_rows(x_ref, i, tr, L):
    nblk = L // tr
    s = pl.multiple_of(i * tr, CONV_HALO)
    cur = x_ref[pl.ds(s, tr), :].astype(F32)
    sp = pl.multiple_of(jnp.maximum(i * tr - CONV_HALO, 0), CONV_HALO)
    sn = pl.multiple_of(jnp.minimum(i * tr + tr, L - CONV_HALO), CONV_HALO)
    prev = x_ref[pl.ds(sp, CONV_HALO), :].astype(F32) * (i > 0).astype(F32)
    nxt = x_ref[pl.ds(sn, CONV_HALO), :].astype(F32) * (i < nblk - 1).astype(F32)
    return jnp.concatenate([prev, cur, nxt], axis=0)


def _shift_rows(v, j):
    n = v.shape[0]
    return v if j % n == 0 else pltpu.roll(v, j % n, 0)


def _conv_taps(xe):
    return [_shift_rows(xe, CONV_W - 1 - k) for k in range(CONV_W)]


def _conv_eval(taps, w_ref, b_ref):
    c = b_ref[...] + w_ref[0:1, :] * taps[0]
    for k in range(1, CONV_W):
        c = c + w_ref[k:k + 1, :] * taps[k]
    return c


def conv_fwd(zx, col0, conv_w, conv_b):
    L = zx.shape[0]
    C = conv_w.shape[1]
    tc = _tile(C, 512)
    tr = _tile(L, 512, CONV_HALO)
    off = col0 // tc

    def body(x_ref, w_ref, b_ref, o_ref):
        i = pl.program_id(1)
        xe = _conv_rows(x_ref, i, tr, L)
        c = _conv_eval(_conv_taps(xe), w_ref, b_ref)[CONV_HALO:CONV_HALO + tr]
        o_ref[...] = _silu(c).astype(BF16)

    return pl.pallas_call(
        body, grid=(C // tc, L // tr),
        in_specs=[pl.BlockSpec((L, tc), lambda j, i: (0, off + j)), pl.BlockSpec((CONV_W, tc), lambda j, i: (0, j)),
                  pl.BlockSpec((1, tc), lambda j, i: (0, j))],
        out_specs=pl.BlockSpec((tr, tc), lambda j, i: (i, j)),
        out_shape=jax.ShapeDtypeStruct((L, C), BF16), compiler_params=_cp(("parallel", "arbitrary")),
        name="conv_fwd")(zx, conv_w, conv_b)


def conv_bwd(zx, col0, conv_w, conv_b, dxbc):
    L = zx.shape[0]
    C = conv_w.shape[1]
    tc = _tile(C, 512)
    tr = _tile(L, 512, CONV_HALO)
    off = col0 // tc
    H = CONV_HALO

    def body(x_ref, g_ref, w_ref, b_ref, dx_ref, dw_ref, db_ref):
        i = pl.program_id(1)
        xe = _conv_rows(x_ref, i, tr, L)
        ge = _conv_rows(g_ref, i, tr, L)
        taps = _conv_taps(xe)
        dc = ge * _dsilu(_conv_eval(taps, w_ref, b_ref))
        dx = w_ref[CONV_W - 1:CONV_W, :] * dc
        for k in range(CONV_W - 1):
            dx = dx + w_ref[k:k + 1, :] * _shift_rows(dc, -(CONV_W - 1 - k))
        dx_ref[...] = dx[H:H + tr].astype(BF16)
        dcc = dc[H:H + tr]
        rows = [jnp.sum(dcc * taps[k][H:H + tr], axis=0, keepdims=True) for k in range(CONV_W)]
        dwv = jnp.concatenate(rows + [jnp.zeros((8 - CONV_W, tc), F32)], axis=0)
        dbv = jnp.sum(dcc, axis=0, keepdims=True)

        @pl.when(i == 0)
        def _():
            dw_ref[...] = dwv
            db_ref[...] = dbv

        @pl.when(i > 0)
        def _():
            dw_ref[...] += dwv
            db_ref[...] += dbv

    dx, dw, db = pl.pallas_call(
        body, grid=(C // tc, L // tr),
        in_specs=[pl.BlockSpec((L, tc), lambda j, i: (0, off + j)), pl.BlockSpec((L, tc), lambda j, i: (0, j)),
                  pl.BlockSpec((CONV_W, tc), lambda j, i: (0, j)), pl.BlockSpec((1, tc), lambda j, i: (0, j))],
        out_specs=[pl.BlockSpec((tr, tc), lambda j, i: (i, j)), pl.BlockSpec((8, tc), lambda j, i: (0, j)),
                   pl.BlockSpec((1, tc), lambda j, i: (0, j))],
        out_shape=[jax.ShapeDtypeStruct((L, C), BF16), jax.ShapeDtypeStruct((8, C), F32),
                   jax.ShapeDtypeStruct((1, C), F32)],
        compiler_params=_cp(("parallel", "arbitrary")), name="conv_bwd")(zx, dxbc, conv_w, conv_b)
    return dx, dw[:CONV_W], db


_NN = (((1,), (0,)), ((), ()))


def _pieces(x, n):
    out, r = [], x
    for _ in range(n):
        p = r.astype(BF16)
        out.append(p)
        r = r - p.astype(F32)
    return out


def _dot01(a, b01, n, dims=_NN):
    b = b01.astype(BF16)
    return functools.reduce(lambda u, v: u + v,
                            [lax.dot_general(p, b, dims, preferred_element_type=F32) for p in _pieces(a, n)])


def _dot01_left(a01, b, n, dims=_NN):
    a = a01.astype(BF16)
    return functools.reduce(lambda u, v: u + v,
                            [lax.dot_general(a, p, dims, preferred_element_type=F32) for p in _pieces(b, n)])


def _ssd_common(dtp_ref, dtpT_ref, bias_ref, biasT_ref, alog_ref, alogT_ref, b_ref, c_ref):
    Q = SSD_Q
    dt = _softplus(dtp_ref[...] + bias_ref[...])
    A = -jnp.exp(alog_ref[...])
    row = lax.broadcasted_iota(jnp.int32, (Q, Q), 0)
    col = lax.broadcasted_iota(jnp.int32, (Q, Q), 1)
    causal = row >= col
    tril = causal.astype(F32)
    Kh = dt.shape[1]
    acum = _dot01_left(tril, dt * A, 3)
    eye = (lax.broadcasted_iota(jnp.int32, (Kh, Kh), 0) == lax.broadcasted_iota(jnp.int32, (Kh, Kh), 1)).astype(F32)
    acumT = _dot01_left(eye, acum, 3, dims=(((1,), (1,)), ((), ())))
    Bm = b_ref[...]
    Cm = c_ref[...]
    cb = lax.dot_general(Cm, Bm, (((1,), (1,)), ((), ())), preferred_element_type=F32)
    return dt, A, causal, row, col, acum, acumT, Bm, Cm, cb


def _ssd_in_specs(Q, GP, N, Kh, DI, cmap):
    nb0 = DI // N
    vec = pl.BlockSpec((None, 1, Kh), lambda g, c: (g, 0, 0))
    vecT = pl.BlockSpec((None, Kh, 1), lambda g, c: (g, 0, 0))
    return [pl.BlockSpec((Q, GP), lambda g, c: (cmap(c), g)),
            pl.BlockSpec((Q, N), lambda g, c: (cmap(c), nb0 + g)),
            pl.BlockSpec((Q, N), lambda g, c: (cmap(c), nb0 + SSD_G + g)),
            pl.BlockSpec((None, Q, Kh), lambda g, c: (g, cmap(c), 0)),
            pl.BlockSpec((None, Kh, Q), lambda g, c: (g, 0, cmap(c))),
            vec, vecT, vec, vecT, vec, vecT]


def _hi(a, b01):
    return _dot01(a, b01, 2)


def _headsum(a, b01):
    return _dot01(a, b01, 1)


def _ssd_heads(dskT_ref, acum, acumT, dt, Kh):
    Q, P, N = SSD_Q, SSD_P, SSD_N
    GP = Kh * P
    sh_p = P.bit_length() - 1
    seg = lambda shape, dim: lax.shift_right_logical(lax.broadcasted_iota(jnp.int32, shape, dim), sh_p)
    E = (seg((Kh, GP), 1) == lax.broadcasted_iota(jnp.int32, (Kh, GP), 0)).astype(F32)
    ET = (seg((GP, Kh), 0) == lax.broadcasted_iota(jnp.int32, (GP, Kh), 1)).astype(F32)
    a_last = acum[Q - 1:Q, :]
    tail = jnp.exp(a_last - acum)
    eLT = jnp.exp(acumT[:, Q - 1:Q])
    rowseg = seg((GP, N), 0)
    eL_b = jnp.zeros((GP, N), F32)
    for k in range(Kh):
        eL_b = jnp.where(rowseg == k, eLT[k:k + 1, :], eL_b)
    return dict(
        E=E, ET=ET, a_last=a_last, tail=tail, eL_b=eL_b,
        dt_all=_hi(dt, E), ea_all=_hi(jnp.exp(acum), E), tail_all=_hi(tail, E),
        dsk_all=jnp.sum(E * dskT_ref[...], axis=0, keepdims=True))


def _head_chunks(GP):
    CW = min(GP, 128)
    return CW, CW // SSD_P, GP // CW


def _head_mask(Q, CW, kk):
    lane = lax.broadcasted_iota(jnp.int32, (Q, CW), 1)
    return jnp.logical_and(lane >= kk * SSD_P, lane < (kk + 1) * SSD_P)


def ssd_fwd(xbc, dtp_g, dtp_gT, bias_g, bias_gT, alog_g, alog_gT, dsk_g, dsk_gT, DI):
    L = xbc.shape[0]
    Q, P, N, G = SSD_Q, SSD_P, SSD_N, SSD_G
    GP = DI // G
    Kh = GP // P
    nc = L // Q

    CW, hpc, nch = _head_chunks(GP)
    nt = (((1,), (1,)), ((), ()))
    tn = (((0,), (0,)), ((), ()))

    def body(xs_ref, b_ref, c_ref, dtp_ref, dtpT_ref, bias_ref, biasT_ref, alog_ref, alogT_ref, dsk_ref, dskT_ref,
             y_ref, st_ref, state):
        @pl.when(pl.program_id(1) == 0)
        def _():
            state[...] = jnp.zeros(state.shape, F32)

        st_ref[...] = state[...]
        dt, A, causal, row, col, acum, acumT, Bm, Cm, cb = _ssd_common(
            dtp_ref, dtpT_ref, bias_ref, biasT_ref, alog_ref, alogT_ref, b_ref, c_ref)
        hd = _ssd_heads(dskT_ref, acum, acumT, dt, Kh)
        xs = xs_ref[...].astype(F32)
        xdt_all = xs * hd["dt_all"]
        S_all = state[...]
        y_all = (lax.dot_general(Cm, S_all.astype(BF16), nt, preferred_element_type=F32) * hd["ea_all"]
                 + xs * hd["dsk_all"])
        state[...] = S_all * hd["eL_b"] + lax.dot_general(
            (xdt_all * hd["tail_all"]).astype(BF16), Bm, tn, preferred_element_type=F32)
        for ch in range(nch):
            cs = slice(ch * CW, (ch + 1) * CW)
            xc = xdt_all[:, cs]
            acc = y_all[:, cs]
            for kk in range(hpc):
                k = ch * hpc + kk
                decay = jnp.exp(jnp.where(causal, acum[:, k:k + 1] - acumT[k:k + 1, :], -jnp.inf))
                xk = xc if hpc == 1 else jnp.where(_head_mask(Q, CW, kk), xc, 0.0)
                acc = acc + jnp.dot((cb * decay).astype(BF16), xk.astype(BF16), preferred_element_type=F32)
            y_ref[:, cs] = acc.astype(BF16)

    return pl.pallas_call(
        body, grid=(G, nc), in_specs=_ssd_in_specs(Q, GP, N, Kh, DI, lambda c: c),
        out_specs=[pl.BlockSpec((Q, GP), lambda g, c: (c, g)),
                   pl.BlockSpec((None, None, GP, N), lambda g, c: (c, g, 0, 0))],
        out_shape=[jax.ShapeDtypeStruct((L, DI), BF16), jax.ShapeDtypeStruct((nc, G, GP, N), F32)],
        scratch_shapes=[pltpu.VMEM((GP, N), F32)], compiler_params=_cp(("parallel", "arbitrary")),
        name="ssd_fwd")(xbc, xbc, xbc, dtp_g, dtp_gT, bias_g, bias_gT, alog_g, alog_gT, dsk_g, dsk_gT)


def ssd_bwd(xbc, dtp_g, dtp_gT, bias_g, bias_gT, alog_g, alog_gT, dsk_g, dsk_gT, states, dy, DI):
    L = xbc.shape[0]
    Q, P, N, G = SSD_Q, SSD_P, SSD_N, SSD_G
    GP = DI // G
    Kh = GP // P
    nc = L // Q
    rev = lambda c: nc - 1 - c

    CW, hpc, nch = _head_chunks(GP)

    def body(xs_ref, b_ref, c_ref, dtp_ref, dtpT_ref, bias_ref, biasT_ref, alog_ref, alogT_ref, dsk_ref, dskT_ref,
             st_ref, dy_ref, dxs_ref, dB_ref, dC_ref, ddtp_ref, dbias_ref, dalog_ref, dD_ref, dstate):
        ci = pl.program_id(1)

        @pl.when(ci == 0)
        def _():
            dstate[...] = jnp.zeros(dstate.shape, F32)

        dt, A, causal, row, col, acum, acumT, Bm, Cm, cb = _ssd_common(
            dtp_ref, dtpT_ref, bias_ref, biasT_ref, alog_ref, alogT_ref, b_ref, c_ref)
        tn = (((0,), (0,)), ((), ()))
        nt = (((1,), (1,)), ((), ()))
        hd = _ssd_heads(dskT_ref, acum, acumT, dt, Kh)
        ET, tail = hd["ET"], hd["tail"]
        cbT = lax.dot_general(Bm, Cm, nt, preferred_element_type=F32)
        causalT = row <= col
        xs = xs_ref[...].astype(F32)
        xdt_all = xs * hd["dt_all"]
        dyb = dy_ref[...]
        dy_all = dyb.astype(F32)
        S_all = st_ref[...]
        S_b = S_all.astype(BF16)
        dS_all = dstate[...]
        dS_b = dS_all.astype(BF16)
        CS_all = lax.dot_general(Cm, S_b, nt, preferred_element_type=F32)
        dyE_b = (dy_all * hd["ea_all"]).astype(BF16)
        dC_acc = jnp.dot(dyE_b, S_b, preferred_element_type=F32)
        dS_y = lax.dot_general(dyE_b, Cm, tn, preferred_element_type=F32)
        BdS_all = lax.dot_general(Bm, dS_b, nt, preferred_element_type=F32)
        dB_acc = jnp.dot((xdt_all * hd["tail_all"]).astype(BF16), dS_b, preferred_element_type=F32)
        dtail = _headsum(xdt_all * BdS_all, ET)
        da_cols = _headsum(dy_all * CS_all * hd["ea_all"], ET) - dtail * tail
        dss = _dot01_left(jnp.ones((8, N), F32), _dot01_left(hd["E"], dS_all * S_all, 2), 2, dims=nt)
        da_last = dss[0:1] * jnp.exp(hd["a_last"]) + jnp.sum(dtail * tail, axis=0, keepdims=True)
        rowi = lax.broadcasted_iota(jnp.int32, (Q, Kh), 0)
        da_cols = da_cols + jnp.where(rowi == Q - 1, da_last, 0.0)
        dstate[...] = hd["eL_b"] * dS_all + dS_y
        sum_mg = jnp.zeros((Q, Q), F32)
        ddt_x = jnp.zeros((Q, Kh), F32)
        da_rows = jnp.zeros((Kh, Q), F32)
        lane_k = lax.broadcasted_iota(jnp.int32, (Q, Kh), 1)
        sub_k = lax.broadcasted_iota(jnp.int32, (Kh, Q), 0)
        for ch in range(nch):
            cs = slice(ch * CW, (ch + 1) * CW)
            dyc = dyb[:, cs]
            xc_b = xdt_all[:, cs].astype(BF16)
            acc = hd["tail_all"][:, cs] * BdS_all[:, cs]
            for kk in range(hpc):
                k = ch * hpc + kk
                a_b = jnp.broadcast_to(acum[:, k:k + 1], (Q, Q))
                a_r = acumT[k:k + 1, :]
                decay = jnp.exp(jnp.where(causal, a_b - a_r, -jnp.inf))
                decayT = jnp.exp(jnp.where(causalT, a_r - a_b, -jnp.inf))
                dyk = dyc if hpc == 1 else jnp.where(_head_mask(Q, CW, kk), dyc, jnp.zeros_like(dyc))
                mg = decay * lax.dot_general(dyk, xc_b, nt, preferred_element_type=F32)
                sum_mg = sum_mg + mg
                w = mg * cb
                da_cols = da_cols + jnp.where(lane_k == k, jnp.sum(w, axis=1, keepdims=True), 0.0)
                da_rows = da_rows + jnp.where(sub_k == k, jnp.sum(w, axis=0, keepdims=True), 0.0)
                acc = acc + jnp.dot((decayT * cbT).astype(BF16), dyk, preferred_element_type=F32)
            dxs_ref[:, cs] = (acc * hd["dt_all"][:, cs] + dy_all[:, cs] * hd["dsk_all"][:, cs]).astype(BF16)
            ddt_x = ddt_x + _headsum(acc * xs[:, cs], ET[cs, :])
        eye_q = (row == col).astype(F32)
        da_cols = da_cols - _dot01_left(eye_q, da_rows, 3, dims=nt)
        dD_row = jnp.sum(_headsum(dy_all * xs, ET), axis=0, keepdims=True)
        sum_mg_b = sum_mg.astype(BF16)
        dB_ref[...] = (dB_acc + lax.dot_general(sum_mg_b, Cm, tn, preferred_element_type=F32)).astype(BF16)
        dC_ref[...] = (dC_acc + jnp.dot(sum_mg_b, Bm, preferred_element_type=F32)).astype(BF16)
        triu = (row <= col).astype(F32)
        ddtA = _dot01_left(triu, da_cols, 3)
        ddt = ddt_x + ddtA * A
        dpre = ddt * _sigmoid(dtp_ref[...] + bias_ref[...])
        ddtp_ref[...] = dpre
        dbias_v = jnp.sum(dpre, axis=0, keepdims=True)
        dalog_v = jnp.sum(ddtA * dt, axis=0, keepdims=True) * A

        @pl.when(ci == 0)
        def _():
            dbias_ref[...] = dbias_v
            dalog_ref[...] = dalog_v
            dD_ref[...] = dD_row

        @pl.when(ci > 0)
        def _():
            dbias_ref[...] += dbias_v
            dalog_ref[...] += dalog_v
            dD_ref[...] += dD_row

    vec_o = pl.BlockSpec((None, 1, Kh), lambda g, c: (g, 0, 0))
    return pl.pallas_call(
        body, grid=(G, nc),
        in_specs=_ssd_in_specs(Q, GP, N, Kh, DI, rev)
        + [pl.BlockSpec((None, None, GP, N), lambda g, c: (rev(c), g, 0, 0)),
           pl.BlockSpec((Q, GP), lambda g, c: (rev(c), g))],
        out_specs=[pl.BlockSpec((Q, GP), lambda g, c: (rev(c), g)), pl.BlockSpec((Q, N), lambda g, c: (rev(c), g)),
                   pl.BlockSpec((Q, N), lambda g, c: (rev(c), g)),
                   pl.BlockSpec((None, Q, Kh), lambda g, c: (g, rev(c), 0)), vec_o, vec_o, vec_o],
        out_shape=[jax.ShapeDtypeStruct((L, DI), BF16), jax.ShapeDtypeStruct((L, G * N), BF16),
                   jax.ShapeDtypeStruct((L, G * N), BF16), jax.ShapeDtypeStruct((G, L, Kh), F32)]
        + [jax.ShapeDtypeStruct((G, 1, Kh), F32)] * 3,
        scratch_shapes=[pltpu.VMEM((GP, N), F32)], compiler_params=_cp(("parallel", "arbitrary")),
        name="ssd_bwd")(xbc, xbc, xbc, dtp_g, dtp_gT, bias_g, bias_gT, alog_g, alog_gT, dsk_g, dsk_gT, states, dy)


def _rms_groups(y2, ng_ref, DI):
    S = DI // SSD_G
    for g in range(SSD_G):
        gs = slice(g * S, (g + 1) * S)
        seg = y2[:, gs]
        r = lax.rsqrt(jnp.mean(seg * seg, axis=-1, keepdims=True) + RMS_EPS)
        yield gs, seg * r, r, ng_ref[:, gs]


def rms_gate_fwd(y, zx, norm_g):
    L, DI = y.shape
    tr = _tile(L, 256, 16)

    def body(y_ref, z_ref, ng_ref, o_ref):
        y2 = y_ref[...].astype(F32) * _silu(z_ref[...].astype(F32))
        for gs, yh, _, ng in _rms_groups(y2, ng_ref, DI):
            o_ref[:, gs] = (yh * ng).astype(BF16)

    return pl.pallas_call(
        body, grid=(L // tr,), in_specs=_row_specs(tr, [DI, DI]) + [_vec_spec(DI)], out_specs=_row_specs(tr, [DI])[0],
        out_shape=jax.ShapeDtypeStruct((L, DI), BF16), compiler_params=_cp(("parallel",)),
        name="rms_gate_fwd")(y, zx, norm_g)


def rms_gate_bwd(dyn, y, zx, norm_g):
    L, DI = y.shape
    tr = _tile(L, 256, 16)

    def body(dyn_ref, y_ref, z_ref, ng_ref, dy_ref, dz_ref, dng_ref):
        i = pl.program_id(0)
        yv = y_ref[...].astype(F32)
        zv = z_ref[...].astype(F32)
        sz = _silu(zv)
        dsz = _dsilu(zv)
        dynv = dyn_ref[...].astype(F32)
        for gs, yh, r, ng in _rms_groups(yv * sz, ng_ref, DI):
            dyh = dynv[:, gs] * ng
            dy2 = r * (dyh - yh * jnp.mean(dyh * yh, axis=-1, keepdims=True))
            dy_ref[:, gs] = (dy2 * sz[:, gs]).astype(BF16)
            dz_ref[:, gs] = (dy2 * yv[:, gs] * dsz[:, gs]).astype(BF16)
            s = jnp.sum(dynv[:, gs] * yh, axis=0, keepdims=True)

            @pl.when(i == 0)
            def _():
                dng_ref[:, gs] = s

            @pl.when(i > 0)
            def _():
                dng_ref[:, gs] += s

    return pl.pallas_call(
        body, grid=(L // tr,), in_specs=_row_specs(tr, [DI, DI, DI]) + [_vec_spec(DI)],
        out_specs=_row_specs(tr, [DI, DI]) + [_vec_spec(DI)],
        out_shape=[jax.ShapeDtypeStruct((L, DI), BF16)] * 2 + [jax.ShapeDtypeStruct((1, DI), F32)],
        compiler_params=_cp(("arbitrary",)), name="rms_gate_bwd")(dyn, y, zx, norm_g)


def _alibi_slope(gi, h):
    n = len(DIL_PATTERNS) * DIL_H
    return float(2.0 ** (-8.0 * (gi * DIL_H + h + 1) / n))


def _attn_masks():
    qi = lax.broadcasted_iota(jnp.int32, (DIL_BLK, DIL_BLK), 0)
    kj = lax.broadcasted_iota(jnp.int32, (DIL_BLK, DIL_BLK), 1)
    dcur = (qi - kj).astype(F32)
    return dcur, qi >= kj, dcur + float(DIL_BLK), kj >= qi


def attn_fwd(q3, kv3, gi):
    window, d = DIL_PATTERNS[gi]
    assert window // d == DIL_BLK
    HW = DIL_H * DIL_E
    M = q3.shape[1]
    nb = M // DIL_BLK
    scale = DIL_E ** -0.5
    nt = (((1,), (1,)), ((), ()))

    def body(q_ref, kp_ref, kc_ref, vp_ref, vc_ref, o_ref, lse_ref):
        n = pl.program_id(1)
        dcur, vcur, dprev, vprev0 = _attn_masks()
        dist = jnp.concatenate([dprev, dcur], axis=1)
        valid = jnp.concatenate([jnp.logical_and(vprev0, n > 0), vcur], axis=1)
        lane = lax.broadcasted_iota(jnp.int32, (DIL_BLK, 128), 1)
        lse_acc = jnp.zeros((DIL_BLK, 128), F32)
        for h in range(DIL_H):
            hs = slice(h * DIL_E, (h + 1) * DIL_E)
            sl = _alibi_slope(gi, h) * d
            kcat = jnp.concatenate([kp_ref[:, hs], kc_ref[:, hs]], axis=0)
            vcat = jnp.concatenate([vp_ref[:, hs], vc_ref[:, hs]], axis=0)
            s = lax.dot_general(q_ref[:, hs], kcat, nt, preferred_element_type=F32) * scale - sl * dist
            s = jnp.where(valid, s, -jnp.inf)
            m = jnp.max(s, axis=-1, keepdims=True)
            p = jnp.exp(s - m)
            den = jnp.sum(p, axis=-1, keepdims=True)
            o = jnp.dot(p.astype(BF16), vcat, preferred_element_type=F32) / den
            o_ref[:, hs] = o.astype(BF16)
            lse_acc = jnp.where(lane == h, m + jnp.log(den), lse_acc)
        lse_ref[...] = lse_acc

    blk = (None, DIL_BLK, HW)
    prev = lambda n: jnp.maximum(n - 1, 0)
    return pl.pallas_call(
        body, grid=(d, nb),
        in_specs=[pl.BlockSpec(blk, lambda r, n: (r, n, 0)),
                  pl.BlockSpec(blk, lambda r, n: (r, prev(n), 0)), pl.BlockSpec(blk, lambda r, n: (r, n, 0)),
                  pl.BlockSpec(blk, lambda r, n: (r, prev(n), 1)), pl.BlockSpec(blk, lambda r, n: (r, n, 1))],
        out_specs=[pl.BlockSpec(blk, lambda r, n: (r, n, 0)), pl.BlockSpec((None, DIL_BLK, 128), lambda r, n: (r, n, 0))],
        out_shape=[jax.ShapeDtypeStruct((d, M, HW), BF16), jax.ShapeDtypeStruct((d, M, 128), F32)],
        compiler_params=_cp(("parallel", "parallel")), name=f"attn_fwd_{gi}")(q3, kv3, kv3, kv3, kv3)


def attn_bwd(q3, kv3, do3, lse3, dpr3, gi):
    window, d = DIL_PATTERNS[gi]
    HW = DIL_H * DIL_E
    M = q3.shape[1]
    L = M * d
    nb = M // DIL_BLK
    scale = DIL_E ** -0.5
    nt = (((1,), (1,)), ((), ()))
    tn = (((0,), (0,)), ((), ()))

    def body(q0_ref, q1_ref, k_ref, v_ref, do0_ref, do1_ref, l0_ref, l1_ref, r0_ref, r1_ref,
             dq_ref, dk_ref, dv_ref, carry):
        n = pl.program_id(1)

        @pl.when(n == 0)
        def _():
            carry[...] = jnp.zeros(carry.shape, F32)

        dcur, vcur, dprev, vprev0 = _attn_masks()
        dist = jnp.concatenate([dcur, dprev], axis=0)
        valid = jnp.concatenate([vcur, jnp.logical_and(vprev0, n < nb - 1)], axis=0)
        B = DIL_BLK
        for h in range(DIL_H):
            hs = slice(h * DIL_E, (h + 1) * DIL_E)
            sl = _alibi_slope(gi, h) * d
            kh = k_ref[:, hs]
            vh = v_ref[:, hs]
            qcat = jnp.concatenate([q0_ref[:, hs], q1_ref[:, hs]], axis=0)
            docat = jnp.concatenate([do0_ref[:, hs], do1_ref[:, hs]], axis=0)
            lcat = jnp.concatenate([l0_ref[:, h:h + 1], l1_ref[:, h:h + 1]], axis=0)
            rcat = jnp.concatenate([r0_ref[:, h:h + 1], r1_ref[:, h:h + 1]], axis=0)
            s = lax.dot_general(qcat, kh, nt, preferred_element_type=F32) * scale - sl * dist
            p = jnp.exp(jnp.where(valid, s - lcat, -jnp.inf))
            ds = p * (lax.dot_general(docat, vh, nt, preferred_element_type=F32) - rcat)
            ds_b = (ds * scale).astype(BF16)
            dv_ref[:, hs] = lax.dot_general(p.astype(BF16), docat, tn, preferred_element_type=F32).astype(BF16)
            dk_ref[:, hs] = lax.dot_general(ds_b, qcat, tn, preferred_element_type=F32).astype(BF16)
            dqc = jnp.dot(ds_b, kh, preferred_element_type=F32)
            dq_ref[:, hs] = (carry[:, hs] + dqc[:B]).astype(BF16)
            carry[:, hs] = dqc[B:]

    blk = (None, DIL_BLK, HW)
    sblk = (None, DIL_BLK, 128)
    oblk = (DIL_BLK, HW)
    nxt = lambda n: jnp.minimum(n + 1, nb - 1)
    here = lambda c: (lambda r, n: (r, n, c))
    ahead = lambda c: (lambda r, n: (r, nxt(n), c))
    outs = pl.pallas_call(
        body, grid=(d, nb),
        in_specs=[pl.BlockSpec(blk, here(0)), pl.BlockSpec(blk, ahead(0)),
                  pl.BlockSpec(blk, here(0)), pl.BlockSpec(blk, here(1)),
                  pl.BlockSpec(blk, here(0)), pl.BlockSpec(blk, ahead(0)),
                  pl.BlockSpec(sblk, here(0)), pl.BlockSpec(sblk, ahead(0)),
                  pl.BlockSpec(sblk, here(0)), pl.BlockSpec(sblk, ahead(0))],
        out_specs=[pl.BlockSpec(oblk, lambda r, n: (n, r))] * 3,
        out_shape=[jax.ShapeDtypeStruct((M, d * HW), BF16)] * 3,
        scratch_shapes=[pltpu.VMEM(oblk, F32)], compiler_params=_cp(("parallel", "arbitrary")),
        name=f"attn_bwd_{gi}")(q3, q3, kv3, kv3, do3, do3, lse3, lse3, dpr3, dpr3)
    return [t.reshape(L, HW) for t in outs]


def _merge_weights(l_tiles, h):
    ls = [t[:, h:h + 1] for t in l_tiles]
    mx = functools.reduce(jnp.maximum, ls)
    es = [jnp.exp(l - mx) for l in ls]
    den = functools.reduce(lambda a, b: a + b, es)
    return [e / den for e in es]


def _dil_specs(tr, arrs):
    return [pl.BlockSpec((a.shape[0], tr // a.shape[0], a.shape[2]), lambda i: (0, i, 0)) for a in arrs]


def _dil_scratch(tr, arrs):
    return [pltpu.VMEM((a.shape[2] // 128, tr, 128), F32) for a in arrs if a.shape[0] > 1]


def _undilate(refs3, scrs, tr):
    out, k = [], 0
    for ref in refs3:
        d, _, W = ref.shape
        if d == 1:
            out.append(lambda c, ref=ref: ref[0, :, c * 128:(c + 1) * 128])
            continue
        scr = scrs[k]
        k += 1
        for r in range(d):
            for c in range(W // 128):
                scr.at[c][pl.ds(r, tr // d, stride=d), :] = ref[r, :, c * 128:(c + 1) * 128].astype(F32)
        out.append(lambda c, scr=scr: scr[c])
    return out


def merge_fwd(os3, lses3, z):
    HW = os3[0].shape[2]
    L = os3[0].shape[0] * os3[0].shape[1]
    tr = _tile(L, 256, 16)
    ng = len(os3)
    n_scr = len(_dil_scratch(tr, os3))

    def body(*refs):
        z_ref, out_ref = refs[2 * ng], refs[2 * ng + 1]
        scrs = refs[2 * ng + 2:]
        o_get = _undilate(refs[:ng], scrs[:n_scr], tr)
        l_tiles = [g(0) for g in _undilate(refs[ng:2 * ng], scrs[n_scr:], tr)]
        for h in range(DIL_H):
            hs = slice(h * DIL_E, (h + 1) * DIL_E)
            ws = _merge_weights(l_tiles, h)
            om = functools.reduce(lambda a, b: a + b, [w * o(h).astype(F32) for w, o in zip(ws, o_get)])
            out_ref[:, hs] = (om * _silu(z_ref[:, hs].astype(F32))).astype(BF16)

    return pl.pallas_call(
        body, grid=(L // tr,),
        in_specs=_dil_specs(tr, os3) + _dil_specs(tr, lses3) + _row_specs(tr, [HW]),
        out_specs=_row_specs(tr, [HW])[0], out_shape=jax.ShapeDtypeStruct((L, HW), BF16),
        scratch_shapes=_dil_scratch(tr, os3) + _dil_scratch(tr, lses3),
        compiler_params=_cp(("parallel",)), name="merge_fwd")(*os3, *lses3, z)


def merge_bwd(dgated, os3, lses3, z):
    HW = os3[0].shape[2]
    L = os3[0].shape[0] * os3[0].shape[1]
    tr = _tile(L, 256, 16)
    ng = len(os3)
    n_scr = len(_dil_scratch(tr, os3))

    def body(*refs):
        dg_ref = refs[0]
        z_ref = refs[1 + 2 * ng]
        outs = refs[2 + 2 * ng:2 + 2 * ng + 2 * ng + 1]
        scrs = refs[2 + 2 * ng + 2 * ng + 1:]
        do_out, dpr_out, dz_ref = outs[:ng], outs[ng:2 * ng], outs[2 * ng]
        o_get = _undilate(refs[1:1 + ng], scrs[:n_scr], tr)
        l_tiles = [g(0) for g in _undilate(refs[1 + ng:1 + 2 * ng], scrs[n_scr:2 * n_scr], tr)]
        stage = scrs[2 * n_scr:]
        do_stage, dpr_stage, k = [], [], 0
        for g in range(ng):
            if do_out[g].shape[0] == 1:
                do_stage.append(None)
                dpr_stage.append(None)
            else:
                do_stage.append(stage[2 * k])
                dpr_stage.append(stage[2 * k + 1])
                k += 1
        lane = lax.broadcasted_iota(jnp.int32, (tr, 128), 1)
        accs = [jnp.zeros((tr, 128), F32) for _ in range(ng)]
        for h in range(DIL_H):
            hs = slice(h * DIL_E, (h + 1) * DIL_E)
            ws = _merge_weights(l_tiles, h)
            ov = [o(h).astype(F32) for o in o_get]
            om = functools.reduce(lambda a, b: a + b, [w * o for w, o in zip(ws, ov)])
            zv = z_ref[:, hs].astype(F32)
            dgv = dg_ref[:, hs].astype(F32)
            dom = dgv * _silu(zv)
            dz_ref[:, hs] = (dgv * om * _dsilu(zv)).astype(BF16)
            dws = [jnp.sum(dom * o, axis=-1, keepdims=True) for o in ov]
            dwbar = functools.reduce(lambda a, b: a + b, [w * dw for w, dw in zip(ws, dws)])
            for g in range(ng):
                if do_stage[g] is None:
                    do_out[g][0, :, hs] = (ws[g] * dom).astype(BF16)
                else:
                    do_stage[g][h] = ws[g] * dom
                accs[g] = jnp.where(lane == h, ws[g] * dwbar, accs[g])
        for g in range(ng):
            d = do_out[g].shape[0]
            if d == 1:
                dpr_out[g][0] = accs[g]
                continue
            dpr_stage[g][0] = accs[g]
            for r in range(d):
                dpr_out[g][r] = dpr_stage[g].at[0][pl.ds(r, tr // d, stride=d), :]
                for c in range(HW // 128):
                    do_out[g][r, :, c * 128:(c + 1) * 128] = do_stage[g].at[c][pl.ds(r, tr // d, stride=d), :].astype(BF16)

    stage_shapes = []
    for o3 in os3:
        if o3.shape[0] > 1:
            stage_shapes += [pltpu.VMEM((HW // 128, tr, 128), F32), pltpu.VMEM((1, tr, 128), F32)]
    outs = pl.pallas_call(
        body, grid=(L // tr,),
        in_specs=_row_specs(tr, [HW]) + _dil_specs(tr, os3) + _dil_specs(tr, lses3) + _row_specs(tr, [HW]),
        out_specs=_dil_specs(tr, os3) + _dil_specs(tr, lses3) + _row_specs(tr, [HW]),
        out_shape=[jax.ShapeDtypeStruct(o.shape, BF16) for o in os3] + [jax.ShapeDtypeStruct(l.shape, F32) for l in lses3]
        + [jax.ShapeDtypeStruct((L, HW), BF16)],
        scratch_shapes=_dil_scratch(tr, os3) + _dil_scratch(tr, lses3) + stage_shapes,
        compiler_params=_cp(("parallel",)), name="merge_bwd")(dgated, *os3, *lses3, z)
    return outs[:ng], outs[ng:2 * ng], outs[2 * ng]


def ada_fwd(c8, ada_w):
    nl, D, Ws = ada_w.shape
    tn = _tile(Ws, 512)

    def body(c_ref, w_ref, o_ref):
        o_ref[...] = jnp.dot(_silu(c_ref[...]), w_ref[...], precision=lax.Precision.HIGHEST,
                             preferred_element_type=F32)

    return pl.pallas_call(
        body, grid=(nl, Ws // tn),
        in_specs=[pl.BlockSpec((N_DEV, D), lambda l, j: (0, 0)), pl.BlockSpec((None, D, tn), lambda l, j: (l, 0, j))],
        out_specs=pl.BlockSpec((None, N_DEV, tn), lambda l, j: (l, 0, j)),
        out_shape=jax.ShapeDtypeStruct((nl, N_DEV, Ws), F32), compiler_params=_cp(("parallel", "parallel")),
        name="ada_fwd")(c8, ada_w)


def ada_wgrad(c8t, dmod):
    nl, _, Ws = dmod.shape
    D = c8t.shape[0]
    tm = _tile(D, 512, 8)

    def body(c_ref, d_ref, o_ref):
        sc = _silu(c_ref[...])
        acc = sc[:, 0:1] * d_ref[0:1, :]
        for e in range(1, N_DEV):
            acc = acc + sc[:, e:e + 1] * d_ref[e:e + 1, :]
        o_ref[...] = acc

    return pl.pallas_call(
        body, grid=(nl, D // tm),
        in_specs=[pl.BlockSpec((tm, N_DEV), lambda l, i: (i, 0)), pl.BlockSpec((None, N_DEV, Ws), lambda l, i: (l, 0, 0))],
        out_specs=pl.BlockSpec((None, tm, Ws), lambda l, i: (l, i, 0)),
        out_shape=jax.ShapeDtypeStruct((nl, D, Ws), F32), compiler_params=_cp(("parallel", "parallel")),
        name="ada_wgrad")(c8t, dmod)


def adamw(w, g, m, v, name):
    R, C = w.shape
    tr = _tile(R, 256, 8)
    c1 = 1.0 - ADAM_B1 ** ADAM_STEP
    c2 = 1.0 - ADAM_B2 ** ADAM_STEP

    def body(w_ref, g_ref, m_ref, v_ref, d_ref, nm_ref, nv_ref):
        gv = g_ref[...]
        nm = ADAM_B1 * m_ref[...] + (1.0 - ADAM_B1) * gv
        nv = ADAM_B2 * v_ref[...] + (1.0 - ADAM_B2) * (gv * gv)
        nm_ref[...] = nm
        nv_ref[...] = nv
        d_ref[...] = -ADAM_LR * ((nm / c1) / (jnp.sqrt(nv / c2) + ADAM_EPS) + ADAM_WD * w_ref[...])

    return pl.pallas_call(
        body, grid=(R // tr,), in_specs=_row_specs(tr, [C] * 4), out_specs=_row_specs(tr, [C] * 3),
        out_shape=[jax.ShapeDtypeStruct((R, C), F32)] * 3, compiler_params=_cp(("parallel",)), name=name)(w, g, m, v)


def sum_leading(t, name, out_dtype=F32):
    S, R, C = t.shape
    tr = _tile(R, 256, 16)

    def body(t_ref, o_ref):
        acc = t_ref[0].astype(F32)
        for s in range(1, S):
            acc = acc + t_ref[s].astype(F32)
        o_ref[...] = acc.astype(out_dtype)

    return pl.pallas_call(
        body, grid=(R // tr,), in_specs=[pl.BlockSpec((S, tr, C), lambda i: (0, i, 0))],
        out_specs=pl.BlockSpec((tr, C), lambda i: (i, 0)), out_shape=jax.ShapeDtypeStruct((R, C), out_dtype),
        compiler_params=_cp(("parallel",)), name=name)(t)


def add_half(g, a, core, name):
    S, R, C = g.shape
    h = R // 2
    tr = _tile(h, 256, 16)
    nb = h // tr

    def body(core_ref, g_ref, a_ref, o_ref):
        o_ref[...] = (g_ref[...].astype(F32) + a_ref[...].astype(F32)).astype(BF16)

    return pl.pallas_call(
        body,
        grid_spec=pltpu.PrefetchScalarGridSpec(
            num_scalar_prefetch=1, grid=(S, nb),
            in_specs=[pl.BlockSpec((None, tr, C), lambda s, i, core_ref: (s, core_ref[0] * nb + i, 0)),
                      pl.BlockSpec((None, tr, C), lambda s, i, core_ref: (s, i, 0))],
            out_specs=pl.BlockSpec((None, tr, C), lambda s, i, core_ref: (s, i, 0))),
        out_shape=jax.ShapeDtypeStruct((S, h, C), BF16), compiler_params=_cp(("parallel", "parallel")),
        name=name)(core, g, a)


def sum_partials(own, landed, chip, name):
    _, h, C = own.shape
    tr = _tile(h, 256, 16)

    def body(chip_ref, own_ref, l_ref, o_ref):
        acc = own_ref[...].astype(F32)
        for j in range(3):
            acc = acc + l_ref[j].astype(F32)
        o_ref[...] = acc

    return pl.pallas_call(
        body,
        grid_spec=pltpu.PrefetchScalarGridSpec(
            num_scalar_prefetch=1, grid=(h // tr,),
            in_specs=[pl.BlockSpec((None, tr, C), lambda i, chip_ref: (chip_ref[0], i, 0)),
                      pl.BlockSpec((3, tr, C), lambda i, chip_ref: (0, i, 0))],
            out_specs=pl.BlockSpec((tr, C), lambda i, chip_ref: (i, 0))),
        out_shape=jax.ShapeDtypeStruct((h, C), F32), compiler_params=_cp(("parallel",)), name=name)(chip, own, landed)


def adamw_halves(w, g_mine, g_theirs, m, v, core, name):
    R, C = w.shape
    h = R // 2
    tr = _tile(h, 256, 8)
    nbh = h // tr
    c1 = 1.0 - ADAM_B1 ** ADAM_STEP
    c2 = 1.0 - ADAM_B2 ** ADAM_STEP

    def body(core_ref, w_ref, gm_ref, gt_ref, m_ref, v_ref, g_ref, d_ref, nm_ref, nv_ref):
        mine = (pl.program_id(0) // nbh) == core_ref[0]
        gv = jnp.where(mine, gm_ref[...], gt_ref[...])
        g_ref[...] = gv
        nm = ADAM_B1 * m_ref[...] + (1.0 - ADAM_B1) * gv
        nv = ADAM_B2 * v_ref[...] + (1.0 - ADAM_B2) * (gv * gv)
        nm_ref[...] = nm
        nv_ref[...] = nv
        d_ref[...] = -ADAM_LR * ((nm / c1) / (jnp.sqrt(nv / c2) + ADAM_EPS) + ADAM_WD * w_ref[...])

    full = pl.BlockSpec((tr, C), lambda i, core_ref: (i, 0))
    halfspec = pl.BlockSpec((tr, C), lambda i, core_ref: (i % nbh, 0))
    return pl.pallas_call(
        body,
        grid_spec=pltpu.PrefetchScalarGridSpec(
            num_scalar_prefetch=1, grid=(2 * nbh,), in_specs=[full, halfspec, halfspec, full, full],
            out_specs=[full] * 4),
        out_shape=[jax.ShapeDtypeStruct((R, C), F32)] * 4, compiler_params=_cp(("parallel",)),
        name=name)(core, w, g_mine, g_theirs, m, v)


_ANY = pl.BlockSpec(memory_space=pl.ANY)


def _place():
    x, y, c = lax.axis_index("x"), lax.axis_index("y"), lax.axis_index("c")
    chips = [(1 - x, y), (x, 1 - y), (1 - x, 1 - y)]
    return x, y, c, chips


def allgather_small(v, name, after=None):
    R, W = v.shape
    extra = [] if after is None else [after]

    def body(x_ref, *rest):
        out_ref, send_sems, recv_sems, local_sem = rest[len(extra):]
        x, y, c, chips = _place()
        me, sibling = (x, y, c), (x, y, 1 - c)

        def rows(px, py, pc):
            return out_ref.at[pl.ds((4 * px + 2 * py + pc) * R, R), :]

        def copy(k, block, to, src=None):
            return pltpu.make_async_remote_copy(
                src_ref=rows(*block) if src is None else src, dst_ref=rows(*block),
                send_sem=send_sems.at[k], recv_sem=recv_sems.at[k], device_id=to, device_id_type=MESH)

        mine = pltpu.make_async_copy(x_ref, rows(*me), local_sem)
        mine.start()
        first = [copy(0, me, sibling, src=x_ref)]
        first += [copy(1 + j, me, (*chip, c), src=x_ref) for j, chip in enumerate(chips)]
        for cp in first:
            cp.start()
        passed = [copy(4 + j, (*chip, c), sibling) for j, chip in enumerate(chips)]
        for j, chip in enumerate(chips):
            copy(1 + j, (*chip, c), me).wait_recv()
            passed[j].start()
        copy(0, sibling, me).wait_recv()
        for j, chip in enumerate(chips):
            copy(4 + j, (*chip, 1 - c), me).wait_recv()
        for cp in first + passed:
            cp.wait_send()
        mine.wait()

    return pl.pallas_call(
        body, out_shape=jax.ShapeDtypeStruct((N_DEV * R, W), v.dtype),
        in_specs=[pl.BlockSpec(memory_space=pltpu.VMEM)] + [_ANY] * len(extra),
        out_specs=pl.BlockSpec(memory_space=pltpu.VMEM),
        scratch_shapes=[pltpu.SemaphoreType.DMA((7,)), pltpu.SemaphoreType.DMA((7,)), pltpu.SemaphoreType.DMA],
        name=name)(v, *extra)


def allgather_weights(shards, name="allgather_weights"):
    n = len(shards)

    def body(*refs):
        ins, outs = refs[:n], refs[n:2 * n]
        send_sems, recv_sems = refs[2 * n:]
        x, y, c, chips = _place()
        p = 2 * x + y
        sibling = (x, y, 1 - c)

        def half(i, chip_id, core, ref=None):
            r = outs[i].at[chip_id] if ref is None else ref
            return r.at[core]

        def copy(i, k, chip_id, core, to, src=None):
            return pltpu.make_async_remote_copy(
                src_ref=half(i, chip_id, core) if src is None else src, dst_ref=half(i, chip_id, core),
                send_sem=send_sems.at[6 * i + k], recv_sem=recv_sems.at[6 * i + k], device_id=to, device_id_type=MESH)

        first = [copy(i, j, p, c, (*chip, c), src=half(i, p, c, ref=ins[i]))
                 for i in range(n) for j, chip in enumerate(chips)]
        for cp in first:
            cp.start()
        passed = []
        for i in range(n):
            for j, (cx, cy) in enumerate(chips):
                copy(i, j, 2 * cx + cy, c, sibling).wait_recv()
                fw = copy(i, 3 + j, 2 * cx + cy, c, sibling)
                fw.start()
                passed.append(fw)
        for i in range(n):
            for j, (cx, cy) in enumerate(chips):
                copy(i, 3 + j, 2 * cx + cy, 1 - c, sibling).wait_recv()
        for cp in first + passed:
            cp.wait_send()

    split = [s.reshape(2, s.shape[0] // 2, s.shape[1]) for s in shards]
    outs = pl.pallas_call(
        body, out_shape=[jax.ShapeDtypeStruct((N_CHIPS,) + s.shape, s.dtype) for s in split],
        in_specs=[_ANY] * n, out_specs=[_ANY] * n,
        scratch_shapes=[pltpu.SemaphoreType.DMA((6 * n,)), pltpu.SemaphoreType.DMA((6 * n,))],
        name=name)(*split)
    chip = 2 * lax.axis_index("x") + lax.axis_index("y")
    return [lax.dynamic_update_index_in_dim(o, s, chip, 0).reshape((N_CHIPS,) + sh.shape)
            for o, s, sh in zip(outs, split, shards)]


_HBM = pl.BlockSpec(memory_space=pltpu.HBM)
_SEM = pl.BlockSpec(memory_space=pltpu.SEMAPHORE)
_EFFECT = pltpu.SideEffectType.DATAFLOW_SIDE_EFFECTING


def _chip_copies(kind, srcs, lands, send_sems, recv_sems):
    x, y, c, chips = _place()
    p = 2 * x + y
    cps = []
    for i in range(len(srcs)):
        for j, (cx, cy) in enumerate(chips):
            if kind == "gather":
                src, dst = srcs[i].at[c], lands[i].at[p, c]
            else:
                src, dst = srcs[i].at[2 * cx + cy], lands[i].at[j]
            cps.append(pltpu.make_async_remote_copy(
                src_ref=src, dst_ref=dst, send_sem=send_sems.at[3 * i + j], recv_sem=recv_sems.at[3 * i + j],
                device_id=(cx, cy, c), device_id_type=MESH))
    return cps


def split_start(kind, srcs, land_shapes, after, name):
    n = len(srcs)

    def body(*refs):
        src_refs, land_refs = refs[:n], refs[n:2 * n]
        send_sems, recv_sems = refs[2 * n + 1], refs[2 * n + 2]
        token = refs[-1]
        for cp in _chip_copies(kind, src_refs, land_refs, send_sems, recv_sems):
            cp.start()
        token[...] = jnp.zeros_like(token)

    lands = [pltpu.with_memory_space_constraint(lax.empty(s, BF16), pltpu.HBM) for s in land_shapes]
    outs = pl.pallas_call(
        body, name=name,
        out_shape=(pltpu.SemaphoreType.DMA((3 * n,)), pltpu.SemaphoreType.DMA((3 * n,)),
                   *[pltpu.HBM(s.shape, s.dtype) for s in srcs], *[pltpu.HBM(s, BF16) for s in land_shapes],
                   jax.ShapeDtypeStruct((8, 128), F32)),
        in_specs=[_HBM] * (2 * n) + [_ANY],
        out_specs=(_SEM, _SEM, *([_HBM] * (2 * n)), pl.BlockSpec(memory_space=pltpu.VMEM)),
        input_output_aliases={i: 2 + i for i in range(2 * n)},
        compiler_params=pltpu.CompilerParams(has_side_effects=_EFFECT),
    )(*[pltpu.with_memory_space_constraint(s, pltpu.HBM) for s in srcs], *lands, after)
    return outs[0], outs[1], outs[2:2 + n], outs[2 + n:2 + 2 * n], outs[-1]


def split_wait(kind, send_sems, recv_sems, srcs, lands, after, name):
    n = len(srcs)

    def body(*refs):
        src_refs, land_refs = refs[:n], refs[n:2 * n]
        ssem, rsem = refs[2 * n], refs[2 * n + 1]
        for cp in _chip_copies(kind, src_refs, land_refs, ssem, rsem):
            cp.wait_send()
            cp.wait_recv()

    outs = pl.pallas_call(
        body, name=name,
        out_shape=[pltpu.HBM(s.shape, s.dtype) for s in srcs] + [pltpu.HBM(s.shape, s.dtype) for s in lands],
        in_specs=[_HBM] * (2 * n) + [_SEM, _SEM, _ANY], out_specs=[_HBM] * (2 * n),
        input_output_aliases={i: i for i in range(2 * n)},
        compiler_params=pltpu.CompilerParams(has_side_effects=_EFFECT),
    )(*srcs, *lands, send_sems, recv_sems, after)
    return outs[:n], outs[n:]


def pass_to_sibling(lands):
    n = len(lands)

    def body(*refs):
        ins, outs = refs[:n], refs[n:2 * n]
        send_sems, recv_sems = refs[2 * n:]
        x, y, c, chips = _place()
        cps = []
        for i in range(n):
            for j, (cx, cy) in enumerate(chips):
                blk = outs[i].at[2 * cx + cy, c]
                cps.append(pltpu.make_async_remote_copy(
                    src_ref=ins[i].at[2 * cx + cy, c], dst_ref=blk, send_sem=send_sems.at[3 * i + j],
                    recv_sem=recv_sems.at[3 * i + j], device_id=(x, y, 1 - c), device_id_type=MESH))
        for cp in cps:
            cp.start()
        for cp in cps:
            cp.wait()

    return pl.pallas_call(
        body, out_shape=[jax.ShapeDtypeStruct(t.shape, t.dtype) for t in lands], in_specs=[_ANY] * n,
        out_specs=[_ANY] * n, input_output_aliases={i: i for i in range(n)},
        scratch_shapes=[pltpu.SemaphoreType.DMA((3 * n,)), pltpu.SemaphoreType.DMA((3 * n,))],
        name="ag_pass_to_sibling")(*lands)


def exchange_halves_to_sibling(gs, name):
    n = len(gs)

    def body(*refs):
        ins, outs = refs[:n], refs[n:2 * n]
        send_sems, recv_sems = refs[2 * n:]
        x, y, c, _ = _place()
        cps = []
        for i in range(n):
            h = ins[i].shape[1] // 2
            cps.append(pltpu.make_async_remote_copy(
                src_ref=ins[i].at[:, pl.ds((1 - c) * h, h), :], dst_ref=outs[i],
                send_sem=send_sems.at[i], recv_sem=recv_sems.at[i], device_id=(x, y, 1 - c), device_id_type=MESH))
        for cp in cps:
            cp.start()
        for cp in cps:
            cp.wait()

    return pl.pallas_call(
        body, out_shape=[jax.ShapeDtypeStruct((g.shape[0], g.shape[1] // 2, g.shape[2]), g.dtype) for g in gs],
        in_specs=[_ANY] * n, out_specs=[_ANY] * n,
        scratch_shapes=[pltpu.SemaphoreType.DMA((n,)), pltpu.SemaphoreType.DMA((n,))],
        name=name)(*gs)


def scatter_to_chips(ps, name):
    n = len(ps)

    def body(*refs):
        ins, outs = refs[:n], refs[n:2 * n]
        send_sems, recv_sems = refs[2 * n:]
        x, y, c, chips = _place()
        cps = []
        for i in range(n):
            for j, (cx, cy) in enumerate(chips):
                cps.append(pltpu.make_async_remote_copy(
                    src_ref=ins[i].at[2 * cx + cy], dst_ref=outs[i].at[j], send_sem=send_sems.at[3 * i + j],
                    recv_sem=recv_sems.at[3 * i + j], device_id=(cx, cy, c), device_id_type=MESH))
        for cp in cps:
            cp.start()
        for cp in cps:
            cp.wait()

    return pl.pallas_call(
        body, out_shape=[jax.ShapeDtypeStruct((3,) + t.shape[1:], t.dtype) for t in ps],
        in_specs=[_ANY] * n, out_specs=[_ANY] * n,
        scratch_shapes=[pltpu.SemaphoreType.DMA((3 * n,)), pltpu.SemaphoreType.DMA((3 * n,))],
        name=name)(*ps)


def join_halves(rs, name):
    n = len(rs)

    def body(*refs):
        ins, outs = refs[:n], refs[n:2 * n]
        send_sems, recv_sems = refs[2 * n:]
        x, y, c, _ = _place()
        cps = [pltpu.make_async_remote_copy(
            src_ref=ins[i], dst_ref=outs[i], send_sem=send_sems.at[i], recv_sem=recv_sems.at[i],
            device_id=(x, y, 1 - c), device_id_type=MESH) for i in range(n)]
        for cp in cps:
            cp.start()
        for cp in cps:
            cp.wait()

    return pl.pallas_call(
        body, out_shape=[jax.ShapeDtypeStruct(r.shape, r.dtype) for r in rs],
        in_specs=[_ANY] * n, out_specs=[_ANY] * n,
        scratch_shapes=[pltpu.SemaphoreType.DMA((n,)), pltpu.SemaphoreType.DMA((n,))],
        name=name)(*rs)


def _pack(parts, row_mult=8):
    flat = jnp.concatenate([p.reshape(-1).astype(F32) for p in parts])
    unit = row_mult * 128
    n = -(-flat.shape[0] // unit) * unit
    return jnp.pad(flat, (0, n - flat.shape[0])).reshape(n // 128, 128)


def _unpack(flat, shapes):
    out, off = [], 0
    for s in shapes:
        n = int(np.prod(s))
        out.append(flat[off:off + n].reshape(s))
        off += n
    return out


def _gather_packed(parts, name):
    packed = _pack(parts)
    g = allgather_small(packed, name).reshape(N_DEV, -1)
    return _unpack_rows(g, [p.shape for p in parts])


def _unpack_rows(g, shapes):
    out, off = [], 0
    for s in shapes:
        n = int(np.prod(s))
        out.append(g[:, off:off + n].reshape((g.shape[0],) + tuple(s)))
        off += n
    return out


def _by_chip(t, axis):
    return jnp.concatenate([t[2 * p] for p in range(N_CHIPS)], axis=axis)


def kernel(x, c, ada_w, ada_b, ln_g, ln_b, a_in_w, a_conv_w, a_conv_b, a_dt_bias, a_A_log, a_D, a_norm_g, a_out_w, kv_w, b_in_w, b_out_w, loss_target, m_ada_w, m_ada_b, m_ln_g, m_ln_b, m_a_in_w, m_a_conv_w, m_a_conv_b, m_a_dt_bias, m_a_A_log, m_a_D, m_a_norm_g, m_a_out_w, m_kv_w, m_b_in_w, m_b_out_w, v_ada_w, v_ada_b, v_ln_g, v_ln_b, v_a_in_w, v_a_conv_w, v_a_conv_b, v_a_dt_bias, v_a_A_log, v_a_D, v_a_norm_g, v_a_out_w, v_kv_w, v_b_in_w, v_b_out_w):
    ax, ay, ac = lax.axis_index("x"), lax.axis_index("y"), lax.axis_index("c")
    chip = 2 * ax + ay
    dev = 4 * ax + 2 * ay + ac
    xin = x[0]
    tgt = loss_target[0]
    L, D = xin.shape
    G, P = SSD_G, SSD_P
    H = a_dt_bias.shape[1]
    Kh = H // G
    DI = H * P
    CONVD = a_conv_b.shape[1] * N_CHIPS
    HW = DIL_H * DIL_E
    Ws = ada_w.shape[2]

    (w_in_g,) = allgather_weights([a_in_w[0].astype(BF16)], "allgather_w_in")
    later = [a_out_w[0].astype(BF16), kv_w.astype(BF16), b_in_w[0].astype(BF16), b_out_w[0].astype(BF16)]
    later_split = [s.reshape(2, s.shape[0] // 2, s.shape[1]) for s in later]
    ag_ssem, ag_rsem, ag_srcs, ag_lands, ag_token = split_start(
        "gather", later_split, [(N_CHIPS,) + s.shape for s in later_split], w_in_g, "ag_later_start")
    w_in = jnp.transpose(w_in_g, (1, 0, 2)).reshape(D, -1)
    w_zx = w_in
    w_dt = jnp.pad(w_in[:, DI + CONVD:], ((0, 0), (0, 128 - H)))

    c8, cw8, cb8, ng8 = _gather_packed([c[0], a_conv_w[0], a_conv_b[0], a_norm_g[0]], "allgather_small_params")
    conv_w = _by_chip(cw8, 1)
    conv_b = _by_chip(cb8, 0).reshape(1, CONVD)
    norm_g = _by_chip(ng8, 0).reshape(1, DI)

    mod_s = ada_fwd(c8, ada_w)
    (mod8,) = _gather_packed([mod_s], "allgather_small_mod")
    mods = _by_chip(mod8, 2)
    mod = lax.dynamic_index_in_dim(mods, dev, axis=1, keepdims=False) + ada_b
    shift = [mod[l:l + 1, :D] for l in range(DEPTH)]
    scale = [mod[l:l + 1, D:2 * D] for l in range(DEPTH)]
    gate = [mod[l:l + 1, 2 * D:] for l in range(DEPTH)]
    lg = [ln_g[l:l + 1] for l in range(DEPTH)]
    lb = [ln_b[l:l + 1] for l in range(DEPTH)]

    h0 = modulate(xin, scale[0] + ag_token[0:1, 0:1], shift[0], "modulate0")
    zx = mm_nn(h0, w_zx, BF16, "mm_in_zx", n_cols=DI + CONVD)
    dtp = mm_nn(h0, w_dt, F32, "mm_in_dt")
    xbc = conv_fwd(zx, DI, conv_w, conv_b)
    dtp_g = jnp.transpose(dtp[:, :H].reshape(L, G, Kh), (1, 0, 2))
    dtp_gT = jnp.transpose(dtp_g, (0, 2, 1))
    vecs = [a_dt_bias.reshape(G, 1, Kh), a_dt_bias.reshape(G, Kh, 1), a_A_log.reshape(G, 1, Kh),
            a_A_log.reshape(G, Kh, 1), a_D.reshape(G, 1, Kh), a_D.reshape(G, Kh, 1)]
    y_ssd, states = ssd_fwd(xbc, dtp_g, dtp_gT, *vecs, DI)
    yn = rms_gate_fwd(y_ssd, zx, norm_g)
    later_split, ag_lands = split_wait("gather", ag_ssem, ag_rsem, ag_srcs, ag_lands, yn, "ag_later_wait")
    ag_lands = pass_to_sibling(ag_lands)
    w_out_g, w_kv_g, w_bin_g, w_bout_g = [
        lax.dynamic_update_index_in_dim(o, s, chip, 0).reshape((N_CHIPS,) + full.shape)
        for o, s, full in zip(ag_lands, later_split, later)]
    ymix0 = mm_nn(yn, w_out_g, F32, "mm_out_a", stack="row")
    x1, x1b, h1 = ln_mid(xin, ymix0, gate[0], lg[0], lb[0], scale[1], shift[1])

    n_grp = len(DIL_PATTERNS)
    cb = HW // 512
    assert w_bin_g.shape[2] == HW
    kv3 = [mm_cols_dilated(x1b, w_kv_g, [g * cb + t for t in range(cb)] + [(n_grp + g) * cb + t for t in range(cb)],
                           DIL_PATTERNS[g][1], f"mm_kv_{g}") for g in range(n_grp)]
    q3 = [mm_cols_dilated(h1, w_bin_g, [g * cb + t for t in range(cb)], DIL_PATTERNS[g][1], f"mm_q_{g}")
          for g in range(n_grp)]
    z_b = mm_nn(h1, w_bin_g[n_grp], BF16, "mm_z_b")
    os_, lses = [], []
    for gi in range(len(DIL_PATTERNS)):
        o, lse = attn_fwd(q3[gi], kv3[gi], gi)
        os_.append(o)
        lses.append(lse)
    om = merge_fwd(os_, lses, z_b)
    ymix1 = mm_nn(om, w_bout_g, F32, "mm_out_b", stack="col")
    dres2, dy2, dg1, db1, dgate1, sq = ln_final_fwd_bwd(x1, ymix1, gate[1], lg[1], lb[1], tgt)
    loss_part = 0.5 * jnp.sum(sq) / D

    g_bout = mm_tn(om, dy2, BF16, "mm_gw_out_b", stack="col")
    dgated = mm_nt(dy2, w_bout_g, BF16, "mm_gx_out_b", stack="col")
    dos, dprs, dz_b = merge_bwd(dgated, os_, lses, z_b)
    dqs, dks, dvs = [], [], []
    for gi in range(len(DIL_PATTERNS)):
        dq, dk, dv = attn_bwd(q3[gi], kv3[gi], dos[gi], lses[gi], dprs[gi], gi)
        dqs.append(dq)
        dks.append(dk)
        dvs.append(dv)
    dqz = jnp.concatenate(dqs + [dz_b], axis=1)
    dkv = jnp.concatenate(dks + dvs, axis=1)
    g_bin = mm_tn(h1, dqz, BF16, "mm_gw_in_b", stack="col")
    dh1 = mm_nt(dqz, w_bin_g, F32, "mm_gx_in_b", stack="col")
    g_kv = mm_tn(x1b, dkv, BF16, "mm_gw_kv", stack="col")
    dx1_kv = mm_nt(dkv, w_kv_g, F32, "mm_gx_kv", stack="col")

    core = ac.astype(jnp.int32).reshape(1)
    chip_i = chip.astype(jnp.int32).reshape(1)

    def begin_scatter(gs, nms, tag):
        sib = exchange_halves_to_sibling(gs, "rs_sibling_exchange_" + tag)
        parts = [add_half(g, a, core, "rs_add_" + nm) for g, a, nm in zip(gs, sib, nms)]
        return split_start("scatter", parts, [(3,) + t.shape[1:] for t in parts], parts[0], "rs_%s_start" % tag)

    def finish_scatter(handles, after, tag):
        nms, owns, landed = [], [], []
        for k, (handle, hn) in enumerate(handles):
            parts, lands = split_wait("scatter", handle[0], handle[1], handle[2], handle[3], after,
                                      "rs_%s%d_wait" % (tag, k))
            nms += hn
            owns += list(parts)
            landed += list(lands)
        halves = [sum_partials(own, t, chip_i, "rs_sum_" + nm) for own, t, nm in zip(owns, landed, nms)]
        theirs = join_halves(halves, "rs_join_halves_" + tag)
        return dict(zip(nms, zip(halves, theirs)))

    names_b = ["kv", "in_b", "out_b"]
    rs_b = begin_scatter([g_kv, g_bin, g_bout], names_b, "b")

    dres1, dy1, dg0, db0, dgate0, dscale1, dshift1 = mod_ln_bwd(
        dres2, dh1, dx1_kv, x1, scale[1], xin, ymix0, gate[0] + rs_b[4][0:1, 0:1], lg[0])
    g_out = mm_tn(yn, dy1, BF16, "mm_gw_out_a", stack="row")
    rs_a1 = begin_scatter([g_out], ["out_a"], "a1")
    dyn = mm_nt(dy1, w_out_g, BF16, "mm_gx_out_a", stack="row")
    dy_ssd, dz_a, dnorm_g = rms_gate_bwd(dyn, y_ssd, zx, norm_g + rs_a1[4][0:1, 0:1])
    dxs, dB, dC, ddtp_g, dbias_g, dalog_g, dD_g = ssd_bwd(xbc, dtp_g, dtp_gT, *vecs, states, dy_ssd, DI)
    dxbc = jnp.concatenate([dxs, dB, dC], axis=1)
    dxbc_pre, dconv_w, dconv_b = conv_bwd(zx, DI, conv_w, conv_b, dxbc)
    dzx = jnp.concatenate([dz_a, dxbc_pre], axis=1)
    ddtp = jnp.pad(jnp.transpose(ddtp_g, (1, 0, 2)).reshape(L, H), ((0, 0), (0, 128 - H)))
    g_zx = mm_tn(h0, dzx, BF16, "mm_gw_in_zx")
    g_dt = mm_tn(h0, ddtp, BF16, "mm_gw_in_dt")
    g_in = jnp.concatenate([g_zx, g_dt[:, :H]], axis=1)
    cs_in = g_in.shape[1] // N_CHIPS
    g_in = jnp.stack([g_in[:, s * cs_in:(s + 1) * cs_in] for s in range(N_CHIPS)])
    rs_a2 = begin_scatter([g_in], ["in_a"], "a2")
    dh0 = mm_nt(dzx, w_zx, F32, "mm_gx_in_zx", after=rs_a2[4])
    dh0_dt = mm_nt(ddtp, w_dt, F32, "mm_gx_in_dt")
    grad_x, dscale0, dshift0 = mod_bwd(dres1, dh0, dh0_dt, xin, scale[0] + rs_a2[4][0:1, 0:1], "mod_bwd0",
                                       through_mod=True)
    g_halves = finish_scatter([(rs_b, names_b)], grad_x, "b")

    def step_halves(w, m, v, nm):
        shp = w.shape
        mine, theirs_ = g_halves[nm]
        outs4 = adamw_halves(w.reshape(-1, shp[-1]), mine, theirs_, m.reshape(-1, shp[-1]), v.reshape(-1, shp[-1]),
                             core, "adamw_" + nm)
        return tuple(t.reshape(shp) for t in outs4)

    big = {
        "kv_w": step_halves(kv_w, m_kv_w, v_kv_w, "kv"),
        "b_in_w": step_halves(b_in_w, m_b_in_w, v_b_in_w, "in_b"),
        "b_out_w": step_halves(b_out_w, m_b_out_w, v_b_out_w, "out_b"),
    }
    g_halves.update(finish_scatter([(rs_a1, ["out_a"]), (rs_a2, ["in_a"])], big["kv_w"][1], "a"))
    big["a_in_w"] = step_halves(a_in_w, m_a_in_w, v_a_in_w, "in_a")
    big["a_out_w"] = step_halves(a_out_w, m_a_out_w, v_a_out_w, "out_a")

    dmod = jnp.concatenate([jnp.concatenate([dshift0, dscale0, dgate0], axis=1),
                            jnp.concatenate([dshift1, dscale1, dgate1], axis=1)], axis=0)
    small_parts = [jnp.concatenate([dg0, dg1], axis=0), jnp.concatenate([db0, db1], axis=0),
                   dbias_g.reshape(1, H), dalog_g.reshape(1, H), dD_g.reshape(1, H),
                   dconv_w, dconv_b, dnorm_g, loss_part.reshape(1, 1)]
    small_shapes = [p.shape for p in small_parts]
    packed = jnp.concatenate([_pack([dmod]), _pack(small_parts)], axis=0)
    n_mod_rows = _pack([dmod]).shape[0]
    gathered = allgather_small(packed, "allgather_small_grads", after=g_halves["in_a"][1]).reshape(N_DEV, -1, 128)
    dmod8 = gathered[:, :n_mod_rows].reshape(N_DEV, -1)[:, :2 * 3 * D].reshape(N_DEV, DEPTH, 3 * D)
    summed = sum_leading(gathered, "sum_small")
    g_ada_b = summed[:n_mod_rows].reshape(-1)[:2 * 3 * D].reshape(DEPTH, 3 * D)
    (g_ln_g, g_ln_b, g_dt_bias, g_a_log, g_dsk, g_conv_w, g_conv_b, g_norm_g, loss_all) = _unpack(
        summed[n_mod_rows:].reshape(-1), small_shapes)
    loss = loss_all.reshape(())
    Cs = CONVD // N_CHIPS
    g_conv_w_s = lax.dynamic_slice_in_dim(g_conv_w, chip * Cs, Cs, axis=1)
    g_conv_b_s = lax.dynamic_slice_in_dim(g_conv_b, chip * Cs, Cs, axis=1)
    g_norm_g_s = lax.dynamic_slice_in_dim(g_norm_g, chip * (DI // N_CHIPS), DI // N_CHIPS, axis=1)
    dmod_s = jnp.transpose(lax.dynamic_slice_in_dim(dmod8, chip * Ws, Ws, axis=2), (1, 0, 2))
    g_ada_w = ada_wgrad(jnp.transpose(c8), dmod_s)

    def step2d(w, g, m, v, nm):
        shp = w.shape
        d_, m_, v_ = adamw(w.reshape(-1, shp[-1]), g.reshape(-1, shp[-1]), m.reshape(-1, shp[-1]),
                           v.reshape(-1, shp[-1]), "adamw_" + nm)
        return g.reshape(shp), d_.reshape(shp), m_.reshape(shp), v_.reshape(shp)

    big["ada_w"] = step2d(ada_w, g_ada_w, m_ada_w, v_ada_w, "ada_w")
    small_names = ["ada_b", "ln_g", "ln_b", "a_conv_w", "a_conv_b", "a_dt_bias", "a_A_log", "a_D", "a_norm_g"]
    small_w = [ada_b, ln_g, ln_b, a_conv_w, a_conv_b, a_dt_bias, a_A_log, a_D, a_norm_g]
    small_m = [m_ada_b, m_ln_g, m_ln_b, m_a_conv_w, m_a_conv_b, m_a_dt_bias, m_a_A_log, m_a_D, m_a_norm_g]
    small_v = [v_ada_b, v_ln_g, v_ln_b, v_a_conv_w, v_a_conv_b, v_a_dt_bias, v_a_A_log, v_a_D, v_a_norm_g]
    small_g = [g_ada_b, g_ln_g, g_ln_b, g_conv_w_s, g_conv_b_s, g_dt_bias, g_a_log, g_dsk, g_norm_g_s]
    shapes = [w.shape for w in small_w]
    small_g = [g.reshape(s) for g, s in zip(small_g, shapes)]
    d_p, m_p, v_p = adamw(_pack(small_w), _pack(small_g), _pack(small_m), _pack(small_v), "adamw_small")
    small = {}
    for nm, g, d_, m_, v_ in zip(small_names, small_g, _unpack(d_p.reshape(-1), shapes), _unpack(m_p.reshape(-1), shapes),
                                 _unpack(v_p.reshape(-1), shapes)):
        small[nm] = (g, d_, m_, v_)
    allw = {**big, **small}
    order = ["ada_w", "ada_b", "ln_g", "ln_b", "a_in_w", "a_conv_w", "a_conv_b", "a_dt_bias", "a_A_log", "a_D",
             "a_norm_g", "a_out_w", "kv_w", "b_in_w", "b_out_w"]
    outs = [loss, grad_x.reshape(x.shape)]
    for k in range(4):
        outs += [allw[n][k] for n in order]
    return tuple(outs)
```

```python
import functools

import jax
import jax.numpy as jnp
import numpy as np
from jax import lax
from jax.experimental import pallas as pl
from jax.experimental.pallas import tpu as pltpu

F32 = jnp.float32
BF16 = jnp.bfloat16
MESH = pl.DeviceIdType.MESH

DEPTH = 2
ALPHA = (2 * DEPTH) ** 0.25
LN_EPS = 1e-5
RMS_EPS = 1e-5
SSD_P = 64
SSD_N = 128
SSD_Q = 256
SSD_G = 8
CONV_W = 4
DIL_PATTERNS = ((128, 1), (512, 4), (2048, 16))
DIL_H = 8
DIL_E = 128
DIL_BLK = 128
ADAM_LR, ADAM_B1, ADAM_B2, ADAM_EPS, ADAM_WD, ADAM_STEP = 0.001, 0.9, 0.999, 1e-08, 0.01, 10

VMEM_LIMIT = 56 * 1024 * 1024
N_CHIPS = 4
N_DEV = 8


def _tile(dim, target, mult=128):
    if dim <= target:
        return dim
    t = (target // mult) * mult
    while t >= mult:
        if dim % t == 0:
            return t
        t -= mult
    return dim


def _cp(sem):
    return pltpu.CompilerParams(dimension_semantics=sem, vmem_limit_bytes=VMEM_LIMIT)


def _sigmoid(x):
    return 1.0 / (1.0 + jnp.exp(-x))


def _silu(x):
    return x * _sigmoid(x)


def _dsilu(x):
    s = _sigmoid(x)
    return s * (1.0 + x * (1.0 - s))


def _softplus(x):
    return jnp.maximum(x, 0.0) + jnp.log(1.0 + jnp.exp(-jnp.abs(x)))


def _mm_call(a, b, out_shape, grid, a_spec, b_spec, o_spec, acc_shape, dims, name, after=None):
    nk = grid[2]
    extra = [] if after is None else [after]

    def prod(a_ref, b_ref):
        return lax.dot_general(a_ref[...].astype(BF16), b_ref[...].astype(BF16), (dims, ((), ())),
                               preferred_element_type=F32)

    def body_single(a_ref, b_ref, *rest):
        o_ref = rest[len(extra)]
        o_ref[...] = prod(a_ref, b_ref).astype(o_ref.dtype)

    def body_multi(a_ref, b_ref, *rest):
        o_ref, acc_ref = rest[len(extra):]
        k = pl.program_id(2)

        @pl.when(k == 0)
        def _():
            acc_ref[...] = prod(a_ref, b_ref)

        @pl.when(jnp.logical_and(k > 0, k < nk - 1))
        def _():
            acc_ref[...] += prod(a_ref, b_ref)

        @pl.when(k == nk - 1)
        def _():
            o_ref[...] = (acc_ref[...] + prod(a_ref, b_ref)).astype(o_ref.dtype)

    return pl.pallas_call(
        body_single if nk == 1 else body_multi, grid=grid, in_specs=[a_spec, b_spec] + [_ANY] * len(extra),
        out_specs=o_spec, out_shape=out_shape, scratch_shapes=[] if nk == 1 else [pltpu.VMEM(acc_shape, F32)],
        compiler_params=_cp(("parallel", "parallel", "arbitrary")), name=name)(a, b, *extra)


def mm_nn(a, b, out_dtype, name, stack=None, tm=1024, tn=1024, tk=2048, n_cols=None):
    M, K = a.shape
    if stack is None:
        N = b.shape[1] if n_cols is None else n_cols
        tn, tk = _tile(N, tn), _tile(K, tk)
        b_spec = pl.BlockSpec((tk, tn), lambda i, j, k: (k, j))
    elif stack == "col":
        S, _, Ns = b.shape
        N = S * Ns
        tn, tk = _tile(Ns, tn), _tile(K, tk)
        npb = Ns // tn
        b_spec = pl.BlockSpec((None, tk, tn), lambda i, j, k: (j // npb, k, j % npb))
    else:
        S, Ks, N = b.shape
        tn, tk = _tile(N, tn), _tile(Ks, tk)
        kpb = Ks // tk
        b_spec = pl.BlockSpec((None, tk, tn), lambda i, j, k: (k // kpb, k % kpb, j))
    tm = _tile(M, tm)
    return _mm_call(a, b, jax.ShapeDtypeStruct((M, N), out_dtype), (M // tm, N // tn, K // tk),
                    pl.BlockSpec((tm, tk), lambda i, j, k: (i, k)), b_spec,
                    pl.BlockSpec((tm, tn), lambda i, j, k: (i, j)), (tm, tn), ((1,), (0,)), name)


def mm_cols_dilated(a, b, gcols, d, name, tm=1024, tn=512):
    L, K = a.shape
    S, _, Ns = b.shape
    tm, tn = _tile(L, tm), _tile(Ns, tn)
    npb = Ns // tn
    nj = len(gcols)
    rows = tm // d

    def body(cols_ref, a_ref, b_ref, o_ref, *scr):
        prod = jnp.dot(a_ref[...], b_ref[...], preferred_element_type=F32)
        if d == 1:
            o_ref[0] = prod.astype(BF16)
        else:
            for c in range(tn // 128):
                scr[0][c] = prod[:, c * 128:(c + 1) * 128]
            for r in range(d):
                for c in range(tn // 128):
                    o_ref[r, :, c * 128:(c + 1) * 128] = scr[0].at[c][pl.ds(r, rows, stride=d), :].astype(BF16)

    return pl.pallas_call(
        body,
        grid_spec=pltpu.PrefetchScalarGridSpec(
            num_scalar_prefetch=1, grid=(L // tm, nj),
            in_specs=[pl.BlockSpec((tm, K), lambda i, j, c: (i, 0)),
                      pl.BlockSpec((None, K, tn), lambda i, j, c: (c[j] // npb, 0, c[j] % npb))],
            out_specs=pl.BlockSpec((d, rows, tn), lambda i, j, c: (0, i, j)),
            scratch_shapes=[] if d == 1 else [pltpu.VMEM((tn // 128, tm, 128), F32)]),
        out_shape=jax.ShapeDtypeStruct((d, L // d, nj * tn), BF16),
        compiler_params=_cp(("parallel", "arbitrary")), name=name)(jnp.asarray(gcols, jnp.int32), a, b)


def mm_nt(a, b, out_dtype, name, stack=None, tm=1024, tn=1024, tk=2048, after=None):
    M, C = a.shape
    if stack is None:
        Kw = b.shape[0]
        tn, tk = _tile(Kw, tn), _tile(C, tk)
        b_spec = pl.BlockSpec((tn, tk), lambda i, j, k: (j, k))
    elif stack == "col":
        S, Kw, Cs = b.shape
        tn, tk = _tile(Kw, tn), _tile(Cs, tk)
        cpb = Cs // tk
        b_spec = pl.BlockSpec((None, tn, tk), lambda i, j, k: (k // cpb, j, k % cpb))
    else:
        S, Ks, _ = b.shape
        Kw = S * Ks
        tn, tk = _tile(Ks, tn), _tile(C, tk)
        jpb = Ks // tn
        b_spec = pl.BlockSpec((None, tn, tk), lambda i, j, k: (j // jpb, j % jpb, k))
    tm = _tile(M, tm)
    return _mm_call(a, b, jax.ShapeDtypeStruct((M, Kw), out_dtype), (M // tm, Kw // tn, C // tk),
                    pl.BlockSpec((tm, tk), lambda i, j, k: (i, k)), b_spec,
                    pl.BlockSpec((tm, tn), lambda i, j, k: (i, j)), (tm, tn), ((1,), (1,)), name, after=after)


def mm_tn(a, b, out_dtype, name, stack=None, n_stack=N_CHIPS, tm=1024, tn=1024, tk=2048):
    L, M = a.shape
    N = b.shape[1]
    tk = _tile(L, tk)
    if stack is None:
        tm, tn = _tile(M, tm), _tile(N, tn)
        o_spec = pl.BlockSpec((tm, tn), lambda i, j, k: (i, j))
        out_shape = (M, N)
    elif stack == "col":
        Ns = N // n_stack
        tm, tn = _tile(M, tm), _tile(Ns, tn)
        npb = Ns // tn
        o_spec = pl.BlockSpec((None, tm, tn), lambda i, j, k: (j // npb, i, j % npb))
        out_shape = (n_stack, M, Ns)
    else:
        Ms = M // n_stack
        tm, tn = _tile(Ms, tm), _tile(N, tn)
        mpb = Ms // tm
        o_spec = pl.BlockSpec((None, tm, tn), lambda i, j, k: (i // mpb, i % mpb, j))
        out_shape = (n_stack, Ms, N)
    return _mm_call(a, b, jax.ShapeDtypeStruct(out_shape, out_dtype), (M // tm, N // tn, L // tk),
                    pl.BlockSpec((tk, tm), lambda i, j, k: (k, i)), pl.BlockSpec((tk, tn), lambda i, j, k: (k, j)),
                    o_spec, (tm, tn), ((0,), (0,)), name)


def _row_specs(tr, widths):
    return [pl.BlockSpec((tr, w), lambda i: (i, 0)) for w in widths]


def _vec_spec(w):
    return pl.BlockSpec((1, w), lambda i: (0, 0))


def _acc_rows(ref, val, i):
    s = jnp.sum(val, axis=0, keepdims=True)

    @pl.when(i == 0)
    def _():
        ref[...] = s

    @pl.when(i > 0)
    def _():
        ref[...] += s


def modulate(x, scale, shift, name):
    L, D = x.shape
    tr = _tile(L, 512, 16)

    def body(x_ref, sc_ref, sh_ref, h_ref):
        h_ref[...] = (x_ref[...] * (1.0 + sc_ref[...]) + sh_ref[...]).astype(BF16)

    return pl.pallas_call(
        body, grid=(L // tr,), in_specs=_row_specs(tr, [D]) + [_vec_spec(D)] * 2, out_specs=_row_specs(tr, [D])[0],
        out_shape=jax.ShapeDtypeStruct((L, D), BF16), compiler_params=_cp(("parallel",)), name=name)(x, scale, shift)


def _ln_core(x, y, gate, g, b):
    u = ALPHA * x + (1.0 + gate) * y
    mu = jnp.mean(u, axis=-1, keepdims=True)
    d = u - mu
    var = jnp.mean(d * d, axis=-1, keepdims=True)
    rstd = lax.rsqrt(var + LN_EPS)
    xhat = d * rstd
    return xhat * g + b, xhat, rstd


def ln_mid(x, y, gate, g, b, scale, shift):
    L, D = x.shape
    tr = _tile(L, 256, 16)

    def body(x_ref, y_ref, gate_ref, g_ref, b_ref, sc_ref, sh_ref, x1_ref, x1b_ref, h_ref):
        x1, _, _ = _ln_core(x_ref[...], y_ref[...], gate_ref[...], g_ref[...], b_ref[...])
        x1_ref[...] = x1
        x1b_ref[...] = x1.astype(BF16)
        h_ref[...] = (x1 * (1.0 + sc_ref[...]) + sh_ref[...]).astype(BF16)

    return pl.pallas_call(
        body, grid=(L // tr,), in_specs=_row_specs(tr, [D, D]) + [_vec_spec(D)] * 5,
        out_specs=_row_specs(tr, [D, D, D]),
        out_shape=[jax.ShapeDtypeStruct((L, D), F32), jax.ShapeDtypeStruct((L, D), BF16),
                   jax.ShapeDtypeStruct((L, D), BF16)],
        compiler_params=_cp(("parallel",)), name="ln_mid")(x, y, gate, g, b, scale, shift)


def _ln_bwd_rows(dout_v, xhat, rstd, g):
    dxh = dout_v * g
    m1 = jnp.mean(dxh, axis=-1, keepdims=True)
    m2 = jnp.mean(dxh * xhat, axis=-1, keepdims=True)
    return rstd * (dxh - m1 - xhat * m2)


def ln_final_fwd_bwd(x, y, gate, g, b, target):
    L, D = x.shape
    tr = _tile(L, 256, 16)

    def body(x_ref, y_ref, gate_ref, g_ref, b_ref, t_ref, dres_ref, dy_ref, dg_ref, db_ref, dgate_ref, sq_ref):
        i = pl.program_id(0)
        yv = y_ref[...]
        out, xhat, rstd = _ln_core(x_ref[...], yv, gate_ref[...], g_ref[...], b_ref[...])
        err = out - t_ref[...]
        dout_v = err * (1.0 / D)
        du = _ln_bwd_rows(dout_v, xhat, rstd, g_ref[...])
        dres_ref[...] = ALPHA * du
        dy_ref[...] = ((1.0 + gate_ref[...]) * du).astype(BF16)
        _acc_rows(dg_ref, dout_v * xhat, i)
        _acc_rows(db_ref, dout_v, i)
        _acc_rows(dgate_ref, du * yv, i)
        _acc_rows(sq_ref, err * err, i)

    return pl.pallas_call(
        body, grid=(L // tr,), in_specs=_row_specs(tr, [D, D]) + [_vec_spec(D)] * 3 + _row_specs(tr, [D]),
        out_specs=_row_specs(tr, [D, D]) + [_vec_spec(D)] * 4,
        out_shape=[jax.ShapeDtypeStruct((L, D), F32), jax.ShapeDtypeStruct((L, D), BF16)]
        + [jax.ShapeDtypeStruct((1, D), F32)] * 4,
        compiler_params=_cp(("arbitrary",)), name="ln_final_fwd_bwd")(x, y, gate, g, b, target)


def mod_ln_bwd(dres_in, dh, dskip, xmid, scale, x, y, gate, g):
    L, D = x.shape
    tr = _tile(L, 256, 16)

    def body(dres_ref, dh_ref, dskip_ref, xm_ref, sc_ref, x_ref, y_ref, gate_ref, g_ref,
             dres_out, dy_ref, dg_ref, db_ref, dgate_ref, dsc_ref, dsh_ref):
        i = pl.program_id(0)
        dh_v = dh_ref[...].astype(F32)
        dout_v = dres_ref[...] + dskip_ref[...].astype(F32) + dh_v * (1.0 + sc_ref[...])
        _acc_rows(dsc_ref, dh_v * xm_ref[...], i)
        _acc_rows(dsh_ref, dh_v, i)
        yv = y_ref[...]
        _, xhat, rstd = _ln_core(x_ref[...], yv, gate_ref[...], g_ref[...], 0.0)
        du = _ln_bwd_rows(dout_v, xhat, rstd, g_ref[...])
        dres_out[...] = ALPHA * du
        dy_ref[...] = ((1.0 + gate_ref[...]) * du).astype(BF16)
        _acc_rows(dg_ref, dout_v * xhat, i)
        _acc_rows(db_ref, dout_v, i)
        _acc_rows(dgate_ref, du * yv, i)

    return pl.pallas_call(
        body, grid=(L // tr,),
        in_specs=_row_specs(tr, [D] * 4) + [_vec_spec(D)] + _row_specs(tr, [D, D]) + [_vec_spec(D)] * 2,
        out_specs=_row_specs(tr, [D, D]) + [_vec_spec(D)] * 5,
        out_shape=[jax.ShapeDtypeStruct((L, D), F32), jax.ShapeDtypeStruct((L, D), BF16)]
        + [jax.ShapeDtypeStruct((1, D), F32)] * 5,
        compiler_params=_cp(("arbitrary",)), name="mod_ln_bwd")(dres_in, dh, dskip, xmid, scale, x, y, gate, g)


def ln_bwd(dout, x, y, gate, g, name):
    L, D = x.shape
    tr = _tile(L, 256, 16)

    def body(do_ref, x_ref, y_ref, gate_ref, g_ref, dres_ref, dy_ref, dg_ref, db_ref, dgate_ref):
        i = pl.program_id(0)
        yv = y_ref[...]
        dout_v = do_ref[...]
        _, xhat, rstd = _ln_core(x_ref[...], yv, gate_ref[...], g_ref[...], 0.0)
        dxh = dout_v * g_ref[...]
        m1 = jnp.mean(dxh, axis=-1, keepdims=True)
        m2 = jnp.mean(dxh * xhat, axis=-1, keepdims=True)
        du = rstd * (dxh - m1 - xhat * m2)
        dres_ref[...] = ALPHA * du
        dy_ref[...] = ((1.0 + gate_ref[...]) * du).astype(BF16)
        _acc_rows(dg_ref, dout_v * xhat, i)
        _acc_rows(db_ref, dout_v, i)
        _acc_rows(dgate_ref, du * yv, i)

    return pl.pallas_call(
        body, grid=(L // tr,), in_specs=_row_specs(tr, [D, D, D]) + [_vec_spec(D)] * 2,
        out_specs=_row_specs(tr, [D, D]) + [_vec_spec(D)] * 3,
        out_shape=[jax.ShapeDtypeStruct((L, D), F32), jax.ShapeDtypeStruct((L, D), BF16)]
        + [jax.ShapeDtypeStruct((1, D), F32)] * 3,
        compiler_params=_cp(("arbitrary",)), name=name)(dout, x, y, gate, g)


def mod_bwd(dres, dh, dh2, xin, scale, name, through_mod):
    L, D = xin.shape
    tr = _tile(L, 256, 16)

    def body(dres_ref, dh_ref, dh2_ref, x_ref, sc_ref, dx_ref, dsc_ref, dsh_ref):
        i = pl.program_id(0)
        dh_v = dh_ref[...].astype(F32)
        tot = dres_ref[...]
        if through_mod:
            dh_v = dh_v + dh2_ref[...].astype(F32)
        else:
            tot = tot + dh2_ref[...].astype(F32)
        dx_ref[...] = tot + dh_v * (1.0 + sc_ref[...])
        _acc_rows(dsc_ref, dh_v * x_ref[...], i)
        _acc_rows(dsh_ref, dh_v, i)

    return pl.pallas_call(
        body, grid=(L // tr,), in_specs=_row_specs(tr, [D, D, D, D]) + [_vec_spec(D)],
        out_specs=_row_specs(tr, [D]) + [_vec_spec(D)] * 2,
        out_shape=[jax.ShapeDtypeStruct((L, D), F32)] + [jax.ShapeDtypeStruct((1, D), F32)] * 2,
        compiler_params=_cp(("arbitrary",)), name=name)(dres, dh, dh2, xin, scale)


CONV_HALO = 16


def _conv_rows(x_ref, i, tr, L):
    nblk = L // tr
    s = pl.multiple_of(i * tr, CONV_HALO)
    cur = x_ref[pl.ds(s, tr), :].astype(F32)
    sp = pl.multiple_of(jnp.maximum(i * tr - CONV_HALO, 0), CONV_HALO)
    sn = pl.multiple_of(jnp.minimum(i * tr + tr, L - CONV_HALO), CONV_HALO)
    prev = x_ref[pl.ds(sp, CONV_HALO), :].astype(F32) * (i > 0).astype(F32)
    nxt = x_ref[pl.ds(sn, CONV_HALO), :].astype(F32) * (i < nblk - 1).astype(F32)
    return jnp.concatenate([prev, cur, nxt], axis=0)


def _shift_rows(v, j):
    n = v.shape[0]
    return v if j % n == 0 else pltpu.roll(v, j % n, 0)


def _conv_taps(xe):
    return [_shift_rows(xe, CONV_W - 1 - k) for k in range(CONV_W)]


def _conv_eval(taps, w_ref, b_ref):
    c = b_ref[...] + w_ref[0:1, :] * taps[0]
    for k in range(1, CONV_W):
        c = c + w_ref[k:k + 1, :] * taps[k]
    return c


def conv_fwd(zx, col0, conv_w, conv_b):
    L = zx.shape[0]
    C = conv_w.shape[1]
    tc = _tile(C, 512)
    tr = _tile(L, 512, CONV_HALO)
    off = col0 // tc

    def body(x_ref, w_ref, b_ref, o_ref):
        i = pl.program_id(1)
        xe = _conv_rows(x_ref, i, tr, L)
        c = _conv_eval(_conv_taps(xe), w_ref, b_ref)[CONV_HALO:CONV_HALO + tr]
        o_ref[...] = _silu(c).astype(BF16)

    return pl.pallas_call(
        body, grid=(C // tc, L // tr),
        in_specs=[pl.BlockSpec((L, tc), lambda j, i: (0, off + j)), pl.BlockSpec((CONV_W, tc), lambda j, i: (0, j)),
                  pl.BlockSpec((1, tc), lambda j, i: (0, j))],
        out_specs=pl.BlockSpec((tr, tc), lambda j, i: (i, j)),
        out_shape=jax.ShapeDtypeStruct((L, C), BF16), compiler_params=_cp(("parallel", "arbitrary")),
        name="conv_fwd")(zx, conv_w, conv_b)


def conv_bwd(zx, col0, conv_w, conv_b, dxbc):
    L = zx.shape[0]
    C = conv_w.shape[1]
    tc = _tile(C, 512)
    tr = _tile(L, 512, CONV_HALO)
    off = col0 // tc
    H = CONV_HALO

    def body(x_ref, g_ref, w_ref, b_ref, dx_ref, dw_ref, db_ref):
        i = pl.program_id(1)
        xe = _conv_rows(x_ref, i, tr, L)
        ge = _conv_rows(g_ref, i, tr, L)
        taps = _conv_taps(xe)
        dc = ge * _dsilu(_conv_eval(taps, w_ref, b_ref))
        dx = w_ref[CONV_W - 1:CONV_W, :] * dc
        for k in range(CONV_W - 1):
            dx = dx + w_ref[k:k + 1, :] * _shift_rows(dc, -(CONV_W - 1 - k))
        dx_ref[...] = dx[H:H + tr].astype(BF16)
        dcc = dc[H:H + tr]
        rows = [jnp.sum(dcc * taps[k][H:H + tr], axis=0, keepdims=True) for k in range(CONV_W)]
        dwv = jnp.concatenate(rows + [jnp.zeros((8 - CONV_W, tc), F32)], axis=0)
        dbv = jnp.sum(dcc, axis=0, keepdims=True)

        @pl.when(i == 0)
        def _():
            dw_ref[...] = dwv
            db_ref[...] = dbv

        @pl.when(i > 0)
        def _():
            dw_ref[...] += dwv
            db_ref[...] += dbv

    dx, dw, db = pl.pallas_call(
        body, grid=(C // tc, L // tr),
        in_specs=[pl.BlockSpec((L, tc), lambda j, i: (0, off + j)), pl.BlockSpec((L, tc), lambda j, i: (0, j)),
                  pl.BlockSpec((CONV_W, tc), lambda j, i: (0, j)), pl.BlockSpec((1, tc), lambda j, i: (0, j))],
        out_specs=[pl.BlockSpec((tr, tc), lambda j, i: (i, j)), pl.BlockSpec((8, tc), lambda j, i: (0, j)),
                   pl.BlockSpec((1, tc), lambda j, i: (0, j))],
        out_shape=[jax.ShapeDtypeStruct((L, C), BF16), jax.ShapeDtypeStruct((8, C), F32),
                   jax.ShapeDtypeStruct((1, C), F32)],
        compiler_params=_cp(("parallel", "arbitrary")), name="conv_bwd")(zx, dxbc, conv_w, conv_b)
    return dx, dw[:CONV_W], db


_NN = (((1,), (0,)), ((), ()))


def _pieces(x, n):
    out, r = [], x
    for _ in range(n):
        p = r.astype(BF16)
        out.append(p)
        r = r - p.astype(F32)
    return out


def _dot01(a, b01, n, dims=_NN):
    b = b01.astype(BF16)
    return functools.reduce(lambda u, v: u + v,
                            [lax.dot_general(p, b, dims, preferred_element_type=F32) for p in _pieces(a, n)])


def _dot01_left(a01, b, n, dims=_NN):
    a = a01.astype(BF16)
    return functools.reduce(lambda u, v: u + v,
                            [lax.dot_general(a, p, dims, preferred_element_type=F32) for p in _pieces(b, n)])


def _ssd_common(dtp_ref, dtpT_ref, bias_ref, biasT_ref, alog_ref, alogT_ref, b_ref, c_ref):
    Q = SSD_Q
    dt = _softplus(dtp_ref[...] + bias_ref[...])
    A = -jnp.exp(alog_ref[...])
    row = lax.broadcasted_iota(jnp.int32, (Q, Q), 0)
    col = lax.broadcasted_iota(jnp.int32, (Q, Q), 1)
    causal = row >= col
    tril = causal.astype(F32)
    Kh = dt.shape[1]
    acum = _dot01_left(tril, dt * A, 3)
    eye = (lax.broadcasted_iota(jnp.int32, (Kh, Kh), 0) == lax.broadcasted_iota(jnp.int32, (Kh, Kh), 1)).astype(F32)
    acumT = _dot01_left(eye, acum, 3, dims=(((1,), (1,)), ((), ())))
    Bm = b_ref[...]
    Cm = c_ref[...]
    cb = lax.dot_general(Cm, Bm, (((1,), (1,)), ((), ())), preferred_element_type=F32)
    return dt, A, causal, row, col, acum, acumT, Bm, Cm, cb


def _ssd_in_specs(Q, GP, N, Kh, DI, cmap):
    nb0 = DI // N
    vec = pl.BlockSpec((None, 1, Kh), lambda g, c: (g, 0, 0))
    vecT = pl.BlockSpec((None, Kh, 1), lambda g, c: (g, 0, 0))
    return [pl.BlockSpec((Q, GP), lambda g, c: (cmap(c), g)),
            pl.BlockSpec((Q, N), lambda g, c: (cmap(c), nb0 + g)),
            pl.BlockSpec((Q, N), lambda g, c: (cmap(c), nb0 + SSD_G + g)),
            pl.BlockSpec((None, Q, Kh), lambda g, c: (g, cmap(c), 0)),
            pl.BlockSpec((None, Kh, Q), lambda g, c: (g, 0, cmap(c))),
            vec, vecT, vec, vecT, vec, vecT]


def _hi(a, b01):
    return _dot01(a, b01, 2)


def _headsum(a, b01):
    return _dot01(a, b01, 1)


def _ssd_heads(dskT_ref, acum, acumT, dt, Kh):
    Q, P, N = SSD_Q, SSD_P, SSD_N
    GP = Kh * P
    sh_p = P.bit_length() - 1
    seg = lambda shape, dim: lax.shift_right_logical(lax.broadcasted_iota(jnp.int32, shape, dim), sh_p)
    E = (seg((Kh, GP), 1) == lax.broadcasted_iota(jnp.int32, (Kh, GP), 0)).astype(F32)
    ET = (seg((GP, Kh), 0) == lax.broadcasted_iota(jnp.int32, (GP, Kh), 1)).astype(F32)
    a_last = acum[Q - 1:Q, :]
    tail = jnp.exp(a_last - acum)
    eLT = jnp.exp(acumT[:, Q - 1:Q])
    rowseg = seg((GP, N), 0)
    eL_b = jnp.zeros((GP, N), F32)
    for k in range(Kh):
        eL_b = jnp.where(rowseg == k, eLT[k:k + 1, :], eL_b)
    return dict(
        E=E, ET=ET, a_last=a_last, tail=tail, eL_b=eL_b,
        dt_all=_hi(dt, E), ea_all=_hi(jnp.exp(acum), E), tail_all=_hi(tail, E),
        dsk_all=jnp.sum(E * dskT_ref[...], axis=0, keepdims=True))


def _head_chunks(GP):
    CW = min(GP, 128)
    return CW, CW // SSD_P, GP // CW


def _head_mask(Q, CW, kk):
    lane = lax.broadcasted_iota(jnp.int32, (Q, CW), 1)
    return jnp.logical_and(lane >= kk * SSD_P, lane < (kk + 1) * SSD_P)


def ssd_fwd(xbc, dtp_g, dtp_gT, bias_g, bias_gT, alog_g, alog_gT, dsk_g, dsk_gT, DI):
    L = xbc.shape[0]
    Q, P, N, G = SSD_Q, SSD_P, SSD_N, SSD_G
    GP = DI // G
    Kh = GP // P
    nc = L // Q

    CW, hpc, nch = _head_chunks(GP)
    nt = (((1,), (1,)), ((), ()))
    tn = (((0,), (0,)), ((), ()))

    def body(xs_ref, b_ref, c_ref, dtp_ref, dtpT_ref, bias_ref, biasT_ref, alog_ref, alogT_ref, dsk_ref, dskT_ref,
             y_ref, st_ref, state):
        @pl.when(pl.program_id(1) == 0)
        def _():
            state[...] = jnp.zeros(state.shape, F32)

        st_ref[...] = state[...]
        dt, A, causal, row, col, acum, acumT, Bm, Cm, cb = _ssd_common(
            dtp_ref, dtpT_ref, bias_ref, biasT_ref, alog_ref, alogT_ref, b_ref, c_ref)
        hd = _ssd_heads(dskT_ref, acum, acumT, dt, Kh)
        xs = xs_ref[...].astype(F32)
        xdt_all = xs * hd["dt_all"]
        S_all = state[...]
        y_all = (lax.dot_general(Cm, S_all.astype(BF16), nt, preferred_element_type=F32) * hd["ea_all"]
                 + xs * hd["dsk_all"])
        state[...] = S_all * hd["eL_b"] + lax.dot_general(
            (xdt_all * hd["tail_all"]).astype(BF16), Bm, tn, preferred_element_type=F32)
        for ch in range(nch):
            cs = slice(ch * CW, (ch + 1) * CW)
            xc = xdt_all[:, cs]
            acc = y_all[:, cs]
            for kk in range(hpc):
                k = ch * hpc + kk
                decay = jnp.exp(jnp.where(causal, acum[:, k:k + 1] - acumT[k:k + 1, :], -jnp.inf))
                xk = xc if hpc == 1 else jnp.where(_head_mask(Q, CW, kk), xc, 0.0)
                acc = acc + jnp.dot((cb * decay).astype(BF16), xk.astype(BF16), preferred_element_type=F32)
            y_ref[:, cs] = acc.astype(BF16)

    return pl.pallas_call(
        body, grid=(G, nc), in_specs=_ssd_in_specs(Q, GP, N, Kh, DI, lambda c: c),
        out_specs=[pl.BlockSpec((Q, GP), lambda g, c: (c, g)),
                   pl.BlockSpec((None, None, GP, N), lambda g, c: (c, g, 0, 0))],
        out_shape=[jax.ShapeDtypeStruct((L, DI), BF16), jax.ShapeDtypeStruct((nc, G, GP, N), F32)],
        scratch_shapes=[pltpu.VMEM((GP, N), F32)], compiler_params=_cp(("parallel", "arbitrary")),
        name="ssd_fwd")(xbc, xbc, xbc, dtp_g, dtp_gT, bias_g, bias_gT, alog_g, alog_gT, dsk_g, dsk_gT)


def ssd_bwd(xbc, dtp_g, dtp_gT, bias_g, bias_gT, alog_g, alog_gT, dsk_g, dsk_gT, states, dy, DI):
    L = xbc.shape[0]
    Q, P, N, G = SSD_Q, SSD_P, SSD_N, SSD_G
    GP = DI // G
    Kh = GP // P
    nc = L // Q
    rev = lambda c: nc - 1 - c

    CW, hpc, nch = _head_chunks(GP)

    def body(xs_ref, b_ref, c_ref, dtp_ref, dtpT_ref, bias_ref, biasT_ref, alog_ref, alogT_ref, dsk_ref, dskT_ref,
             st_ref, dy_ref, dxs_ref, dB_ref, dC_ref, ddtp_ref, dbias_ref, dalog_ref, dD_ref, dstate):
        ci = pl.program_id(1)

        @pl.when(ci == 0)
        def _():
            dstate[...] = jnp.zeros(dstate.shape, F32)

        dt, A, causal, row, col, acum, acumT, Bm, Cm, cb = _ssd_common(
            dtp_ref, dtpT_ref, bias_ref, biasT_ref, alog_ref, alogT_ref, b_ref, c_ref)
        tn = (((0,), (0,)), ((), ()))
        nt = (((1,), (1,)), ((), ()))
        hd = _ssd_heads(dskT_ref, acum, acumT, dt, Kh)
        ET, tail = hd["ET"], hd["tail"]
        cbT = lax.dot_general(Bm, Cm, nt, preferred_element_type=F32)
        causalT = row <= col
        xs = xs_ref[...].astype(F32)
        xdt_all = xs * hd["dt_all"]
        dyb = dy_ref[...]
        dy_all = dyb.astype(F32)
        S_all = st_ref[...]
        S_b = S_all.astype(BF16)
        dS_all = dstate[...]
        dS_b = dS_all.astype(BF16)
        CS_all = lax.dot_general(Cm, S_b, nt, preferred_element_type=F32)
        dyE_b = (dy_all * hd["ea_all"]).astype(BF16)
        dC_acc = jnp.dot(dyE_b, S_b, preferred_element_type=F32)
        dS_y = lax.dot_general(dyE_b, Cm, tn, preferred_element_type=F32)
        BdS_all = lax.dot_general(Bm, dS_b, nt, preferred_element_type=F32)
        dB_acc = jnp.dot((xdt_all * hd["tail_all"]).astype(BF16), dS_b, preferred_element_type=F32)
        dtail = _headsum(xdt_all * BdS_all, ET)
        da_cols = _headsum(dy_all * CS_all * hd["ea_all"], ET) - dtail * tail
        dss = _dot01_left(jnp.ones((8, N), F32), _dot01_left(hd["E"], dS_all * S_all, 2), 2, dims=nt)
        da_last = dss[0:1] * jnp.exp(hd["a_last"]) + jnp.sum(dtail * tail, axis=0, keepdims=True)
        rowi = lax.broadcasted_iota(jnp.int32, (Q, Kh), 0)
        da_cols = da_cols + jnp.where(rowi == Q - 1, da_last, 0.0)
        dstate[...] = hd["eL_b"] * dS_all + dS_y
        sum_mg = jnp.zeros((Q, Q), F32)
        ddt_x = jnp.zeros((Q, Kh), F32)
        da_rows = jnp.zeros((Kh, Q), F32)
        lane_k = lax.broadcasted_iota(jnp.int32, (Q, Kh), 1)
        sub_k = lax.broadcasted_iota(jnp.int32, (Kh, Q), 0)
        for ch in range(nch):
            cs = slice(ch * CW, (ch + 1) * CW)
            dyc = dyb[:, cs]
            xc_b = xdt_all[:, cs].astype(BF16)
            acc = hd["tail_all"][:, cs] * BdS_all[:, cs]
            for kk in range(hpc):
                k = ch * hpc + kk
                a_b = jnp.broadcast_to(acum[:, k:k + 1], (Q, Q))
                a_r = acumT[k:k + 1, :]
                decay = jnp.exp(jnp.where(causal, a_b - a_r, -jnp.inf))
                decayT = jnp.exp(jnp.where(causalT, a_r - a_b, -jnp.inf))
                dyk = dyc if hpc == 1 else jnp.where(_head_mask(Q, CW, kk), dyc, jnp.zeros_like(dyc))
                mg = decay * lax.dot_general(dyk, xc_b, nt, preferred_element_type=F32)
                sum_mg = sum_mg + mg
                w = mg * cb
                da_cols = da_cols + jnp.where(lane_k == k, jnp.sum(w, axis=1, keepdims=True), 0.0)
                da_rows = da_rows + jnp.where(sub_k == k, jnp.sum(w, axis=0, keepdims=True), 0.0)
                acc = acc + jnp.dot((decayT * cbT).astype(BF16), dyk, preferred_element_type=F32)
            dxs_ref[:, cs] = (acc * hd["dt_all"][:, cs] + dy_all[:, cs] * hd["dsk_all"][:, cs]).astype(BF16)
            ddt_x = ddt_x + _headsum(acc * xs[:, cs], ET[cs, :])
        eye_q = (row == col).astype(F32)
        da_cols = da_cols - _dot01_left(eye_q, da_rows, 3, dims=nt)
        dD_row = jnp.sum(_headsum(dy_all * xs, ET), axis=0, keepdims=True)
        sum_mg_b = sum_mg.astype(BF16)
        dB_ref[...] = (dB_acc + lax.dot_general(sum_mg_b, Cm, tn, preferred_element_type=F32)).astype(BF16)
        dC_ref[...] = (dC_acc + jnp.dot(sum_mg_b, Bm, preferred_element_type=F32)).astype(BF16)
        triu = (row <= col).astype(F32)
        ddtA = _dot01_left(triu, da_cols, 3)
        ddt = ddt_x + ddtA * A
        dpre = ddt * _sigmoid(dtp_ref[...] + bias_ref[...])
        ddtp_ref[...] = dpre
        dbias_v = jnp.sum(dpre, axis=0, keepdims=True)
        dalog_v = jnp.sum(ddtA * dt, axis=0, keepdims=True) * A

        @pl.when(ci == 0)
        def _():
            dbias_ref[...] = dbias_v
            dalog_ref[...] = dalog_v
            dD_ref[...] = dD_row

        @pl.when(ci > 0)
        def _():
            dbias_ref[...] += dbias_v
            dalog_ref[...] += dalog_v
            dD_ref[...] += dD_row

    vec_o = pl.BlockSpec((None, 1, Kh), lambda g, c: (g, 0, 0))
    return pl.pallas_call(
        body, grid=(G, nc),
        in_specs=_ssd_in_specs(Q, GP, N, Kh, DI, rev)
        + [pl.BlockSpec((None, None, GP, N), lambda g, c: (rev(c), g, 0, 0)),
           pl.BlockSpec((Q, GP), lambda g, c: (rev(c), g))],
        out_specs=[pl.BlockSpec((Q, GP), lambda g, c: (rev(c), g)), pl.BlockSpec((Q, N), lambda g, c: (rev(c), g)),
                   pl.BlockSpec((Q, N), lambda g, c: (rev(c), g)),
                   pl.BlockSpec((None, Q, Kh), lambda g, c: (g, rev(c), 0)), vec_o, vec_o, vec_o],
        out_shape=[jax.ShapeDtypeStruct((L, DI), BF16), jax.ShapeDtypeStruct((L, G * N), BF16),
                   jax.ShapeDtypeStruct((L, G * N), BF16), jax.ShapeDtypeStruct((G, L, Kh), F32)]
        + [jax.ShapeDtypeStruct((G, 1, Kh), F32)] * 3,
        scratch_shapes=[pltpu.VMEM((GP, N), F32)], compiler_params=_cp(("parallel", "arbitrary")),
        name="ssd_bwd")(xbc, xbc, xbc, dtp_g, dtp_gT, bias_g, bias_gT, alog_g, alog_gT, dsk_g, dsk_gT, states, dy)


def _rms_groups(y2, ng_ref, DI):
    S = DI // SSD_G
    for g in range(SSD_G):
        gs = slice(g * S, (g + 1) * S)
        seg = y2[:, gs]
        r = lax.rsqrt(jnp.mean(seg * seg, axis=-1, keepdims=True) + RMS_EPS)
        yield gs, seg * r, r, ng_ref[:, gs]


def rms_gate_fwd(y, zx, norm_g):
    L, DI = y.shape
    tr = _tile(L, 256, 16)

    def body(y_ref, z_ref, ng_ref, o_ref):
        y2 = y_ref[...].astype(F32) * _silu(z_ref[...].astype(F32))
        for gs, yh, _, ng in _rms_groups(y2, ng_ref, DI):
            o_ref[:, gs] = (yh * ng).astype(BF16)

    return pl.pallas_call(
        body, grid=(L // tr,), in_specs=_row_specs(tr, [DI, DI]) + [_vec_spec(DI)], out_specs=_row_specs(tr, [DI])[0],
        out_shape=jax.ShapeDtypeStruct((L, DI), BF16), compiler_params=_cp(("parallel",)),
        name="rms_gate_fwd")(y, zx, norm_g)


def rms_gate_bwd(dyn, y, zx, norm_g):
    L, DI = y.shape
    tr = _tile(L, 256, 16)

    def body(dyn_ref, y_ref, z_ref, ng_ref, dy_ref, dz_ref, dng_ref):
        i = pl.program_id(0)
        yv = y_ref[...].astype(F32)
        zv = z_ref[...].astype(F32)
        sz = _silu(zv)
        dsz = _dsilu(zv)
        dynv = dyn_ref[...].astype(F32)
        for gs, yh, r, ng in _rms_groups(yv * sz, ng_ref, DI):
            dyh = dynv[:, gs] * ng
            dy2 = r * (dyh - yh * jnp.mean(dyh * yh, axis=-1, keepdims=True))
            dy_ref[:, gs] = (dy2 * sz[:, gs]).astype(BF16)
            dz_ref[:, gs] = (dy2 * yv[:, gs] * dsz[:, gs]).astype(BF16)
            s = jnp.sum(dynv[:, gs] * yh, axis=0, keepdims=True)

            @pl.when(i == 0)
            def _():
                dng_ref[:, gs] = s

            @pl.when(i > 0)
            def _():
                dng_ref[:, gs] += s

    return pl.pallas_call(
        body, grid=(L // tr,), in_specs=_row_specs(tr, [DI, DI, DI]) + [_vec_spec(DI)],
        out_specs=_row_specs(tr, [DI, DI]) + [_vec_spec(DI)],
        out_shape=[jax.ShapeDtypeStruct((L, DI), BF16)] * 2 + [jax.ShapeDtypeStruct((1, DI), F32)],
        compiler_params=_cp(("arbitrary",)), name="rms_gate_bwd")(dyn, y, zx, norm_g)


def _alibi_slope(gi, h):
    n = len(DIL_PATTERNS) * DIL_H
    return float(2.0 ** (-8.0 * (gi * DIL_H + h + 1) / n))


def _attn_masks():
    qi = lax.broadcasted_iota(jnp.int32, (DIL_BLK, DIL_BLK), 0)
    kj = lax.broadcasted_iota(jnp.int32, (DIL_BLK, DIL_BLK), 1)
    dcur = (qi - kj).astype(F32)
    return dcur, qi >= kj, dcur + float(DIL_BLK), kj >= qi


def attn_fwd(q3, kv3, gi):
    window, d = DIL_PATTERNS[gi]
    assert window // d == DIL_BLK
    HW = DIL_H * DIL_E
    M = q3.shape[1]
    nb = M // DIL_BLK
    scale = DIL_E ** -0.5
    nt = (((1,), (1,)), ((), ()))

    def body(q_ref, kp_ref, kc_ref, vp_ref, vc_ref, o_ref, lse_ref):
        n = pl.program_id(1)
        dcur, vcur, dprev, vprev0 = _attn_masks()
        dist = jnp.concatenate([dprev, dcur], axis=1)
        valid = jnp.concatenate([jnp.logical_and(vprev0, n > 0), vcur], axis=1)
        lane = lax.broadcasted_iota(jnp.int32, (DIL_BLK, 128), 1)
        lse_acc = jnp.zeros((DIL_BLK, 128), F32)
        for h in range(DIL_H):
            hs = slice(h * DIL_E, (h + 1) * DIL_E)
            sl = _alibi_slope(gi, h) * d
            kcat = jnp.concatenate([kp_ref[:, hs], kc_ref[:, hs]], axis=0)
            vcat = jnp.concatenate([vp_ref[:, hs], vc_ref[:, hs]], axis=0)
            s = lax.dot_general(q_ref[:, hs], kcat, nt, preferred_element_type=F32) * scale - sl * dist
            s = jnp.where(valid, s, -jnp.inf)
            m = jnp.max(s, axis=-1, keepdims=True)
            p = jnp.exp(s - m)
            den = jnp.sum(p, axis=-1, keepdims=True)
            o = jnp.dot(p.astype(BF16), vcat, preferred_element_type=F32) / den
            o_ref[:, hs] = o.astype(BF16)
            lse_acc = jnp.where(lane == h, m + jnp.log(den), lse_acc)
        lse_ref[...] = lse_acc

    blk = (None, DIL_BLK, HW)
    prev = lambda n: jnp.maximum(n - 1, 0)
    return pl.pallas_call(
        body, grid=(d, nb),
        in_specs=[pl.BlockSpec(blk, lambda r, n: (r, n, 0)),
                  pl.BlockSpec(blk, lambda r, n: (r, prev(n), 0)), pl.BlockSpec(blk, lambda r, n: (r, n, 0)),
                  pl.BlockSpec(blk, lambda r, n: (r, prev(n), 1)), pl.BlockSpec(blk, lambda r, n: (r, n, 1))],
        out_specs=[pl.BlockSpec(blk, lambda r, n: (r, n, 0)), pl.BlockSpec((None, DIL_BLK, 128), lambda r, n: (r, n, 0))],
        out_shape=[jax.ShapeDtypeStruct((d, M, HW), BF16), jax.ShapeDtypeStruct((d, M, 128), F32)],
        compiler_params=_cp(("parallel", "parallel")), name=f"attn_fwd_{gi}")(q3, kv3, kv3, kv3, kv3)


def attn_bwd(q3, kv3, do3, lse3, dpr3, gi):
    window, d = DIL_PATTERNS[gi]
    HW = DIL_H * DIL_E
    M = q3.shape[1]
    L = M * d
    nb = M // DIL_BLK
    scale = DIL_E ** -0.5
    nt = (((1,), (1,)), ((), ()))
    tn = (((0,), (0,)), ((), ()))

    def body(q0_ref, q1_ref, k_ref, v_ref, do0_ref, do1_ref, l0_ref, l1_ref, r0_ref, r1_ref,
             dq_ref, dk_ref, dv_ref, carry):
        n = pl.program_id(1)

        @pl.when(n == 0)
        def _():
            carry[...] = jnp.zeros(carry.shape, F32)

        dcur, vcur, dprev, vprev0 = _attn_masks()
        dist = jnp.concatenate([dcur, dprev], axis=0)
        valid = jnp.concatenate([vcur, jnp.logical_and(vprev0, n < nb - 1)], axis=0)
        B = DIL_BLK
        for h in range(DIL_H):
            hs = slice(h * DIL_E, (h + 1) * DIL_E)
            sl = _alibi_slope(gi, h) * d
            kh = k_ref[:, hs]
            vh = v_ref[:, hs]
            qcat = jnp.concatenate([q0_ref[:, hs], q1_ref[:, hs]], axis=0)
            docat = jnp.concatenate([do0_ref[:, hs], do1_ref[:, hs]], axis=0)
            lcat = jnp.concatenate([l0_ref[:, h:h + 1], l1_ref[:, h:h + 1]], axis=0)
            rcat = jnp.concatenate([r0_ref[:, h:h + 1], r1_ref[:, h:h + 1]], axis=0)
            s = lax.dot_general(qcat, kh, nt, preferred_element_type=F32) * scale - sl * dist
            p = jnp.exp(jnp.where(valid, s - lcat, -jnp.inf))
            ds = p * (lax.dot_general(docat, vh, nt, preferred_element_type=F32) - rcat)
            ds_b = (ds * scale).astype(BF16)
            dv_ref[:, hs] = lax.dot_general(p.astype(BF16), docat, tn, preferred_element_type=F32).astype(BF16)
            dk_ref[:, hs] = lax.dot_general(ds_b, qcat, tn, preferred_element_type=F32).astype(BF16)
            dqc = jnp.dot(ds_b, kh, preferred_element_type=F32)
            dq_ref[:, hs] = (carry[:, hs] + dqc[:B]).astype(BF16)
            carry[:, hs] = dqc[B:]

    blk = (None, DIL_BLK, HW)
    sblk = (None, DIL_BLK, 128)
    oblk = (DIL_BLK, HW)
    nxt = lambda n: jnp.minimum(n + 1, nb - 1)
    here = lambda c: (lambda r, n: (r, n, c))
    ahead = lambda c: (lambda r, n: (r, nxt(n), c))
    outs = pl.pallas_call(
        body, grid=(d, nb),
        in_specs=[pl.BlockSpec(blk, here(0)), pl.BlockSpec(blk, ahead(0)),
                  pl.BlockSpec(blk, here(0)), pl.BlockSpec(blk, here(1)),
                  pl.BlockSpec(blk, here(0)), pl.BlockSpec(blk, ahead(0)),
                  pl.BlockSpec(sblk, here(0)), pl.BlockSpec(sblk, ahead(0)),
                  pl.BlockSpec(sblk, here(0)), pl.BlockSpec(sblk, ahead(0))],
        out_specs=[pl.BlockSpec(oblk, lambda r, n: (n, r))] * 3,
        out_shape=[jax.ShapeDtypeStruct((M, d * HW), BF16)] * 3,
        scratch_shapes=[pltpu.VMEM(oblk, F32)], compiler_params=_cp(("parallel", "arbitrary")),
        name=f"attn_bwd_{gi}")(q3, q3, kv3, kv3, do3, do3, lse3, lse3, dpr3, dpr3)
    return [t.reshape(L, HW) for t in outs]


def _merge_weights(l_tiles, h):
    ls = [t[:, h:h + 1] for t in l_tiles]
    mx = functools.reduce(jnp.maximum, ls)
    es = [jnp.exp(l - mx) for l in ls]
    den = functools.reduce(lambda a, b: a + b, es)
    return [e / den for e in es]


def _dil_specs(tr, arrs):
    return [pl.BlockSpec((a.shape[0], tr // a.shape[0], a.shape[2]), lambda i: (0, i, 0)) for a in arrs]


def _dil_scratch(tr, arrs):
    return [pltpu.VMEM((a.shape[2] // 128, tr, 128), F32) for a in arrs if a.shape[0] > 1]


def _undilate(refs3, scrs, tr):
    out, k = [], 0
    for ref in refs3:
        d, _, W = ref.shape
        if d == 1:
            out.append(lambda c, ref=ref: ref[0, :, c * 128:(c + 1) * 128])
            continue
        scr = scrs[k]
        k += 1
        for r in range(d):
            for c in range(W // 128):
                scr.at[c][pl.ds(r, tr // d, stride=d), :] = ref[r, :, c * 128:(c + 1) * 128].astype(F32)
        out.append(lambda c, scr=scr: scr[c])
    return out


def merge_fwd(os3, lses3, z):
    HW = os3[0].shape[2]
    L = os3[0].shape[0] * os3[0].shape[1]
    tr = _tile(L, 256, 16)
    ng = len(os3)
    n_scr = len(_dil_scratch(tr, os3))

    def body(*refs):
        z_ref, out_ref = refs[2 * ng], refs[2 * ng + 1]
        scrs = refs[2 * ng + 2:]
        o_get = _undilate(refs[:ng], scrs[:n_scr], tr)
        l_tiles = [g(0) for g in _undilate(refs[ng:2 * ng], scrs[n_scr:], tr)]
        for h in range(DIL_H):
            hs = slice(h * DIL_E, (h + 1) * DIL_E)
            ws = _merge_weights(l_tiles, h)
            om = functools.reduce(lambda a, b: a + b, [w * o(h).astype(F32) for w, o in zip(ws, o_get)])
            out_ref[:, hs] = (om * _silu(z_ref[:, hs].astype(F32))).astype(BF16)

    return pl.pallas_call(
        body, grid=(L // tr,),
        in_specs=_dil_specs(tr, os3) + _dil_specs(tr, lses3) + _row_specs(tr, [HW]),
        out_specs=_row_specs(tr, [HW])[0], out_shape=jax.ShapeDtypeStruct((L, HW), BF16),
        scratch_shapes=_dil_scratch(tr, os3) + _dil_scratch(tr, lses3),
        compiler_params=_cp(("parallel",)), name="merge_fwd")(*os3, *lses3, z)


def merge_bwd(dgated, os3, lses3, z):
    HW = os3[0].shape[2]
    L = os3[0].shape[0] * os3[0].shape[1]
    tr = _tile(L, 256, 16)
    ng = len(os3)
    n_scr = len(_dil_scratch(tr, os3))

    def body(*refs):
        dg_ref = refs[0]
        z_ref = refs[1 + 2 * ng]
        outs = refs[2 + 2 * ng:2 + 2 * ng + 2 * ng + 1]
        scrs = refs[2 + 2 * ng + 2 * ng + 1:]
        do_out, dpr_out, dz_ref = outs[:ng], outs[ng:2 * ng], outs[2 * ng]
        o_get = _undilate(refs[1:1 + ng], scrs[:n_scr], tr)
        l_tiles = [g(0) for g in _undilate(refs[1 + ng:1 + 2 * ng], scrs[n_scr:2 * n_scr], tr)]
        stage = scrs[2 * n_scr:]
        do_stage, dpr_stage, k = [], [], 0
        for g in range(ng):
            if do_out[g].shape[0] == 1:
                do_stage.append(None)
                dpr_stage.append(None)
            else:
                do_stage.append(stage[2 * k])
                dpr_stage.append(stage[2 * k + 1])
                k += 1
        lane = lax.broadcasted_iota(jnp.int32, (tr, 128), 1)
        accs = [jnp.zeros((tr, 128), F32) for _ in range(ng)]
        for h in range(DIL_H):
            hs = slice(h * DIL_E, (h + 1) * DIL_E)
            ws = _merge_weights(l_tiles, h)
            ov = [o(h).astype(F32) for o in o_get]
            om = functools.reduce(lambda a, b: a + b, [w * o for w, o in zip(ws, ov)])
            zv = z_ref[:, hs].astype(F32)
            dgv = dg_ref[:, hs].astype(F32)
            dom = dgv * _silu(zv)
            dz_ref[:, hs] = (dgv * om * _dsilu(zv)).astype(BF16)
            dws = [jnp.sum(dom * o, axis=-1, keepdims=True) for o in ov]
            dwbar = functools.reduce(lambda a, b: a + b, [w * dw for w, dw in zip(ws, dws)])
            for g in range(ng):
                if do_stage[g] is None:
                    do_out[g][0, :, hs] = (ws[g] * dom).astype(BF16)
                else:
                    do_stage[g][h] = ws[g] * dom
                accs[g] = jnp.where(lane == h, ws[g] * dwbar, accs[g])
        for g in range(ng):
            d = do_out[g].shape[0]
            if d == 1:
                dpr_out[g][0] = accs[g]
                continue
            dpr_stage[g][0] = accs[g]
            for r in range(d):
                dpr_out[g][r] = dpr_stage[g].at[0][pl.ds(r, tr // d, stride=d), :]
                for c in range(HW // 128):
                    do_out[g][r, :, c * 128:(c + 1) * 128] = do_stage[g].at[c][pl.ds(r, tr // d, stride=d), :].astype(BF16)

    stage_shapes = []
    for o3 in os3:
        if o3.shape[0] > 1:
            stage_shapes += [pltpu.VMEM((HW // 128, tr, 128), F32), pltpu.VMEM((1, tr, 128), F32)]
    outs = pl.pallas_call(
        body, grid=(L // tr,),
        in_specs=_row_specs(tr, [HW]) + _dil_specs(tr, os3) + _dil_specs(tr, lses3) + _row_specs(tr, [HW]),
        out_specs=_dil_specs(tr, os3) + _dil_specs(tr, lses3) + _row_specs(tr, [HW]),
        out_shape=[jax.ShapeDtypeStruct(o.shape, BF16) for o in os3] + [jax.ShapeDtypeStruct(l.shape, F32) for l in lses3]
        + [jax.ShapeDtypeStruct((L, HW), BF16)],
        scratch_shapes=_dil_scratch(tr, os3) + _dil_scratch(tr, lses3) + stage_shapes,
        compiler_params=_cp(("parallel",)), name="merge_bwd")(dgated, *os3, *lses3, z)
    return outs[:ng], outs[ng:2 * ng], outs[2 * ng]


def ada_fwd(c8, ada_w):
    nl, D, Ws = ada_w.shape
    tn = _tile(Ws, 512)

    def body(c_ref, w_ref, o_ref):
        o_ref[...] = jnp.dot(_silu(c_ref[...]), w_ref[...], precision=lax.Precision.HIGHEST,
                             preferred_element_type=F32)

    return pl.pallas_call(
        body, grid=(nl, Ws // tn),
        in_specs=[pl.BlockSpec((N_DEV, D), lambda l, j: (0, 0)), pl.BlockSpec((None, D, tn), lambda l, j: (l, 0, j))],
        out_specs=pl.BlockSpec((None, N_DEV, tn), lambda l, j: (l, 0, j)),
        out_shape=jax.ShapeDtypeStruct((nl, N_DEV, Ws), F32), compiler_params=_cp(("parallel", "parallel")),
        name="ada_fwd")(c8, ada_w)


def ada_wgrad(c8t, dmod):
    nl, _, Ws = dmod.shape
    D = c8t.shape[0]
    tm = _tile(D, 512, 8)

    def body(c_ref, d_ref, o_ref):
        sc = _silu(c_ref[...])
        acc = sc[:, 0:1] * d_ref[0:1, :]
        for e in range(1, N_DEV):
            acc = acc + sc[:, e:e + 1] * d_ref[e:e + 1, :]
        o_ref[...] = acc

    return pl.pallas_call(
        body, grid=(nl, D // tm),
        in_specs=[pl.BlockSpec((tm, N_DEV), lambda l, i: (i, 0)), pl.BlockSpec((None, N_DEV, Ws), lambda l, i: (l, 0, 0))],
        out_specs=pl.BlockSpec((None, tm, Ws), lambda l, i: (l, i, 0)),
        out_shape=jax.ShapeDtypeStruct((nl, D, Ws), F32), compiler_params=_cp(("parallel", "parallel")),
        name="ada_wgrad")(c8t, dmod)


def _adamw_math(w, gv, m, v):
    c1 = 1.0 - ADAM_B1 ** ADAM_STEP
    c2 = 1.0 - ADAM_B2 ** ADAM_STEP
    nm = ADAM_B1 * m + (1.0 - ADAM_B1) * gv
    nv = ADAM_B2 * v + (1.0 - ADAM_B2) * (gv * gv)
    return -ADAM_LR * ((nm / c1) / (jnp.sqrt(nv / c2) + ADAM_EPS) + ADAM_WD * w), nm, nv


def adamw_ada(w, c8t, dmod, m, v):
    nl, D, Ws = w.shape
    tr = _tile(D, 256, 8)

    def body(c_ref, d_ref, w_ref, m_ref, v_ref, g_ref, dl_ref, nm_ref, nv_ref):
        sc = _silu(c_ref[...])
        gv = sc[:, 0:1] * d_ref[0:1, :]
        for e in range(1, N_DEV):
            gv = gv + sc[:, e:e + 1] * d_ref[e:e + 1, :]
        g_ref[...] = gv
        dl_ref[...], nm_ref[...], nv_ref[...] = _adamw_math(w_ref[...], gv, m_ref[...], v_ref[...])

    blk = pl.BlockSpec((None, tr, Ws), lambda l, i: (l, i, 0))
    return pl.pallas_call(
        body, grid=(nl, D // tr),
        in_specs=[pl.BlockSpec((tr, N_DEV), lambda l, i: (i, 0)), pl.BlockSpec((None, N_DEV, Ws), lambda l, i: (l, 0, 0)),
                  blk, blk, blk],
        out_specs=[blk] * 4, out_shape=[jax.ShapeDtypeStruct((nl, D, Ws), F32)] * 4,
        compiler_params=_cp(("parallel", "parallel")), name="adamw_ada_w")(c8t, dmod, w, m, v)


def adamw(w, g, m, v, name):
    R, C = w.shape
    tr = _tile(R, 256, 8)
    c1 = 1.0 - ADAM_B1 ** ADAM_STEP
    c2 = 1.0 - ADAM_B2 ** ADAM_STEP

    def body(w_ref, g_ref, m_ref, v_ref, d_ref, nm_ref, nv_ref):
        gv = g_ref[...]
        nm = ADAM_B1 * m_ref[...] + (1.0 - ADAM_B1) * gv
        nv = ADAM_B2 * v_ref[...] + (1.0 - ADAM_B2) * (gv * gv)
        nm_ref[...] = nm
        nv_ref[...] = nv
        d_ref[...] = -ADAM_LR * ((nm / c1) / (jnp.sqrt(nv / c2) + ADAM_EPS) + ADAM_WD * w_ref[...])

    return pl.pallas_call(
        body, grid=(R // tr,), in_specs=_row_specs(tr, [C] * 4), out_specs=_row_specs(tr, [C] * 3),
        out_shape=[jax.ShapeDtypeStruct((R, C), F32)] * 3, compiler_params=_cp(("parallel",)), name=name)(w, g, m, v)


def sum_leading(t, name, out_dtype=F32):
    S, R, C = t.shape
    tr = _tile(R, 256, 16)

    def body(t_ref, o_ref):
        acc = t_ref[0].astype(F32)
        for s in range(1, S):
            acc = acc + t_ref[s].astype(F32)
        o_ref[...] = acc.astype(out_dtype)

    return pl.pallas_call(
        body, grid=(R // tr,), in_specs=[pl.BlockSpec((S, tr, C), lambda i: (0, i, 0))],
        out_specs=pl.BlockSpec((tr, C), lambda i: (i, 0)), out_shape=jax.ShapeDtypeStruct((R, C), out_dtype),
        compiler_params=_cp(("parallel",)), name=name)(t)


def add_half(g, a, core, name):
    S, R, C = g.shape
    h = R // 2
    tr = _tile(h, 256, 16)
    nb = h // tr

    def body(core_ref, g_ref, a_ref, o_ref):
        o_ref[...] = (g_ref[...].astype(F32) + a_ref[...].astype(F32)).astype(BF16)

    return pl.pallas_call(
        body,
        grid_spec=pltpu.PrefetchScalarGridSpec(
            num_scalar_prefetch=1, grid=(S, nb),
            in_specs=[pl.BlockSpec((None, tr, C), lambda s, i, core_ref: (s, core_ref[0] * nb + i, 0)),
                      pl.BlockSpec((None, tr, C), lambda s, i, core_ref: (s, i, 0))],
            out_specs=pl.BlockSpec((None, tr, C), lambda s, i, core_ref: (s, i, 0))),
        out_shape=jax.ShapeDtypeStruct((S, h, C), BF16), compiler_params=_cp(("parallel", "parallel")),
        name=name)(core, g, a)


def sum_partials(own, landed, chip, name):
    _, h, C = own.shape
    tr = _tile(h, 256, 16)

    def body(chip_ref, own_ref, l_ref, o_ref):
        acc = own_ref[...].astype(F32)
        for j in range(3):
            acc = acc + l_ref[j].astype(F32)
        o_ref[...] = acc

    return pl.pallas_call(
        body,
        grid_spec=pltpu.PrefetchScalarGridSpec(
            num_scalar_prefetch=1, grid=(h // tr,),
            in_specs=[pl.BlockSpec((None, tr, C), lambda i, chip_ref: (chip_ref[0], i, 0)),
                      pl.BlockSpec((3, tr, C), lambda i, chip_ref: (0, i, 0))],
            out_specs=pl.BlockSpec((tr, C), lambda i, chip_ref: (i, 0))),
        out_shape=jax.ShapeDtypeStruct((h, C), F32), compiler_params=_cp(("parallel",)), name=name)(chip, own, landed)


def adamw_halves(w, g_mine, g_theirs, m, v, core, name):
    R, C = w.shape
    h = R // 2
    tr = _tile(h, 256, 8)
    nbh = h // tr
    c1 = 1.0 - ADAM_B1 ** ADAM_STEP
    c2 = 1.0 - ADAM_B2 ** ADAM_STEP

    def body(core_ref, w_ref, gm_ref, gt_ref, m_ref, v_ref, g_ref, d_ref, nm_ref, nv_ref):
        mine = (pl.program_id(0) // nbh) == core_ref[0]
        gv = jnp.where(mine, gm_ref[...], gt_ref[...])
        g_ref[...] = gv
        nm = ADAM_B1 * m_ref[...] + (1.0 - ADAM_B1) * gv
        nv = ADAM_B2 * v_ref[...] + (1.0 - ADAM_B2) * (gv * gv)
        nm_ref[...] = nm
        nv_ref[...] = nv
        d_ref[...] = -ADAM_LR * ((nm / c1) / (jnp.sqrt(nv / c2) + ADAM_EPS) + ADAM_WD * w_ref[...])

    full = pl.BlockSpec((tr, C), lambda i, core_ref: (i, 0))
    halfspec = pl.BlockSpec((tr, C), lambda i, core_ref: (i % nbh, 0))
    return pl.pallas_call(
        body,
        grid_spec=pltpu.PrefetchScalarGridSpec(
            num_scalar_prefetch=1, grid=(2 * nbh,), in_specs=[full, halfspec, halfspec, full, full],
            out_specs=[full] * 4),
        out_shape=[jax.ShapeDtypeStruct((R, C), F32)] * 4, compiler_params=_cp(("parallel",)),
        name=name)(core, w, g_mine, g_theirs, m, v)


_ANY = pl.BlockSpec(memory_space=pl.ANY)


def _place():
    x, y, c = lax.axis_index("x"), lax.axis_index("y"), lax.axis_index("c")
    chips = [(1 - x, y), (x, 1 - y), (1 - x, 1 - y)]
    return x, y, c, chips


def allgather_small(v, name, after=None):
    R, W = v.shape
    extra = [] if after is None else [after]

    def body(x_ref, *rest):
        out_ref, send_sems, recv_sems, local_sem = rest[len(extra):]
        x, y, c, chips = _place()
        me, sibling = (x, y, c), (x, y, 1 - c)

        def rows(px, py, pc):
            return out_ref.at[pl.ds((4 * px + 2 * py + pc) * R, R), :]

        def copy(k, block, to, src=None):
            return pltpu.make_async_remote_copy(
                src_ref=rows(*block) if src is None else src, dst_ref=rows(*block),
                send_sem=send_sems.at[k], recv_sem=recv_sems.at[k], device_id=to, device_id_type=MESH)

        mine = pltpu.make_async_copy(x_ref, rows(*me), local_sem)
        mine.start()
        first = [copy(0, me, sibling, src=x_ref)]
        first += [copy(1 + j, me, (*chip, c), src=x_ref) for j, chip in enumerate(chips)]
        for cp in first:
            cp.start()
        passed = [copy(4 + j, (*chip, c), sibling) for j, chip in enumerate(chips)]
        for j, chip in enumerate(chips):
            copy(1 + j, (*chip, c), me).wait_recv()
            passed[j].start()
        copy(0, sibling, me).wait_recv()
        for j, chip in enumerate(chips):
            copy(4 + j, (*chip, 1 - c), me).wait_recv()
        for cp in first + passed:
            cp.wait_send()
        mine.wait()

    return pl.pallas_call(
        body, out_shape=jax.ShapeDtypeStruct((N_DEV * R, W), v.dtype),
        in_specs=[pl.BlockSpec(memory_space=pltpu.VMEM)] + [_ANY] * len(extra),
        out_specs=pl.BlockSpec(memory_space=pltpu.VMEM),
        scratch_shapes=[pltpu.SemaphoreType.DMA((7,)), pltpu.SemaphoreType.DMA((7,)), pltpu.SemaphoreType.DMA],
        name=name)(v, *extra)


def allgather_weights(shards, name="allgather_weights"):
    n = len(shards)

    def body(*refs):
        ins, outs = refs[:n], refs[n:2 * n]
        send_sems, recv_sems = refs[2 * n:]
        x, y, c, chips = _place()
        p = 2 * x + y
        sibling = (x, y, 1 - c)

        def half(i, chip_id, core, ref=None):
            r = outs[i].at[chip_id] if ref is None else ref
            return r.at[core]

        def copy(i, k, chip_id, core, to, src=None):
            return pltpu.make_async_remote_copy(
                src_ref=half(i, chip_id, core) if src is None else src, dst_ref=half(i, chip_id, core),
                send_sem=send_sems.at[6 * i + k], recv_sem=recv_sems.at[6 * i + k], device_id=to, device_id_type=MESH)

        first = [copy(i, j, p, c, (*chip, c), src=half(i, p, c, ref=ins[i]))
                 for i in range(n) for j, chip in enumerate(chips)]
        for cp in first:
            cp.start()
        passed = []
        for i in range(n):
            for j, (cx, cy) in enumerate(chips):
                copy(i, j, 2 * cx + cy, c, sibling).wait_recv()
                fw = copy(i, 3 + j, 2 * cx + cy, c, sibling)
                fw.start()
                passed.append(fw)
        for i in range(n):
            for j, (cx, cy) in enumerate(chips):
                copy(i, 3 + j, 2 * cx + cy, 1 - c, sibling).wait_recv()
        for cp in first + passed:
            cp.wait_send()

    split = [s.reshape(2, s.shape[0] // 2, s.shape[1]) for s in shards]
    outs = pl.pallas_call(
        body, out_shape=[jax.ShapeDtypeStruct((N_CHIPS,) + s.shape, s.dtype) for s in split],
        in_specs=[_ANY] * n, out_specs=[_ANY] * n,
        scratch_shapes=[pltpu.SemaphoreType.DMA((6 * n,)), pltpu.SemaphoreType.DMA((6 * n,))],
        name=name)(*split)
    chip = 2 * lax.axis_index("x") + lax.axis_index("y")
    return [lax.dynamic_update_index_in_dim(o, s, chip, 0).reshape((N_CHIPS,) + sh.shape)
            for o, s, sh in zip(outs, split, shards)]


_HBM = pl.BlockSpec(memory_space=pltpu.HBM)
_SEM = pl.BlockSpec(memory_space=pltpu.SEMAPHORE)
_EFFECT = pltpu.SideEffectType.DATAFLOW_SIDE_EFFECTING


def _chip_copies(kind, srcs, lands, send_sems, recv_sems):
    x, y, c, chips = _place()
    p = 2 * x + y
    cps = []
    if kind == "sibling":
        for i in range(len(srcs)):
            h = srcs[i].shape[1] // 2
            cps.append(pltpu.make_async_remote_copy(
                src_ref=srcs[i].at[:, pl.ds((1 - c) * h, h), :], dst_ref=lands[i], send_sem=send_sems.at[3 * i],
                recv_sem=recv_sems.at[3 * i], device_id=(x, y, 1 - c), device_id_type=MESH))
        return cps
    for i in range(len(srcs)):
        for j, (cx, cy) in enumerate(chips):
            if kind == "gather":
                src, dst = srcs[i].at[c], lands[i].at[p, c]
            else:
                src, dst = srcs[i].at[2 * cx + cy], lands[i].at[j]
            cps.append(pltpu.make_async_remote_copy(
                src_ref=src, dst_ref=dst, send_sem=send_sems.at[3 * i + j], recv_sem=recv_sems.at[3 * i + j],
                device_id=(cx, cy, c), device_id_type=MESH))
    return cps


def split_start(kind, srcs, land_shapes, after, name):
    n = len(srcs)

    def body(*refs):
        src_refs, land_refs = refs[:n], refs[n:2 * n]
        send_sems, recv_sems = refs[2 * n + 1], refs[2 * n + 2]
        token = refs[-1]
        for cp in _chip_copies(kind, src_refs, land_refs, send_sems, recv_sems):
            cp.start()
        token[...] = jnp.zeros_like(token)

    lands = [pltpu.with_memory_space_constraint(lax.empty(s, BF16), pltpu.HBM) for s in land_shapes]
    outs = pl.pallas_call(
        body, name=name,
        out_shape=(pltpu.SemaphoreType.DMA((3 * n,)), pltpu.SemaphoreType.DMA((3 * n,)),
                   *[pltpu.HBM(s.shape, s.dtype) for s in srcs], *[pltpu.HBM(s, BF16) for s in land_shapes],
                   jax.ShapeDtypeStruct((8, 128), F32)),
        in_specs=[_HBM] * (2 * n) + [_ANY],
        out_specs=(_SEM, _SEM, *([_HBM] * (2 * n)), pl.BlockSpec(memory_space=pltpu.VMEM)),
        input_output_aliases={i: 2 + i for i in range(2 * n)},
        compiler_params=pltpu.CompilerParams(has_side_effects=_EFFECT),
    )(*[pltpu.with_memory_space_constraint(s, pltpu.HBM) for s in srcs], *lands, after)
    return outs[0], outs[1], outs[2:2 + n], outs[2 + n:2 + 2 * n], outs[-1]


def split_wait(kind, send_sems, recv_sems, srcs, lands, after, name):
    n = len(srcs)

    def body(*refs):
        src_refs, land_refs = refs[:n], refs[n:2 * n]
        ssem, rsem = refs[2 * n], refs[2 * n + 1]
        for cp in _chip_copies(kind, src_refs, land_refs, ssem, rsem):
            cp.wait_send()
            cp.wait_recv()

    outs = pl.pallas_call(
        body, name=name,
        out_shape=[pltpu.HBM(s.shape, s.dtype) for s in srcs] + [pltpu.HBM(s.shape, s.dtype) for s in lands],
        in_specs=[_HBM] * (2 * n) + [_SEM, _SEM, _ANY], out_specs=[_HBM] * (2 * n),
        input_output_aliases={i: i for i in range(2 * n)},
        compiler_params=pltpu.CompilerParams(has_side_effects=_EFFECT),
    )(*srcs, *lands, send_sems, recv_sems, after)
    return outs[:n], outs[n:]


def pass_to_sibling(lands):
    n = len(lands)

    def body(*refs):
        ins, outs = refs[:n], refs[n:2 * n]
        send_sems, recv_sems = refs[2 * n:]
        x, y, c, chips = _place()
        cps = []
        for i in range(n):
            for j, (cx, cy) in enumerate(chips):
                blk = outs[i].at[2 * cx + cy, c]
                cps.append(pltpu.make_async_remote_copy(
                    src_ref=ins[i].at[2 * cx + cy, c], dst_ref=blk, send_sem=send_sems.at[3 * i + j],
                    recv_sem=recv_sems.at[3 * i + j], device_id=(x, y, 1 - c), device_id_type=MESH))
        for cp in cps:
            cp.start()
        for cp in cps:
            cp.wait()

    return pl.pallas_call(
        body, out_shape=[jax.ShapeDtypeStruct(t.shape, t.dtype) for t in lands], in_specs=[_ANY] * n,
        out_specs=[_ANY] * n, input_output_aliases={i: i for i in range(n)},
        scratch_shapes=[pltpu.SemaphoreType.DMA((3 * n,)), pltpu.SemaphoreType.DMA((3 * n,))],
        name="ag_pass_to_sibling")(*lands)


def exchange_halves_to_sibling(gs, name):
    n = len(gs)

    def body(*refs):
        ins, outs = refs[:n], refs[n:2 * n]
        send_sems, recv_sems = refs[2 * n:]
        x, y, c, _ = _place()
        cps = []
        for i in range(n):
            h = ins[i].shape[1] // 2
            cps.append(pltpu.make_async_remote_copy(
                src_ref=ins[i].at[:, pl.ds((1 - c) * h, h), :], dst_ref=outs[i],
                send_sem=send_sems.at[i], recv_sem=recv_sems.at[i], device_id=(x, y, 1 - c), device_id_type=MESH))
        for cp in cps:
            cp.start()
        for cp in cps:
            cp.wait()

    return pl.pallas_call(
        body, out_shape=[jax.ShapeDtypeStruct((g.shape[0], g.shape[1] // 2, g.shape[2]), g.dtype) for g in gs],
        in_specs=[_ANY] * n, out_specs=[_ANY] * n,
        scratch_shapes=[pltpu.SemaphoreType.DMA((n,)), pltpu.SemaphoreType.DMA((n,))],
        name=name)(*gs)


def scatter_to_chips(ps, name):
    n = len(ps)

    def body(*refs):
        ins, outs = refs[:n], refs[n:2 * n]
        send_sems, recv_sems = refs[2 * n:]
        x, y, c, chips = _place()
        cps = []
        for i in range(n):
            for j, (cx, cy) in enumerate(chips):
                cps.append(pltpu.make_async_remote_copy(
                    src_ref=ins[i].at[2 * cx + cy], dst_ref=outs[i].at[j], send_sem=send_sems.at[3 * i + j],
                    recv_sem=recv_sems.at[3 * i + j], device_id=(cx, cy, c), device_id_type=MESH))
        for cp in cps:
            cp.start()
        for cp in cps:
            cp.wait()

    return pl.pallas_call(
        body, out_shape=[jax.ShapeDtypeStruct((3,) + t.shape[1:], t.dtype) for t in ps],
        in_specs=[_ANY] * n, out_specs=[_ANY] * n,
        scratch_shapes=[pltpu.SemaphoreType.DMA((3 * n,)), pltpu.SemaphoreType.DMA((3 * n,))],
        name=name)(*ps)


def join_halves(rs, name):
    n = len(rs)

    def body(*refs):
        ins, outs = refs[:n], refs[n:2 * n]
        send_sems, recv_sems = refs[2 * n:]
        x, y, c, _ = _place()
        cps = [pltpu.make_async_remote_copy(
            src_ref=ins[i], dst_ref=outs[i], send_sem=send_sems.at[i], recv_sem=recv_sems.at[i],
            device_id=(x, y, 1 - c), device_id_type=MESH) for i in range(n)]
        for cp in cps:
            cp.start()
        for cp in cps:
            cp.wait()

    return pl.pallas_call(
        body, out_shape=[jax.ShapeDtypeStruct(r.shape, r.dtype) for r in rs],
        in_specs=[_ANY] * n, out_specs=[_ANY] * n,
        scratch_shapes=[pltpu.SemaphoreType.DMA((n,)), pltpu.SemaphoreType.DMA((n,))],
        name=name)(*rs)


def _pack(parts, row_mult=8):
    flat = jnp.concatenate([p.reshape(-1).astype(F32) for p in parts])
    unit = row_mult * 128
    n = -(-flat.shape[0] // unit) * unit
    return jnp.pad(flat, (0, n - flat.shape[0])).reshape(n // 128, 128)


def _unpack(flat, shapes):
    out, off = [], 0
    for s in shapes:
        n = int(np.prod(s))
        out.append(flat[off:off + n].reshape(s))
        off += n
    return out


def _gather_packed(parts, name):
    packed = _pack(parts)
    g = allgather_small(packed, name).reshape(N_DEV, -1)
    return _unpack_rows(g, [p.shape for p in parts])


def _unpack_rows(g, shapes):
    out, off = [], 0
    for s in shapes:
        n = int(np.prod(s))
        out.append(g[:, off:off + n].reshape((g.shape[0],) + tuple(s)))
        off += n
    return out


def _by_chip(t, axis):
    return jnp.concatenate([t[2 * p] for p in range(N_CHIPS)], axis=axis)


def kernel(x, c, ada_w, ada_b, ln_g, ln_b, a_in_w, a_conv_w, a_conv_b, a_dt_bias, a_A_log, a_D, a_norm_g, a_out_w, kv_w, b_in_w, b_out_w, loss_target, m_ada_w, m_ada_b, m_ln_g, m_ln_b, m_a_in_w, m_a_conv_w, m_a_conv_b, m_a_dt_bias, m_a_A_log, m_a_D, m_a_norm_g, m_a_out_w, m_kv_w, m_b_in_w, m_b_out_w, v_ada_w, v_ada_b, v_ln_g, v_ln_b, v_a_in_w, v_a_conv_w, v_a_conv_b, v_a_dt_bias, v_a_A_log, v_a_D, v_a_norm_g, v_a_out_w, v_kv_w, v_b_in_w, v_b_out_w):
    ax, ay, ac = lax.axis_index("x"), lax.axis_index("y"), lax.axis_index("c")
    chip = 2 * ax + ay
    dev = 4 * ax + 2 * ay + ac
    xin = x[0]
    tgt = loss_target[0]
    L, D = xin.shape
    G, P = SSD_G, SSD_P
    H = a_dt_bias.shape[1]
    Kh = H // G
    DI = H * P
    CONVD = a_conv_b.shape[1] * N_CHIPS
    HW = DIL_H * DIL_E
    Ws = ada_w.shape[2]

    (w_in_g,) = allgather_weights([a_in_w[0].astype(BF16)], "allgather_w_in")
    later = [a_out_w[0].astype(BF16), kv_w.astype(BF16), b_in_w[0].astype(BF16), b_out_w[0].astype(BF16)]
    later_split = [s.reshape(2, s.shape[0] // 2, s.shape[1]) for s in later]
    ag_ssem, ag_rsem, ag_srcs, ag_lands, ag_token = split_start(
        "gather", later_split, [(N_CHIPS,) + s.shape for s in later_split], w_in_g, "ag_later_start")
    w_in = jnp.transpose(w_in_g, (1, 0, 2)).reshape(D, -1)
    w_zx = w_in
    w_dt = jnp.pad(w_in[:, DI + CONVD:], ((0, 0), (0, 128 - H)))

    c8, cw8, cb8, ng8 = _gather_packed([c[0], a_conv_w[0], a_conv_b[0], a_norm_g[0]], "allgather_small_params")
    conv_w = _by_chip(cw8, 1)
    conv_b = _by_chip(cb8, 0).reshape(1, CONVD)
    norm_g = _by_chip(ng8, 0).reshape(1, DI)

    mod_s = ada_fwd(c8, ada_w)
    (mod8,) = _gather_packed([mod_s], "allgather_small_mod")
    mods = _by_chip(mod8, 2)
    mod = lax.dynamic_index_in_dim(mods, dev, axis=1, keepdims=False) + ada_b
    shift = [mod[l:l + 1, :D] for l in range(DEPTH)]
    scale = [mod[l:l + 1, D:2 * D] for l in range(DEPTH)]
    gate = [mod[l:l + 1, 2 * D:] for l in range(DEPTH)]
    lg = [ln_g[l:l + 1] for l in range(DEPTH)]
    lb = [ln_b[l:l + 1] for l in range(DEPTH)]

    h0 = modulate(xin, scale[0] + ag_token[0:1, 0:1], shift[0], "modulate0")
    zx = mm_nn(h0, w_zx, BF16, "mm_in_zx", n_cols=DI + CONVD)
    dtp = mm_nn(h0, w_dt, F32, "mm_in_dt")
    xbc = conv_fwd(zx, DI, conv_w, conv_b)
    dtp_g = jnp.transpose(dtp[:, :H].reshape(L, G, Kh), (1, 0, 2))
    dtp_gT = jnp.transpose(dtp_g, (0, 2, 1))
    vecs = [a_dt_bias.reshape(G, 1, Kh), a_dt_bias.reshape(G, Kh, 1), a_A_log.reshape(G, 1, Kh),
            a_A_log.reshape(G, Kh, 1), a_D.reshape(G, 1, Kh), a_D.reshape(G, Kh, 1)]
    y_ssd, states = ssd_fwd(xbc, dtp_g, dtp_gT, *vecs, DI)
    yn = rms_gate_fwd(y_ssd, zx, norm_g)
    later_split, ag_lands = split_wait("gather", ag_ssem, ag_rsem, ag_srcs, ag_lands, yn, "ag_later_wait")
    ag_lands = pass_to_sibling(ag_lands)
    w_out_g, w_kv_g, w_bin_g, w_bout_g = [
        lax.dynamic_update_index_in_dim(o, s, chip, 0).reshape((N_CHIPS,) + full.shape)
        for o, s, full in zip(ag_lands, later_split, later)]
    ymix0 = mm_nn(yn, w_out_g, F32, "mm_out_a", stack="row")
    x1, x1b, h1 = ln_mid(xin, ymix0, gate[0], lg[0], lb[0], scale[1], shift[1])

    n_grp = len(DIL_PATTERNS)
    cb = HW // 512
    assert w_bin_g.shape[2] == HW
    kv3 = [mm_cols_dilated(x1b, w_kv_g, [g * cb + t for t in range(cb)] + [(n_grp + g) * cb + t for t in range(cb)],
                           DIL_PATTERNS[g][1], f"mm_kv_{g}") for g in range(n_grp)]
    q3 = [mm_cols_dilated(h1, w_bin_g, [g * cb + t for t in range(cb)], DIL_PATTERNS[g][1], f"mm_q_{g}")
          for g in range(n_grp)]
    z_b = mm_nn(h1, w_bin_g[n_grp], BF16, "mm_z_b")
    os_, lses = [], []
    for gi in range(len(DIL_PATTERNS)):
        o, lse = attn_fwd(q3[gi], kv3[gi], gi)
        os_.append(o)
        lses.append(lse)
    om = merge_fwd(os_, lses, z_b)
    ymix1 = mm_nn(om, w_bout_g, F32, "mm_out_b", stack="col")
    dres2, dy2, dg1, db1, dgate1, sq = ln_final_fwd_bwd(x1, ymix1, gate[1], lg[1], lb[1], tgt)
    loss_part = 0.5 * jnp.sum(sq) / D

    g_bout = mm_tn(om, dy2, BF16, "mm_gw_out_b", stack="col")
    dgated = mm_nt(dy2, w_bout_g, BF16, "mm_gx_out_b", stack="col")
    dos, dprs, dz_b = merge_bwd(dgated, os_, lses, z_b)
    dqs, dks, dvs = [], [], []
    for gi in range(len(DIL_PATTERNS)):
        dq, dk, dv = attn_bwd(q3[gi], kv3[gi], dos[gi], lses[gi], dprs[gi], gi)
        dqs.append(dq)
        dks.append(dk)
        dvs.append(dv)
    dqz = jnp.concatenate(dqs + [dz_b], axis=1)
    dkv = jnp.concatenate(dks + dvs, axis=1)
    g_bin = mm_tn(h1, dqz, BF16, "mm_gw_in_b", stack="col")
    dh1 = mm_nt(dqz, w_bin_g, BF16, "mm_gx_in_b", stack="col")
    g_kv = mm_tn(x1b, dkv, BF16, "mm_gw_kv", stack="col")

    core = ac.astype(jnp.int32).reshape(1)
    chip_i = chip.astype(jnp.int32).reshape(1)

    def begin_exchange(gs, tag):
        shapes = [(g.shape[0], g.shape[1] // 2, g.shape[2]) for g in gs]
        return split_start("sibling", gs, shapes, gs[0], "rs_x%s_start" % tag)

    def begin_scatter(gs, nms, tag, exchange=None, after=None):
        if exchange is None:
            sib = exchange_halves_to_sibling(gs, "rs_sibling_exchange_" + tag)
        else:
            gs, sib = split_wait("sibling", exchange[0], exchange[1], exchange[2], exchange[3], after,
                                 "rs_x%s_wait" % tag)
        parts = [add_half(g, a, core, "rs_add_" + nm) for g, a, nm in zip(gs, sib, nms)]
        return split_start("scatter", parts, [(3,) + t.shape[1:] for t in parts], parts[0], "rs_%s_start" % tag)

    def finish_scatter(handles, after, tag):
        nms, owns, landed = [], [], []
        for k, (handle, hn) in enumerate(handles):
            parts, lands = split_wait("scatter", handle[0], handle[1], handle[2], handle[3], after,
                                      "rs_%s%d_wait" % (tag, k))
            nms += hn
            owns += list(parts)
            landed += list(lands)
        halves = [sum_partials(own, t, chip_i, "rs_sum_" + nm) for own, t, nm in zip(owns, landed, nms)]
        theirs = join_halves(halves, "rs_join_halves_" + tag)
        return dict(zip(nms, zip(halves, theirs)))

    names_b = ["kv", "in_b", "out_b"]
    ex_b = begin_exchange([g_kv, g_bin, g_bout], "b")
    dx1_kv = mm_nt(dkv, w_kv_g, BF16, "mm_gx_kv", stack="col", after=ex_b[4])
    rs_b = begin_scatter(None, names_b, "b", exchange=ex_b, after=dx1_kv)

    dres1, dy1, dg0, db0, dgate0, dscale1, dshift1 = mod_ln_bwd(
        dres2, dh1, dx1_kv, x1, scale[1], xin, ymix0, gate[0] + rs_b[4][0:1, 0:1], lg[0])
    g_out = mm_tn(yn, dy1, BF16, "mm_gw_out_a", stack="row")
    ex_a1 = begin_exchange([g_out], "a1")
    dyn = mm_nt(dy1, w_out_g, BF16, "mm_gx_out_a", stack="row", after=ex_a1[4])
    rs_a1 = begin_scatter(None, ["out_a"], "a1", exchange=ex_a1, after=dyn)
    dy_ssd, dz_a, dnorm_g = rms_gate_bwd(dyn, y_ssd, zx, norm_g + rs_a1[4][0:1, 0:1])
    dxs, dB, dC, ddtp_g, dbias_g, dalog_g, dD_g = ssd_bwd(xbc, dtp_g, dtp_gT, *vecs, states, dy_ssd, DI)
    dxbc = jnp.concatenate([dxs, dB, dC], axis=1)
    dxbc_pre, dconv_w, dconv_b = conv_bwd(zx, DI, conv_w, conv_b, dxbc)
    dzx = jnp.concatenate([dz_a, dxbc_pre], axis=1)
    ddtp = jnp.pad(jnp.transpose(ddtp_g, (1, 0, 2)).reshape(L, H), ((0, 0), (0, 128 - H)))
    g_zx = mm_tn(h0, dzx, BF16, "mm_gw_in_zx")
    g_dt = mm_tn(h0, ddtp, BF16, "mm_gw_in_dt")
    g_in = jnp.concatenate([g_zx, g_dt[:, :H]], axis=1)
    cs_in = g_in.shape[1] // N_CHIPS
    g_in = jnp.stack([g_in[:, s * cs_in:(s + 1) * cs_in] for s in range(N_CHIPS)])
    rs_a2 = begin_scatter([g_in], ["in_a"], "a2")
    dh0 = mm_nt(dzx, w_zx, BF16, "mm_gx_in_zx", after=rs_a2[4])
    dh0_dt = mm_nt(ddtp, w_dt, F32, "mm_gx_in_dt")
    grad_x, dscale0, dshift0 = mod_bwd(dres1, dh0, dh0_dt, xin, scale[0] + rs_a2[4][0:1, 0:1], "mod_bwd0",
                                       through_mod=True)
    g_halves = finish_scatter([(rs_b, names_b)], grad_x, "b")

    def step_halves(w, m, v, nm):
        shp = w.shape
        mine, theirs_ = g_halves[nm]
        outs4 = adamw_halves(w.reshape(-1, shp[-1]), mine, theirs_, m.reshape(-1, shp[-1]), v.reshape(-1, shp[-1]),
                             core, "adamw_" + nm)
        return tuple(t.reshape(shp) for t in outs4)

    big = {
        "kv_w": step_halves(kv_w, m_kv_w, v_kv_w, "kv"),
        "b_in_w": step_halves(b_in_w, m_b_in_w, v_b_in_w, "in_b"),
        "b_out_w": step_halves(b_out_w, m_b_out_w, v_b_out_w, "out_b"),
    }
    g_halves.update(finish_scatter([(rs_a1, ["out_a"]), (rs_a2, ["in_a"])], big["kv_w"][1], "a"))
    big["a_in_w"] = step_halves(a_in_w, m_a_in_w, v_a_in_w, "in_a")
    big["a_out_w"] = step_halves(a_out_w, m_a_out_w, v_a_out_w, "out_a")

    dmod = jnp.concatenate([jnp.concatenate([dshift0, dscale0, dgate0], axis=1),
                            jnp.concatenate([dshift1, dscale1, dgate1], axis=1)], axis=0)
    small_parts = [jnp.concatenate([dg0, dg1], axis=0), jnp.concatenate([db0, db1], axis=0),
                   dbias_g.reshape(1, H), dalog_g.reshape(1, H), dD_g.reshape(1, H),
                   dconv_w, dconv_b, dnorm_g, loss_part.reshape(1, 1)]
    small_shapes = [p.shape for p in small_parts]
    packed = jnp.concatenate([_pack([dmod]), _pack(small_parts)], axis=0)
    n_mod_rows = _pack([dmod]).shape[0]
    gathered = allgather_small(packed, "allgather_small_grads", after=g_halves["in_a"][1]).reshape(N_DEV, -1, 128)
    dmod8 = gathered[:, :n_mod_rows].reshape(N_DEV, -1)[:, :2 * 3 * D].reshape(N_DEV, DEPTH, 3 * D)
    summed = sum_leading(gathered, "sum_small")
    g_ada_b = summed[:n_mod_rows].reshape(-1)[:2 * 3 * D].reshape(DEPTH, 3 * D)
    (g_ln_g, g_ln_b, g_dt_bias, g_a_log, g_dsk, g_conv_w, g_conv_b, g_norm_g, loss_all) = _unpack(
        summed[n_mod_rows:].reshape(-1), small_shapes)
    loss = loss_all.reshape(())
    Cs = CONVD // N_CHIPS
    g_conv_w_s = lax.dynamic_slice_in_dim(g_conv_w, chip * Cs, Cs, axis=1)
    g_conv_b_s = lax.dynamic_slice_in_dim(g_conv_b, chip * Cs, Cs, axis=1)
    g_norm_g_s = lax.dynamic_slice_in_dim(g_norm_g, chip * (DI // N_CHIPS), DI // N_CHIPS, axis=1)
    dmod_s = jnp.transpose(lax.dynamic_slice_in_dim(dmod8, chip * Ws, Ws, axis=2), (1, 0, 2))

    def step2d(w, g, m, v, nm):
        shp = w.shape
        d_, m_, v_ = adamw(w.reshape(-1, shp[-1]), g.reshape(-1, shp[-1]), m.reshape(-1, shp[-1]),
                           v.reshape(-1, shp[-1]), "adamw_" + nm)
        return g.reshape(shp), d_.reshape(shp), m_.reshape(shp), v_.reshape(shp)

    big["ada_w"] = adamw_ada(ada_w, jnp.transpose(c8), dmod_s, m_ada_w, v_ada_w)
    small_names = ["ada_b", "ln_g", "ln_b", "a_conv_w", "a_conv_b", "a_dt_bias", "a_A_log", "a_D", "a_norm_g"]
    small_w = [ada_b, ln_g, ln_b, a_conv_w, a_conv_b, a_dt_bias, a_A_log, a_D, a_norm_g]
    small_m = [m_ada_b, m_ln_g, m_ln_b, m_a_conv_w, m_a_conv_b, m_a_dt_bias, m_a_A_log, m_a_D, m_a_norm_g]
    small_v = [v_ada_b, v_ln_g, v_ln_b, v_a_conv_w, v_a_conv_b, v_a_dt_bias, v_a_A_log, v_a_D, v_a_norm_g]
    small_g = [g_ada_b, g_ln_g, g_ln_b, g_conv_w_s, g_conv_b_s, g_dt_bias, g_a_log, g_dsk, g_norm_g_s]
    shapes = [w.shape for w in small_w]
    small_g = [g.reshape(s) for g, s in zip(small_g, shapes)]
    d_p, m_p, v_p = adamw(_pack(small_w), _pack(small_g), _pack(small_m), _pack(small_v), "adamw_small")
    small = {}
    for nm, g, d_, m_, v_ in zip(small_names, small_g, _unpack(d_p.reshape(-1), shapes), _unpack(m_p.reshape(-1), shapes),
                                 _unpack(v_p.reshape(-1), shapes)):
        small[nm] = (g, d_, m_, v_)
    allw = {**big, **small}
    order = ["ada_w", "ada_b", "ln_g", "ln_b", "a_in_w", "a_conv_w", "a_conv_b", "a_dt_bias", "a_A_log", "a_D",
             "a_norm_g", "a_out_w", "kv_w", "b_in_w", "b_out_w"]
    outs = [loss, grad_x.reshape(x.shape)]
    for k in range(4):
        outs += [allw[n][k] for n in order]
    return tuple(outs)
```

```python
import functools

import jax
import jax.numpy as jnp
import numpy as np
from jax import lax
from jax.experimental import pallas as pl
from jax.experimental.pallas import tpu as pltpu

F32 = jnp.float32
BF16 = jnp.bfloat16
MESH = pl.DeviceIdType.MESH

DEPTH = 2
ALPHA = (2 * DEPTH) ** 0.25
LN_EPS = 1e-5
RMS_EPS = 1e-5
SSD_P = 64
SSD_N = 128
SSD_Q = 256
SSD_G = 8
CONV_W = 4
DIL_PATTERNS = ((128, 1), (512, 4), (2048, 16))
DIL_H = 8
DIL_E = 128
DIL_BLK = 128
ADAM_LR, ADAM_B1, ADAM_B2, ADAM_EPS, ADAM_WD, ADAM_STEP = 0.001, 0.9, 0.999, 1e-08, 0.01, 10

VMEM_LIMIT = 56 * 1024 * 1024
N_CHIPS = 4
N_DEV = 8


def _tile(dim, target, mult=128):
    if dim <= target:
        return dim
    t = (target // mult) * mult
    while t >= mult:
        if dim % t == 0:
            return t
        t -= mult
    return dim


def _cp(sem):
    return pltpu.CompilerParams(dimension_semantics=sem, vmem_limit_bytes=VMEM_LIMIT)


def _sigmoid(x):
    return 1.0 / (1.0 + jnp.exp(-x))


def _silu(x):
    return x * _sigmoid(x)


def _dsilu(x):
    s = _sigmoid(x)
    return s * (1.0 + x * (1.0 - s))


def _softplus(x):
    return jnp.maximum(x, 0.0) + jnp.log(1.0 + jnp.exp(-jnp.abs(x)))


def _mm_call(a, b, out_shape, grid, a_spec, b_spec, o_spec, acc_shape, dims, name, after=None):
    nk = grid[2]
    extra = [] if after is None else [after]

    def prod(a_ref, b_ref):
        return lax.dot_general(a_ref[...].astype(BF16), b_ref[...].astype(BF16), (dims, ((), ())),
                               preferred_element_type=F32)

    def body_single(a_ref, b_ref, *rest):
        o_ref = rest[len(extra)]
        o_ref[...] = prod(a_ref, b_ref).astype(o_ref.dtype)

    def body_multi(a_ref, b_ref, *rest):
        o_ref, acc_ref = rest[len(extra):]
        k = pl.program_id(2)

        @pl.when(k == 0)
        def _():
            acc_ref[...] = prod(a_ref, b_ref)

        @pl.when(jnp.logical_and(k > 0, k < nk - 1))
        def _():
            acc_ref[...] += prod(a_ref, b_ref)

        @pl.when(k == nk - 1)
        def _():
            o_ref[...] = (acc_ref[...] + prod(a_ref, b_ref)).astype(o_ref.dtype)

    return pl.pallas_call(
        body_single if nk == 1 else body_multi, grid=grid, in_specs=[a_spec, b_spec] + [_ANY] * len(extra),
        out_specs=o_spec, out_shape=out_shape, scratch_shapes=[] if nk == 1 else [pltpu.VMEM(acc_shape, F32)],
        compiler_params=_cp(("parallel", "parallel", "arbitrary")), name=name)(a, b, *extra)


def mm_nn(a, b, out_dtype, name, stack=None, tm=1024, tn=1024, tk=2048, n_cols=None, after=None):
    M, K = a.shape
    if stack is None:
        N = b.shape[1] if n_cols is None else n_cols
        tn, tk = _tile(N, tn), _tile(K, tk)
        b_spec = pl.BlockSpec((tk, tn), lambda i, j, k: (k, j))
    elif stack == "col":
        S, _, Ns = b.shape
        N = S * Ns
        tn, tk = _tile(Ns, tn), _tile(K, tk)
        npb = Ns // tn
        b_spec = pl.BlockSpec((None, tk, tn), lambda i, j, k: (j // npb, k, j % npb))
    else:
        S, Ks, N = b.shape
        tn, tk = _tile(N, tn), _tile(Ks, tk)
        kpb = Ks // tk
        b_spec = pl.BlockSpec((None, tk, tn), lambda i, j, k: (k // kpb, k % kpb, j))
    tm = _tile(M, tm)
    return _mm_call(a, b, jax.ShapeDtypeStruct((M, N), out_dtype), (M // tm, N // tn, K // tk),
                    pl.BlockSpec((tm, tk), lambda i, j, k: (i, k)), b_spec,
                    pl.BlockSpec((tm, tn), lambda i, j, k: (i, j)), (tm, tn), ((1,), (0,)), name, after=after)


def mm_cols_dilated(a, b, gcols, d, name, tm=1024, tn=512):
    L, K = a.shape
    S, _, Ns = b.shape
    tm, tn = _tile(L, tm), _tile(Ns, tn)
    npb = Ns // tn
    nj = len(gcols)
    rows = tm // d

    def body(cols_ref, a_ref, b_ref, o_ref, *scr):
        prod = jnp.dot(a_ref[...], b_ref[...], preferred_element_type=F32)
        if d == 1:
            o_ref[0] = prod.astype(BF16)
        else:
            for c in range(tn // 128):
                scr[0][c] = prod[:, c * 128:(c + 1) * 128]
            for r in range(d):
                for c in range(tn // 128):
                    o_ref[r, :, c * 128:(c + 1) * 128] = scr[0].at[c][pl.ds(r, rows, stride=d), :].astype(BF16)

    return pl.pallas_call(
        body,
        grid_spec=pltpu.PrefetchScalarGridSpec(
            num_scalar_prefetch=1, grid=(L // tm, nj),
            in_specs=[pl.BlockSpec((tm, K), lambda i, j, c: (i, 0)),
                      pl.BlockSpec((None, K, tn), lambda i, j, c: (c[j] // npb, 0, c[j] % npb))],
            out_specs=pl.BlockSpec((d, rows, tn), lambda i, j, c: (0, i, j)),
            scratch_shapes=[] if d == 1 else [pltpu.VMEM((tn // 128, tm, 128), F32)]),
        out_shape=jax.ShapeDtypeStruct((d, L // d, nj * tn), BF16),
        compiler_params=_cp(("parallel", "arbitrary")), name=name)(jnp.asarray(gcols, jnp.int32), a, b)


def mm_nt(a, b, out_dtype, name, stack=None, tm=1024, tn=1024, tk=2048, after=None, kw_rows=None):
    M, C = a.shape
    if stack is None:
        Kw = b.shape[0] if kw_rows is None else kw_rows
        tn, tk = _tile(Kw, tn), _tile(C, tk)
        b_spec = pl.BlockSpec((tn, tk), lambda i, j, k: (j, k))
    elif stack == "col":
        S, Kw, Cs = b.shape
        tn, tk = _tile(Kw, tn), _tile(Cs, tk)
        cpb = Cs // tk
        b_spec = pl.BlockSpec((None, tn, tk), lambda i, j, k: (k // cpb, j, k % cpb))
    else:
        S, Ks, _ = b.shape
        Kw = S * Ks
        tn, tk = _tile(Ks, tn), _tile(C, tk)
        jpb = Ks // tn
        b_spec = pl.BlockSpec((None, tn, tk), lambda i, j, k: (j // jpb, j % jpb, k))
    tm = _tile(M, tm)
    return _mm_call(a, b, jax.ShapeDtypeStruct((M, Kw), out_dtype), (M // tm, Kw // tn, C // tk),
                    pl.BlockSpec((tm, tk), lambda i, j, k: (i, k)), b_spec,
                    pl.BlockSpec((tm, tn), lambda i, j, k: (i, j)), (tm, tn), ((1,), (1,)), name, after=after)


def mm_tn(a, b, out_dtype, name, stack=None, n_stack=N_CHIPS, tm=1024, tn=1024, tk=2048, m_rows=None):
    L, M = a.shape
    N = b.shape[1]
    tk = _tile(L, tk)
    if stack is None:
        tm, tn = _tile(M, tm), _tile(N, tn)
        o_spec = pl.BlockSpec((tm, tn), lambda i, j, k: (i, j))
        out_shape = (M if m_rows is None else m_rows, N)
    elif stack == "col":
        Ns = N // n_stack
        tm, tn = _tile(M, tm), _tile(Ns, tn)
        npb = Ns // tn
        o_spec = pl.BlockSpec((None, tm, tn), lambda i, j, k: (j // npb, i, j % npb))
        out_shape = (n_stack, M, Ns)
    else:
        Ms = M // n_stack
        tm, tn = _tile(Ms, tm), _tile(N, tn)
        mpb = Ms // tm
        o_spec = pl.BlockSpec((None, tm, tn), lambda i, j, k: (i // mpb, i % mpb, j))
        out_shape = (n_stack, Ms, N)
    return _mm_call(a, b, jax.ShapeDtypeStruct(out_shape, out_dtype), (M // tm, N // tn, L // tk),
                    pl.BlockSpec((tk, tm), lambda i, j, k: (k, i)), pl.BlockSpec((tk, tn), lambda i, j, k: (k, j)),
                    o_spec, (tm, tn), ((0,), (0,)), name)


def _row_specs(tr, widths):
    return [pl.BlockSpec((tr, w), lambda i: (i, 0)) for w in widths]


def _vec_spec(w):
    return pl.BlockSpec((1, w), lambda i: (0, 0))


def _acc_rows(ref, val, i):
    s = jnp.sum(val, axis=0, keepdims=True)

    @pl.when(i == 0)
    def _():
        ref[...] = s

    @pl.when(i > 0)
    def _():
        ref[...] += s


def modulate(x, scale, shift, name):
    L, D = x.shape
    tr = _tile(L, 512, 16)

    def body(x_ref, sc_ref, sh_ref, h_ref):
        h_ref[...] = (x_ref[...] * (1.0 + sc_ref[...]) + sh_ref[...]).astype(BF16)

    return pl.pallas_call(
        body, grid=(L // tr,), in_specs=_row_specs(tr, [D]) + [_vec_spec(D)] * 2, out_specs=_row_specs(tr, [D])[0],
        out_shape=jax.ShapeDtypeStruct((L, D), BF16), compiler_params=_cp(("parallel",)), name=name)(x, scale, shift)


def _ln_core(x, y, gate, g, b):
    u = ALPHA * x + (1.0 + gate) * y
    mu = jnp.mean(u, axis=-1, keepdims=True)
    d = u - mu
    var = jnp.mean(d * d, axis=-1, keepdims=True)
    rstd = lax.rsqrt(var + LN_EPS)
    xhat = d * rstd
    return xhat * g + b, xhat, rstd


def ln_mid(x, y, gate, g, b, scale, shift):
    L, D = x.shape
    tr = _tile(L, 256, 16)

    def body(x_ref, y_ref, gate_ref, g_ref, b_ref, sc_ref, sh_ref, x1_ref, x1b_ref, h_ref):
        x1, _, _ = _ln_core(x_ref[...], y_ref[...], gate_ref[...], g_ref[...], b_ref[...])
        x1_ref[...] = x1
        x1b_ref[...] = x1.astype(BF16)
        h_ref[...] = (x1 * (1.0 + sc_ref[...]) + sh_ref[...]).astype(BF16)

    return pl.pallas_call(
        body, grid=(L // tr,), in_specs=_row_specs(tr, [D, D]) + [_vec_spec(D)] * 5,
        out_specs=_row_specs(tr, [D, D, D]),
        out_shape=[jax.ShapeDtypeStruct((L, D), F32), jax.ShapeDtypeStruct((L, D), BF16),
                   jax.ShapeDtypeStruct((L, D), BF16)],
        compiler_params=_cp(("parallel",)), name="ln_mid")(x, y, gate, g, b, scale, shift)


def _ln_bwd_rows(dout_v, xhat, rstd, g):
    dxh = dout_v * g
    m1 = jnp.mean(dxh, axis=-1, keepdims=True)
    m2 = jnp.mean(dxh * xhat, axis=-1, keepdims=True)
    return rstd * (dxh - m1 - xhat * m2)


def ln_final_fwd_bwd(x, y, gate, g, b, target):
    L, D = x.shape
    tr = _tile(L, 256, 16)

    def body(x_ref, y_ref, gate_ref, g_ref, b_ref, t_ref, dres_ref, dy_ref, dg_ref, db_ref, dgate_ref, sq_ref):
        i = pl.program_id(0)
        yv = y_ref[...]
        out, xhat, rstd = _ln_core(x_ref[...], yv, gate_ref[...], g_ref[...], b_ref[...])
        err = out - t_ref[...]
        dout_v = err * (1.0 / D)
        du = _ln_bwd_rows(dout_v, xhat, rstd, g_ref[...])
        dres_ref[...] = ALPHA * du
        dy_ref[...] = ((1.0 + gate_ref[...]) * du).astype(BF16)
        _acc_rows(dg_ref, dout_v * xhat, i)
        _acc_rows(db_ref, dout_v, i)
        _acc_rows(dgate_ref, du * yv, i)
        _acc_rows(sq_ref, err * err, i)

    return pl.pallas_call(
        body, grid=(L // tr,), in_specs=_row_specs(tr, [D, D]) + [_vec_spec(D)] * 3 + _row_specs(tr, [D]),
        out_specs=_row_specs(tr, [D, D]) + [_vec_spec(D)] * 4,
        out_shape=[jax.ShapeDtypeStruct((L, D), F32), jax.ShapeDtypeStruct((L, D), BF16)]
        + [jax.ShapeDtypeStruct((1, D), F32)] * 4,
        compiler_params=_cp(("arbitrary",)), name="ln_final_fwd_bwd")(x, y, gate, g, b, target)


def mod_ln_bwd(dres_in, dh, dskip, xmid, scale, x, y, gate, g):
    L, D = x.shape
    tr = _tile(L, 256, 16)

    def body(dres_ref, dh_ref, dskip_ref, xm_ref, sc_ref, x_ref, y_ref, gate_ref, g_ref,
             dres_out, dy_ref, dg_ref, db_ref, dgate_ref, dsc_ref, dsh_ref):
        i = pl.program_id(0)
        dh_v = dh_ref[...].astype(F32)
        dout_v = dres_ref[...] + dskip_ref[...].astype(F32) + dh_v * (1.0 + sc_ref[...])
        _acc_rows(dsc_ref, dh_v * xm_ref[...], i)
        _acc_rows(dsh_ref, dh_v, i)
        yv = y_ref[...]
        _, xhat, rstd = _ln_core(x_ref[...], yv, gate_ref[...], g_ref[...], 0.0)
        du = _ln_bwd_rows(dout_v, xhat, rstd, g_ref[...])
        dres_out[...] = ALPHA * du
        dy_ref[...] = ((1.0 + gate_ref[...]) * du).astype(BF16)
        _acc_rows(dg_ref, dout_v * xhat, i)
        _acc_rows(db_ref, dout_v, i)
        _acc_rows(dgate_ref, du * yv, i)

    return pl.pallas_call(
        body, grid=(L // tr,),
        in_specs=_row_specs(tr, [D] * 4) + [_vec_spec(D)] + _row_specs(tr, [D, D]) + [_vec_spec(D)] * 2,
        out_specs=_row_specs(tr, [D, D]) + [_vec_spec(D)] * 5,
        out_shape=[jax.ShapeDtypeStruct((L, D), F32), jax.ShapeDtypeStruct((L, D), BF16)]
        + [jax.ShapeDtypeStruct((1, D), F32)] * 5,
        compiler_params=_cp(("arbitrary",)), name="mod_ln_bwd")(dres_in, dh, dskip, xmid, scale, x, y, gate, g)


def ln_bwd(dout, x, y, gate, g, name):
    L, D = x.shape
    tr = _tile(L, 256, 16)

    def body(do_ref, x_ref, y_ref, gate_ref, g_ref, dres_ref, dy_ref, dg_ref, db_ref, dgate_ref):
        i = pl.program_id(0)
        yv = y_ref[...]
        dout_v = do_ref[...]
        _, xhat, rstd = _ln_core(x_ref[...], yv, gate_ref[...], g_ref[...], 0.0)
        dxh = dout_v * g_ref[...]
        m1 = jnp.mean(dxh, axis=-1, keepdims=True)
        m2 = jnp.mean(dxh * xhat, axis=-1, keepdims=True)
        du = rstd * (dxh - m1 - xhat * m2)
        dres_ref[...] = ALPHA * du
        dy_ref[...] = ((1.0 + gate_ref[...]) * du).astype(BF16)
        _acc_rows(dg_ref, dout_v * xhat, i)
        _acc_rows(db_ref, dout_v, i)
        _acc_rows(dgate_ref, du * yv, i)

    return pl.pallas_call(
        body, grid=(L // tr,), in_specs=_row_specs(tr, [D, D, D]) + [_vec_spec(D)] * 2,
        out_specs=_row_specs(tr, [D, D]) + [_vec_spec(D)] * 3,
        out_shape=[jax.ShapeDtypeStruct((L, D), F32), jax.ShapeDtypeStruct((L, D), BF16)]
        + [jax.ShapeDtypeStruct((1, D), F32)] * 3,
        compiler_params=_cp(("arbitrary",)), name=name)(dout, x, y, gate, g)


def mod_bwd(dres, dh, dh2, xin, scale, name, through_mod):
    L, D = xin.shape
    tr = _tile(L, 256, 16)

    def body(dres_ref, dh_ref, dh2_ref, x_ref, sc_ref, dx_ref, dsc_ref, dsh_ref):
        i = pl.program_id(0)
        dh_v = dh_ref[...].astype(F32)
        tot = dres_ref[...]
        if through_mod:
            dh_v = dh_v + dh2_ref[...].astype(F32)
        else:
            tot = tot + dh2_ref[...].astype(F32)
        dx_ref[...] = tot + dh_v * (1.0 + sc_ref[...])
        _acc_rows(dsc_ref, dh_v * x_ref[...], i)
        _acc_rows(dsh_ref, dh_v, i)

    return pl.pallas_call(
        body, grid=(L // tr,), in_specs=_row_specs(tr, [D, D, D, D]) + [_vec_spec(D)],
        out_specs=_row_specs(tr, [D]) + [_vec_spec(D)] * 2,
        out_shape=[jax.ShapeDtypeStruct((L, D), F32)] + [jax.ShapeDtypeStruct((1, D), F32)] * 2,
        compiler_params=_cp(("arbitrary",)), name=name)(dres, dh, dh2, xin, scale)


CONV_HALO = 16


def _conv_rows(x_ref, i, tr, L):
    nblk = L // tr
    s = pl.multiple_of(i * tr, CONV_HALO)
    cur = x_ref[pl.ds(s, tr), :].astype(F32)
    sp = pl.multiple_of(jnp.maximum(i * tr - CONV_HALO, 0), CONV_HALO)
    sn = pl.multiple_of(jnp.minimum(i * tr + tr, L - CONV_HALO), CONV_HALO)
    prev = x_ref[pl.ds(sp, CONV_HALO), :].astype(F32) * (i > 0).astype(F32)
    nxt = x_ref[pl.ds(sn, CONV_HALO), :].astype(F32) * (i < nblk - 1).astype(F32)
    return jnp.concatenate([prev, cur, nxt], axis=0)


def _shift_rows(v, j):
    n = v.shape[0]
    return v if j % n == 0 else pltpu.roll(v, j % n, 0)


def _conv_taps(xe):
    return [_shift_rows(xe, CONV_W - 1 - k) for k in range(CONV_W)]


def _conv_eval(taps, w_ref, b_ref):
    c = b_ref[...] + w_ref[0:1, :] * taps[0]
    for k in range(1, CONV_W):
        c = c + w_ref[k:k + 1, :] * taps[k]
    return c


def conv_fwd(zx, col0, conv_w, conv_b):
    L = zx.shape[0]
    C = conv_w.shape[1]
    tc = _tile(C, 512)
    tr = _tile(L, 512, CONV_HALO)
    off = col0 // tc

    def body(x_ref, w_ref, b_ref, o_ref):
        i = pl.program_id(1)
        xe = _conv_rows(x_ref, i, tr, L)
        c = _conv_eval(_conv_taps(xe), w_ref, b_ref)[CONV_HALO:CONV_HALO + tr]
        o_ref[...] = _silu(c).astype(BF16)

    return pl.pallas_call(
        body, grid=(C // tc, L // tr),
        in_specs=[pl.BlockSpec((L, tc), lambda j, i: (0, off + j)), pl.BlockSpec((CONV_W, tc), lambda j, i: (0, j)),
                  pl.BlockSpec((1, tc), lambda j, i: (0, j))],
        out_specs=pl.BlockSpec((tr, tc), lambda j, i: (i, j)),
        out_shape=jax.ShapeDtypeStruct((L, C), BF16), compiler_params=_cp(("parallel", "arbitrary")),
        name="conv_fwd")(zx, conv_w, conv_b)


def conv_bwd(zx, col0, conv_w, conv_b, dxbc):
    L = zx.shape[0]
    C = conv_w.shape[1]
    tc = _tile(C, 512)
    tr = _tile(L, 512, CONV_HALO)
    off = col0 // tc
    H = CONV_HALO

    def body(x_ref, g_ref, w_ref, b_ref, dx_ref, dw_ref, db_ref):
        i = pl.program_id(1)
        xe = _conv_rows(x_ref, i, tr, L)
        ge = _conv_rows(g_ref, i, tr, L)
        taps = _conv_taps(xe)
        dc = ge * _dsilu(_conv_eval(taps, w_ref, b_ref))
        dx = w_ref[CONV_W - 1:CONV_W, :] * dc
        for k in range(CONV_W - 1):
            dx = dx + w_ref[k:k + 1, :] * _shift_rows(dc, -(CONV_W - 1 - k))
        dx_ref[...] = dx[H:H + tr].astype(BF16)
        dcc = dc[H:H + tr]
        rows = [jnp.sum(dcc * taps[k][H:H + tr], axis=0, keepdims=True) for k in range(CONV_W)]
        dwv = jnp.concatenate(rows + [jnp.zeros((8 - CONV_W, tc), F32)], axis=0)
        dbv = jnp.sum(dcc, axis=0, keepdims=True)

        @pl.when(i == 0)
        def _():
            dw_ref[...] = dwv
            db_ref[...] = dbv

        @pl.when(i > 0)
        def _():
            dw_ref[...] += dwv
            db_ref[...] += dbv

    dx, dw, db = pl.pallas_call(
        body, grid=(C // tc, L // tr),
        in_specs=[pl.BlockSpec((L, tc), lambda j, i: (0, off + j)), pl.BlockSpec((L, tc), lambda j, i: (0, j)),
                  pl.BlockSpec((CONV_W, tc), lambda j, i: (0, j)), pl.BlockSpec((1, tc), lambda j, i: (0, j))],
        out_specs=[pl.BlockSpec((tr, tc), lambda j, i: (i, j)), pl.BlockSpec((8, tc), lambda j, i: (0, j)),
                   pl.BlockSpec((1, tc), lambda j, i: (0, j))],
        out_shape=[jax.ShapeDtypeStruct((L, C), BF16), jax.ShapeDtypeStruct((8, C), F32),
                   jax.ShapeDtypeStruct((1, C), F32)],
        compiler_params=_cp(("parallel", "arbitrary")), name="conv_bwd")(zx, dxbc, conv_w, conv_b)
    return dx, dw[:CONV_W], db


_NN = (((1,), (0,)), ((), ()))


def _pieces(x, n):
    out, r = [], x
    for _ in range(n):
        p = r.astype(BF16)
        out.append(p)
        r = r - p.astype(F32)
    return out


def _dot01(a, b01, n, dims=_NN):
    b = b01.astype(BF16)
    return functools.reduce(lambda u, v: u + v,
                            [lax.dot_general(p, b, dims, preferred_element_type=F32) for p in _pieces(a, n)])


def _dot01_left(a01, b, n, dims=_NN):
    a = a01.astype(BF16)
    return functools.reduce(lambda u, v: u + v,
                            [lax.dot_general(a, p, dims, preferred_element_type=F32) for p in _pieces(b, n)])


def _ssd_common(dtp_ref, dtpT_ref, bias_ref, biasT_ref, alog_ref, alogT_ref, b_ref, c_ref):
    Q = SSD_Q
    dt = _softplus(dtp_ref[...] + bias_ref[...])
    A = -jnp.exp(alog_ref[...])
    row = lax.broadcasted_iota(jnp.int32, (Q, Q), 0)
    col = lax.broadcasted_iota(jnp.int32, (Q, Q), 1)
    causal = row >= col
    tril = causal.astype(F32)
    Kh = dt.shape[1]
    acum = _dot01_left(tril, dt * A, 3)
    eye = (lax.broadcasted_iota(jnp.int32, (Kh, Kh), 0) == lax.broadcasted_iota(jnp.int32, (Kh, Kh), 1)).astype(F32)
    acumT = _dot01_left(eye, acum, 3, dims=(((1,), (1,)), ((), ())))
    Bm = b_ref[...]
    Cm = c_ref[...]
    cb = lax.dot_general(Cm, Bm, (((1,), (1,)), ((), ())), preferred_element_type=F32)
    return dt, A, causal, row, col, acum, acumT, Bm, Cm, cb


def _ssd_in_specs(Q, GP, N, Kh, DI, cmap):
    nb0 = DI // N
    vec = pl.BlockSpec((None, 1, Kh), lambda g, c: (g, 0, 0))
    vecT = pl.BlockSpec((None, Kh, 1), lambda g, c: (g, 0, 0))
    return [pl.BlockSpec((Q, GP), lambda g, c: (cmap(c), g)),
            pl.BlockSpec((Q, N), lambda g, c: (cmap(c), nb0 + g)),
            pl.BlockSpec((Q, N), lambda g, c: (cmap(c), nb0 + SSD_G + g)),
            pl.BlockSpec((None, Q, Kh), lambda g, c: (g, cmap(c), 0)),
            pl.BlockSpec((None, Kh, Q), lambda g, c: (g, 0, cmap(c))),
            vec, vecT, vec, vecT, vec, vecT]


def _hi(a, b01):
    return _dot01(a, b01, 2)


def _headsum(a, b01):
    return _dot01(a, b01, 1)


def _ssd_heads(dskT_ref, acum, acumT, dt, Kh):
    Q, P, N = SSD_Q, SSD_P, SSD_N
    GP = Kh * P
    sh_p = P.bit_length() - 1
    seg = lambda shape, dim: lax.shift_right_logical(lax.broadcasted_iota(jnp.int32, shape, dim), sh_p)
    E = (seg((Kh, GP), 1) == lax.broadcasted_iota(jnp.int32, (Kh, GP), 0)).astype(F32)
    ET = (seg((GP, Kh), 0) == lax.broadcasted_iota(jnp.int32, (GP, Kh), 1)).astype(F32)
    a_last = acum[Q - 1:Q, :]
    tail = jnp.exp(a_last - acum)
    eLT = jnp.exp(acumT[:, Q - 1:Q])
    rowseg = seg((GP, N), 0)
    eL_b = jnp.zeros((GP, N), F32)
    for k in range(Kh):
        eL_b = jnp.where(rowseg == k, eLT[k:k + 1, :], eL_b)
    return dict(
        E=E, ET=ET, a_last=a_last, tail=tail, eL_b=eL_b,
        dt_all=_hi(dt, E), ea_all=_hi(jnp.exp(acum), E), tail_all=_hi(tail, E),
        dsk_all=jnp.sum(E * dskT_ref[...], axis=0, keepdims=True))


def _head_chunks(GP):
    CW = min(GP, 128)
    return CW, CW // SSD_P, GP // CW


def _head_mask(Q, CW, kk):
    lane = lax.broadcasted_iota(jnp.int32, (Q, CW), 1)
    return jnp.logical_and(lane >= kk * SSD_P, lane < (kk + 1) * SSD_P)


def ssd_fwd(xbc, dtp_g, dtp_gT, bias_g, bias_gT, alog_g, alog_gT, dsk_g, dsk_gT, DI):
    L = xbc.shape[0]
    Q, P, N, G = SSD_Q, SSD_P, SSD_N, SSD_G
    GP = DI // G
    Kh = GP // P
    nc = L // Q

    CW, hpc, nch = _head_chunks(GP)
    nt = (((1,), (1,)), ((), ()))
    tn = (((0,), (0,)), ((), ()))

    def body(xs_ref, b_ref, c_ref, dtp_ref, dtpT_ref, bias_ref, biasT_ref, alog_ref, alogT_ref, dsk_ref, dskT_ref,
             y_ref, st_ref, state):
        @pl.when(pl.program_id(1) == 0)
        def _():
            state[...] = jnp.zeros(state.shape, F32)

        st_ref[...] = state[...]
        dt, A, causal, row, col, acum, acumT, Bm, Cm, cb = _ssd_common(
            dtp_ref, dtpT_ref, bias_ref, biasT_ref, alog_ref, alogT_ref, b_ref, c_ref)
        hd = _ssd_heads(dskT_ref, acum, acumT, dt, Kh)
        xs = xs_ref[...].astype(F32)
        xdt_all = xs * hd["dt_all"]
        S_all = state[...]
        y_all = (lax.dot_general(Cm, S_all.astype(BF16), nt, preferred_element_type=F32) * hd["ea_all"]
                 + xs * hd["dsk_all"])
        state[...] = S_all * hd["eL_b"] + lax.dot_general(
            (xdt_all * hd["tail_all"]).astype(BF16), Bm, tn, preferred_element_type=F32)
        for ch in range(nch):
            cs = slice(ch * CW, (ch + 1) * CW)
            xc = xdt_all[:, cs]
            acc = y_all[:, cs]
            for kk in range(hpc):
                k = ch * hpc + kk
                decay = jnp.exp(jnp.where(causal, acum[:, k:k + 1] - acumT[k:k + 1, :], -jnp.inf))
                xk = xc if hpc == 1 else jnp.where(_head_mask(Q, CW, kk), xc, 0.0)
                acc = acc + jnp.dot((cb * decay).astype(BF16), xk.astype(BF16), preferred_element_type=F32)
            y_ref[:, cs] = acc.astype(BF16)

    return pl.pallas_call(
        body, grid=(G, nc), in_specs=_ssd_in_specs(Q, GP, N, Kh, DI, lambda c: c),
        out_specs=[pl.BlockSpec((Q, GP), lambda g, c: (c, g)),
                   pl.BlockSpec((None, None, GP, N), lambda g, c: (c, g, 0, 0))],
        out_shape=[jax.ShapeDtypeStruct((L, DI), BF16), jax.ShapeDtypeStruct((nc, G, GP, N), F32)],
        scratch_shapes=[pltpu.VMEM((GP, N), F32)], compiler_params=_cp(("parallel", "arbitrary")),
        name="ssd_fwd")(xbc, xbc, xbc, dtp_g, dtp_gT, bias_g, bias_gT, alog_g, alog_gT, dsk_g, dsk_gT)


def ssd_bwd(xbc, dtp_g, dtp_gT, bias_g, bias_gT, alog_g, alog_gT, dsk_g, dsk_gT, states, dy, DI):
    L = xbc.shape[0]
    Q, P, N, G = SSD_Q, SSD_P, SSD_N, SSD_G
    GP = DI // G
    Kh = GP // P
    nc = L // Q
    rev = lambda c: nc - 1 - c

    CW, hpc, nch = _head_chunks(GP)

    def body(xs_ref, b_ref, c_ref, dtp_ref, dtpT_ref, bias_ref, biasT_ref, alog_ref, alogT_ref, dsk_ref, dskT_ref,
             st_ref, dy_ref, dxs_ref, dB_ref, dC_ref, ddtp_ref, dbias_ref, dalog_ref, dD_ref, dstate):
        ci = pl.program_id(1)

        @pl.when(ci == 0)
        def _():
            dstate[...] = jnp.zeros(dstate.shape, F32)

        dt, A, causal, row, col, acum, acumT, Bm, Cm, cb = _ssd_common(
            dtp_ref, dtpT_ref, bias_ref, biasT_ref, alog_ref, alogT_ref, b_ref, c_ref)
        tn = (((0,), (0,)), ((), ()))
        nt = (((1,), (1,)), ((), ()))
        hd = _ssd_heads(dskT_ref, acum, acumT, dt, Kh)
        ET, tail = hd["ET"], hd["tail"]
        cbT = lax.dot_general(Bm, Cm, nt, preferred_element_type=F32)
        causalT = row <= col
        xs = xs_ref[...].astype(F32)
        xdt_all = xs * hd["dt_all"]
        dyb = dy_ref[...]
        dy_all = dyb.astype(F32)
        S_all = st_ref[...]
        S_b = S_all.astype(BF16)
        dS_all = dstate[...]
        dS_b = dS_all.astype(BF16)
        CS_all = lax.dot_general(Cm, S_b, nt, preferred_element_type=F32)
        dyE_b = (dy_all * hd["ea_all"]).astype(BF16)
        dC_acc = jnp.dot(dyE_b, S_b, preferred_element_type=F32)
        dS_y = lax.dot_general(dyE_b, Cm, tn, preferred_element_type=F32)
        BdS_all = lax.dot_general(Bm, dS_b, nt, preferred_element_type=F32)
        dB_acc = jnp.dot((xdt_all * hd["tail_all"]).astype(BF16), dS_b, preferred_element_type=F32)
        dtail = _headsum(xdt_all * BdS_all, ET)
        da_cols = _headsum(dy_all * CS_all * hd["ea_all"], ET) - dtail * tail
        dss = _dot01_left(jnp.ones((8, N), F32), _dot01_left(hd["E"], dS_all * S_all, 2), 2, dims=nt)
        da_last = dss[0:1] * jnp.exp(hd["a_last"]) + jnp.sum(dtail * tail, axis=0, keepdims=True)
        rowi = lax.broadcasted_iota(jnp.int32, (Q, Kh), 0)
        da_cols = da_cols + jnp.where(rowi == Q - 1, da_last, 0.0)
        dstate[...] = hd["eL_b"] * dS_all + dS_y
        sum_mg = jnp.zeros((Q, Q), F32)
        ddt_x = jnp.zeros((Q, Kh), F32)
        da_rows = jnp.zeros((Kh, Q), F32)
        lane_k = lax.broadcasted_iota(jnp.int32, (Q, Kh), 1)
        sub_k = lax.broadcasted_iota(jnp.int32, (Kh, Q), 0)
        for ch in range(nch):
            cs = slice(ch * CW, (ch + 1) * CW)
            dyc = dyb[:, cs]
            xc_b = xdt_all[:, cs].astype(BF16)
            acc = hd["tail_all"][:, cs] * BdS_all[:, cs]
            for kk in range(hpc):
                k = ch * hpc + kk
                a_b = jnp.broadcast_to(acum[:, k:k + 1], (Q, Q))
                a_r = acumT[k:k + 1, :]
                decay = jnp.exp(jnp.where(causal, a_b - a_r, -jnp.inf))
                decayT = jnp.exp(jnp.where(causalT, a_r - a_b, -jnp.inf))
                dyk = dyc if hpc == 1 else jnp.where(_head_mask(Q, CW, kk), dyc, jnp.zeros_like(dyc))
                mg = decay * lax.dot_general(dyk, xc_b, nt, preferred_element_type=F32)
                sum_mg = sum_mg + mg
                w = mg * cb
                da_cols = da_cols + jnp.where(lane_k == k, jnp.sum(w, axis=1, keepdims=True), 0.0)
                da_rows = da_rows + jnp.where(sub_k == k, jnp.sum(w, axis=0, keepdims=True), 0.0)
                acc = acc + jnp.dot((decayT * cbT).astype(BF16), dyk, preferred_element_type=F32)
            dxs_ref[:, cs] = (acc * hd["dt_all"][:, cs] + dy_all[:, cs] * hd["dsk_all"][:, cs]).astype(BF16)
            ddt_x = ddt_x + _headsum(acc * xs[:, cs], ET[cs, :])
        eye_q = (row == col).astype(F32)
        da_cols = da_cols - _dot01_left(eye_q, da_rows, 3, dims=nt)
        dD_row = jnp.sum(_headsum(dy_all * xs, ET), axis=0, keepdims=True)
        sum_mg_b = sum_mg.astype(BF16)
        dB_ref[...] = (dB_acc + lax.dot_general(sum_mg_b, Cm, tn, preferred_element_type=F32)).astype(BF16)
        dC_ref[...] = (dC_acc + jnp.dot(sum_mg_b, Bm, preferred_element_type=F32)).astype(BF16)
        triu = (row <= col).astype(F32)
        ddtA = _dot01_left(triu, da_cols, 3)
        ddt = ddt_x + ddtA * A
        dpre = ddt * _sigmoid(dtp_ref[...] + bias_ref[...])
        ddtp_ref[...] = dpre
        dbias_v = jnp.sum(dpre, axis=0, keepdims=True)
        dalog_v = jnp.sum(ddtA * dt, axis=0, keepdims=True) * A

        @pl.when(ci == 0)
        def _():
            dbias_ref[...] = dbias_v
            dalog_ref[...] = dalog_v
            dD_ref[...] = dD_row

        @pl.when(ci > 0)
        def _():
            dbias_ref[...] += dbias_v
            dalog_ref[...] += dalog_v
            dD_ref[...] += dD_row

    vec_o = pl.BlockSpec((None, 1, Kh), lambda g, c: (g, 0, 0))
    return pl.pallas_call(
        body, grid=(G, nc),
        in_specs=_ssd_in_specs(Q, GP, N, Kh, DI, rev)
        + [pl.BlockSpec((None, None, GP, N), lambda g, c: (rev(c), g, 0, 0)),
           pl.BlockSpec((Q, GP), lambda g, c: (rev(c), g))],
        out_specs=[pl.BlockSpec((Q, GP), lambda g, c: (rev(c), g)), pl.BlockSpec((Q, N), lambda g, c: (rev(c), g)),
                   pl.BlockSpec((Q, N), lambda g, c: (rev(c), g)),
                   pl.BlockSpec((None, Q, Kh), lambda g, c: (g, rev(c), 0)), vec_o, vec_o, vec_o],
        out_shape=[jax.ShapeDtypeStruct((L, DI), BF16), jax.ShapeDtypeStruct((L, G * N), BF16),
                   jax.ShapeDtypeStruct((L, G * N), BF16), jax.ShapeDtypeStruct((G, L, Kh), F32)]
        + [jax.ShapeDtypeStruct((G, 1, Kh), F32)] * 3,
        scratch_shapes=[pltpu.VMEM((GP, N), F32)], compiler_params=_cp(("parallel", "arbitrary")),
        name="ssd_bwd")(xbc, xbc, xbc, dtp_g, dtp_gT, bias_g, bias_gT, alog_g, alog_gT, dsk_g, dsk_gT, states, dy)


def _rms_groups(y2, ng_ref, DI):
    S = DI // SSD_G
    for g in range(SSD_G):
        gs = slice(g * S, (g + 1) * S)
        seg = y2[:, gs]
        r = lax.rsqrt(jnp.mean(seg * seg, axis=-1, keepdims=True) + RMS_EPS)
        yield gs, seg * r, r, ng_ref[:, gs]


def rms_gate_fwd(y, zx, norm_g):
    L, DI = y.shape
    tr = _tile(L, 256, 16)

    def body(y_ref, z_ref, ng_ref, o_ref):
        y2 = y_ref[...].astype(F32) * _silu(z_ref[...].astype(F32))
        for gs, yh, _, ng in _rms_groups(y2, ng_ref, DI):
            o_ref[:, gs] = (yh * ng).astype(BF16)

    return pl.pallas_call(
        body, grid=(L // tr,), in_specs=_row_specs(tr, [DI, DI]) + [_vec_spec(DI)], out_specs=_row_specs(tr, [DI])[0],
        out_shape=jax.ShapeDtypeStruct((L, DI), BF16), compiler_params=_cp(("parallel",)),
        name="rms_gate_fwd")(y, zx, norm_g)


def rms_gate_bwd(dyn, y, zx, norm_g):
    L, DI = y.shape
    tr = _tile(L, 256, 16)

    def body(dyn_ref, y_ref, z_ref, ng_ref, dy_ref, dz_ref, dng_ref):
        i = pl.program_id(0)
        yv = y_ref[...].astype(F32)
        zv = z_ref[...].astype(F32)
        sz = _silu(zv)
        dsz = _dsilu(zv)
        dynv = dyn_ref[...].astype(F32)
        for gs, yh, r, ng in _rms_groups(yv * sz, ng_ref, DI):
            dyh = dynv[:, gs] * ng
            dy2 = r * (dyh - yh * jnp.mean(dyh * yh, axis=-1, keepdims=True))
            dy_ref[:, gs] = (dy2 * sz[:, gs]).astype(BF16)
            dz_ref[:, gs] = (dy2 * yv[:, gs] * dsz[:, gs]).astype(BF16)
            s = jnp.sum(dynv[:, gs] * yh, axis=0, keepdims=True)

            @pl.when(i == 0)
            def _():
                dng_ref[:, gs] = s

            @pl.when(i > 0)
            def _():
                dng_ref[:, gs] += s

    return pl.pallas_call(
        body, grid=(L // tr,), in_specs=_row_specs(tr, [DI, DI, DI]) + [_vec_spec(DI)],
        out_specs=_row_specs(tr, [DI, DI]) + [_vec_spec(DI)],
        out_shape=[jax.ShapeDtypeStruct((L, DI), BF16)] * 2 + [jax.ShapeDtypeStruct((1, DI), F32)],
        compiler_params=_cp(("arbitrary",)), name="rms_gate_bwd")(dyn, y, zx, norm_g)


def _alibi_slope(gi, h):
    n = len(DIL_PATTERNS) * DIL_H
    return float(2.0 ** (-8.0 * (gi * DIL_H + h + 1) / n))


def _attn_masks():
    qi = lax.broadcasted_iota(jnp.int32, (DIL_BLK, DIL_BLK), 0)
    kj = lax.broadcasted_iota(jnp.int32, (DIL_BLK, DIL_BLK), 1)
    dcur = (qi - kj).astype(F32)
    return dcur, qi >= kj, dcur + float(DIL_BLK), kj >= qi


def attn_fwd(q3, kv3, gi):
    window, d = DIL_PATTERNS[gi]
    assert window // d == DIL_BLK
    HW = DIL_H * DIL_E
    M = q3.shape[1]
    nb = M // DIL_BLK
    scale = DIL_E ** -0.5
    nt = (((1,), (1,)), ((), ()))

    def body(q_ref, kp_ref, kc_ref, vp_ref, vc_ref, o_ref, lse_ref):
        n = pl.program_id(1)
        dcur, vcur, dprev, vprev0 = _attn_masks()
        dist = jnp.concatenate([dprev, dcur], axis=1)
        valid = jnp.concatenate([jnp.logical_and(vprev0, n > 0), vcur], axis=1)
        lane = lax.broadcasted_iota(jnp.int32, (DIL_BLK, 128), 1)
        lse_acc = jnp.zeros((DIL_BLK, 128), F32)
        for h in range(DIL_H):
            hs = slice(h * DIL_E, (h + 1) * DIL_E)
            sl = _alibi_slope(gi, h) * d
            kcat = jnp.concatenate([kp_ref[:, hs], kc_ref[:, hs]], axis=0)
            vcat = jnp.concatenate([vp_ref[:, hs], vc_ref[:, hs]], axis=0)
            s = lax.dot_general(q_ref[:, hs], kcat, nt, preferred_element_type=F32) * scale - sl * dist
            s = jnp.where(valid, s, -jnp.inf)
            m = jnp.max(s, axis=-1, keepdims=True)
            p = jnp.exp(s - m)
            den = jnp.sum(p, axis=-1, keepdims=True)
            o = jnp.dot(p.astype(BF16), vcat, preferred_element_type=F32) / den
            o_ref[:, hs] = o.astype(BF16)
            lse_acc = jnp.where(lane == h, m + jnp.log(den), lse_acc)
        lse_ref[...] = lse_acc

    blk = (None, DIL_BLK, HW)
    prev = lambda n: jnp.maximum(n - 1, 0)
    return pl.pallas_call(
        body, grid=(d, nb),
        in_specs=[pl.BlockSpec(blk, lambda r, n: (r, n, 0)),
                  pl.BlockSpec(blk, lambda r, n: (r, prev(n), 0)), pl.BlockSpec(blk, lambda r, n: (r, n, 0)),
                  pl.BlockSpec(blk, lambda r, n: (r, prev(n), 1)), pl.BlockSpec(blk, lambda r, n: (r, n, 1))],
        out_specs=[pl.BlockSpec(blk, lambda r, n: (r, n, 0)), pl.BlockSpec((None, DIL_BLK, 128), lambda r, n: (r, n, 0))],
        out_shape=[jax.ShapeDtypeStruct((d, M, HW), BF16), jax.ShapeDtypeStruct((d, M, 128), F32)],
        compiler_params=_cp(("parallel", "parallel")), name=f"attn_fwd_{gi}")(q3, kv3, kv3, kv3, kv3)


def attn_bwd(q3, kv3, do3, lse3, dpr3, gi):
    window, d = DIL_PATTERNS[gi]
    HW = DIL_H * DIL_E
    M = q3.shape[1]
    L = M * d
    nb = M // DIL_BLK
    scale = DIL_E ** -0.5
    nt = (((1,), (1,)), ((), ()))
    tn = (((0,), (0,)), ((), ()))

    def body(q0_ref, q1_ref, k_ref, v_ref, do0_ref, do1_ref, l0_ref, l1_ref, r0_ref, r1_ref,
             dq_ref, dk_ref, dv_ref, carry):
        n = pl.program_id(1)

        @pl.when(n == 0)
        def _():
            carry[...] = jnp.zeros(carry.shape, F32)

        dcur, vcur, dprev, vprev0 = _attn_masks()
        dist = jnp.concatenate([dcur, dprev], axis=0)
        valid = jnp.concatenate([vcur, jnp.logical_and(vprev0, n < nb - 1)], axis=0)
        B = DIL_BLK
        for h in range(DIL_H):
            hs = slice(h * DIL_E, (h + 1) * DIL_E)
            sl = _alibi_slope(gi, h) * d
            kh = k_ref[:, hs]
            vh = v_ref[:, hs]
            qcat = jnp.concatenate([q0_ref[:, hs], q1_ref[:, hs]], axis=0)
            docat = jnp.concatenate([do0_ref[:, hs], do1_ref[:, hs]], axis=0)
            lcat = jnp.concatenate([l0_ref[:, h:h + 1], l1_ref[:, h:h + 1]], axis=0)
            rcat = jnp.concatenate([r0_ref[:, h:h + 1], r1_ref[:, h:h + 1]], axis=0)
            s = lax.dot_general(qcat, kh, nt, preferred_element_type=F32) * scale - sl * dist
            p = jnp.exp(jnp.where(valid, s - lcat, -jnp.inf))
            ds = p * (lax.dot_general(docat, vh, nt, preferred_element_type=F32) - rcat)
            ds_b = (ds * scale).astype(BF16)
            dv_ref[:, hs] = lax.dot_general(p.astype(BF16), docat, tn, preferred_element_type=F32).astype(BF16)
            dk_ref[:, hs] = lax.dot_general(ds_b, qcat, tn, preferred_element_type=F32).astype(BF16)
            dqc = jnp.dot(ds_b, kh, preferred_element_type=F32)
            dq_ref[:, hs] = (carry[:, hs] + dqc[:B]).astype(BF16)
            carry[:, hs] = dqc[B:]

    blk = (None, DIL_BLK, HW)
    sblk = (None, DIL_BLK, 128)
    oblk = (DIL_BLK, HW)
    nxt = lambda n: jnp.minimum(n + 1, nb - 1)
    here = lambda c: (lambda r, n: (r, n, c))
    ahead = lambda c: (lambda r, n: (r, nxt(n), c))
    outs = pl.pallas_call(
        body, grid=(d, nb),
        in_specs=[pl.BlockSpec(blk, here(0)), pl.BlockSpec(blk, ahead(0)),
                  pl.BlockSpec(blk, here(0)), pl.BlockSpec(blk, here(1)),
                  pl.BlockSpec(blk, here(0)), pl.BlockSpec(blk, ahead(0)),
                  pl.BlockSpec(sblk, here(0)), pl.BlockSpec(sblk, ahead(0)),
                  pl.BlockSpec(sblk, here(0)), pl.BlockSpec(sblk, ahead(0))],
        out_specs=[pl.BlockSpec(oblk, lambda r, n: (n, r))] * 3,
        out_shape=[jax.ShapeDtypeStruct((M, d * HW), BF16)] * 3,
        scratch_shapes=[pltpu.VMEM(oblk, F32)], compiler_params=_cp(("parallel", "arbitrary")),
        name=f"attn_bwd_{gi}")(q3, q3, kv3, kv3, do3, do3, lse3, lse3, dpr3, dpr3)
    return [t.reshape(L, HW) for t in outs]


def _merge_weights(l_tiles, h):
    ls = [t[:, h:h + 1] for t in l_tiles]
    mx = functools.reduce(jnp.maximum, ls)
    es = [jnp.exp(l - mx) for l in ls]
    den = functools.reduce(lambda a, b: a + b, es)
    return [e / den for e in es]


def _dil_specs(tr, arrs):
    return [pl.BlockSpec((a.shape[0], tr // a.shape[0], a.shape[2]), lambda i: (0, i, 0)) for a in arrs]


def _dil_scratch(tr, arrs):
    return [pltpu.VMEM((a.shape[2] // 128, tr, 128), F32) for a in arrs if a.shape[0] > 1]


def _undilate(refs3, scrs, tr):
    out, k = [], 0
    for ref in refs3:
        d, _, W = ref.shape
        if d == 1:
            out.append(lambda c, ref=ref: ref[0, :, c * 128:(c + 1) * 128])
            continue
        scr = scrs[k]
        k += 1
        for r in range(d):
            for c in range(W // 128):
                scr.at[c][pl.ds(r, tr // d, stride=d), :] = ref[r, :, c * 128:(c + 1) * 128].astype(F32)
        out.append(lambda c, scr=scr: scr[c])
    return out


def merge_fwd(os3, lses3, z):
    HW = os3[0].shape[2]
    L = os3[0].shape[0] * os3[0].shape[1]
    tr = _tile(L, 256, 16)
    ng = len(os3)
    n_scr = len(_dil_scratch(tr, os3))

    def body(*refs):
        z_ref, out_ref = refs[2 * ng], refs[2 * ng + 1]
        scrs = refs[2 * ng + 2:]
        o_get = _undilate(refs[:ng], scrs[:n_scr], tr)
        l_tiles = [g(0) for g in _undilate(refs[ng:2 * ng], scrs[n_scr:], tr)]
        for h in range(DIL_H):
            hs = slice(h * DIL_E, (h + 1) * DIL_E)
            ws = _merge_weights(l_tiles, h)
            om = functools.reduce(lambda a, b: a + b, [w * o(h).astype(F32) for w, o in zip(ws, o_get)])
            out_ref[:, hs] = (om * _silu(z_ref[:, hs].astype(F32))).astype(BF16)

    return pl.pallas_call(
        body, grid=(L // tr,),
        in_specs=_dil_specs(tr, os3) + _dil_specs(tr, lses3) + _row_specs(tr, [HW]),
        out_specs=_row_specs(tr, [HW])[0], out_shape=jax.ShapeDtypeStruct((L, HW), BF16),
        scratch_shapes=_dil_scratch(tr, os3) + _dil_scratch(tr, lses3),
        compiler_params=_cp(("parallel",)), name="merge_fwd")(*os3, *lses3, z)


def merge_bwd(dgated, os3, lses3, z):
    HW = os3[0].shape[2]
    L = os3[0].shape[0] * os3[0].shape[1]
    tr = _tile(L, 256, 16)
    ng = len(os3)
    n_scr = len(_dil_scratch(tr, os3))

    def body(*refs):
        dg_ref = refs[0]
        z_ref = refs[1 + 2 * ng]
        outs = refs[2 + 2 * ng:2 + 2 * ng + 2 * ng + 1]
        scrs = refs[2 + 2 * ng + 2 * ng + 1:]
        do_out, dpr_out, dz_ref = outs[:ng], outs[ng:2 * ng], outs[2 * ng]
        o_get = _undilate(refs[1:1 + ng], scrs[:n_scr], tr)
        l_tiles = [g(0) for g in _undilate(refs[1 + ng:1 + 2 * ng], scrs[n_scr:2 * n_scr], tr)]
        stage = scrs[2 * n_scr:]
        do_stage, dpr_stage, k = [], [], 0
        for g in range(ng):
            if do_out[g].shape[0] == 1:
                do_stage.append(None)
                dpr_stage.append(None)
            else:
                do_stage.append(stage[2 * k])
                dpr_stage.append(stage[2 * k + 1])
                k += 1
        lane = lax.broadcasted_iota(jnp.int32, (tr, 128), 1)
        accs = [jnp.zeros((tr, 128), F32) for _ in range(ng)]
        for h in range(DIL_H):
            hs = slice(h * DIL_E, (h + 1) * DIL_E)
            ws = _merge_weights(l_tiles, h)
            ov = [o(h).astype(F32) for o in o_get]
            om = functools.reduce(lambda a, b: a + b, [w * o for w, o in zip(ws, ov)])
            zv = z_ref[:, hs].astype(F32)
            dgv = dg_ref[:, hs].astype(F32)
            dom = dgv * _silu(zv)
            dz_ref[:, hs] = (dgv * om * _dsilu(zv)).astype(BF16)
            dws = [jnp.sum(dom * o, axis=-1, keepdims=True) for o in ov]
            dwbar = functools.reduce(lambda a, b: a + b, [w * dw for w, dw in zip(ws, dws)])
            for g in range(ng):
                if do_stage[g] is None:
                    do_out[g][0, :, hs] = (ws[g] * dom).astype(BF16)
                else:
                    do_stage[g][h] = ws[g] * dom
                accs[g] = jnp.where(lane == h, ws[g] * dwbar, accs[g])
        for g in range(ng):
            d = do_out[g].shape[0]
            if d == 1:
                dpr_out[g][0] = accs[g]
                continue
            dpr_stage[g][0] = accs[g]
            for r in range(d):
                dpr_out[g][r] = dpr_stage[g].at[0][pl.ds(r, tr // d, stride=d), :]
                for c in range(HW // 128):
                    do_out[g][r, :, c * 128:(c + 1) * 128] = do_stage[g].at[c][pl.ds(r, tr // d, stride=d), :].astype(BF16)

    stage_shapes = []
    for o3 in os3:
        if o3.shape[0] > 1:
            stage_shapes += [pltpu.VMEM((HW // 128, tr, 128), F32), pltpu.VMEM((1, tr, 128), F32)]
    outs = pl.pallas_call(
        body, grid=(L // tr,),
        in_specs=_row_specs(tr, [HW]) + _dil_specs(tr, os3) + _dil_specs(tr, lses3) + _row_specs(tr, [HW]),
        out_specs=_dil_specs(tr, os3) + _dil_specs(tr, lses3) + _row_specs(tr, [HW]),
        out_shape=[jax.ShapeDtypeStruct(o.shape, BF16) for o in os3] + [jax.ShapeDtypeStruct(l.shape, F32) for l in lses3]
        + [jax.ShapeDtypeStruct((L, HW), BF16)],
        scratch_shapes=_dil_scratch(tr, os3) + _dil_scratch(tr, lses3) + stage_shapes,
        compiler_params=_cp(("parallel",)), name="merge_bwd")(dgated, *os3, *lses3, z)
    return outs[:ng], outs[ng:2 * ng], outs[2 * ng]


def ada_fwd(c8, ada_w):
    nl, D, Ws = ada_w.shape
    tn = _tile(Ws, 512)

    def body(c_ref, w_ref, o_ref):
        o_ref[...] = jnp.dot(_silu(c_ref[...]), w_ref[...], precision=lax.Precision.HIGHEST,
                             preferred_element_type=F32)

    return pl.pallas_call(
        body, grid=(nl, Ws // tn),
        in_specs=[pl.BlockSpec((N_DEV, D), lambda l, j: (0, 0)), pl.BlockSpec((None, D, tn), lambda l, j: (l, 0, j))],
        out_specs=pl.BlockSpec((None, N_DEV, tn), lambda l, j: (l, 0, j)),
        out_shape=jax.ShapeDtypeStruct((nl, N_DEV, Ws), F32), compiler_params=_cp(("parallel", "parallel")),
        name="ada_fwd")(c8, ada_w)


def ada_wgrad(c8t, dmod):
    nl, _, Ws = dmod.shape
    D = c8t.shape[0]
    tm = _tile(D, 512, 8)

    def body(c_ref, d_ref, o_ref):
        sc = _silu(c_ref[...])
        acc = sc[:, 0:1] * d_ref[0:1, :]
        for e in range(1, N_DEV):
            acc = acc + sc[:, e:e + 1] * d_ref[e:e + 1, :]
        o_ref[...] = acc

    return pl.pallas_call(
        body, grid=(nl, D // tm),
        in_specs=[pl.BlockSpec((tm, N_DEV), lambda l, i: (i, 0)), pl.BlockSpec((None, N_DEV, Ws), lambda l, i: (l, 0, 0))],
        out_specs=pl.BlockSpec((None, tm, Ws), lambda l, i: (l, i, 0)),
        out_shape=jax.ShapeDtypeStruct((nl, D, Ws), F32), compiler_params=_cp(("parallel", "parallel")),
        name="ada_wgrad")(c8t, dmod)


def _adamw_math(w, gv, m, v):
    c1 = 1.0 - ADAM_B1 ** ADAM_STEP
    c2 = 1.0 - ADAM_B2 ** ADAM_STEP
    nm = ADAM_B1 * m + (1.0 - ADAM_B1) * gv
    nv = ADAM_B2 * v + (1.0 - ADAM_B2) * (gv * gv)
    return -ADAM_LR * ((nm / c1) / (jnp.sqrt(nv / c2) + ADAM_EPS) + ADAM_WD * w), nm, nv


def adamw_ada(w, c8t, dmod, m, v):
    nl, D, Ws = w.shape
    tr = _tile(D, 256, 8)

    def body(c_ref, d_ref, w_ref, m_ref, v_ref, g_ref, dl_ref, nm_ref, nv_ref):
        sc = _silu(c_ref[...])
        gv = sc[:, 0:1] * d_ref[0:1, :]
        for e in range(1, N_DEV):
            gv = gv + sc[:, e:e + 1] * d_ref[e:e + 1, :]
        g_ref[...] = gv
        dl_ref[...], nm_ref[...], nv_ref[...] = _adamw_math(w_ref[...], gv, m_ref[...], v_ref[...])

    blk = pl.BlockSpec((None, tr, Ws), lambda l, i: (l, i, 0))
    return pl.pallas_call(
        body, grid=(nl, D // tr),
        in_specs=[pl.BlockSpec((tr, N_DEV), lambda l, i: (i, 0)), pl.BlockSpec((None, N_DEV, Ws), lambda l, i: (l, 0, 0)),
                  blk, blk, blk],
        out_specs=[blk] * 4, out_shape=[jax.ShapeDtypeStruct((nl, D, Ws), F32)] * 4,
        compiler_params=_cp(("parallel", "parallel")), name="adamw_ada_w")(c8t, dmod, w, m, v)


def adamw(w, g, m, v, name):
    R, C = w.shape
    tr = _tile(R, 256, 8)
    c1 = 1.0 - ADAM_B1 ** ADAM_STEP
    c2 = 1.0 - ADAM_B2 ** ADAM_STEP

    def body(w_ref, g_ref, m_ref, v_ref, d_ref, nm_ref, nv_ref):
        gv = g_ref[...]
        nm = ADAM_B1 * m_ref[...] + (1.0 - ADAM_B1) * gv
        nv = ADAM_B2 * v_ref[...] + (1.0 - ADAM_B2) * (gv * gv)
        nm_ref[...] = nm
        nv_ref[...] = nv
        d_ref[...] = -ADAM_LR * ((nm / c1) / (jnp.sqrt(nv / c2) + ADAM_EPS) + ADAM_WD * w_ref[...])

    return pl.pallas_call(
        body, grid=(R // tr,), in_specs=_row_specs(tr, [C] * 4), out_specs=_row_specs(tr, [C] * 3),
        out_shape=[jax.ShapeDtypeStruct((R, C), F32)] * 3, compiler_params=_cp(("parallel",)), name=name)(w, g, m, v)


def sum_leading(t, name, out_dtype=F32):
    S, R, C = t.shape
    tr = _tile(R, 256, 16)

    def body(t_ref, o_ref):
        acc = t_ref[0].astype(F32)
        for s in range(1, S):
            acc = acc + t_ref[s].astype(F32)
        o_ref[...] = acc.astype(out_dtype)

    return pl.pallas_call(
        body, grid=(R // tr,), in_specs=[pl.BlockSpec((S, tr, C), lambda i: (0, i, 0))],
        out_specs=pl.BlockSpec((tr, C), lambda i: (i, 0)), out_shape=jax.ShapeDtypeStruct((R, C), out_dtype),
        compiler_params=_cp(("parallel",)), name=name)(t)


def add_half(g, a, core, name, by_cols=False):
    S, R, C = g.shape

    def body(core_ref, g_ref, a_ref, o_ref):
        o_ref[...] = (g_ref[...].astype(F32) + a_ref[...].astype(F32)).astype(BF16)

    if by_cols:
        hc = C // 2
        tr = _tile(R, 256, 16)
        return pl.pallas_call(
            body,
            grid_spec=pltpu.PrefetchScalarGridSpec(
                num_scalar_prefetch=1, grid=(S, R // tr),
                in_specs=[pl.BlockSpec((None, tr, hc), lambda s, i, core_ref: (s, i, core_ref[0])),
                          pl.BlockSpec((None, tr, hc), lambda s, i, core_ref: (s, i, 0))],
                out_specs=pl.BlockSpec((None, tr, hc), lambda s, i, core_ref: (s, i, 0))),
            out_shape=jax.ShapeDtypeStruct((S, R, hc), BF16), compiler_params=_cp(("parallel", "parallel")),
            name=name)(core, g, a)
    h = R // 2
    tr = _tile(h, 256, 16)
    nb = h // tr

    return pl.pallas_call(
        body,
        grid_spec=pltpu.PrefetchScalarGridSpec(
            num_scalar_prefetch=1, grid=(S, nb),
            in_specs=[pl.BlockSpec((None, tr, C), lambda s, i, core_ref: (s, core_ref[0] * nb + i, 0)),
                      pl.BlockSpec((None, tr, C), lambda s, i, core_ref: (s, i, 0))],
            out_specs=pl.BlockSpec((None, tr, C), lambda s, i, core_ref: (s, i, 0))),
        out_shape=jax.ShapeDtypeStruct((S, h, C), BF16), compiler_params=_cp(("parallel", "parallel")),
        name=name)(core, g, a)


def sum_partials(own, landed, chip, name):
    _, h, C = own.shape
    tr = _tile(h, 256, 16)

    def body(chip_ref, own_ref, l_ref, o_ref):
        acc = own_ref[...].astype(F32)
        for j in range(3):
            acc = acc + l_ref[j].astype(F32)
        o_ref[...] = acc

    return pl.pallas_call(
        body,
        grid_spec=pltpu.PrefetchScalarGridSpec(
            num_scalar_prefetch=1, grid=(h // tr,),
            in_specs=[pl.BlockSpec((None, tr, C), lambda i, chip_ref: (chip_ref[0], i, 0)),
                      pl.BlockSpec((3, tr, C), lambda i, chip_ref: (0, i, 0))],
            out_specs=pl.BlockSpec((tr, C), lambda i, chip_ref: (i, 0))),
        out_shape=jax.ShapeDtypeStruct((h, C), F32), compiler_params=_cp(("parallel",)), name=name)(chip, own, landed)


def adamw_halves(w, g_mine, g_theirs, m, v, core, name):
    R, C = w.shape
    h = R // 2
    tr = _tile(h, 256, 8)
    nbh = h // tr
    c1 = 1.0 - ADAM_B1 ** ADAM_STEP
    c2 = 1.0 - ADAM_B2 ** ADAM_STEP

    def body(core_ref, w_ref, gm_ref, gt_ref, m_ref, v_ref, g_ref, d_ref, nm_ref, nv_ref):
        mine = (pl.program_id(0) // nbh) == core_ref[0]
        gv = jnp.where(mine, gm_ref[...], gt_ref[...])
        g_ref[...] = gv
        nm = ADAM_B1 * m_ref[...] + (1.0 - ADAM_B1) * gv
        nv = ADAM_B2 * v_ref[...] + (1.0 - ADAM_B2) * (gv * gv)
        nm_ref[...] = nm
        nv_ref[...] = nv
        d_ref[...] = -ADAM_LR * ((nm / c1) / (jnp.sqrt(nv / c2) + ADAM_EPS) + ADAM_WD * w_ref[...])

    full = pl.BlockSpec((tr, C), lambda i, core_ref: (i, 0))
    halfspec = pl.BlockSpec((tr, C), lambda i, core_ref: (i % nbh, 0))
    return pl.pallas_call(
        body,
        grid_spec=pltpu.PrefetchScalarGridSpec(
            num_scalar_prefetch=1, grid=(2 * nbh,), in_specs=[full, halfspec, halfspec, full, full],
            out_specs=[full] * 4),
        out_shape=[jax.ShapeDtypeStruct((R, C), F32)] * 4, compiler_params=_cp(("parallel",)),
        name=name)(core, w, g_mine, g_theirs, m, v)


_ANY = pl.BlockSpec(memory_space=pl.ANY)


def _place():
    x, y, c = lax.axis_index("x"), lax.axis_index("y"), lax.axis_index("c")
    chips = [(1 - x, y), (x, 1 - y), (1 - x, 1 - y)]
    return x, y, c, chips


def allgather_small(v, name, after=None):
    R, W = v.shape
    extra = [] if after is None else [after]

    def body(x_ref, *rest):
        out_ref, send_sems, recv_sems, local_sem = rest[len(extra):]
        x, y, c, chips = _place()
        me, sibling = (x, y, c), (x, y, 1 - c)

        def rows(px, py, pc):
            return out_ref.at[pl.ds((4 * px + 2 * py + pc) * R, R), :]

        def copy(k, block, to, src=None):
            return pltpu.make_async_remote_copy(
                src_ref=rows(*block) if src is None else src, dst_ref=rows(*block),
                send_sem=send_sems.at[k], recv_sem=recv_sems.at[k], device_id=to, device_id_type=MESH)

        mine = pltpu.make_async_copy(x_ref, rows(*me), local_sem)
        mine.start()
        first = [copy(0, me, sibling, src=x_ref)]
        first += [copy(1 + j, me, (*chip, c), src=x_ref) for j, chip in enumerate(chips)]
        for cp in first:
            cp.start()
        passed = [copy(4 + j, (*chip, c), sibling) for j, chip in enumerate(chips)]
        for j, chip in enumerate(chips):
            copy(1 + j, (*chip, c), me).wait_recv()
            passed[j].start()
        copy(0, sibling, me).wait_recv()
        for j, chip in enumerate(chips):
            copy(4 + j, (*chip, 1 - c), me).wait_recv()
        for cp in first + passed:
            cp.wait_send()
        mine.wait()

    return pl.pallas_call(
        body, out_shape=jax.ShapeDtypeStruct((N_DEV * R, W), v.dtype),
        in_specs=[pl.BlockSpec(memory_space=pltpu.VMEM)] + [_ANY] * len(extra),
        out_specs=pl.BlockSpec(memory_space=pltpu.VMEM),
        scratch_shapes=[pltpu.SemaphoreType.DMA((7,)), pltpu.SemaphoreType.DMA((7,)), pltpu.SemaphoreType.DMA],
        name=name)(v, *extra)


def allgather_weights(shards, name="allgather_weights", by_cols=False):
    n = len(shards)

    def body(*refs):
        ins, outs = refs[:n], refs[n:2 * n]
        send_sems, recv_sems = refs[2 * n:]
        x, y, c, chips = _place()
        p = 2 * x + y
        sibling = (x, y, 1 - c)

        def half(i, chip_id, core, ref=None):
            r = outs[i].at[chip_id] if ref is None else ref
            if by_cols:
                hc = r.shape[1] // 2
                return r.at[:, pl.ds(pl.multiple_of(core * hc, 128), hc)]
            return r.at[core]

        def copy(i, k, chip_id, core, to, src=None):
            return pltpu.make_async_remote_copy(
                src_ref=half(i, chip_id, core) if src is None else src, dst_ref=half(i, chip_id, core),
                send_sem=send_sems.at[6 * i + k], recv_sem=recv_sems.at[6 * i + k], device_id=to, device_id_type=MESH)

        first = [copy(i, j, p, c, (*chip, c), src=half(i, p, c, ref=ins[i]))
                 for i in range(n) for j, chip in enumerate(chips)]
        for cp in first:
            cp.start()
        passed = []
        for i in range(n):
            for j, (cx, cy) in enumerate(chips):
                copy(i, j, 2 * cx + cy, c, sibling).wait_recv()
                fw = copy(i, 3 + j, 2 * cx + cy, c, sibling)
                fw.start()
                passed.append(fw)
        for i in range(n):
            for j, (cx, cy) in enumerate(chips):
                copy(i, 3 + j, 2 * cx + cy, 1 - c, sibling).wait_recv()
        for cp in first + passed:
            cp.wait_send()

    split = list(shards) if by_cols else [s.reshape(2, s.shape[0] // 2, s.shape[1]) for s in shards]
    outs = pl.pallas_call(
        body, out_shape=[jax.ShapeDtypeStruct((N_CHIPS,) + s.shape, s.dtype) for s in split],
        in_specs=[_ANY] * n, out_specs=[_ANY] * n,
        scratch_shapes=[pltpu.SemaphoreType.DMA((6 * n,)), pltpu.SemaphoreType.DMA((6 * n,))],
        name=name)(*split)
    chip = 2 * lax.axis_index("x") + lax.axis_index("y")
    return [lax.dynamic_update_index_in_dim(o, s, chip, 0).reshape((N_CHIPS,) + sh.shape)
            for o, s, sh in zip(outs, split, shards)]


_HBM = pl.BlockSpec(memory_space=pltpu.HBM)
_SEM = pl.BlockSpec(memory_space=pltpu.SEMAPHORE)
_EFFECT = pltpu.SideEffectType.DATAFLOW_SIDE_EFFECTING


def _chip_copies(kind, srcs, lands, send_sems, recv_sems):
    x, y, c, chips = _place()
    p = 2 * x + y
    cps = []
    if kind == "sibling":
        for i in range(len(srcs)):
            h = srcs[i].shape[1] // 2
            cps.append(pltpu.make_async_remote_copy(
                src_ref=srcs[i].at[:, pl.ds((1 - c) * h, h), :], dst_ref=lands[i], send_sem=send_sems.at[3 * i],
                recv_sem=recv_sems.at[3 * i], device_id=(x, y, 1 - c), device_id_type=MESH))
        return cps
    for i in range(len(srcs)):
        for j, (cx, cy) in enumerate(chips):
            if kind == "gather":
                src, dst = srcs[i].at[c], lands[i].at[p, c]
            else:
                src, dst = srcs[i].at[2 * cx + cy], lands[i].at[j]
            cps.append(pltpu.make_async_remote_copy(
                src_ref=src, dst_ref=dst, send_sem=send_sems.at[3 * i + j], recv_sem=recv_sems.at[3 * i + j],
                device_id=(cx, cy, c), device_id_type=MESH))
    return cps


def split_start(kind, srcs, land_shapes, after, name):
    n = len(srcs)

    def body(*refs):
        src_refs, land_refs = refs[:n], refs[n:2 * n]
        send_sems, recv_sems = refs[2 * n + 1], refs[2 * n + 2]
        token = refs[-1]
        for cp in _chip_copies(kind, src_refs, land_refs, send_sems, recv_sems):
            cp.start()
        token[...] = jnp.zeros_like(token)

    lands = [pltpu.with_memory_space_constraint(lax.empty(s, BF16), pltpu.HBM) for s in land_shapes]
    outs = pl.pallas_call(
        body, name=name,
        out_shape=(pltpu.SemaphoreType.DMA((3 * n,)), pltpu.SemaphoreType.DMA((3 * n,)),
                   *[pltpu.HBM(s.shape, s.dtype) for s in srcs], *[pltpu.HBM(s, BF16) for s in land_shapes],
                   jax.ShapeDtypeStruct((8, 128), F32)),
        in_specs=[_HBM] * (2 * n) + [_ANY],
        out_specs=(_SEM, _SEM, *([_HBM] * (2 * n)), pl.BlockSpec(memory_space=pltpu.VMEM)),
        input_output_aliases={i: 2 + i for i in range(2 * n)},
        compiler_params=pltpu.CompilerParams(has_side_effects=_EFFECT),
    )(*[pltpu.with_memory_space_constraint(s, pltpu.HBM) for s in srcs], *lands, after)
    return outs[0], outs[1], outs[2:2 + n], outs[2 + n:2 + 2 * n], outs[-1]


def split_wait(kind, send_sems, recv_sems, srcs, lands, after, name):
    n = len(srcs)

    def body(*refs):
        src_refs, land_refs = refs[:n], refs[n:2 * n]
        ssem, rsem = refs[2 * n], refs[2 * n + 1]
        for cp in _chip_copies(kind, src_refs, land_refs, ssem, rsem):
            cp.wait_send()
            cp.wait_recv()

    outs = pl.pallas_call(
        body, name=name,
        out_shape=[pltpu.HBM(s.shape, s.dtype) for s in srcs] + [pltpu.HBM(s.shape, s.dtype) for s in lands],
        in_specs=[_HBM] * (2 * n) + [_SEM, _SEM, _ANY], out_specs=[_HBM] * (2 * n),
        input_output_aliases={i: i for i in range(2 * n)},
        compiler_params=pltpu.CompilerParams(has_side_effects=_EFFECT),
    )(*srcs, *lands, send_sems, recv_sems, after)
    return outs[:n], outs[n:]


def pass_to_sibling(lands):
    n = len(lands)

    def body(*refs):
        ins, outs = refs[:n], refs[n:2 * n]
        send_sems, recv_sems = refs[2 * n:]
        x, y, c, chips = _place()
        cps = []
        for i in range(n):
            for j, (cx, cy) in enumerate(chips):
                blk = outs[i].at[2 * cx + cy, c]
                cps.append(pltpu.make_async_remote_copy(
                    src_ref=ins[i].at[2 * cx + cy, c], dst_ref=blk, send_sem=send_sems.at[3 * i + j],
                    recv_sem=recv_sems.at[3 * i + j], device_id=(x, y, 1 - c), device_id_type=MESH))
        for cp in cps:
            cp.start()
        for cp in cps:
            cp.wait()

    return pl.pallas_call(
        body, out_shape=[jax.ShapeDtypeStruct(t.shape, t.dtype) for t in lands], in_specs=[_ANY] * n,
        out_specs=[_ANY] * n, input_output_aliases={i: i for i in range(n)},
        scratch_shapes=[pltpu.SemaphoreType.DMA((3 * n,)), pltpu.SemaphoreType.DMA((3 * n,))],
        name="ag_pass_to_sibling")(*lands)


def exchange_halves_to_sibling(gs, name, by_cols=False):
    n = len(gs)

    def body(*refs):
        ins, outs = refs[:n], refs[n:2 * n]
        send_sems, recv_sems = refs[2 * n:]
        x, y, c, _ = _place()
        cps = []
        for i in range(n):
            if by_cols:
                hc = ins[i].shape[2] // 2
                src = ins[i].at[:, :, pl.ds(pl.multiple_of((1 - c) * hc, 128), hc)]
            else:
                h = ins[i].shape[1] // 2
                src = ins[i].at[:, pl.ds((1 - c) * h, h), :]
            cps.append(pltpu.make_async_remote_copy(
                src_ref=src, dst_ref=outs[i],
                send_sem=send_sems.at[i], recv_sem=recv_sems.at[i], device_id=(x, y, 1 - c), device_id_type=MESH))
        for cp in cps:
            cp.start()
        for cp in cps:
            cp.wait()

    halve = (lambda s: (s[0], s[1], s[2] // 2)) if by_cols else (lambda s: (s[0], s[1] // 2, s[2]))
    return pl.pallas_call(
        body, out_shape=[jax.ShapeDtypeStruct(halve(g.shape), g.dtype) for g in gs],
        in_specs=[_ANY] * n, out_specs=[_ANY] * n,
        scratch_shapes=[pltpu.SemaphoreType.DMA((n,)), pltpu.SemaphoreType.DMA((n,))],
        name=name)(*gs)


def scatter_to_chips(ps, name):
    n = len(ps)

    def body(*refs):
        ins, outs = refs[:n], refs[n:2 * n]
        send_sems, recv_sems = refs[2 * n:]
        x, y, c, chips = _place()
        cps = []
        for i in range(n):
            for j, (cx, cy) in enumerate(chips):
                cps.append(pltpu.make_async_remote_copy(
                    src_ref=ins[i].at[2 * cx + cy], dst_ref=outs[i].at[j], send_sem=send_sems.at[3 * i + j],
                    recv_sem=recv_sems.at[3 * i + j], device_id=(cx, cy, c), device_id_type=MESH))
        for cp in cps:
            cp.start()
        for cp in cps:
            cp.wait()

    return pl.pallas_call(
        body, out_shape=[jax.ShapeDtypeStruct((3,) + t.shape[1:], t.dtype) for t in ps],
        in_specs=[_ANY] * n, out_specs=[_ANY] * n,
        scratch_shapes=[pltpu.SemaphoreType.DMA((3 * n,)), pltpu.SemaphoreType.DMA((3 * n,))],
        name=name)(*ps)


def join_halves(rs, name):
    n = len(rs)

    def body(*refs):
        ins, outs = refs[:n], refs[n:2 * n]
        send_sems, recv_sems = refs[2 * n:]
        x, y, c, _ = _place()
        cps = [pltpu.make_async_remote_copy(
            src_ref=ins[i], dst_ref=outs[i], send_sem=send_sems.at[i], recv_sem=recv_sems.at[i],
            device_id=(x, y, 1 - c), device_id_type=MESH) for i in range(n)]
        for cp in cps:
            cp.start()
        for cp in cps:
            cp.wait()

    return pl.pallas_call(
        body, out_shape=[jax.ShapeDtypeStruct(r.shape, r.dtype) for r in rs],
        in_specs=[_ANY] * n, out_specs=[_ANY] * n,
        scratch_shapes=[pltpu.SemaphoreType.DMA((n,)), pltpu.SemaphoreType.DMA((n,))],
        name=name)(*rs)


def _pack(parts, row_mult=8):
    flat = jnp.concatenate([p.reshape(-1).astype(F32) for p in parts])
    unit = row_mult * 128
    n = -(-flat.shape[0] // unit) * unit
    return jnp.pad(flat, (0, n - flat.shape[0])).reshape(n // 128, 128)


def _unpack(flat, shapes):
    out, off = [], 0
    for s in shapes:
        n = int(np.prod(s))
        out.append(flat[off:off + n].reshape(s))
        off += n
    return out


def _gather_packed(parts, name):
    packed = _pack(parts)
    g = allgather_small(packed, name).reshape(N_DEV, -1)
    return _unpack_rows(g, [p.shape for p in parts])


def _unpack_rows(g, shapes):
    out, off = [], 0
    for s in shapes:
        n = int(np.prod(s))
        out.append(g[:, off:off + n].reshape((g.shape[0],) + tuple(s)))
        off += n
    return out


def _by_chip(t, axis):
    return jnp.concatenate([t[2 * p] for p in range(N_CHIPS)], axis=axis)


def kernel(x, c, ada_w, ada_b, ln_g, ln_b, a_in_w, a_conv_w, a_conv_b, a_dt_bias, a_A_log, a_D, a_norm_g, a_out_w, kv_w, b_in_w, b_out_w, loss_target, m_ada_w, m_ada_b, m_ln_g, m_ln_b, m_a_in_w, m_a_conv_w, m_a_conv_b, m_a_dt_bias, m_a_A_log, m_a_D, m_a_norm_g, m_a_out_w, m_kv_w, m_b_in_w, m_b_out_w, v_ada_w, v_ada_b, v_ln_g, v_ln_b, v_a_in_w, v_a_conv_w, v_a_conv_b, v_a_dt_bias, v_a_A_log, v_a_D, v_a_norm_g, v_a_out_w, v_kv_w, v_b_in_w, v_b_out_w):
    ax, ay, ac = lax.axis_index("x"), lax.axis_index("y"), lax.axis_index("c")
    chip = 2 * ax + ay
    dev = 4 * ax + 2 * ay + ac
    xin = x[0]
    tgt = loss_target[0]
    L, D = xin.shape
    G, P = SSD_G, SSD_P
    H = a_dt_bias.shape[1]
    Kh = H // G
    DI = H * P
    CONVD = a_conv_b.shape[1] * N_CHIPS
    HW = DIL_H * DIL_E
    Ws = ada_w.shape[2]

    (w_in_g,) = allgather_weights([jnp.transpose(a_in_w[0]).astype(BF16)], "allgather_w_in", by_cols=True)
    later = [a_out_w[0].astype(BF16), kv_w.astype(BF16), b_in_w[0].astype(BF16), b_out_w[0].astype(BF16)]
    later_split = [s.reshape(2, s.shape[0] // 2, s.shape[1]) for s in later]
    ag_ssem, ag_rsem, ag_srcs, ag_lands, ag_token = split_start(
        "gather", later_split, [(N_CHIPS,) + s.shape for s in later_split], w_in_g, "ag_later_start")
    w_in_t = w_in_g.reshape(-1, D)
    w_dt_t = jnp.pad(w_in_t[DI + CONVD:], ((0, 128 - H), (0, 0)))

    c8, cw8, cb8, ng8 = _gather_packed([c[0], a_conv_w[0], a_conv_b[0], a_norm_g[0]], "allgather_small_params")
    conv_w = _by_chip(cw8, 1)
    conv_b = _by_chip(cb8, 0).reshape(1, CONVD)
    norm_g = _by_chip(ng8, 0).reshape(1, DI)

    mod_s = ada_fwd(c8, ada_w)
    (mod8,) = _gather_packed([mod_s], "allgather_small_mod")
    mods = _by_chip(mod8, 2)
    mod = lax.dynamic_index_in_dim(mods, dev, axis=1, keepdims=False) + ada_b
    shift = [mod[l:l + 1, :D] for l in range(DEPTH)]
    scale = [mod[l:l + 1, D:2 * D] for l in range(DEPTH)]
    gate = [mod[l:l + 1, 2 * D:] for l in range(DEPTH)]
    lg = [ln_g[l:l + 1] for l in range(DEPTH)]
    lb = [ln_b[l:l + 1] for l in range(DEPTH)]

    h0 = modulate(xin, scale[0] + ag_token[0:1, 0:1], shift[0], "modulate0")
    zx = mm_nt(h0, w_in_t, BF16, "mm_in_zx", kw_rows=DI + CONVD)
    dtp = mm_nt(h0, w_dt_t, F32, "mm_in_dt")
    xbc = conv_fwd(zx, DI, conv_w, conv_b)
    dtp_g = jnp.transpose(dtp[:, :H].reshape(L, G, Kh), (1, 0, 2))
    dtp_gT = jnp.transpose(dtp_g, (0, 2, 1))
    vecs = [a_dt_bias.reshape(G, 1, Kh), a_dt_bias.reshape(G, Kh, 1), a_A_log.reshape(G, 1, Kh),
            a_A_log.reshape(G, Kh, 1), a_D.reshape(G, 1, Kh), a_D.reshape(G, Kh, 1)]
    y_ssd, states = ssd_fwd(xbc, dtp_g, dtp_gT, *vecs, DI)
    yn = rms_gate_fwd(y_ssd, zx, norm_g)
    later_split, ag_lands = split_wait("gather", ag_ssem, ag_rsem, ag_srcs, ag_lands, yn, "ag_later_wait")
    ag_lands = pass_to_sibling(ag_lands)
    w_out_g, w_kv_g, w_bin_g, w_bout_g = [
        lax.dynamic_update_index_in_dim(o, s, chip, 0).reshape((N_CHIPS,) + full.shape)
        for o, s, full in zip(ag_lands, later_split, later)]
    ymix0 = mm_nn(yn, w_out_g, F32, "mm_out_a", stack="row")
    x1, x1b, h1 = ln_mid(xin, ymix0, gate[0], lg[0], lb[0], scale[1], shift[1])

    n_grp = len(DIL_PATTERNS)
    cb = HW // 512
    assert w_bin_g.shape[2] == HW
    kv3 = [mm_cols_dilated(x1b, w_kv_g, [g * cb + t for t in range(cb)] + [(n_grp + g) * cb + t for t in range(cb)],
                           DIL_PATTERNS[g][1], f"mm_kv_{g}") for g in range(n_grp)]
    q3 = [mm_cols_dilated(h1, w_bin_g, [g * cb + t for t in range(cb)], DIL_PATTERNS[g][1], f"mm_q_{g}")
          for g in range(n_grp)]
    z_b = mm_nn(h1, w_bin_g[n_grp], BF16, "mm_z_b")
    os_, lses = [], []
    for gi in range(len(DIL_PATTERNS)):
        o, lse = attn_fwd(q3[gi], kv3[gi], gi)
        os_.append(o)
        lses.append(lse)
    om = merge_fwd(os_, lses, z_b)
    ymix1 = mm_nn(om, w_bout_g, F32, "mm_out_b", stack="col")
    dres2, dy2, dg1, db1, dgate1, sq = ln_final_fwd_bwd(x1, ymix1, gate[1], lg[1], lb[1], tgt)
    loss_part = 0.5 * jnp.sum(sq) / D

    g_bout = mm_tn(om, dy2, BF16, "mm_gw_out_b", stack="col")
    dgated = mm_nt(dy2, w_bout_g, BF16, "mm_gx_out_b", stack="col")
    dos, dprs, dz_b = merge_bwd(dgated, os_, lses, z_b)
    dqs, dks, dvs = [], [], []
    for gi in range(len(DIL_PATTERNS)):
        dq, dk, dv = attn_bwd(q3[gi], kv3[gi], dos[gi], lses[gi], dprs[gi], gi)
        dqs.append(dq)
        dks.append(dk)
        dvs.append(dv)
    dqz = jnp.concatenate(dqs + [dz_b], axis=1)
    dkv = jnp.concatenate(dks + dvs, axis=1)
    g_bin = mm_tn(h1, dqz, BF16, "mm_gw_in_b", stack="col")
    dh1 = mm_nt(dqz, w_bin_g, BF16, "mm_gx_in_b", stack="col")
    g_kv = mm_tn(x1b, dkv, BF16, "mm_gw_kv", stack="col")

    core = ac.astype(jnp.int32).reshape(1)
    chip_i = chip.astype(jnp.int32).reshape(1)

    def begin_exchange(gs, tag):
        shapes = [(g.shape[0], g.shape[1] // 2, g.shape[2]) for g in gs]
        return split_start("sibling", gs, shapes, gs[0], "rs_x%s_start" % tag)

    def begin_scatter(gs, nms, tag, exchange=None, after=None, by_cols=False):
        if exchange is None:
            sib = exchange_halves_to_sibling(gs, "rs_sibling_exchange_" + tag, by_cols=by_cols)
        else:
            gs, sib = split_wait("sibling", exchange[0], exchange[1], exchange[2], exchange[3], after,
                                 "rs_x%s_wait" % tag)
        parts = [add_half(g, a, core, "rs_add_" + nm, by_cols=by_cols) for g, a, nm in zip(gs, sib, nms)]
        return split_start("scatter", parts, [(3,) + t.shape[1:] for t in parts], parts[0], "rs_%s_start" % tag)

    def finish_scatter(handles, after, tag):
        nms, owns, landed = [], [], []
        for k, (handle, hn) in enumerate(handles):
            parts, lands = split_wait("scatter", handle[0], handle[1], handle[2], handle[3], after,
                                      "rs_%s%d_wait" % (tag, k))
            nms += hn
            owns += list(parts)
            landed += list(lands)
        halves = [sum_partials(own, t, chip_i, "rs_sum_" + nm) for own, t, nm in zip(owns, landed, nms)]
        theirs = join_halves(halves, "rs_join_halves_" + tag)
        return dict(zip(nms, zip(halves, theirs)))

    names_b = ["kv", "in_b", "out_b"]
    ex_b = begin_exchange([g_kv, g_bin, g_bout], "b")
    dx1_kv = mm_nt(dkv, w_kv_g, BF16, "mm_gx_kv", stack="col", after=ex_b[4])
    rs_b = begin_scatter(None, names_b, "b", exchange=ex_b, after=dx1_kv)

    dres1, dy1, dg0, db0, dgate0, dscale1, dshift1 = mod_ln_bwd(
        dres2, dh1, dx1_kv, x1, scale[1], xin, ymix0, gate[0] + rs_b[4][0:1, 0:1], lg[0])
    g_out = mm_tn(yn, dy1, BF16, "mm_gw_out_a", stack="row")
    ex_a1 = begin_exchange([g_out], "a1")
    dyn = mm_nt(dy1, w_out_g, BF16, "mm_gx_out_a", stack="row", after=ex_a1[4])
    rs_a1 = begin_scatter(None, ["out_a"], "a1", exchange=ex_a1, after=dyn)
    dy_ssd, dz_a, dnorm_g = rms_gate_bwd(dyn, y_ssd, zx, norm_g + rs_a1[4][0:1, 0:1])
    dxs, dB, dC, ddtp_g, dbias_g, dalog_g, dD_g = ssd_bwd(xbc, dtp_g, dtp_gT, *vecs, states, dy_ssd, DI)
    dxbc = jnp.concatenate([dxs, dB, dC], axis=1)
    dxbc_pre, dconv_w, dconv_b = conv_bwd(zx, DI, conv_w, conv_b, dxbc)
    dzx = jnp.concatenate([dz_a, dxbc_pre], axis=1)
    ddtp = jnp.pad(jnp.transpose(ddtp_g, (1, 0, 2)).reshape(L, H), ((0, 0), (0, 128 - H)))
    g_inT = mm_tn(dzx, h0, BF16, "mm_gw_in_zx", m_rows=DI + CONVD + H)
    g_dtT = mm_tn(ddtp, h0, BF16, "mm_gw_in_dt")
    g_inT = lax.dynamic_update_slice(g_inT, g_dtT[:H], (DI + CONVD, 0))
    rs_a2 = begin_scatter([g_inT.reshape(N_CHIPS, -1, D)], ["in_a"], "a2", by_cols=True)
    dh0 = mm_nn(dzx, w_in_t, BF16, "mm_gx_in_zx", after=rs_a2[4])
    dh0_dt = mm_nn(ddtp, w_dt_t, F32, "mm_gx_in_dt")
    grad_x, dscale0, dshift0 = mod_bwd(dres1, dh0, dh0_dt, xin, scale[0] + rs_a2[4][0:1, 0:1], "mod_bwd0",
                                       through_mod=True)
    g_halves = finish_scatter([(rs_b, names_b)], grad_x, "b")

    def step_halves(w, m, v, nm):
        shp = w.shape
        mine, theirs_ = g_halves[nm]
        outs4 = adamw_halves(w.reshape(-1, shp[-1]), mine, theirs_, m.reshape(-1, shp[-1]), v.reshape(-1, shp[-1]),
                             core, "adamw_" + nm)
        return tuple(t.reshape(shp) for t in outs4)

    big = {
        "kv_w": step_halves(kv_w, m_kv_w, v_kv_w, "kv"),
        "b_in_w": step_halves(b_in_w, m_b_in_w, v_b_in_w, "in_b"),
        "b_out_w": step_halves(b_out_w, m_b_out_w, v_b_out_w, "out_b"),
    }
    g_halves.update(finish_scatter([(rs_a1, ["out_a"]), (rs_a2, ["in_a"])], big["kv_w"][1], "a"))
    g_halves["in_a"] = tuple(jnp.transpose(t) for t in g_halves["in_a"])
    big["a_in_w"] = step_halves(a_in_w, m_a_in_w, v_a_in_w, "in_a")
    big["a_out_w"] = step_halves(a_out_w, m_a_out_w, v_a_out_w, "out_a")

    dmod = jnp.concatenate([jnp.concatenate([dshift0, dscale0, dgate0], axis=1),
                            jnp.concatenate([dshift1, dscale1, dgate1], axis=1)], axis=0)
    small_parts = [jnp.concatenate([dg0, dg1], axis=0), jnp.concatenate([db0, db1], axis=0),
                   dbias_g.reshape(1, H), dalog_g.reshape(1, H), dD_g.reshape(1, H),
                   dconv_w, dconv_b, dnorm_g, loss_part.reshape(1, 1)]
    small_shapes = [p.shape for p in small_parts]
    packed = jnp.concatenate([_pack([dmod]), _pack(small_parts)], axis=0)
    n_mod_rows = _pack([dmod]).shape[0]
    gathered = allgather_small(packed, "allgather_small_grads", after=g_halves["in_a"][1]).reshape(N_DEV, -1, 128)
    dmod8 = gathered[:, :n_mod_rows].reshape(N_DEV, -1)[:, :2 * 3 * D].reshape(N_DEV, DEPTH, 3 * D)
    summed = sum_leading(gathered, "sum_small")
    g_ada_b = summed[:n_mod_rows].reshape(-1)[:2 * 3 * D].reshape(DEPTH, 3 * D)
    (g_ln_g, g_ln_b, g_dt_bias, g_a_log, g_dsk, g_conv_w, g_conv_b, g_norm_g, loss_all) = _unpack(
        summed[n_mod_rows:].reshape(-1), small_shapes)
    loss = loss_all.reshape(())
    Cs = CONVD // N_CHIPS
    g_conv_w_s = lax.dynamic_slice_in_dim(g_conv_w, chip * Cs, Cs, axis=1)
    g_conv_b_s = lax.dynamic_slice_in_dim(g_conv_b, chip * Cs, Cs, axis=1)
    g_norm_g_s = lax.dynamic_slice_in_dim(g_norm_g, chip * (DI // N_CHIPS), DI // N_CHIPS, axis=1)
    dmod_s = jnp.transpose(lax.dynamic_slice_in_dim(dmod8, chip * Ws, Ws, axis=2), (1, 0, 2))

    def step2d(w, g, m, v, nm):
        shp = w.shape
        d_, m_, v_ = adamw(w.reshape(-1, shp[-1]), g.reshape(-1, shp[-1]), m.reshape(-1, shp[-1]),
                           v.reshape(-1, shp[-1]), "adamw_" + nm)
        return g.reshape(shp), d_.reshape(shp), m_.reshape(shp), v_.reshape(shp)

    big["ada_w"] = step2d(ada_w, ada_wgrad(jnp.transpose(c8), dmod_s), m_ada_w, v_ada_w, "ada_w")
    small_names = ["ada_b", "ln_g", "ln_b", "a_conv_w", "a_conv_b", "a_dt_bias", "a_A_log", "a_D", "a_norm_g"]
    small_w = [ada_b, ln_g, ln_b, a_conv_w, a_conv_b, a_dt_bias, a_A_log, a_D, a_norm_g]
    small_m = [m_ada_b, m_ln_g, m_ln_b, m_a_conv_w, m_a_conv_b, m_a_dt_bias, m_a_A_log, m_a_D, m_a_norm_g]
    small_v = [v_ada_b, v_ln_g, v_ln_b, v_a_conv_w, v_a_conv_b, v_a_dt_bias, v_a_A_log, v_a_D, v_a_norm_g]
    small_g = [g_ada_b, g_ln_g, g_ln_b, g_conv_w_s, g_conv_b_s, g_dt_bias, g_a_log, g_dsk, g_norm_g_s]
    shapes = [w.shape for w in small_w]
    small_g = [g.reshape(s) for g, s in zip(small_g, shapes)]
    d_p, m_p, v_p = adamw(_pack(small_w), _pack(small_g), _pack(small_m), _pack(small_v), "adamw_small")
    small = {}
    for nm, g, d_, m_, v_ in zip(small_names, small_g, _unpack(d_p.reshape(-1), shapes), _unpack(m_p.reshape(-1), shapes),
                                 _unpack(v_p.reshape(-1), shapes)):
        small[nm] = (g, d_, m_, v_)
    allw = {**big, **small}
    order = ["ada_w", "ada_b", "ln_g", "ln_b", "a_in_w", "a_conv_w", "a_conv_b", "a_dt_bias", "a_A_log", "a_D",
             "a_norm_g", "a_out_w", "kv_w", "b_in_w", "b_out_w"]
    outs = [loss, grad_x.reshape(x.shape)]
    for k in range(4):
        outs += [allw[n][k] for n in order]
    return tuple(outs)
```

```python
import functools

import jax
import jax.numpy as jnp
import numpy as np
from jax import lax
from jax.experimental import pallas as pl
from jax.experimental.pallas import tpu as pltpu

F32 = jnp.float32
BF16 = jnp.bfloat16
MESH = pl.DeviceIdType.MESH

DEPTH = 2
ALPHA = (2 * DEPTH) ** 0.25
LN_EPS = 1e-5
RMS_EPS = 1e-5
SSD_P = 64
SSD_N = 128
SSD_Q = 256
SSD_G = 8
CONV_W = 4
DIL_PATTERNS = ((128, 1), (512, 4), (2048, 16))
DIL_H = 8
DIL_E = 128
DIL_BLK = 128
ADAM_LR, ADAM_B1, ADAM_B2, ADAM_EPS, ADAM_WD, ADAM_STEP = 0.001, 0.9, 0.999, 1e-08, 0.01, 10

VMEM_LIMIT = 56 * 1024 * 1024
N_CHIPS = 4
N_DEV = 8


def _tile(dim, target, mult=128):
    if dim <= target:
        return dim
    t = (target // mult) * mult
    while t >= mult:
        if dim % t == 0:
            return t
        t -= mult
    return dim


def _cp(sem):
    return pltpu.CompilerParams(dimension_semantics=sem, vmem_limit_bytes=VMEM_LIMIT)


def _sigmoid(x):
    return 1.0 / (1.0 + jnp.exp(-x))


def _silu(x):
    return x * _sigmoid(x)


def _dsilu(x):
    s = _sigmoid(x)
    return s * (1.0 + x * (1.0 - s))


def _softplus(x):
    return jnp.maximum(x, 0.0) + jnp.log(1.0 + jnp.exp(-jnp.abs(x)))


def _mm_call(a, b, out_shape, grid, a_spec, b_spec, o_spec, acc_shape, dims, name, after=None):
    nk = grid[2]
    extra = [] if after is None else [after]

    def prod(a_ref, b_ref):
        return lax.dot_general(a_ref[...].astype(BF16), b_ref[...].astype(BF16), (dims, ((), ())),
                               preferred_element_type=F32)

    def body_single(a_ref, b_ref, *rest):
        o_ref = rest[len(extra)]
        o_ref[...] = prod(a_ref, b_ref).astype(o_ref.dtype)

    def body_multi(a_ref, b_ref, *rest):
        o_ref, acc_ref = rest[len(extra):]
        k = pl.program_id(2)

        @pl.when(k == 0)
        def _():
            acc_ref[...] = prod(a_ref, b_ref)

        @pl.when(jnp.logical_and(k > 0, k < nk - 1))
        def _():
            acc_ref[...] += prod(a_ref, b_ref)

        @pl.when(k == nk - 1)
        def _():
            o_ref[...] = (acc_ref[...] + prod(a_ref, b_ref)).astype(o_ref.dtype)

    return pl.pallas_call(
        body_single if nk == 1 else body_multi, grid=grid, in_specs=[a_spec, b_spec] + [_ANY] * len(extra),
        out_specs=o_spec, out_shape=out_shape, scratch_shapes=[] if nk == 1 else [pltpu.VMEM(acc_shape, F32)],
        compiler_params=_cp(("parallel", "parallel", "arbitrary")), name=name)(a, b, *extra)


def mm_nn(a, b, out_dtype, name, stack=None, tm=1024, tn=1024, tk=2048, n_cols=None, after=None):
    M, K = a.shape
    if stack is None:
        N = b.shape[1] if n_cols is None else n_cols
        tn, tk = _tile(N, tn), _tile(K, tk)
        b_spec = pl.BlockSpec((tk, tn), lambda i, j, k: (k, j))
    elif stack == "col":
        S, _, Ns = b.shape
        N = S * Ns
        tn, tk = _tile(Ns, tn), _tile(K, tk)
        npb = Ns // tn
        b_spec = pl.BlockSpec((None, tk, tn), lambda i, j, k: (j // npb, k, j % npb))
    else:
        S, Ks, N = b.shape
        tn, tk = _tile(N, tn), _tile(Ks, tk)
        kpb = Ks // tk
        b_spec = pl.BlockSpec((None, tk, tn), lambda i, j, k: (k // kpb, k % kpb, j))
    tm = _tile(M, tm)
    return _mm_call(a, b, jax.ShapeDtypeStruct((M, N), out_dtype), (M // tm, N // tn, K // tk),
                    pl.BlockSpec((tm, tk), lambda i, j, k: (i, k)), b_spec,
                    pl.BlockSpec((tm, tn), lambda i, j, k: (i, j)), (tm, tn), ((1,), (0,)), name, after=after)


def mm_cols_dilated(a, b, gcols, d, name, tm=1024, tn=512):
    L, K = a.shape
    S, _, Ns = b.shape
    tm, tn = _tile(L, tm), _tile(Ns, tn)
    npb = Ns // tn
    nj = len(gcols)
    rows = tm // d

    def body(cols_ref, a_ref, b_ref, o_ref, *scr):
        prod = jnp.dot(a_ref[...], b_ref[...], preferred_element_type=F32)
        if d == 1:
            o_ref[0] = prod.astype(BF16)
        else:
            for c in range(tn // 128):
                scr[0][c] = prod[:, c * 128:(c + 1) * 128]
            for r in range(d):
                for c in range(tn // 128):
                    o_ref[r, :, c * 128:(c + 1) * 128] = scr[0].at[c][pl.ds(r, rows, stride=d), :].astype(BF16)

    return pl.pallas_call(
        body,
        grid_spec=pltpu.PrefetchScalarGridSpec(
            num_scalar_prefetch=1, grid=(L // tm, nj),
            in_specs=[pl.BlockSpec((tm, K), lambda i, j, c: (i, 0)),
                      pl.BlockSpec((None, K, tn), lambda i, j, c: (c[j] // npb, 0, c[j] % npb))],
            out_specs=pl.BlockSpec((d, rows, tn), lambda i, j, c: (0, i, j)),
            scratch_shapes=[] if d == 1 else [pltpu.VMEM((tn // 128, tm, 128), F32)]),
        out_shape=jax.ShapeDtypeStruct((d, L // d, nj * tn), BF16),
        compiler_params=_cp(("parallel", "arbitrary")), name=name)(jnp.asarray(gcols, jnp.int32), a, b)


def mm_nt(a, b, out_dtype, name, stack=None, tm=1024, tn=1024, tk=2048, after=None, kw_rows=None):
    M, C = a.shape
    if stack is None:
        Kw = b.shape[0] if kw_rows is None else kw_rows
        tn, tk = _tile(Kw, tn), _tile(C, tk)
        b_spec = pl.BlockSpec((tn, tk), lambda i, j, k: (j, k))
    elif stack == "col":
        S, Kw, Cs = b.shape
        tn, tk = _tile(Kw, tn), _tile(Cs, tk)
        cpb = Cs // tk
        b_spec = pl.BlockSpec((None, tn, tk), lambda i, j, k: (k // cpb, j, k % cpb))
    else:
        S, Ks, _ = b.shape
        Kw = S * Ks
        tn, tk = _tile(Ks, tn), _tile(C, tk)
        jpb = Ks // tn
        b_spec = pl.BlockSpec((None, tn, tk), lambda i, j, k: (j // jpb, j % jpb, k))
    tm = _tile(M, tm)
    return _mm_call(a, b, jax.ShapeDtypeStruct((M, Kw), out_dtype), (M // tm, Kw // tn, C // tk),
                    pl.BlockSpec((tm, tk), lambda i, j, k: (i, k)), b_spec,
                    pl.BlockSpec((tm, tn), lambda i, j, k: (i, j)), (tm, tn), ((1,), (1,)), name, after=after)


def mm_tn(a, b, out_dtype, name, stack=None, n_stack=N_CHIPS, tm=1024, tn=1024, tk=2048, m_rows=None):
    L, M = a.shape
    N = b.shape[1]
    tk = _tile(L, tk)
    if stack is None:
        tm, tn = _tile(M, tm), _tile(N, tn)
        o_spec = pl.BlockSpec((tm, tn), lambda i, j, k: (i, j))
        out_shape = (M if m_rows is None else m_rows, N)
    elif stack == "col":
        Ns = N // n_stack
        tm, tn = _tile(M, tm), _tile(Ns, tn)
        npb = Ns // tn
        o_spec = pl.BlockSpec((None, tm, tn), lambda i, j, k: (j // npb, i, j % npb))
        out_shape = (n_stack, M, Ns)
    else:
        Ms = M // n_stack
        tm, tn = _tile(Ms, tm), _tile(N, tn)
        mpb = Ms // tm
        o_spec = pl.BlockSpec((None, tm, tn), lambda i, j, k: (i // mpb, i % mpb, j))
        out_shape = (n_stack, Ms, N)
    return _mm_call(a, b, jax.ShapeDtypeStruct(out_shape, out_dtype), (M // tm, N // tn, L // tk),
                    pl.BlockSpec((tk, tm), lambda i, j, k: (k, i)), pl.BlockSpec((tk, tn), lambda i, j, k: (k, j)),
                    o_spec, (tm, tn), ((0,), (0,)), name)


def _row_specs(tr, widths):
    return [pl.BlockSpec((tr, w), lambda i: (i, 0)) for w in widths]


def _vec_spec(w):
    return pl.BlockSpec((1, w), lambda i: (0, 0))


def _acc_rows(ref, val, i):
    s = jnp.sum(val, axis=0, keepdims=True)

    @pl.when(i == 0)
    def _():
        ref[...] = s

    @pl.when(i > 0)
    def _():
        ref[...] += s


def modulate(x, scale, shift, name):
    L, D = x.shape
    tr = _tile(L, 512, 16)

    def body(x_ref, sc_ref, sh_ref, h_ref):
        h_ref[...] = (x_ref[...] * (1.0 + sc_ref[...]) + sh_ref[...]).astype(BF16)

    return pl.pallas_call(
        body, grid=(L // tr,), in_specs=_row_specs(tr, [D]) + [_vec_spec(D)] * 2, out_specs=_row_specs(tr, [D])[0],
        out_shape=jax.ShapeDtypeStruct((L, D), BF16), compiler_params=_cp(("parallel",)), name=name)(x, scale, shift)


def _ln_core(x, y, gate, g, b):
    u = ALPHA * x + (1.0 + gate) * y
    mu = jnp.mean(u, axis=-1, keepdims=True)
    d = u - mu
    var = jnp.mean(d * d, axis=-1, keepdims=True)
    rstd = lax.rsqrt(var + LN_EPS)
    xhat = d * rstd
    return xhat * g + b, xhat, rstd


def ln_mid(x, y, gate, g, b, scale, shift):
    L, D = x.shape
    tr = _tile(L, 256, 16)

    def body(x_ref, y_ref, gate_ref, g_ref, b_ref, sc_ref, sh_ref, x1_ref, x1b_ref, h_ref):
        x1, _, _ = _ln_core(x_ref[...], y_ref[...], gate_ref[...], g_ref[...], b_ref[...])
        x1_ref[...] = x1
        x1b_ref[...] = x1.astype(BF16)
        h_ref[...] = (x1 * (1.0 + sc_ref[...]) + sh_ref[...]).astype(BF16)

    return pl.pallas_call(
        body, grid=(L // tr,), in_specs=_row_specs(tr, [D, D]) + [_vec_spec(D)] * 5,
        out_specs=_row_specs(tr, [D, D, D]),
        out_shape=[jax.ShapeDtypeStruct((L, D), F32), jax.ShapeDtypeStruct((L, D), BF16),
                   jax.ShapeDtypeStruct((L, D), BF16)],
        compiler_params=_cp(("parallel",)), name="ln_mid")(x, y, gate, g, b, scale, shift)


def _ln_bwd_rows(dout_v, xhat, rstd, g):
    dxh = dout_v * g
    m1 = jnp.mean(dxh, axis=-1, keepdims=True)
    m2 = jnp.mean(dxh * xhat, axis=-1, keepdims=True)
    return rstd * (dxh - m1 - xhat * m2)


def ln_final_fwd_bwd(x, y, gate, g, b, target):
    L, D = x.shape
    tr = _tile(L, 256, 16)

    def body(x_ref, y_ref, gate_ref, g_ref, b_ref, t_ref, dres_ref, dy_ref, dg_ref, db_ref, dgate_ref, sq_ref):
        i = pl.program_id(0)
        yv = y_ref[...]
        out, xhat, rstd = _ln_core(x_ref[...], yv, gate_ref[...], g_ref[...], b_ref[...])
        err = out - t_ref[...]
        dout_v = err * (1.0 / D)
        du = _ln_bwd_rows(dout_v, xhat, rstd, g_ref[...])
        dres_ref[...] = ALPHA * du
        dy_ref[...] = ((1.0 + gate_ref[...]) * du).astype(BF16)
        _acc_rows(dg_ref, dout_v * xhat, i)
        _acc_rows(db_ref, dout_v, i)
        _acc_rows(dgate_ref, du * yv, i)
        _acc_rows(sq_ref, err * err, i)

    return pl.pallas_call(
        body, grid=(L // tr,), in_specs=_row_specs(tr, [D, D]) + [_vec_spec(D)] * 3 + _row_specs(tr, [D]),
        out_specs=_row_specs(tr, [D, D]) + [_vec_spec(D)] * 4,
        out_shape=[jax.ShapeDtypeStruct((L, D), F32), jax.ShapeDtypeStruct((L, D), BF16)]
        + [jax.ShapeDtypeStruct((1, D), F32)] * 4,
        compiler_params=_cp(("arbitrary",)), name="ln_final_fwd_bwd")(x, y, gate, g, b, target)


def mod_ln_bwd(dres_in, dh, dskip, xmid, scale, x, y, gate, g):
    L, D = x.shape
    tr = _tile(L, 256, 16)

    def body(dres_ref, dh_ref, dskip_ref, xm_ref, sc_ref, x_ref, y_ref, gate_ref, g_ref,
             dres_out, dy_ref, dg_ref, db_ref, dgate_ref, dsc_ref, dsh_ref):
        i = pl.program_id(0)
        dh_v = dh_ref[...].astype(F32)
        dout_v = dres_ref[...] + dskip_ref[...].astype(F32) + dh_v * (1.0 + sc_ref[...])
        _acc_rows(dsc_ref, dh_v * xm_ref[...], i)
        _acc_rows(dsh_ref, dh_v, i)
        yv = y_ref[...]
        _, xhat, rstd = _ln_core(x_ref[...], yv, gate_ref[...], g_ref[...], 0.0)
        du = _ln_bwd_rows(dout_v, xhat, rstd, g_ref[...])
        dres_out[...] = ALPHA * du
        dy_ref[...] = ((1.0 + gate_ref[...]) * du).astype(BF16)
        _acc_rows(dg_ref, dout_v * xhat, i)
        _acc_rows(db_ref, dout_v, i)
        _acc_rows(dgate_ref, du * yv, i)

    return pl.pallas_call(
        body, grid=(L // tr,),
        in_specs=_row_specs(tr, [D] * 4) + [_vec_spec(D)] + _row_specs(tr, [D, D]) + [_vec_spec(D)] * 2,
        out_specs=_row_specs(tr, [D, D]) + [_vec_spec(D)] * 5,
        out_shape=[jax.ShapeDtypeStruct((L, D), F32), jax.ShapeDtypeStruct((L, D), BF16)]
        + [jax.ShapeDtypeStruct((1, D), F32)] * 5,
        compiler_params=_cp(("arbitrary",)), name="mod_ln_bwd")(dres_in, dh, dskip, xmid, scale, x, y, gate, g)


def ln_bwd(dout, x, y, gate, g, name):
    L, D = x.shape
    tr = _tile(L, 256, 16)

    def body(do_ref, x_ref, y_ref, gate_ref, g_ref, dres_ref, dy_ref, dg_ref, db_ref, dgate_ref):
        i = pl.program_id(0)
        yv = y_ref[...]
        dout_v = do_ref[...]
        _, xhat, rstd = _ln_core(x_ref[...], yv, gate_ref[...], g_ref[...], 0.0)
        dxh = dout_v * g_ref[...]
        m1 = jnp.mean(dxh, axis=-1, keepdims=True)
        m2 = jnp.mean(dxh * xhat, axis=-1, keepdims=True)
        du = rstd * (dxh - m1 - xhat * m2)
        dres_ref[...] = ALPHA * du
        dy_ref[...] = ((1.0 + gate_ref[...]) * du).astype(BF16)
        _acc_rows(dg_ref, dout_v * xhat, i)
        _acc_rows(db_ref, dout_v, i)
        _acc_rows(dgate_ref, du * yv, i)

    return pl.pallas_call(
        body, grid=(L // tr,), in_specs=_row_specs(tr, [D, D, D]) + [_vec_spec(D)] * 2,
        out_specs=_row_specs(tr, [D, D]) + [_vec_spec(D)] * 3,
        out_shape=[jax.ShapeDtypeStruct((L, D), F32), jax.ShapeDtypeStruct((L, D), BF16)]
        + [jax.ShapeDtypeStruct((1, D), F32)] * 3,
        compiler_params=_cp(("arbitrary",)), name=name)(dout, x, y, gate, g)


def mod_bwd(dres, dh, dh2, xin, scale, name, through_mod):
    L, D = xin.shape
    tr = _tile(L, 256, 16)

    def body(dres_ref, dh_ref, dh2_ref, x_ref, sc_ref, dx_ref, dsc_ref, dsh_ref):
        i = pl.program_id(0)
        dh_v = dh_ref[...].astype(F32)
        tot = dres_ref[...]
        if through_mod:
            dh_v = dh_v + dh2_ref[...].astype(F32)
        else:
            tot = tot + dh2_ref[...].astype(F32)
        dx_ref[...] = tot + dh_v * (1.0 + sc_ref[...])
        _acc_rows(dsc_ref, dh_v * x_ref[...], i)
        _acc_rows(dsh_ref, dh_v, i)

    return pl.pallas_call(
        body, grid=(L // tr,), in_specs=_row_specs(tr, [D, D, D, D]) + [_vec_spec(D)],
        out_specs=_row_specs(tr, [D]) + [_vec_spec(D)] * 2,
        out_shape=[jax.ShapeDtypeStruct((L, D), F32)] + [jax.ShapeDtypeStruct((1, D), F32)] * 2,
        compiler_params=_cp(("arbitrary",)), name=name)(dres, dh, dh2, xin, scale)


CONV_HALO = 16
CONV_SUB = 128


def _conv_rows(x_ref, i, tr, L):
    nblk = L // tr
    s = pl.multiple_of(i * tr, CONV_HALO)
    cur = x_ref[pl.ds(s, tr), :].astype(F32)
    sp = pl.multiple_of(jnp.maximum(i * tr - CONV_HALO, 0), CONV_HALO)
    sn = pl.multiple_of(jnp.minimum(i * tr + tr, L - CONV_HALO), CONV_HALO)
    prev = x_ref[pl.ds(sp, CONV_HALO), :].astype(F32) * (i > 0).astype(F32)
    nxt = x_ref[pl.ds(sn, CONV_HALO), :].astype(F32) * (i < nblk - 1).astype(F32)
    return jnp.concatenate([prev, cur, nxt], axis=0)


def _conv_rows_raw(x_ref, i, tr, L):
    nblk = L // tr
    s = pl.multiple_of(i * tr, CONV_HALO)
    sp = pl.multiple_of(jnp.maximum(i * tr - CONV_HALO, 0), CONV_HALO)
    sn = pl.multiple_of(jnp.minimum(i * tr + tr, L - CONV_HALO), CONV_HALO)
    prev = x_ref[pl.ds(sp, CONV_HALO), :]
    nxt = x_ref[pl.ds(sn, CONV_HALO), :]
    prev = jnp.where(i > 0, prev, jnp.zeros_like(prev))
    nxt = jnp.where(i < nblk - 1, nxt, jnp.zeros_like(nxt))
    return jnp.concatenate([prev, x_ref[pl.ds(s, tr), :], nxt], axis=0)


def _shift_mats(n_out, n_in, offsets):
    r = lax.broadcasted_iota(jnp.int32, (n_out, n_in), 0)
    c = lax.broadcasted_iota(jnp.int32, (n_out, n_in), 1)
    return jnp.stack([(c == r + o) for o in offsets]).astype(BF16)


def _shift_rows(v, j):
    n = v.shape[0]
    return v if j % n == 0 else pltpu.roll(v, j % n, 0)


def _conv_taps(xe):
    return [_shift_rows(xe, CONV_W - 1 - k) for k in range(CONV_W)]


def _conv_eval(taps, w_ref, b_ref):
    c = b_ref[...] + w_ref[0:1, :] * taps[0]
    for k in range(1, CONV_W):
        c = c + w_ref[k:k + 1, :] * taps[k]
    return c


def conv_fwd(zx, col0, conv_w, conv_b):
    L = zx.shape[0]
    C = conv_w.shape[1]
    tc = _tile(C, 512)
    tr = _tile(L, 512, CONV_HALO)
    off = col0 // tc

    H = CONV_HALO
    SB = min(CONV_SUB, tr)
    n_in = SB + 2 * H

    def body(x_ref, s_ref, w_ref, b_ref, o_ref):
        i = pl.program_id(1)
        xe = _conv_rows_raw(x_ref, i, tr, L)
        for sb in range(tr // SB):
            win = xe[sb * SB:sb * SB + n_in]
            c = b_ref[...] + w_ref[CONV_W - 1:CONV_W, :] * win[H:H + SB].astype(F32)
            for k in range(CONV_W - 1):
                c = c + w_ref[k:k + 1, :] * jnp.dot(s_ref[k], win, preferred_element_type=F32)
            o_ref[sb * SB:(sb + 1) * SB, :] = _silu(c).astype(BF16)

    shifts = _shift_mats(SB, n_in, [H - (CONV_W - 1) + k for k in range(CONV_W - 1)])
    return pl.pallas_call(
        body, grid=(C // tc, L // tr),
        in_specs=[pl.BlockSpec((L, tc), lambda j, i: (0, off + j)),
                  pl.BlockSpec((CONV_W - 1, SB, n_in), lambda j, i: (0, 0, 0)),
                  pl.BlockSpec((CONV_W, tc), lambda j, i: (0, j)), pl.BlockSpec((1, tc), lambda j, i: (0, j))],
        out_specs=pl.BlockSpec((tr, tc), lambda j, i: (i, j)),
        out_shape=jax.ShapeDtypeStruct((L, C), BF16), compiler_params=_cp(("parallel", "arbitrary")),
        name="conv_fwd")(zx, shifts, conv_w, conv_b)


def conv_bwd(zx, col0, conv_w, conv_b, dxbc):
    L = zx.shape[0]
    C = conv_w.shape[1]
    tc = _tile(C, 512)
    tr = _tile(L, 512, CONV_HALO)
    off = col0 // tc
    H = CONV_HALO

    def body(x_ref, g_ref, w_ref, b_ref, dx_ref, dw_ref, db_ref):
        i = pl.program_id(1)
        xe = _conv_rows(x_ref, i, tr, L)
        ge = _conv_rows(g_ref, i, tr, L)
        taps = _conv_taps(xe)
        dc = ge * _dsilu(_conv_eval(taps, w_ref, b_ref))
        dx = w_ref[CONV_W - 1:CONV_W, :] * dc
        for k in range(CONV_W - 1):
            dx = dx + w_ref[k:k + 1, :] * _shift_rows(dc, -(CONV_W - 1 - k))
        dx_ref[...] = dx[H:H + tr].astype(BF16)
        dcc = dc[H:H + tr]
        rows = [jnp.sum(dcc * taps[k][H:H + tr], axis=0, keepdims=True) for k in range(CONV_W)]
        dwv = jnp.concatenate(rows + [jnp.zeros((8 - CONV_W, tc), F32)], axis=0)
        dbv = jnp.sum(dcc, axis=0, keepdims=True)

        @pl.when(i == 0)
        def _():
            dw_ref[...] = dwv
            db_ref[...] = dbv

        @pl.when(i > 0)
        def _():
            dw_ref[...] += dwv
            db_ref[...] += dbv

    dx, dw, db = pl.pallas_call(
        body, grid=(C // tc, L // tr),
        in_specs=[pl.BlockSpec((L, tc), lambda j, i: (0, off + j)), pl.BlockSpec((L, tc), lambda j, i: (0, j)),
                  pl.BlockSpec((CONV_W, tc), lambda j, i: (0, j)), pl.BlockSpec((1, tc), lambda j, i: (0, j))],
        out_specs=[pl.BlockSpec((tr, tc), lambda j, i: (i, j)), pl.BlockSpec((8, tc), lambda j, i: (0, j)),
                   pl.BlockSpec((1, tc), lambda j, i: (0, j))],
        out_shape=[jax.ShapeDtypeStruct((L, C), BF16), jax.ShapeDtypeStruct((8, C), F32),
                   jax.ShapeDtypeStruct((1, C), F32)],
        compiler_params=_cp(("parallel", "arbitrary")), name="conv_bwd")(zx, dxbc, conv_w, conv_b)
    return dx, dw[:CONV_W], db


_NN = (((1,), (0,)), ((), ()))


def _pieces(x, n):
    out, r = [], x
    for _ in range(n):
        p = r.astype(BF16)
        out.append(p)
        r = r - p.astype(F32)
    return out


def _dot01(a, b01, n, dims=_NN):
    b = b01.astype(BF16)
    return functools.reduce(lambda u, v: u + v,
                            [lax.dot_general(p, b, dims, preferred_element_type=F32) for p in _pieces(a, n)])


def _dot01_left(a01, b, n, dims=_NN):
    a = a01.astype(BF16)
    return functools.reduce(lambda u, v: u + v,
                            [lax.dot_general(a, p, dims, preferred_element_type=F32) for p in _pieces(b, n)])


def _ssd_common(dtp_ref, dtpT_ref, bias_ref, biasT_ref, alog_ref, alogT_ref, b_ref, c_ref):
    Q = SSD_Q
    dt = _softplus(dtp_ref[...] + bias_ref[...])
    A = -jnp.exp(alog_ref[...])
    row = lax.broadcasted_iota(jnp.int32, (Q, Q), 0)
    col = lax.broadcasted_iota(jnp.int32, (Q, Q), 1)
    causal = row >= col
    tril = causal.astype(F32)
    Kh = dt.shape[1]
    acum = _dot01_left(tril, dt * A, 3)
    eye = (lax.broadcasted_iota(jnp.int32, (Kh, Kh), 0) == lax.broadcasted_iota(jnp.int32, (Kh, Kh), 1)).astype(F32)
    acumT = _dot01_left(eye, acum, 3, dims=(((1,), (1,)), ((), ())))
    Bm = b_ref[...]
    Cm = c_ref[...]
    cb = lax.dot_general(Cm, Bm, (((1,), (1,)), ((), ())), preferred_element_type=F32)
    return dt, A, causal, row, col, acum, acumT, Bm, Cm, cb


def _ssd_in_specs(Q, GP, N, Kh, DI, cmap):
    nb0 = DI // N
    vec = pl.BlockSpec((None, 1, Kh), lambda g, c: (g, 0, 0))
    vecT = pl.BlockSpec((None, Kh, 1), lambda g, c: (g, 0, 0))
    return [pl.BlockSpec((Q, GP), lambda g, c: (cmap(c), g)),
            pl.BlockSpec((Q, N), lambda g, c: (cmap(c), nb0 + g)),
            pl.BlockSpec((Q, N), lambda g, c: (cmap(c), nb0 + SSD_G + g)),
            pl.BlockSpec((None, Q, Kh), lambda g, c: (g, cmap(c), 0)),
            pl.BlockSpec((None, Kh, Q), lambda g, c: (g, 0, cmap(c))),
            vec, vecT, vec, vecT, vec, vecT]


def _hi(a, b01):
    return _dot01(a, b01, 2)


def _headsum(a, b01):
    return _dot01(a, b01, 1)


def _ssd_heads(dskT_ref, acum, acumT, dt, Kh):
    Q, P, N = SSD_Q, SSD_P, SSD_N
    GP = Kh * P
    sh_p = P.bit_length() - 1
    seg = lambda shape, dim: lax.shift_right_logical(lax.broadcasted_iota(jnp.int32, shape, dim), sh_p)
    E = (seg((Kh, GP), 1) == lax.broadcasted_iota(jnp.int32, (Kh, GP), 0)).astype(F32)
    ET = (seg((GP, Kh), 0) == lax.broadcasted_iota(jnp.int32, (GP, Kh), 1)).astype(F32)
    a_last = acum[Q - 1:Q, :]
    tail = jnp.exp(a_last - acum)
    eLT = jnp.exp(acumT[:, Q - 1:Q])
    rowseg = seg((GP, N), 0)
    eL_b = jnp.zeros((GP, N), F32)
    for k in range(Kh):
        eL_b = jnp.where(rowseg == k, eLT[k:k + 1, :], eL_b)
    return dict(
        E=E, ET=ET, a_last=a_last, tail=tail, eL_b=eL_b,
        dt_all=_hi(dt, E), ea_all=_hi(jnp.exp(acum), E), tail_all=_hi(tail, E),
        dsk_all=jnp.sum(E * dskT_ref[...], axis=0, keepdims=True))


def _head_chunks(GP):
    CW = min(GP, 128)
    return CW, CW // SSD_P, GP // CW


def _head_mask(Q, CW, kk):
    lane = lax.broadcasted_iota(jnp.int32, (Q, CW), 1)
    return jnp.logical_and(lane >= kk * SSD_P, lane < (kk + 1) * SSD_P)


def ssd_fwd(xbc, dtp_g, dtp_gT, bias_g, bias_gT, alog_g, alog_gT, dsk_g, dsk_gT, DI):
    L = xbc.shape[0]
    Q, P, N, G = SSD_Q, SSD_P, SSD_N, SSD_G
    GP = DI // G
    Kh = GP // P
    nc = L // Q

    CW, hpc, nch = _head_chunks(GP)
    nt = (((1,), (1,)), ((), ()))
    tn = (((0,), (0,)), ((), ()))

    def body(xs_ref, b_ref, c_ref, dtp_ref, dtpT_ref, bias_ref, biasT_ref, alog_ref, alogT_ref, dsk_ref, dskT_ref,
             y_ref, st_ref, state):
        @pl.when(pl.program_id(1) == 0)
        def _():
            state[...] = jnp.zeros(state.shape, F32)

        st_ref[...] = state[...]
        dt, A, causal, row, col, acum, acumT, Bm, Cm, cb = _ssd_common(
            dtp_ref, dtpT_ref, bias_ref, biasT_ref, alog_ref, alogT_ref, b_ref, c_ref)
        hd = _ssd_heads(dskT_ref, acum, acumT, dt, Kh)
        xs = xs_ref[...].astype(F32)
        xdt_all = xs * hd["dt_all"]
        S_all = state[...]
        y_all = (lax.dot_general(Cm, S_all.astype(BF16), nt, preferred_element_type=F32) * hd["ea_all"]
                 + xs * hd["dsk_all"])
        state[...] = S_all * hd["eL_b"] + lax.dot_general(
            (xdt_all * hd["tail_all"]).astype(BF16), Bm, tn, preferred_element_type=F32)
        for ch in range(nch):
            cs = slice(ch * CW, (ch + 1) * CW)
            xc = xdt_all[:, cs]
            acc = y_all[:, cs]
            for kk in range(hpc):
                k = ch * hpc + kk
                decay = jnp.exp(jnp.where(causal, acum[:, k:k + 1] - acumT[k:k + 1, :], -jnp.inf))
                xk = xc if hpc == 1 else jnp.where(_head_mask(Q, CW, kk), xc, 0.0)
                acc = acc + jnp.dot((cb * decay).astype(BF16), xk.astype(BF16), preferred_element_type=F32)
            y_ref[:, cs] = acc.astype(BF16)

    return pl.pallas_call(
        body, grid=(G, nc), in_specs=_ssd_in_specs(Q, GP, N, Kh, DI, lambda c: c),
        out_specs=[pl.BlockSpec((Q, GP), lambda g, c: (c, g)),
                   pl.BlockSpec((None, None, GP, N), lambda g, c: (c, g, 0, 0))],
        out_shape=[jax.ShapeDtypeStruct((L, DI), BF16), jax.ShapeDtypeStruct((nc, G, GP, N), F32)],
        scratch_shapes=[pltpu.VMEM((GP, N), F32)], compiler_params=_cp(("parallel", "arbitrary")),
        name="ssd_fwd")(xbc, xbc, xbc, dtp_g, dtp_gT, bias_g, bias_gT, alog_g, alog_gT, dsk_g, dsk_gT)


def ssd_bwd(xbc, dtp_g, dtp_gT, bias_g, bias_gT, alog_g, alog_gT, dsk_g, dsk_gT, states, dy, DI):
    L = xbc.shape[0]
    Q, P, N, G = SSD_Q, SSD_P, SSD_N, SSD_G
    GP = DI // G
    Kh = GP // P
    nc = L // Q
    rev = lambda c: nc - 1 - c

    CW, hpc, nch = _head_chunks(GP)

    def body(xs_ref, b_ref, c_ref, dtp_ref, dtpT_ref, bias_ref, biasT_ref, alog_ref, alogT_ref, dsk_ref, dskT_ref,
             st_ref, dy_ref, dxs_ref, dB_ref, dC_ref, ddtp_ref, dbias_ref, dalog_ref, dD_ref, dstate):
        ci = pl.program_id(1)

        @pl.when(ci == 0)
        def _():
            dstate[...] = jnp.zeros(dstate.shape, F32)

        dt, A, causal, row, col, acum, acumT, Bm, Cm, cb = _ssd_common(
            dtp_ref, dtpT_ref, bias_ref, biasT_ref, alog_ref, alogT_ref, b_ref, c_ref)
        tn = (((0,), (0,)), ((), ()))
        nt = (((1,), (1,)), ((), ()))
        hd = _ssd_heads(dskT_ref, acum, acumT, dt, Kh)
        ET, tail = hd["ET"], hd["tail"]
        cbT = lax.dot_general(Bm, Cm, nt, preferred_element_type=F32)
        causalT = row <= col
        xs = xs_ref[...].astype(F32)
        xdt_all = xs * hd["dt_all"]
        dyb = dy_ref[...]
        dy_all = dyb.astype(F32)
        S_all = st_ref[...]
        S_b = S_all.astype(BF16)
        dS_all = dstate[...]
        dS_b = dS_all.astype(BF16)
        CS_all = lax.dot_general(Cm, S_b, nt, preferred_element_type=F32)
        dyE_b = (dy_all * hd["ea_all"]).astype(BF16)
        dC_acc = jnp.dot(dyE_b, S_b, preferred_element_type=F32)
        dS_y = lax.dot_general(dyE_b, Cm, tn, preferred_element_type=F32)
        BdS_all = lax.dot_general(Bm, dS_b, nt, preferred_element_type=F32)
        dB_acc = jnp.dot((xdt_all * hd["tail_all"]).astype(BF16), dS_b, preferred_element_type=F32)
        dtail = _headsum(xdt_all * BdS_all, ET)
        da_cols = _headsum(dy_all * CS_all * hd["ea_all"], ET) - dtail * tail
        dss = _dot01_left(jnp.ones((8, N), F32), _dot01_left(hd["E"], dS_all * S_all, 2), 2, dims=nt)
        da_last = dss[0:1] * jnp.exp(hd["a_last"]) + jnp.sum(dtail * tail, axis=0, keepdims=True)
        rowi = lax.broadcasted_iota(jnp.int32, (Q, Kh), 0)
        da_cols = da_cols + jnp.where(rowi == Q - 1, da_last, 0.0)
        dstate[...] = hd["eL_b"] * dS_all + dS_y
        sum_mg = jnp.zeros((Q, Q), F32)
        ddt_x = jnp.zeros((Q, Kh), F32)
        da_rows = jnp.zeros((Kh, Q), F32)
        lane_k = lax.broadcasted_iota(jnp.int32, (Q, Kh), 1)
        sub_k = lax.broadcasted_iota(jnp.int32, (Kh, Q), 0)
        for ch in range(nch):
            cs = slice(ch * CW, (ch + 1) * CW)
            dyc = dyb[:, cs]
            xc_b = xdt_all[:, cs].astype(BF16)
            acc = hd["tail_all"][:, cs] * BdS_all[:, cs]
            for kk in range(hpc):
                k = ch * hpc + kk
                a_b = jnp.broadcast_to(acum[:, k:k + 1], (Q, Q))
                a_r = acumT[k:k + 1, :]
                decay = jnp.exp(jnp.where(causal, a_b - a_r, -jnp.inf))
                decayT = jnp.exp(jnp.where(causalT, a_r - a_b, -jnp.inf))
                dyk = dyc if hpc == 1 else jnp.where(_head_mask(Q, CW, kk), dyc, jnp.zeros_like(dyc))
                mg = decay * lax.dot_general(dyk, xc_b, nt, preferred_element_type=F32)
                sum_mg = sum_mg + mg
                w = mg * cb
                da_cols = da_cols + jnp.where(lane_k == k, jnp.sum(w, axis=1, keepdims=True), 0.0)
                da_rows = da_rows + jnp.where(sub_k == k, jnp.sum(w, axis=0, keepdims=True), 0.0)
                acc = acc + jnp.dot((decayT * cbT).astype(BF16), dyk, preferred_element_type=F32)
            dxs_ref[:, cs] = (acc * hd["dt_all"][:, cs] + dy_all[:, cs] * hd["dsk_all"][:, cs]).astype(BF16)
            ddt_x = ddt_x + _headsum(acc * xs[:, cs], ET[cs, :])
        eye_q = (row == col).astype(F32)
        da_cols = da_cols - _dot01_left(eye_q, da_rows, 3, dims=nt)
        dD_row = jnp.sum(_headsum(dy_all * xs, ET), axis=0, keepdims=True)
        sum_mg_b = sum_mg.astype(BF16)
        dB_ref[...] = (dB_acc + lax.dot_general(sum_mg_b, Cm, tn, preferred_element_type=F32)).astype(BF16)
        dC_ref[...] = (dC_acc + jnp.dot(sum_mg_b, Bm, preferred_element_type=F32)).astype(BF16)
        triu = (row <= col).astype(F32)
        ddtA = _dot01_left(triu, da_cols, 3)
        ddt = ddt_x + ddtA * A
        dpre = ddt * _sigmoid(dtp_ref[...] + bias_ref[...])
        ddtp_ref[...] = dpre
        dbias_v = jnp.sum(dpre, axis=0, keepdims=True)
        dalog_v = jnp.sum(ddtA * dt, axis=0, keepdims=True) * A

        @pl.when(ci == 0)
        def _():
            dbias_ref[...] = dbias_v
            dalog_ref[...] = dalog_v
            dD_ref[...] = dD_row

        @pl.when(ci > 0)
        def _():
            dbias_ref[...] += dbias_v
            dalog_ref[...] += dalog_v
            dD_ref[...] += dD_row

    vec_o = pl.BlockSpec((None, 1, Kh), lambda g, c: (g, 0, 0))
    return pl.pallas_call(
        body, grid=(G, nc),
        in_specs=_ssd_in_specs(Q, GP, N, Kh, DI, rev)
        + [pl.BlockSpec((None, None, GP, N), lambda g, c: (rev(c), g, 0, 0)),
           pl.BlockSpec((Q, GP), lambda g, c: (rev(c), g))],
        out_specs=[pl.BlockSpec((Q, GP), lambda g, c: (rev(c), g)), pl.BlockSpec((Q, N), lambda g, c: (rev(c), g)),
                   pl.BlockSpec((Q, N), lambda g, c: (rev(c), g)),
                   pl.BlockSpec((None, Q, Kh), lambda g, c: (g, rev(c), 0)), vec_o, vec_o, vec_o],
        out_shape=[jax.ShapeDtypeStruct((L, DI), BF16), jax.ShapeDtypeStruct((L, G * N), BF16),
                   jax.ShapeDtypeStruct((L, G * N), BF16), jax.ShapeDtypeStruct((G, L, Kh), F32)]
        + [jax.ShapeDtypeStruct((G, 1, Kh), F32)] * 3,
        scratch_shapes=[pltpu.VMEM((GP, N), F32)], compiler_params=_cp(("parallel", "arbitrary")),
        name="ssd_bwd")(xbc, xbc, xbc, dtp_g, dtp_gT, bias_g, bias_gT, alog_g, alog_gT, dsk_g, dsk_gT, states, dy)


def _rms_groups(y2, ng_ref, DI):
    S = DI // SSD_G
    for g in range(SSD_G):
        gs = slice(g * S, (g + 1) * S)
        seg = y2[:, gs]
        r = lax.rsqrt(jnp.mean(seg * seg, axis=-1, keepdims=True) + RMS_EPS)
        yield gs, seg * r, r, ng_ref[:, gs]


def rms_gate_fwd(y, zx, norm_g):
    L, DI = y.shape
    tr = _tile(L, 256, 16)

    def body(y_ref, z_ref, ng_ref, o_ref):
        y2 = y_ref[...].astype(F32) * _silu(z_ref[...].astype(F32))
        for gs, yh, _, ng in _rms_groups(y2, ng_ref, DI):
            o_ref[:, gs] = (yh * ng).astype(BF16)

    return pl.pallas_call(
        body, grid=(L // tr,), in_specs=_row_specs(tr, [DI, DI]) + [_vec_spec(DI)], out_specs=_row_specs(tr, [DI])[0],
        out_shape=jax.ShapeDtypeStruct((L, DI), BF16), compiler_params=_cp(("parallel",)),
        name="rms_gate_fwd")(y, zx, norm_g)


def rms_gate_bwd(dyn, y, zx, norm_g):
    L, DI = y.shape
    tr = _tile(L, 256, 16)

    def body(dyn_ref, y_ref, z_ref, ng_ref, dy_ref, dz_ref, dng_ref):
        i = pl.program_id(0)
        yv = y_ref[...].astype(F32)
        zv = z_ref[...].astype(F32)
        sz = _silu(zv)
        dsz = _dsilu(zv)
        dynv = dyn_ref[...].astype(F32)
        for gs, yh, r, ng in _rms_groups(yv * sz, ng_ref, DI):
            dyh = dynv[:, gs] * ng
            dy2 = r * (dyh - yh * jnp.mean(dyh * yh, axis=-1, keepdims=True))
            dy_ref[:, gs] = (dy2 * sz[:, gs]).astype(BF16)
            dz_ref[:, gs] = (dy2 * yv[:, gs] * dsz[:, gs]).astype(BF16)
            s = jnp.sum(dynv[:, gs] * yh, axis=0, keepdims=True)

            @pl.when(i == 0)
            def _():
                dng_ref[:, gs] = s

            @pl.when(i > 0)
            def _():
                dng_ref[:, gs] += s

    return pl.pallas_call(
        body, grid=(L // tr,), in_specs=_row_specs(tr, [DI, DI, DI]) + [_vec_spec(DI)],
        out_specs=_row_specs(tr, [DI, DI]) + [_vec_spec(DI)],
        out_shape=[jax.ShapeDtypeStruct((L, DI), BF16)] * 2 + [jax.ShapeDtypeStruct((1, DI), F32)],
        compiler_params=_cp(("arbitrary",)), name="rms_gate_bwd")(dyn, y, zx, norm_g)


def _alibi_slope(gi, h):
    n = len(DIL_PATTERNS) * DIL_H
    return float(2.0 ** (-8.0 * (gi * DIL_H + h + 1) / n))


def _attn_masks():
    qi = lax.broadcasted_iota(jnp.int32, (DIL_BLK, DIL_BLK), 0)
    kj = lax.broadcasted_iota(jnp.int32, (DIL_BLK, DIL_BLK), 1)
    dcur = (qi - kj).astype(F32)
    return dcur, qi >= kj, dcur + float(DIL_BLK), kj >= qi


def attn_fwd(q3, kv3, gi):
    window, d = DIL_PATTERNS[gi]
    assert window // d == DIL_BLK
    HW = DIL_H * DIL_E
    M = q3.shape[1]
    nb = M // DIL_BLK
    scale = DIL_E ** -0.5
    nt = (((1,), (1,)), ((), ()))

    def body(q_ref, kp_ref, kc_ref, vp_ref, vc_ref, o_ref, lse_ref):
        n = pl.program_id(1)
        dcur, vcur, dprev, vprev0 = _attn_masks()
        dist = jnp.concatenate([dprev, dcur], axis=1)
        valid = jnp.concatenate([jnp.logical_and(vprev0, n > 0), vcur], axis=1)
        lane = lax.broadcasted_iota(jnp.int32, (DIL_BLK, 128), 1)
        lse_acc = jnp.zeros((DIL_BLK, 128), F32)
        for h in range(DIL_H):
            hs = slice(h * DIL_E, (h + 1) * DIL_E)
            sl = _alibi_slope(gi, h) * d
            kcat = jnp.concatenate([kp_ref[:, hs], kc_ref[:, hs]], axis=0)
            vcat = jnp.concatenate([vp_ref[:, hs], vc_ref[:, hs]], axis=0)
            s = lax.dot_general(q_ref[:, hs], kcat, nt, preferred_element_type=F32) * scale - sl * dist
            s = jnp.where(valid, s, -jnp.inf)
            m = jnp.max(s, axis=-1, keepdims=True)
            p = jnp.exp(s - m)
            den = jnp.sum(p, axis=-1, keepdims=True)
            o = jnp.dot(p.astype(BF16), vcat, preferred_element_type=F32) / den
            o_ref[:, hs] = o.astype(BF16)
            lse_acc = jnp.where(lane == h, m + jnp.log(den), lse_acc)
        lse_ref[...] = lse_acc

    blk = (None, DIL_BLK, HW)
    prev = lambda n: jnp.maximum(n - 1, 0)
    return pl.pallas_call(
        body, grid=(d, nb),
        in_specs=[pl.BlockSpec(blk, lambda r, n: (r, n, 0)),
                  pl.BlockSpec(blk, lambda r, n: (r, prev(n), 0)), pl.BlockSpec(blk, lambda r, n: (r, n, 0)),
                  pl.BlockSpec(blk, lambda r, n: (r, prev(n), 1)), pl.BlockSpec(blk, lambda r, n: (r, n, 1))],
        out_specs=[pl.BlockSpec(blk, lambda r, n: (r, n, 0)), pl.BlockSpec((None, DIL_BLK, 128), lambda r, n: (r, n, 0))],
        out_shape=[jax.ShapeDtypeStruct((d, M, HW), BF16), jax.ShapeDtypeStruct((d, M, 128), F32)],
        compiler_params=_cp(("parallel", "parallel")), name=f"attn_fwd_{gi}")(q3, kv3, kv3, kv3, kv3)


def attn_bwd(q3, kv3, do3, lse3, dpr3, gi):
    window, d = DIL_PATTERNS[gi]
    HW = DIL_H * DIL_E
    M = q3.shape[1]
    L = M * d
    nb = M // DIL_BLK
    scale = DIL_E ** -0.5
    nt = (((1,), (1,)), ((), ()))
    tn = (((0,), (0,)), ((), ()))

    def body(q0_ref, q1_ref, k_ref, v_ref, do0_ref, do1_ref, l0_ref, l1_ref, r0_ref, r1_ref,
             dq_ref, dk_ref, dv_ref, carry):
        n = pl.program_id(1)

        @pl.when(n == 0)
        def _():
            carry[...] = jnp.zeros(carry.shape, F32)

        dcur, vcur, dprev, vprev0 = _attn_masks()
        dist = jnp.concatenate([dcur, dprev], axis=0)
        valid = jnp.concatenate([vcur, jnp.logical_and(vprev0, n < nb - 1)], axis=0)
        B = DIL_BLK
        for h in range(DIL_H):
            hs = slice(h * DIL_E, (h + 1) * DIL_E)
            sl = _alibi_slope(gi, h) * d
            kh = k_ref[:, hs]
            vh = v_ref[:, hs]
            qcat = jnp.concatenate([q0_ref[:, hs], q1_ref[:, hs]], axis=0)
            docat = jnp.concatenate([do0_ref[:, hs], do1_ref[:, hs]], axis=0)
            lcat = jnp.concatenate([l0_ref[:, h:h + 1], l1_ref[:, h:h + 1]], axis=0)
            rcat = jnp.concatenate([r0_ref[:, h:h + 1], r1_ref[:, h:h + 1]], axis=0)
            s = lax.dot_general(qcat, kh, nt, preferred_element_type=F32) * scale - sl * dist
            p = jnp.exp(jnp.where(valid, s - lcat, -jnp.inf))
            ds = p * (lax.dot_general(docat, vh, nt, preferred_element_type=F32) - rcat)
            ds_b = (ds * scale).astype(BF16)
            dv_ref[:, hs] = lax.dot_general(p.astype(BF16), docat, tn, preferred_element_type=F32).astype(BF16)
            dk_ref[:, hs] = lax.dot_general(ds_b, qcat, tn, preferred_element_type=F32).astype(BF16)
            dqc = jnp.dot(ds_b, kh, preferred_element_type=F32)
            dq_ref[:, hs] = (carry[:, hs] + dqc[:B]).astype(BF16)
            carry[:, hs] = dqc[B:]

    blk = (None, DIL_BLK, HW)
    sblk = (None, DIL_BLK, 128)
    oblk = (DIL_BLK, HW)
    nxt = lambda n: jnp.minimum(n + 1, nb - 1)
    here = lambda c: (lambda r, n: (r, n, c))
    ahead = lambda c: (lambda r, n: (r, nxt(n), c))
    outs = pl.pallas_call(
        body, grid=(d, nb),
        in_specs=[pl.BlockSpec(blk, here(0)), pl.BlockSpec(blk, ahead(0)),
                  pl.BlockSpec(blk, here(0)), pl.BlockSpec(blk, here(1)),
                  pl.BlockSpec(blk, here(0)), pl.BlockSpec(blk, ahead(0)),
                  pl.BlockSpec(sblk, here(0)), pl.BlockSpec(sblk, ahead(0)),
                  pl.BlockSpec(sblk, here(0)), pl.BlockSpec(sblk, ahead(0))],
        out_specs=[pl.BlockSpec(oblk, lambda r, n: (n, r))] * 3,
        out_shape=[jax.ShapeDtypeStruct((M, d * HW), BF16)] * 3,
        scratch_shapes=[pltpu.VMEM(oblk, F32)], compiler_params=_cp(("parallel", "arbitrary")),
        name=f"attn_bwd_{gi}")(q3, q3, kv3, kv3, do3, do3, lse3, lse3, dpr3, dpr3)
    return [t.reshape(L, HW) for t in outs]


def _merge_weights(l_tiles, h):
    ls = [t[:, h:h + 1] for t in l_tiles]
    mx = functools.reduce(jnp.maximum, ls)
    es = [jnp.exp(l - mx) for l in ls]
    den = functools.reduce(lambda a, b: a + b, es)
    return [e / den for e in es]


def _dil_specs(tr, arrs):
    return [pl.BlockSpec((a.shape[0], tr // a.shape[0], a.shape[2]), lambda i: (0, i, 0)) for a in arrs]


def _dil_scratch(tr, arrs):
    return [pltpu.VMEM((a.shape[2] // 128, tr, 128), F32) for a in arrs if a.shape[0] > 1]


def _undilate(refs3, scrs, tr):
    out, k = [], 0
    for ref in refs3:
        d, _, W = ref.shape
        if d == 1:
            out.append(lambda c, ref=ref: ref[0, :, c * 128:(c + 1) * 128])
            continue
        scr = scrs[k]
        k += 1
        for r in range(d):
            for c in range(W // 128):
                scr.at[c][pl.ds(r, tr // d, stride=d), :] = ref[r, :, c * 128:(c + 1) * 128].astype(F32)
        out.append(lambda c, scr=scr: scr[c])
    return out


def merge_fwd(os3, lses3, z):
    HW = os3[0].shape[2]
    L = os3[0].shape[0] * os3[0].shape[1]
    tr = _tile(L, 256, 16)
    ng = len(os3)
    n_scr = len(_dil_scratch(tr, os3))

    def body(*refs):
        z_ref, out_ref = refs[2 * ng], refs[2 * ng + 1]
        scrs = refs[2 * ng + 2:]
        o_get = _undilate(refs[:ng], scrs[:n_scr], tr)
        l_tiles = [g(0) for g in _undilate(refs[ng:2 * ng], scrs[n_scr:], tr)]
        for h in range(DIL_H):
            hs = slice(h * DIL_E, (h + 1) * DIL_E)
            ws = _merge_weights(l_tiles, h)
            om = functools.reduce(lambda a, b: a + b, [w * o(h).astype(F32) for w, o in zip(ws, o_get)])
            out_ref[:, hs] = (om * _silu(z_ref[:, hs].astype(F32))).astype(BF16)

    return pl.pallas_call(
        body, grid=(L // tr,),
        in_specs=_dil_specs(tr, os3) + _dil_specs(tr, lses3) + _row_specs(tr, [HW]),
        out_specs=_row_specs(tr, [HW])[0], out_shape=jax.ShapeDtypeStruct((L, HW), BF16),
        scratch_shapes=_dil_scratch(tr, os3) + _dil_scratch(tr, lses3),
        compiler_params=_cp(("parallel",)), name="merge_fwd")(*os3, *lses3, z)


def merge_bwd(dgated, os3, lses3, z):
    HW = os3[0].shape[2]
    L = os3[0].shape[0] * os3[0].shape[1]
    tr = _tile(L, 256, 16)
    ng = len(os3)
    n_scr = len(_dil_scratch(tr, os3))

    def body(*refs):
        dg_ref = refs[0]
        z_ref = refs[1 + 2 * ng]
        outs = refs[2 + 2 * ng:2 + 2 * ng + 2 * ng + 1]
        scrs = refs[2 + 2 * ng + 2 * ng + 1:]
        do_out, dpr_out, dz_ref = outs[:ng], outs[ng:2 * ng], outs[2 * ng]
        o_get = _undilate(refs[1:1 + ng], scrs[:n_scr], tr)
        l_tiles = [g(0) for g in _undilate(refs[1 + ng:1 + 2 * ng], scrs[n_scr:2 * n_scr], tr)]
        stage = scrs[2 * n_scr:]
        do_stage, dpr_stage, k = [], [], 0
        for g in range(ng):
            if do_out[g].shape[0] == 1:
                do_stage.append(None)
                dpr_stage.append(None)
            else:
                do_stage.append(stage[2 * k])
                dpr_stage.append(stage[2 * k + 1])
                k += 1
        lane = lax.broadcasted_iota(jnp.int32, (tr, 128), 1)
        accs = [jnp.zeros((tr, 128), F32) for _ in range(ng)]
        for h in range(DIL_H):
            hs = slice(h * DIL_E, (h + 1) * DIL_E)
            ws = _merge_weights(l_tiles, h)
            ov = [o(h).astype(F32) for o in o_get]
            om = functools.reduce(lambda a, b: a + b, [w * o for w, o in zip(ws, ov)])
            zv = z_ref[:, hs].astype(F32)
            dgv = dg_ref[:, hs].astype(F32)
            dom = dgv * _silu(zv)
            dz_ref[:, hs] = (dgv * om * _dsilu(zv)).astype(BF16)
            dws = [jnp.sum(dom * o, axis=-1, keepdims=True) for o in ov]
            dwbar = functools.reduce(lambda a, b: a + b, [w * dw for w, dw in zip(ws, dws)])
            for g in range(ng):
                if do_stage[g] is None:
                    do_out[g][0, :, hs] = (ws[g] * dom).astype(BF16)
                else:
                    do_stage[g][h] = ws[g] * dom
                accs[g] = jnp.where(lane == h, ws[g] * dwbar, accs[g])
        for g in range(ng):
            d = do_out[g].shape[0]
            if d == 1:
                dpr_out[g][0] = accs[g]
                continue
            dpr_stage[g][0] = accs[g]
            for r in range(d):
                dpr_out[g][r] = dpr_stage[g].at[0][pl.ds(r, tr // d, stride=d), :]
                for c in range(HW // 128):
                    do_out[g][r, :, c * 128:(c + 1) * 128] = do_stage[g].at[c][pl.ds(r, tr // d, stride=d), :].astype(BF16)

    stage_shapes = []
    for o3 in os3:
        if o3.shape[0] > 1:
            stage_shapes += [pltpu.VMEM((HW // 128, tr, 128), F32), pltpu.VMEM((1, tr, 128), F32)]
    outs = pl.pallas_call(
        body, grid=(L // tr,),
        in_specs=_row_specs(tr, [HW]) + _dil_specs(tr, os3) + _dil_specs(tr, lses3) + _row_specs(tr, [HW]),
        out_specs=_dil_specs(tr, os3) + _dil_specs(tr, lses3) + _row_specs(tr, [HW]),
        out_shape=[jax.ShapeDtypeStruct(o.shape, BF16) for o in os3] + [jax.ShapeDtypeStruct(l.shape, F32) for l in lses3]
        + [jax.ShapeDtypeStruct((L, HW), BF16)],
        scratch_shapes=_dil_scratch(tr, os3) + _dil_scratch(tr, lses3) + stage_shapes,
        compiler_params=_cp(("parallel",)), name="merge_bwd")(dgated, *os3, *lses3, z)
    return outs[:ng], outs[ng:2 * ng], outs[2 * ng]


def ada_fwd(c8, ada_w):
    nl, D, Ws = ada_w.shape
    tn = _tile(Ws, 512)

    def body(c_ref, w_ref, o_ref):
        o_ref[...] = jnp.dot(_silu(c_ref[...]), w_ref[...], precision=lax.Precision.HIGHEST,
                             preferred_element_type=F32)

    return pl.pallas_call(
        body, grid=(nl, Ws // tn),
        in_specs=[pl.BlockSpec((N_DEV, D), lambda l, j: (0, 0)), pl.BlockSpec((None, D, tn), lambda l, j: (l, 0, j))],
        out_specs=pl.BlockSpec((None, N_DEV, tn), lambda l, j: (l, 0, j)),
        out_shape=jax.ShapeDtypeStruct((nl, N_DEV, Ws), F32), compiler_params=_cp(("parallel", "parallel")),
        name="ada_fwd")(c8, ada_w)


def ada_wgrad(c8t, dmod):
    nl, _, Ws = dmod.shape
    D = c8t.shape[0]
    tm = _tile(D, 512, 8)

    def body(c_ref, d_ref, o_ref):
        sc = _silu(c_ref[...])
        acc = sc[:, 0:1] * d_ref[0:1, :]
        for e in range(1, N_DEV):
            acc = acc + sc[:, e:e + 1] * d_ref[e:e + 1, :]
        o_ref[...] = acc

    return pl.pallas_call(
        body, grid=(nl, D // tm),
        in_specs=[pl.BlockSpec((tm, N_DEV), lambda l, i: (i, 0)), pl.BlockSpec((None, N_DEV, Ws), lambda l, i: (l, 0, 0))],
        out_specs=pl.BlockSpec((None, tm, Ws), lambda l, i: (l, i, 0)),
        out_shape=jax.ShapeDtypeStruct((nl, D, Ws), F32), compiler_params=_cp(("parallel", "parallel")),
        name="ada_wgrad")(c8t, dmod)


def _adamw_math(w, gv, m, v):
    c1 = 1.0 - ADAM_B1 ** ADAM_STEP
    c2 = 1.0 - ADAM_B2 ** ADAM_STEP
    nm = ADAM_B1 * m + (1.0 - ADAM_B1) * gv
    nv = ADAM_B2 * v + (1.0 - ADAM_B2) * (gv * gv)
    return -ADAM_LR * ((nm / c1) / (jnp.sqrt(nv / c2) + ADAM_EPS) + ADAM_WD * w), nm, nv


def adamw_ada(w, c8t, dmod, m, v):
    nl, D, Ws = w.shape
    tr = _tile(D, 256, 8)

    def body(c_ref, d_ref, w_ref, m_ref, v_ref, g_ref, dl_ref, nm_ref, nv_ref):
        sc = _silu(c_ref[...])
        gv = sc[:, 0:1] * d_ref[0:1, :]
        for e in range(1, N_DEV):
            gv = gv + sc[:, e:e + 1] * d_ref[e:e + 1, :]
        g_ref[...] = gv
        dl_ref[...], nm_ref[...], nv_ref[...] = _adamw_math(w_ref[...], gv, m_ref[...], v_ref[...])

    blk = pl.BlockSpec((None, tr, Ws), lambda l, i: (l, i, 0))
    return pl.pallas_call(
        body, grid=(nl, D // tr),
        in_specs=[pl.BlockSpec((tr, N_DEV), lambda l, i: (i, 0)), pl.BlockSpec((None, N_DEV, Ws), lambda l, i: (l, 0, 0)),
                  blk, blk, blk],
        out_specs=[blk] * 4, out_shape=[jax.ShapeDtypeStruct((nl, D, Ws), F32)] * 4,
        compiler_params=_cp(("parallel", "parallel")), name="adamw_ada_w")(c8t, dmod, w, m, v)


def adamw(w, g, m, v, name):
    R, C = w.shape
    tr = _tile(R, 256, 8)
    c1 = 1.0 - ADAM_B1 ** ADAM_STEP
    c2 = 1.0 - ADAM_B2 ** ADAM_STEP

    def body(w_ref, g_ref, m_ref, v_ref, d_ref, nm_ref, nv_ref):
        gv = g_ref[...]
        nm = ADAM_B1 * m_ref[...] + (1.0 - ADAM_B1) * gv
        nv = ADAM_B2 * v_ref[...] + (1.0 - ADAM_B2) * (gv * gv)
        nm_ref[...] = nm
        nv_ref[...] = nv
        d_ref[...] = -ADAM_LR * ((nm / c1) / (jnp.sqrt(nv / c2) + ADAM_EPS) + ADAM_WD * w_ref[...])

    return pl.pallas_call(
        body, grid=(R // tr,), in_specs=_row_specs(tr, [C] * 4), out_specs=_row_specs(tr, [C] * 3),
        out_shape=[jax.ShapeDtypeStruct((R, C), F32)] * 3, compiler_params=_cp(("parallel",)), name=name)(w, g, m, v)


def sum_leading(t, name, out_dtype=F32):
    S, R, C = t.shape
    tr = _tile(R, 256, 16)

    def body(t_ref, o_ref):
        acc = t_ref[0].astype(F32)
        for s in range(1, S):
            acc = acc + t_ref[s].astype(F32)
        o_ref[...] = acc.astype(out_dtype)

    return pl.pallas_call(
        body, grid=(R // tr,), in_specs=[pl.BlockSpec((S, tr, C), lambda i: (0, i, 0))],
        out_specs=pl.BlockSpec((tr, C), lambda i: (i, 0)), out_shape=jax.ShapeDtypeStruct((R, C), out_dtype),
        compiler_params=_cp(("parallel",)), name=name)(t)


def add_half(g, a, core, name, by_cols=False):
    S, R, C = g.shape

    def body(core_ref, g_ref, a_ref, o_ref):
        o_ref[...] = (g_ref[...].astype(F32) + a_ref[...].astype(F32)).astype(BF16)

    if by_cols:
        hc = C // 2
        tr = _tile(R, 256, 16)
        return pl.pallas_call(
            body,
            grid_spec=pltpu.PrefetchScalarGridSpec(
                num_scalar_prefetch=1, grid=(S, R // tr),
                in_specs=[pl.BlockSpec((None, tr, hc), lambda s, i, core_ref: (s, i, core_ref[0])),
                          pl.BlockSpec((None, tr, hc), lambda s, i, core_ref: (s, i, 0))],
                out_specs=pl.BlockSpec((None, tr, hc), lambda s, i, core_ref: (s, i, 0))),
            out_shape=jax.ShapeDtypeStruct((S, R, hc), BF16), compiler_params=_cp(("parallel", "parallel")),
            name=name)(core, g, a)
    h = R // 2
    tr = _tile(h, 256, 16)
    nb = h // tr

    return pl.pallas_call(
        body,
        grid_spec=pltpu.PrefetchScalarGridSpec(
            num_scalar_prefetch=1, grid=(S, nb),
            in_specs=[pl.BlockSpec((None, tr, C), lambda s, i, core_ref: (s, core_ref[0] * nb + i, 0)),
                      pl.BlockSpec((None, tr, C), lambda s, i, core_ref: (s, i, 0))],
            out_specs=pl.BlockSpec((None, tr, C), lambda s, i, core_ref: (s, i, 0))),
        out_shape=jax.ShapeDtypeStruct((S, h, C), BF16), compiler_params=_cp(("parallel", "parallel")),
        name=name)(core, g, a)


def sum_partials(own, landed, chip, name):
    _, h, C = own.shape
    tr = _tile(h, 256, 16)

    def body(chip_ref, own_ref, l_ref, o_ref):
        acc = own_ref[...].astype(F32)
        for j in range(3):
            acc = acc + l_ref[j].astype(F32)
        o_ref[...] = acc

    return pl.pallas_call(
        body,
        grid_spec=pltpu.PrefetchScalarGridSpec(
            num_scalar_prefetch=1, grid=(h // tr,),
            in_specs=[pl.BlockSpec((None, tr, C), lambda i, chip_ref: (chip_ref[0], i, 0)),
                      pl.BlockSpec((3, tr, C), lambda i, chip_ref: (0, i, 0))],
            out_specs=pl.BlockSpec((tr, C), lambda i, chip_ref: (i, 0))),
        out_shape=jax.ShapeDtypeStruct((h, C), F32), compiler_params=_cp(("parallel",)), name=name)(chip, own, landed)


def adamw_halves(w, g_mine, g_theirs, m, v, core, name):
    R, C = w.shape
    h = R // 2
    tr = _tile(h, 256, 8)
    nbh = h // tr
    c1 = 1.0 - ADAM_B1 ** ADAM_STEP
    c2 = 1.0 - ADAM_B2 ** ADAM_STEP

    def body(core_ref, w_ref, gm_ref, gt_ref, m_ref, v_ref, g_ref, d_ref, nm_ref, nv_ref):
        mine = (pl.program_id(0) // nbh) == core_ref[0]
        gv = jnp.where(mine, gm_ref[...], gt_ref[...])
        g_ref[...] = gv
        nm = ADAM_B1 * m_ref[...] + (1.0 - ADAM_B1) * gv
        nv = ADAM_B2 * v_ref[...] + (1.0 - ADAM_B2) * (gv * gv)
        nm_ref[...] = nm
        nv_ref[...] = nv
        d_ref[...] = -ADAM_LR * ((nm / c1) / (jnp.sqrt(nv / c2) + ADAM_EPS) + ADAM_WD * w_ref[...])

    full = pl.BlockSpec((tr, C), lambda i, core_ref: (i, 0))
    halfspec = pl.BlockSpec((tr, C), lambda i, core_ref: (i % nbh, 0))
    return pl.pallas_call(
        body,
        grid_spec=pltpu.PrefetchScalarGridSpec(
            num_scalar_prefetch=1, grid=(2 * nbh,), in_specs=[full, halfspec, halfspec, full, full],
            out_specs=[full] * 4),
        out_shape=[jax.ShapeDtypeStruct((R, C), F32)] * 4, compiler_params=_cp(("parallel",)),
        name=name)(core, w, g_mine, g_theirs, m, v)


_ANY = pl.BlockSpec(memory_space=pl.ANY)


def _place():
    x, y, c = lax.axis_index("x"), lax.axis_index("y"), lax.axis_index("c")
    chips = [(1 - x, y), (x, 1 - y), (1 - x, 1 - y)]
    return x, y, c, chips


def allgather_small(v, name, after=None):
    R, W = v.shape
    extra = [] if after is None else [after]

    def body(x_ref, *rest):
        out_ref, send_sems, recv_sems, local_sem = rest[len(extra):]
        x, y, c, chips = _place()
        me, sibling = (x, y, c), (x, y, 1 - c)

        def rows(px, py, pc):
            return out_ref.at[pl.ds((4 * px + 2 * py + pc) * R, R), :]

        def copy(k, block, to, src=None):
            return pltpu.make_async_remote_copy(
                src_ref=rows(*block) if src is None else src, dst_ref=rows(*block),
                send_sem=send_sems.at[k], recv_sem=recv_sems.at[k], device_id=to, device_id_type=MESH)

        mine = pltpu.make_async_copy(x_ref, rows(*me), local_sem)
        mine.start()
        first = [copy(0, me, sibling, src=x_ref)]
        first += [copy(1 + j, me, (*chip, c), src=x_ref) for j, chip in enumerate(chips)]
        for cp in first:
            cp.start()
        passed = [copy(4 + j, (*chip, c), sibling) for j, chip in enumerate(chips)]
        for j, chip in enumerate(chips):
            copy(1 + j, (*chip, c), me).wait_recv()
            passed[j].start()
        copy(0, sibling, me).wait_recv()
        for j, chip in enumerate(chips):
            copy(4 + j, (*chip, 1 - c), me).wait_recv()
        for cp in first + passed:
            cp.wait_send()
        mine.wait()

    return pl.pallas_call(
        body, out_shape=jax.ShapeDtypeStruct((N_DEV * R, W), v.dtype),
        in_specs=[pl.BlockSpec(memory_space=pltpu.VMEM)] + [_ANY] * len(extra),
        out_specs=pl.BlockSpec(memory_space=pltpu.VMEM),
        scratch_shapes=[pltpu.SemaphoreType.DMA((7,)), pltpu.SemaphoreType.DMA((7,)), pltpu.SemaphoreType.DMA],
        name=name)(v, *extra)


def allgather_weights(shards, name="allgather_weights", by_cols=False):
    n = len(shards)

    def body(*refs):
        ins, outs = refs[:n], refs[n:2 * n]
        send_sems, recv_sems = refs[2 * n:]
        x, y, c, chips = _place()
        p = 2 * x + y
        sibling = (x, y, 1 - c)

        def half(i, chip_id, core, ref=None):
            r = outs[i].at[chip_id] if ref is None else ref
            if by_cols:
                hc = r.shape[1] // 2
                return r.at[:, pl.ds(pl.multiple_of(core * hc, 128), hc)]
            return r.at[core]

        def copy(i, k, chip_id, core, to, src=None):
            return pltpu.make_async_remote_copy(
                src_ref=half(i, chip_id, core) if src is None else src, dst_ref=half(i, chip_id, core),
                send_sem=send_sems.at[6 * i + k], recv_sem=recv_sems.at[6 * i + k], device_id=to, device_id_type=MESH)

        first = [copy(i, j, p, c, (*chip, c), src=half(i, p, c, ref=ins[i]))
                 for i in range(n) for j, chip in enumerate(chips)]
        for cp in first:
            cp.start()
        passed = []
        for i in range(n):
            for j, (cx, cy) in enumerate(chips):
                copy(i, j, 2 * cx + cy, c, sibling).wait_recv()
                fw = copy(i, 3 + j, 2 * cx + cy, c, sibling)
                fw.start()
                passed.append(fw)
        for i in range(n):
            for j, (cx, cy) in enumerate(chips):
                copy(i, 3 + j, 2 * cx + cy, 1 - c, sibling).wait_recv()
        for cp in first + passed:
            cp.wait_send()

    split = list(shards) if by_cols else [s.reshape(2, s.shape[0] // 2, s.shape[1]) for s in shards]
    outs = pl.pallas_call(
        body, out_shape=[jax.ShapeDtypeStruct((N_CHIPS,) + s.shape, s.dtype) for s in split],
        in_specs=[_ANY] * n, out_specs=[_ANY] * n,
        scratch_shapes=[pltpu.SemaphoreType.DMA((6 * n,)), pltpu.SemaphoreType.DMA((6 * n,))],
        name=name)(*split)
    chip = 2 * lax.axis_index("x") + lax.axis_index("y")
    return [lax.dynamic_update_index_in_dim(o, s, chip, 0).reshape((N_CHIPS,) + sh.shape)
            for o, s, sh in zip(outs, split, shards)]


def allgather_routed(shard, name):
    R, C = shard.shape
    hc = C // 2
    ra = (R // 2) // 16 * 16

    def body(in_ref, out_ref, send_sems, recv_sems):
        x, y, c, _ = _place()
        xn, yn = (1 - x, y, c), (x, 1 - y, c)
        sibling = (x, y, 1 - c)
        p, pxn, pyn, pdg = 2 * x + y, 2 * (1 - x) + y, 2 * x + (1 - y), 2 * (1 - x) + (1 - y)
        rows_a, rows_b, rows_all = pl.ds(0, ra), pl.ds(ra, R - ra), pl.ds(0, R)

        def win(ref, rows, core):
            return ref.at[rows, pl.ds(pl.multiple_of(core * hc, 128), hc)]

        def copy(k, chip_id, rows, core, to, src=None):
            blk = win(out_ref.at[chip_id], rows, core)
            return pltpu.make_async_remote_copy(
                src_ref=blk if src is None else src, dst_ref=blk, send_sem=send_sems.at[k], recv_sem=recv_sems.at[k],
                device_id=to, device_id_type=MESH)

        own = [copy(0, p, rows_a, c, xn, src=win(in_ref, rows_a, c)), copy(1, p, rows_b, c, xn, src=win(in_ref, rows_b, c)),
               copy(2, p, rows_b, c, yn, src=win(in_ref, rows_b, c)), copy(3, p, rows_a, c, yn, src=win(in_ref, rows_a, c))]
        for cp in own:
            cp.start()
        copy(0, pxn, rows_a, c, xn).wait_recv()
        fwd_a = copy(4, pxn, rows_a, c, yn)
        fwd_a.start()
        copy(2, pyn, rows_b, c, yn).wait_recv()
        fwd_b = copy(5, pyn, rows_b, c, xn)
        fwd_b.start()
        copy(1, pxn, rows_b, c, xn).wait_recv()
        copy(3, pyn, rows_a, c, yn).wait_recv()
        passed = [copy(6, pxn, rows_all, c, sibling), copy(7, pyn, rows_all, c, sibling)]
        for cp in passed:
            cp.start()
        copy(4, pdg, rows_a, c, yn).wait_recv()
        passed.append(copy(8, pdg, rows_a, c, sibling))
        passed[-1].start()
        copy(5, pdg, rows_b, c, xn).wait_recv()
        passed.append(copy(9, pdg, rows_b, c, sibling))
        passed[-1].start()
        for k, (chip_id, rows) in enumerate([(pxn, rows_all), (pyn, rows_all), (pdg, rows_a), (pdg, rows_b)]):
            copy(6 + k, chip_id, rows, 1 - c, sibling).wait_recv()
        for cp in own + [fwd_a, fwd_b] + passed:
            cp.wait_send()

    out = pl.pallas_call(
        body, out_shape=jax.ShapeDtypeStruct((N_CHIPS, R, C), shard.dtype), in_specs=[_ANY], out_specs=_ANY,
        scratch_shapes=[pltpu.SemaphoreType.DMA((10,)), pltpu.SemaphoreType.DMA((10,))], name=name)(shard)
    chip = 2 * lax.axis_index("x") + lax.axis_index("y")
    return lax.dynamic_update_index_in_dim(out, shard, chip, 0)


_HBM = pl.BlockSpec(memory_space=pltpu.HBM)
_SEM = pl.BlockSpec(memory_space=pltpu.SEMAPHORE)
_EFFECT = pltpu.SideEffectType.DATAFLOW_SIDE_EFFECTING


def _chip_copies(kind, srcs, lands, send_sems, recv_sems):
    x, y, c, chips = _place()
    p = 2 * x + y
    cps = []
    if kind == "sibling":
        for i in range(len(srcs)):
            h = srcs[i].shape[1] // 2
            cps.append(pltpu.make_async_remote_copy(
                src_ref=srcs[i].at[:, pl.ds((1 - c) * h, h), :], dst_ref=lands[i], send_sem=send_sems.at[3 * i],
                recv_sem=recv_sems.at[3 * i], device_id=(x, y, 1 - c), device_id_type=MESH))
        return cps
    for i in range(len(srcs)):
        for j, (cx, cy) in enumerate(chips):
            if kind == "gather":
                src, dst = srcs[i].at[c], lands[i].at[p, c]
            else:
                src, dst = srcs[i].at[2 * cx + cy], lands[i].at[j]
            cps.append(pltpu.make_async_remote_copy(
                src_ref=src, dst_ref=dst, send_sem=send_sems.at[3 * i + j], recv_sem=recv_sems.at[3 * i + j],
                device_id=(cx, cy, c), device_id_type=MESH))
    return cps


def split_start(kind, srcs, land_shapes, after, name):
    n = len(srcs)

    def body(*refs):
        src_refs, land_refs = refs[:n], refs[n:2 * n]
        send_sems, recv_sems = refs[2 * n + 1], refs[2 * n + 2]
        token = refs[-1]
        for cp in _chip_copies(kind, src_refs, land_refs, send_sems, recv_sems):
            cp.start()
        token[...] = jnp.zeros_like(token)

    lands = [pltpu.with_memory_space_constraint(lax.empty(s, BF16), pltpu.HBM) for s in land_shapes]
    outs = pl.pallas_call(
        body, name=name,
        out_shape=(pltpu.SemaphoreType.DMA((3 * n,)), pltpu.SemaphoreType.DMA((3 * n,)),
                   *[pltpu.HBM(s.shape, s.dtype) for s in srcs], *[pltpu.HBM(s, BF16) for s in land_shapes],
                   jax.ShapeDtypeStruct((8, 128), F32)),
        in_specs=[_HBM] * (2 * n) + [_ANY],
        out_specs=(_SEM, _SEM, *([_HBM] * (2 * n)), pl.BlockSpec(memory_space=pltpu.VMEM)),
        input_output_aliases={i: 2 + i for i in range(2 * n)},
        compiler_params=pltpu.CompilerParams(has_side_effects=_EFFECT),
    )(*[pltpu.with_memory_space_constraint(s, pltpu.HBM) for s in srcs], *lands, after)
    return outs[0], outs[1], outs[2:2 + n], outs[2 + n:2 + 2 * n], outs[-1]


def split_wait(kind, send_sems, recv_sems, srcs, lands, after, name):
    n = len(srcs)

    def body(*refs):
        src_refs, land_refs = refs[:n], refs[n:2 * n]
        ssem, rsem = refs[2 * n], refs[2 * n + 1]
        for cp in _chip_copies(kind, src_refs, land_refs, ssem, rsem):
            cp.wait_send()
            cp.wait_recv()

    outs = pl.pallas_call(
        body, name=name,
        out_shape=[pltpu.HBM(s.shape, s.dtype) for s in srcs] + [pltpu.HBM(s.shape, s.dtype) for s in lands],
        in_specs=[_HBM] * (2 * n) + [_SEM, _SEM, _ANY], out_specs=[_HBM] * (2 * n),
        input_output_aliases={i: i for i in range(2 * n)},
        compiler_params=pltpu.CompilerParams(has_side_effects=_EFFECT),
    )(*srcs, *lands, send_sems, recv_sems, after)
    return outs[:n], outs[n:]


def pass_to_sibling(lands):
    n = len(lands)

    def body(*refs):
        ins, outs = refs[:n], refs[n:2 * n]
        send_sems, recv_sems = refs[2 * n:]
        x, y, c, chips = _place()
        cps = []
        for i in range(n):
            for j, (cx, cy) in enumerate(chips):
                blk = outs[i].at[2 * cx + cy, c]
                cps.append(pltpu.make_async_remote_copy(
                    src_ref=ins[i].at[2 * cx + cy, c], dst_ref=blk, send_sem=send_sems.at[3 * i + j],
                    recv_sem=recv_sems.at[3 * i + j], device_id=(x, y, 1 - c), device_id_type=MESH))
        for cp in cps:
            cp.start()
        for cp in cps:
            cp.wait()

    return pl.pallas_call(
        body, out_shape=[jax.ShapeDtypeStruct(t.shape, t.dtype) for t in lands], in_specs=[_ANY] * n,
        out_specs=[_ANY] * n, input_output_aliases={i: i for i in range(n)},
        scratch_shapes=[pltpu.SemaphoreType.DMA((3 * n,)), pltpu.SemaphoreType.DMA((3 * n,))],
        name="ag_pass_to_sibling")(*lands)


def exchange_halves_to_sibling(gs, name, by_cols=False):
    n = len(gs)

    def body(*refs):
        ins, outs = refs[:n], refs[n:2 * n]
        send_sems, recv_sems = refs[2 * n:]
        x, y, c, _ = _place()
        cps = []
        for i in range(n):
            if by_cols:
                hc = ins[i].shape[2] // 2
                src = ins[i].at[:, :, pl.ds(pl.multiple_of((1 - c) * hc, 128), hc)]
            else:
                h = ins[i].shape[1] // 2
                src = ins[i].at[:, pl.ds((1 - c) * h, h), :]
            cps.append(pltpu.make_async_remote_copy(
                src_ref=src, dst_ref=outs[i],
                send_sem=send_sems.at[i], recv_sem=recv_sems.at[i], device_id=(x, y, 1 - c), device_id_type=MESH))
        for cp in cps:
            cp.start()
        for cp in cps:
            cp.wait()

    halve = (lambda s: (s[0], s[1], s[2] // 2)) if by_cols else (lambda s: (s[0], s[1] // 2, s[2]))
    return pl.pallas_call(
        body, out_shape=[jax.ShapeDtypeStruct(halve(g.shape), g.dtype) for g in gs],
        in_specs=[_ANY] * n, out_specs=[_ANY] * n,
        scratch_shapes=[pltpu.SemaphoreType.DMA((n,)), pltpu.SemaphoreType.DMA((n,))],
        name=name)(*gs)


def scatter_to_chips(ps, name):
    n = len(ps)

    def body(*refs):
        ins, outs = refs[:n], refs[n:2 * n]
        send_sems, recv_sems = refs[2 * n:]
        x, y, c, chips = _place()
        cps = []
        for i in range(n):
            for j, (cx, cy) in enumerate(chips):
                cps.append(pltpu.make_async_remote_copy(
                    src_ref=ins[i].at[2 * cx + cy], dst_ref=outs[i].at[j], send_sem=send_sems.at[3 * i + j],
                    recv_sem=recv_sems.at[3 * i + j], device_id=(cx, cy, c), device_id_type=MESH))
        for cp in cps:
            cp.start()
        for cp in cps:
            cp.wait()

    return pl.pallas_call(
        body, out_shape=[jax.ShapeDtypeStruct((3,) + t.shape[1:], t.dtype) for t in ps],
        in_specs=[_ANY] * n, out_specs=[_ANY] * n,
        scratch_shapes=[pltpu.SemaphoreType.DMA((3 * n,)), pltpu.SemaphoreType.DMA((3 * n,))],
        name=name)(*ps)


def join_halves(rs, name):
    n = len(rs)

    def body(*refs):
        ins, outs = refs[:n], refs[n:2 * n]
        send_sems, recv_sems = refs[2 * n:]
        x, y, c, _ = _place()
        cps = [pltpu.make_async_remote_copy(
            src_ref=ins[i], dst_ref=outs[i], send_sem=send_sems.at[i], recv_sem=recv_sems.at[i],
            device_id=(x, y, 1 - c), device_id_type=MESH) for i in range(n)]
        for cp in cps:
            cp.start()
        for cp in cps:
            cp.wait()

    return pl.pallas_call(
        body, out_shape=[jax.ShapeDtypeStruct(r.shape, r.dtype) for r in rs],
        in_specs=[_ANY] * n, out_specs=[_ANY] * n,
        scratch_shapes=[pltpu.SemaphoreType.DMA((n,)), pltpu.SemaphoreType.DMA((n,))],
        name=name)(*rs)


def _pack(parts, row_mult=8):
    flat = jnp.concatenate([p.reshape(-1).astype(F32) for p in parts])
    unit = row_mult * 128
    n = -(-flat.shape[0] // unit) * unit
    return jnp.pad(flat, (0, n - flat.shape[0])).reshape(n // 128, 128)


def _unpack(flat, shapes):
    out, off = [], 0
    for s in shapes:
        n = int(np.prod(s))
        out.append(flat[off:off + n].reshape(s))
        off += n
    return out


def _gather_packed(parts, name):
    packed = _pack(parts)
    g = allgather_small(packed, name).reshape(N_DEV, -1)
    return _unpack_rows(g, [p.shape for p in parts])


def _unpack_rows(g, shapes):
    out, off = [], 0
    for s in shapes:
        n = int(np.prod(s))
        out.append(g[:, off:off + n].reshape((g.shape[0],) + tuple(s)))
        off += n
    return out


def _by_chip(t, axis):
    return jnp.concatenate([t[2 * p] for p in range(N_CHIPS)], axis=axis)


def kernel(x, c, ada_w, ada_b, ln_g, ln_b, a_in_w, a_conv_w, a_conv_b, a_dt_bias, a_A_log, a_D, a_norm_g, a_out_w, kv_w, b_in_w, b_out_w, loss_target, m_ada_w, m_ada_b, m_ln_g, m_ln_b, m_a_in_w, m_a_conv_w, m_a_conv_b, m_a_dt_bias, m_a_A_log, m_a_D, m_a_norm_g, m_a_out_w, m_kv_w, m_b_in_w, m_b_out_w, v_ada_w, v_ada_b, v_ln_g, v_ln_b, v_a_in_w, v_a_conv_w, v_a_conv_b, v_a_dt_bias, v_a_A_log, v_a_D, v_a_norm_g, v_a_out_w, v_kv_w, v_b_in_w, v_b_out_w):
    ax, ay, ac = lax.axis_index("x"), lax.axis_index("y"), lax.axis_index("c")
    chip = 2 * ax + ay
    dev = 4 * ax + 2 * ay + ac
    xin = x[0]
    tgt = loss_target[0]
    L, D = xin.shape
    G, P = SSD_G, SSD_P
    H = a_dt_bias.shape[1]
    Kh = H // G
    DI = H * P
    CONVD = a_conv_b.shape[1] * N_CHIPS
    HW = DIL_H * DIL_E
    Ws = ada_w.shape[2]

    w_in_g = allgather_routed(jnp.transpose(a_in_w[0]).astype(BF16), "allgather_w_in")
    later = [a_out_w[0].astype(BF16), kv_w.astype(BF16), b_in_w[0].astype(BF16), b_out_w[0].astype(BF16)]
    later_split = [s.reshape(2, s.shape[0] // 2, s.shape[1]) for s in later]
    ag_ssem, ag_rsem, ag_srcs, ag_lands, ag_token = split_start(
        "gather", later_split, [(N_CHIPS,) + s.shape for s in later_split], w_in_g, "ag_later_start")
    w_in_t = w_in_g.reshape(-1, D)
    w_dt_t = jnp.pad(w_in_t[DI + CONVD:], ((0, 128 - H), (0, 0)))

    c8, cw8, cb8, ng8 = _gather_packed([c[0], a_conv_w[0], a_conv_b[0], a_norm_g[0]], "allgather_small_params")
    conv_w = _by_chip(cw8, 1)
    conv_b = _by_chip(cb8, 0).reshape(1, CONVD)
    norm_g = _by_chip(ng8, 0).reshape(1, DI)

    mod_s = ada_fwd(c8, ada_w)
    (mod8,) = _gather_packed([mod_s], "allgather_small_mod")
    mods = _by_chip(mod8, 2)
    mod = lax.dynamic_index_in_dim(mods, dev, axis=1, keepdims=False) + ada_b
    shift = [mod[l:l + 1, :D] for l in range(DEPTH)]
    scale = [mod[l:l + 1, D:2 * D] for l in range(DEPTH)]
    gate = [mod[l:l + 1, 2 * D:] for l in range(DEPTH)]
    lg = [ln_g[l:l + 1] for l in range(DEPTH)]
    lb = [ln_b[l:l + 1] for l in range(DEPTH)]

    h0 = modulate(xin, scale[0] + ag_token[0:1, 0:1], shift[0], "modulate0")
    zx = mm_nt(h0, w_in_t, BF16, "mm_in_zx", kw_rows=DI + CONVD)
    dtp = mm_nt(h0, w_dt_t, F32, "mm_in_dt")
    xbc = conv_fwd(zx, DI, conv_w, conv_b)
    dtp_g = jnp.transpose(dtp[:, :H].reshape(L, G, Kh), (1, 0, 2))
    dtp_gT = jnp.transpose(dtp_g, (0, 2, 1))
    vecs = [a_dt_bias.reshape(G, 1, Kh), a_dt_bias.reshape(G, Kh, 1), a_A_log.reshape(G, 1, Kh),
            a_A_log.reshape(G, Kh, 1), a_D.reshape(G, 1, Kh), a_D.reshape(G, Kh, 1)]
    y_ssd, states = ssd_fwd(xbc, dtp_g, dtp_gT, *vecs, DI)
    yn = rms_gate_fwd(y_ssd, zx, norm_g)
    later_split, ag_lands = split_wait("gather", ag_ssem, ag_rsem, ag_srcs, ag_lands, yn, "ag_later_wait")
    ag_lands = pass_to_sibling(ag_lands)
    w_out_g, w_kv_g, w_bin_g, w_bout_g = [
        lax.dynamic_update_index_in_dim(o, s, chip, 0).reshape((N_CHIPS,) + full.shape)
        for o, s, full in zip(ag_lands, later_split, later)]
    ymix0 = mm_nn(yn, w_out_g.reshape(-1, D), F32, "mm_out_a")
    x1, x1b, h1 = ln_mid(xin, ymix0, gate[0], lg[0], lb[0], scale[1], shift[1])

    n_grp = len(DIL_PATTERNS)
    cb = HW // 512
    assert w_bin_g.shape[2] == HW
    kv3 = [mm_cols_dilated(x1b, w_kv_g, [g * cb + t for t in range(cb)] + [(n_grp + g) * cb + t for t in range(cb)],
                           DIL_PATTERNS[g][1], f"mm_kv_{g}") for g in range(n_grp)]
    q3 = [mm_cols_dilated(h1, w_bin_g, [g * cb + t for t in range(cb)], DIL_PATTERNS[g][1], f"mm_q_{g}")
          for g in range(n_grp)]
    z_b = mm_nn(h1, w_bin_g[n_grp], BF16, "mm_z_b")
    os_, lses = [], []
    for gi in range(len(DIL_PATTERNS)):
        o, lse = attn_fwd(q3[gi], kv3[gi], gi)
        os_.append(o)
        lses.append(lse)
    om = merge_fwd(os_, lses, z_b)
    ymix1 = mm_nn(om, w_bout_g, F32, "mm_out_b", stack="col")
    dres2, dy2, dg1, db1, dgate1, sq = ln_final_fwd_bwd(x1, ymix1, gate[1], lg[1], lb[1], tgt)
    loss_part = 0.5 * jnp.sum(sq) / D

    g_bout = mm_tn(om, dy2, BF16, "mm_gw_out_b", stack="col")
    dgated = mm_nt(dy2, w_bout_g, BF16, "mm_gx_out_b", stack="col")
    dos, dprs, dz_b = merge_bwd(dgated, os_, lses, z_b)
    dqs, dks, dvs = [], [], []
    for gi in range(len(DIL_PATTERNS)):
        dq, dk, dv = attn_bwd(q3[gi], kv3[gi], dos[gi], lses[gi], dprs[gi], gi)
        dqs.append(dq)
        dks.append(dk)
        dvs.append(dv)
    dqz = jnp.concatenate(dqs + [dz_b], axis=1)
    dkv = jnp.concatenate(dks + dvs, axis=1)
    g_bin = mm_tn(h1, dqz, BF16, "mm_gw_in_b", stack="col")
    dh1 = mm_nt(dqz, w_bin_g, BF16, "mm_gx_in_b", stack="col")
    g_kv = mm_tn(x1b, dkv, BF16, "mm_gw_kv", stack="col")

    core = ac.astype(jnp.int32).reshape(1)
    chip_i = chip.astype(jnp.int32).reshape(1)

    def begin_exchange(gs, tag):
        shapes = [(g.shape[0], g.shape[1] // 2, g.shape[2]) for g in gs]
        return split_start("sibling", gs, shapes, gs[0], "rs_x%s_start" % tag)

    def begin_scatter(gs, nms, tag, exchange=None, after=None, by_cols=False):
        if exchange is None:
            sib = exchange_halves_to_sibling(gs, "rs_sibling_exchange_" + tag, by_cols=by_cols)
        else:
            gs, sib = split_wait("sibling", exchange[0], exchange[1], exchange[2], exchange[3], after,
                                 "rs_x%s_wait" % tag)
        parts = [add_half(g, a, core, "rs_add_" + nm, by_cols=by_cols) for g, a, nm in zip(gs, sib, nms)]
        return split_start("scatter", parts, [(3,) + t.shape[1:] for t in parts], parts[0], "rs_%s_start" % tag)

    def finish_scatter(handles, after, tag):
        nms, owns, landed = [], [], []
        for k, (handle, hn) in enumerate(handles):
            parts, lands = split_wait("scatter", handle[0], handle[1], handle[2], handle[3], after,
                                      "rs_%s%d_wait" % (tag, k))
            nms += hn
            owns += list(parts)
            landed += list(lands)
        halves = [sum_partials(own, t, chip_i, "rs_sum_" + nm) for own, t, nm in zip(owns, landed, nms)]
        theirs = join_halves(halves, "rs_join_halves_" + tag)
        return dict(zip(nms, zip(halves, theirs)))

    names_b = ["kv", "in_b", "out_b"]
    ex_b = begin_exchange([g_kv, g_bin, g_bout], "b")
    dx1_kv = mm_nt(dkv, w_kv_g, BF16, "mm_gx_kv", stack="col", after=ex_b[4])
    rs_b = begin_scatter(None, names_b, "b", exchange=ex_b, after=dx1_kv)

    dres1, dy1, dg0, db0, dgate0, dscale1, dshift1 = mod_ln_bwd(
        dres2, dh1, dx1_kv, x1, scale[1], xin, ymix0, gate[0] + rs_b[4][0:1, 0:1], lg[0])
    g_out = mm_tn(yn, dy1, BF16, "mm_gw_out_a", stack="row")
    ex_a1 = begin_exchange([g_out], "a1")
    dyn = mm_nt(dy1, w_out_g, BF16, "mm_gx_out_a", stack="row", after=ex_a1[4])
    rs_a1 = begin_scatter(None, ["out_a"], "a1", exchange=ex_a1, after=dyn)
    dy_ssd, dz_a, dnorm_g = rms_gate_bwd(dyn, y_ssd, zx, norm_g + rs_a1[4][0:1, 0:1])
    dxs, dB, dC, ddtp_g, dbias_g, dalog_g, dD_g = ssd_bwd(xbc, dtp_g, dtp_gT, *vecs, states, dy_ssd, DI)
    dxbc = jnp.concatenate([dxs, dB, dC], axis=1)
    dxbc_pre, dconv_w, dconv_b = conv_bwd(zx, DI, conv_w, conv_b, dxbc)
    dzx = jnp.concatenate([dz_a, dxbc_pre], axis=1)
    ddtp = jnp.pad(jnp.transpose(ddtp_g, (1, 0, 2)).reshape(L, H), ((0, 0), (0, 128 - H)))
    g_inT = mm_tn(dzx, h0, BF16, "mm_gw_in_zx", m_rows=DI + CONVD + H)
    g_dtT = mm_tn(ddtp, h0, BF16, "mm_gw_in_dt")
    g_inT = lax.dynamic_update_slice(g_inT, g_dtT[:H], (DI + CONVD, 0))
    rs_a2 = begin_scatter([g_inT.reshape(N_CHIPS, -1, D)], ["in_a"], "a2", by_cols=True)
    dh0 = mm_nn(dzx, w_in_t, BF16, "mm_gx_in_zx", after=rs_a2[4])
    dh0_dt = mm_nn(ddtp, w_dt_t, F32, "mm_gx_in_dt")
    grad_x, dscale0, dshift0 = mod_bwd(dres1, dh0, dh0_dt, xin, scale[0] + rs_a2[4][0:1, 0:1], "mod_bwd0",
                                       through_mod=True)
    g_halves = finish_scatter([(rs_b, names_b)], grad_x, "b")

    def step_halves(w, m, v, nm):
        shp = w.shape
        mine, theirs_ = g_halves[nm]
        outs4 = adamw_halves(w.reshape(-1, shp[-1]), mine, theirs_, m.reshape(-1, shp[-1]), v.reshape(-1, shp[-1]),
                             core, "adamw_" + nm)
        return tuple(t.reshape(shp) for t in outs4)

    big = {
        "kv_w": step_halves(kv_w, m_kv_w, v_kv_w, "kv"),
        "b_in_w": step_halves(b_in_w, m_b_in_w, v_b_in_w, "in_b"),
        "b_out_w": step_halves(b_out_w, m_b_out_w, v_b_out_w, "out_b"),
    }
    g_halves.update(finish_scatter([(rs_a1, ["out_a"]), (rs_a2, ["in_a"])], big["kv_w"][1], "a"))
    g_halves["in_a"] = tuple(jnp.transpose(t) for t in g_halves["in_a"])
    big["a_in_w"] = step_halves(a_in_w, m_a_in_w, v_a_in_w, "in_a")
    big["a_out_w"] = step_halves(a_out_w, m_a_out_w, v_a_out_w, "out_a")

    dmod = jnp.concatenate([jnp.concatenate([dshift0, dscale0, dgate0], axis=1),
                            jnp.concatenate([dshift1, dscale1, dgate1], axis=1)], axis=0)
    small_parts = [jnp.concatenate([dg0, dg1], axis=0), jnp.concatenate([db0, db1], axis=0),
                   dbias_g.reshape(1, H), dalog_g.reshape(1, H), dD_g.reshape(1, H),
                   dconv_w, dconv_b, dnorm_g, loss_part.reshape(1, 1)]
    small_shapes = [p.shape for p in small_parts]
    packed = jnp.concatenate([_pack([dmod]), _pack(small_parts)], axis=0)
    n_mod_rows = _pack([dmod]).shape[0]
    gathered = allgather_small(packed, "allgather_small_grads", after=g_halves["in_a"][1]).reshape(N_DEV, -1, 128)
    dmod8 = gathered[:, :n_mod_rows].reshape(N_DEV, -1)[:, :2 * 3 * D].reshape(N_DEV, DEPTH, 3 * D)
    summed = sum_leading(gathered, "sum_small")
    g_ada_b = summed[:n_mod_rows].reshape(-1)[:2 * 3 * D].reshape(DEPTH, 3 * D)
    (g_ln_g, g_ln_b, g_dt_bias, g_a_log, g_dsk, g_conv_w, g_conv_b, g_norm_g, loss_all) = _unpack(
        summed[n_mod_rows:].reshape(-1), small_shapes)
    loss = loss_all.reshape(())
    Cs = CONVD // N_CHIPS
    g_conv_w_s = lax.dynamic_slice_in_dim(g_conv_w, chip * Cs, Cs, axis=1)
    g_conv_b_s = lax.dynamic_slice_in_dim(g_conv_b, chip * Cs, Cs, axis=1)
    g_norm_g_s = lax.dynamic_slice_in_dim(g_norm_g, chip * (DI // N_CHIPS), DI // N_CHIPS, axis=1)
    dmod_s = jnp.transpose(lax.dynamic_slice_in_dim(dmod8, chip * Ws, Ws, axis=2), (1, 0, 2))

    def step2d(w, g, m, v, nm):
        shp = w.shape
        d_, m_, v_ = adamw(w.reshape(-1, shp[-1]), g.reshape(-1, shp[-1]), m.reshape(-1, shp[-1]),
                           v.reshape(-1, shp[-1]), "adamw_" + nm)
        return g.reshape(shp), d_.reshape(shp), m_.reshape(shp), v_.reshape(shp)

    big["ada_w"] = step2d(ada_w, ada_wgrad(jnp.transpose(c8), dmod_s), m_ada_w, v_ada_w, "ada_w")
    small_names = ["ada_b", "ln_g", "ln_b", "a_conv_w", "a_conv_b", "a_dt_bias", "a_A_log", "a_D", "a_norm_g"]
    small_w = [ada_b, ln_g, ln_b, a_conv_w, a_conv_b, a_dt_bias, a_A_log, a_D, a_norm_g]
    small_m = [m_ada_b, m_ln_g, m_ln_b, m_a_conv_w, m_a_conv_b, m_a_dt_bias, m_a_A_log, m_a_D, m_a_norm_g]
    small_v = [v_ada_b, v_ln_g, v_ln_b, v_a_conv_w, v_a_conv_b, v_a_dt_bias, v_a_A_log, v_a_D, v_a_norm_g]
    small_g = [g_ada_b, g_ln_g, g_ln_b, g_conv_w_s, g_conv_b_s, g_dt_bias, g_a_log, g_dsk, g_norm_g_s]
    shapes = [w.shape for w in small_w]
    small_g = [g.reshape(s) for g, s in zip(small_g, shapes)]
    d_p, m_p, v_p = adamw(_pack(small_w), _pack(small_g), _pack(small_m), _pack(small_v), "adamw_small")
    small = {}
    for nm, g, d_, m_, v_ in zip(small_names, small_g, _unpack(d_p.reshape(-1), shapes), _unpack(m_p.reshape(-1), shapes),
                                 _unpack(v_p.reshape(-1), shapes)):
        small[nm] = (g, d_, m_, v_)
    allw = {**big, **small}
    order = ["ada_w", "ada_b", "ln_g", "ln_b", "a_in_w", "a_conv_w", "a_conv_b", "a_dt_bias", "a_A_log", "a_D",
             "a_norm_g", "a_out_w", "kv_w", "b_in_w", "b_out_w"]
    outs = [loss, grad_x.reshape(x.shape)]
    for k in range(4):
        outs += [allw[n][k] for n in order]
    return tuple(outs)
```

```python
import functools

import jax
import jax.numpy as jnp
import numpy as np
from jax import lax
from jax.experimental import pallas as pl
from jax.experimental.pallas import tpu as pltpu

F32 = jnp.float32
BF16 = jnp.bfloat16
MESH = pl.DeviceIdType.MESH

DEPTH = 2
ALPHA = (2 * DEPTH) ** 0.25
LN_EPS = 1e-5
RMS_EPS = 1e-5
SSD_P = 64
SSD_N = 128
SSD_Q = 256
SSD_G = 8
CONV_W = 4
DIL_PATTERNS = ((128, 1), (512, 4), (2048, 16))
DIL_H = 8
DIL_E = 128
DIL_BLK = 128
ADAM_LR, ADAM_B1, ADAM_B2, ADAM_EPS, ADAM_WD, ADAM_STEP = 0.001, 0.9, 0.999, 1e-08, 0.01, 10

VMEM_LIMIT = 56 * 1024 * 1024
N_CHIPS = 4
N_DEV = 8


def _tile(dim, target, mult=128):
    if dim <= target:
        return dim
    t = (target // mult) * mult
    while t >= mult:
        if dim % t == 0:
            return t
        t -= mult
    return dim


def _cp(sem):
    return pltpu.CompilerParams(dimension_semantics=sem, vmem_limit_bytes=VMEM_LIMIT)


def _sigmoid(x):
    return 1.0 / (1.0 + jnp.exp(-x))


def _silu(x):
    return x * _sigmoid(x)


def _dsilu(x):
    s = _sigmoid(x)
    return s * (1.0 + x * (1.0 - s))


def _softplus(x):
    return jnp.maximum(x, 0.0) + jnp.log(1.0 + jnp.exp(-jnp.abs(x)))


def _mm_call(a, b, out_shape, grid, a_spec, b_spec, o_spec, acc_shape, dims, name, after=None):
    nk = grid[2]
    extra = [] if after is None else [after]

    def prod(a_ref, b_ref):
        return lax.dot_general(a_ref[...].astype(BF16), b_ref[...].astype(BF16), (dims, ((), ())),
                               preferred_element_type=F32)

    def body_single(a_ref, b_ref, *rest):
        o_ref = rest[len(extra)]
        o_ref[...] = prod(a_ref, b_ref).astype(o_ref.dtype)

    def body_multi(a_ref, b_ref, *rest):
        o_ref, acc_ref = rest[len(extra):]
        k = pl.program_id(2)

        @pl.when(k == 0)
        def _():
            acc_ref[...] = prod(a_ref, b_ref)

        @pl.when(jnp.logical_and(k > 0, k < nk - 1))
        def _():
            acc_ref[...] += prod(a_ref, b_ref)

        @pl.when(k == nk - 1)
        def _():
            o_ref[...] = (acc_ref[...] + prod(a_ref, b_ref)).astype(o_ref.dtype)

    return pl.pallas_call(
        body_single if nk == 1 else body_multi, grid=grid, in_specs=[a_spec, b_spec] + [_ANY] * len(extra),
        out_specs=o_spec, out_shape=out_shape, scratch_shapes=[] if nk == 1 else [pltpu.VMEM(acc_shape, F32)],
        compiler_params=_cp(("parallel", "parallel", "arbitrary")), name=name)(a, b, *extra)


def mm_nn(a, b, out_dtype, name, stack=None, tm=1024, tn=1024, tk=2048, n_cols=None, after=None):
    M, K = a.shape
    if stack is None:
        N = b.shape[1] if n_cols is None else n_cols
        tn, tk = _tile(N, tn), _tile(K, tk)
        b_spec = pl.BlockSpec((tk, tn), lambda i, j, k: (k, j))
    elif stack == "col":
        S, _, Ns = b.shape
        N = S * Ns
        tn, tk = _tile(Ns, tn), _tile(K, tk)
        npb = Ns // tn
        b_spec = pl.BlockSpec((None, tk, tn), lambda i, j, k: (j // npb, k, j % npb))
    else:
        S, Ks, N = b.shape
        tn, tk = _tile(N, tn), _tile(Ks, tk)
        kpb = Ks // tk
        b_spec = pl.BlockSpec((None, tk, tn), lambda i, j, k: (k // kpb, k % kpb, j))
    tm = _tile(M, tm)
    return _mm_call(a, b, jax.ShapeDtypeStruct((M, N), out_dtype), (M // tm, N // tn, K // tk),
                    pl.BlockSpec((tm, tk), lambda i, j, k: (i, k)), b_spec,
                    pl.BlockSpec((tm, tn), lambda i, j, k: (i, j)), (tm, tn), ((1,), (0,)), name, after=after)


def mm_cols_dilated(a, b, gcols, d, name, tm=1024, tn=512):
    L, K = a.shape
    S, _, Ns = b.shape
    tm, tn = _tile(L, tm), _tile(Ns, tn)
    npb = Ns // tn
    nj = len(gcols)
    rows = tm // d

    def body(cols_ref, a_ref, b_ref, o_ref, *scr):
        prod = jnp.dot(a_ref[...], b_ref[...], preferred_element_type=F32)
        if d == 1:
            o_ref[0] = prod.astype(BF16)
        else:
            for c in range(tn // 128):
                scr[0][c] = prod[:, c * 128:(c + 1) * 128]
            for r in range(d):
                for c in range(tn // 128):
                    o_ref[r, :, c * 128:(c + 1) * 128] = scr[0].at[c][pl.ds(r, rows, stride=d), :].astype(BF16)

    return pl.pallas_call(
        body,
        grid_spec=pltpu.PrefetchScalarGridSpec(
            num_scalar_prefetch=1, grid=(L // tm, nj),
            in_specs=[pl.BlockSpec((tm, K), lambda i, j, c: (i, 0)),
                      pl.BlockSpec((None, K, tn), lambda i, j, c: (c[j] // npb, 0, c[j] % npb))],
            out_specs=pl.BlockSpec((d, rows, tn), lambda i, j, c: (0, i, j)),
            scratch_shapes=[] if d == 1 else [pltpu.VMEM((tn // 128, tm, 128), F32)]),
        out_shape=jax.ShapeDtypeStruct((d, L // d, nj * tn), BF16),
        compiler_params=_cp(("parallel", "arbitrary")), name=name)(jnp.asarray(gcols, jnp.int32), a, b)


def mm_nt(a, b, out_dtype, name, stack=None, tm=1024, tn=1024, tk=2048, after=None, kw_rows=None):
    M, C = a.shape
    if stack is None:
        Kw = b.shape[0] if kw_rows is None else kw_rows
        tn, tk = _tile(Kw, tn), _tile(C, tk)
        b_spec = pl.BlockSpec((tn, tk), lambda i, j, k: (j, k))
    elif stack == "col":
        S, Kw, Cs = b.shape
        tn, tk = _tile(Kw, tn), _tile(Cs, tk)
        cpb = Cs // tk
        b_spec = pl.BlockSpec((None, tn, tk), lambda i, j, k: (k // cpb, j, k % cpb))
    else:
        S, Ks, _ = b.shape
        Kw = S * Ks
        tn, tk = _tile(Ks, tn), _tile(C, tk)
        jpb = Ks // tn
        b_spec = pl.BlockSpec((None, tn, tk), lambda i, j, k: (j // jpb, j % jpb, k))
    tm = _tile(M, tm)
    return _mm_call(a, b, jax.ShapeDtypeStruct((M, Kw), out_dtype), (M // tm, Kw // tn, C // tk),
                    pl.BlockSpec((tm, tk), lambda i, j, k: (i, k)), b_spec,
                    pl.BlockSpec((tm, tn), lambda i, j, k: (i, j)), (tm, tn), ((1,), (1,)), name, after=after)


def mm_tn(a, b, out_dtype, name, stack=None, n_stack=N_CHIPS, tm=1024, tn=1024, tk=2048, m_rows=None):
    L, M = a.shape
    N = b.shape[1]
    tk = _tile(L, tk)
    if stack is None:
        tm, tn = _tile(M, tm), _tile(N, tn)
        o_spec = pl.BlockSpec((tm, tn), lambda i, j, k: (i, j))
        out_shape = (M if m_rows is None else m_rows, N)
    elif stack == "col":
        Ns = N // n_stack
        tm, tn = _tile(M, tm), _tile(Ns, tn)
        npb = Ns // tn
        o_spec = pl.BlockSpec((None, tm, tn), lambda i, j, k: (j // npb, i, j % npb))
        out_shape = (n_stack, M, Ns)
    else:
        Ms = M // n_stack
        tm, tn = _tile(Ms, tm), _tile(N, tn)
        mpb = Ms // tm
        o_spec = pl.BlockSpec((None, tm, tn), lambda i, j, k: (i // mpb, i % mpb, j))
        out_shape = (n_stack, Ms, N)
    return _mm_call(a, b, jax.ShapeDtypeStruct(out_shape, out_dtype), (M // tm, N // tn, L // tk),
                    pl.BlockSpec((tk, tm), lambda i, j, k: (k, i)), pl.BlockSpec((tk, tn), lambda i, j, k: (k, j)),
                    o_spec, (tm, tn), ((0,), (0,)), name)


def _row_specs(tr, widths):
    return [pl.BlockSpec((tr, w), lambda i: (i, 0)) for w in widths]


def _vec_spec(w):
    return pl.BlockSpec((1, w), lambda i: (0, 0))


def _acc_rows(ref, val, i):
    s = jnp.sum(val, axis=0, keepdims=True)

    @pl.when(i == 0)
    def _():
        ref[...] = s

    @pl.when(i > 0)
    def _():
        ref[...] += s


def modulate(x, scale, shift, name):
    L, D = x.shape
    tr = _tile(L, 512, 16)

    def body(x_ref, sc_ref, sh_ref, h_ref):
        h_ref[...] = (x_ref[...] * (1.0 + sc_ref[...]) + sh_ref[...]).astype(BF16)

    return pl.pallas_call(
        body, grid=(L // tr,), in_specs=_row_specs(tr, [D]) + [_vec_spec(D)] * 2, out_specs=_row_specs(tr, [D])[0],
        out_shape=jax.ShapeDtypeStruct((L, D), BF16), compiler_params=_cp(("parallel",)), name=name)(x, scale, shift)


def _ln_core(x, y, gate, g, b):
    u = ALPHA * x + (1.0 + gate) * y
    mu = jnp.mean(u, axis=-1, keepdims=True)
    d = u - mu
    var = jnp.mean(d * d, axis=-1, keepdims=True)
    rstd = lax.rsqrt(var + LN_EPS)
    xhat = d * rstd
    return xhat * g + b, xhat, rstd


def ln_mid(x, y, gate, g, b, scale, shift):
    L, D = x.shape
    tr = _tile(L, 256, 16)

    def body(x_ref, y_ref, gate_ref, g_ref, b_ref, sc_ref, sh_ref, x1_ref, x1b_ref, h_ref):
        x1, _, _ = _ln_core(x_ref[...], y_ref[...], gate_ref[...], g_ref[...], b_ref[...])
        x1_ref[...] = x1
        x1b_ref[...] = x1.astype(BF16)
        h_ref[...] = (x1 * (1.0 + sc_ref[...]) + sh_ref[...]).astype(BF16)

    return pl.pallas_call(
        body, grid=(L // tr,), in_specs=_row_specs(tr, [D, D]) + [_vec_spec(D)] * 5,
        out_specs=_row_specs(tr, [D, D, D]),
        out_shape=[jax.ShapeDtypeStruct((L, D), F32), jax.ShapeDtypeStruct((L, D), BF16),
                   jax.ShapeDtypeStruct((L, D), BF16)],
        compiler_params=_cp(("parallel",)), name="ln_mid")(x, y, gate, g, b, scale, shift)


def _ln_bwd_rows(dout_v, xhat, rstd, g):
    dxh = dout_v * g
    m1 = jnp.mean(dxh, axis=-1, keepdims=True)
    m2 = jnp.mean(dxh * xhat, axis=-1, keepdims=True)
    return rstd * (dxh - m1 - xhat * m2)


def ln_final_fwd_bwd(x, y, gate, g, b, target):
    L, D = x.shape
    tr = _tile(L, 256, 16)

    def body(x_ref, y_ref, gate_ref, g_ref, b_ref, t_ref, dres_ref, dy_ref, dg_ref, db_ref, dgate_ref, sq_ref):
        i = pl.program_id(0)
        yv = y_ref[...]
        out, xhat, rstd = _ln_core(x_ref[...], yv, gate_ref[...], g_ref[...], b_ref[...])
        err = out - t_ref[...]
        dout_v = err * (1.0 / D)
        du = _ln_bwd_rows(dout_v, xhat, rstd, g_ref[...])
        dres_ref[...] = ALPHA * du
        dy_ref[...] = ((1.0 + gate_ref[...]) * du).astype(BF16)
        _acc_rows(dg_ref, dout_v * xhat, i)
        _acc_rows(db_ref, dout_v, i)
        _acc_rows(dgate_ref, du * yv, i)
        _acc_rows(sq_ref, err * err, i)

    return pl.pallas_call(
        body, grid=(L // tr,), in_specs=_row_specs(tr, [D, D]) + [_vec_spec(D)] * 3 + _row_specs(tr, [D]),
        out_specs=_row_specs(tr, [D, D]) + [_vec_spec(D)] * 4,
        out_shape=[jax.ShapeDtypeStruct((L, D), F32), jax.ShapeDtypeStruct((L, D), BF16)]
        + [jax.ShapeDtypeStruct((1, D), F32)] * 4,
        compiler_params=_cp(("arbitrary",)), name="ln_final_fwd_bwd")(x, y, gate, g, b, target)


def mod_ln_bwd(dres_in, dh, dskip, xmid, scale, x, y, gate, g):
    L, D = x.shape
    tr = _tile(L, 256, 16)

    def body(dres_ref, dh_ref, dskip_ref, xm_ref, sc_ref, x_ref, y_ref, gate_ref, g_ref,
             dres_out, dy_ref, dg_ref, db_ref, dgate_ref, dsc_ref, dsh_ref):
        i = pl.program_id(0)
        dh_v = dh_ref[...].astype(F32)
        dout_v = dres_ref[...] + dskip_ref[...].astype(F32) + dh_v * (1.0 + sc_ref[...])
        _acc_rows(dsc_ref, dh_v * xm_ref[...], i)
        _acc_rows(dsh_ref, dh_v, i)
        yv = y_ref[...]
        _, xhat, rstd = _ln_core(x_ref[...], yv, gate_ref[...], g_ref[...], 0.0)
        du = _ln_bwd_rows(dout_v, xhat, rstd, g_ref[...])
        dres_out[...] = ALPHA * du
        dy_ref[...] = ((1.0 + gate_ref[...]) * du).astype(BF16)
        _acc_rows(dg_ref, dout_v * xhat, i)
        _acc_rows(db_ref, dout_v, i)
        _acc_rows(dgate_ref, du * yv, i)

    return pl.pallas_call(
        body, grid=(L // tr,),
        in_specs=_row_specs(tr, [D] * 4) + [_vec_spec(D)] + _row_specs(tr, [D, D]) + [_vec_spec(D)] * 2,
        out_specs=_row_specs(tr, [D, D]) + [_vec_spec(D)] * 5,
        out_shape=[jax.ShapeDtypeStruct((L, D), F32), jax.ShapeDtypeStruct((L, D), BF16)]
        + [jax.ShapeDtypeStruct((1, D), F32)] * 5,
        compiler_params=_cp(("arbitrary",)), name="mod_ln_bwd")(dres_in, dh, dskip, xmid, scale, x, y, gate, g)


def mod_bwd(dres, dh, dh2, xin, scale, name):
    L, D = xin.shape
    tr = _tile(L, 256, 16)

    def body(dres_ref, dh_ref, dh2_ref, x_ref, sc_ref, dx_ref, dsc_ref, dsh_ref):
        i = pl.program_id(0)
        dh_v = dh_ref[...].astype(F32) + dh2_ref[...].astype(F32)
        dx_ref[...] = dres_ref[...] + dh_v * (1.0 + sc_ref[...])
        _acc_rows(dsc_ref, dh_v * x_ref[...], i)
        _acc_rows(dsh_ref, dh_v, i)

    return pl.pallas_call(
        body, grid=(L // tr,), in_specs=_row_specs(tr, [D, D, D, D]) + [_vec_spec(D)],
        out_specs=_row_specs(tr, [D]) + [_vec_spec(D)] * 2,
        out_shape=[jax.ShapeDtypeStruct((L, D), F32)] + [jax.ShapeDtypeStruct((1, D), F32)] * 2,
        compiler_params=_cp(("arbitrary",)), name=name)(dres, dh, dh2, xin, scale)


CONV_HALO = 16


def _conv_rows(x_ref, i, tr, L):
    nblk = L // tr
    s = pl.multiple_of(i * tr, CONV_HALO)
    cur = x_ref[pl.ds(s, tr), :].astype(F32)
    sp = pl.multiple_of(jnp.maximum(i * tr - CONV_HALO, 0), CONV_HALO)
    sn = pl.multiple_of(jnp.minimum(i * tr + tr, L - CONV_HALO), CONV_HALO)
    prev = x_ref[pl.ds(sp, CONV_HALO), :].astype(F32) * (i > 0).astype(F32)
    nxt = x_ref[pl.ds(sn, CONV_HALO), :].astype(F32) * (i < nblk - 1).astype(F32)
    return jnp.concatenate([prev, cur, nxt], axis=0)


def _shift_rows(v, j):
    n = v.shape[0]
    return v if j % n == 0 else pltpu.roll(v, j % n, 0)


def _conv_taps(xe):
    return [_shift_rows(xe, CONV_W - 1 - k) for k in range(CONV_W)]


def _conv_eval(taps, w_ref, b_ref):
    c = b_ref[...] + w_ref[0:1, :] * taps[0]
    for k in range(1, CONV_W):
        c = c + w_ref[k:k + 1, :] * taps[k]
    return c


def conv_fwd(zx, col0, conv_w, conv_b):
    L = zx.shape[0]
    C = conv_w.shape[1]
    tc = _tile(C, 512)
    tr = _tile(L, 512, CONV_HALO)
    off = col0 // tc

    def body(x_ref, w_ref, b_ref, o_ref):
        i = pl.program_id(1)
        xe = _conv_rows(x_ref, i, tr, L)
        c = _conv_eval(_conv_taps(xe), w_ref, b_ref)[CONV_HALO:CONV_HALO + tr]
        o_ref[...] = _silu(c).astype(BF16)

    return pl.pallas_call(
        body, grid=(C // tc, L // tr),
        in_specs=[pl.BlockSpec((L, tc), lambda j, i: (0, off + j)), pl.BlockSpec((CONV_W, tc), lambda j, i: (0, j)),
                  pl.BlockSpec((1, tc), lambda j, i: (0, j))],
        out_specs=pl.BlockSpec((tr, tc), lambda j, i: (i, j)),
        out_shape=jax.ShapeDtypeStruct((L, C), BF16), compiler_params=_cp(("parallel", "arbitrary")),
        name="conv_fwd")(zx, conv_w, conv_b)


def conv_bwd(zx, col0, conv_w, conv_b, dxbc):
    L = zx.shape[0]
    C = conv_w.shape[1]
    tc = _tile(C, 512)
    tr = _tile(L, 512, CONV_HALO)
    off = col0 // tc
    H = CONV_HALO

    def body(x_ref, g_ref, w_ref, b_ref, dx_ref, dw_ref, db_ref):
        i = pl.program_id(1)
        xe = _conv_rows(x_ref, i, tr, L)
        ge = _conv_rows(g_ref, i, tr, L)
        taps = _conv_taps(xe)
        dc = ge * _dsilu(_conv_eval(taps, w_ref, b_ref))
        dx = w_ref[CONV_W - 1:CONV_W, :] * dc
        for k in range(CONV_W - 1):
            dx = dx + w_ref[k:k + 1, :] * _shift_rows(dc, -(CONV_W - 1 - k))
        dx_ref[...] = dx[H:H + tr].astype(BF16)
        dcc = dc[H:H + tr]
        rows = [jnp.sum(dcc * taps[k][H:H + tr], axis=0, keepdims=True) for k in range(CONV_W)]
        dwv = jnp.concatenate(rows + [jnp.zeros((8 - CONV_W, tc), F32)], axis=0)
        dbv = jnp.sum(dcc, axis=0, keepdims=True)

        @pl.when(i == 0)
        def _():
            dw_ref[...] = dwv
            db_ref[...] = dbv

        @pl.when(i > 0)
        def _():
            dw_ref[...] += dwv
            db_ref[...] += dbv

    dx, dw, db = pl.pallas_call(
        body, grid=(C // tc, L // tr),
        in_specs=[pl.BlockSpec((L, tc), lambda j, i: (0, off + j)), pl.BlockSpec((L, tc), lambda j, i: (0, j)),
                  pl.BlockSpec((CONV_W, tc), lambda j, i: (0, j)), pl.BlockSpec((1, tc), lambda j, i: (0, j))],
        out_specs=[pl.BlockSpec((tr, tc), lambda j, i: (i, j)), pl.BlockSpec((8, tc), lambda j, i: (0, j)),
                   pl.BlockSpec((1, tc), lambda j, i: (0, j))],
        out_shape=[jax.ShapeDtypeStruct((L, C), BF16), jax.ShapeDtypeStruct((8, C), F32),
                   jax.ShapeDtypeStruct((1, C), F32)],
        compiler_params=_cp(("parallel", "arbitrary")), name="conv_bwd")(zx, dxbc, conv_w, conv_b)
    return dx, dw[:CONV_W], db


_NN = (((1,), (0,)), ((), ()))


def _pieces(x, n):
    out, r = [], x
    for _ in range(n):
        p = r.astype(BF16)
        out.append(p)
        r = r - p.astype(F32)
    return out


def _dot01(a, b01, n, dims=_NN):
    b = b01.astype(BF16)
    return functools.reduce(lambda u, v: u + v,
                            [lax.dot_general(p, b, dims, preferred_element_type=F32) for p in _pieces(a, n)])


def _dot01_left(a01, b, n, dims=_NN):
    a = a01.astype(BF16)
    return functools.reduce(lambda u, v: u + v,
                            [lax.dot_general(a, p, dims, preferred_element_type=F32) for p in _pieces(b, n)])


def _ssd_common(dtp_ref, dtpT_ref, bias_ref, biasT_ref, alog_ref, alogT_ref, b_ref, c_ref):
    Q = SSD_Q
    dt = _softplus(dtp_ref[...] + bias_ref[...])
    A = -jnp.exp(alog_ref[...])
    row = lax.broadcasted_iota(jnp.int32, (Q, Q), 0)
    col = lax.broadcasted_iota(jnp.int32, (Q, Q), 1)
    causal = row >= col
    tril = causal.astype(F32)
    Kh = dt.shape[1]
    acum = _dot01_left(tril, dt * A, 3)
    eye = (lax.broadcasted_iota(jnp.int32, (Kh, Kh), 0) == lax.broadcasted_iota(jnp.int32, (Kh, Kh), 1)).astype(F32)
    acumT = _dot01_left(eye, acum, 3, dims=(((1,), (1,)), ((), ())))
    Bm = b_ref[...]
    Cm = c_ref[...]
    cb = lax.dot_general(Cm, Bm, (((1,), (1,)), ((), ())), preferred_element_type=F32)
    return dt, A, causal, row, col, acum, acumT, Bm, Cm, cb


def _ssd_in_specs(Q, GP, N, Kh, DI, cmap):
    nb0 = DI // N
    vec = pl.BlockSpec((None, 1, Kh), lambda g, c: (g, 0, 0))
    vecT = pl.BlockSpec((None, Kh, 1), lambda g, c: (g, 0, 0))
    return [pl.BlockSpec((Q, GP), lambda g, c: (cmap(c), g)),
            pl.BlockSpec((Q, N), lambda g, c: (cmap(c), nb0 + g)),
            pl.BlockSpec((Q, N), lambda g, c: (cmap(c), nb0 + SSD_G + g)),
            pl.BlockSpec((None, Q, Kh), lambda g, c: (g, cmap(c), 0)),
            pl.BlockSpec((None, Kh, Q), lambda g, c: (g, 0, cmap(c))),
            vec, vecT, vec, vecT, vec, vecT]


def _hi(a, b01):
    return _dot01(a, b01, 2)


def _headsum(a, b01):
    return _dot01(a, b01, 1)


def _ssd_heads(dskT_ref, acum, acumT, dt, Kh):
    Q, P, N = SSD_Q, SSD_P, SSD_N
    GP = Kh * P
    sh_p = P.bit_length() - 1
    seg = lambda shape, dim: lax.shift_right_logical(lax.broadcasted_iota(jnp.int32, shape, dim), sh_p)
    E = (seg((Kh, GP), 1) == lax.broadcasted_iota(jnp.int32, (Kh, GP), 0)).astype(F32)
    ET = (seg((GP, Kh), 0) == lax.broadcasted_iota(jnp.int32, (GP, Kh), 1)).astype(F32)
    a_last = acum[Q - 1:Q, :]
    tail = jnp.exp(a_last - acum)
    eLT = jnp.exp(acumT[:, Q - 1:Q])
    rowseg = seg((GP, N), 0)
    eL_b = jnp.zeros((GP, N), F32)
    for k in range(Kh):
        eL_b = jnp.where(rowseg == k, eLT[k:k + 1, :], eL_b)
    return dict(
        E=E, ET=ET, a_last=a_last, tail=tail, eL_b=eL_b,
        dt_all=_hi(dt, E), ea_all=_hi(jnp.exp(acum), E), tail_all=_hi(tail, E),
        dsk_all=jnp.sum(E * dskT_ref[...], axis=0, keepdims=True))


def _head_chunks(GP):
    CW = min(GP, 128)
    return CW, CW // SSD_P, GP // CW


def _head_mask(Q, CW, kk):
    lane = lax.broadcasted_iota(jnp.int32, (Q, CW), 1)
    return jnp.logical_and(lane >= kk * SSD_P, lane < (kk + 1) * SSD_P)


def ssd_fwd(xbc, dtp_g, dtp_gT, bias_g, bias_gT, alog_g, alog_gT, dsk_g, dsk_gT, DI):
    L = xbc.shape[0]
    Q, P, N, G = SSD_Q, SSD_P, SSD_N, SSD_G
    GP = DI // G
    Kh = GP // P
    nc = L // Q

    CW, hpc, nch = _head_chunks(GP)
    nt = (((1,), (1,)), ((), ()))
    tn = (((0,), (0,)), ((), ()))

    def body(xs_ref, b_ref, c_ref, dtp_ref, dtpT_ref, bias_ref, biasT_ref, alog_ref, alogT_ref, dsk_ref, dskT_ref,
             y_ref, st_ref, state):
        @pl.when(pl.program_id(1) == 0)
        def _():
            state[...] = jnp.zeros(state.shape, F32)

        st_ref[...] = state[...]
        dt, A, causal, row, col, acum, acumT, Bm, Cm, cb = _ssd_common(
            dtp_ref, dtpT_ref, bias_ref, biasT_ref, alog_ref, alogT_ref, b_ref, c_ref)
        hd = _ssd_heads(dskT_ref, acum, acumT, dt, Kh)
        xs = xs_ref[...].astype(F32)
        xdt_all = xs * hd["dt_all"]
        S_all = state[...]
        y_all = (lax.dot_general(Cm, S_all.astype(BF16), nt, preferred_element_type=F32) * hd["ea_all"]
                 + xs * hd["dsk_all"])
        state[...] = S_all * hd["eL_b"] + lax.dot_general(
            (xdt_all * hd["tail_all"]).astype(BF16), Bm, tn, preferred_element_type=F32)
        for ch in range(nch):
            cs = slice(ch * CW, (ch + 1) * CW)
            xc = xdt_all[:, cs]
            acc = y_all[:, cs]
            for kk in range(hpc):
                k = ch * hpc + kk
                decay = jnp.exp(jnp.where(causal, acum[:, k:k + 1] - acumT[k:k + 1, :], -jnp.inf))
                xk = xc if hpc == 1 else jnp.where(_head_mask(Q, CW, kk), xc, 0.0)
                acc = acc + jnp.dot((cb * decay).astype(BF16), xk.astype(BF16), preferred_element_type=F32)
            y_ref[:, cs] = acc.astype(BF16)

    return pl.pallas_call(
        body, grid=(G, nc), in_specs=_ssd_in_specs(Q, GP, N, Kh, DI, lambda c: c),
        out_specs=[pl.BlockSpec((Q, GP), lambda g, c: (c, g)),
                   pl.BlockSpec((None, None, GP, N), lambda g, c: (c, g, 0, 0))],
        out_shape=[jax.ShapeDtypeStruct((L, DI), BF16), jax.ShapeDtypeStruct((nc, G, GP, N), F32)],
        scratch_shapes=[pltpu.VMEM((GP, N), F32)], compiler_params=_cp(("parallel", "arbitrary")),
        name="ssd_fwd")(xbc, xbc, xbc, dtp_g, dtp_gT, bias_g, bias_gT, alog_g, alog_gT, dsk_g, dsk_gT)


def ssd_bwd(xbc, dtp_g, dtp_gT, bias_g, bias_gT, alog_g, alog_gT, dsk_g, dsk_gT, states, dy, DI):
    L = xbc.shape[0]
    Q, P, N, G = SSD_Q, SSD_P, SSD_N, SSD_G
    GP = DI // G
    Kh = GP // P
    nc = L // Q
    rev = lambda c: nc - 1 - c

    CW, hpc, nch = _head_chunks(GP)

    def body(xs_ref, b_ref, c_ref, dtp_ref, dtpT_ref, bias_ref, biasT_ref, alog_ref, alogT_ref, dsk_ref, dskT_ref,
             st_ref, dy_ref, dxs_ref, dB_ref, dC_ref, ddtp_ref, dbias_ref, dalog_ref, dD_ref, dstate):
        ci = pl.program_id(1)

        @pl.when(ci == 0)
        def _():
            dstate[...] = jnp.zeros(dstate.shape, F32)

        dt, A, causal, row, col, acum, acumT, Bm, Cm, cb = _ssd_common(
            dtp_ref, dtpT_ref, bias_ref, biasT_ref, alog_ref, alogT_ref, b_ref, c_ref)
        tn = (((0,), (0,)), ((), ()))
        nt = (((1,), (1,)), ((), ()))
        hd = _ssd_heads(dskT_ref, acum, acumT, dt, Kh)
        ET, tail = hd["ET"], hd["tail"]
        cbT = lax.dot_general(Bm, Cm, nt, preferred_element_type=F32)
        causalT = row <= col
        xs = xs_ref[...].astype(F32)
        xdt_all = xs * hd["dt_all"]
        dyb = dy_ref[...]
        dy_all = dyb.astype(F32)
        S_all = st_ref[...]
        S_b = S_all.astype(BF16)
        dS_all = dstate[...]
        dS_b = dS_all.astype(BF16)
        CS_all = lax.dot_general(Cm, S_b, nt, preferred_element_type=F32)
        dyE_b = (dy_all * hd["ea_all"]).astype(BF16)
        dC_acc = jnp.dot(dyE_b, S_b, preferred_element_type=F32)
        dS_y = lax.dot_general(dyE_b, Cm, tn, preferred_element_type=F32)
        BdS_all = lax.dot_general(Bm, dS_b, nt, preferred_element_type=F32)
        dB_acc = jnp.dot((xdt_all * hd["tail_all"]).astype(BF16), dS_b, preferred_element_type=F32)
        dtail = _headsum(xdt_all * BdS_all, ET)
        da_cols = _headsum(dy_all * CS_all * hd["ea_all"], ET) - dtail * tail
        dss = _dot01_left(jnp.ones((8, N), F32), _dot01_left(hd["E"], dS_all * S_all, 2), 2, dims=nt)
        da_last = dss[0:1] * jnp.exp(hd["a_last"]) + jnp.sum(dtail * tail, axis=0, keepdims=True)
        rowi = lax.broadcasted_iota(jnp.int32, (Q, Kh), 0)
        da_cols = da_cols + jnp.where(rowi == Q - 1, da_last, 0.0)
        dstate[...] = hd["eL_b"] * dS_all + dS_y
        sum_mg = jnp.zeros((Q, Q), F32)
        ddt_x = jnp.zeros((Q, Kh), F32)
        da_rows = jnp.zeros((Kh, Q), F32)
        lane_k = lax.broadcasted_iota(jnp.int32, (Q, Kh), 1)
        sub_k = lax.broadcasted_iota(jnp.int32, (Kh, Q), 0)
        for ch in range(nch):
            cs = slice(ch * CW, (ch + 1) * CW)
            dyc = dyb[:, cs]
            xc_b = xdt_all[:, cs].astype(BF16)
            acc = hd["tail_all"][:, cs] * BdS_all[:, cs]
            for kk in range(hpc):
                k = ch * hpc + kk
                a_b = jnp.broadcast_to(acum[:, k:k + 1], (Q, Q))
                a_r = acumT[k:k + 1, :]
                decay = jnp.exp(jnp.where(causal, a_b - a_r, -jnp.inf))
                decayT = jnp.exp(jnp.where(causalT, a_r - a_b, -jnp.inf))
                dyk = dyc if hpc == 1 else jnp.where(_head_mask(Q, CW, kk), dyc, jnp.zeros_like(dyc))
                mg = decay * lax.dot_general(dyk, xc_b, nt, preferred_element_type=F32)
                sum_mg = sum_mg + mg
                w = mg * cb
                da_cols = da_cols + jnp.where(lane_k == k, jnp.sum(w, axis=1, keepdims=True), 0.0)
                da_rows = da_rows + jnp.where(sub_k == k, jnp.sum(w, axis=0, keepdims=True), 0.0)
                acc = acc + jnp.dot((decayT * cbT).astype(BF16), dyk, preferred_element_type=F32)
            dxs_ref[:, cs] = (acc * hd["dt_all"][:, cs] + dy_all[:, cs] * hd["dsk_all"][:, cs]).astype(BF16)
            ddt_x = ddt_x + _headsum(acc * xs[:, cs], ET[cs, :])
        eye_q = (row == col).astype(F32)
        da_cols = da_cols - _dot01_left(eye_q, da_rows, 3, dims=nt)
        dD_row = jnp.sum(_headsum(dy_all * xs, ET), axis=0, keepdims=True)
        sum_mg_b = sum_mg.astype(BF16)
        dB_ref[...] = (dB_acc + lax.dot_general(sum_mg_b, Cm, tn, preferred_element_type=F32)).astype(BF16)
        dC_ref[...] = (dC_acc + jnp.dot(sum_mg_b, Bm, preferred_element_type=F32)).astype(BF16)
        triu = (row <= col).astype(F32)
        ddtA = _dot01_left(triu, da_cols, 3)
        ddt = ddt_x + ddtA * A
        dpre = ddt * _sigmoid(dtp_ref[...] + bias_ref[...])
        ddtp_ref[...] = dpre
        dbias_v = jnp.sum(dpre, axis=0, keepdims=True)
        dalog_v = jnp.sum(ddtA * dt, axis=0, keepdims=True) * A

        @pl.when(ci == 0)
        def _():
            dbias_ref[...] = dbias_v
            dalog_ref[...] = dalog_v
            dD_ref[...] = dD_row

        @pl.when(ci > 0)
        def _():
            dbias_ref[...] += dbias_v
            dalog_ref[...] += dalog_v
            dD_ref[...] += dD_row

    vec_o = pl.BlockSpec((None, 1, Kh), lambda g, c: (g, 0, 0))
    return pl.pallas_call(
        body, grid=(G, nc),
        in_specs=_ssd_in_specs(Q, GP, N, Kh, DI, rev)
        + [pl.BlockSpec((None, None, GP, N), lambda g, c: (rev(c), g, 0, 0)),
           pl.BlockSpec((Q, GP), lambda g, c: (rev(c), g))],
        out_specs=[pl.BlockSpec((Q, GP), lambda g, c: (rev(c), g)), pl.BlockSpec((Q, N), lambda g, c: (rev(c), g)),
                   pl.BlockSpec((Q, N), lambda g, c: (rev(c), g)),
                   pl.BlockSpec((None, Q, Kh), lambda g, c: (g, rev(c), 0)), vec_o, vec_o, vec_o],
        out_shape=[jax.ShapeDtypeStruct((L, DI), BF16), jax.ShapeDtypeStruct((L, G * N), BF16),
                   jax.ShapeDtypeStruct((L, G * N), BF16), jax.ShapeDtypeStruct((G, L, Kh), F32)]
        + [jax.ShapeDtypeStruct((G, 1, Kh), F32)] * 3,
        scratch_shapes=[pltpu.VMEM((GP, N), F32)], compiler_params=_cp(("parallel", "arbitrary")),
        name="ssd_bwd")(xbc, xbc, xbc, dtp_g, dtp_gT, bias_g, bias_gT, alog_g, alog_gT, dsk_g, dsk_gT, states, dy)


def _rms_groups(y2, ng_ref, DI):
    S = DI // SSD_G
    for g in range(SSD_G):
        gs = slice(g * S, (g + 1) * S)
        seg = y2[:, gs]
        r = lax.rsqrt(jnp.mean(seg * seg, axis=-1, keepdims=True) + RMS_EPS)
        yield gs, seg * r, r, ng_ref[:, gs]


def rms_gate_fwd(y, zx, norm_g):
    L, DI = y.shape
    tr = _tile(L, 256, 16)

    def body(y_ref, z_ref, ng_ref, o_ref):
        y2 = y_ref[...].astype(F32) * _silu(z_ref[...].astype(F32))
        for gs, yh, _, ng in _rms_groups(y2, ng_ref, DI):
            o_ref[:, gs] = (yh * ng).astype(BF16)

    return pl.pallas_call(
        body, grid=(L // tr,), in_specs=_row_specs(tr, [DI, DI]) + [_vec_spec(DI)], out_specs=_row_specs(tr, [DI])[0],
        out_shape=jax.ShapeDtypeStruct((L, DI), BF16), compiler_params=_cp(("parallel",)),
        name="rms_gate_fwd")(y, zx, norm_g)


def rms_gate_bwd(dyn, y, zx, norm_g):
    L, DI = y.shape
    tr = _tile(L, 256, 16)

    def body(dyn_ref, y_ref, z_ref, ng_ref, dy_ref, dz_ref, dng_ref):
        i = pl.program_id(0)
        yv = y_ref[...].astype(F32)
        zv = z_ref[...].astype(F32)
        sz = _silu(zv)
        dsz = _dsilu(zv)
        dynv = dyn_ref[...].astype(F32)
        for gs, yh, r, ng in _rms_groups(yv * sz, ng_ref, DI):
            dyh = dynv[:, gs] * ng
            dy2 = r * (dyh - yh * jnp.mean(dyh * yh, axis=-1, keepdims=True))
            dy_ref[:, gs] = (dy2 * sz[:, gs]).astype(BF16)
            dz_ref[:, gs] = (dy2 * yv[:, gs] * dsz[:, gs]).astype(BF16)
            s = jnp.sum(dynv[:, gs] * yh, axis=0, keepdims=True)

            @pl.when(i == 0)
            def _():
                dng_ref[:, gs] = s

            @pl.when(i > 0)
            def _():
                dng_ref[:, gs] += s

    return pl.pallas_call(
        body, grid=(L // tr,), in_specs=_row_specs(tr, [DI, DI, DI]) + [_vec_spec(DI)],
        out_specs=_row_specs(tr, [DI, DI]) + [_vec_spec(DI)],
        out_shape=[jax.ShapeDtypeStruct((L, DI), BF16)] * 2 + [jax.ShapeDtypeStruct((1, DI), F32)],
        compiler_params=_cp(("arbitrary",)), name="rms_gate_bwd")(dyn, y, zx, norm_g)


def _alibi_slope(gi, h):
    n = len(DIL_PATTERNS) * DIL_H
    return float(2.0 ** (-8.0 * (gi * DIL_H + h + 1) / n))


def _attn_masks():
    qi = lax.broadcasted_iota(jnp.int32, (DIL_BLK, DIL_BLK), 0)
    kj = lax.broadcasted_iota(jnp.int32, (DIL_BLK, DIL_BLK), 1)
    dcur = (qi - kj).astype(F32)
    return dcur, qi >= kj, dcur + float(DIL_BLK), kj >= qi


def attn_fwd(q3, kv3, gi):
    window, d = DIL_PATTERNS[gi]
    assert window // d == DIL_BLK
    HW = DIL_H * DIL_E
    M = q3.shape[1]
    nb = M // DIL_BLK
    scale = DIL_E ** -0.5
    nt = (((1,), (1,)), ((), ()))

    def body(q_ref, kp_ref, kc_ref, vp_ref, vc_ref, o_ref, lse_ref):
        n = pl.program_id(1)
        dcur, vcur, dprev, vprev0 = _attn_masks()
        dist = jnp.concatenate([dprev, dcur], axis=1)
        valid = jnp.concatenate([jnp.logical_and(vprev0, n > 0), vcur], axis=1)
        lane = lax.broadcasted_iota(jnp.int32, (DIL_BLK, 128), 1)
        lse_acc = jnp.zeros((DIL_BLK, 128), F32)
        for h in range(DIL_H):
            hs = slice(h * DIL_E, (h + 1) * DIL_E)
            sl = _alibi_slope(gi, h) * d
            kcat = jnp.concatenate([kp_ref[:, hs], kc_ref[:, hs]], axis=0)
            vcat = jnp.concatenate([vp_ref[:, hs], vc_ref[:, hs]], axis=0)
            s = lax.dot_general(q_ref[:, hs], kcat, nt, preferred_element_type=F32) * scale - sl * dist
            s = jnp.where(valid, s, -jnp.inf)
            m = jnp.max(s, axis=-1, keepdims=True)
            p = jnp.exp(s - m)
            den = jnp.sum(p, axis=-1, keepdims=True)
            o = jnp.dot(p.astype(BF16), vcat, preferred_element_type=F32) / den
            o_ref[:, hs] = o.astype(BF16)
            lse_acc = jnp.where(lane == h, m + jnp.log(den), lse_acc)
        lse_ref[...] = lse_acc

    blk = (None, DIL_BLK, HW)
    prev = lambda n: jnp.maximum(n - 1, 0)
    return pl.pallas_call(
        body, grid=(d, nb),
        in_specs=[pl.BlockSpec(blk, lambda r, n: (r, n, 0)),
                  pl.BlockSpec(blk, lambda r, n: (r, prev(n), 0)), pl.BlockSpec(blk, lambda r, n: (r, n, 0)),
                  pl.BlockSpec(blk, lambda r, n: (r, prev(n), 1)), pl.BlockSpec(blk, lambda r, n: (r, n, 1))],
        out_specs=[pl.BlockSpec(blk, lambda r, n: (r, n, 0)), pl.BlockSpec((None, DIL_BLK, 128), lambda r, n: (r, n, 0))],
        out_shape=[jax.ShapeDtypeStruct((d, M, HW), BF16), jax.ShapeDtypeStruct((d, M, 128), F32)],
        compiler_params=_cp(("parallel", "parallel")), name=f"attn_fwd_{gi}")(q3, kv3, kv3, kv3, kv3)


def attn_bwd(q3, kv3, do3, lse3, dpr3, gi):
    window, d = DIL_PATTERNS[gi]
    HW = DIL_H * DIL_E
    M = q3.shape[1]
    L = M * d
    nb = M // DIL_BLK
    scale = DIL_E ** -0.5
    nt = (((1,), (1,)), ((), ()))
    tn = (((0,), (0,)), ((), ()))

    def body(q0_ref, q1_ref, k_ref, v_ref, do0_ref, do1_ref, l0_ref, l1_ref, r0_ref, r1_ref,
             dq_ref, dk_ref, dv_ref, carry):
        n = pl.program_id(1)

        @pl.when(n == 0)
        def _():
            carry[...] = jnp.zeros(carry.shape, F32)

        dcur, vcur, dprev, vprev0 = _attn_masks()
        dist = jnp.concatenate([dcur, dprev], axis=0)
        valid = jnp.concatenate([vcur, jnp.logical_and(vprev0, n < nb - 1)], axis=0)
        B = DIL_BLK
        for h in range(DIL_H):
            hs = slice(h * DIL_E, (h + 1) * DIL_E)
            sl = _alibi_slope(gi, h) * d
            kh = k_ref[:, hs]
            vh = v_ref[:, hs]
            qcat = jnp.concatenate([q0_ref[:, hs], q1_ref[:, hs]], axis=0)
            docat = jnp.concatenate([do0_ref[:, hs], do1_ref[:, hs]], axis=0)
            lcat = jnp.concatenate([l0_ref[:, h:h + 1], l1_ref[:, h:h + 1]], axis=0)
            rcat = jnp.concatenate([r0_ref[:, h:h + 1], r1_ref[:, h:h + 1]], axis=0)
            s = lax.dot_general(qcat, kh, nt, preferred_element_type=F32) * scale - sl * dist
            p = jnp.exp(jnp.where(valid, s - lcat, -jnp.inf))
            ds = p * (lax.dot_general(docat, vh, nt, preferred_element_type=F32) - rcat)
            ds_b = (ds * scale).astype(BF16)
            dv_ref[:, hs] = lax.dot_general(p.astype(BF16), docat, tn, preferred_element_type=F32).astype(BF16)
            dk_ref[:, hs] = lax.dot_general(ds_b, qcat, tn, preferred_element_type=F32).astype(BF16)
            dqc = jnp.dot(ds_b, kh, preferred_element_type=F32)
            dq_ref[:, hs] = (carry[:, hs] + dqc[:B]).astype(BF16)
            carry[:, hs] = dqc[B:]

    blk = (None, DIL_BLK, HW)
    sblk = (None, DIL_BLK, 128)
    oblk = (DIL_BLK, HW)
    nxt = lambda n: jnp.minimum(n + 1, nb - 1)
    here = lambda c: (lambda r, n: (r, n, c))
    ahead = lambda c: (lambda r, n: (r, nxt(n), c))
    outs = pl.pallas_call(
        body, grid=(d, nb),
        in_specs=[pl.BlockSpec(blk, here(0)), pl.BlockSpec(blk, ahead(0)),
                  pl.BlockSpec(blk, here(0)), pl.BlockSpec(blk, here(1)),
                  pl.BlockSpec(blk, here(0)), pl.BlockSpec(blk, ahead(0)),
                  pl.BlockSpec(sblk, here(0)), pl.BlockSpec(sblk, ahead(0)),
                  pl.BlockSpec(sblk, here(0)), pl.BlockSpec(sblk, ahead(0))],
        out_specs=[pl.BlockSpec(oblk, lambda r, n: (n, r))] * 3,
        out_shape=[jax.ShapeDtypeStruct((M, d * HW), BF16)] * 3,
        scratch_shapes=[pltpu.VMEM(oblk, F32)], compiler_params=_cp(("parallel", "arbitrary")),
        name=f"attn_bwd_{gi}")(q3, q3, kv3, kv3, do3, do3, lse3, lse3, dpr3, dpr3)
    return [t.reshape(L, HW) for t in outs]


def _merge_weights(l_tiles, h):
    ls = [t[:, h:h + 1] for t in l_tiles]
    mx = functools.reduce(jnp.maximum, ls)
    es = [jnp.exp(l - mx) for l in ls]
    den = functools.reduce(lambda a, b: a + b, es)
    return [e / den for e in es]


def _dil_specs(tr, arrs):
    return [pl.BlockSpec((a.shape[0], tr // a.shape[0], a.shape[2]), lambda i: (0, i, 0)) for a in arrs]


def _dil_scratch(tr, arrs):
    return [pltpu.VMEM((a.shape[2] // 128, tr, 128), F32) for a in arrs if a.shape[0] > 1]


def _undilate(refs3, scrs, tr):
    out, k = [], 0
    for ref in refs3:
        d, _, W = ref.shape
        if d == 1:
            out.append(lambda c, ref=ref: ref[0, :, c * 128:(c + 1) * 128])
            continue
        scr = scrs[k]
        k += 1
        for r in range(d):
            for c in range(W // 128):
                scr.at[c][pl.ds(r, tr // d, stride=d), :] = ref[r, :, c * 128:(c + 1) * 128].astype(F32)
        out.append(lambda c, scr=scr: scr[c])
    return out


def merge_fwd(os3, lses3, z):
    HW = os3[0].shape[2]
    L = os3[0].shape[0] * os3[0].shape[1]
    tr = _tile(L, 256, 16)
    ng = len(os3)
    n_scr = len(_dil_scratch(tr, os3))

    def body(*refs):
        z_ref, out_ref = refs[2 * ng], refs[2 * ng + 1]
        scrs = refs[2 * ng + 2:]
        o_get = _undilate(refs[:ng], scrs[:n_scr], tr)
        l_tiles = [g(0) for g in _undilate(refs[ng:2 * ng], scrs[n_scr:], tr)]
        for h in range(DIL_H):
            hs = slice(h * DIL_E, (h + 1) * DIL_E)
            ws = _merge_weights(l_tiles, h)
            om = functools.reduce(lambda a, b: a + b, [w * o(h).astype(F32) for w, o in zip(ws, o_get)])
            out_ref[:, hs] = (om * _silu(z_ref[:, hs].astype(F32))).astype(BF16)

    return pl.pallas_call(
        body, grid=(L // tr,),
        in_specs=_dil_specs(tr, os3) + _dil_specs(tr, lses3) + _row_specs(tr, [HW]),
        out_specs=_row_specs(tr, [HW])[0], out_shape=jax.ShapeDtypeStruct((L, HW), BF16),
        scratch_shapes=_dil_scratch(tr, os3) + _dil_scratch(tr, lses3),
        compiler_params=_cp(("parallel",)), name="merge_fwd")(*os3, *lses3, z)


def merge_bwd(dgated, os3, lses3, z):
    HW = os3[0].shape[2]
    L = os3[0].shape[0] * os3[0].shape[1]
    tr = _tile(L, 256, 16)
    ng = len(os3)
    n_scr = len(_dil_scratch(tr, os3))

    def body(*refs):
        dg_ref = refs[0]
        z_ref = refs[1 + 2 * ng]
        outs = refs[2 + 2 * ng:2 + 2 * ng + 2 * ng + 1]
        scrs = refs[2 + 2 * ng + 2 * ng + 1:]
        do_out, dpr_out, dz_ref = outs[:ng], outs[ng:2 * ng], outs[2 * ng]
        o_get = _undilate(refs[1:1 + ng], scrs[:n_scr], tr)
        l_tiles = [g(0) for g in _undilate(refs[1 + ng:1 + 2 * ng], scrs[n_scr:2 * n_scr], tr)]
        stage = scrs[2 * n_scr:]
        do_stage, dpr_stage, k = [], [], 0
        for g in range(ng):
            if do_out[g].shape[0] == 1:
                do_stage.append(None)
                dpr_stage.append(None)
            else:
                do_stage.append(stage[2 * k])
                dpr_stage.append(stage[2 * k + 1])
                k += 1
        lane = lax.broadcasted_iota(jnp.int32, (tr, 128), 1)
        accs = [jnp.zeros((tr, 128), F32) for _ in range(ng)]
        for h in range(DIL_H):
            hs = slice(h * DIL_E, (h + 1) * DIL_E)
            ws = _merge_weights(l_tiles, h)
            ov = [o(h).astype(F32) for o in o_get]
            om = functools.reduce(lambda a, b: a + b, [w * o for w, o in zip(ws, ov)])
            zv = z_ref[:, hs].astype(F32)
            dgv = dg_ref[:, hs].astype(F32)
            dom = dgv * _silu(zv)
            dz_ref[:, hs] = (dgv * om * _dsilu(zv)).astype(BF16)
            dws = [jnp.sum(dom * o, axis=-1, keepdims=True) for o in ov]
            dwbar = functools.reduce(lambda a, b: a + b, [w * dw for w, dw in zip(ws, dws)])
            for g in range(ng):
                if do_stage[g] is None:
                    do_out[g][0, :, hs] = (ws[g] * dom).astype(BF16)
                else:
                    do_stage[g][h] = ws[g] * dom
                accs[g] = jnp.where(lane == h, ws[g] * dwbar, accs[g])
        for g in range(ng):
            d = do_out[g].shape[0]
            if d == 1:
                dpr_out[g][0] = accs[g]
                continue
            dpr_stage[g][0] = accs[g]
            for r in range(d):
                dpr_out[g][r] = dpr_stage[g].at[0][pl.ds(r, tr // d, stride=d), :]
                for c in range(HW // 128):
                    do_out[g][r, :, c * 128:(c + 1) * 128] = do_stage[g].at[c][pl.ds(r, tr // d, stride=d), :].astype(BF16)

    stage_shapes = []
    for o3 in os3:
        if o3.shape[0] > 1:
            stage_shapes += [pltpu.VMEM((HW // 128, tr, 128), F32), pltpu.VMEM((1, tr, 128), F32)]
    outs = pl.pallas_call(
        body, grid=(L // tr,),
        in_specs=_row_specs(tr, [HW]) + _dil_specs(tr, os3) + _dil_specs(tr, lses3) + _row_specs(tr, [HW]),
        out_specs=_dil_specs(tr, os3) + _dil_specs(tr, lses3) + _row_specs(tr, [HW]),
        out_shape=[jax.ShapeDtypeStruct(o.shape, BF16) for o in os3] + [jax.ShapeDtypeStruct(l.shape, F32) for l in lses3]
        + [jax.ShapeDtypeStruct((L, HW), BF16)],
        scratch_shapes=_dil_scratch(tr, os3) + _dil_scratch(tr, lses3) + stage_shapes,
        compiler_params=_cp(("parallel",)), name="merge_bwd")(dgated, *os3, *lses3, z)
    return outs[:ng], outs[ng:2 * ng], outs[2 * ng]


def ada_fwd(c8, ada_w):
    nl, D, Ws = ada_w.shape
    tn = _tile(Ws, 512)

    def body(c_ref, w_ref, o_ref):
        o_ref[...] = jnp.dot(_silu(c_ref[...]), w_ref[...], precision=lax.Precision.HIGHEST,
                             preferred_element_type=F32)

    return pl.pallas_call(
        body, grid=(nl, Ws // tn),
        in_specs=[pl.BlockSpec((N_DEV, D), lambda l, j: (0, 0)), pl.BlockSpec((None, D, tn), lambda l, j: (l, 0, j))],
        out_specs=pl.BlockSpec((None, N_DEV, tn), lambda l, j: (l, 0, j)),
        out_shape=jax.ShapeDtypeStruct((nl, N_DEV, Ws), F32), compiler_params=_cp(("parallel", "parallel")),
        name="ada_fwd")(c8, ada_w)


def ada_wgrad(c8t, dmod):
    nl, _, Ws = dmod.shape
    D = c8t.shape[0]
    tm = _tile(D, 512, 8)

    def body(c_ref, d_ref, o_ref):
        sc = _silu(c_ref[...])
        acc = sc[:, 0:1] * d_ref[0:1, :]
        for e in range(1, N_DEV):
            acc = acc + sc[:, e:e + 1] * d_ref[e:e + 1, :]
        o_ref[...] = acc

    return pl.pallas_call(
        body, grid=(nl, D // tm),
        in_specs=[pl.BlockSpec((tm, N_DEV), lambda l, i: (i, 0)), pl.BlockSpec((None, N_DEV, Ws), lambda l, i: (l, 0, 0))],
        out_specs=pl.BlockSpec((None, tm, Ws), lambda l, i: (l, i, 0)),
        out_shape=jax.ShapeDtypeStruct((nl, D, Ws), F32), compiler_params=_cp(("parallel", "parallel")),
        name="ada_wgrad")(c8t, dmod)


def adamw(w, g, m, v, name):
    R, C = w.shape
    tr = _tile(R, 256, 8)
    c1 = 1.0 - ADAM_B1 ** ADAM_STEP
    c2 = 1.0 - ADAM_B2 ** ADAM_STEP

    def body(w_ref, g_ref, m_ref, v_ref, d_ref, nm_ref, nv_ref):
        gv = g_ref[...]
        nm = ADAM_B1 * m_ref[...] + (1.0 - ADAM_B1) * gv
        nv = ADAM_B2 * v_ref[...] + (1.0 - ADAM_B2) * (gv * gv)
        nm_ref[...] = nm
        nv_ref[...] = nv
        d_ref[...] = -ADAM_LR * ((nm / c1) / (jnp.sqrt(nv / c2) + ADAM_EPS) + ADAM_WD * w_ref[...])

    return pl.pallas_call(
        body, grid=(R // tr,), in_specs=_row_specs(tr, [C] * 4), out_specs=_row_specs(tr, [C] * 3),
        out_shape=[jax.ShapeDtypeStruct((R, C), F32)] * 3, compiler_params=_cp(("parallel",)), name=name)(w, g, m, v)


def sum_leading(t, name, out_dtype=F32):
    S, R, C = t.shape
    tr = _tile(R, 256, 16)

    def body(t_ref, o_ref):
        acc = t_ref[0].astype(F32)
        for s in range(1, S):
            acc = acc + t_ref[s].astype(F32)
        o_ref[...] = acc.astype(out_dtype)

    return pl.pallas_call(
        body, grid=(R // tr,), in_specs=[pl.BlockSpec((S, tr, C), lambda i: (0, i, 0))],
        out_specs=pl.BlockSpec((tr, C), lambda i: (i, 0)), out_shape=jax.ShapeDtypeStruct((R, C), out_dtype),
        compiler_params=_cp(("parallel",)), name=name)(t)


def add_half(g, a, core, name, by_cols=False):
    S, R, C = g.shape

    def body(core_ref, g_ref, a_ref, o_ref):
        o_ref[...] = (g_ref[...].astype(F32) + a_ref[...].astype(F32)).astype(BF16)

    if by_cols:
        hc = C // 2
        tr = _tile(R, 256, 16)
        return pl.pallas_call(
            body,
            grid_spec=pltpu.PrefetchScalarGridSpec(
                num_scalar_prefetch=1, grid=(S, R // tr),
                in_specs=[pl.BlockSpec((None, tr, hc), lambda s, i, core_ref: (s, i, core_ref[0])),
                          pl.BlockSpec((None, tr, hc), lambda s, i, core_ref: (s, i, 0))],
                out_specs=pl.BlockSpec((None, tr, hc), lambda s, i, core_ref: (s, i, 0))),
            out_shape=jax.ShapeDtypeStruct((S, R, hc), BF16), compiler_params=_cp(("parallel", "parallel")),
            name=name)(core, g, a)
    h = R // 2
    tr = _tile(h, 256, 16)
    nb = h // tr

    return pl.pallas_call(
        body,
        grid_spec=pltpu.PrefetchScalarGridSpec(
            num_scalar_prefetch=1, grid=(S, nb),
            in_specs=[pl.BlockSpec((None, tr, C), lambda s, i, core_ref: (s, core_ref[0] * nb + i, 0)),
                      pl.BlockSpec((None, tr, C), lambda s, i, core_ref: (s, i, 0))],
            out_specs=pl.BlockSpec((None, tr, C), lambda s, i, core_ref: (s, i, 0))),
        out_shape=jax.ShapeDtypeStruct((S, h, C), BF16), compiler_params=_cp(("parallel", "parallel")),
        name=name)(core, g, a)


def sum_partials(own, landed, chip, name):
    _, h, C = own.shape
    tr = _tile(h, 256, 16)

    def body(chip_ref, own_ref, l_ref, o_ref):
        acc = own_ref[...].astype(F32)
        for j in range(3):
            acc = acc + l_ref[j].astype(F32)
        o_ref[...] = acc

    return pl.pallas_call(
        body,
        grid_spec=pltpu.PrefetchScalarGridSpec(
            num_scalar_prefetch=1, grid=(h // tr,),
            in_specs=[pl.BlockSpec((None, tr, C), lambda i, chip_ref: (chip_ref[0], i, 0)),
                      pl.BlockSpec((3, tr, C), lambda i, chip_ref: (0, i, 0))],
            out_specs=pl.BlockSpec((tr, C), lambda i, chip_ref: (i, 0))),
        out_shape=jax.ShapeDtypeStruct((h, C), F32), compiler_params=_cp(("parallel",)), name=name)(chip, own, landed)


def adamw_halves(w, g_mine, g_theirs, m, v, core, name):
    R, C = w.shape
    h = R // 2
    tr = _tile(h, 256, 8)
    nbh = h // tr
    c1 = 1.0 - ADAM_B1 ** ADAM_STEP
    c2 = 1.0 - ADAM_B2 ** ADAM_STEP

    def body(core_ref, w_ref, gm_ref, gt_ref, m_ref, v_ref, g_ref, d_ref, nm_ref, nv_ref):
        mine = (pl.program_id(0) // nbh) == core_ref[0]
        gv = jnp.where(mine, gm_ref[...], gt_ref[...])
        g_ref[...] = gv
        nm = ADAM_B1 * m_ref[...] + (1.0 - ADAM_B1) * gv
        nv = ADAM_B2 * v_ref[...] + (1.0 - ADAM_B2) * (gv * gv)
        nm_ref[...] = nm
        nv_ref[...] = nv
        d_ref[...] = -ADAM_LR * ((nm / c1) / (jnp.sqrt(nv / c2) + ADAM_EPS) + ADAM_WD * w_ref[...])

    full = pl.BlockSpec((tr, C), lambda i, core_ref: (i, 0))
    halfspec = pl.BlockSpec((tr, C), lambda i, core_ref: (i % nbh, 0))
    return pl.pallas_call(
        body,
        grid_spec=pltpu.PrefetchScalarGridSpec(
            num_scalar_prefetch=1, grid=(2 * nbh,), in_specs=[full, halfspec, halfspec, full, full],
            out_specs=[full] * 4),
        out_shape=[jax.ShapeDtypeStruct((R, C), F32)] * 4, compiler_params=_cp(("parallel",)),
        name=name)(core, w, g_mine, g_theirs, m, v)


_ANY = pl.BlockSpec(memory_space=pl.ANY)


def _place():
    x, y, c = lax.axis_index("x"), lax.axis_index("y"), lax.axis_index("c")
    chips = [(1 - x, y), (x, 1 - y), (1 - x, 1 - y)]
    return x, y, c, chips


def allgather_small(v, name, after=None):
    R, W = v.shape
    extra = [] if after is None else [after]

    def body(x_ref, *rest):
        out_ref, send_sems, recv_sems, local_sem = rest[len(extra):]
        x, y, c, chips = _place()
        me, sibling = (x, y, c), (x, y, 1 - c)

        def rows(px, py, pc):
            return out_ref.at[pl.ds((4 * px + 2 * py + pc) * R, R), :]

        def copy(k, block, to, src=None):
            return pltpu.make_async_remote_copy(
                src_ref=rows(*block) if src is None else src, dst_ref=rows(*block),
                send_sem=send_sems.at[k], recv_sem=recv_sems.at[k], device_id=to, device_id_type=MESH)

        mine = pltpu.make_async_copy(x_ref, rows(*me), local_sem)
        mine.start()
        first = [copy(0, me, sibling, src=x_ref)]
        first += [copy(1 + j, me, (*chip, c), src=x_ref) for j, chip in enumerate(chips)]
        for cp in first:
            cp.start()
        passed = [copy(4 + j, (*chip, c), sibling) for j, chip in enumerate(chips)]
        for j, chip in enumerate(chips):
            copy(1 + j, (*chip, c), me).wait_recv()
            passed[j].start()
        copy(0, sibling, me).wait_recv()
        for j, chip in enumerate(chips):
            copy(4 + j, (*chip, 1 - c), me).wait_recv()
        for cp in first + passed:
            cp.wait_send()
        mine.wait()

    return pl.pallas_call(
        body, out_shape=jax.ShapeDtypeStruct((N_DEV * R, W), v.dtype),
        in_specs=[pl.BlockSpec(memory_space=pltpu.VMEM)] + [_ANY] * len(extra),
        out_specs=pl.BlockSpec(memory_space=pltpu.VMEM),
        scratch_shapes=[pltpu.SemaphoreType.DMA((7,)), pltpu.SemaphoreType.DMA((7,)), pltpu.SemaphoreType.DMA],
        name=name)(v, *extra)


def allgather_routed(shard, name):
    R, C = shard.shape
    hc = C // 2
    ra = (R // 2) // 16 * 16

    def body(in_ref, out_ref, send_sems, recv_sems):
        x, y, c, _ = _place()
        xn, yn = (1 - x, y, c), (x, 1 - y, c)
        sibling = (x, y, 1 - c)
        p, pxn, pyn, pdg = 2 * x + y, 2 * (1 - x) + y, 2 * x + (1 - y), 2 * (1 - x) + (1 - y)
        rows_a, rows_b, rows_all = pl.ds(0, ra), pl.ds(ra, R - ra), pl.ds(0, R)

        def win(ref, rows, core):
            return ref.at[rows, pl.ds(pl.multiple_of(core * hc, 128), hc)]

        def copy(k, chip_id, rows, core, to, src=None):
            blk = win(out_ref.at[chip_id], rows, core)
            return pltpu.make_async_remote_copy(
                src_ref=blk if src is None else src, dst_ref=blk, send_sem=send_sems.at[k], recv_sem=recv_sems.at[k],
                device_id=to, device_id_type=MESH)

        own = [copy(0, p, rows_a, c, xn, src=win(in_ref, rows_a, c)), copy(1, p, rows_b, c, xn, src=win(in_ref, rows_b, c)),
               copy(2, p, rows_b, c, yn, src=win(in_ref, rows_b, c)), copy(3, p, rows_a, c, yn, src=win(in_ref, rows_a, c))]
        for cp in own:
            cp.start()
        copy(0, pxn, rows_a, c, xn).wait_recv()
        fwd_a = copy(4, pxn, rows_a, c, yn)
        fwd_a.start()
        copy(2, pyn, rows_b, c, yn).wait_recv()
        fwd_b = copy(5, pyn, rows_b, c, xn)
        fwd_b.start()
        copy(1, pxn, rows_b, c, xn).wait_recv()
        copy(3, pyn, rows_a, c, yn).wait_recv()
        passed = [copy(6, pxn, rows_all, c, sibling), copy(7, pyn, rows_all, c, sibling)]
        for cp in passed:
            cp.start()
        copy(4, pdg, rows_a, c, yn).wait_recv()
        passed.append(copy(8, pdg, rows_a, c, sibling))
        passed[-1].start()
        copy(5, pdg, rows_b, c, xn).wait_recv()
        passed.append(copy(9, pdg, rows_b, c, sibling))
        passed[-1].start()
        for k, (chip_id, rows) in enumerate([(pxn, rows_all), (pyn, rows_all), (pdg, rows_a), (pdg, rows_b)]):
            copy(6 + k, chip_id, rows, 1 - c, sibling).wait_recv()
        for cp in own + [fwd_a, fwd_b] + passed:
            cp.wait_send()

    out = pl.pallas_call(
        body, out_shape=jax.ShapeDtypeStruct((N_CHIPS, R, C), shard.dtype), in_specs=[_ANY], out_specs=_ANY,
        scratch_shapes=[pltpu.SemaphoreType.DMA((10,)), pltpu.SemaphoreType.DMA((10,))], name=name)(shard)
    chip = 2 * lax.axis_index("x") + lax.axis_index("y")
    return lax.dynamic_update_index_in_dim(out, shard, chip, 0)


_HBM = pl.BlockSpec(memory_space=pltpu.HBM)
_SEM = pl.BlockSpec(memory_space=pltpu.SEMAPHORE)
_EFFECT = pltpu.SideEffectType.DATAFLOW_SIDE_EFFECTING


def _chip_copies(kind, srcs, lands, send_sems, recv_sems):
    x, y, c, chips = _place()
    p = 2 * x + y
    cps = []
    if kind == "sibling":
        for i in range(len(srcs)):
            h = srcs[i].shape[1] // 2
            cps.append(pltpu.make_async_remote_copy(
                src_ref=srcs[i].at[:, pl.ds((1 - c) * h, h), :], dst_ref=lands[i], send_sem=send_sems.at[3 * i],
                recv_sem=recv_sems.at[3 * i], device_id=(x, y, 1 - c), device_id_type=MESH))
        return cps
    for i in range(len(srcs)):
        for j, (cx, cy) in enumerate(chips):
            if kind == "gather":
                src, dst = srcs[i].at[c], lands[i].at[p, c]
            else:
                src, dst = srcs[i].at[2 * cx + cy], lands[i].at[j]
            cps.append(pltpu.make_async_remote_copy(
                src_ref=src, dst_ref=dst, send_sem=send_sems.at[3 * i + j], recv_sem=recv_sems.at[3 * i + j],
                device_id=(cx, cy, c), device_id_type=MESH))
    return cps


def split_start(kind, srcs, land_shapes, after, name):
    n = len(srcs)

    def body(*refs):
        src_refs, land_refs = refs[:n], refs[n:2 * n]
        send_sems, recv_sems = refs[2 * n + 1], refs[2 * n + 2]
        token = refs[-1]
        for cp in _chip_copies(kind, src_refs, land_refs, send_sems, recv_sems):
            cp.start()
        token[...] = jnp.zeros_like(token)

    lands = [pltpu.with_memory_space_constraint(lax.empty(s, BF16), pltpu.HBM) for s in land_shapes]
    outs = pl.pallas_call(
        body, name=name,
        out_shape=(pltpu.SemaphoreType.DMA((3 * n,)), pltpu.SemaphoreType.DMA((3 * n,)),
                   *[pltpu.HBM(s.shape, s.dtype) for s in srcs], *[pltpu.HBM(s, BF16) for s in land_shapes],
                   jax.ShapeDtypeStruct((8, 128), F32)),
        in_specs=[_HBM] * (2 * n) + [_ANY],
        out_specs=(_SEM, _SEM, *([_HBM] * (2 * n)), pl.BlockSpec(memory_space=pltpu.VMEM)),
        input_output_aliases={i: 2 + i for i in range(2 * n)},
        compiler_params=pltpu.CompilerParams(has_side_effects=_EFFECT),
    )(*[pltpu.with_memory_space_constraint(s, pltpu.HBM) for s in srcs], *lands, after)
    return outs[0], outs[1], outs[2:2 + n], outs[2 + n:2 + 2 * n], outs[-1]


def split_wait(kind, send_sems, recv_sems, srcs, lands, after, name):
    n = len(srcs)

    def body(*refs):
        src_refs, land_refs = refs[:n], refs[n:2 * n]
        ssem, rsem = refs[2 * n], refs[2 * n + 1]
        for cp in _chip_copies(kind, src_refs, land_refs, ssem, rsem):
            cp.wait_send()
            cp.wait_recv()

    outs = pl.pallas_call(
        body, name=name,
        out_shape=[pltpu.HBM(s.shape, s.dtype) for s in srcs] + [pltpu.HBM(s.shape, s.dtype) for s in lands],
        in_specs=[_HBM] * (2 * n) + [_SEM, _SEM, _ANY], out_specs=[_HBM] * (2 * n),
        input_output_aliases={i: i for i in range(2 * n)},
        compiler_params=pltpu.CompilerParams(has_side_effects=_EFFECT),
    )(*srcs, *lands, send_sems, recv_sems, after)
    return outs[:n], outs[n:]


def pass_to_sibling(lands):
    n = len(lands)

    def body(*refs):
        ins, outs = refs[:n], refs[n:2 * n]
        send_sems, recv_sems = refs[2 * n:]
        x, y, c, chips = _place()
        cps = []
        for i in range(n):
            for j, (cx, cy) in enumerate(chips):
                blk = outs[i].at[2 * cx + cy, c]
                cps.append(pltpu.make_async_remote_copy(
                    src_ref=ins[i].at[2 * cx + cy, c], dst_ref=blk, send_sem=send_sems.at[3 * i + j],
                    recv_sem=recv_sems.at[3 * i + j], device_id=(x, y, 1 - c), device_id_type=MESH))
        for cp in cps:
            cp.start()
        for cp in cps:
            cp.wait()

    return pl.pallas_call(
        body, out_shape=[jax.ShapeDtypeStruct(t.shape, t.dtype) for t in lands], in_specs=[_ANY] * n,
        out_specs=[_ANY] * n, input_output_aliases={i: i for i in range(n)},
        scratch_shapes=[pltpu.SemaphoreType.DMA((3 * n,)), pltpu.SemaphoreType.DMA((3 * n,))],
        name="ag_pass_to_sibling")(*lands)


def exchange_halves_to_sibling(gs, name, by_cols=False):
    n = len(gs)

    def body(*refs):
        ins, outs = refs[:n], refs[n:2 * n]
        send_sems, recv_sems = refs[2 * n:]
        x, y, c, _ = _place()
        cps = []
        for i in range(n):
            if by_cols:
                hc = ins[i].shape[2] // 2
                src = ins[i].at[:, :, pl.ds(pl.multiple_of((1 - c) * hc, 128), hc)]
            else:
                h = ins[i].shape[1] // 2
                src = ins[i].at[:, pl.ds((1 - c) * h, h), :]
            cps.append(pltpu.make_async_remote_copy(
                src_ref=src, dst_ref=outs[i],
                send_sem=send_sems.at[i], recv_sem=recv_sems.at[i], device_id=(x, y, 1 - c), device_id_type=MESH))
        for cp in cps:
            cp.start()
        for cp in cps:
            cp.wait()

    halve = (lambda s: (s[0], s[1], s[2] // 2)) if by_cols else (lambda s: (s[0], s[1] // 2, s[2]))
    return pl.pallas_call(
        body, out_shape=[jax.ShapeDtypeStruct(halve(g.shape), g.dtype) for g in gs],
        in_specs=[_ANY] * n, out_specs=[_ANY] * n,
        scratch_shapes=[pltpu.SemaphoreType.DMA((n,)), pltpu.SemaphoreType.DMA((n,))],
        name=name)(*gs)


def join_halves(rs, name):
    n = len(rs)

    def body(*refs):
        ins, outs = refs[:n], refs[n:2 * n]
        send_sems, recv_sems = refs[2 * n:]
        x, y, c, _ = _place()
        cps = [pltpu.make_async_remote_copy(
            src_ref=ins[i], dst_ref=outs[i], send_sem=send_sems.at[i], recv_sem=recv_sems.at[i],
            device_id=(x, y, 1 - c), device_id_type=MESH) for i in range(n)]
        for cp in cps:
            cp.start()
        for cp in cps:
            cp.wait()

    return pl.pallas_call(
        body, out_shape=[jax.ShapeDtypeStruct(r.shape, r.dtype) for r in rs],
        in_specs=[_ANY] * n, out_specs=[_ANY] * n,
        scratch_shapes=[pltpu.SemaphoreType.DMA((n,)), pltpu.SemaphoreType.DMA((n,))],
        name=name)(*rs)


def _pack(parts, row_mult=8):
    flat = jnp.concatenate([p.reshape(-1).astype(F32) for p in parts])
    unit = row_mult * 128
    n = -(-flat.shape[0] // unit) * unit
    return jnp.pad(flat, (0, n - flat.shape[0])).reshape(n // 128, 128)


def _unpack(flat, shapes):
    out, off = [], 0
    for s in shapes:
        n = int(np.prod(s))
        out.append(flat[off:off + n].reshape(s))
        off += n
    return out


def _gather_packed(parts, name):
    packed = _pack(parts)
    g = allgather_small(packed, name).reshape(N_DEV, -1)
    return _unpack_rows(g, [p.shape for p in parts])


def _unpack_rows(g, shapes):
    out, off = [], 0
    for s in shapes:
        n = int(np.prod(s))
        out.append(g[:, off:off + n].reshape((g.shape[0],) + tuple(s)))
        off += n
    return out


def _by_chip(t, axis):
    return jnp.concatenate([t[2 * p] for p in range(N_CHIPS)], axis=axis)


def kernel(x, c, ada_w, ada_b, ln_g, ln_b, a_in_w, a_conv_w, a_conv_b, a_dt_bias, a_A_log, a_D, a_norm_g, a_out_w, kv_w, b_in_w, b_out_w, loss_target, m_ada_w, m_ada_b, m_ln_g, m_ln_b, m_a_in_w, m_a_conv_w, m_a_conv_b, m_a_dt_bias, m_a_A_log, m_a_D, m_a_norm_g, m_a_out_w, m_kv_w, m_b_in_w, m_b_out_w, v_ada_w, v_ada_b, v_ln_g, v_ln_b, v_a_in_w, v_a_conv_w, v_a_conv_b, v_a_dt_bias, v_a_A_log, v_a_D, v_a_norm_g, v_a_out_w, v_kv_w, v_b_in_w, v_b_out_w):
    ax, ay, ac = lax.axis_index("x"), lax.axis_index("y"), lax.axis_index("c")
    chip = 2 * ax + ay
    dev = 4 * ax + 2 * ay + ac
    xin = x[0]
    tgt = loss_target[0]
    L, D = xin.shape
    G, P = SSD_G, SSD_P
    H = a_dt_bias.shape[1]
    Kh = H // G
    DI = H * P
    CONVD = a_conv_b.shape[1] * N_CHIPS
    HW = DIL_H * DIL_E
    Ws = ada_w.shape[2]

    w_in_g = allgather_routed(jnp.transpose(a_in_w[0]).astype(BF16), "allgather_w_in")
    later = [a_out_w[0].astype(BF16), kv_w.astype(BF16), b_in_w[0].astype(BF16), b_out_w[0].astype(BF16)]
    later_split = [s.reshape(2, s.shape[0] // 2, s.shape[1]) for s in later]
    ag_ssem, ag_rsem, ag_srcs, ag_lands, ag_token = split_start(
        "gather", later_split, [(N_CHIPS,) + s.shape for s in later_split], w_in_g, "ag_later_start")
    w_in_t = w_in_g.reshape(-1, D)
    w_dt_t = jnp.pad(w_in_t[DI + CONVD:], ((0, 128 - H), (0, 0)))

    c8, cw8, cb8, ng8 = _gather_packed([c[0], a_conv_w[0], a_conv_b[0], a_norm_g[0]], "allgather_small_params")
    conv_w = _by_chip(cw8, 1)
    conv_b = _by_chip(cb8, 0).reshape(1, CONVD)
    norm_g = _by_chip(ng8, 0).reshape(1, DI)

    mod_s = ada_fwd(c8, ada_w)
    (mod8,) = _gather_packed([mod_s], "allgather_small_mod")
    mods = _by_chip(mod8, 2)
    mod = lax.dynamic_index_in_dim(mods, dev, axis=1, keepdims=False) + ada_b
    shift = [mod[l:l + 1, :D] for l in range(DEPTH)]
    scale = [mod[l:l + 1, D:2 * D] for l in range(DEPTH)]
    gate = [mod[l:l + 1, 2 * D:] for l in range(DEPTH)]
    lg = [ln_g[l:l + 1] for l in range(DEPTH)]
    lb = [ln_b[l:l + 1] for l in range(DEPTH)]

    h0 = modulate(xin, scale[0] + ag_token[0:1, 0:1], shift[0], "modulate0")
    zx = mm_nt(h0, w_in_t, BF16, "mm_in_zx", kw_rows=DI + CONVD)
    dtp = mm_nt(h0, w_dt_t, F32, "mm_in_dt")
    xbc = conv_fwd(zx, DI, conv_w, conv_b)
    dtp_g = jnp.transpose(dtp[:, :H].reshape(L, G, Kh), (1, 0, 2))
    dtp_gT = jnp.transpose(dtp_g, (0, 2, 1))
    vecs = [a_dt_bias.reshape(G, 1, Kh), a_dt_bias.reshape(G, Kh, 1), a_A_log.reshape(G, 1, Kh),
            a_A_log.reshape(G, Kh, 1), a_D.reshape(G, 1, Kh), a_D.reshape(G, Kh, 1)]
    y_ssd, states = ssd_fwd(xbc, dtp_g, dtp_gT, *vecs, DI)
    yn = rms_gate_fwd(y_ssd, zx, norm_g)
    later_split, ag_lands = split_wait("gather", ag_ssem, ag_rsem, ag_srcs, ag_lands, yn, "ag_later_wait")
    ag_lands = pass_to_sibling(ag_lands)
    w_out_g, w_kv_g, w_bin_g, w_bout_g = [
        lax.dynamic_update_index_in_dim(o, s, chip, 0).reshape((N_CHIPS,) + full.shape)
        for o, s, full in zip(ag_lands, later_split, later)]
    ymix0 = mm_nn(yn, w_out_g.reshape(-1, D), F32, "mm_out_a")
    x1, x1b, h1 = ln_mid(xin, ymix0, gate[0], lg[0], lb[0], scale[1], shift[1])

    n_grp = len(DIL_PATTERNS)
    cb = HW // 512
    assert w_bin_g.shape[2] == HW
    kv3 = [mm_cols_dilated(x1b, w_kv_g, [g * cb + t for t in range(cb)] + [(n_grp + g) * cb + t for t in range(cb)],
                           DIL_PATTERNS[g][1], f"mm_kv_{g}") for g in range(n_grp)]
    q3 = [mm_cols_dilated(h1, w_bin_g, [g], DIL_PATTERNS[g][1], f"mm_q_{g}", tn=HW) for g in range(n_grp)]
    z_b = mm_nn(h1, w_bin_g[n_grp], BF16, "mm_z_b")
    os_, lses = [], []
    for gi in range(len(DIL_PATTERNS)):
        o, lse = attn_fwd(q3[gi], kv3[gi], gi)
        os_.append(o)
        lses.append(lse)
    om = merge_fwd(os_, lses, z_b)
    ymix1 = mm_nn(om, w_bout_g, F32, "mm_out_b", stack="col")
    dres2, dy2, dg1, db1, dgate1, sq = ln_final_fwd_bwd(x1, ymix1, gate[1], lg[1], lb[1], tgt)
    loss_part = 0.5 * jnp.sum(sq) / D

    g_bout = mm_tn(om, dy2, BF16, "mm_gw_out_b", stack="col")
    dgated = mm_nt(dy2, w_bout_g, BF16, "mm_gx_out_b", stack="col")
    dos, dprs, dz_b = merge_bwd(dgated, os_, lses, z_b)
    dqs, dks, dvs = [], [], []
    for gi in range(len(DIL_PATTERNS)):
        dq, dk, dv = attn_bwd(q3[gi], kv3[gi], dos[gi], lses[gi], dprs[gi], gi)
        dqs.append(dq)
        dks.append(dk)
        dvs.append(dv)
    dqz = jnp.concatenate(dqs + [dz_b], axis=1)
    dkv = jnp.concatenate(dks + dvs, axis=1)
    g_bin = mm_tn(h1, dqz, BF16, "mm_gw_in_b", stack="col")
    dh1 = mm_nt(dqz, w_bin_g, BF16, "mm_gx_in_b", stack="col")
    g_kv = mm_tn(x1b, dkv, BF16, "mm_gw_kv", stack="col")

    core = ac.astype(jnp.int32).reshape(1)
    chip_i = chip.astype(jnp.int32).reshape(1)

    def begin_exchange(gs, tag):
        shapes = [(g.shape[0], g.shape[1] // 2, g.shape[2]) for g in gs]
        return split_start("sibling", gs, shapes, gs[0], "rs_x%s_start" % tag)

    def begin_scatter(gs, nms, tag, exchange=None, after=None, by_cols=False):
        if exchange is None:
            sib = exchange_halves_to_sibling(gs, "rs_sibling_exchange_" + tag, by_cols=by_cols)
        else:
            gs, sib = split_wait("sibling", exchange[0], exchange[1], exchange[2], exchange[3], after,
                                 "rs_x%s_wait" % tag)
        parts = [add_half(g, a, core, "rs_add_" + nm, by_cols=by_cols) for g, a, nm in zip(gs, sib, nms)]
        return split_start("scatter", parts, [(3,) + t.shape[1:] for t in parts], parts[0], "rs_%s_start" % tag)

    def finish_scatter(handles, after, tag):
        nms, owns, landed = [], [], []
        for k, (handle, hn) in enumerate(handles):
            parts, lands = split_wait("scatter", handle[0], handle[1], handle[2], handle[3], after,
                                      "rs_%s%d_wait" % (tag, k))
            nms += hn
            owns += list(parts)
            landed += list(lands)
        halves = [sum_partials(own, t, chip_i, "rs_sum_" + nm) for own, t, nm in zip(owns, landed, nms)]
        theirs = join_halves(halves, "rs_join_halves_" + tag)
        return dict(zip(nms, zip(halves, theirs)))

    names_b = ["kv", "in_b", "out_b"]
    ex_b = begin_exchange([g_kv, g_bin, g_bout], "b")
    dx1_kv = mm_nt(dkv, w_kv_g, BF16, "mm_gx_kv", stack="col", after=ex_b[4])
    rs_b = begin_scatter(None, names_b, "b", exchange=ex_b, after=dx1_kv)

    dres1, dy1, dg0, db0, dgate0, dscale1, dshift1 = mod_ln_bwd(
        dres2, dh1, dx1_kv, x1, scale[1], xin, ymix0, gate[0] + rs_b[4][0:1, 0:1], lg[0])
    g_out = mm_tn(yn, dy1, BF16, "mm_gw_out_a", stack="row")
    ex_a1 = begin_exchange([g_out], "a1")
    dyn = mm_nt(dy1, w_out_g, BF16, "mm_gx_out_a", stack="row", after=ex_a1[4])
    rs_a1 = begin_scatter(None, ["out_a"], "a1", exchange=ex_a1, after=dyn)
    dy_ssd, dz_a, dnorm_g = rms_gate_bwd(dyn, y_ssd, zx, norm_g + rs_a1[4][0:1, 0:1])
    dxs, dB, dC, ddtp_g, dbias_g, dalog_g, dD_g = ssd_bwd(xbc, dtp_g, dtp_gT, *vecs, states, dy_ssd, DI)
    dxbc = jnp.concatenate([dxs, dB, dC], axis=1)
    dxbc_pre, dconv_w, dconv_b = conv_bwd(zx, DI, conv_w, conv_b, dxbc)
    dzx = jnp.concatenate([dz_a, dxbc_pre], axis=1)
    ddtp = jnp.pad(jnp.transpose(ddtp_g, (1, 0, 2)).reshape(L, H), ((0, 0), (0, 128 - H)))
    g_inT = mm_tn(dzx, h0, BF16, "mm_gw_in_zx", m_rows=DI + CONVD + H)
    g_dtT = mm_tn(ddtp, h0, BF16, "mm_gw_in_dt")
    g_inT = lax.dynamic_update_slice(g_inT, g_dtT[:H], (DI + CONVD, 0))
    rs_a2 = begin_scatter([g_inT.reshape(N_CHIPS, -1, D)], ["in_a"], "a2", by_cols=True)
    dh0 = mm_nn(dzx, w_in_t, BF16, "mm_gx_in_zx", after=rs_a2[4])
    dh0_dt = mm_nn(ddtp, w_dt_t, F32, "mm_gx_in_dt")
    grad_x, dscale0, dshift0 = mod_bwd(dres1, dh0, dh0_dt, xin, scale[0] + rs_a2[4][0:1, 0:1], "mod_bwd0")
    g_halves = finish_scatter([(rs_b, names_b)], grad_x, "b")

    def step_halves(w, m, v, nm):
        shp = w.shape
        mine, theirs_ = g_halves[nm]
        outs4 = adamw_halves(w.reshape(-1, shp[-1]), mine, theirs_, m.reshape(-1, shp[-1]), v.reshape(-1, shp[-1]),
                             core, "adamw_" + nm)
        return tuple(t.reshape(shp) for t in outs4)

    big = {
        "kv_w": step_halves(kv_w, m_kv_w, v_kv_w, "kv"),
        "b_in_w": step_halves(b_in_w, m_b_in_w, v_b_in_w, "in_b"),
        "b_out_w": step_halves(b_out_w, m_b_out_w, v_b_out_w, "out_b"),
    }
    g_halves.update(finish_scatter([(rs_a1, ["out_a"]), (rs_a2, ["in_a"])], big["kv_w"][1], "a"))
    g_halves["in_a"] = tuple(jnp.transpose(t) for t in g_halves["in_a"])
    big["a_in_w"] = step_halves(a_in_w, m_a_in_w, v_a_in_w, "in_a")
    big["a_out_w"] = step_halves(a_out_w, m_a_out_w, v_a_out_w, "out_a")

    dmod = jnp.concatenate([jnp.concatenate([dshift0, dscale0, dgate0], axis=1),
                            jnp.concatenate([dshift1, dscale1, dgate1], axis=1)], axis=0)
    small_parts = [jnp.concatenate([dg0, dg1], axis=0), jnp.concatenate([db0, db1], axis=0),
                   dbias_g.reshape(1, H), dalog_g.reshape(1, H), dD_g.reshape(1, H),
                   dconv_w, dconv_b, dnorm_g, loss_part.reshape(1, 1)]
    small_shapes = [p.shape for p in small_parts]
    packed = jnp.concatenate([_pack([dmod]), _pack(small_parts)], axis=0)
    n_mod_rows = _pack([dmod]).shape[0]
    gathered = allgather_small(packed, "allgather_small_grads", after=g_halves["in_a"][1]).reshape(N_DEV, -1, 128)
    dmod8 = gathered[:, :n_mod_rows].reshape(N_DEV, -1)[:, :2 * 3 * D].reshape(N_DEV, DEPTH, 3 * D)
    summed = sum_leading(gathered, "sum_small")
    g_ada_b = summed[:n_mod_rows].reshape(-1)[:2 * 3 * D].reshape(DEPTH, 3 * D)
    (g_ln_g, g_ln_b, g_dt_bias, g_a_log, g_dsk, g_conv_w, g_conv_b, g_norm_g, loss_all) = _unpack(
        summed[n_mod_rows:].reshape(-1), small_shapes)
    loss = loss_all.reshape(())
    Cs = CONVD // N_CHIPS
    g_conv_w_s = lax.dynamic_slice_in_dim(g_conv_w, chip * Cs, Cs, axis=1)
    g_conv_b_s = lax.dynamic_slice_in_dim(g_conv_b, chip * Cs, Cs, axis=1)
    g_norm_g_s = lax.dynamic_slice_in_dim(g_norm_g, chip * (DI // N_CHIPS), DI // N_CHIPS, axis=1)
    dmod_s = jnp.transpose(lax.dynamic_slice_in_dim(dmod8, chip * Ws, Ws, axis=2), (1, 0, 2))

    def step2d(w, g, m, v, nm):
        shp = w.shape
        d_, m_, v_ = adamw(w.reshape(-1, shp[-1]), g.reshape(-1, shp[-1]), m.reshape(-1, shp[-1]),
                           v.reshape(-1, shp[-1]), "adamw_" + nm)
        return g.reshape(shp), d_.reshape(shp), m_.reshape(shp), v_.reshape(shp)

    big["ada_w"] = step2d(ada_w, ada_wgrad(jnp.transpose(c8), dmod_s), m_ada_w, v_ada_w, "ada_w")
    small_names = ["ada_b", "ln_g", "ln_b", "a_conv_w", "a_conv_b", "a_dt_bias", "a_A_log", "a_D", "a_norm_g"]
    small_w = [ada_b, ln_g, ln_b, a_conv_w, a_conv_b, a_dt_bias, a_A_log, a_D, a_norm_g]
    small_m = [m_ada_b, m_ln_g, m_ln_b, m_a_conv_w, m_a_conv_b, m_a_dt_bias, m_a_A_log, m_a_D, m_a_norm_g]
    small_v = [v_ada_b, v_ln_g, v_ln_b, v_a_conv_w, v_a_conv_b, v_a_dt_bias, v_a_A_log, v_a_D, v_a_norm_g]
    small_g = [g_ada_b, g_ln_g, g_ln_b, g_conv_w_s, g_conv_b_s, g_dt_bias, g_a_log, g_dsk, g_norm_g_s]
    shapes = [w.shape for w in small_w]
    small_g = [g.reshape(s) for g, s in zip(small_g, shapes)]
    d_p, m_p, v_p = adamw(_pack(small_w), _pack(small_g), _pack(small_m), _pack(small_v), "adamw_small")
    small = {}
    for nm, g, d_, m_, v_ in zip(small_names, small_g, _unpack(d_p.reshape(-1), shapes), _unpack(m_p.reshape(-1), shapes),
                                 _unpack(v_p.reshape(-1), shapes)):
        small[nm] = (g, d_, m_, v_)
    allw = {**big, **small}
    order = ["ada_w", "ada_b", "ln_g", "ln_b", "a_in_w", "a_conv_w", "a_conv_b", "a_dt_bias", "a_A_log", "a_D",
             "a_norm_g", "a_out_w", "kv_w", "b_in_w", "b_out_w"]
    outs = [loss, grad_x.reshape(x.shape)]
    for k in range(4):
        outs += [allw[n][k] for n in order]
    return tuple(outs)
```

```python
import functools

import jax
import jax.numpy as jnp
import numpy as np
from jax import lax
from jax.experimental import pallas as pl
from jax.experimental.pallas import tpu as pltpu

F32 = jnp.float32
BF16 = jnp.bfloat16
MESH = pl.DeviceIdType.MESH

DEPTH = 2
ALPHA = (2 * DEPTH) ** 0.25
LN_EPS = 1e-5
RMS_EPS = 1e-5
SSD_P = 64
SSD_N = 128
SSD_Q = 256
SSD_G = 8
CONV_W = 4
DIL_PATTERNS = ((128, 1), (512, 4), (2048, 16))
DIL_H = 8
DIL_E = 128
DIL_BLK = 128
ADAM_LR, ADAM_B1, ADAM_B2, ADAM_EPS, ADAM_WD, ADAM_STEP = 0.001, 0.9, 0.999, 1e-08, 0.01, 10

VMEM_LIMIT = 56 * 1024 * 1024
N_CHIPS = 4
N_DEV = 8


def _tile(dim, target, mult=128):
    if dim <= target:
        return dim
    t = (target // mult) * mult
    while t >= mult:
        if dim % t == 0:
            return t
        t -= mult
    return dim


def _cp(sem):
    return pltpu.CompilerParams(dimension_semantics=sem, vmem_limit_bytes=VMEM_LIMIT)


def _sigmoid(x):
    return 1.0 / (1.0 + jnp.exp(-x))


def _silu(x):
    return x * _sigmoid(x)


def _dsilu(x):
    s = _sigmoid(x)
    return s * (1.0 + x * (1.0 - s))


def _softplus(x):
    return jnp.maximum(x, 0.0) + jnp.log(1.0 + jnp.exp(-jnp.abs(x)))


def _mm_call(a, b, out_shape, grid, a_spec, b_spec, o_spec, acc_shape, dims, name, after=None):
    nk = grid[2]
    extra = [] if after is None else [after]

    def prod(a_ref, b_ref):
        return lax.dot_general(a_ref[...].astype(BF16), b_ref[...].astype(BF16), (dims, ((), ())),
                               preferred_element_type=F32)

    def body_single(a_ref, b_ref, *rest):
        o_ref = rest[len(extra)]
        o_ref[...] = prod(a_ref, b_ref).astype(o_ref.dtype)

    def body_multi(a_ref, b_ref, *rest):
        o_ref, acc_ref = rest[len(extra):]
        k = pl.program_id(2)

        @pl.when(k == 0)
        def _():
            acc_ref[...] = prod(a_ref, b_ref)

        @pl.when(jnp.logical_and(k > 0, k < nk - 1))
        def _():
            acc_ref[...] += prod(a_ref, b_ref)

        @pl.when(k == nk - 1)
        def _():
            o_ref[...] = (acc_ref[...] + prod(a_ref, b_ref)).astype(o_ref.dtype)

    return pl.pallas_call(
        body_single if nk == 1 else body_multi, grid=grid, in_specs=[a_spec, b_spec] + [_ANY] * len(extra),
        out_specs=o_spec, out_shape=out_shape, scratch_shapes=[] if nk == 1 else [pltpu.VMEM(acc_shape, F32)],
        compiler_params=_cp(("parallel", "parallel", "arbitrary")), name=name)(a, b, *extra)


def mm_nn(a, b, out_dtype, name, stack=None, tm=1024, tn=1024, tk=2048, n_cols=None, after=None):
    M, K = a.shape
    if stack is None:
        N = b.shape[1] if n_cols is None else n_cols
        tn, tk = _tile(N, tn), _tile(K, tk)
        b_spec = pl.BlockSpec((tk, tn), lambda i, j, k: (k, j))
    elif stack == "col":
        S, _, Ns = b.shape
        N = S * Ns
        tn, tk = _tile(Ns, tn), _tile(K, tk)
        npb = Ns // tn
        b_spec = pl.BlockSpec((None, tk, tn), lambda i, j, k: (j // npb, k, j % npb))
    else:
        S, Ks, N = b.shape
        tn, tk = _tile(N, tn), _tile(Ks, tk)
        kpb = Ks // tk
        b_spec = pl.BlockSpec((None, tk, tn), lambda i, j, k: (k // kpb, k % kpb, j))
    tm = _tile(M, tm)
    return _mm_call(a, b, jax.ShapeDtypeStruct((M, N), out_dtype), (M // tm, N // tn, K // tk),
                    pl.BlockSpec((tm, tk), lambda i, j, k: (i, k)), b_spec,
                    pl.BlockSpec((tm, tn), lambda i, j, k: (i, j)), (tm, tn), ((1,), (0,)), name, after=after)


def mm_cols_dilated(a, b, gcols, d, name, tm=1024, tn=512):
    L, K = a.shape
    S, _, Ns = b.shape
    tm, tn = _tile(L, tm), _tile(Ns, tn)
    npb = Ns // tn
    nj = len(gcols)
    rows = tm // d

    def body(cols_ref, a_ref, b_ref, o_ref, *scr):
        prod = jnp.dot(a_ref[...], b_ref[...], preferred_element_type=F32)
        if d == 1:
            o_ref[0] = prod.astype(BF16)
        else:
            for c in range(tn // 128):
                scr[0][c] = prod[:, c * 128:(c + 1) * 128]
            for r in range(d):
                for c in range(tn // 128):
                    o_ref[r, :, c * 128:(c + 1) * 128] = scr[0].at[c][pl.ds(r, rows, stride=d), :].astype(BF16)

    return pl.pallas_call(
        body,
        grid_spec=pltpu.PrefetchScalarGridSpec(
            num_scalar_prefetch=1, grid=(L // tm, nj),
            in_specs=[pl.BlockSpec((tm, K), lambda i, j, c: (i, 0)),
                      pl.BlockSpec((None, K, tn), lambda i, j, c: (c[j] // npb, 0, c[j] % npb))],
            out_specs=pl.BlockSpec((d, rows, tn), lambda i, j, c: (0, i, j)),
            scratch_shapes=[] if d == 1 else [pltpu.VMEM((tn // 128, tm, 128), F32)]),
        out_shape=jax.ShapeDtypeStruct((d, L // d, nj * tn), BF16),
        compiler_params=_cp(("parallel", "arbitrary")), name=name)(jnp.asarray(gcols, jnp.int32), a, b)


def mm_nt(a, b, out_dtype, name, stack=None, tm=1024, tn=1024, tk=2048, after=None, kw_rows=None):
    M, C = a.shape
    if stack is None:
        Kw = b.shape[0] if kw_rows is None else kw_rows
        tn, tk = _tile(Kw, tn), _tile(C, tk)
        b_spec = pl.BlockSpec((tn, tk), lambda i, j, k: (j, k))
    elif stack == "col":
        S, Kw, Cs = b.shape
        tn, tk = _tile(Kw, tn), _tile(Cs, tk)
        cpb = Cs // tk
        b_spec = pl.BlockSpec((None, tn, tk), lambda i, j, k: (k // cpb, j, k % cpb))
    else:
        S, Ks, _ = b.shape
        Kw = S * Ks
        tn, tk = _tile(Ks, tn), _tile(C, tk)
        jpb = Ks // tn
        b_spec = pl.BlockSpec((None, tn, tk), lambda i, j, k: (j // jpb, j % jpb, k))
    tm = _tile(M, tm)
    return _mm_call(a, b, jax.ShapeDtypeStruct((M, Kw), out_dtype), (M // tm, Kw // tn, C // tk),
                    pl.BlockSpec((tm, tk), lambda i, j, k: (i, k)), b_spec,
                    pl.BlockSpec((tm, tn), lambda i, j, k: (i, j)), (tm, tn), ((1,), (1,)), name, after=after)


def mm_tn(a, b, out_dtype, name, stack=None, n_stack=N_CHIPS, tm=1024, tn=1024, tk=2048, m_rows=None):
    L, M = a.shape
    N = b.shape[1]
    tk = _tile(L, tk)
    if stack is None:
        tm, tn = _tile(M, tm), _tile(N, tn)
        o_spec = pl.BlockSpec((tm, tn), lambda i, j, k: (i, j))
        out_shape = (M if m_rows is None else m_rows, N)
    elif stack == "col":
        Ns = N // n_stack
        tm, tn = _tile(M, tm), _tile(Ns, tn)
        npb = Ns // tn
        o_spec = pl.BlockSpec((None, tm, tn), lambda i, j, k: (j // npb, i, j % npb))
        out_shape = (n_stack, M, Ns)
    else:
        Ms = M // n_stack
        tm, tn = _tile(Ms, tm), _tile(N, tn)
        mpb = Ms // tm
        o_spec = pl.BlockSpec((None, tm, tn), lambda i, j, k: (i // mpb, i % mpb, j))
        out_shape = (n_stack, Ms, N)
    return _mm_call(a, b, jax.ShapeDtypeStruct(out_shape, out_dtype), (M // tm, N // tn, L // tk),
                    pl.BlockSpec((tk, tm), lambda i, j, k: (k, i)), pl.BlockSpec((tk, tn), lambda i, j, k: (k, j)),
                    o_spec, (tm, tn), ((0,), (0,)), name)


def _row_specs(tr, widths):
    return [pl.BlockSpec((tr, w), lambda i: (i, 0)) for w in widths]


def _vec_spec(w):
    return pl.BlockSpec((1, w), lambda i: (0, 0))


def _acc_rows(ref, val, i):
    s = jnp.sum(val, axis=0, keepdims=True)

    @pl.when(i == 0)
    def _():
        ref[...] = s

    @pl.when(i > 0)
    def _():
        ref[...] += s


def modulate(x, scale, shift, name):
    L, D = x.shape
    tr = _tile(L, 512, 16)

    def body(x_ref, sc_ref, sh_ref, h_ref):
        h_ref[...] = (x_ref[...] * (1.0 + sc_ref[...]) + sh_ref[...]).astype(BF16)

    return pl.pallas_call(
        body, grid=(L // tr,), in_specs=_row_specs(tr, [D]) + [_vec_spec(D)] * 2, out_specs=_row_specs(tr, [D])[0],
        out_shape=jax.ShapeDtypeStruct((L, D), BF16), compiler_params=_cp(("parallel",)), name=name)(x, scale, shift)


def _ln_core(x, y, gate, g, b):
    u = ALPHA * x + (1.0 + gate) * y
    mu = jnp.mean(u, axis=-1, keepdims=True)
    d = u - mu
    var = jnp.mean(d * d, axis=-1, keepdims=True)
    rstd = lax.rsqrt(var + LN_EPS)
    xhat = d * rstd
    return xhat * g + b, xhat, rstd


def ln_mid(x, y, gate, g, b, scale, shift):
    L, D = x.shape
    tr = _tile(L, 256, 16)

    def body(x_ref, y_ref, gate_ref, g_ref, b_ref, sc_ref, sh_ref, x1_ref, x1b_ref, h_ref):
        x1, _, _ = _ln_core(x_ref[...], y_ref[...], gate_ref[...], g_ref[...], b_ref[...])
        x1_ref[...] = x1
        x1b_ref[...] = x1.astype(BF16)
        h_ref[...] = (x1 * (1.0 + sc_ref[...]) + sh_ref[...]).astype(BF16)

    return pl.pallas_call(
        body, grid=(L // tr,), in_specs=_row_specs(tr, [D, D]) + [_vec_spec(D)] * 5,
        out_specs=_row_specs(tr, [D, D, D]),
        out_shape=[jax.ShapeDtypeStruct((L, D), F32), jax.ShapeDtypeStruct((L, D), BF16),
                   jax.ShapeDtypeStruct((L, D), BF16)],
        compiler_params=_cp(("parallel",)), name="ln_mid")(x, y, gate, g, b, scale, shift)


def _ln_bwd_rows(dout_v, xhat, rstd, g):
    dxh = dout_v * g
    m1 = jnp.mean(dxh, axis=-1, keepdims=True)
    m2 = jnp.mean(dxh * xhat, axis=-1, keepdims=True)
    return rstd * (dxh - m1 - xhat * m2)


def ln_final_fwd_bwd(x, y, gate, g, b, target):
    L, D = x.shape
    tr = _tile(L, 256, 16)

    def body(x_ref, y_ref, gate_ref, g_ref, b_ref, t_ref, dres_ref, dy_ref, dg_ref, db_ref, dgate_ref, sq_ref):
        i = pl.program_id(0)
        yv = y_ref[...]
        out, xhat, rstd = _ln_core(x_ref[...], yv, gate_ref[...], g_ref[...], b_ref[...])
        err = out - t_ref[...]
        dout_v = err * (1.0 / D)
        du = _ln_bwd_rows(dout_v, xhat, rstd, g_ref[...])
        dres_ref[...] = ALPHA * du
        dy_ref[...] = ((1.0 + gate_ref[...]) * du).astype(BF16)
        _acc_rows(dg_ref, dout_v * xhat, i)
        _acc_rows(db_ref, dout_v, i)
        _acc_rows(dgate_ref, du * yv, i)
        _acc_rows(sq_ref, err * err, i)

    return pl.pallas_call(
        body, grid=(L // tr,), in_specs=_row_specs(tr, [D, D]) + [_vec_spec(D)] * 3 + _row_specs(tr, [D]),
        out_specs=_row_specs(tr, [D, D]) + [_vec_spec(D)] * 4,
        out_shape=[jax.ShapeDtypeStruct((L, D), F32), jax.ShapeDtypeStruct((L, D), BF16)]
        + [jax.ShapeDtypeStruct((1, D), F32)] * 4,
        compiler_params=_cp(("arbitrary",)), name="ln_final_fwd_bwd")(x, y, gate, g, b, target)


def mod_ln_bwd(dres_in, dh, dskip, xmid, scale, x, y, gate, g):
    L, D = x.shape
    tr = _tile(L, 256, 16)

    def body(dres_ref, dh_ref, dskip_ref, xm_ref, sc_ref, x_ref, y_ref, gate_ref, g_ref,
             dres_out, dy_ref, dg_ref, db_ref, dgate_ref, dsc_ref, dsh_ref):
        i = pl.program_id(0)
        dh_v = dh_ref[...].astype(F32)
        dout_v = dres_ref[...] + dskip_ref[...].astype(F32) + dh_v * (1.0 + sc_ref[...])
        _acc_rows(dsc_ref, dh_v * xm_ref[...], i)
        _acc_rows(dsh_ref, dh_v, i)
        yv = y_ref[...]
        _, xhat, rstd = _ln_core(x_ref[...], yv, gate_ref[...], g_ref[...], 0.0)
        du = _ln_bwd_rows(dout_v, xhat, rstd, g_ref[...])
        dres_out[...] = ALPHA * du
        dy_ref[...] = ((1.0 + gate_ref[...]) * du).astype(BF16)
        _acc_rows(dg_ref, dout_v * xhat, i)
        _acc_rows(db_ref, dout_v, i)
        _acc_rows(dgate_ref, du * yv, i)

    return pl.pallas_call(
        body, grid=(L // tr,),
        in_specs=_row_specs(tr, [D] * 4) + [_vec_spec(D)] + _row_specs(tr, [D, D]) + [_vec_spec(D)] * 2,
        out_specs=_row_specs(tr, [D, D]) + [_vec_spec(D)] * 5,
        out_shape=[jax.ShapeDtypeStruct((L, D), F32), jax.ShapeDtypeStruct((L, D), BF16)]
        + [jax.ShapeDtypeStruct((1, D), F32)] * 5,
        compiler_params=_cp(("arbitrary",)), name="mod_ln_bwd")(dres_in, dh, dskip, xmid, scale, x, y, gate, g)


def mod_bwd(dres, dh, dh2, xin, scale, name):
    L, D = xin.shape
    tr = _tile(L, 256, 16)

    def body(dres_ref, dh_ref, dh2_ref, x_ref, sc_ref, dx_ref, dsc_ref, dsh_ref):
        i = pl.program_id(0)
        dh_v = dh_ref[...].astype(F32) + dh2_ref[...].astype(F32)
        dx_ref[...] = dres_ref[...] + dh_v * (1.0 + sc_ref[...])
        _acc_rows(dsc_ref, dh_v * x_ref[...], i)
        _acc_rows(dsh_ref, dh_v, i)

    return pl.pallas_call(
        body, grid=(L // tr,), in_specs=_row_specs(tr, [D, D, D, D]) + [_vec_spec(D)],
        out_specs=_row_specs(tr, [D]) + [_vec_spec(D)] * 2,
        out_shape=[jax.ShapeDtypeStruct((L, D), F32)] + [jax.ShapeDtypeStruct((1, D), F32)] * 2,
        compiler_params=_cp(("arbitrary",)), name=name)(dres, dh, dh2, xin, scale)


CONV_HALO = 16


def _conv_rows(x_ref, i, tr, L):
    nblk = L // tr
    s = pl.multiple_of(i * tr, CONV_HALO)
    cur = x_ref[pl.ds(s, tr), :].astype(F32)
    sp = pl.multiple_of(jnp.maximum(i * tr - CONV_HALO, 0), CONV_HALO)
    sn = pl.multiple_of(jnp.minimum(i * tr + tr, L - CONV_HALO), CONV_HALO)
    prev = x_ref[pl.ds(sp, CONV_HALO), :].astype(F32) * (i > 0).astype(F32)
    nxt = x_ref[pl.ds(sn, CONV_HALO), :].astype(F32) * (i < nblk - 1).astype(F32)
    return jnp.concatenate([prev, cur, nxt], axis=0)


def _shift_rows(v, j):
    n = v.shape[0]
    return v if j % n == 0 else pltpu.roll(v, j % n, 0)


def _conv_taps(xe):
    return [_shift_rows(xe, CONV_W - 1 - k) for k in range(CONV_W)]


def _conv_eval(taps, w_ref, b_ref):
    c = b_ref[...] + w_ref[0:1, :] * taps[0]
    for k in range(1, CONV_W):
        c = c + w_ref[k:k + 1, :] * taps[k]
    return c


def conv_fwd(zx, col0, conv_w, conv_b):
    L = zx.shape[0]
    C = conv_w.shape[1]
    tc = _tile(C, 512)
    tr = _tile(L, 512, CONV_HALO)
    off = col0 // tc

    def body(x_ref, w_ref, b_ref, o_ref):
        i = pl.program_id(1)
        xe = _conv_rows(x_ref, i, tr, L)
        c = _conv_eval(_conv_taps(xe), w_ref, b_ref)[CONV_HALO:CONV_HALO + tr]
        o_ref[...] = _silu(c).astype(BF16)

    return pl.pallas_call(
        body, grid=(C // tc, L // tr),
        in_specs=[pl.BlockSpec((L, tc), lambda j, i: (0, off + j)), pl.BlockSpec((CONV_W, tc), lambda j, i: (0, j)),
                  pl.BlockSpec((1, tc), lambda j, i: (0, j))],
        out_specs=pl.BlockSpec((tr, tc), lambda j, i: (i, j)),
        out_shape=jax.ShapeDtypeStruct((L, C), BF16), compiler_params=_cp(("parallel", "arbitrary")),
        name="conv_fwd")(zx, conv_w, conv_b)


def conv_bwd(zx, col0, conv_w, conv_b, dxbc):
    L = zx.shape[0]
    C = conv_w.shape[1]
    tc = _tile(C, 512)
    tr = _tile(L, 512, CONV_HALO)
    off = col0 // tc
    H = CONV_HALO

    def body(x_ref, g_ref, w_ref, b_ref, dx_ref, dw_ref, db_ref):
        i = pl.program_id(1)
        xe = _conv_rows(x_ref, i, tr, L)
        ge = _conv_rows(g_ref, i, tr, L)
        taps = _conv_taps(xe)
        dc = ge * _dsilu(_conv_eval(taps, w_ref, b_ref))
        dx = w_ref[CONV_W - 1:CONV_W, :] * dc
        for k in range(CONV_W - 1):
            dx = dx + w_ref[k:k + 1, :] * _shift_rows(dc, -(CONV_W - 1 - k))
        dx_ref[...] = dx[H:H + tr].astype(BF16)
        dcc = dc[H:H + tr]
        rows = [jnp.sum(dcc * taps[k][H:H + tr], axis=0, keepdims=True) for k in range(CONV_W)]
        dwv = jnp.concatenate(rows + [jnp.zeros((8 - CONV_W, tc), F32)], axis=0)
        dbv = jnp.sum(dcc, axis=0, keepdims=True)

        @pl.when(i == 0)
        def _():
            dw_ref[...] = dwv
            db_ref[...] = dbv

        @pl.when(i > 0)
        def _():
            dw_ref[...] += dwv
            db_ref[...] += dbv

    dx, dw, db = pl.pallas_call(
        body, grid=(C // tc, L // tr),
        in_specs=[pl.BlockSpec((L, tc), lambda j, i: (0, off + j)), pl.BlockSpec((L, tc), lambda j, i: (0, j)),
                  pl.BlockSpec((CONV_W, tc), lambda j, i: (0, j)), pl.BlockSpec((1, tc), lambda j, i: (0, j))],
        out_specs=[pl.BlockSpec((tr, tc), lambda j, i: (i, j)), pl.BlockSpec((8, tc), lambda j, i: (0, j)),
                   pl.BlockSpec((1, tc), lambda j, i: (0, j))],
        out_shape=[jax.ShapeDtypeStruct((L, C), BF16), jax.ShapeDtypeStruct((8, C), F32),
                   jax.ShapeDtypeStruct((1, C), F32)],
        compiler_params=_cp(("parallel", "arbitrary")), name="conv_bwd")(zx, dxbc, conv_w, conv_b)
    return dx, dw[:CONV_W], db


_NN = (((1,), (0,)), ((), ()))


def _pieces(x, n):
    out, r = [], x
    for _ in range(n):
        p = r.astype(BF16)
        out.append(p)
        r = r - p.astype(F32)
    return out


def _dot01(a, b01, n, dims=_NN):
    b = b01.astype(BF16)
    return functools.reduce(lambda u, v: u + v,
                            [lax.dot_general(p, b, dims, preferred_element_type=F32) for p in _pieces(a, n)])


def _dot01_left(a01, b, n, dims=_NN):
    a = a01.astype(BF16)
    return functools.reduce(lambda u, v: u + v,
                            [lax.dot_general(a, p, dims, preferred_element_type=F32) for p in _pieces(b, n)])


def _ssd_common(dtp_ref, dtpT_ref, bias_ref, biasT_ref, alog_ref, alogT_ref, b_ref, c_ref):
    Q = SSD_Q
    dt = _softplus(dtp_ref[...] + bias_ref[...])
    A = -jnp.exp(alog_ref[...])
    row = lax.broadcasted_iota(jnp.int32, (Q, Q), 0)
    col = lax.broadcasted_iota(jnp.int32, (Q, Q), 1)
    causal = row >= col
    tril = causal.astype(F32)
    Kh = dt.shape[1]
    acum = _dot01_left(tril, dt * A, 3)
    eye = (lax.broadcasted_iota(jnp.int32, (Kh, Kh), 0) == lax.broadcasted_iota(jnp.int32, (Kh, Kh), 1)).astype(F32)
    acumT = _dot01_left(eye, acum, 3, dims=(((1,), (1,)), ((), ())))
    Bm = b_ref[...]
    Cm = c_ref[...]
    cb = lax.dot_general(Cm, Bm, (((1,), (1,)), ((), ())), preferred_element_type=F32)
    return dt, A, causal, row, col, acum, acumT, Bm, Cm, cb


def _ssd_in_specs(Q, GP, N, Kh, DI, cmap):
    nb0 = DI // N
    vec = pl.BlockSpec((None, 1, Kh), lambda g, c: (g, 0, 0))
    vecT = pl.BlockSpec((None, Kh, 1), lambda g, c: (g, 0, 0))
    return [pl.BlockSpec((Q, GP), lambda g, c: (cmap(c), g)),
            pl.BlockSpec((Q, N), lambda g, c: (cmap(c), nb0 + g)),
            pl.BlockSpec((Q, N), lambda g, c: (cmap(c), nb0 + SSD_G + g)),
            pl.BlockSpec((None, Q, Kh), lambda g, c: (g, cmap(c), 0)),
            pl.BlockSpec((None, Kh, Q), lambda g, c: (g, 0, cmap(c))),
            vec, vecT, vec, vecT, vec, vecT]


def _hi(a, b01):
    return _dot01(a, b01, 2)


def _headsum(a, b01):
    return _dot01(a, b01, 1)


def _ssd_heads(dskT_ref, acum, acumT, dt, Kh):
    Q, P, N = SSD_Q, SSD_P, SSD_N
    GP = Kh * P
    sh_p = P.bit_length() - 1
    seg = lambda shape, dim: lax.shift_right_logical(lax.broadcasted_iota(jnp.int32, shape, dim), sh_p)
    E = (seg((Kh, GP), 1) == lax.broadcasted_iota(jnp.int32, (Kh, GP), 0)).astype(F32)
    ET = (seg((GP, Kh), 0) == lax.broadcasted_iota(jnp.int32, (GP, Kh), 1)).astype(F32)
    a_last = acum[Q - 1:Q, :]
    tail = jnp.exp(a_last - acum)
    eLT = jnp.exp(acumT[:, Q - 1:Q])
    rowseg = seg((GP, N), 0)
    eL_b = jnp.zeros((GP, N), F32)
    for k in range(Kh):
        eL_b = jnp.where(rowseg == k, eLT[k:k + 1, :], eL_b)
    return dict(
        E=E, ET=ET, a_last=a_last, tail=tail, eL_b=eL_b,
        dt_all=_hi(dt, E), ea_all=_hi(jnp.exp(acum), E), tail_all=_hi(tail, E),
        dsk_all=jnp.sum(E * dskT_ref[...], axis=0, keepdims=True))


def _head_chunks(GP):
    CW = min(GP, 128)
    return CW, CW // SSD_P, GP // CW


def _head_mask(Q, CW, kk):
    lane = lax.broadcasted_iota(jnp.int32, (Q, CW), 1)
    return jnp.logical_and(lane >= kk * SSD_P, lane < (kk + 1) * SSD_P)


def ssd_fwd(xbc, dtp_g, dtp_gT, bias_g, bias_gT, alog_g, alog_gT, dsk_g, dsk_gT, DI):
    L = xbc.shape[0]
    Q, P, N, G = SSD_Q, SSD_P, SSD_N, SSD_G
    GP = DI // G
    Kh = GP // P
    nc = L // Q

    CW, hpc, nch = _head_chunks(GP)
    nt = (((1,), (1,)), ((), ()))
    tn = (((0,), (0,)), ((), ()))

    def body(xs_ref, b_ref, c_ref, dtp_ref, dtpT_ref, bias_ref, biasT_ref, alog_ref, alogT_ref, dsk_ref, dskT_ref,
             y_ref, st_ref, state):
        @pl.when(pl.program_id(1) == 0)
        def _():
            state[...] = jnp.zeros(state.shape, F32)

        st_ref[...] = state[...]
        dt, A, causal, row, col, acum, acumT, Bm, Cm, cb = _ssd_common(
            dtp_ref, dtpT_ref, bias_ref, biasT_ref, alog_ref, alogT_ref, b_ref, c_ref)
        hd = _ssd_heads(dskT_ref, acum, acumT, dt, Kh)
        xs = xs_ref[...].astype(F32)
        xdt_all = xs * hd["dt_all"]
        S_all = state[...]
        y_all = (lax.dot_general(Cm, S_all.astype(BF16), nt, preferred_element_type=F32) * hd["ea_all"]
                 + xs * hd["dsk_all"])
        state[...] = S_all * hd["eL_b"] + lax.dot_general(
            (xdt_all * hd["tail_all"]).astype(BF16), Bm, tn, preferred_element_type=F32)
        for ch in range(nch):
            cs = slice(ch * CW, (ch + 1) * CW)
            xc = xdt_all[:, cs]
            acc = y_all[:, cs]
            for kk in range(hpc):
                k = ch * hpc + kk
                decay = jnp.exp(jnp.where(causal, acum[:, k:k + 1] - acumT[k:k + 1, :], -jnp.inf))
                xk = xc if hpc == 1 else jnp.where(_head_mask(Q, CW, kk), xc, 0.0)
                acc = acc + jnp.dot((cb * decay).astype(BF16), xk.astype(BF16), preferred_element_type=F32)
            y_ref[:, cs] = acc.astype(BF16)

    return pl.pallas_call(
        body, grid=(G, nc), in_specs=_ssd_in_specs(Q, GP, N, Kh, DI, lambda c: c),
        out_specs=[pl.BlockSpec((Q, GP), lambda g, c: (c, g)),
                   pl.BlockSpec((None, None, GP, N), lambda g, c: (c, g, 0, 0))],
        out_shape=[jax.ShapeDtypeStruct((L, DI), BF16), jax.ShapeDtypeStruct((nc, G, GP, N), F32)],
        scratch_shapes=[pltpu.VMEM((GP, N), F32)], compiler_params=_cp(("parallel", "arbitrary")),
        name="ssd_fwd")(xbc, xbc, xbc, dtp_g, dtp_gT, bias_g, bias_gT, alog_g, alog_gT, dsk_g, dsk_gT)


def ssd_bwd(xbc, dtp_g, dtp_gT, bias_g, bias_gT, alog_g, alog_gT, dsk_g, dsk_gT, states, dy, DI):
    L = xbc.shape[0]
    Q, P, N, G = SSD_Q, SSD_P, SSD_N, SSD_G
    GP = DI // G
    Kh = GP // P
    nc = L // Q
    rev = lambda c: nc - 1 - c

    CW, hpc, nch = _head_chunks(GP)

    def body(xs_ref, b_ref, c_ref, dtp_ref, dtpT_ref, bias_ref, biasT_ref, alog_ref, alogT_ref, dsk_ref, dskT_ref,
             st_ref, dy_ref, dxs_ref, dB_ref, dC_ref, ddtp_ref, dbias_ref, dalog_ref, dD_ref, dstate):
        ci = pl.program_id(1)

        @pl.when(ci == 0)
        def _():
            dstate[...] = jnp.zeros(dstate.shape, F32)

        dt, A, causal, row, col, acum, acumT, Bm, Cm, cb = _ssd_common(
            dtp_ref, dtpT_ref, bias_ref, biasT_ref, alog_ref, alogT_ref, b_ref, c_ref)
        tn = (((0,), (0,)), ((), ()))
        nt = (((1,), (1,)), ((), ()))
        hd = _ssd_heads(dskT_ref, acum, acumT, dt, Kh)
        ET, tail = hd["ET"], hd["tail"]
        cbT = lax.dot_general(Bm, Cm, nt, preferred_element_type=F32)
        causalT = row <= col
        xs = xs_ref[...].astype(F32)
        xdt_all = xs * hd["dt_all"]
        dyb = dy_ref[...]
        dy_all = dyb.astype(F32)
        S_all = st_ref[...]
        S_b = S_all.astype(BF16)
        dS_all = dstate[...]
        dS_b = dS_all.astype(BF16)
        CS_all = lax.dot_general(Cm, S_b, nt, preferred_element_type=F32)
        dyE_b = (dy_all * hd["ea_all"]).astype(BF16)
        dC_acc = jnp.dot(dyE_b, S_b, preferred_element_type=F32)
        dS_y = lax.dot_general(dyE_b, Cm, tn, preferred_element_type=F32)
        BdS_all = lax.dot_general(Bm, dS_b, nt, preferred_element_type=F32)
        dB_acc = jnp.dot((xdt_all * hd["tail_all"]).astype(BF16), dS_b, preferred_element_type=F32)
        dtail = _headsum(xdt_all * BdS_all, ET)
        da_cols = _headsum(dy_all * CS_all * hd["ea_all"], ET) - dtail * tail
        dss = _dot01_left(jnp.ones((8, N), F32), _dot01_left(hd["E"], dS_all * S_all, 2), 2, dims=nt)
        da_last = dss[0:1] * jnp.exp(hd["a_last"]) + jnp.sum(dtail * tail, axis=0, keepdims=True)
        rowi = lax.broadcasted_iota(jnp.int32, (Q, Kh), 0)
        da_cols = da_cols + jnp.where(rowi == Q - 1, da_last, 0.0)
        dstate[...] = hd["eL_b"] * dS_all + dS_y
        sum_mg = jnp.zeros((Q, Q), F32)
        ddt_x = jnp.zeros((Q, Kh), F32)
        da_rows = jnp.zeros((Kh, Q), F32)
        lane_k = lax.broadcasted_iota(jnp.int32, (Q, Kh), 1)
        sub_k = lax.broadcasted_iota(jnp.int32, (Kh, Q), 0)
        for ch in range(nch):
            cs = slice(ch * CW, (ch + 1) * CW)
            dyc = dyb[:, cs]
            xc_b = xdt_all[:, cs].astype(BF16)
            acc = hd["tail_all"][:, cs] * BdS_all[:, cs]
            for kk in range(hpc):
                k = ch * hpc + kk
                a_b = jnp.broadcast_to(acum[:, k:k + 1], (Q, Q))
                a_r = acumT[k:k + 1, :]
                decay = jnp.exp(jnp.where(causal, a_b - a_r, -jnp.inf))
                decayT = jnp.exp(jnp.where(causalT, a_r - a_b, -jnp.inf))
                dyk = dyc if hpc == 1 else jnp.where(_head_mask(Q, CW, kk), dyc, jnp.zeros_like(dyc))
                mg = decay * lax.dot_general(dyk, xc_b, nt, preferred_element_type=F32)
                sum_mg = sum_mg + mg
                w = mg * cb
                da_cols = da_cols + jnp.where(lane_k == k, jnp.sum(w, axis=1, keepdims=True), 0.0)
                da_rows = da_rows + jnp.where(sub_k == k, jnp.sum(w, axis=0, keepdims=True), 0.0)
                acc = acc + jnp.dot((decayT * cbT).astype(BF16), dyk, preferred_element_type=F32)
            dxs_ref[:, cs] = (acc * hd["dt_all"][:, cs] + dy_all[:, cs] * hd["dsk_all"][:, cs]).astype(BF16)
            ddt_x = ddt_x + _headsum(acc * xs[:, cs], ET[cs, :])
        eye_q = (row == col).astype(F32)
        da_cols = da_cols - _dot01_left(eye_q, da_rows, 3, dims=nt)
        dD_row = jnp.sum(_headsum(dy_all * xs, ET), axis=0, keepdims=True)
        sum_mg_b = sum_mg.astype(BF16)
        dB_ref[...] = (dB_acc + lax.dot_general(sum_mg_b, Cm, tn, preferred_element_type=F32)).astype(BF16)
        dC_ref[...] = (dC_acc + jnp.dot(sum_mg_b, Bm, preferred_element_type=F32)).astype(BF16)
        triu = (row <= col).astype(F32)
        ddtA = _dot01_left(triu, da_cols, 3)
        ddt = ddt_x + ddtA * A
        dpre = ddt * _sigmoid(dtp_ref[...] + bias_ref[...])
        ddtp_ref[...] = dpre
        dbias_v = jnp.sum(dpre, axis=0, keepdims=True)
        dalog_v = jnp.sum(ddtA * dt, axis=0, keepdims=True) * A

        @pl.when(ci == 0)
        def _():
            dbias_ref[...] = dbias_v
            dalog_ref[...] = dalog_v
            dD_ref[...] = dD_row

        @pl.when(ci > 0)
        def _():
            dbias_ref[...] += dbias_v
            dalog_ref[...] += dalog_v
            dD_ref[...] += dD_row

    vec_o = pl.BlockSpec((None, 1, Kh), lambda g, c: (g, 0, 0))
    return pl.pallas_call(
        body, grid=(G, nc),
        in_specs=_ssd_in_specs(Q, GP, N, Kh, DI, rev)
        + [pl.BlockSpec((None, None, GP, N), lambda g, c: (rev(c), g, 0, 0)),
           pl.BlockSpec((Q, GP), lambda g, c: (rev(c), g))],
        out_specs=[pl.BlockSpec((Q, GP), lambda g, c: (rev(c), g)), pl.BlockSpec((Q, N), lambda g, c: (rev(c), g)),
                   pl.BlockSpec((Q, N), lambda g, c: (rev(c), g)),
                   pl.BlockSpec((None, Q, Kh), lambda g, c: (g, rev(c), 0)), vec_o, vec_o, vec_o],
        out_shape=[jax.ShapeDtypeStruct((L, DI), BF16), jax.ShapeDtypeStruct((L, G * N), BF16),
                   jax.ShapeDtypeStruct((L, G * N), BF16), jax.ShapeDtypeStruct((G, L, Kh), F32)]
        + [jax.ShapeDtypeStruct((G, 1, Kh), F32)] * 3,
        scratch_shapes=[pltpu.VMEM((GP, N), F32)], compiler_params=_cp(("parallel", "arbitrary")),
        name="ssd_bwd")(xbc, xbc, xbc, dtp_g, dtp_gT, bias_g, bias_gT, alog_g, alog_gT, dsk_g, dsk_gT, states, dy)


def _rms_groups(y2, ng_ref, DI):
    S = DI // SSD_G
    for g in range(SSD_G):
        gs = slice(g * S, (g + 1) * S)
        seg = y2[:, gs]
        r = lax.rsqrt(jnp.mean(seg * seg, axis=-1, keepdims=True) + RMS_EPS)
        yield gs, seg * r, r, ng_ref[:, gs]


def rms_gate_fwd(y, zx, norm_g):
    L, DI = y.shape
    tr = _tile(L, 256, 16)

    def body(y_ref, z_ref, ng_ref, o_ref):
        y2 = y_ref[...].astype(F32) * _silu(z_ref[...].astype(F32))
        for gs, yh, _, ng in _rms_groups(y2, ng_ref, DI):
            o_ref[:, gs] = (yh * ng).astype(BF16)

    return pl.pallas_call(
        body, grid=(L // tr,), in_specs=_row_specs(tr, [DI, DI]) + [_vec_spec(DI)], out_specs=_row_specs(tr, [DI])[0],
        out_shape=jax.ShapeDtypeStruct((L, DI), BF16), compiler_params=_cp(("parallel",)),
        name="rms_gate_fwd")(y, zx, norm_g)


def rms_gate_bwd(dyn, y, zx, norm_g):
    L, DI = y.shape
    tr = _tile(L, 256, 16)

    def body(dyn_ref, y_ref, z_ref, ng_ref, dy_ref, dz_ref, dng_ref):
        i = pl.program_id(0)
        yv = y_ref[...].astype(F32)
        zv = z_ref[...].astype(F32)
        sz = _silu(zv)
        dsz = _dsilu(zv)
        dynv = dyn_ref[...].astype(F32)
        for gs, yh, r, ng in _rms_groups(yv * sz, ng_ref, DI):
            dyh = dynv[:, gs] * ng
            dy2 = r * (dyh - yh * jnp.mean(dyh * yh, axis=-1, keepdims=True))
            dy_ref[:, gs] = (dy2 * sz[:, gs]).astype(BF16)
            dz_ref[:, gs] = (dy2 * yv[:, gs] * dsz[:, gs]).astype(BF16)
            s = jnp.sum(dynv[:, gs] * yh, axis=0, keepdims=True)

            @pl.when(i == 0)
            def _():
                dng_ref[:, gs] = s

            @pl.when(i > 0)
            def _():
                dng_ref[:, gs] += s

    return pl.pallas_call(
        body, grid=(L // tr,), in_specs=_row_specs(tr, [DI, DI, DI]) + [_vec_spec(DI)],
        out_specs=_row_specs(tr, [DI, DI]) + [_vec_spec(DI)],
        out_shape=[jax.ShapeDtypeStruct((L, DI), BF16)] * 2 + [jax.ShapeDtypeStruct((1, DI), F32)],
        compiler_params=_cp(("arbitrary",)), name="rms_gate_bwd")(dyn, y, zx, norm_g)


def _alibi_slope(gi, h):
    n = len(DIL_PATTERNS) * DIL_H
    return float(2.0 ** (-8.0 * (gi * DIL_H + h + 1) / n))


def _attn_masks():
    qi = lax.broadcasted_iota(jnp.int32, (DIL_BLK, DIL_BLK), 0)
    kj = lax.broadcasted_iota(jnp.int32, (DIL_BLK, DIL_BLK), 1)
    dcur = (qi - kj).astype(F32)
    return dcur, qi >= kj, dcur + float(DIL_BLK), kj >= qi


def attn_fwd(q3, kv3, gi):
    window, d = DIL_PATTERNS[gi]
    assert window // d == DIL_BLK
    HW = DIL_H * DIL_E
    M = q3.shape[1]
    nb = M // DIL_BLK
    scale = DIL_E ** -0.5
    nt = (((1,), (1,)), ((), ()))

    def body(q_ref, kp_ref, kc_ref, vp_ref, vc_ref, o_ref, lse_ref):
        n = pl.program_id(1)
        dcur, vcur, dprev, vprev0 = _attn_masks()
        dist = jnp.concatenate([dprev, dcur], axis=1)
        valid = jnp.concatenate([jnp.logical_and(vprev0, n > 0), vcur], axis=1)
        lane = lax.broadcasted_iota(jnp.int32, (DIL_BLK, 128), 1)
        lse_acc = jnp.zeros((DIL_BLK, 128), F32)
        for h in range(DIL_H):
            hs = slice(h * DIL_E, (h + 1) * DIL_E)
            sl = _alibi_slope(gi, h) * d
            kcat = jnp.concatenate([kp_ref[:, hs], kc_ref[:, hs]], axis=0)
            vcat = jnp.concatenate([vp_ref[:, hs], vc_ref[:, hs]], axis=0)
            s = lax.dot_general(q_ref[:, hs], kcat, nt, preferred_element_type=F32) * scale - sl * dist
            s = jnp.where(valid, s, -jnp.inf)
            m = jnp.max(s, axis=-1, keepdims=True)
            p = jnp.exp(s - m)
            den = jnp.sum(p, axis=-1, keepdims=True)
            o = jnp.dot(p.astype(BF16), vcat, preferred_element_type=F32) / den
            o_ref[:, hs] = o.astype(BF16)
            lse_acc = jnp.where(lane == h, m + jnp.log(den), lse_acc)
        lse_ref[...] = lse_acc

    blk = (None, DIL_BLK, HW)
    prev = lambda n: jnp.maximum(n - 1, 0)
    return pl.pallas_call(
        body, grid=(d, nb),
        in_specs=[pl.BlockSpec(blk, lambda r, n: (r, n, 0)),
                  pl.BlockSpec(blk, lambda r, n: (r, prev(n), 0)), pl.BlockSpec(blk, lambda r, n: (r, n, 0)),
                  pl.BlockSpec(blk, lambda r, n: (r, prev(n), 1)), pl.BlockSpec(blk, lambda r, n: (r, n, 1))],
        out_specs=[pl.BlockSpec(blk, lambda r, n: (r, n, 0)), pl.BlockSpec((None, DIL_BLK, 128), lambda r, n: (r, n, 0))],
        out_shape=[jax.ShapeDtypeStruct((d, M, HW), BF16), jax.ShapeDtypeStruct((d, M, 128), F32)],
        compiler_params=_cp(("parallel", "parallel")), name=f"attn_fwd_{gi}")(q3, kv3, kv3, kv3, kv3)


def attn_bwd(q3, kv3, do3, lse3, dpr3, gi):
    window, d = DIL_PATTERNS[gi]
    HW = DIL_H * DIL_E
    M = q3.shape[1]
    L = M * d
    nb = M // DIL_BLK
    scale = DIL_E ** -0.5
    nt = (((1,), (1,)), ((), ()))
    tn = (((0,), (0,)), ((), ()))

    def body(q0_ref, q1_ref, k_ref, v_ref, do0_ref, do1_ref, l0_ref, l1_ref, r0_ref, r1_ref,
             dq_ref, dk_ref, dv_ref, carry):
        n = pl.program_id(1)

        @pl.when(n == 0)
        def _():
            carry[...] = jnp.zeros(carry.shape, F32)

        dcur, vcur, dprev, vprev0 = _attn_masks()
        dist = jnp.concatenate([dcur, dprev], axis=0)
        valid = jnp.concatenate([vcur, jnp.logical_and(vprev0, n < nb - 1)], axis=0)
        B = DIL_BLK
        for h in range(DIL_H):
            hs = slice(h * DIL_E, (h + 1) * DIL_E)
            sl = _alibi_slope(gi, h) * d
            kh = k_ref[:, hs]
            vh = v_ref[:, hs]
            qcat = jnp.concatenate([q0_ref[:, hs], q1_ref[:, hs]], axis=0)
            docat = jnp.concatenate([do0_ref[:, hs], do1_ref[:, hs]], axis=0)
            lcat = jnp.concatenate([l0_ref[:, h:h + 1], l1_ref[:, h:h + 1]], axis=0)
            rcat = jnp.concatenate([r0_ref[:, h:h + 1], r1_ref[:, h:h + 1]], axis=0)
            s = lax.dot_general(qcat, kh, nt, preferred_element_type=F32) * scale - sl * dist
            p = jnp.exp(jnp.where(valid, s - lcat, -jnp.inf))
            ds = p * (lax.dot_general(docat, vh, nt, preferred_element_type=F32) - rcat)
            ds_b = (ds * scale).astype(BF16)
            dv_ref[:, hs] = lax.dot_general(p.astype(BF16), docat, tn, preferred_element_type=F32).astype(BF16)
            dk_ref[:, hs] = lax.dot_general(ds_b, qcat, tn, preferred_element_type=F32).astype(BF16)
            dqc = jnp.dot(ds_b, kh, preferred_element_type=F32)
            dq_ref[:, hs] = (carry[:, hs] + dqc[:B]).astype(BF16)
            carry[:, hs] = dqc[B:]

    blk = (None, DIL_BLK, HW)
    sblk = (None, DIL_BLK, 128)
    oblk = (DIL_BLK, HW)
    nxt = lambda n: jnp.minimum(n + 1, nb - 1)
    here = lambda c: (lambda r, n: (r, n, c))
    ahead = lambda c: (lambda r, n: (r, nxt(n), c))
    outs = pl.pallas_call(
        body, grid=(d, nb),
        in_specs=[pl.BlockSpec(blk, here(0)), pl.BlockSpec(blk, ahead(0)),
                  pl.BlockSpec(blk, here(0)), pl.BlockSpec(blk, here(1)),
                  pl.BlockSpec(blk, here(0)), pl.BlockSpec(blk, ahead(0)),
                  pl.BlockSpec(sblk, here(0)), pl.BlockSpec(sblk, ahead(0)),
                  pl.BlockSpec(sblk, here(0)), pl.BlockSpec(sblk, ahead(0))],
        out_specs=[pl.BlockSpec(oblk, lambda r, n: (n, r))] * 3,
        out_shape=[jax.ShapeDtypeStruct((M, d * HW), BF16)] * 3,
        scratch_shapes=[pltpu.VMEM(oblk, F32)], compiler_params=_cp(("parallel", "arbitrary")),
        name=f"attn_bwd_{gi}")(q3, q3, kv3, kv3, do3, do3, lse3, lse3, dpr3, dpr3)
    return [t.reshape(L, HW) for t in outs]


def _merge_weights(l_tiles, h):
    ls = [t[:, h:h + 1] for t in l_tiles]
    mx = functools.reduce(jnp.maximum, ls)
    es = [jnp.exp(l - mx) for l in ls]
    den = functools.reduce(lambda a, b: a + b, es)
    return [e / den for e in es]


def _dil_specs(tr, arrs):
    return [pl.BlockSpec((a.shape[0], tr // a.shape[0], a.shape[2]), lambda i: (0, i, 0)) for a in arrs]


def _dil_scratch(tr, arrs):
    return [pltpu.VMEM((a.shape[2] // 128, tr, 128), F32) for a in arrs if a.shape[0] > 1]


def _undilate(refs3, scrs, tr):
    out, k = [], 0
    for ref in refs3:
        d, _, W = ref.shape
        if d == 1:
            out.append(lambda c, ref=ref: ref[0, :, c * 128:(c + 1) * 128])
            continue
        scr = scrs[k]
        k += 1
        for r in range(d):
            for c in range(W // 128):
                scr.at[c][pl.ds(r, tr // d, stride=d), :] = ref[r, :, c * 128:(c + 1) * 128].astype(F32)
        out.append(lambda c, scr=scr: scr[c])
    return out


def merge_fwd(os3, lses3, z):
    HW = os3[0].shape[2]
    L = os3[0].shape[0] * os3[0].shape[1]
    tr = _tile(L, 256, 16)
    ng = len(os3)
    n_scr = len(_dil_scratch(tr, os3))

    def body(*refs):
        z_ref, out_ref = refs[2 * ng], refs[2 * ng + 1]
        scrs = refs[2 * ng + 2:]
        o_get = _undilate(refs[:ng], scrs[:n_scr], tr)
        l_tiles = [g(0) for g in _undilate(refs[ng:2 * ng], scrs[n_scr:], tr)]
        for h in range(DIL_H):
            hs = slice(h * DIL_E, (h + 1) * DIL_E)
            ws = _merge_weights(l_tiles, h)
            om = functools.reduce(lambda a, b: a + b, [w * o(h).astype(F32) for w, o in zip(ws, o_get)])
            out_ref[:, hs] = (om * _silu(z_ref[:, hs].astype(F32))).astype(BF16)

    return pl.pallas_call(
        body, grid=(L // tr,),
        in_specs=_dil_specs(tr, os3) + _dil_specs(tr, lses3) + _row_specs(tr, [HW]),
        out_specs=_row_specs(tr, [HW])[0], out_shape=jax.ShapeDtypeStruct((L, HW), BF16),
        scratch_shapes=_dil_scratch(tr, os3) + _dil_scratch(tr, lses3),
        compiler_params=_cp(("parallel",)), name="merge_fwd")(*os3, *lses3, z)


def merge_bwd(dgated, os3, lses3, z):
    HW = os3[0].shape[2]
    L = os3[0].shape[0] * os3[0].shape[1]
    tr = _tile(L, 256, 16)
    ng = len(os3)
    n_scr = len(_dil_scratch(tr, os3))

    def body(*refs):
        dg_ref = refs[0]
        z_ref = refs[1 + 2 * ng]
        outs = refs[2 + 2 * ng:2 + 2 * ng + 2 * ng + 1]
        scrs = refs[2 + 2 * ng + 2 * ng + 1:]
        do_out, dpr_out, dz_ref = outs[:ng], outs[ng:2 * ng], outs[2 * ng]
        o_get = _undilate(refs[1:1 + ng], scrs[:n_scr], tr)
        l_tiles = [g(0) for g in _undilate(refs[1 + ng:1 + 2 * ng], scrs[n_scr:2 * n_scr], tr)]
        stage = scrs[2 * n_scr:]
        do_stage, dpr_stage, k = [], [], 0
        for g in range(ng):
            if do_out[g].shape[0] == 1:
                do_stage.append(None)
                dpr_stage.append(None)
            else:
                do_stage.append(stage[2 * k])
                dpr_stage.append(stage[2 * k + 1])
                k += 1
        lane = lax.broadcasted_iota(jnp.int32, (tr, 128), 1)
        accs = [jnp.zeros((tr, 128), F32) for _ in range(ng)]
        for h in range(DIL_H):
            hs = slice(h * DIL_E, (h + 1) * DIL_E)
            ws = _merge_weights(l_tiles, h)
            ov = [o(h).astype(F32) for o in o_get]
            om = functools.reduce(lambda a, b: a + b, [w * o for w, o in zip(ws, ov)])
            zv = z_ref[:, hs].astype(F32)
            dgv = dg_ref[:, hs].astype(F32)
            dom = dgv * _silu(zv)
            dz_ref[:, hs] = (dgv * om * _dsilu(zv)).astype(BF16)
            dws = [jnp.sum(dom * o, axis=-1, keepdims=True) for o in ov]
            dwbar = functools.reduce(lambda a, b: a + b, [w * dw for w, dw in zip(ws, dws)])
            for g in range(ng):
                if do_stage[g] is None:
                    do_out[g][0, :, hs] = (ws[g] * dom).astype(BF16)
                else:
                    do_stage[g][h] = ws[g] * dom
                accs[g] = jnp.where(lane == h, ws[g] * dwbar, accs[g])
        for g in range(ng):
            d = do_out[g].shape[0]
            if d == 1:
                dpr_out[g][0] = accs[g]
                continue
            dpr_stage[g][0] = accs[g]
            for r in range(d):
                dpr_out[g][r] = dpr_stage[g].at[0][pl.ds(r, tr // d, stride=d), :]
                for c in range(HW // 128):
                    do_out[g][r, :, c * 128:(c + 1) * 128] = do_stage[g].at[c][pl.ds(r, tr // d, stride=d), :].astype(BF16)

    stage_shapes = []
    for o3 in os3:
        if o3.shape[0] > 1:
            stage_shapes += [pltpu.VMEM((HW // 128, tr, 128), F32), pltpu.VMEM((1, tr, 128), F32)]
    outs = pl.pallas_call(
        body, grid=(L // tr,),
        in_specs=_row_specs(tr, [HW]) + _dil_specs(tr, os3) + _dil_specs(tr, lses3) + _row_specs(tr, [HW]),
        out_specs=_dil_specs(tr, os3) + _dil_specs(tr, lses3) + _row_specs(tr, [HW]),
        out_shape=[jax.ShapeDtypeStruct(o.shape, BF16) for o in os3] + [jax.ShapeDtypeStruct(l.shape, F32) for l in lses3]
        + [jax.ShapeDtypeStruct((L, HW), BF16)],
        scratch_shapes=_dil_scratch(tr, os3) + _dil_scratch(tr, lses3) + stage_shapes,
        compiler_params=_cp(("parallel",)), name="merge_bwd")(dgated, *os3, *lses3, z)
    return outs[:ng], outs[ng:2 * ng], outs[2 * ng]


def ada_fwd(c8, ada_w):
    nl, D, Ws = ada_w.shape
    tn = _tile(Ws, 512)

    def body(c_ref, w_ref, o_ref):
        o_ref[...] = jnp.dot(_silu(c_ref[...]), w_ref[...], precision=lax.Precision.HIGHEST,
                             preferred_element_type=F32)

    return pl.pallas_call(
        body, grid=(nl, Ws // tn),
        in_specs=[pl.BlockSpec((N_DEV, D), lambda l, j: (0, 0)), pl.BlockSpec((None, D, tn), lambda l, j: (l, 0, j))],
        out_specs=pl.BlockSpec((None, N_DEV, tn), lambda l, j: (l, 0, j)),
        out_shape=jax.ShapeDtypeStruct((nl, N_DEV, Ws), F32), compiler_params=_cp(("parallel", "parallel")),
        name="ada_fwd")(c8, ada_w)


def ada_wgrad(c8t, dmod):
    nl, _, Ws = dmod.shape
    D = c8t.shape[0]
    tm = _tile(D, 512, 8)

    def body(c_ref, d_ref, o_ref):
        sc = _silu(c_ref[...])
        acc = sc[:, 0:1] * d_ref[0:1, :]
        for e in range(1, N_DEV):
            acc = acc + sc[:, e:e + 1] * d_ref[e:e + 1, :]
        o_ref[...] = acc

    return pl.pallas_call(
        body, grid=(nl, D // tm),
        in_specs=[pl.BlockSpec((tm, N_DEV), lambda l, i: (i, 0)), pl.BlockSpec((None, N_DEV, Ws), lambda l, i: (l, 0, 0))],
        out_specs=pl.BlockSpec((None, tm, Ws), lambda l, i: (l, i, 0)),
        out_shape=jax.ShapeDtypeStruct((nl, D, Ws), F32), compiler_params=_cp(("parallel", "parallel")),
        name="ada_wgrad")(c8t, dmod)


def adamw(w, g, m, v, name):
    R, C = w.shape
    tr = _tile(R, 256, 8)
    c1 = 1.0 - ADAM_B1 ** ADAM_STEP
    c2 = 1.0 - ADAM_B2 ** ADAM_STEP

    def body(w_ref, g_ref, m_ref, v_ref, d_ref, nm_ref, nv_ref):
        gv = g_ref[...]
        nm = ADAM_B1 * m_ref[...] + (1.0 - ADAM_B1) * gv
        nv = ADAM_B2 * v_ref[...] + (1.0 - ADAM_B2) * (gv * gv)
        nm_ref[...] = nm
        nv_ref[...] = nv
        d_ref[...] = -ADAM_LR * ((nm / c1) / (jnp.sqrt(nv / c2) + ADAM_EPS) + ADAM_WD * w_ref[...])

    return pl.pallas_call(
        body, grid=(R // tr,), in_specs=_row_specs(tr, [C] * 4), out_specs=_row_specs(tr, [C] * 3),
        out_shape=[jax.ShapeDtypeStruct((R, C), F32)] * 3, compiler_params=_cp(("parallel",)), name=name)(w, g, m, v)


def sum_leading(t, name, out_dtype=F32):
    S, R, C = t.shape
    tr = _tile(R, 256, 16)

    def body(t_ref, o_ref):
        acc = t_ref[0].astype(F32)
        for s in range(1, S):
            acc = acc + t_ref[s].astype(F32)
        o_ref[...] = acc.astype(out_dtype)

    return pl.pallas_call(
        body, grid=(R // tr,), in_specs=[pl.BlockSpec((S, tr, C), lambda i: (0, i, 0))],
        out_specs=pl.BlockSpec((tr, C), lambda i: (i, 0)), out_shape=jax.ShapeDtypeStruct((R, C), out_dtype),
        compiler_params=_cp(("parallel",)), name=name)(t)


def add_half(g, a, core, name, by_cols=False):
    S, R, C = g.shape

    def body(core_ref, g_ref, a_ref, o_ref):
        o_ref[...] = (g_ref[...].astype(F32) + a_ref[...].astype(F32)).astype(BF16)

    if by_cols:
        hc = C // 2
        tr = _tile(R, 256, 16)
        return pl.pallas_call(
            body,
            grid_spec=pltpu.PrefetchScalarGridSpec(
                num_scalar_prefetch=1, grid=(S, R // tr),
                in_specs=[pl.BlockSpec((None, tr, hc), lambda s, i, core_ref: (s, i, core_ref[0])),
                          pl.BlockSpec((None, tr, hc), lambda s, i, core_ref: (s, i, 0))],
                out_specs=pl.BlockSpec((None, tr, hc), lambda s, i, core_ref: (s, i, 0))),
            out_shape=jax.ShapeDtypeStruct((S, R, hc), BF16), compiler_params=_cp(("parallel", "parallel")),
            name=name)(core, g, a)
    h = R // 2
    tr = _tile(h, 256, 16)
    nb = h // tr

    return pl.pallas_call(
        body,
        grid_spec=pltpu.PrefetchScalarGridSpec(
            num_scalar_prefetch=1, grid=(S, nb),
            in_specs=[pl.BlockSpec((None, tr, C), lambda s, i, core_ref: (s, core_ref[0] * nb + i, 0)),
                      pl.BlockSpec((None, tr, C), lambda s, i, core_ref: (s, i, 0))],
            out_specs=pl.BlockSpec((None, tr, C), lambda s, i, core_ref: (s, i, 0))),
        out_shape=jax.ShapeDtypeStruct((S, h, C), BF16), compiler_params=_cp(("parallel", "parallel")),
        name=name)(core, g, a)


def sum_partials(own, landed, chip, name):
    _, h, C = own.shape
    tr = _tile(h, 256, 16)

    def body(chip_ref, own_ref, l_ref, o_ref):
        acc = own_ref[...].astype(F32)
        for j in range(3):
            acc = acc + l_ref[j].astype(F32)
        o_ref[...] = acc

    return pl.pallas_call(
        body,
        grid_spec=pltpu.PrefetchScalarGridSpec(
            num_scalar_prefetch=1, grid=(h // tr,),
            in_specs=[pl.BlockSpec((None, tr, C), lambda i, chip_ref: (chip_ref[0], i, 0)),
                      pl.BlockSpec((3, tr, C), lambda i, chip_ref: (0, i, 0))],
            out_specs=pl.BlockSpec((tr, C), lambda i, chip_ref: (i, 0))),
        out_shape=jax.ShapeDtypeStruct((h, C), F32), compiler_params=_cp(("parallel",)), name=name)(chip, own, landed)


def adamw_halves(w, g_mine, g_theirs, m, v, core, name):
    R, C = w.shape
    h = R // 2
    tr = _tile(h, 256, 8)
    nbh = h // tr
    c1 = 1.0 - ADAM_B1 ** ADAM_STEP
    c2 = 1.0 - ADAM_B2 ** ADAM_STEP

    def body(core_ref, w_ref, gm_ref, gt_ref, m_ref, v_ref, g_ref, d_ref, nm_ref, nv_ref):
        mine = (pl.program_id(0) // nbh) == core_ref[0]
        gv = jnp.where(mine, gm_ref[...], gt_ref[...])
        g_ref[...] = gv
        nm = ADAM_B1 * m_ref[...] + (1.0 - ADAM_B1) * gv
        nv = ADAM_B2 * v_ref[...] + (1.0 - ADAM_B2) * (gv * gv)
        nm_ref[...] = nm
        nv_ref[...] = nv
        d_ref[...] = -ADAM_LR * ((nm / c1) / (jnp.sqrt(nv / c2) + ADAM_EPS) + ADAM_WD * w_ref[...])

    full = pl.BlockSpec((tr, C), lambda i, core_ref: (i, 0))
    halfspec = pl.BlockSpec((tr, C), lambda i, core_ref: (i % nbh, 0))
    return pl.pallas_call(
        body,
        grid_spec=pltpu.PrefetchScalarGridSpec(
            num_scalar_prefetch=1, grid=(2 * nbh,), in_specs=[full, halfspec, halfspec, full, full],
            out_specs=[full] * 4),
        out_shape=[jax.ShapeDtypeStruct((R, C), F32)] * 4, compiler_params=_cp(("parallel",)),
        name=name)(core, w, g_mine, g_theirs, m, v)


_ANY = pl.BlockSpec(memory_space=pl.ANY)


def _place():
    x, y, c = lax.axis_index("x"), lax.axis_index("y"), lax.axis_index("c")
    chips = [(1 - x, y), (x, 1 - y), (1 - x, 1 - y)]
    return x, y, c, chips


def allgather_small(v, name, after=None):
    R, W = v.shape
    extra = [] if after is None else [after]

    def body(x_ref, *rest):
        out_ref, send_sems, recv_sems, local_sem = rest[len(extra):]
        x, y, c, chips = _place()
        me, sibling = (x, y, c), (x, y, 1 - c)

        def rows(px, py, pc):
            return out_ref.at[pl.ds((4 * px + 2 * py + pc) * R, R), :]

        def copy(k, block, to, src=None):
            return pltpu.make_async_remote_copy(
                src_ref=rows(*block) if src is None else src, dst_ref=rows(*block),
                send_sem=send_sems.at[k], recv_sem=recv_sems.at[k], device_id=to, device_id_type=MESH)

        mine = pltpu.make_async_copy(x_ref, rows(*me), local_sem)
        mine.start()
        first = [copy(0, me, sibling, src=x_ref)]
        first += [copy(1 + j, me, (*chip, c), src=x_ref) for j, chip in enumerate(chips)]
        for cp in first:
            cp.start()
        passed = [copy(4 + j, (*chip, c), sibling) for j, chip in enumerate(chips)]
        for j, chip in enumerate(chips):
            copy(1 + j, (*chip, c), me).wait_recv()
            passed[j].start()
        copy(0, sibling, me).wait_recv()
        for j, chip in enumerate(chips):
            copy(4 + j, (*chip, 1 - c), me).wait_recv()
        for cp in first + passed:
            cp.wait_send()
        mine.wait()

    return pl.pallas_call(
        body, out_shape=jax.ShapeDtypeStruct((N_DEV * R, W), v.dtype),
        in_specs=[pl.BlockSpec(memory_space=pltpu.VMEM)] + [_ANY] * len(extra),
        out_specs=pl.BlockSpec(memory_space=pltpu.VMEM),
        scratch_shapes=[pltpu.SemaphoreType.DMA((7,)), pltpu.SemaphoreType.DMA((7,)), pltpu.SemaphoreType.DMA],
        name=name)(v, *extra)


def allgather_routed(shard, name):
    R, C = shard.shape
    hc = C // 2
    ra = (R // 2) // 16 * 16

    def body(in_ref, out_ref, send_sems, recv_sems):
        x, y, c, _ = _place()
        xn, yn = (1 - x, y, c), (x, 1 - y, c)
        sibling = (x, y, 1 - c)
        p, pxn, pyn, pdg = 2 * x + y, 2 * (1 - x) + y, 2 * x + (1 - y), 2 * (1 - x) + (1 - y)
        rows_a, rows_b, rows_all = pl.ds(0, ra), pl.ds(ra, R - ra), pl.ds(0, R)

        def win(ref, rows, core):
            return ref.at[rows, pl.ds(pl.multiple_of(core * hc, 128), hc)]

        def copy(k, chip_id, rows, core, to, src=None):
            blk = win(out_ref.at[chip_id], rows, core)
            return pltpu.make_async_remote_copy(
                src_ref=blk if src is None else src, dst_ref=blk, send_sem=send_sems.at[k], recv_sem=recv_sems.at[k],
                device_id=to, device_id_type=MESH)

        own = [copy(0, p, rows_a, c, xn, src=win(in_ref, rows_a, c)), copy(1, p, rows_b, c, xn, src=win(in_ref, rows_b, c)),
               copy(2, p, rows_b, c, yn, src=win(in_ref, rows_b, c)), copy(3, p, rows_a, c, yn, src=win(in_ref, rows_a, c))]
        for cp in own:
            cp.start()
        copy(0, pxn, rows_a, c, xn).wait_recv()
        fwd_a = copy(4, pxn, rows_a, c, yn)
        fwd_a.start()
        copy(2, pyn, rows_b, c, yn).wait_recv()
        fwd_b = copy(5, pyn, rows_b, c, xn)
        fwd_b.start()
        copy(1, pxn, rows_b, c, xn).wait_recv()
        copy(3, pyn, rows_a, c, yn).wait_recv()
        passed = [copy(6, pxn, rows_all, c, sibling), copy(7, pyn, rows_all, c, sibling)]
        for cp in passed:
            cp.start()
        copy(4, pdg, rows_a, c, yn).wait_recv()
        passed.append(copy(8, pdg, rows_a, c, sibling))
        passed[-1].start()
        copy(5, pdg, rows_b, c, xn).wait_recv()
        passed.append(copy(9, pdg, rows_b, c, sibling))
        passed[-1].start()
        for k, (chip_id, rows) in enumerate([(pxn, rows_all), (pyn, rows_all), (pdg, rows_a), (pdg, rows_b)]):
            copy(6 + k, chip_id, rows, 1 - c, sibling).wait_recv()
        for cp in own + [fwd_a, fwd_b] + passed:
            cp.wait_send()

    out = pl.pallas_call(
        body, out_shape=jax.ShapeDtypeStruct((N_CHIPS, R, C), shard.dtype), in_specs=[_ANY], out_specs=_ANY,
        scratch_shapes=[pltpu.SemaphoreType.DMA((10,)), pltpu.SemaphoreType.DMA((10,))], name=name)(shard)
    chip = 2 * lax.axis_index("x") + lax.axis_index("y")
    return lax.dynamic_update_index_in_dim(out, shard, chip, 0)


_HBM = pl.BlockSpec(memory_space=pltpu.HBM)
_SEM = pl.BlockSpec(memory_space=pltpu.SEMAPHORE)
_EFFECT = pltpu.SideEffectType.DATAFLOW_SIDE_EFFECTING


def _chip_copies(kind, srcs, lands, send_sems, recv_sems):
    x, y, c, chips = _place()
    p = 2 * x + y
    cps = []
    if kind == "sibling":
        for i in range(len(srcs)):
            h = srcs[i].shape[1] // 2
            cps.append(pltpu.make_async_remote_copy(
                src_ref=srcs[i].at[:, pl.ds((1 - c) * h, h), :], dst_ref=lands[i], send_sem=send_sems.at[3 * i],
                recv_sem=recv_sems.at[3 * i], device_id=(x, y, 1 - c), device_id_type=MESH))
        return cps
    for i in range(len(srcs)):
        for j, (cx, cy) in enumerate(chips):
            if kind == "gather":
                src, dst = srcs[i].at[c], lands[i].at[p, c]
            else:
                src, dst = srcs[i].at[2 * cx + cy], lands[i].at[j]
            cps.append(pltpu.make_async_remote_copy(
                src_ref=src, dst_ref=dst, send_sem=send_sems.at[3 * i + j], recv_sem=recv_sems.at[3 * i + j],
                device_id=(cx, cy, c), device_id_type=MESH))
    return cps


def split_start(kind, srcs, land_shapes, after, name):
    n = len(srcs)

    def body(*refs):
        src_refs, land_refs = refs[:n], refs[n:2 * n]
        send_sems, recv_sems = refs[2 * n + 1], refs[2 * n + 2]
        token = refs[-1]
        for cp in _chip_copies(kind, src_refs, land_refs, send_sems, recv_sems):
            cp.start()
        token[...] = jnp.zeros_like(token)

    lands = [pltpu.with_memory_space_constraint(lax.empty(s, BF16), pltpu.HBM) for s in land_shapes]
    outs = pl.pallas_call(
        body, name=name,
        out_shape=(pltpu.SemaphoreType.DMA((3 * n,)), pltpu.SemaphoreType.DMA((3 * n,)),
                   *[pltpu.HBM(s.shape, s.dtype) for s in srcs], *[pltpu.HBM(s, BF16) for s in land_shapes],
                   jax.ShapeDtypeStruct((8, 128), F32)),
        in_specs=[_HBM] * (2 * n) + [_ANY],
        out_specs=(_SEM, _SEM, *([_HBM] * (2 * n)), pl.BlockSpec(memory_space=pltpu.VMEM)),
        input_output_aliases={i: 2 + i for i in range(2 * n)},
        compiler_params=pltpu.CompilerParams(has_side_effects=_EFFECT),
    )(*[pltpu.with_memory_space_constraint(s, pltpu.HBM) for s in srcs], *lands, after)
    return outs[0], outs[1], outs[2:2 + n], outs[2 + n:2 + 2 * n], outs[-1]


def split_wait(kind, send_sems, recv_sems, srcs, lands, after, name):
    n = len(srcs)

    def body(*refs):
        src_refs, land_refs = refs[:n], refs[n:2 * n]
        ssem, rsem = refs[2 * n], refs[2 * n + 1]
        for cp in _chip_copies(kind, src_refs, land_refs, ssem, rsem):
            cp.wait_send()
            cp.wait_recv()

    outs = pl.pallas_call(
        body, name=name,
        out_shape=[pltpu.HBM(s.shape, s.dtype) for s in srcs] + [pltpu.HBM(s.shape, s.dtype) for s in lands],
        in_specs=[_HBM] * (2 * n) + [_SEM, _SEM, _ANY], out_specs=[_HBM] * (2 * n),
        input_output_aliases={i: i for i in range(2 * n)},
        compiler_params=pltpu.CompilerParams(has_side_effects=_EFFECT),
    )(*srcs, *lands, send_sems, recv_sems, after)
    return outs[:n], outs[n:]


def pass_to_sibling(lands):
    n = len(lands)

    def body(*refs):
        ins, outs = refs[:n], refs[n:2 * n]
        send_sems, recv_sems = refs[2 * n:]
        x, y, c, chips = _place()
        cps = []
        for i in range(n):
            for j, (cx, cy) in enumerate(chips):
                blk = outs[i].at[2 * cx + cy, c]
                cps.append(pltpu.make_async_remote_copy(
                    src_ref=ins[i].at[2 * cx + cy, c], dst_ref=blk, send_sem=send_sems.at[3 * i + j],
                    recv_sem=recv_sems.at[3 * i + j], device_id=(x, y, 1 - c), device_id_type=MESH))
        for cp in cps:
            cp.start()
        for cp in cps:
            cp.wait()

    return pl.pallas_call(
        body, out_shape=[jax.ShapeDtypeStruct(t.shape, t.dtype) for t in lands], in_specs=[_ANY] * n,
        out_specs=[_ANY] * n, input_output_aliases={i: i for i in range(n)},
        scratch_shapes=[pltpu.SemaphoreType.DMA((3 * n,)), pltpu.SemaphoreType.DMA((3 * n,))],
        name="ag_pass_to_sibling")(*lands)


def _pass_copies(bufs, send_sems, recv_sems):
    x, y, c, chips = _place()
    cps = []
    for i in range(len(bufs)):
        for j, (cx, cy) in enumerate(chips):
            blk = bufs[i].at[2 * cx + cy, c]
            cps.append(pltpu.make_async_remote_copy(
                src_ref=blk, dst_ref=blk, send_sem=send_sems.at[3 * i + j], recv_sem=recv_sems.at[3 * i + j],
                device_id=(x, y, 1 - c), device_id_type=MESH))
    return cps


def pass_start(bufs, after, name):
    n = len(bufs)

    def body(*refs):
        send_sems, recv_sems = refs[n + 1], refs[n + 2]
        for cp in _pass_copies(refs[:n], send_sems, recv_sems):
            cp.start()
        refs[-1][...] = jnp.zeros_like(refs[-1])

    outs = pl.pallas_call(
        body, name=name,
        out_shape=(pltpu.SemaphoreType.DMA((3 * n,)), pltpu.SemaphoreType.DMA((3 * n,)),
                   *[pltpu.HBM(b.shape, b.dtype) for b in bufs], jax.ShapeDtypeStruct((8, 128), F32)),
        in_specs=[_HBM] * n + [_ANY],
        out_specs=(_SEM, _SEM, *([_HBM] * n), pl.BlockSpec(memory_space=pltpu.VMEM)),
        input_output_aliases={i: 2 + i for i in range(n)},
        compiler_params=pltpu.CompilerParams(has_side_effects=_EFFECT),
    )(*[pltpu.with_memory_space_constraint(b, pltpu.HBM) for b in bufs], after)
    return outs[0], outs[1], outs[2:2 + n], outs[-1]


def pass_wait(send_sems, recv_sems, bufs, after, name):
    n = len(bufs)

    def body(*refs):
        for cp in _pass_copies(refs[:n], refs[n], refs[n + 1]):
            cp.wait_send()
            cp.wait_recv()

    return pl.pallas_call(
        body, name=name, out_shape=[pltpu.HBM(b.shape, b.dtype) for b in bufs],
        in_specs=[_HBM] * n + [_SEM, _SEM, _ANY], out_specs=[_HBM] * n,
        input_output_aliases={i: i for i in range(n)},
        compiler_params=pltpu.CompilerParams(has_side_effects=_EFFECT),
    )(*bufs, send_sems, recv_sems, after)


def exchange_halves_to_sibling(gs, name, by_cols=False):
    n = len(gs)

    def body(*refs):
        ins, outs = refs[:n], refs[n:2 * n]
        send_sems, recv_sems = refs[2 * n:]
        x, y, c, _ = _place()
        cps = []
        for i in range(n):
            if by_cols:
                hc = ins[i].shape[2] // 2
                src = ins[i].at[:, :, pl.ds(pl.multiple_of((1 - c) * hc, 128), hc)]
            else:
                h = ins[i].shape[1] // 2
                src = ins[i].at[:, pl.ds((1 - c) * h, h), :]
            cps.append(pltpu.make_async_remote_copy(
                src_ref=src, dst_ref=outs[i],
                send_sem=send_sems.at[i], recv_sem=recv_sems.at[i], device_id=(x, y, 1 - c), device_id_type=MESH))
        for cp in cps:
            cp.start()
        for cp in cps:
            cp.wait()

    halve = (lambda s: (s[0], s[1], s[2] // 2)) if by_cols else (lambda s: (s[0], s[1] // 2, s[2]))
    return pl.pallas_call(
        body, out_shape=[jax.ShapeDtypeStruct(halve(g.shape), g.dtype) for g in gs],
        in_specs=[_ANY] * n, out_specs=[_ANY] * n,
        scratch_shapes=[pltpu.SemaphoreType.DMA((n,)), pltpu.SemaphoreType.DMA((n,))],
        name=name)(*gs)


def join_halves(rs, name):
    n = len(rs)

    def body(*refs):
        ins, outs = refs[:n], refs[n:2 * n]
        send_sems, recv_sems = refs[2 * n:]
        x, y, c, _ = _place()
        cps = [pltpu.make_async_remote_copy(
            src_ref=ins[i], dst_ref=outs[i], send_sem=send_sems.at[i], recv_sem=recv_sems.at[i],
            device_id=(x, y, 1 - c), device_id_type=MESH) for i in range(n)]
        for cp in cps:
            cp.start()
        for cp in cps:
            cp.wait()

    return pl.pallas_call(
        body, out_shape=[jax.ShapeDtypeStruct(r.shape, r.dtype) for r in rs],
        in_specs=[_ANY] * n, out_specs=[_ANY] * n,
        scratch_shapes=[pltpu.SemaphoreType.DMA((n,)), pltpu.SemaphoreType.DMA((n,))],
        name=name)(*rs)


def _pack(parts, row_mult=8):
    flat = jnp.concatenate([p.reshape(-1).astype(F32) for p in parts])
    unit = row_mult * 128
    n = -(-flat.shape[0] // unit) * unit
    return jnp.pad(flat, (0, n - flat.shape[0])).reshape(n // 128, 128)


def _unpack(flat, shapes):
    out, off = [], 0
    for s in shapes:
        n = int(np.prod(s))
        out.append(flat[off:off + n].reshape(s))
        off += n
    return out


def _gather_packed(parts, name):
    packed = _pack(parts)
    g = allgather_small(packed, name).reshape(N_DEV, -1)
    return _unpack_rows(g, [p.shape for p in parts])


def _unpack_rows(g, shapes):
    out, off = [], 0
    for s in shapes:
        n = int(np.prod(s))
        out.append(g[:, off:off + n].reshape((g.shape[0],) + tuple(s)))
        off += n
    return out


def _by_chip(t, axis):
    return jnp.concatenate([t[2 * p] for p in range(N_CHIPS)], axis=axis)


def kernel(x, c, ada_w, ada_b, ln_g, ln_b, a_in_w, a_conv_w, a_conv_b, a_dt_bias, a_A_log, a_D, a_norm_g, a_out_w, kv_w, b_in_w, b_out_w, loss_target, m_ada_w, m_ada_b, m_ln_g, m_ln_b, m_a_in_w, m_a_conv_w, m_a_conv_b, m_a_dt_bias, m_a_A_log, m_a_D, m_a_norm_g, m_a_out_w, m_kv_w, m_b_in_w, m_b_out_w, v_ada_w, v_ada_b, v_ln_g, v_ln_b, v_a_in_w, v_a_conv_w, v_a_conv_b, v_a_dt_bias, v_a_A_log, v_a_D, v_a_norm_g, v_a_out_w, v_kv_w, v_b_in_w, v_b_out_w):
    ax, ay, ac = lax.axis_index("x"), lax.axis_index("y"), lax.axis_index("c")
    chip = 2 * ax + ay
    dev = 4 * ax + 2 * ay + ac
    xin = x[0]
    tgt = loss_target[0]
    L, D = xin.shape
    G, P = SSD_G, SSD_P
    H = a_dt_bias.shape[1]
    Kh = H // G
    DI = H * P
    CONVD = a_conv_b.shape[1] * N_CHIPS
    HW = DIL_H * DIL_E
    Ws = ada_w.shape[2]

    w_in_g = allgather_routed(jnp.transpose(a_in_w[0]).astype(BF16), "allgather_w_in")
    later = [a_out_w[0].astype(BF16), kv_w.astype(BF16), b_in_w[0].astype(BF16), b_out_w[0].astype(BF16)]
    later_split = [s.reshape(2, s.shape[0] // 2, s.shape[1]) for s in later]
    ag_ssem, ag_rsem, ag_srcs, ag_lands, ag_token = split_start(
        "gather", later_split, [(N_CHIPS,) + s.shape for s in later_split], w_in_g, "ag_later_start")
    w_in_t = w_in_g.reshape(-1, D)
    w_dt_t = jnp.pad(w_in_t[DI + CONVD:], ((0, 128 - H), (0, 0)))

    c8, cw8, cb8, ng8 = _gather_packed([c[0], a_conv_w[0], a_conv_b[0], a_norm_g[0]], "allgather_small_params")
    conv_w = _by_chip(cw8, 1)
    conv_b = _by_chip(cb8, 0).reshape(1, CONVD)
    norm_g = _by_chip(ng8, 0).reshape(1, DI)

    mod_s = ada_fwd(c8, ada_w)
    (mod8,) = _gather_packed([mod_s], "allgather_small_mod")
    mods = _by_chip(mod8, 2)
    mod = lax.dynamic_index_in_dim(mods, dev, axis=1, keepdims=False) + ada_b
    shift = [mod[l:l + 1, :D] for l in range(DEPTH)]
    scale = [mod[l:l + 1, D:2 * D] for l in range(DEPTH)]
    gate = [mod[l:l + 1, 2 * D:] for l in range(DEPTH)]
    lg = [ln_g[l:l + 1] for l in range(DEPTH)]
    lb = [ln_b[l:l + 1] for l in range(DEPTH)]

    h0 = modulate(xin, scale[0] + ag_token[0:1, 0:1], shift[0], "modulate0")
    zx = mm_nt(h0, w_in_t, BF16, "mm_in_zx", kw_rows=DI + CONVD)
    dtp = mm_nt(h0, w_dt_t, F32, "mm_in_dt")
    xbc = conv_fwd(zx, DI, conv_w, conv_b)
    dtp_g = jnp.transpose(dtp[:, :H].reshape(L, G, Kh), (1, 0, 2))
    dtp_gT = jnp.transpose(dtp_g, (0, 2, 1))
    vecs = [a_dt_bias.reshape(G, 1, Kh), a_dt_bias.reshape(G, Kh, 1), a_A_log.reshape(G, 1, Kh),
            a_A_log.reshape(G, Kh, 1), a_D.reshape(G, 1, Kh), a_D.reshape(G, Kh, 1)]
    y_ssd, states = ssd_fwd(xbc, dtp_g, dtp_gT, *vecs, DI)
    yn = rms_gate_fwd(y_ssd, zx, norm_g)
    later_split, ag_lands = split_wait("gather", ag_ssem, ag_rsem, ag_srcs, ag_lands, yn, "ag_later_wait")
    (land_out,) = pass_to_sibling(ag_lands[:1])
    ps_ssem, ps_rsem, lands_b, ps_token = pass_start(ag_lands[1:], land_out, "ag_pass_start")

    def place_own(o, s, full):
        return lax.dynamic_update_index_in_dim(o, s, chip, 0).reshape((N_CHIPS,) + full.shape)

    w_out_g = place_own(land_out, later_split[0], later[0])
    ymix0 = mm_nn(yn, w_out_g.reshape(-1, D), F32, "mm_out_a", after=ps_token)
    x1, x1b, h1 = ln_mid(xin, ymix0, gate[0], lg[0], lb[0], scale[1], shift[1])
    lands_b = pass_wait(ps_ssem, ps_rsem, lands_b, x1b, "ag_pass_wait")
    w_kv_g, w_bin_g, w_bout_g = [place_own(o, s, full) for o, s, full in zip(lands_b, later_split[1:], later[1:])]

    n_grp = len(DIL_PATTERNS)
    cb = HW // 512
    assert w_bin_g.shape[2] == HW
    kv3 = [mm_cols_dilated(x1b, w_kv_g, [g * cb + t for t in range(cb)] + [(n_grp + g) * cb + t for t in range(cb)],
                           DIL_PATTERNS[g][1], f"mm_kv_{g}") for g in range(n_grp)]
    q3 = [mm_cols_dilated(h1, w_bin_g, [g], DIL_PATTERNS[g][1], f"mm_q_{g}", tn=HW) for g in range(n_grp)]
    z_b = mm_nn(h1, w_bin_g[n_grp], BF16, "mm_z_b")
    os_, lses = [], []
    for gi in range(len(DIL_PATTERNS)):
        o, lse = attn_fwd(q3[gi], kv3[gi], gi)
        os_.append(o)
        lses.append(lse)
    om = merge_fwd(os_, lses, z_b)
    ymix1 = mm_nn(om, w_bout_g, F32, "mm_out_b", stack="col")
    dres2, dy2, dg1, db1, dgate1, sq = ln_final_fwd_bwd(x1, ymix1, gate[1], lg[1], lb[1], tgt)
    loss_part = 0.5 * jnp.sum(sq) / D

    g_bout = mm_tn(om, dy2, BF16, "mm_gw_out_b", stack="col")
    dgated = mm_nt(dy2, w_bout_g, BF16, "mm_gx_out_b", stack="col")
    dos, dprs, dz_b = merge_bwd(dgated, os_, lses, z_b)
    dqs, dks, dvs = [], [], []
    for gi in range(len(DIL_PATTERNS)):
        dq, dk, dv = attn_bwd(q3[gi], kv3[gi], dos[gi], lses[gi], dprs[gi], gi)
        dqs.append(dq)
        dks.append(dk)
        dvs.append(dv)
    dqz = jnp.concatenate(dqs + [dz_b], axis=1)
    dkv = jnp.concatenate(dks + dvs, axis=1)
    g_bin = mm_tn(h1, dqz, BF16, "mm_gw_in_b", stack="col")
    dh1 = mm_nt(dqz, w_bin_g, BF16, "mm_gx_in_b", stack="col")
    g_kv = mm_tn(x1b, dkv, BF16, "mm_gw_kv", stack="col")

    core = ac.astype(jnp.int32).reshape(1)
    chip_i = chip.astype(jnp.int32).reshape(1)

    def begin_exchange(gs, tag):
        shapes = [(g.shape[0], g.shape[1] // 2, g.shape[2]) for g in gs]
        return split_start("sibling", gs, shapes, gs[0], "rs_x%s_start" % tag)

    def begin_scatter(gs, nms, tag, exchange=None, after=None, by_cols=False):
        if exchange is None:
            sib = exchange_halves_to_sibling(gs, "rs_sibling_exchange_" + tag, by_cols=by_cols)
        else:
            gs, sib = split_wait("sibling", exchange[0], exchange[1], exchange[2], exchange[3], after,
                                 "rs_x%s_wait" % tag)
        parts = [add_half(g, a, core, "rs_add_" + nm, by_cols=by_cols) for g, a, nm in zip(gs, sib, nms)]
        return split_start("scatter", parts, [(3,) + t.shape[1:] for t in parts], parts[0], "rs_%s_start" % tag)

    def finish_scatter(handles, after, tag):
        nms, owns, landed = [], [], []
        for k, (handle, hn) in enumerate(handles):
            parts, lands = split_wait("scatter", handle[0], handle[1], handle[2], handle[3], after,
                                      "rs_%s%d_wait" % (tag, k))
            nms += hn
            owns += list(parts)
            landed += list(lands)
        halves = [sum_partials(own, t, chip_i, "rs_sum_" + nm) for own, t, nm in zip(owns, landed, nms)]
        theirs = join_halves(halves, "rs_join_halves_" + tag)
        return dict(zip(nms, zip(halves, theirs)))

    names_b = ["kv", "in_b", "out_b"]
    ex_b = begin_exchange([g_kv, g_bin, g_bout], "b")
    dx1_kv = mm_nt(dkv, w_kv_g, BF16, "mm_gx_kv", stack="col", after=ex_b[4])
    rs_b = begin_scatter(None, names_b, "b", exchange=ex_b, after=dx1_kv)

    dres1, dy1, dg0, db0, dgate0, dscale1, dshift1 = mod_ln_bwd(
        dres2, dh1, dx1_kv, x1, scale[1], xin, ymix0, gate[0] + rs_b[4][0:1, 0:1], lg[0])
    g_out = mm_tn(yn, dy1, BF16, "mm_gw_out_a", stack="row")
    ex_a1 = begin_exchange([g_out], "a1")
    dyn = mm_nt(dy1, w_out_g, BF16, "mm_gx_out_a", stack="row", after=ex_a1[4])
    rs_a1 = begin_scatter(None, ["out_a"], "a1", exchange=ex_a1, after=dyn)
    dy_ssd, dz_a, dnorm_g = rms_gate_bwd(dyn, y_ssd, zx, norm_g + rs_a1[4][0:1, 0:1])
    dxs, dB, dC, ddtp_g, dbias_g, dalog_g, dD_g = ssd_bwd(xbc, dtp_g, dtp_gT, *vecs, states, dy_ssd, DI)
    dxbc = jnp.concatenate([dxs, dB, dC], axis=1)
    dxbc_pre, dconv_w, dconv_b = conv_bwd(zx, DI, conv_w, conv_b, dxbc)
    dzx = jnp.concatenate([dz_a, dxbc_pre], axis=1)
    ddtp = jnp.pad(jnp.transpose(ddtp_g, (1, 0, 2)).reshape(L, H), ((0, 0), (0, 128 - H)))
    g_inT = mm_tn(dzx, h0, BF16, "mm_gw_in_zx", m_rows=DI + CONVD + H)
    g_dtT = mm_tn(ddtp, h0, BF16, "mm_gw_in_dt")
    g_inT = lax.dynamic_update_slice(g_inT, g_dtT[:H], (DI + CONVD, 0))
    rs_a2 = begin_scatter([g_inT.reshape(N_CHIPS, -1, D)], ["in_a"], "a2", by_cols=True)
    dh0 = mm_nn(dzx, w_in_t, BF16, "mm_gx_in_zx", after=rs_a2[4])
    dh0_dt = mm_nn(ddtp, w_dt_t, F32, "mm_gx_in_dt")
    grad_x, dscale0, dshift0 = mod_bwd(dres1, dh0, dh0_dt, xin, scale[0] + rs_a2[4][0:1, 0:1], "mod_bwd0")
    g_halves = finish_scatter([(rs_b, names_b)], grad_x, "b")

    def step_halves(w, m, v, nm):
        shp = w.shape
        mine, theirs_ = g_halves[nm]
        outs4 = adamw_halves(w.reshape(-1, shp[-1]), mine, theirs_, m.reshape(-1, shp[-1]), v.reshape(-1, shp[-1]),
                             core, "adamw_" + nm)
        return tuple(t.reshape(shp) for t in outs4)

    big = {
        "kv_w": step_halves(kv_w, m_kv_w, v_kv_w, "kv"),
        "b_in_w": step_halves(b_in_w, m_b_in_w, v_b_in_w, "in_b"),
        "b_out_w": step_halves(b_out_w, m_b_out_w, v_b_out_w, "out_b"),
    }
    g_halves.update(finish_scatter([(rs_a1, ["out_a"]), (rs_a2, ["in_a"])], big["kv_w"][1], "a"))
    g_halves["in_a"] = tuple(jnp.transpose(t) for t in g_halves["in_a"])
    big["a_in_w"] = step_halves(a_in_w, m_a_in_w, v_a_in_w, "in_a")
    big["a_out_w"] = step_halves(a_out_w, m_a_out_w, v_a_out_w, "out_a")

    dmod = jnp.concatenate([jnp.concatenate([dshift0, dscale0, dgate0], axis=1),
                            jnp.concatenate([dshift1, dscale1, dgate1], axis=1)], axis=0)
    small_parts = [jnp.concatenate([dg0, dg1], axis=0), jnp.concatenate([db0, db1], axis=0),
                   dbias_g.reshape(1, H), dalog_g.reshape(1, H), dD_g.reshape(1, H),
                   dconv_w, dconv_b, dnorm_g, loss_part.reshape(1, 1)]
    small_shapes = [p.shape for p in small_parts]
    packed = jnp.concatenate([_pack([dmod]), _pack(small_parts)], axis=0)
    n_mod_rows = _pack([dmod]).shape[0]
    gathered = allgather_small(packed, "allgather_small_grads", after=g_halves["in_a"][1]).reshape(N_DEV, -1, 128)
    dmod8 = gathered[:, :n_mod_rows].reshape(N_DEV, -1)[:, :2 * 3 * D].reshape(N_DEV, DEPTH, 3 * D)
    summed = sum_leading(gathered, "sum_small")
    g_ada_b = summed[:n_mod_rows].reshape(-1)[:2 * 3 * D].reshape(DEPTH, 3 * D)
    (g_ln_g, g_ln_b, g_dt_bias, g_a_log, g_dsk, g_conv_w, g_conv_b, g_norm_g, loss_all) = _unpack(
        summed[n_mod_rows:].reshape(-1), small_shapes)
    loss = loss_all.reshape(())
    Cs = CONVD // N_CHIPS
    g_conv_w_s = lax.dynamic_slice_in_dim(g_conv_w, chip * Cs, Cs, axis=1)
    g_conv_b_s = lax.dynamic_slice_in_dim(g_conv_b, chip * Cs, Cs, axis=1)
    g_norm_g_s = lax.dynamic_slice_in_dim(g_norm_g, chip * (DI // N_CHIPS), DI // N_CHIPS, axis=1)
    dmod_s = jnp.transpose(lax.dynamic_slice_in_dim(dmod8, chip * Ws, Ws, axis=2), (1, 0, 2))

    def step2d(w, g, m, v, nm):
        shp = w.shape
        d_, m_, v_ = adamw(w.reshape(-1, shp[-1]), g.reshape(-1, shp[-1]), m.reshape(-1, shp[-1]),
                           v.reshape(-1, shp[-1]), "adamw_" + nm)
        return g.reshape(shp), d_.reshape(shp), m_.reshape(shp), v_.reshape(shp)

    big["ada_w"] = step2d(ada_w, ada_wgrad(jnp.transpose(c8), dmod_s), m_ada_w, v_ada_w, "ada_w")
    small_names = ["ada_b", "ln_g", "ln_b", "a_conv_w", "a_conv_b", "a_dt_bias", "a_A_log", "a_D", "a_norm_g"]
    small_w = [ada_b, ln_g, ln_b, a_conv_w, a_conv_b, a_dt_bias, a_A_log, a_D, a_norm_g]
    small_m = [m_ada_b, m_ln_g, m_ln_b, m_a_conv_w, m_a_conv_b, m_a_dt_bias, m_a_A_log, m_a_D, m_a_norm_g]
    small_v = [v_ada_b, v_ln_g, v_ln_b, v_a_conv_w, v_a_conv_b, v_a_dt_bias, v_a_A_log, v_a_D, v_a_norm_g]
    small_g = [g_ada_b, g_ln_g, g_ln_b, g_conv_w_s, g_conv_b_s, g_dt_bias, g_a_log, g_dsk, g_norm_g_s]
    shapes = [w.shape for w in small_w]
    small_g = [g.reshape(s) for g, s in zip(small_g, shapes)]
    d_p, m_p, v_p = adamw(_pack(small_w), _pack(small_g), _pack(small_m), _pack(small_v), "adamw_small")
    small = {}
    for nm, g, d_, m_, v_ in zip(small_names, small_g, _unpack(d_p.reshape(-1), shapes), _unpack(m_p.reshape(-1), shapes),
                                 _unpack(v_p.reshape(-1), shapes)):
        small[nm] = (g, d_, m_, v_)
    allw = {**big, **small}
    order = ["ada_w", "ada_b", "ln_g", "ln_b", "a_in_w", "a_conv_w", "a_conv_b", "a_dt_bias", "a_A_log", "a_D",
             "a_norm_g", "a_out_w", "kv_w", "b_in_w", "b_out_w"]
    outs = [loss, grad_x.reshape(x.shape)]
    for k in range(4):
        outs += [allw[n][k] for n in order]
    return tuple(outs)
```

```python
import functools

import jax
import jax.numpy as jnp
import numpy as np
from jax import lax
from jax.experimental import pallas as pl
from jax.experimental.pallas import tpu as pltpu

F32 = jnp.float32
BF16 = jnp.bfloat16
MESH = pl.DeviceIdType.MESH

DEPTH = 2
ALPHA = (2 * DEPTH) ** 0.25
LN_EPS = 1e-5
RMS_EPS = 1e-5
SSD_P = 64
SSD_N = 128
SSD_Q = 256
SSD_G = 8
CONV_W = 4
DIL_PATTERNS = ((128, 1), (512, 4), (2048, 16))
DIL_H = 8
DIL_E = 128
DIL_BLK = 128
ADAM_LR, ADAM_B1, ADAM_B2, ADAM_EPS, ADAM_WD, ADAM_STEP = 0.001, 0.9, 0.999, 1e-08, 0.01, 10

VMEM_LIMIT = 56 * 1024 * 1024
N_CHIPS = 4
N_DEV = 8


def _tile(dim, target, mult=128):
    if dim <= target:
        return dim
    t = (target // mult) * mult
    while t >= mult:
        if dim % t == 0:
            return t
        t -= mult
    return dim


def _cp(sem):
    return pltpu.CompilerParams(dimension_semantics=sem, vmem_limit_bytes=VMEM_LIMIT)


def _sigmoid(x):
    return 1.0 / (1.0 + jnp.exp(-x))


def _silu(x):
    return x * _sigmoid(x)


def _dsilu(x):
    s = _sigmoid(x)
    return s * (1.0 + x * (1.0 - s))


def _softplus(x):
    return jnp.maximum(x, 0.0) + jnp.log(1.0 + jnp.exp(-jnp.abs(x)))


def _mm_call(a, b, out_shape, grid, a_spec, b_spec, o_spec, acc_shape, dims, name, after=None):
    nk = grid[2]
    extra = [] if after is None else [after]

    def prod(a_ref, b_ref):
        return lax.dot_general(a_ref[...].astype(BF16), b_ref[...].astype(BF16), (dims, ((), ())),
                               preferred_element_type=F32)

    def body_single(a_ref, b_ref, *rest):
        o_ref = rest[len(extra)]
        o_ref[...] = prod(a_ref, b_ref).astype(o_ref.dtype)

    def body_multi(a_ref, b_ref, *rest):
        o_ref, acc_ref = rest[len(extra):]
        k = pl.program_id(2)

        @pl.when(k == 0)
        def _():
            acc_ref[...] = prod(a_ref, b_ref)

        @pl.when(jnp.logical_and(k > 0, k < nk - 1))
        def _():
            acc_ref[...] += prod(a_ref, b_ref)

        @pl.when(k == nk - 1)
        def _():
            o_ref[...] = (acc_ref[...] + prod(a_ref, b_ref)).astype(o_ref.dtype)

    return pl.pallas_call(
        body_single if nk == 1 else body_multi, grid=grid, in_specs=[a_spec, b_spec] + [_ANY] * len(extra),
        out_specs=o_spec, out_shape=out_shape, scratch_shapes=[] if nk == 1 else [pltpu.VMEM(acc_shape, F32)],
        compiler_params=_cp(("parallel", "parallel", "arbitrary")), name=name)(a, b, *extra)


def mm_nn(a, b, out_dtype, name, stack=None, tm=1024, tn=1024, tk=2048, n_cols=None, after=None):
    M, K = a.shape
    if stack is None:
        N = b.shape[1] if n_cols is None else n_cols
        tn, tk = _tile(N, tn), _tile(K, tk)
        b_spec = pl.BlockSpec((tk, tn), lambda i, j, k: (k, j))
    elif stack == "col":
        S, _, Ns = b.shape
        N = S * Ns
        tn, tk = _tile(Ns, tn), _tile(K, tk)
        npb = Ns // tn
        b_spec = pl.BlockSpec((None, tk, tn), lambda i, j, k: (j // npb, k, j % npb))
    else:
        S, Ks, N = b.shape
        tn, tk = _tile(N, tn), _tile(Ks, tk)
        kpb = Ks // tk
        b_spec = pl.BlockSpec((None, tk, tn), lambda i, j, k: (k // kpb, k % kpb, j))
    tm = _tile(M, tm)
    return _mm_call(a, b, jax.ShapeDtypeStruct((M, N), out_dtype), (M // tm, N // tn, K // tk),
                    pl.BlockSpec((tm, tk), lambda i, j, k: (i, k)), b_spec,
                    pl.BlockSpec((tm, tn), lambda i, j, k: (i, j)), (tm, tn), ((1,), (0,)), name, after=after)


def mm_cols_dilated(a, b, gcols, d, name, tm=1024, tn=512):
    L, K = a.shape
    S, _, Ns = b.shape
    tm, tn = _tile(L, tm), _tile(Ns, tn)
    npb = Ns // tn
    nj = len(gcols)
    rows = tm // d

    def body(cols_ref, a_ref, b_ref, o_ref, *scr):
        prod = jnp.dot(a_ref[...], b_ref[...], preferred_element_type=F32)
        if d == 1:
            o_ref[0] = prod.astype(BF16)
        else:
            for c in range(tn // 128):
                scr[0][c] = prod[:, c * 128:(c + 1) * 128]
            for r in range(d):
                for c in range(tn // 128):
                    o_ref[r, :, c * 128:(c + 1) * 128] = scr[0].at[c][pl.ds(r, rows, stride=d), :].astype(BF16)

    return pl.pallas_call(
        body,
        grid_spec=pltpu.PrefetchScalarGridSpec(
            num_scalar_prefetch=1, grid=(L // tm, nj),
            in_specs=[pl.BlockSpec((tm, K), lambda i, j, c: (i, 0)),
                      pl.BlockSpec((None, K, tn), lambda i, j, c: (c[j] // npb, 0, c[j] % npb))],
            out_specs=pl.BlockSpec((d, rows, tn), lambda i, j, c: (0, i, j)),
            scratch_shapes=[] if d == 1 else [pltpu.VMEM((tn // 128, tm, 128), F32)]),
        out_shape=jax.ShapeDtypeStruct((d, L // d, nj * tn), BF16),
        compiler_params=_cp(("parallel", "arbitrary")), name=name)(jnp.asarray(gcols, jnp.int32), a, b)


def mm_nt(a, b, out_dtype, name, stack=None, tm=1024, tn=1024, tk=2048, after=None, kw_rows=None):
    M, C = a.shape
    if stack is None:
        Kw = b.shape[0] if kw_rows is None else kw_rows
        tn, tk = _tile(Kw, tn), _tile(C, tk)
        b_spec = pl.BlockSpec((tn, tk), lambda i, j, k: (j, k))
    elif stack == "col":
        S, Kw, Cs = b.shape
        tn, tk = _tile(Kw, tn), _tile(Cs, tk)
        cpb = Cs // tk
        b_spec = pl.BlockSpec((None, tn, tk), lambda i, j, k: (k // cpb, j, k % cpb))
    else:
        S, Ks, _ = b.shape
        Kw = S * Ks
        tn, tk = _tile(Ks, tn), _tile(C, tk)
        jpb = Ks // tn
        b_spec = pl.BlockSpec((None, tn, tk), lambda i, j, k: (j // jpb, j % jpb, k))
    tm = _tile(M, tm)
    return _mm_call(a, b, jax.ShapeDtypeStruct((M, Kw), out_dtype), (M // tm, Kw // tn, C // tk),
                    pl.BlockSpec((tm, tk), lambda i, j, k: (i, k)), b_spec,
                    pl.BlockSpec((tm, tn), lambda i, j, k: (i, j)), (tm, tn), ((1,), (1,)), name, after=after)


def mm_tn(a, b, out_dtype, name, stack=None, n_stack=N_CHIPS, tm=1024, tn=1024, tk=2048, m_rows=None):
    L, M = a.shape
    N = b.shape[1]
    tk = _tile(L, tk)
    if stack is None:
        tm, tn = _tile(M, tm), _tile(N, tn)
        o_spec = pl.BlockSpec((tm, tn), lambda i, j, k: (i, j))
        out_shape = (M if m_rows is None else m_rows, N)
    elif stack == "col":
        Ns = N // n_stack
        tm, tn = _tile(M, tm), _tile(Ns, tn)
        npb = Ns // tn
        o_spec = pl.BlockSpec((None, tm, tn), lambda i, j, k: (j // npb, i, j % npb))
        out_shape = (n_stack, M, Ns)
    else:
        Ms = M // n_stack
        tm, tn = _tile(Ms, tm), _tile(N, tn)
        mpb = Ms // tm
        o_spec = pl.BlockSpec((None, tm, tn), lambda i, j, k: (i // mpb, i % mpb, j))
        out_shape = (n_stack, Ms, N)
    return _mm_call(a, b, jax.ShapeDtypeStruct(out_shape, out_dtype), (M // tm, N // tn, L // tk),
                    pl.BlockSpec((tk, tm), lambda i, j, k: (k, i)), pl.BlockSpec((tk, tn), lambda i, j, k: (k, j)),
                    o_spec, (tm, tn), ((0,), (0,)), name)


def _row_specs(tr, widths):
    return [pl.BlockSpec((tr, w), lambda i: (i, 0)) for w in widths]


def _vec_spec(w):
    return pl.BlockSpec((1, w), lambda i: (0, 0))


def _acc_rows(ref, val, i):
    s = jnp.sum(val, axis=0, keepdims=True)

    @pl.when(i == 0)
    def _():
        ref[...] = s

    @pl.when(i > 0)
    def _():
        ref[...] += s


def modulate(x, scale, shift, name):
    L, D = x.shape
    tr = _tile(L, 512, 16)

    def body(x_ref, sc_ref, sh_ref, h_ref):
        h_ref[...] = (x_ref[...] * (1.0 + sc_ref[...]) + sh_ref[...]).astype(BF16)

    return pl.pallas_call(
        body, grid=(L // tr,), in_specs=_row_specs(tr, [D]) + [_vec_spec(D)] * 2, out_specs=_row_specs(tr, [D])[0],
        out_shape=jax.ShapeDtypeStruct((L, D), BF16), compiler_params=_cp(("parallel",)), name=name)(x, scale, shift)


def _ln_core(x, y, gate, g, b):
    u = ALPHA * x + (1.0 + gate) * y
    mu = jnp.mean(u, axis=-1, keepdims=True)
    d = u - mu
    var = jnp.mean(d * d, axis=-1, keepdims=True)
    rstd = lax.rsqrt(var + LN_EPS)
    xhat = d * rstd
    return xhat * g + b, xhat, rstd


def ln_mid(x, y, gate, g, b, scale, shift):
    L, D = x.shape
    tr = _tile(L, 256, 16)

    def body(x_ref, y_ref, gate_ref, g_ref, b_ref, sc_ref, sh_ref, x1_ref, x1b_ref, h_ref):
        x1, _, _ = _ln_core(x_ref[...], y_ref[...], gate_ref[...], g_ref[...], b_ref[...])
        x1_ref[...] = x1
        x1b_ref[...] = x1.astype(BF16)
        h_ref[...] = (x1 * (1.0 + sc_ref[...]) + sh_ref[...]).astype(BF16)

    return pl.pallas_call(
        body, grid=(L // tr,), in_specs=_row_specs(tr, [D, D]) + [_vec_spec(D)] * 5,
        out_specs=_row_specs(tr, [D, D, D]),
        out_shape=[jax.ShapeDtypeStruct((L, D), F32), jax.ShapeDtypeStruct((L, D), BF16),
                   jax.ShapeDtypeStruct((L, D), BF16)],
        compiler_params=_cp(("parallel",)), name="ln_mid")(x, y, gate, g, b, scale, shift)


def _ln_bwd_rows(dout_v, xhat, rstd, g):
    dxh = dout_v * g
    m1 = jnp.mean(dxh, axis=-1, keepdims=True)
    m2 = jnp.mean(dxh * xhat, axis=-1, keepdims=True)
    return rstd * (dxh - m1 - xhat * m2)


def ln_final_fwd_bwd(x, y, gate, g, b, target):
    L, D = x.shape
    tr = _tile(L, 256, 16)

    def body(x_ref, y_ref, gate_ref, g_ref, b_ref, t_ref, dres_ref, dy_ref, dg_ref, db_ref, dgate_ref, sq_ref):
        i = pl.program_id(0)
        yv = y_ref[...]
        out, xhat, rstd = _ln_core(x_ref[...], yv, gate_ref[...], g_ref[...], b_ref[...])
        err = out - t_ref[...]
        dout_v = err * (1.0 / D)
        du = _ln_bwd_rows(dout_v, xhat, rstd, g_ref[...])
        dres_ref[...] = ALPHA * du
        dy_ref[...] = ((1.0 + gate_ref[...]) * du).astype(BF16)
        _acc_rows(dg_ref, dout_v * xhat, i)
        _acc_rows(db_ref, dout_v, i)
        _acc_rows(dgate_ref, du * yv, i)
        _acc_rows(sq_ref, err * err, i)

    return pl.pallas_call(
        body, grid=(L // tr,), in_specs=_row_specs(tr, [D, D]) + [_vec_spec(D)] * 3 + _row_specs(tr, [D]),
        out_specs=_row_specs(tr, [D, D]) + [_vec_spec(D)] * 4,
        out_shape=[jax.ShapeDtypeStruct((L, D), F32), jax.ShapeDtypeStruct((L, D), BF16)]
        + [jax.ShapeDtypeStruct((1, D), F32)] * 4,
        compiler_params=_cp(("arbitrary",)), name="ln_final_fwd_bwd")(x, y, gate, g, b, target)


def mod_ln_bwd(dres_in, dh, dskip, xmid, scale, x, y, gate, g):
    L, D = x.shape
    tr = _tile(L, 256, 16)

    def body(dres_ref, dh_ref, dskip_ref, xm_ref, sc_ref, x_ref, y_ref, gate_ref, g_ref,
             dres_out, dy_ref, dg_ref, db_ref, dgate_ref, dsc_ref, dsh_ref):
        i = pl.program_id(0)
        dh_v = dh_ref[...].astype(F32)
        dout_v = dres_ref[...] + dskip_ref[...].astype(F32) + dh_v * (1.0 + sc_ref[...])
        _acc_rows(dsc_ref, dh_v * xm_ref[...], i)
        _acc_rows(dsh_ref, dh_v, i)
        yv = y_ref[...]
        _, xhat, rstd = _ln_core(x_ref[...], yv, gate_ref[...], g_ref[...], 0.0)
        du = _ln_bwd_rows(dout_v, xhat, rstd, g_ref[...])
        dres_out[...] = ALPHA * du
        dy_ref[...] = ((1.0 + gate_ref[...]) * du).astype(BF16)
        _acc_rows(dg_ref, dout_v * xhat, i)
        _acc_rows(db_ref, dout_v, i)
        _acc_rows(dgate_ref, du * yv, i)

    return pl.pallas_call(
        body, grid=(L // tr,),
        in_specs=_row_specs(tr, [D] * 4) + [_vec_spec(D)] + _row_specs(tr, [D, D]) + [_vec_spec(D)] * 2,
        out_specs=_row_specs(tr, [D, D]) + [_vec_spec(D)] * 5,
        out_shape=[jax.ShapeDtypeStruct((L, D), F32), jax.ShapeDtypeStruct((L, D), BF16)]
        + [jax.ShapeDtypeStruct((1, D), F32)] * 5,
        compiler_params=_cp(("arbitrary",)), name="mod_ln_bwd")(dres_in, dh, dskip, xmid, scale, x, y, gate, g)


def mod_bwd(dres, dh, dh2, xin, scale, name):
    L, D = xin.shape
    tr = _tile(L, 256, 16)

    def body(dres_ref, dh_ref, dh2_ref, x_ref, sc_ref, dx_ref, dsc_ref, dsh_ref):
        i = pl.program_id(0)
        dh_v = dh_ref[...].astype(F32) + dh2_ref[...].astype(F32)
        dx_ref[...] = dres_ref[...] + dh_v * (1.0 + sc_ref[...])
        _acc_rows(dsc_ref, dh_v * x_ref[...], i)
        _acc_rows(dsh_ref, dh_v, i)

    return pl.pallas_call(
        body, grid=(L // tr,), in_specs=_row_specs(tr, [D, D, D, D]) + [_vec_spec(D)],
        out_specs=_row_specs(tr, [D]) + [_vec_spec(D)] * 2,
        out_shape=[jax.ShapeDtypeStruct((L, D), F32)] + [jax.ShapeDtypeStruct((1, D), F32)] * 2,
        compiler_params=_cp(("arbitrary",)), name=name)(dres, dh, dh2, xin, scale)


CONV_HALO = 16


def _conv_rows(x_ref, i, tr, L):
    nblk = L // tr
    s = pl.multiple_of(i * tr, CONV_HALO)
    cur = x_ref[pl.ds(s, tr), :].astype(F32)
    sp = pl.multiple_of(jnp.maximum(i * tr - CONV_HALO, 0), CONV_HALO)
    sn = pl.multiple_of(jnp.minimum(i * tr + tr, L - CONV_HALO), CONV_HALO)
    prev = x_ref[pl.ds(sp, CONV_HALO), :].astype(F32) * (i > 0).astype(F32)
    nxt = x_ref[pl.ds(sn, CONV_HALO), :].astype(F32) * (i < nblk - 1).astype(F32)
    return jnp.concatenate([prev, cur, nxt], axis=0)


def _shift_rows(v, j):
    n = v.shape[0]
    return v if j % n == 0 else pltpu.roll(v, j % n, 0)


def _conv_taps(xe):
    return [_shift_rows(xe, CONV_W - 1 - k) for k in range(CONV_W)]


def _conv_eval(taps, w_ref, b_ref):
    c = b_ref[...] + w_ref[0:1, :] * taps[0]
    for k in range(1, CONV_W):
        c = c + w_ref[k:k + 1, :] * taps[k]
    return c


def conv_fwd(zx, col0, conv_w, conv_b):
    L = zx.shape[0]
    C = conv_w.shape[1]
    tc = _tile(C, 512)
    tr = _tile(L, 512, CONV_HALO)
    off = col0 // tc

    def body(x_ref, w_ref, b_ref, o_ref):
        i = pl.program_id(1)
        xe = _conv_rows(x_ref, i, tr, L)
        c = _conv_eval(_conv_taps(xe), w_ref, b_ref)[CONV_HALO:CONV_HALO + tr]
        o_ref[...] = _silu(c).astype(BF16)

    return pl.pallas_call(
        body, grid=(C // tc, L // tr),
        in_specs=[pl.BlockSpec((L, tc), lambda j, i: (0, off + j)), pl.BlockSpec((CONV_W, tc), lambda j, i: (0, j)),
                  pl.BlockSpec((1, tc), lambda j, i: (0, j))],
        out_specs=pl.BlockSpec((tr, tc), lambda j, i: (i, j)),
        out_shape=jax.ShapeDtypeStruct((L, C), BF16), compiler_params=_cp(("parallel", "arbitrary")),
        name="conv_fwd")(zx, conv_w, conv_b)


def conv_bwd(zx, col0, conv_w, conv_b, dxbc):
    L = zx.shape[0]
    C = conv_w.shape[1]
    tc = _tile(C, 512)
    tr = _tile(L, 512, CONV_HALO)
    off = col0 // tc
    H = CONV_HALO

    def body(x_ref, g_ref, w_ref, b_ref, dx_ref, dw_ref, db_ref):
        i = pl.program_id(1)
        xe = _conv_rows(x_ref, i, tr, L)
        ge = _conv_rows(g_ref, i, tr, L)
        taps = _conv_taps(xe)
        dc = ge * _dsilu(_conv_eval(taps, w_ref, b_ref))
        dx = w_ref[CONV_W - 1:CONV_W, :] * dc
        for k in range(CONV_W - 1):
            dx = dx + w_ref[k:k + 1, :] * _shift_rows(dc, -(CONV_W - 1 - k))
        dx_ref[...] = dx[H:H + tr].astype(BF16)
        dcc = dc[H:H + tr]
        rows = [jnp.sum(dcc * taps[k][H:H + tr], axis=0, keepdims=True) for k in range(CONV_W)]
        dwv = jnp.concatenate(rows + [jnp.zeros((8 - CONV_W, tc), F32)], axis=0)
        dbv = jnp.sum(dcc, axis=0, keepdims=True)

        @pl.when(i == 0)
        def _():
            dw_ref[...] = dwv
            db_ref[...] = dbv

        @pl.when(i > 0)
        def _():
            dw_ref[...] += dwv
            db_ref[...] += dbv

    dx, dw, db = pl.pallas_call(
        body, grid=(C // tc, L // tr),
        in_specs=[pl.BlockSpec((L, tc), lambda j, i: (0, off + j)), pl.BlockSpec((L, tc), lambda j, i: (0, j)),
                  pl.BlockSpec((CONV_W, tc), lambda j, i: (0, j)), pl.BlockSpec((1, tc), lambda j, i: (0, j))],
        out_specs=[pl.BlockSpec((tr, tc), lambda j, i: (i, j)), pl.BlockSpec((8, tc), lambda j, i: (0, j)),
                   pl.BlockSpec((1, tc), lambda j, i: (0, j))],
        out_shape=[jax.ShapeDtypeStruct((L, C), BF16), jax.ShapeDtypeStruct((8, C), F32),
                   jax.ShapeDtypeStruct((1, C), F32)],
        compiler_params=_cp(("parallel", "arbitrary")), name="conv_bwd")(zx, dxbc, conv_w, conv_b)
    return dx, dw[:CONV_W], db


_NN = (((1,), (0,)), ((), ()))


def _pieces(x, n):
    out, r = [], x
    for _ in range(n):
        p = r.astype(BF16)
        out.append(p)
        r = r - p.astype(F32)
    return out


def _dot01(a, b01, n, dims=_NN):
    b = b01.astype(BF16)
    return functools.reduce(lambda u, v: u + v,
                            [lax.dot_general(p, b, dims, preferred_element_type=F32) for p in _pieces(a, n)])


def _dot01_left(a01, b, n, dims=_NN):
    a = a01.astype(BF16)
    return functools.reduce(lambda u, v: u + v,
                            [lax.dot_general(a, p, dims, preferred_element_type=F32) for p in _pieces(b, n)])


def _ssd_common(dtp_ref, dtpT_ref, bias_ref, biasT_ref, alog_ref, alogT_ref, b_ref, c_ref):
    Q = SSD_Q
    dt = _softplus(dtp_ref[...] + bias_ref[...])
    A = -jnp.exp(alog_ref[...])
    row = lax.broadcasted_iota(jnp.int32, (Q, Q), 0)
    col = lax.broadcasted_iota(jnp.int32, (Q, Q), 1)
    causal = row >= col
    tril = causal.astype(F32)
    Kh = dt.shape[1]
    acum = _dot01_left(tril, dt * A, 3)
    eye = (lax.broadcasted_iota(jnp.int32, (Kh, Kh), 0) == lax.broadcasted_iota(jnp.int32, (Kh, Kh), 1)).astype(F32)
    acumT = _dot01_left(eye, acum, 3, dims=(((1,), (1,)), ((), ())))
    Bm = b_ref[...]
    Cm = c_ref[...]
    cb = lax.dot_general(Cm, Bm, (((1,), (1,)), ((), ())), preferred_element_type=F32)
    return dt, A, causal, row, col, acum, acumT, Bm, Cm, cb


def _ssd_in_specs(Q, GP, N, Kh, DI, cmap):
    nb0 = DI // N
    vec = pl.BlockSpec((None, 1, Kh), lambda g, c: (g, 0, 0))
    vecT = pl.BlockSpec((None, Kh, 1), lambda g, c: (g, 0, 0))
    return [pl.BlockSpec((Q, GP), lambda g, c: (cmap(c), g)),
            pl.BlockSpec((Q, N), lambda g, c: (cmap(c), nb0 + g)),
            pl.BlockSpec((Q, N), lambda g, c: (cmap(c), nb0 + SSD_G + g)),
            pl.BlockSpec((None, Q, Kh), lambda g, c: (g, cmap(c), 0)),
            pl.BlockSpec((None, Kh, Q), lambda g, c: (g, 0, cmap(c))),
            vec, vecT, vec, vecT, vec, vecT]


def _hi(a, b01):
    return _dot01(a, b01, 2)


def _headsum(a, b01):
    return _dot01(a, b01, 1)


def _ssd_heads(dskT_ref, acum, acumT, dt, Kh):
    Q, P, N = SSD_Q, SSD_P, SSD_N
    GP = Kh * P
    sh_p = P.bit_length() - 1
    seg = lambda shape, dim: lax.shift_right_logical(lax.broadcasted_iota(jnp.int32, shape, dim), sh_p)
    E = (seg((Kh, GP), 1) == lax.broadcasted_iota(jnp.int32, (Kh, GP), 0)).astype(F32)
    ET = (seg((GP, Kh), 0) == lax.broadcasted_iota(jnp.int32, (GP, Kh), 1)).astype(F32)
    a_last = acum[Q - 1:Q, :]
    tail = jnp.exp(a_last - acum)
    eLT = jnp.exp(acumT[:, Q - 1:Q])
    rowseg = seg((GP, N), 0)
    eL_b = jnp.zeros((GP, N), F32)
    for k in range(Kh):
        eL_b = jnp.where(rowseg == k, eLT[k:k + 1, :], eL_b)
    return dict(
        E=E, ET=ET, a_last=a_last, tail=tail, eL_b=eL_b,
        dt_all=_hi(dt, E), ea_all=_hi(jnp.exp(acum), E), tail_all=_hi(tail, E),
        dsk_all=jnp.sum(E * dskT_ref[...], axis=0, keepdims=True))


def _head_chunks(GP):
    CW = min(GP, 128)
    return CW, CW // SSD_P, GP // CW


def _head_mask(Q, CW, kk):
    lane = lax.broadcasted_iota(jnp.int32, (Q, CW), 1)
    return jnp.logical_and(lane >= kk * SSD_P, lane < (kk + 1) * SSD_P)


def ssd_fwd(xbc, dtp_g, dtp_gT, bias_g, bias_gT, alog_g, alog_gT, dsk_g, dsk_gT, DI):
    L = xbc.shape[0]
    Q, P, N, G = SSD_Q, SSD_P, SSD_N, SSD_G
    GP = DI // G
    Kh = GP // P
    nc = L // Q

    CW, hpc, nch = _head_chunks(GP)
    nt = (((1,), (1,)), ((), ()))
    tn = (((0,), (0,)), ((), ()))

    def body(xs_ref, b_ref, c_ref, dtp_ref, dtpT_ref, bias_ref, biasT_ref, alog_ref, alogT_ref, dsk_ref, dskT_ref,
             y_ref, st_ref, state):
        @pl.when(pl.program_id(1) == 0)
        def _():
            state[...] = jnp.zeros(state.shape, F32)

        st_ref[...] = state[...]
        dt, A, causal, row, col, acum, acumT, Bm, Cm, cb = _ssd_common(
            dtp_ref, dtpT_ref, bias_ref, biasT_ref, alog_ref, alogT_ref, b_ref, c_ref)
        hd = _ssd_heads(dskT_ref, acum, acumT, dt, Kh)
        xs = xs_ref[...].astype(F32)
        xdt_all = xs * hd["dt_all"]
        S_all = state[...]
        y_all = (lax.dot_general(Cm, S_all.astype(BF16), nt, preferred_element_type=F32) * hd["ea_all"]
                 + xs * hd["dsk_all"])
        state[...] = S_all * hd["eL_b"] + lax.dot_general(
            (xdt_all * hd["tail_all"]).astype(BF16), Bm, tn, preferred_element_type=F32)
        for ch in range(nch):
            cs = slice(ch * CW, (ch + 1) * CW)
            xc = xdt_all[:, cs]
            acc = y_all[:, cs]
            for kk in range(hpc):
                k = ch * hpc + kk
                decay = jnp.exp(jnp.where(causal, acum[:, k:k + 1] - acumT[k:k + 1, :], -jnp.inf))
                xk = xc if hpc == 1 else jnp.where(_head_mask(Q, CW, kk), xc, 0.0)
                acc = acc + jnp.dot((cb * decay).astype(BF16), xk.astype(BF16), preferred_element_type=F32)
            y_ref[:, cs] = acc.astype(BF16)

    return pl.pallas_call(
        body, grid=(G, nc), in_specs=_ssd_in_specs(Q, GP, N, Kh, DI, lambda c: c),
        out_specs=[pl.BlockSpec((Q, GP), lambda g, c: (c, g)),
                   pl.BlockSpec((None, None, GP, N), lambda g, c: (c, g, 0, 0))],
        out_shape=[jax.ShapeDtypeStruct((L, DI), BF16), jax.ShapeDtypeStruct((nc, G, GP, N), F32)],
        scratch_shapes=[pltpu.VMEM((GP, N), F32)], compiler_params=_cp(("parallel", "arbitrary")),
        name="ssd_fwd")(xbc, xbc, xbc, dtp_g, dtp_gT, bias_g, bias_gT, alog_g, alog_gT, dsk_g, dsk_gT)


def ssd_bwd(xbc, dtp_g, dtp_gT, bias_g, bias_gT, alog_g, alog_gT, dsk_g, dsk_gT, states, dyn, y, zx, norm_g, DI):
    L = xbc.shape[0]
    Q, P, N, G = SSD_Q, SSD_P, SSD_N, SSD_G
    GP = DI // G
    Kh = GP // P
    nc = L // Q
    rev = lambda c: nc - 1 - c

    CW, hpc, nch = _head_chunks(GP)

    def body(xs_ref, b_ref, c_ref, dtp_ref, dtpT_ref, bias_ref, biasT_ref, alog_ref, alogT_ref, dsk_ref, dskT_ref,
             st_ref, dyn_ref, y_ref, z_ref, ng_ref,
             dxs_ref, dB_ref, dC_ref, ddtp_ref, dbias_ref, dalog_ref, dD_ref, dz_ref, dng_ref, dstate):
        ci = pl.program_id(1)

        @pl.when(ci == 0)
        def _():
            dstate[...] = jnp.zeros(dstate.shape, F32)

        dt, A, causal, row, col, acum, acumT, Bm, Cm, cb = _ssd_common(
            dtp_ref, dtpT_ref, bias_ref, biasT_ref, alog_ref, alogT_ref, b_ref, c_ref)
        tn = (((0,), (0,)), ((), ()))
        nt = (((1,), (1,)), ((), ()))
        hd = _ssd_heads(dskT_ref, acum, acumT, dt, Kh)
        ET, tail = hd["ET"], hd["tail"]
        cbT = lax.dot_general(Bm, Cm, nt, preferred_element_type=F32)
        causalT = row <= col
        xs = xs_ref[...].astype(F32)
        xdt_all = xs * hd["dt_all"]
        yv = y_ref[...].astype(F32)
        zv = z_ref[...].astype(F32)
        dynv = dyn_ref[...].astype(F32)
        sz = _silu(zv)
        y2 = yv * sz
        rr = lax.rsqrt(jnp.mean(y2 * y2, axis=-1, keepdims=True) + RMS_EPS)
        yh = y2 * rr
        dyh = dynv * ng_ref[...]
        dy2 = rr * (dyh - yh * jnp.mean(dyh * yh, axis=-1, keepdims=True))
        dz_ref[...] = (dy2 * yv * _dsilu(zv)).astype(BF16)
        dng_v = jnp.sum(dynv * yh, axis=0, keepdims=True)
        dyb = (dy2 * sz).astype(BF16)
        dy_all = dyb.astype(F32)
        S_all = st_ref[...]
        S_b = S_all.astype(BF16)
        dS_all = dstate[...]
        dS_b = dS_all.astype(BF16)
        CS_all = lax.dot_general(Cm, S_b, nt, preferred_element_type=F32)
        dyE_b = (dy_all * hd["ea_all"]).astype(BF16)
        dC_acc = jnp.dot(dyE_b, S_b, preferred_element_type=F32)
        dS_y = lax.dot_general(dyE_b, Cm, tn, preferred_element_type=F32)
        BdS_all = lax.dot_general(Bm, dS_b, nt, preferred_element_type=F32)
        dB_acc = jnp.dot((xdt_all * hd["tail_all"]).astype(BF16), dS_b, preferred_element_type=F32)
        dtail = _headsum(xdt_all * BdS_all, ET)
        da_cols = _headsum(dy_all * CS_all * hd["ea_all"], ET) - dtail * tail
        dss = _dot01_left(jnp.ones((8, N), F32), _dot01_left(hd["E"], dS_all * S_all, 2), 2, dims=nt)
        da_last = dss[0:1] * jnp.exp(hd["a_last"]) + jnp.sum(dtail * tail, axis=0, keepdims=True)
        rowi = lax.broadcasted_iota(jnp.int32, (Q, Kh), 0)
        da_cols = da_cols + jnp.where(rowi == Q - 1, da_last, 0.0)
        dstate[...] = hd["eL_b"] * dS_all + dS_y
        sum_mg = jnp.zeros((Q, Q), F32)
        ddt_x = jnp.zeros((Q, Kh), F32)
        da_rows = jnp.zeros((Kh, Q), F32)
        lane_k = lax.broadcasted_iota(jnp.int32, (Q, Kh), 1)
        sub_k = lax.broadcasted_iota(jnp.int32, (Kh, Q), 0)
        for ch in range(nch):
            cs = slice(ch * CW, (ch + 1) * CW)
            dyc = dyb[:, cs]
            xc_b = xdt_all[:, cs].astype(BF16)
            acc = hd["tail_all"][:, cs] * BdS_all[:, cs]
            for kk in range(hpc):
                k = ch * hpc + kk
                a_b = jnp.broadcast_to(acum[:, k:k + 1], (Q, Q))
                a_r = acumT[k:k + 1, :]
                decay = jnp.exp(jnp.where(causal, a_b - a_r, -jnp.inf))
                decayT = jnp.exp(jnp.where(causalT, a_r - a_b, -jnp.inf))
                dyk = dyc if hpc == 1 else jnp.where(_head_mask(Q, CW, kk), dyc, jnp.zeros_like(dyc))
                mg = decay * lax.dot_general(dyk, xc_b, nt, preferred_element_type=F32)
                sum_mg = sum_mg + mg
                w = mg * cb
                da_cols = da_cols + jnp.where(lane_k == k, jnp.sum(w, axis=1, keepdims=True), 0.0)
                da_rows = da_rows + jnp.where(sub_k == k, jnp.sum(w, axis=0, keepdims=True), 0.0)
                acc = acc + jnp.dot((decayT * cbT).astype(BF16), dyk, preferred_element_type=F32)
            dxs_ref[:, cs] = (acc * hd["dt_all"][:, cs] + dy_all[:, cs] * hd["dsk_all"][:, cs]).astype(BF16)
            ddt_x = ddt_x + _headsum(acc * xs[:, cs], ET[cs, :])
        eye_q = (row == col).astype(F32)
        da_cols = da_cols - _dot01_left(eye_q, da_rows, 3, dims=nt)
        dD_row = jnp.sum(_headsum(dy_all * xs, ET), axis=0, keepdims=True)
        sum_mg_b = sum_mg.astype(BF16)
        dB_ref[...] = (dB_acc + lax.dot_general(sum_mg_b, Cm, tn, preferred_element_type=F32)).astype(BF16)
        dC_ref[...] = (dC_acc + jnp.dot(sum_mg_b, Bm, preferred_element_type=F32)).astype(BF16)
        triu = (row <= col).astype(F32)
        ddtA = _dot01_left(triu, da_cols, 3)
        ddt = ddt_x + ddtA * A
        dpre = ddt * _sigmoid(dtp_ref[...] + bias_ref[...])
        ddtp_ref[...] = dpre
        dbias_v = jnp.sum(dpre, axis=0, keepdims=True)
        dalog_v = jnp.sum(ddtA * dt, axis=0, keepdims=True) * A

        @pl.when(ci == 0)
        def _():
            dbias_ref[...] = dbias_v
            dalog_ref[...] = dalog_v
            dD_ref[...] = dD_row
            dng_ref[...] = dng_v

        @pl.when(ci > 0)
        def _():
            dbias_ref[...] += dbias_v
            dalog_ref[...] += dalog_v
            dD_ref[...] += dD_row
            dng_ref[...] += dng_v

    vec_o = pl.BlockSpec((None, 1, Kh), lambda g, c: (g, 0, 0))
    tile = pl.BlockSpec((Q, GP), lambda g, c: (rev(c), g))
    return pl.pallas_call(
        body, grid=(G, nc),
        in_specs=_ssd_in_specs(Q, GP, N, Kh, DI, rev)
        + [pl.BlockSpec((None, None, GP, N), lambda g, c: (rev(c), g, 0, 0)), tile, tile, tile,
           pl.BlockSpec((1, GP), lambda g, c: (0, g))],
        out_specs=[tile, pl.BlockSpec((Q, N), lambda g, c: (rev(c), g)), pl.BlockSpec((Q, N), lambda g, c: (rev(c), g)),
                   pl.BlockSpec((None, Q, Kh), lambda g, c: (g, rev(c), 0)), vec_o, vec_o, vec_o,
                   tile, pl.BlockSpec((1, GP), lambda g, c: (0, g))],
        out_shape=[jax.ShapeDtypeStruct((L, DI), BF16), jax.ShapeDtypeStruct((L, G * N), BF16),
                   jax.ShapeDtypeStruct((L, G * N), BF16), jax.ShapeDtypeStruct((G, L, Kh), F32)]
        + [jax.ShapeDtypeStruct((G, 1, Kh), F32)] * 3
        + [jax.ShapeDtypeStruct((L, DI), BF16), jax.ShapeDtypeStruct((1, DI), F32)],
        scratch_shapes=[pltpu.VMEM((GP, N), F32)], compiler_params=_cp(("parallel", "arbitrary")),
        name="ssd_bwd")(xbc, xbc, xbc, dtp_g, dtp_gT, bias_g, bias_gT, alog_g, alog_gT, dsk_g, dsk_gT, states,
                        dyn, y, zx, norm_g)


def _rms_groups(y2, ng_ref, DI):
    S = DI // SSD_G
    for g in range(SSD_G):
        gs = slice(g * S, (g + 1) * S)
        seg = y2[:, gs]
        r = lax.rsqrt(jnp.mean(seg * seg, axis=-1, keepdims=True) + RMS_EPS)
        yield gs, seg * r, r, ng_ref[:, gs]


def rms_gate_fwd(y, zx, norm_g):
    L, DI = y.shape
    tr = _tile(L, 256, 16)

    def body(y_ref, z_ref, ng_ref, o_ref):
        y2 = y_ref[...].astype(F32) * _silu(z_ref[...].astype(F32))
        for gs, yh, _, ng in _rms_groups(y2, ng_ref, DI):
            o_ref[:, gs] = (yh * ng).astype(BF16)

    return pl.pallas_call(
        body, grid=(L // tr,), in_specs=_row_specs(tr, [DI, DI]) + [_vec_spec(DI)], out_specs=_row_specs(tr, [DI])[0],
        out_shape=jax.ShapeDtypeStruct((L, DI), BF16), compiler_params=_cp(("parallel",)),
        name="rms_gate_fwd")(y, zx, norm_g)


def _alibi_slope(gi, h):
    n = len(DIL_PATTERNS) * DIL_H
    return float(2.0 ** (-8.0 * (gi * DIL_H + h + 1) / n))


def _attn_masks():
    qi = lax.broadcasted_iota(jnp.int32, (DIL_BLK, DIL_BLK), 0)
    kj = lax.broadcasted_iota(jnp.int32, (DIL_BLK, DIL_BLK), 1)
    dcur = (qi - kj).astype(F32)
    return dcur, qi >= kj, dcur + float(DIL_BLK), kj >= qi


def attn_fwd(q3, kv3, gi):
    window, d = DIL_PATTERNS[gi]
    assert window // d == DIL_BLK
    HW = DIL_H * DIL_E
    M = q3.shape[1]
    nb = M // DIL_BLK
    scale = DIL_E ** -0.5
    nt = (((1,), (1,)), ((), ()))

    def body(q_ref, kp_ref, kc_ref, vp_ref, vc_ref, o_ref, lse_ref):
        n = pl.program_id(1)
        dcur, vcur, dprev, vprev0 = _attn_masks()
        dist = jnp.concatenate([dprev, dcur], axis=1)
        valid = jnp.concatenate([jnp.logical_and(vprev0, n > 0), vcur], axis=1)
        lane = lax.broadcasted_iota(jnp.int32, (DIL_BLK, 128), 1)
        lse_acc = jnp.zeros((DIL_BLK, 128), F32)
        for h in range(DIL_H):
            hs = slice(h * DIL_E, (h + 1) * DIL_E)
            sl = _alibi_slope(gi, h) * d
            kcat = jnp.concatenate([kp_ref[:, hs], kc_ref[:, hs]], axis=0)
            vcat = jnp.concatenate([vp_ref[:, hs], vc_ref[:, hs]], axis=0)
            s = lax.dot_general(q_ref[:, hs], kcat, nt, preferred_element_type=F32) * scale - sl * dist
            s = jnp.where(valid, s, -jnp.inf)
            m = jnp.max(s, axis=-1, keepdims=True)
            p = jnp.exp(s - m)
            den = jnp.sum(p, axis=-1, keepdims=True)
            o = jnp.dot(p.astype(BF16), vcat, preferred_element_type=F32) / den
            o_ref[:, hs] = o.astype(BF16)
            lse_acc = jnp.where(lane == h, m + jnp.log(den), lse_acc)
        lse_ref[...] = lse_acc

    blk = (None, DIL_BLK, HW)
    prev = lambda n: jnp.maximum(n - 1, 0)
    return pl.pallas_call(
        body, grid=(d, nb),
        in_specs=[pl.BlockSpec(blk, lambda r, n: (r, n, 0)),
                  pl.BlockSpec(blk, lambda r, n: (r, prev(n), 0)), pl.BlockSpec(blk, lambda r, n: (r, n, 0)),
                  pl.BlockSpec(blk, lambda r, n: (r, prev(n), 1)), pl.BlockSpec(blk, lambda r, n: (r, n, 1))],
        out_specs=[pl.BlockSpec(blk, lambda r, n: (r, n, 0)), pl.BlockSpec((None, DIL_BLK, 128), lambda r, n: (r, n, 0))],
        out_shape=[jax.ShapeDtypeStruct((d, M, HW), BF16), jax.ShapeDtypeStruct((d, M, 128), F32)],
        compiler_params=_cp(("parallel", "parallel")), name=f"attn_fwd_{gi}")(q3, kv3, kv3, kv3, kv3)


def attn_bwd(q3, kv3, do3, lse3, dpr3, gi):
    window, d = DIL_PATTERNS[gi]
    HW = DIL_H * DIL_E
    M = q3.shape[1]
    L = M * d
    nb = M // DIL_BLK
    scale = DIL_E ** -0.5
    nt = (((1,), (1,)), ((), ()))
    tn = (((0,), (0,)), ((), ()))

    def body(q0_ref, q1_ref, k_ref, v_ref, do0_ref, do1_ref, l0_ref, l1_ref, r0_ref, r1_ref,
             dq_ref, dk_ref, dv_ref, carry):
        n = pl.program_id(1)

        @pl.when(n == 0)
        def _():
            carry[...] = jnp.zeros(carry.shape, F32)

        dcur, vcur, dprev, vprev0 = _attn_masks()
        dist = jnp.concatenate([dcur, dprev], axis=0)
        valid = jnp.concatenate([vcur, jnp.logical_and(vprev0, n < nb - 1)], axis=0)
        B = DIL_BLK
        for h in range(DIL_H):
            hs = slice(h * DIL_E, (h + 1) * DIL_E)
            sl = _alibi_slope(gi, h) * d
            kh = k_ref[:, hs]
            vh = v_ref[:, hs]
            qcat = jnp.concatenate([q0_ref[:, hs], q1_ref[:, hs]], axis=0)
            docat = jnp.concatenate([do0_ref[:, hs], do1_ref[:, hs]], axis=0)
            lcat = jnp.concatenate([l0_ref[:, h:h + 1], l1_ref[:, h:h + 1]], axis=0)
            rcat = jnp.concatenate([r0_ref[:, h:h + 1], r1_ref[:, h:h + 1]], axis=0)
            s = lax.dot_general(qcat, kh, nt, preferred_element_type=F32) * scale - sl * dist
            p = jnp.exp(jnp.where(valid, s - lcat, -jnp.inf))
            ds = p * (lax.dot_general(docat, vh, nt, preferred_element_type=F32) - rcat)
            ds_b = (ds * scale).astype(BF16)
            dv_ref[:, hs] = lax.dot_general(p.astype(BF16), docat, tn, preferred_element_type=F32).astype(BF16)
            dk_ref[:, hs] = lax.dot_general(ds_b, qcat, tn, preferred_element_type=F32).astype(BF16)
            dqc = jnp.dot(ds_b, kh, preferred_element_type=F32)
            dq_ref[:, hs] = (carry[:, hs] + dqc[:B]).astype(BF16)
            carry[:, hs] = dqc[B:]

    blk = (None, DIL_BLK, HW)
    sblk = (None, DIL_BLK, 128)
    oblk = (DIL_BLK, HW)
    nxt = lambda n: jnp.minimum(n + 1, nb - 1)
    here = lambda c: (lambda r, n: (r, n, c))
    ahead = lambda c: (lambda r, n: (r, nxt(n), c))
    outs = pl.pallas_call(
        body, grid=(d, nb),
        in_specs=[pl.BlockSpec(blk, here(0)), pl.BlockSpec(blk, ahead(0)),
                  pl.BlockSpec(blk, here(0)), pl.BlockSpec(blk, here(1)),
                  pl.BlockSpec(blk, here(0)), pl.BlockSpec(blk, ahead(0)),
                  pl.BlockSpec(sblk, here(0)), pl.BlockSpec(sblk, ahead(0)),
                  pl.BlockSpec(sblk, here(0)), pl.BlockSpec(sblk, ahead(0))],
        out_specs=[pl.BlockSpec(oblk, lambda r, n: (n, r))] * 3,
        out_shape=[jax.ShapeDtypeStruct((M, d * HW), BF16)] * 3,
        scratch_shapes=[pltpu.VMEM(oblk, F32)], compiler_params=_cp(("parallel", "arbitrary")),
        name=f"attn_bwd_{gi}")(q3, q3, kv3, kv3, do3, do3, lse3, lse3, dpr3, dpr3)
    return [t.reshape(L, HW) for t in outs]


def _merge_weights(l_tiles, h):
    ls = [t[:, h:h + 1] for t in l_tiles]
    mx = functools.reduce(jnp.maximum, ls)
    es = [jnp.exp(l - mx) for l in ls]
    den = functools.reduce(lambda a, b: a + b, es)
    return [e / den for e in es]


def _dil_specs(tr, arrs):
    return [pl.BlockSpec((a.shape[0], tr // a.shape[0], a.shape[2]), lambda i: (0, i, 0)) for a in arrs]


def _dil_scratch(tr, arrs):
    return [pltpu.VMEM((a.shape[2] // 128, tr, 128), F32) for a in arrs if a.shape[0] > 1]


def _undilate(refs3, scrs, tr):
    out, k = [], 0
    for ref in refs3:
        d, _, W = ref.shape
        if d == 1:
            out.append(lambda c, ref=ref: ref[0, :, c * 128:(c + 1) * 128])
            continue
        scr = scrs[k]
        k += 1
        for r in range(d):
            for c in range(W // 128):
                scr.at[c][pl.ds(r, tr // d, stride=d), :] = ref[r, :, c * 128:(c + 1) * 128].astype(F32)
        out.append(lambda c, scr=scr: scr[c])
    return out


def merge_fwd(os3, lses3, z):
    HW = os3[0].shape[2]
    L = os3[0].shape[0] * os3[0].shape[1]
    tr = _tile(L, 256, 16)
    ng = len(os3)
    n_scr = len(_dil_scratch(tr, os3))

    def body(*refs):
        z_ref, out_ref = refs[2 * ng], refs[2 * ng + 1]
        scrs = refs[2 * ng + 2:]
        o_get = _undilate(refs[:ng], scrs[:n_scr], tr)
        l_tiles = [g(0) for g in _undilate(refs[ng:2 * ng], scrs[n_scr:], tr)]
        for h in range(DIL_H):
            hs = slice(h * DIL_E, (h + 1) * DIL_E)
            ws = _merge_weights(l_tiles, h)
            om = functools.reduce(lambda a, b: a + b, [w * o(h).astype(F32) for w, o in zip(ws, o_get)])
            out_ref[:, hs] = (om * _silu(z_ref[:, hs].astype(F32))).astype(BF16)

    return pl.pallas_call(
        body, grid=(L // tr,),
        in_specs=_dil_specs(tr, os3) + _dil_specs(tr, lses3) + _row_specs(tr, [HW]),
        out_specs=_row_specs(tr, [HW])[0], out_shape=jax.ShapeDtypeStruct((L, HW), BF16),
        scratch_shapes=_dil_scratch(tr, os3) + _dil_scratch(tr, lses3),
        compiler_params=_cp(("parallel",)), name="merge_fwd")(*os3, *lses3, z)


def merge_bwd(dgated, os3, lses3, z):
    HW = os3[0].shape[2]
    L = os3[0].shape[0] * os3[0].shape[1]
    tr = _tile(L, 256, 16)
    ng = len(os3)
    n_scr = len(_dil_scratch(tr, os3))

    def body(*refs):
        dg_ref = refs[0]
        z_ref = refs[1 + 2 * ng]
        outs = refs[2 + 2 * ng:2 + 2 * ng + 2 * ng + 1]
        scrs = refs[2 + 2 * ng + 2 * ng + 1:]
        do_out, dpr_out, dz_ref = outs[:ng], outs[ng:2 * ng], outs[2 * ng]
        o_get = _undilate(refs[1:1 + ng], scrs[:n_scr], tr)
        l_tiles = [g(0) for g in _undilate(refs[1 + ng:1 + 2 * ng], scrs[n_scr:2 * n_scr], tr)]
        stage = scrs[2 * n_scr:]
        do_stage, dpr_stage, k = [], [], 0
        for g in range(ng):
            if do_out[g].shape[0] == 1:
                do_stage.append(None)
                dpr_stage.append(None)
            else:
                do_stage.append(stage[2 * k])
                dpr_stage.append(stage[2 * k + 1])
                k += 1
        lane = lax.broadcasted_iota(jnp.int32, (tr, 128), 1)
        accs = [jnp.zeros((tr, 128), F32) for _ in range(ng)]
        for h in range(DIL_H):
            hs = slice(h * DIL_E, (h + 1) * DIL_E)
            ws = _merge_weights(l_tiles, h)
            ov = [o(h).astype(F32) for o in o_get]
            om = functools.reduce(lambda a, b: a + b, [w * o for w, o in zip(ws, ov)])
            zv = z_ref[:, hs].astype(F32)
            dgv = dg_ref[:, hs].astype(F32)
            dom = dgv * _silu(zv)
            dz_ref[:, hs] = (dgv * om * _dsilu(zv)).astype(BF16)
            dws = [jnp.sum(dom * o, axis=-1, keepdims=True) for o in ov]
            dwbar = functools.reduce(lambda a, b: a + b, [w * dw for w, dw in zip(ws, dws)])
            for g in range(ng):
                if do_stage[g] is None:
                    do_out[g][0, :, hs] = (ws[g] * dom).astype(BF16)
                else:
                    do_stage[g][h] = ws[g] * dom
                accs[g] = jnp.where(lane == h, ws[g] * dwbar, accs[g])
        for g in range(ng):
            d = do_out[g].shape[0]
            if d == 1:
                dpr_out[g][0] = accs[g]
                continue
            dpr_stage[g][0] = accs[g]
            for r in range(d):
                dpr_out[g][r] = dpr_stage[g].at[0][pl.ds(r, tr // d, stride=d), :]
                for c in range(HW // 128):
                    do_out[g][r, :, c * 128:(c + 1) * 128] = do_stage[g].at[c][pl.ds(r, tr // d, stride=d), :].astype(BF16)

    stage_shapes = []
    for o3 in os3:
        if o3.shape[0] > 1:
            stage_shapes += [pltpu.VMEM((HW // 128, tr, 128), F32), pltpu.VMEM((1, tr, 128), F32)]
    outs = pl.pallas_call(
        body, grid=(L // tr,),
        in_specs=_row_specs(tr, [HW]) + _dil_specs(tr, os3) + _dil_specs(tr, lses3) + _row_specs(tr, [HW]),
        out_specs=_dil_specs(tr, os3) + _dil_specs(tr, lses3) + _row_specs(tr, [HW]),
        out_shape=[jax.ShapeDtypeStruct(o.shape, BF16) for o in os3] + [jax.ShapeDtypeStruct(l.shape, F32) for l in lses3]
        + [jax.ShapeDtypeStruct((L, HW), BF16)],
        scratch_shapes=_dil_scratch(tr, os3) + _dil_scratch(tr, lses3) + stage_shapes,
        compiler_params=_cp(("parallel",)), name="merge_bwd")(dgated, *os3, *lses3, z)
    return outs[:ng], outs[ng:2 * ng], outs[2 * ng]


def ada_fwd(c8, ada_w):
    nl, D, Ws = ada_w.shape
    tn = _tile(Ws, 512)

    def body(c_ref, w_ref, o_ref):
        o_ref[...] = jnp.dot(_silu(c_ref[...]), w_ref[...], precision=lax.Precision.HIGHEST,
                             preferred_element_type=F32)

    return pl.pallas_call(
        body, grid=(nl, Ws // tn),
        in_specs=[pl.BlockSpec((N_DEV, D), lambda l, j: (0, 0)), pl.BlockSpec((None, D, tn), lambda l, j: (l, 0, j))],
        out_specs=pl.BlockSpec((None, N_DEV, tn), lambda l, j: (l, 0, j)),
        out_shape=jax.ShapeDtypeStruct((nl, N_DEV, Ws), F32), compiler_params=_cp(("parallel", "parallel")),
        name="ada_fwd")(c8, ada_w)


def ada_wgrad(c8t, dmod):
    nl, _, Ws = dmod.shape
    D = c8t.shape[0]
    tm = _tile(D, 512, 8)

    def body(c_ref, d_ref, o_ref):
        sc = _silu(c_ref[...])
        acc = sc[:, 0:1] * d_ref[0:1, :]
        for e in range(1, N_DEV):
            acc = acc + sc[:, e:e + 1] * d_ref[e:e + 1, :]
        o_ref[...] = acc

    return pl.pallas_call(
        body, grid=(nl, D // tm),
        in_specs=[pl.BlockSpec((tm, N_DEV), lambda l, i: (i, 0)), pl.BlockSpec((None, N_DEV, Ws), lambda l, i: (l, 0, 0))],
        out_specs=pl.BlockSpec((None, tm, Ws), lambda l, i: (l, i, 0)),
        out_shape=jax.ShapeDtypeStruct((nl, D, Ws), F32), compiler_params=_cp(("parallel", "parallel")),
        name="ada_wgrad")(c8t, dmod)


def adamw(w, g, m, v, name):
    R, C = w.shape
    tr = _tile(R, 256, 8)
    c1 = 1.0 - ADAM_B1 ** ADAM_STEP
    c2 = 1.0 - ADAM_B2 ** ADAM_STEP

    def body(w_ref, g_ref, m_ref, v_ref, d_ref, nm_ref, nv_ref):
        gv = g_ref[...]
        nm = ADAM_B1 * m_ref[...] + (1.0 - ADAM_B1) * gv
        nv = ADAM_B2 * v_ref[...] + (1.0 - ADAM_B2) * (gv * gv)
        nm_ref[...] = nm
        nv_ref[...] = nv
        d_ref[...] = -ADAM_LR * ((nm / c1) / (jnp.sqrt(nv / c2) + ADAM_EPS) + ADAM_WD * w_ref[...])

    return pl.pallas_call(
        body, grid=(R // tr,), in_specs=_row_specs(tr, [C] * 4), out_specs=_row_specs(tr, [C] * 3),
        out_shape=[jax.ShapeDtypeStruct((R, C), F32)] * 3, compiler_params=_cp(("parallel",)), name=name)(w, g, m, v)


def sum_leading(t, name, out_dtype=F32):
    S, R, C = t.shape
    tr = _tile(R, 256, 16)

    def body(t_ref, o_ref):
        acc = t_ref[0].astype(F32)
        for s in range(1, S):
            acc = acc + t_ref[s].astype(F32)
        o_ref[...] = acc.astype(out_dtype)

    return pl.pallas_call(
        body, grid=(R // tr,), in_specs=[pl.BlockSpec((S, tr, C), lambda i: (0, i, 0))],
        out_specs=pl.BlockSpec((tr, C), lambda i: (i, 0)), out_shape=jax.ShapeDtypeStruct((R, C), out_dtype),
        compiler_params=_cp(("parallel",)), name=name)(t)


def add_half(g, a, core, name, by_cols=False):
    S, R, C = g.shape

    def body(core_ref, g_ref, a_ref, o_ref):
        o_ref[...] = (g_ref[...].astype(F32) + a_ref[...].astype(F32)).astype(BF16)

    if by_cols:
        hc = C // 2
        tr = _tile(R, 256, 16)
        return pl.pallas_call(
            body,
            grid_spec=pltpu.PrefetchScalarGridSpec(
                num_scalar_prefetch=1, grid=(S, R // tr),
                in_specs=[pl.BlockSpec((None, tr, hc), lambda s, i, core_ref: (s, i, core_ref[0])),
                          pl.BlockSpec((None, tr, hc), lambda s, i, core_ref: (s, i, 0))],
                out_specs=pl.BlockSpec((None, tr, hc), lambda s, i, core_ref: (s, i, 0))),
            out_shape=jax.ShapeDtypeStruct((S, R, hc), BF16), compiler_params=_cp(("parallel", "parallel")),
            name=name)(core, g, a)
    h = R // 2
    tr = _tile(h, 256, 16)
    nb = h // tr

    return pl.pallas_call(
        body,
        grid_spec=pltpu.PrefetchScalarGridSpec(
            num_scalar_prefetch=1, grid=(S, nb),
            in_specs=[pl.BlockSpec((None, tr, C), lambda s, i, core_ref: (s, core_ref[0] * nb + i, 0)),
                      pl.BlockSpec((None, tr, C), lambda s, i, core_ref: (s, i, 0))],
            out_specs=pl.BlockSpec((None, tr, C), lambda s, i, core_ref: (s, i, 0))),
        out_shape=jax.ShapeDtypeStruct((S, h, C), BF16), compiler_params=_cp(("parallel", "parallel")),
        name=name)(core, g, a)


def sum_partials(own, landed, chip, name):
    _, h, C = own.shape
    tr = _tile(h, 256, 16)

    def body(chip_ref, own_ref, l_ref, o_ref):
        acc = own_ref[...].astype(F32)
        for j in range(3):
            acc = acc + l_ref[j].astype(F32)
        o_ref[...] = acc

    return pl.pallas_call(
        body,
        grid_spec=pltpu.PrefetchScalarGridSpec(
            num_scalar_prefetch=1, grid=(h // tr,),
            in_specs=[pl.BlockSpec((None, tr, C), lambda i, chip_ref: (chip_ref[0], i, 0)),
                      pl.BlockSpec((3, tr, C), lambda i, chip_ref: (0, i, 0))],
            out_specs=pl.BlockSpec((tr, C), lambda i, chip_ref: (i, 0))),
        out_shape=jax.ShapeDtypeStruct((h, C), F32), compiler_params=_cp(("parallel",)), name=name)(chip, own, landed)


def adamw_halves(w, g_mine, g_theirs, m, v, core, name):
    R, C = w.shape
    h = R // 2
    tr = _tile(h, 256, 8)
    nbh = h // tr
    c1 = 1.0 - ADAM_B1 ** ADAM_STEP
    c2 = 1.0 - ADAM_B2 ** ADAM_STEP

    def body(core_ref, w_ref, gm_ref, gt_ref, m_ref, v_ref, g_ref, d_ref, nm_ref, nv_ref):
        mine = (pl.program_id(0) // nbh) == core_ref[0]
        gv = jnp.where(mine, gm_ref[...], gt_ref[...])
        g_ref[...] = gv
        nm = ADAM_B1 * m_ref[...] + (1.0 - ADAM_B1) * gv
        nv = ADAM_B2 * v_ref[...] + (1.0 - ADAM_B2) * (gv * gv)
        nm_ref[...] = nm
        nv_ref[...] = nv
        d_ref[...] = -ADAM_LR * ((nm / c1) / (jnp.sqrt(nv / c2) + ADAM_EPS) + ADAM_WD * w_ref[...])

    full = pl.BlockSpec((tr, C), lambda i, core_ref: (i, 0))
    halfspec = pl.BlockSpec((tr, C), lambda i, core_ref: (i % nbh, 0))
    return pl.pallas_call(
        body,
        grid_spec=pltpu.PrefetchScalarGridSpec(
            num_scalar_prefetch=1, grid=(2 * nbh,), in_specs=[full, halfspec, halfspec, full, full],
            out_specs=[full] * 4),
        out_shape=[jax.ShapeDtypeStruct((R, C), F32)] * 4, compiler_params=_cp(("parallel",)),
        name=name)(core, w, g_mine, g_theirs, m, v)


_ANY = pl.BlockSpec(memory_space=pl.ANY)


def _place():
    x, y, c = lax.axis_index("x"), lax.axis_index("y"), lax.axis_index("c")
    chips = [(1 - x, y), (x, 1 - y), (1 - x, 1 - y)]
    return x, y, c, chips


def allgather_small(v, name, after=None):
    R, W = v.shape
    extra = [] if after is None else [after]

    def body(x_ref, *rest):
        out_ref, send_sems, recv_sems, local_sem = rest[len(extra):]
        x, y, c, chips = _place()
        me, sibling = (x, y, c), (x, y, 1 - c)

        def rows(px, py, pc):
            return out_ref.at[pl.ds((4 * px + 2 * py + pc) * R, R), :]

        def copy(k, block, to, src=None):
            return pltpu.make_async_remote_copy(
                src_ref=rows(*block) if src is None else src, dst_ref=rows(*block),
                send_sem=send_sems.at[k], recv_sem=recv_sems.at[k], device_id=to, device_id_type=MESH)

        mine = pltpu.make_async_copy(x_ref, rows(*me), local_sem)
        mine.start()
        first = [copy(0, me, sibling, src=x_ref)]
        first += [copy(1 + j, me, (*chip, c), src=x_ref) for j, chip in enumerate(chips)]
        for cp in first:
            cp.start()
        passed = [copy(4 + j, (*chip, c), sibling) for j, chip in enumerate(chips)]
        for j, chip in enumerate(chips):
            copy(1 + j, (*chip, c), me).wait_recv()
            passed[j].start()
        copy(0, sibling, me).wait_recv()
        for j, chip in enumerate(chips):
            copy(4 + j, (*chip, 1 - c), me).wait_recv()
        for cp in first + passed:
            cp.wait_send()
        mine.wait()

    return pl.pallas_call(
        body, out_shape=jax.ShapeDtypeStruct((N_DEV * R, W), v.dtype),
        in_specs=[pl.BlockSpec(memory_space=pltpu.VMEM)] + [_ANY] * len(extra),
        out_specs=pl.BlockSpec(memory_space=pltpu.VMEM),
        scratch_shapes=[pltpu.SemaphoreType.DMA((7,)), pltpu.SemaphoreType.DMA((7,)), pltpu.SemaphoreType.DMA],
        name=name)(v, *extra)


def allgather_routed(shard, name):
    R, C = shard.shape
    hc = C // 2
    ra = (R // 2) // 16 * 16

    def body(in_ref, out_ref, send_sems, recv_sems):
        x, y, c, _ = _place()
        xn, yn = (1 - x, y, c), (x, 1 - y, c)
        sibling = (x, y, 1 - c)
        p, pxn, pyn, pdg = 2 * x + y, 2 * (1 - x) + y, 2 * x + (1 - y), 2 * (1 - x) + (1 - y)
        rows_a, rows_b, rows_all = pl.ds(0, ra), pl.ds(ra, R - ra), pl.ds(0, R)

        def win(ref, rows, core):
            return ref.at[rows, pl.ds(pl.multiple_of(core * hc, 128), hc)]

        def copy(k, chip_id, rows, core, to, src=None):
            blk = win(out_ref.at[chip_id], rows, core)
            return pltpu.make_async_remote_copy(
                src_ref=blk if src is None else src, dst_ref=blk, send_sem=send_sems.at[k], recv_sem=recv_sems.at[k],
                device_id=to, device_id_type=MESH)

        own = [copy(0, p, rows_a, c, xn, src=win(in_ref, rows_a, c)), copy(1, p, rows_b, c, xn, src=win(in_ref, rows_b, c)),
               copy(2, p, rows_b, c, yn, src=win(in_ref, rows_b, c)), copy(3, p, rows_a, c, yn, src=win(in_ref, rows_a, c))]
        for cp in own:
            cp.start()
        copy(0, pxn, rows_a, c, xn).wait_recv()
        fwd_a = copy(4, pxn, rows_a, c, yn)
        fwd_a.start()
        copy(2, pyn, rows_b, c, yn).wait_recv()
        fwd_b = copy(5, pyn, rows_b, c, xn)
        fwd_b.start()
        copy(1, pxn, rows_b, c, xn).wait_recv()
        copy(3, pyn, rows_a, c, yn).wait_recv()
        passed = [copy(6, pxn, rows_all, c, sibling), copy(7, pyn, rows_all, c, sibling)]
        for cp in passed:
            cp.start()
        copy(4, pdg, rows_a, c, yn).wait_recv()
        passed.append(copy(8, pdg, rows_a, c, sibling))
        passed[-1].start()
        copy(5, pdg, rows_b, c, xn).wait_recv()
        passed.append(copy(9, pdg, rows_b, c, sibling))
        passed[-1].start()
        for k, (chip_id, rows) in enumerate([(pxn, rows_all), (pyn, rows_all), (pdg, rows_a), (pdg, rows_b)]):
            copy(6 + k, chip_id, rows, 1 - c, sibling).wait_recv()
        for cp in own + [fwd_a, fwd_b] + passed:
            cp.wait_send()

    out = pl.pallas_call(
        body, out_shape=jax.ShapeDtypeStruct((N_CHIPS, R, C), shard.dtype), in_specs=[_ANY], out_specs=_ANY,
        scratch_shapes=[pltpu.SemaphoreType.DMA((10,)), pltpu.SemaphoreType.DMA((10,))], name=name)(shard)
    chip = 2 * lax.axis_index("x") + lax.axis_index("y")
    return lax.dynamic_update_index_in_dim(out, shard, chip, 0)


_HBM = pl.BlockSpec(memory_space=pltpu.HBM)
_SEM = pl.BlockSpec(memory_space=pltpu.SEMAPHORE)
_EFFECT = pltpu.SideEffectType.DATAFLOW_SIDE_EFFECTING


def _chip_copies(kind, srcs, lands, send_sems, recv_sems):
    x, y, c, chips = _place()
    p = 2 * x + y
    cps = []
    if kind == "sibling":
        for i in range(len(srcs)):
            h = srcs[i].shape[1] // 2
            cps.append(pltpu.make_async_remote_copy(
                src_ref=srcs[i].at[:, pl.ds((1 - c) * h, h), :], dst_ref=lands[i], send_sem=send_sems.at[3 * i],
                recv_sem=recv_sems.at[3 * i], device_id=(x, y, 1 - c), device_id_type=MESH))
        return cps
    for i in range(len(srcs)):
        for j, (cx, cy) in enumerate(chips):
            if kind == "gather":
                src, dst = srcs[i].at[c], lands[i].at[p, c]
            else:
                src, dst = srcs[i].at[2 * cx + cy], lands[i].at[j]
            cps.append(pltpu.make_async_remote_copy(
                src_ref=src, dst_ref=dst, send_sem=send_sems.at[3 * i + j], recv_sem=recv_sems.at[3 * i + j],
                device_id=(cx, cy, c), device_id_type=MESH))
    return cps


def split_start(kind, srcs, land_shapes, after, name):
    n = len(srcs)

    def body(*refs):
        src_refs, land_refs = refs[:n], refs[n:2 * n]
        send_sems, recv_sems = refs[2 * n + 1], refs[2 * n + 2]
        token = refs[-1]
        for cp in _chip_copies(kind, src_refs, land_refs, send_sems, recv_sems):
            cp.start()
        token[...] = jnp.zeros_like(token)

    lands = [pltpu.with_memory_space_constraint(lax.empty(s, BF16), pltpu.HBM) for s in land_shapes]
    outs = pl.pallas_call(
        body, name=name,
        out_shape=(pltpu.SemaphoreType.DMA((3 * n,)), pltpu.SemaphoreType.DMA((3 * n,)),
                   *[pltpu.HBM(s.shape, s.dtype) for s in srcs], *[pltpu.HBM(s, BF16) for s in land_shapes],
                   jax.ShapeDtypeStruct((8, 128), F32)),
        in_specs=[_HBM] * (2 * n) + [_ANY],
        out_specs=(_SEM, _SEM, *([_HBM] * (2 * n)), pl.BlockSpec(memory_space=pltpu.VMEM)),
        input_output_aliases={i: 2 + i for i in range(2 * n)},
        compiler_params=pltpu.CompilerParams(has_side_effects=_EFFECT),
    )(*[pltpu.with_memory_space_constraint(s, pltpu.HBM) for s in srcs], *lands, after)
    return outs[0], outs[1], outs[2:2 + n], outs[2 + n:2 + 2 * n], outs[-1]


def split_wait(kind, send_sems, recv_sems, srcs, lands, after, name):
    n = len(srcs)

    def body(*refs):
        src_refs, land_refs = refs[:n], refs[n:2 * n]
        ssem, rsem = refs[2 * n], refs[2 * n + 1]
        for cp in _chip_copies(kind, src_refs, land_refs, ssem, rsem):
            cp.wait_send()
            cp.wait_recv()

    outs = pl.pallas_call(
        body, name=name,
        out_shape=[pltpu.HBM(s.shape, s.dtype) for s in srcs] + [pltpu.HBM(s.shape, s.dtype) for s in lands],
        in_specs=[_HBM] * (2 * n) + [_SEM, _SEM, _ANY], out_specs=[_HBM] * (2 * n),
        input_output_aliases={i: i for i in range(2 * n)},
        compiler_params=pltpu.CompilerParams(has_side_effects=_EFFECT),
    )(*srcs, *lands, send_sems, recv_sems, after)
    return outs[:n], outs[n:]


def pass_to_sibling(lands):
    n = len(lands)

    def body(*refs):
        ins, outs = refs[:n], refs[n:2 * n]
        send_sems, recv_sems = refs[2 * n:]
        x, y, c, chips = _place()
        cps = []
        for i in range(n):
            for j, (cx, cy) in enumerate(chips):
                blk = outs[i].at[2 * cx + cy, c]
                cps.append(pltpu.make_async_remote_copy(
                    src_ref=ins[i].at[2 * cx + cy, c], dst_ref=blk, send_sem=send_sems.at[3 * i + j],
                    recv_sem=recv_sems.at[3 * i + j], device_id=(x, y, 1 - c), device_id_type=MESH))
        for cp in cps:
            cp.start()
        for cp in cps:
            cp.wait()

    return pl.pallas_call(
        body, out_shape=[jax.ShapeDtypeStruct(t.shape, t.dtype) for t in lands], in_specs=[_ANY] * n,
        out_specs=[_ANY] * n, input_output_aliases={i: i for i in range(n)},
        scratch_shapes=[pltpu.SemaphoreType.DMA((3 * n,)), pltpu.SemaphoreType.DMA((3 * n,))],
        name="ag_pass_to_sibling")(*lands)


def _pass_copies(bufs, send_sems, recv_sems):
    x, y, c, chips = _place()
    cps = []
    for i in range(len(bufs)):
        for j, (cx, cy) in enumerate(chips):
            blk = bufs[i].at[2 * cx + cy, c]
            cps.append(pltpu.make_async_remote_copy(
                src_ref=blk, dst_ref=blk, send_sem=send_sems.at[3 * i + j], recv_sem=recv_sems.at[3 * i + j],
                device_id=(x, y, 1 - c), device_id_type=MESH))
    return cps


def pass_start(bufs, after, name):
    n = len(bufs)

    def body(*refs):
        send_sems, recv_sems = refs[n + 1], refs[n + 2]
        for cp in _pass_copies(refs[:n], send_sems, recv_sems):
            cp.start()
        refs[-1][...] = jnp.zeros_like(refs[-1])

    outs = pl.pallas_call(
        body, name=name,
        out_shape=(pltpu.SemaphoreType.DMA((3 * n,)), pltpu.SemaphoreType.DMA((3 * n,)),
                   *[pltpu.HBM(b.shape, b.dtype) for b in bufs], jax.ShapeDtypeStruct((8, 128), F32)),
        in_specs=[_HBM] * n + [_ANY],
        out_specs=(_SEM, _SEM, *([_HBM] * n), pl.BlockSpec(memory_space=pltpu.VMEM)),
        input_output_aliases={i: 2 + i for i in range(n)},
        compiler_params=pltpu.CompilerParams(has_side_effects=_EFFECT),
    )(*[pltpu.with_memory_space_constraint(b, pltpu.HBM) for b in bufs], after)
    return outs[0], outs[1], outs[2:2 + n], outs[-1]


def pass_wait(send_sems, recv_sems, bufs, after, name):
    n = len(bufs)

    def body(*refs):
        for cp in _pass_copies(refs[:n], refs[n], refs[n + 1]):
            cp.wait_send()
            cp.wait_recv()

    return pl.pallas_call(
        body, name=name, out_shape=[pltpu.HBM(b.shape, b.dtype) for b in bufs],
        in_specs=[_HBM] * n + [_SEM, _SEM, _ANY], out_specs=[_HBM] * n,
        input_output_aliases={i: i for i in range(n)},
        compiler_params=pltpu.CompilerParams(has_side_effects=_EFFECT),
    )(*bufs, send_sems, recv_sems, after)


def exchange_halves_to_sibling(gs, name, by_cols=False):
    n = len(gs)

    def body(*refs):
        ins, outs = refs[:n], refs[n:2 * n]
        send_sems, recv_sems = refs[2 * n:]
        x, y, c, _ = _place()
        cps = []
        for i in range(n):
            if by_cols:
                hc = ins[i].shape[2] // 2
                src = ins[i].at[:, :, pl.ds(pl.multiple_of((1 - c) * hc, 128), hc)]
            else:
                h = ins[i].shape[1] // 2
                src = ins[i].at[:, pl.ds((1 - c) * h, h), :]
            cps.append(pltpu.make_async_remote_copy(
                src_ref=src, dst_ref=outs[i],
                send_sem=send_sems.at[i], recv_sem=recv_sems.at[i], device_id=(x, y, 1 - c), device_id_type=MESH))
        for cp in cps:
            cp.start()
        for cp in cps:
            cp.wait()

    halve = (lambda s: (s[0], s[1], s[2] // 2)) if by_cols else (lambda s: (s[0], s[1] // 2, s[2]))
    return pl.pallas_call(
        body, out_shape=[jax.ShapeDtypeStruct(halve(g.shape), g.dtype) for g in gs],
        in_specs=[_ANY] * n, out_specs=[_ANY] * n,
        scratch_shapes=[pltpu.SemaphoreType.DMA((n,)), pltpu.SemaphoreType.DMA((n,))],
        name=name)(*gs)


def join_halves(rs, name):
    n = len(rs)

    def body(*refs):
        ins, outs = refs[:n], refs[n:2 * n]
        send_sems, recv_sems = refs[2 * n:]
        x, y, c, _ = _place()
        cps = [pltpu.make_async_remote_copy(
            src_ref=ins[i], dst_ref=outs[i], send_sem=send_sems.at[i], recv_sem=recv_sems.at[i],
            device_id=(x, y, 1 - c), device_id_type=MESH) for i in range(n)]
        for cp in cps:
            cp.start()
        for cp in cps:
            cp.wait()

    return pl.pallas_call(
        body, out_shape=[jax.ShapeDtypeStruct(r.shape, r.dtype) for r in rs],
        in_specs=[_ANY] * n, out_specs=[_ANY] * n,
        scratch_shapes=[pltpu.SemaphoreType.DMA((n,)), pltpu.SemaphoreType.DMA((n,))],
        name=name)(*rs)


def _pack(parts, row_mult=8):
    flat = jnp.concatenate([p.reshape(-1).astype(F32) for p in parts])
    unit = row_mult * 128
    n = -(-flat.shape[0] // unit) * unit
    return jnp.pad(flat, (0, n - flat.shape[0])).reshape(n // 128, 128)


def _unpack(flat, shapes):
    out, off = [], 0
    for s in shapes:
        n = int(np.prod(s))
        out.append(flat[off:off + n].reshape(s))
        off += n
    return out


def _gather_packed(parts, name):
    packed = _pack(parts)
    g = allgather_small(packed, name).reshape(N_DEV, -1)
    return _unpack_rows(g, [p.shape for p in parts])


def _unpack_rows(g, shapes):
    out, off = [], 0
    for s in shapes:
        n = int(np.prod(s))
        out.append(g[:, off:off + n].reshape((g.shape[0],) + tuple(s)))
        off += n
    return out


def _by_chip(t, axis):
    return jnp.concatenate([t[2 * p] for p in range(N_CHIPS)], axis=axis)


def kernel(x, c, ada_w, ada_b, ln_g, ln_b, a_in_w, a_conv_w, a_conv_b, a_dt_bias, a_A_log, a_D, a_norm_g, a_out_w, kv_w, b_in_w, b_out_w, loss_target, m_ada_w, m_ada_b, m_ln_g, m_ln_b, m_a_in_w, m_a_conv_w, m_a_conv_b, m_a_dt_bias, m_a_A_log, m_a_D, m_a_norm_g, m_a_out_w, m_kv_w, m_b_in_w, m_b_out_w, v_ada_w, v_ada_b, v_ln_g, v_ln_b, v_a_in_w, v_a_conv_w, v_a_conv_b, v_a_dt_bias, v_a_A_log, v_a_D, v_a_norm_g, v_a_out_w, v_kv_w, v_b_in_w, v_b_out_w):
    ax, ay, ac = lax.axis_index("x"), lax.axis_index("y"), lax.axis_index("c")
    chip = 2 * ax + ay
    dev = 4 * ax + 2 * ay + ac
    xin = x[0]
    tgt = loss_target[0]
    L, D = xin.shape
    G, P = SSD_G, SSD_P
    H = a_dt_bias.shape[1]
    Kh = H // G
    DI = H * P
    CONVD = a_conv_b.shape[1] * N_CHIPS
    HW = DIL_H * DIL_E
    Ws = ada_w.shape[2]

    w_in_g = allgather_routed(jnp.transpose(a_in_w[0]).astype(BF16), "allgather_w_in")
    later = [a_out_w[0].astype(BF16), kv_w.astype(BF16), b_in_w[0].astype(BF16), b_out_w[0].astype(BF16)]
    later_split = [s.reshape(2, s.shape[0] // 2, s.shape[1]) for s in later]
    ag_ssem, ag_rsem, ag_srcs, ag_lands, ag_token = split_start(
        "gather", later_split, [(N_CHIPS,) + s.shape for s in later_split], w_in_g, "ag_later_start")
    w_in_t = w_in_g.reshape(-1, D)
    w_dt_t = jnp.pad(w_in_t[DI + CONVD:], ((0, 128 - H), (0, 0)))

    c8, cw8, cb8, ng8 = _gather_packed([c[0], a_conv_w[0], a_conv_b[0], a_norm_g[0]], "allgather_small_params")
    conv_w = _by_chip(cw8, 1)
    conv_b = _by_chip(cb8, 0).reshape(1, CONVD)
    norm_g = _by_chip(ng8, 0).reshape(1, DI)

    mod_s = ada_fwd(c8, ada_w)
    (mod8,) = _gather_packed([mod_s], "allgather_small_mod")
    mods = _by_chip(mod8, 2)
    mod = lax.dynamic_index_in_dim(mods, dev, axis=1, keepdims=False) + ada_b
    shift = [mod[l:l + 1, :D] for l in range(DEPTH)]
    scale = [mod[l:l + 1, D:2 * D] for l in range(DEPTH)]
    gate = [mod[l:l + 1, 2 * D:] for l in range(DEPTH)]
    lg = [ln_g[l:l + 1] for l in range(DEPTH)]
    lb = [ln_b[l:l + 1] for l in range(DEPTH)]

    h0 = modulate(xin, scale[0] + ag_token[0:1, 0:1], shift[0], "modulate0")
    zx = mm_nt(h0, w_in_t, BF16, "mm_in_zx", kw_rows=DI + CONVD)
    dtp = mm_nt(h0, w_dt_t, F32, "mm_in_dt")
    xbc = conv_fwd(zx, DI, conv_w, conv_b)
    dtp_g = jnp.transpose(dtp[:, :H].reshape(L, G, Kh), (1, 0, 2))
    dtp_gT = jnp.transpose(dtp_g, (0, 2, 1))
    vecs = [a_dt_bias.reshape(G, 1, Kh), a_dt_bias.reshape(G, Kh, 1), a_A_log.reshape(G, 1, Kh),
            a_A_log.reshape(G, Kh, 1), a_D.reshape(G, 1, Kh), a_D.reshape(G, Kh, 1)]
    y_ssd, states = ssd_fwd(xbc, dtp_g, dtp_gT, *vecs, DI)
    yn = rms_gate_fwd(y_ssd, zx, norm_g)
    later_split, ag_lands = split_wait("gather", ag_ssem, ag_rsem, ag_srcs, ag_lands, yn, "ag_later_wait")
    (land_out,) = pass_to_sibling(ag_lands[:1])
    ps_ssem, ps_rsem, lands_b, ps_token = pass_start(ag_lands[1:], land_out, "ag_pass_start")

    def place_own(o, s, full):
        return lax.dynamic_update_index_in_dim(o, s, chip, 0).reshape((N_CHIPS,) + full.shape)

    w_out_g = place_own(land_out, later_split[0], later[0])
    ymix0 = mm_nn(yn, w_out_g.reshape(-1, D), F32, "mm_out_a", after=ps_token)
    x1, x1b, h1 = ln_mid(xin, ymix0, gate[0], lg[0], lb[0], scale[1], shift[1])
    lands_b = pass_wait(ps_ssem, ps_rsem, lands_b, x1b, "ag_pass_wait")
    w_kv_g, w_bin_g, w_bout_g = [place_own(o, s, full) for o, s, full in zip(lands_b, later_split[1:], later[1:])]

    n_grp = len(DIL_PATTERNS)
    cb = HW // 512
    assert w_bin_g.shape[2] == HW
    kv3 = [mm_cols_dilated(x1b, w_kv_g, [g * cb + t for t in range(cb)] + [(n_grp + g) * cb + t for t in range(cb)],
                           DIL_PATTERNS[g][1], f"mm_kv_{g}") for g in range(n_grp)]
    q3 = [mm_cols_dilated(h1, w_bin_g, [g], DIL_PATTERNS[g][1], f"mm_q_{g}", tn=HW) for g in range(n_grp)]
    z_b = mm_nn(h1, w_bin_g[n_grp], BF16, "mm_z_b")
    os_, lses = [], []
    for gi in range(len(DIL_PATTERNS)):
        o, lse = attn_fwd(q3[gi], kv3[gi], gi)
        os_.append(o)
        lses.append(lse)
    om = merge_fwd(os_, lses, z_b)
    ymix1 = mm_nn(om, w_bout_g, F32, "mm_out_b", stack="col")
    dres2, dy2, dg1, db1, dgate1, sq = ln_final_fwd_bwd(x1, ymix1, gate[1], lg[1], lb[1], tgt)
    loss_part = 0.5 * jnp.sum(sq) / D

    g_bout = mm_tn(om, dy2, BF16, "mm_gw_out_b", stack="col")
    dgated = mm_nt(dy2, w_bout_g, BF16, "mm_gx_out_b", stack="col")
    dos, dprs, dz_b = merge_bwd(dgated, os_, lses, z_b)
    dqs, dks, dvs = [], [], []
    for gi in range(len(DIL_PATTERNS)):
        dq, dk, dv = attn_bwd(q3[gi], kv3[gi], dos[gi], lses[gi], dprs[gi], gi)
        dqs.append(dq)
        dks.append(dk)
        dvs.append(dv)
    dqz = jnp.concatenate(dqs + [dz_b], axis=1)
    dkv = jnp.concatenate(dks + dvs, axis=1)
    g_bin = mm_tn(h1, dqz, BF16, "mm_gw_in_b", stack="col")
    dh1 = mm_nt(dqz, w_bin_g, BF16, "mm_gx_in_b", stack="col")
    g_kv = mm_tn(x1b, dkv, BF16, "mm_gw_kv", stack="col")

    core = ac.astype(jnp.int32).reshape(1)
    chip_i = chip.astype(jnp.int32).reshape(1)

    def begin_exchange(gs, tag):
        shapes = [(g.shape[0], g.shape[1] // 2, g.shape[2]) for g in gs]
        return split_start("sibling", gs, shapes, gs[0], "rs_x%s_start" % tag)

    def begin_scatter(gs, nms, tag, exchange=None, after=None, by_cols=False):
        if exchange is None:
            sib = exchange_halves_to_sibling(gs, "rs_sibling_exchange_" + tag, by_cols=by_cols)
        else:
            gs, sib = split_wait("sibling", exchange[0], exchange[1], exchange[2], exchange[3], after,
                                 "rs_x%s_wait" % tag)
        parts = [add_half(g, a, core, "rs_add_" + nm, by_cols=by_cols) for g, a, nm in zip(gs, sib, nms)]
        return split_start("scatter", parts, [(3,) + t.shape[1:] for t in parts], parts[0], "rs_%s_start" % tag)

    def finish_scatter(handles, after, tag):
        nms, owns, landed = [], [], []
        for k, (handle, hn) in enumerate(handles):
            parts, lands = split_wait("scatter", handle[0], handle[1], handle[2], handle[3], after,
                                      "rs_%s%d_wait" % (tag, k))
            nms += hn
            owns += list(parts)
            landed += list(lands)
        halves = [sum_partials(own, t, chip_i, "rs_sum_" + nm) for own, t, nm in zip(owns, landed, nms)]
        theirs = join_halves(halves, "rs_join_halves_" + tag)
        return dict(zip(nms, zip(halves, theirs)))

    names_b = ["kv", "in_b", "out_b"]
    ex_b = begin_exchange([g_kv, g_bin, g_bout], "b")
    dx1_kv = mm_nt(dkv, w_kv_g, BF16, "mm_gx_kv", stack="col", after=ex_b[4])
    rs_b = begin_scatter(None, names_b, "b", exchange=ex_b, after=dx1_kv)

    dres1, dy1, dg0, db0, dgate0, dscale1, dshift1 = mod_ln_bwd(
        dres2, dh1, dx1_kv, x1, scale[1], xin, ymix0, gate[0] + rs_b[4][0:1, 0:1], lg[0])
    g_out = mm_tn(yn, dy1, BF16, "mm_gw_out_a", stack="row")
    ex_a1 = begin_exchange([g_out], "a1")
    dyn = mm_nt(dy1, w_out_g, BF16, "mm_gx_out_a", stack="row", after=ex_a1[4])
    rs_a1 = begin_scatter(None, ["out_a"], "a1", exchange=ex_a1, after=dyn)
    dxs, dB, dC, ddtp_g, dbias_g, dalog_g, dD_g, dz_a, dnorm_g = ssd_bwd(
        xbc, dtp_g, dtp_gT, *vecs, states, dyn, y_ssd, zx, norm_g + rs_a1[4][0:1, 0:1], DI)
    dxbc = jnp.concatenate([dxs, dB, dC], axis=1)
    dxbc_pre, dconv_w, dconv_b = conv_bwd(zx, DI, conv_w, conv_b, dxbc)
    dzx = jnp.concatenate([dz_a, dxbc_pre], axis=1)
    ddtp = jnp.pad(jnp.transpose(ddtp_g, (1, 0, 2)).reshape(L, H), ((0, 0), (0, 128 - H)))
    g_inT = mm_tn(dzx, h0, BF16, "mm_gw_in_zx", m_rows=DI + CONVD + H)
    g_dtT = mm_tn(ddtp, h0, BF16, "mm_gw_in_dt")
    g_inT = lax.dynamic_update_slice(g_inT, g_dtT[:H], (DI + CONVD, 0))
    rs_a2 = begin_scatter([g_inT.reshape(N_CHIPS, -1, D)], ["in_a"], "a2", by_cols=True)
    dh0 = mm_nn(dzx, w_in_t, BF16, "mm_gx_in_zx", after=rs_a2[4])
    dh0_dt = mm_nn(ddtp, w_dt_t, F32, "mm_gx_in_dt")
    grad_x, dscale0, dshift0 = mod_bwd(dres1, dh0, dh0_dt, xin, scale[0] + rs_a2[4][0:1, 0:1], "mod_bwd0")
    g_halves = finish_scatter([(rs_b, names_b)], grad_x, "b")

    def step_halves(w, m, v, nm):
        shp = w.shape
        mine, theirs_ = g_halves[nm]
        outs4 = adamw_halves(w.reshape(-1, shp[-1]), mine, theirs_, m.reshape(-1, shp[-1]), v.reshape(-1, shp[-1]),
                             core, "adamw_" + nm)
        return tuple(t.reshape(shp) for t in outs4)

    big = {
        "kv_w": step_halves(kv_w, m_kv_w, v_kv_w, "kv"),
        "b_in_w": step_halves(b_in_w, m_b_in_w, v_b_in_w, "in_b"),
        "b_out_w": step_halves(b_out_w, m_b_out_w, v_b_out_w, "out_b"),
    }
    g_halves.update(finish_scatter([(rs_a1, ["out_a"]), (rs_a2, ["in_a"])], big["kv_w"][1], "a"))
    g_halves["in_a"] = tuple(jnp.transpose(t) for t in g_halves["in_a"])
    big["a_in_w"] = step_halves(a_in_w, m_a_in_w, v_a_in_w, "in_a")
    big["a_out_w"] = step_halves(a_out_w, m_a_out_w, v_a_out_w, "out_a")

    dmod = jnp.concatenate([jnp.concatenate([dshift0, dscale0, dgate0], axis=1),
                            jnp.concatenate([dshift1, dscale1, dgate1], axis=1)], axis=0)
    small_parts = [jnp.concatenate([dg0, dg1], axis=0), jnp.concatenate([db0, db1], axis=0),
                   dbias_g.reshape(1, H), dalog_g.reshape(1, H), dD_g.reshape(1, H),
                   dconv_w, dconv_b, dnorm_g, loss_part.reshape(1, 1)]
    small_shapes = [p.shape for p in small_parts]
    packed = jnp.concatenate([_pack([dmod]), _pack(small_parts)], axis=0)
    n_mod_rows = _pack([dmod]).shape[0]
    gathered = allgather_small(packed, "allgather_small_grads", after=g_halves["in_a"][1]).reshape(N_DEV, -1, 128)
    dmod8 = gathered[:, :n_mod_rows].reshape(N_DEV, -1)[:, :2 * 3 * D].reshape(N_DEV, DEPTH, 3 * D)
    summed = sum_leading(gathered, "sum_small")
    g_ada_b = summed[:n_mod_rows].reshape(-1)[:2 * 3 * D].reshape(DEPTH, 3 * D)
    (g_ln_g, g_ln_b, g_dt_bias, g_a_log, g_dsk, g_conv_w, g_conv_b, g_norm_g, loss_all) = _unpack(
        summed[n_mod_rows:].reshape(-1), small_shapes)
    loss = loss_all.reshape(())
    Cs = CONVD // N_CHIPS
    g_conv_w_s = lax.dynamic_slice_in_dim(g_conv_w, chip * Cs, Cs, axis=1)
    g_conv_b_s = lax.dynamic_slice_in_dim(g_conv_b, chip * Cs, Cs, axis=1)
    g_norm_g_s = lax.dynamic_slice_in_dim(g_norm_g, chip * (DI // N_CHIPS), DI // N_CHIPS, axis=1)
    dmod_s = jnp.transpose(lax.dynamic_slice_in_dim(dmod8, chip * Ws, Ws, axis=2), (1, 0, 2))

    def step2d(w, g, m, v, nm):
        shp = w.shape
        d_, m_, v_ = adamw(w.reshape(-1, shp[-1]), g.reshape(-1, shp[-1]), m.reshape(-1, shp[-1]),
                           v.reshape(-1, shp[-1]), "adamw_" + nm)
        return g.reshape(shp), d_.reshape(shp), m_.reshape(shp), v_.reshape(shp)

    big["ada_w"] = step2d(ada_w, ada_wgrad(jnp.transpose(c8), dmod_s), m_ada_w, v_ada_w, "ada_w")
    small_names = ["ada_b", "ln_g", "ln_b", "a_conv_w", "a_conv_b", "a_dt_bias", "a_A_log", "a_D", "a_norm_g"]
    small_w = [ada_b, ln_g, ln_b, a_conv_w, a_conv_b, a_dt_bias, a_A_log, a_D, a_norm_g]
    small_m = [m_ada_b, m_ln_g, m_ln_b, m_a_conv_w, m_a_conv_b, m_a_dt_bias, m_a_A_log, m_a_D, m_a_norm_g]
    small_v = [v_ada_b, v_ln_g, v_ln_b, v_a_conv_w, v_a_conv_b, v_a_dt_bias, v_a_A_log, v_a_D, v_a_norm_g]
    small_g = [g_ada_b, g_ln_g, g_ln_b, g_conv_w_s, g_conv_b_s, g_dt_bias, g_a_log, g_dsk, g_norm_g_s]
    shapes = [w.shape for w in small_w]
    small_g = [g.reshape(s) for g, s in zip(small_g, shapes)]
    d_p, m_p, v_p = adamw(_pack(small_w), _pack(small_g), _pack(small_m), _pack(small_v), "adamw_small")
    small = {}
    for nm, g, d_, m_, v_ in zip(small_names, small_g, _unpack(d_p.reshape(-1), shapes), _unpack(m_p.reshape(-1), shapes),
                                 _unpack(v_p.reshape(-1), shapes)):
        small[nm] = (g, d_, m_, v_)
    allw = {**big, **small}
    order = ["ada_w", "ada_b", "ln_g", "ln_b", "a_in_w", "a_conv_w", "a_conv_b", "a_dt_bias", "a_A_log", "a_D",
             "a_norm_g", "a_out_w", "kv_w", "b_in_w", "b_out_w"]
    outs = [loss, grad_x.reshape(x.shape)]
    for k in range(4):
        outs += [allw[n][k] for n in order]
    return tuple(outs)
```

```python
import functools

import jax
import jax.numpy as jnp
import numpy as np
from jax import lax
from jax.experimental import pallas as pl
from jax.experimental.pallas import tpu as pltpu

F32 = jnp.float32
BF16 = jnp.bfloat16
MESH = pl.DeviceIdType.MESH

DEPTH = 2
ALPHA = (2 * DEPTH) ** 0.25
LN_EPS = 1e-5
RMS_EPS = 1e-5
SSD_P = 64
SSD_N = 128
SSD_Q = 256
SSD_G = 8
CONV_W = 4
DIL_PATTERNS = ((128, 1), (512, 4), (2048, 16))
DIL_H = 8
DIL_E = 128
DIL_BLK = 128
ADAM_LR, ADAM_B1, ADAM_B2, ADAM_EPS, ADAM_WD, ADAM_STEP = 0.001, 0.9, 0.999, 1e-08, 0.01, 10

VMEM_LIMIT = 56 * 1024 * 1024
N_CHIPS = 4
N_DEV = 8


def _tile(dim, target, mult=128):
    if dim <= target:
        return dim
    t = (target // mult) * mult
    while t >= mult:
        if dim % t == 0:
            return t
        t -= mult
    return dim


def _cp(sem):
    return pltpu.CompilerParams(dimension_semantics=sem, vmem_limit_bytes=VMEM_LIMIT)


def _sigmoid(x):
    return 1.0 / (1.0 + jnp.exp(-x))


def _silu(x):
    return x * _sigmoid(x)


def _dsilu(x):
    s = _sigmoid(x)
    return s * (1.0 + x * (1.0 - s))


def _softplus(x):
    return jnp.maximum(x, 0.0) + jnp.log(1.0 + jnp.exp(-jnp.abs(x)))


def _mm_call(a, b, out_shape, grid, a_spec, b_spec, o_spec, acc_shape, dims, name, after=None):
    nk = grid[2]
    extra = [] if after is None else [after]

    def prod(a_ref, b_ref):
        return lax.dot_general(a_ref[...].astype(BF16), b_ref[...].astype(BF16), (dims, ((), ())),
                               preferred_element_type=F32)

    def body_single(a_ref, b_ref, *rest):
        o_ref = rest[len(extra)]
        o_ref[...] = prod(a_ref, b_ref).astype(o_ref.dtype)

    def body_multi(a_ref, b_ref, *rest):
        o_ref, acc_ref = rest[len(extra):]
        k = pl.program_id(2)

        @pl.when(k == 0)
        def _():
            acc_ref[...] = prod(a_ref, b_ref)

        @pl.when(jnp.logical_and(k > 0, k < nk - 1))
        def _():
            acc_ref[...] += prod(a_ref, b_ref)

        @pl.when(k == nk - 1)
        def _():
            o_ref[...] = (acc_ref[...] + prod(a_ref, b_ref)).astype(o_ref.dtype)

    return pl.pallas_call(
        body_single if nk == 1 else body_multi, grid=grid, in_specs=[a_spec, b_spec] + [_ANY] * len(extra),
        out_specs=o_spec, out_shape=out_shape, scratch_shapes=[] if nk == 1 else [pltpu.VMEM(acc_shape, F32)],
        compiler_params=_cp(("parallel", "parallel", "arbitrary")), name=name)(a, b, *extra)


def mm_nn(a, b, out_dtype, name, stack=None, tm=1024, tn=1024, tk=2048, n_cols=None, after=None):
    M, K = a.shape
    if stack is None:
        N = b.shape[1] if n_cols is None else n_cols
        tn, tk = _tile(N, tn), _tile(K, tk)
        b_spec = pl.BlockSpec((tk, tn), lambda i, j, k: (k, j))
    elif stack == "col":
        S, _, Ns = b.shape
        N = S * Ns
        tn, tk = _tile(Ns, tn), _tile(K, tk)
        npb = Ns // tn
        b_spec = pl.BlockSpec((None, tk, tn), lambda i, j, k: (j // npb, k, j % npb))
    else:
        S, Ks, N = b.shape
        tn, tk = _tile(N, tn), _tile(Ks, tk)
        kpb = Ks // tk
        b_spec = pl.BlockSpec((None, tk, tn), lambda i, j, k: (k // kpb, k % kpb, j))
    tm = _tile(M, tm)
    return _mm_call(a, b, jax.ShapeDtypeStruct((M, N), out_dtype), (M // tm, N // tn, K // tk),
                    pl.BlockSpec((tm, tk), lambda i, j, k: (i, k)), b_spec,
                    pl.BlockSpec((tm, tn), lambda i, j, k: (i, j)), (tm, tn), ((1,), (0,)), name, after=after)


def mm_cols_dilated(a, b, gcols, d, name, tm=1024, tn=512):
    L, K = a.shape
    S, _, Ns = b.shape
    tm, tn = _tile(L, tm), _tile(Ns, tn)
    npb = Ns // tn
    nj = len(gcols)
    rows = tm // d

    def body(cols_ref, a_ref, b_ref, o_ref, *scr):
        prod = jnp.dot(a_ref[...], b_ref[...], preferred_element_type=F32)
        if d == 1:
            o_ref[0] = prod.astype(BF16)
        else:
            for c in range(tn // 128):
                scr[0][c] = prod[:, c * 128:(c + 1) * 128]
            for r in range(d):
                for c in range(tn // 128):
                    o_ref[r, :, c * 128:(c + 1) * 128] = scr[0].at[c][pl.ds(r, rows, stride=d), :].astype(BF16)

    return pl.pallas_call(
        body,
        grid_spec=pltpu.PrefetchScalarGridSpec(
            num_scalar_prefetch=1, grid=(L // tm, nj),
            in_specs=[pl.BlockSpec((tm, K), lambda i, j, c: (i, 0)),
                      pl.BlockSpec((None, K, tn), lambda i, j, c: (c[j] // npb, 0, c[j] % npb))],
            out_specs=pl.BlockSpec((d, rows, tn), lambda i, j, c: (0, i, j)),
            scratch_shapes=[] if d == 1 else [pltpu.VMEM((tn // 128, tm, 128), F32)]),
        out_shape=jax.ShapeDtypeStruct((d, L // d, nj * tn), BF16),
        compiler_params=_cp(("parallel", "arbitrary")), name=name)(jnp.asarray(gcols, jnp.int32), a, b)


def mm_nt(a, b, out_dtype, name, stack=None, tm=1024, tn=1024, tk=2048, after=None, kw_rows=None):
    M, C = a.shape
    if stack is None:
        Kw = b.shape[0] if kw_rows is None else kw_rows
        tn, tk = _tile(Kw, tn), _tile(C, tk)
        b_spec = pl.BlockSpec((tn, tk), lambda i, j, k: (j, k))
    elif stack == "col":
        S, Kw, Cs = b.shape
        tn, tk = _tile(Kw, tn), _tile(Cs, tk)
        cpb = Cs // tk
        b_spec = pl.BlockSpec((None, tn, tk), lambda i, j, k: (k // cpb, j, k % cpb))
    else:
        S, Ks, _ = b.shape
        Kw = S * Ks
        tn, tk = _tile(Ks, tn), _tile(C, tk)
        jpb = Ks // tn
        b_spec = pl.BlockSpec((None, tn, tk), lambda i, j, k: (j // jpb, j % jpb, k))
    tm = _tile(M, tm)
    return _mm_call(a, b, jax.ShapeDtypeStruct((M, Kw), out_dtype), (M // tm, Kw // tn, C // tk),
                    pl.BlockSpec((tm, tk), lambda i, j, k: (i, k)), b_spec,
                    pl.BlockSpec((tm, tn), lambda i, j, k: (i, j)), (tm, tn), ((1,), (1,)), name, after=after)


def mm_tn(a, b, out_dtype, name, stack=None, n_stack=N_CHIPS, tm=1024, tn=1024, tk=2048, m_rows=None):
    L, M = a.shape
    N = b.shape[1]
    tk = _tile(L, tk)
    if stack is None:
        tm, tn = _tile(M, tm), _tile(N, tn)
        o_spec = pl.BlockSpec((tm, tn), lambda i, j, k: (i, j))
        out_shape = (M if m_rows is None else m_rows, N)
    elif stack == "col":
        Ns = N // n_stack
        tm, tn = _tile(M, tm), _tile(Ns, tn)
        npb = Ns // tn
        o_spec = pl.BlockSpec((None, tm, tn), lambda i, j, k: (j // npb, i, j % npb))
        out_shape = (n_stack, M, Ns)
    else:
        Ms = M // n_stack
        tm, tn = _tile(Ms, tm), _tile(N, tn)
        mpb = Ms // tm
        o_spec = pl.BlockSpec((None, tm, tn), lambda i, j, k: (i // mpb, i % mpb, j))
        out_shape = (n_stack, Ms, N)
    return _mm_call(a, b, jax.ShapeDtypeStruct(out_shape, out_dtype), (M // tm, N // tn, L // tk),
                    pl.BlockSpec((tk, tm), lambda i, j, k: (k, i)), pl.BlockSpec((tk, tn), lambda i, j, k: (k, j)),
                    o_spec, (tm, tn), ((0,), (0,)), name)


def _row_specs(tr, widths):
    return [pl.BlockSpec((tr, w), lambda i: (i, 0)) for w in widths]


def _vec_spec(w):
    return pl.BlockSpec((1, w), lambda i: (0, 0))


def _acc_rows(ref, val, i):
    s = jnp.sum(val, axis=0, keepdims=True)

    @pl.when(i == 0)
    def _():
        ref[...] = s

    @pl.when(i > 0)
    def _():
        ref[...] += s


def modulate(x, scale, shift, name):
    L, D = x.shape
    tr = _tile(L, 512, 16)

    def body(x_ref, sc_ref, sh_ref, h_ref):
        h_ref[...] = (x_ref[...] * (1.0 + sc_ref[...]) + sh_ref[...]).astype(BF16)

    return pl.pallas_call(
        body, grid=(L // tr,), in_specs=_row_specs(tr, [D]) + [_vec_spec(D)] * 2, out_specs=_row_specs(tr, [D])[0],
        out_shape=jax.ShapeDtypeStruct((L, D), BF16), compiler_params=_cp(("parallel",)), name=name)(x, scale, shift)


def _ln_core(x, y, gate, g, b):
    u = ALPHA * x + (1.0 + gate) * y
    mu = jnp.mean(u, axis=-1, keepdims=True)
    d = u - mu
    var = jnp.mean(d * d, axis=-1, keepdims=True)
    rstd = lax.rsqrt(var + LN_EPS)
    xhat = d * rstd
    return xhat * g + b, xhat, rstd


def ln_mid(x, y, gate, g, b, scale, shift):
    L, D = x.shape
    tr = _tile(L, 256, 16)

    def body(x_ref, y_ref, gate_ref, g_ref, b_ref, sc_ref, sh_ref, x1_ref, x1b_ref, h_ref):
        x1, _, _ = _ln_core(x_ref[...], y_ref[...], gate_ref[...], g_ref[...], b_ref[...])
        x1_ref[...] = x1
        x1b_ref[...] = x1.astype(BF16)
        h_ref[...] = (x1 * (1.0 + sc_ref[...]) + sh_ref[...]).astype(BF16)

    return pl.pallas_call(
        body, grid=(L // tr,), in_specs=_row_specs(tr, [D, D]) + [_vec_spec(D)] * 5,
        out_specs=_row_specs(tr, [D, D, D]),
        out_shape=[jax.ShapeDtypeStruct((L, D), F32), jax.ShapeDtypeStruct((L, D), BF16),
                   jax.ShapeDtypeStruct((L, D), BF16)],
        compiler_params=_cp(("parallel",)), name="ln_mid")(x, y, gate, g, b, scale, shift)


def _ln_bwd_rows(dout_v, xhat, rstd, g):
    dxh = dout_v * g
    m1 = jnp.mean(dxh, axis=-1, keepdims=True)
    m2 = jnp.mean(dxh * xhat, axis=-1, keepdims=True)
    return rstd * (dxh - m1 - xhat * m2)


def ln_final_fwd_bwd(x, y, gate, g, b, target):
    L, D = x.shape
    tr = _tile(L, 256, 16)

    def body(x_ref, y_ref, gate_ref, g_ref, b_ref, t_ref, dres_ref, dy_ref, dg_ref, db_ref, dgate_ref, sq_ref):
        i = pl.program_id(0)
        yv = y_ref[...]
        out, xhat, rstd = _ln_core(x_ref[...], yv, gate_ref[...], g_ref[...], b_ref[...])
        err = out - t_ref[...]
        dout_v = err * (1.0 / D)
        du = _ln_bwd_rows(dout_v, xhat, rstd, g_ref[...])
        dres_ref[...] = ALPHA * du
        dy_ref[...] = ((1.0 + gate_ref[...]) * du).astype(BF16)
        _acc_rows(dg_ref, dout_v * xhat, i)
        _acc_rows(db_ref, dout_v, i)
        _acc_rows(dgate_ref, du * yv, i)
        _acc_rows(sq_ref, err * err, i)

    return pl.pallas_call(
        body, grid=(L // tr,), in_specs=_row_specs(tr, [D, D]) + [_vec_spec(D)] * 3 + _row_specs(tr, [D]),
        out_specs=_row_specs(tr, [D, D]) + [_vec_spec(D)] * 4,
        out_shape=[jax.ShapeDtypeStruct((L, D), F32), jax.ShapeDtypeStruct((L, D), BF16)]
        + [jax.ShapeDtypeStruct((1, D), F32)] * 4,
        compiler_params=_cp(("arbitrary",)), name="ln_final_fwd_bwd")(x, y, gate, g, b, target)


def mod_ln_bwd(dres_in, dh, dskip, xmid, scale, x, y, gate, g):
    L, D = x.shape
    tr = _tile(L, 256, 16)

    def body(dres_ref, dh_ref, dskip_ref, xm_ref, sc_ref, x_ref, y_ref, gate_ref, g_ref,
             dres_out, dy_ref, dg_ref, db_ref, dgate_ref, dsc_ref, dsh_ref):
        i = pl.program_id(0)
        dh_v = dh_ref[...].astype(F32)
        dout_v = dres_ref[...] + dskip_ref[...].astype(F32) + dh_v * (1.0 + sc_ref[...])
        _acc_rows(dsc_ref, dh_v * xm_ref[...], i)
        _acc_rows(dsh_ref, dh_v, i)
        yv = y_ref[...]
        _, xhat, rstd = _ln_core(x_ref[...], yv, gate_ref[...], g_ref[...], 0.0)
        du = _ln_bwd_rows(dout_v, xhat, rstd, g_ref[...])
        dres_out[...] = ALPHA * du
        dy_ref[...] = ((1.0 + gate_ref[...]) * du).astype(BF16)
        _acc_rows(dg_ref, dout_v * xhat, i)
        _acc_rows(db_ref, dout_v, i)
        _acc_rows(dgate_ref, du * yv, i)

    return pl.pallas_call(
        body, grid=(L // tr,),
        in_specs=_row_specs(tr, [D] * 4) + [_vec_spec(D)] + _row_specs(tr, [D, D]) + [_vec_spec(D)] * 2,
        out_specs=_row_specs(tr, [D, D]) + [_vec_spec(D)] * 5,
        out_shape=[jax.ShapeDtypeStruct((L, D), F32), jax.ShapeDtypeStruct((L, D), BF16)]
        + [jax.ShapeDtypeStruct((1, D), F32)] * 5,
        compiler_params=_cp(("arbitrary",)), name="mod_ln_bwd")(dres_in, dh, dskip, xmid, scale, x, y, gate, g)


def mod_bwd(dres, dh, dh2, xin, scale, name):
    L, D = xin.shape
    tr = _tile(L, 256, 16)

    def body(dres_ref, dh_ref, dh2_ref, x_ref, sc_ref, dx_ref, dsc_ref, dsh_ref):
        i = pl.program_id(0)
        dh_v = dh_ref[...].astype(F32) + dh2_ref[...].astype(F32)
        dx_ref[...] = dres_ref[...] + dh_v * (1.0 + sc_ref[...])
        _acc_rows(dsc_ref, dh_v * x_ref[...], i)
        _acc_rows(dsh_ref, dh_v, i)

    return pl.pallas_call(
        body, grid=(L // tr,), in_specs=_row_specs(tr, [D, D, D, D]) + [_vec_spec(D)],
        out_specs=_row_specs(tr, [D]) + [_vec_spec(D)] * 2,
        out_shape=[jax.ShapeDtypeStruct((L, D), F32)] + [jax.ShapeDtypeStruct((1, D), F32)] * 2,
        compiler_params=_cp(("arbitrary",)), name=name)(dres, dh, dh2, xin, scale)


CONV_HALO = 16


def _conv_rows(x_ref, i, tr, L):
    nblk = L // tr
    s = pl.multiple_of(i * tr, CONV_HALO)
    cur = x_ref[pl.ds(s, tr), :].astype(F32)
    sp = pl.multiple_of(jnp.maximum(i * tr - CONV_HALO, 0), CONV_HALO)
    sn = pl.multiple_of(jnp.minimum(i * tr + tr, L - CONV_HALO), CONV_HALO)
    prev = x_ref[pl.ds(sp, CONV_HALO), :].astype(F32) * (i > 0).astype(F32)
    nxt = x_ref[pl.ds(sn, CONV_HALO), :].astype(F32) * (i < nblk - 1).astype(F32)
    return jnp.concatenate([prev, cur, nxt], axis=0)


def _shift_rows(v, j):
    n = v.shape[0]
    return v if j % n == 0 else pltpu.roll(v, j % n, 0)


def _conv_taps(xe):
    return [_shift_rows(xe, CONV_W - 1 - k) for k in range(CONV_W)]


def _conv_eval(taps, w_ref, b_ref):
    c = b_ref[...] + w_ref[0:1, :] * taps[0]
    for k in range(1, CONV_W):
        c = c + w_ref[k:k + 1, :] * taps[k]
    return c


def conv_fwd(zx, col0, conv_w, conv_b):
    L = zx.shape[0]
    C = conv_w.shape[1]
    tc = _tile(C, 512)
    tr = _tile(L, 512, CONV_HALO)
    off = col0 // tc

    def body(x_ref, w_ref, b_ref, o_ref):
        i = pl.program_id(1)
        xe = _conv_rows(x_ref, i, tr, L)
        c = _conv_eval(_conv_taps(xe), w_ref, b_ref)[CONV_HALO:CONV_HALO + tr]
        o_ref[...] = _silu(c).astype(BF16)

    return pl.pallas_call(
        body, grid=(C // tc, L // tr),
        in_specs=[pl.BlockSpec((L, tc), lambda j, i: (0, off + j)), pl.BlockSpec((CONV_W, tc), lambda j, i: (0, j)),
                  pl.BlockSpec((1, tc), lambda j, i: (0, j))],
        out_specs=pl.BlockSpec((tr, tc), lambda j, i: (i, j)),
        out_shape=jax.ShapeDtypeStruct((L, C), BF16), compiler_params=_cp(("parallel", "arbitrary")),
        name="conv_fwd")(zx, conv_w, conv_b)


def conv_bwd(zx, col0, conv_w, conv_b, g, dzx, name):
    L = zx.shape[0]
    C = conv_w.shape[1]
    tc = _tile(C, 512)
    tr = _tile(L, 512, CONV_HALO)
    off = col0 // tc
    H = CONV_HALO

    def body(x_ref, g_ref, w_ref, b_ref, buf_ref, dx_ref, dw_ref, db_ref):
        i = pl.program_id(1)
        xe = _conv_rows(x_ref, i, tr, L)
        ge = _conv_rows(g_ref, i, tr, L)
        taps = _conv_taps(xe)
        dc = ge * _dsilu(_conv_eval(taps, w_ref, b_ref))
        dx = w_ref[CONV_W - 1:CONV_W, :] * dc
        for k in range(CONV_W - 1):
            dx = dx + w_ref[k:k + 1, :] * _shift_rows(dc, -(CONV_W - 1 - k))
        dx_ref[...] = dx[H:H + tr].astype(BF16)
        dcc = dc[H:H + tr]
        rows = [jnp.sum(dcc * taps[k][H:H + tr], axis=0, keepdims=True) for k in range(CONV_W)]
        dwv = jnp.concatenate(rows + [jnp.zeros((8 - CONV_W, tc), F32)], axis=0)
        dbv = jnp.sum(dcc, axis=0, keepdims=True)

        @pl.when(i == 0)
        def _():
            dw_ref[...] = dwv
            db_ref[...] = dbv

        @pl.when(i > 0)
        def _():
            dw_ref[...] += dwv
            db_ref[...] += dbv

    dx, dw, db = pl.pallas_call(
        body, grid=(C // tc, L // tr),
        in_specs=[pl.BlockSpec((L, tc), lambda j, i: (0, off + j)), pl.BlockSpec((L, tc), lambda j, i: (0, j)),
                  pl.BlockSpec((CONV_W, tc), lambda j, i: (0, j)), pl.BlockSpec((1, tc), lambda j, i: (0, j)), _ANY],
        out_specs=[pl.BlockSpec((tr, tc), lambda j, i: (i, off + j)), pl.BlockSpec((8, tc), lambda j, i: (0, j)),
                   pl.BlockSpec((1, tc), lambda j, i: (0, j))],
        out_shape=[jax.ShapeDtypeStruct(dzx.shape, BF16), jax.ShapeDtypeStruct((8, C), F32),
                   jax.ShapeDtypeStruct((1, C), F32)],
        input_output_aliases={4: 0},
        compiler_params=_cp(("parallel", "arbitrary")), name=name)(zx, g, conv_w, conv_b, dzx)
    return dx, dw[:CONV_W], db


_NN = (((1,), (0,)), ((), ()))


def _pieces(x, n):
    out, r = [], x
    for _ in range(n):
        p = r.astype(BF16)
        out.append(p)
        r = r - p.astype(F32)
    return out


def _dot01(a, b01, n, dims=_NN):
    b = b01.astype(BF16)
    return functools.reduce(lambda u, v: u + v,
                            [lax.dot_general(p, b, dims, preferred_element_type=F32) for p in _pieces(a, n)])


def _dot01_left(a01, b, n, dims=_NN):
    a = a01.astype(BF16)
    return functools.reduce(lambda u, v: u + v,
                            [lax.dot_general(a, p, dims, preferred_element_type=F32) for p in _pieces(b, n)])


def _ssd_common(dtp_ref, dtpT_ref, bias_ref, biasT_ref, alog_ref, alogT_ref, b_ref, c_ref):
    Q = SSD_Q
    dt = _softplus(dtp_ref[...] + bias_ref[...])
    A = -jnp.exp(alog_ref[...])
    row = lax.broadcasted_iota(jnp.int32, (Q, Q), 0)
    col = lax.broadcasted_iota(jnp.int32, (Q, Q), 1)
    causal = row >= col
    tril = causal.astype(F32)
    Kh = dt.shape[1]
    acum = _dot01_left(tril, dt * A, 3)
    eye = (lax.broadcasted_iota(jnp.int32, (Kh, Kh), 0) == lax.broadcasted_iota(jnp.int32, (Kh, Kh), 1)).astype(F32)
    acumT = _dot01_left(eye, acum, 3, dims=(((1,), (1,)), ((), ())))
    Bm = b_ref[...]
    Cm = c_ref[...]
    cb = lax.dot_general(Cm, Bm, (((1,), (1,)), ((), ())), preferred_element_type=F32)
    return dt, A, causal, row, col, acum, acumT, Bm, Cm, cb


def _ssd_in_specs(Q, GP, N, Kh, DI, cmap):
    nb0 = DI // N
    vec = pl.BlockSpec((None, 1, Kh), lambda g, c: (g, 0, 0))
    vecT = pl.BlockSpec((None, Kh, 1), lambda g, c: (g, 0, 0))
    return [pl.BlockSpec((Q, GP), lambda g, c: (cmap(c), g)),
            pl.BlockSpec((Q, N), lambda g, c: (cmap(c), nb0 + g)),
            pl.BlockSpec((Q, N), lambda g, c: (cmap(c), nb0 + SSD_G + g)),
            pl.BlockSpec((None, Q, Kh), lambda g, c: (g, cmap(c), 0)),
            pl.BlockSpec((None, Kh, Q), lambda g, c: (g, 0, cmap(c))),
            vec, vecT, vec, vecT, vec, vecT]


def _hi(a, b01):
    return _dot01(a, b01, 2)


def _headsum(a, b01):
    return _dot01(a, b01, 1)


def _ssd_heads(dskT_ref, acum, acumT, dt, Kh):
    Q, P, N = SSD_Q, SSD_P, SSD_N
    GP = Kh * P
    sh_p = P.bit_length() - 1
    seg = lambda shape, dim: lax.shift_right_logical(lax.broadcasted_iota(jnp.int32, shape, dim), sh_p)
    E = (seg((Kh, GP), 1) == lax.broadcasted_iota(jnp.int32, (Kh, GP), 0)).astype(F32)
    ET = (seg((GP, Kh), 0) == lax.broadcasted_iota(jnp.int32, (GP, Kh), 1)).astype(F32)
    a_last = acum[Q - 1:Q, :]
    tail = jnp.exp(a_last - acum)
    eLT = jnp.exp(acumT[:, Q - 1:Q])
    rowseg = seg((GP, N), 0)
    eL_b = jnp.zeros((GP, N), F32)
    for k in range(Kh):
        eL_b = jnp.where(rowseg == k, eLT[k:k + 1, :], eL_b)
    return dict(
        E=E, ET=ET, a_last=a_last, tail=tail, eL_b=eL_b,
        dt_all=_hi(dt, E), ea_all=_hi(jnp.exp(acum), E), tail_all=_hi(tail, E),
        dsk_all=jnp.sum(E * dskT_ref[...], axis=0, keepdims=True))


def _head_chunks(GP):
    CW = min(GP, 128)
    return CW, CW // SSD_P, GP // CW


def _head_mask(Q, CW, kk):
    lane = lax.broadcasted_iota(jnp.int32, (Q, CW), 1)
    return jnp.logical_and(lane >= kk * SSD_P, lane < (kk + 1) * SSD_P)


def ssd_fwd(xbc, dtp_g, dtp_gT, bias_g, bias_gT, alog_g, alog_gT, dsk_g, dsk_gT, zx, norm_g, DI):
    L = xbc.shape[0]
    Q, P, N, G = SSD_Q, SSD_P, SSD_N, SSD_G
    GP = DI // G
    Kh = GP // P
    nc = L // Q

    CW, hpc, nch = _head_chunks(GP)
    nt = (((1,), (1,)), ((), ()))
    tn = (((0,), (0,)), ((), ()))

    def body(xs_ref, b_ref, c_ref, dtp_ref, dtpT_ref, bias_ref, biasT_ref, alog_ref, alogT_ref, dsk_ref, dskT_ref,
             z_ref, ng_ref, y_ref, st_ref, yn_ref, state):
        @pl.when(pl.program_id(1) == 0)
        def _():
            state[...] = jnp.zeros(state.shape, F32)

        st_ref[...] = state[...]
        dt, A, causal, row, col, acum, acumT, Bm, Cm, cb = _ssd_common(
            dtp_ref, dtpT_ref, bias_ref, biasT_ref, alog_ref, alogT_ref, b_ref, c_ref)
        hd = _ssd_heads(dskT_ref, acum, acumT, dt, Kh)
        xs = xs_ref[...].astype(F32)
        xdt_all = xs * hd["dt_all"]
        S_all = state[...]
        y_all = (lax.dot_general(Cm, S_all.astype(BF16), nt, preferred_element_type=F32) * hd["ea_all"]
                 + xs * hd["dsk_all"])
        state[...] = S_all * hd["eL_b"] + lax.dot_general(
            (xdt_all * hd["tail_all"]).astype(BF16), Bm, tn, preferred_element_type=F32)
        for ch in range(nch):
            cs = slice(ch * CW, (ch + 1) * CW)
            xc = xdt_all[:, cs]
            acc = y_all[:, cs]
            for kk in range(hpc):
                k = ch * hpc + kk
                decay = jnp.exp(jnp.where(causal, acum[:, k:k + 1] - acumT[k:k + 1, :], -jnp.inf))
                xk = xc if hpc == 1 else jnp.where(_head_mask(Q, CW, kk), xc, 0.0)
                acc = acc + jnp.dot((cb * decay).astype(BF16), xk.astype(BF16), preferred_element_type=F32)
            y_ref[:, cs] = acc.astype(BF16)
        y2 = y_ref[...].astype(F32) * _silu(z_ref[...].astype(F32))
        rr = lax.rsqrt(jnp.mean(y2 * y2, axis=-1, keepdims=True) + RMS_EPS)
        yn_ref[...] = (y2 * rr * ng_ref[...]).astype(BF16)

    tile = pl.BlockSpec((Q, GP), lambda g, c: (c, g))
    return pl.pallas_call(
        body, grid=(G, nc),
        in_specs=_ssd_in_specs(Q, GP, N, Kh, DI, lambda c: c) + [tile, pl.BlockSpec((1, GP), lambda g, c: (0, g))],
        out_specs=[tile, pl.BlockSpec((None, None, GP, N), lambda g, c: (c, g, 0, 0)), tile],
        out_shape=[jax.ShapeDtypeStruct((L, DI), BF16), jax.ShapeDtypeStruct((nc, G, GP, N), F32),
                   jax.ShapeDtypeStruct((L, DI), BF16)],
        scratch_shapes=[pltpu.VMEM((GP, N), F32)], compiler_params=_cp(("parallel", "arbitrary")),
        name="ssd_fwd")(xbc, xbc, xbc, dtp_g, dtp_gT, bias_g, bias_gT, alog_g, alog_gT, dsk_g, dsk_gT, zx, norm_g)


def ssd_bwd(xbc, dtp_g, dtp_gT, bias_g, bias_gT, alog_g, alog_gT, dsk_g, dsk_gT, states, dyn, y, zx, norm_g, DI):
    L = xbc.shape[0]
    Q, P, N, G = SSD_Q, SSD_P, SSD_N, SSD_G
    GP = DI // G
    Kh = GP // P
    nc = L // Q
    rev = lambda c: nc - 1 - c

    CW, hpc, nch = _head_chunks(GP)

    def body(xs_ref, b_ref, c_ref, dtp_ref, dtpT_ref, bias_ref, biasT_ref, alog_ref, alogT_ref, dsk_ref, dskT_ref,
             st_ref, dyn_ref, y_ref, z_ref, ng_ref,
             dxs_ref, dB_ref, dC_ref, ddtp_ref, dbias_ref, dalog_ref, dD_ref, dz_ref, dng_ref, dstate):
        ci = pl.program_id(1)

        @pl.when(ci == 0)
        def _():
            dstate[...] = jnp.zeros(dstate.shape, F32)

        dt, A, causal, row, col, acum, acumT, Bm, Cm, cb = _ssd_common(
            dtp_ref, dtpT_ref, bias_ref, biasT_ref, alog_ref, alogT_ref, b_ref, c_ref)
        tn = (((0,), (0,)), ((), ()))
        nt = (((1,), (1,)), ((), ()))
        hd = _ssd_heads(dskT_ref, acum, acumT, dt, Kh)
        ET, tail = hd["ET"], hd["tail"]
        cbT = lax.dot_general(Bm, Cm, nt, preferred_element_type=F32)
        causalT = row <= col
        xs = xs_ref[...].astype(F32)
        xdt_all = xs * hd["dt_all"]
        yv = y_ref[...].astype(F32)
        zv = z_ref[...].astype(F32)
        dynv = dyn_ref[...].astype(F32)
        sz = _silu(zv)
        y2 = yv * sz
        rr = lax.rsqrt(jnp.mean(y2 * y2, axis=-1, keepdims=True) + RMS_EPS)
        yh = y2 * rr
        dyh = dynv * ng_ref[...]
        dy2 = rr * (dyh - yh * jnp.mean(dyh * yh, axis=-1, keepdims=True))
        dz_ref[...] = (dy2 * yv * _dsilu(zv)).astype(BF16)
        dng_v = jnp.sum(dynv * yh, axis=0, keepdims=True)
        dyb = (dy2 * sz).astype(BF16)
        dy_all = dyb.astype(F32)
        S_all = st_ref[...]
        S_b = S_all.astype(BF16)
        dS_all = dstate[...]
        dS_b = dS_all.astype(BF16)
        CS_all = lax.dot_general(Cm, S_b, nt, preferred_element_type=F32)
        dyE_b = (dy_all * hd["ea_all"]).astype(BF16)
        dC_acc = jnp.dot(dyE_b, S_b, preferred_element_type=F32)
        dS_y = lax.dot_general(dyE_b, Cm, tn, preferred_element_type=F32)
        BdS_all = lax.dot_general(Bm, dS_b, nt, preferred_element_type=F32)
        dB_acc = jnp.dot((xdt_all * hd["tail_all"]).astype(BF16), dS_b, preferred_element_type=F32)
        dtail = _headsum(xdt_all * BdS_all, ET)
        da_cols = _headsum(dy_all * CS_all * hd["ea_all"], ET) - dtail * tail
        dss = _dot01_left(jnp.ones((8, N), F32), _dot01_left(hd["E"], dS_all * S_all, 2), 2, dims=nt)
        da_last = dss[0:1] * jnp.exp(hd["a_last"]) + jnp.sum(dtail * tail, axis=0, keepdims=True)
        rowi = lax.broadcasted_iota(jnp.int32, (Q, Kh), 0)
        da_cols = da_cols + jnp.where(rowi == Q - 1, da_last, 0.0)
        dstate[...] = hd["eL_b"] * dS_all + dS_y
        sum_mg = jnp.zeros((Q, Q), F32)
        ddt_x = jnp.zeros((Q, Kh), F32)
        da_rows = jnp.zeros((Kh, Q), F32)
        lane_k = lax.broadcasted_iota(jnp.int32, (Q, Kh), 1)
        sub_k = lax.broadcasted_iota(jnp.int32, (Kh, Q), 0)
        for ch in range(nch):
            cs = slice(ch * CW, (ch + 1) * CW)
            dyc = dyb[:, cs]
            xc_b = xdt_all[:, cs].astype(BF16)
            acc = hd["tail_all"][:, cs] * BdS_all[:, cs]
            for kk in range(hpc):
                k = ch * hpc + kk
                a_b = jnp.broadcast_to(acum[:, k:k + 1], (Q, Q))
                a_r = acumT[k:k + 1, :]
                decay = jnp.exp(jnp.where(causal, a_b - a_r, -jnp.inf))
                decayT = jnp.exp(jnp.where(causalT, a_r - a_b, -jnp.inf))
                dyk = dyc if hpc == 1 else jnp.where(_head_mask(Q, CW, kk), dyc, jnp.zeros_like(dyc))
                mg = decay * lax.dot_general(dyk, xc_b, nt, preferred_element_type=F32)
                sum_mg = sum_mg + mg
                w = mg * cb
                da_cols = da_cols + jnp.where(lane_k == k, jnp.sum(w, axis=1, keepdims=True), 0.0)
                da_rows = da_rows + jnp.where(sub_k == k, jnp.sum(w, axis=0, keepdims=True), 0.0)
                acc = acc + jnp.dot((decayT * cbT).astype(BF16), dyk, preferred_element_type=F32)
            dxs_ref[:, cs] = (acc * hd["dt_all"][:, cs] + dy_all[:, cs] * hd["dsk_all"][:, cs]).astype(BF16)
            ddt_x = ddt_x + _headsum(acc * xs[:, cs], ET[cs, :])
        eye_q = (row == col).astype(F32)
        da_cols = da_cols - _dot01_left(eye_q, da_rows, 3, dims=nt)
        dD_row = jnp.sum(_headsum(dy_all * xs, ET), axis=0, keepdims=True)
        sum_mg_b = sum_mg.astype(BF16)
        dB_ref[...] = (dB_acc + lax.dot_general(sum_mg_b, Cm, tn, preferred_element_type=F32)).astype(BF16)
        dC_ref[...] = (dC_acc + jnp.dot(sum_mg_b, Bm, preferred_element_type=F32)).astype(BF16)
        triu = (row <= col).astype(F32)
        ddtA = _dot01_left(triu, da_cols, 3)
        ddt = ddt_x + ddtA * A
        dpre = ddt * _sigmoid(dtp_ref[...] + bias_ref[...])
        ddtp_ref[...] = dpre
        dbias_v = jnp.sum(dpre, axis=0, keepdims=True)
        dalog_v = jnp.sum(ddtA * dt, axis=0, keepdims=True) * A

        @pl.when(ci == 0)
        def _():
            dbias_ref[...] = dbias_v
            dalog_ref[...] = dalog_v
            dD_ref[...] = dD_row
            dng_ref[...] = dng_v

        @pl.when(ci > 0)
        def _():
            dbias_ref[...] += dbias_v
            dalog_ref[...] += dalog_v
            dD_ref[...] += dD_row
            dng_ref[...] += dng_v

    vec_o = pl.BlockSpec((None, 1, Kh), lambda g, c: (g, 0, 0))
    tile = pl.BlockSpec((Q, GP), lambda g, c: (rev(c), g))
    return pl.pallas_call(
        body, grid=(G, nc),
        in_specs=_ssd_in_specs(Q, GP, N, Kh, DI, rev)
        + [pl.BlockSpec((None, None, GP, N), lambda g, c: (rev(c), g, 0, 0)), tile, tile, tile,
           pl.BlockSpec((1, GP), lambda g, c: (0, g))],
        out_specs=[tile, pl.BlockSpec((Q, N), lambda g, c: (rev(c), g)), pl.BlockSpec((Q, N), lambda g, c: (rev(c), g)),
                   pl.BlockSpec((None, Q, Kh), lambda g, c: (g, rev(c), 0)), vec_o, vec_o, vec_o,
                   tile, pl.BlockSpec((1, GP), lambda g, c: (0, g))],
        out_shape=[jax.ShapeDtypeStruct((L, DI), BF16), jax.ShapeDtypeStruct((L, G * N), BF16),
                   jax.ShapeDtypeStruct((L, G * N), BF16), jax.ShapeDtypeStruct((G, L, Kh), F32)]
        + [jax.ShapeDtypeStruct((G, 1, Kh), F32)] * 3
        + [jax.ShapeDtypeStruct(zx.shape, BF16), jax.ShapeDtypeStruct((1, DI), F32)],
        scratch_shapes=[pltpu.VMEM((GP, N), F32)], compiler_params=_cp(("parallel", "arbitrary")),
        name="ssd_bwd")(xbc, xbc, xbc, dtp_g, dtp_gT, bias_g, bias_gT, alog_g, alog_gT, dsk_g, dsk_gT, states,
                        dyn, y, zx, norm_g)


def _alibi_slope(gi, h):
    n = len(DIL_PATTERNS) * DIL_H
    return float(2.0 ** (-8.0 * (gi * DIL_H + h + 1) / n))


def _attn_masks():
    qi = lax.broadcasted_iota(jnp.int32, (DIL_BLK, DIL_BLK), 0)
    kj = lax.broadcasted_iota(jnp.int32, (DIL_BLK, DIL_BLK), 1)
    dcur = (qi - kj).astype(F32)
    return dcur, qi >= kj, dcur + float(DIL_BLK), kj >= qi


def attn_fwd(q3, kv3, gi):
    window, d = DIL_PATTERNS[gi]
    assert window // d == DIL_BLK
    HW = DIL_H * DIL_E
    M = q3.shape[1]
    nb = M // DIL_BLK
    scale = DIL_E ** -0.5
    nt = (((1,), (1,)), ((), ()))

    def body(q_ref, kp_ref, kc_ref, vp_ref, vc_ref, o_ref, lse_ref):
        n = pl.program_id(1)
        dcur, vcur, dprev, vprev0 = _attn_masks()
        dist = jnp.concatenate([dprev, dcur], axis=1)
        valid = jnp.concatenate([jnp.logical_and(vprev0, n > 0), vcur], axis=1)
        lane = lax.broadcasted_iota(jnp.int32, (DIL_BLK, 128), 1)
        lse_acc = jnp.zeros((DIL_BLK, 128), F32)
        for h in range(DIL_H):
            hs = slice(h * DIL_E, (h + 1) * DIL_E)
            sl = _alibi_slope(gi, h) * d
            kcat = jnp.concatenate([kp_ref[:, hs], kc_ref[:, hs]], axis=0)
            vcat = jnp.concatenate([vp_ref[:, hs], vc_ref[:, hs]], axis=0)
            s = lax.dot_general(q_ref[:, hs], kcat, nt, preferred_element_type=F32) * scale - sl * dist
            s = jnp.where(valid, s, -jnp.inf)
            m = jnp.max(s, axis=-1, keepdims=True)
            p = jnp.exp(s - m)
            den = jnp.sum(p, axis=-1, keepdims=True)
            o = jnp.dot(p.astype(BF16), vcat, preferred_element_type=F32) / den
            o_ref[:, hs] = o.astype(BF16)
            lse_acc = jnp.where(lane == h, m + jnp.log(den), lse_acc)
        lse_ref[...] = lse_acc

    blk = (None, DIL_BLK, HW)
    prev = lambda n: jnp.maximum(n - 1, 0)
    return pl.pallas_call(
        body, grid=(d, nb),
        in_specs=[pl.BlockSpec(blk, lambda r, n: (r, n, 0)),
                  pl.BlockSpec(blk, lambda r, n: (r, prev(n), 0)), pl.BlockSpec(blk, lambda r, n: (r, n, 0)),
                  pl.BlockSpec(blk, lambda r, n: (r, prev(n), 1)), pl.BlockSpec(blk, lambda r, n: (r, n, 1))],
        out_specs=[pl.BlockSpec(blk, lambda r, n: (r, n, 0)), pl.BlockSpec((None, DIL_BLK, 128), lambda r, n: (r, n, 0))],
        out_shape=[jax.ShapeDtypeStruct((d, M, HW), BF16), jax.ShapeDtypeStruct((d, M, 128), F32)],
        compiler_params=_cp(("parallel", "parallel")), name=f"attn_fwd_{gi}")(q3, kv3, kv3, kv3, kv3)


def attn_bwd(q3, kv3, do3, lse3, dpr3, gi):
    window, d = DIL_PATTERNS[gi]
    HW = DIL_H * DIL_E
    M = q3.shape[1]
    L = M * d
    nb = M // DIL_BLK
    scale = DIL_E ** -0.5
    nt = (((1,), (1,)), ((), ()))
    tn = (((0,), (0,)), ((), ()))

    def body(q0_ref, q1_ref, k_ref, v_ref, do0_ref, do1_ref, l0_ref, l1_ref, r0_ref, r1_ref,
             dq_ref, dk_ref, dv_ref, carry):
        n = pl.program_id(1)

        @pl.when(n == 0)
        def _():
            carry[...] = jnp.zeros(carry.shape, F32)

        dcur, vcur, dprev, vprev0 = _attn_masks()
        dist = jnp.concatenate([dcur, dprev], axis=0)
        valid = jnp.concatenate([vcur, jnp.logical_and(vprev0, n < nb - 1)], axis=0)
        B = DIL_BLK
        for h in range(DIL_H):
            hs = slice(h * DIL_E, (h + 1) * DIL_E)
            sl = _alibi_slope(gi, h) * d
            kh = k_ref[:, hs]
            vh = v_ref[:, hs]
            qcat = jnp.concatenate([q0_ref[:, hs], q1_ref[:, hs]], axis=0)
            docat = jnp.concatenate([do0_ref[:, hs], do1_ref[:, hs]], axis=0)
            lcat = jnp.concatenate([l0_ref[:, h:h + 1], l1_ref[:, h:h + 1]], axis=0)
            rcat = jnp.concatenate([r0_ref[:, h:h + 1], r1_ref[:, h:h + 1]], axis=0)
            s = lax.dot_general(qcat, kh, nt, preferred_element_type=F32) * scale - sl * dist
            p = jnp.exp(jnp.where(valid, s - lcat, -jnp.inf))
            ds = p * (lax.dot_general(docat, vh, nt, preferred_element_type=F32) - rcat)
            ds_b = (ds * scale).astype(BF16)
            dv_ref[:, hs] = lax.dot_general(p.astype(BF16), docat, tn, preferred_element_type=F32).astype(BF16)
            dk_ref[:, hs] = lax.dot_general(ds_b, qcat, tn, preferred_element_type=F32).astype(BF16)
            dqc = jnp.dot(ds_b, kh, preferred_element_type=F32)
            dq_ref[:, hs] = (carry[:, hs] + dqc[:B]).astype(BF16)
            carry[:, hs] = dqc[B:]

    blk = (None, DIL_BLK, HW)
    sblk = (None, DIL_BLK, 128)
    oblk = (DIL_BLK, HW)
    nxt = lambda n: jnp.minimum(n + 1, nb - 1)
    here = lambda c: (lambda r, n: (r, n, c))
    ahead = lambda c: (lambda r, n: (r, nxt(n), c))
    outs = pl.pallas_call(
        body, grid=(d, nb),
        in_specs=[pl.BlockSpec(blk, here(0)), pl.BlockSpec(blk, ahead(0)),
                  pl.BlockSpec(blk, here(0)), pl.BlockSpec(blk, here(1)),
                  pl.BlockSpec(blk, here(0)), pl.BlockSpec(blk, ahead(0)),
                  pl.BlockSpec(sblk, here(0)), pl.BlockSpec(sblk, ahead(0)),
                  pl.BlockSpec(sblk, here(0)), pl.BlockSpec(sblk, ahead(0))],
        out_specs=[pl.BlockSpec(oblk, lambda r, n: (n, r))] * 3,
        out_shape=[jax.ShapeDtypeStruct((M, d * HW), BF16)] * 3,
        scratch_shapes=[pltpu.VMEM(oblk, F32)], compiler_params=_cp(("parallel", "arbitrary")),
        name=f"attn_bwd_{gi}")(q3, q3, kv3, kv3, do3, do3, lse3, lse3, dpr3, dpr3)
    return [t.reshape(L, HW) for t in outs]


def _merge_weights(l_tiles, h):
    ls = [t[:, h:h + 1] for t in l_tiles]
    mx = functools.reduce(jnp.maximum, ls)
    es = [jnp.exp(l - mx) for l in ls]
    den = functools.reduce(lambda a, b: a + b, es)
    return [e / den for e in es]


def _dil_specs(tr, arrs):
    return [pl.BlockSpec((a.shape[0], tr // a.shape[0], a.shape[2]), lambda i: (0, i, 0)) for a in arrs]


def _dil_scratch(tr, arrs):
    return [pltpu.VMEM((a.shape[2] // 128, tr, 128), F32) for a in arrs if a.shape[0] > 1]


def _undilate(refs3, scrs, tr):
    out, k = [], 0
    for ref in refs3:
        d, _, W = ref.shape
        if d == 1:
            out.append(lambda c, ref=ref: ref[0, :, c * 128:(c + 1) * 128])
            continue
        scr = scrs[k]
        k += 1
        for r in range(d):
            for c in range(W // 128):
                scr.at[c][pl.ds(r, tr // d, stride=d), :] = ref[r, :, c * 128:(c + 1) * 128].astype(F32)
        out.append(lambda c, scr=scr: scr[c])
    return out


def merge_fwd(os3, lses3, z):
    HW = os3[0].shape[2]
    L = os3[0].shape[0] * os3[0].shape[1]
    tr = _tile(L, 256, 16)
    ng = len(os3)
    n_scr = len(_dil_scratch(tr, os3))

    def body(*refs):
        z_ref, out_ref = refs[2 * ng], refs[2 * ng + 1]
        scrs = refs[2 * ng + 2:]
        o_get = _undilate(refs[:ng], scrs[:n_scr], tr)
        l_tiles = [g(0) for g in _undilate(refs[ng:2 * ng], scrs[n_scr:], tr)]
        for h in range(DIL_H):
            hs = slice(h * DIL_E, (h + 1) * DIL_E)
            ws = _merge_weights(l_tiles, h)
            om = functools.reduce(lambda a, b: a + b, [w * o(h).astype(F32) for w, o in zip(ws, o_get)])
            out_ref[:, hs] = (om * _silu(z_ref[:, hs].astype(F32))).astype(BF16)

    return pl.pallas_call(
        body, grid=(L // tr,),
        in_specs=_dil_specs(tr, os3) + _dil_specs(tr, lses3) + _row_specs(tr, [HW]),
        out_specs=_row_specs(tr, [HW])[0], out_shape=jax.ShapeDtypeStruct((L, HW), BF16),
        scratch_shapes=_dil_scratch(tr, os3) + _dil_scratch(tr, lses3),
        compiler_params=_cp(("parallel",)), name="merge_fwd")(*os3, *lses3, z)


def merge_bwd(dgated, os3, lses3, z):
    HW = os3[0].shape[2]
    L = os3[0].shape[0] * os3[0].shape[1]
    tr = _tile(L, 256, 16)
    ng = len(os3)
    n_scr = len(_dil_scratch(tr, os3))

    def body(*refs):
        dg_ref = refs[0]
        z_ref = refs[1 + 2 * ng]
        outs = refs[2 + 2 * ng:2 + 2 * ng + 2 * ng + 1]
        scrs = refs[2 + 2 * ng + 2 * ng + 1:]
        do_out, dpr_out, dz_ref = outs[:ng], outs[ng:2 * ng], outs[2 * ng]
        o_get = _undilate(refs[1:1 + ng], scrs[:n_scr], tr)
        l_tiles = [g(0) for g in _undilate(refs[1 + ng:1 + 2 * ng], scrs[n_scr:2 * n_scr], tr)]
        stage = scrs[2 * n_scr:]
        do_stage, dpr_stage, k = [], [], 0
        for g in range(ng):
            if do_out[g].shape[0] == 1:
                do_stage.append(None)
                dpr_stage.append(None)
            else:
                do_stage.append(stage[2 * k])
                dpr_stage.append(stage[2 * k + 1])
                k += 1
        lane = lax.broadcasted_iota(jnp.int32, (tr, 128), 1)
        accs = [jnp.zeros((tr, 128), F32) for _ in range(ng)]
        for h in range(DIL_H):
            hs = slice(h * DIL_E, (h + 1) * DIL_E)
            ws = _merge_weights(l_tiles, h)
            ov = [o(h).astype(F32) for o in o_get]
            om = functools.reduce(lambda a, b: a + b, [w * o for w, o in zip(ws, ov)])
            zv = z_ref[:, hs].astype(F32)
            dgv = dg_ref[:, hs].astype(F32)
            dom = dgv * _silu(zv)
            dz_ref[:, hs] = (dgv * om * _dsilu(zv)).astype(BF16)
            dws = [jnp.sum(dom * o, axis=-1, keepdims=True) for o in ov]
            dwbar = functools.reduce(lambda a, b: a + b, [w * dw for w, dw in zip(ws, dws)])
            for g in range(ng):
                if do_stage[g] is None:
                    do_out[g][0, :, hs] = (ws[g] * dom).astype(BF16)
                else:
                    do_stage[g][h] = ws[g] * dom
                accs[g] = jnp.where(lane == h, ws[g] * dwbar, accs[g])
        for g in range(ng):
            d = do_out[g].shape[0]
            if d == 1:
                dpr_out[g][0] = accs[g]
                continue
            dpr_stage[g][0] = accs[g]
            for r in range(d):
                dpr_out[g][r] = dpr_stage[g].at[0][pl.ds(r, tr // d, stride=d), :]
                for c in range(HW // 128):
                    do_out[g][r, :, c * 128:(c + 1) * 128] = do_stage[g].at[c][pl.ds(r, tr // d, stride=d), :].astype(BF16)

    stage_shapes = []
    for o3 in os3:
        if o3.shape[0] > 1:
            stage_shapes += [pltpu.VMEM((HW // 128, tr, 128), F32), pltpu.VMEM((1, tr, 128), F32)]
    outs = pl.pallas_call(
        body, grid=(L // tr,),
        in_specs=_row_specs(tr, [HW]) + _dil_specs(tr, os3) + _dil_specs(tr, lses3) + _row_specs(tr, [HW]),
        out_specs=_dil_specs(tr, os3) + _dil_specs(tr, lses3) + _row_specs(tr, [HW]),
        out_shape=[jax.ShapeDtypeStruct(o.shape, BF16) for o in os3] + [jax.ShapeDtypeStruct(l.shape, F32) for l in lses3]
        + [jax.ShapeDtypeStruct((L, HW), BF16)],
        scratch_shapes=_dil_scratch(tr, os3) + _dil_scratch(tr, lses3) + stage_shapes,
        compiler_params=_cp(("parallel",)), name="merge_bwd")(dgated, *os3, *lses3, z)
    return outs[:ng], outs[ng:2 * ng], outs[2 * ng]


def ada_fwd(c8, ada_w):
    nl, D, Ws = ada_w.shape
    tn = _tile(Ws, 512)

    def body(c_ref, w_ref, o_ref):
        o_ref[...] = jnp.dot(_silu(c_ref[...]), w_ref[...], precision=lax.Precision.HIGHEST,
                             preferred_element_type=F32)

    return pl.pallas_call(
        body, grid=(nl, Ws // tn),
        in_specs=[pl.BlockSpec((N_DEV, D), lambda l, j: (0, 0)), pl.BlockSpec((None, D, tn), lambda l, j: (l, 0, j))],
        out_specs=pl.BlockSpec((None, N_DEV, tn), lambda l, j: (l, 0, j)),
        out_shape=jax.ShapeDtypeStruct((nl, N_DEV, Ws), F32), compiler_params=_cp(("parallel", "parallel")),
        name="ada_fwd")(c8, ada_w)


def ada_wgrad(c8t, dmod):
    nl, _, Ws = dmod.shape
    D = c8t.shape[0]
    tm = _tile(D, 512, 8)

    def body(c_ref, d_ref, o_ref):
        sc = _silu(c_ref[...])
        acc = sc[:, 0:1] * d_ref[0:1, :]
        for e in range(1, N_DEV):
            acc = acc + sc[:, e:e + 1] * d_ref[e:e + 1, :]
        o_ref[...] = acc

    return pl.pallas_call(
        body, grid=(nl, D // tm),
        in_specs=[pl.BlockSpec((tm, N_DEV), lambda l, i: (i, 0)), pl.BlockSpec((None, N_DEV, Ws), lambda l, i: (l, 0, 0))],
        out_specs=pl.BlockSpec((None, tm, Ws), lambda l, i: (l, i, 0)),
        out_shape=jax.ShapeDtypeStruct((nl, D, Ws), F32), compiler_params=_cp(("parallel", "parallel")),
        name="ada_wgrad")(c8t, dmod)


def adamw(w, g, m, v, name):
    R, C = w.shape
    tr = _tile(R, 256, 8)
    c1 = 1.0 - ADAM_B1 ** ADAM_STEP
    c2 = 1.0 - ADAM_B2 ** ADAM_STEP

    def body(w_ref, g_ref, m_ref, v_ref, d_ref, nm_ref, nv_ref):
        gv = g_ref[...]
        nm = ADAM_B1 * m_ref[...] + (1.0 - ADAM_B1) * gv
        nv = ADAM_B2 * v_ref[...] + (1.0 - ADAM_B2) * (gv * gv)
        nm_ref[...] = nm
        nv_ref[...] = nv
        d_ref[...] = -ADAM_LR * ((nm / c1) / (jnp.sqrt(nv / c2) + ADAM_EPS) + ADAM_WD * w_ref[...])

    return pl.pallas_call(
        body, grid=(R // tr,), in_specs=_row_specs(tr, [C] * 4), out_specs=_row_specs(tr, [C] * 3),
        out_shape=[jax.ShapeDtypeStruct((R, C), F32)] * 3, compiler_params=_cp(("parallel",)), name=name)(w, g, m, v)


def sum_leading(t, name, out_dtype=F32):
    S, R, C = t.shape
    tr = _tile(R, 256, 16)

    def body(t_ref, o_ref):
        acc = t_ref[0].astype(F32)
        for s in range(1, S):
            acc = acc + t_ref[s].astype(F32)
        o_ref[...] = acc.astype(out_dtype)

    return pl.pallas_call(
        body, grid=(R // tr,), in_specs=[pl.BlockSpec((S, tr, C), lambda i: (0, i, 0))],
        out_specs=pl.BlockSpec((tr, C), lambda i: (i, 0)), out_shape=jax.ShapeDtypeStruct((R, C), out_dtype),
        compiler_params=_cp(("parallel",)), name=name)(t)


def add_half(g, a, core, name, by_cols=False):
    S, R, C = g.shape

    def body(core_ref, g_ref, a_ref, o_ref):
        o_ref[...] = (g_ref[...].astype(F32) + a_ref[...].astype(F32)).astype(BF16)

    if by_cols:
        hc = C // 2
        tr = _tile(R, 256, 16)
        return pl.pallas_call(
            body,
            grid_spec=pltpu.PrefetchScalarGridSpec(
                num_scalar_prefetch=1, grid=(S, R // tr),
                in_specs=[pl.BlockSpec((None, tr, hc), lambda s, i, core_ref: (s, i, core_ref[0])),
                          pl.BlockSpec((None, tr, hc), lambda s, i, core_ref: (s, i, 0))],
                out_specs=pl.BlockSpec((None, tr, hc), lambda s, i, core_ref: (s, i, 0))),
            out_shape=jax.ShapeDtypeStruct((S, R, hc), BF16), compiler_params=_cp(("parallel", "parallel")),
            name=name)(core, g, a)
    h = R // 2
    tr = _tile(h, 256, 16)
    nb = h // tr

    return pl.pallas_call(
        body,
        grid_spec=pltpu.PrefetchScalarGridSpec(
            num_scalar_prefetch=1, grid=(S, nb),
            in_specs=[pl.BlockSpec((None, tr, C), lambda s, i, core_ref: (s, core_ref[0] * nb + i, 0)),
                      pl.BlockSpec((None, tr, C), lambda s, i, core_ref: (s, i, 0))],
            out_specs=pl.BlockSpec((None, tr, C), lambda s, i, core_ref: (s, i, 0))),
        out_shape=jax.ShapeDtypeStruct((S, h, C), BF16), compiler_params=_cp(("parallel", "parallel")),
        name=name)(core, g, a)


def sum_partials(own, landed, chip, name):
    _, h, C = own.shape
    tr = _tile(h, 256, 16)

    def body(chip_ref, own_ref, l_ref, o_ref):
        acc = own_ref[...].astype(F32)
        for j in range(3):
            acc = acc + l_ref[j].astype(F32)
        o_ref[...] = acc

    return pl.pallas_call(
        body,
        grid_spec=pltpu.PrefetchScalarGridSpec(
            num_scalar_prefetch=1, grid=(h // tr,),
            in_specs=[pl.BlockSpec((None, tr, C), lambda i, chip_ref: (chip_ref[0], i, 0)),
                      pl.BlockSpec((3, tr, C), lambda i, chip_ref: (0, i, 0))],
            out_specs=pl.BlockSpec((tr, C), lambda i, chip_ref: (i, 0))),
        out_shape=jax.ShapeDtypeStruct((h, C), F32), compiler_params=_cp(("parallel",)), name=name)(chip, own, landed)


def adamw_halves(w, g_mine, g_theirs, m, v, core, name):
    R, C = w.shape
    h = R // 2
    tr = _tile(h, 256, 8)
    nbh = h // tr
    c1 = 1.0 - ADAM_B1 ** ADAM_STEP
    c2 = 1.0 - ADAM_B2 ** ADAM_STEP

    def body(core_ref, w_ref, gm_ref, gt_ref, m_ref, v_ref, g_ref, d_ref, nm_ref, nv_ref):
        mine = (pl.program_id(0) // nbh) == core_ref[0]
        gv = jnp.where(mine, gm_ref[...], gt_ref[...])
        g_ref[...] = gv
        nm = ADAM_B1 * m_ref[...] + (1.0 - ADAM_B1) * gv
        nv = ADAM_B2 * v_ref[...] + (1.0 - ADAM_B2) * (gv * gv)
        nm_ref[...] = nm
        nv_ref[...] = nv
        d_ref[...] = -ADAM_LR * ((nm / c1) / (jnp.sqrt(nv / c2) + ADAM_EPS) + ADAM_WD * w_ref[...])

    full = pl.BlockSpec((tr, C), lambda i, core_ref: (i, 0))
    halfspec = pl.BlockSpec((tr, C), lambda i, core_ref: (i % nbh, 0))
    return pl.pallas_call(
        body,
        grid_spec=pltpu.PrefetchScalarGridSpec(
            num_scalar_prefetch=1, grid=(2 * nbh,), in_specs=[full, halfspec, halfspec, full, full],
            out_specs=[full] * 4),
        out_shape=[jax.ShapeDtypeStruct((R, C), F32)] * 4, compiler_params=_cp(("parallel",)),
        name=name)(core, w, g_mine, g_theirs, m, v)


_ANY = pl.BlockSpec(memory_space=pl.ANY)


def _place():
    x, y, c = lax.axis_index("x"), lax.axis_index("y"), lax.axis_index("c")
    chips = [(1 - x, y), (x, 1 - y), (1 - x, 1 - y)]
    return x, y, c, chips


def allgather_small(v, name, after=None):
    R, W = v.shape
    extra = [] if after is None else [after]

    def body(x_ref, *rest):
        out_ref, send_sems, recv_sems, local_sem = rest[len(extra):]
        x, y, c, chips = _place()
        me, sibling = (x, y, c), (x, y, 1 - c)

        def rows(px, py, pc):
            return out_ref.at[pl.ds((4 * px + 2 * py + pc) * R, R), :]

        def copy(k, block, to, src=None):
            return pltpu.make_async_remote_copy(
                src_ref=rows(*block) if src is None else src, dst_ref=rows(*block),
                send_sem=send_sems.at[k], recv_sem=recv_sems.at[k], device_id=to, device_id_type=MESH)

        mine = pltpu.make_async_copy(x_ref, rows(*me), local_sem)
        mine.start()
        first = [copy(0, me, sibling, src=x_ref)]
        first += [copy(1 + j, me, (*chip, c), src=x_ref) for j, chip in enumerate(chips)]
        for cp in first:
            cp.start()
        passed = [copy(4 + j, (*chip, c), sibling) for j, chip in enumerate(chips)]
        for j, chip in enumerate(chips):
            copy(1 + j, (*chip, c), me).wait_recv()
            passed[j].start()
        copy(0, sibling, me).wait_recv()
        for j, chip in enumerate(chips):
            copy(4 + j, (*chip, 1 - c), me).wait_recv()
        for cp in first + passed:
            cp.wait_send()
        mine.wait()

    return pl.pallas_call(
        body, out_shape=jax.ShapeDtypeStruct((N_DEV * R, W), v.dtype),
        in_specs=[pl.BlockSpec(memory_space=pltpu.VMEM)] + [_ANY] * len(extra),
        out_specs=pl.BlockSpec(memory_space=pltpu.VMEM),
        scratch_shapes=[pltpu.SemaphoreType.DMA((7,)), pltpu.SemaphoreType.DMA((7,)), pltpu.SemaphoreType.DMA],
        name=name)(v, *extra)


def allgather_routed(shard, name):
    R, C = shard.shape
    hc = C // 2
    ra = (R // 2) // 16 * 16

    def body(in_ref, out_ref, send_sems, recv_sems):
        x, y, c, _ = _place()
        xn, yn = (1 - x, y, c), (x, 1 - y, c)
        sibling = (x, y, 1 - c)
        p, pxn, pyn, pdg = 2 * x + y, 2 * (1 - x) + y, 2 * x + (1 - y), 2 * (1 - x) + (1 - y)
        rows_a, rows_b, rows_all = pl.ds(0, ra), pl.ds(ra, R - ra), pl.ds(0, R)

        def win(ref, rows, core):
            return ref.at[rows, pl.ds(pl.multiple_of(core * hc, 128), hc)]

        def copy(k, chip_id, rows, core, to, src=None):
            blk = win(out_ref.at[chip_id], rows, core)
            return pltpu.make_async_remote_copy(
                src_ref=blk if src is None else src, dst_ref=blk, send_sem=send_sems.at[k], recv_sem=recv_sems.at[k],
                device_id=to, device_id_type=MESH)

        own = [copy(0, p, rows_a, c, xn, src=win(in_ref, rows_a, c)), copy(1, p, rows_b, c, xn, src=win(in_ref, rows_b, c)),
               copy(2, p, rows_b, c, yn, src=win(in_ref, rows_b, c)), copy(3, p, rows_a, c, yn, src=win(in_ref, rows_a, c))]
        for cp in own:
            cp.start()
        copy(0, pxn, rows_a, c, xn).wait_recv()
        fwd_a = copy(4, pxn, rows_a, c, yn)
        fwd_a.start()
        copy(2, pyn, rows_b, c, yn).wait_recv()
        fwd_b = copy(5, pyn, rows_b, c, xn)
        fwd_b.start()
        copy(1, pxn, rows_b, c, xn).wait_recv()
        copy(3, pyn, rows_a, c, yn).wait_recv()
        passed = [copy(6, pxn, rows_all, c, sibling), copy(7, pyn, rows_all, c, sibling)]
        for cp in passed:
            cp.start()
        copy(4, pdg, rows_a, c, yn).wait_recv()
        passed.append(copy(8, pdg, rows_a, c, sibling))
        passed[-1].start()
        copy(5, pdg, rows_b, c, xn).wait_recv()
        passed.append(copy(9, pdg, rows_b, c, sibling))
        passed[-1].start()
        for k, (chip_id, rows) in enumerate([(pxn, rows_all), (pyn, rows_all), (pdg, rows_a), (pdg, rows_b)]):
            copy(6 + k, chip_id, rows, 1 - c, sibling).wait_recv()
        for cp in own + [fwd_a, fwd_b] + passed:
            cp.wait_send()

    out = pl.pallas_call(
        body, out_shape=jax.ShapeDtypeStruct((N_CHIPS, R, C), shard.dtype), in_specs=[_ANY], out_specs=_ANY,
        scratch_shapes=[pltpu.SemaphoreType.DMA((10,)), pltpu.SemaphoreType.DMA((10,))], name=name)(shard)
    chip = 2 * lax.axis_index("x") + lax.axis_index("y")
    return lax.dynamic_update_index_in_dim(out, shard, chip, 0)


_HBM = pl.BlockSpec(memory_space=pltpu.HBM)
_SEM = pl.BlockSpec(memory_space=pltpu.SEMAPHORE)
_EFFECT = pltpu.SideEffectType.DATAFLOW_SIDE_EFFECTING


def _chip_copies(kind, srcs, lands, send_sems, recv_sems):
    x, y, c, chips = _place()
    p = 2 * x + y
    cps = []
    if kind == "sibling":
        for i in range(len(srcs)):
            h = srcs[i].shape[1] // 2
            cps.append(pltpu.make_async_remote_copy(
                src_ref=srcs[i].at[:, pl.ds((1 - c) * h, h), :], dst_ref=lands[i], send_sem=send_sems.at[3 * i],
                recv_sem=recv_sems.at[3 * i], device_id=(x, y, 1 - c), device_id_type=MESH))
        return cps
    for i in range(len(srcs)):
        for j, (cx, cy) in enumerate(chips):
            if kind == "gather":
                src, dst = srcs[i].at[c], lands[i].at[p, c]
            else:
                src, dst = srcs[i].at[2 * cx + cy], lands[i].at[j]
            cps.append(pltpu.make_async_remote_copy(
                src_ref=src, dst_ref=dst, send_sem=send_sems.at[3 * i + j], recv_sem=recv_sems.at[3 * i + j],
                device_id=(cx, cy, c), device_id_type=MESH))
    return cps


def split_start(kind, srcs, land_shapes, after, name):
    n = len(srcs)

    def body(*refs):
        src_refs, land_refs = refs[:n], refs[n:2 * n]
        send_sems, recv_sems = refs[2 * n + 1], refs[2 * n + 2]
        token = refs[-1]
        for cp in _chip_copies(kind, src_refs, land_refs, send_sems, recv_sems):
            cp.start()
        token[...] = jnp.zeros_like(token)

    lands = [pltpu.with_memory_space_constraint(lax.empty(s, BF16), pltpu.HBM) for s in land_shapes]
    outs = pl.pallas_call(
        body, name=name,
        out_shape=(pltpu.SemaphoreType.DMA((3 * n,)), pltpu.SemaphoreType.DMA((3 * n,)),
                   *[pltpu.HBM(s.shape, s.dtype) for s in srcs], *[pltpu.HBM(s, BF16) for s in land_shapes],
                   jax.ShapeDtypeStruct((8, 128), F32)),
        in_specs=[_HBM] * (2 * n) + [_ANY],
        out_specs=(_SEM, _SEM, *([_HBM] * (2 * n)), pl.BlockSpec(memory_space=pltpu.VMEM)),
        input_output_aliases={i: 2 + i for i in range(2 * n)},
        compiler_params=pltpu.CompilerParams(has_side_effects=_EFFECT),
    )(*[pltpu.with_memory_space_constraint(s, pltpu.HBM) for s in srcs], *lands, after)
    return outs[0], outs[1], outs[2:2 + n], outs[2 + n:2 + 2 * n], outs[-1]


def split_wait(kind, send_sems, recv_sems, srcs, lands, after, name):
    n = len(srcs)

    def body(*refs):
        src_refs, land_refs = refs[:n], refs[n:2 * n]
        ssem, rsem = refs[2 * n], refs[2 * n + 1]
        for cp in _chip_copies(kind, src_refs, land_refs, ssem, rsem):
            cp.wait_send()
            cp.wait_recv()

    outs = pl.pallas_call(
        body, name=name,
        out_shape=[pltpu.HBM(s.shape, s.dtype) for s in srcs] + [pltpu.HBM(s.shape, s.dtype) for s in lands],
        in_specs=[_HBM] * (2 * n) + [_SEM, _SEM, _ANY], out_specs=[_HBM] * (2 * n),
        input_output_aliases={i: i for i in range(2 * n)},
        compiler_params=pltpu.CompilerParams(has_side_effects=_EFFECT),
    )(*srcs, *lands, send_sems, recv_sems, after)
    return outs[:n], outs[n:]


def pass_to_sibling(lands):
    n = len(lands)

    def body(*refs):
        ins, outs = refs[:n], refs[n:2 * n]
        send_sems, recv_sems = refs[2 * n:]
        x, y, c, chips = _place()
        cps = []
        for i in range(n):
            for j, (cx, cy) in enumerate(chips):
                blk = outs[i].at[2 * cx + cy, c]
                cps.append(pltpu.make_async_remote_copy(
                    src_ref=ins[i].at[2 * cx + cy, c], dst_ref=blk, send_sem=send_sems.at[3 * i + j],
                    recv_sem=recv_sems.at[3 * i + j], device_id=(x, y, 1 - c), device_id_type=MESH))
        for cp in cps:
            cp.start()
        for cp in cps:
            cp.wait()

    return pl.pallas_call(
        body, out_shape=[jax.ShapeDtypeStruct(t.shape, t.dtype) for t in lands], in_specs=[_ANY] * n,
        out_specs=[_ANY] * n, input_output_aliases={i: i for i in range(n)},
        scratch_shapes=[pltpu.SemaphoreType.DMA((3 * n,)), pltpu.SemaphoreType.DMA((3 * n,))],
        name="ag_pass_to_sibling")(*lands)


def _pass_copies(bufs, send_sems, recv_sems):
    x, y, c, chips = _place()
    cps = []
    for i in range(len(bufs)):
        for j, (cx, cy) in enumerate(chips):
            blk = bufs[i].at[2 * cx + cy, c]
            cps.append(pltpu.make_async_remote_copy(
                src_ref=blk, dst_ref=blk, send_sem=send_sems.at[3 * i + j], recv_sem=recv_sems.at[3 * i + j],
                device_id=(x, y, 1 - c), device_id_type=MESH))
    return cps


def pass_start(bufs, after, name):
    n = len(bufs)

    def body(*refs):
        send_sems, recv_sems = refs[n + 1], refs[n + 2]
        for cp in _pass_copies(refs[:n], send_sems, recv_sems):
            cp.start()
        refs[-1][...] = jnp.zeros_like(refs[-1])

    outs = pl.pallas_call(
        body, name=name,
        out_shape=(pltpu.SemaphoreType.DMA((3 * n,)), pltpu.SemaphoreType.DMA((3 * n,)),
                   *[pltpu.HBM(b.shape, b.dtype) for b in bufs], jax.ShapeDtypeStruct((8, 128), F32)),
        in_specs=[_HBM] * n + [_ANY],
        out_specs=(_SEM, _SEM, *([_HBM] * n), pl.BlockSpec(memory_space=pltpu.VMEM)),
        input_output_aliases={i: 2 + i for i in range(n)},
        compiler_params=pltpu.CompilerParams(has_side_effects=_EFFECT),
    )(*[pltpu.with_memory_space_constraint(b, pltpu.HBM) for b in bufs], after)
    return outs[0], outs[1], outs[2:2 + n], outs[-1]


def pass_wait(send_sems, recv_sems, bufs, after, name):
    n = len(bufs)

    def body(*refs):
        for cp in _pass_copies(refs[:n], refs[n], refs[n + 1]):
            cp.wait_send()
            cp.wait_recv()

    return pl.pallas_call(
        body, name=name, out_shape=[pltpu.HBM(b.shape, b.dtype) for b in bufs],
        in_specs=[_HBM] * n + [_SEM, _SEM, _ANY], out_specs=[_HBM] * n,
        input_output_aliases={i: i for i in range(n)},
        compiler_params=pltpu.CompilerParams(has_side_effects=_EFFECT),
    )(*bufs, send_sems, recv_sems, after)


def exchange_halves_to_sibling(gs, name, by_cols=False):
    n = len(gs)

    def body(*refs):
        ins, outs = refs[:n], refs[n:2 * n]
        send_sems, recv_sems = refs[2 * n:]
        x, y, c, _ = _place()
        cps = []
        for i in range(n):
            if by_cols:
                hc = ins[i].shape[2] // 2
                src = ins[i].at[:, :, pl.ds(pl.multiple_of((1 - c) * hc, 128), hc)]
            else:
                h = ins[i].shape[1] // 2
                src = ins[i].at[:, pl.ds((1 - c) * h, h), :]
            cps.append(pltpu.make_async_remote_copy(
                src_ref=src, dst_ref=outs[i],
                send_sem=send_sems.at[i], recv_sem=recv_sems.at[i], device_id=(x, y, 1 - c), device_id_type=MESH))
        for cp in cps:
            cp.start()
        for cp in cps:
            cp.wait()

    halve = (lambda s: (s[0], s[1], s[2] // 2)) if by_cols else (lambda s: (s[0], s[1] // 2, s[2]))
    return pl.pallas_call(
        body, out_shape=[jax.ShapeDtypeStruct(halve(g.shape), g.dtype) for g in gs],
        in_specs=[_ANY] * n, out_specs=[_ANY] * n,
        scratch_shapes=[pltpu.SemaphoreType.DMA((n,)), pltpu.SemaphoreType.DMA((n,))],
        name=name)(*gs)


def join_halves(rs, name):
    n = len(rs)

    def body(*refs):
        ins, outs = refs[:n], refs[n:2 * n]
        send_sems, recv_sems = refs[2 * n:]
        x, y, c, _ = _place()
        cps = [pltpu.make_async_remote_copy(
            src_ref=ins[i], dst_ref=outs[i], send_sem=send_sems.at[i], recv_sem=recv_sems.at[i],
            device_id=(x, y, 1 - c), device_id_type=MESH) for i in range(n)]
        for cp in cps:
            cp.start()
        for cp in cps:
            cp.wait()

    return pl.pallas_call(
        body, out_shape=[jax.ShapeDtypeStruct(r.shape, r.dtype) for r in rs],
        in_specs=[_ANY] * n, out_specs=[_ANY] * n,
        scratch_shapes=[pltpu.SemaphoreType.DMA((n,)), pltpu.SemaphoreType.DMA((n,))],
        name=name)(*rs)


def _pack(parts, row_mult=8):
    flat = jnp.concatenate([p.reshape(-1).astype(F32) for p in parts])
    unit = row_mult * 128
    n = -(-flat.shape[0] // unit) * unit
    return jnp.pad(flat, (0, n - flat.shape[0])).reshape(n // 128, 128)


def _unpack(flat, shapes):
    out, off = [], 0
    for s in shapes:
        n = int(np.prod(s))
        out.append(flat[off:off + n].reshape(s))
        off += n
    return out


def _gather_packed(parts, name):
    packed = _pack(parts)
    g = allgather_small(packed, name).reshape(N_DEV, -1)
    return _unpack_rows(g, [p.shape for p in parts])


def _unpack_rows(g, shapes):
    out, off = [], 0
    for s in shapes:
        n = int(np.prod(s))
        out.append(g[:, off:off + n].reshape((g.shape[0],) + tuple(s)))
        off += n
    return out


def _by_chip(t, axis):
    return jnp.concatenate([t[2 * p] for p in range(N_CHIPS)], axis=axis)


def kernel(x, c, ada_w, ada_b, ln_g, ln_b, a_in_w, a_conv_w, a_conv_b, a_dt_bias, a_A_log, a_D, a_norm_g, a_out_w, kv_w, b_in_w, b_out_w, loss_target, m_ada_w, m_ada_b, m_ln_g, m_ln_b, m_a_in_w, m_a_conv_w, m_a_conv_b, m_a_dt_bias, m_a_A_log, m_a_D, m_a_norm_g, m_a_out_w, m_kv_w, m_b_in_w, m_b_out_w, v_ada_w, v_ada_b, v_ln_g, v_ln_b, v_a_in_w, v_a_conv_w, v_a_conv_b, v_a_dt_bias, v_a_A_log, v_a_D, v_a_norm_g, v_a_out_w, v_kv_w, v_b_in_w, v_b_out_w):
    ax, ay, ac = lax.axis_index("x"), lax.axis_index("y"), lax.axis_index("c")
    chip = 2 * ax + ay
    dev = 4 * ax + 2 * ay + ac
    xin = x[0]
    tgt = loss_target[0]
    L, D = xin.shape
    G, P = SSD_G, SSD_P
    H = a_dt_bias.shape[1]
    Kh = H // G
    DI = H * P
    CONVD = a_conv_b.shape[1] * N_CHIPS
    HW = DIL_H * DIL_E
    Ws = ada_w.shape[2]

    w_in_g = allgather_routed(jnp.transpose(a_in_w[0]).astype(BF16), "allgather_w_in")
    later = [a_out_w[0].astype(BF16), kv_w.astype(BF16), b_in_w[0].astype(BF16), b_out_w[0].astype(BF16)]
    later_split = [s.reshape(2, s.shape[0] // 2, s.shape[1]) for s in later]
    ag_ssem, ag_rsem, ag_srcs, ag_lands, ag_token = split_start(
        "gather", later_split, [(N_CHIPS,) + s.shape for s in later_split], w_in_g, "ag_later_start")
    w_in_t = w_in_g.reshape(-1, D)
    w_dt_t = jnp.pad(w_in_t[DI + CONVD:], ((0, 128 - H), (0, 0)))

    c8, cw8, cb8, ng8 = _gather_packed([c[0], a_conv_w[0], a_conv_b[0], a_norm_g[0]], "allgather_small_params")
    conv_w = _by_chip(cw8, 1)
    conv_b = _by_chip(cb8, 0).reshape(1, CONVD)
    norm_g = _by_chip(ng8, 0).reshape(1, DI)

    mod_s = ada_fwd(c8, ada_w)
    (mod8,) = _gather_packed([mod_s], "allgather_small_mod")
    mods = _by_chip(mod8, 2)
    mod = lax.dynamic_index_in_dim(mods, dev, axis=1, keepdims=False) + ada_b
    shift = [mod[l:l + 1, :D] for l in range(DEPTH)]
    scale = [mod[l:l + 1, D:2 * D] for l in range(DEPTH)]
    gate = [mod[l:l + 1, 2 * D:] for l in range(DEPTH)]
    lg = [ln_g[l:l + 1] for l in range(DEPTH)]
    lb = [ln_b[l:l + 1] for l in range(DEPTH)]

    h0 = modulate(xin, scale[0] + ag_token[0:1, 0:1], shift[0], "modulate0")
    zx = mm_nt(h0, w_in_t, BF16, "mm_in_zx", kw_rows=DI + CONVD)
    dtp = mm_nt(h0, w_dt_t, F32, "mm_in_dt")
    xbc = conv_fwd(zx, DI, conv_w, conv_b)
    dtp_g = jnp.transpose(dtp[:, :H].reshape(L, G, Kh), (1, 0, 2))
    dtp_gT = jnp.transpose(dtp_g, (0, 2, 1))
    vecs = [a_dt_bias.reshape(G, 1, Kh), a_dt_bias.reshape(G, Kh, 1), a_A_log.reshape(G, 1, Kh),
            a_A_log.reshape(G, Kh, 1), a_D.reshape(G, 1, Kh), a_D.reshape(G, Kh, 1)]
    y_ssd, states, yn = ssd_fwd(xbc, dtp_g, dtp_gT, *vecs, zx, norm_g, DI)
    later_split, ag_lands = split_wait("gather", ag_ssem, ag_rsem, ag_srcs, ag_lands, yn, "ag_later_wait")
    (land_out,) = pass_to_sibling(ag_lands[:1])
    ps_ssem, ps_rsem, lands_b, ps_token = pass_start(ag_lands[1:], land_out, "ag_pass_start")

    def place_own(o, s, full):
        return lax.dynamic_update_index_in_dim(o, s, chip, 0).reshape((N_CHIPS,) + full.shape)

    w_out_g = place_own(land_out, later_split[0], later[0])
    ymix0 = mm_nn(yn, w_out_g.reshape(-1, D), F32, "mm_out_a", after=ps_token)
    x1, x1b, h1 = ln_mid(xin, ymix0, gate[0], lg[0], lb[0], scale[1], shift[1])
    lands_b = pass_wait(ps_ssem, ps_rsem, lands_b, x1b, "ag_pass_wait")
    w_kv_g, w_bin_g, w_bout_g = [place_own(o, s, full) for o, s, full in zip(lands_b, later_split[1:], later[1:])]

    n_grp = len(DIL_PATTERNS)
    cb = HW // 512
    assert w_bin_g.shape[2] == HW
    kv3 = [mm_cols_dilated(x1b, w_kv_g, [g * cb + t for t in range(cb)] + [(n_grp + g) * cb + t for t in range(cb)],
                           DIL_PATTERNS[g][1], f"mm_kv_{g}") for g in range(n_grp)]
    q3 = [mm_cols_dilated(h1, w_bin_g, [g], DIL_PATTERNS[g][1], f"mm_q_{g}", tn=HW) for g in range(n_grp)]
    z_b = mm_nn(h1, w_bin_g[n_grp], BF16, "mm_z_b")
    os_, lses = [], []
    for gi in range(len(DIL_PATTERNS)):
        o, lse = attn_fwd(q3[gi], kv3[gi], gi)
        os_.append(o)
        lses.append(lse)
    om = merge_fwd(os_, lses, z_b)
    ymix1 = mm_nn(om, w_bout_g, F32, "mm_out_b", stack="col")
    dres2, dy2, dg1, db1, dgate1, sq = ln_final_fwd_bwd(x1, ymix1, gate[1], lg[1], lb[1], tgt)
    loss_part = 0.5 * jnp.sum(sq) / D

    g_bout = mm_tn(om, dy2, BF16, "mm_gw_out_b", stack="col")
    dgated = mm_nt(dy2, w_bout_g, BF16, "mm_gx_out_b", stack="col")
    dos, dprs, dz_b = merge_bwd(dgated, os_, lses, z_b)
    dqs, dks, dvs = [], [], []
    for gi in range(len(DIL_PATTERNS)):
        dq, dk, dv = attn_bwd(q3[gi], kv3[gi], dos[gi], lses[gi], dprs[gi], gi)
        dqs.append(dq)
        dks.append(dk)
        dvs.append(dv)
    dqz = jnp.concatenate(dqs + [dz_b], axis=1)
    dkv = jnp.concatenate(dks + dvs, axis=1)
    g_bin = mm_tn(h1, dqz, BF16, "mm_gw_in_b", stack="col")
    dh1 = mm_nt(dqz, w_bin_g, BF16, "mm_gx_in_b", stack="col")
    g_kv = mm_tn(x1b, dkv, BF16, "mm_gw_kv", stack="col")

    core = ac.astype(jnp.int32).reshape(1)
    chip_i = chip.astype(jnp.int32).reshape(1)

    def begin_exchange(gs, tag):
        shapes = [(g.shape[0], g.shape[1] // 2, g.shape[2]) for g in gs]
        return split_start("sibling", gs, shapes, gs[0], "rs_x%s_start" % tag)

    def begin_scatter(gs, nms, tag, exchange=None, after=None, by_cols=False):
        if exchange is None:
            sib = exchange_halves_to_sibling(gs, "rs_sibling_exchange_" + tag, by_cols=by_cols)
        else:
            gs, sib = split_wait("sibling", exchange[0], exchange[1], exchange[2], exchange[3], after,
                                 "rs_x%s_wait" % tag)
        parts = [add_half(g, a, core, "rs_add_" + nm, by_cols=by_cols) for g, a, nm in zip(gs, sib, nms)]
        return split_start("scatter", parts, [(3,) + t.shape[1:] for t in parts], parts[0], "rs_%s_start" % tag)

    def finish_scatter(handles, after, tag):
        nms, owns, landed = [], [], []
        for k, (handle, hn) in enumerate(handles):
            parts, lands = split_wait("scatter", handle[0], handle[1], handle[2], handle[3], after,
                                      "rs_%s%d_wait" % (tag, k))
            nms += hn
            owns += list(parts)
            landed += list(lands)
        halves = [sum_partials(own, t, chip_i, "rs_sum_" + nm) for own, t, nm in zip(owns, landed, nms)]
        theirs = join_halves(halves, "rs_join_halves_" + tag)
        return dict(zip(nms, zip(halves, theirs)))

    names_b = ["kv", "in_b", "out_b"]
    ex_b = begin_exchange([g_kv, g_bin, g_bout], "b")
    dx1_kv = mm_nt(dkv, w_kv_g, BF16, "mm_gx_kv", stack="col", after=ex_b[4])
    rs_b = begin_scatter(None, names_b, "b", exchange=ex_b, after=dx1_kv)

    dres1, dy1, dg0, db0, dgate0, dscale1, dshift1 = mod_ln_bwd(
        dres2, dh1, dx1_kv, x1, scale[1], xin, ymix0, gate[0] + rs_b[4][0:1, 0:1], lg[0])
    g_out = mm_tn(yn, dy1, BF16, "mm_gw_out_a", stack="row")
    ex_a1 = begin_exchange([g_out], "a1")
    dyn = mm_nt(dy1, w_out_g, BF16, "mm_gx_out_a", stack="row", after=ex_a1[4])
    rs_a1 = begin_scatter(None, ["out_a"], "a1", exchange=ex_a1, after=dyn)
    dxs, dB, dC, ddtp_g, dbias_g, dalog_g, dD_g, dz_a, dnorm_g = ssd_bwd(
        xbc, dtp_g, dtp_gT, *vecs, states, dyn, y_ssd, zx, norm_g + rs_a1[4][0:1, 0:1], DI)
    dzx, dws, dbs, lo = dz_a, [], [], 0
    for tag, gpart in (("xs", dxs), ("b", dB), ("c", dC)):
        hi = lo + gpart.shape[1]
        dzx, dw_p, db_p = conv_bwd(zx, DI + lo, conv_w[:, lo:hi], conv_b[:, lo:hi], gpart, dzx, "conv_bwd_" + tag)
        dws.append(dw_p)
        dbs.append(db_p)
        lo = hi
    dconv_w = jnp.concatenate(dws, axis=1)
    dconv_b = jnp.concatenate(dbs, axis=1)
    ddtp = jnp.pad(jnp.transpose(ddtp_g, (1, 0, 2)).reshape(L, H), ((0, 0), (0, 128 - H)))
    g_inT = mm_tn(dzx, h0, BF16, "mm_gw_in_zx", m_rows=DI + CONVD + H)
    g_dtT = mm_tn(ddtp, h0, BF16, "mm_gw_in_dt")
    g_inT = lax.dynamic_update_slice(g_inT, g_dtT[:H], (DI + CONVD, 0))
    rs_a2 = begin_scatter([g_inT.reshape(N_CHIPS, -1, D)], ["in_a"], "a2", by_cols=True)
    dh0 = mm_nn(dzx, w_in_t, BF16, "mm_gx_in_zx", after=rs_a2[4])
    dh0_dt = mm_nn(ddtp, w_dt_t, F32, "mm_gx_in_dt")
    grad_x, dscale0, dshift0 = mod_bwd(dres1, dh0, dh0_dt, xin, scale[0] + rs_a2[4][0:1, 0:1], "mod_bwd0")
    g_halves = finish_scatter([(rs_b, names_b)], grad_x, "b")

    def step_halves(w, m, v, nm):
        shp = w.shape
        mine, theirs_ = g_halves[nm]
        outs4 = adamw_halves(w.reshape(-1, shp[-1]), mine, theirs_, m.reshape(-1, shp[-1]), v.reshape(-1, shp[-1]),
                             core, "adamw_" + nm)
        return tuple(t.reshape(shp) for t in outs4)

    big = {
        "kv_w": step_halves(kv_w, m_kv_w, v_kv_w, "kv"),
        "b_in_w": step_halves(b_in_w, m_b_in_w, v_b_in_w, "in_b"),
        "b_out_w": step_halves(b_out_w, m_b_out_w, v_b_out_w, "out_b"),
    }
    g_halves.update(finish_scatter([(rs_a1, ["out_a"]), (rs_a2, ["in_a"])], big["kv_w"][1], "a"))
    g_halves["in_a"] = tuple(jnp.transpose(t) for t in g_halves["in_a"])
    big["a_in_w"] = step_halves(a_in_w, m_a_in_w, v_a_in_w, "in_a")
    big["a_out_w"] = step_halves(a_out_w, m_a_out_w, v_a_out_w, "out_a")

    dmod = jnp.concatenate([jnp.concatenate([dshift0, dscale0, dgate0], axis=1),
                            jnp.concatenate([dshift1, dscale1, dgate1], axis=1)], axis=0)
    small_parts = [jnp.concatenate([dg0, dg1], axis=0), jnp.concatenate([db0, db1], axis=0),
                   dbias_g.reshape(1, H), dalog_g.reshape(1, H), dD_g.reshape(1, H),
                   dconv_w, dconv_b, dnorm_g, loss_part.reshape(1, 1)]
    small_shapes = [p.shape for p in small_parts]
    packed = jnp.concatenate([_pack([dmod]), _pack(small_parts)], axis=0)
    n_mod_rows = _pack([dmod]).shape[0]
    gathered = allgather_small(packed, "allgather_small_grads", after=g_halves["in_a"][1]).reshape(N_DEV, -1, 128)
    dmod8 = gathered[:, :n_mod_rows].reshape(N_DEV, -1)[:, :2 * 3 * D].reshape(N_DEV, DEPTH, 3 * D)
    summed = sum_leading(gathered, "sum_small")
    g_ada_b = summed[:n_mod_rows].reshape(-1)[:2 * 3 * D].reshape(DEPTH, 3 * D)
    (g_ln_g, g_ln_b, g_dt_bias, g_a_log, g_dsk, g_conv_w, g_conv_b, g_norm_g, loss_all) = _unpack(
        summed[n_mod_rows:].reshape(-1), small_shapes)
    loss = loss_all.reshape(())
    Cs = CONVD // N_CHIPS
    g_conv_w_s = lax.dynamic_slice_in_dim(g_conv_w, chip * Cs, Cs, axis=1)
    g_conv_b_s = lax.dynamic_slice_in_dim(g_conv_b, chip * Cs, Cs, axis=1)
    g_norm_g_s = lax.dynamic_slice_in_dim(g_norm_g, chip * (DI // N_CHIPS), DI // N_CHIPS, axis=1)
    dmod_s = jnp.transpose(lax.dynamic_slice_in_dim(dmod8, chip * Ws, Ws, axis=2), (1, 0, 2))

    def step2d(w, g, m, v, nm):
        shp = w.shape
        d_, m_, v_ = adamw(w.reshape(-1, shp[-1]), g.reshape(-1, shp[-1]), m.reshape(-1, shp[-1]),
                           v.reshape(-1, shp[-1]), "adamw_" + nm)
        return g.reshape(shp), d_.reshape(shp), m_.reshape(shp), v_.reshape(shp)

    big["ada_w"] = step2d(ada_w, ada_wgrad(jnp.transpose(c8), dmod_s), m_ada_w, v_ada_w, "ada_w")
    small_names = ["ada_b", "ln_g", "ln_b", "a_conv_w", "a_conv_b", "a_dt_bias", "a_A_log", "a_D", "a_norm_g"]
    small_w = [ada_b, ln_g, ln_b, a_conv_w, a_conv_b, a_dt_bias, a_A_log, a_D, a_norm_g]
    small_m = [m_ada_b, m_ln_g, m_ln_b, m_a_conv_w, m_a_conv_b, m_a_dt_bias, m_a_A_log, m_a_D, m_a_norm_g]
    small_v = [v_ada_b, v_ln_g, v_ln_b, v_a_conv_w, v_a_conv_b, v_a_dt_bias, v_a_A_log, v_a_D, v_a_norm_g]
    small_g = [g_ada_b, g_ln_g, g_ln_b, g_conv_w_s, g_conv_b_s, g_dt_bias, g_a_log, g_dsk, g_norm_g_s]
    shapes = [w.shape for w in small_w]
    small_g = [g.reshape(s) for g, s in zip(small_g, shapes)]
    d_p, m_p, v_p = adamw(_pack(small_w), _pack(small_g), _pack(small_m), _pack(small_v), "adamw_small")
    small = {}
    for nm, g, d_, m_, v_ in zip(small_names, small_g, _unpack(d_p.reshape(-1), shapes), _unpack(m_p.reshape(-1), shapes),
                                 _unpack(v_p.reshape(-1), shapes)):
        small[nm] = (g, d_, m_, v_)
    allw = {**big, **small}
    order = ["ada_w", "ada_b", "ln_g", "ln_b", "a_in_w", "a_conv_w", "a_conv_b", "a_dt_bias", "a_A_log", "a_D",
             "a_norm_g", "a_out_w", "kv_w", "b_in_w", "b_out_w"]
    outs = [loss, grad_x.reshape(x.shape)]
    for k in range(4):
        outs += [allw[n][k] for n in order]
    return tuple(outs)
```

```python
import functools

import jax
import jax.numpy as jnp
import numpy as np
from jax import lax
from jax.experimental import pallas as pl
from jax.experimental.pallas import tpu as pltpu

F32 = jnp.float32
BF16 = jnp.bfloat16
MESH = pl.DeviceIdType.MESH

DEPTH = 2
ALPHA = (2 * DEPTH) ** 0.25
LN_EPS = 1e-5
RMS_EPS = 1e-5
SSD_P = 64
SSD_N = 128
SSD_Q = 256
SSD_G = 8
CONV_W = 4
DIL_PATTERNS = ((128, 1), (512, 4), (2048, 16))
DIL_H = 8
DIL_E = 128
DIL_BLK = 128
ADAM_LR, ADAM_B1, ADAM_B2, ADAM_EPS, ADAM_WD, ADAM_STEP = 0.001, 0.9, 0.999, 1e-08, 0.01, 10

VMEM_LIMIT = 56 * 1024 * 1024
N_CHIPS = 4
N_DEV = 8


def _tile(dim, target, mult=128):
    if dim <= target:
        return dim
    t = (target // mult) * mult
    while t >= mult:
        if dim % t == 0:
            return t
        t -= mult
    return dim


def _cp(sem):
    return pltpu.CompilerParams(dimension_semantics=sem, vmem_limit_bytes=VMEM_LIMIT)


def _sigmoid(x):
    return 1.0 / (1.0 + jnp.exp(-x))


def _silu(x):
    return x * _sigmoid(x)


def _dsilu(x):
    s = _sigmoid(x)
    return s * (1.0 + x * (1.0 - s))


def _softplus(x):
    return jnp.maximum(x, 0.0) + jnp.log(1.0 + jnp.exp(-jnp.abs(x)))


def _mm_call(a, b, out_shape, grid, a_spec, b_spec, o_spec, acc_shape, dims, name, after=None):
    nk = grid[2]
    extra = [] if after is None else [after]

    def prod(a_ref, b_ref):
        return lax.dot_general(a_ref[...].astype(BF16), b_ref[...].astype(BF16), (dims, ((), ())),
                               preferred_element_type=F32)

    def body_single(a_ref, b_ref, *rest):
        o_ref = rest[len(extra)]
        o_ref[...] = prod(a_ref, b_ref).astype(o_ref.dtype)

    def body_multi(a_ref, b_ref, *rest):
        o_ref, acc_ref = rest[len(extra):]
        k = pl.program_id(2)

        @pl.when(k == 0)
        def _():
            acc_ref[...] = prod(a_ref, b_ref)

        @pl.when(jnp.logical_and(k > 0, k < nk - 1))
        def _():
            acc_ref[...] += prod(a_ref, b_ref)

        @pl.when(k == nk - 1)
        def _():
            o_ref[...] = (acc_ref[...] + prod(a_ref, b_ref)).astype(o_ref.dtype)

    return pl.pallas_call(
        body_single if nk == 1 else body_multi, grid=grid, in_specs=[a_spec, b_spec] + [_ANY] * len(extra),
        out_specs=o_spec, out_shape=out_shape, scratch_shapes=[] if nk == 1 else [pltpu.VMEM(acc_shape, F32)],
        compiler_params=_cp(("parallel", "parallel", "arbitrary")), name=name)(a, b, *extra)


def mm_nn(a, b, out_dtype, name, stack=None, tm=1024, tn=1024, tk=2048, n_cols=None, after=None):
    M, K = a.shape
    if stack is None:
        N = b.shape[1] if n_cols is None else n_cols
        tn, tk = _tile(N, tn), _tile(K, tk)
        b_spec = pl.BlockSpec((tk, tn), lambda i, j, k: (k, j))
    elif stack == "col":
        S, _, Ns = b.shape
        N = S * Ns
        tn, tk = _tile(Ns, tn), _tile(K, tk)
        npb = Ns // tn
        b_spec = pl.BlockSpec((None, tk, tn), lambda i, j, k: (j // npb, k, j % npb))
    else:
        S, Ks, N = b.shape
        tn, tk = _tile(N, tn), _tile(Ks, tk)
        kpb = Ks // tk
        b_spec = pl.BlockSpec((None, tk, tn), lambda i, j, k: (k // kpb, k % kpb, j))
    tm = _tile(M, tm)
    return _mm_call(a, b, jax.ShapeDtypeStruct((M, N), out_dtype), (M // tm, N // tn, K // tk),
                    pl.BlockSpec((tm, tk), lambda i, j, k: (i, k)), b_spec,
                    pl.BlockSpec((tm, tn), lambda i, j, k: (i, j)), (tm, tn), ((1,), (0,)), name, after=after)


def mm_nn_mod_bwd(a, b, extra, dres, xin, scale, name, after, tm=512, tn=1024, tk=2048):
    M, K = a.shape
    N = b.shape[1]
    tm, tn, tk = _tile(M, tm), _tile(N, tn), _tile(K, tk)
    nk = K // tk

    def body(a_ref, b_ref, e_ref, r_ref, x_ref, sc_ref, aft_ref, dx_ref, dsc_ref, dsh_ref, acc_ref):
        k = pl.program_id(2)

        def prod():
            return jnp.dot(a_ref[...], b_ref[...], preferred_element_type=F32)

        @pl.when(k == 0)
        def _():
            acc_ref[...] = prod()

        @pl.when(jnp.logical_and(k > 0, k < nk - 1))
        def _():
            acc_ref[...] += prod()

        @pl.when(k == nk - 1)
        def _():
            dh = acc_ref[...] + prod() + e_ref[...]
            dx_ref[...] = r_ref[...] + dh * (1.0 + sc_ref[...])
            dsc_ref[...] = jnp.sum(dh * x_ref[...], axis=0, keepdims=True)
            dsh_ref[...] = jnp.sum(dh, axis=0, keepdims=True)

    assert nk >= 2
    tile = pl.BlockSpec((tm, tn), lambda i, j, k: (i, j))
    part = pl.BlockSpec((None, 1, tn), lambda i, j, k: (i, 0, j))
    return pl.pallas_call(
        body, grid=(M // tm, N // tn, nk),
        in_specs=[pl.BlockSpec((tm, tk), lambda i, j, k: (i, k)), pl.BlockSpec((tk, tn), lambda i, j, k: (k, j)),
                  tile, tile, tile, pl.BlockSpec((1, tn), lambda i, j, k: (0, j)), _ANY],
        out_specs=[tile, part, part],
        out_shape=[jax.ShapeDtypeStruct((M, N), F32)] + [jax.ShapeDtypeStruct((M // tm, 1, N), F32)] * 2,
        scratch_shapes=[pltpu.VMEM((tm, tn), F32)],
        compiler_params=_cp(("parallel", "parallel", "arbitrary")), name=name)(a, b, extra, dres, xin, scale, after)


def mm_cols_dilated(a, b, gcols, d, name, tm=1024, tn=512):
    L, K = a.shape
    S, _, Ns = b.shape
    tm, tn = _tile(L, tm), _tile(Ns, tn)
    npb = Ns // tn
    nj = len(gcols)
    rows = tm // d

    def body(cols_ref, a_ref, b_ref, o_ref, *scr):
        prod = jnp.dot(a_ref[...], b_ref[...], preferred_element_type=F32)
        if d == 1:
            o_ref[0] = prod.astype(BF16)
        else:
            for c in range(tn // 128):
                scr[0][c] = prod[:, c * 128:(c + 1) * 128]
            for r in range(d):
                for c in range(tn // 128):
                    o_ref[r, :, c * 128:(c + 1) * 128] = scr[0].at[c][pl.ds(r, rows, stride=d), :].astype(BF16)

    return pl.pallas_call(
        body,
        grid_spec=pltpu.PrefetchScalarGridSpec(
            num_scalar_prefetch=1, grid=(L // tm, nj),
            in_specs=[pl.BlockSpec((tm, K), lambda i, j, c: (i, 0)),
                      pl.BlockSpec((None, K, tn), lambda i, j, c: (c[j] // npb, 0, c[j] % npb))],
            out_specs=pl.BlockSpec((d, rows, tn), lambda i, j, c: (0, i, j)),
            scratch_shapes=[] if d == 1 else [pltpu.VMEM((tn // 128, tm, 128), F32)]),
        out_shape=jax.ShapeDtypeStruct((d, L // d, nj * tn), BF16),
        compiler_params=_cp(("parallel", "arbitrary")), name=name)(jnp.asarray(gcols, jnp.int32), a, b)


def mm_nt(a, b, out_dtype, name, stack=None, tm=1024, tn=1024, tk=2048, after=None, kw_rows=None):
    M, C = a.shape
    if stack is None:
        Kw = b.shape[0] if kw_rows is None else kw_rows
        tn, tk = _tile(Kw, tn), _tile(C, tk)
        b_spec = pl.BlockSpec((tn, tk), lambda i, j, k: (j, k))
    elif stack == "col":
        S, Kw, Cs = b.shape
        tn, tk = _tile(Kw, tn), _tile(Cs, tk)
        cpb = Cs // tk
        b_spec = pl.BlockSpec((None, tn, tk), lambda i, j, k: (k // cpb, j, k % cpb))
    else:
        S, Ks, _ = b.shape
        Kw = S * Ks
        tn, tk = _tile(Ks, tn), _tile(C, tk)
        jpb = Ks // tn
        b_spec = pl.BlockSpec((None, tn, tk), lambda i, j, k: (j // jpb, j % jpb, k))
    tm = _tile(M, tm)
    return _mm_call(a, b, jax.ShapeDtypeStruct((M, Kw), out_dtype), (M // tm, Kw // tn, C // tk),
                    pl.BlockSpec((tm, tk), lambda i, j, k: (i, k)), b_spec,
                    pl.BlockSpec((tm, tn), lambda i, j, k: (i, j)), (tm, tn), ((1,), (1,)), name, after=after)


def mm_tn(a, b, out_dtype, name, stack=None, n_stack=N_CHIPS, tm=1024, tn=1024, tk=2048, m_rows=None):
    L, M = a.shape
    N = b.shape[1]
    tk = _tile(L, tk)
    if stack is None:
        tm, tn = _tile(M, tm), _tile(N, tn)
        o_spec = pl.BlockSpec((tm, tn), lambda i, j, k: (i, j))
        out_shape = (M if m_rows is None else m_rows, N)
    elif stack == "col":
        Ns = N // n_stack
        tm, tn = _tile(M, tm), _tile(Ns, tn)
        npb = Ns // tn
        o_spec = pl.BlockSpec((None, tm, tn), lambda i, j, k: (j // npb, i, j % npb))
        out_shape = (n_stack, M, Ns)
    else:
        Ms = M // n_stack
        tm, tn = _tile(Ms, tm), _tile(N, tn)
        mpb = Ms // tm
        o_spec = pl.BlockSpec((None, tm, tn), lambda i, j, k: (i // mpb, i % mpb, j))
        out_shape = (n_stack, Ms, N)
    return _mm_call(a, b, jax.ShapeDtypeStruct(out_shape, out_dtype), (M // tm, N // tn, L // tk),
                    pl.BlockSpec((tk, tm), lambda i, j, k: (k, i)), pl.BlockSpec((tk, tn), lambda i, j, k: (k, j)),
                    o_spec, (tm, tn), ((0,), (0,)), name)


def _row_specs(tr, widths):
    return [pl.BlockSpec((tr, w), lambda i: (i, 0)) for w in widths]


def _vec_spec(w):
    return pl.BlockSpec((1, w), lambda i: (0, 0))


def _acc_rows(ref, val, i):
    s = jnp.sum(val, axis=0, keepdims=True)

    @pl.when(i == 0)
    def _():
        ref[...] = s

    @pl.when(i > 0)
    def _():
        ref[...] += s


def modulate(x, scale, shift, name):
    L, D = x.shape
    tr = _tile(L, 512, 16)

    def body(x_ref, sc_ref, sh_ref, h_ref):
        h_ref[...] = (x_ref[...] * (1.0 + sc_ref[...]) + sh_ref[...]).astype(BF16)

    return pl.pallas_call(
        body, grid=(L // tr,), in_specs=_row_specs(tr, [D]) + [_vec_spec(D)] * 2, out_specs=_row_specs(tr, [D])[0],
        out_shape=jax.ShapeDtypeStruct((L, D), BF16), compiler_params=_cp(("parallel",)), name=name)(x, scale, shift)


def _ln_core(x, y, gate, g, b):
    u = ALPHA * x + (1.0 + gate) * y
    mu = jnp.mean(u, axis=-1, keepdims=True)
    d = u - mu
    var = jnp.mean(d * d, axis=-1, keepdims=True)
    rstd = lax.rsqrt(var + LN_EPS)
    xhat = d * rstd
    return xhat * g + b, xhat, rstd


def ln_mid(x, y, gate, g, b, scale, shift):
    L, D = x.shape
    tr = _tile(L, 256, 16)

    def body(x_ref, y_ref, gate_ref, g_ref, b_ref, sc_ref, sh_ref, x1_ref, x1b_ref, h_ref):
        x1, _, _ = _ln_core(x_ref[...], y_ref[...], gate_ref[...], g_ref[...], b_ref[...])
        x1_ref[...] = x1
        x1b_ref[...] = x1.astype(BF16)
        h_ref[...] = (x1 * (1.0 + sc_ref[...]) + sh_ref[...]).astype(BF16)

    return pl.pallas_call(
        body, grid=(L // tr,), in_specs=_row_specs(tr, [D, D]) + [_vec_spec(D)] * 5,
        out_specs=_row_specs(tr, [D, D, D]),
        out_shape=[jax.ShapeDtypeStruct((L, D), F32), jax.ShapeDtypeStruct((L, D), BF16),
                   jax.ShapeDtypeStruct((L, D), BF16)],
        compiler_params=_cp(("parallel",)), name="ln_mid")(x, y, gate, g, b, scale, shift)


def _ln_bwd_rows(dout_v, xhat, rstd, g):
    dxh = dout_v * g
    m1 = jnp.mean(dxh, axis=-1, keepdims=True)
    m2 = jnp.mean(dxh * xhat, axis=-1, keepdims=True)
    return rstd * (dxh - m1 - xhat * m2)


def ln_final_fwd_bwd(x, y, gate, g, b, target):
    L, D = x.shape
    tr = _tile(L, 256, 16)

    def body(x_ref, y_ref, gate_ref, g_ref, b_ref, t_ref, dres_ref, dy_ref, dg_ref, db_ref, dgate_ref, sq_ref):
        i = pl.program_id(0)
        yv = y_ref[...]
        out, xhat, rstd = _ln_core(x_ref[...], yv, gate_ref[...], g_ref[...], b_ref[...])
        err = out - t_ref[...]
        dout_v = err * (1.0 / D)
        du = _ln_bwd_rows(dout_v, xhat, rstd, g_ref[...])
        dres_ref[...] = ALPHA * du
        dy_ref[...] = ((1.0 + gate_ref[...]) * du).astype(BF16)
        _acc_rows(dg_ref, dout_v * xhat, i)
        _acc_rows(db_ref, dout_v, i)
        _acc_rows(dgate_ref, du * yv, i)
        _acc_rows(sq_ref, err * err, i)

    return pl.pallas_call(
        body, grid=(L // tr,), in_specs=_row_specs(tr, [D, D]) + [_vec_spec(D)] * 3 + _row_specs(tr, [D]),
        out_specs=_row_specs(tr, [D, D]) + [_vec_spec(D)] * 4,
        out_shape=[jax.ShapeDtypeStruct((L, D), F32), jax.ShapeDtypeStruct((L, D), BF16)]
        + [jax.ShapeDtypeStruct((1, D), F32)] * 4,
        compiler_params=_cp(("arbitrary",)), name="ln_final_fwd_bwd")(x, y, gate, g, b, target)


def mod_ln_bwd(dres_in, dh, dskip, xmid, scale, x, y, gate, g):
    L, D = x.shape
    tr = _tile(L, 256, 16)

    def body(dres_ref, dh_ref, dskip_ref, xm_ref, sc_ref, x_ref, y_ref, gate_ref, g_ref,
             dres_out, dy_ref, dg_ref, db_ref, dgate_ref, dsc_ref, dsh_ref):
        i = pl.program_id(0)
        dh_v = dh_ref[...].astype(F32)
        dout_v = dres_ref[...] + dskip_ref[...].astype(F32) + dh_v * (1.0 + sc_ref[...])
        _acc_rows(dsc_ref, dh_v * xm_ref[...], i)
        _acc_rows(dsh_ref, dh_v, i)
        yv = y_ref[...]
        _, xhat, rstd = _ln_core(x_ref[...], yv, gate_ref[...], g_ref[...], 0.0)
        du = _ln_bwd_rows(dout_v, xhat, rstd, g_ref[...])
        dres_out[...] = ALPHA * du
        dy_ref[...] = ((1.0 + gate_ref[...]) * du).astype(BF16)
        _acc_rows(dg_ref, dout_v * xhat, i)
        _acc_rows(db_ref, dout_v, i)
        _acc_rows(dgate_ref, du * yv, i)

    return pl.pallas_call(
        body, grid=(L // tr,),
        in_specs=_row_specs(tr, [D] * 4) + [_vec_spec(D)] + _row_specs(tr, [D, D]) + [_vec_spec(D)] * 2,
        out_specs=_row_specs(tr, [D, D]) + [_vec_spec(D)] * 5,
        out_shape=[jax.ShapeDtypeStruct((L, D), F32), jax.ShapeDtypeStruct((L, D), BF16)]
        + [jax.ShapeDtypeStruct((1, D), F32)] * 5,
        compiler_params=_cp(("arbitrary",)), name="mod_ln_bwd")(dres_in, dh, dskip, xmid, scale, x, y, gate, g)


CONV_HALO = 16


def _conv_rows(x_ref, i, tr, L):
    nblk = L // tr
    s = pl.multiple_of(i * tr, CONV_HALO)
    cur = x_ref[pl.ds(s, tr), :].astype(F32)
    sp = pl.multiple_of(jnp.maximum(i * tr - CONV_HALO, 0), CONV_HALO)
    sn = pl.multiple_of(jnp.minimum(i * tr + tr, L - CONV_HALO), CONV_HALO)
    prev = x_ref[pl.ds(sp, CONV_HALO), :].astype(F32) * (i > 0).astype(F32)
    nxt = x_ref[pl.ds(sn, CONV_HALO), :].astype(F32) * (i < nblk - 1).astype(F32)
    return jnp.concatenate([prev, cur, nxt], axis=0)


def _shift_rows(v, j):
    n = v.shape[0]
    return v if j % n == 0 else pltpu.roll(v, j % n, 0)


def _conv_taps(xe):
    return [_shift_rows(xe, CONV_W - 1 - k) for k in range(CONV_W)]


def _conv_eval(taps, w_ref, b_ref):
    c = b_ref[...] + w_ref[0:1, :] * taps[0]
    for k in range(1, CONV_W):
        c = c + w_ref[k:k + 1, :] * taps[k]
    return c


def conv_fwd(zx, col0, conv_w, conv_b):
    L = zx.shape[0]
    C = conv_w.shape[1]
    tc = _tile(C, 512)
    tr = _tile(L, 512, CONV_HALO)
    off = col0 // tc

    def body(x_ref, w_ref, b_ref, o_ref):
        i = pl.program_id(1)
        xe = _conv_rows(x_ref, i, tr, L)
        c = _conv_eval(_conv_taps(xe), w_ref, b_ref)[CONV_HALO:CONV_HALO + tr]
        o_ref[...] = _silu(c).astype(BF16)

    return pl.pallas_call(
        body, grid=(C // tc, L // tr),
        in_specs=[pl.BlockSpec((L, tc), lambda j, i: (0, off + j)), pl.BlockSpec((CONV_W, tc), lambda j, i: (0, j)),
                  pl.BlockSpec((1, tc), lambda j, i: (0, j))],
        out_specs=pl.BlockSpec((tr, tc), lambda j, i: (i, j)),
        out_shape=jax.ShapeDtypeStruct((L, C), BF16), compiler_params=_cp(("parallel", "arbitrary")),
        name="conv_fwd")(zx, conv_w, conv_b)


def conv_bwd(zx, col0, conv_w, conv_b, g, dzx, name):
    L = zx.shape[0]
    C = conv_w.shape[1]
    tc = _tile(C, 512)
    tr = _tile(L, 512, CONV_HALO)
    off = col0 // tc
    H = CONV_HALO

    def body(x_ref, g_ref, w_ref, b_ref, buf_ref, dx_ref, dw_ref, db_ref):
        i = pl.program_id(1)
        xe = _conv_rows(x_ref, i, tr, L)
        ge = _conv_rows(g_ref, i, tr, L)
        taps = _conv_taps(xe)
        dc = ge * _dsilu(_conv_eval(taps, w_ref, b_ref))
        dx = w_ref[CONV_W - 1:CONV_W, :] * dc
        for k in range(CONV_W - 1):
            dx = dx + w_ref[k:k + 1, :] * _shift_rows(dc, -(CONV_W - 1 - k))
        dx_ref[...] = dx[H:H + tr].astype(BF16)
        dcc = dc[H:H + tr]
        rows = [jnp.sum(dcc * taps[k][H:H + tr], axis=0, keepdims=True) for k in range(CONV_W)]
        dwv = jnp.concatenate(rows + [jnp.zeros((8 - CONV_W, tc), F32)], axis=0)
        dbv = jnp.sum(dcc, axis=0, keepdims=True)

        @pl.when(i == 0)
        def _():
            dw_ref[...] = dwv
            db_ref[...] = dbv

        @pl.when(i > 0)
        def _():
            dw_ref[...] += dwv
            db_ref[...] += dbv

    dx, dw, db = pl.pallas_call(
        body, grid=(C // tc, L // tr),
        in_specs=[pl.BlockSpec((L, tc), lambda j, i: (0, off + j)), pl.BlockSpec((L, tc), lambda j, i: (0, j)),
                  pl.BlockSpec((CONV_W, tc), lambda j, i: (0, j)), pl.BlockSpec((1, tc), lambda j, i: (0, j)), _ANY],
        out_specs=[pl.BlockSpec((tr, tc), lambda j, i: (i, off + j)), pl.BlockSpec((8, tc), lambda j, i: (0, j)),
                   pl.BlockSpec((1, tc), lambda j, i: (0, j))],
        out_shape=[jax.ShapeDtypeStruct(dzx.shape, BF16), jax.ShapeDtypeStruct((8, C), F32),
                   jax.ShapeDtypeStruct((1, C), F32)],
        input_output_aliases={4: 0},
        compiler_params=_cp(("parallel", "arbitrary")), name=name)(zx, g, conv_w, conv_b, dzx)
    return dx, dw[:CONV_W], db


_NN = (((1,), (0,)), ((), ()))


def _pieces(x, n):
    out, r = [], x
    for _ in range(n):
        p = r.astype(BF16)
        out.append(p)
        r = r - p.astype(F32)
    return out


def _dot01(a, b01, n, dims=_NN):
    b = b01.astype(BF16)
    return functools.reduce(lambda u, v: u + v,
                            [lax.dot_general(p, b, dims, preferred_element_type=F32) for p in _pieces(a, n)])


def _dot01_left(a01, b, n, dims=_NN):
    a = a01.astype(BF16)
    return functools.reduce(lambda u, v: u + v,
                            [lax.dot_general(a, p, dims, preferred_element_type=F32) for p in _pieces(b, n)])


def _ssd_common(dtp_ref, dtpT_ref, bias_ref, biasT_ref, alog_ref, alogT_ref, b_ref, c_ref):
    Q = SSD_Q
    dt = _softplus(dtp_ref[...] + bias_ref[...])
    A = -jnp.exp(alog_ref[...])
    row = lax.broadcasted_iota(jnp.int32, (Q, Q), 0)
    col = lax.broadcasted_iota(jnp.int32, (Q, Q), 1)
    causal = row >= col
    tril = causal.astype(F32)
    Kh = dt.shape[1]
    acum = _dot01_left(tril, dt * A, 3)
    eye = (lax.broadcasted_iota(jnp.int32, (Kh, Kh), 0) == lax.broadcasted_iota(jnp.int32, (Kh, Kh), 1)).astype(F32)
    acumT = _dot01_left(eye, acum, 3, dims=(((1,), (1,)), ((), ())))
    Bm = b_ref[...]
    Cm = c_ref[...]
    cb = lax.dot_general(Cm, Bm, (((1,), (1,)), ((), ())), preferred_element_type=F32)
    return dt, A, causal, row, col, acum, acumT, Bm, Cm, cb


def _ssd_in_specs(Q, GP, N, Kh, DI, cmap):
    nb0 = DI // N
    vec = pl.BlockSpec((None, 1, Kh), lambda g, c: (g, 0, 0))
    vecT = pl.BlockSpec((None, Kh, 1), lambda g, c: (g, 0, 0))
    return [pl.BlockSpec((Q, GP), lambda g, c: (cmap(c), g)),
            pl.BlockSpec((Q, N), lambda g, c: (cmap(c), nb0 + g)),
            pl.BlockSpec((Q, N), lambda g, c: (cmap(c), nb0 + SSD_G + g)),
            pl.BlockSpec((None, Q, Kh), lambda g, c: (g, cmap(c), 0)),
            pl.BlockSpec((None, Kh, Q), lambda g, c: (g, 0, cmap(c))),
            vec, vecT, vec, vecT, vec, vecT]


def _hi(a, b01):
    return _dot01(a, b01, 2)


def _headsum(a, b01):
    return _dot01(a, b01, 1)


def _ssd_heads(dskT_ref, acum, acumT, dt, Kh):
    Q, P, N = SSD_Q, SSD_P, SSD_N
    GP = Kh * P
    sh_p = P.bit_length() - 1
    seg = lambda shape, dim: lax.shift_right_logical(lax.broadcasted_iota(jnp.int32, shape, dim), sh_p)
    E = (seg((Kh, GP), 1) == lax.broadcasted_iota(jnp.int32, (Kh, GP), 0)).astype(F32)
    ET = (seg((GP, Kh), 0) == lax.broadcasted_iota(jnp.int32, (GP, Kh), 1)).astype(F32)
    a_last = acum[Q - 1:Q, :]
    tail = jnp.exp(a_last - acum)
    eLT = jnp.exp(acumT[:, Q - 1:Q])
    rowseg = seg((GP, N), 0)
    eL_b = jnp.zeros((GP, N), F32)
    for k in range(Kh):
        eL_b = jnp.where(rowseg == k, eLT[k:k + 1, :], eL_b)
    return dict(
        E=E, ET=ET, a_last=a_last, tail=tail, eL_b=eL_b,
        dt_all=_hi(dt, E), ea_all=_hi(jnp.exp(acum), E), tail_all=_hi(tail, E),
        dsk_all=jnp.sum(E * dskT_ref[...], axis=0, keepdims=True))


def _head_chunks(GP):
    CW = min(GP, 128)
    return CW, CW // SSD_P, GP // CW


def _head_mask(Q, CW, kk):
    lane = lax.broadcasted_iota(jnp.int32, (Q, CW), 1)
    return jnp.logical_and(lane >= kk * SSD_P, lane < (kk + 1) * SSD_P)


def ssd_fwd(xbc, dtp_g, dtp_gT, bias_g, bias_gT, alog_g, alog_gT, dsk_g, dsk_gT, zx, norm_g, DI):
    L = xbc.shape[0]
    Q, P, N, G = SSD_Q, SSD_P, SSD_N, SSD_G
    GP = DI // G
    Kh = GP // P
    nc = L // Q

    CW, hpc, nch = _head_chunks(GP)
    nt = (((1,), (1,)), ((), ()))
    tn = (((0,), (0,)), ((), ()))

    def body(xs_ref, b_ref, c_ref, dtp_ref, dtpT_ref, bias_ref, biasT_ref, alog_ref, alogT_ref, dsk_ref, dskT_ref,
             z_ref, ng_ref, y_ref, st_ref, yn_ref, state):
        @pl.when(pl.program_id(1) == 0)
        def _():
            state[...] = jnp.zeros(state.shape, F32)

        st_ref[...] = state[...]
        dt, A, causal, row, col, acum, acumT, Bm, Cm, cb = _ssd_common(
            dtp_ref, dtpT_ref, bias_ref, biasT_ref, alog_ref, alogT_ref, b_ref, c_ref)
        hd = _ssd_heads(dskT_ref, acum, acumT, dt, Kh)
        xs = xs_ref[...].astype(F32)
        xdt_all = xs * hd["dt_all"]
        S_all = state[...]
        y_all = (lax.dot_general(Cm, S_all.astype(BF16), nt, preferred_element_type=F32) * hd["ea_all"]
                 + xs * hd["dsk_all"])
        state[...] = S_all * hd["eL_b"] + lax.dot_general(
            (xdt_all * hd["tail_all"]).astype(BF16), Bm, tn, preferred_element_type=F32)
        for ch in range(nch):
            cs = slice(ch * CW, (ch + 1) * CW)
            xc = xdt_all[:, cs]
            acc = y_all[:, cs]
            for kk in range(hpc):
                k = ch * hpc + kk
                decay = jnp.exp(jnp.where(causal, acum[:, k:k + 1] - acumT[k:k + 1, :], -jnp.inf))
                xk = xc if hpc == 1 else jnp.where(_head_mask(Q, CW, kk), xc, 0.0)
                acc = acc + jnp.dot((cb * decay).astype(BF16), xk.astype(BF16), preferred_element_type=F32)
            y_ref[:, cs] = acc.astype(BF16)
        y2 = y_ref[...].astype(F32) * _silu(z_ref[...].astype(F32))
        rr = lax.rsqrt(jnp.mean(y2 * y2, axis=-1, keepdims=True) + RMS_EPS)
        yn_ref[...] = (y2 * rr * ng_ref[...]).astype(BF16)

    tile = pl.BlockSpec((Q, GP), lambda g, c: (c, g))
    return pl.pallas_call(
        body, grid=(G, nc),
        in_specs=_ssd_in_specs(Q, GP, N, Kh, DI, lambda c: c) + [tile, pl.BlockSpec((1, GP), lambda g, c: (0, g))],
        out_specs=[tile, pl.BlockSpec((None, None, GP, N), lambda g, c: (c, g, 0, 0)), tile],
        out_shape=[jax.ShapeDtypeStruct((L, DI), BF16), jax.ShapeDtypeStruct((nc, G, GP, N), F32),
                   jax.ShapeDtypeStruct((L, DI), BF16)],
        scratch_shapes=[pltpu.VMEM((GP, N), F32)], compiler_params=_cp(("parallel", "arbitrary")),
        name="ssd_fwd")(xbc, xbc, xbc, dtp_g, dtp_gT, bias_g, bias_gT, alog_g, alog_gT, dsk_g, dsk_gT, zx, norm_g)


def ssd_bwd(xbc, dtp_g, dtp_gT, bias_g, bias_gT, alog_g, alog_gT, dsk_g, dsk_gT, states, dyn, y, zx, norm_g, DI):
    L = xbc.shape[0]
    Q, P, N, G = SSD_Q, SSD_P, SSD_N, SSD_G
    GP = DI // G
    Kh = GP // P
    nc = L // Q
    rev = lambda c: nc - 1 - c

    CW, hpc, nch = _head_chunks(GP)

    def body(xs_ref, b_ref, c_ref, dtp_ref, dtpT_ref, bias_ref, biasT_ref, alog_ref, alogT_ref, dsk_ref, dskT_ref,
             st_ref, dyn_ref, y_ref, z_ref, ng_ref,
             dxs_ref, dB_ref, dC_ref, ddtp_ref, dbias_ref, dalog_ref, dD_ref, dz_ref, dng_ref, dstate):
        ci = pl.program_id(1)

        @pl.when(ci == 0)
        def _():
            dstate[...] = jnp.zeros(dstate.shape, F32)

        dt, A, causal, row, col, acum, acumT, Bm, Cm, cb = _ssd_common(
            dtp_ref, dtpT_ref, bias_ref, biasT_ref, alog_ref, alogT_ref, b_ref, c_ref)
        tn = (((0,), (0,)), ((), ()))
        nt = (((1,), (1,)), ((), ()))
        hd = _ssd_heads(dskT_ref, acum, acumT, dt, Kh)
        ET, tail = hd["ET"], hd["tail"]
        cbT = lax.dot_general(Bm, Cm, nt, preferred_element_type=F32)
        causalT = row <= col
        xs = xs_ref[...].astype(F32)
        xdt_all = xs * hd["dt_all"]
        yv = y_ref[...].astype(F32)
        zv = z_ref[...].astype(F32)
        dynv = dyn_ref[...].astype(F32)
        sz = _silu(zv)
        y2 = yv * sz
        rr = lax.rsqrt(jnp.mean(y2 * y2, axis=-1, keepdims=True) + RMS_EPS)
        yh = y2 * rr
        dyh = dynv * ng_ref[...]
        dy2 = rr * (dyh - yh * jnp.mean(dyh * yh, axis=-1, keepdims=True))
        dz_ref[...] = (dy2 * yv * _dsilu(zv)).astype(BF16)
        dng_v = jnp.sum(dynv * yh, axis=0, keepdims=True)
        dyb = (dy2 * sz).astype(BF16)
        dy_all = dyb.astype(F32)
        S_all = st_ref[...]
        S_b = S_all.astype(BF16)
        dS_all = dstate[...]
        dS_b = dS_all.astype(BF16)
        CS_all = lax.dot_general(Cm, S_b, nt, preferred_element_type=F32)
        dyE_b = (dy_all * hd["ea_all"]).astype(BF16)
        dC_acc = jnp.dot(dyE_b, S_b, preferred_element_type=F32)
        dS_y = lax.dot_general(dyE_b, Cm, tn, preferred_element_type=F32)
        BdS_all = lax.dot_general(Bm, dS_b, nt, preferred_element_type=F32)
        dB_acc = jnp.dot((xdt_all * hd["tail_all"]).astype(BF16), dS_b, preferred_element_type=F32)
        dtail = _headsum(xdt_all * BdS_all, ET)
        da_cols = _headsum(dy_all * CS_all * hd["ea_all"], ET) - dtail * tail
        dss = _dot01_left(jnp.ones((8, N), F32), _dot01_left(hd["E"], dS_all * S_all, 2), 2, dims=nt)
        da_last = dss[0:1] * jnp.exp(hd["a_last"]) + jnp.sum(dtail * tail, axis=0, keepdims=True)
        rowi = lax.broadcasted_iota(jnp.int32, (Q, Kh), 0)
        da_cols = da_cols + jnp.where(rowi == Q - 1, da_last, 0.0)
        dstate[...] = hd["eL_b"] * dS_all + dS_y
        sum_mg = jnp.zeros((Q, Q), F32)
        ddt_x = jnp.zeros((Q, Kh), F32)
        da_rows = jnp.zeros((Kh, Q), F32)
        lane_k = lax.broadcasted_iota(jnp.int32, (Q, Kh), 1)
        sub_k = lax.broadcasted_iota(jnp.int32, (Kh, Q), 0)
        for ch in range(nch):
            cs = slice(ch * CW, (ch + 1) * CW)
            dyc = dyb[:, cs]
            xc_b = xdt_all[:, cs].astype(BF16)
            acc = hd["tail_all"][:, cs] * BdS_all[:, cs]
            for kk in range(hpc):
                k = ch * hpc + kk
                a_b = jnp.broadcast_to(acum[:, k:k + 1], (Q, Q))
                a_r = acumT[k:k + 1, :]
                decay = jnp.exp(jnp.where(causal, a_b - a_r, -jnp.inf))
                decayT = jnp.exp(jnp.where(causalT, a_r - a_b, -jnp.inf))
                dyk = dyc if hpc == 1 else jnp.where(_head_mask(Q, CW, kk), dyc, jnp.zeros_like(dyc))
                mg = decay * lax.dot_general(dyk, xc_b, nt, preferred_element_type=F32)
                sum_mg = sum_mg + mg
                w = mg * cb
                da_cols = da_cols + jnp.where(lane_k == k, jnp.sum(w, axis=1, keepdims=True), 0.0)
                da_rows = da_rows + jnp.where(sub_k == k, jnp.sum(w, axis=0, keepdims=True), 0.0)
                acc = acc + jnp.dot((decayT * cbT).astype(BF16), dyk, preferred_element_type=F32)
            dxs_ref[:, cs] = (acc * hd["dt_all"][:, cs] + dy_all[:, cs] * hd["dsk_all"][:, cs]).astype(BF16)
            ddt_x = ddt_x + _headsum(acc * xs[:, cs], ET[cs, :])
        eye_q = (row == col).astype(F32)
        da_cols = da_cols - _dot01_left(eye_q, da_rows, 3, dims=nt)
        dD_row = jnp.sum(_headsum(dy_all * xs, ET), axis=0, keepdims=True)
        sum_mg_b = sum_mg.astype(BF16)
        dB_ref[...] = (dB_acc + lax.dot_general(sum_mg_b, Cm, tn, preferred_element_type=F32)).astype(BF16)
        dC_ref[...] = (dC_acc + jnp.dot(sum_mg_b, Bm, preferred_element_type=F32)).astype(BF16)
        triu = (row <= col).astype(F32)
        ddtA = _dot01_left(triu, da_cols, 3)
        ddt = ddt_x + ddtA * A
        dpre = ddt * _sigmoid(dtp_ref[...] + bias_ref[...])
        ddtp_ref[...] = dpre
        dbias_v = jnp.sum(dpre, axis=0, keepdims=True)
        dalog_v = jnp.sum(ddtA * dt, axis=0, keepdims=True) * A

        @pl.when(ci == 0)
        def _():
            dbias_ref[...] = dbias_v
            dalog_ref[...] = dalog_v
            dD_ref[...] = dD_row
            dng_ref[...] = dng_v

        @pl.when(ci > 0)
        def _():
            dbias_ref[...] += dbias_v
            dalog_ref[...] += dalog_v
            dD_ref[...] += dD_row
            dng_ref[...] += dng_v

    vec_o = pl.BlockSpec((None, 1, Kh), lambda g, c: (g, 0, 0))
    tile = pl.BlockSpec((Q, GP), lambda g, c: (rev(c), g))
    return pl.pallas_call(
        body, grid=(G, nc),
        in_specs=_ssd_in_specs(Q, GP, N, Kh, DI, rev)
        + [pl.BlockSpec((None, None, GP, N), lambda g, c: (rev(c), g, 0, 0)), tile, tile, tile,
           pl.BlockSpec((1, GP), lambda g, c: (0, g))],
        out_specs=[tile, pl.BlockSpec((Q, N), lambda g, c: (rev(c), g)), pl.BlockSpec((Q, N), lambda g, c: (rev(c), g)),
                   pl.BlockSpec((None, Q, Kh), lambda g, c: (g, rev(c), 0)), vec_o, vec_o, vec_o,
                   tile, pl.BlockSpec((1, GP), lambda g, c: (0, g))],
        out_shape=[jax.ShapeDtypeStruct((L, DI), BF16), jax.ShapeDtypeStruct((L, G * N), BF16),
                   jax.ShapeDtypeStruct((L, G * N), BF16), jax.ShapeDtypeStruct((G, L, Kh), F32)]
        + [jax.ShapeDtypeStruct((G, 1, Kh), F32)] * 3
        + [jax.ShapeDtypeStruct(zx.shape, BF16), jax.ShapeDtypeStruct((1, DI), F32)],
        scratch_shapes=[pltpu.VMEM((GP, N), F32)], compiler_params=_cp(("parallel", "arbitrary")),
        name="ssd_bwd")(xbc, xbc, xbc, dtp_g, dtp_gT, bias_g, bias_gT, alog_g, alog_gT, dsk_g, dsk_gT, states,
                        dyn, y, zx, norm_g)


def _alibi_slope(gi, h):
    n = len(DIL_PATTERNS) * DIL_H
    return float(2.0 ** (-8.0 * (gi * DIL_H + h + 1) / n))


def _attn_masks():
    qi = lax.broadcasted_iota(jnp.int32, (DIL_BLK, DIL_BLK), 0)
    kj = lax.broadcasted_iota(jnp.int32, (DIL_BLK, DIL_BLK), 1)
    dcur = (qi - kj).astype(F32)
    return dcur, qi >= kj, dcur + float(DIL_BLK), kj >= qi


def attn_fwd(q3, kv3, gi):
    window, d = DIL_PATTERNS[gi]
    assert window // d == DIL_BLK
    HW = DIL_H * DIL_E
    M = q3.shape[1]
    nb = M // DIL_BLK
    scale = DIL_E ** -0.5
    nt = (((1,), (1,)), ((), ()))

    def body(q_ref, kp_ref, kc_ref, vp_ref, vc_ref, o_ref, lse_ref):
        n = pl.program_id(1)
        dcur, vcur, dprev, vprev0 = _attn_masks()
        dist = jnp.concatenate([dprev, dcur], axis=1)
        valid = jnp.concatenate([jnp.logical_and(vprev0, n > 0), vcur], axis=1)
        lane = lax.broadcasted_iota(jnp.int32, (DIL_BLK, 128), 1)
        lse_acc = jnp.zeros((DIL_BLK, 128), F32)
        for h in range(DIL_H):
            hs = slice(h * DIL_E, (h + 1) * DIL_E)
            sl = _alibi_slope(gi, h) * d
            kcat = jnp.concatenate([kp_ref[:, hs], kc_ref[:, hs]], axis=0)
            vcat = jnp.concatenate([vp_ref[:, hs], vc_ref[:, hs]], axis=0)
            s = lax.dot_general(q_ref[:, hs], kcat, nt, preferred_element_type=F32) * scale - sl * dist
            s = jnp.where(valid, s, -jnp.inf)
            m = jnp.max(s, axis=-1, keepdims=True)
            p = jnp.exp(s - m)
            den = jnp.sum(p, axis=-1, keepdims=True)
            o = jnp.dot(p.astype(BF16), vcat, preferred_element_type=F32) / den
            o_ref[:, hs] = o.astype(BF16)
            lse_acc = jnp.where(lane == h, m + jnp.log(den), lse_acc)
        lse_ref[...] = lse_acc

    blk = (None, DIL_BLK, HW)
    prev = lambda n: jnp.maximum(n - 1, 0)
    return pl.pallas_call(
        body, grid=(d, nb),
        in_specs=[pl.BlockSpec(blk, lambda r, n: (r, n, 0)),
                  pl.BlockSpec(blk, lambda r, n: (r, prev(n), 0)), pl.BlockSpec(blk, lambda r, n: (r, n, 0)),
                  pl.BlockSpec(blk, lambda r, n: (r, prev(n), 1)), pl.BlockSpec(blk, lambda r, n: (r, n, 1))],
        out_specs=[pl.BlockSpec(blk, lambda r, n: (r, n, 0)), pl.BlockSpec((None, DIL_BLK, 128), lambda r, n: (r, n, 0))],
        out_shape=[jax.ShapeDtypeStruct((d, M, HW), BF16), jax.ShapeDtypeStruct((d, M, 128), F32)],
        compiler_params=_cp(("parallel", "parallel")), name=f"attn_fwd_{gi}")(q3, kv3, kv3, kv3, kv3)


def attn_bwd(q3, kv3, do3, lse3, dpr3, gi):
    window, d = DIL_PATTERNS[gi]
    HW = DIL_H * DIL_E
    M = q3.shape[1]
    L = M * d
    nb = M // DIL_BLK
    scale = DIL_E ** -0.5
    nt = (((1,), (1,)), ((), ()))
    tn = (((0,), (0,)), ((), ()))

    def body(q0_ref, q1_ref, k_ref, v_ref, do0_ref, do1_ref, l0_ref, l1_ref, r0_ref, r1_ref,
             dq_ref, dk_ref, dv_ref, carry):
        n = pl.program_id(1)

        @pl.when(n == 0)
        def _():
            carry[...] = jnp.zeros(carry.shape, F32)

        dcur, vcur, dprev, vprev0 = _attn_masks()
        dist = jnp.concatenate([dcur, dprev], axis=0)
        valid = jnp.concatenate([vcur, jnp.logical_and(vprev0, n < nb - 1)], axis=0)
        B = DIL_BLK
        for h in range(DIL_H):
            hs = slice(h * DIL_E, (h + 1) * DIL_E)
            sl = _alibi_slope(gi, h) * d
            kh = k_ref[:, hs]
            vh = v_ref[:, hs]
            qcat = jnp.concatenate([q0_ref[:, hs], q1_ref[:, hs]], axis=0)
            docat = jnp.concatenate([do0_ref[:, hs], do1_ref[:, hs]], axis=0)
            lcat = jnp.concatenate([l0_ref[:, h:h + 1], l1_ref[:, h:h + 1]], axis=0)
            rcat = jnp.concatenate([r0_ref[:, h:h + 1], r1_ref[:, h:h + 1]], axis=0)
            s = lax.dot_general(qcat, kh, nt, preferred_element_type=F32) * scale - sl * dist
            p = jnp.exp(jnp.where(valid, s - lcat, -jnp.inf))
            ds = p * (lax.dot_general(docat, vh, nt, preferred_element_type=F32) - rcat)
            ds_b = (ds * scale).astype(BF16)
            dv_ref[:, hs] = lax.dot_general(p.astype(BF16), docat, tn, preferred_element_type=F32).astype(BF16)
            dk_ref[:, hs] = lax.dot_general(ds_b, qcat, tn, preferred_element_type=F32).astype(BF16)
            dqc = jnp.dot(ds_b, kh, preferred_element_type=F32)
            dq_ref[:, hs] = (carry[:, hs] + dqc[:B]).astype(BF16)
            carry[:, hs] = dqc[B:]

    blk = (None, DIL_BLK, HW)
    sblk = (None, DIL_BLK, 128)
    oblk = (DIL_BLK, HW)
    nxt = lambda n: jnp.minimum(n + 1, nb - 1)
    here = lambda c: (lambda r, n: (r, n, c))
    ahead = lambda c: (lambda r, n: (r, nxt(n), c))
    outs = pl.pallas_call(
        body, grid=(d, nb),
        in_specs=[pl.BlockSpec(blk, here(0)), pl.BlockSpec(blk, ahead(0)),
                  pl.BlockSpec(blk, here(0)), pl.BlockSpec(blk, here(1)),
                  pl.BlockSpec(blk, here(0)), pl.BlockSpec(blk, ahead(0)),
                  pl.BlockSpec(sblk, here(0)), pl.BlockSpec(sblk, ahead(0)),
                  pl.BlockSpec(sblk, here(0)), pl.BlockSpec(sblk, ahead(0))],
        out_specs=[pl.BlockSpec(oblk, lambda r, n: (n, r))] * 3,
        out_shape=[jax.ShapeDtypeStruct((M, d * HW), BF16)] * 3,
        scratch_shapes=[pltpu.VMEM(oblk, F32)], compiler_params=_cp(("parallel", "arbitrary")),
        name=f"attn_bwd_{gi}")(q3, q3, kv3, kv3, do3, do3, lse3, lse3, dpr3, dpr3)
    return [t.reshape(L, HW) for t in outs]


def _merge_weights(l_tiles, h):
    ls = [t[:, h:h + 1] for t in l_tiles]
    mx = functools.reduce(jnp.maximum, ls)
    es = [jnp.exp(l - mx) for l in ls]
    den = functools.reduce(lambda a, b: a + b, es)
    return [e / den for e in es]


def _dil_specs(tr, arrs):
    return [pl.BlockSpec((a.shape[0], tr // a.shape[0], a.shape[2]), lambda i: (0, i, 0)) for a in arrs]


def _dil_scratch(tr, arrs):
    return [pltpu.VMEM((a.shape[2] // 128, tr, 128), F32) for a in arrs if a.shape[0] > 1]


def _undilate(refs3, scrs, tr):
    out, k = [], 0
    for ref in refs3:
        d, _, W = ref.shape
        if d == 1:
            out.append(lambda c, ref=ref: ref[0, :, c * 128:(c + 1) * 128])
            continue
        scr = scrs[k]
        k += 1
        for r in range(d):
            for c in range(W // 128):
                scr.at[c][pl.ds(r, tr // d, stride=d), :] = ref[r, :, c * 128:(c + 1) * 128].astype(F32)
        out.append(lambda c, scr=scr: scr[c])
    return out


def merge_fwd(os3, lses3, z):
    HW = os3[0].shape[2]
    L = os3[0].shape[0] * os3[0].shape[1]
    tr = _tile(L, 256, 16)
    ng = len(os3)
    n_scr = len(_dil_scratch(tr, os3))

    def body(*refs):
        z_ref, out_ref = refs[2 * ng], refs[2 * ng + 1]
        scrs = refs[2 * ng + 2:]
        o_get = _undilate(refs[:ng], scrs[:n_scr], tr)
        l_tiles = [g(0) for g in _undilate(refs[ng:2 * ng], scrs[n_scr:], tr)]
        for h in range(DIL_H):
            hs = slice(h * DIL_E, (h + 1) * DIL_E)
            ws = _merge_weights(l_tiles, h)
            om = functools.reduce(lambda a, b: a + b, [w * o(h).astype(F32) for w, o in zip(ws, o_get)])
            out_ref[:, hs] = (om * _silu(z_ref[:, hs].astype(F32))).astype(BF16)

    return pl.pallas_call(
        body, grid=(L // tr,),
        in_specs=_dil_specs(tr, os3) + _dil_specs(tr, lses3) + _row_specs(tr, [HW]),
        out_specs=_row_specs(tr, [HW])[0], out_shape=jax.ShapeDtypeStruct((L, HW), BF16),
        scratch_shapes=_dil_scratch(tr, os3) + _dil_scratch(tr, lses3),
        compiler_params=_cp(("parallel",)), name="merge_fwd")(*os3, *lses3, z)


def merge_bwd(dgated, os3, lses3, z):
    HW = os3[0].shape[2]
    L = os3[0].shape[0] * os3[0].shape[1]
    tr = _tile(L, 256, 16)
    ng = len(os3)
    n_scr = len(_dil_scratch(tr, os3))

    def body(*refs):
        dg_ref = refs[0]
        z_ref = refs[1 + 2 * ng]
        outs = refs[2 + 2 * ng:2 + 2 * ng + 2 * ng + 1]
        scrs = refs[2 + 2 * ng + 2 * ng + 1:]
        do_out, dpr_out, dz_ref = outs[:ng], outs[ng:2 * ng], outs[2 * ng]
        o_get = _undilate(refs[1:1 + ng], scrs[:n_scr], tr)
        l_tiles = [g(0) for g in _undilate(refs[1 + ng:1 + 2 * ng], scrs[n_scr:2 * n_scr], tr)]
        stage = scrs[2 * n_scr:]
        do_stage, dpr_stage, k = [], [], 0
        for g in range(ng):
            if do_out[g].shape[0] == 1:
                do_stage.append(None)
                dpr_stage.append(None)
            else:
                do_stage.append(stage[2 * k])
                dpr_stage.append(stage[2 * k + 1])
                k += 1
        lane = lax.broadcasted_iota(jnp.int32, (tr, 128), 1)
        accs = [jnp.zeros((tr, 128), F32) for _ in range(ng)]
        for h in range(DIL_H):
            hs = slice(h * DIL_E, (h + 1) * DIL_E)
            ws = _merge_weights(l_tiles, h)
            ov = [o(h).astype(F32) for o in o_get]
            om = functools.reduce(lambda a, b: a + b, [w * o for w, o in zip(ws, ov)])
            zv = z_ref[:, hs].astype(F32)
            dgv = dg_ref[:, hs].astype(F32)
            dom = dgv * _silu(zv)
            dz_ref[:, hs] = (dgv * om * _dsilu(zv)).astype(BF16)
            dws = [jnp.sum(dom * o, axis=-1, keepdims=True) for o in ov]
            dwbar = functools.reduce(lambda a, b: a + b, [w * dw for w, dw in zip(ws, dws)])
            for g in range(ng):
                if do_stage[g] is None:
                    do_out[g][0, :, hs] = (ws[g] * dom).astype(BF16)
                else:
                    do_stage[g][h] = ws[g] * dom
                accs[g] = jnp.where(lane == h, ws[g] * dwbar, accs[g])
        for g in range(ng):
            d = do_out[g].shape[0]
            if d == 1:
                dpr_out[g][0] = accs[g]
                continue
            dpr_stage[g][0] = accs[g]
            for r in range(d):
                dpr_out[g][r] = dpr_stage[g].at[0][pl.ds(r, tr // d, stride=d), :]
                for c in range(HW // 128):
                    do_out[g][r, :, c * 128:(c + 1) * 128] = do_stage[g].at[c][pl.ds(r, tr // d, stride=d), :].astype(BF16)

    stage_shapes = []
    for o3 in os3:
        if o3.shape[0] > 1:
            stage_shapes += [pltpu.VMEM((HW // 128, tr, 128), F32), pltpu.VMEM((1, tr, 128), F32)]
    outs = pl.pallas_call(
        body, grid=(L // tr,),
        in_specs=_row_specs(tr, [HW]) + _dil_specs(tr, os3) + _dil_specs(tr, lses3) + _row_specs(tr, [HW]),
        out_specs=_dil_specs(tr, os3) + _dil_specs(tr, lses3) + _row_specs(tr, [HW]),
        out_shape=[jax.ShapeDtypeStruct(o.shape, BF16) for o in os3] + [jax.ShapeDtypeStruct(l.shape, F32) for l in lses3]
        + [jax.ShapeDtypeStruct((L, HW), BF16)],
        scratch_shapes=_dil_scratch(tr, os3) + _dil_scratch(tr, lses3) + stage_shapes,
        compiler_params=_cp(("parallel",)), name="merge_bwd")(dgated, *os3, *lses3, z)
    return outs[:ng], outs[ng:2 * ng], outs[2 * ng]


def ada_fwd(c8, ada_w):
    nl, D, Ws = ada_w.shape
    tn = _tile(Ws, 512)

    def body(c_ref, w_ref, o_ref):
        o_ref[...] = jnp.dot(_silu(c_ref[...]), w_ref[...], precision=lax.Precision.HIGHEST,
                             preferred_element_type=F32)

    return pl.pallas_call(
        body, grid=(nl, Ws // tn),
        in_specs=[pl.BlockSpec((N_DEV, D), lambda l, j: (0, 0)), pl.BlockSpec((None, D, tn), lambda l, j: (l, 0, j))],
        out_specs=pl.BlockSpec((None, N_DEV, tn), lambda l, j: (l, 0, j)),
        out_shape=jax.ShapeDtypeStruct((nl, N_DEV, Ws), F32), compiler_params=_cp(("parallel", "parallel")),
        name="ada_fwd")(c8, ada_w)


def ada_wgrad(c8t, dmod):
    nl, _, Ws = dmod.shape
    D = c8t.shape[0]
    tm = _tile(D, 512, 8)

    def body(c_ref, d_ref, o_ref):
        sc = _silu(c_ref[...])
        acc = sc[:, 0:1] * d_ref[0:1, :]
        for e in range(1, N_DEV):
            acc = acc + sc[:, e:e + 1] * d_ref[e:e + 1, :]
        o_ref[...] = acc

    return pl.pallas_call(
        body, grid=(nl, D // tm),
        in_specs=[pl.BlockSpec((tm, N_DEV), lambda l, i: (i, 0)), pl.BlockSpec((None, N_DEV, Ws), lambda l, i: (l, 0, 0))],
        out_specs=pl.BlockSpec((None, tm, Ws), lambda l, i: (l, i, 0)),
        out_shape=jax.ShapeDtypeStruct((nl, D, Ws), F32), compiler_params=_cp(("parallel", "parallel")),
        name="ada_wgrad")(c8t, dmod)


def adamw(w, g, m, v, name):
    R, C = w.shape
    tr = _tile(R, 256, 8)
    c1 = 1.0 - ADAM_B1 ** ADAM_STEP
    c2 = 1.0 - ADAM_B2 ** ADAM_STEP

    def body(w_ref, g_ref, m_ref, v_ref, d_ref, nm_ref, nv_ref):
        gv = g_ref[...]
        nm = ADAM_B1 * m_ref[...] + (1.0 - ADAM_B1) * gv
        nv = ADAM_B2 * v_ref[...] + (1.0 - ADAM_B2) * (gv * gv)
        nm_ref[...] = nm
        nv_ref[...] = nv
        d_ref[...] = -ADAM_LR * ((nm / c1) / (jnp.sqrt(nv / c2) + ADAM_EPS) + ADAM_WD * w_ref[...])

    return pl.pallas_call(
        body, grid=(R // tr,), in_specs=_row_specs(tr, [C] * 4), out_specs=_row_specs(tr, [C] * 3),
        out_shape=[jax.ShapeDtypeStruct((R, C), F32)] * 3, compiler_params=_cp(("parallel",)), name=name)(w, g, m, v)


def sum_leading(t, name, out_dtype=F32):
    S, R, C = t.shape
    tr = _tile(R, 256, 16)

    def body(t_ref, o_ref):
        acc = t_ref[0].astype(F32)
        for s in range(1, S):
            acc = acc + t_ref[s].astype(F32)
        o_ref[...] = acc.astype(out_dtype)

    return pl.pallas_call(
        body, grid=(R // tr,), in_specs=[pl.BlockSpec((S, tr, C), lambda i: (0, i, 0))],
        out_specs=pl.BlockSpec((tr, C), lambda i: (i, 0)), out_shape=jax.ShapeDtypeStruct((R, C), out_dtype),
        compiler_params=_cp(("parallel",)), name=name)(t)


def add_half(g, a, core, name, by_cols=False):
    S, R, C = g.shape

    def body(core_ref, g_ref, a_ref, o_ref):
        o_ref[...] = (g_ref[...].astype(F32) + a_ref[...].astype(F32)).astype(BF16)

    if by_cols:
        hc = C // 2
        tr = _tile(R, 512, 16)
        return pl.pallas_call(
            body,
            grid_spec=pltpu.PrefetchScalarGridSpec(
                num_scalar_prefetch=1, grid=(S, R // tr),
                in_specs=[pl.BlockSpec((None, tr, hc), lambda s, i, core_ref: (s, i, core_ref[0])),
                          pl.BlockSpec((None, tr, hc), lambda s, i, core_ref: (s, i, 0))],
                out_specs=pl.BlockSpec((None, tr, hc), lambda s, i, core_ref: (s, i, 0))),
            out_shape=jax.ShapeDtypeStruct((S, R, hc), BF16), compiler_params=_cp(("parallel", "parallel")),
            name=name)(core, g, a)
    h = R // 2
    tr = _tile(h, 256, 16)
    nb = h // tr

    return pl.pallas_call(
        body,
        grid_spec=pltpu.PrefetchScalarGridSpec(
            num_scalar_prefetch=1, grid=(S, nb),
            in_specs=[pl.BlockSpec((None, tr, C), lambda s, i, core_ref: (s, core_ref[0] * nb + i, 0)),
                      pl.BlockSpec((None, tr, C), lambda s, i, core_ref: (s, i, 0))],
            out_specs=pl.BlockSpec((None, tr, C), lambda s, i, core_ref: (s, i, 0))),
        out_shape=jax.ShapeDtypeStruct((S, h, C), BF16), compiler_params=_cp(("parallel", "parallel")),
        name=name)(core, g, a)


def sum_partials(own, landed, chip, name):
    _, h, C = own.shape
    tr = _tile(h, 512, 16)

    def body(chip_ref, own_ref, l_ref, o_ref):
        acc = own_ref[...].astype(F32)
        for j in range(3):
            acc = acc + l_ref[j].astype(F32)
        o_ref[...] = acc

    return pl.pallas_call(
        body,
        grid_spec=pltpu.PrefetchScalarGridSpec(
            num_scalar_prefetch=1, grid=(h // tr,),
            in_specs=[pl.BlockSpec((None, tr, C), lambda i, chip_ref: (chip_ref[0], i, 0)),
                      pl.BlockSpec((3, tr, C), lambda i, chip_ref: (0, i, 0))],
            out_specs=pl.BlockSpec((tr, C), lambda i, chip_ref: (i, 0))),
        out_shape=jax.ShapeDtypeStruct((h, C), F32), compiler_params=_cp(("parallel",)), name=name)(chip, own, landed)


def adamw_halves(w, g_mine, g_theirs, m, v, core, name):
    R, C = w.shape
    h = R // 2
    tr = _tile(h, 256, 8)
    nbh = h // tr
    c1 = 1.0 - ADAM_B1 ** ADAM_STEP
    c2 = 1.0 - ADAM_B2 ** ADAM_STEP

    def body(core_ref, w_ref, gm_ref, gt_ref, m_ref, v_ref, g_ref, d_ref, nm_ref, nv_ref):
        mine = (pl.program_id(0) // nbh) == core_ref[0]
        gv = jnp.where(mine, gm_ref[...], gt_ref[...])
        g_ref[...] = gv
        nm = ADAM_B1 * m_ref[...] + (1.0 - ADAM_B1) * gv
        nv = ADAM_B2 * v_ref[...] + (1.0 - ADAM_B2) * (gv * gv)
        nm_ref[...] = nm
        nv_ref[...] = nv
        d_ref[...] = -ADAM_LR * ((nm / c1) / (jnp.sqrt(nv / c2) + ADAM_EPS) + ADAM_WD * w_ref[...])

    full = pl.BlockSpec((tr, C), lambda i, core_ref: (i, 0))
    halfspec = pl.BlockSpec((tr, C), lambda i, core_ref: (i % nbh, 0))
    return pl.pallas_call(
        body,
        grid_spec=pltpu.PrefetchScalarGridSpec(
            num_scalar_prefetch=1, grid=(2 * nbh,), in_specs=[full, halfspec, halfspec, full, full],
            out_specs=[full] * 4),
        out_shape=[jax.ShapeDtypeStruct((R, C), F32)] * 4, compiler_params=_cp(("parallel",)),
        name=name)(core, w, g_mine, g_theirs, m, v)


_ANY = pl.BlockSpec(memory_space=pl.ANY)


def _place():
    x, y, c = lax.axis_index("x"), lax.axis_index("y"), lax.axis_index("c")
    chips = [(1 - x, y), (x, 1 - y), (1 - x, 1 - y)]
    return x, y, c, chips


def allgather_small(v, name, after=None):
    R, W = v.shape
    extra = [] if after is None else [after]

    def body(x_ref, *rest):
        out_ref, send_sems, recv_sems, local_sem = rest[len(extra):]
        x, y, c, chips = _place()
        me, sibling = (x, y, c), (x, y, 1 - c)

        def rows(px, py, pc):
            return out_ref.at[pl.ds((4 * px + 2 * py + pc) * R, R), :]

        def copy(k, block, to, src=None):
            return pltpu.make_async_remote_copy(
                src_ref=rows(*block) if src is None else src, dst_ref=rows(*block),
                send_sem=send_sems.at[k], recv_sem=recv_sems.at[k], device_id=to, device_id_type=MESH)

        mine = pltpu.make_async_copy(x_ref, rows(*me), local_sem)
        mine.start()
        first = [copy(0, me, sibling, src=x_ref)]
        first += [copy(1 + j, me, (*chip, c), src=x_ref) for j, chip in enumerate(chips)]
        for cp in first:
            cp.start()
        passed = [copy(4 + j, (*chip, c), sibling) for j, chip in enumerate(chips)]
        for j, chip in enumerate(chips):
            copy(1 + j, (*chip, c), me).wait_recv()
            passed[j].start()
        copy(0, sibling, me).wait_recv()
        for j, chip in enumerate(chips):
            copy(4 + j, (*chip, 1 - c), me).wait_recv()
        for cp in first + passed:
            cp.wait_send()
        mine.wait()

    return pl.pallas_call(
        body, out_shape=jax.ShapeDtypeStruct((N_DEV * R, W), v.dtype),
        in_specs=[pl.BlockSpec(memory_space=pltpu.VMEM)] + [_ANY] * len(extra),
        out_specs=pl.BlockSpec(memory_space=pltpu.VMEM),
        scratch_shapes=[pltpu.SemaphoreType.DMA((7,)), pltpu.SemaphoreType.DMA((7,)), pltpu.SemaphoreType.DMA],
        name=name)(v, *extra)


def allgather_routed(shard, name):
    R, C = shard.shape
    hc = C // 2
    ra = (R // 2) // 16 * 16

    def body(in_ref, out_ref, send_sems, recv_sems):
        x, y, c, _ = _place()
        xn, yn = (1 - x, y, c), (x, 1 - y, c)
        sibling = (x, y, 1 - c)
        p, pxn, pyn, pdg = 2 * x + y, 2 * (1 - x) + y, 2 * x + (1 - y), 2 * (1 - x) + (1 - y)
        rows_a, rows_b, rows_all = pl.ds(0, ra), pl.ds(ra, R - ra), pl.ds(0, R)

        def win(ref, rows, core):
            return ref.at[rows, pl.ds(pl.multiple_of(core * hc, 128), hc)]

        def copy(k, chip_id, rows, core, to, src=None):
            blk = win(out_ref.at[chip_id], rows, core)
            return pltpu.make_async_remote_copy(
                src_ref=blk if src is None else src, dst_ref=blk, send_sem=send_sems.at[k], recv_sem=recv_sems.at[k],
                device_id=to, device_id_type=MESH)

        own = [copy(0, p, rows_a, c, xn, src=win(in_ref, rows_a, c)), copy(1, p, rows_b, c, xn, src=win(in_ref, rows_b, c)),
               copy(2, p, rows_b, c, yn, src=win(in_ref, rows_b, c)), copy(3, p, rows_a, c, yn, src=win(in_ref, rows_a, c))]
        for cp in own:
            cp.start()
        copy(0, pxn, rows_a, c, xn).wait_recv()
        fwd_a = copy(4, pxn, rows_a, c, yn)
        fwd_a.start()
        copy(2, pyn, rows_b, c, yn).wait_recv()
        fwd_b = copy(5, pyn, rows_b, c, xn)
        fwd_b.start()
        copy(1, pxn, rows_b, c, xn).wait_recv()
        copy(3, pyn, rows_a, c, yn).wait_recv()
        passed = [copy(6, pxn, rows_all, c, sibling), copy(7, pyn, rows_all, c, sibling)]
        for cp in passed:
            cp.start()
        copy(4, pdg, rows_a, c, yn).wait_recv()
        passed.append(copy(8, pdg, rows_a, c, sibling))
        passed[-1].start()
        copy(5, pdg, rows_b, c, xn).wait_recv()
        passed.append(copy(9, pdg, rows_b, c, sibling))
        passed[-1].start()
        for k, (chip_id, rows) in enumerate([(pxn, rows_all), (pyn, rows_all), (pdg, rows_a), (pdg, rows_b)]):
            copy(6 + k, chip_id, rows, 1 - c, sibling).wait_recv()
        for cp in own + [fwd_a, fwd_b] + passed:
            cp.wait_send()

    out = pl.pallas_call(
        body, out_shape=jax.ShapeDtypeStruct((N_CHIPS, R, C), shard.dtype), in_specs=[_ANY], out_specs=_ANY,
        scratch_shapes=[pltpu.SemaphoreType.DMA((10,)), pltpu.SemaphoreType.DMA((10,))], name=name)(shard)
    chip = 2 * lax.axis_index("x") + lax.axis_index("y")
    return lax.dynamic_update_index_in_dim(out, shard, chip, 0)


_HBM = pl.BlockSpec(memory_space=pltpu.HBM)
_SEM = pl.BlockSpec(memory_space=pltpu.SEMAPHORE)
_EFFECT = pltpu.SideEffectType.DATAFLOW_SIDE_EFFECTING


def _chip_copies(kind, srcs, lands, send_sems, recv_sems):
    x, y, c, chips = _place()
    p = 2 * x + y
    cps = []
    if kind == "sibling":
        for i in range(len(srcs)):
            h = srcs[i].shape[1] // 2
            cps.append(pltpu.make_async_remote_copy(
                src_ref=srcs[i].at[:, pl.ds((1 - c) * h, h), :], dst_ref=lands[i], send_sem=send_sems.at[3 * i],
                recv_sem=recv_sems.at[3 * i], device_id=(x, y, 1 - c), device_id_type=MESH))
        return cps
    for i in range(len(srcs)):
        for j, (cx, cy) in enumerate(chips):
            if kind == "gather":
                src, dst = srcs[i].at[c], lands[i].at[p, c]
            else:
                src, dst = srcs[i].at[2 * cx + cy], lands[i].at[j]
            cps.append(pltpu.make_async_remote_copy(
                src_ref=src, dst_ref=dst, send_sem=send_sems.at[3 * i + j], recv_sem=recv_sems.at[3 * i + j],
                device_id=(cx, cy, c), device_id_type=MESH))
    return cps


def split_start(kind, srcs, land_shapes, after, name):
    n = len(srcs)

    def body(*refs):
        src_refs, land_refs = refs[:n], refs[n:2 * n]
        send_sems, recv_sems = refs[2 * n + 1], refs[2 * n + 2]
        token = refs[-1]
        for cp in _chip_copies(kind, src_refs, land_refs, send_sems, recv_sems):
            cp.start()
        token[...] = jnp.zeros_like(token)

    lands = [pltpu.with_memory_space_constraint(lax.empty(s, BF16), pltpu.HBM) for s in land_shapes]
    outs = pl.pallas_call(
        body, name=name,
        out_shape=(pltpu.SemaphoreType.DMA((3 * n,)), pltpu.SemaphoreType.DMA((3 * n,)),
                   *[pltpu.HBM(s.shape, s.dtype) for s in srcs], *[pltpu.HBM(s, BF16) for s in land_shapes],
                   jax.ShapeDtypeStruct((8, 128), F32)),
        in_specs=[_HBM] * (2 * n) + [_ANY],
        out_specs=(_SEM, _SEM, *([_HBM] * (2 * n)), pl.BlockSpec(memory_space=pltpu.VMEM)),
        input_output_aliases={i: 2 + i for i in range(2 * n)},
        compiler_params=pltpu.CompilerParams(has_side_effects=_EFFECT),
    )(*[pltpu.with_memory_space_constraint(s, pltpu.HBM) for s in srcs], *lands, after)
    return outs[0], outs[1], outs[2:2 + n], outs[2 + n:2 + 2 * n], outs[-1]


def split_wait(kind, send_sems, recv_sems, srcs, lands, after, name):
    n = len(srcs)

    def body(*refs):
        src_refs, land_refs = refs[:n], refs[n:2 * n]
        ssem, rsem = refs[2 * n], refs[2 * n + 1]
        for cp in _chip_copies(kind, src_refs, land_refs, ssem, rsem):
            cp.wait_send()
            cp.wait_recv()

    outs = pl.pallas_call(
        body, name=name,
        out_shape=[pltpu.HBM(s.shape, s.dtype) for s in srcs] + [pltpu.HBM(s.shape, s.dtype) for s in lands],
        in_specs=[_HBM] * (2 * n) + [_SEM, _SEM, _ANY], out_specs=[_HBM] * (2 * n),
        input_output_aliases={i: i for i in range(2 * n)},
        compiler_params=pltpu.CompilerParams(has_side_effects=_EFFECT),
    )(*srcs, *lands, send_sems, recv_sems, after)
    return outs[:n], outs[n:]


def pass_to_sibling(lands):
    n = len(lands)

    def body(*refs):
        ins, outs = refs[:n], refs[n:2 * n]
        send_sems, recv_sems = refs[2 * n:]
        x, y, c, chips = _place()
        cps = []
        for i in range(n):
            for j, (cx, cy) in enumerate(chips):
                blk = outs[i].at[2 * cx + cy, c]
                cps.append(pltpu.make_async_remote_copy(
                    src_ref=ins[i].at[2 * cx + cy, c], dst_ref=blk, send_sem=send_sems.at[3 * i + j],
                    recv_sem=recv_sems.at[3 * i + j], device_id=(x, y, 1 - c), device_id_type=MESH))
        for cp in cps:
            cp.start()
        for cp in cps:
            cp.wait()

    return pl.pallas_call(
        body, out_shape=[jax.ShapeDtypeStruct(t.shape, t.dtype) for t in lands], in_specs=[_ANY] * n,
        out_specs=[_ANY] * n, input_output_aliases={i: i for i in range(n)},
        scratch_shapes=[pltpu.SemaphoreType.DMA((3 * n,)), pltpu.SemaphoreType.DMA((3 * n,))],
        name="ag_pass_to_sibling")(*lands)


def _pass_copies(bufs, send_sems, recv_sems):
    x, y, c, chips = _place()
    cps = []
    for i in range(len(bufs)):
        for j, (cx, cy) in enumerate(chips):
            blk = bufs[i].at[2 * cx + cy, c]
            cps.append(pltpu.make_async_remote_copy(
                src_ref=blk, dst_ref=blk, send_sem=send_sems.at[3 * i + j], recv_sem=recv_sems.at[3 * i + j],
                device_id=(x, y, 1 - c), device_id_type=MESH))
    return cps


def pass_start(bufs, after, name):
    n = len(bufs)

    def body(*refs):
        send_sems, recv_sems = refs[n + 1], refs[n + 2]
        for cp in _pass_copies(refs[:n], send_sems, recv_sems):
            cp.start()
        refs[-1][...] = jnp.zeros_like(refs[-1])

    outs = pl.pallas_call(
        body, name=name,
        out_shape=(pltpu.SemaphoreType.DMA((3 * n,)), pltpu.SemaphoreType.DMA((3 * n,)),
                   *[pltpu.HBM(b.shape, b.dtype) for b in bufs], jax.ShapeDtypeStruct((8, 128), F32)),
        in_specs=[_HBM] * n + [_ANY],
        out_specs=(_SEM, _SEM, *([_HBM] * n), pl.BlockSpec(memory_space=pltpu.VMEM)),
        input_output_aliases={i: 2 + i for i in range(n)},
        compiler_params=pltpu.CompilerParams(has_side_effects=_EFFECT),
    )(*[pltpu.with_memory_space_constraint(b, pltpu.HBM) for b in bufs], after)
    return outs[0], outs[1], outs[2:2 + n], outs[-1]


def pass_wait(send_sems, recv_sems, bufs, after, name):
    n = len(bufs)

    def body(*refs):
        for cp in _pass_copies(refs[:n], refs[n], refs[n + 1]):
            cp.wait_send()
            cp.wait_recv()

    return pl.pallas_call(
        body, name=name, out_shape=[pltpu.HBM(b.shape, b.dtype) for b in bufs],
        in_specs=[_HBM] * n + [_SEM, _SEM, _ANY], out_specs=[_HBM] * n,
        input_output_aliases={i: i for i in range(n)},
        compiler_params=pltpu.CompilerParams(has_side_effects=_EFFECT),
    )(*bufs, send_sems, recv_sems, after)


def exchange_halves_to_sibling(gs, name, by_cols=False):
    n = len(gs)

    def body(*refs):
        ins, outs = refs[:n], refs[n:2 * n]
        send_sems, recv_sems = refs[2 * n:]
        x, y, c, _ = _place()
        cps = []
        for i in range(n):
            if by_cols:
                hc = ins[i].shape[2] // 2
                src = ins[i].at[:, :, pl.ds(pl.multiple_of((1 - c) * hc, 128), hc)]
            else:
                h = ins[i].shape[1] // 2
                src = ins[i].at[:, pl.ds((1 - c) * h, h), :]
            cps.append(pltpu.make_async_remote_copy(
                src_ref=src, dst_ref=outs[i],
                send_sem=send_sems.at[i], recv_sem=recv_sems.at[i], device_id=(x, y, 1 - c), device_id_type=MESH))
        for cp in cps:
            cp.start()
        for cp in cps:
            cp.wait()

    halve = (lambda s: (s[0], s[1], s[2] // 2)) if by_cols else (lambda s: (s[0], s[1] // 2, s[2]))
    return pl.pallas_call(
        body, out_shape=[jax.ShapeDtypeStruct(halve(g.shape), g.dtype) for g in gs],
        in_specs=[_ANY] * n, out_specs=[_ANY] * n,
        scratch_shapes=[pltpu.SemaphoreType.DMA((n,)), pltpu.SemaphoreType.DMA((n,))],
        name=name)(*gs)


def join_halves(rs, name):
    n = len(rs)

    def body(*refs):
        ins, outs = refs[:n], refs[n:2 * n]
        send_sems, recv_sems = refs[2 * n:]
        x, y, c, _ = _place()
        cps = [pltpu.make_async_remote_copy(
            src_ref=ins[i], dst_ref=outs[i], send_sem=send_sems.at[i], recv_sem=recv_sems.at[i],
            device_id=(x, y, 1 - c), device_id_type=MESH) for i in range(n)]
        for cp in cps:
            cp.start()
        for cp in cps:
            cp.wait()

    return pl.pallas_call(
        body, out_shape=[jax.ShapeDtypeStruct(r.shape, r.dtype) for r in rs],
        in_specs=[_ANY] * n, out_specs=[_ANY] * n,
        scratch_shapes=[pltpu.SemaphoreType.DMA((n,)), pltpu.SemaphoreType.DMA((n,))],
        name=name)(*rs)


def _pack(parts, row_mult=8):
    flat = jnp.concatenate([p.reshape(-1).astype(F32) for p in parts])
    unit = row_mult * 128
    n = -(-flat.shape[0] // unit) * unit
    return jnp.pad(flat, (0, n - flat.shape[0])).reshape(n // 128, 128)


def _unpack(flat, shapes):
    out, off = [], 0
    for s in shapes:
        n = int(np.prod(s))
        out.append(flat[off:off + n].reshape(s))
        off += n
    return out


def _gather_packed(parts, name):
    packed = _pack(parts)
    g = allgather_small(packed, name).reshape(N_DEV, -1)
    return _unpack_rows(g, [p.shape for p in parts])


def _unpack_rows(g, shapes):
    out, off = [], 0
    for s in shapes:
        n = int(np.prod(s))
        out.append(g[:, off:off + n].reshape((g.shape[0],) + tuple(s)))
        off += n
    return out


def _by_chip(t, axis):
    return jnp.concatenate([t[2 * p] for p in range(N_CHIPS)], axis=axis)


def kernel(x, c, ada_w, ada_b, ln_g, ln_b, a_in_w, a_conv_w, a_conv_b, a_dt_bias, a_A_log, a_D, a_norm_g, a_out_w, kv_w, b_in_w, b_out_w, loss_target, m_ada_w, m_ada_b, m_ln_g, m_ln_b, m_a_in_w, m_a_conv_w, m_a_conv_b, m_a_dt_bias, m_a_A_log, m_a_D, m_a_norm_g, m_a_out_w, m_kv_w, m_b_in_w, m_b_out_w, v_ada_w, v_ada_b, v_ln_g, v_ln_b, v_a_in_w, v_a_conv_w, v_a_conv_b, v_a_dt_bias, v_a_A_log, v_a_D, v_a_norm_g, v_a_out_w, v_kv_w, v_b_in_w, v_b_out_w):
    ax, ay, ac = lax.axis_index("x"), lax.axis_index("y"), lax.axis_index("c")
    chip = 2 * ax + ay
    dev = 4 * ax + 2 * ay + ac
    xin = x[0]
    tgt = loss_target[0]
    L, D = xin.shape
    G, P = SSD_G, SSD_P
    H = a_dt_bias.shape[1]
    Kh = H // G
    DI = H * P
    CONVD = a_conv_b.shape[1] * N_CHIPS
    HW = DIL_H * DIL_E
    Ws = ada_w.shape[2]

    w_in_g = allgather_routed(jnp.transpose(a_in_w[0]).astype(BF16), "allgather_w_in")
    later = [a_out_w[0].astype(BF16), kv_w.astype(BF16), b_in_w[0].astype(BF16), b_out_w[0].astype(BF16)]
    later_split = [s.reshape(2, s.shape[0] // 2, s.shape[1]) for s in later]
    ag_ssem, ag_rsem, ag_srcs, ag_lands, ag_token = split_start(
        "gather", later_split, [(N_CHIPS,) + s.shape for s in later_split], w_in_g, "ag_later_start")
    w_in_t = w_in_g.reshape(-1, D)
    w_dt_t = jnp.pad(w_in_t[DI + CONVD:], ((0, 128 - H), (0, 0)))

    c8, cw8, cb8, ng8 = _gather_packed([c[0], a_conv_w[0], a_conv_b[0], a_norm_g[0]], "allgather_small_params")
    conv_w = _by_chip(cw8, 1)
    conv_b = _by_chip(cb8, 0).reshape(1, CONVD)
    norm_g = _by_chip(ng8, 0).reshape(1, DI)

    mod_s = ada_fwd(c8, ada_w)
    (mod8,) = _gather_packed([mod_s], "allgather_small_mod")
    mods = _by_chip(mod8, 2)
    mod = lax.dynamic_index_in_dim(mods, dev, axis=1, keepdims=False) + ada_b
    shift = [mod[l:l + 1, :D] for l in range(DEPTH)]
    scale = [mod[l:l + 1, D:2 * D] for l in range(DEPTH)]
    gate = [mod[l:l + 1, 2 * D:] for l in range(DEPTH)]
    lg = [ln_g[l:l + 1] for l in range(DEPTH)]
    lb = [ln_b[l:l + 1] for l in range(DEPTH)]

    h0 = modulate(xin, scale[0] + ag_token[0:1, 0:1], shift[0], "modulate0")
    zx = mm_nt(h0, w_in_t, BF16, "mm_in_zx", kw_rows=DI + CONVD)
    dtp = mm_nt(h0, w_dt_t, F32, "mm_in_dt")
    xbc = conv_fwd(zx, DI, conv_w, conv_b)
    dtp_g = jnp.transpose(dtp[:, :H].reshape(L, G, Kh), (1, 0, 2))
    dtp_gT = jnp.transpose(dtp_g, (0, 2, 1))
    vecs = [a_dt_bias.reshape(G, 1, Kh), a_dt_bias.reshape(G, Kh, 1), a_A_log.reshape(G, 1, Kh),
            a_A_log.reshape(G, Kh, 1), a_D.reshape(G, 1, Kh), a_D.reshape(G, Kh, 1)]
    y_ssd, states, yn = ssd_fwd(xbc, dtp_g, dtp_gT, *vecs, zx, norm_g, DI)
    later_split, ag_lands = split_wait("gather", ag_ssem, ag_rsem, ag_srcs, ag_lands, yn, "ag_later_wait")
    (land_out,) = pass_to_sibling(ag_lands[:1])
    ps_ssem, ps_rsem, lands_b, ps_token = pass_start(ag_lands[1:], land_out, "ag_pass_start")

    def place_own(o, s, full):
        return lax.dynamic_update_index_in_dim(o, s, chip, 0).reshape((N_CHIPS,) + full.shape)

    w_out_g = place_own(land_out, later_split[0], later[0])
    ymix0 = mm_nn(yn, w_out_g.reshape(-1, D), F32, "mm_out_a", after=ps_token)
    x1, x1b, h1 = ln_mid(xin, ymix0, gate[0], lg[0], lb[0], scale[1], shift[1])
    lands_b = pass_wait(ps_ssem, ps_rsem, lands_b, x1b, "ag_pass_wait")
    w_kv_g, w_bin_g, w_bout_g = [place_own(o, s, full) for o, s, full in zip(lands_b, later_split[1:], later[1:])]

    n_grp = len(DIL_PATTERNS)
    cb = HW // 512
    assert w_bin_g.shape[2] == HW
    kv3 = [mm_cols_dilated(x1b, w_kv_g, [g * cb + t for t in range(cb)] + [(n_grp + g) * cb + t for t in range(cb)],
                           DIL_PATTERNS[g][1], f"mm_kv_{g}") for g in range(n_grp)]
    q3 = [mm_cols_dilated(h1, w_bin_g, [g], DIL_PATTERNS[g][1], f"mm_q_{g}", tn=HW) for g in range(n_grp)]
    z_b = mm_nn(h1, w_bin_g[n_grp], BF16, "mm_z_b")
    os_, lses = [], []
    for gi in range(len(DIL_PATTERNS)):
        o, lse = attn_fwd(q3[gi], kv3[gi], gi)
        os_.append(o)
        lses.append(lse)
    om = merge_fwd(os_, lses, z_b)
    ymix1 = mm_nn(om, w_bout_g, F32, "mm_out_b", stack="col")
    dres2, dy2, dg1, db1, dgate1, sq = ln_final_fwd_bwd(x1, ymix1, gate[1], lg[1], lb[1], tgt)
    loss_part = 0.5 * jnp.sum(sq) / D

    g_bout = mm_tn(om, dy2, BF16, "mm_gw_out_b", stack="col")
    dgated = mm_nt(dy2, w_bout_g, BF16, "mm_gx_out_b", stack="col")
    dos, dprs, dz_b = merge_bwd(dgated, os_, lses, z_b)
    dqs, dks, dvs = [], [], []
    for gi in range(len(DIL_PATTERNS)):
        dq, dk, dv = attn_bwd(q3[gi], kv3[gi], dos[gi], lses[gi], dprs[gi], gi)
        dqs.append(dq)
        dks.append(dk)
        dvs.append(dv)
    dqz = jnp.concatenate(dqs + [dz_b], axis=1)
    dkv = jnp.concatenate(dks + dvs, axis=1)
    g_bin = mm_tn(h1, dqz, BF16, "mm_gw_in_b", stack="col")
    dh1 = mm_nt(dqz, w_bin_g, BF16, "mm_gx_in_b", stack="col")
    g_kv = mm_tn(x1b, dkv, BF16, "mm_gw_kv", stack="col")

    core = ac.astype(jnp.int32).reshape(1)
    chip_i = chip.astype(jnp.int32).reshape(1)

    def begin_exchange(gs, tag):
        shapes = [(g.shape[0], g.shape[1] // 2, g.shape[2]) for g in gs]
        return split_start("sibling", gs, shapes, gs[0], "rs_x%s_start" % tag)

    def begin_scatter(gs, nms, tag, exchange=None, after=None, by_cols=False):
        if exchange is None:
            sib = exchange_halves_to_sibling(gs, "rs_sibling_exchange_" + tag, by_cols=by_cols)
        else:
            gs, sib = split_wait("sibling", exchange[0], exchange[1], exchange[2], exchange[3], after,
                                 "rs_x%s_wait" % tag)
        parts = [add_half(g, a, core, "rs_add_" + nm, by_cols=by_cols) for g, a, nm in zip(gs, sib, nms)]
        return split_start("scatter", parts, [(3,) + t.shape[1:] for t in parts], parts[0], "rs_%s_start" % tag)

    def finish_scatter(handles, after, tag):
        nms, owns, landed = [], [], []
        for k, (handle, hn) in enumerate(handles):
            parts, lands = split_wait("scatter", handle[0], handle[1], handle[2], handle[3], after,
                                      "rs_%s%d_wait" % (tag, k))
            nms += hn
            owns += list(parts)
            landed += list(lands)
        halves = [sum_partials(own, t, chip_i, "rs_sum_" + nm) for own, t, nm in zip(owns, landed, nms)]
        theirs = join_halves(halves, "rs_join_halves_" + tag)
        return dict(zip(nms, zip(halves, theirs)))

    names_b = ["kv", "in_b", "out_b"]
    ex_b = begin_exchange([g_kv, g_bin, g_bout], "b")
    dx1_kv = mm_nt(dkv, w_kv_g, BF16, "mm_gx_kv", stack="col", after=ex_b[4])
    rs_b = begin_scatter(None, names_b, "b", exchange=ex_b, after=dx1_kv)

    dres1, dy1, dg0, db0, dgate0, dscale1, dshift1 = mod_ln_bwd(
        dres2, dh1, dx1_kv, x1, scale[1], xin, ymix0, gate[0] + rs_b[4][0:1, 0:1], lg[0])
    g_out = mm_tn(yn, dy1, BF16, "mm_gw_out_a", stack="row")
    ex_a1 = begin_exchange([g_out], "a1")
    dyn = mm_nt(dy1, w_out_g, BF16, "mm_gx_out_a", stack="row", after=ex_a1[4])
    rs_a1 = begin_scatter(None, ["out_a"], "a1", exchange=ex_a1, after=dyn)
    dxs, dB, dC, ddtp_g, dbias_g, dalog_g, dD_g, dz_a, dnorm_g = ssd_bwd(
        xbc, dtp_g, dtp_gT, *vecs, states, dyn, y_ssd, zx, norm_g + rs_a1[4][0:1, 0:1], DI)
    dzx, dws, dbs, lo = dz_a, [], [], 0
    for tag, gpart in (("xs", dxs), ("b", dB), ("c", dC)):
        hi = lo + gpart.shape[1]
        dzx, dw_p, db_p = conv_bwd(zx, DI + lo, conv_w[:, lo:hi], conv_b[:, lo:hi], gpart, dzx, "conv_bwd_" + tag)
        dws.append(dw_p)
        dbs.append(db_p)
        lo = hi
    dconv_w = jnp.concatenate(dws, axis=1)
    dconv_b = jnp.concatenate(dbs, axis=1)
    ddtp = jnp.pad(jnp.transpose(ddtp_g, (1, 0, 2)).reshape(L, H), ((0, 0), (0, 128 - H)))
    g_inT = mm_tn(dzx, h0, BF16, "mm_gw_in_zx", m_rows=DI + CONVD + H)
    g_dtT = mm_tn(ddtp, h0, BF16, "mm_gw_in_dt")
    g_inT = lax.dynamic_update_slice(g_inT, g_dtT[:H], (DI + CONVD, 0))
    rs_a2 = begin_scatter([g_inT.reshape(N_CHIPS, -1, D)], ["in_a"], "a2", by_cols=True)
    dh0_dt = mm_nn(ddtp, w_dt_t, F32, "mm_gx_in_dt")
    grad_x, dsc_parts, dsh_parts = mm_nn_mod_bwd(dzx, w_in_t, dh0_dt, dres1, xin, scale[0], "mm_gx_in_zx",
                                                 after=rs_a2[4])
    dscale0, dshift0 = jnp.sum(dsc_parts, axis=0), jnp.sum(dsh_parts, axis=0)
    g_halves = finish_scatter([(rs_b, names_b)], grad_x, "b")

    def step_halves(w, m, v, nm):
        shp = w.shape
        mine, theirs_ = g_halves[nm]
        outs4 = adamw_halves(w.reshape(-1, shp[-1]), mine, theirs_, m.reshape(-1, shp[-1]), v.reshape(-1, shp[-1]),
                             core, "adamw_" + nm)
        return tuple(t.reshape(shp) for t in outs4)

    big = {
        "kv_w": step_halves(kv_w, m_kv_w, v_kv_w, "kv"),
        "b_in_w": step_halves(b_in_w, m_b_in_w, v_b_in_w, "in_b"),
        "b_out_w": step_halves(b_out_w, m_b_out_w, v_b_out_w, "out_b"),
    }
    g_halves.update(finish_scatter([(rs_a1, ["out_a"]), (rs_a2, ["in_a"])], big["kv_w"][1], "a"))
    g_halves["in_a"] = tuple(jnp.transpose(t) for t in g_halves["in_a"])
    big["a_in_w"] = step_halves(a_in_w, m_a_in_w, v_a_in_w, "in_a")
    big["a_out_w"] = step_halves(a_out_w, m_a_out_w, v_a_out_w, "out_a")

    dmod = jnp.concatenate([jnp.concatenate([dshift0, dscale0, dgate0], axis=1),
                            jnp.concatenate([dshift1, dscale1, dgate1], axis=1)], axis=0)
    small_parts = [jnp.concatenate([dg0, dg1], axis=0), jnp.concatenate([db0, db1], axis=0),
                   dbias_g.reshape(1, H), dalog_g.reshape(1, H), dD_g.reshape(1, H),
                   dconv_w, dconv_b, dnorm_g, loss_part.reshape(1, 1)]
    small_shapes = [p.shape for p in small_parts]
    packed = jnp.concatenate([_pack([dmod]), _pack(small_parts)], axis=0)
    n_mod_rows = _pack([dmod]).shape[0]
    gathered = allgather_small(packed, "allgather_small_grads", after=g_halves["in_a"][1]).reshape(N_DEV, -1, 128)
    dmod8 = gathered[:, :n_mod_rows].reshape(N_DEV, -1)[:, :2 * 3 * D].reshape(N_DEV, DEPTH, 3 * D)
    summed = sum_leading(gathered, "sum_small")
    g_ada_b = summed[:n_mod_rows].reshape(-1)[:2 * 3 * D].reshape(DEPTH, 3 * D)
    (g_ln_g, g_ln_b, g_dt_bias, g_a_log, g_dsk, g_conv_w, g_conv_b, g_norm_g, loss_all) = _unpack(
        summed[n_mod_rows:].reshape(-1), small_shapes)
    loss = loss_all.reshape(())
    Cs = CONVD // N_CHIPS
    g_conv_w_s = lax.dynamic_slice_in_dim(g_conv_w, chip * Cs, Cs, axis=1)
    g_conv_b_s = lax.dynamic_slice_in_dim(g_conv_b, chip * Cs, Cs, axis=1)
    g_norm_g_s = lax.dynamic_slice_in_dim(g_norm_g, chip * (DI // N_CHIPS), DI // N_CHIPS, axis=1)
    dmod_s = jnp.transpose(lax.dynamic_slice_in_dim(dmod8, chip * Ws, Ws, axis=2), (1, 0, 2))

    def step2d(w, g, m, v, nm):
        shp = w.shape
        d_, m_, v_ = adamw(w.reshape(-1, shp[-1]), g.reshape(-1, shp[-1]), m.reshape(-1, shp[-1]),
                           v.reshape(-1, shp[-1]), "adamw_" + nm)
        return g.reshape(shp), d_.reshape(shp), m_.reshape(shp), v_.reshape(shp)

    big["ada_w"] = step2d(ada_w, ada_wgrad(jnp.transpose(c8), dmod_s), m_ada_w, v_ada_w, "ada_w")
    small_names = ["ada_b", "ln_g", "ln_b", "a_conv_w", "a_conv_b", "a_dt_bias", "a_A_log", "a_D", "a_norm_g"]
    small_w = [ada_b, ln_g, ln_b, a_conv_w, a_conv_b, a_dt_bias, a_A_log, a_D, a_norm_g]
    small_m = [m_ada_b, m_ln_g, m_ln_b, m_a_conv_w, m_a_conv_b, m_a_dt_bias, m_a_A_log, m_a_D, m_a_norm_g]
    small_v = [v_ada_b, v_ln_g, v_ln_b, v_a_conv_w, v_a_conv_b, v_a_dt_bias, v_a_A_log, v_a_D, v_a_norm_g]
    small_g = [g_ada_b, g_ln_g, g_ln_b, g_conv_w_s, g_conv_b_s, g_dt_bias, g_a_log, g_dsk, g_norm_g_s]
    shapes = [w.shape for w in small_w]
    small_g = [g.reshape(s) for g, s in zip(small_g, shapes)]
    d_p, m_p, v_p = adamw(_pack(small_w), _pack(small_g), _pack(small_m), _pack(small_v), "adamw_small")
    small = {}
    for nm, g, d_, m_, v_ in zip(small_names, small_g, _unpack(d_p.reshape(-1), shapes), _unpack(m_p.reshape(-1), shapes),
                                 _unpack(v_p.reshape(-1), shapes)):
        small[nm] = (g, d_, m_, v_)
    allw = {**big, **small}
    order = ["ada_w", "ada_b", "ln_g", "ln_b", "a_in_w", "a_conv_w", "a_conv_b", "a_dt_bias", "a_A_log", "a_D",
             "a_norm_g", "a_out_w", "kv_w", "b_in_w", "b_out_w"]
    outs = [loss, grad_x.reshape(x.shape)]
    for k in range(4):
        outs += [allw[n][k] for n in order]
    return tuple(outs)
```

```python
import functools

import jax
import jax.numpy as jnp
import numpy as np
from jax import lax
from jax.experimental import pallas as pl
from jax.experimental.pallas import tpu as pltpu

F32 = jnp.float32
BF16 = jnp.bfloat16
MESH = pl.DeviceIdType.MESH

DEPTH = 2
ALPHA = (2 * DEPTH) ** 0.25
LN_EPS = 1e-5
RMS_EPS = 1e-5
SSD_P = 64
SSD_N = 128
SSD_Q = 256
SSD_G = 8
CONV_W = 4
DIL_PATTERNS = ((128, 1), (512, 4), (2048, 16))
DIL_H = 8
DIL_E = 128
DIL_BLK = 128
ADAM_LR, ADAM_B1, ADAM_B2, ADAM_EPS, ADAM_WD, ADAM_STEP = 0.001, 0.9, 0.999, 1e-08, 0.01, 10

VMEM_LIMIT = 56 * 1024 * 1024
N_CHIPS = 4
N_DEV = 8


def _tile(dim, target, mult=128):
    if dim <= target:
        return dim
    t = (target // mult) * mult
    while t >= mult:
        if dim % t == 0:
            return t
        t -= mult
    return dim


def _cp(sem):
    return pltpu.CompilerParams(dimension_semantics=sem, vmem_limit_bytes=VMEM_LIMIT)


def _sigmoid(x):
    return 1.0 / (1.0 + jnp.exp(-x))


def _silu(x):
    return x * _sigmoid(x)


def _dsilu(x):
    s = _sigmoid(x)
    return s * (1.0 + x * (1.0 - s))


def _softplus(x):
    return jnp.maximum(x, 0.0) + jnp.log(1.0 + jnp.exp(-jnp.abs(x)))


def _mm_call(a, b, out_shape, grid, a_spec, b_spec, o_spec, acc_shape, dims, name, after=None):
    nk = grid[2]
    extra = [] if after is None else [after]

    def prod(a_ref, b_ref):
        return lax.dot_general(a_ref[...].astype(BF16), b_ref[...].astype(BF16), (dims, ((), ())),
                               preferred_element_type=F32)

    def body_single(a_ref, b_ref, *rest):
        o_ref = rest[len(extra)]
        o_ref[...] = prod(a_ref, b_ref).astype(o_ref.dtype)

    def body_multi(a_ref, b_ref, *rest):
        o_ref, acc_ref = rest[len(extra):]
        k = pl.program_id(2)

        @pl.when(k == 0)
        def _():
            acc_ref[...] = prod(a_ref, b_ref)

        @pl.when(jnp.logical_and(k > 0, k < nk - 1))
        def _():
            acc_ref[...] += prod(a_ref, b_ref)

        @pl.when(k == nk - 1)
        def _():
            o_ref[...] = (acc_ref[...] + prod(a_ref, b_ref)).astype(o_ref.dtype)

    return pl.pallas_call(
        body_single if nk == 1 else body_multi, grid=grid, in_specs=[a_spec, b_spec] + [_ANY] * len(extra),
        out_specs=o_spec, out_shape=out_shape, scratch_shapes=[] if nk == 1 else [pltpu.VMEM(acc_shape, F32)],
        compiler_params=_cp(("parallel", "parallel", "arbitrary")), name=name)(a, b, *extra)


def mm_nn(a, b, out_dtype, name, stack=None, tm=1024, tn=1024, tk=2048, n_cols=None, after=None):
    M, K = a.shape
    if stack is None:
        N = b.shape[1] if n_cols is None else n_cols
        tn, tk = _tile(N, tn), _tile(K, tk)
        b_spec = pl.BlockSpec((tk, tn), lambda i, j, k: (k, j))
    elif stack == "col":
        S, _, Ns = b.shape
        N = S * Ns
        tn, tk = _tile(Ns, tn), _tile(K, tk)
        npb = Ns // tn
        b_spec = pl.BlockSpec((None, tk, tn), lambda i, j, k: (j // npb, k, j % npb))
    else:
        S, Ks, N = b.shape
        tn, tk = _tile(N, tn), _tile(Ks, tk)
        kpb = Ks // tk
        b_spec = pl.BlockSpec((None, tk, tn), lambda i, j, k: (k // kpb, k % kpb, j))
    tm = _tile(M, tm)
    return _mm_call(a, b, jax.ShapeDtypeStruct((M, N), out_dtype), (M // tm, N // tn, K // tk),
                    pl.BlockSpec((tm, tk), lambda i, j, k: (i, k)), b_spec,
                    pl.BlockSpec((tm, tn), lambda i, j, k: (i, j)), (tm, tn), ((1,), (0,)), name, after=after)


def mm_cols_dilated(a, b, gcols, d, name, tm=1024, tn=512):
    L, K = a.shape
    S, _, Ns = b.shape
    tm, tn = _tile(L, tm), _tile(Ns, tn)
    npb = Ns // tn
    nj = len(gcols)
    rows = tm // d

    def body(cols_ref, a_ref, b_ref, o_ref, *scr):
        prod = jnp.dot(a_ref[...], b_ref[...], preferred_element_type=F32)
        if d == 1:
            o_ref[0] = prod.astype(BF16)
        else:
            for c in range(tn // 128):
                scr[0][c] = prod[:, c * 128:(c + 1) * 128]
            for r in range(d):
                for c in range(tn // 128):
                    o_ref[r, :, c * 128:(c + 1) * 128] = scr[0].at[c][pl.ds(r, rows, stride=d), :].astype(BF16)

    return pl.pallas_call(
        body,
        grid_spec=pltpu.PrefetchScalarGridSpec(
            num_scalar_prefetch=1, grid=(L // tm, nj),
            in_specs=[pl.BlockSpec((tm, K), lambda i, j, c: (i, 0)),
                      pl.BlockSpec((None, K, tn), lambda i, j, c: (c[j] // npb, 0, c[j] % npb))],
            out_specs=pl.BlockSpec((d, rows, tn), lambda i, j, c: (0, i, j)),
            scratch_shapes=[] if d == 1 else [pltpu.VMEM((tn // 128, tm, 128), F32)]),
        out_shape=jax.ShapeDtypeStruct((d, L // d, nj * tn), BF16),
        compiler_params=_cp(("parallel", "arbitrary")), name=name)(jnp.asarray(gcols, jnp.int32), a, b)


def mm_nt(a, b, out_dtype, name, stack=None, tm=1024, tn=1024, tk=2048, after=None, kw_rows=None):
    M, C = a.shape
    if stack is None:
        Kw = b.shape[0] if kw_rows is None else kw_rows
        tn, tk = _tile(Kw, tn), _tile(C, tk)
        b_spec = pl.BlockSpec((tn, tk), lambda i, j, k: (j, k))
    elif stack == "col":
        S, Kw, Cs = b.shape
        tn, tk = _tile(Kw, tn), _tile(Cs, tk)
        cpb = Cs // tk
        b_spec = pl.BlockSpec((None, tn, tk), lambda i, j, k: (k // cpb, j, k % cpb))
    else:
        S, Ks, _ = b.shape
        Kw = S * Ks
        tn, tk = _tile(Ks, tn), _tile(C, tk)
        jpb = Ks // tn
        b_spec = pl.BlockSpec((None, tn, tk), lambda i, j, k: (j // jpb, j % jpb, k))
    tm = _tile(M, tm)
    return _mm_call(a, b, jax.ShapeDtypeStruct((M, Kw), out_dtype), (M // tm, Kw // tn, C // tk),
                    pl.BlockSpec((tm, tk), lambda i, j, k: (i, k)), b_spec,
                    pl.BlockSpec((tm, tn), lambda i, j, k: (i, j)), (tm, tn), ((1,), (1,)), name, after=after)


def mm_tn(a, b, out_dtype, name, stack=None, n_stack=N_CHIPS, tm=1024, tn=1024, tk=2048, m_rows=None):
    L, M = a.shape
    N = b.shape[1]
    tk = _tile(L, tk)
    if stack is None:
        tm, tn = _tile(M, tm), _tile(N, tn)
        o_spec = pl.BlockSpec((tm, tn), lambda i, j, k: (i, j))
        out_shape = (M if m_rows is None else m_rows, N)
    elif stack == "col":
        Ns = N // n_stack
        tm, tn = _tile(M, tm), _tile(Ns, tn)
        npb = Ns // tn
        o_spec = pl.BlockSpec((None, tm, tn), lambda i, j, k: (j // npb, i, j % npb))
        out_shape = (n_stack, M, Ns)
    else:
        Ms = M // n_stack
        tm, tn = _tile(Ms, tm), _tile(N, tn)
        mpb = Ms // tm
        o_spec = pl.BlockSpec((None, tm, tn), lambda i, j, k: (i // mpb, i % mpb, j))
        out_shape = (n_stack, Ms, N)
    return _mm_call(a, b, jax.ShapeDtypeStruct(out_shape, out_dtype), (M // tm, N // tn, L // tk),
                    pl.BlockSpec((tk, tm), lambda i, j, k: (k, i)), pl.BlockSpec((tk, tn), lambda i, j, k: (k, j)),
                    o_spec, (tm, tn), ((0,), (0,)), name)


def _row_specs(tr, widths):
    return [pl.BlockSpec((tr, w), lambda i: (i, 0)) for w in widths]


def _vec_spec(w):
    return pl.BlockSpec((1, w), lambda i: (0, 0))


def _acc_rows(ref, val, i):
    s = jnp.sum(val, axis=0, keepdims=True)

    @pl.when(i == 0)
    def _():
        ref[...] = s

    @pl.when(i > 0)
    def _():
        ref[...] += s


def modulate(x, scale, shift, name):
    L, D = x.shape
    tr = _tile(L, 512, 16)

    def body(x_ref, sc_ref, sh_ref, h_ref):
        h_ref[...] = (x_ref[...] * (1.0 + sc_ref[...]) + sh_ref[...]).astype(BF16)

    return pl.pallas_call(
        body, grid=(L // tr,), in_specs=_row_specs(tr, [D]) + [_vec_spec(D)] * 2, out_specs=_row_specs(tr, [D])[0],
        out_shape=jax.ShapeDtypeStruct((L, D), BF16), compiler_params=_cp(("parallel",)), name=name)(x, scale, shift)


def _ln_core(x, y, gate, g, b):
    u = ALPHA * x + (1.0 + gate) * y
    mu = jnp.mean(u, axis=-1, keepdims=True)
    d = u - mu
    var = jnp.mean(d * d, axis=-1, keepdims=True)
    rstd = lax.rsqrt(var + LN_EPS)
    xhat = d * rstd
    return xhat * g + b, xhat, rstd


def ln_mid(x, y, gate, g, b, scale, shift):
    L, D = x.shape
    tr = _tile(L, 256, 16)

    def body(x_ref, y_ref, gate_ref, g_ref, b_ref, sc_ref, sh_ref, x1_ref, x1b_ref, h_ref):
        x1, _, _ = _ln_core(x_ref[...], y_ref[...], gate_ref[...], g_ref[...], b_ref[...])
        x1_ref[...] = x1
        x1b_ref[...] = x1.astype(BF16)
        h_ref[...] = (x1 * (1.0 + sc_ref[...]) + sh_ref[...]).astype(BF16)

    return pl.pallas_call(
        body, grid=(L // tr,), in_specs=_row_specs(tr, [D, D]) + [_vec_spec(D)] * 5,
        out_specs=_row_specs(tr, [D, D, D]),
        out_shape=[jax.ShapeDtypeStruct((L, D), F32), jax.ShapeDtypeStruct((L, D), BF16),
                   jax.ShapeDtypeStruct((L, D), BF16)],
        compiler_params=_cp(("parallel",)), name="ln_mid")(x, y, gate, g, b, scale, shift)


def _ln_bwd_rows(dout_v, xhat, rstd, g):
    dxh = dout_v * g
    m1 = jnp.mean(dxh, axis=-1, keepdims=True)
    m2 = jnp.mean(dxh * xhat, axis=-1, keepdims=True)
    return rstd * (dxh - m1 - xhat * m2)


def ln_final_fwd_bwd(x, y, gate, g, b, target):
    L, D = x.shape
    tr = _tile(L, 256, 16)

    def body(x_ref, y_ref, gate_ref, g_ref, b_ref, t_ref, dres_ref, dy_ref, dg_ref, db_ref, dgate_ref, sq_ref):
        i = pl.program_id(0)
        yv = y_ref[...]
        out, xhat, rstd = _ln_core(x_ref[...], yv, gate_ref[...], g_ref[...], b_ref[...])
        err = out - t_ref[...]
        dout_v = err * (1.0 / D)
        du = _ln_bwd_rows(dout_v, xhat, rstd, g_ref[...])
        dres_ref[...] = ALPHA * du
        dy_ref[...] = ((1.0 + gate_ref[...]) * du).astype(BF16)
        _acc_rows(dg_ref, dout_v * xhat, i)
        _acc_rows(db_ref, dout_v, i)
        _acc_rows(dgate_ref, du * yv, i)
        _acc_rows(sq_ref, err * err, i)

    return pl.pallas_call(
        body, grid=(L // tr,), in_specs=_row_specs(tr, [D, D]) + [_vec_spec(D)] * 3 + _row_specs(tr, [D]),
        out_specs=_row_specs(tr, [D, D]) + [_vec_spec(D)] * 4,
        out_shape=[jax.ShapeDtypeStruct((L, D), F32), jax.ShapeDtypeStruct((L, D), BF16)]
        + [jax.ShapeDtypeStruct((1, D), F32)] * 4,
        compiler_params=_cp(("arbitrary",)), name="ln_final_fwd_bwd")(x, y, gate, g, b, target)


def mod_bwd(dres, dh, dh2, xin, scale, name):
    L, D = xin.shape
    tr = _tile(L, 256, 16)

    def body(dres_ref, dh_ref, dh2_ref, x_ref, sc_ref, dx_ref, dsc_ref, dsh_ref):
        i = pl.program_id(0)
        dh_v = dh_ref[...].astype(F32) + dh2_ref[...].astype(F32)
        dx_ref[...] = dres_ref[...] + dh_v * (1.0 + sc_ref[...])
        _acc_rows(dsc_ref, dh_v * x_ref[...], i)
        _acc_rows(dsh_ref, dh_v, i)

    return pl.pallas_call(
        body, grid=(L // tr,), in_specs=_row_specs(tr, [D, D, D, D]) + [_vec_spec(D)],
        out_specs=_row_specs(tr, [D]) + [_vec_spec(D)] * 2,
        out_shape=[jax.ShapeDtypeStruct((L, D), F32)] + [jax.ShapeDtypeStruct((1, D), F32)] * 2,
        compiler_params=_cp(("arbitrary",)), name=name)(dres, dh, dh2, xin, scale)


def mod_ln_bwd(dres_in, dh, dskip, xmid, scale, x, y, gate, g):
    L, D = x.shape
    tr = _tile(L, 256, 16)

    def body(dres_ref, dh_ref, dskip_ref, xm_ref, sc_ref, x_ref, y_ref, gate_ref, g_ref,
             dres_out, dy_ref, dg_ref, db_ref, dgate_ref, dsc_ref, dsh_ref):
        i = pl.program_id(0)
        dh_v = dh_ref[...].astype(F32)
        dout_v = dres_ref[...] + dskip_ref[...].astype(F32) + dh_v * (1.0 + sc_ref[...])
        _acc_rows(dsc_ref, dh_v * xm_ref[...], i)
        _acc_rows(dsh_ref, dh_v, i)
        yv = y_ref[...]
        _, xhat, rstd = _ln_core(x_ref[...], yv, gate_ref[...], g_ref[...], 0.0)
        du = _ln_bwd_rows(dout_v, xhat, rstd, g_ref[...])
        dres_out[...] = ALPHA * du
        dy_ref[...] = ((1.0 + gate_ref[...]) * du).astype(BF16)
        _acc_rows(dg_ref, dout_v * xhat, i)
        _acc_rows(db_ref, dout_v, i)
        _acc_rows(dgate_ref, du * yv, i)

    return pl.pallas_call(
        body, grid=(L // tr,),
        in_specs=_row_specs(tr, [D] * 4) + [_vec_spec(D)] + _row_specs(tr, [D, D]) + [_vec_spec(D)] * 2,
        out_specs=_row_specs(tr, [D, D]) + [_vec_spec(D)] * 5,
        out_shape=[jax.ShapeDtypeStruct((L, D), F32), jax.ShapeDtypeStruct((L, D), BF16)]
        + [jax.ShapeDtypeStruct((1, D), F32)] * 5,
        compiler_params=_cp(("arbitrary",)), name="mod_ln_bwd")(dres_in, dh, dskip, xmid, scale, x, y, gate, g)


CONV_HALO = 16


def _conv_rows(x_ref, i, tr, L):
    nblk = L // tr
    s = pl.multiple_of(i * tr, CONV_HALO)
    cur = x_ref[pl.ds(s, tr), :].astype(F32)
    sp = pl.multiple_of(jnp.maximum(i * tr - CONV_HALO, 0), CONV_HALO)
    sn = pl.multiple_of(jnp.minimum(i * tr + tr, L - CONV_HALO), CONV_HALO)
    prev = x_ref[pl.ds(sp, CONV_HALO), :].astype(F32) * (i > 0).astype(F32)
    nxt = x_ref[pl.ds(sn, CONV_HALO), :].astype(F32) * (i < nblk - 1).astype(F32)
    return jnp.concatenate([prev, cur, nxt], axis=0)


def _shift_rows(v, j):
    n = v.shape[0]
    return v if j % n == 0 else pltpu.roll(v, j % n, 0)


def _conv_taps(xe):
    return [_shift_rows(xe, CONV_W - 1 - k) for k in range(CONV_W)]


def _conv_eval(taps, w_ref, b_ref):
    c = b_ref[...] + w_ref[0:1, :] * taps[0]
    for k in range(1, CONV_W):
        c = c + w_ref[k:k + 1, :] * taps[k]
    return c


def conv_fwd(zx, col0, conv_w, conv_b):
    L = zx.shape[0]
    C = conv_w.shape[1]
    tc = _tile(C, 512)
    tr = _tile(L, 512, CONV_HALO)
    off = col0 // tc

    def body(x_ref, w_ref, b_ref, o_ref):
        i = pl.program_id(1)
        xe = _conv_rows(x_ref, i, tr, L)
        c = _conv_eval(_conv_taps(xe), w_ref, b_ref)[CONV_HALO:CONV_HALO + tr]
        o_ref[...] = _silu(c).astype(BF16)

    return pl.pallas_call(
        body, grid=(C // tc, L // tr),
        in_specs=[pl.BlockSpec((L, tc), lambda j, i: (0, off + j)), pl.BlockSpec((CONV_W, tc), lambda j, i: (0, j)),
                  pl.BlockSpec((1, tc), lambda j, i: (0, j))],
        out_specs=pl.BlockSpec((tr, tc), lambda j, i: (i, j)),
        out_shape=jax.ShapeDtypeStruct((L, C), BF16), compiler_params=_cp(("parallel", "arbitrary")),
        name="conv_fwd")(zx, conv_w, conv_b)


def conv_bwd(zx, col0, conv_w, conv_b, g, dzx, name):
    L = zx.shape[0]
    C = conv_w.shape[1]
    tc = _tile(C, 512)
    tr = _tile(L, 512, CONV_HALO)
    off = col0 // tc
    H = CONV_HALO

    def body(x_ref, g_ref, w_ref, b_ref, buf_ref, dx_ref, dw_ref, db_ref):
        i = pl.program_id(1)
        xe = _conv_rows(x_ref, i, tr, L)
        ge = _conv_rows(g_ref, i, tr, L)
        taps = _conv_taps(xe)
        dc = ge * _dsilu(_conv_eval(taps, w_ref, b_ref))
        dx = w_ref[CONV_W - 1:CONV_W, :] * dc
        for k in range(CONV_W - 1):
            dx = dx + w_ref[k:k + 1, :] * _shift_rows(dc, -(CONV_W - 1 - k))
        dx_ref[...] = dx[H:H + tr].astype(BF16)
        dcc = dc[H:H + tr]
        rows = [jnp.sum(dcc * taps[k][H:H + tr], axis=0, keepdims=True) for k in range(CONV_W)]
        dwv = jnp.concatenate(rows + [jnp.zeros((8 - CONV_W, tc), F32)], axis=0)
        dbv = jnp.sum(dcc, axis=0, keepdims=True)

        @pl.when(i == 0)
        def _():
            dw_ref[...] = dwv
            db_ref[...] = dbv

        @pl.when(i > 0)
        def _():
            dw_ref[...] += dwv
            db_ref[...] += dbv

    dx, dw, db = pl.pallas_call(
        body, grid=(C // tc, L // tr),
        in_specs=[pl.BlockSpec((L, tc), lambda j, i: (0, off + j)), pl.BlockSpec((L, tc), lambda j, i: (0, j)),
                  pl.BlockSpec((CONV_W, tc), lambda j, i: (0, j)), pl.BlockSpec((1, tc), lambda j, i: (0, j)), _ANY],
        out_specs=[pl.BlockSpec((tr, tc), lambda j, i: (i, off + j)), pl.BlockSpec((8, tc), lambda j, i: (0, j)),
                   pl.BlockSpec((1, tc), lambda j, i: (0, j))],
        out_shape=[jax.ShapeDtypeStruct(dzx.shape, BF16), jax.ShapeDtypeStruct((8, C), F32),
                   jax.ShapeDtypeStruct((1, C), F32)],
        input_output_aliases={4: 0},
        compiler_params=_cp(("parallel", "arbitrary")), name=name)(zx, g, conv_w, conv_b, dzx)
    return dx, dw[:CONV_W], db


_NN = (((1,), (0,)), ((), ()))


def _pieces(x, n):
    out, r = [], x
    for _ in range(n):
        p = r.astype(BF16)
        out.append(p)
        r = r - p.astype(F32)
    return out


def _dot01(a, b01, n, dims=_NN):
    b = b01.astype(BF16)
    return functools.reduce(lambda u, v: u + v,
                            [lax.dot_general(p, b, dims, preferred_element_type=F32) for p in _pieces(a, n)])


def _dot01_left(a01, b, n, dims=_NN):
    a = a01.astype(BF16)
    return functools.reduce(lambda u, v: u + v,
                            [lax.dot_general(a, p, dims, preferred_element_type=F32) for p in _pieces(b, n)])


def _ssd_common(dtp_ref, dtpT_ref, bias_ref, biasT_ref, alog_ref, alogT_ref, b_ref, c_ref):
    Q = SSD_Q
    dt = _softplus(dtp_ref[...] + bias_ref[...])
    A = -jnp.exp(alog_ref[...])
    row = lax.broadcasted_iota(jnp.int32, (Q, Q), 0)
    col = lax.broadcasted_iota(jnp.int32, (Q, Q), 1)
    causal = row >= col
    tril = causal.astype(F32)
    Kh = dt.shape[1]
    acum = _dot01_left(tril, dt * A, 3)
    eye = (lax.broadcasted_iota(jnp.int32, (Kh, Kh), 0) == lax.broadcasted_iota(jnp.int32, (Kh, Kh), 1)).astype(F32)
    acumT = _dot01_left(eye, acum, 3, dims=(((1,), (1,)), ((), ())))
    Bm = b_ref[...]
    Cm = c_ref[...]
    cb = lax.dot_general(Cm, Bm, (((1,), (1,)), ((), ())), preferred_element_type=F32)
    return dt, A, causal, row, col, acum, acumT, Bm, Cm, cb


def _ssd_in_specs(Q, GP, N, Kh, DI, cmap):
    nb0 = DI // N
    vec = pl.BlockSpec((None, 1, Kh), lambda g, c: (g, 0, 0))
    vecT = pl.BlockSpec((None, Kh, 1), lambda g, c: (g, 0, 0))
    return [pl.BlockSpec((Q, GP), lambda g, c: (cmap(c), g)),
            pl.BlockSpec((Q, N), lambda g, c: (cmap(c), nb0 + g)),
            pl.BlockSpec((Q, N), lambda g, c: (cmap(c), nb0 + SSD_G + g)),
            pl.BlockSpec((None, Q, Kh), lambda g, c: (g, cmap(c), 0)),
            pl.BlockSpec((None, Kh, Q), lambda g, c: (g, 0, cmap(c))),
            vec, vecT, vec, vecT, vec, vecT]


def _hi(a, b01):
    return _dot01(a, b01, 2)


def _headsum(a, b01):
    return _dot01(a, b01, 1)


def _ssd_heads(dskT_ref, acum, acumT, dt, Kh):
    Q, P, N = SSD_Q, SSD_P, SSD_N
    GP = Kh * P
    sh_p = P.bit_length() - 1
    seg = lambda shape, dim: lax.shift_right_logical(lax.broadcasted_iota(jnp.int32, shape, dim), sh_p)
    E = (seg((Kh, GP), 1) == lax.broadcasted_iota(jnp.int32, (Kh, GP), 0)).astype(F32)
    ET = (seg((GP, Kh), 0) == lax.broadcasted_iota(jnp.int32, (GP, Kh), 1)).astype(F32)
    a_last = acum[Q - 1:Q, :]
    tail = jnp.exp(a_last - acum)
    eLT = jnp.exp(acumT[:, Q - 1:Q])
    rowseg = seg((GP, N), 0)
    eL_b = jnp.zeros((GP, N), F32)
    for k in range(Kh):
        eL_b = jnp.where(rowseg == k, eLT[k:k + 1, :], eL_b)
    return dict(
        E=E, ET=ET, a_last=a_last, tail=tail, eL_b=eL_b,
        dt_all=_hi(dt, E), ea_all=_hi(jnp.exp(acum), E), tail_all=_hi(tail, E),
        dsk_all=jnp.sum(E * dskT_ref[...], axis=0, keepdims=True))


def _head_chunks(GP):
    CW = min(GP, 128)
    return CW, CW // SSD_P, GP // CW


def _head_mask(Q, CW, kk):
    lane = lax.broadcasted_iota(jnp.int32, (Q, CW), 1)
    return jnp.logical_and(lane >= kk * SSD_P, lane < (kk + 1) * SSD_P)


def ssd_fwd(xbc, dtp_g, dtp_gT, bias_g, bias_gT, alog_g, alog_gT, dsk_g, dsk_gT, zx, norm_g, DI):
    L = xbc.shape[0]
    Q, P, N, G = SSD_Q, SSD_P, SSD_N, SSD_G
    GP = DI // G
    Kh = GP // P
    nc = L // Q

    CW, hpc, nch = _head_chunks(GP)
    nt = (((1,), (1,)), ((), ()))
    tn = (((0,), (0,)), ((), ()))

    def body(xs_ref, b_ref, c_ref, dtp_ref, dtpT_ref, bias_ref, biasT_ref, alog_ref, alogT_ref, dsk_ref, dskT_ref,
             z_ref, ng_ref, y_ref, st_ref, yn_ref, state):
        @pl.when(pl.program_id(1) == 0)
        def _():
            state[...] = jnp.zeros(state.shape, F32)

        st_ref[...] = state[...]
        dt, A, causal, row, col, acum, acumT, Bm, Cm, cb = _ssd_common(
            dtp_ref, dtpT_ref, bias_ref, biasT_ref, alog_ref, alogT_ref, b_ref, c_ref)
        hd = _ssd_heads(dskT_ref, acum, acumT, dt, Kh)
        xs = xs_ref[...].astype(F32)
        xdt_all = xs * hd["dt_all"]
        S_all = state[...]
        y_all = (lax.dot_general(Cm, S_all.astype(BF16), nt, preferred_element_type=F32) * hd["ea_all"]
                 + xs * hd["dsk_all"])
        state[...] = S_all * hd["eL_b"] + lax.dot_general(
            (xdt_all * hd["tail_all"]).astype(BF16), Bm, tn, preferred_element_type=F32)
        for ch in range(nch):
            cs = slice(ch * CW, (ch + 1) * CW)
            xc = xdt_all[:, cs]
            acc = y_all[:, cs]
            for kk in range(hpc):
                k = ch * hpc + kk
                decay = jnp.exp(jnp.where(causal, acum[:, k:k + 1] - acumT[k:k + 1, :], -jnp.inf))
                xk = xc if hpc == 1 else jnp.where(_head_mask(Q, CW, kk), xc, 0.0)
                acc = acc + jnp.dot((cb * decay).astype(BF16), xk.astype(BF16), preferred_element_type=F32)
            y_ref[:, cs] = acc.astype(BF16)
        y2 = y_ref[...].astype(F32) * _silu(z_ref[...].astype(F32))
        rr = lax.rsqrt(jnp.mean(y2 * y2, axis=-1, keepdims=True) + RMS_EPS)
        yn_ref[...] = (y2 * rr * ng_ref[...]).astype(BF16)

    tile = pl.BlockSpec((Q, GP), lambda g, c: (c, g))
    return pl.pallas_call(
        body, grid=(G, nc),
        in_specs=_ssd_in_specs(Q, GP, N, Kh, DI, lambda c: c) + [tile, pl.BlockSpec((1, GP), lambda g, c: (0, g))],
        out_specs=[tile, pl.BlockSpec((None, None, GP, N), lambda g, c: (c, g, 0, 0)), tile],
        out_shape=[jax.ShapeDtypeStruct((L, DI), BF16), jax.ShapeDtypeStruct((nc, G, GP, N), F32),
                   jax.ShapeDtypeStruct((L, DI), BF16)],
        scratch_shapes=[pltpu.VMEM((GP, N), F32)], compiler_params=_cp(("parallel", "arbitrary")),
        name="ssd_fwd")(xbc, xbc, xbc, dtp_g, dtp_gT, bias_g, bias_gT, alog_g, alog_gT, dsk_g, dsk_gT, zx, norm_g)


def ssd_bwd(xbc, dtp_g, dtp_gT, bias_g, bias_gT, alog_g, alog_gT, dsk_g, dsk_gT, states, dyn, y, zx, norm_g, DI):
    L = xbc.shape[0]
    Q, P, N, G = SSD_Q, SSD_P, SSD_N, SSD_G
    GP = DI // G
    Kh = GP // P
    nc = L // Q
    rev = lambda c: nc - 1 - c

    CW, hpc, nch = _head_chunks(GP)

    def body(xs_ref, b_ref, c_ref, dtp_ref, dtpT_ref, bias_ref, biasT_ref, alog_ref, alogT_ref, dsk_ref, dskT_ref,
             st_ref, dyn_ref, y_ref, z_ref, ng_ref,
             dxs_ref, dB_ref, dC_ref, ddtp_ref, dbias_ref, dalog_ref, dD_ref, dz_ref, dng_ref, dstate):
        ci = pl.program_id(1)

        @pl.when(ci == 0)
        def _():
            dstate[...] = jnp.zeros(dstate.shape, F32)

        dt, A, causal, row, col, acum, acumT, Bm, Cm, cb = _ssd_common(
            dtp_ref, dtpT_ref, bias_ref, biasT_ref, alog_ref, alogT_ref, b_ref, c_ref)
        tn = (((0,), (0,)), ((), ()))
        nt = (((1,), (1,)), ((), ()))
        hd = _ssd_heads(dskT_ref, acum, acumT, dt, Kh)
        ET, tail = hd["ET"], hd["tail"]
        cbT = lax.dot_general(Bm, Cm, nt, preferred_element_type=F32)
        causalT = row <= col
        xs = xs_ref[...].astype(F32)
        xdt_all = xs * hd["dt_all"]
        yv = y_ref[...].astype(F32)
        zv = z_ref[...].astype(F32)
        dynv = dyn_ref[...].astype(F32)
        sz = _silu(zv)
        y2 = yv * sz
        rr = lax.rsqrt(jnp.mean(y2 * y2, axis=-1, keepdims=True) + RMS_EPS)
        yh = y2 * rr
        dyh = dynv * ng_ref[...]
        dy2 = rr * (dyh - yh * jnp.mean(dyh * yh, axis=-1, keepdims=True))
        dz_ref[...] = (dy2 * yv * _dsilu(zv)).astype(BF16)
        dng_v = jnp.sum(dynv * yh, axis=0, keepdims=True)
        dyb = (dy2 * sz).astype(BF16)
        dy_all = dyb.astype(F32)
        S_all = st_ref[...]
        S_b = S_all.astype(BF16)
        dS_all = dstate[...]
        dS_b = dS_all.astype(BF16)
        CS_all = lax.dot_general(Cm, S_b, nt, preferred_element_type=F32)
        dyE_b = (dy_all * hd["ea_all"]).astype(BF16)
        dC_acc = jnp.dot(dyE_b, S_b, preferred_element_type=F32)
        dS_y = lax.dot_general(dyE_b, Cm, tn, preferred_element_type=F32)
        BdS_all = lax.dot_general(Bm, dS_b, nt, preferred_element_type=F32)
        dB_acc = jnp.dot((xdt_all * hd["tail_all"]).astype(BF16), dS_b, preferred_element_type=F32)
        dtail = _headsum(xdt_all * BdS_all, ET)
        da_cols = _headsum(dy_all * CS_all * hd["ea_all"], ET) - dtail * tail
        dss = _dot01_left(jnp.ones((8, N), F32), _dot01_left(hd["E"], dS_all * S_all, 2), 2, dims=nt)
        da_last = dss[0:1] * jnp.exp(hd["a_last"]) + jnp.sum(dtail * tail, axis=0, keepdims=True)
        rowi = lax.broadcasted_iota(jnp.int32, (Q, Kh), 0)
        da_cols = da_cols + jnp.where(rowi == Q - 1, da_last, 0.0)
        dstate[...] = hd["eL_b"] * dS_all + dS_y
        sum_mg = jnp.zeros((Q, Q), F32)
        ddt_x = jnp.zeros((Q, Kh), F32)
        da_rows = jnp.zeros((Kh, Q), F32)
        lane_k = lax.broadcasted_iota(jnp.int32, (Q, Kh), 1)
        sub_k = lax.broadcasted_iota(jnp.int32, (Kh, Q), 0)
        for ch in range(nch):
            cs = slice(ch * CW, (ch + 1) * CW)
            dyc = dyb[:, cs]
            xc_b = xdt_all[:, cs].astype(BF16)
            acc = hd["tail_all"][:, cs] * BdS_all[:, cs]
            for kk in range(hpc):
                k = ch * hpc + kk
                a_b = jnp.broadcast_to(acum[:, k:k + 1], (Q, Q))
                a_r = acumT[k:k + 1, :]
                decay = jnp.exp(jnp.where(causal, a_b - a_r, -jnp.inf))
                decayT = jnp.exp(jnp.where(causalT, a_r - a_b, -jnp.inf))
                dyk = dyc if hpc == 1 else jnp.where(_head_mask(Q, CW, kk), dyc, jnp.zeros_like(dyc))
                mg = decay * lax.dot_general(dyk, xc_b, nt, preferred_element_type=F32)
                sum_mg = sum_mg + mg
                w = mg * cb
                da_cols = da_cols + jnp.where(lane_k == k, jnp.sum(w, axis=1, keepdims=True), 0.0)
                da_rows = da_rows + jnp.where(sub_k == k, jnp.sum(w, axis=0, keepdims=True), 0.0)
                acc = acc + jnp.dot((decayT * cbT).astype(BF16), dyk, preferred_element_type=F32)
            dxs_ref[:, cs] = (acc * hd["dt_all"][:, cs] + dy_all[:, cs] * hd["dsk_all"][:, cs]).astype(BF16)
            ddt_x = ddt_x + _headsum(acc * xs[:, cs], ET[cs, :])
        eye_q = (row == col).astype(F32)
        da_cols = da_cols - _dot01_left(eye_q, da_rows, 3, dims=nt)
        dD_row = jnp.sum(_headsum(dy_all * xs, ET), axis=0, keepdims=True)
        sum_mg_b = sum_mg.astype(BF16)
        dB_ref[...] = (dB_acc + lax.dot_general(sum_mg_b, Cm, tn, preferred_element_type=F32)).astype(BF16)
        dC_ref[...] = (dC_acc + jnp.dot(sum_mg_b, Bm, preferred_element_type=F32)).astype(BF16)
        triu = (row <= col).astype(F32)
        ddtA = _dot01_left(triu, da_cols, 3)
        ddt = ddt_x + ddtA * A
        dpre = ddt * _sigmoid(dtp_ref[...] + bias_ref[...])
        ddtp_ref[...] = dpre
        dbias_v = jnp.sum(dpre, axis=0, keepdims=True)
        dalog_v = jnp.sum(ddtA * dt, axis=0, keepdims=True) * A

        @pl.when(ci == 0)
        def _():
            dbias_ref[...] = dbias_v
            dalog_ref[...] = dalog_v
            dD_ref[...] = dD_row
            dng_ref[...] = dng_v

        @pl.when(ci > 0)
        def _():
            dbias_ref[...] += dbias_v
            dalog_ref[...] += dalog_v
            dD_ref[...] += dD_row
            dng_ref[...] += dng_v

    vec_o = pl.BlockSpec((None, 1, Kh), lambda g, c: (g, 0, 0))
    tile = pl.BlockSpec((Q, GP), lambda g, c: (rev(c), g))
    return pl.pallas_call(
        body, grid=(G, nc),
        in_specs=_ssd_in_specs(Q, GP, N, Kh, DI, rev)
        + [pl.BlockSpec((None, None, GP, N), lambda g, c: (rev(c), g, 0, 0)), tile, tile, tile,
           pl.BlockSpec((1, GP), lambda g, c: (0, g))],
        out_specs=[tile, pl.BlockSpec((Q, N), lambda g, c: (rev(c), g)), pl.BlockSpec((Q, N), lambda g, c: (rev(c), g)),
                   pl.BlockSpec((None, Q, Kh), lambda g, c: (g, rev(c), 0)), vec_o, vec_o, vec_o,
                   tile, pl.BlockSpec((1, GP), lambda g, c: (0, g))],
        out_shape=[jax.ShapeDtypeStruct((L, DI), BF16), jax.ShapeDtypeStruct((L, G * N), BF16),
                   jax.ShapeDtypeStruct((L, G * N), BF16), jax.ShapeDtypeStruct((G, L, Kh), F32)]
        + [jax.ShapeDtypeStruct((G, 1, Kh), F32)] * 3
        + [jax.ShapeDtypeStruct(zx.shape, BF16), jax.ShapeDtypeStruct((1, DI), F32)],
        scratch_shapes=[pltpu.VMEM((GP, N), F32)], compiler_params=_cp(("parallel", "arbitrary")),
        name="ssd_bwd")(xbc, xbc, xbc, dtp_g, dtp_gT, bias_g, bias_gT, alog_g, alog_gT, dsk_g, dsk_gT, states,
                        dyn, y, zx, norm_g)


def _alibi_slope(gi, h):
    n = len(DIL_PATTERNS) * DIL_H
    return float(2.0 ** (-8.0 * (gi * DIL_H + h + 1) / n))


def _attn_masks():
    qi = lax.broadcasted_iota(jnp.int32, (DIL_BLK, DIL_BLK), 0)
    kj = lax.broadcasted_iota(jnp.int32, (DIL_BLK, DIL_BLK), 1)
    dcur = (qi - kj).astype(F32)
    return dcur, qi >= kj, dcur + float(DIL_BLK), kj >= qi


def attn_fwd(q3, kv3, gi):
    window, d = DIL_PATTERNS[gi]
    assert window // d == DIL_BLK
    HW = DIL_H * DIL_E
    M = q3.shape[1]
    nb = M // DIL_BLK
    scale = DIL_E ** -0.5
    nt = (((1,), (1,)), ((), ()))

    def body(q_ref, kp_ref, kc_ref, vp_ref, vc_ref, o_ref, lse_ref):
        n = pl.program_id(1)
        dcur, vcur, dprev, vprev0 = _attn_masks()
        dist = jnp.concatenate([dprev, dcur], axis=1)
        valid = jnp.concatenate([jnp.logical_and(vprev0, n > 0), vcur], axis=1)
        lane = lax.broadcasted_iota(jnp.int32, (DIL_BLK, 128), 1)
        lse_acc = jnp.zeros((DIL_BLK, 128), F32)
        for h in range(DIL_H):
            hs = slice(h * DIL_E, (h + 1) * DIL_E)
            sl = _alibi_slope(gi, h) * d
            kcat = jnp.concatenate([kp_ref[:, hs], kc_ref[:, hs]], axis=0)
            vcat = jnp.concatenate([vp_ref[:, hs], vc_ref[:, hs]], axis=0)
            s = lax.dot_general(q_ref[:, hs], kcat, nt, preferred_element_type=F32) * scale - sl * dist
            s = jnp.where(valid, s, -jnp.inf)
            m = jnp.max(s, axis=-1, keepdims=True)
            p = jnp.exp(s - m)
            den = jnp.sum(p, axis=-1, keepdims=True)
            o = jnp.dot(p.astype(BF16), vcat, preferred_element_type=F32) / den
            o_ref[:, hs] = o.astype(BF16)
            lse_acc = jnp.where(lane == h, m + jnp.log(den), lse_acc)
        lse_ref[...] = lse_acc

    blk = (None, DIL_BLK, HW)
    prev = lambda n: jnp.maximum(n - 1, 0)
    return pl.pallas_call(
        body, grid=(d, nb),
        in_specs=[pl.BlockSpec(blk, lambda r, n: (r, n, 0)),
                  pl.BlockSpec(blk, lambda r, n: (r, prev(n), 0)), pl.BlockSpec(blk, lambda r, n: (r, n, 0)),
                  pl.BlockSpec(blk, lambda r, n: (r, prev(n), 1)), pl.BlockSpec(blk, lambda r, n: (r, n, 1))],
        out_specs=[pl.BlockSpec(blk, lambda r, n: (r, n, 0)), pl.BlockSpec((None, DIL_BLK, 128), lambda r, n: (r, n, 0))],
        out_shape=[jax.ShapeDtypeStruct((d, M, HW), BF16), jax.ShapeDtypeStruct((d, M, 128), F32)],
        compiler_params=_cp(("parallel", "parallel")), name=f"attn_fwd_{gi}")(q3, kv3, kv3, kv3, kv3)


def attn_bwd(q3, kv3, do3, lse3, dpr3, gi):
    window, d = DIL_PATTERNS[gi]
    HW = DIL_H * DIL_E
    M = q3.shape[1]
    L = M * d
    nb = M // DIL_BLK
    scale = DIL_E ** -0.5
    nt = (((1,), (1,)), ((), ()))
    tn = (((0,), (0,)), ((), ()))

    def body(q0_ref, q1_ref, k_ref, v_ref, do0_ref, do1_ref, l0_ref, l1_ref, r0_ref, r1_ref,
             dq_ref, dk_ref, dv_ref, carry):
        n = pl.program_id(1)

        @pl.when(n == 0)
        def _():
            carry[...] = jnp.zeros(carry.shape, F32)

        dcur, vcur, dprev, vprev0 = _attn_masks()
        dist = jnp.concatenate([dcur, dprev], axis=0)
        valid = jnp.concatenate([vcur, jnp.logical_and(vprev0, n < nb - 1)], axis=0)
        B = DIL_BLK
        for h in range(DIL_H):
            hs = slice(h * DIL_E, (h + 1) * DIL_E)
            sl = _alibi_slope(gi, h) * d
            kh = k_ref[:, hs]
            vh = v_ref[:, hs]
            qcat = jnp.concatenate([q0_ref[:, hs], q1_ref[:, hs]], axis=0)
            docat = jnp.concatenate([do0_ref[:, hs], do1_ref[:, hs]], axis=0)
            lcat = jnp.concatenate([l0_ref[:, h:h + 1], l1_ref[:, h:h + 1]], axis=0)
            rcat = jnp.concatenate([r0_ref[:, h:h + 1], r1_ref[:, h:h + 1]], axis=0)
            s = lax.dot_general(qcat, kh, nt, preferred_element_type=F32) * scale - sl * dist
            p = jnp.exp(jnp.where(valid, s - lcat, -jnp.inf))
            ds = p * (lax.dot_general(docat, vh, nt, preferred_element_type=F32) - rcat)
            ds_b = (ds * scale).astype(BF16)
            dv_ref[:, hs] = lax.dot_general(p.astype(BF16), docat, tn, preferred_element_type=F32).astype(BF16)
            dk_ref[:, hs] = lax.dot_general(ds_b, qcat, tn, preferred_element_type=F32).astype(BF16)
            dqc = jnp.dot(ds_b, kh, preferred_element_type=F32)
            dq_ref[:, hs] = (carry[:, hs] + dqc[:B]).astype(BF16)
            carry[:, hs] = dqc[B:]

    blk = (None, DIL_BLK, HW)
    sblk = (None, DIL_BLK, 128)
    oblk = (DIL_BLK, HW)
    nxt = lambda n: jnp.minimum(n + 1, nb - 1)
    here = lambda c: (lambda r, n: (r, n, c))
    ahead = lambda c: (lambda r, n: (r, nxt(n), c))
    outs = pl.pallas_call(
        body, grid=(d, nb),
        in_specs=[pl.BlockSpec(blk, here(0)), pl.BlockSpec(blk, ahead(0)),
                  pl.BlockSpec(blk, here(0)), pl.BlockSpec(blk, here(1)),
                  pl.BlockSpec(blk, here(0)), pl.BlockSpec(blk, ahead(0)),
                  pl.BlockSpec(sblk, here(0)), pl.BlockSpec(sblk, ahead(0)),
                  pl.BlockSpec(sblk, here(0)), pl.BlockSpec(sblk, ahead(0))],
        out_specs=[pl.BlockSpec(oblk, lambda r, n: (n, r))] * 3,
        out_shape=[jax.ShapeDtypeStruct((M, d * HW), BF16)] * 3,
        scratch_shapes=[pltpu.VMEM(oblk, F32)], compiler_params=_cp(("parallel", "arbitrary")),
        name=f"attn_bwd_{gi}")(q3, q3, kv3, kv3, do3, do3, lse3, lse3, dpr3, dpr3)
    return [t.reshape(L, HW) for t in outs]


def _merge_weights(l_tiles, h):
    ls = [t[:, h:h + 1] for t in l_tiles]
    mx = functools.reduce(jnp.maximum, ls)
    es = [jnp.exp(l - mx) for l in ls]
    den = functools.reduce(lambda a, b: a + b, es)
    return [e / den for e in es]


def _dil_specs(tr, arrs):
    return [pl.BlockSpec((a.shape[0], tr // a.shape[0], a.shape[2]), lambda i: (0, i, 0)) for a in arrs]


def _dil_scratch(tr, arrs):
    return [pltpu.VMEM((a.shape[2] // 128, tr, 128), F32) for a in arrs if a.shape[0] > 1]


def _undilate(refs3, scrs, tr):
    out, k = [], 0
    for ref in refs3:
        d, _, W = ref.shape
        if d == 1:
            out.append(lambda c, ref=ref: ref[0, :, c * 128:(c + 1) * 128])
            continue
        scr = scrs[k]
        k += 1
        for r in range(d):
            for c in range(W // 128):
                scr.at[c][pl.ds(r, tr // d, stride=d), :] = ref[r, :, c * 128:(c + 1) * 128].astype(F32)
        out.append(lambda c, scr=scr: scr[c])
    return out


def merge_fwd(os3, lses3, z):
    HW = os3[0].shape[2]
    L = os3[0].shape[0] * os3[0].shape[1]
    tr = _tile(L, 256, 16)
    ng = len(os3)
    n_scr = len(_dil_scratch(tr, os3))

    def body(*refs):
        z_ref, out_ref = refs[2 * ng], refs[2 * ng + 1]
        scrs = refs[2 * ng + 2:]
        o_get = _undilate(refs[:ng], scrs[:n_scr], tr)
        l_tiles = [g(0) for g in _undilate(refs[ng:2 * ng], scrs[n_scr:], tr)]
        for h in range(DIL_H):
            hs = slice(h * DIL_E, (h + 1) * DIL_E)
            ws = _merge_weights(l_tiles, h)
            om = functools.reduce(lambda a, b: a + b, [w * o(h).astype(F32) for w, o in zip(ws, o_get)])
            out_ref[:, hs] = (om * _silu(z_ref[:, hs].astype(F32))).astype(BF16)

    return pl.pallas_call(
        body, grid=(L // tr,),
        in_specs=_dil_specs(tr, os3) + _dil_specs(tr, lses3) + _row_specs(tr, [HW]),
        out_specs=_row_specs(tr, [HW])[0], out_shape=jax.ShapeDtypeStruct((L, HW), BF16),
        scratch_shapes=_dil_scratch(tr, os3) + _dil_scratch(tr, lses3),
        compiler_params=_cp(("parallel",)), name="merge_fwd")(*os3, *lses3, z)


def merge_bwd(dgated, os3, lses3, z):
    HW = os3[0].shape[2]
    L = os3[0].shape[0] * os3[0].shape[1]
    tr = _tile(L, 256, 16)
    ng = len(os3)
    n_scr = len(_dil_scratch(tr, os3))

    def body(*refs):
        dg_ref = refs[0]
        z_ref = refs[1 + 2 * ng]
        outs = refs[2 + 2 * ng:2 + 2 * ng + 2 * ng + 1]
        scrs = refs[2 + 2 * ng + 2 * ng + 1:]
        do_out, dpr_out, dz_ref = outs[:ng], outs[ng:2 * ng], outs[2 * ng]
        o_get = _undilate(refs[1:1 + ng], scrs[:n_scr], tr)
        l_tiles = [g(0) for g in _undilate(refs[1 + ng:1 + 2 * ng], scrs[n_scr:2 * n_scr], tr)]
        stage = scrs[2 * n_scr:]
        do_stage, dpr_stage, k = [], [], 0
        for g in range(ng):
            if do_out[g].shape[0] == 1:
                do_stage.append(None)
                dpr_stage.append(None)
            else:
                do_stage.append(stage[2 * k])
                dpr_stage.append(stage[2 * k + 1])
                k += 1
        lane = lax.broadcasted_iota(jnp.int32, (tr, 128), 1)
        accs = [jnp.zeros((tr, 128), F32) for _ in range(ng)]
        for h in range(DIL_H):
            hs = slice(h * DIL_E, (h + 1) * DIL_E)
            ws = _merge_weights(l_tiles, h)
            ov = [o(h).astype(F32) for o in o_get]
            om = functools.reduce(lambda a, b: a + b, [w * o for w, o in zip(ws, ov)])
            zv = z_ref[:, hs].astype(F32)
            dgv = dg_ref[:, hs].astype(F32)
            dom = dgv * _silu(zv)
            dz_ref[:, hs] = (dgv * om * _dsilu(zv)).astype(BF16)
            dws = [jnp.sum(dom * o, axis=-1, keepdims=True) for o in ov]
            dwbar = functools.reduce(lambda a, b: a + b, [w * dw for w, dw in zip(ws, dws)])
            for g in range(ng):
                if do_stage[g] is None:
                    do_out[g][0, :, hs] = (ws[g] * dom).astype(BF16)
                else:
                    do_stage[g][h] = ws[g] * dom
                accs[g] = jnp.where(lane == h, ws[g] * dwbar, accs[g])
        for g in range(ng):
            d = do_out[g].shape[0]
            if d == 1:
                dpr_out[g][0] = accs[g]
                continue
            dpr_stage[g][0] = accs[g]
            for r in range(d):
                dpr_out[g][r] = dpr_stage[g].at[0][pl.ds(r, tr // d, stride=d), :]
                for c in range(HW // 128):
                    do_out[g][r, :, c * 128:(c + 1) * 128] = do_stage[g].at[c][pl.ds(r, tr // d, stride=d), :].astype(BF16)

    stage_shapes = []
    for o3 in os3:
        if o3.shape[0] > 1:
            stage_shapes += [pltpu.VMEM((HW // 128, tr, 128), F32), pltpu.VMEM((1, tr, 128), F32)]
    outs = pl.pallas_call(
        body, grid=(L // tr,),
        in_specs=_row_specs(tr, [HW]) + _dil_specs(tr, os3) + _dil_specs(tr, lses3) + _row_specs(tr, [HW]),
        out_specs=_dil_specs(tr, os3) + _dil_specs(tr, lses3) + _row_specs(tr, [HW]),
        out_shape=[jax.ShapeDtypeStruct(o.shape, BF16) for o in os3] + [jax.ShapeDtypeStruct(l.shape, F32) for l in lses3]
        + [jax.ShapeDtypeStruct((L, HW), BF16)],
        scratch_shapes=_dil_scratch(tr, os3) + _dil_scratch(tr, lses3) + stage_shapes,
        compiler_params=_cp(("parallel",)), name="merge_bwd")(dgated, *os3, *lses3, z)
    return outs[:ng], outs[ng:2 * ng], outs[2 * ng]


def ada_fwd(c8, ada_w):
    nl, D, Ws = ada_w.shape
    tn = _tile(Ws, 512)

    def body(c_ref, w_ref, o_ref):
        o_ref[...] = jnp.dot(_silu(c_ref[...]), w_ref[...], precision=lax.Precision.HIGHEST,
                             preferred_element_type=F32)

    return pl.pallas_call(
        body, grid=(nl, Ws // tn),
        in_specs=[pl.BlockSpec((N_DEV, D), lambda l, j: (0, 0)), pl.BlockSpec((None, D, tn), lambda l, j: (l, 0, j))],
        out_specs=pl.BlockSpec((None, N_DEV, tn), lambda l, j: (l, 0, j)),
        out_shape=jax.ShapeDtypeStruct((nl, N_DEV, Ws), F32), compiler_params=_cp(("parallel", "parallel")),
        name="ada_fwd")(c8, ada_w)


def ada_wgrad(c8t, dmod):
    nl, _, Ws = dmod.shape
    D = c8t.shape[0]
    tm = _tile(D, 512, 8)

    def body(c_ref, d_ref, o_ref):
        sc = _silu(c_ref[...])
        acc = sc[:, 0:1] * d_ref[0:1, :]
        for e in range(1, N_DEV):
            acc = acc + sc[:, e:e + 1] * d_ref[e:e + 1, :]
        o_ref[...] = acc

    return pl.pallas_call(
        body, grid=(nl, D // tm),
        in_specs=[pl.BlockSpec((tm, N_DEV), lambda l, i: (i, 0)), pl.BlockSpec((None, N_DEV, Ws), lambda l, i: (l, 0, 0))],
        out_specs=pl.BlockSpec((None, tm, Ws), lambda l, i: (l, i, 0)),
        out_shape=jax.ShapeDtypeStruct((nl, D, Ws), F32), compiler_params=_cp(("parallel", "parallel")),
        name="ada_wgrad")(c8t, dmod)


def adamw(w, g, m, v, name):
    R, C = w.shape
    tr = _tile(R, 256, 8)
    c1 = 1.0 - ADAM_B1 ** ADAM_STEP
    c2 = 1.0 - ADAM_B2 ** ADAM_STEP

    def body(w_ref, g_ref, m_ref, v_ref, d_ref, nm_ref, nv_ref):
        gv = g_ref[...]
        nm = ADAM_B1 * m_ref[...] + (1.0 - ADAM_B1) * gv
        nv = ADAM_B2 * v_ref[...] + (1.0 - ADAM_B2) * (gv * gv)
        nm_ref[...] = nm
        nv_ref[...] = nv
        d_ref[...] = -ADAM_LR * ((nm / c1) / (jnp.sqrt(nv / c2) + ADAM_EPS) + ADAM_WD * w_ref[...])

    return pl.pallas_call(
        body, grid=(R // tr,), in_specs=_row_specs(tr, [C] * 4), out_specs=_row_specs(tr, [C] * 3),
        out_shape=[jax.ShapeDtypeStruct((R, C), F32)] * 3, compiler_params=_cp(("parallel",)), name=name)(w, g, m, v)


def sum_leading(t, name, out_dtype=F32):
    S, R, C = t.shape
    tr = _tile(R, 256, 16)

    def body(t_ref, o_ref):
        acc = t_ref[0].astype(F32)
        for s in range(1, S):
            acc = acc + t_ref[s].astype(F32)
        o_ref[...] = acc.astype(out_dtype)

    return pl.pallas_call(
        body, grid=(R // tr,), in_specs=[pl.BlockSpec((S, tr, C), lambda i: (0, i, 0))],
        out_specs=pl.BlockSpec((tr, C), lambda i: (i, 0)), out_shape=jax.ShapeDtypeStruct((R, C), out_dtype),
        compiler_params=_cp(("parallel",)), name=name)(t)


def add_half(g, a, core, name, by_cols=False):
    S, R, C = g.shape

    def body(core_ref, g_ref, a_ref, o_ref):
        o_ref[...] = (g_ref[...].astype(F32) + a_ref[...].astype(F32)).astype(BF16)

    if by_cols:
        hc = C // 2
        tr = _tile(R, 512, 16)
        return pl.pallas_call(
            body,
            grid_spec=pltpu.PrefetchScalarGridSpec(
                num_scalar_prefetch=1, grid=(S, R // tr),
                in_specs=[pl.BlockSpec((None, tr, hc), lambda s, i, core_ref: (s, i, core_ref[0])),
                          pl.BlockSpec((None, tr, hc), lambda s, i, core_ref: (s, i, 0))],
                out_specs=pl.BlockSpec((None, tr, hc), lambda s, i, core_ref: (s, i, 0))),
            out_shape=jax.ShapeDtypeStruct((S, R, hc), BF16), compiler_params=_cp(("parallel", "parallel")),
            name=name)(core, g, a)
    h = R // 2
    tr = _tile(h, 256, 16)
    nb = h // tr

    return pl.pallas_call(
        body,
        grid_spec=pltpu.PrefetchScalarGridSpec(
            num_scalar_prefetch=1, grid=(S, nb),
            in_specs=[pl.BlockSpec((None, tr, C), lambda s, i, core_ref: (s, core_ref[0] * nb + i, 0)),
                      pl.BlockSpec((None, tr, C), lambda s, i, core_ref: (s, i, 0))],
            out_specs=pl.BlockSpec((None, tr, C), lambda s, i, core_ref: (s, i, 0))),
        out_shape=jax.ShapeDtypeStruct((S, h, C), BF16), compiler_params=_cp(("parallel", "parallel")),
        name=name)(core, g, a)


def sum_partials(own, landed, chip, name):
    _, h, C = own.shape
    tr = _tile(h, 512, 16)

    def body(chip_ref, own_ref, l_ref, o_ref):
        acc = own_ref[...].astype(F32)
        for j in range(3):
            acc = acc + l_ref[j].astype(F32)
        o_ref[...] = acc

    return pl.pallas_call(
        body,
        grid_spec=pltpu.PrefetchScalarGridSpec(
            num_scalar_prefetch=1, grid=(h // tr,),
            in_specs=[pl.BlockSpec((None, tr, C), lambda i, chip_ref: (chip_ref[0], i, 0)),
                      pl.BlockSpec((3, tr, C), lambda i, chip_ref: (0, i, 0))],
            out_specs=pl.BlockSpec((tr, C), lambda i, chip_ref: (i, 0))),
        out_shape=jax.ShapeDtypeStruct((h, C), F32), compiler_params=_cp(("parallel",)), name=name)(chip, own, landed)


def adamw_halves(w, g_mine, g_theirs, m, v, core, name):
    R, C = w.shape
    h = R // 2
    tr = _tile(h, 256, 8)
    nbh = h // tr
    c1 = 1.0 - ADAM_B1 ** ADAM_STEP
    c2 = 1.0 - ADAM_B2 ** ADAM_STEP

    def body(core_ref, w_ref, gm_ref, gt_ref, m_ref, v_ref, g_ref, d_ref, nm_ref, nv_ref):
        mine = (pl.program_id(0) // nbh) == core_ref[0]
        gv = jnp.where(mine, gm_ref[...], gt_ref[...])
        g_ref[...] = gv
        nm = ADAM_B1 * m_ref[...] + (1.0 - ADAM_B1) * gv
        nv = ADAM_B2 * v_ref[...] + (1.0 - ADAM_B2) * (gv * gv)
        nm_ref[...] = nm
        nv_ref[...] = nv
        d_ref[...] = -ADAM_LR * ((nm / c1) / (jnp.sqrt(nv / c2) + ADAM_EPS) + ADAM_WD * w_ref[...])

    full = pl.BlockSpec((tr, C), lambda i, core_ref: (i, 0))
    halfspec = pl.BlockSpec((tr, C), lambda i, core_ref: (i % nbh, 0))
    return pl.pallas_call(
        body,
        grid_spec=pltpu.PrefetchScalarGridSpec(
            num_scalar_prefetch=1, grid=(2 * nbh,), in_specs=[full, halfspec, halfspec, full, full],
            out_specs=[full] * 4),
        out_shape=[jax.ShapeDtypeStruct((R, C), F32)] * 4, compiler_params=_cp(("parallel",)),
        name=name)(core, w, g_mine, g_theirs, m, v)


_ANY = pl.BlockSpec(memory_space=pl.ANY)


def _place():
    x, y, c = lax.axis_index("x"), lax.axis_index("y"), lax.axis_index("c")
    chips = [(1 - x, y), (x, 1 - y), (1 - x, 1 - y)]
    return x, y, c, chips


def allgather_small(v, name, after=None):
    R, W = v.shape
    extra = [] if after is None else [after]

    def body(x_ref, *rest):
        out_ref, send_sems, recv_sems, local_sem = rest[len(extra):]
        x, y, c, chips = _place()
        me, sibling = (x, y, c), (x, y, 1 - c)

        def rows(px, py, pc):
            return out_ref.at[pl.ds((4 * px + 2 * py + pc) * R, R), :]

        def copy(k, block, to, src=None):
            return pltpu.make_async_remote_copy(
                src_ref=rows(*block) if src is None else src, dst_ref=rows(*block),
                send_sem=send_sems.at[k], recv_sem=recv_sems.at[k], device_id=to, device_id_type=MESH)

        mine = pltpu.make_async_copy(x_ref, rows(*me), local_sem)
        mine.start()
        first = [copy(0, me, sibling, src=x_ref)]
        first += [copy(1 + j, me, (*chip, c), src=x_ref) for j, chip in enumerate(chips)]
        for cp in first:
            cp.start()
        passed = [copy(4 + j, (*chip, c), sibling) for j, chip in enumerate(chips)]
        for j, chip in enumerate(chips):
            copy(1 + j, (*chip, c), me).wait_recv()
            passed[j].start()
        copy(0, sibling, me).wait_recv()
        for j, chip in enumerate(chips):
            copy(4 + j, (*chip, 1 - c), me).wait_recv()
        for cp in first + passed:
            cp.wait_send()
        mine.wait()

    return pl.pallas_call(
        body, out_shape=jax.ShapeDtypeStruct((N_DEV * R, W), v.dtype),
        in_specs=[pl.BlockSpec(memory_space=pltpu.VMEM)] + [_ANY] * len(extra),
        out_specs=pl.BlockSpec(memory_space=pltpu.VMEM),
        scratch_shapes=[pltpu.SemaphoreType.DMA((7,)), pltpu.SemaphoreType.DMA((7,)), pltpu.SemaphoreType.DMA],
        name=name)(v, *extra)


def allgather_routed(shard, name):
    R, C = shard.shape
    hc = C // 2
    ra = (R // 2) // 16 * 16

    def body(in_ref, out_ref, send_sems, recv_sems):
        x, y, c, _ = _place()
        xn, yn = (1 - x, y, c), (x, 1 - y, c)
        sibling = (x, y, 1 - c)
        p, pxn, pyn, pdg = 2 * x + y, 2 * (1 - x) + y, 2 * x + (1 - y), 2 * (1 - x) + (1 - y)
        rows_a, rows_b, rows_all = pl.ds(0, ra), pl.ds(ra, R - ra), pl.ds(0, R)

        def win(ref, rows, core):
            return ref.at[rows, pl.ds(pl.multiple_of(core * hc, 128), hc)]

        def copy(k, chip_id, rows, core, to, src=None):
            blk = win(out_ref.at[chip_id], rows, core)
            return pltpu.make_async_remote_copy(
                src_ref=blk if src is None else src, dst_ref=blk, send_sem=send_sems.at[k], recv_sem=recv_sems.at[k],
                device_id=to, device_id_type=MESH)

        own = [copy(0, p, rows_a, c, xn, src=win(in_ref, rows_a, c)), copy(1, p, rows_b, c, xn, src=win(in_ref, rows_b, c)),
               copy(2, p, rows_b, c, yn, src=win(in_ref, rows_b, c)), copy(3, p, rows_a, c, yn, src=win(in_ref, rows_a, c))]
        for cp in own:
            cp.start()
        copy(0, pxn, rows_a, c, xn).wait_recv()
        fwd_a = copy(4, pxn, rows_a, c, yn)
        fwd_a.start()
        copy(2, pyn, rows_b, c, yn).wait_recv()
        fwd_b = copy(5, pyn, rows_b, c, xn)
        fwd_b.start()
        copy(1, pxn, rows_b, c, xn).wait_recv()
        copy(3, pyn, rows_a, c, yn).wait_recv()
        passed = [copy(6, pxn, rows_all, c, sibling), copy(7, pyn, rows_all, c, sibling)]
        for cp in passed:
            cp.start()
        copy(4, pdg, rows_a, c, yn).wait_recv()
        passed.append(copy(8, pdg, rows_a, c, sibling))
        passed[-1].start()
        copy(5, pdg, rows_b, c, xn).wait_recv()
        passed.append(copy(9, pdg, rows_b, c, sibling))
        passed[-1].start()
        for k, (chip_id, rows) in enumerate([(pxn, rows_all), (pyn, rows_all), (pdg, rows_a), (pdg, rows_b)]):
            copy(6 + k, chip_id, rows, 1 - c, sibling).wait_recv()
        for cp in own + [fwd_a, fwd_b] + passed:
            cp.wait_send()

    out = pl.pallas_call(
        body, out_shape=jax.ShapeDtypeStruct((N_CHIPS, R, C), shard.dtype), in_specs=[_ANY], out_specs=_ANY,
        scratch_shapes=[pltpu.SemaphoreType.DMA((10,)), pltpu.SemaphoreType.DMA((10,))], name=name)(shard)
    chip = 2 * lax.axis_index("x") + lax.axis_index("y")
    return lax.dynamic_update_index_in_dim(out, shard, chip, 0)


_HBM = pl.BlockSpec(memory_space=pltpu.HBM)
_SEM = pl.BlockSpec(memory_space=pltpu.SEMAPHORE)
_EFFECT = pltpu.SideEffectType.DATAFLOW_SIDE_EFFECTING


def _chip_copies(kind, srcs, lands, send_sems, recv_sems):
    x, y, c, chips = _place()
    p = 2 * x + y
    cps = []
    if kind == "sibling":
        for i in range(len(srcs)):
            h = srcs[i].shape[1] // 2
            cps.append(pltpu.make_async_remote_copy(
                src_ref=srcs[i].at[:, pl.ds((1 - c) * h, h), :], dst_ref=lands[i], send_sem=send_sems.at[3 * i],
                recv_sem=recv_sems.at[3 * i], device_id=(x, y, 1 - c), device_id_type=MESH))
        return cps
    for i in range(len(srcs)):
        for j, (cx, cy) in enumerate(chips):
            if kind == "gather":
                src, dst = srcs[i].at[c], lands[i].at[p, c]
            else:
                src, dst = srcs[i].at[2 * cx + cy], lands[i].at[j]
            cps.append(pltpu.make_async_remote_copy(
                src_ref=src, dst_ref=dst, send_sem=send_sems.at[3 * i + j], recv_sem=recv_sems.at[3 * i + j],
                device_id=(cx, cy, c), device_id_type=MESH))
    return cps


def split_start(kind, srcs, land_shapes, after, name):
    n = len(srcs)

    def body(*refs):
        src_refs, land_refs = refs[:n], refs[n:2 * n]
        send_sems, recv_sems = refs[2 * n + 1], refs[2 * n + 2]
        token = refs[-1]
        for cp in _chip_copies(kind, src_refs, land_refs, send_sems, recv_sems):
            cp.start()
        token[...] = jnp.zeros_like(token)

    lands = [pltpu.with_memory_space_constraint(lax.empty(s, BF16), pltpu.HBM) for s in land_shapes]
    outs = pl.pallas_call(
        body, name=name,
        out_shape=(pltpu.SemaphoreType.DMA((3 * n,)), pltpu.SemaphoreType.DMA((3 * n,)),
                   *[pltpu.HBM(s.shape, s.dtype) for s in srcs], *[pltpu.HBM(s, BF16) for s in land_shapes],
                   jax.ShapeDtypeStruct((8, 128), F32)),
        in_specs=[_HBM] * (2 * n) + [_ANY],
        out_specs=(_SEM, _SEM, *([_HBM] * (2 * n)), pl.BlockSpec(memory_space=pltpu.VMEM)),
        input_output_aliases={i: 2 + i for i in range(2 * n)},
        compiler_params=pltpu.CompilerParams(has_side_effects=_EFFECT),
    )(*[pltpu.with_memory_space_constraint(s, pltpu.HBM) for s in srcs], *lands, after)
    return outs[0], outs[1], outs[2:2 + n], outs[2 + n:2 + 2 * n], outs[-1]


def split_wait(kind, send_sems, recv_sems, srcs, lands, after, name):
    n = len(srcs)

    def body(*refs):
        src_refs, land_refs = refs[:n], refs[n:2 * n]
        ssem, rsem = refs[2 * n], refs[2 * n + 1]
        for cp in _chip_copies(kind, src_refs, land_refs, ssem, rsem):
            cp.wait_send()
            cp.wait_recv()

    outs = pl.pallas_call(
        body, name=name,
        out_shape=[pltpu.HBM(s.shape, s.dtype) for s in srcs] + [pltpu.HBM(s.shape, s.dtype) for s in lands],
        in_specs=[_HBM] * (2 * n) + [_SEM, _SEM, _ANY], out_specs=[_HBM] * (2 * n),
        input_output_aliases={i: i for i in range(2 * n)},
        compiler_params=pltpu.CompilerParams(has_side_effects=_EFFECT),
    )(*srcs, *lands, send_sems, recv_sems, after)
    return outs[:n], outs[n:]


def pass_to_sibling(lands):
    n = len(lands)

    def body(*refs):
        ins, outs = refs[:n], refs[n:2 * n]
        send_sems, recv_sems = refs[2 * n:]
        x, y, c, chips = _place()
        cps = []
        for i in range(n):
            for j, (cx, cy) in enumerate(chips):
                blk = outs[i].at[2 * cx + cy, c]
                cps.append(pltpu.make_async_remote_copy(
                    src_ref=ins[i].at[2 * cx + cy, c], dst_ref=blk, send_sem=send_sems.at[3 * i + j],
                    recv_sem=recv_sems.at[3 * i + j], device_id=(x, y, 1 - c), device_id_type=MESH))
        for cp in cps:
            cp.start()
        for cp in cps:
            cp.wait()

    return pl.pallas_call(
        body, out_shape=[jax.ShapeDtypeStruct(t.shape, t.dtype) for t in lands], in_specs=[_ANY] * n,
        out_specs=[_ANY] * n, input_output_aliases={i: i for i in range(n)},
        scratch_shapes=[pltpu.SemaphoreType.DMA((3 * n,)), pltpu.SemaphoreType.DMA((3 * n,))],
        name="ag_pass_to_sibling")(*lands)


def _pass_copies(bufs, send_sems, recv_sems):
    x, y, c, chips = _place()
    cps = []
    for i in range(len(bufs)):
        for j, (cx, cy) in enumerate(chips):
            blk = bufs[i].at[2 * cx + cy, c]
            cps.append(pltpu.make_async_remote_copy(
                src_ref=blk, dst_ref=blk, send_sem=send_sems.at[3 * i + j], recv_sem=recv_sems.at[3 * i + j],
                device_id=(x, y, 1 - c), device_id_type=MESH))
    return cps


def pass_start(bufs, after, name):
    n = len(bufs)

    def body(*refs):
        send_sems, recv_sems = refs[n + 1], refs[n + 2]
        for cp in _pass_copies(refs[:n], send_sems, recv_sems):
            cp.start()
        refs[-1][...] = jnp.zeros_like(refs[-1])

    outs = pl.pallas_call(
        body, name=name,
        out_shape=(pltpu.SemaphoreType.DMA((3 * n,)), pltpu.SemaphoreType.DMA((3 * n,)),
                   *[pltpu.HBM(b.shape, b.dtype) for b in bufs], jax.ShapeDtypeStruct((8, 128), F32)),
        in_specs=[_HBM] * n + [_ANY],
        out_specs=(_SEM, _SEM, *([_HBM] * n), pl.BlockSpec(memory_space=pltpu.VMEM)),
        input_output_aliases={i: 2 + i for i in range(n)},
        compiler_params=pltpu.CompilerParams(has_side_effects=_EFFECT),
    )(*[pltpu.with_memory_space_constraint(b, pltpu.HBM) for b in bufs], after)
    return outs[0], outs[1], outs[2:2 + n], outs[-1]


def pass_wait(send_sems, recv_sems, bufs, after, name):
    n = len(bufs)

    def body(*refs):
        for cp in _pass_copies(refs[:n], refs[n], refs[n + 1]):
            cp.wait_send()
            cp.wait_recv()

    return pl.pallas_call(
        body, name=name, out_shape=[pltpu.HBM(b.shape, b.dtype) for b in bufs],
        in_specs=[_HBM] * n + [_SEM, _SEM, _ANY], out_specs=[_HBM] * n,
        input_output_aliases={i: i for i in range(n)},
        compiler_params=pltpu.CompilerParams(has_side_effects=_EFFECT),
    )(*bufs, send_sems, recv_sems, after)


def exchange_halves_to_sibling(gs, name, by_cols=False):
    n = len(gs)

    def body(*refs):
        ins, outs = refs[:n], refs[n:2 * n]
        send_sems, recv_sems = refs[2 * n:]
        x, y, c, _ = _place()
        cps = []
        for i in range(n):
            if by_cols:
                hc = ins[i].shape[2] // 2
                src = ins[i].at[:, :, pl.ds(pl.multiple_of((1 - c) * hc, 128), hc)]
            else:
                h = ins[i].shape[1] // 2
                src = ins[i].at[:, pl.ds((1 - c) * h, h), :]
            cps.append(pltpu.make_async_remote_copy(
                src_ref=src, dst_ref=outs[i],
                send_sem=send_sems.at[i], recv_sem=recv_sems.at[i], device_id=(x, y, 1 - c), device_id_type=MESH))
        for cp in cps:
            cp.start()
        for cp in cps:
            cp.wait()

    halve = (lambda s: (s[0], s[1], s[2] // 2)) if by_cols else (lambda s: (s[0], s[1] // 2, s[2]))
    return pl.pallas_call(
        body, out_shape=[jax.ShapeDtypeStruct(halve(g.shape), g.dtype) for g in gs],
        in_specs=[_ANY] * n, out_specs=[_ANY] * n,
        scratch_shapes=[pltpu.SemaphoreType.DMA((n,)), pltpu.SemaphoreType.DMA((n,))],
        name=name)(*gs)


def join_halves(rs, name):
    n = len(rs)

    def body(*refs):
        ins, outs = refs[:n], refs[n:2 * n]
        send_sems, recv_sems = refs[2 * n:]
        x, y, c, _ = _place()
        cps = [pltpu.make_async_remote_copy(
            src_ref=ins[i], dst_ref=outs[i], send_sem=send_sems.at[i], recv_sem=recv_sems.at[i],
            device_id=(x, y, 1 - c), device_id_type=MESH) for i in range(n)]
        for cp in cps:
            cp.start()
        for cp in cps:
            cp.wait()

    return pl.pallas_call(
        body, out_shape=[jax.ShapeDtypeStruct(r.shape, r.dtype) for r in rs],
        in_specs=[_ANY] * n, out_specs=[_ANY] * n,
        scratch_shapes=[pltpu.SemaphoreType.DMA((n,)), pltpu.SemaphoreType.DMA((n,))],
        name=name)(*rs)


def _pack(parts, row_mult=8):
    flat = jnp.concatenate([p.reshape(-1).astype(F32) for p in parts])
    unit = row_mult * 128
    n = -(-flat.shape[0] // unit) * unit
    return jnp.pad(flat, (0, n - flat.shape[0])).reshape(n // 128, 128)


def _unpack(flat, shapes):
    out, off = [], 0
    for s in shapes:
        n = int(np.prod(s))
        out.append(flat[off:off + n].reshape(s))
        off += n
    return out


def _gather_packed(parts, name):
    packed = _pack(parts)
    g = allgather_small(packed, name).reshape(N_DEV, -1)
    return _unpack_rows(g, [p.shape for p in parts])


def _unpack_rows(g, shapes):
    out, off = [], 0
    for s in shapes:
        n = int(np.prod(s))
        out.append(g[:, off:off + n].reshape((g.shape[0],) + tuple(s)))
        off += n
    return out


def _by_chip(t, axis):
    return jnp.concatenate([t[2 * p] for p in range(N_CHIPS)], axis=axis)


def kernel(x, c, ada_w, ada_b, ln_g, ln_b, a_in_w, a_conv_w, a_conv_b, a_dt_bias, a_A_log, a_D, a_norm_g, a_out_w, kv_w, b_in_w, b_out_w, loss_target, m_ada_w, m_ada_b, m_ln_g, m_ln_b, m_a_in_w, m_a_conv_w, m_a_conv_b, m_a_dt_bias, m_a_A_log, m_a_D, m_a_norm_g, m_a_out_w, m_kv_w, m_b_in_w, m_b_out_w, v_ada_w, v_ada_b, v_ln_g, v_ln_b, v_a_in_w, v_a_conv_w, v_a_conv_b, v_a_dt_bias, v_a_A_log, v_a_D, v_a_norm_g, v_a_out_w, v_kv_w, v_b_in_w, v_b_out_w):
    ax, ay, ac = lax.axis_index("x"), lax.axis_index("y"), lax.axis_index("c")
    chip = 2 * ax + ay
    dev = 4 * ax + 2 * ay + ac
    xin = x[0]
    tgt = loss_target[0]
    L, D = xin.shape
    G, P = SSD_G, SSD_P
    H = a_dt_bias.shape[1]
    Kh = H // G
    DI = H * P
    CONVD = a_conv_b.shape[1] * N_CHIPS
    HW = DIL_H * DIL_E
    Ws = ada_w.shape[2]

    w_in_g = allgather_routed(jnp.transpose(a_in_w[0]).astype(BF16), "allgather_w_in")
    later = [a_out_w[0].astype(BF16), kv_w.astype(BF16), b_in_w[0].astype(BF16), b_out_w[0].astype(BF16)]
    later_split = [s.reshape(2, s.shape[0] // 2, s.shape[1]) for s in later]
    ag_ssem, ag_rsem, ag_srcs, ag_lands, ag_token = split_start(
        "gather", later_split, [(N_CHIPS,) + s.shape for s in later_split], w_in_g, "ag_later_start")
    w_in_t = w_in_g.reshape(-1, D)
    w_dt_t = jnp.pad(w_in_t[DI + CONVD:], ((0, 128 - H), (0, 0)))

    c8, cw8, cb8, ng8 = _gather_packed([c[0], a_conv_w[0], a_conv_b[0], a_norm_g[0]], "allgather_small_params")
    conv_w = _by_chip(cw8, 1)
    conv_b = _by_chip(cb8, 0).reshape(1, CONVD)
    norm_g = _by_chip(ng8, 0).reshape(1, DI)

    mod_s = ada_fwd(c8, ada_w)
    (mod8,) = _gather_packed([mod_s], "allgather_small_mod")
    mods = _by_chip(mod8, 2)
    mod = lax.dynamic_index_in_dim(mods, dev, axis=1, keepdims=False) + ada_b
    shift = [mod[l:l + 1, :D] for l in range(DEPTH)]
    scale = [mod[l:l + 1, D:2 * D] for l in range(DEPTH)]
    gate = [mod[l:l + 1, 2 * D:] for l in range(DEPTH)]
    lg = [ln_g[l:l + 1] for l in range(DEPTH)]
    lb = [ln_b[l:l + 1] for l in range(DEPTH)]

    h0 = modulate(xin, scale[0] + ag_token[0:1, 0:1], shift[0], "modulate0")
    zx = mm_nt(h0, w_in_t, BF16, "mm_in_zx", kw_rows=DI + CONVD)
    dtp = mm_nt(h0, w_dt_t, F32, "mm_in_dt")
    xbc = conv_fwd(zx, DI, conv_w, conv_b)
    dtp_g = jnp.transpose(dtp[:, :H].reshape(L, G, Kh), (1, 0, 2))
    dtp_gT = jnp.transpose(dtp_g, (0, 2, 1))
    vecs = [a_dt_bias.reshape(G, 1, Kh), a_dt_bias.reshape(G, Kh, 1), a_A_log.reshape(G, 1, Kh),
            a_A_log.reshape(G, Kh, 1), a_D.reshape(G, 1, Kh), a_D.reshape(G, Kh, 1)]
    y_ssd, states, yn = ssd_fwd(xbc, dtp_g, dtp_gT, *vecs, zx, norm_g, DI)
    later_split, ag_lands = split_wait("gather", ag_ssem, ag_rsem, ag_srcs, ag_lands, yn, "ag_later_wait")
    (land_out,) = pass_to_sibling(ag_lands[:1])
    ps_ssem, ps_rsem, lands_b, ps_token = pass_start(ag_lands[1:], land_out, "ag_pass_start")

    def place_own(o, s, full):
        return lax.dynamic_update_index_in_dim(o, s, chip, 0).reshape((N_CHIPS,) + full.shape)

    w_out_g = place_own(land_out, later_split[0], later[0])
    ymix0 = mm_nn(yn, w_out_g.reshape(-1, D), F32, "mm_out_a", after=ps_token)
    x1, x1b, h1 = ln_mid(xin, ymix0, gate[0], lg[0], lb[0], scale[1], shift[1])
    lands_b = pass_wait(ps_ssem, ps_rsem, lands_b, x1b, "ag_pass_wait")
    w_kv_g, w_bin_g, w_bout_g = [place_own(o, s, full) for o, s, full in zip(lands_b, later_split[1:], later[1:])]

    n_grp = len(DIL_PATTERNS)
    cb = HW // 512
    assert w_bin_g.shape[2] == HW
    kv3 = [mm_cols_dilated(x1b, w_kv_g, [g * cb + t for t in range(cb)] + [(n_grp + g) * cb + t for t in range(cb)],
                           DIL_PATTERNS[g][1], f"mm_kv_{g}") for g in range(n_grp)]
    q3 = [mm_cols_dilated(h1, w_bin_g, [g], DIL_PATTERNS[g][1], f"mm_q_{g}", tn=HW) for g in range(n_grp)]
    z_b = mm_nn(h1, w_bin_g[n_grp], BF16, "mm_z_b")
    os_, lses = [], []
    for gi in range(len(DIL_PATTERNS)):
        o, lse = attn_fwd(q3[gi], kv3[gi], gi)
        os_.append(o)
        lses.append(lse)
    om = merge_fwd(os_, lses, z_b)
    ymix1 = mm_nn(om, w_bout_g, F32, "mm_out_b", stack="col")
    dres2, dy2, dg1, db1, dgate1, sq = ln_final_fwd_bwd(x1, ymix1, gate[1], lg[1], lb[1], tgt)
    loss_part = 0.5 * jnp.sum(sq) / D

    g_bout = mm_tn(om, dy2, BF16, "mm_gw_out_b", stack="col")
    dgated = mm_nt(dy2, w_bout_g, BF16, "mm_gx_out_b", stack="col")
    dos, dprs, dz_b = merge_bwd(dgated, os_, lses, z_b)
    dqs, dks, dvs = [], [], []
    for gi in range(len(DIL_PATTERNS)):
        dq, dk, dv = attn_bwd(q3[gi], kv3[gi], dos[gi], lses[gi], dprs[gi], gi)
        dqs.append(dq)
        dks.append(dk)
        dvs.append(dv)
    dqz = jnp.concatenate(dqs + [dz_b], axis=1)
    dkv = jnp.concatenate(dks + dvs, axis=1)
    g_bin = mm_tn(h1, dqz, BF16, "mm_gw_in_b", stack="col")
    dh1 = mm_nt(dqz, w_bin_g, BF16, "mm_gx_in_b", stack="col")
    g_kv = mm_tn(x1b, dkv, BF16, "mm_gw_kv", stack="col")

    core = ac.astype(jnp.int32).reshape(1)
    chip_i = chip.astype(jnp.int32).reshape(1)

    def begin_exchange(gs, tag):
        shapes = [(g.shape[0], g.shape[1] // 2, g.shape[2]) for g in gs]
        return split_start("sibling", gs, shapes, gs[0], "rs_x%s_start" % tag)

    def begin_scatter(gs, nms, tag, exchange=None, after=None, by_cols=False):
        if exchange is None:
            sib = exchange_halves_to_sibling(gs, "rs_sibling_exchange_" + tag, by_cols=by_cols)
        else:
            gs, sib = split_wait("sibling", exchange[0], exchange[1], exchange[2], exchange[3], after,
                                 "rs_x%s_wait" % tag)
        parts = [add_half(g, a, core, "rs_add_" + nm, by_cols=by_cols) for g, a, nm in zip(gs, sib, nms)]
        return split_start("scatter", parts, [(3,) + t.shape[1:] for t in parts], parts[0], "rs_%s_start" % tag)

    def finish_scatter(handles, after, tag):
        nms, owns, landed = [], [], []
        for k, (handle, hn) in enumerate(handles):
            parts, lands = split_wait("scatter", handle[0], handle[1], handle[2], handle[3], after,
                                      "rs_%s%d_wait" % (tag, k))
            nms += hn
            owns += list(parts)
            landed += list(lands)
        halves = [sum_partials(own, t, chip_i, "rs_sum_" + nm) for own, t, nm in zip(owns, landed, nms)]
        theirs = join_halves(halves, "rs_join_halves_" + tag)
        return dict(zip(nms, zip(halves, theirs)))

    names_b = ["kv", "in_b", "out_b"]
    ex_b = begin_exchange([g_kv, g_bin, g_bout], "b")
    dx1_kv = mm_nt(dkv, w_kv_g, BF16, "mm_gx_kv", stack="col", after=ex_b[4])
    rs_b = begin_scatter(None, names_b, "b", exchange=ex_b, after=dx1_kv)

    dres1, dy1, dg0, db0, dgate0, dscale1, dshift1 = mod_ln_bwd(
        dres2, dh1, dx1_kv, x1, scale[1], xin, ymix0, gate[0] + rs_b[4][0:1, 0:1], lg[0])
    g_out = mm_tn(yn, dy1, BF16, "mm_gw_out_a", stack="row")
    ex_a1 = begin_exchange([g_out], "a1")
    dyn = mm_nt(dy1, w_out_g, BF16, "mm_gx_out_a", stack="row", after=ex_a1[4])
    rs_a1 = begin_scatter(None, ["out_a"], "a1", exchange=ex_a1, after=dyn)
    dxs, dB, dC, ddtp_g, dbias_g, dalog_g, dD_g, dz_a, dnorm_g = ssd_bwd(
        xbc, dtp_g, dtp_gT, *vecs, states, dyn, y_ssd, zx, norm_g + rs_a1[4][0:1, 0:1], DI)
    dzx, dws, dbs, lo = dz_a, [], [], 0
    for tag, gpart in (("xs", dxs), ("b", dB), ("c", dC)):
        hi = lo + gpart.shape[1]
        dzx, dw_p, db_p = conv_bwd(zx, DI + lo, conv_w[:, lo:hi], conv_b[:, lo:hi], gpart, dzx, "conv_bwd_" + tag)
        dws.append(dw_p)
        dbs.append(db_p)
        lo = hi
    dconv_w = jnp.concatenate(dws, axis=1)
    dconv_b = jnp.concatenate(dbs, axis=1)
    ddtp = jnp.pad(jnp.transpose(ddtp_g, (1, 0, 2)).reshape(L, H), ((0, 0), (0, 128 - H)))
    g_inT = mm_tn(dzx, h0, BF16, "mm_gw_in_zx", m_rows=DI + CONVD + H)
    g_dtT = mm_tn(ddtp, h0, BF16, "mm_gw_in_dt")
    g_inT = lax.dynamic_update_slice(g_inT, g_dtT[:H], (DI + CONVD, 0))
    rs_a2 = begin_scatter([g_inT.reshape(N_CHIPS, -1, D)], ["in_a"], "a2", by_cols=True)
    dh0 = mm_nn(dzx, w_in_t, BF16, "mm_gx_in_zx", after=rs_a2[4])
    dh0_dt = mm_nn(ddtp, w_dt_t, F32, "mm_gx_in_dt")
    grad_x, dscale0, dshift0 = mod_bwd(dres1, dh0, dh0_dt, xin, scale[0] + rs_a2[4][0:1, 0:1], "mod_bwd0")
    g_halves = finish_scatter([(rs_b, names_b)], grad_x, "b")

    def step_halves(w, m, v, nm):
        shp = w.shape
        mine, theirs_ = g_halves[nm]
        outs4 = adamw_halves(w.reshape(-1, shp[-1]), mine, theirs_, m.reshape(-1, shp[-1]), v.reshape(-1, shp[-1]),
                             core, "adamw_" + nm)
        return tuple(t.reshape(shp) for t in outs4)

    big = {
        "kv_w": step_halves(kv_w, m_kv_w, v_kv_w, "kv"),
        "b_in_w": step_halves(b_in_w, m_b_in_w, v_b_in_w, "in_b"),
        "b_out_w": step_halves(b_out_w, m_b_out_w, v_b_out_w, "out_b"),
    }
    g_halves.update(finish_scatter([(rs_a1, ["out_a"]), (rs_a2, ["in_a"])], big["kv_w"][1], "a"))
    g_halves["in_a"] = tuple(jnp.transpose(t) for t in g_halves["in_a"])
    big["a_in_w"] = step_halves(a_in_w, m_a_in_w, v_a_in_w, "in_a")
    big["a_out_w"] = step_halves(a_out_w, m_a_out_w, v_a_out_w, "out_a")

    dmod = jnp.concatenate([jnp.concatenate([dshift0, dscale0, dgate0], axis=1),
                            jnp.concatenate([dshift1, dscale1, dgate1], axis=1)], axis=0)
    small_parts = [jnp.concatenate([dg0, dg1], axis=0), jnp.concatenate([db0, db1], axis=0),
                   dbias_g.reshape(1, H), dalog_g.reshape(1, H), dD_g.reshape(1, H),
                   dconv_w, dconv_b, dnorm_g, loss_part.reshape(1, 1)]
    small_shapes = [p.shape for p in small_parts]
    packed = jnp.concatenate([_pack([dmod]), _pack(small_parts)], axis=0)
    n_mod_rows = _pack([dmod]).shape[0]
    gathered = allgather_small(packed, "allgather_small_grads", after=g_halves["in_a"][1]).reshape(N_DEV, -1, 128)
    dmod8 = gathered[:, :n_mod_rows].reshape(N_DEV, -1)[:, :2 * 3 * D].reshape(N_DEV, DEPTH, 3 * D)
    summed = sum_leading(gathered, "sum_small")
    g_ada_b = summed[:n_mod_rows].reshape(-1)[:2 * 3 * D].reshape(DEPTH, 3 * D)
    (g_ln_g, g_ln_b, g_dt_bias, g_a_log, g_dsk, g_conv_w, g_conv_b, g_norm_g, loss_all) = _unpack(
        summed[n_mod_rows:].reshape(-1), small_shapes)
    loss = loss_all.reshape(())
    Cs = CONVD // N_CHIPS
    g_conv_w_s = lax.dynamic_slice_in_dim(g_conv_w, chip * Cs, Cs, axis=1)
    g_conv_b_s = lax.dynamic_slice_in_dim(g_conv_b, chip * Cs, Cs, axis=1)
    g_norm_g_s = lax.dynamic_slice_in_dim(g_norm_g, chip * (DI // N_CHIPS), DI // N_CHIPS, axis=1)
    dmod_s = jnp.transpose(lax.dynamic_slice_in_dim(dmod8, chip * Ws, Ws, axis=2), (1, 0, 2))

    def step2d(w, g, m, v, nm):
        shp = w.shape
        d_, m_, v_ = adamw(w.reshape(-1, shp[-1]), g.reshape(-1, shp[-1]), m.reshape(-1, shp[-1]),
                           v.reshape(-1, shp[-1]), "adamw_" + nm)
        return g.reshape(shp), d_.reshape(shp), m_.reshape(shp), v_.reshape(shp)

    big["ada_w"] = step2d(ada_w, ada_wgrad(jnp.transpose(c8), dmod_s), m_ada_w, v_ada_w, "ada_w")
    small_names = ["ada_b", "ln_g", "ln_b", "a_conv_w", "a_conv_b", "a_dt_bias", "a_A_log", "a_D", "a_norm_g"]
    small_w = [ada_b, ln_g, ln_b, a_conv_w, a_conv_b, a_dt_bias, a_A_log, a_D, a_norm_g]
    small_m = [m_ada_b, m_ln_g, m_ln_b, m_a_conv_w, m_a_conv_b, m_a_dt_bias, m_a_A_log, m_a_D, m_a_norm_g]
    small_v = [v_ada_b, v_ln_g, v_ln_b, v_a_conv_w, v_a_conv_b, v_a_dt_bias, v_a_A_log, v_a_D, v_a_norm_g]
    small_g = [g_ada_b, g_ln_g, g_ln_b, g_conv_w_s, g_conv_b_s, g_dt_bias, g_a_log, g_dsk, g_norm_g_s]
    shapes = [w.shape for w in small_w]
    small_g = [g.reshape(s) for g, s in zip(small_g, shapes)]
    d_p, m_p, v_p = adamw(_pack(small_w), _pack(small_g), _pack(small_m), _pack(small_v), "adamw_small")
    small = {}
    for nm, g, d_, m_, v_ in zip(small_names, small_g, _unpack(d_p.reshape(-1), shapes), _unpack(m_p.reshape(-1), shapes),
                                 _unpack(v_p.reshape(-1), shapes)):
        small[nm] = (g, d_, m_, v_)
    allw = {**big, **small}
    order = ["ada_w", "ada_b", "ln_g", "ln_b", "a_in_w", "a_conv_w", "a_conv_b", "a_dt_bias", "a_A_log", "a_D",
             "a_norm_g", "a_out_w", "kv_w", "b_in_w", "b_out_w"]
    outs = [loss, grad_x.reshape(x.shape)]
    for k in range(4):
        outs += [allw[n][k] for n in order]
    return tuple(outs)
```

```python
import functools

import jax
import jax.numpy as jnp
import numpy as np
from jax import lax
from jax.experimental import pallas as pl
from jax.experimental.pallas import tpu as pltpu

F32 = jnp.float32
BF16 = jnp.bfloat16
MESH = pl.DeviceIdType.MESH

DEPTH = 2
ALPHA = (2 * DEPTH) ** 0.25
LN_EPS = 1e-5
RMS_EPS = 1e-5
SSD_P = 64
SSD_N = 128
SSD_Q = 256
SSD_G = 8
CONV_W = 4
DIL_PATTERNS = ((128, 1), (512, 4), (2048, 16))
DIL_H = 8
DIL_E = 128
DIL_BLK = 128
ADAM_LR, ADAM_B1, ADAM_B2, ADAM_EPS, ADAM_WD, ADAM_STEP = 0.001, 0.9, 0.999, 1e-08, 0.01, 10

VMEM_LIMIT = 56 * 1024 * 1024
N_CHIPS = 4
N_DEV = 8


def _tile(dim, target, mult=128):
    if dim <= target:
        return dim
    t = (target // mult) * mult
    while t >= mult:
        if dim % t == 0:
            return t
        t -= mult
    return dim


def _cp(sem):
    return pltpu.CompilerParams(dimension_semantics=sem, vmem_limit_bytes=VMEM_LIMIT)


def _sigmoid(x):
    return 1.0 / (1.0 + jnp.exp(-x))


def _silu(x):
    return x * _sigmoid(x)


def _dsilu(x):
    s = _sigmoid(x)
    return s * (1.0 + x * (1.0 - s))


def _softplus(x):
    return jnp.maximum(x, 0.0) + jnp.log(1.0 + jnp.exp(-jnp.abs(x)))


def _mm_call(a, b, out_shape, grid, a_spec, b_spec, o_spec, acc_shape, dims, name, after=None):
    nk = grid[2]
    extra = [] if after is None else [after]

    def prod(a_ref, b_ref):
        return lax.dot_general(a_ref[...].astype(BF16), b_ref[...].astype(BF16), (dims, ((), ())),
                               preferred_element_type=F32)

    def body_single(a_ref, b_ref, *rest):
        o_ref = rest[len(extra)]
        o_ref[...] = prod(a_ref, b_ref).astype(o_ref.dtype)

    def body_multi(a_ref, b_ref, *rest):
        o_ref, acc_ref = rest[len(extra):]
        k = pl.program_id(2)

        @pl.when(k == 0)
        def _():
            acc_ref[...] = prod(a_ref, b_ref)

        @pl.when(jnp.logical_and(k > 0, k < nk - 1))
        def _():
            acc_ref[...] += prod(a_ref, b_ref)

        @pl.when(k == nk - 1)
        def _():
            o_ref[...] = (acc_ref[...] + prod(a_ref, b_ref)).astype(o_ref.dtype)

    return pl.pallas_call(
        body_single if nk == 1 else body_multi, grid=grid, in_specs=[a_spec, b_spec] + [_ANY] * len(extra),
        out_specs=o_spec, out_shape=out_shape, scratch_shapes=[] if nk == 1 else [pltpu.VMEM(acc_shape, F32)],
        compiler_params=_cp(("parallel", "parallel", "arbitrary")), name=name)(a, b, *extra)


def mm_nn(a, b, out_dtype, name, stack=None, tm=1024, tn=1024, tk=2048, n_cols=None, after=None):
    M, K = a.shape
    if stack is None:
        N = b.shape[1] if n_cols is None else n_cols
        tn, tk = _tile(N, tn), _tile(K, tk)
        b_spec = pl.BlockSpec((tk, tn), lambda i, j, k: (k, j))
    elif stack == "col":
        S, _, Ns = b.shape
        N = S * Ns
        tn, tk = _tile(Ns, tn), _tile(K, tk)
        npb = Ns // tn
        b_spec = pl.BlockSpec((None, tk, tn), lambda i, j, k: (j // npb, k, j % npb))
    else:
        S, Ks, N = b.shape
        tn, tk = _tile(N, tn), _tile(Ks, tk)
        kpb = Ks // tk
        b_spec = pl.BlockSpec((None, tk, tn), lambda i, j, k: (k // kpb, k % kpb, j))
    tm = _tile(M, tm)
    return _mm_call(a, b, jax.ShapeDtypeStruct((M, N), out_dtype), (M // tm, N // tn, K // tk),
                    pl.BlockSpec((tm, tk), lambda i, j, k: (i, k)), b_spec,
                    pl.BlockSpec((tm, tn), lambda i, j, k: (i, j)), (tm, tn), ((1,), (0,)), name, after=after)


def mm_cols_dilated(a, b, gcols, d, name, tm=1024, tn=512):
    L, K = a.shape
    S, _, Ns = b.shape
    tm, tn = _tile(L, tm), _tile(Ns, tn)
    npb = Ns // tn
    nj = len(gcols)
    rows = tm // d

    def body(cols_ref, a_ref, b_ref, o_ref, *scr):
        prod = jnp.dot(a_ref[...], b_ref[...], preferred_element_type=F32)
        if d == 1:
            o_ref[0] = prod.astype(BF16)
        else:
            for c in range(tn // 128):
                scr[0][c] = prod[:, c * 128:(c + 1) * 128]
            for r in range(d):
                for c in range(tn // 128):
                    o_ref[r, :, c * 128:(c + 1) * 128] = scr[0].at[c][pl.ds(r, rows, stride=d), :].astype(BF16)

    return pl.pallas_call(
        body,
        grid_spec=pltpu.PrefetchScalarGridSpec(
            num_scalar_prefetch=1, grid=(L // tm, nj),
            in_specs=[pl.BlockSpec((tm, K), lambda i, j, c: (i, 0)),
                      pl.BlockSpec((None, K, tn), lambda i, j, c: (c[j] // npb, 0, c[j] % npb))],
            out_specs=pl.BlockSpec((d, rows, tn), lambda i, j, c: (0, i, j)),
            scratch_shapes=[] if d == 1 else [pltpu.VMEM((tn // 128, tm, 128), F32)]),
        out_shape=jax.ShapeDtypeStruct((d, L // d, nj * tn), BF16),
        compiler_params=_cp(("parallel", "arbitrary")), name=name)(jnp.asarray(gcols, jnp.int32), a, b)


def mm_nt(a, b, out_dtype, name, stack=None, tm=1024, tn=1024, tk=2048, after=None, kw_rows=None):
    M, C = a.shape
    if stack is None:
        Kw = b.shape[0] if kw_rows is None else kw_rows
        tn, tk = _tile(Kw, tn), _tile(C, tk)
        b_spec = pl.BlockSpec((tn, tk), lambda i, j, k: (j, k))
    elif stack == "col":
        S, Kw, Cs = b.shape
        tn, tk = _tile(Kw, tn), _tile(Cs, tk)
        cpb = Cs // tk
        b_spec = pl.BlockSpec((None, tn, tk), lambda i, j, k: (k // cpb, j, k % cpb))
    else:
        S, Ks, _ = b.shape
        Kw = S * Ks
        tn, tk = _tile(Ks, tn), _tile(C, tk)
        jpb = Ks // tn
        b_spec = pl.BlockSpec((None, tn, tk), lambda i, j, k: (j // jpb, j % jpb, k))
    tm = _tile(M, tm)
    return _mm_call(a, b, jax.ShapeDtypeStruct((M, Kw), out_dtype), (M // tm, Kw // tn, C // tk),
                    pl.BlockSpec((tm, tk), lambda i, j, k: (i, k)), b_spec,
                    pl.BlockSpec((tm, tn), lambda i, j, k: (i, j)), (tm, tn), ((1,), (1,)), name, after=after)


def mm_tn(a, b, out_dtype, name, stack=None, n_stack=N_CHIPS, tm=1024, tn=1024, tk=2048, m_rows=None):
    L, M = a.shape
    N = b.shape[1]
    tk = _tile(L, tk)
    if stack is None:
        tm, tn = _tile(M, tm), _tile(N, tn)
        o_spec = pl.BlockSpec((tm, tn), lambda i, j, k: (i, j))
        out_shape = (M if m_rows is None else m_rows, N)
    elif stack == "col":
        Ns = N // n_stack
        tm, tn = _tile(M, tm), _tile(Ns, tn)
        npb = Ns // tn
        o_spec = pl.BlockSpec((None, tm, tn), lambda i, j, k: (j // npb, i, j % npb))
        out_shape = (n_stack, M, Ns)
    else:
        Ms = M // n_stack
        tm, tn = _tile(Ms, tm), _tile(N, tn)
        mpb = Ms // tm
        o_spec = pl.BlockSpec((None, tm, tn), lambda i, j, k: (i // mpb, i % mpb, j))
        out_shape = (n_stack, Ms, N)
    return _mm_call(a, b, jax.ShapeDtypeStruct(out_shape, out_dtype), (M // tm, N // tn, L // tk),
                    pl.BlockSpec((tk, tm), lambda i, j, k: (k, i)), pl.BlockSpec((tk, tn), lambda i, j, k: (k, j)),
                    o_spec, (tm, tn), ((0,), (0,)), name)


def _row_specs(tr, widths):
    return [pl.BlockSpec((tr, w), lambda i: (i, 0)) for w in widths]


def _vec_spec(w):
    return pl.BlockSpec((1, w), lambda i: (0, 0))


def _acc_rows(ref, val, i):
    s = jnp.sum(val, axis=0, keepdims=True)

    @pl.when(i == 0)
    def _():
        ref[...] = s

    @pl.when(i > 0)
    def _():
        ref[...] += s


def modulate(x, scale, shift, name):
    L, D = x.shape
    tr = _tile(L, 512, 16)

    def body(x_ref, sc_ref, sh_ref, h_ref):
        h_ref[...] = (x_ref[...] * (1.0 + sc_ref[...]) + sh_ref[...]).astype(BF16)

    return pl.pallas_call(
        body, grid=(L // tr,), in_specs=_row_specs(tr, [D]) + [_vec_spec(D)] * 2, out_specs=_row_specs(tr, [D])[0],
        out_shape=jax.ShapeDtypeStruct((L, D), BF16), compiler_params=_cp(("parallel",)), name=name)(x, scale, shift)


def _ln_core(x, y, gate, g, b):
    u = ALPHA * x + (1.0 + gate) * y
    mu = jnp.mean(u, axis=-1, keepdims=True)
    d = u - mu
    var = jnp.mean(d * d, axis=-1, keepdims=True)
    rstd = lax.rsqrt(var + LN_EPS)
    xhat = d * rstd
    return xhat * g + b, xhat, rstd


def ln_mid(x, y, gate, g, b, scale, shift):
    L, D = x.shape
    tr = _tile(L, 256, 16)

    def body(x_ref, y_ref, gate_ref, g_ref, b_ref, sc_ref, sh_ref, x1_ref, x1b_ref, h_ref):
        x1, _, _ = _ln_core(x_ref[...], y_ref[...], gate_ref[...], g_ref[...], b_ref[...])
        x1_ref[...] = x1
        x1b_ref[...] = x1.astype(BF16)
        h_ref[...] = (x1 * (1.0 + sc_ref[...]) + sh_ref[...]).astype(BF16)

    return pl.pallas_call(
        body, grid=(L // tr,), in_specs=_row_specs(tr, [D, D]) + [_vec_spec(D)] * 5,
        out_specs=_row_specs(tr, [D, D, D]),
        out_shape=[jax.ShapeDtypeStruct((L, D), F32), jax.ShapeDtypeStruct((L, D), BF16),
                   jax.ShapeDtypeStruct((L, D), BF16)],
        compiler_params=_cp(("parallel",)), name="ln_mid")(x, y, gate, g, b, scale, shift)


def _ln_bwd_rows(dout_v, xhat, rstd, g):
    dxh = dout_v * g
    m1 = jnp.mean(dxh, axis=-1, keepdims=True)
    m2 = jnp.mean(dxh * xhat, axis=-1, keepdims=True)
    return rstd * (dxh - m1 - xhat * m2)


def ln_final_fwd_bwd(x, y, gate, g, b, target):
    L, D = x.shape
    tr = _tile(L, 256, 16)

    def body(x_ref, y_ref, gate_ref, g_ref, b_ref, t_ref, dres_ref, dy_ref, dg_ref, db_ref, dgate_ref, sq_ref):
        i = pl.program_id(0)
        yv = y_ref[...]
        out, xhat, rstd = _ln_core(x_ref[...], yv, gate_ref[...], g_ref[...], b_ref[...])
        err = out - t_ref[...]
        dout_v = err * (1.0 / D)
        du = _ln_bwd_rows(dout_v, xhat, rstd, g_ref[...])
        dres_ref[...] = ALPHA * du
        dy_ref[...] = ((1.0 + gate_ref[...]) * du).astype(BF16)
        _acc_rows(dg_ref, dout_v * xhat, i)
        _acc_rows(db_ref, dout_v, i)
        _acc_rows(dgate_ref, du * yv, i)
        _acc_rows(sq_ref, err * err, i)

    return pl.pallas_call(
        body, grid=(L // tr,), in_specs=_row_specs(tr, [D, D]) + [_vec_spec(D)] * 3 + _row_specs(tr, [D]),
        out_specs=_row_specs(tr, [D, D]) + [_vec_spec(D)] * 4,
        out_shape=[jax.ShapeDtypeStruct((L, D), F32), jax.ShapeDtypeStruct((L, D), BF16)]
        + [jax.ShapeDtypeStruct((1, D), F32)] * 4,
        compiler_params=_cp(("arbitrary",)), name="ln_final_fwd_bwd")(x, y, gate, g, b, target)


def mod_bwd(dres, dh, dh2, xin, scale, name):
    L, D = xin.shape
    tr = _tile(L, 256, 16)

    def body(dres_ref, dh_ref, dh2_ref, x_ref, sc_ref, dx_ref, dsc_ref, dsh_ref):
        i = pl.program_id(0)
        dh_v = dh_ref[...].astype(F32) + dh2_ref[...].astype(F32)
        dx_ref[...] = dres_ref[...] + dh_v * (1.0 + sc_ref[...])
        _acc_rows(dsc_ref, dh_v * x_ref[...], i)
        _acc_rows(dsh_ref, dh_v, i)

    return pl.pallas_call(
        body, grid=(L // tr,), in_specs=_row_specs(tr, [D, D, D, D]) + [_vec_spec(D)],
        out_specs=_row_specs(tr, [D]) + [_vec_spec(D)] * 2,
        out_shape=[jax.ShapeDtypeStruct((L, D), F32)] + [jax.ShapeDtypeStruct((1, D), F32)] * 2,
        compiler_params=_cp(("arbitrary",)), name=name)(dres, dh, dh2, xin, scale)


def mod_ln_bwd(dres_in, dh, dskip, xmid, scale, x, y, gate, g):
    L, D = x.shape
    tr = _tile(L, 256, 16)

    def body(dres_ref, dh_ref, dskip_ref, xm_ref, sc_ref, x_ref, y_ref, gate_ref, g_ref,
             dres_out, dy_ref, dg_ref, db_ref, dgate_ref, dsc_ref, dsh_ref):
        i = pl.program_id(0)
        dh_v = dh_ref[...].astype(F32)
        dout_v = dres_ref[...] + dskip_ref[...].astype(F32) + dh_v * (1.0 + sc_ref[...])
        _acc_rows(dsc_ref, dh_v * xm_ref[...], i)
        _acc_rows(dsh_ref, dh_v, i)
        yv = y_ref[...]
        _, xhat, rstd = _ln_core(x_ref[...], yv, gate_ref[...], g_ref[...], 0.0)
        du = _ln_bwd_rows(dout_v, xhat, rstd, g_ref[...])
        dres_out[...] = ALPHA * du
        dy_ref[...] = ((1.0 + gate_ref[...]) * du).astype(BF16)
        _acc_rows(dg_ref, dout_v * xhat, i)
        _acc_rows(db_ref, dout_v, i)
        _acc_rows(dgate_ref, du * yv, i)

    return pl.pallas_call(
        body, grid=(L // tr,),
        in_specs=_row_specs(tr, [D] * 4) + [_vec_spec(D)] + _row_specs(tr, [D, D]) + [_vec_spec(D)] * 2,
        out_specs=_row_specs(tr, [D, D]) + [_vec_spec(D)] * 5,
        out_shape=[jax.ShapeDtypeStruct((L, D), F32), jax.ShapeDtypeStruct((L, D), BF16)]
        + [jax.ShapeDtypeStruct((1, D), F32)] * 5,
        compiler_params=_cp(("arbitrary",)), name="mod_ln_bwd")(dres_in, dh, dskip, xmid, scale, x, y, gate, g)


CONV_HALO = 16


def _conv_rows(x_ref, i, tr, L):
    nblk = L // tr
    s = pl.multiple_of(i * tr, CONV_HALO)
    cur = x_ref[pl.ds(s, tr), :].astype(F32)
    sp = pl.multiple_of(jnp.maximum(i * tr - CONV_HALO, 0), CONV_HALO)
    sn = pl.multiple_of(jnp.minimum(i * tr + tr, L - CONV_HALO), CONV_HALO)
    prev = x_ref[pl.ds(sp, CONV_HALO), :].astype(F32) * (i > 0).astype(F32)
    nxt = x_ref[pl.ds(sn, CONV_HALO), :].astype(F32) * (i < nblk - 1).astype(F32)
    return jnp.concatenate([prev, cur, nxt], axis=0)


def _shift_rows(v, j):
    n = v.shape[0]
    return v if j % n == 0 else pltpu.roll(v, j % n, 0)


def _conv_taps(xe):
    return [_shift_rows(xe, CONV_W - 1 - k) for k in range(CONV_W)]


def _conv_eval(taps, w_ref, b_ref):
    c = b_ref[...] + w_ref[0:1, :] * taps[0]
    for k in range(1, CONV_W):
        c = c + w_ref[k:k + 1, :] * taps[k]
    return c


def conv_fwd(zx, col0, conv_w, conv_b):
    L = zx.shape[0]
    C = conv_w.shape[1]
    tc = _tile(C, 512)
    tr = _tile(L, 512, CONV_HALO)
    off = col0 // tc

    def body(x_ref, w_ref, b_ref, o_ref):
        i = pl.program_id(1)
        xe = _conv_rows(x_ref, i, tr, L)
        c = _conv_eval(_conv_taps(xe), w_ref, b_ref)[CONV_HALO:CONV_HALO + tr]
        o_ref[...] = _silu(c).astype(BF16)

    return pl.pallas_call(
        body, grid=(C // tc, L // tr),
        in_specs=[pl.BlockSpec((L, tc), lambda j, i: (0, off + j)), pl.BlockSpec((CONV_W, tc), lambda j, i: (0, j)),
                  pl.BlockSpec((1, tc), lambda j, i: (0, j))],
        out_specs=pl.BlockSpec((tr, tc), lambda j, i: (i, j)),
        out_shape=jax.ShapeDtypeStruct((L, C), BF16), compiler_params=_cp(("parallel", "arbitrary")),
        name="conv_fwd")(zx, conv_w, conv_b)


def conv_bwd(zx, col0, conv_w, conv_b, g, dzx, name):
    L = zx.shape[0]
    C = conv_w.shape[1]
    tc = _tile(C, 512)
    tr = _tile(L, 512, CONV_HALO)
    off = col0 // tc
    H = CONV_HALO

    def body(x_ref, g_ref, w_ref, b_ref, buf_ref, dx_ref, dw_ref, db_ref):
        i = pl.program_id(1)
        xe = _conv_rows(x_ref, i, tr, L)
        ge = _conv_rows(g_ref, i, tr, L)
        taps = _conv_taps(xe)
        dc = ge * _dsilu(_conv_eval(taps, w_ref, b_ref))
        dx = w_ref[CONV_W - 1:CONV_W, :] * dc
        for k in range(CONV_W - 1):
            dx = dx + w_ref[k:k + 1, :] * _shift_rows(dc, -(CONV_W - 1 - k))
        dx_ref[...] = dx[H:H + tr].astype(BF16)
        dcc = dc[H:H + tr]
        rows = [jnp.sum(dcc * taps[k][H:H + tr], axis=0, keepdims=True) for k in range(CONV_W)]
        dwv = jnp.concatenate(rows + [jnp.zeros((8 - CONV_W, tc), F32)], axis=0)
        dbv = jnp.sum(dcc, axis=0, keepdims=True)

        @pl.when(i == 0)
        def _():
            dw_ref[...] = dwv
            db_ref[...] = dbv

        @pl.when(i > 0)
        def _():
            dw_ref[...] += dwv
            db_ref[...] += dbv

    dx, dw, db = pl.pallas_call(
        body, grid=(C // tc, L // tr),
        in_specs=[pl.BlockSpec((L, tc), lambda j, i: (0, off + j)), pl.BlockSpec((L, tc), lambda j, i: (0, j)),
                  pl.BlockSpec((CONV_W, tc), lambda j, i: (0, j)), pl.BlockSpec((1, tc), lambda j, i: (0, j)), _ANY],
        out_specs=[pl.BlockSpec((tr, tc), lambda j, i: (i, off + j)), pl.BlockSpec((8, tc), lambda j, i: (0, j)),
                   pl.BlockSpec((1, tc), lambda j, i: (0, j))],
        out_shape=[jax.ShapeDtypeStruct(dzx.shape, BF16), jax.ShapeDtypeStruct((8, C), F32),
                   jax.ShapeDtypeStruct((1, C), F32)],
        input_output_aliases={4: 0},
        compiler_params=_cp(("parallel", "arbitrary")), name=name)(zx, g, conv_w, conv_b, dzx)
    return dx, dw[:CONV_W], db


_NN = (((1,), (0,)), ((), ()))


def _pieces(x, n):
    out, r = [], x
    for _ in range(n):
        p = r.astype(BF16)
        out.append(p)
        r = r - p.astype(F32)
    return out


def _dot01(a, b01, n, dims=_NN):
    b = b01.astype(BF16)
    return functools.reduce(lambda u, v: u + v,
                            [lax.dot_general(p, b, dims, preferred_element_type=F32) for p in _pieces(a, n)])


def _dot01_left(a01, b, n, dims=_NN):
    a = a01.astype(BF16)
    return functools.reduce(lambda u, v: u + v,
                            [lax.dot_general(a, p, dims, preferred_element_type=F32) for p in _pieces(b, n)])


def _ssd_common(dtp_ref, dtpT_ref, bias_ref, biasT_ref, alog_ref, alogT_ref, b_ref, c_ref):
    Q = SSD_Q
    dt = _softplus(dtp_ref[...] + bias_ref[...])
    A = -jnp.exp(alog_ref[...])
    row = lax.broadcasted_iota(jnp.int32, (Q, Q), 0)
    col = lax.broadcasted_iota(jnp.int32, (Q, Q), 1)
    causal = row >= col
    tril = causal.astype(F32)
    Kh = dt.shape[1]
    acum = _dot01_left(tril, dt * A, 3)
    eye = (lax.broadcasted_iota(jnp.int32, (Kh, Kh), 0) == lax.broadcasted_iota(jnp.int32, (Kh, Kh), 1)).astype(F32)
    acumT = _dot01_left(eye, acum, 3, dims=(((1,), (1,)), ((), ())))
    Bm = b_ref[...]
    Cm = c_ref[...]
    cb = lax.dot_general(Cm, Bm, (((1,), (1,)), ((), ())), preferred_element_type=F32)
    return dt, A, causal, row, col, acum, acumT, Bm, Cm, cb


def _ssd_in_specs(Q, GP, N, Kh, DI, cmap):
    nb0 = DI // N
    vec = pl.BlockSpec((None, 1, Kh), lambda g, c: (g, 0, 0))
    vecT = pl.BlockSpec((None, Kh, 1), lambda g, c: (g, 0, 0))
    return [pl.BlockSpec((Q, GP), lambda g, c: (cmap(c), g)),
            pl.BlockSpec((Q, N), lambda g, c: (cmap(c), nb0 + g)),
            pl.BlockSpec((Q, N), lambda g, c: (cmap(c), nb0 + SSD_G + g)),
            pl.BlockSpec((None, Q, Kh), lambda g, c: (g, cmap(c), 0)),
            pl.BlockSpec((None, Kh, Q), lambda g, c: (g, 0, cmap(c))),
            vec, vecT, vec, vecT, vec, vecT]


def _hi(a, b01):
    return _dot01(a, b01, 2)


def _headsum(a, b01):
    return _dot01(a, b01, 1)


def _ssd_heads(dskT_ref, acum, acumT, dt, Kh):
    Q, P, N = SSD_Q, SSD_P, SSD_N
    GP = Kh * P
    sh_p = P.bit_length() - 1
    seg = lambda shape, dim: lax.shift_right_logical(lax.broadcasted_iota(jnp.int32, shape, dim), sh_p)
    E = (seg((Kh, GP), 1) == lax.broadcasted_iota(jnp.int32, (Kh, GP), 0)).astype(F32)
    ET = (seg((GP, Kh), 0) == lax.broadcasted_iota(jnp.int32, (GP, Kh), 1)).astype(F32)
    a_last = acum[Q - 1:Q, :]
    tail = jnp.exp(a_last - acum)
    eLT = jnp.exp(acumT[:, Q - 1:Q])
    rowseg = seg((GP, N), 0)
    eL_b = jnp.zeros((GP, N), F32)
    for k in range(Kh):
        eL_b = jnp.where(rowseg == k, eLT[k:k + 1, :], eL_b)
    return dict(
        E=E, ET=ET, a_last=a_last, tail=tail, eL_b=eL_b,
        dt_all=_hi(dt, E), ea_all=_headsum(jnp.exp(acum), E), tail_all=_headsum(tail, E),
        dsk_all=jnp.sum(E * dskT_ref[...], axis=0, keepdims=True))


def _head_chunks(GP):
    CW = min(GP, 128)
    return CW, CW // SSD_P, GP // CW


def _head_mask(Q, CW, kk):
    lane = lax.broadcasted_iota(jnp.int32, (Q, CW), 1)
    return jnp.logical_and(lane >= kk * SSD_P, lane < (kk + 1) * SSD_P)


def ssd_fwd(xbc, dtp_g, dtp_gT, bias_g, bias_gT, alog_g, alog_gT, dsk_g, dsk_gT, zx, norm_g, DI):
    L = xbc.shape[0]
    Q, P, N, G = SSD_Q, SSD_P, SSD_N, SSD_G
    GP = DI // G
    Kh = GP // P
    nc = L // Q

    CW, hpc, nch = _head_chunks(GP)
    nt = (((1,), (1,)), ((), ()))
    tn = (((0,), (0,)), ((), ()))

    def body(xs_ref, b_ref, c_ref, dtp_ref, dtpT_ref, bias_ref, biasT_ref, alog_ref, alogT_ref, dsk_ref, dskT_ref,
             z_ref, ng_ref, y_ref, st_ref, yn_ref, state):
        @pl.when(pl.program_id(1) == 0)
        def _():
            state[...] = jnp.zeros(state.shape, F32)

        st_ref[...] = state[...]
        dt, A, causal, row, col, acum, acumT, Bm, Cm, cb = _ssd_common(
            dtp_ref, dtpT_ref, bias_ref, biasT_ref, alog_ref, alogT_ref, b_ref, c_ref)
        hd = _ssd_heads(dskT_ref, acum, acumT, dt, Kh)
        xs = xs_ref[...].astype(F32)
        xdt_all = xs * hd["dt_all"]
        S_all = state[...]
        y_all = (lax.dot_general(Cm, S_all.astype(BF16), nt, preferred_element_type=F32) * hd["ea_all"]
                 + xs * hd["dsk_all"])
        state[...] = S_all * hd["eL_b"] + lax.dot_general(
            (xdt_all * hd["tail_all"]).astype(BF16), Bm, tn, preferred_element_type=F32)
        for ch in range(nch):
            cs = slice(ch * CW, (ch + 1) * CW)
            xc = xdt_all[:, cs]
            acc = y_all[:, cs]
            for kk in range(hpc):
                k = ch * hpc + kk
                decay = jnp.exp(jnp.where(causal, acum[:, k:k + 1] - acumT[k:k + 1, :], -jnp.inf))
                xk = xc if hpc == 1 else jnp.where(_head_mask(Q, CW, kk), xc, 0.0)
                acc = acc + jnp.dot((cb * decay).astype(BF16), xk.astype(BF16), preferred_element_type=F32)
            y_ref[:, cs] = acc.astype(BF16)
        y2 = y_ref[...].astype(F32) * _silu(z_ref[...].astype(F32))
        rr = lax.rsqrt(jnp.mean(y2 * y2, axis=-1, keepdims=True) + RMS_EPS)
        yn_ref[...] = (y2 * rr * ng_ref[...]).astype(BF16)

    tile = pl.BlockSpec((Q, GP), lambda g, c: (c, g))
    return pl.pallas_call(
        body, grid=(G, nc),
        in_specs=_ssd_in_specs(Q, GP, N, Kh, DI, lambda c: c) + [tile, pl.BlockSpec((1, GP), lambda g, c: (0, g))],
        out_specs=[tile, pl.BlockSpec((None, None, GP, N), lambda g, c: (c, g, 0, 0)), tile],
        out_shape=[jax.ShapeDtypeStruct((L, DI), BF16), jax.ShapeDtypeStruct((nc, G, GP, N), F32),
                   jax.ShapeDtypeStruct((L, DI), BF16)],
        scratch_shapes=[pltpu.VMEM((GP, N), F32)], compiler_params=_cp(("parallel", "arbitrary")),
        name="ssd_fwd")(xbc, xbc, xbc, dtp_g, dtp_gT, bias_g, bias_gT, alog_g, alog_gT, dsk_g, dsk_gT, zx, norm_g)


def ssd_bwd(xbc, dtp_g, dtp_gT, bias_g, bias_gT, alog_g, alog_gT, dsk_g, dsk_gT, states, dyn, y, zx, norm_g, DI):
    L = xbc.shape[0]
    Q, P, N, G = SSD_Q, SSD_P, SSD_N, SSD_G
    GP = DI // G
    Kh = GP // P
    nc = L // Q
    rev = lambda c: nc - 1 - c

    CW, hpc, nch = _head_chunks(GP)

    def body(xs_ref, b_ref, c_ref, dtp_ref, dtpT_ref, bias_ref, biasT_ref, alog_ref, alogT_ref, dsk_ref, dskT_ref,
             st_ref, dyn_ref, y_ref, z_ref, ng_ref,
             dxs_ref, dB_ref, dC_ref, ddtp_ref, dbias_ref, dalog_ref, dD_ref, dz_ref, dng_ref, dstate):
        ci = pl.program_id(1)

        @pl.when(ci == 0)
        def _():
            dstate[...] = jnp.zeros(dstate.shape, F32)

        dt, A, causal, row, col, acum, acumT, Bm, Cm, cb = _ssd_common(
            dtp_ref, dtpT_ref, bias_ref, biasT_ref, alog_ref, alogT_ref, b_ref, c_ref)
        tn = (((0,), (0,)), ((), ()))
        nt = (((1,), (1,)), ((), ()))
        hd = _ssd_heads(dskT_ref, acum, acumT, dt, Kh)
        ET, tail = hd["ET"], hd["tail"]
        cbT = lax.dot_general(Bm, Cm, nt, preferred_element_type=F32)
        causalT = row <= col
        xs = xs_ref[...].astype(F32)
        xdt_all = xs * hd["dt_all"]
        yv = y_ref[...].astype(F32)
        zv = z_ref[...].astype(F32)
        dynv = dyn_ref[...].astype(F32)
        sz = _silu(zv)
        y2 = yv * sz
        rr = lax.rsqrt(jnp.mean(y2 * y2, axis=-1, keepdims=True) + RMS_EPS)
        yh = y2 * rr
        dyh = dynv * ng_ref[...]
        dy2 = rr * (dyh - yh * jnp.mean(dyh * yh, axis=-1, keepdims=True))
        dz_ref[...] = (dy2 * yv * _dsilu(zv)).astype(BF16)
        dng_v = jnp.sum(dynv * yh, axis=0, keepdims=True)
        dyb = (dy2 * sz).astype(BF16)
        dy_all = dyb.astype(F32)
        S_all = st_ref[...]
        S_b = S_all.astype(BF16)
        dS_all = dstate[...]
        dS_b = dS_all.astype(BF16)
        CS_all = lax.dot_general(Cm, S_b, nt, preferred_element_type=F32)
        dyE_b = (dy_all * hd["ea_all"]).astype(BF16)
        dC_acc = jnp.dot(dyE_b, S_b, preferred_element_type=F32)
        dS_y = lax.dot_general(dyE_b, Cm, tn, preferred_element_type=F32)
        BdS_all = lax.dot_general(Bm, dS_b, nt, preferred_element_type=F32)
        dB_acc = jnp.dot((xdt_all * hd["tail_all"]).astype(BF16), dS_b, preferred_element_type=F32)
        dtail = _headsum(xdt_all * BdS_all, ET)
        da_cols = _headsum(dy_all * CS_all * hd["ea_all"], ET) - dtail * tail
        dss = _dot01_left(jnp.ones((8, N), F32), _dot01_left(hd["E"], dS_all * S_all, 2), 2, dims=nt)
        da_last = dss[0:1] * jnp.exp(hd["a_last"]) + jnp.sum(dtail * tail, axis=0, keepdims=True)
        rowi = lax.broadcasted_iota(jnp.int32, (Q, Kh), 0)
        da_cols = da_cols + jnp.where(rowi == Q - 1, da_last, 0.0)
        dstate[...] = hd["eL_b"] * dS_all + dS_y
        sum_mg = jnp.zeros((Q, Q), F32)
        ddt_x = jnp.zeros((Q, Kh), F32)
        da_rows = jnp.zeros((Kh, Q), F32)
        lane_k = lax.broadcasted_iota(jnp.int32, (Q, Kh), 1)
        sub_k = lax.broadcasted_iota(jnp.int32, (Kh, Q), 0)
        for ch in range(nch):
            cs = slice(ch * CW, (ch + 1) * CW)
            dyc = dyb[:, cs]
            xc_b = xdt_all[:, cs].astype(BF16)
            acc = hd["tail_all"][:, cs] * BdS_all[:, cs]
            for kk in range(hpc):
                k = ch * hpc + kk
                a_b = jnp.broadcast_to(acum[:, k:k + 1], (Q, Q))
                a_r = acumT[k:k + 1, :]
                decay = jnp.exp(jnp.where(causal, a_b - a_r, -jnp.inf))
                decayT = jnp.exp(jnp.where(causalT, a_r - a_b, -jnp.inf))
                dyk = dyc if hpc == 1 else jnp.where(_head_mask(Q, CW, kk), dyc, jnp.zeros_like(dyc))
                mg = decay * lax.dot_general(dyk, xc_b, nt, preferred_element_type=F32)
                sum_mg = sum_mg + mg
                w = mg * cb
                da_cols = da_cols + jnp.where(lane_k == k, jnp.sum(w, axis=1, keepdims=True), 0.0)
                da_rows = da_rows + jnp.where(sub_k == k, jnp.sum(w, axis=0, keepdims=True), 0.0)
                acc = acc + jnp.dot((decayT * cbT).astype(BF16), dyk, preferred_element_type=F32)
            dxs_ref[:, cs] = (acc * hd["dt_all"][:, cs] + dy_all[:, cs] * hd["dsk_all"][:, cs]).astype(BF16)
            ddt_x = ddt_x + _headsum(acc * xs[:, cs], ET[cs, :])
        eye_q = (row == col).astype(F32)
        da_cols = da_cols - _dot01_left(eye_q, da_rows, 3, dims=nt)
        dD_row = jnp.sum(_headsum(dy_all * xs, ET), axis=0, keepdims=True)
        sum_mg_b = sum_mg.astype(BF16)
        dB_ref[...] = (dB_acc + lax.dot_general(sum_mg_b, Cm, tn, preferred_element_type=F32)).astype(BF16)
        dC_ref[...] = (dC_acc + jnp.dot(sum_mg_b, Bm, preferred_element_type=F32)).astype(BF16)
        triu = (row <= col).astype(F32)
        ddtA = _dot01_left(triu, da_cols, 3)
        ddt = ddt_x + ddtA * A
        dpre = ddt * _sigmoid(dtp_ref[...] + bias_ref[...])
        ddtp_ref[...] = dpre
        dbias_v = jnp.sum(dpre, axis=0, keepdims=True)
        dalog_v = jnp.sum(ddtA * dt, axis=0, keepdims=True) * A

        @pl.when(ci == 0)
        def _():
            dbias_ref[...] = dbias_v
            dalog_ref[...] = dalog_v
            dD_ref[...] = dD_row
            dng_ref[...] = dng_v

        @pl.when(ci > 0)
        def _():
            dbias_ref[...] += dbias_v
            dalog_ref[...] += dalog_v
            dD_ref[...] += dD_row
            dng_ref[...] += dng_v

    vec_o = pl.BlockSpec((None, 1, Kh), lambda g, c: (g, 0, 0))
    tile = pl.BlockSpec((Q, GP), lambda g, c: (rev(c), g))
    return pl.pallas_call(
        body, grid=(G, nc),
        in_specs=_ssd_in_specs(Q, GP, N, Kh, DI, rev)
        + [pl.BlockSpec((None, None, GP, N), lambda g, c: (rev(c), g, 0, 0)), tile, tile, tile,
           pl.BlockSpec((1, GP), lambda g, c: (0, g))],
        out_specs=[tile, pl.BlockSpec((Q, N), lambda g, c: (rev(c), g)), pl.BlockSpec((Q, N), lambda g, c: (rev(c), g)),
                   pl.BlockSpec((None, Q, Kh), lambda g, c: (g, rev(c), 0)), vec_o, vec_o, vec_o,
                   tile, pl.BlockSpec((1, GP), lambda g, c: (0, g))],
        out_shape=[jax.ShapeDtypeStruct((L, DI), BF16), jax.ShapeDtypeStruct((L, G * N), BF16),
                   jax.ShapeDtypeStruct((L, G * N), BF16), jax.ShapeDtypeStruct((G, L, Kh), F32)]
        + [jax.ShapeDtypeStruct((G, 1, Kh), F32)] * 3
        + [jax.ShapeDtypeStruct(zx.shape, BF16), jax.ShapeDtypeStruct((1, DI), F32)],
        scratch_shapes=[pltpu.VMEM((GP, N), F32)], compiler_params=_cp(("parallel", "arbitrary")),
        name="ssd_bwd")(xbc, xbc, xbc, dtp_g, dtp_gT, bias_g, bias_gT, alog_g, alog_gT, dsk_g, dsk_gT, states,
                        dyn, y, zx, norm_g)


def _alibi_slope(gi, h):
    n = len(DIL_PATTERNS) * DIL_H
    return float(2.0 ** (-8.0 * (gi * DIL_H + h + 1) / n))


def _attn_masks():
    qi = lax.broadcasted_iota(jnp.int32, (DIL_BLK, DIL_BLK), 0)
    kj = lax.broadcasted_iota(jnp.int32, (DIL_BLK, DIL_BLK), 1)
    dcur = (qi - kj).astype(F32)
    return dcur, qi >= kj, dcur + float(DIL_BLK), kj >= qi


def attn_fwd(q3, kv3, gi):
    window, d = DIL_PATTERNS[gi]
    assert window // d == DIL_BLK
    HW = DIL_H * DIL_E
    M = q3.shape[1]
    nb = M // DIL_BLK
    scale = DIL_E ** -0.5
    nt = (((1,), (1,)), ((), ()))

    def body(q_ref, kp_ref, kc_ref, vp_ref, vc_ref, o_ref, lse_ref):
        n = pl.program_id(1)
        dcur, vcur, dprev, vprev0 = _attn_masks()
        dist = jnp.concatenate([dprev, dcur], axis=1)
        valid = jnp.concatenate([jnp.logical_and(vprev0, n > 0), vcur], axis=1)
        lane = lax.broadcasted_iota(jnp.int32, (DIL_BLK, 128), 1)
        lse_acc = jnp.zeros((DIL_BLK, 128), F32)
        for h in range(DIL_H):
            hs = slice(h * DIL_E, (h + 1) * DIL_E)
            sl = _alibi_slope(gi, h) * d
            kcat = jnp.concatenate([kp_ref[:, hs], kc_ref[:, hs]], axis=0)
            vcat = jnp.concatenate([vp_ref[:, hs], vc_ref[:, hs]], axis=0)
            s = lax.dot_general(q_ref[:, hs], kcat, nt, preferred_element_type=F32) * scale - sl * dist
            s = jnp.where(valid, s, -jnp.inf)
            m = jnp.max(s, axis=-1, keepdims=True)
            p = jnp.exp(s - m)
            den = jnp.sum(p, axis=-1, keepdims=True)
            o = jnp.dot(p.astype(BF16), vcat, preferred_element_type=F32) / den
            o_ref[:, hs] = o.astype(BF16)
            lse_acc = jnp.where(lane == h, m + jnp.log(den), lse_acc)
        lse_ref[...] = lse_acc

    blk = (None, DIL_BLK, HW)
    prev = lambda n: jnp.maximum(n - 1, 0)
    return pl.pallas_call(
        body, grid=(d, nb),
        in_specs=[pl.BlockSpec(blk, lambda r, n: (r, n, 0)),
                  pl.BlockSpec(blk, lambda r, n: (r, prev(n), 0)), pl.BlockSpec(blk, lambda r, n: (r, n, 0)),
                  pl.BlockSpec(blk, lambda r, n: (r, prev(n), 1)), pl.BlockSpec(blk, lambda r, n: (r, n, 1))],
        out_specs=[pl.BlockSpec(blk, lambda r, n: (r, n, 0)), pl.BlockSpec((None, DIL_BLK, 128), lambda r, n: (r, n, 0))],
        out_shape=[jax.ShapeDtypeStruct((d, M, HW), BF16), jax.ShapeDtypeStruct((d, M, 128), F32)],
        compiler_params=_cp(("parallel", "parallel")), name=f"attn_fwd_{gi}")(q3, kv3, kv3, kv3, kv3)


def attn_bwd(q3, kv3, do3, lse3, dpr3, gi):
    window, d = DIL_PATTERNS[gi]
    HW = DIL_H * DIL_E
    M = q3.shape[1]
    L = M * d
    nb = M // DIL_BLK
    scale = DIL_E ** -0.5
    nt = (((1,), (1,)), ((), ()))
    tn = (((0,), (0,)), ((), ()))

    def body(q0_ref, q1_ref, k_ref, v_ref, do0_ref, do1_ref, l0_ref, l1_ref, r0_ref, r1_ref,
             dq_ref, dk_ref, dv_ref, carry):
        n = pl.program_id(1)

        @pl.when(n == 0)
        def _():
            carry[...] = jnp.zeros(carry.shape, F32)

        dcur, vcur, dprev, vprev0 = _attn_masks()
        dist = jnp.concatenate([dcur, dprev], axis=0)
        valid = jnp.concatenate([vcur, jnp.logical_and(vprev0, n < nb - 1)], axis=0)
        B = DIL_BLK
        for h in range(DIL_H):
            hs = slice(h * DIL_E, (h + 1) * DIL_E)
            sl = _alibi_slope(gi, h) * d
            kh = k_ref[:, hs]
            vh = v_ref[:, hs]
            qcat = jnp.concatenate([q0_ref[:, hs], q1_ref[:, hs]], axis=0)
            docat = jnp.concatenate([do0_ref[:, hs], do1_ref[:, hs]], axis=0)
            lcat = jnp.concatenate([l0_ref[:, h:h + 1], l1_ref[:, h:h + 1]], axis=0)
            rcat = jnp.concatenate([r0_ref[:, h:h + 1], r1_ref[:, h:h + 1]], axis=0)
            s = lax.dot_general(qcat, kh, nt, preferred_element_type=F32) * scale - sl * dist
            p = jnp.exp(jnp.where(valid, s - lcat, -jnp.inf))
            ds = p * (lax.dot_general(docat, vh, nt, preferred_element_type=F32) - rcat)
            ds_b = (ds * scale).astype(BF16)
            dv_ref[:, hs] = lax.dot_general(p.astype(BF16), docat, tn, preferred_element_type=F32).astype(BF16)
            dk_ref[:, hs] = lax.dot_general(ds_b, qcat, tn, preferred_element_type=F32).astype(BF16)
            dqc = jnp.dot(ds_b, kh, preferred_element_type=F32)
            dq_ref[:, hs] = (carry[:, hs] + dqc[:B]).astype(BF16)
            carry[:, hs] = dqc[B:]

    blk = (None, DIL_BLK, HW)
    sblk = (None, DIL_BLK, 128)
    oblk = (DIL_BLK, HW)
    nxt = lambda n: jnp.minimum(n + 1, nb - 1)
    here = lambda c: (lambda r, n: (r, n, c))
    ahead = lambda c: (lambda r, n: (r, nxt(n), c))
    outs = pl.pallas_call(
        body, grid=(d, nb),
        in_specs=[pl.BlockSpec(blk, here(0)), pl.BlockSpec(blk, ahead(0)),
                  pl.BlockSpec(blk, here(0)), pl.BlockSpec(blk, here(1)),
                  pl.BlockSpec(blk, here(0)), pl.BlockSpec(blk, ahead(0)),
                  pl.BlockSpec(sblk, here(0)), pl.BlockSpec(sblk, ahead(0)),
                  pl.BlockSpec(sblk, here(0)), pl.BlockSpec(sblk, ahead(0))],
        out_specs=[pl.BlockSpec(oblk, lambda r, n: (n, r))] * 3,
        out_shape=[jax.ShapeDtypeStruct((M, d * HW), BF16)] * 3,
        scratch_shapes=[pltpu.VMEM(oblk, F32)], compiler_params=_cp(("parallel", "arbitrary")),
        name=f"attn_bwd_{gi}")(q3, q3, kv3, kv3, do3, do3, lse3, lse3, dpr3, dpr3)
    return [t.reshape(L, HW) for t in outs]


def _merge_weights(l_tiles, h):
    ls = [t[:, h:h + 1] for t in l_tiles]
    mx = functools.reduce(jnp.maximum, ls)
    es = [jnp.exp(l - mx) for l in ls]
    den = functools.reduce(lambda a, b: a + b, es)
    return [e / den for e in es]


def _dil_specs(tr, arrs):
    return [pl.BlockSpec((a.shape[0], tr // a.shape[0], a.shape[2]), lambda i: (0, i, 0)) for a in arrs]


def _dil_scratch(tr, arrs):
    return [pltpu.VMEM((a.shape[2] // 128, tr, 128), F32) for a in arrs if a.shape[0] > 1]


def _undilate(refs3, scrs, tr):
    out, k = [], 0
    for ref in refs3:
        d, _, W = ref.shape
        if d == 1:
            out.append(lambda c, ref=ref: ref[0, :, c * 128:(c + 1) * 128])
            continue
        scr = scrs[k]
        k += 1
        for r in range(d):
            for c in range(W // 128):
                scr.at[c][pl.ds(r, tr // d, stride=d), :] = ref[r, :, c * 128:(c + 1) * 128].astype(F32)
        out.append(lambda c, scr=scr: scr[c])
    return out


def merge_fwd(os3, lses3, z):
    HW = os3[0].shape[2]
    L = os3[0].shape[0] * os3[0].shape[1]
    tr = _tile(L, 256, 16)
    ng = len(os3)
    n_scr = len(_dil_scratch(tr, os3))

    def body(*refs):
        z_ref, out_ref = refs[2 * ng], refs[2 * ng + 1]
        scrs = refs[2 * ng + 2:]
        o_get = _undilate(refs[:ng], scrs[:n_scr], tr)
        l_tiles = [g(0) for g in _undilate(refs[ng:2 * ng], scrs[n_scr:], tr)]
        for h in range(DIL_H):
            hs = slice(h * DIL_E, (h + 1) * DIL_E)
            ws = _merge_weights(l_tiles, h)
            om = functools.reduce(lambda a, b: a + b, [w * o(h).astype(F32) for w, o in zip(ws, o_get)])
            out_ref[:, hs] = (om * _silu(z_ref[:, hs].astype(F32))).astype(BF16)

    return pl.pallas_call(
        body, grid=(L // tr,),
        in_specs=_dil_specs(tr, os3) + _dil_specs(tr, lses3) + _row_specs(tr, [HW]),
        out_specs=_row_specs(tr, [HW])[0], out_shape=jax.ShapeDtypeStruct((L, HW), BF16),
        scratch_shapes=_dil_scratch(tr, os3) + _dil_scratch(tr, lses3),
        compiler_params=_cp(("parallel",)), name="merge_fwd")(*os3, *lses3, z)


def merge_bwd(dgated, os3, lses3, z):
    HW = os3[0].shape[2]
    L = os3[0].shape[0] * os3[0].shape[1]
    tr = _tile(L, 256, 16)
    ng = len(os3)
    n_scr = len(_dil_scratch(tr, os3))

    def body(*refs):
        dg_ref = refs[0]
        z_ref = refs[1 + 2 * ng]
        outs = refs[2 + 2 * ng:2 + 2 * ng + 2 * ng + 1]
        scrs = refs[2 + 2 * ng + 2 * ng + 1:]
        do_out, dpr_out, dz_ref = outs[:ng], outs[ng:2 * ng], outs[2 * ng]
        o_get = _undilate(refs[1:1 + ng], scrs[:n_scr], tr)
        l_tiles = [g(0) for g in _undilate(refs[1 + ng:1 + 2 * ng], scrs[n_scr:2 * n_scr], tr)]
        stage = scrs[2 * n_scr:]
        do_stage, dpr_stage, k = [], [], 0
        for g in range(ng):
            if do_out[g].shape[0] == 1:
                do_stage.append(None)
                dpr_stage.append(None)
            else:
                do_stage.append(stage[2 * k])
                dpr_stage.append(stage[2 * k + 1])
                k += 1
        lane = lax.broadcasted_iota(jnp.int32, (tr, 128), 1)
        accs = [jnp.zeros((tr, 128), F32) for _ in range(ng)]
        for h in range(DIL_H):
            hs = slice(h * DIL_E, (h + 1) * DIL_E)
            ws = _merge_weights(l_tiles, h)
            ov = [o(h).astype(F32) for o in o_get]
            om = functools.reduce(lambda a, b: a + b, [w * o for w, o in zip(ws, ov)])
            zv = z_ref[:, hs].astype(F32)
            dgv = dg_ref[:, hs].astype(F32)
            dom = dgv * _silu(zv)
            dz_ref[:, hs] = (dgv * om * _dsilu(zv)).astype(BF16)
            dws = [jnp.sum(dom * o, axis=-1, keepdims=True) for o in ov]
            dwbar = functools.reduce(lambda a, b: a + b, [w * dw for w, dw in zip(ws, dws)])
            for g in range(ng):
                if do_stage[g] is None:
                    do_out[g][0, :, hs] = (ws[g] * dom).astype(BF16)
                else:
                    do_stage[g][h] = ws[g] * dom
                accs[g] = jnp.where(lane == h, ws[g] * dwbar, accs[g])
        for g in range(ng):
            d = do_out[g].shape[0]
            if d == 1:
                dpr_out[g][0] = accs[g]
                continue
            dpr_stage[g][0] = accs[g]
            for r in range(d):
                dpr_out[g][r] = dpr_stage[g].at[0][pl.ds(r, tr // d, stride=d), :]
                for c in range(HW // 128):
                    do_out[g][r, :, c * 128:(c + 1) * 128] = do_stage[g].at[c][pl.ds(r, tr // d, stride=d), :].astype(BF16)

    stage_shapes = []
    for o3 in os3:
        if o3.shape[0] > 1:
            stage_shapes += [pltpu.VMEM((HW // 128, tr, 128), F32), pltpu.VMEM((1, tr, 128), F32)]
    outs = pl.pallas_call(
        body, grid=(L // tr,),
        in_specs=_row_specs(tr, [HW]) + _dil_specs(tr, os3) + _dil_specs(tr, lses3) + _row_specs(tr, [HW]),
        out_specs=_dil_specs(tr, os3) + _dil_specs(tr, lses3) + _row_specs(tr, [HW]),
        out_shape=[jax.ShapeDtypeStruct(o.shape, BF16) for o in os3] + [jax.ShapeDtypeStruct(l.shape, F32) for l in lses3]
        + [jax.ShapeDtypeStruct((L, HW), BF16)],
        scratch_shapes=_dil_scratch(tr, os3) + _dil_scratch(tr, lses3) + stage_shapes,
        compiler_params=_cp(("parallel",)), name="merge_bwd")(dgated, *os3, *lses3, z)
    return outs[:ng], outs[ng:2 * ng], outs[2 * ng]


def ada_fwd(c8, ada_w):
    nl, D, Ws = ada_w.shape
    tn = _tile(Ws, 512)

    def body(c_ref, w_ref, o_ref):
        o_ref[...] = jnp.dot(_silu(c_ref[...]), w_ref[...], precision=lax.Precision.HIGHEST,
                             preferred_element_type=F32)

    return pl.pallas_call(
        body, grid=(nl, Ws // tn),
        in_specs=[pl.BlockSpec((N_DEV, D), lambda l, j: (0, 0)), pl.BlockSpec((None, D, tn), lambda l, j: (l, 0, j))],
        out_specs=pl.BlockSpec((None, N_DEV, tn), lambda l, j: (l, 0, j)),
        out_shape=jax.ShapeDtypeStruct((nl, N_DEV, Ws), F32), compiler_params=_cp(("parallel", "parallel")),
        name="ada_fwd")(c8, ada_w)


def ada_wgrad(c8t, dmod):
    nl, _, Ws = dmod.shape
    D = c8t.shape[0]
    tm = _tile(D, 512, 8)

    def body(c_ref, d_ref, o_ref):
        sc = _silu(c_ref[...])
        acc = sc[:, 0:1] * d_ref[0:1, :]
        for e in range(1, N_DEV):
            acc = acc + sc[:, e:e + 1] * d_ref[e:e + 1, :]
        o_ref[...] = acc

    return pl.pallas_call(
        body, grid=(nl, D // tm),
        in_specs=[pl.BlockSpec((tm, N_DEV), lambda l, i: (i, 0)), pl.BlockSpec((None, N_DEV, Ws), lambda l, i: (l, 0, 0))],
        out_specs=pl.BlockSpec((None, tm, Ws), lambda l, i: (l, i, 0)),
        out_shape=jax.ShapeDtypeStruct((nl, D, Ws), F32), compiler_params=_cp(("parallel", "parallel")),
        name="ada_wgrad")(c8t, dmod)


def adamw(w, g, m, v, name):
    R, C = w.shape
    tr = _tile(R, 256, 8)
    c1 = 1.0 - ADAM_B1 ** ADAM_STEP
    c2 = 1.0 - ADAM_B2 ** ADAM_STEP

    def body(w_ref, g_ref, m_ref, v_ref, d_ref, nm_ref, nv_ref):
        gv = g_ref[...]
        nm = ADAM_B1 * m_ref[...] + (1.0 - ADAM_B1) * gv
        nv = ADAM_B2 * v_ref[...] + (1.0 - ADAM_B2) * (gv * gv)
        nm_ref[...] = nm
        nv_ref[...] = nv
        d_ref[...] = -ADAM_LR * ((nm / c1) / (jnp.sqrt(nv / c2) + ADAM_EPS) + ADAM_WD * w_ref[...])

    return pl.pallas_call(
        body, grid=(R // tr,), in_specs=_row_specs(tr, [C] * 4), out_specs=_row_specs(tr, [C] * 3),
        out_shape=[jax.ShapeDtypeStruct((R, C), F32)] * 3, compiler_params=_cp(("parallel",)), name=name)(w, g, m, v)


def sum_leading(t, name, out_dtype=F32):
    S, R, C = t.shape
    tr = _tile(R, 256, 16)

    def body(t_ref, o_ref):
        acc = t_ref[0].astype(F32)
        for s in range(1, S):
            acc = acc + t_ref[s].astype(F32)
        o_ref[...] = acc.astype(out_dtype)

    return pl.pallas_call(
        body, grid=(R // tr,), in_specs=[pl.BlockSpec((S, tr, C), lambda i: (0, i, 0))],
        out_specs=pl.BlockSpec((tr, C), lambda i: (i, 0)), out_shape=jax.ShapeDtypeStruct((R, C), out_dtype),
        compiler_params=_cp(("parallel",)), name=name)(t)


def add_half(g, a, core, name, by_cols=False):
    S, R, C = g.shape

    def body(core_ref, g_ref, a_ref, o_ref):
        o_ref[...] = (g_ref[...].astype(F32) + a_ref[...].astype(F32)).astype(BF16)

    if by_cols:
        hc = C // 2
        tr = _tile(R, 512, 16)
        return pl.pallas_call(
            body,
            grid_spec=pltpu.PrefetchScalarGridSpec(
                num_scalar_prefetch=1, grid=(S, R // tr),
                in_specs=[pl.BlockSpec((None, tr, hc), lambda s, i, core_ref: (s, i, core_ref[0])),
                          pl.BlockSpec((None, tr, hc), lambda s, i, core_ref: (s, i, 0))],
                out_specs=pl.BlockSpec((None, tr, hc), lambda s, i, core_ref: (s, i, 0))),
            out_shape=jax.ShapeDtypeStruct((S, R, hc), BF16), compiler_params=_cp(("parallel", "parallel")),
            name=name)(core, g, a)
    h = R // 2
    tr = _tile(h, 256, 16)
    nb = h // tr

    return pl.pallas_call(
        body,
        grid_spec=pltpu.PrefetchScalarGridSpec(
            num_scalar_prefetch=1, grid=(S, nb),
            in_specs=[pl.BlockSpec((None, tr, C), lambda s, i, core_ref: (s, core_ref[0] * nb + i, 0)),
                      pl.BlockSpec((None, tr, C), lambda s, i, core_ref: (s, i, 0))],
            out_specs=pl.BlockSpec((None, tr, C), lambda s, i, core_ref: (s, i, 0))),
        out_shape=jax.ShapeDtypeStruct((S, h, C), BF16), compiler_params=_cp(("parallel", "parallel")),
        name=name)(core, g, a)


def sum_partials(own, landed, chip, name):
    _, h, C = own.shape
    tr = _tile(h, 512, 16)

    def body(chip_ref, own_ref, l_ref, o_ref):
        acc = own_ref[...].astype(F32)
        for j in range(3):
            acc = acc + l_ref[j].astype(F32)
        o_ref[...] = acc

    return pl.pallas_call(
        body,
        grid_spec=pltpu.PrefetchScalarGridSpec(
            num_scalar_prefetch=1, grid=(h // tr,),
            in_specs=[pl.BlockSpec((None, tr, C), lambda i, chip_ref: (chip_ref[0], i, 0)),
                      pl.BlockSpec((3, tr, C), lambda i, chip_ref: (0, i, 0))],
            out_specs=pl.BlockSpec((tr, C), lambda i, chip_ref: (i, 0))),
        out_shape=jax.ShapeDtypeStruct((h, C), F32), compiler_params=_cp(("parallel",)), name=name)(chip, own, landed)


def adamw_halves(w, g_mine, g_theirs, m, v, core, name):
    R, C = w.shape
    h = R // 2
    tr = _tile(h, 256, 8)
    nbh = h // tr
    c1 = 1.0 - ADAM_B1 ** ADAM_STEP
    c2 = 1.0 - ADAM_B2 ** ADAM_STEP

    def body(core_ref, w_ref, gm_ref, gt_ref, m_ref, v_ref, g_ref, d_ref, nm_ref, nv_ref):
        mine = (pl.program_id(0) // nbh) == core_ref[0]
        gv = jnp.where(mine, gm_ref[...], gt_ref[...])
        g_ref[...] = gv
        nm = ADAM_B1 * m_ref[...] + (1.0 - ADAM_B1) * gv
        nv = ADAM_B2 * v_ref[...] + (1.0 - ADAM_B2) * (gv * gv)
        nm_ref[...] = nm
        nv_ref[...] = nv
        d_ref[...] = -ADAM_LR * ((nm / c1) / (jnp.sqrt(nv / c2) + ADAM_EPS) + ADAM_WD * w_ref[...])

    full = pl.BlockSpec((tr, C), lambda i, core_ref: (i, 0))
    halfspec = pl.BlockSpec((tr, C), lambda i, core_ref: (i % nbh, 0))
    return pl.pallas_call(
        body,
        grid_spec=pltpu.PrefetchScalarGridSpec(
            num_scalar_prefetch=1, grid=(2 * nbh,), in_specs=[full, halfspec, halfspec, full, full],
            out_specs=[full] * 4),
        out_shape=[jax.ShapeDtypeStruct((R, C), F32)] * 4, compiler_params=_cp(("parallel",)),
        name=name)(core, w, g_mine, g_theirs, m, v)


_ANY = pl.BlockSpec(memory_space=pl.ANY)


def _place():
    x, y, c = lax.axis_index("x"), lax.axis_index("y"), lax.axis_index("c")
    chips = [(1 - x, y), (x, 1 - y), (1 - x, 1 - y)]
    return x, y, c, chips


def allgather_small(v, name, after=None):
    R, W = v.shape
    extra = [] if after is None else [after]

    def body(x_ref, *rest):
        out_ref, send_sems, recv_sems, local_sem = rest[len(extra):]
        x, y, c, chips = _place()
        me, sibling = (x, y, c), (x, y, 1 - c)

        def rows(px, py, pc):
            return out_ref.at[pl.ds((4 * px + 2 * py + pc) * R, R), :]

        def copy(k, block, to, src=None):
            return pltpu.make_async_remote_copy(
                src_ref=rows(*block) if src is None else src, dst_ref=rows(*block),
                send_sem=send_sems.at[k], recv_sem=recv_sems.at[k], device_id=to, device_id_type=MESH)

        mine = pltpu.make_async_copy(x_ref, rows(*me), local_sem)
        mine.start()
        first = [copy(0, me, sibling, src=x_ref)]
        first += [copy(1 + j, me, (*chip, c), src=x_ref) for j, chip in enumerate(chips)]
        for cp in first:
            cp.start()
        passed = [copy(4 + j, (*chip, c), sibling) for j, chip in enumerate(chips)]
        for j, chip in enumerate(chips):
            copy(1 + j, (*chip, c), me).wait_recv()
            passed[j].start()
        copy(0, sibling, me).wait_recv()
        for j, chip in enumerate(chips):
            copy(4 + j, (*chip, 1 - c), me).wait_recv()
        for cp in first + passed:
            cp.wait_send()
        mine.wait()

    return pl.pallas_call(
        body, out_shape=jax.ShapeDtypeStruct((N_DEV * R, W), v.dtype),
        in_specs=[pl.BlockSpec(memory_space=pltpu.VMEM)] + [_ANY] * len(extra),
        out_specs=pl.BlockSpec(memory_space=pltpu.VMEM),
        scratch_shapes=[pltpu.SemaphoreType.DMA((7,)), pltpu.SemaphoreType.DMA((7,)), pltpu.SemaphoreType.DMA],
        name=name)(v, *extra)


def allgather_routed(shard, name):
    R, C = shard.shape
    hc = C // 2
    ra = (R // 2) // 16 * 16

    def body(in_ref, out_ref, send_sems, recv_sems):
        x, y, c, _ = _place()
        xn, yn = (1 - x, y, c), (x, 1 - y, c)
        sibling = (x, y, 1 - c)
        p, pxn, pyn, pdg = 2 * x + y, 2 * (1 - x) + y, 2 * x + (1 - y), 2 * (1 - x) + (1 - y)
        rows_a, rows_b, rows_all = pl.ds(0, ra), pl.ds(ra, R - ra), pl.ds(0, R)

        def win(ref, rows, core):
            return ref.at[rows, pl.ds(pl.multiple_of(core * hc, 128), hc)]

        def copy(k, chip_id, rows, core, to, src=None):
            blk = win(out_ref.at[chip_id], rows, core)
            return pltpu.make_async_remote_copy(
                src_ref=blk if src is None else src, dst_ref=blk, send_sem=send_sems.at[k], recv_sem=recv_sems.at[k],
                device_id=to, device_id_type=MESH)

        own = [copy(0, p, rows_a, c, xn, src=win(in_ref, rows_a, c)), copy(1, p, rows_b, c, xn, src=win(in_ref, rows_b, c)),
               copy(2, p, rows_b, c, yn, src=win(in_ref, rows_b, c)), copy(3, p, rows_a, c, yn, src=win(in_ref, rows_a, c))]
        for cp in own:
            cp.start()
        copy(0, pxn, rows_a, c, xn).wait_recv()
        fwd_a = copy(4, pxn, rows_a, c, yn)
        fwd_a.start()
        copy(2, pyn, rows_b, c, yn).wait_recv()
        fwd_b = copy(5, pyn, rows_b, c, xn)
        fwd_b.start()
        copy(1, pxn, rows_b, c, xn).wait_recv()
        copy(3, pyn, rows_a, c, yn).wait_recv()
        passed = [copy(6, pxn, rows_all, c, sibling), copy(7, pyn, rows_all, c, sibling)]
        for cp in passed:
            cp.start()
        copy(4, pdg, rows_a, c, yn).wait_recv()
        passed.append(copy(8, pdg, rows_a, c, sibling))
        passed[-1].start()
        copy(5, pdg, rows_b, c, xn).wait_recv()
        passed.append(copy(9, pdg, rows_b, c, sibling))
        passed[-1].start()
        for k, (chip_id, rows) in enumerate([(pxn, rows_all), (pyn, rows_all), (pdg, rows_a), (pdg, rows_b)]):
            copy(6 + k, chip_id, rows, 1 - c, sibling).wait_recv()
        for cp in own + [fwd_a, fwd_b] + passed:
            cp.wait_send()

    out = pl.pallas_call(
        body, out_shape=jax.ShapeDtypeStruct((N_CHIPS, R, C), shard.dtype), in_specs=[_ANY], out_specs=_ANY,
        scratch_shapes=[pltpu.SemaphoreType.DMA((10,)), pltpu.SemaphoreType.DMA((10,))], name=name)(shard)
    chip = 2 * lax.axis_index("x") + lax.axis_index("y")
    return lax.dynamic_update_index_in_dim(out, shard, chip, 0)


_HBM = pl.BlockSpec(memory_space=pltpu.HBM)
_SEM = pl.BlockSpec(memory_space=pltpu.SEMAPHORE)
_EFFECT = pltpu.SideEffectType.DATAFLOW_SIDE_EFFECTING


def _chip_copies(kind, srcs, lands, send_sems, recv_sems):
    x, y, c, chips = _place()
    p = 2 * x + y
    cps = []
    if kind == "sibling":
        for i in range(len(srcs)):
            h = srcs[i].shape[1] // 2
            cps.append(pltpu.make_async_remote_copy(
                src_ref=srcs[i].at[:, pl.ds((1 - c) * h, h), :], dst_ref=lands[i], send_sem=send_sems.at[3 * i],
                recv_sem=recv_sems.at[3 * i], device_id=(x, y, 1 - c), device_id_type=MESH))
        return cps
    for i in range(len(srcs)):
        for j, (cx, cy) in enumerate(chips):
            if kind == "gather":
                src, dst = srcs[i].at[c], lands[i].at[p, c]
            else:
                src, dst = srcs[i].at[2 * cx + cy], lands[i].at[j]
            cps.append(pltpu.make_async_remote_copy(
                src_ref=src, dst_ref=dst, send_sem=send_sems.at[3 * i + j], recv_sem=recv_sems.at[3 * i + j],
                device_id=(cx, cy, c), device_id_type=MESH))
    return cps


def split_start(kind, srcs, land_shapes, after, name):
    n = len(srcs)

    def body(*refs):
        src_refs, land_refs = refs[:n], refs[n:2 * n]
        send_sems, recv_sems = refs[2 * n + 1], refs[2 * n + 2]
        token = refs[-1]
        for cp in _chip_copies(kind, src_refs, land_refs, send_sems, recv_sems):
            cp.start()
        token[...] = jnp.zeros_like(token)

    lands = [pltpu.with_memory_space_constraint(lax.empty(s, BF16), pltpu.HBM) for s in land_shapes]
    outs = pl.pallas_call(
        body, name=name,
        out_shape=(pltpu.SemaphoreType.DMA((3 * n,)), pltpu.SemaphoreType.DMA((3 * n,)),
                   *[pltpu.HBM(s.shape, s.dtype) for s in srcs], *[pltpu.HBM(s, BF16) for s in land_shapes],
                   jax.ShapeDtypeStruct((8, 128), F32)),
        in_specs=[_HBM] * (2 * n) + [_ANY],
        out_specs=(_SEM, _SEM, *([_HBM] * (2 * n)), pl.BlockSpec(memory_space=pltpu.VMEM)),
        input_output_aliases={i: 2 + i for i in range(2 * n)},
        compiler_params=pltpu.CompilerParams(has_side_effects=_EFFECT),
    )(*[pltpu.with_memory_space_constraint(s, pltpu.HBM) for s in srcs], *lands, after)
    return outs[0], outs[1], outs[2:2 + n], outs[2 + n:2 + 2 * n], outs[-1]


def split_wait(kind, send_sems, recv_sems, srcs, lands, after, name):
    n = len(srcs)

    def body(*refs):
        src_refs, land_refs = refs[:n], refs[n:2 * n]
        ssem, rsem = refs[2 * n], refs[2 * n + 1]
        for cp in _chip_copies(kind, src_refs, land_refs, ssem, rsem):
            cp.wait_send()
            cp.wait_recv()

    outs = pl.pallas_call(
        body, name=name,
        out_shape=[pltpu.HBM(s.shape, s.dtype) for s in srcs] + [pltpu.HBM(s.shape, s.dtype) for s in lands],
        in_specs=[_HBM] * (2 * n) + [_SEM, _SEM, _ANY], out_specs=[_HBM] * (2 * n),
        input_output_aliases={i: i for i in range(2 * n)},
        compiler_params=pltpu.CompilerParams(has_side_effects=_EFFECT),
    )(*srcs, *lands, send_sems, recv_sems, after)
    return outs[:n], outs[n:]


def pass_to_sibling(lands):
    n = len(lands)

    def body(*refs):
        ins, outs = refs[:n], refs[n:2 * n]
        send_sems, recv_sems = refs[2 * n:]
        x, y, c, chips = _place()
        cps = []
        for i in range(n):
            for j, (cx, cy) in enumerate(chips):
                blk = outs[i].at[2 * cx + cy, c]
                cps.append(pltpu.make_async_remote_copy(
                    src_ref=ins[i].at[2 * cx + cy, c], dst_ref=blk, send_sem=send_sems.at[3 * i + j],
                    recv_sem=recv_sems.at[3 * i + j], device_id=(x, y, 1 - c), device_id_type=MESH))
        for cp in cps:
            cp.start()
        for cp in cps:
            cp.wait()

    return pl.pallas_call(
        body, out_shape=[jax.ShapeDtypeStruct(t.shape, t.dtype) for t in lands], in_specs=[_ANY] * n,
        out_specs=[_ANY] * n, input_output_aliases={i: i for i in range(n)},
        scratch_shapes=[pltpu.SemaphoreType.DMA((3 * n,)), pltpu.SemaphoreType.DMA((3 * n,))],
        name="ag_pass_to_sibling")(*lands)


def _pass_copies(bufs, send_sems, recv_sems):
    x, y, c, chips = _place()
    cps = []
    for i in range(len(bufs)):
        for j, (cx, cy) in enumerate(chips):
            blk = bufs[i].at[2 * cx + cy, c]
            cps.append(pltpu.make_async_remote_copy(
                src_ref=blk, dst_ref=blk, send_sem=send_sems.at[3 * i + j], recv_sem=recv_sems.at[3 * i + j],
                device_id=(x, y, 1 - c), device_id_type=MESH))
    return cps


def pass_start(bufs, after, name):
    n = len(bufs)

    def body(*refs):
        send_sems, recv_sems = refs[n + 1], refs[n + 2]
        for cp in _pass_copies(refs[:n], send_sems, recv_sems):
            cp.start()
        refs[-1][...] = jnp.zeros_like(refs[-1])

    outs = pl.pallas_call(
        body, name=name,
        out_shape=(pltpu.SemaphoreType.DMA((3 * n,)), pltpu.SemaphoreType.DMA((3 * n,)),
                   *[pltpu.HBM(b.shape, b.dtype) for b in bufs], jax.ShapeDtypeStruct((8, 128), F32)),
        in_specs=[_HBM] * n + [_ANY],
        out_specs=(_SEM, _SEM, *([_HBM] * n), pl.BlockSpec(memory_space=pltpu.VMEM)),
        input_output_aliases={i: 2 + i for i in range(n)},
        compiler_params=pltpu.CompilerParams(has_side_effects=_EFFECT),
    )(*[pltpu.with_memory_space_constraint(b, pltpu.HBM) for b in bufs], after)
    return outs[0], outs[1], outs[2:2 + n], outs[-1]


def pass_wait(send_sems, recv_sems, bufs, after, name):
    n = len(bufs)

    def body(*refs):
        for cp in _pass_copies(refs[:n], refs[n], refs[n + 1]):
            cp.wait_send()
            cp.wait_recv()

    return pl.pallas_call(
        body, name=name, out_shape=[pltpu.HBM(b.shape, b.dtype) for b in bufs],
        in_specs=[_HBM] * n + [_SEM, _SEM, _ANY], out_specs=[_HBM] * n,
        input_output_aliases={i: i for i in range(n)},
        compiler_params=pltpu.CompilerParams(has_side_effects=_EFFECT),
    )(*bufs, send_sems, recv_sems, after)


def exchange_halves_to_sibling(gs, name, by_cols=False):
    n = len(gs)

    def body(*refs):
        ins, outs = refs[:n], refs[n:2 * n]
        send_sems, recv_sems = refs[2 * n:]
        x, y, c, _ = _place()
        cps = []
        for i in range(n):
            if by_cols:
                hc = ins[i].shape[2] // 2
                src = ins[i].at[:, :, pl.ds(pl.multiple_of((1 - c) * hc, 128), hc)]
            else:
                h = ins[i].shape[1] // 2
                src = ins[i].at[:, pl.ds((1 - c) * h, h), :]
            cps.append(pltpu.make_async_remote_copy(
                src_ref=src, dst_ref=outs[i],
                send_sem=send_sems.at[i], recv_sem=recv_sems.at[i], device_id=(x, y, 1 - c), device_id_type=MESH))
        for cp in cps:
            cp.start()
        for cp in cps:
            cp.wait()

    halve = (lambda s: (s[0], s[1], s[2] // 2)) if by_cols else (lambda s: (s[0], s[1] // 2, s[2]))
    return pl.pallas_call(
        body, out_shape=[jax.ShapeDtypeStruct(halve(g.shape), g.dtype) for g in gs],
        in_specs=[_ANY] * n, out_specs=[_ANY] * n,
        scratch_shapes=[pltpu.SemaphoreType.DMA((n,)), pltpu.SemaphoreType.DMA((n,))],
        name=name)(*gs)


def join_halves(rs, name):
    n = len(rs)

    def body(*refs):
        ins, outs = refs[:n], refs[n:2 * n]
        send_sems, recv_sems = refs[2 * n:]
        x, y, c, _ = _place()
        cps = [pltpu.make_async_remote_copy(
            src_ref=ins[i], dst_ref=outs[i], send_sem=send_sems.at[i], recv_sem=recv_sems.at[i],
            device_id=(x, y, 1 - c), device_id_type=MESH) for i in range(n)]
        for cp in cps:
            cp.start()
        for cp in cps:
            cp.wait()

    return pl.pallas_call(
        body, out_shape=[jax.ShapeDtypeStruct(r.shape, r.dtype) for r in rs],
        in_specs=[_ANY] * n, out_specs=[_ANY] * n,
        scratch_shapes=[pltpu.SemaphoreType.DMA((n,)), pltpu.SemaphoreType.DMA((n,))],
        name=name)(*rs)


def _pack(parts, row_mult=8):
    flat = jnp.concatenate([p.reshape(-1).astype(F32) for p in parts])
    unit = row_mult * 128
    n = -(-flat.shape[0] // unit) * unit
    return jnp.pad(flat, (0, n - flat.shape[0])).reshape(n // 128, 128)


def _unpack(flat, shapes):
    out, off = [], 0
    for s in shapes:
        n = int(np.prod(s))
        out.append(flat[off:off + n].reshape(s))
        off += n
    return out


def _gather_packed(parts, name):
    packed = _pack(parts)
    g = allgather_small(packed, name).reshape(N_DEV, -1)
    return _unpack_rows(g, [p.shape for p in parts])


def _unpack_rows(g, shapes):
    out, off = [], 0
    for s in shapes:
        n = int(np.prod(s))
        out.append(g[:, off:off + n].reshape((g.shape[0],) + tuple(s)))
        off += n
    return out


def _by_chip(t, axis):
    return jnp.concatenate([t[2 * p] for p in range(N_CHIPS)], axis=axis)


def kernel(x, c, ada_w, ada_b, ln_g, ln_b, a_in_w, a_conv_w, a_conv_b, a_dt_bias, a_A_log, a_D, a_norm_g, a_out_w, kv_w, b_in_w, b_out_w, loss_target, m_ada_w, m_ada_b, m_ln_g, m_ln_b, m_a_in_w, m_a_conv_w, m_a_conv_b, m_a_dt_bias, m_a_A_log, m_a_D, m_a_norm_g, m_a_out_w, m_kv_w, m_b_in_w, m_b_out_w, v_ada_w, v_ada_b, v_ln_g, v_ln_b, v_a_in_w, v_a_conv_w, v_a_conv_b, v_a_dt_bias, v_a_A_log, v_a_D, v_a_norm_g, v_a_out_w, v_kv_w, v_b_in_w, v_b_out_w):
    ax, ay, ac = lax.axis_index("x"), lax.axis_index("y"), lax.axis_index("c")
    chip = 2 * ax + ay
    dev = 4 * ax + 2 * ay + ac
    xin = x[0]
    tgt = loss_target[0]
    L, D = xin.shape
    G, P = SSD_G, SSD_P
    H = a_dt_bias.shape[1]
    Kh = H // G
    DI = H * P
    CONVD = a_conv_b.shape[1] * N_CHIPS
    HW = DIL_H * DIL_E
    Ws = ada_w.shape[2]

    w_in_g = allgather_routed(jnp.transpose(a_in_w[0]).astype(BF16), "allgather_w_in")
    later = [a_out_w[0].astype(BF16), kv_w.astype(BF16), b_in_w[0].astype(BF16), b_out_w[0].astype(BF16)]
    later_split = [s.reshape(2, s.shape[0] // 2, s.shape[1]) for s in later]
    ag_ssem, ag_rsem, ag_srcs, ag_lands, ag_token = split_start(
        "gather", later_split, [(N_CHIPS,) + s.shape for s in later_split], w_in_g, "ag_later_start")
    w_in_t = w_in_g.reshape(-1, D)
    w_dt_t = jnp.pad(w_in_t[DI + CONVD:], ((0, 128 - H), (0, 0)))

    c8, cw8, cb8, ng8 = _gather_packed([c[0], a_conv_w[0], a_conv_b[0], a_norm_g[0]], "allgather_small_params")
    conv_w = _by_chip(cw8, 1)
    conv_b = _by_chip(cb8, 0).reshape(1, CONVD)
    norm_g = _by_chip(ng8, 0).reshape(1, DI)

    mod_s = ada_fwd(c8, ada_w)
    (mod8,) = _gather_packed([mod_s], "allgather_small_mod")
    mods = _by_chip(mod8, 2)
    mod = lax.dynamic_index_in_dim(mods, dev, axis=1, keepdims=False) + ada_b
    shift = [mod[l:l + 1, :D] for l in range(DEPTH)]
    scale = [mod[l:l + 1, D:2 * D] for l in range(DEPTH)]
    gate = [mod[l:l + 1, 2 * D:] for l in range(DEPTH)]
    lg = [ln_g[l:l + 1] for l in range(DEPTH)]
    lb = [ln_b[l:l + 1] for l in range(DEPTH)]

    h0 = modulate(xin, scale[0] + ag_token[0:1, 0:1], shift[0], "modulate0")
    zx = mm_nt(h0, w_in_t, BF16, "mm_in_zx", kw_rows=DI + CONVD)
    dtp = mm_nt(h0, w_dt_t, F32, "mm_in_dt")
    xbc = conv_fwd(zx, DI, conv_w, conv_b)
    dtp_g = jnp.transpose(dtp[:, :H].reshape(L, G, Kh), (1, 0, 2))
    dtp_gT = jnp.transpose(dtp_g, (0, 2, 1))
    vecs = [a_dt_bias.reshape(G, 1, Kh), a_dt_bias.reshape(G, Kh, 1), a_A_log.reshape(G, 1, Kh),
            a_A_log.reshape(G, Kh, 1), a_D.reshape(G, 1, Kh), a_D.reshape(G, Kh, 1)]
    y_ssd, states, yn = ssd_fwd(xbc, dtp_g, dtp_gT, *vecs, zx, norm_g, DI)
    later_split, ag_lands = split_wait("gather", ag_ssem, ag_rsem, ag_srcs, ag_lands, yn, "ag_later_wait")
    (land_out,) = pass_to_sibling(ag_lands[:1])
    ps_ssem, ps_rsem, lands_b, ps_token = pass_start(ag_lands[1:], land_out, "ag_pass_start")

    def place_own(o, s, full):
        return lax.dynamic_update_index_in_dim(o, s, chip, 0).reshape((N_CHIPS,) + full.shape)

    w_out_g = place_own(land_out, later_split[0], later[0])
    ymix0 = mm_nn(yn, w_out_g.reshape(-1, D), F32, "mm_out_a", after=ps_token)
    x1, x1b, h1 = ln_mid(xin, ymix0, gate[0], lg[0], lb[0], scale[1], shift[1])
    lands_b = pass_wait(ps_ssem, ps_rsem, lands_b, x1b, "ag_pass_wait")
    w_kv_g, w_bin_g, w_bout_g = [place_own(o, s, full) for o, s, full in zip(lands_b, later_split[1:], later[1:])]

    n_grp = len(DIL_PATTERNS)
    cb = HW // 512
    assert w_bin_g.shape[2] == HW
    kv3 = [mm_cols_dilated(x1b, w_kv_g, [g * cb + t for t in range(cb)] + [(n_grp + g) * cb + t for t in range(cb)],
                           DIL_PATTERNS[g][1], f"mm_kv_{g}") for g in range(n_grp)]
    q3 = [mm_cols_dilated(h1, w_bin_g, [g], DIL_PATTERNS[g][1], f"mm_q_{g}", tn=HW) for g in range(n_grp)]
    z_b = mm_nn(h1, w_bin_g[n_grp], BF16, "mm_z_b")
    os_, lses = [], []
    for gi in range(len(DIL_PATTERNS)):
        o, lse = attn_fwd(q3[gi], kv3[gi], gi)
        os_.append(o)
        lses.append(lse)
    om = merge_fwd(os_, lses, z_b)
    ymix1 = mm_nn(om, w_bout_g, F32, "mm_out_b", stack="col")
    dres2, dy2, dg1, db1, dgate1, sq = ln_final_fwd_bwd(x1, ymix1, gate[1], lg[1], lb[1], tgt)
    loss_part = 0.5 * jnp.sum(sq) / D

    g_bout = mm_tn(om, dy2, BF16, "mm_gw_out_b", stack="col")
    dgated = mm_nt(dy2, w_bout_g, BF16, "mm_gx_out_b", stack="col")
    dos, dprs, dz_b = merge_bwd(dgated, os_, lses, z_b)
    dqs, dks, dvs = [], [], []
    for gi in range(len(DIL_PATTERNS)):
        dq, dk, dv = attn_bwd(q3[gi], kv3[gi], dos[gi], lses[gi], dprs[gi], gi)
        dqs.append(dq)
        dks.append(dk)
        dvs.append(dv)
    dqz = jnp.concatenate(dqs + [dz_b], axis=1)
    dkv = jnp.concatenate(dks + dvs, axis=1)
    g_bin = mm_tn(h1, dqz, BF16, "mm_gw_in_b", stack="col")
    dh1 = mm_nt(dqz, w_bin_g, BF16, "mm_gx_in_b", stack="col")
    g_kv = mm_tn(x1b, dkv, BF16, "mm_gw_kv", stack="col")

    core = ac.astype(jnp.int32).reshape(1)
    chip_i = chip.astype(jnp.int32).reshape(1)

    def begin_exchange(gs, tag):
        shapes = [(g.shape[0], g.shape[1] // 2, g.shape[2]) for g in gs]
        return split_start("sibling", gs, shapes, gs[0], "rs_x%s_start" % tag)

    def begin_scatter(gs, nms, tag, exchange=None, after=None, by_cols=False):
        if exchange is None:
            sib = exchange_halves_to_sibling(gs, "rs_sibling_exchange_" + tag, by_cols=by_cols)
        else:
            gs, sib = split_wait("sibling", exchange[0], exchange[1], exchange[2], exchange[3], after,
                                 "rs_x%s_wait" % tag)
        parts = [add_half(g, a, core, "rs_add_" + nm, by_cols=by_cols) for g, a, nm in zip(gs, sib, nms)]
        return split_start("scatter", parts, [(3,) + t.shape[1:] for t in parts], parts[0], "rs_%s_start" % tag)

    def finish_scatter(handles, after, tag):
        nms, owns, landed = [], [], []
        for k, (handle, hn) in enumerate(handles):
            parts, lands = split_wait("scatter", handle[0], handle[1], handle[2], handle[3], after,
                                      "rs_%s%d_wait" % (tag, k))
            nms += hn
            owns += list(parts)
            landed += list(lands)
        halves = [sum_partials(own, t, chip_i, "rs_sum_" + nm) for own, t, nm in zip(owns, landed, nms)]
        theirs = join_halves(halves, "rs_join_halves_" + tag)
        return dict(zip(nms, zip(halves, theirs)))

    names_b = ["kv", "in_b", "out_b"]
    ex_b = begin_exchange([g_kv, g_bin, g_bout], "b")
    dx1_kv = mm_nt(dkv, w_kv_g, BF16, "mm_gx_kv", stack="col", after=ex_b[4])
    rs_b = begin_scatter(None, names_b, "b", exchange=ex_b, after=dx1_kv)

    dres1, dy1, dg0, db0, dgate0, dscale1, dshift1 = mod_ln_bwd(
        dres2, dh1, dx1_kv, x1, scale[1], xin, ymix0, gate[0] + rs_b[4][0:1, 0:1], lg[0])
    g_out = mm_tn(yn, dy1, BF16, "mm_gw_out_a", stack="row")
    ex_a1 = begin_exchange([g_out], "a1")
    dyn = mm_nt(dy1, w_out_g, BF16, "mm_gx_out_a", stack="row", after=ex_a1[4])
    rs_a1 = begin_scatter(None, ["out_a"], "a1", exchange=ex_a1, after=dyn)
    dxs, dB, dC, ddtp_g, dbias_g, dalog_g, dD_g, dz_a, dnorm_g = ssd_bwd(
        xbc, dtp_g, dtp_gT, *vecs, states, dyn, y_ssd, zx, norm_g + rs_a1[4][0:1, 0:1], DI)
    dzx, dws, dbs, lo = dz_a, [], [], 0
    for tag, gpart in (("xs", dxs), ("b", dB), ("c", dC)):
        hi = lo + gpart.shape[1]
        dzx, dw_p, db_p = conv_bwd(zx, DI + lo, conv_w[:, lo:hi], conv_b[:, lo:hi], gpart, dzx, "conv_bwd_" + tag)
        dws.append(dw_p)
        dbs.append(db_p)
        lo = hi
    dconv_w = jnp.concatenate(dws, axis=1)
    dconv_b = jnp.concatenate(dbs, axis=1)
    ddtp = jnp.pad(jnp.transpose(ddtp_g, (1, 0, 2)).reshape(L, H), ((0, 0), (0, 128 - H)))
    g_inT = mm_tn(dzx, h0, BF16, "mm_gw_in_zx", m_rows=DI + CONVD + H)
    g_dtT = mm_tn(ddtp, h0, BF16, "mm_gw_in_dt")
    g_inT = lax.dynamic_update_slice(g_inT, g_dtT[:H], (DI + CONVD, 0))
    rs_a2 = begin_scatter([g_inT.reshape(N_CHIPS, -1, D)], ["in_a"], "a2", by_cols=True)
    dh0 = mm_nn(dzx, w_in_t, BF16, "mm_gx_in_zx", after=rs_a2[4])
    dh0_dt = mm_nn(ddtp, w_dt_t, F32, "mm_gx_in_dt")
    grad_x, dscale0, dshift0 = mod_bwd(dres1, dh0, dh0_dt, xin, scale[0] + rs_a2[4][0:1, 0:1], "mod_bwd0")
    g_halves = finish_scatter([(rs_b, names_b)], grad_x, "b")

    def step_halves(w, m, v, nm):
        shp = w.shape
        mine, theirs_ = g_halves[nm]
        outs4 = adamw_halves(w.reshape(-1, shp[-1]), mine, theirs_, m.reshape(-1, shp[-1]), v.reshape(-1, shp[-1]),
                             core, "adamw_" + nm)
        return tuple(t.reshape(shp) for t in outs4)

    big = {
        "kv_w": step_halves(kv_w, m_kv_w, v_kv_w, "kv"),
        "b_in_w": step_halves(b_in_w, m_b_in_w, v_b_in_w, "in_b"),
        "b_out_w": step_halves(b_out_w, m_b_out_w, v_b_out_w, "out_b"),
    }
    g_halves.update(finish_scatter([(rs_a1, ["out_a"]), (rs_a2, ["in_a"])], big["kv_w"][1], "a"))
    g_halves["in_a"] = tuple(jnp.transpose(t) for t in g_halves["in_a"])
    big["a_in_w"] = step_halves(a_in_w, m_a_in_w, v_a_in_w, "in_a")
    big["a_out_w"] = step_halves(a_out_w, m_a_out_w, v_a_out_w, "out_a")

    dmod = jnp.concatenate([jnp.concatenate([dshift0, dscale0, dgate0], axis=1),
                            jnp.concatenate([dshift1, dscale1, dgate1], axis=1)], axis=0)
    small_parts = [jnp.concatenate([dg0, dg1], axis=0), jnp.concatenate([db0, db1], axis=0),
                   dbias_g.reshape(1, H), dalog_g.reshape(1, H), dD_g.reshape(1, H),
                   dconv_w, dconv_b, dnorm_g, loss_part.reshape(1, 1)]
    small_shapes = [p.shape for p in small_parts]
    packed = jnp.concatenate([_pack([dmod]), _pack(small_parts)], axis=0)
    n_mod_rows = _pack([dmod]).shape[0]
    gathered = allgather_small(packed, "allgather_small_grads", after=g_halves["in_a"][1]).reshape(N_DEV, -1, 128)
    dmod8 = gathered[:, :n_mod_rows].reshape(N_DEV, -1)[:, :2 * 3 * D].reshape(N_DEV, DEPTH, 3 * D)
    summed = sum_leading(gathered, "sum_small")
    g_ada_b = summed[:n_mod_rows].reshape(-1)[:2 * 3 * D].reshape(DEPTH, 3 * D)
    (g_ln_g, g_ln_b, g_dt_bias, g_a_log, g_dsk, g_conv_w, g_conv_b, g_norm_g, loss_all) = _unpack(
        summed[n_mod_rows:].reshape(-1), small_shapes)
    loss = loss_all.reshape(())
    Cs = CONVD // N_CHIPS
    g_conv_w_s = lax.dynamic_slice_in_dim(g_conv_w, chip * Cs, Cs, axis=1)
    g_conv_b_s = lax.dynamic_slice_in_dim(g_conv_b, chip * Cs, Cs, axis=1)
    g_norm_g_s = lax.dynamic_slice_in_dim(g_norm_g, chip * (DI // N_CHIPS), DI // N_CHIPS, axis=1)
    dmod_s = jnp.transpose(lax.dynamic_slice_in_dim(dmod8, chip * Ws, Ws, axis=2), (1, 0, 2))

    def step2d(w, g, m, v, nm):
        shp = w.shape
        d_, m_, v_ = adamw(w.reshape(-1, shp[-1]), g.reshape(-1, shp[-1]), m.reshape(-1, shp[-1]),
                           v.reshape(-1, shp[-1]), "adamw_" + nm)
        return g.reshape(shp), d_.reshape(shp), m_.reshape(shp), v_.reshape(shp)

    big["ada_w"] = step2d(ada_w, ada_wgrad(jnp.transpose(c8), dmod_s), m_ada_w, v_ada_w, "ada_w")
    small_names = ["ada_b", "ln_g", "ln_b", "a_conv_w", "a_conv_b", "a_dt_bias", "a_A_log", "a_D", "a_norm_g"]
    small_w = [ada_b, ln_g, ln_b, a_conv_w, a_conv_b, a_dt_bias, a_A_log, a_D, a_norm_g]
    small_m = [m_ada_b, m_ln_g, m_ln_b, m_a_conv_w, m_a_conv_b, m_a_dt_bias, m_a_A_log, m_a_D, m_a_norm_g]
    small_v = [v_ada_b, v_ln_g, v_ln_b, v_a_conv_w, v_a_conv_b, v_a_dt_bias, v_a_A_log, v_a_D, v_a_norm_g]
    small_g = [g_ada_b, g_ln_g, g_ln_b, g_conv_w_s, g_conv_b_s, g_dt_bias, g_a_log, g_dsk, g_norm_g_s]
    shapes = [w.shape for w in small_w]
    small_g = [g.reshape(s) for g, s in zip(small_g, shapes)]
    d_p, m_p, v_p = adamw(_pack(small_w), _pack(small_g), _pack(small_m), _pack(small_v), "adamw_small")
    small = {}
    for nm, g, d_, m_, v_ in zip(small_names, small_g, _unpack(d_p.reshape(-1), shapes), _unpack(m_p.reshape(-1), shapes),
                                 _unpack(v_p.reshape(-1), shapes)):
        small[nm] = (g, d_, m_, v_)
    allw = {**big, **small}
    order = ["ada_w", "ada_b", "ln_g", "ln_b", "a_in_w", "a_conv_w", "a_conv_b", "a_dt_bias", "a_A_log", "a_D",
             "a_norm_g", "a_out_w", "kv_w", "b_in_w", "b_out_w"]
    outs = [loss, grad_x.reshape(x.shape)]
    for k in range(4):
        outs += [allw[n][k] for n in order]
    return tuple(outs)
```

```python
import functools

import jax
import jax.numpy as jnp
import numpy as np
from jax import lax
from jax.experimental import pallas as pl
from jax.experimental.pallas import tpu as pltpu

F32 = jnp.float32
BF16 = jnp.bfloat16
MESH = pl.DeviceIdType.MESH

DEPTH = 2
ALPHA = (2 * DEPTH) ** 0.25
LN_EPS = 1e-5
RMS_EPS = 1e-5
SSD_P = 64
SSD_N = 128
SSD_Q = 256
SSD_G = 8
CONV_W = 4
DIL_PATTERNS = ((128, 1), (512, 4), (2048, 16))
DIL_H = 8
DIL_E = 128
DIL_BLK = 128
ADAM_LR, ADAM_B1, ADAM_B2, ADAM_EPS, ADAM_WD, ADAM_STEP = 0.001, 0.9, 0.999, 1e-08, 0.01, 10

VMEM_LIMIT = 56 * 1024 * 1024
N_CHIPS = 4
N_DEV = 8


def _tile(dim, target, mult=128):
    if dim <= target:
        return dim
    t = (target // mult) * mult
    while t >= mult:
        if dim % t == 0:
            return t
        t -= mult
    return dim


def _cp(sem):
    return pltpu.CompilerParams(dimension_semantics=sem, vmem_limit_bytes=VMEM_LIMIT)


def _sigmoid(x):
    return 1.0 / (1.0 + jnp.exp(-x))


def _silu(x):
    return x * _sigmoid(x)


def _dsilu(x):
    s = _sigmoid(x)
    return s * (1.0 + x * (1.0 - s))


def _softplus(x):
    return jnp.maximum(x, 0.0) + jnp.log(1.0 + jnp.exp(-jnp.abs(x)))


def _mm_call(a, b, out_shape, grid, a_spec, b_spec, o_spec, acc_shape, dims, name, after=None):
    nk = grid[2]
    extra = [] if after is None else [after]

    def prod(a_ref, b_ref):
        return lax.dot_general(a_ref[...].astype(BF16), b_ref[...].astype(BF16), (dims, ((), ())),
                               preferred_element_type=F32)

    def body_single(a_ref, b_ref, *rest):
        o_ref = rest[len(extra)]
        o_ref[...] = prod(a_ref, b_ref).astype(o_ref.dtype)

    def body_multi(a_ref, b_ref, *rest):
        o_ref, acc_ref = rest[len(extra):]
        k = pl.program_id(2)

        @pl.when(k == 0)
        def _():
            acc_ref[...] = prod(a_ref, b_ref)

        @pl.when(jnp.logical_and(k > 0, k < nk - 1))
        def _():
            acc_ref[...] += prod(a_ref, b_ref)

        @pl.when(k == nk - 1)
        def _():
            o_ref[...] = (acc_ref[...] + prod(a_ref, b_ref)).astype(o_ref.dtype)

    return pl.pallas_call(
        body_single if nk == 1 else body_multi, grid=grid, in_specs=[a_spec, b_spec] + [_ANY] * len(extra),
        out_specs=o_spec, out_shape=out_shape, scratch_shapes=[] if nk == 1 else [pltpu.VMEM(acc_shape, F32)],
        compiler_params=_cp(("parallel", "parallel", "arbitrary")), name=name)(a, b, *extra)


def mm_nn(a, b, out_dtype, name, stack=None, tm=1024, tn=1024, tk=2048, n_cols=None, after=None):
    M, K = a.shape
    if stack is None:
        N = b.shape[1] if n_cols is None else n_cols
        tn, tk = _tile(N, tn), _tile(K, tk)
        b_spec = pl.BlockSpec((tk, tn), lambda i, j, k: (k, j))
    elif stack == "col":
        S, _, Ns = b.shape
        N = S * Ns
        tn, tk = _tile(Ns, tn), _tile(K, tk)
        npb = Ns // tn
        b_spec = pl.BlockSpec((None, tk, tn), lambda i, j, k: (j // npb, k, j % npb))
    else:
        S, Ks, N = b.shape
        tn, tk = _tile(N, tn), _tile(Ks, tk)
        kpb = Ks // tk
        b_spec = pl.BlockSpec((None, tk, tn), lambda i, j, k: (k // kpb, k % kpb, j))
    tm = _tile(M, tm)
    return _mm_call(a, b, jax.ShapeDtypeStruct((M, N), out_dtype), (M // tm, N // tn, K // tk),
                    pl.BlockSpec((tm, tk), lambda i, j, k: (i, k)), b_spec,
                    pl.BlockSpec((tm, tn), lambda i, j, k: (i, j)), (tm, tn), ((1,), (0,)), name, after=after)


def mm_cols_dilated(a, b, gcols, d, name, tm=1024, tn=512):
    L, K = a.shape
    S, _, Ns = b.shape
    tm, tn = _tile(L, tm), _tile(Ns, tn)
    npb = Ns // tn
    nj = len(gcols)
    rows = tm // d

    def body(cols_ref, a_ref, b_ref, o_ref, *scr):
        prod = jnp.dot(a_ref[...], b_ref[...], preferred_element_type=F32)
        if d == 1:
            o_ref[0] = prod.astype(BF16)
        else:
            for c in range(tn // 128):
                scr[0][c] = prod[:, c * 128:(c + 1) * 128]
            for r in range(d):
                for c in range(tn // 128):
                    o_ref[r, :, c * 128:(c + 1) * 128] = scr[0].at[c][pl.ds(r, rows, stride=d), :].astype(BF16)

    return pl.pallas_call(
        body,
        grid_spec=pltpu.PrefetchScalarGridSpec(
            num_scalar_prefetch=1, grid=(L // tm, nj),
            in_specs=[pl.BlockSpec((tm, K), lambda i, j, c: (i, 0)),
                      pl.BlockSpec((None, K, tn), lambda i, j, c: (c[j] // npb, 0, c[j] % npb))],
            out_specs=pl.BlockSpec((d, rows, tn), lambda i, j, c: (0, i, j)),
            scratch_shapes=[] if d == 1 else [pltpu.VMEM((tn // 128, tm, 128), F32)]),
        out_shape=jax.ShapeDtypeStruct((d, L // d, nj * tn), BF16),
        compiler_params=_cp(("parallel", "arbitrary")), name=name)(jnp.asarray(gcols, jnp.int32), a, b)


def mm_nt(a, b, out_dtype, name, stack=None, tm=1024, tn=1024, tk=2048, after=None, kw_rows=None):
    M, C = a.shape
    if stack is None:
        Kw = b.shape[0] if kw_rows is None else kw_rows
        tn, tk = _tile(Kw, tn), _tile(C, tk)
        b_spec = pl.BlockSpec((tn, tk), lambda i, j, k: (j, k))
    elif stack == "col":
        S, Kw, Cs = b.shape
        tn, tk = _tile(Kw, tn), _tile(Cs, tk)
        cpb = Cs // tk
        b_spec = pl.BlockSpec((None, tn, tk), lambda i, j, k: (k // cpb, j, k % cpb))
    else:
        S, Ks, _ = b.shape
        Kw = S * Ks
        tn, tk = _tile(Ks, tn), _tile(C, tk)
        jpb = Ks // tn
        b_spec = pl.BlockSpec((None, tn, tk), lambda i, j, k: (j // jpb, j % jpb, k))
    tm = _tile(M, tm)
    return _mm_call(a, b, jax.ShapeDtypeStruct((M, Kw), out_dtype), (M // tm, Kw // tn, C // tk),
                    pl.BlockSpec((tm, tk), lambda i, j, k: (i, k)), b_spec,
                    pl.BlockSpec((tm, tn), lambda i, j, k: (i, j)), (tm, tn), ((1,), (1,)), name, after=after)


def mm_tn(a, b, out_dtype, name, stack=None, n_stack=N_CHIPS, tm=1024, tn=1024, tk=2048, m_rows=None):
    L, M = a.shape
    N = b.shape[1]
    tk = _tile(L, tk)
    if stack is None:
        tm, tn = _tile(M, tm), _tile(N, tn)
        o_spec = pl.BlockSpec((tm, tn), lambda i, j, k: (i, j))
        out_shape = (M if m_rows is None else m_rows, N)
    elif stack == "col":
        Ns = N // n_stack
        tm, tn = _tile(M, tm), _tile(Ns, tn)
        npb = Ns // tn
        o_spec = pl.BlockSpec((None, tm, tn), lambda i, j, k: (j // npb, i, j % npb))
        out_shape = (n_stack, M, Ns)
    else:
        Ms = M // n_stack
        tm, tn = _tile(Ms, tm), _tile(N, tn)
        mpb = Ms // tm
        o_spec = pl.BlockSpec((None, tm, tn), lambda i, j, k: (i // mpb, i % mpb, j))
        out_shape = (n_stack, Ms, N)
    return _mm_call(a, b, jax.ShapeDtypeStruct(out_shape, out_dtype), (M // tm, N // tn, L // tk),
                    pl.BlockSpec((tk, tm), lambda i, j, k: (k, i)), pl.BlockSpec((tk, tn), lambda i, j, k: (k, j)),
                    o_spec, (tm, tn), ((0,), (0,)), name)


def _row_specs(tr, widths):
    return [pl.BlockSpec((tr, w), lambda i: (i, 0)) for w in widths]


def _vec_spec(w):
    return pl.BlockSpec((1, w), lambda i: (0, 0))


def _acc_rows(ref, val, i):
    s = jnp.sum(val, axis=0, keepdims=True)

    @pl.when(i == 0)
    def _():
        ref[...] = s

    @pl.when(i > 0)
    def _():
        ref[...] += s


def modulate(x, scale, shift, name):
    L, D = x.shape
    tr = _tile(L, 512, 16)

    def body(x_ref, sc_ref, sh_ref, h_ref):
        h_ref[...] = (x_ref[...] * (1.0 + sc_ref[...]) + sh_ref[...]).astype(BF16)

    return pl.pallas_call(
        body, grid=(L // tr,), in_specs=_row_specs(tr, [D]) + [_vec_spec(D)] * 2, out_specs=_row_specs(tr, [D])[0],
        out_shape=jax.ShapeDtypeStruct((L, D), BF16), compiler_params=_cp(("parallel",)), name=name)(x, scale, shift)


def _ln_core(x, y, gate, g, b):
    u = ALPHA * x + (1.0 + gate) * y
    mu = jnp.mean(u, axis=-1, keepdims=True)
    d = u - mu
    var = jnp.mean(d * d, axis=-1, keepdims=True)
    rstd = lax.rsqrt(var + LN_EPS)
    xhat = d * rstd
    return xhat * g + b, xhat, rstd


def ln_mid(x, y, gate, g, b, scale, shift):
    L, D = x.shape
    tr = _tile(L, 256, 16)

    def body(x_ref, y_ref, gate_ref, g_ref, b_ref, sc_ref, sh_ref, x1_ref, x1b_ref, h_ref):
        x1, _, _ = _ln_core(x_ref[...], y_ref[...], gate_ref[...], g_ref[...], b_ref[...])
        x1_ref[...] = x1
        x1b_ref[...] = x1.astype(BF16)
        h_ref[...] = (x1 * (1.0 + sc_ref[...]) + sh_ref[...]).astype(BF16)

    return pl.pallas_call(
        body, grid=(L // tr,), in_specs=_row_specs(tr, [D, D]) + [_vec_spec(D)] * 5,
        out_specs=_row_specs(tr, [D, D, D]),
        out_shape=[jax.ShapeDtypeStruct((L, D), F32), jax.ShapeDtypeStruct((L, D), BF16),
                   jax.ShapeDtypeStruct((L, D), BF16)],
        compiler_params=_cp(("parallel",)), name="ln_mid")(x, y, gate, g, b, scale, shift)


def _ln_bwd_rows(dout_v, xhat, rstd, g):
    dxh = dout_v * g
    m1 = jnp.mean(dxh, axis=-1, keepdims=True)
    m2 = jnp.mean(dxh * xhat, axis=-1, keepdims=True)
    return rstd * (dxh - m1 - xhat * m2)


def ln_final_fwd_bwd(x, y, gate, g, b, target):
    L, D = x.shape
    tr = _tile(L, 256, 16)

    def body(x_ref, y_ref, gate_ref, g_ref, b_ref, t_ref, dres_ref, dy_ref, dg_ref, db_ref, dgate_ref, sq_ref):
        i = pl.program_id(0)
        yv = y_ref[...]
        out, xhat, rstd = _ln_core(x_ref[...], yv, gate_ref[...], g_ref[...], b_ref[...])
        err = out - t_ref[...]
        dout_v = err * (1.0 / D)
        du = _ln_bwd_rows(dout_v, xhat, rstd, g_ref[...])
        dres_ref[...] = ALPHA * du
        dy_ref[...] = ((1.0 + gate_ref[...]) * du).astype(BF16)
        _acc_rows(dg_ref, dout_v * xhat, i)
        _acc_rows(db_ref, dout_v, i)
        _acc_rows(dgate_ref, du * yv, i)
        _acc_rows(sq_ref, err * err, i)

    return pl.pallas_call(
        body, grid=(L // tr,), in_specs=_row_specs(tr, [D, D]) + [_vec_spec(D)] * 3 + _row_specs(tr, [D]),
        out_specs=_row_specs(tr, [D, D]) + [_vec_spec(D)] * 4,
        out_shape=[jax.ShapeDtypeStruct((L, D), F32), jax.ShapeDtypeStruct((L, D), BF16)]
        + [jax.ShapeDtypeStruct((1, D), F32)] * 4,
        compiler_params=_cp(("arbitrary",)), name="ln_final_fwd_bwd")(x, y, gate, g, b, target)


def mod_bwd(dres, dh, dh2, xin, scale, name):
    L, D = xin.shape
    tr = _tile(L, 256, 16)

    def body(dres_ref, dh_ref, dh2_ref, x_ref, sc_ref, dx_ref, dsc_ref, dsh_ref):
        i = pl.program_id(0)
        dh_v = dh_ref[...].astype(F32) + dh2_ref[...].astype(F32)
        dx_ref[...] = dres_ref[...] + dh_v * (1.0 + sc_ref[...])
        _acc_rows(dsc_ref, dh_v * x_ref[...], i)
        _acc_rows(dsh_ref, dh_v, i)

    return pl.pallas_call(
        body, grid=(L // tr,), in_specs=_row_specs(tr, [D, D, D, D]) + [_vec_spec(D)],
        out_specs=_row_specs(tr, [D]) + [_vec_spec(D)] * 2,
        out_shape=[jax.ShapeDtypeStruct((L, D), F32)] + [jax.ShapeDtypeStruct((1, D), F32)] * 2,
        compiler_params=_cp(("arbitrary",)), name=name)(dres, dh, dh2, xin, scale)


def mod_ln_bwd(dres_in, dh, dskip, xmid, scale, x, y, gate, g):
    L, D = x.shape
    tr = _tile(L, 256, 16)

    def body(dres_ref, dh_ref, dskip_ref, xm_ref, sc_ref, x_ref, y_ref, gate_ref, g_ref,
             dres_out, dy_ref, dg_ref, db_ref, dgate_ref, dsc_ref, dsh_ref):
        i = pl.program_id(0)
        dh_v = dh_ref[...].astype(F32)
        dout_v = dres_ref[...] + dskip_ref[...].astype(F32) + dh_v * (1.0 + sc_ref[...])
        _acc_rows(dsc_ref, dh_v * xm_ref[...], i)
        _acc_rows(dsh_ref, dh_v, i)
        yv = y_ref[...]
        _, xhat, rstd = _ln_core(x_ref[...], yv, gate_ref[...], g_ref[...], 0.0)
        du = _ln_bwd_rows(dout_v, xhat, rstd, g_ref[...])
        dres_out[...] = ALPHA * du
        dy_ref[...] = ((1.0 + gate_ref[...]) * du).astype(BF16)
        _acc_rows(dg_ref, dout_v * xhat, i)
        _acc_rows(db_ref, dout_v, i)
        _acc_rows(dgate_ref, du * yv, i)

    return pl.pallas_call(
        body, grid=(L // tr,),
        in_specs=_row_specs(tr, [D] * 4) + [_vec_spec(D)] + _row_specs(tr, [D, D]) + [_vec_spec(D)] * 2,
        out_specs=_row_specs(tr, [D, D]) + [_vec_spec(D)] * 5,
        out_shape=[jax.ShapeDtypeStruct((L, D), F32), jax.ShapeDtypeStruct((L, D), BF16)]
        + [jax.ShapeDtypeStruct((1, D), F32)] * 5,
        compiler_params=_cp(("arbitrary",)), name="mod_ln_bwd")(dres_in, dh, dskip, xmid, scale, x, y, gate, g)


CONV_HALO = 16


def _conv_rows(x_ref, i, tr, L):
    nblk = L // tr
    s = pl.multiple_of(i * tr, CONV_HALO)
    cur = x_ref[pl.ds(s, tr), :].astype(F32)
    sp = pl.multiple_of(jnp.maximum(i * tr - CONV_HALO, 0), CONV_HALO)
    sn = pl.multiple_of(jnp.minimum(i * tr + tr, L - CONV_HALO), CONV_HALO)
    prev = x_ref[pl.ds(sp, CONV_HALO), :].astype(F32) * (i > 0).astype(F32)
    nxt = x_ref[pl.ds(sn, CONV_HALO), :].astype(F32) * (i < nblk - 1).astype(F32)
    return jnp.concatenate([prev, cur, nxt], axis=0)


def _shift_rows(v, j):
    n = v.shape[0]
    return v if j % n == 0 else pltpu.roll(v, j % n, 0)


def _conv_taps(xe):
    return [_shift_rows(xe, CONV_W - 1 - k) for k in range(CONV_W)]


def _conv_eval(taps, w_ref, b_ref):
    c = b_ref[...] + w_ref[0:1, :] * taps[0]
    for k in range(1, CONV_W):
        c = c + w_ref[k:k + 1, :] * taps[k]
    return c


def conv_fwd(zx, col0, conv_w, conv_b):
    L = zx.shape[0]
    C = conv_w.shape[1]
    tc = _tile(C, 512)
    tr = _tile(L, 512, CONV_HALO)
    off = col0 // tc

    def body(x_ref, w_ref, b_ref, o_ref):
        i = pl.program_id(1)
        xe = _conv_rows(x_ref, i, tr, L)
        c = _conv_eval(_conv_taps(xe), w_ref, b_ref)[CONV_HALO:CONV_HALO + tr]
        o_ref[...] = _silu(c).astype(BF16)

    return pl.pallas_call(
        body, grid=(C // tc, L // tr),
        in_specs=[pl.BlockSpec((L, tc), lambda j, i: (0, off + j)), pl.BlockSpec((CONV_W, tc), lambda j, i: (0, j)),
                  pl.BlockSpec((1, tc), lambda j, i: (0, j))],
        out_specs=pl.BlockSpec((tr, tc), lambda j, i: (i, j)),
        out_shape=jax.ShapeDtypeStruct((L, C), BF16), compiler_params=_cp(("parallel", "arbitrary")),
        name="conv_fwd")(zx, conv_w, conv_b)


def conv_bwd(zx, col0, conv_w, conv_b, g, dzx, name):
    L = zx.shape[0]
    C = conv_w.shape[1]
    tc = _tile(C, 512)
    tr = _tile(L, 512, CONV_HALO)
    off = col0 // tc
    H = CONV_HALO

    def body(x_ref, g_ref, w_ref, b_ref, buf_ref, dx_ref, dw_ref, db_ref):
        i = pl.program_id(1)
        xe = _conv_rows(x_ref, i, tr, L)
        ge = _conv_rows(g_ref, i, tr, L)
        taps = _conv_taps(xe)
        dc = ge * _dsilu(_conv_eval(taps, w_ref, b_ref))
        dx = w_ref[CONV_W - 1:CONV_W, :] * dc
        for k in range(CONV_W - 1):
            dx = dx + w_ref[k:k + 1, :] * _shift_rows(dc, -(CONV_W - 1 - k))
        dx_ref[...] = dx[H:H + tr].astype(BF16)
        dcc = dc[H:H + tr]
        rows = [jnp.sum(dcc * taps[k][H:H + tr], axis=0, keepdims=True) for k in range(CONV_W)]
        dwv = jnp.concatenate(rows + [jnp.zeros((8 - CONV_W, tc), F32)], axis=0)
        dbv = jnp.sum(dcc, axis=0, keepdims=True)

        @pl.when(i == 0)
        def _():
            dw_ref[...] = dwv
            db_ref[...] = dbv

        @pl.when(i > 0)
        def _():
            dw_ref[...] += dwv
            db_ref[...] += dbv

    dx, dw, db = pl.pallas_call(
        body, grid=(C // tc, L // tr),
        in_specs=[pl.BlockSpec((L, tc), lambda j, i: (0, off + j)), pl.BlockSpec((L, tc), lambda j, i: (0, j)),
                  pl.BlockSpec((CONV_W, tc), lambda j, i: (0, j)), pl.BlockSpec((1, tc), lambda j, i: (0, j)), _ANY],
        out_specs=[pl.BlockSpec((tr, tc), lambda j, i: (i, off + j)), pl.BlockSpec((8, tc), lambda j, i: (0, j)),
                   pl.BlockSpec((1, tc), lambda j, i: (0, j))],
        out_shape=[jax.ShapeDtypeStruct(dzx.shape, BF16), jax.ShapeDtypeStruct((8, C), F32),
                   jax.ShapeDtypeStruct((1, C), F32)],
        input_output_aliases={4: 0},
        compiler_params=_cp(("parallel", "arbitrary")), name=name)(zx, g, conv_w, conv_b, dzx)
    return dx, dw[:CONV_W], db


_NN = (((1,), (0,)), ((), ()))


def _pieces(x, n):
    out, r = [], x
    for _ in range(n):
        p = r.astype(BF16)
        out.append(p)
        r = r - p.astype(F32)
    return out


def _dot01(a, b01, n, dims=_NN):
    b = b01.astype(BF16)
    return functools.reduce(lambda u, v: u + v,
                            [lax.dot_general(p, b, dims, preferred_element_type=F32) for p in _pieces(a, n)])


def _dot01_left(a01, b, n, dims=_NN):
    a = a01.astype(BF16)
    return functools.reduce(lambda u, v: u + v,
                            [lax.dot_general(a, p, dims, preferred_element_type=F32) for p in _pieces(b, n)])


def _ssd_common(dtp_ref, dtpT_ref, bias_ref, biasT_ref, alog_ref, alogT_ref, b_ref, c_ref):
    Q = SSD_Q
    dt = _softplus(dtp_ref[...] + bias_ref[...])
    A = -jnp.exp(alog_ref[...])
    row = lax.broadcasted_iota(jnp.int32, (Q, Q), 0)
    col = lax.broadcasted_iota(jnp.int32, (Q, Q), 1)
    causal = row >= col
    tril = causal.astype(F32)
    Kh = dt.shape[1]
    acum = _dot01_left(tril, dt * A, 3)
    eye = (lax.broadcasted_iota(jnp.int32, (Kh, Kh), 0) == lax.broadcasted_iota(jnp.int32, (Kh, Kh), 1)).astype(F32)
    acumT = _dot01_left(eye, acum, 3, dims=(((1,), (1,)), ((), ())))
    Bm = b_ref[...]
    Cm = c_ref[...]
    cb = lax.dot_general(Cm, Bm, (((1,), (1,)), ((), ())), preferred_element_type=F32)
    return dt, A, causal, row, col, acum, acumT, Bm, Cm, cb


def _ssd_in_specs(Q, GP, N, Kh, DI, cmap):
    nb0 = DI // N
    vec = pl.BlockSpec((None, 1, Kh), lambda g, c: (g, 0, 0))
    vecT = pl.BlockSpec((None, Kh, 1), lambda g, c: (g, 0, 0))
    return [pl.BlockSpec((Q, GP), lambda g, c: (cmap(c), g)),
            pl.BlockSpec((Q, N), lambda g, c: (cmap(c), nb0 + g)),
            pl.BlockSpec((Q, N), lambda g, c: (cmap(c), nb0 + SSD_G + g)),
            pl.BlockSpec((None, Q, Kh), lambda g, c: (g, cmap(c), 0)),
            pl.BlockSpec((None, Kh, Q), lambda g, c: (g, 0, cmap(c))),
            vec, vecT, vec, vecT, vec, vecT]


def _hi(a, b01):
    return _dot01(a, b01, 2)


def _headsum(a, b01):
    return _dot01(a, b01, 1)


def _ssd_heads(dskT_ref, acum, acumT, dt, Kh):
    Q, P, N = SSD_Q, SSD_P, SSD_N
    GP = Kh * P
    sh_p = P.bit_length() - 1
    seg = lambda shape, dim: lax.shift_right_logical(lax.broadcasted_iota(jnp.int32, shape, dim), sh_p)
    E = (seg((Kh, GP), 1) == lax.broadcasted_iota(jnp.int32, (Kh, GP), 0)).astype(F32)
    ET = (seg((GP, Kh), 0) == lax.broadcasted_iota(jnp.int32, (GP, Kh), 1)).astype(F32)
    a_last = acum[Q - 1:Q, :]
    tail = jnp.exp(a_last - acum)
    eLT = jnp.exp(acumT[:, Q - 1:Q])
    rowseg = seg((GP, N), 0)
    eL_b = jnp.zeros((GP, N), F32)
    for k in range(Kh):
        eL_b = jnp.where(rowseg == k, eLT[k:k + 1, :], eL_b)
    return dict(
        E=E, ET=ET, a_last=a_last, tail=tail, eL_b=eL_b,
        dt_all=_hi(dt, E), ea_all=_headsum(jnp.exp(acum), E), tail_all=_headsum(tail, E),
        dsk_all=jnp.sum(E * dskT_ref[...], axis=0, keepdims=True))


def _head_chunks(GP):
    CW = min(GP, 128)
    return CW, CW // SSD_P, GP // CW


def _head_mask(Q, CW, kk):
    lane = lax.broadcasted_iota(jnp.int32, (Q, CW), 1)
    return jnp.logical_and(lane >= kk * SSD_P, lane < (kk + 1) * SSD_P)


def ssd_fwd(xbc, dtp_g, dtp_gT, bias_g, bias_gT, alog_g, alog_gT, dsk_g, dsk_gT, zx, norm_g, DI):
    L = xbc.shape[0]
    Q, P, N, G = SSD_Q, SSD_P, SSD_N, SSD_G
    GP = DI // G
    Kh = GP // P
    nc = L // Q

    CW, hpc, nch = _head_chunks(GP)
    nt = (((1,), (1,)), ((), ()))
    tn = (((0,), (0,)), ((), ()))

    def body(xs_ref, b_ref, c_ref, dtp_ref, dtpT_ref, bias_ref, biasT_ref, alog_ref, alogT_ref, dsk_ref, dskT_ref,
             z_ref, ng_ref, y_ref, st_ref, yn_ref, state):
        @pl.when(pl.program_id(1) == 0)
        def _():
            state[...] = jnp.zeros(state.shape, F32)

        st_ref[...] = state[...]
        dt, A, causal, row, col, acum, acumT, Bm, Cm, cb = _ssd_common(
            dtp_ref, dtpT_ref, bias_ref, biasT_ref, alog_ref, alogT_ref, b_ref, c_ref)
        hd = _ssd_heads(dskT_ref, acum, acumT, dt, Kh)
        xs = xs_ref[...].astype(F32)
        xdt_all = xs * hd["dt_all"]
        S_all = state[...]
        y_all = (lax.dot_general(Cm, S_all.astype(BF16), nt, preferred_element_type=F32) * hd["ea_all"]
                 + xs * hd["dsk_all"])
        state[...] = S_all * hd["eL_b"] + lax.dot_general(
            (xdt_all * hd["tail_all"]).astype(BF16), Bm, tn, preferred_element_type=F32)
        for ch in range(nch):
            cs = slice(ch * CW, (ch + 1) * CW)
            xc = xdt_all[:, cs]
            acc = y_all[:, cs]
            for kk in range(hpc):
                k = ch * hpc + kk
                decay = jnp.exp(jnp.where(causal, acum[:, k:k + 1] - acumT[k:k + 1, :], -jnp.inf))
                xk = xc if hpc == 1 else jnp.where(_head_mask(Q, CW, kk), xc, 0.0)
                acc = acc + jnp.dot((cb * decay).astype(BF16), xk.astype(BF16), preferred_element_type=F32)
            y_ref[:, cs] = acc.astype(BF16)
        y2 = y_ref[...].astype(F32) * _silu(z_ref[...].astype(F32))
        rr = lax.rsqrt(jnp.mean(y2 * y2, axis=-1, keepdims=True) + RMS_EPS)
        yn_ref[...] = (y2 * rr * ng_ref[...]).astype(BF16)

    tile = pl.BlockSpec((Q, GP), lambda g, c: (c, g))
    return pl.pallas_call(
        body, grid=(G, nc),
        in_specs=_ssd_in_specs(Q, GP, N, Kh, DI, lambda c: c) + [tile, pl.BlockSpec((1, GP), lambda g, c: (0, g))],
        out_specs=[tile, pl.BlockSpec((None, None, GP, N), lambda g, c: (c, g, 0, 0)), tile],
        out_shape=[jax.ShapeDtypeStruct((L, DI), BF16), jax.ShapeDtypeStruct((nc, G, GP, N), F32),
                   jax.ShapeDtypeStruct((L, DI), BF16)],
        scratch_shapes=[pltpu.VMEM((GP, N), F32)], compiler_params=_cp(("parallel", "arbitrary")),
        name="ssd_fwd")(xbc, xbc, xbc, dtp_g, dtp_gT, bias_g, bias_gT, alog_g, alog_gT, dsk_g, dsk_gT, zx, norm_g)


def ssd_bwd(xbc, dtp_g, dtp_gT, bias_g, bias_gT, alog_g, alog_gT, dsk_g, dsk_gT, states, dyn, y, zx, norm_g, DI):
    L = xbc.shape[0]
    Q, P, N, G = SSD_Q, SSD_P, SSD_N, SSD_G
    GP = DI // G
    Kh = GP // P
    nc = L // Q
    rev = lambda c: nc - 1 - c

    CW, hpc, nch = _head_chunks(GP)

    def body(xs_ref, b_ref, c_ref, dtp_ref, dtpT_ref, bias_ref, biasT_ref, alog_ref, alogT_ref, dsk_ref, dskT_ref,
             st_ref, dyn_ref, y_ref, z_ref, ng_ref,
             dxs_ref, dB_ref, dC_ref, ddtp_ref, dbias_ref, dalog_ref, dD_ref, dz_ref, dng_ref, dstate):
        ci = pl.program_id(1)

        @pl.when(ci == 0)
        def _():
            dstate[...] = jnp.zeros(dstate.shape, F32)

        dt, A, causal, row, col, acum, acumT, Bm, Cm, cb = _ssd_common(
            dtp_ref, dtpT_ref, bias_ref, biasT_ref, alog_ref, alogT_ref, b_ref, c_ref)
        tn = (((0,), (0,)), ((), ()))
        nt = (((1,), (1,)), ((), ()))
        hd = _ssd_heads(dskT_ref, acum, acumT, dt, Kh)
        ET, tail = hd["ET"], hd["tail"]
        cbT = lax.dot_general(Bm, Cm, nt, preferred_element_type=F32)
        causalT = row <= col
        xs = xs_ref[...].astype(F32)
        xdt_all = xs * hd["dt_all"]
        yv = y_ref[...].astype(F32)
        zv = z_ref[...].astype(F32)
        dynv = dyn_ref[...].astype(F32)
        sz = _silu(zv)
        y2 = yv * sz
        rr = lax.rsqrt(jnp.mean(y2 * y2, axis=-1, keepdims=True) + RMS_EPS)
        yh = y2 * rr
        dyh = dynv * ng_ref[...]
        dy2 = rr * (dyh - yh * jnp.mean(dyh * yh, axis=-1, keepdims=True))
        dz_ref[...] = (dy2 * yv * _dsilu(zv)).astype(BF16)
        dng_v = jnp.sum(dynv * yh, axis=0, keepdims=True)
        dyb = (dy2 * sz).astype(BF16)
        dy_all = dyb.astype(F32)
        S_all = st_ref[...]
        S_b = S_all.astype(BF16)
        dS_all = dstate[...]
        dS_b = dS_all.astype(BF16)
        CS_all = lax.dot_general(Cm, S_b, nt, preferred_element_type=F32)
        dyE_b = (dy_all * hd["ea_all"]).astype(BF16)
        dC_acc = jnp.dot(dyE_b, S_b, preferred_element_type=F32)
        dS_y = lax.dot_general(dyE_b, Cm, tn, preferred_element_type=F32)
        BdS_all = lax.dot_general(Bm, dS_b, nt, preferred_element_type=F32)
        dB_acc = jnp.dot((xdt_all * hd["tail_all"]).astype(BF16), dS_b, preferred_element_type=F32)
        dtail = _headsum(xdt_all * BdS_all, ET)
        da_cols = _headsum(dy_all * CS_all * hd["ea_all"], ET) - dtail * tail
        dss = _dot01_left(jnp.ones((8, N), F32), _dot01_left(hd["E"], dS_all * S_all, 2), 2, dims=nt)
        da_last = dss[0:1] * jnp.exp(hd["a_last"]) + jnp.sum(dtail * tail, axis=0, keepdims=True)
        rowi = lax.broadcasted_iota(jnp.int32, (Q, Kh), 0)
        da_cols = da_cols + jnp.where(rowi == Q - 1, da_last, 0.0)
        dstate[...] = hd["eL_b"] * dS_all + dS_y
        sum_mg = jnp.zeros((Q, Q), F32)
        ddt_x = jnp.zeros((Q, Kh), F32)
        da_rows = jnp.zeros((Kh, Q), F32)
        lane_k = lax.broadcasted_iota(jnp.int32, (Q, Kh), 1)
        sub_k = lax.broadcasted_iota(jnp.int32, (Kh, Q), 0)
        for ch in range(nch):
            cs = slice(ch * CW, (ch + 1) * CW)
            dyc = dyb[:, cs]
            xc_b = xdt_all[:, cs].astype(BF16)
            acc = hd["tail_all"][:, cs] * BdS_all[:, cs]
            for kk in range(hpc):
                k = ch * hpc + kk
                a_b = jnp.broadcast_to(acum[:, k:k + 1], (Q, Q))
                a_r = acumT[k:k + 1, :]
                decay = jnp.exp(jnp.where(causal, a_b - a_r, -jnp.inf))
                decayT = jnp.exp(jnp.where(causalT, a_r - a_b, -jnp.inf))
                dyk = dyc if hpc == 1 else jnp.where(_head_mask(Q, CW, kk), dyc, jnp.zeros_like(dyc))
                mg = decay * lax.dot_general(dyk, xc_b, nt, preferred_element_type=F32)
                sum_mg = sum_mg + mg
                w = mg * cb
                da_cols = da_cols + jnp.where(lane_k == k, jnp.sum(w, axis=1, keepdims=True), 0.0)
                da_rows = da_rows + jnp.where(sub_k == k, jnp.sum(w, axis=0, keepdims=True), 0.0)
                acc = acc + jnp.dot((decayT * cbT).astype(BF16), dyk, preferred_element_type=F32)
            dxs_ref[:, cs] = (acc * hd["dt_all"][:, cs] + dy_all[:, cs] * hd["dsk_all"][:, cs]).astype(BF16)
            ddt_x = ddt_x + _headsum(acc * xs[:, cs], ET[cs, :])
        eye_q = (row == col).astype(F32)
        da_cols = da_cols - _dot01_left(eye_q, da_rows, 3, dims=nt)
        dD_row = jnp.sum(_headsum(dy_all * xs, ET), axis=0, keepdims=True)
        sum_mg_b = sum_mg.astype(BF16)
        dB_ref[...] = (dB_acc + lax.dot_general(sum_mg_b, Cm, tn, preferred_element_type=F32)).astype(BF16)
        dC_ref[...] = (dC_acc + jnp.dot(sum_mg_b, Bm, preferred_element_type=F32)).astype(BF16)
        triu = (row <= col).astype(F32)
        ddtA = _dot01_left(triu, da_cols, 3)
        ddt = ddt_x + ddtA * A
        dpre = ddt * _sigmoid(dtp_ref[...] + bias_ref[...])
        ddtp_ref[...] = dpre
        dbias_v = jnp.sum(dpre, axis=0, keepdims=True)
        dalog_v = jnp.sum(ddtA * dt, axis=0, keepdims=True) * A

        @pl.when(ci == 0)
        def _():
            dbias_ref[...] = dbias_v
            dalog_ref[...] = dalog_v
            dD_ref[...] = dD_row
            dng_ref[...] = dng_v

        @pl.when(ci > 0)
        def _():
            dbias_ref[...] += dbias_v
            dalog_ref[...] += dalog_v
            dD_ref[...] += dD_row
            dng_ref[...] += dng_v

    vec_o = pl.BlockSpec((None, 1, Kh), lambda g, c: (g, 0, 0))
    tile = pl.BlockSpec((Q, GP), lambda g, c: (rev(c), g))
    return pl.pallas_call(
        body, grid=(G, nc),
        in_specs=_ssd_in_specs(Q, GP, N, Kh, DI, rev)
        + [pl.BlockSpec((None, None, GP, N), lambda g, c: (rev(c), g, 0, 0)), tile, tile, tile,
           pl.BlockSpec((1, GP), lambda g, c: (0, g))],
        out_specs=[tile, pl.BlockSpec((Q, N), lambda g, c: (rev(c), g)), pl.BlockSpec((Q, N), lambda g, c: (rev(c), g)),
                   pl.BlockSpec((None, Q, Kh), lambda g, c: (g, rev(c), 0)), vec_o, vec_o, vec_o,
                   tile, pl.BlockSpec((1, GP), lambda g, c: (0, g))],
        out_shape=[jax.ShapeDtypeStruct((L, DI), BF16), jax.ShapeDtypeStruct((L, G * N), BF16),
                   jax.ShapeDtypeStruct((L, G * N), BF16), jax.ShapeDtypeStruct((G, L, Kh), F32)]
        + [jax.ShapeDtypeStruct((G, 1, Kh), F32)] * 3
        + [jax.ShapeDtypeStruct(zx.shape, BF16), jax.ShapeDtypeStruct((1, DI), F32)],
        scratch_shapes=[pltpu.VMEM((GP, N), F32)], compiler_params=_cp(("parallel", "arbitrary")),
        name="ssd_bwd")(xbc, xbc, xbc, dtp_g, dtp_gT, bias_g, bias_gT, alog_g, alog_gT, dsk_g, dsk_gT, states,
                        dyn, y, zx, norm_g)


def _alibi_slope(gi, h):
    n = len(DIL_PATTERNS) * DIL_H
    return float(2.0 ** (-8.0 * (gi * DIL_H + h + 1) / n))


def _attn_masks():
    qi = lax.broadcasted_iota(jnp.int32, (DIL_BLK, DIL_BLK), 0)
    kj = lax.broadcasted_iota(jnp.int32, (DIL_BLK, DIL_BLK), 1)
    dcur = (qi - kj).astype(F32)
    return dcur, qi >= kj, dcur + float(DIL_BLK), kj >= qi


def attn_fwd(q3, kv3, gi):
    window, d = DIL_PATTERNS[gi]
    assert window // d == DIL_BLK
    HW = DIL_H * DIL_E
    M = q3.shape[1]
    nb = M // DIL_BLK
    scale = DIL_E ** -0.5
    nt = (((1,), (1,)), ((), ()))

    def body(q_ref, kp_ref, kc_ref, vp_ref, vc_ref, o_ref, lse_ref):
        n = pl.program_id(1)
        dcur, vcur, dprev, vprev0 = _attn_masks()
        dist = jnp.concatenate([dprev, dcur], axis=1)
        valid = jnp.concatenate([jnp.logical_and(vprev0, n > 0), vcur], axis=1)
        lane = lax.broadcasted_iota(jnp.int32, (DIL_BLK, 128), 1)
        lse_acc = jnp.zeros((DIL_BLK, 128), F32)
        for h in range(DIL_H):
            hs = slice(h * DIL_E, (h + 1) * DIL_E)
            sl = _alibi_slope(gi, h) * d
            kcat = jnp.concatenate([kp_ref[:, hs], kc_ref[:, hs]], axis=0)
            vcat = jnp.concatenate([vp_ref[:, hs], vc_ref[:, hs]], axis=0)
            s = lax.dot_general(q_ref[:, hs], kcat, nt, preferred_element_type=F32) * scale - sl * dist
            s = jnp.where(valid, s, -jnp.inf)
            m = jnp.max(s, axis=-1, keepdims=True)
            p = jnp.exp(s - m)
            den = jnp.sum(p, axis=-1, keepdims=True)
            o = jnp.dot(p.astype(BF16), vcat, preferred_element_type=F32) / den
            o_ref[:, hs] = o.astype(BF16)
            lse_acc = jnp.where(lane == h, m + jnp.log(den), lse_acc)
        lse_ref[...] = lse_acc

    blk = (None, DIL_BLK, HW)
    prev = lambda n: jnp.maximum(n - 1, 0)
    return pl.pallas_call(
        body, grid=(d, nb),
        in_specs=[pl.BlockSpec(blk, lambda r, n: (r, n, 0)),
                  pl.BlockSpec(blk, lambda r, n: (r, prev(n), 0)), pl.BlockSpec(blk, lambda r, n: (r, n, 0)),
                  pl.BlockSpec(blk, lambda r, n: (r, prev(n), 1)), pl.BlockSpec(blk, lambda r, n: (r, n, 1))],
        out_specs=[pl.BlockSpec(blk, lambda r, n: (r, n, 0)), pl.BlockSpec((None, DIL_BLK, 128), lambda r, n: (r, n, 0))],
        out_shape=[jax.ShapeDtypeStruct((d, M, HW), BF16), jax.ShapeDtypeStruct((d, M, 128), F32)],
        compiler_params=_cp(("parallel", "parallel")), name=f"attn_fwd_{gi}")(q3, kv3, kv3, kv3, kv3)


def attn_bwd(q3, kv3, do3, lse3, dpr3, gi):
    window, d = DIL_PATTERNS[gi]
    HW = DIL_H * DIL_E
    M = q3.shape[1]
    L = M * d
    nb = M // DIL_BLK
    scale = DIL_E ** -0.5
    nt = (((1,), (1,)), ((), ()))
    tn = (((0,), (0,)), ((), ()))

    def body(q0_ref, q1_ref, k_ref, v_ref, do0_ref, do1_ref, l0_ref, l1_ref, r0_ref, r1_ref,
             dq_ref, dk_ref, dv_ref, carry):
        n = pl.program_id(1)

        @pl.when(n == 0)
        def _():
            carry[...] = jnp.zeros(carry.shape, F32)

        dcur, vcur, dprev, vprev0 = _attn_masks()
        dist = jnp.concatenate([dcur, dprev], axis=0)
        valid = jnp.concatenate([vcur, jnp.logical_and(vprev0, n < nb - 1)], axis=0)
        B = DIL_BLK
        for h in range(DIL_H):
            hs = slice(h * DIL_E, (h + 1) * DIL_E)
            sl = _alibi_slope(gi, h) * d
            kh = k_ref[:, hs]
            vh = v_ref[:, hs]
            qcat = jnp.concatenate([q0_ref[:, hs], q1_ref[:, hs]], axis=0)
            docat = jnp.concatenate([do0_ref[:, hs], do1_ref[:, hs]], axis=0)
            lcat = jnp.concatenate([l0_ref[:, h:h + 1], l1_ref[:, h:h + 1]], axis=0)
            rcat = jnp.concatenate([r0_ref[:, h:h + 1], r1_ref[:, h:h + 1]], axis=0)
            s = lax.dot_general(qcat, kh, nt, preferred_element_type=F32) * scale - sl * dist
            p = jnp.exp(jnp.where(valid, s - lcat, -jnp.inf))
            ds = p * (lax.dot_general(docat, vh, nt, preferred_element_type=F32) - rcat)
            ds_b = (ds * scale).astype(BF16)
            dv_ref[:, hs] = lax.dot_general(p.astype(BF16), docat, tn, preferred_element_type=F32).astype(BF16)
            dk_ref[:, hs] = lax.dot_general(ds_b, qcat, tn, preferred_element_type=F32).astype(BF16)
            dqc = jnp.dot(ds_b, kh, preferred_element_type=F32)
            dq_ref[:, hs] = (carry[:, hs] + dqc[:B]).astype(BF16)
            carry[:, hs] = dqc[B:]

    blk = (None, DIL_BLK, HW)
    sblk = (None, DIL_BLK, 128)
    oblk = (DIL_BLK, HW)
    nxt = lambda n: jnp.minimum(n + 1, nb - 1)
    here = lambda c: (lambda r, n: (r, n, c))
    ahead = lambda c: (lambda r, n: (r, nxt(n), c))
    outs = pl.pallas_call(
        body, grid=(d, nb),
        in_specs=[pl.BlockSpec(blk, here(0)), pl.BlockSpec(blk, ahead(0)),
                  pl.BlockSpec(blk, here(0)), pl.BlockSpec(blk, here(1)),
                  pl.BlockSpec(blk, here(0)), pl.BlockSpec(blk, ahead(0)),
                  pl.BlockSpec(sblk, here(0)), pl.BlockSpec(sblk, ahead(0)),
                  pl.BlockSpec(sblk, here(0)), pl.BlockSpec(sblk, ahead(0))],
        out_specs=[pl.BlockSpec(oblk, lambda r, n: (n, r))] * 3,
        out_shape=[jax.ShapeDtypeStruct((M, d * HW), BF16)] * 3,
        scratch_shapes=[pltpu.VMEM(oblk, F32)], compiler_params=_cp(("parallel", "arbitrary")),
        name=f"attn_bwd_{gi}")(q3, q3, kv3, kv3, do3, do3, lse3, lse3, dpr3, dpr3)
    return [t.reshape(L, HW) for t in outs]


def _merge_weights(l_tiles, h):
    ls = [t[:, h:h + 1] for t in l_tiles]
    mx = functools.reduce(jnp.maximum, ls)
    es = [jnp.exp(l - mx) for l in ls]
    den = functools.reduce(lambda a, b: a + b, es)
    return [e / den for e in es]


def _dil_specs(tr, arrs):
    return [pl.BlockSpec((a.shape[0], tr // a.shape[0], a.shape[2]), lambda i: (0, i, 0)) for a in arrs]


def _dil_scratch(tr, arrs):
    return [pltpu.VMEM((a.shape[2] // 128, tr, 128), F32) for a in arrs if a.shape[0] > 1]


def _undilate(refs3, scrs, tr):
    out, k = [], 0
    for ref in refs3:
        d, _, W = ref.shape
        if d == 1:
            out.append(lambda c, ref=ref: ref[0, :, c * 128:(c + 1) * 128])
            continue
        scr = scrs[k]
        k += 1
        for r in range(d):
            for c in range(W // 128):
                scr.at[c][pl.ds(r, tr // d, stride=d), :] = ref[r, :, c * 128:(c + 1) * 128].astype(F32)
        out.append(lambda c, scr=scr: scr[c])
    return out


def merge_fwd(os3, lses3, z):
    HW = os3[0].shape[2]
    L = os3[0].shape[0] * os3[0].shape[1]
    tr = _tile(L, 256, 16)
    ng = len(os3)
    n_scr = len(_dil_scratch(tr, os3))

    def body(*refs):
        z_ref, out_ref = refs[2 * ng], refs[2 * ng + 1]
        scrs = refs[2 * ng + 2:]
        o_get = _undilate(refs[:ng], scrs[:n_scr], tr)
        l_tiles = [g(0) for g in _undilate(refs[ng:2 * ng], scrs[n_scr:], tr)]
        for h in range(DIL_H):
            hs = slice(h * DIL_E, (h + 1) * DIL_E)
            ws = _merge_weights(l_tiles, h)
            om = functools.reduce(lambda a, b: a + b, [w * o(h).astype(F32) for w, o in zip(ws, o_get)])
            out_ref[:, hs] = (om * _silu(z_ref[:, hs].astype(F32))).astype(BF16)

    return pl.pallas_call(
        body, grid=(L // tr,),
        in_specs=_dil_specs(tr, os3) + _dil_specs(tr, lses3) + _row_specs(tr, [HW]),
        out_specs=_row_specs(tr, [HW])[0], out_shape=jax.ShapeDtypeStruct((L, HW), BF16),
        scratch_shapes=_dil_scratch(tr, os3) + _dil_scratch(tr, lses3),
        compiler_params=_cp(("parallel",)), name="merge_fwd")(*os3, *lses3, z)


def merge_bwd(dgated, os3, lses3, z):
    HW = os3[0].shape[2]
    L = os3[0].shape[0] * os3[0].shape[1]
    tr = _tile(L, 256, 16)
    ng = len(os3)
    n_scr = len(_dil_scratch(tr, os3))

    def body(*refs):
        dg_ref = refs[0]
        z_ref = refs[1 + 2 * ng]
        outs = refs[2 + 2 * ng:2 + 2 * ng + 2 * ng + 1]
        scrs = refs[2 + 2 * ng + 2 * ng + 1:]
        do_out, dpr_out, dz_ref = outs[:ng], outs[ng:2 * ng], outs[2 * ng]
        o_get = _undilate(refs[1:1 + ng], scrs[:n_scr], tr)
        l_tiles = [g(0) for g in _undilate(refs[1 + ng:1 + 2 * ng], scrs[n_scr:2 * n_scr], tr)]
        stage = scrs[2 * n_scr:]
        do_stage, dpr_stage, k = [], [], 0
        for g in range(ng):
            if do_out[g].shape[0] == 1:
                do_stage.append(None)
                dpr_stage.append(None)
            else:
                do_stage.append(stage[2 * k])
                dpr_stage.append(stage[2 * k + 1])
                k += 1
        lane = lax.broadcasted_iota(jnp.int32, (tr, 128), 1)
        accs = [jnp.zeros((tr, 128), F32) for _ in range(ng)]
        for h in range(DIL_H):
            hs = slice(h * DIL_E, (h + 1) * DIL_E)
            ws = _merge_weights(l_tiles, h)
            ov = [o(h).astype(F32) for o in o_get]
            om = functools.reduce(lambda a, b: a + b, [w * o for w, o in zip(ws, ov)])
            zv = z_ref[:, hs].astype(F32)
            dgv = dg_ref[:, hs].astype(F32)
            dom = dgv * _silu(zv)
            dz_ref[:, hs] = (dgv * om * _dsilu(zv)).astype(BF16)
            dws = [jnp.sum(dom * o, axis=-1, keepdims=True) for o in ov]
            dwbar = functools.reduce(lambda a, b: a + b, [w * dw for w, dw in zip(ws, dws)])
            for g in range(ng):
                if do_stage[g] is None:
                    do_out[g][0, :, hs] = (ws[g] * dom).astype(BF16)
                else:
                    do_stage[g][h] = ws[g] * dom
                accs[g] = jnp.where(lane == h, ws[g] * dwbar, accs[g])
        for g in range(ng):
            d = do_out[g].shape[0]
            if d == 1:
                dpr_out[g][0] = accs[g]
                continue
            dpr_stage[g][0] = accs[g]
            for r in range(d):
                dpr_out[g][r] = dpr_stage[g].at[0][pl.ds(r, tr // d, stride=d), :]
                for c in range(HW // 128):
                    do_out[g][r, :, c * 128:(c + 1) * 128] = do_stage[g].at[c][pl.ds(r, tr // d, stride=d), :].astype(BF16)

    stage_shapes = []
    for o3 in os3:
        if o3.shape[0] > 1:
            stage_shapes += [pltpu.VMEM((HW // 128, tr, 128), F32), pltpu.VMEM((1, tr, 128), F32)]
    outs = pl.pallas_call(
        body, grid=(L // tr,),
        in_specs=_row_specs(tr, [HW]) + _dil_specs(tr, os3) + _dil_specs(tr, lses3) + _row_specs(tr, [HW]),
        out_specs=_dil_specs(tr, os3) + _dil_specs(tr, lses3) + _row_specs(tr, [HW]),
        out_shape=[jax.ShapeDtypeStruct(o.shape, BF16) for o in os3] + [jax.ShapeDtypeStruct(l.shape, F32) for l in lses3]
        + [jax.ShapeDtypeStruct((L, HW), BF16)],
        scratch_shapes=_dil_scratch(tr, os3) + _dil_scratch(tr, lses3) + stage_shapes,
        compiler_params=_cp(("parallel",)), name="merge_bwd")(dgated, *os3, *lses3, z)
    return outs[:ng], outs[ng:2 * ng], outs[2 * ng]


def ada_fwd(c8, ada_w):
    nl, D, Ws = ada_w.shape
    tn = _tile(Ws, 512)

    def body(c_ref, w_ref, o_ref):
        o_ref[...] = jnp.dot(_silu(c_ref[...]), w_ref[...], precision=lax.Precision.HIGHEST,
                             preferred_element_type=F32)

    return pl.pallas_call(
        body, grid=(nl, Ws // tn),
        in_specs=[pl.BlockSpec((N_DEV, D), lambda l, j: (0, 0)), pl.BlockSpec((None, D, tn), lambda l, j: (l, 0, j))],
        out_specs=pl.BlockSpec((None, N_DEV, tn), lambda l, j: (l, 0, j)),
        out_shape=jax.ShapeDtypeStruct((nl, N_DEV, Ws), F32), compiler_params=_cp(("parallel", "parallel")),
        name="ada_fwd")(c8, ada_w)


def ada_wgrad(c8t, dmod):
    nl, _, Ws = dmod.shape
    D = c8t.shape[0]
    tm = _tile(D, 512, 8)

    def body(c_ref, d_ref, o_ref):
        sc = _silu(c_ref[...])
        acc = sc[:, 0:1] * d_ref[0:1, :]
        for e in range(1, N_DEV):
            acc = acc + sc[:, e:e + 1] * d_ref[e:e + 1, :]
        o_ref[...] = acc

    return pl.pallas_call(
        body, grid=(nl, D // tm),
        in_specs=[pl.BlockSpec((tm, N_DEV), lambda l, i: (i, 0)), pl.BlockSpec((None, N_DEV, Ws), lambda l, i: (l, 0, 0))],
        out_specs=pl.BlockSpec((None, tm, Ws), lambda l, i: (l, i, 0)),
        out_shape=jax.ShapeDtypeStruct((nl, D, Ws), F32), compiler_params=_cp(("parallel", "parallel")),
        name="ada_wgrad")(c8t, dmod)


def adamw(w, g, m, v, name):
    R, C = w.shape
    tr = _tile(R, 256, 8)
    c1 = 1.0 - ADAM_B1 ** ADAM_STEP
    c2 = 1.0 - ADAM_B2 ** ADAM_STEP

    def body(w_ref, g_ref, m_ref, v_ref, d_ref, nm_ref, nv_ref):
        gv = g_ref[...]
        nm = ADAM_B1 * m_ref[...] + (1.0 - ADAM_B1) * gv
        nv = ADAM_B2 * v_ref[...] + (1.0 - ADAM_B2) * (gv * gv)
        nm_ref[...] = nm
        nv_ref[...] = nv
        d_ref[...] = -ADAM_LR * ((nm / c1) / (jnp.sqrt(nv / c2) + ADAM_EPS) + ADAM_WD * w_ref[...])

    return pl.pallas_call(
        body, grid=(R // tr,), in_specs=_row_specs(tr, [C] * 4), out_specs=_row_specs(tr, [C] * 3),
        out_shape=[jax.ShapeDtypeStruct((R, C), F32)] * 3, compiler_params=_cp(("parallel",)), name=name)(w, g, m, v)


def sum_leading(t, name, out_dtype=F32):
    S, R, C = t.shape
    tr = _tile(R, 256, 16)

    def body(t_ref, o_ref):
        acc = t_ref[0].astype(F32)
        for s in range(1, S):
            acc = acc + t_ref[s].astype(F32)
        o_ref[...] = acc.astype(out_dtype)

    return pl.pallas_call(
        body, grid=(R // tr,), in_specs=[pl.BlockSpec((S, tr, C), lambda i: (0, i, 0))],
        out_specs=pl.BlockSpec((tr, C), lambda i: (i, 0)), out_shape=jax.ShapeDtypeStruct((R, C), out_dtype),
        compiler_params=_cp(("parallel",)), name=name)(t)


def add_half(g, a, core, name, by_cols=False):
    S, R, C = g.shape

    def body(core_ref, g_ref, a_ref, o_ref):
        o_ref[...] = (g_ref[...].astype(F32) + a_ref[...].astype(F32)).astype(BF16)

    if by_cols:
        hc = C // 2
        tr = _tile(R, 512, 16)
        return pl.pallas_call(
            body,
            grid_spec=pltpu.PrefetchScalarGridSpec(
                num_scalar_prefetch=1, grid=(S, R // tr),
                in_specs=[pl.BlockSpec((None, tr, hc), lambda s, i, core_ref: (s, i, core_ref[0])),
                          pl.BlockSpec((None, tr, hc), lambda s, i, core_ref: (s, i, 0))],
                out_specs=pl.BlockSpec((None, tr, hc), lambda s, i, core_ref: (s, i, 0))),
            out_shape=jax.ShapeDtypeStruct((S, R, hc), BF16), compiler_params=_cp(("parallel", "parallel")),
            name=name)(core, g, a)
    h = R // 2
    tr = _tile(h, 256, 16)
    nb = h // tr

    return pl.pallas_call(
        body,
        grid_spec=pltpu.PrefetchScalarGridSpec(
            num_scalar_prefetch=1, grid=(S, nb),
            in_specs=[pl.BlockSpec((None, tr, C), lambda s, i, core_ref: (s, core_ref[0] * nb + i, 0)),
                      pl.BlockSpec((None, tr, C), lambda s, i, core_ref: (s, i, 0))],
            out_specs=pl.BlockSpec((None, tr, C), lambda s, i, core_ref: (s, i, 0))),
        out_shape=jax.ShapeDtypeStruct((S, h, C), BF16), compiler_params=_cp(("parallel", "parallel")),
        name=name)(core, g, a)


def sum_partials(own, landed, chip, name):
    _, h, C = own.shape
    tr = _tile(h, 512, 16)

    def body(chip_ref, own_ref, l_ref, o_ref):
        acc = own_ref[...].astype(F32)
        for j in range(3):
            acc = acc + l_ref[j].astype(F32)
        o_ref[...] = acc

    return pl.pallas_call(
        body,
        grid_spec=pltpu.PrefetchScalarGridSpec(
            num_scalar_prefetch=1, grid=(h // tr,),
            in_specs=[pl.BlockSpec((None, tr, C), lambda i, chip_ref: (chip_ref[0], i, 0)),
                      pl.BlockSpec((3, tr, C), lambda i, chip_ref: (0, i, 0))],
            out_specs=pl.BlockSpec((tr, C), lambda i, chip_ref: (i, 0))),
        out_shape=jax.ShapeDtypeStruct((h, C), F32), compiler_params=_cp(("parallel",)), name=name)(chip, own, landed)


def adamw_halves(w, g_mine, g_theirs, m, v, core, name, after=None):
    R, C = w.shape
    h = R // 2
    tr = _tile(h, 256, 8)
    nbh = h // tr
    c1 = 1.0 - ADAM_B1 ** ADAM_STEP
    c2 = 1.0 - ADAM_B2 ** ADAM_STEP
    extra = [] if after is None else [after]

    def body(core_ref, w_ref, gm_ref, gt_ref, m_ref, v_ref, *rest):
        g_ref, d_ref, nm_ref, nv_ref = rest[len(extra):]
        mine = (pl.program_id(0) // nbh) == core_ref[0]
        gv = jnp.where(mine, gm_ref[...], gt_ref[...])
        g_ref[...] = gv
        nm = ADAM_B1 * m_ref[...] + (1.0 - ADAM_B1) * gv
        nv = ADAM_B2 * v_ref[...] + (1.0 - ADAM_B2) * (gv * gv)
        nm_ref[...] = nm
        nv_ref[...] = nv
        d_ref[...] = -ADAM_LR * ((nm / c1) / (jnp.sqrt(nv / c2) + ADAM_EPS) + ADAM_WD * w_ref[...])

    full = pl.BlockSpec((tr, C), lambda i, core_ref: (i, 0))
    halfspec = pl.BlockSpec((tr, C), lambda i, core_ref: (i % nbh, 0))
    return pl.pallas_call(
        body,
        grid_spec=pltpu.PrefetchScalarGridSpec(
            num_scalar_prefetch=1, grid=(2 * nbh,),
            in_specs=[full, halfspec, halfspec, full, full] + [_ANY] * len(extra), out_specs=[full] * 4),
        out_shape=[jax.ShapeDtypeStruct((R, C), F32)] * 4, compiler_params=_cp(("parallel",)),
        name=name)(core, w, g_mine, g_theirs, m, v, *extra)


_ANY = pl.BlockSpec(memory_space=pl.ANY)


def _place():
    x, y, c = lax.axis_index("x"), lax.axis_index("y"), lax.axis_index("c")
    chips = [(1 - x, y), (x, 1 - y), (1 - x, 1 - y)]
    return x, y, c, chips


def allgather_small(v, name, after=None):
    R, W = v.shape
    extra = [] if after is None else [after]

    def body(x_ref, *rest):
        out_ref, send_sems, recv_sems, local_sem = rest[len(extra):]
        x, y, c, chips = _place()
        me, sibling = (x, y, c), (x, y, 1 - c)

        def rows(px, py, pc):
            return out_ref.at[pl.ds((4 * px + 2 * py + pc) * R, R), :]

        def copy(k, block, to, src=None):
            return pltpu.make_async_remote_copy(
                src_ref=rows(*block) if src is None else src, dst_ref=rows(*block),
                send_sem=send_sems.at[k], recv_sem=recv_sems.at[k], device_id=to, device_id_type=MESH)

        mine = pltpu.make_async_copy(x_ref, rows(*me), local_sem)
        mine.start()
        first = [copy(0, me, sibling, src=x_ref)]
        first += [copy(1 + j, me, (*chip, c), src=x_ref) for j, chip in enumerate(chips)]
        for cp in first:
            cp.start()
        passed = [copy(4 + j, (*chip, c), sibling) for j, chip in enumerate(chips)]
        for j, chip in enumerate(chips):
            copy(1 + j, (*chip, c), me).wait_recv()
            passed[j].start()
        copy(0, sibling, me).wait_recv()
        for j, chip in enumerate(chips):
            copy(4 + j, (*chip, 1 - c), me).wait_recv()
        for cp in first + passed:
            cp.wait_send()
        mine.wait()

    return pl.pallas_call(
        body, out_shape=jax.ShapeDtypeStruct((N_DEV * R, W), v.dtype),
        in_specs=[pl.BlockSpec(memory_space=pltpu.VMEM)] + [_ANY] * len(extra),
        out_specs=pl.BlockSpec(memory_space=pltpu.VMEM),
        scratch_shapes=[pltpu.SemaphoreType.DMA((7,)), pltpu.SemaphoreType.DMA((7,)), pltpu.SemaphoreType.DMA],
        name=name)(v, *extra)


def allgather_routed(shard, name):
    R, C = shard.shape
    hc = C // 2
    ra = (R // 2) // 16 * 16

    def body(in_ref, out_ref, send_sems, recv_sems):
        x, y, c, _ = _place()
        xn, yn = (1 - x, y, c), (x, 1 - y, c)
        sibling = (x, y, 1 - c)
        p, pxn, pyn, pdg = 2 * x + y, 2 * (1 - x) + y, 2 * x + (1 - y), 2 * (1 - x) + (1 - y)
        rows_a, rows_b, rows_all = pl.ds(0, ra), pl.ds(ra, R - ra), pl.ds(0, R)

        def win(ref, rows, core):
            return ref.at[rows, pl.ds(pl.multiple_of(core * hc, 128), hc)]

        def copy(k, chip_id, rows, core, to, src=None):
            blk = win(out_ref.at[chip_id], rows, core)
            return pltpu.make_async_remote_copy(
                src_ref=blk if src is None else src, dst_ref=blk, send_sem=send_sems.at[k], recv_sem=recv_sems.at[k],
                device_id=to, device_id_type=MESH)

        own = [copy(0, p, rows_a, c, xn, src=win(in_ref, rows_a, c)), copy(1, p, rows_b, c, xn, src=win(in_ref, rows_b, c)),
               copy(2, p, rows_b, c, yn, src=win(in_ref, rows_b, c)), copy(3, p, rows_a, c, yn, src=win(in_ref, rows_a, c))]
        for cp in own:
            cp.start()
        copy(0, pxn, rows_a, c, xn).wait_recv()
        fwd_a = copy(4, pxn, rows_a, c, yn)
        fwd_a.start()
        copy(2, pyn, rows_b, c, yn).wait_recv()
        fwd_b = copy(5, pyn, rows_b, c, xn)
        fwd_b.start()
        copy(1, pxn, rows_b, c, xn).wait_recv()
        copy(3, pyn, rows_a, c, yn).wait_recv()
        passed = [copy(6, pxn, rows_all, c, sibling), copy(7, pyn, rows_all, c, sibling)]
        for cp in passed:
            cp.start()
        copy(4, pdg, rows_a, c, yn).wait_recv()
        passed.append(copy(8, pdg, rows_a, c, sibling))
        passed[-1].start()
        copy(5, pdg, rows_b, c, xn).wait_recv()
        passed.append(copy(9, pdg, rows_b, c, sibling))
        passed[-1].start()
        for k, (chip_id, rows) in enumerate([(pxn, rows_all), (pyn, rows_all), (pdg, rows_a), (pdg, rows_b)]):
            copy(6 + k, chip_id, rows, 1 - c, sibling).wait_recv()
        for cp in own + [fwd_a, fwd_b] + passed:
            cp.wait_send()

    out = pl.pallas_call(
        body, out_shape=jax.ShapeDtypeStruct((N_CHIPS, R, C), shard.dtype), in_specs=[_ANY], out_specs=_ANY,
        scratch_shapes=[pltpu.SemaphoreType.DMA((10,)), pltpu.SemaphoreType.DMA((10,))], name=name)(shard)
    chip = 2 * lax.axis_index("x") + lax.axis_index("y")
    return lax.dynamic_update_index_in_dim(out, shard, chip, 0)


_HBM = pl.BlockSpec(memory_space=pltpu.HBM)
_SEM = pl.BlockSpec(memory_space=pltpu.SEMAPHORE)
_EFFECT = pltpu.SideEffectType.DATAFLOW_SIDE_EFFECTING


def _chip_copies(kind, srcs, lands, send_sems, recv_sems):
    x, y, c, chips = _place()
    p = 2 * x + y
    cps = []
    if kind == "sibling":
        for i in range(len(srcs)):
            h = srcs[i].shape[1] // 2
            cps.append(pltpu.make_async_remote_copy(
                src_ref=srcs[i].at[:, pl.ds((1 - c) * h, h), :], dst_ref=lands[i], send_sem=send_sems.at[3 * i],
                recv_sem=recv_sems.at[3 * i], device_id=(x, y, 1 - c), device_id_type=MESH))
        return cps
    for i in range(len(srcs)):
        for j, (cx, cy) in enumerate(chips):
            if kind == "gather":
                src, dst = srcs[i].at[c], lands[i].at[p, c]
            else:
                src, dst = srcs[i].at[2 * cx + cy], lands[i].at[j]
            cps.append(pltpu.make_async_remote_copy(
                src_ref=src, dst_ref=dst, send_sem=send_sems.at[3 * i + j], recv_sem=recv_sems.at[3 * i + j],
                device_id=(cx, cy, c), device_id_type=MESH))
    return cps


def split_start(kind, srcs, land_shapes, after, name):
    n = len(srcs)

    def body(*refs):
        src_refs, land_refs = refs[:n], refs[n:2 * n]
        send_sems, recv_sems = refs[2 * n + 1], refs[2 * n + 2]
        token = refs[-1]
        for cp in _chip_copies(kind, src_refs, land_refs, send_sems, recv_sems):
            cp.start()
        token[...] = jnp.zeros_like(token)

    lands = [pltpu.with_memory_space_constraint(lax.empty(s, BF16), pltpu.HBM) for s in land_shapes]
    outs = pl.pallas_call(
        body, name=name,
        out_shape=(pltpu.SemaphoreType.DMA((3 * n,)), pltpu.SemaphoreType.DMA((3 * n,)),
                   *[pltpu.HBM(s.shape, s.dtype) for s in srcs], *[pltpu.HBM(s, BF16) for s in land_shapes],
                   jax.ShapeDtypeStruct((8, 128), F32)),
        in_specs=[_HBM] * (2 * n) + [_ANY],
        out_specs=(_SEM, _SEM, *([_HBM] * (2 * n)), pl.BlockSpec(memory_space=pltpu.VMEM)),
        input_output_aliases={i: 2 + i for i in range(2 * n)},
        compiler_params=pltpu.CompilerParams(has_side_effects=_EFFECT),
    )(*[pltpu.with_memory_space_constraint(s, pltpu.HBM) for s in srcs], *lands, after)
    return outs[0], outs[1], outs[2:2 + n], outs[2 + n:2 + 2 * n], outs[-1]


def split_wait(kind, send_sems, recv_sems, srcs, lands, after, name):
    n = len(srcs)

    def body(*refs):
        src_refs, land_refs = refs[:n], refs[n:2 * n]
        ssem, rsem = refs[2 * n], refs[2 * n + 1]
        for cp in _chip_copies(kind, src_refs, land_refs, ssem, rsem):
            cp.wait_send()
            cp.wait_recv()

    outs = pl.pallas_call(
        body, name=name,
        out_shape=[pltpu.HBM(s.shape, s.dtype) for s in srcs] + [pltpu.HBM(s.shape, s.dtype) for s in lands],
        in_specs=[_HBM] * (2 * n) + [_SEM, _SEM, _ANY], out_specs=[_HBM] * (2 * n),
        input_output_aliases={i: i for i in range(2 * n)},
        compiler_params=pltpu.CompilerParams(has_side_effects=_EFFECT),
    )(*srcs, *lands, send_sems, recv_sems, after)
    return outs[:n], outs[n:]


def pass_to_sibling(lands):
    n = len(lands)

    def body(*refs):
        ins, outs = refs[:n], refs[n:2 * n]
        send_sems, recv_sems = refs[2 * n:]
        x, y, c, chips = _place()
        cps = []
        for i in range(n):
            for j, (cx, cy) in enumerate(chips):
                blk = outs[i].at[2 * cx + cy, c]
                cps.append(pltpu.make_async_remote_copy(
                    src_ref=ins[i].at[2 * cx + cy, c], dst_ref=blk, send_sem=send_sems.at[3 * i + j],
                    recv_sem=recv_sems.at[3 * i + j], device_id=(x, y, 1 - c), device_id_type=MESH))
        for cp in cps:
            cp.start()
        for cp in cps:
            cp.wait()

    return pl.pallas_call(
        body, out_shape=[jax.ShapeDtypeStruct(t.shape, t.dtype) for t in lands], in_specs=[_ANY] * n,
        out_specs=[_ANY] * n, input_output_aliases={i: i for i in range(n)},
        scratch_shapes=[pltpu.SemaphoreType.DMA((3 * n,)), pltpu.SemaphoreType.DMA((3 * n,))],
        name="ag_pass_to_sibling")(*lands)


def _all8_copies(src, land, send_sems, recv_sems):
    x, y, c, _ = _place()
    me = 4 * x + 2 * y + c
    cps = []
    for k, (fx, fy, fc) in enumerate([(a, b, d) for a in (0, 1) for b in (0, 1) for d in (0, 1)][1:]):
        peer = (1 - x if fx else x, 1 - y if fy else y, 1 - c if fc else c)
        cps.append(pltpu.make_async_remote_copy(
            src_ref=src, dst_ref=land.at[me], send_sem=send_sems.at[k], recv_sem=recv_sems.at[k],
            device_id=peer, device_id_type=MESH))
    return cps


def gather8_start(v, after, name):
    R, W = v.shape

    def body(v_ref, land_ref, aft_ref, send_sems, recv_sems, v_thru, land_thru, token):
        for cp in _all8_copies(v_ref, land_ref, send_sems, recv_sems):
            cp.start()
        token[...] = jnp.zeros_like(token)

    land = pltpu.with_memory_space_constraint(lax.empty((N_DEV, R, W), v.dtype), pltpu.HBM)
    outs = pl.pallas_call(
        body, name=name,
        out_shape=(pltpu.SemaphoreType.DMA((N_DEV - 1,)), pltpu.SemaphoreType.DMA((N_DEV - 1,)),
                   pltpu.HBM(v.shape, v.dtype), pltpu.HBM((N_DEV, R, W), v.dtype), jax.ShapeDtypeStruct((8, 128), F32)),
        in_specs=[_HBM, _HBM, _ANY],
        out_specs=(_SEM, _SEM, _HBM, _HBM, pl.BlockSpec(memory_space=pltpu.VMEM)),
        input_output_aliases={0: 2, 1: 3},
        compiler_params=pltpu.CompilerParams(has_side_effects=_EFFECT),
    )(pltpu.with_memory_space_constraint(v, pltpu.HBM), land, after)
    return outs


def gather8_wait(send_sems, recv_sems, v, land, after, name):
    def body(v_ref, land_ref, ssem, rsem, aft_ref, v_dead, land_out):
        for cp in _all8_copies(v_ref, land_ref, ssem, rsem):
            cp.wait_send()
            cp.wait_recv()

    return pl.pallas_call(
        body, name=name, out_shape=[pltpu.HBM(v.shape, v.dtype), pltpu.HBM(land.shape, land.dtype)],
        in_specs=[_HBM, _HBM, _SEM, _SEM, _ANY], out_specs=[_HBM, _HBM], input_output_aliases={0: 0, 1: 1},
        compiler_params=pltpu.CompilerParams(has_side_effects=_EFFECT),
    )(v, land, send_sems, recv_sems, after)[1]


def _pass_copies(bufs, send_sems, recv_sems):
    x, y, c, chips = _place()
    cps = []
    for i in range(len(bufs)):
        for j, (cx, cy) in enumerate(chips):
            blk = bufs[i].at[2 * cx + cy, c]
            cps.append(pltpu.make_async_remote_copy(
                src_ref=blk, dst_ref=blk, send_sem=send_sems.at[3 * i + j], recv_sem=recv_sems.at[3 * i + j],
                device_id=(x, y, 1 - c), device_id_type=MESH))
    return cps


def pass_start(bufs, after, name):
    n = len(bufs)

    def body(*refs):
        send_sems, recv_sems = refs[n + 1], refs[n + 2]
        for cp in _pass_copies(refs[:n], send_sems, recv_sems):
            cp.start()
        refs[-1][...] = jnp.zeros_like(refs[-1])

    outs = pl.pallas_call(
        body, name=name,
        out_shape=(pltpu.SemaphoreType.DMA((3 * n,)), pltpu.SemaphoreType.DMA((3 * n,)),
                   *[pltpu.HBM(b.shape, b.dtype) for b in bufs], jax.ShapeDtypeStruct((8, 128), F32)),
        in_specs=[_HBM] * n + [_ANY],
        out_specs=(_SEM, _SEM, *([_HBM] * n), pl.BlockSpec(memory_space=pltpu.VMEM)),
        input_output_aliases={i: 2 + i for i in range(n)},
        compiler_params=pltpu.CompilerParams(has_side_effects=_EFFECT),
    )(*[pltpu.with_memory_space_constraint(b, pltpu.HBM) for b in bufs], after)
    return outs[0], outs[1], outs[2:2 + n], outs[-1]


def pass_wait(send_sems, recv_sems, bufs, after, name):
    n = len(bufs)

    def body(*refs):
        for cp in _pass_copies(refs[:n], refs[n], refs[n + 1]):
            cp.wait_send()
            cp.wait_recv()

    return pl.pallas_call(
        body, name=name, out_shape=[pltpu.HBM(b.shape, b.dtype) for b in bufs],
        in_specs=[_HBM] * n + [_SEM, _SEM, _ANY], out_specs=[_HBM] * n,
        input_output_aliases={i: i for i in range(n)},
        compiler_params=pltpu.CompilerParams(has_side_effects=_EFFECT),
    )(*bufs, send_sems, recv_sems, after)


def exchange_halves_to_sibling(gs, name, by_cols=False):
    n = len(gs)

    def body(*refs):
        ins, outs = refs[:n], refs[n:2 * n]
        send_sems, recv_sems = refs[2 * n:]
        x, y, c, _ = _place()
        cps = []
        for i in range(n):
            if by_cols:
                hc = ins[i].shape[2] // 2
                src = ins[i].at[:, :, pl.ds(pl.multiple_of((1 - c) * hc, 128), hc)]
            else:
                h = ins[i].shape[1] // 2
                src = ins[i].at[:, pl.ds((1 - c) * h, h), :]
            cps.append(pltpu.make_async_remote_copy(
                src_ref=src, dst_ref=outs[i],
                send_sem=send_sems.at[i], recv_sem=recv_sems.at[i], device_id=(x, y, 1 - c), device_id_type=MESH))
        for cp in cps:
            cp.start()
        for cp in cps:
            cp.wait()

    halve = (lambda s: (s[0], s[1], s[2] // 2)) if by_cols else (lambda s: (s[0], s[1] // 2, s[2]))
    return pl.pallas_call(
        body, out_shape=[jax.ShapeDtypeStruct(halve(g.shape), g.dtype) for g in gs],
        in_specs=[_ANY] * n, out_specs=[_ANY] * n,
        scratch_shapes=[pltpu.SemaphoreType.DMA((n,)), pltpu.SemaphoreType.DMA((n,))],
        name=name)(*gs)


def join_halves(rs, name):
    n = len(rs)

    def body(*refs):
        ins, outs = refs[:n], refs[n:2 * n]
        send_sems, recv_sems = refs[2 * n:]
        x, y, c, _ = _place()
        cps = [pltpu.make_async_remote_copy(
            src_ref=ins[i], dst_ref=outs[i], send_sem=send_sems.at[i], recv_sem=recv_sems.at[i],
            device_id=(x, y, 1 - c), device_id_type=MESH) for i in range(n)]
        for cp in cps:
            cp.start()
        for cp in cps:
            cp.wait()

    return pl.pallas_call(
        body, out_shape=[jax.ShapeDtypeStruct(r.shape, r.dtype) for r in rs],
        in_specs=[_ANY] * n, out_specs=[_ANY] * n,
        scratch_shapes=[pltpu.SemaphoreType.DMA((n,)), pltpu.SemaphoreType.DMA((n,))],
        name=name)(*rs)


def _pack(parts, row_mult=8):
    flat = jnp.concatenate([p.reshape(-1).astype(F32) for p in parts])
    unit = row_mult * 128
    n = -(-flat.shape[0] // unit) * unit
    return jnp.pad(flat, (0, n - flat.shape[0])).reshape(n // 128, 128)


def _unpack(flat, shapes):
    out, off = [], 0
    for s in shapes:
        n = int(np.prod(s))
        out.append(flat[off:off + n].reshape(s))
        off += n
    return out


def _gather_packed(parts, name):
    packed = _pack(parts)
    g = allgather_small(packed, name).reshape(N_DEV, -1)
    return _unpack_rows(g, [p.shape for p in parts])


def _unpack_rows(g, shapes):
    out, off = [], 0
    for s in shapes:
        n = int(np.prod(s))
        out.append(g[:, off:off + n].reshape((g.shape[0],) + tuple(s)))
        off += n
    return out


def _by_chip(t, axis):
    return jnp.concatenate([t[2 * p] for p in range(N_CHIPS)], axis=axis)


def kernel(x, c, ada_w, ada_b, ln_g, ln_b, a_in_w, a_conv_w, a_conv_b, a_dt_bias, a_A_log, a_D, a_norm_g, a_out_w, kv_w, b_in_w, b_out_w, loss_target, m_ada_w, m_ada_b, m_ln_g, m_ln_b, m_a_in_w, m_a_conv_w, m_a_conv_b, m_a_dt_bias, m_a_A_log, m_a_D, m_a_norm_g, m_a_out_w, m_kv_w, m_b_in_w, m_b_out_w, v_ada_w, v_ada_b, v_ln_g, v_ln_b, v_a_in_w, v_a_conv_w, v_a_conv_b, v_a_dt_bias, v_a_A_log, v_a_D, v_a_norm_g, v_a_out_w, v_kv_w, v_b_in_w, v_b_out_w):
    ax, ay, ac = lax.axis_index("x"), lax.axis_index("y"), lax.axis_index("c")
    chip = 2 * ax + ay
    dev = 4 * ax + 2 * ay + ac
    xin = x[0]
    tgt = loss_target[0]
    L, D = xin.shape
    G, P = SSD_G, SSD_P
    H = a_dt_bias.shape[1]
    Kh = H // G
    DI = H * P
    CONVD = a_conv_b.shape[1] * N_CHIPS
    HW = DIL_H * DIL_E
    Ws = ada_w.shape[2]

    w_in_g = allgather_routed(jnp.transpose(a_in_w[0]).astype(BF16), "allgather_w_in")
    later = [a_out_w[0].astype(BF16), kv_w.astype(BF16), b_in_w[0].astype(BF16), b_out_w[0].astype(BF16)]
    later_split = [s.reshape(2, s.shape[0] // 2, s.shape[1]) for s in later]
    ag_ssem, ag_rsem, ag_srcs, ag_lands, ag_token = split_start(
        "gather", later_split, [(N_CHIPS,) + s.shape for s in later_split], w_in_g, "ag_later_start")
    w_in_t = w_in_g.reshape(-1, D)
    w_dt_t = jnp.pad(w_in_t[DI + CONVD:], ((0, 128 - H), (0, 0)))

    c8, cw8, cb8, ng8 = _gather_packed([c[0], a_conv_w[0], a_conv_b[0], a_norm_g[0]], "allgather_small_params")
    conv_w = _by_chip(cw8, 1)
    conv_b = _by_chip(cb8, 0).reshape(1, CONVD)
    norm_g = _by_chip(ng8, 0).reshape(1, DI)

    mod_s = ada_fwd(c8, ada_w)
    (mod8,) = _gather_packed([mod_s], "allgather_small_mod")
    mods = _by_chip(mod8, 2)
    mod = lax.dynamic_index_in_dim(mods, dev, axis=1, keepdims=False) + ada_b
    shift = [mod[l:l + 1, :D] for l in range(DEPTH)]
    scale = [mod[l:l + 1, D:2 * D] for l in range(DEPTH)]
    gate = [mod[l:l + 1, 2 * D:] for l in range(DEPTH)]
    lg = [ln_g[l:l + 1] for l in range(DEPTH)]
    lb = [ln_b[l:l + 1] for l in range(DEPTH)]

    h0 = modulate(xin, scale[0] + ag_token[0:1, 0:1], shift[0], "modulate0")
    zx = mm_nt(h0, w_in_t, BF16, "mm_in_zx", kw_rows=DI + CONVD)
    dtp = mm_nt(h0, w_dt_t, F32, "mm_in_dt")
    xbc = conv_fwd(zx, DI, conv_w, conv_b)
    dtp_g = jnp.transpose(dtp[:, :H].reshape(L, G, Kh), (1, 0, 2))
    dtp_gT = jnp.transpose(dtp_g, (0, 2, 1))
    vecs = [a_dt_bias.reshape(G, 1, Kh), a_dt_bias.reshape(G, Kh, 1), a_A_log.reshape(G, 1, Kh),
            a_A_log.reshape(G, Kh, 1), a_D.reshape(G, 1, Kh), a_D.reshape(G, Kh, 1)]
    y_ssd, states, yn = ssd_fwd(xbc, dtp_g, dtp_gT, *vecs, zx, norm_g, DI)
    later_split, ag_lands = split_wait("gather", ag_ssem, ag_rsem, ag_srcs, ag_lands, yn, "ag_later_wait")
    (land_out,) = pass_to_sibling(ag_lands[:1])
    ps_ssem, ps_rsem, lands_b, ps_token = pass_start(ag_lands[1:], land_out, "ag_pass_start")

    def place_own(o, s, full):
        return lax.dynamic_update_index_in_dim(o, s, chip, 0).reshape((N_CHIPS,) + full.shape)

    w_out_g = place_own(land_out, later_split[0], later[0])
    ymix0 = mm_nn(yn, w_out_g.reshape(-1, D), F32, "mm_out_a", after=ps_token)
    x1, x1b, h1 = ln_mid(xin, ymix0, gate[0], lg[0], lb[0], scale[1], shift[1])
    lands_b = pass_wait(ps_ssem, ps_rsem, lands_b, x1b, "ag_pass_wait")
    w_kv_g, w_bin_g, w_bout_g = [place_own(o, s, full) for o, s, full in zip(lands_b, later_split[1:], later[1:])]

    n_grp = len(DIL_PATTERNS)
    cb = HW // 512
    assert w_bin_g.shape[2] == HW
    kv3 = [mm_cols_dilated(x1b, w_kv_g, [g * cb + t for t in range(cb)] + [(n_grp + g) * cb + t for t in range(cb)],
                           DIL_PATTERNS[g][1], f"mm_kv_{g}") for g in range(n_grp)]
    q3 = [mm_cols_dilated(h1, w_bin_g, [g], DIL_PATTERNS[g][1], f"mm_q_{g}", tn=HW) for g in range(n_grp)]
    z_b = mm_nn(h1, w_bin_g[n_grp], BF16, "mm_z_b")
    os_, lses = [], []
    for gi in range(len(DIL_PATTERNS)):
        o, lse = attn_fwd(q3[gi], kv3[gi], gi)
        os_.append(o)
        lses.append(lse)
    om = merge_fwd(os_, lses, z_b)
    ymix1 = mm_nn(om, w_bout_g, F32, "mm_out_b", stack="col")
    dres2, dy2, dg1, db1, dgate1, sq = ln_final_fwd_bwd(x1, ymix1, gate[1], lg[1], lb[1], tgt)
    loss_part = 0.5 * jnp.sum(sq) / D

    g_bout = mm_tn(om, dy2, BF16, "mm_gw_out_b", stack="col")
    dgated = mm_nt(dy2, w_bout_g, BF16, "mm_gx_out_b", stack="col")
    dos, dprs, dz_b = merge_bwd(dgated, os_, lses, z_b)
    dqs, dks, dvs = [], [], []
    for gi in range(len(DIL_PATTERNS)):
        dq, dk, dv = attn_bwd(q3[gi], kv3[gi], dos[gi], lses[gi], dprs[gi], gi)
        dqs.append(dq)
        dks.append(dk)
        dvs.append(dv)
    dqz = jnp.concatenate(dqs + [dz_b], axis=1)
    dkv = jnp.concatenate(dks + dvs, axis=1)
    g_bin = mm_tn(h1, dqz, BF16, "mm_gw_in_b", stack="col")
    dh1 = mm_nt(dqz, w_bin_g, BF16, "mm_gx_in_b", stack="col")
    g_kv = mm_tn(x1b, dkv, BF16, "mm_gw_kv", stack="col")

    core = ac.astype(jnp.int32).reshape(1)
    chip_i = chip.astype(jnp.int32).reshape(1)

    def begin_exchange(gs, tag):
        shapes = [(g.shape[0], g.shape[1] // 2, g.shape[2]) for g in gs]
        return split_start("sibling", gs, shapes, gs[0], "rs_x%s_start" % tag)

    def begin_scatter(gs, nms, tag, exchange=None, after=None, by_cols=False):
        if exchange is None:
            sib = exchange_halves_to_sibling(gs, "rs_sibling_exchange_" + tag, by_cols=by_cols)
        else:
            gs, sib = split_wait("sibling", exchange[0], exchange[1], exchange[2], exchange[3], after,
                                 "rs_x%s_wait" % tag)
        parts = [add_half(g, a, core, "rs_add_" + nm, by_cols=by_cols) for g, a, nm in zip(gs, sib, nms)]
        return split_start("scatter", parts, [(3,) + t.shape[1:] for t in parts], parts[0], "rs_%s_start" % tag)

    def finish_scatter(handles, after, tag):
        nms, owns, landed = [], [], []
        for k, (handle, hn) in enumerate(handles):
            parts, lands = split_wait("scatter", handle[0], handle[1], handle[2], handle[3], after,
                                      "rs_%s%d_wait" % (tag, k))
            nms += hn
            owns += list(parts)
            landed += list(lands)
        halves = [sum_partials(own, t, chip_i, "rs_sum_" + nm) for own, t, nm in zip(owns, landed, nms)]
        theirs = join_halves(halves, "rs_join_halves_" + tag)
        return dict(zip(nms, zip(halves, theirs)))

    names_b = ["kv", "in_b", "out_b"]
    ex_b = begin_exchange([g_kv, g_bin, g_bout], "b")
    dx1_kv = mm_nt(dkv, w_kv_g, BF16, "mm_gx_kv", stack="col", after=ex_b[4])
    rs_b = begin_scatter(None, names_b, "b", exchange=ex_b, after=dx1_kv)

    dres1, dy1, dg0, db0, dgate0, dscale1, dshift1 = mod_ln_bwd(
        dres2, dh1, dx1_kv, x1, scale[1], xin, ymix0, gate[0] + rs_b[4][0:1, 0:1], lg[0])
    g_out = mm_tn(yn, dy1, BF16, "mm_gw_out_a", stack="row")
    ex_a1 = begin_exchange([g_out], "a1")
    dyn = mm_nt(dy1, w_out_g, BF16, "mm_gx_out_a", stack="row", after=ex_a1[4])
    rs_a1 = begin_scatter(None, ["out_a"], "a1", exchange=ex_a1, after=dyn)
    dxs, dB, dC, ddtp_g, dbias_g, dalog_g, dD_g, dz_a, dnorm_g = ssd_bwd(
        xbc, dtp_g, dtp_gT, *vecs, states, dyn, y_ssd, zx, norm_g + rs_a1[4][0:1, 0:1], DI)
    dzx, dws, dbs, lo = dz_a, [], [], 0
    for tag, gpart in (("xs", dxs), ("b", dB), ("c", dC)):
        hi = lo + gpart.shape[1]
        dzx, dw_p, db_p = conv_bwd(zx, DI + lo, conv_w[:, lo:hi], conv_b[:, lo:hi], gpart, dzx, "conv_bwd_" + tag)
        dws.append(dw_p)
        dbs.append(db_p)
        lo = hi
    dconv_w = jnp.concatenate(dws, axis=1)
    dconv_b = jnp.concatenate(dbs, axis=1)
    ddtp = jnp.pad(jnp.transpose(ddtp_g, (1, 0, 2)).reshape(L, H), ((0, 0), (0, 128 - H)))
    g_inT = mm_tn(dzx, h0, BF16, "mm_gw_in_zx", m_rows=DI + CONVD + H)
    g_dtT = mm_tn(ddtp, h0, BF16, "mm_gw_in_dt")
    g_inT = lax.dynamic_update_slice(g_inT, g_dtT[:H], (DI + CONVD, 0))
    rs_a2 = begin_scatter([g_inT.reshape(N_CHIPS, -1, D)], ["in_a"], "a2", by_cols=True)
    dh0 = mm_nn(dzx, w_in_t, BF16, "mm_gx_in_zx", after=rs_a2[4])
    dh0_dt = mm_nn(ddtp, w_dt_t, F32, "mm_gx_in_dt")
    grad_x, dscale0, dshift0 = mod_bwd(dres1, dh0, dh0_dt, xin, scale[0] + rs_a2[4][0:1, 0:1], "mod_bwd0")
    g_halves = finish_scatter([(rs_b, names_b)], grad_x, "b")

    def step_halves(w, m, v, nm, after=None):
        shp = w.shape
        mine, theirs_ = g_halves[nm]
        outs4 = adamw_halves(w.reshape(-1, shp[-1]), mine, theirs_, m.reshape(-1, shp[-1]), v.reshape(-1, shp[-1]),
                             core, "adamw_" + nm, after=after)
        return tuple(t.reshape(shp) for t in outs4)

    big = {
        "kv_w": step_halves(kv_w, m_kv_w, v_kv_w, "kv"),
        "b_in_w": step_halves(b_in_w, m_b_in_w, v_b_in_w, "in_b"),
        "b_out_w": step_halves(b_out_w, m_b_out_w, v_b_out_w, "out_b"),
    }
    g_halves.update(finish_scatter([(rs_a1, ["out_a"]), (rs_a2, ["in_a"])], big["kv_w"][1], "a"))
    g_halves["in_a"] = tuple(jnp.transpose(t) for t in g_halves["in_a"])

    dmod = jnp.concatenate([jnp.concatenate([dshift0, dscale0, dgate0], axis=1),
                            jnp.concatenate([dshift1, dscale1, dgate1], axis=1)], axis=0)
    small_parts = [jnp.concatenate([dg0, dg1], axis=0), jnp.concatenate([db0, db1], axis=0),
                   dbias_g.reshape(1, H), dalog_g.reshape(1, H), dD_g.reshape(1, H),
                   dconv_w, dconv_b, dnorm_g, loss_part.reshape(1, 1)]
    small_shapes = [p.shape for p in small_parts]
    packed = jnp.concatenate([_pack([dmod]), _pack(small_parts)], axis=0)
    n_mod_rows = _pack([dmod]).shape[0]
    sg_ssem, sg_rsem, sg_src, sg_land, sg_token = gather8_start(packed, g_halves["in_a"][1], "small_grads_start")
    big["a_in_w"] = step_halves(a_in_w, m_a_in_w, v_a_in_w, "in_a", after=sg_token)
    big["a_out_w"] = step_halves(a_out_w, m_a_out_w, v_a_out_w, "out_a")
    sg_land = gather8_wait(sg_ssem, sg_rsem, sg_src, sg_land, big["a_in_w"][1], "small_grads_wait")
    gathered = lax.dynamic_update_index_in_dim(sg_land, packed, dev, 0)
    dmod8 = gathered[:, :n_mod_rows].reshape(N_DEV, -1)[:, :2 * 3 * D].reshape(N_DEV, DEPTH, 3 * D)
    summed = sum_leading(gathered, "sum_small")
    g_ada_b = summed[:n_mod_rows].reshape(-1)[:2 * 3 * D].reshape(DEPTH, 3 * D)
    (g_ln_g, g_ln_b, g_dt_bias, g_a_log, g_dsk, g_conv_w, g_conv_b, g_norm_g, loss_all) = _unpack(
        summed[n_mod_rows:].reshape(-1), small_shapes)
    loss = loss_all.reshape(())
    Cs = CONVD // N_CHIPS
    g_conv_w_s = lax.dynamic_slice_in_dim(g_conv_w, chip * Cs, Cs, axis=1)
    g_conv_b_s = lax.dynamic_slice_in_dim(g_conv_b, chip * Cs, Cs, axis=1)
    g_norm_g_s = lax.dynamic_slice_in_dim(g_norm_g, chip * (DI // N_CHIPS), DI // N_CHIPS, axis=1)
    dmod_s = jnp.transpose(lax.dynamic_slice_in_dim(dmod8, chip * Ws, Ws, axis=2), (1, 0, 2))

    def step2d(w, g, m, v, nm):
        shp = w.shape
        d_, m_, v_ = adamw(w.reshape(-1, shp[-1]), g.reshape(-1, shp[-1]), m.reshape(-1, shp[-1]),
                           v.reshape(-1, shp[-1]), "adamw_" + nm)
        return g.reshape(shp), d_.reshape(shp), m_.reshape(shp), v_.reshape(shp)

    big["ada_w"] = step2d(ada_w, ada_wgrad(jnp.transpose(c8), dmod_s), m_ada_w, v_ada_w, "ada_w")
    small_names = ["ada_b", "ln_g", "ln_b", "a_conv_w", "a_conv_b", "a_dt_bias", "a_A_log", "a_D", "a_norm_g"]
    small_w = [ada_b, ln_g, ln_b, a_conv_w, a_conv_b, a_dt_bias, a_A_log, a_D, a_norm_g]
    small_m = [m_ada_b, m_ln_g, m_ln_b, m_a_conv_w, m_a_conv_b, m_a_dt_bias, m_a_A_log, m_a_D, m_a_norm_g]
    small_v = [v_ada_b, v_ln_g, v_ln_b, v_a_conv_w, v_a_conv_b, v_a_dt_bias, v_a_A_log, v_a_D, v_a_norm_g]
    small_g = [g_ada_b, g_ln_g, g_ln_b, g_conv_w_s, g_conv_b_s, g_dt_bias, g_a_log, g_dsk, g_norm_g_s]
    shapes = [w.shape for w in small_w]
    small_g = [g.reshape(s) for g, s in zip(small_g, shapes)]
    d_p, m_p, v_p = adamw(_pack(small_w), _pack(small_g), _pack(small_m), _pack(small_v), "adamw_small")
    small = {}
    for nm, g, d_, m_, v_ in zip(small_names, small_g, _unpack(d_p.reshape(-1), shapes), _unpack(m_p.reshape(-1), shapes),
                                 _unpack(v_p.reshape(-1), shapes)):
        small[nm] = (g, d_, m_, v_)
    allw = {**big, **small}
    order = ["ada_w", "ada_b", "ln_g", "ln_b", "a_in_w", "a_conv_w", "a_conv_b", "a_dt_bias", "a_A_log", "a_D",
             "a_norm_g", "a_out_w", "kv_w", "b_in_w", "b_out_w"]
    outs = [loss, grad_x.reshape(x.shape)]
    for k in range(4):
        outs += [allw[n][k] for n in order]
    return tuple(outs)
```

```python
import functools

import jax
import jax.numpy as jnp
import numpy as np
from jax import lax
from jax.experimental import pallas as pl
from jax.experimental.pallas import tpu as pltpu

F32 = jnp.float32
BF16 = jnp.bfloat16
MESH = pl.DeviceIdType.MESH

DEPTH = 2
ALPHA = (2 * DEPTH) ** 0.25
LN_EPS = 1e-5
RMS_EPS = 1e-5
SSD_P = 64
SSD_N = 128
SSD_Q = 256
SSD_G = 8
CONV_W = 4
DIL_PATTERNS = ((128, 1), (512, 4), (2048, 16))
DIL_H = 8
DIL_E = 128
DIL_BLK = 128
ADAM_LR, ADAM_B1, ADAM_B2, ADAM_EPS, ADAM_WD, ADAM_STEP = 0.001, 0.9, 0.999, 1e-08, 0.01, 10

VMEM_LIMIT = 56 * 1024 * 1024
N_CHIPS = 4
N_DEV = 8


def _tile(dim, target, mult=128):
    if dim <= target:
        return dim
    t = (target // mult) * mult
    while t >= mult:
        if dim % t == 0:
            return t
        t -= mult
    return dim


def _cp(sem):
    return pltpu.CompilerParams(dimension_semantics=sem, vmem_limit_bytes=VMEM_LIMIT)


def _sigmoid(x):
    return 1.0 / (1.0 + jnp.exp(-x))


def _silu(x):
    return x * _sigmoid(x)


def _dsilu(x):
    s = _sigmoid(x)
    return s * (1.0 + x * (1.0 - s))


def _softplus(x):
    return jnp.maximum(x, 0.0) + jnp.log(1.0 + jnp.exp(-jnp.abs(x)))


def _mm_call(a, b, out_shape, grid, a_spec, b_spec, o_spec, acc_shape, dims, name, after=None):
    nk = grid[2]
    extra = [] if after is None else [after]

    def prod(a_ref, b_ref):
        return lax.dot_general(a_ref[...].astype(BF16), b_ref[...].astype(BF16), (dims, ((), ())),
                               preferred_element_type=F32)

    def body_single(a_ref, b_ref, *rest):
        o_ref = rest[len(extra)]
        o_ref[...] = prod(a_ref, b_ref).astype(o_ref.dtype)

    def body_multi(a_ref, b_ref, *rest):
        o_ref, acc_ref = rest[len(extra):]
        k = pl.program_id(2)

        @pl.when(k == 0)
        def _():
            acc_ref[...] = prod(a_ref, b_ref)

        @pl.when(jnp.logical_and(k > 0, k < nk - 1))
        def _():
            acc_ref[...] += prod(a_ref, b_ref)

        @pl.when(k == nk - 1)
        def _():
            o_ref[...] = (acc_ref[...] + prod(a_ref, b_ref)).astype(o_ref.dtype)

    return pl.pallas_call(
        body_single if nk == 1 else body_multi, grid=grid, in_specs=[a_spec, b_spec] + [_ANY] * len(extra),
        out_specs=o_spec, out_shape=out_shape, scratch_shapes=[] if nk == 1 else [pltpu.VMEM(acc_shape, F32)],
        compiler_params=_cp(("parallel", "parallel", "arbitrary")), name=name)(a, b, *extra)


def mm_nn(a, b, out_dtype, name, stack=None, tm=1024, tn=1024, tk=2048, n_cols=None, after=None):
    M, K = a.shape
    if stack is None:
        N = b.shape[1] if n_cols is None else n_cols
        tn, tk = _tile(N, tn), _tile(K, tk)
        b_spec = pl.BlockSpec((tk, tn), lambda i, j, k: (k, j))
    elif stack == "col":
        S, _, Ns = b.shape
        N = S * Ns
        tn, tk = _tile(Ns, tn), _tile(K, tk)
        npb = Ns // tn
        b_spec = pl.BlockSpec((None, tk, tn), lambda i, j, k: (j // npb, k, j % npb))
    else:
        S, Ks, N = b.shape
        tn, tk = _tile(N, tn), _tile(Ks, tk)
        kpb = Ks // tk
        b_spec = pl.BlockSpec((None, tk, tn), lambda i, j, k: (k // kpb, k % kpb, j))
    tm = _tile(M, tm)
    return _mm_call(a, b, jax.ShapeDtypeStruct((M, N), out_dtype), (M // tm, N // tn, K // tk),
                    pl.BlockSpec((tm, tk), lambda i, j, k: (i, k)), b_spec,
                    pl.BlockSpec((tm, tn), lambda i, j, k: (i, j)), (tm, tn), ((1,), (0,)), name, after=after)


def mm_cols_dilated(a, b, gcols, d, name, tm=1024, tn=512):
    L, K = a.shape
    S, _, Ns = b.shape
    tm, tn = _tile(L, tm), _tile(Ns, tn)
    npb = Ns // tn
    nj = len(gcols)
    rows = tm // d

    def body(cols_ref, a_ref, b_ref, o_ref, *scr):
        prod = jnp.dot(a_ref[...], b_ref[...], preferred_element_type=F32)
        if d == 1:
            o_ref[0] = prod.astype(BF16)
        else:
            for c in range(tn // 128):
                scr[0][c] = prod[:, c * 128:(c + 1) * 128]
            for r in range(d):
                for c in range(tn // 128):
                    o_ref[r, :, c * 128:(c + 1) * 128] = scr[0].at[c][pl.ds(r, rows, stride=d), :].astype(BF16)

    return pl.pallas_call(
        body,
        grid_spec=pltpu.PrefetchScalarGridSpec(
            num_scalar_prefetch=1, grid=(L // tm, nj),
            in_specs=[pl.BlockSpec((tm, K), lambda i, j, c: (i, 0)),
                      pl.BlockSpec((None, K, tn), lambda i, j, c: (c[j] // npb, 0, c[j] % npb))],
            out_specs=pl.BlockSpec((d, rows, tn), lambda i, j, c: (0, i, j)),
            scratch_shapes=[] if d == 1 else [pltpu.VMEM((tn // 128, tm, 128), F32)]),
        out_shape=jax.ShapeDtypeStruct((d, L // d, nj * tn), BF16),
        compiler_params=_cp(("parallel", "arbitrary")), name=name)(jnp.asarray(gcols, jnp.int32), a, b)


def mm_nt(a, b, out_dtype, name, stack=None, tm=1024, tn=1024, tk=2048, after=None, kw_rows=None):
    M, C = a.shape
    if stack is None:
        Kw = b.shape[0] if kw_rows is None else kw_rows
        tn, tk = _tile(Kw, tn), _tile(C, tk)
        b_spec = pl.BlockSpec((tn, tk), lambda i, j, k: (j, k))
    elif stack == "col":
        S, Kw, Cs = b.shape
        tn, tk = _tile(Kw, tn), _tile(Cs, tk)
        cpb = Cs // tk
        b_spec = pl.BlockSpec((None, tn, tk), lambda i, j, k: (k // cpb, j, k % cpb))
    else:
        S, Ks, _ = b.shape
        Kw = S * Ks
        tn, tk = _tile(Ks, tn), _tile(C, tk)
        jpb = Ks // tn
        b_spec = pl.BlockSpec((None, tn, tk), lambda i, j, k: (j // jpb, j % jpb, k))
    tm = _tile(M, tm)
    return _mm_call(a, b, jax.ShapeDtypeStruct((M, Kw), out_dtype), (M // tm, Kw // tn, C // tk),
                    pl.BlockSpec((tm, tk), lambda i, j, k: (i, k)), b_spec,
                    pl.BlockSpec((tm, tn), lambda i, j, k: (i, j)), (tm, tn), ((1,), (1,)), name, after=after)


def mm_tn(a, b, out_dtype, name, stack=None, n_stack=N_CHIPS, tm=1024, tn=1024, tk=2048, m_rows=None):
    L, M = a.shape
    N = b.shape[1]
    tk = _tile(L, tk)
    if stack is None:
        tm, tn = _tile(M, tm), _tile(N, tn)
        o_spec = pl.BlockSpec((tm, tn), lambda i, j, k: (i, j))
        out_shape = (M if m_rows is None else m_rows, N)
    elif stack == "col":
        Ns = N // n_stack
        tm, tn = _tile(M, tm), _tile(Ns, tn)
        npb = Ns // tn
        o_spec = pl.BlockSpec((None, tm, tn), lambda i, j, k: (j // npb, i, j % npb))
        out_shape = (n_stack, M, Ns)
    else:
        Ms = M // n_stack
        tm, tn = _tile(Ms, tm), _tile(N, tn)
        mpb = Ms // tm
        o_spec = pl.BlockSpec((None, tm, tn), lambda i, j, k: (i // mpb, i % mpb, j))
        out_shape = (n_stack, Ms, N)
    return _mm_call(a, b, jax.ShapeDtypeStruct(out_shape, out_dtype), (M // tm, N // tn, L // tk),
                    pl.BlockSpec((tk, tm), lambda i, j, k: (k, i)), pl.BlockSpec((tk, tn), lambda i, j, k: (k, j)),
                    o_spec, (tm, tn), ((0,), (0,)), name)


def _row_specs(tr, widths):
    return [pl.BlockSpec((tr, w), lambda i: (i, 0)) for w in widths]


def _vec_spec(w):
    return pl.BlockSpec((1, w), lambda i: (0, 0))


def _acc_rows(ref, val, i):
    s = jnp.sum(val, axis=0, keepdims=True)

    @pl.when(i == 0)
    def _():
        ref[...] = s

    @pl.when(i > 0)
    def _():
        ref[...] += s


def modulate(x, scale, shift, name):
    L, D = x.shape
    tr = _tile(L, 512, 16)

    def body(x_ref, sc_ref, sh_ref, h_ref):
        h_ref[...] = (x_ref[...] * (1.0 + sc_ref[...]) + sh_ref[...]).astype(BF16)

    return pl.pallas_call(
        body, grid=(L // tr,), in_specs=_row_specs(tr, [D]) + [_vec_spec(D)] * 2, out_specs=_row_specs(tr, [D])[0],
        out_shape=jax.ShapeDtypeStruct((L, D), BF16), compiler_params=_cp(("parallel",)), name=name)(x, scale, shift)


def _ln_core(x, y, gate, g, b):
    u = ALPHA * x + (1.0 + gate) * y
    mu = jnp.mean(u, axis=-1, keepdims=True)
    d = u - mu
    var = jnp.mean(d * d, axis=-1, keepdims=True)
    rstd = lax.rsqrt(var + LN_EPS)
    xhat = d * rstd
    return xhat * g + b, xhat, rstd


def ln_mid(x, y, gate, g, b, scale, shift):
    L, D = x.shape
    tr = _tile(L, 256, 16)

    def body(x_ref, y_ref, gate_ref, g_ref, b_ref, sc_ref, sh_ref, x1_ref, x1b_ref, h_ref):
        x1, _, _ = _ln_core(x_ref[...], y_ref[...], gate_ref[...], g_ref[...], b_ref[...])
        x1_ref[...] = x1
        x1b_ref[...] = x1.astype(BF16)
        h_ref[...] = (x1 * (1.0 + sc_ref[...]) + sh_ref[...]).astype(BF16)

    return pl.pallas_call(
        body, grid=(L // tr,), in_specs=_row_specs(tr, [D, D]) + [_vec_spec(D)] * 5,
        out_specs=_row_specs(tr, [D, D, D]),
        out_shape=[jax.ShapeDtypeStruct((L, D), F32), jax.ShapeDtypeStruct((L, D), BF16),
                   jax.ShapeDtypeStruct((L, D), BF16)],
        compiler_params=_cp(("parallel",)), name="ln_mid")(x, y, gate, g, b, scale, shift)


def _ln_bwd_rows(dout_v, xhat, rstd, g):
    dxh = dout_v * g
    m1 = jnp.mean(dxh, axis=-1, keepdims=True)
    m2 = jnp.mean(dxh * xhat, axis=-1, keepdims=True)
    return rstd * (dxh - m1 - xhat * m2)


def ln_final_fwd_bwd(x, y, gate, g, b, target):
    L, D = x.shape
    tr = _tile(L, 256, 16)

    def body(x_ref, y_ref, gate_ref, g_ref, b_ref, t_ref, dres_ref, dy_ref, dg_ref, db_ref, dgate_ref, sq_ref):
        i = pl.program_id(0)
        yv = y_ref[...]
        out, xhat, rstd = _ln_core(x_ref[...], yv, gate_ref[...], g_ref[...], b_ref[...])
        err = out - t_ref[...]
        dout_v = err * (1.0 / D)
        du = _ln_bwd_rows(dout_v, xhat, rstd, g_ref[...])
        dres_ref[...] = ALPHA * du
        dy_ref[...] = ((1.0 + gate_ref[...]) * du).astype(BF16)
        _acc_rows(dg_ref, dout_v * xhat, i)
        _acc_rows(db_ref, dout_v, i)
        _acc_rows(dgate_ref, du * yv, i)
        _acc_rows(sq_ref, err * err, i)

    return pl.pallas_call(
        body, grid=(L // tr,), in_specs=_row_specs(tr, [D, D]) + [_vec_spec(D)] * 3 + _row_specs(tr, [D]),
        out_specs=_row_specs(tr, [D, D]) + [_vec_spec(D)] * 4,
        out_shape=[jax.ShapeDtypeStruct((L, D), F32), jax.ShapeDtypeStruct((L, D), BF16)]
        + [jax.ShapeDtypeStruct((1, D), F32)] * 4,
        compiler_params=_cp(("arbitrary",)), name="ln_final_fwd_bwd")(x, y, gate, g, b, target)


def mod_bwd(dres, dh, dh2, xin, scale, name):
    L, D = xin.shape
    tr = _tile(L, 256, 16)

    def body(dres_ref, dh_ref, dh2_ref, x_ref, sc_ref, dx_ref, dsc_ref, dsh_ref):
        i = pl.program_id(0)
        dh_v = dh_ref[...].astype(F32) + dh2_ref[...].astype(F32)
        dx_ref[...] = dres_ref[...] + dh_v * (1.0 + sc_ref[...])
        _acc_rows(dsc_ref, dh_v * x_ref[...], i)
        _acc_rows(dsh_ref, dh_v, i)

    return pl.pallas_call(
        body, grid=(L // tr,), in_specs=_row_specs(tr, [D, D, D, D]) + [_vec_spec(D)],
        out_specs=_row_specs(tr, [D]) + [_vec_spec(D)] * 2,
        out_shape=[jax.ShapeDtypeStruct((L, D), F32)] + [jax.ShapeDtypeStruct((1, D), F32)] * 2,
        compiler_params=_cp(("arbitrary",)), name=name)(dres, dh, dh2, xin, scale)


def mod_ln_bwd(dres_in, dh, dskip, xmid, scale, x, y, gate, g):
    L, D = x.shape
    tr = _tile(L, 256, 16)

    def body(dres_ref, dh_ref, dskip_ref, xm_ref, sc_ref, x_ref, y_ref, gate_ref, g_ref,
             dres_out, dy_ref, dg_ref, db_ref, dgate_ref, dsc_ref, dsh_ref):
        i = pl.program_id(0)
        dh_v = dh_ref[...].astype(F32)
        dout_v = dres_ref[...] + dskip_ref[...].astype(F32) + dh_v * (1.0 + sc_ref[...])
        _acc_rows(dsc_ref, dh_v * xm_ref[...], i)
        _acc_rows(dsh_ref, dh_v, i)
        yv = y_ref[...]
        _, xhat, rstd = _ln_core(x_ref[...], yv, gate_ref[...], g_ref[...], 0.0)
        du = _ln_bwd_rows(dout_v, xhat, rstd, g_ref[...])
        dres_out[...] = ALPHA * du
        dy_ref[...] = ((1.0 + gate_ref[...]) * du).astype(BF16)
        _acc_rows(dg_ref, dout_v * xhat, i)
        _acc_rows(db_ref, dout_v, i)
        _acc_rows(dgate_ref, du * yv, i)

    return pl.pallas_call(
        body, grid=(L // tr,),
        in_specs=_row_specs(tr, [D] * 4) + [_vec_spec(D)] + _row_specs(tr, [D, D]) + [_vec_spec(D)] * 2,
        out_specs=_row_specs(tr, [D, D]) + [_vec_spec(D)] * 5,
        out_shape=[jax.ShapeDtypeStruct((L, D), F32), jax.ShapeDtypeStruct((L, D), BF16)]
        + [jax.ShapeDtypeStruct((1, D), F32)] * 5,
        compiler_params=_cp(("arbitrary",)), name="mod_ln_bwd")(dres_in, dh, dskip, xmid, scale, x, y, gate, g)


CONV_HALO = 16


def _conv_rows(x_ref, i, tr, L):
    nblk = L // tr
    s = pl.multiple_of(i * tr, CONV_HALO)
    cur = x_ref[pl.ds(s, tr), :].astype(F32)
    sp = pl.multiple_of(jnp.maximum(i * tr - CONV_HALO, 0), CONV_HALO)
    sn = pl.multiple_of(jnp.minimum(i * tr + tr, L - CONV_HALO), CONV_HALO)
    prev = x_ref[pl.ds(sp, CONV_HALO), :].astype(F32) * (i > 0).astype(F32)
    nxt = x_ref[pl.ds(sn, CONV_HALO), :].astype(F32) * (i < nblk - 1).astype(F32)
    return jnp.concatenate([prev, cur, nxt], axis=0)


def _shift_rows(v, j):
    n = v.shape[0]
    return v if j % n == 0 else pltpu.roll(v, j % n, 0)


def _conv_taps(xe):
    return [_shift_rows(xe, CONV_W - 1 - k) for k in range(CONV_W)]


def _conv_eval(taps, w_ref, b_ref):
    c = b_ref[...] + w_ref[0:1, :] * taps[0]
    for k in range(1, CONV_W):
        c = c + w_ref[k:k + 1, :] * taps[k]
    return c


def conv_fwd(zx, col0, conv_w, conv_b):
    L = zx.shape[0]
    C = conv_w.shape[1]
    tc = _tile(C, 512)
    tr = _tile(L, 512, CONV_HALO)
    off = col0 // tc

    def body(x_ref, w_ref, b_ref, o_ref):
        i = pl.program_id(1)
        xe = _conv_rows(x_ref, i, tr, L)
        c = _conv_eval(_conv_taps(xe), w_ref, b_ref)[CONV_HALO:CONV_HALO + tr]
        o_ref[...] = _silu(c).astype(BF16)

    return pl.pallas_call(
        body, grid=(C // tc, L // tr),
        in_specs=[pl.BlockSpec((L, tc), lambda j, i: (0, off + j)), pl.BlockSpec((CONV_W, tc), lambda j, i: (0, j)),
                  pl.BlockSpec((1, tc), lambda j, i: (0, j))],
        out_specs=pl.BlockSpec((tr, tc), lambda j, i: (i, j)),
        out_shape=jax.ShapeDtypeStruct((L, C), BF16), compiler_params=_cp(("parallel", "arbitrary")),
        name="conv_fwd")(zx, conv_w, conv_b)


def conv_bwd(zx, col0, conv_w, conv_b, g, dzx, name):
    L = zx.shape[0]
    C = conv_w.shape[1]
    tc = _tile(C, 512)
    tr = _tile(L, 512, CONV_HALO)
    off = col0 // tc
    H = CONV_HALO

    def body(x_ref, g_ref, w_ref, b_ref, buf_ref, dx_ref, dw_ref, db_ref):
        i = pl.program_id(1)
        xe = _conv_rows(x_ref, i, tr, L)
        ge = _conv_rows(g_ref, i, tr, L)
        taps = _conv_taps(xe)
        dc = ge * _dsilu(_conv_eval(taps, w_ref, b_ref))
        dx = w_ref[CONV_W - 1:CONV_W, :] * dc
        for k in range(CONV_W - 1):
            dx = dx + w_ref[k:k + 1, :] * _shift_rows(dc, -(CONV_W - 1 - k))
        dx_ref[...] = dx[H:H + tr].astype(BF16)
        dcc = dc[H:H + tr]
        rows = [jnp.sum(dcc * taps[k][H:H + tr], axis=0, keepdims=True) for k in range(CONV_W)]
        dwv = jnp.concatenate(rows + [jnp.zeros((8 - CONV_W, tc), F32)], axis=0)
        dbv = jnp.sum(dcc, axis=0, keepdims=True)

        @pl.when(i == 0)
        def _():
            dw_ref[...] = dwv
            db_ref[...] = dbv

        @pl.when(i > 0)
        def _():
            dw_ref[...] += dwv
            db_ref[...] += dbv

    dx, dw, db = pl.pallas_call(
        body, grid=(C // tc, L // tr),
        in_specs=[pl.BlockSpec((L, tc), lambda j, i: (0, off + j)), pl.BlockSpec((L, tc), lambda j, i: (0, j)),
                  pl.BlockSpec((CONV_W, tc), lambda j, i: (0, j)), pl.BlockSpec((1, tc), lambda j, i: (0, j)), _ANY],
        out_specs=[pl.BlockSpec((tr, tc), lambda j, i: (i, off + j)), pl.BlockSpec((8, tc), lambda j, i: (0, j)),
                   pl.BlockSpec((1, tc), lambda j, i: (0, j))],
        out_shape=[jax.ShapeDtypeStruct(dzx.shape, BF16), jax.ShapeDtypeStruct((8, C), F32),
                   jax.ShapeDtypeStruct((1, C), F32)],
        input_output_aliases={4: 0},
        compiler_params=_cp(("parallel", "arbitrary")), name=name)(zx, g, conv_w, conv_b, dzx)
    return dx, dw[:CONV_W], db


_NN = (((1,), (0,)), ((), ()))


def _pieces(x, n):
    out, r = [], x
    for _ in range(n):
        p = r.astype(BF16)
        out.append(p)
        r = r - p.astype(F32)
    return out


def _dot01(a, b01, n, dims=_NN):
    b = b01.astype(BF16)
    return functools.reduce(lambda u, v: u + v,
                            [lax.dot_general(p, b, dims, preferred_element_type=F32) for p in _pieces(a, n)])


def _dot01_left(a01, b, n, dims=_NN):
    a = a01.astype(BF16)
    return functools.reduce(lambda u, v: u + v,
                            [lax.dot_general(a, p, dims, preferred_element_type=F32) for p in _pieces(b, n)])


def _ssd_common(dtp_ref, dtpT_ref, bias_ref, biasT_ref, alog_ref, alogT_ref, b_ref, c_ref):
    Q = SSD_Q
    dt = _softplus(dtp_ref[...] + bias_ref[...])
    A = -jnp.exp(alog_ref[...])
    row = lax.broadcasted_iota(jnp.int32, (Q, Q), 0)
    col = lax.broadcasted_iota(jnp.int32, (Q, Q), 1)
    causal = row >= col
    tril = causal.astype(F32)
    Kh = dt.shape[1]
    acum = _dot01_left(tril, dt * A, 3)
    eye = (lax.broadcasted_iota(jnp.int32, (Kh, Kh), 0) == lax.broadcasted_iota(jnp.int32, (Kh, Kh), 1)).astype(F32)
    acumT = _dot01_left(eye, acum, 3, dims=(((1,), (1,)), ((), ())))
    Bm = b_ref[...]
    Cm = c_ref[...]
    cb = lax.dot_general(Cm, Bm, (((1,), (1,)), ((), ())), preferred_element_type=F32)
    return dt, A, causal, row, col, acum, acumT, Bm, Cm, cb


def _ssd_in_specs(Q, GP, N, Kh, DI, cmap):
    nb0 = DI // N
    vec = pl.BlockSpec((None, 1, Kh), lambda g, c: (g, 0, 0))
    vecT = pl.BlockSpec((None, Kh, 1), lambda g, c: (g, 0, 0))
    return [pl.BlockSpec((Q, GP), lambda g, c: (cmap(c), g)),
            pl.BlockSpec((Q, N), lambda g, c: (cmap(c), nb0 + g)),
            pl.BlockSpec((Q, N), lambda g, c: (cmap(c), nb0 + SSD_G + g)),
            pl.BlockSpec((None, Q, Kh), lambda g, c: (g, cmap(c), 0)),
            pl.BlockSpec((None, Kh, Q), lambda g, c: (g, 0, cmap(c))),
            vec, vecT, vec, vecT, vec, vecT]


def _hi(a, b01):
    return _dot01(a, b01, 2)


def _headsum(a, b01):
    return _dot01(a, b01, 1)


def _ssd_heads(dskT_ref, acum, acumT, dt, Kh):
    Q, P, N = SSD_Q, SSD_P, SSD_N
    GP = Kh * P
    sh_p = P.bit_length() - 1
    seg = lambda shape, dim: lax.shift_right_logical(lax.broadcasted_iota(jnp.int32, shape, dim), sh_p)
    E = (seg((Kh, GP), 1) == lax.broadcasted_iota(jnp.int32, (Kh, GP), 0)).astype(F32)
    ET = (seg((GP, Kh), 0) == lax.broadcasted_iota(jnp.int32, (GP, Kh), 1)).astype(F32)
    a_last = acum[Q - 1:Q, :]
    tail = jnp.exp(a_last - acum)
    eLT = jnp.exp(acumT[:, Q - 1:Q])
    rowseg = seg((GP, N), 0)
    eL_b = jnp.zeros((GP, N), F32)
    for k in range(Kh):
        eL_b = jnp.where(rowseg == k, eLT[k:k + 1, :], eL_b)
    return dict(
        E=E, ET=ET, a_last=a_last, tail=tail, eL_b=eL_b,
        dt_all=_hi(dt, E), ea_all=_headsum(jnp.exp(acum), E), tail_all=_headsum(tail, E),
        dsk_all=jnp.sum(E * dskT_ref[...], axis=0, keepdims=True))


def _head_chunks(GP):
    CW = min(GP, 128)
    return CW, CW // SSD_P, GP // CW


def _head_mask(Q, CW, kk):
    lane = lax.broadcasted_iota(jnp.int32, (Q, CW), 1)
    return jnp.logical_and(lane >= kk * SSD_P, lane < (kk + 1) * SSD_P)


def ssd_fwd(xbc, dtp_g, dtp_gT, bias_g, bias_gT, alog_g, alog_gT, dsk_g, dsk_gT, zx, norm_g, DI):
    L = xbc.shape[0]
    Q, P, N, G = SSD_Q, SSD_P, SSD_N, SSD_G
    GP = DI // G
    Kh = GP // P
    nc = L // Q

    CW, hpc, nch = _head_chunks(GP)
    nt = (((1,), (1,)), ((), ()))
    tn = (((0,), (0,)), ((), ()))

    def body(xs_ref, b_ref, c_ref, dtp_ref, dtpT_ref, bias_ref, biasT_ref, alog_ref, alogT_ref, dsk_ref, dskT_ref,
             z_ref, ng_ref, y_ref, st_ref, yn_ref, state):
        @pl.when(pl.program_id(1) == 0)
        def _():
            state[...] = jnp.zeros(state.shape, F32)

        st_ref[...] = state[...]
        dt, A, causal, row, col, acum, acumT, Bm, Cm, cb = _ssd_common(
            dtp_ref, dtpT_ref, bias_ref, biasT_ref, alog_ref, alogT_ref, b_ref, c_ref)
        hd = _ssd_heads(dskT_ref, acum, acumT, dt, Kh)
        xs = xs_ref[...].astype(F32)
        xdt_all = xs * hd["dt_all"]
        S_all = state[...]
        y_all = (lax.dot_general(Cm, S_all.astype(BF16), nt, preferred_element_type=F32) * hd["ea_all"]
                 + xs * hd["dsk_all"])
        state[...] = S_all * hd["eL_b"] + lax.dot_general(
            (xdt_all * hd["tail_all"]).astype(BF16), Bm, tn, preferred_element_type=F32)
        for ch in range(nch):
            cs = slice(ch * CW, (ch + 1) * CW)
            xc = xdt_all[:, cs]
            acc = y_all[:, cs]
            for kk in range(hpc):
                k = ch * hpc + kk
                decay = jnp.exp(jnp.where(causal, acum[:, k:k + 1] - acumT[k:k + 1, :], -jnp.inf))
                xk = xc if hpc == 1 else jnp.where(_head_mask(Q, CW, kk), xc, 0.0)
                acc = acc + jnp.dot((cb * decay).astype(BF16), xk.astype(BF16), preferred_element_type=F32)
            y_ref[:, cs] = acc.astype(BF16)
        y2 = y_ref[...].astype(F32) * _silu(z_ref[...].astype(F32))
        rr = lax.rsqrt(jnp.mean(y2 * y2, axis=-1, keepdims=True) + RMS_EPS)
        yn_ref[...] = (y2 * rr * ng_ref[...]).astype(BF16)

    tile = pl.BlockSpec((Q, GP), lambda g, c: (c, g))
    return pl.pallas_call(
        body, grid=(G, nc),
        in_specs=_ssd_in_specs(Q, GP, N, Kh, DI, lambda c: c) + [tile, pl.BlockSpec((1, GP), lambda g, c: (0, g))],
        out_specs=[tile, pl.BlockSpec((None, None, GP, N), lambda g, c: (c, g, 0, 0)), tile],
        out_shape=[jax.ShapeDtypeStruct((L, DI), BF16), jax.ShapeDtypeStruct((nc, G, GP, N), F32),
                   jax.ShapeDtypeStruct((L, DI), BF16)],
        scratch_shapes=[pltpu.VMEM((GP, N), F32)], compiler_params=_cp(("parallel", "arbitrary")),
        name="ssd_fwd")(xbc, xbc, xbc, dtp_g, dtp_gT, bias_g, bias_gT, alog_g, alog_gT, dsk_g, dsk_gT, zx, norm_g)


def ssd_bwd(xbc, dtp_g, dtp_gT, bias_g, bias_gT, alog_g, alog_gT, dsk_g, dsk_gT, states, dyn, y, zx, norm_g, DI):
    L = xbc.shape[0]
    Q, P, N, G = SSD_Q, SSD_P, SSD_N, SSD_G
    GP = DI // G
    Kh = GP // P
    nc = L // Q
    rev = lambda c: nc - 1 - c

    CW, hpc, nch = _head_chunks(GP)

    def body(xs_ref, b_ref, c_ref, dtp_ref, dtpT_ref, bias_ref, biasT_ref, alog_ref, alogT_ref, dsk_ref, dskT_ref,
             st_ref, dyn_ref, y_ref, z_ref, ng_ref,
             dxs_ref, dB_ref, dC_ref, ddtp_ref, dbias_ref, dalog_ref, dD_ref, dz_ref, dng_ref, dstate):
        ci = pl.program_id(1)

        @pl.when(ci == 0)
        def _():
            dstate[...] = jnp.zeros(dstate.shape, F32)

        dt, A, causal, row, col, acum, acumT, Bm, Cm, cb = _ssd_common(
            dtp_ref, dtpT_ref, bias_ref, biasT_ref, alog_ref, alogT_ref, b_ref, c_ref)
        tn = (((0,), (0,)), ((), ()))
        nt = (((1,), (1,)), ((), ()))
        hd = _ssd_heads(dskT_ref, acum, acumT, dt, Kh)
        ET, tail = hd["ET"], hd["tail"]
        cbT = lax.dot_general(Bm, Cm, nt, preferred_element_type=F32)
        causalT = row <= col
        xs = xs_ref[...].astype(F32)
        xdt_all = xs * hd["dt_all"]
        yv = y_ref[...].astype(F32)
        zv = z_ref[...].astype(F32)
        dynv = dyn_ref[...].astype(F32)
        sz = _silu(zv)
        y2 = yv * sz
        rr = lax.rsqrt(jnp.mean(y2 * y2, axis=-1, keepdims=True) + RMS_EPS)
        yh = y2 * rr
        dyh = dynv * ng_ref[...]
        dy2 = rr * (dyh - yh * jnp.mean(dyh * yh, axis=-1, keepdims=True))
        dz_ref[...] = (dy2 * yv * _dsilu(zv)).astype(BF16)
        dng_v = jnp.sum(dynv * yh, axis=0, keepdims=True)
        dyb = (dy2 * sz).astype(BF16)
        dy_all = dyb.astype(F32)
        S_all = st_ref[...]
        S_b = S_all.astype(BF16)
        dS_all = dstate[...]
        dS_b = dS_all.astype(BF16)
        CS_all = lax.dot_general(Cm, S_b, nt, preferred_element_type=F32)
        dyE_b = (dy_all * hd["ea_all"]).astype(BF16)
        dC_acc = jnp.dot(dyE_b, S_b, preferred_element_type=F32)
        dS_y = lax.dot_general(dyE_b, Cm, tn, preferred_element_type=F32)
        BdS_all = lax.dot_general(Bm, dS_b, nt, preferred_element_type=F32)
        dB_acc = jnp.dot((xdt_all * hd["tail_all"]).astype(BF16), dS_b, preferred_element_type=F32)
        dtail = _headsum(xdt_all * BdS_all, ET)
        da_cols = _headsum(dy_all * CS_all * hd["ea_all"], ET) - dtail * tail
        dss = _dot01_left(jnp.ones((8, N), F32), _dot01_left(hd["E"], dS_all * S_all, 2), 2, dims=nt)
        da_last = dss[0:1] * jnp.exp(hd["a_last"]) + jnp.sum(dtail * tail, axis=0, keepdims=True)
        rowi = lax.broadcasted_iota(jnp.int32, (Q, Kh), 0)
        da_cols = da_cols + jnp.where(rowi == Q - 1, da_last, 0.0)
        dstate[...] = hd["eL_b"] * dS_all + dS_y
        sum_mg = jnp.zeros((Q, Q), F32)
        ddt_x = jnp.zeros((Q, Kh), F32)
        da_rows = jnp.zeros((Kh, Q), F32)
        lane_k = lax.broadcasted_iota(jnp.int32, (Q, Kh), 1)
        sub_k = lax.broadcasted_iota(jnp.int32, (Kh, Q), 0)
        for ch in range(nch):
            cs = slice(ch * CW, (ch + 1) * CW)
            dyc = dyb[:, cs]
            xc_b = xdt_all[:, cs].astype(BF16)
            acc = hd["tail_all"][:, cs] * BdS_all[:, cs]
            for kk in range(hpc):
                k = ch * hpc + kk
                a_b = jnp.broadcast_to(acum[:, k:k + 1], (Q, Q))
                a_r = acumT[k:k + 1, :]
                decay = jnp.exp(jnp.where(causal, a_b - a_r, -jnp.inf))
                decayT = jnp.exp(jnp.where(causalT, a_r - a_b, -jnp.inf))
                dyk = dyc if hpc == 1 else jnp.where(_head_mask(Q, CW, kk), dyc, jnp.zeros_like(dyc))
                mg = decay * lax.dot_general(dyk, xc_b, nt, preferred_element_type=F32)
                sum_mg = sum_mg + mg
                w = mg * cb
                da_cols = da_cols + jnp.where(lane_k == k, jnp.sum(w, axis=1, keepdims=True), 0.0)
                da_rows = da_rows + jnp.where(sub_k == k, jnp.sum(w, axis=0, keepdims=True), 0.0)
                acc = acc + jnp.dot((decayT * cbT).astype(BF16), dyk, preferred_element_type=F32)
            dxs_ref[:, cs] = (acc * hd["dt_all"][:, cs] + dy_all[:, cs] * hd["dsk_all"][:, cs]).astype(BF16)
            ddt_x = ddt_x + _headsum(acc * xs[:, cs], ET[cs, :])
        eye_q = (row == col).astype(F32)
        da_cols = da_cols - _dot01_left(eye_q, da_rows, 3, dims=nt)
        dD_row = jnp.sum(_headsum(dy_all * xs, ET), axis=0, keepdims=True)
        sum_mg_b = sum_mg.astype(BF16)
        dB_ref[...] = (dB_acc + lax.dot_general(sum_mg_b, Cm, tn, preferred_element_type=F32)).astype(BF16)
        dC_ref[...] = (dC_acc + jnp.dot(sum_mg_b, Bm, preferred_element_type=F32)).astype(BF16)
        triu = (row <= col).astype(F32)
        ddtA = _dot01_left(triu, da_cols, 3)
        ddt = ddt_x + ddtA * A
        dpre = ddt * _sigmoid(dtp_ref[...] + bias_ref[...])
        ddtp_ref[...] = dpre
        dbias_v = jnp.sum(dpre, axis=0, keepdims=True)
        dalog_v = jnp.sum(ddtA * dt, axis=0, keepdims=True) * A

        @pl.when(ci == 0)
        def _():
            dbias_ref[...] = dbias_v
            dalog_ref[...] = dalog_v
            dD_ref[...] = dD_row
            dng_ref[...] = dng_v

        @pl.when(ci > 0)
        def _():
            dbias_ref[...] += dbias_v
            dalog_ref[...] += dalog_v
            dD_ref[...] += dD_row
            dng_ref[...] += dng_v

    vec_o = pl.BlockSpec((None, 1, Kh), lambda g, c: (g, 0, 0))
    tile = pl.BlockSpec((Q, GP), lambda g, c: (rev(c), g))
    return pl.pallas_call(
        body, grid=(G, nc),
        in_specs=_ssd_in_specs(Q, GP, N, Kh, DI, rev)
        + [pl.BlockSpec((None, None, GP, N), lambda g, c: (rev(c), g, 0, 0)), tile, tile, tile,
           pl.BlockSpec((1, GP), lambda g, c: (0, g))],
        out_specs=[tile, pl.BlockSpec((Q, N), lambda g, c: (rev(c), g)), pl.BlockSpec((Q, N), lambda g, c: (rev(c), g)),
                   pl.BlockSpec((None, Q, Kh), lambda g, c: (g, rev(c), 0)), vec_o, vec_o, vec_o,
                   tile, pl.BlockSpec((1, GP), lambda g, c: (0, g))],
        out_shape=[jax.ShapeDtypeStruct((L, DI), BF16), jax.ShapeDtypeStruct((L, G * N), BF16),
                   jax.ShapeDtypeStruct((L, G * N), BF16), jax.ShapeDtypeStruct((G, L, Kh), F32)]
        + [jax.ShapeDtypeStruct((G, 1, Kh), F32)] * 3
        + [jax.ShapeDtypeStruct(zx.shape, BF16), jax.ShapeDtypeStruct((1, DI), F32)],
        scratch_shapes=[pltpu.VMEM((GP, N), F32)], compiler_params=_cp(("parallel", "arbitrary")),
        name="ssd_bwd")(xbc, xbc, xbc, dtp_g, dtp_gT, bias_g, bias_gT, alog_g, alog_gT, dsk_g, dsk_gT, states,
                        dyn, y, zx, norm_g)


def _alibi_slope(gi, h):
    n = len(DIL_PATTERNS) * DIL_H
    return float(2.0 ** (-8.0 * (gi * DIL_H + h + 1) / n))


def _attn_masks():
    qi = lax.broadcasted_iota(jnp.int32, (DIL_BLK, DIL_BLK), 0)
    kj = lax.broadcasted_iota(jnp.int32, (DIL_BLK, DIL_BLK), 1)
    dcur = (qi - kj).astype(F32)
    return dcur, qi >= kj, dcur + float(DIL_BLK), kj >= qi


def attn_fwd(q3, kv3, gi):
    window, d = DIL_PATTERNS[gi]
    assert window // d == DIL_BLK
    HW = DIL_H * DIL_E
    M = q3.shape[1]
    nb = M // DIL_BLK
    scale = DIL_E ** -0.5
    nt = (((1,), (1,)), ((), ()))

    def body(q_ref, kp_ref, kc_ref, vp_ref, vc_ref, o_ref, lse_ref):
        n = pl.program_id(1)
        dcur, vcur, dprev, vprev0 = _attn_masks()
        dist = jnp.concatenate([dprev, dcur], axis=1)
        valid = jnp.concatenate([jnp.logical_and(vprev0, n > 0), vcur], axis=1)
        lane = lax.broadcasted_iota(jnp.int32, (DIL_BLK, 128), 1)
        lse_acc = jnp.zeros((DIL_BLK, 128), F32)
        for h in range(DIL_H):
            hs = slice(h * DIL_E, (h + 1) * DIL_E)
            sl = _alibi_slope(gi, h) * d
            kcat = jnp.concatenate([kp_ref[:, hs], kc_ref[:, hs]], axis=0)
            vcat = jnp.concatenate([vp_ref[:, hs], vc_ref[:, hs]], axis=0)
            s = lax.dot_general(q_ref[:, hs], kcat, nt, preferred_element_type=F32) * scale - sl * dist
            s = jnp.where(valid, s, -jnp.inf)
            m = jnp.max(s, axis=-1, keepdims=True)
            p = jnp.exp(s - m)
            den = jnp.sum(p, axis=-1, keepdims=True)
            o = jnp.dot(p.astype(BF16), vcat, preferred_element_type=F32) / den
            o_ref[:, hs] = o.astype(BF16)
            lse_acc = jnp.where(lane == h, m + jnp.log(den), lse_acc)
        lse_ref[...] = lse_acc

    blk = (None, DIL_BLK, HW)
    prev = lambda n: jnp.maximum(n - 1, 0)
    return pl.pallas_call(
        body, grid=(d, nb),
        in_specs=[pl.BlockSpec(blk, lambda r, n: (r, n, 0)),
                  pl.BlockSpec(blk, lambda r, n: (r, prev(n), 0)), pl.BlockSpec(blk, lambda r, n: (r, n, 0)),
                  pl.BlockSpec(blk, lambda r, n: (r, prev(n), 1)), pl.BlockSpec(blk, lambda r, n: (r, n, 1))],
        out_specs=[pl.BlockSpec(blk, lambda r, n: (r, n, 0)), pl.BlockSpec((None, DIL_BLK, 128), lambda r, n: (r, n, 0))],
        out_shape=[jax.ShapeDtypeStruct((d, M, HW), BF16), jax.ShapeDtypeStruct((d, M, 128), F32)],
        compiler_params=_cp(("parallel", "parallel")), name=f"attn_fwd_{gi}")(q3, kv3, kv3, kv3, kv3)


def attn_bwd(q3, kv3, do3, lse3, dpr3, gi):
    window, d = DIL_PATTERNS[gi]
    HW = DIL_H * DIL_E
    M = q3.shape[1]
    L = M * d
    nb = M // DIL_BLK
    scale = DIL_E ** -0.5
    nt = (((1,), (1,)), ((), ()))
    tn = (((0,), (0,)), ((), ()))

    def body(q0_ref, q1_ref, k_ref, v_ref, do0_ref, do1_ref, l0_ref, l1_ref, r0_ref, r1_ref,
             dq_ref, dk_ref, dv_ref, carry):
        n = pl.program_id(1)

        @pl.when(n == 0)
        def _():
            carry[...] = jnp.zeros(carry.shape, F32)

        dcur, vcur, dprev, vprev0 = _attn_masks()
        dist = jnp.concatenate([dcur, dprev], axis=0)
        valid = jnp.concatenate([vcur, jnp.logical_and(vprev0, n < nb - 1)], axis=0)
        B = DIL_BLK
        for h in range(DIL_H):
            hs = slice(h * DIL_E, (h + 1) * DIL_E)
            sl = _alibi_slope(gi, h) * d
            kh = k_ref[:, hs]
            vh = v_ref[:, hs]
            qcat = jnp.concatenate([q0_ref[:, hs], q1_ref[:, hs]], axis=0)
            docat = jnp.concatenate([do0_ref[:, hs], do1_ref[:, hs]], axis=0)
            lcat = jnp.concatenate([l0_ref[:, h:h + 1], l1_ref[:, h:h + 1]], axis=0)
            rcat = jnp.concatenate([r0_ref[:, h:h + 1], r1_ref[:, h:h + 1]], axis=0)
            s = lax.dot_general(qcat, kh, nt, preferred_element_type=F32) * scale - sl * dist
            p = jnp.exp(jnp.where(valid, s - lcat, -jnp.inf))
            ds = p * (lax.dot_general(docat, vh, nt, preferred_element_type=F32) - rcat)
            ds_b = (ds * scale).astype(BF16)
            dv_ref[:, hs] = lax.dot_general(p.astype(BF16), docat, tn, preferred_element_type=F32).astype(BF16)
            dk_ref[:, hs] = lax.dot_general(ds_b, qcat, tn, preferred_element_type=F32).astype(BF16)
            dqc = jnp.dot(ds_b, kh, preferred_element_type=F32)
            dq_ref[:, hs] = (carry[:, hs] + dqc[:B]).astype(BF16)
            carry[:, hs] = dqc[B:]

    blk = (None, DIL_BLK, HW)
    sblk = (None, DIL_BLK, 128)
    oblk = (DIL_BLK, HW)
    nxt = lambda n: jnp.minimum(n + 1, nb - 1)
    here = lambda c: (lambda r, n: (r, n, c))
    ahead = lambda c: (lambda r, n: (r, nxt(n), c))
    outs = pl.pallas_call(
        body, grid=(d, nb),
        in_specs=[pl.BlockSpec(blk, here(0)), pl.BlockSpec(blk, ahead(0)),
                  pl.BlockSpec(blk, here(0)), pl.BlockSpec(blk, here(1)),
                  pl.BlockSpec(blk, here(0)), pl.BlockSpec(blk, ahead(0)),
                  pl.BlockSpec(sblk, here(0)), pl.BlockSpec(sblk, ahead(0)),
                  pl.BlockSpec(sblk, here(0)), pl.BlockSpec(sblk, ahead(0))],
        out_specs=[pl.BlockSpec(oblk, lambda r, n: (n, r))] * 3,
        out_shape=[jax.ShapeDtypeStruct((M, d * HW), BF16)] * 3,
        scratch_shapes=[pltpu.VMEM(oblk, F32)], compiler_params=_cp(("parallel", "arbitrary")),
        name=f"attn_bwd_{gi}")(q3, q3, kv3, kv3, do3, do3, lse3, lse3, dpr3, dpr3)
    return [t.reshape(L, HW) for t in outs]


def _merge_weights(l_tiles, h):
    ls = [t[:, h:h + 1] for t in l_tiles]
    mx = functools.reduce(jnp.maximum, ls)
    es = [jnp.exp(l - mx) for l in ls]
    den = functools.reduce(lambda a, b: a + b, es)
    return [e / den for e in es]


def _dil_specs(tr, arrs):
    return [pl.BlockSpec((a.shape[0], tr // a.shape[0], a.shape[2]), lambda i: (0, i, 0)) for a in arrs]


def _dil_scratch(tr, arrs):
    return [pltpu.VMEM((a.shape[2] // 128, tr, 128), F32) for a in arrs if a.shape[0] > 1]


def _undilate(refs3, scrs, tr):
    out, k = [], 0
    for ref in refs3:
        d, _, W = ref.shape
        if d == 1:
            out.append(lambda c, ref=ref: ref[0, :, c * 128:(c + 1) * 128])
            continue
        scr = scrs[k]
        k += 1
        for r in range(d):
            for c in range(W // 128):
                scr.at[c][pl.ds(r, tr // d, stride=d), :] = ref[r, :, c * 128:(c + 1) * 128].astype(F32)
        out.append(lambda c, scr=scr: scr[c])
    return out


def merge_fwd(os3, lses3, z):
    HW = os3[0].shape[2]
    L = os3[0].shape[0] * os3[0].shape[1]
    tr = _tile(L, 256, 16)
    ng = len(os3)
    n_scr = len(_dil_scratch(tr, os3))

    def body(*refs):
        z_ref, out_ref = refs[2 * ng], refs[2 * ng + 1]
        scrs = refs[2 * ng + 2:]
        o_get = _undilate(refs[:ng], scrs[:n_scr], tr)
        l_tiles = [g(0) for g in _undilate(refs[ng:2 * ng], scrs[n_scr:], tr)]
        for h in range(DIL_H):
            hs = slice(h * DIL_E, (h + 1) * DIL_E)
            ws = _merge_weights(l_tiles, h)
            om = functools.reduce(lambda a, b: a + b, [w * o(h).astype(F32) for w, o in zip(ws, o_get)])
            out_ref[:, hs] = (om * _silu(z_ref[:, hs].astype(F32))).astype(BF16)

    return pl.pallas_call(
        body, grid=(L // tr,),
        in_specs=_dil_specs(tr, os3) + _dil_specs(tr, lses3) + _row_specs(tr, [HW]),
        out_specs=_row_specs(tr, [HW])[0], out_shape=jax.ShapeDtypeStruct((L, HW), BF16),
        scratch_shapes=_dil_scratch(tr, os3) + _dil_scratch(tr, lses3),
        compiler_params=_cp(("parallel",)), name="merge_fwd")(*os3, *lses3, z)


def merge_bwd(dgated, os3, lses3, z):
    HW = os3[0].shape[2]
    L = os3[0].shape[0] * os3[0].shape[1]
    tr = _tile(L, 256, 16)
    ng = len(os3)
    n_scr = len(_dil_scratch(tr, os3))

    def body(*refs):
        dg_ref = refs[0]
        z_ref = refs[1 + 2 * ng]
        outs = refs[2 + 2 * ng:2 + 2 * ng + 2 * ng + 1]
        scrs = refs[2 + 2 * ng + 2 * ng + 1:]
        do_out, dpr_out, dz_ref = outs[:ng], outs[ng:2 * ng], outs[2 * ng]
        o_get = _undilate(refs[1:1 + ng], scrs[:n_scr], tr)
        l_tiles = [g(0) for g in _undilate(refs[1 + ng:1 + 2 * ng], scrs[n_scr:2 * n_scr], tr)]
        stage = scrs[2 * n_scr:]
        do_stage, dpr_stage, k = [], [], 0
        for g in range(ng):
            if do_out[g].shape[0] == 1:
                do_stage.append(None)
                dpr_stage.append(None)
            else:
                do_stage.append(stage[2 * k])
                dpr_stage.append(stage[2 * k + 1])
                k += 1
        lane = lax.broadcasted_iota(jnp.int32, (tr, 128), 1)
        accs = [jnp.zeros((tr, 128), F32) for _ in range(ng)]
        for h in range(DIL_H):
            hs = slice(h * DIL_E, (h + 1) * DIL_E)
            ws = _merge_weights(l_tiles, h)
            ov = [o(h).astype(F32) for o in o_get]
            om = functools.reduce(lambda a, b: a + b, [w * o for w, o in zip(ws, ov)])
            zv = z_ref[:, hs].astype(F32)
            dgv = dg_ref[:, hs].astype(F32)
            dom = dgv * _silu(zv)
            dz_ref[:, hs] = (dgv * om * _dsilu(zv)).astype(BF16)
            dws = [jnp.sum(dom * o, axis=-1, keepdims=True) for o in ov]
            dwbar = functools.reduce(lambda a, b: a + b, [w * dw for w, dw in zip(ws, dws)])
            for g in range(ng):
                if do_stage[g] is None:
                    do_out[g][0, :, hs] = (ws[g] * dom).astype(BF16)
                else:
                    do_stage[g][h] = ws[g] * dom
                accs[g] = jnp.where(lane == h, ws[g] * dwbar, accs[g])
        for g in range(ng):
            d = do_out[g].shape[0]
            if d == 1:
                dpr_out[g][0] = accs[g]
                continue
            dpr_stage[g][0] = accs[g]
            for r in range(d):
                dpr_out[g][r] = dpr_stage[g].at[0][pl.ds(r, tr // d, stride=d), :]
                for c in range(HW // 128):
                    do_out[g][r, :, c * 128:(c + 1) * 128] = do_stage[g].at[c][pl.ds(r, tr // d, stride=d), :].astype(BF16)

    stage_shapes = []
    for o3 in os3:
        if o3.shape[0] > 1:
            stage_shapes += [pltpu.VMEM((HW // 128, tr, 128), F32), pltpu.VMEM((1, tr, 128), F32)]
    outs = pl.pallas_call(
        body, grid=(L // tr,),
        in_specs=_row_specs(tr, [HW]) + _dil_specs(tr, os3) + _dil_specs(tr, lses3) + _row_specs(tr, [HW]),
        out_specs=_dil_specs(tr, os3) + _dil_specs(tr, lses3) + _row_specs(tr, [HW]),
        out_shape=[jax.ShapeDtypeStruct(o.shape, BF16) for o in os3] + [jax.ShapeDtypeStruct(l.shape, F32) for l in lses3]
        + [jax.ShapeDtypeStruct((L, HW), BF16)],
        scratch_shapes=_dil_scratch(tr, os3) + _dil_scratch(tr, lses3) + stage_shapes,
        compiler_params=_cp(("parallel",)), name="merge_bwd")(dgated, *os3, *lses3, z)
    return outs[:ng], outs[ng:2 * ng], outs[2 * ng]


def ada_fwd(c8, ada_w):
    nl, D, Ws = ada_w.shape
    tn = _tile(Ws, 512)

    def body(c_ref, w_ref, o_ref):
        o_ref[...] = jnp.dot(_silu(c_ref[...]), w_ref[...], precision=lax.Precision.HIGHEST,
                             preferred_element_type=F32)

    return pl.pallas_call(
        body, grid=(nl, Ws // tn),
        in_specs=[pl.BlockSpec((N_DEV, D), lambda l, j: (0, 0)), pl.BlockSpec((None, D, tn), lambda l, j: (l, 0, j))],
        out_specs=pl.BlockSpec((None, N_DEV, tn), lambda l, j: (l, 0, j)),
        out_shape=jax.ShapeDtypeStruct((nl, N_DEV, Ws), F32), compiler_params=_cp(("parallel", "parallel")),
        name="ada_fwd")(c8, ada_w)


def ada_wgrad(c8t, dmod):
    nl, _, Ws = dmod.shape
    D = c8t.shape[0]
    tm = _tile(D, 512, 8)

    def body(c_ref, d_ref, o_ref):
        sc = _silu(c_ref[...])
        acc = sc[:, 0:1] * d_ref[0:1, :]
        for e in range(1, N_DEV):
            acc = acc + sc[:, e:e + 1] * d_ref[e:e + 1, :]
        o_ref[...] = acc

    return pl.pallas_call(
        body, grid=(nl, D // tm),
        in_specs=[pl.BlockSpec((tm, N_DEV), lambda l, i: (i, 0)), pl.BlockSpec((None, N_DEV, Ws), lambda l, i: (l, 0, 0))],
        out_specs=pl.BlockSpec((None, tm, Ws), lambda l, i: (l, i, 0)),
        out_shape=jax.ShapeDtypeStruct((nl, D, Ws), F32), compiler_params=_cp(("parallel", "parallel")),
        name="ada_wgrad")(c8t, dmod)


def adamw(w, g, m, v, name):
    R, C = w.shape
    tr = _tile(R, 256, 8)
    c1 = 1.0 - ADAM_B1 ** ADAM_STEP
    c2 = 1.0 - ADAM_B2 ** ADAM_STEP

    def body(w_ref, g_ref, m_ref, v_ref, d_ref, nm_ref, nv_ref):
        gv = g_ref[...]
        nm = ADAM_B1 * m_ref[...] + (1.0 - ADAM_B1) * gv
        nv = ADAM_B2 * v_ref[...] + (1.0 - ADAM_B2) * (gv * gv)
        nm_ref[...] = nm
        nv_ref[...] = nv
        d_ref[...] = -ADAM_LR * ((nm / c1) / (jnp.sqrt(nv / c2) + ADAM_EPS) + ADAM_WD * w_ref[...])

    return pl.pallas_call(
        body, grid=(R // tr,), in_specs=_row_specs(tr, [C] * 4), out_specs=_row_specs(tr, [C] * 3),
        out_shape=[jax.ShapeDtypeStruct((R, C), F32)] * 3, compiler_params=_cp(("parallel",)), name=name)(w, g, m, v)


def sum_leading(t, name, out_dtype=F32):
    S, R, C = t.shape
    tr = _tile(R, 256, 16)

    def body(t_ref, o_ref):
        acc = t_ref[0].astype(F32)
        for s in range(1, S):
            acc = acc + t_ref[s].astype(F32)
        o_ref[...] = acc.astype(out_dtype)

    return pl.pallas_call(
        body, grid=(R // tr,), in_specs=[pl.BlockSpec((S, tr, C), lambda i: (0, i, 0))],
        out_specs=pl.BlockSpec((tr, C), lambda i: (i, 0)), out_shape=jax.ShapeDtypeStruct((R, C), out_dtype),
        compiler_params=_cp(("parallel",)), name=name)(t)


def add_half(g, a, core, name, by_cols=False):
    S, R, C = g.shape

    def body(core_ref, g_ref, a_ref, o_ref):
        o_ref[...] = (g_ref[...].astype(F32) + a_ref[...].astype(F32)).astype(BF16)

    if by_cols:
        hc = C // 2
        tr = _tile(R, 512, 16)
        return pl.pallas_call(
            body,
            grid_spec=pltpu.PrefetchScalarGridSpec(
                num_scalar_prefetch=1, grid=(S, R // tr),
                in_specs=[pl.BlockSpec((None, tr, hc), lambda s, i, core_ref: (s, i, core_ref[0])),
                          pl.BlockSpec((None, tr, hc), lambda s, i, core_ref: (s, i, 0))],
                out_specs=pl.BlockSpec((None, tr, hc), lambda s, i, core_ref: (s, i, 0))),
            out_shape=jax.ShapeDtypeStruct((S, R, hc), BF16), compiler_params=_cp(("parallel", "parallel")),
            name=name)(core, g, a)
    h = R // 2
    tr = _tile(h, 256, 16)
    nb = h // tr

    return pl.pallas_call(
        body,
        grid_spec=pltpu.PrefetchScalarGridSpec(
            num_scalar_prefetch=1, grid=(S, nb),
            in_specs=[pl.BlockSpec((None, tr, C), lambda s, i, core_ref: (s, core_ref[0] * nb + i, 0)),
                      pl.BlockSpec((None, tr, C), lambda s, i, core_ref: (s, i, 0))],
            out_specs=pl.BlockSpec((None, tr, C), lambda s, i, core_ref: (s, i, 0))),
        out_shape=jax.ShapeDtypeStruct((S, h, C), BF16), compiler_params=_cp(("parallel", "parallel")),
        name=name)(core, g, a)


def sum_partials(own, landed, chip, name):
    _, h, C = own.shape
    tr = _tile(h, 512, 16)

    def body(chip_ref, own_ref, l_ref, o_ref):
        acc = own_ref[...].astype(F32)
        for j in range(3):
            acc = acc + l_ref[j].astype(F32)
        o_ref[...] = acc

    return pl.pallas_call(
        body,
        grid_spec=pltpu.PrefetchScalarGridSpec(
            num_scalar_prefetch=1, grid=(h // tr,),
            in_specs=[pl.BlockSpec((None, tr, C), lambda i, chip_ref: (chip_ref[0], i, 0)),
                      pl.BlockSpec((3, tr, C), lambda i, chip_ref: (0, i, 0))],
            out_specs=pl.BlockSpec((tr, C), lambda i, chip_ref: (i, 0))),
        out_shape=jax.ShapeDtypeStruct((h, C), F32), compiler_params=_cp(("parallel",)), name=name)(chip, own, landed)


def adamw_halves(w, g_mine, g_theirs, m, v, core, name, after=None):
    R, C = w.shape
    h = R // 2
    tr = _tile(h, 256, 8)
    nbh = h // tr
    c1 = 1.0 - ADAM_B1 ** ADAM_STEP
    c2 = 1.0 - ADAM_B2 ** ADAM_STEP
    extra = [] if after is None else [after]

    def body(core_ref, w_ref, gm_ref, gt_ref, m_ref, v_ref, *rest):
        g_ref, d_ref, nm_ref, nv_ref = rest[len(extra):]
        mine = (pl.program_id(0) // nbh) == core_ref[0]
        gv = jnp.where(mine, gm_ref[...], gt_ref[...])
        g_ref[...] = gv
        nm = ADAM_B1 * m_ref[...] + (1.0 - ADAM_B1) * gv
        nv = ADAM_B2 * v_ref[...] + (1.0 - ADAM_B2) * (gv * gv)
        nm_ref[...] = nm
        nv_ref[...] = nv
        d_ref[...] = -ADAM_LR * ((nm / c1) / (jnp.sqrt(nv / c2) + ADAM_EPS) + ADAM_WD * w_ref[...])

    full = pl.BlockSpec((tr, C), lambda i, core_ref: (i, 0))
    halfspec = pl.BlockSpec((tr, C), lambda i, core_ref: (i % nbh, 0))
    return pl.pallas_call(
        body,
        grid_spec=pltpu.PrefetchScalarGridSpec(
            num_scalar_prefetch=1, grid=(2 * nbh,),
            in_specs=[full, halfspec, halfspec, full, full] + [_ANY] * len(extra), out_specs=[full] * 4),
        out_shape=[jax.ShapeDtypeStruct((R, C), F32)] * 4, compiler_params=_cp(("parallel",)),
        name=name)(core, w, g_mine, g_theirs, m, v, *extra)


_ANY = pl.BlockSpec(memory_space=pl.ANY)


def _place():
    x, y, c = lax.axis_index("x"), lax.axis_index("y"), lax.axis_index("c")
    chips = [(1 - x, y), (x, 1 - y), (1 - x, 1 - y)]
    return x, y, c, chips


def allgather_small(v, name, after=None):
    R, W = v.shape
    extra = [] if after is None else [after]

    def body(x_ref, *rest):
        out_ref, send_sems, recv_sems, local_sem = rest[len(extra):]
        x, y, c, chips = _place()
        me, sibling = (x, y, c), (x, y, 1 - c)

        def rows(px, py, pc):
            return out_ref.at[pl.ds((4 * px + 2 * py + pc) * R, R), :]

        def copy(k, block, to, src=None):
            return pltpu.make_async_remote_copy(
                src_ref=rows(*block) if src is None else src, dst_ref=rows(*block),
                send_sem=send_sems.at[k], recv_sem=recv_sems.at[k], device_id=to, device_id_type=MESH)

        mine = pltpu.make_async_copy(x_ref, rows(*me), local_sem)
        mine.start()
        first = [copy(0, me, sibling, src=x_ref)]
        first += [copy(1 + j, me, (*chip, c), src=x_ref) for j, chip in enumerate(chips)]
        for cp in first:
            cp.start()
        passed = [copy(4 + j, (*chip, c), sibling) for j, chip in enumerate(chips)]
        for j, chip in enumerate(chips):
            copy(1 + j, (*chip, c), me).wait_recv()
            passed[j].start()
        copy(0, sibling, me).wait_recv()
        for j, chip in enumerate(chips):
            copy(4 + j, (*chip, 1 - c), me).wait_recv()
        for cp in first + passed:
            cp.wait_send()
        mine.wait()

    return pl.pallas_call(
        body, out_shape=jax.ShapeDtypeStruct((N_DEV * R, W), v.dtype),
        in_specs=[pl.BlockSpec(memory_space=pltpu.VMEM)] + [_ANY] * len(extra),
        out_specs=pl.BlockSpec(memory_space=pltpu.VMEM),
        scratch_shapes=[pltpu.SemaphoreType.DMA((7,)), pltpu.SemaphoreType.DMA((7,)), pltpu.SemaphoreType.DMA],
        name=name)(v, *extra)


def allgather_routed(shard, name):
    R, C = shard.shape
    hc = C // 2
    ra = (R // 2) // 16 * 16

    def body(in_ref, out_ref, send_sems, recv_sems):
        x, y, c, _ = _place()
        xn, yn = (1 - x, y, c), (x, 1 - y, c)
        sibling = (x, y, 1 - c)
        p, pxn, pyn, pdg = 2 * x + y, 2 * (1 - x) + y, 2 * x + (1 - y), 2 * (1 - x) + (1 - y)
        rows_a, rows_b, rows_all = pl.ds(0, ra), pl.ds(ra, R - ra), pl.ds(0, R)

        def win(ref, rows, core):
            return ref.at[rows, pl.ds(pl.multiple_of(core * hc, 128), hc)]

        def copy(k, chip_id, rows, core, to, src=None):
            blk = win(out_ref.at[chip_id], rows, core)
            return pltpu.make_async_remote_copy(
                src_ref=blk if src is None else src, dst_ref=blk, send_sem=send_sems.at[k], recv_sem=recv_sems.at[k],
                device_id=to, device_id_type=MESH)

        own = [copy(0, p, rows_a, c, xn, src=win(in_ref, rows_a, c)), copy(1, p, rows_b, c, xn, src=win(in_ref, rows_b, c)),
               copy(2, p, rows_b, c, yn, src=win(in_ref, rows_b, c)), copy(3, p, rows_a, c, yn, src=win(in_ref, rows_a, c))]
        for cp in own:
            cp.start()
        copy(0, pxn, rows_a, c, xn).wait_recv()
        fwd_a = copy(4, pxn, rows_a, c, yn)
        fwd_a.start()
        copy(2, pyn, rows_b, c, yn).wait_recv()
        fwd_b = copy(5, pyn, rows_b, c, xn)
        fwd_b.start()
        copy(1, pxn, rows_b, c, xn).wait_recv()
        copy(3, pyn, rows_a, c, yn).wait_recv()
        passed = [copy(6, pxn, rows_all, c, sibling), copy(7, pyn, rows_all, c, sibling)]
        for cp in passed:
            cp.start()
        copy(4, pdg, rows_a, c, yn).wait_recv()
        passed.append(copy(8, pdg, rows_a, c, sibling))
        passed[-1].start()
        copy(5, pdg, rows_b, c, xn).wait_recv()
        passed.append(copy(9, pdg, rows_b, c, sibling))
        passed[-1].start()
        for k, (chip_id, rows) in enumerate([(pxn, rows_all), (pyn, rows_all), (pdg, rows_a), (pdg, rows_b)]):
            copy(6 + k, chip_id, rows, 1 - c, sibling).wait_recv()
        for cp in own + [fwd_a, fwd_b] + passed:
            cp.wait_send()

    out = pl.pallas_call(
        body, out_shape=jax.ShapeDtypeStruct((N_CHIPS, R, C), shard.dtype), in_specs=[_ANY], out_specs=_ANY,
        scratch_shapes=[pltpu.SemaphoreType.DMA((10,)), pltpu.SemaphoreType.DMA((10,))], name=name)(shard)
    chip = 2 * lax.axis_index("x") + lax.axis_index("y")
    return lax.dynamic_update_index_in_dim(out, shard, chip, 0)


_HBM = pl.BlockSpec(memory_space=pltpu.HBM)
_SEM = pl.BlockSpec(memory_space=pltpu.SEMAPHORE)
_EFFECT = pltpu.SideEffectType.DATAFLOW_SIDE_EFFECTING


def _chip_copies(kind, srcs, lands, send_sems, recv_sems):
    x, y, c, chips = _place()
    p = 2 * x + y
    cps = []
    if kind == "join":
        return [pltpu.make_async_remote_copy(
            src_ref=srcs[i], dst_ref=lands[i], send_sem=send_sems.at[3 * i], recv_sem=recv_sems.at[3 * i],
            device_id=(x, y, 1 - c), device_id_type=MESH) for i in range(len(srcs))]
    if kind == "sibling":
        for i in range(len(srcs)):
            h = srcs[i].shape[1] // 2
            cps.append(pltpu.make_async_remote_copy(
                src_ref=srcs[i].at[:, pl.ds((1 - c) * h, h), :], dst_ref=lands[i], send_sem=send_sems.at[3 * i],
                recv_sem=recv_sems.at[3 * i], device_id=(x, y, 1 - c), device_id_type=MESH))
        return cps
    for i in range(len(srcs)):
        for j, (cx, cy) in enumerate(chips):
            if kind == "gather":
                src, dst = srcs[i].at[c], lands[i].at[p, c]
            else:
                src, dst = srcs[i].at[2 * cx + cy], lands[i].at[j]
            cps.append(pltpu.make_async_remote_copy(
                src_ref=src, dst_ref=dst, send_sem=send_sems.at[3 * i + j], recv_sem=recv_sems.at[3 * i + j],
                device_id=(cx, cy, c), device_id_type=MESH))
    return cps


def split_start(kind, srcs, land_shapes, after, name, land_dtype=BF16):
    n = len(srcs)

    def body(*refs):
        src_refs, land_refs = refs[:n], refs[n:2 * n]
        send_sems, recv_sems = refs[2 * n + 1], refs[2 * n + 2]
        token = refs[-1]
        for cp in _chip_copies(kind, src_refs, land_refs, send_sems, recv_sems):
            cp.start()
        token[...] = jnp.zeros_like(token)

    lands = [pltpu.with_memory_space_constraint(lax.empty(s, land_dtype), pltpu.HBM) for s in land_shapes]
    outs = pl.pallas_call(
        body, name=name,
        out_shape=(pltpu.SemaphoreType.DMA((3 * n,)), pltpu.SemaphoreType.DMA((3 * n,)),
                   *[pltpu.HBM(s.shape, s.dtype) for s in srcs], *[pltpu.HBM(s, land_dtype) for s in land_shapes],
                   jax.ShapeDtypeStruct((8, 128), F32)),
        in_specs=[_HBM] * (2 * n) + [_ANY],
        out_specs=(_SEM, _SEM, *([_HBM] * (2 * n)), pl.BlockSpec(memory_space=pltpu.VMEM)),
        input_output_aliases={i: 2 + i for i in range(2 * n)},
        compiler_params=pltpu.CompilerParams(has_side_effects=_EFFECT),
    )(*[pltpu.with_memory_space_constraint(s, pltpu.HBM) for s in srcs], *lands, after)
    return outs[0], outs[1], outs[2:2 + n], outs[2 + n:2 + 2 * n], outs[-1]


def split_wait(kind, send_sems, recv_sems, srcs, lands, after, name):
    n = len(srcs)

    def body(*refs):
        src_refs, land_refs = refs[:n], refs[n:2 * n]
        ssem, rsem = refs[2 * n], refs[2 * n + 1]
        for cp in _chip_copies(kind, src_refs, land_refs, ssem, rsem):
            cp.wait_send()
            cp.wait_recv()

    outs = pl.pallas_call(
        body, name=name,
        out_shape=[pltpu.HBM(s.shape, s.dtype) for s in srcs] + [pltpu.HBM(s.shape, s.dtype) for s in lands],
        in_specs=[_HBM] * (2 * n) + [_SEM, _SEM, _ANY], out_specs=[_HBM] * (2 * n),
        input_output_aliases={i: i for i in range(2 * n)},
        compiler_params=pltpu.CompilerParams(has_side_effects=_EFFECT),
    )(*srcs, *lands, send_sems, recv_sems, after)
    return outs[:n], outs[n:]


def pass_to_sibling(lands):
    n = len(lands)

    def body(*refs):
        ins, outs = refs[:n], refs[n:2 * n]
        send_sems, recv_sems = refs[2 * n:]
        x, y, c, chips = _place()
        cps = []
        for i in range(n):
            for j, (cx, cy) in enumerate(chips):
                blk = outs[i].at[2 * cx + cy, c]
                cps.append(pltpu.make_async_remote_copy(
                    src_ref=ins[i].at[2 * cx + cy, c], dst_ref=blk, send_sem=send_sems.at[3 * i + j],
                    recv_sem=recv_sems.at[3 * i + j], device_id=(x, y, 1 - c), device_id_type=MESH))
        for cp in cps:
            cp.start()
        for cp in cps:
            cp.wait()

    return pl.pallas_call(
        body, out_shape=[jax.ShapeDtypeStruct(t.shape, t.dtype) for t in lands], in_specs=[_ANY] * n,
        out_specs=[_ANY] * n, input_output_aliases={i: i for i in range(n)},
        scratch_shapes=[pltpu.SemaphoreType.DMA((3 * n,)), pltpu.SemaphoreType.DMA((3 * n,))],
        name="ag_pass_to_sibling")(*lands)


def _all8_copies(src, land, send_sems, recv_sems):
    x, y, c, _ = _place()
    me = 4 * x + 2 * y + c
    cps = []
    for k, (fx, fy, fc) in enumerate([(a, b, d) for a in (0, 1) for b in (0, 1) for d in (0, 1)][1:]):
        peer = (1 - x if fx else x, 1 - y if fy else y, 1 - c if fc else c)
        cps.append(pltpu.make_async_remote_copy(
            src_ref=src, dst_ref=land.at[me], send_sem=send_sems.at[k], recv_sem=recv_sems.at[k],
            device_id=peer, device_id_type=MESH))
    return cps


def gather8_start(v, after, name):
    R, W = v.shape

    def body(v_ref, land_ref, aft_ref, send_sems, recv_sems, v_thru, land_thru, token):
        for cp in _all8_copies(v_ref, land_ref, send_sems, recv_sems):
            cp.start()
        token[...] = jnp.zeros_like(token)

    land = pltpu.with_memory_space_constraint(lax.empty((N_DEV, R, W), v.dtype), pltpu.HBM)
    outs = pl.pallas_call(
        body, name=name,
        out_shape=(pltpu.SemaphoreType.DMA((N_DEV - 1,)), pltpu.SemaphoreType.DMA((N_DEV - 1,)),
                   pltpu.HBM(v.shape, v.dtype), pltpu.HBM((N_DEV, R, W), v.dtype), jax.ShapeDtypeStruct((8, 128), F32)),
        in_specs=[_HBM, _HBM, _ANY],
        out_specs=(_SEM, _SEM, _HBM, _HBM, pl.BlockSpec(memory_space=pltpu.VMEM)),
        input_output_aliases={0: 2, 1: 3},
        compiler_params=pltpu.CompilerParams(has_side_effects=_EFFECT),
    )(pltpu.with_memory_space_constraint(v, pltpu.HBM), land, after)
    return outs


def gather8_wait(send_sems, recv_sems, v, land, after, name):
    def body(v_ref, land_ref, ssem, rsem, aft_ref, v_dead, land_out):
        for cp in _all8_copies(v_ref, land_ref, ssem, rsem):
            cp.wait_send()
            cp.wait_recv()

    return pl.pallas_call(
        body, name=name, out_shape=[pltpu.HBM(v.shape, v.dtype), pltpu.HBM(land.shape, land.dtype)],
        in_specs=[_HBM, _HBM, _SEM, _SEM, _ANY], out_specs=[_HBM, _HBM], input_output_aliases={0: 0, 1: 1},
        compiler_params=pltpu.CompilerParams(has_side_effects=_EFFECT),
    )(v, land, send_sems, recv_sems, after)[1]


def _pass_copies(bufs, send_sems, recv_sems):
    x, y, c, chips = _place()
    cps = []
    for i in range(len(bufs)):
        for j, (cx, cy) in enumerate(chips):
            blk = bufs[i].at[2 * cx + cy, c]
            cps.append(pltpu.make_async_remote_copy(
                src_ref=blk, dst_ref=blk, send_sem=send_sems.at[3 * i + j], recv_sem=recv_sems.at[3 * i + j],
                device_id=(x, y, 1 - c), device_id_type=MESH))
    return cps


def pass_start(bufs, after, name):
    n = len(bufs)

    def body(*refs):
        send_sems, recv_sems = refs[n + 1], refs[n + 2]
        for cp in _pass_copies(refs[:n], send_sems, recv_sems):
            cp.start()
        refs[-1][...] = jnp.zeros_like(refs[-1])

    outs = pl.pallas_call(
        body, name=name,
        out_shape=(pltpu.SemaphoreType.DMA((3 * n,)), pltpu.SemaphoreType.DMA((3 * n,)),
                   *[pltpu.HBM(b.shape, b.dtype) for b in bufs], jax.ShapeDtypeStruct((8, 128), F32)),
        in_specs=[_HBM] * n + [_ANY],
        out_specs=(_SEM, _SEM, *([_HBM] * n), pl.BlockSpec(memory_space=pltpu.VMEM)),
        input_output_aliases={i: 2 + i for i in range(n)},
        compiler_params=pltpu.CompilerParams(has_side_effects=_EFFECT),
    )(*[pltpu.with_memory_space_constraint(b, pltpu.HBM) for b in bufs], after)
    return outs[0], outs[1], outs[2:2 + n], outs[-1]


def pass_wait(send_sems, recv_sems, bufs, after, name):
    n = len(bufs)

    def body(*refs):
        for cp in _pass_copies(refs[:n], refs[n], refs[n + 1]):
            cp.wait_send()
            cp.wait_recv()

    return pl.pallas_call(
        body, name=name, out_shape=[pltpu.HBM(b.shape, b.dtype) for b in bufs],
        in_specs=[_HBM] * n + [_SEM, _SEM, _ANY], out_specs=[_HBM] * n,
        input_output_aliases={i: i for i in range(n)},
        compiler_params=pltpu.CompilerParams(has_side_effects=_EFFECT),
    )(*bufs, send_sems, recv_sems, after)


def exchange_halves_to_sibling(gs, name, by_cols=False):
    n = len(gs)

    def body(*refs):
        ins, outs = refs[:n], refs[n:2 * n]
        send_sems, recv_sems = refs[2 * n:]
        x, y, c, _ = _place()
        cps = []
        for i in range(n):
            if by_cols:
                hc = ins[i].shape[2] // 2
                src = ins[i].at[:, :, pl.ds(pl.multiple_of((1 - c) * hc, 128), hc)]
            else:
                h = ins[i].shape[1] // 2
                src = ins[i].at[:, pl.ds((1 - c) * h, h), :]
            cps.append(pltpu.make_async_remote_copy(
                src_ref=src, dst_ref=outs[i],
                send_sem=send_sems.at[i], recv_sem=recv_sems.at[i], device_id=(x, y, 1 - c), device_id_type=MESH))
        for cp in cps:
            cp.start()
        for cp in cps:
            cp.wait()

    halve = (lambda s: (s[0], s[1], s[2] // 2)) if by_cols else (lambda s: (s[0], s[1] // 2, s[2]))
    return pl.pallas_call(
        body, out_shape=[jax.ShapeDtypeStruct(halve(g.shape), g.dtype) for g in gs],
        in_specs=[_ANY] * n, out_specs=[_ANY] * n,
        scratch_shapes=[pltpu.SemaphoreType.DMA((n,)), pltpu.SemaphoreType.DMA((n,))],
        name=name)(*gs)


def join_halves(rs, name):
    n = len(rs)

    def body(*refs):
        ins, outs = refs[:n], refs[n:2 * n]
        send_sems, recv_sems = refs[2 * n:]
        x, y, c, _ = _place()
        cps = [pltpu.make_async_remote_copy(
            src_ref=ins[i], dst_ref=outs[i], send_sem=send_sems.at[i], recv_sem=recv_sems.at[i],
            device_id=(x, y, 1 - c), device_id_type=MESH) for i in range(n)]
        for cp in cps:
            cp.start()
        for cp in cps:
            cp.wait()

    return pl.pallas_call(
        body, out_shape=[jax.ShapeDtypeStruct(r.shape, r.dtype) for r in rs],
        in_specs=[_ANY] * n, out_specs=[_ANY] * n,
        scratch_shapes=[pltpu.SemaphoreType.DMA((n,)), pltpu.SemaphoreType.DMA((n,))],
        name=name)(*rs)


def _pack(parts, row_mult=8):
    flat = jnp.concatenate([p.reshape(-1).astype(F32) for p in parts])
    unit = row_mult * 128
    n = -(-flat.shape[0] // unit) * unit
    return jnp.pad(flat, (0, n - flat.shape[0])).reshape(n // 128, 128)


def _unpack(flat, shapes):
    out, off = [], 0
    for s in shapes:
        n = int(np.prod(s))
        out.append(flat[off:off + n].reshape(s))
        off += n
    return out


def _gather_packed(parts, name):
    packed = _pack(parts)
    g = allgather_small(packed, name).reshape(N_DEV, -1)
    return _unpack_rows(g, [p.shape for p in parts])


def _unpack_rows(g, shapes):
    out, off = [], 0
    for s in shapes:
        n = int(np.prod(s))
        out.append(g[:, off:off + n].reshape((g.shape[0],) + tuple(s)))
        off += n
    return out


def _by_chip(t, axis):
    return jnp.concatenate([t[2 * p] for p in range(N_CHIPS)], axis=axis)


def kernel(x, c, ada_w, ada_b, ln_g, ln_b, a_in_w, a_conv_w, a_conv_b, a_dt_bias, a_A_log, a_D, a_norm_g, a_out_w, kv_w, b_in_w, b_out_w, loss_target, m_ada_w, m_ada_b, m_ln_g, m_ln_b, m_a_in_w, m_a_conv_w, m_a_conv_b, m_a_dt_bias, m_a_A_log, m_a_D, m_a_norm_g, m_a_out_w, m_kv_w, m_b_in_w, m_b_out_w, v_ada_w, v_ada_b, v_ln_g, v_ln_b, v_a_in_w, v_a_conv_w, v_a_conv_b, v_a_dt_bias, v_a_A_log, v_a_D, v_a_norm_g, v_a_out_w, v_kv_w, v_b_in_w, v_b_out_w):
    ax, ay, ac = lax.axis_index("x"), lax.axis_index("y"), lax.axis_index("c")
    chip = 2 * ax + ay
    dev = 4 * ax + 2 * ay + ac
    xin = x[0]
    tgt = loss_target[0]
    L, D = xin.shape
    G, P = SSD_G, SSD_P
    H = a_dt_bias.shape[1]
    Kh = H // G
    DI = H * P
    CONVD = a_conv_b.shape[1] * N_CHIPS
    HW = DIL_H * DIL_E
    Ws = ada_w.shape[2]

    w_in_g = allgather_routed(jnp.transpose(a_in_w[0]).astype(BF16), "allgather_w_in")
    later = [a_out_w[0].astype(BF16), kv_w.astype(BF16), b_in_w[0].astype(BF16), b_out_w[0].astype(BF16)]
    later_split = [s.reshape(2, s.shape[0] // 2, s.shape[1]) for s in later]
    ag_ssem, ag_rsem, ag_srcs, ag_lands, ag_token = split_start(
        "gather", later_split, [(N_CHIPS,) + s.shape for s in later_split], w_in_g, "ag_later_start")
    w_in_t = w_in_g.reshape(-1, D)
    w_dt_t = jnp.pad(w_in_t[DI + CONVD:], ((0, 128 - H), (0, 0)))

    c8, cw8, cb8, ng8 = _gather_packed([c[0], a_conv_w[0], a_conv_b[0], a_norm_g[0]], "allgather_small_params")
    conv_w = _by_chip(cw8, 1)
    conv_b = _by_chip(cb8, 0).reshape(1, CONVD)
    norm_g = _by_chip(ng8, 0).reshape(1, DI)

    mod_s = ada_fwd(c8, ada_w)
    (mod8,) = _gather_packed([mod_s], "allgather_small_mod")
    mods = _by_chip(mod8, 2)
    mod = lax.dynamic_index_in_dim(mods, dev, axis=1, keepdims=False) + ada_b
    shift = [mod[l:l + 1, :D] for l in range(DEPTH)]
    scale = [mod[l:l + 1, D:2 * D] for l in range(DEPTH)]
    gate = [mod[l:l + 1, 2 * D:] for l in range(DEPTH)]
    lg = [ln_g[l:l + 1] for l in range(DEPTH)]
    lb = [ln_b[l:l + 1] for l in range(DEPTH)]

    h0 = modulate(xin, scale[0] + ag_token[0:1, 0:1], shift[0], "modulate0")
    zx = mm_nt(h0, w_in_t, BF16, "mm_in_zx", kw_rows=DI + CONVD)
    dtp = mm_nt(h0, w_dt_t, F32, "mm_in_dt")
    xbc = conv_fwd(zx, DI, conv_w, conv_b)
    dtp_g = jnp.transpose(dtp[:, :H].reshape(L, G, Kh), (1, 0, 2))
    dtp_gT = jnp.transpose(dtp_g, (0, 2, 1))
    vecs = [a_dt_bias.reshape(G, 1, Kh), a_dt_bias.reshape(G, Kh, 1), a_A_log.reshape(G, 1, Kh),
            a_A_log.reshape(G, Kh, 1), a_D.reshape(G, 1, Kh), a_D.reshape(G, Kh, 1)]
    y_ssd, states, yn = ssd_fwd(xbc, dtp_g, dtp_gT, *vecs, zx, norm_g, DI)
    later_split, ag_lands = split_wait("gather", ag_ssem, ag_rsem, ag_srcs, ag_lands, yn, "ag_later_wait")
    (land_out,) = pass_to_sibling(ag_lands[:1])
    ps_ssem, ps_rsem, lands_b, ps_token = pass_start(ag_lands[1:], land_out, "ag_pass_start")

    def place_own(o, s, full):
        return lax.dynamic_update_index_in_dim(o, s, chip, 0).reshape((N_CHIPS,) + full.shape)

    w_out_g = place_own(land_out, later_split[0], later[0])
    ymix0 = mm_nn(yn, w_out_g.reshape(-1, D), F32, "mm_out_a", after=ps_token)
    x1, x1b, h1 = ln_mid(xin, ymix0, gate[0], lg[0], lb[0], scale[1], shift[1])
    lands_b = pass_wait(ps_ssem, ps_rsem, lands_b, x1b, "ag_pass_wait")
    w_kv_g, w_bin_g, w_bout_g = [place_own(o, s, full) for o, s, full in zip(lands_b, later_split[1:], later[1:])]

    n_grp = len(DIL_PATTERNS)
    cb = HW // 512
    assert w_bin_g.shape[2] == HW
    kv3 = [mm_cols_dilated(x1b, w_kv_g, [g * cb + t for t in range(cb)] + [(n_grp + g) * cb + t for t in range(cb)],
                           DIL_PATTERNS[g][1], f"mm_kv_{g}") for g in range(n_grp)]
    q3 = [mm_cols_dilated(h1, w_bin_g, [g], DIL_PATTERNS[g][1], f"mm_q_{g}", tn=HW) for g in range(n_grp)]
    z_b = mm_nn(h1, w_bin_g[n_grp], BF16, "mm_z_b")
    os_, lses = [], []
    for gi in range(len(DIL_PATTERNS)):
        o, lse = attn_fwd(q3[gi], kv3[gi], gi)
        os_.append(o)
        lses.append(lse)
    om = merge_fwd(os_, lses, z_b)
    ymix1 = mm_nn(om, w_bout_g, F32, "mm_out_b", stack="col")
    dres2, dy2, dg1, db1, dgate1, sq = ln_final_fwd_bwd(x1, ymix1, gate[1], lg[1], lb[1], tgt)
    loss_part = 0.5 * jnp.sum(sq) / D

    g_bout = mm_tn(om, dy2, BF16, "mm_gw_out_b", stack="col")
    dgated = mm_nt(dy2, w_bout_g, BF16, "mm_gx_out_b", stack="col")
    dos, dprs, dz_b = merge_bwd(dgated, os_, lses, z_b)
    dqs, dks, dvs = [], [], []
    for gi in range(len(DIL_PATTERNS)):
        dq, dk, dv = attn_bwd(q3[gi], kv3[gi], dos[gi], lses[gi], dprs[gi], gi)
        dqs.append(dq)
        dks.append(dk)
        dvs.append(dv)
    dqz = jnp.concatenate(dqs + [dz_b], axis=1)
    dkv = jnp.concatenate(dks + dvs, axis=1)
    g_bin = mm_tn(h1, dqz, BF16, "mm_gw_in_b", stack="col")
    dh1 = mm_nt(dqz, w_bin_g, BF16, "mm_gx_in_b", stack="col")
    g_kv = mm_tn(x1b, dkv, BF16, "mm_gw_kv", stack="col")

    core = ac.astype(jnp.int32).reshape(1)
    chip_i = chip.astype(jnp.int32).reshape(1)

    def begin_exchange(gs, tag):
        shapes = [(g.shape[0], g.shape[1] // 2, g.shape[2]) for g in gs]
        return split_start("sibling", gs, shapes, gs[0], "rs_x%s_start" % tag)

    def begin_scatter(gs, nms, tag, exchange=None, after=None, by_cols=False):
        if exchange is None:
            sib = exchange_halves_to_sibling(gs, "rs_sibling_exchange_" + tag, by_cols=by_cols)
        else:
            gs, sib = split_wait("sibling", exchange[0], exchange[1], exchange[2], exchange[3], after,
                                 "rs_x%s_wait" % tag)
        parts = [add_half(g, a, core, "rs_add_" + nm, by_cols=by_cols) for g, a, nm in zip(gs, sib, nms)]
        return split_start("scatter", parts, [(3,) + t.shape[1:] for t in parts], parts[0], "rs_%s_start" % tag)

    def sum_scattered(handles, after, tag):
        nms, owns, landed = [], [], []
        for k, (handle, hn) in enumerate(handles):
            parts, lands = split_wait("scatter", handle[0], handle[1], handle[2], handle[3], after,
                                      "rs_%s%d_wait" % (tag, k))
            nms += hn
            owns += list(parts)
            landed += list(lands)
        return nms, [sum_partials(own, t, chip_i, "rs_sum_" + nm) for own, t, nm in zip(owns, landed, nms)]

    def begin_join(halves, tag):
        return split_start("join", halves, [t.shape for t in halves], halves[0], "rs_j%s_start" % tag, land_dtype=F32)

    def end_join(handle, after, tag):
        return split_wait("join", handle[0], handle[1], handle[2], handle[3], after, "rs_j%s_wait" % tag)

    names_b = ["kv", "in_b", "out_b"]
    ex_b = begin_exchange([g_kv, g_bin, g_bout], "b")
    dx1_kv = mm_nt(dkv, w_kv_g, BF16, "mm_gx_kv", stack="col", after=ex_b[4])
    rs_b = begin_scatter(None, names_b, "b", exchange=ex_b, after=dx1_kv)

    dres1, dy1, dg0, db0, dgate0, dscale1, dshift1 = mod_ln_bwd(
        dres2, dh1, dx1_kv, x1, scale[1], xin, ymix0, gate[0] + rs_b[4][0:1, 0:1], lg[0])
    g_out = mm_tn(yn, dy1, BF16, "mm_gw_out_a", stack="row")
    ex_a1 = begin_exchange([g_out], "a1")
    dyn = mm_nt(dy1, w_out_g, BF16, "mm_gx_out_a", stack="row", after=ex_a1[4])
    rs_a1 = begin_scatter(None, ["out_a"], "a1", exchange=ex_a1, after=dyn)
    dxs, dB, dC, ddtp_g, dbias_g, dalog_g, dD_g, dz_a, dnorm_g = ssd_bwd(
        xbc, dtp_g, dtp_gT, *vecs, states, dyn, y_ssd, zx, norm_g + rs_a1[4][0:1, 0:1], DI)
    dzx, dws, dbs, lo = dz_a, [], [], 0
    for tag, gpart in (("xs", dxs), ("b", dB), ("c", dC)):
        hi = lo + gpart.shape[1]
        dzx, dw_p, db_p = conv_bwd(zx, DI + lo, conv_w[:, lo:hi], conv_b[:, lo:hi], gpart, dzx, "conv_bwd_" + tag)
        dws.append(dw_p)
        dbs.append(db_p)
        lo = hi
    dconv_w = jnp.concatenate(dws, axis=1)
    dconv_b = jnp.concatenate(dbs, axis=1)
    ddtp = jnp.pad(jnp.transpose(ddtp_g, (1, 0, 2)).reshape(L, H), ((0, 0), (0, 128 - H)))
    g_inT = mm_tn(dzx, h0, BF16, "mm_gw_in_zx", m_rows=DI + CONVD + H)
    g_dtT = mm_tn(ddtp, h0, BF16, "mm_gw_in_dt")
    g_inT = lax.dynamic_update_slice(g_inT, g_dtT[:H], (DI + CONVD, 0))
    rs_a2 = begin_scatter([g_inT.reshape(N_CHIPS, -1, D)], ["in_a"], "a2", by_cols=True)
    dh0 = mm_nn(dzx, w_in_t, BF16, "mm_gx_in_zx", after=rs_a2[4])
    dh0_dt = mm_nn(ddtp, w_dt_t, F32, "mm_gx_in_dt")
    grad_x, dscale0, dshift0 = mod_bwd(dres1, dh0, dh0_dt, xin, scale[0] + rs_a2[4][0:1, 0:1], "mod_bwd0")
    nms_b, halves_b = sum_scattered([(rs_b, names_b)], grad_x, "b")
    join_b = begin_join(halves_b, "b")
    nms_a, halves_a = sum_scattered([(rs_a1, ["out_a"]), (rs_a2, ["in_a"])], join_b[4], "a")
    g_halves = dict(zip(nms_b, zip(*end_join(join_b, halves_a[0], "b"))))
    join_a = begin_join(halves_a, "a")

    def step_halves(w, m, v, nm, after=None):
        shp = w.shape
        mine, theirs_ = g_halves[nm]
        outs4 = adamw_halves(w.reshape(-1, shp[-1]), mine, theirs_, m.reshape(-1, shp[-1]), v.reshape(-1, shp[-1]),
                             core, "adamw_" + nm, after=after)
        return tuple(t.reshape(shp) for t in outs4)

    big = {
        "kv_w": step_halves(kv_w, m_kv_w, v_kv_w, "kv", after=join_a[4]),
        "b_in_w": step_halves(b_in_w, m_b_in_w, v_b_in_w, "in_b"),
        "b_out_w": step_halves(b_out_w, m_b_out_w, v_b_out_w, "out_b"),
    }
    g_halves.update(dict(zip(nms_a, zip(*end_join(join_a, big["kv_w"][1], "a")))))
    g_halves["in_a"] = tuple(jnp.transpose(t) for t in g_halves["in_a"])

    dmod = jnp.concatenate([jnp.concatenate([dshift0, dscale0, dgate0], axis=1),
                            jnp.concatenate([dshift1, dscale1, dgate1], axis=1)], axis=0)
    small_parts = [jnp.concatenate([dg0, dg1], axis=0), jnp.concatenate([db0, db1], axis=0),
                   dbias_g.reshape(1, H), dalog_g.reshape(1, H), dD_g.reshape(1, H),
                   dconv_w, dconv_b, dnorm_g, loss_part.reshape(1, 1)]
    small_shapes = [p.shape for p in small_parts]
    packed = jnp.concatenate([_pack([dmod]), _pack(small_parts)], axis=0)
    n_mod_rows = _pack([dmod]).shape[0]
    sg_ssem, sg_rsem, sg_src, sg_land, sg_token = gather8_start(packed, g_halves["in_a"][1], "small_grads_start")
    big["a_in_w"] = step_halves(a_in_w, m_a_in_w, v_a_in_w, "in_a", after=sg_token)
    big["a_out_w"] = step_halves(a_out_w, m_a_out_w, v_a_out_w, "out_a")
    sg_land = gather8_wait(sg_ssem, sg_rsem, sg_src, sg_land, big["a_in_w"][1], "small_grads_wait")
    gathered = lax.dynamic_update_index_in_dim(sg_land, packed, dev, 0)
    dmod8 = gathered[:, :n_mod_rows].reshape(N_DEV, -1)[:, :2 * 3 * D].reshape(N_DEV, DEPTH, 3 * D)
    summed = sum_leading(gathered, "sum_small")
    g_ada_b = summed[:n_mod_rows].reshape(-1)[:2 * 3 * D].reshape(DEPTH, 3 * D)
    (g_ln_g, g_ln_b, g_dt_bias, g_a_log, g_dsk, g_conv_w, g_conv_b, g_norm_g, loss_all) = _unpack(
        summed[n_mod_rows:].reshape(-1), small_shapes)
    loss = loss_all.reshape(())
    Cs = CONVD // N_CHIPS
    g_conv_w_s = lax.dynamic_slice_in_dim(g_conv_w, chip * Cs, Cs, axis=1)
    g_conv_b_s = lax.dynamic_slice_in_dim(g_conv_b, chip * Cs, Cs, axis=1)
    g_norm_g_s = lax.dynamic_slice_in_dim(g_norm_g, chip * (DI // N_CHIPS), DI // N_CHIPS, axis=1)
    dmod_s = jnp.transpose(lax.dynamic_slice_in_dim(dmod8, chip * Ws, Ws, axis=2), (1, 0, 2))

    def step2d(w, g, m, v, nm):
        shp = w.shape
        d_, m_, v_ = adamw(w.reshape(-1, shp[-1]), g.reshape(-1, shp[-1]), m.reshape(-1, shp[-1]),
                           v.reshape(-1, shp[-1]), "adamw_" + nm)
        return g.reshape(shp), d_.reshape(shp), m_.reshape(shp), v_.reshape(shp)

    big["ada_w"] = step2d(ada_w, ada_wgrad(jnp.transpose(c8), dmod_s), m_ada_w, v_ada_w, "ada_w")
    small_names = ["ada_b", "ln_g", "ln_b", "a_conv_w", "a_conv_b", "a_dt_bias", "a_A_log", "a_D", "a_norm_g"]
    small_w = [ada_b, ln_g, ln_b, a_conv_w, a_conv_b, a_dt_bias, a_A_log, a_D, a_norm_g]
    small_m = [m_ada_b, m_ln_g, m_ln_b, m_a_conv_w, m_a_conv_b, m_a_dt_bias, m_a_A_log, m_a_D, m_a_norm_g]
    small_v = [v_ada_b, v_ln_g, v_ln_b, v_a_conv_w, v_a_conv_b, v_a_dt_bias, v_a_A_log, v_a_D, v_a_norm_g]
    small_g = [g_ada_b, g_ln_g, g_ln_b, g_conv_w_s, g_conv_b_s, g_dt_bias, g_a_log, g_dsk, g_norm_g_s]
    shapes = [w.shape for w in small_w]
    small_g = [g.reshape(s) for g, s in zip(small_g, shapes)]
    d_p, m_p, v_p = adamw(_pack(small_w), _pack(small_g), _pack(small_m), _pack(small_v), "adamw_small")
    small = {}
    for nm, g, d_, m_, v_ in zip(small_names, small_g, _unpack(d_p.reshape(-1), shapes), _unpack(m_p.reshape(-1), shapes),
                                 _unpack(v_p.reshape(-1), shapes)):
        small[nm] = (g, d_, m_, v_)
    allw = {**big, **small}
    order = ["ada_w", "ada_b", "ln_g", "ln_b", "a_in_w", "a_conv_w", "a_conv_b", "a_dt_bias", "a_A_log", "a_D",
             "a_norm_g", "a_out_w", "kv_w", "b_in_w", "b_out_w"]
    outs = [loss, grad_x.reshape(x.shape)]
    for k in range(4):
        outs += [allw[n][k] for n in order]
    return tuple(outs)
```

```python
import functools

import jax
import jax.numpy as jnp
import numpy as np
from jax import lax
from jax.experimental import pallas as pl
from jax.experimental.pallas import tpu as pltpu

F32 = jnp.float32
BF16 = jnp.bfloat16
MESH = pl.DeviceIdType.MESH

DEPTH = 2
ALPHA = (2 * DEPTH) ** 0.25
LN_EPS = 1e-5
RMS_EPS = 1e-5
SSD_P = 64
SSD_N = 128
SSD_Q = 256
SSD_G = 8
CONV_W = 4
DIL_PATTERNS = ((128, 1), (512, 4), (2048, 16))
DIL_H = 8
DIL_E = 128
DIL_BLK = 128
ADAM_LR, ADAM_B1, ADAM_B2, ADAM_EPS, ADAM_WD, ADAM_STEP = 0.001, 0.9, 0.999, 1e-08, 0.01, 10

VMEM_LIMIT = 56 * 1024 * 1024
N_CHIPS = 4
N_DEV = 8


def _tile(dim, target, mult=128):
    if dim <= target:
        return dim
    t = (target // mult) * mult
    while t >= mult:
        if dim % t == 0:
            return t
        t -= mult
    return dim


def _cp(sem):
    return pltpu.CompilerParams(dimension_semantics=sem, vmem_limit_bytes=VMEM_LIMIT)


def _sigmoid(x):
    return 1.0 / (1.0 + jnp.exp(-x))


def _silu(x):
    return x * _sigmoid(x)


def _dsilu(x):
    s = _sigmoid(x)
    return s * (1.0 + x * (1.0 - s))


def _softplus(x):
    return jnp.maximum(x, 0.0) + jnp.log(1.0 + jnp.exp(-jnp.abs(x)))


def _mm_call(a, b, out_shape, grid, a_spec, b_spec, o_spec, acc_shape, dims, name, after=None):
    nk = grid[2]
    extra = [] if after is None else [after]

    def prod(a_ref, b_ref):
        return lax.dot_general(a_ref[...].astype(BF16), b_ref[...].astype(BF16), (dims, ((), ())),
                               preferred_element_type=F32)

    def body_single(a_ref, b_ref, *rest):
        o_ref = rest[len(extra)]
        o_ref[...] = prod(a_ref, b_ref).astype(o_ref.dtype)

    def body_multi(a_ref, b_ref, *rest):
        o_ref, acc_ref = rest[len(extra):]
        k = pl.program_id(2)

        @pl.when(k == 0)
        def _():
            acc_ref[...] = prod(a_ref, b_ref)

        @pl.when(jnp.logical_and(k > 0, k < nk - 1))
        def _():
            acc_ref[...] += prod(a_ref, b_ref)

        @pl.when(k == nk - 1)
        def _():
            o_ref[...] = (acc_ref[...] + prod(a_ref, b_ref)).astype(o_ref.dtype)

    return pl.pallas_call(
        body_single if nk == 1 else body_multi, grid=grid, in_specs=[a_spec, b_spec] + [_ANY] * len(extra),
        out_specs=o_spec, out_shape=out_shape, scratch_shapes=[] if nk == 1 else [pltpu.VMEM(acc_shape, F32)],
        compiler_params=_cp(("parallel", "parallel", "arbitrary")), name=name)(a, b, *extra)


def mm_nn(a, b, out_dtype, name, stack=None, tm=1024, tn=1024, tk=2048, n_cols=None, after=None):
    M, K = a.shape
    if stack is None:
        N = b.shape[1] if n_cols is None else n_cols
        tn, tk = _tile(N, tn), _tile(K, tk)
        b_spec = pl.BlockSpec((tk, tn), lambda i, j, k: (k, j))
    elif stack == "col":
        S, _, Ns = b.shape
        N = S * Ns
        tn, tk = _tile(Ns, tn), _tile(K, tk)
        npb = Ns // tn
        b_spec = pl.BlockSpec((None, tk, tn), lambda i, j, k: (j // npb, k, j % npb))
    else:
        S, Ks, N = b.shape
        tn, tk = _tile(N, tn), _tile(Ks, tk)
        kpb = Ks // tk
        b_spec = pl.BlockSpec((None, tk, tn), lambda i, j, k: (k // kpb, k % kpb, j))
    tm = _tile(M, tm)
    return _mm_call(a, b, jax.ShapeDtypeStruct((M, N), out_dtype), (M // tm, N // tn, K // tk),
                    pl.BlockSpec((tm, tk), lambda i, j, k: (i, k)), b_spec,
                    pl.BlockSpec((tm, tn), lambda i, j, k: (i, j)), (tm, tn), ((1,), (0,)), name, after=after)


def mm_cols_dilated(a, b, gcols, d, name, tm=1024, tn=512):
    L, K = a.shape
    S, _, Ns = b.shape
    tm, tn = _tile(L, tm), _tile(Ns, tn)
    npb = Ns // tn
    nj = len(gcols)
    rows = tm // d

    def body(cols_ref, a_ref, b_ref, o_ref, *scr):
        prod = jnp.dot(a_ref[...], b_ref[...], preferred_element_type=F32)
        if d == 1:
            o_ref[0] = prod.astype(BF16)
        else:
            for c in range(tn // 128):
                scr[0][c] = prod[:, c * 128:(c + 1) * 128]
            for r in range(d):
                for c in range(tn // 128):
                    o_ref[r, :, c * 128:(c + 1) * 128] = scr[0].at[c][pl.ds(r, rows, stride=d), :].astype(BF16)

    return pl.pallas_call(
        body,
        grid_spec=pltpu.PrefetchScalarGridSpec(
            num_scalar_prefetch=1, grid=(L // tm, nj),
            in_specs=[pl.BlockSpec((tm, K), lambda i, j, c: (i, 0)),
                      pl.BlockSpec((None, K, tn), lambda i, j, c: (c[j] // npb, 0, c[j] % npb))],
            out_specs=pl.BlockSpec((d, rows, tn), lambda i, j, c: (0, i, j)),
            scratch_shapes=[] if d == 1 else [pltpu.VMEM((tn // 128, tm, 128), F32)]),
        out_shape=jax.ShapeDtypeStruct((d, L // d, nj * tn), BF16),
        compiler_params=_cp(("parallel", "arbitrary")), name=name)(jnp.asarray(gcols, jnp.int32), a, b)


def mm_nt(a, b, out_dtype, name, stack=None, tm=1024, tn=1024, tk=2048, after=None, kw_rows=None):
    M, C = a.shape
    if stack is None:
        Kw = b.shape[0] if kw_rows is None else kw_rows
        tn, tk = _tile(Kw, tn), _tile(C, tk)
        b_spec = pl.BlockSpec((tn, tk), lambda i, j, k: (j, k))
    elif stack == "col":
        S, Kw, Cs = b.shape
        tn, tk = _tile(Kw, tn), _tile(Cs, tk)
        cpb = Cs // tk
        b_spec = pl.BlockSpec((None, tn, tk), lambda i, j, k: (k // cpb, j, k % cpb))
    else:
        S, Ks, _ = b.shape
        Kw = S * Ks
        tn, tk = _tile(Ks, tn), _tile(C, tk)
        jpb = Ks // tn
        b_spec = pl.BlockSpec((None, tn, tk), lambda i, j, k: (j // jpb, j % jpb, k))
    tm = _tile(M, tm)
    return _mm_call(a, b, jax.ShapeDtypeStruct((M, Kw), out_dtype), (M // tm, Kw // tn, C // tk),
                    pl.BlockSpec((tm, tk), lambda i, j, k: (i, k)), b_spec,
                    pl.BlockSpec((tm, tn), lambda i, j, k: (i, j)), (tm, tn), ((1,), (1,)), name, after=after)


def mm_tn(a, b, out_dtype, name, stack=None, n_stack=N_CHIPS, tm=1024, tn=1024, tk=2048, m_rows=None):
    L, M = a.shape
    N = b.shape[1]
    tk = _tile(L, tk)
    if stack is None:
        tm, tn = _tile(M, tm), _tile(N, tn)
        o_spec = pl.BlockSpec((tm, tn), lambda i, j, k: (i, j))
        out_shape = (M if m_rows is None else m_rows, N)
    elif stack == "col":
        Ns = N // n_stack
        tm, tn = _tile(M, tm), _tile(Ns, tn)
        npb = Ns // tn
        o_spec = pl.BlockSpec((None, tm, tn), lambda i, j, k: (j // npb, i, j % npb))
        out_shape = (n_stack, M, Ns)
    else:
        Ms = M // n_stack
        tm, tn = _tile(Ms, tm), _tile(N, tn)
        mpb = Ms // tm
        o_spec = pl.BlockSpec((None, tm, tn), lambda i, j, k: (i // mpb, i % mpb, j))
        out_shape = (n_stack, Ms, N)
    return _mm_call(a, b, jax.ShapeDtypeStruct(out_shape, out_dtype), (M // tm, N // tn, L // tk),
                    pl.BlockSpec((tk, tm), lambda i, j, k: (k, i)), pl.BlockSpec((tk, tn), lambda i, j, k: (k, j)),
                    o_spec, (tm, tn), ((0,), (0,)), name)


def _row_specs(tr, widths):
    return [pl.BlockSpec((tr, w), lambda i: (i, 0)) for w in widths]


def _vec_spec(w):
    return pl.BlockSpec((1, w), lambda i: (0, 0))


def _acc_rows(ref, val, i):
    s = jnp.sum(val, axis=0, keepdims=True)

    @pl.when(i == 0)
    def _():
        ref[...] = s

    @pl.when(i > 0)
    def _():
        ref[...] += s


def modulate(x, scale, shift, name):
    L, D = x.shape
    tr = _tile(L, 256, 16)

    def body(x_ref, sc_ref, sh_ref, h_ref):
        h_ref[...] = (x_ref[...] * (1.0 + sc_ref[...]) + sh_ref[...]).astype(BF16)

    return pl.pallas_call(
        body, grid=(L // tr,), in_specs=_row_specs(tr, [D]) + [_vec_spec(D)] * 2, out_specs=_row_specs(tr, [D])[0],
        out_shape=jax.ShapeDtypeStruct((L, D), BF16), compiler_params=_cp(("parallel",)), name=name)(x, scale, shift)


def _ln_core(x, y, gate, g, b):
    u = ALPHA * x + (1.0 + gate) * y
    mu = jnp.mean(u, axis=-1, keepdims=True)
    d = u - mu
    var = jnp.mean(d * d, axis=-1, keepdims=True)
    rstd = lax.rsqrt(var + LN_EPS)
    xhat = d * rstd
    return xhat * g + b, xhat, rstd


def ln_mid(x, y, gate, g, b, scale, shift):
    L, D = x.shape
    tr = _tile(L, 256, 16)

    def body(x_ref, y_ref, gate_ref, g_ref, b_ref, sc_ref, sh_ref, x1_ref, x1b_ref, h_ref):
        x1, _, _ = _ln_core(x_ref[...], y_ref[...], gate_ref[...], g_ref[...], b_ref[...])
        x1_ref[...] = x1
        x1b_ref[...] = x1.astype(BF16)
        h_ref[...] = (x1 * (1.0 + sc_ref[...]) + sh_ref[...]).astype(BF16)

    return pl.pallas_call(
        body, grid=(L // tr,), in_specs=_row_specs(tr, [D, D]) + [_vec_spec(D)] * 5,
        out_specs=_row_specs(tr, [D, D, D]),
        out_shape=[jax.ShapeDtypeStruct((L, D), F32), jax.ShapeDtypeStruct((L, D), BF16),
                   jax.ShapeDtypeStruct((L, D), BF16)],
        compiler_params=_cp(("parallel",)), name="ln_mid")(x, y, gate, g, b, scale, shift)


def _ln_bwd_rows(dout_v, xhat, rstd, g):
    dxh = dout_v * g
    m1 = jnp.mean(dxh, axis=-1, keepdims=True)
    m2 = jnp.mean(dxh * xhat, axis=-1, keepdims=True)
    return rstd * (dxh - m1 - xhat * m2)


def ln_final_fwd_bwd(x, y, gate, g, b, target):
    L, D = x.shape
    tr = _tile(L, 256, 16)

    def body(x_ref, y_ref, gate_ref, g_ref, b_ref, t_ref, dres_ref, dy_ref, dg_ref, db_ref, dgate_ref, sq_ref):
        i = pl.program_id(0)
        yv = y_ref[...]
        out, xhat, rstd = _ln_core(x_ref[...], yv, gate_ref[...], g_ref[...], b_ref[...])
        err = out - t_ref[...]
        dout_v = err * (1.0 / D)
        du = _ln_bwd_rows(dout_v, xhat, rstd, g_ref[...])
        dres_ref[...] = ALPHA * du
        dy_ref[...] = ((1.0 + gate_ref[...]) * du).astype(BF16)
        _acc_rows(dg_ref, dout_v * xhat, i)
        _acc_rows(db_ref, dout_v, i)
        _acc_rows(dgate_ref, du * yv, i)
        _acc_rows(sq_ref, err * err, i)

    return pl.pallas_call(
        body, grid=(L // tr,), in_specs=_row_specs(tr, [D, D]) + [_vec_spec(D)] * 3 + _row_specs(tr, [D]),
        out_specs=_row_specs(tr, [D, D]) + [_vec_spec(D)] * 4,
        out_shape=[jax.ShapeDtypeStruct((L, D), F32), jax.ShapeDtypeStruct((L, D), BF16)]
        + [jax.ShapeDtypeStruct((1, D), F32)] * 4,
        compiler_params=_cp(("arbitrary",)), name="ln_final_fwd_bwd")(x, y, gate, g, b, target)


def mod_bwd(dres, dh, dh2, xin, scale, name):
    L, D = xin.shape
    tr = _tile(L, 256, 16)

    def body(dres_ref, dh_ref, dh2_ref, x_ref, sc_ref, dx_ref, dsc_ref, dsh_ref):
        i = pl.program_id(0)
        dh_v = dh_ref[...].astype(F32) + dh2_ref[...].astype(F32)
        dx_ref[...] = dres_ref[...] + dh_v * (1.0 + sc_ref[...])
        _acc_rows(dsc_ref, dh_v * x_ref[...], i)
        _acc_rows(dsh_ref, dh_v, i)

    return pl.pallas_call(
        body, grid=(L // tr,), in_specs=_row_specs(tr, [D, D, D, D]) + [_vec_spec(D)],
        out_specs=_row_specs(tr, [D]) + [_vec_spec(D)] * 2,
        out_shape=[jax.ShapeDtypeStruct((L, D), F32)] + [jax.ShapeDtypeStruct((1, D), F32)] * 2,
        compiler_params=_cp(("arbitrary",)), name=name)(dres, dh, dh2, xin, scale)


def mod_ln_bwd(dres_in, dh, dskip, xmid, scale, x, y, gate, g):
    L, D = x.shape
    tr = _tile(L, 256, 16)

    def body(dres_ref, dh_ref, dskip_ref, xm_ref, sc_ref, x_ref, y_ref, gate_ref, g_ref,
             dres_out, dy_ref, dg_ref, db_ref, dgate_ref, dsc_ref, dsh_ref):
        i = pl.program_id(0)
        dh_v = dh_ref[...].astype(F32)
        dout_v = dres_ref[...] + dskip_ref[...].astype(F32) + dh_v * (1.0 + sc_ref[...])
        _acc_rows(dsc_ref, dh_v * xm_ref[...], i)
        _acc_rows(dsh_ref, dh_v, i)
        yv = y_ref[...]
        _, xhat, rstd = _ln_core(x_ref[...], yv, gate_ref[...], g_ref[...], 0.0)
        du = _ln_bwd_rows(dout_v, xhat, rstd, g_ref[...])
        dres_out[...] = ALPHA * du
        dy_ref[...] = ((1.0 + gate_ref[...]) * du).astype(BF16)
        _acc_rows(dg_ref, dout_v * xhat, i)
        _acc_rows(db_ref, dout_v, i)
        _acc_rows(dgate_ref, du * yv, i)

    return pl.pallas_call(
        body, grid=(L // tr,),
        in_specs=_row_specs(tr, [D] * 4) + [_vec_spec(D)] + _row_specs(tr, [D, D]) + [_vec_spec(D)] * 2,
        out_specs=_row_specs(tr, [D, D]) + [_vec_spec(D)] * 5,
        out_shape=[jax.ShapeDtypeStruct((L, D), F32), jax.ShapeDtypeStruct((L, D), BF16)]
        + [jax.ShapeDtypeStruct((1, D), F32)] * 5,
        compiler_params=_cp(("arbitrary",)), name="mod_ln_bwd")(dres_in, dh, dskip, xmid, scale, x, y, gate, g)


CONV_HALO = 16


def _conv_rows(x_ref, i, tr, L):
    nblk = L // tr
    s = pl.multiple_of(i * tr, CONV_HALO)
    cur = x_ref[pl.ds(s, tr), :].astype(F32)
    sp = pl.multiple_of(jnp.maximum(i * tr - CONV_HALO, 0), CONV_HALO)
    sn = pl.multiple_of(jnp.minimum(i * tr + tr, L - CONV_HALO), CONV_HALO)
    prev = x_ref[pl.ds(sp, CONV_HALO), :].astype(F32) * (i > 0).astype(F32)
    nxt = x_ref[pl.ds(sn, CONV_HALO), :].astype(F32) * (i < nblk - 1).astype(F32)
    return jnp.concatenate([prev, cur, nxt], axis=0)


def _shift_rows(v, j):
    n = v.shape[0]
    return v if j % n == 0 else pltpu.roll(v, j % n, 0)


def _conv_taps(xe):
    return [_shift_rows(xe, CONV_W - 1 - k) for k in range(CONV_W)]


def _conv_eval(taps, w_ref, b_ref):
    c = b_ref[...] + w_ref[0:1, :] * taps[0]
    for k in range(1, CONV_W):
        c = c + w_ref[k:k + 1, :] * taps[k]
    return c


def conv_fwd(zx, col0, conv_w, conv_b):
    L = zx.shape[0]
    C = conv_w.shape[1]
    tc = _tile(C, 512)
    tr = _tile(L, 512, CONV_HALO)
    off = col0 // tc

    def body(x_ref, w_ref, b_ref, o_ref):
        i = pl.program_id(1)
        xe = _conv_rows(x_ref, i, tr, L)
        c = _conv_eval(_conv_taps(xe), w_ref, b_ref)[CONV_HALO:CONV_HALO + tr]
        o_ref[...] = _silu(c).astype(BF16)

    return pl.pallas_call(
        body, grid=(C // tc, L // tr),
        in_specs=[pl.BlockSpec((L, tc), lambda j, i: (0, off + j)), pl.BlockSpec((CONV_W, tc), lambda j, i: (0, j)),
                  pl.BlockSpec((1, tc), lambda j, i: (0, j))],
        out_specs=pl.BlockSpec((tr, tc), lambda j, i: (i, j)),
        out_shape=jax.ShapeDtypeStruct((L, C), BF16), compiler_params=_cp(("parallel", "arbitrary")),
        name="conv_fwd")(zx, conv_w, conv_b)


def conv_bwd(zx, col0, conv_w, conv_b, g, dzx, name):
    L = zx.shape[0]
    C = conv_w.shape[1]
    tc = _tile(C, 512)
    tr = _tile(L, 512, CONV_HALO)
    off = col0 // tc
    H = CONV_HALO

    def body(x_ref, g_ref, w_ref, b_ref, buf_ref, dx_ref, dw_ref, db_ref):
        i = pl.program_id(1)
        xe = _conv_rows(x_ref, i, tr, L)
        ge = _conv_rows(g_ref, i, tr, L)
        taps = _conv_taps(xe)
        dc = ge * _dsilu(_conv_eval(taps, w_ref, b_ref))
        dx = w_ref[CONV_W - 1:CONV_W, :] * dc
        for k in range(CONV_W - 1):
            dx = dx + w_ref[k:k + 1, :] * _shift_rows(dc, -(CONV_W - 1 - k))
        dx_ref[...] = dx[H:H + tr].astype(BF16)
        dcc = dc[H:H + tr]
        rows = [jnp.sum(dcc * taps[k][H:H + tr], axis=0, keepdims=True) for k in range(CONV_W)]
        dwv = jnp.concatenate(rows + [jnp.zeros((8 - CONV_W, tc), F32)], axis=0)
        dbv = jnp.sum(dcc, axis=0, keepdims=True)

        @pl.when(i == 0)
        def _():
            dw_ref[...] = dwv
            db_ref[...] = dbv

        @pl.when(i > 0)
        def _():
            dw_ref[...] += dwv
            db_ref[...] += dbv

    dx, dw, db = pl.pallas_call(
        body, grid=(C // tc, L // tr),
        in_specs=[pl.BlockSpec((L, tc), lambda j, i: (0, off + j)), pl.BlockSpec((L, tc), lambda j, i: (0, j)),
                  pl.BlockSpec((CONV_W, tc), lambda j, i: (0, j)), pl.BlockSpec((1, tc), lambda j, i: (0, j)), _ANY],
        out_specs=[pl.BlockSpec((tr, tc), lambda j, i: (i, off + j)), pl.BlockSpec((8, tc), lambda j, i: (0, j)),
                   pl.BlockSpec((1, tc), lambda j, i: (0, j))],
        out_shape=[jax.ShapeDtypeStruct(dzx.shape, BF16), jax.ShapeDtypeStruct((8, C), F32),
                   jax.ShapeDtypeStruct((1, C), F32)],
        input_output_aliases={4: 0},
        compiler_params=_cp(("parallel", "arbitrary")), name=name)(zx, g, conv_w, conv_b, dzx)
    return dx, dw[:CONV_W], db


_NN = (((1,), (0,)), ((), ()))


def _pieces(x, n):
    out, r = [], x
    for _ in range(n):
        p = r.astype(BF16)
        out.append(p)
        r = r - p.astype(F32)
    return out


def _dot01(a, b01, n, dims=_NN):
    b = b01.astype(BF16)
    return functools.reduce(lambda u, v: u + v,
                            [lax.dot_general(p, b, dims, preferred_element_type=F32) for p in _pieces(a, n)])


def _dot01_left(a01, b, n, dims=_NN):
    a = a01.astype(BF16)
    return functools.reduce(lambda u, v: u + v,
                            [lax.dot_general(a, p, dims, preferred_element_type=F32) for p in _pieces(b, n)])


def _ssd_common(dtp_ref, dtpT_ref, bias_ref, biasT_ref, alog_ref, alogT_ref, b_ref, c_ref):
    Q = SSD_Q
    dt = _softplus(dtp_ref[...] + bias_ref[...])
    A = -jnp.exp(alog_ref[...])
    row = lax.broadcasted_iota(jnp.int32, (Q, Q), 0)
    col = lax.broadcasted_iota(jnp.int32, (Q, Q), 1)
    causal = row >= col
    tril = causal.astype(F32)
    Kh = dt.shape[1]
    acum = _dot01_left(tril, dt * A, 3)
    eye = (lax.broadcasted_iota(jnp.int32, (Kh, Kh), 0) == lax.broadcasted_iota(jnp.int32, (Kh, Kh), 1)).astype(F32)
    acumT = _dot01_left(eye, acum, 3, dims=(((1,), (1,)), ((), ())))
    Bm = b_ref[...]
    Cm = c_ref[...]
    cb = lax.dot_general(Cm, Bm, (((1,), (1,)), ((), ())), preferred_element_type=F32)
    return dt, A, causal, row, col, acum, acumT, Bm, Cm, cb


def _ssd_in_specs(Q, GP, N, Kh, DI, cmap):
    nb0 = DI // N
    vec = pl.BlockSpec((None, 1, Kh), lambda g, c: (g, 0, 0))
    vecT = pl.BlockSpec((None, Kh, 1), lambda g, c: (g, 0, 0))
    return [pl.BlockSpec((Q, GP), lambda g, c: (cmap(c), g)),
            pl.BlockSpec((Q, N), lambda g, c: (cmap(c), nb0 + g)),
            pl.BlockSpec((Q, N), lambda g, c: (cmap(c), nb0 + SSD_G + g)),
            pl.BlockSpec((None, Q, Kh), lambda g, c: (g, cmap(c), 0)),
            pl.BlockSpec((None, Kh, Q), lambda g, c: (g, 0, cmap(c))),
            vec, vecT, vec, vecT, vec, vecT]


def _hi(a, b01):
    return _dot01(a, b01, 2)


def _headsum(a, b01):
    return _dot01(a, b01, 1)


def _ssd_heads(dskT_ref, acum, acumT, dt, Kh):
    Q, P, N = SSD_Q, SSD_P, SSD_N
    GP = Kh * P
    sh_p = P.bit_length() - 1
    seg = lambda shape, dim: lax.shift_right_logical(lax.broadcasted_iota(jnp.int32, shape, dim), sh_p)
    E = (seg((Kh, GP), 1) == lax.broadcasted_iota(jnp.int32, (Kh, GP), 0)).astype(F32)
    ET = (seg((GP, Kh), 0) == lax.broadcasted_iota(jnp.int32, (GP, Kh), 1)).astype(F32)
    a_last = acum[Q - 1:Q, :]
    tail = jnp.exp(a_last - acum)
    eLT = jnp.exp(acumT[:, Q - 1:Q])
    rowseg = seg((GP, N), 0)
    eL_b = jnp.zeros((GP, N), F32)
    for k in range(Kh):
        eL_b = jnp.where(rowseg == k, eLT[k:k + 1, :], eL_b)
    return dict(
        E=E, ET=ET, a_last=a_last, tail=tail, eL_b=eL_b,
        dt_all=_hi(dt, E), ea_all=_headsum(jnp.exp(acum), E), tail_all=_headsum(tail, E),
        dsk_all=jnp.sum(E * dskT_ref[...], axis=0, keepdims=True))


def _head_chunks(GP):
    CW = min(GP, 128)
    return CW, CW // SSD_P, GP // CW


def _head_mask(Q, CW, kk):
    lane = lax.broadcasted_iota(jnp.int32, (Q, CW), 1)
    return jnp.logical_and(lane >= kk * SSD_P, lane < (kk + 1) * SSD_P)


def ssd_fwd(xbc, dtp_g, dtp_gT, bias_g, bias_gT, alog_g, alog_gT, dsk_g, dsk_gT, zx, norm_g, DI):
    L = xbc.shape[0]
    Q, P, N, G = SSD_Q, SSD_P, SSD_N, SSD_G
    GP = DI // G
    Kh = GP // P
    nc = L // Q

    CW, hpc, nch = _head_chunks(GP)
    nt = (((1,), (1,)), ((), ()))
    tn = (((0,), (0,)), ((), ()))

    def body(xs_ref, b_ref, c_ref, dtp_ref, dtpT_ref, bias_ref, biasT_ref, alog_ref, alogT_ref, dsk_ref, dskT_ref,
             z_ref, ng_ref, y_ref, st_ref, yn_ref, state):
        @pl.when(pl.program_id(1) == 0)
        def _():
            state[...] = jnp.zeros(state.shape, F32)

        st_ref[...] = state[...]
        dt, A, causal, row, col, acum, acumT, Bm, Cm, cb = _ssd_common(
            dtp_ref, dtpT_ref, bias_ref, biasT_ref, alog_ref, alogT_ref, b_ref, c_ref)
        hd = _ssd_heads(dskT_ref, acum, acumT, dt, Kh)
        xs = xs_ref[...].astype(F32)
        xdt_all = xs * hd["dt_all"]
        S_all = state[...]
        y_all = (lax.dot_general(Cm, S_all.astype(BF16), nt, preferred_element_type=F32) * hd["ea_all"]
                 + xs * hd["dsk_all"])
        state[...] = S_all * hd["eL_b"] + lax.dot_general(
            (xdt_all * hd["tail_all"]).astype(BF16), Bm, tn, preferred_element_type=F32)
        for ch in range(nch):
            cs = slice(ch * CW, (ch + 1) * CW)
            xc = xdt_all[:, cs]
            acc = y_all[:, cs]
            for kk in range(hpc):
                k = ch * hpc + kk
                decay = jnp.exp(jnp.where(causal, acum[:, k:k + 1] - acumT[k:k + 1, :], -jnp.inf))
                xk = xc if hpc == 1 else jnp.where(_head_mask(Q, CW, kk), xc, 0.0)
                acc = acc + jnp.dot((cb * decay).astype(BF16), xk.astype(BF16), preferred_element_type=F32)
            y_ref[:, cs] = acc.astype(BF16)
        y2 = y_ref[...].astype(F32) * _silu(z_ref[...].astype(F32))
        rr = lax.rsqrt(jnp.mean(y2 * y2, axis=-1, keepdims=True) + RMS_EPS)
        yn_ref[...] = (y2 * rr * ng_ref[...]).astype(BF16)

    tile = pl.BlockSpec((Q, GP), lambda g, c: (c, g))
    return pl.pallas_call(
        body, grid=(G, nc),
        in_specs=_ssd_in_specs(Q, GP, N, Kh, DI, lambda c: c) + [tile, pl.BlockSpec((1, GP), lambda g, c: (0, g))],
        out_specs=[tile, pl.BlockSpec((None, None, GP, N), lambda g, c: (c, g, 0, 0)), tile],
        out_shape=[jax.ShapeDtypeStruct((L, DI), BF16), jax.ShapeDtypeStruct((nc, G, GP, N), F32),
                   jax.ShapeDtypeStruct((L, DI), BF16)],
        scratch_shapes=[pltpu.VMEM((GP, N), F32)], compiler_params=_cp(("parallel", "arbitrary")),
        name="ssd_fwd")(xbc, xbc, xbc, dtp_g, dtp_gT, bias_g, bias_gT, alog_g, alog_gT, dsk_g, dsk_gT, zx, norm_g)


def ssd_bwd(xbc, dtp_g, dtp_gT, bias_g, bias_gT, alog_g, alog_gT, dsk_g, dsk_gT, states, dyn, y, zx, norm_g, DI):
    L = xbc.shape[0]
    Q, P, N, G = SSD_Q, SSD_P, SSD_N, SSD_G
    GP = DI // G
    Kh = GP // P
    nc = L // Q
    rev = lambda c: nc - 1 - c

    CW, hpc, nch = _head_chunks(GP)

    def body(xs_ref, b_ref, c_ref, dtp_ref, dtpT_ref, bias_ref, biasT_ref, alog_ref, alogT_ref, dsk_ref, dskT_ref,
             st_ref, dyn_ref, y_ref, z_ref, ng_ref,
             dxs_ref, dB_ref, dC_ref, ddtp_ref, dbias_ref, dalog_ref, dD_ref, dz_ref, dng_ref, dstate):
        ci = pl.program_id(1)

        @pl.when(ci == 0)
        def _():
            dstate[...] = jnp.zeros(dstate.shape, F32)

        dt, A, causal, row, col, acum, acumT, Bm, Cm, cb = _ssd_common(
            dtp_ref, dtpT_ref, bias_ref, biasT_ref, alog_ref, alogT_ref, b_ref, c_ref)
        tn = (((0,), (0,)), ((), ()))
        nt = (((1,), (1,)), ((), ()))
        hd = _ssd_heads(dskT_ref, acum, acumT, dt, Kh)
        ET, tail = hd["ET"], hd["tail"]
        cbT = lax.dot_general(Bm, Cm, nt, preferred_element_type=F32)
        causalT = row <= col
        xs = xs_ref[...].astype(F32)
        xdt_all = xs * hd["dt_all"]
        yv = y_ref[...].astype(F32)
        zv = z_ref[...].astype(F32)
        dynv = dyn_ref[...].astype(F32)
        sz = _silu(zv)
        y2 = yv * sz
        rr = lax.rsqrt(jnp.mean(y2 * y2, axis=-1, keepdims=True) + RMS_EPS)
        yh = y2 * rr
        dyh = dynv * ng_ref[...]
        dy2 = rr * (dyh - yh * jnp.mean(dyh * yh, axis=-1, keepdims=True))
        dz_ref[...] = (dy2 * yv * _dsilu(zv)).astype(BF16)
        dng_v = jnp.sum(dynv * yh, axis=0, keepdims=True)
        dyb = (dy2 * sz).astype(BF16)
        dy_all = dyb.astype(F32)
        S_all = st_ref[...]
        S_b = S_all.astype(BF16)
        dS_all = dstate[...]
        dS_b = dS_all.astype(BF16)
        CS_all = lax.dot_general(Cm, S_b, nt, preferred_element_type=F32)
        dyE_b = (dy_all * hd["ea_all"]).astype(BF16)
        dC_acc = jnp.dot(dyE_b, S_b, preferred_element_type=F32)
        dS_y = lax.dot_general(dyE_b, Cm, tn, preferred_element_type=F32)
        BdS_all = lax.dot_general(Bm, dS_b, nt, preferred_element_type=F32)
        dB_acc = jnp.dot((xdt_all * hd["tail_all"]).astype(BF16), dS_b, preferred_element_type=F32)
        dtail = _headsum(xdt_all * BdS_all, ET)
        da_cols = _headsum(dy_all * CS_all * hd["ea_all"], ET) - dtail * tail
        dss = _dot01_left(jnp.ones((8, N), F32), _dot01_left(hd["E"], dS_all * S_all, 2), 2, dims=nt)
        da_last = dss[0:1] * jnp.exp(hd["a_last"]) + jnp.sum(dtail * tail, axis=0, keepdims=True)
        rowi = lax.broadcasted_iota(jnp.int32, (Q, Kh), 0)
        da_cols = da_cols + jnp.where(rowi == Q - 1, da_last, 0.0)
        dstate[...] = hd["eL_b"] * dS_all + dS_y
        sum_mg = jnp.zeros((Q, Q), F32)
        ddt_x = jnp.zeros((Q, Kh), F32)
        da_rows = jnp.zeros((Kh, Q), F32)
        lane_k = lax.broadcasted_iota(jnp.int32, (Q, Kh), 1)
        sub_k = lax.broadcasted_iota(jnp.int32, (Kh, Q), 0)
        for ch in range(nch):
            cs = slice(ch * CW, (ch + 1) * CW)
            dyc = dyb[:, cs]
            xc_b = xdt_all[:, cs].astype(BF16)
            acc = hd["tail_all"][:, cs] * BdS_all[:, cs]
            for kk in range(hpc):
                k = ch * hpc + kk
                a_b = jnp.broadcast_to(acum[:, k:k + 1], (Q, Q))
                a_r = acumT[k:k + 1, :]
                decay = jnp.exp(jnp.where(causal, a_b - a_r, -jnp.inf))
                decayT = jnp.exp(jnp.where(causalT, a_r - a_b, -jnp.inf))
                dyk = dyc if hpc == 1 else jnp.where(_head_mask(Q, CW, kk), dyc, jnp.zeros_like(dyc))
                mg = decay * lax.dot_general(dyk, xc_b, nt, preferred_element_type=F32)
                sum_mg = sum_mg + mg
                w = mg * cb
                da_cols = da_cols + jnp.where(lane_k == k, jnp.sum(w, axis=1, keepdims=True), 0.0)
                da_rows = da_rows + jnp.where(sub_k == k, jnp.sum(w, axis=0, keepdims=True), 0.0)
                acc = acc + jnp.dot((decayT * cbT).astype(BF16), dyk, preferred_element_type=F32)
            dxs_ref[:, cs] = (acc * hd["dt_all"][:, cs] + dy_all[:, cs] * hd["dsk_all"][:, cs]).astype(BF16)
            ddt_x = ddt_x + _headsum(acc * xs[:, cs], ET[cs, :])
        eye_q = (row == col).astype(F32)
        da_cols = da_cols - _dot01_left(eye_q, da_rows, 3, dims=nt)
        dD_row = jnp.sum(_headsum(dy_all * xs, ET), axis=0, keepdims=True)
        sum_mg_b = sum_mg.astype(BF16)
        dB_ref[...] = (dB_acc + lax.dot_general(sum_mg_b, Cm, tn, preferred_element_type=F32)).astype(BF16)
        dC_ref[...] = (dC_acc + jnp.dot(sum_mg_b, Bm, preferred_element_type=F32)).astype(BF16)
        triu = (row <= col).astype(F32)
        ddtA = _dot01_left(triu, da_cols, 3)
        ddt = ddt_x + ddtA * A
        dpre = ddt * _sigmoid(dtp_ref[...] + bias_ref[...])
        ddtp_ref[...] = dpre
        dbias_v = jnp.sum(dpre, axis=0, keepdims=True)
        dalog_v = jnp.sum(ddtA * dt, axis=0, keepdims=True) * A

        @pl.when(ci == 0)
        def _():
            dbias_ref[...] = dbias_v
            dalog_ref[...] = dalog_v
            dD_ref[...] = dD_row
            dng_ref[...] = dng_v

        @pl.when(ci > 0)
        def _():
            dbias_ref[...] += dbias_v
            dalog_ref[...] += dalog_v
            dD_ref[...] += dD_row
            dng_ref[...] += dng_v

    vec_o = pl.BlockSpec((None, 1, Kh), lambda g, c: (g, 0, 0))
    tile = pl.BlockSpec((Q, GP), lambda g, c: (rev(c), g))
    return pl.pallas_call(
        body, grid=(G, nc),
        in_specs=_ssd_in_specs(Q, GP, N, Kh, DI, rev)
        + [pl.BlockSpec((None, None, GP, N), lambda g, c: (rev(c), g, 0, 0)), tile, tile, tile,
           pl.BlockSpec((1, GP), lambda g, c: (0, g))],
        out_specs=[tile, pl.BlockSpec((Q, N), lambda g, c: (rev(c), g)), pl.BlockSpec((Q, N), lambda g, c: (rev(c), g)),
                   pl.BlockSpec((None, Q, Kh), lambda g, c: (g, rev(c), 0)), vec_o, vec_o, vec_o,
                   tile, pl.BlockSpec((1, GP), lambda g, c: (0, g))],
        out_shape=[jax.ShapeDtypeStruct((L, DI), BF16), jax.ShapeDtypeStruct((L, G * N), BF16),
                   jax.ShapeDtypeStruct((L, G * N), BF16), jax.ShapeDtypeStruct((G, L, Kh), F32)]
        + [jax.ShapeDtypeStruct((G, 1, Kh), F32)] * 3
        + [jax.ShapeDtypeStruct(zx.shape, BF16), jax.ShapeDtypeStruct((1, DI), F32)],
        scratch_shapes=[pltpu.VMEM((GP, N), F32)], compiler_params=_cp(("parallel", "arbitrary")),
        name="ssd_bwd")(xbc, xbc, xbc, dtp_g, dtp_gT, bias_g, bias_gT, alog_g, alog_gT, dsk_g, dsk_gT, states,
                        dyn, y, zx, norm_g)


def _alibi_slope(gi, h):
    n = len(DIL_PATTERNS) * DIL_H
    return float(2.0 ** (-8.0 * (gi * DIL_H + h + 1) / n))


def _attn_masks():
    qi = lax.broadcasted_iota(jnp.int32, (DIL_BLK, DIL_BLK), 0)
    kj = lax.broadcasted_iota(jnp.int32, (DIL_BLK, DIL_BLK), 1)
    dcur = (qi - kj).astype(F32)
    return dcur, qi >= kj, dcur + float(DIL_BLK), kj >= qi


def attn_fwd(q3, kv3, gi):
    window, d = DIL_PATTERNS[gi]
    assert window // d == DIL_BLK
    HW = DIL_H * DIL_E
    M = q3.shape[1]
    nb = M // DIL_BLK
    scale = DIL_E ** -0.5
    nt = (((1,), (1,)), ((), ()))

    def body(q_ref, kp_ref, kc_ref, vp_ref, vc_ref, o_ref, lse_ref):
        n = pl.program_id(1)
        dcur, vcur, dprev, vprev0 = _attn_masks()
        dist = jnp.concatenate([dprev, dcur], axis=1)
        valid = jnp.concatenate([jnp.logical_and(vprev0, n > 0), vcur], axis=1)
        lane = lax.broadcasted_iota(jnp.int32, (DIL_BLK, 128), 1)
        lse_acc = jnp.zeros((DIL_BLK, 128), F32)
        for h in range(DIL_H):
            hs = slice(h * DIL_E, (h + 1) * DIL_E)
            sl = _alibi_slope(gi, h) * d
            kcat = jnp.concatenate([kp_ref[:, hs], kc_ref[:, hs]], axis=0)
            vcat = jnp.concatenate([vp_ref[:, hs], vc_ref[:, hs]], axis=0)
            s = lax.dot_general(q_ref[:, hs], kcat, nt, preferred_element_type=F32) * scale - sl * dist
            s = jnp.where(valid, s, -jnp.inf)
            m = jnp.max(s, axis=-1, keepdims=True)
            p = jnp.exp(s - m)
            den = jnp.sum(p, axis=-1, keepdims=True)
            o = jnp.dot(p.astype(BF16), vcat, preferred_element_type=F32) / den
            o_ref[:, hs] = o.astype(BF16)
            lse_acc = jnp.where(lane == h, m + jnp.log(den), lse_acc)
        lse_ref[...] = lse_acc

    blk = (None, DIL_BLK, HW)
    prev = lambda n: jnp.maximum(n - 1, 0)
    return pl.pallas_call(
        body, grid=(d, nb),
        in_specs=[pl.BlockSpec(blk, lambda r, n: (r, n, 0)),
                  pl.BlockSpec(blk, lambda r, n: (r, prev(n), 0)), pl.BlockSpec(blk, lambda r, n: (r, n, 0)),
                  pl.BlockSpec(blk, lambda r, n: (r, prev(n), 1)), pl.BlockSpec(blk, lambda r, n: (r, n, 1))],
        out_specs=[pl.BlockSpec(blk, lambda r, n: (r, n, 0)), pl.BlockSpec((None, DIL_BLK, 128), lambda r, n: (r, n, 0))],
        out_shape=[jax.ShapeDtypeStruct((d, M, HW), BF16), jax.ShapeDtypeStruct((d, M, 128), F32)],
        compiler_params=_cp(("parallel", "parallel")), name=f"attn_fwd_{gi}")(q3, kv3, kv3, kv3, kv3)


def attn_bwd(q3, kv3, do3, lse3, dpr3, gi):
    window, d = DIL_PATTERNS[gi]
    HW = DIL_H * DIL_E
    M = q3.shape[1]
    L = M * d
    nb = M // DIL_BLK
    scale = DIL_E ** -0.5
    nt = (((1,), (1,)), ((), ()))
    tn = (((0,), (0,)), ((), ()))

    def body(q0_ref, q1_ref, k_ref, v_ref, do0_ref, do1_ref, l0_ref, l1_ref, r0_ref, r1_ref,
             dq_ref, dk_ref, dv_ref, carry):
        n = pl.program_id(1)

        @pl.when(n == 0)
        def _():
            carry[...] = jnp.zeros(carry.shape, F32)

        dcur, vcur, dprev, vprev0 = _attn_masks()
        dist = jnp.concatenate([dcur, dprev], axis=0)
        valid = jnp.concatenate([vcur, jnp.logical_and(vprev0, n < nb - 1)], axis=0)
        B = DIL_BLK
        for h in range(DIL_H):
            hs = slice(h * DIL_E, (h + 1) * DIL_E)
            sl = _alibi_slope(gi, h) * d
            kh = k_ref[:, hs]
            vh = v_ref[:, hs]
            qcat = jnp.concatenate([q0_ref[:, hs], q1_ref[:, hs]], axis=0)
            docat = jnp.concatenate([do0_ref[:, hs], do1_ref[:, hs]], axis=0)
            lcat = jnp.concatenate([l0_ref[:, h:h + 1], l1_ref[:, h:h + 1]], axis=0)
            rcat = jnp.concatenate([r0_ref[:, h:h + 1], r1_ref[:, h:h + 1]], axis=0)
            s = lax.dot_general(qcat, kh, nt, preferred_element_type=F32) * scale - sl * dist
            p = jnp.exp(jnp.where(valid, s - lcat, -jnp.inf))
            ds = p * (lax.dot_general(docat, vh, nt, preferred_element_type=F32) - rcat)
            ds_b = (ds * scale).astype(BF16)
            dv_ref[:, hs] = lax.dot_general(p.astype(BF16), docat, tn, preferred_element_type=F32).astype(BF16)
            dk_ref[:, hs] = lax.dot_general(ds_b, qcat, tn, preferred_element_type=F32).astype(BF16)
            dqc = jnp.dot(ds_b, kh, preferred_element_type=F32)
            dq_ref[:, hs] = (carry[:, hs] + dqc[:B]).astype(BF16)
            carry[:, hs] = dqc[B:]

    blk = (None, DIL_BLK, HW)
    sblk = (None, DIL_BLK, 128)
    oblk = (DIL_BLK, HW)
    nxt = lambda n: jnp.minimum(n + 1, nb - 1)
    here = lambda c: (lambda r, n: (r, n, c))
    ahead = lambda c: (lambda r, n: (r, nxt(n), c))
    outs = pl.pallas_call(
        body, grid=(d, nb),
        in_specs=[pl.BlockSpec(blk, here(0)), pl.BlockSpec(blk, ahead(0)),
                  pl.BlockSpec(blk, here(0)), pl.BlockSpec(blk, here(1)),
                  pl.BlockSpec(blk, here(0)), pl.BlockSpec(blk, ahead(0)),
                  pl.BlockSpec(sblk, here(0)), pl.BlockSpec(sblk, ahead(0)),
                  pl.BlockSpec(sblk, here(0)), pl.BlockSpec(sblk, ahead(0))],
        out_specs=[pl.BlockSpec(oblk, lambda r, n: (n, r))] * 3,
        out_shape=[jax.ShapeDtypeStruct((M, d * HW), BF16)] * 3,
        scratch_shapes=[pltpu.VMEM(oblk, F32)], compiler_params=_cp(("parallel", "arbitrary")),
        name=f"attn_bwd_{gi}")(q3, q3, kv3, kv3, do3, do3, lse3, lse3, dpr3, dpr3)
    return [t.reshape(L, HW) for t in outs]


def _merge_weights(l_tiles, h):
    ls = [t[:, h:h + 1] for t in l_tiles]
    mx = functools.reduce(jnp.maximum, ls)
    es = [jnp.exp(l - mx) for l in ls]
    den = functools.reduce(lambda a, b: a + b, es)
    return [e / den for e in es]


def _dil_specs(tr, arrs):
    return [pl.BlockSpec((a.shape[0], tr // a.shape[0], a.shape[2]), lambda i: (0, i, 0)) for a in arrs]


def _dil_scratch(tr, arrs):
    return [pltpu.VMEM((a.shape[2] // 128, tr, 128), F32) for a in arrs if a.shape[0] > 1]


def _undilate(refs3, scrs, tr):
    out, k = [], 0
    for ref in refs3:
        d, _, W = ref.shape
        if d == 1:
            out.append(lambda c, ref=ref: ref[0, :, c * 128:(c + 1) * 128])
            continue
        scr = scrs[k]
        k += 1
        for r in range(d):
            for c in range(W // 128):
                scr.at[c][pl.ds(r, tr // d, stride=d), :] = ref[r, :, c * 128:(c + 1) * 128].astype(F32)
        out.append(lambda c, scr=scr: scr[c])
    return out


def merge_fwd(os3, lses3, z):
    HW = os3[0].shape[2]
    L = os3[0].shape[0] * os3[0].shape[1]
    tr = _tile(L, 256, 16)
    ng = len(os3)
    n_scr = len(_dil_scratch(tr, os3))

    def body(*refs):
        z_ref, out_ref = refs[2 * ng], refs[2 * ng + 1]
        scrs = refs[2 * ng + 2:]
        o_get = _undilate(refs[:ng], scrs[:n_scr], tr)
        l_tiles = [g(0) for g in _undilate(refs[ng:2 * ng], scrs[n_scr:], tr)]
        for h in range(DIL_H):
            hs = slice(h * DIL_E, (h + 1) * DIL_E)
            ws = _merge_weights(l_tiles, h)
            om = functools.reduce(lambda a, b: a + b, [w * o(h).astype(F32) for w, o in zip(ws, o_get)])
            out_ref[:, hs] = (om * _silu(z_ref[:, hs].astype(F32))).astype(BF16)

    return pl.pallas_call(
        body, grid=(L // tr,),
        in_specs=_dil_specs(tr, os3) + _dil_specs(tr, lses3) + _row_specs(tr, [HW]),
        out_specs=_row_specs(tr, [HW])[0], out_shape=jax.ShapeDtypeStruct((L, HW), BF16),
        scratch_shapes=_dil_scratch(tr, os3) + _dil_scratch(tr, lses3),
        compiler_params=_cp(("parallel",)), name="merge_fwd")(*os3, *lses3, z)


def merge_bwd(dgated, os3, lses3, z):
    HW = os3[0].shape[2]
    L = os3[0].shape[0] * os3[0].shape[1]
    tr = _tile(L, 256, 16)
    ng = len(os3)
    n_scr = len(_dil_scratch(tr, os3))

    def body(*refs):
        dg_ref = refs[0]
        z_ref = refs[1 + 2 * ng]
        outs = refs[2 + 2 * ng:2 + 2 * ng + 2 * ng + 1]
        scrs = refs[2 + 2 * ng + 2 * ng + 1:]
        do_out, dpr_out, dz_ref = outs[:ng], outs[ng:2 * ng], outs[2 * ng]
        o_get = _undilate(refs[1:1 + ng], scrs[:n_scr], tr)
        l_tiles = [g(0) for g in _undilate(refs[1 + ng:1 + 2 * ng], scrs[n_scr:2 * n_scr], tr)]
        stage = scrs[2 * n_scr:]
        do_stage, dpr_stage, k = [], [], 0
        for g in range(ng):
            if do_out[g].shape[0] == 1:
                do_stage.append(None)
                dpr_stage.append(None)
            else:
                do_stage.append(stage[2 * k])
                dpr_stage.append(stage[2 * k + 1])
                k += 1
        lane = lax.broadcasted_iota(jnp.int32, (tr, 128), 1)
        accs = [jnp.zeros((tr, 128), F32) for _ in range(ng)]
        for h in range(DIL_H):
            hs = slice(h * DIL_E, (h + 1) * DIL_E)
            ws = _merge_weights(l_tiles, h)
            ov = [o(h).astype(F32) for o in o_get]
            om = functools.reduce(lambda a, b: a + b, [w * o for w, o in zip(ws, ov)])
            zv = z_ref[:, hs].astype(F32)
            dgv = dg_ref[:, hs].astype(F32)
            dom = dgv * _silu(zv)
            dz_ref[:, hs] = (dgv * om * _dsilu(zv)).astype(BF16)
            dws = [jnp.sum(dom * o, axis=-1, keepdims=True) for o in ov]
            dwbar = functools.reduce(lambda a, b: a + b, [w * dw for w, dw in zip(ws, dws)])
            for g in range(ng):
                if do_stage[g] is None:
                    do_out[g][0, :, hs] = (ws[g] * dom).astype(BF16)
                else:
                    do_stage[g][h] = ws[g] * dom
                accs[g] = jnp.where(lane == h, ws[g] * dwbar, accs[g])
        for g in range(ng):
            d = do_out[g].shape[0]
            if d == 1:
                dpr_out[g][0] = accs[g]
                continue
            dpr_stage[g][0] = accs[g]
            for r in range(d):
                dpr_out[g][r] = dpr_stage[g].at[0][pl.ds(r, tr // d, stride=d), :]
                for c in range(HW // 128):
                    do_out[g][r, :, c * 128:(c + 1) * 128] = do_stage[g].at[c][pl.ds(r, tr // d, stride=d), :].astype(BF16)

    stage_shapes = []
    for o3 in os3:
        if o3.shape[0] > 1:
            stage_shapes += [pltpu.VMEM((HW // 128, tr, 128), F32), pltpu.VMEM((1, tr, 128), F32)]
    outs = pl.pallas_call(
        body, grid=(L // tr,),
        in_specs=_row_specs(tr, [HW]) + _dil_specs(tr, os3) + _dil_specs(tr, lses3) + _row_specs(tr, [HW]),
        out_specs=_dil_specs(tr, os3) + _dil_specs(tr, lses3) + _row_specs(tr, [HW]),
        out_shape=[jax.ShapeDtypeStruct(o.shape, BF16) for o in os3] + [jax.ShapeDtypeStruct(l.shape, F32) for l in lses3]
        + [jax.ShapeDtypeStruct((L, HW), BF16)],
        scratch_shapes=_dil_scratch(tr, os3) + _dil_scratch(tr, lses3) + stage_shapes,
        compiler_params=_cp(("parallel",)), name="merge_bwd")(dgated, *os3, *lses3, z)
    return outs[:ng], outs[ng:2 * ng], outs[2 * ng]


def ada_fwd(c8, ada_w):
    nl, D, Ws = ada_w.shape
    tn = _tile(Ws, 512)

    def body(c_ref, w_ref, o_ref):
        o_ref[...] = jnp.dot(_silu(c_ref[...]), w_ref[...], precision=lax.Precision.HIGHEST,
                             preferred_element_type=F32)

    return pl.pallas_call(
        body, grid=(nl, Ws // tn),
        in_specs=[pl.BlockSpec((N_DEV, D), lambda l, j: (0, 0)), pl.BlockSpec((None, D, tn), lambda l, j: (l, 0, j))],
        out_specs=pl.BlockSpec((None, N_DEV, tn), lambda l, j: (l, 0, j)),
        out_shape=jax.ShapeDtypeStruct((nl, N_DEV, Ws), F32), compiler_params=_cp(("parallel", "parallel")),
        name="ada_fwd")(c8, ada_w)


def ada_wgrad(c8t, dmod):
    nl, _, Ws = dmod.shape
    D = c8t.shape[0]
    tm = _tile(D, 512, 8)

    def body(c_ref, d_ref, o_ref):
        sc = _silu(c_ref[...])
        acc = sc[:, 0:1] * d_ref[0:1, :]
        for e in range(1, N_DEV):
            acc = acc + sc[:, e:e + 1] * d_ref[e:e + 1, :]
        o_ref[...] = acc

    return pl.pallas_call(
        body, grid=(nl, D // tm),
        in_specs=[pl.BlockSpec((tm, N_DEV), lambda l, i: (i, 0)), pl.BlockSpec((None, N_DEV, Ws), lambda l, i: (l, 0, 0))],
        out_specs=pl.BlockSpec((None, tm, Ws), lambda l, i: (l, i, 0)),
        out_shape=jax.ShapeDtypeStruct((nl, D, Ws), F32), compiler_params=_cp(("parallel", "parallel")),
        name="ada_wgrad")(c8t, dmod)


def adamw(w, g, m, v, name):
    R, C = w.shape
    tr = _tile(R, 256, 8)
    c1 = 1.0 - ADAM_B1 ** ADAM_STEP
    c2 = 1.0 - ADAM_B2 ** ADAM_STEP

    def body(w_ref, g_ref, m_ref, v_ref, d_ref, nm_ref, nv_ref):
        gv = g_ref[...]
        nm = ADAM_B1 * m_ref[...] + (1.0 - ADAM_B1) * gv
        nv = ADAM_B2 * v_ref[...] + (1.0 - ADAM_B2) * (gv * gv)
        nm_ref[...] = nm
        nv_ref[...] = nv
        d_ref[...] = -ADAM_LR * ((nm / c1) / (jnp.sqrt(nv / c2) + ADAM_EPS) + ADAM_WD * w_ref[...])

    return pl.pallas_call(
        body, grid=(R // tr,), in_specs=_row_specs(tr, [C] * 4), out_specs=_row_specs(tr, [C] * 3),
        out_shape=[jax.ShapeDtypeStruct((R, C), F32)] * 3, compiler_params=_cp(("parallel",)), name=name)(w, g, m, v)


def sum_leading(t, name, out_dtype=F32):
    S, R, C = t.shape
    tr = _tile(R, 256, 16)

    def body(t_ref, o_ref):
        acc = t_ref[0].astype(F32)
        for s in range(1, S):
            acc = acc + t_ref[s].astype(F32)
        o_ref[...] = acc.astype(out_dtype)

    return pl.pallas_call(
        body, grid=(R // tr,), in_specs=[pl.BlockSpec((S, tr, C), lambda i: (0, i, 0))],
        out_specs=pl.BlockSpec((tr, C), lambda i: (i, 0)), out_shape=jax.ShapeDtypeStruct((R, C), out_dtype),
        compiler_params=_cp(("parallel",)), name=name)(t)


def add_half(g, a, core, name, by_cols=False):
    S, R, C = g.shape

    def body(core_ref, g_ref, a_ref, o_ref):
        o_ref[...] = (g_ref[...].astype(F32) + a_ref[...].astype(F32)).astype(BF16)

    if by_cols:
        hc = C // 2
        tr = _tile(R, 512, 16)
        return pl.pallas_call(
            body,
            grid_spec=pltpu.PrefetchScalarGridSpec(
                num_scalar_prefetch=1, grid=(S, R // tr),
                in_specs=[pl.BlockSpec((None, tr, hc), lambda s, i, core_ref: (s, i, core_ref[0])),
                          pl.BlockSpec((None, tr, hc), lambda s, i, core_ref: (s, i, 0))],
                out_specs=pl.BlockSpec((None, tr, hc), lambda s, i, core_ref: (s, i, 0))),
            out_shape=jax.ShapeDtypeStruct((S, R, hc), BF16), compiler_params=_cp(("parallel", "parallel")),
            name=name)(core, g, a)
    h = R // 2
    tr = _tile(h, 256, 16)
    nb = h // tr

    return pl.pallas_call(
        body,
        grid_spec=pltpu.PrefetchScalarGridSpec(
            num_scalar_prefetch=1, grid=(S, nb),
            in_specs=[pl.BlockSpec((None, tr, C), lambda s, i, core_ref: (s, core_ref[0] * nb + i, 0)),
                      pl.BlockSpec((None, tr, C), lambda s, i, core_ref: (s, i, 0))],
            out_specs=pl.BlockSpec((None, tr, C), lambda s, i, core_ref: (s, i, 0))),
        out_shape=jax.ShapeDtypeStruct((S, h, C), BF16), compiler_params=_cp(("parallel", "parallel")),
        name=name)(core, g, a)


def sum_partials(own, landed, chip, name):
    _, h, C = own.shape
    tr = _tile(h, 512, 16)

    def body(chip_ref, own_ref, l_ref, o_ref):
        acc = own_ref[...].astype(F32)
        for j in range(3):
            acc = acc + l_ref[j].astype(F32)
        o_ref[...] = acc

    return pl.pallas_call(
        body,
        grid_spec=pltpu.PrefetchScalarGridSpec(
            num_scalar_prefetch=1, grid=(h // tr,),
            in_specs=[pl.BlockSpec((None, tr, C), lambda i, chip_ref: (chip_ref[0], i, 0)),
                      pl.BlockSpec((3, tr, C), lambda i, chip_ref: (0, i, 0))],
            out_specs=pl.BlockSpec((tr, C), lambda i, chip_ref: (i, 0))),
        out_shape=jax.ShapeDtypeStruct((h, C), F32), compiler_params=_cp(("parallel",)), name=name)(chip, own, landed)


def adamw_halves(w, g_mine, g_theirs, m, v, core, name, after=None):
    R, C = w.shape
    h = R // 2
    tr = _tile(h, 256, 8)
    nbh = h // tr
    c1 = 1.0 - ADAM_B1 ** ADAM_STEP
    c2 = 1.0 - ADAM_B2 ** ADAM_STEP
    extra = [] if after is None else [after]

    def body(core_ref, w_ref, gm_ref, gt_ref, m_ref, v_ref, *rest):
        g_ref, d_ref, nm_ref, nv_ref = rest[len(extra):]
        mine = (pl.program_id(0) // nbh) == core_ref[0]
        gv = jnp.where(mine, gm_ref[...], gt_ref[...])
        g_ref[...] = gv
        nm = ADAM_B1 * m_ref[...] + (1.0 - ADAM_B1) * gv
        nv = ADAM_B2 * v_ref[...] + (1.0 - ADAM_B2) * (gv * gv)
        nm_ref[...] = nm
        nv_ref[...] = nv
        d_ref[...] = -ADAM_LR * ((nm / c1) / (jnp.sqrt(nv / c2) + ADAM_EPS) + ADAM_WD * w_ref[...])

    full = pl.BlockSpec((tr, C), lambda i, core_ref: (i, 0))
    halfspec = pl.BlockSpec((tr, C), lambda i, core_ref: (i % nbh, 0))
    return pl.pallas_call(
        body,
        grid_spec=pltpu.PrefetchScalarGridSpec(
            num_scalar_prefetch=1, grid=(2 * nbh,),
            in_specs=[full, halfspec, halfspec, full, full] + [_ANY] * len(extra), out_specs=[full] * 4),
        out_shape=[jax.ShapeDtypeStruct((R, C), F32)] * 4, compiler_params=_cp(("parallel",)),
        name=name)(core, w, g_mine, g_theirs, m, v, *extra)


_ANY = pl.BlockSpec(memory_space=pl.ANY)


def _place():
    x, y, c = lax.axis_index("x"), lax.axis_index("y"), lax.axis_index("c")
    chips = [(1 - x, y), (x, 1 - y), (1 - x, 1 - y)]
    return x, y, c, chips


def allgather_small(v, name):
    R, W = v.shape

    def body(x_ref, out_ref, send_sems, recv_sems, local_sem):
        x, y, c, chips = _place()
        me, sibling = (x, y, c), (x, y, 1 - c)

        def rows(px, py, pc):
            return out_ref.at[pl.ds((4 * px + 2 * py + pc) * R, R), :]

        def copy(k, block, to, src=None):
            return pltpu.make_async_remote_copy(
                src_ref=rows(*block) if src is None else src, dst_ref=rows(*block),
                send_sem=send_sems.at[k], recv_sem=recv_sems.at[k], device_id=to, device_id_type=MESH)

        mine = pltpu.make_async_copy(x_ref, rows(*me), local_sem)
        mine.start()
        first = [copy(0, me, sibling, src=x_ref)]
        first += [copy(1 + j, me, (*chip, c), src=x_ref) for j, chip in enumerate(chips)]
        for cp in first:
            cp.start()
        passed = [copy(4 + j, (*chip, c), sibling) for j, chip in enumerate(chips)]
        for j, chip in enumerate(chips):
            copy(1 + j, (*chip, c), me).wait_recv()
            passed[j].start()
        copy(0, sibling, me).wait_recv()
        for j, chip in enumerate(chips):
            copy(4 + j, (*chip, 1 - c), me).wait_recv()
        for cp in first + passed:
            cp.wait_send()
        mine.wait()

    return pl.pallas_call(
        body, out_shape=jax.ShapeDtypeStruct((N_DEV * R, W), v.dtype),
        in_specs=[pl.BlockSpec(memory_space=pltpu.VMEM)], out_specs=pl.BlockSpec(memory_space=pltpu.VMEM),
        scratch_shapes=[pltpu.SemaphoreType.DMA((7,)), pltpu.SemaphoreType.DMA((7,)), pltpu.SemaphoreType.DMA],
        name=name)(v)


def allgather_routed(shard, name):
    R, C = shard.shape
    hc = C // 2
    ra = (R // 2) // 16 * 16

    def body(in_ref, out_ref, send_sems, recv_sems):
        x, y, c, _ = _place()
        xn, yn = (1 - x, y, c), (x, 1 - y, c)
        sibling = (x, y, 1 - c)
        p, pxn, pyn, pdg = 2 * x + y, 2 * (1 - x) + y, 2 * x + (1 - y), 2 * (1 - x) + (1 - y)
        rows_a, rows_b, rows_all = pl.ds(0, ra), pl.ds(ra, R - ra), pl.ds(0, R)

        def win(ref, rows, core):
            return ref.at[rows, pl.ds(pl.multiple_of(core * hc, 128), hc)]

        def copy(k, chip_id, rows, core, to, src=None):
            blk = win(out_ref.at[chip_id], rows, core)
            return pltpu.make_async_remote_copy(
                src_ref=blk if src is None else src, dst_ref=blk, send_sem=send_sems.at[k], recv_sem=recv_sems.at[k],
                device_id=to, device_id_type=MESH)

        own = [copy(0, p, rows_a, c, xn, src=win(in_ref, rows_a, c)), copy(1, p, rows_b, c, xn, src=win(in_ref, rows_b, c)),
               copy(2, p, rows_b, c, yn, src=win(in_ref, rows_b, c)), copy(3, p, rows_a, c, yn, src=win(in_ref, rows_a, c))]
        for cp in own:
            cp.start()
        copy(0, pxn, rows_a, c, xn).wait_recv()
        fwd_a = copy(4, pxn, rows_a, c, yn)
        fwd_a.start()
        copy(2, pyn, rows_b, c, yn).wait_recv()
        fwd_b = copy(5, pyn, rows_b, c, xn)
        fwd_b.start()
        copy(1, pxn, rows_b, c, xn).wait_recv()
        copy(3, pyn, rows_a, c, yn).wait_recv()
        passed = [copy(6, pxn, rows_all, c, sibling), copy(7, pyn, rows_all, c, sibling)]
        for cp in passed:
            cp.start()
        copy(4, pdg, rows_a, c, yn).wait_recv()
        passed.append(copy(8, pdg, rows_a, c, sibling))
        passed[-1].start()
        copy(5, pdg, rows_b, c, xn).wait_recv()
        passed.append(copy(9, pdg, rows_b, c, sibling))
        passed[-1].start()
        for k, (chip_id, rows) in enumerate([(pxn, rows_all), (pyn, rows_all), (pdg, rows_a), (pdg, rows_b)]):
            copy(6 + k, chip_id, rows, 1 - c, sibling).wait_recv()
        for cp in own + [fwd_a, fwd_b] + passed:
            cp.wait_send()

    out = pl.pallas_call(
        body, out_shape=jax.ShapeDtypeStruct((N_CHIPS, R, C), shard.dtype), in_specs=[_ANY], out_specs=_ANY,
        scratch_shapes=[pltpu.SemaphoreType.DMA((10,)), pltpu.SemaphoreType.DMA((10,))], name=name)(shard)
    chip = 2 * lax.axis_index("x") + lax.axis_index("y")
    return lax.dynamic_update_index_in_dim(out, shard, chip, 0)


_HBM = pl.BlockSpec(memory_space=pltpu.HBM)
_SEM = pl.BlockSpec(memory_space=pltpu.SEMAPHORE)
_EFFECT = pltpu.SideEffectType.DATAFLOW_SIDE_EFFECTING


def _chip_copies(kind, srcs, lands, send_sems, recv_sems):
    x, y, c, chips = _place()
    p = 2 * x + y
    cps = []
    if kind == "join":
        return [pltpu.make_async_remote_copy(
            src_ref=srcs[i], dst_ref=lands[i], send_sem=send_sems.at[3 * i], recv_sem=recv_sems.at[3 * i],
            device_id=(x, y, 1 - c), device_id_type=MESH) for i in range(len(srcs))]
    if kind == "sibling":
        for i in range(len(srcs)):
            h = srcs[i].shape[1] // 2
            cps.append(pltpu.make_async_remote_copy(
                src_ref=srcs[i].at[:, pl.ds((1 - c) * h, h), :], dst_ref=lands[i], send_sem=send_sems.at[3 * i],
                recv_sem=recv_sems.at[3 * i], device_id=(x, y, 1 - c), device_id_type=MESH))
        return cps
    for i in range(len(srcs)):
        for j, (cx, cy) in enumerate(chips):
            if kind == "gather":
                src, dst = srcs[i].at[c], lands[i].at[p, c]
            else:
                src, dst = srcs[i].at[2 * cx + cy], lands[i].at[j]
            cps.append(pltpu.make_async_remote_copy(
                src_ref=src, dst_ref=dst, send_sem=send_sems.at[3 * i + j], recv_sem=recv_sems.at[3 * i + j],
                device_id=(cx, cy, c), device_id_type=MESH))
    return cps


def split_start(kind, srcs, land_shapes, after, name, land_dtype=BF16):
    n = len(srcs)

    def body(*refs):
        src_refs, land_refs = refs[:n], refs[n:2 * n]
        send_sems, recv_sems = refs[2 * n + 1], refs[2 * n + 2]
        token = refs[-1]
        for cp in _chip_copies(kind, src_refs, land_refs, send_sems, recv_sems):
            cp.start()
        token[...] = jnp.zeros_like(token)

    lands = [pltpu.with_memory_space_constraint(lax.empty(s, land_dtype), pltpu.HBM) for s in land_shapes]
    outs = pl.pallas_call(
        body, name=name,
        out_shape=(pltpu.SemaphoreType.DMA((3 * n,)), pltpu.SemaphoreType.DMA((3 * n,)),
                   *[pltpu.HBM(s.shape, s.dtype) for s in srcs], *[pltpu.HBM(s, land_dtype) for s in land_shapes],
                   jax.ShapeDtypeStruct((8, 128), F32)),
        in_specs=[_HBM] * (2 * n) + [_ANY],
        out_specs=(_SEM, _SEM, *([_HBM] * (2 * n)), pl.BlockSpec(memory_space=pltpu.VMEM)),
        input_output_aliases={i: 2 + i for i in range(2 * n)},
        compiler_params=pltpu.CompilerParams(has_side_effects=_EFFECT),
    )(*[pltpu.with_memory_space_constraint(s, pltpu.HBM) for s in srcs], *lands, after)
    return outs[0], outs[1], outs[2:2 + n], outs[2 + n:2 + 2 * n], outs[-1]


def split_wait(kind, send_sems, recv_sems, srcs, lands, after, name):
    n = len(srcs)

    def body(*refs):
        src_refs, land_refs = refs[:n], refs[n:2 * n]
        ssem, rsem = refs[2 * n], refs[2 * n + 1]
        for cp in _chip_copies(kind, src_refs, land_refs, ssem, rsem):
            cp.wait_send()
            cp.wait_recv()

    outs = pl.pallas_call(
        body, name=name,
        out_shape=[pltpu.HBM(s.shape, s.dtype) for s in srcs] + [pltpu.HBM(s.shape, s.dtype) for s in lands],
        in_specs=[_HBM] * (2 * n) + [_SEM, _SEM, _ANY], out_specs=[_HBM] * (2 * n),
        input_output_aliases={i: i for i in range(2 * n)},
        compiler_params=pltpu.CompilerParams(has_side_effects=_EFFECT),
    )(*srcs, *lands, send_sems, recv_sems, after)
    return outs[:n], outs[n:]


def pass_to_sibling(lands):
    n = len(lands)

    def body(*refs):
        ins, outs = refs[:n], refs[n:2 * n]
        send_sems, recv_sems = refs[2 * n:]
        x, y, c, chips = _place()
        cps = []
        for i in range(n):
            for j, (cx, cy) in enumerate(chips):
                blk = outs[i].at[2 * cx + cy, c]
                cps.append(pltpu.make_async_remote_copy(
                    src_ref=ins[i].at[2 * cx + cy, c], dst_ref=blk, send_sem=send_sems.at[3 * i + j],
                    recv_sem=recv_sems.at[3 * i + j], device_id=(x, y, 1 - c), device_id_type=MESH))
        for cp in cps:
            cp.start()
        for cp in cps:
            cp.wait()

    return pl.pallas_call(
        body, out_shape=[jax.ShapeDtypeStruct(t.shape, t.dtype) for t in lands], in_specs=[_ANY] * n,
        out_specs=[_ANY] * n, input_output_aliases={i: i for i in range(n)},
        scratch_shapes=[pltpu.SemaphoreType.DMA((3 * n,)), pltpu.SemaphoreType.DMA((3 * n,))],
        name="ag_pass_to_sibling")(*lands)


def _all8_copies(src, land, send_sems, recv_sems):
    x, y, c, _ = _place()
    me = 4 * x + 2 * y + c
    cps = []
    for k, (fx, fy, fc) in enumerate([(a, b, d) for a in (0, 1) for b in (0, 1) for d in (0, 1)][1:]):
        peer = (1 - x if fx else x, 1 - y if fy else y, 1 - c if fc else c)
        cps.append(pltpu.make_async_remote_copy(
            src_ref=src, dst_ref=land.at[me], send_sem=send_sems.at[k], recv_sem=recv_sems.at[k],
            device_id=peer, device_id_type=MESH))
    return cps


def gather8_start(v, after, name):
    R, W = v.shape

    def body(v_ref, land_ref, aft_ref, send_sems, recv_sems, v_thru, land_thru, token):
        for cp in _all8_copies(v_ref, land_ref, send_sems, recv_sems):
            cp.start()
        token[...] = jnp.zeros_like(token)

    land = pltpu.with_memory_space_constraint(lax.empty((N_DEV, R, W), v.dtype), pltpu.HBM)
    outs = pl.pallas_call(
        body, name=name,
        out_shape=(pltpu.SemaphoreType.DMA((N_DEV - 1,)), pltpu.SemaphoreType.DMA((N_DEV - 1,)),
                   pltpu.HBM(v.shape, v.dtype), pltpu.HBM((N_DEV, R, W), v.dtype), jax.ShapeDtypeStruct((8, 128), F32)),
        in_specs=[_HBM, _HBM, _ANY],
        out_specs=(_SEM, _SEM, _HBM, _HBM, pl.BlockSpec(memory_space=pltpu.VMEM)),
        input_output_aliases={0: 2, 1: 3},
        compiler_params=pltpu.CompilerParams(has_side_effects=_EFFECT),
    )(pltpu.with_memory_space_constraint(v, pltpu.HBM), land, after)
    return outs


def gather8_wait(send_sems, recv_sems, v, land, after, name):
    def body(v_ref, land_ref, ssem, rsem, aft_ref, v_dead, land_out):
        for cp in _all8_copies(v_ref, land_ref, ssem, rsem):
            cp.wait_send()
            cp.wait_recv()

    return pl.pallas_call(
        body, name=name, out_shape=[pltpu.HBM(v.shape, v.dtype), pltpu.HBM(land.shape, land.dtype)],
        in_specs=[_HBM, _HBM, _SEM, _SEM, _ANY], out_specs=[_HBM, _HBM], input_output_aliases={0: 0, 1: 1},
        compiler_params=pltpu.CompilerParams(has_side_effects=_EFFECT),
    )(v, land, send_sems, recv_sems, after)[1]


def _pass_copies(bufs, send_sems, recv_sems):
    x, y, c, chips = _place()
    cps = []
    for i in range(len(bufs)):
        for j, (cx, cy) in enumerate(chips):
            blk = bufs[i].at[2 * cx + cy, c]
            cps.append(pltpu.make_async_remote_copy(
                src_ref=blk, dst_ref=blk, send_sem=send_sems.at[3 * i + j], recv_sem=recv_sems.at[3 * i + j],
                device_id=(x, y, 1 - c), device_id_type=MESH))
    return cps


def pass_start(bufs, after, name):
    n = len(bufs)

    def body(*refs):
        send_sems, recv_sems = refs[n + 1], refs[n + 2]
        for cp in _pass_copies(refs[:n], send_sems, recv_sems):
            cp.start()
        refs[-1][...] = jnp.zeros_like(refs[-1])

    outs = pl.pallas_call(
        body, name=name,
        out_shape=(pltpu.SemaphoreType.DMA((3 * n,)), pltpu.SemaphoreType.DMA((3 * n,)),
                   *[pltpu.HBM(b.shape, b.dtype) for b in bufs], jax.ShapeDtypeStruct((8, 128), F32)),
        in_specs=[_HBM] * n + [_ANY],
        out_specs=(_SEM, _SEM, *([_HBM] * n), pl.BlockSpec(memory_space=pltpu.VMEM)),
        input_output_aliases={i: 2 + i for i in range(n)},
        compiler_params=pltpu.CompilerParams(has_side_effects=_EFFECT),
    )(*[pltpu.with_memory_space_constraint(b, pltpu.HBM) for b in bufs], after)
    return outs[0], outs[1], outs[2:2 + n], outs[-1]


def pass_wait(send_sems, recv_sems, bufs, after, name):
    n = len(bufs)

    def body(*refs):
        for cp in _pass_copies(refs[:n], refs[n], refs[n + 1]):
            cp.wait_send()
            cp.wait_recv()

    return pl.pallas_call(
        body, name=name, out_shape=[pltpu.HBM(b.shape, b.dtype) for b in bufs],
        in_specs=[_HBM] * n + [_SEM, _SEM, _ANY], out_specs=[_HBM] * n,
        input_output_aliases={i: i for i in range(n)},
        compiler_params=pltpu.CompilerParams(has_side_effects=_EFFECT),
    )(*bufs, send_sems, recv_sems, after)


def exchange_halves_to_sibling(gs, name, by_cols=False):
    n = len(gs)

    def body(*refs):
        ins, outs = refs[:n], refs[n:2 * n]
        send_sems, recv_sems = refs[2 * n:]
        x, y, c, _ = _place()
        cps = []
        for i in range(n):
            if by_cols:
                hc = ins[i].shape[2] // 2
                src = ins[i].at[:, :, pl.ds(pl.multiple_of((1 - c) * hc, 128), hc)]
            else:
                h = ins[i].shape[1] // 2
                src = ins[i].at[:, pl.ds((1 - c) * h, h), :]
            cps.append(pltpu.make_async_remote_copy(
                src_ref=src, dst_ref=outs[i],
                send_sem=send_sems.at[i], recv_sem=recv_sems.at[i], device_id=(x, y, 1 - c), device_id_type=MESH))
        for cp in cps:
            cp.start()
        for cp in cps:
            cp.wait()

    halve = (lambda s: (s[0], s[1], s[2] // 2)) if by_cols else (lambda s: (s[0], s[1] // 2, s[2]))
    return pl.pallas_call(
        body, out_shape=[jax.ShapeDtypeStruct(halve(g.shape), g.dtype) for g in gs],
        in_specs=[_ANY] * n, out_specs=[_ANY] * n,
        scratch_shapes=[pltpu.SemaphoreType.DMA((n,)), pltpu.SemaphoreType.DMA((n,))],
        name=name)(*gs)


def _pack(parts, row_mult=8):
    flat = jnp.concatenate([p.reshape(-1).astype(F32) for p in parts])
    unit = row_mult * 128
    n = -(-flat.shape[0] // unit) * unit
    return jnp.pad(flat, (0, n - flat.shape[0])).reshape(n // 128, 128)


def _unpack(flat, shapes):
    out, off = [], 0
    for s in shapes:
        n = int(np.prod(s))
        out.append(flat[off:off + n].reshape(s))
        off += n
    return out


def _gather_packed(parts, name):
    packed = _pack(parts)
    g = allgather_small(packed, name).reshape(N_DEV, -1)
    return _unpack_rows(g, [p.shape for p in parts])


def _unpack_rows(g, shapes):
    out, off = [], 0
    for s in shapes:
        n = int(np.prod(s))
        out.append(g[:, off:off + n].reshape((g.shape[0],) + tuple(s)))
        off += n
    return out


def _by_chip(t, axis):
    return jnp.concatenate([t[2 * p] for p in range(N_CHIPS)], axis=axis)


def kernel(x, c, ada_w, ada_b, ln_g, ln_b, a_in_w, a_conv_w, a_conv_b, a_dt_bias, a_A_log, a_D, a_norm_g, a_out_w, kv_w, b_in_w, b_out_w, loss_target, m_ada_w, m_ada_b, m_ln_g, m_ln_b, m_a_in_w, m_a_conv_w, m_a_conv_b, m_a_dt_bias, m_a_A_log, m_a_D, m_a_norm_g, m_a_out_w, m_kv_w, m_b_in_w, m_b_out_w, v_ada_w, v_ada_b, v_ln_g, v_ln_b, v_a_in_w, v_a_conv_w, v_a_conv_b, v_a_dt_bias, v_a_A_log, v_a_D, v_a_norm_g, v_a_out_w, v_kv_w, v_b_in_w, v_b_out_w):
    ax, ay, ac = lax.axis_index("x"), lax.axis_index("y"), lax.axis_index("c")
    chip = 2 * ax + ay
    dev = 4 * ax + 2 * ay + ac
    xin = x[0]
    tgt = loss_target[0]
    L, D = xin.shape
    G, P = SSD_G, SSD_P
    H = a_dt_bias.shape[1]
    Kh = H // G
    DI = H * P
    CONVD = a_conv_b.shape[1] * N_CHIPS
    HW = DIL_H * DIL_E
    Ws = ada_w.shape[2]

    w_in_g = allgather_routed(jnp.transpose(a_in_w[0]).astype(BF16), "allgather_w_in")
    later = [a_out_w[0].astype(BF16), kv_w.astype(BF16), b_in_w[0].astype(BF16), b_out_w[0].astype(BF16)]
    later_split = [s.reshape(2, s.shape[0] // 2, s.shape[1]) for s in later]
    ag_ssem, ag_rsem, ag_srcs, ag_lands, ag_token = split_start(
        "gather", later_split, [(N_CHIPS,) + s.shape for s in later_split], w_in_g, "ag_later_start")
    w_in_t = w_in_g.reshape(-1, D)
    w_dt_t = jnp.pad(w_in_t[DI + CONVD:], ((0, 128 - H), (0, 0)))

    c8, cw8, cb8, ng8 = _gather_packed([c[0], a_conv_w[0], a_conv_b[0], a_norm_g[0]], "allgather_small_params")
    conv_w = _by_chip(cw8, 1)
    conv_b = _by_chip(cb8, 0).reshape(1, CONVD)
    norm_g = _by_chip(ng8, 0).reshape(1, DI)

    mod_s = ada_fwd(c8, ada_w)
    (mod8,) = _gather_packed([mod_s], "allgather_small_mod")
    mods = _by_chip(mod8, 2)
    mod = lax.dynamic_index_in_dim(mods, dev, axis=1, keepdims=False) + ada_b
    shift = [mod[l:l + 1, :D] for l in range(DEPTH)]
    scale = [mod[l:l + 1, D:2 * D] for l in range(DEPTH)]
    gate = [mod[l:l + 1, 2 * D:] for l in range(DEPTH)]
    lg = [ln_g[l:l + 1] for l in range(DEPTH)]
    lb = [ln_b[l:l + 1] for l in range(DEPTH)]

    h0 = modulate(xin, scale[0] + ag_token[0:1, 0:1], shift[0], "modulate0")
    zx = mm_nt(h0, w_in_t, BF16, "mm_in_zx", kw_rows=DI + CONVD)
    dtp = mm_nt(h0, w_dt_t, F32, "mm_in_dt")
    xbc = conv_fwd(zx, DI, conv_w, conv_b)
    dtp_g = jnp.transpose(dtp[:, :H].reshape(L, G, Kh), (1, 0, 2))
    dtp_gT = jnp.transpose(dtp_g, (0, 2, 1))
    vecs = [a_dt_bias.reshape(G, 1, Kh), a_dt_bias.reshape(G, Kh, 1), a_A_log.reshape(G, 1, Kh),
            a_A_log.reshape(G, Kh, 1), a_D.reshape(G, 1, Kh), a_D.reshape(G, Kh, 1)]
    y_ssd, states, yn = ssd_fwd(xbc, dtp_g, dtp_gT, *vecs, zx, norm_g, DI)
    later_split, ag_lands = split_wait("gather", ag_ssem, ag_rsem, ag_srcs, ag_lands, yn, "ag_later_wait")
    (land_out,) = pass_to_sibling(ag_lands[:1])
    ps_ssem, ps_rsem, lands_b, ps_token = pass_start(ag_lands[1:], land_out, "ag_pass_start")

    def place_own(o, s, full):
        return lax.dynamic_update_index_in_dim(o, s, chip, 0).reshape((N_CHIPS,) + full.shape)

    w_out_g = place_own(land_out, later_split[0], later[0])
    ymix0 = mm_nn(yn, w_out_g.reshape(-1, D), F32, "mm_out_a", after=ps_token)
    x1, x1b, h1 = ln_mid(xin, ymix0, gate[0], lg[0], lb[0], scale[1], shift[1])
    lands_b = pass_wait(ps_ssem, ps_rsem, lands_b, x1b, "ag_pass_wait")
    w_kv_g, w_bin_g, w_bout_g = [place_own(o, s, full) for o, s, full in zip(lands_b, later_split[1:], later[1:])]

    n_grp = len(DIL_PATTERNS)
    cb = HW // 512
    assert w_bin_g.shape[2] == HW
    kv3 = [mm_cols_dilated(x1b, w_kv_g, [g * cb + t for t in range(cb)] + [(n_grp + g) * cb + t for t in range(cb)],
                           DIL_PATTERNS[g][1], f"mm_kv_{g}") for g in range(n_grp)]
    q3 = [mm_cols_dilated(h1, w_bin_g, [g], DIL_PATTERNS[g][1], f"mm_q_{g}", tn=HW) for g in range(n_grp)]
    z_b = mm_nn(h1, w_bin_g[n_grp], BF16, "mm_z_b")
    os_, lses = [], []
    for gi in range(len(DIL_PATTERNS)):
        o, lse = attn_fwd(q3[gi], kv3[gi], gi)
        os_.append(o)
        lses.append(lse)
    om = merge_fwd(os_, lses, z_b)
    ymix1 = mm_nn(om, w_bout_g, F32, "mm_out_b", stack="col")
    dres2, dy2, dg1, db1, dgate1, sq = ln_final_fwd_bwd(x1, ymix1, gate[1], lg[1], lb[1], tgt)
    loss_part = 0.5 * jnp.sum(sq) / D

    g_bout = mm_tn(om, dy2, BF16, "mm_gw_out_b", stack="col")
    dgated = mm_nt(dy2, w_bout_g, BF16, "mm_gx_out_b", stack="col")
    dos, dprs, dz_b = merge_bwd(dgated, os_, lses, z_b)
    dqs, dks, dvs = [], [], []
    for gi in range(len(DIL_PATTERNS)):
        dq, dk, dv = attn_bwd(q3[gi], kv3[gi], dos[gi], lses[gi], dprs[gi], gi)
        dqs.append(dq)
        dks.append(dk)
        dvs.append(dv)
    dqz = jnp.concatenate(dqs + [dz_b], axis=1)
    dkv = jnp.concatenate(dks + dvs, axis=1)
    g_bin = mm_tn(h1, dqz, BF16, "mm_gw_in_b", stack="col")
    dh1 = mm_nt(dqz, w_bin_g, BF16, "mm_gx_in_b", stack="col")
    g_kv = mm_tn(x1b, dkv, BF16, "mm_gw_kv", stack="col")

    core = ac.astype(jnp.int32).reshape(1)
    chip_i = chip.astype(jnp.int32).reshape(1)

    def begin_exchange(gs, tag):
        shapes = [(g.shape[0], g.shape[1] // 2, g.shape[2]) for g in gs]
        return split_start("sibling", gs, shapes, gs[0], "rs_x%s_start" % tag)

    def begin_scatter(gs, nms, tag, exchange=None, after=None, by_cols=False):
        if exchange is None:
            sib = exchange_halves_to_sibling(gs, "rs_sibling_exchange_" + tag, by_cols=by_cols)
        else:
            gs, sib = split_wait("sibling", exchange[0], exchange[1], exchange[2], exchange[3], after,
                                 "rs_x%s_wait" % tag)
        parts = [add_half(g, a, core, "rs_add_" + nm, by_cols=by_cols) for g, a, nm in zip(gs, sib, nms)]
        return split_start("scatter", parts, [(3,) + t.shape[1:] for t in parts], parts[0], "rs_%s_start" % tag)

    def sum_scattered(handles, after, tag):
        nms, owns, landed = [], [], []
        for k, (handle, hn) in enumerate(handles):
            parts, lands = split_wait("scatter", handle[0], handle[1], handle[2], handle[3], after,
                                      "rs_%s%d_wait" % (tag, k))
            nms += hn
            owns += list(parts)
            landed += list(lands)
        return nms, [sum_partials(own, t, chip_i, "rs_sum_" + nm) for own, t, nm in zip(owns, landed, nms)]

    def begin_join(halves, tag):
        return split_start("join", halves, [t.shape for t in halves], halves[0], "rs_j%s_start" % tag, land_dtype=F32)

    def end_join(handle, after, tag):
        return split_wait("join", handle[0], handle[1], handle[2], handle[3], after, "rs_j%s_wait" % tag)

    names_b = ["kv", "in_b", "out_b"]
    ex_b = begin_exchange([g_kv, g_bin, g_bout], "b")
    dx1_kv = mm_nt(dkv, w_kv_g, BF16, "mm_gx_kv", stack="col", after=ex_b[4])
    rs_b = begin_scatter(None, names_b, "b", exchange=ex_b, after=dx1_kv)

    dres1, dy1, dg0, db0, dgate0, dscale1, dshift1 = mod_ln_bwd(
        dres2, dh1, dx1_kv, x1, scale[1], xin, ymix0, gate[0] + rs_b[4][0:1, 0:1], lg[0])
    g_out = mm_tn(yn, dy1, BF16, "mm_gw_out_a", stack="row")
    ex_a1 = begin_exchange([g_out], "a1")
    dyn = mm_nt(dy1, w_out_g, BF16, "mm_gx_out_a", stack="row", after=ex_a1[4])
    rs_a1 = begin_scatter(None, ["out_a"], "a1", exchange=ex_a1, after=dyn)
    dxs, dB, dC, ddtp_g, dbias_g, dalog_g, dD_g, dz_a, dnorm_g = ssd_bwd(
        xbc, dtp_g, dtp_gT, *vecs, states, dyn, y_ssd, zx, norm_g + rs_a1[4][0:1, 0:1], DI)
    dzx, dws, dbs, lo = dz_a, [], [], 0
    for tag, gpart in (("xs", dxs), ("b", dB), ("c", dC)):
        hi = lo + gpart.shape[1]
        dzx, dw_p, db_p = conv_bwd(zx, DI + lo, conv_w[:, lo:hi], conv_b[:, lo:hi], gpart, dzx, "conv_bwd_" + tag)
        dws.append(dw_p)
        dbs.append(db_p)
        lo = hi
    dconv_w = jnp.concatenate(dws, axis=1)
    dconv_b = jnp.concatenate(dbs, axis=1)
    ddtp = jnp.pad(jnp.transpose(ddtp_g, (1, 0, 2)).reshape(L, H), ((0, 0), (0, 128 - H)))
    g_inT = mm_tn(dzx, h0, BF16, "mm_gw_in_zx", m_rows=DI + CONVD + H)
    g_dtT = mm_tn(ddtp, h0, BF16, "mm_gw_in_dt")
    g_inT = lax.dynamic_update_slice(g_inT, g_dtT[:H], (DI + CONVD, 0))
    rs_a2 = begin_scatter([g_inT.reshape(N_CHIPS, -1, D)], ["in_a"], "a2", by_cols=True)
    dh0 = mm_nn(dzx, w_in_t, BF16, "mm_gx_in_zx", after=rs_a2[4])
    dh0_dt = mm_nn(ddtp, w_dt_t, F32, "mm_gx_in_dt")
    grad_x, dscale0, dshift0 = mod_bwd(dres1, dh0, dh0_dt, xin, scale[0] + rs_a2[4][0:1, 0:1], "mod_bwd0")
    nms_b, halves_b = sum_scattered([(rs_b, names_b)], grad_x, "b")
    join_b = begin_join(halves_b, "b")
    nms_a, halves_a = sum_scattered([(rs_a1, ["out_a"]), (rs_a2, ["in_a"])], join_b[4], "a")
    g_halves = dict(zip(nms_b, zip(*end_join(join_b, halves_a[0], "b"))))
    join_a = begin_join(halves_a, "a")

    def step_halves(w, m, v, nm, after=None):
        shp = w.shape
        mine, theirs_ = g_halves[nm]
        outs4 = adamw_halves(w.reshape(-1, shp[-1]), mine, theirs_, m.reshape(-1, shp[-1]), v.reshape(-1, shp[-1]),
                             core, "adamw_" + nm, after=after)
        return tuple(t.reshape(shp) for t in outs4)

    big = {
        "kv_w": step_halves(kv_w, m_kv_w, v_kv_w, "kv", after=join_a[4]),
        "b_in_w": step_halves(b_in_w, m_b_in_w, v_b_in_w, "in_b"),
        "b_out_w": step_halves(b_out_w, m_b_out_w, v_b_out_w, "out_b"),
    }
    g_halves.update(dict(zip(nms_a, zip(*end_join(join_a, big["kv_w"][1], "a")))))
    g_halves["in_a"] = tuple(jnp.transpose(t) for t in g_halves["in_a"])

    dmod = jnp.concatenate([jnp.concatenate([dshift0, dscale0, dgate0], axis=1),
                            jnp.concatenate([dshift1, dscale1, dgate1], axis=1)], axis=0)
    small_parts = [jnp.concatenate([dg0, dg1], axis=0), jnp.concatenate([db0, db1], axis=0),
                   dbias_g.reshape(1, H), dalog_g.reshape(1, H), dD_g.reshape(1, H),
                   dconv_w, dconv_b, dnorm_g, loss_part.reshape(1, 1)]
    small_shapes = [p.shape for p in small_parts]
    packed = jnp.concatenate([_pack([dmod]), _pack(small_parts)], axis=0)
    n_mod_rows = _pack([dmod]).shape[0]
    sg_ssem, sg_rsem, sg_src, sg_land, sg_token = gather8_start(packed, g_halves["in_a"][1], "small_grads_start")
    big["a_in_w"] = step_halves(a_in_w, m_a_in_w, v_a_in_w, "in_a", after=sg_token)
    big["a_out_w"] = step_halves(a_out_w, m_a_out_w, v_a_out_w, "out_a")
    sg_land = gather8_wait(sg_ssem, sg_rsem, sg_src, sg_land, big["a_in_w"][1], "small_grads_wait")
    gathered = lax.dynamic_update_index_in_dim(sg_land, packed, dev, 0)
    dmod8 = gathered[:, :n_mod_rows].reshape(N_DEV, -1)[:, :2 * 3 * D].reshape(N_DEV, DEPTH, 3 * D)
    summed = sum_leading(gathered, "sum_small")
    g_ada_b = summed[:n_mod_rows].reshape(-1)[:2 * 3 * D].reshape(DEPTH, 3 * D)
    (g_ln_g, g_ln_b, g_dt_bias, g_a_log, g_dsk, g_conv_w, g_conv_b, g_norm_g, loss_all) = _unpack(
        summed[n_mod_rows:].reshape(-1), small_shapes)
    loss = loss_all.reshape(())
    Cs = CONVD // N_CHIPS
    g_conv_w_s = lax.dynamic_slice_in_dim(g_conv_w, chip * Cs, Cs, axis=1)
    g_conv_b_s = lax.dynamic_slice_in_dim(g_conv_b, chip * Cs, Cs, axis=1)
    g_norm_g_s = lax.dynamic_slice_in_dim(g_norm_g, chip * (DI // N_CHIPS), DI // N_CHIPS, axis=1)
    dmod_s = jnp.transpose(lax.dynamic_slice_in_dim(dmod8, chip * Ws, Ws, axis=2), (1, 0, 2))

    def step2d(w, g, m, v, nm):
        shp = w.shape
        d_, m_, v_ = adamw(w.reshape(-1, shp[-1]), g.reshape(-1, shp[-1]), m.reshape(-1, shp[-1]),
                           v.reshape(-1, shp[-1]), "adamw_" + nm)
        return g.reshape(shp), d_.reshape(shp), m_.reshape(shp), v_.reshape(shp)

    big["ada_w"] = step2d(ada_w, ada_wgrad(jnp.transpose(c8), dmod_s), m_ada_w, v_ada_w, "ada_w")
    small_names = ["ada_b", "ln_g", "ln_b", "a_conv_w", "a_conv_b", "a_dt_bias", "a_A_log", "a_D", "a_norm_g"]
    small_w = [ada_b, ln_g, ln_b, a_conv_w, a_conv_b, a_dt_bias, a_A_log, a_D, a_norm_g]
    small_m = [m_ada_b, m_ln_g, m_ln_b, m_a_conv_w, m_a_conv_b, m_a_dt_bias, m_a_A_log, m_a_D, m_a_norm_g]
    small_v = [v_ada_b, v_ln_g, v_ln_b, v_a_conv_w, v_a_conv_b, v_a_dt_bias, v_a_A_log, v_a_D, v_a_norm_g]
    small_g = [g_ada_b, g_ln_g, g_ln_b, g_conv_w_s, g_conv_b_s, g_dt_bias, g_a_log, g_dsk, g_norm_g_s]
    shapes = [w.shape for w in small_w]
    small_g = [g.reshape(s) for g, s in zip(small_g, shapes)]
    d_p, m_p, v_p = adamw(_pack(small_w), _pack(small_g), _pack(small_m), _pack(small_v), "adamw_small")
    small = {}
    for nm, g, d_, m_, v_ in zip(small_names, small_g, _unpack(d_p.reshape(-1), shapes), _unpack(m_p.reshape(-1), shapes),
                                 _unpack(v_p.reshape(-1), shapes)):
        small[nm] = (g, d_, m_, v_)
    allw = {**big, **small}
    order = ["ada_w", "ada_b", "ln_g", "ln_b", "a_in_w", "a_conv_w", "a_conv_b", "a_dt_bias", "a_A_log", "a_D",
             "a_norm_g", "a_out_w", "kv_w", "b_in_w", "b_out_w"]
    outs = [loss, grad_x.reshape(x.shape)]
    for k in range(4):
        outs += [allw[n][k] for n in order]
    return tuple(outs)
```

```python
import functools

import jax
import jax.numpy as jnp
import numpy as np
from jax import lax
from jax.experimental import pallas as pl
from jax.experimental.pallas import tpu as pltpu

F32 = jnp.float32
BF16 = jnp.bfloat16
MESH = pl.DeviceIdType.MESH

DEPTH = 2
ALPHA = (2 * DEPTH) ** 0.25
LN_EPS = 1e-5
RMS_EPS = 1e-5
SSD_P = 64
SSD_N = 128
SSD_Q = 256
SSD_G = 8
CONV_W = 4
DIL_PATTERNS = ((128, 1), (512, 4), (2048, 16))
DIL_H = 8
DIL_E = 128
DIL_BLK = 128
ADAM_LR, ADAM_B1, ADAM_B2, ADAM_EPS, ADAM_WD, ADAM_STEP = 0.001, 0.9, 0.999, 1e-08, 0.01, 10

VMEM_LIMIT = 56 * 1024 * 1024
N_CHIPS = 4
N_DEV = 8


def _tile(dim, target, mult=128):
    if dim <= target:
        return dim
    t = (target // mult) * mult
    while t >= mult:
        if dim % t == 0:
            return t
        t -= mult
    return dim


def _cp(sem):
    return pltpu.CompilerParams(dimension_semantics=sem, vmem_limit_bytes=VMEM_LIMIT)


def _sigmoid(x):
    return 1.0 / (1.0 + jnp.exp(-x))


def _silu(x):
    return x * _sigmoid(x)


def _dsilu(x):
    s = _sigmoid(x)
    return s * (1.0 + x * (1.0 - s))


def _softplus(x):
    return jnp.maximum(x, 0.0) + jnp.log(1.0 + jnp.exp(-jnp.abs(x)))


def _mm_call(a, b, out_shape, grid, a_spec, b_spec, o_spec, acc_shape, dims, name, after=None):
    nk = grid[2]
    extra = [] if after is None else [after]

    def prod(a_ref, b_ref):
        return lax.dot_general(a_ref[...].astype(BF16), b_ref[...].astype(BF16), (dims, ((), ())),
                               preferred_element_type=F32)

    def body_single(a_ref, b_ref, *rest):
        o_ref = rest[len(extra)]
        o_ref[...] = prod(a_ref, b_ref).astype(o_ref.dtype)

    def body_multi(a_ref, b_ref, *rest):
        o_ref, acc_ref = rest[len(extra):]
        k = pl.program_id(2)

        @pl.when(k == 0)
        def _():
            acc_ref[...] = prod(a_ref, b_ref)

        @pl.when(jnp.logical_and(k > 0, k < nk - 1))
        def _():
            acc_ref[...] += prod(a_ref, b_ref)

        @pl.when(k == nk - 1)
        def _():
            o_ref[...] = (acc_ref[...] + prod(a_ref, b_ref)).astype(o_ref.dtype)

    return pl.pallas_call(
        body_single if nk == 1 else body_multi, grid=grid, in_specs=[a_spec, b_spec] + [_ANY] * len(extra),
        out_specs=o_spec, out_shape=out_shape, scratch_shapes=[] if nk == 1 else [pltpu.VMEM(acc_shape, F32)],
        compiler_params=_cp(("parallel", "parallel", "arbitrary")), name=name)(a, b, *extra)


def mm_nn(a, b, out_dtype, name, stack=None, tm=1024, tn=1024, tk=2048, n_cols=None, after=None):
    M, K = a.shape
    if stack is None:
        N = b.shape[1] if n_cols is None else n_cols
        tn, tk = _tile(N, tn), _tile(K, tk)
        b_spec = pl.BlockSpec((tk, tn), lambda i, j, k: (k, j))
    elif stack == "col":
        S, _, Ns = b.shape
        N = S * Ns
        tn, tk = _tile(Ns, tn), _tile(K, tk)
        npb = Ns // tn
        b_spec = pl.BlockSpec((None, tk, tn), lambda i, j, k: (j // npb, k, j % npb))
    else:
        S, Ks, N = b.shape
        tn, tk = _tile(N, tn), _tile(Ks, tk)
        kpb = Ks // tk
        b_spec = pl.BlockSpec((None, tk, tn), lambda i, j, k: (k // kpb, k % kpb, j))
    tm = _tile(M, tm)
    return _mm_call(a, b, jax.ShapeDtypeStruct((M, N), out_dtype), (M // tm, N // tn, K // tk),
                    pl.BlockSpec((tm, tk), lambda i, j, k: (i, k)), b_spec,
                    pl.BlockSpec((tm, tn), lambda i, j, k: (i, j)), (tm, tn), ((1,), (0,)), name, after=after)


def mm_cols_dilated(a, b, gcols, d, name, tm=1024, tn=512):
    L, K = a.shape
    S, _, Ns = b.shape
    tm, tn = _tile(L, tm), _tile(Ns, tn)
    npb = Ns // tn
    nj = len(gcols)
    rows = tm // d

    def body(cols_ref, a_ref, b_ref, o_ref, *scr):
        prod = jnp.dot(a_ref[...], b_ref[...], preferred_element_type=F32)
        if d == 1:
            o_ref[0] = prod.astype(BF16)
        else:
            for c in range(tn // 128):
                scr[0][c] = prod[:, c * 128:(c + 1) * 128]
            for r in range(d):
                for c in range(tn // 128):
                    o_ref[r, :, c * 128:(c + 1) * 128] = scr[0].at[c][pl.ds(r, rows, stride=d), :].astype(BF16)

    return pl.pallas_call(
        body,
        grid_spec=pltpu.PrefetchScalarGridSpec(
            num_scalar_prefetch=1, grid=(L // tm, nj),
            in_specs=[pl.BlockSpec((tm, K), lambda i, j, c: (i, 0)),
                      pl.BlockSpec((None, K, tn), lambda i, j, c: (c[j] // npb, 0, c[j] % npb))],
            out_specs=pl.BlockSpec((d, rows, tn), lambda i, j, c: (0, i, j)),
            scratch_shapes=[] if d == 1 else [pltpu.VMEM((tn // 128, tm, 128), F32)]),
        out_shape=jax.ShapeDtypeStruct((d, L // d, nj * tn), BF16),
        compiler_params=_cp(("parallel", "arbitrary")), name=name)(jnp.asarray(gcols, jnp.int32), a, b)


def mm_nt(a, b, out_dtype, name, stack=None, tm=1024, tn=1024, tk=2048, after=None, kw_rows=None):
    M, C = a.shape
    if stack is None:
        Kw = b.shape[0] if kw_rows is None else kw_rows
        tn, tk = _tile(Kw, tn), _tile(C, tk)
        b_spec = pl.BlockSpec((tn, tk), lambda i, j, k: (j, k))
    elif stack == "col":
        S, Kw, Cs = b.shape
        tn, tk = _tile(Kw, tn), _tile(Cs, tk)
        cpb = Cs // tk
        b_spec = pl.BlockSpec((None, tn, tk), lambda i, j, k: (k // cpb, j, k % cpb))
    else:
        S, Ks, _ = b.shape
        Kw = S * Ks
        tn, tk = _tile(Ks, tn), _tile(C, tk)
        jpb = Ks // tn
        b_spec = pl.BlockSpec((None, tn, tk), lambda i, j, k: (j // jpb, j % jpb, k))
    tm = _tile(M, tm)
    return _mm_call(a, b, jax.ShapeDtypeStruct((M, Kw), out_dtype), (M // tm, Kw // tn, C // tk),
                    pl.BlockSpec((tm, tk), lambda i, j, k: (i, k)), b_spec,
                    pl.BlockSpec((tm, tn), lambda i, j, k: (i, j)), (tm, tn), ((1,), (1,)), name, after=after)


def mm_tn(a, b, out_dtype, name, stack=None, n_stack=N_CHIPS, tm=1024, tn=1024, tk=2048, m_rows=None):
    L, M = a.shape
    N = b.shape[1]
    tk = _tile(L, tk)
    if stack is None:
        tm, tn = _tile(M, tm), _tile(N, tn)
        o_spec = pl.BlockSpec((tm, tn), lambda i, j, k: (i, j))
        out_shape = (M if m_rows is None else m_rows, N)
    elif stack == "col":
        Ns = N // n_stack
        tm, tn = _tile(M, tm), _tile(Ns, tn)
        npb = Ns // tn
        o_spec = pl.BlockSpec((None, tm, tn), lambda i, j, k: (j // npb, i, j % npb))
        out_shape = (n_stack, M, Ns)
    else:
        Ms = M // n_stack
        tm, tn = _tile(Ms, tm), _tile(N, tn)
        mpb = Ms // tm
        o_spec = pl.BlockSpec((None, tm, tn), lambda i, j, k: (i // mpb, i % mpb, j))
        out_shape = (n_stack, Ms, N)
    return _mm_call(a, b, jax.ShapeDtypeStruct(out_shape, out_dtype), (M // tm, N // tn, L // tk),
                    pl.BlockSpec((tk, tm), lambda i, j, k: (k, i)), pl.BlockSpec((tk, tn), lambda i, j, k: (k, j)),
                    o_spec, (tm, tn), ((0,), (0,)), name)


def _row_specs(tr, widths):
    return [pl.BlockSpec((tr, w), lambda i: (i, 0)) for w in widths]


def _vec_spec(w):
    return pl.BlockSpec((1, w), lambda i: (0, 0))


def _acc_rows(ref, val, i):
    s = jnp.sum(val, axis=0, keepdims=True)

    @pl.when(i == 0)
    def _():
        ref[...] = s

    @pl.when(i > 0)
    def _():
        ref[...] += s


def modulate(x, scale, shift, name):
    L, D = x.shape
    tr = _tile(L, 256, 16)

    def body(x_ref, sc_ref, sh_ref, h_ref):
        h_ref[...] = (x_ref[...] * (1.0 + sc_ref[...]) + sh_ref[...]).astype(BF16)

    return pl.pallas_call(
        body, grid=(L // tr,), in_specs=_row_specs(tr, [D]) + [_vec_spec(D)] * 2, out_specs=_row_specs(tr, [D])[0],
        out_shape=jax.ShapeDtypeStruct((L, D), BF16), compiler_params=_cp(("parallel",)), name=name)(x, scale, shift)


def _ln_core(x, y, gate, g, b):
    u = ALPHA * x + (1.0 + gate) * y
    mu = jnp.mean(u, axis=-1, keepdims=True)
    d = u - mu
    var = jnp.mean(d * d, axis=-1, keepdims=True)
    rstd = lax.rsqrt(var + LN_EPS)
    xhat = d * rstd
    return xhat * g + b, xhat, rstd


def ln_mid(x, y, gate, g, b, scale, shift):
    L, D = x.shape
    tr = _tile(L, 128, 16)

    def body(x_ref, y_ref, gate_ref, g_ref, b_ref, sc_ref, sh_ref, x1_ref, x1b_ref, h_ref):
        x1, _, _ = _ln_core(x_ref[...], y_ref[...], gate_ref[...], g_ref[...], b_ref[...])
        x1_ref[...] = x1
        x1b_ref[...] = x1.astype(BF16)
        h_ref[...] = (x1 * (1.0 + sc_ref[...]) + sh_ref[...]).astype(BF16)

    return pl.pallas_call(
        body, grid=(L // tr,), in_specs=_row_specs(tr, [D, D]) + [_vec_spec(D)] * 5,
        out_specs=_row_specs(tr, [D, D, D]),
        out_shape=[jax.ShapeDtypeStruct((L, D), F32), jax.ShapeDtypeStruct((L, D), BF16),
                   jax.ShapeDtypeStruct((L, D), BF16)],
        compiler_params=_cp(("parallel",)), name="ln_mid")(x, y, gate, g, b, scale, shift)


def _ln_bwd_rows(dout_v, xhat, rstd, g):
    dxh = dout_v * g
    m1 = jnp.mean(dxh, axis=-1, keepdims=True)
    m2 = jnp.mean(dxh * xhat, axis=-1, keepdims=True)
    return rstd * (dxh - m1 - xhat * m2)


def ln_final_fwd_bwd(x, y, gate, g, b, target):
    L, D = x.shape
    tr = _tile(L, 128, 16)

    def body(x_ref, y_ref, gate_ref, g_ref, b_ref, t_ref, dres_ref, dy_ref, dg_ref, db_ref, dgate_ref, sq_ref):
        i = pl.program_id(0)
        yv = y_ref[...]
        out, xhat, rstd = _ln_core(x_ref[...], yv, gate_ref[...], g_ref[...], b_ref[...])
        err = out - t_ref[...]
        dout_v = err * (1.0 / D)
        du = _ln_bwd_rows(dout_v, xhat, rstd, g_ref[...])
        dres_ref[...] = ALPHA * du
        dy_ref[...] = ((1.0 + gate_ref[...]) * du).astype(BF16)
        _acc_rows(dg_ref, dout_v * xhat, i)
        _acc_rows(db_ref, dout_v, i)
        _acc_rows(dgate_ref, du * yv, i)
        _acc_rows(sq_ref, err * err, i)

    return pl.pallas_call(
        body, grid=(L // tr,), in_specs=_row_specs(tr, [D, D]) + [_vec_spec(D)] * 3 + _row_specs(tr, [D]),
        out_specs=_row_specs(tr, [D, D]) + [_vec_spec(D)] * 4,
        out_shape=[jax.ShapeDtypeStruct((L, D), F32), jax.ShapeDtypeStruct((L, D), BF16)]
        + [jax.ShapeDtypeStruct((1, D), F32)] * 4,
        compiler_params=_cp(("arbitrary",)), name="ln_final_fwd_bwd")(x, y, gate, g, b, target)


def mod_bwd(dres, dh, dh2, xin, scale, name):
    L, D = xin.shape
    tr = _tile(L, 128, 16)

    def body(dres_ref, dh_ref, dh2_ref, x_ref, sc_ref, dx_ref, dsc_ref, dsh_ref):
        i = pl.program_id(0)
        dh_v = dh_ref[...].astype(F32) + dh2_ref[...].astype(F32)
        dx_ref[...] = dres_ref[...] + dh_v * (1.0 + sc_ref[...])
        _acc_rows(dsc_ref, dh_v * x_ref[...], i)
        _acc_rows(dsh_ref, dh_v, i)

    return pl.pallas_call(
        body, grid=(L // tr,), in_specs=_row_specs(tr, [D, D, D, D]) + [_vec_spec(D)],
        out_specs=_row_specs(tr, [D]) + [_vec_spec(D)] * 2,
        out_shape=[jax.ShapeDtypeStruct((L, D), F32)] + [jax.ShapeDtypeStruct((1, D), F32)] * 2,
        compiler_params=_cp(("arbitrary",)), name=name)(dres, dh, dh2, xin, scale)


def mod_ln_bwd(dres_in, dh, dskip, xmid, scale, x, y, gate, g):
    L, D = x.shape
    tr = _tile(L, 128, 16)

    def body(dres_ref, dh_ref, dskip_ref, xm_ref, sc_ref, x_ref, y_ref, gate_ref, g_ref,
             dres_out, dy_ref, dg_ref, db_ref, dgate_ref, dsc_ref, dsh_ref):
        i = pl.program_id(0)
        dh_v = dh_ref[...].astype(F32)
        dout_v = dres_ref[...] + dskip_ref[...].astype(F32) + dh_v * (1.0 + sc_ref[...])
        _acc_rows(dsc_ref, dh_v * xm_ref[...], i)
        _acc_rows(dsh_ref, dh_v, i)
        yv = y_ref[...]
        _, xhat, rstd = _ln_core(x_ref[...], yv, gate_ref[...], g_ref[...], 0.0)
        du = _ln_bwd_rows(dout_v, xhat, rstd, g_ref[...])
        dres_out[...] = ALPHA * du
        dy_ref[...] = ((1.0 + gate_ref[...]) * du).astype(BF16)
        _acc_rows(dg_ref, dout_v * xhat, i)
        _acc_rows(db_ref, dout_v, i)
        _acc_rows(dgate_ref, du * yv, i)

    return pl.pallas_call(
        body, grid=(L // tr,),
        in_specs=_row_specs(tr, [D] * 4) + [_vec_spec(D)] + _row_specs(tr, [D, D]) + [_vec_spec(D)] * 2,
        out_specs=_row_specs(tr, [D, D]) + [_vec_spec(D)] * 5,
        out_shape=[jax.ShapeDtypeStruct((L, D), F32), jax.ShapeDtypeStruct((L, D), BF16)]
        + [jax.ShapeDtypeStruct((1, D), F32)] * 5,
        compiler_params=_cp(("arbitrary",)), name="mod_ln_bwd")(dres_in, dh, dskip, xmid, scale, x, y, gate, g)


CONV_HALO = 16


def _conv_rows(x_ref, i, tr, L):
    nblk = L // tr
    s = pl.multiple_of(i * tr, CONV_HALO)
    cur = x_ref[pl.ds(s, tr), :].astype(F32)
    sp = pl.multiple_of(jnp.maximum(i * tr - CONV_HALO, 0), CONV_HALO)
    sn = pl.multiple_of(jnp.minimum(i * tr + tr, L - CONV_HALO), CONV_HALO)
    prev = x_ref[pl.ds(sp, CONV_HALO), :].astype(F32) * (i > 0).astype(F32)
    nxt = x_ref[pl.ds(sn, CONV_HALO), :].astype(F32) * (i < nblk - 1).astype(F32)
    return jnp.concatenate([prev, cur, nxt], axis=0)


def _shift_rows(v, j):
    n = v.shape[0]
    return v if j % n == 0 else pltpu.roll(v, j % n, 0)


def _conv_taps(xe):
    return [_shift_rows(xe, CONV_W - 1 - k) for k in range(CONV_W)]


def _conv_eval(taps, w_ref, b_ref):
    c = b_ref[...] + w_ref[0:1, :] * taps[0]
    for k in range(1, CONV_W):
        c = c + w_ref[k:k + 1, :] * taps[k]
    return c


def conv_fwd(zx, col0, conv_w, conv_b):
    L = zx.shape[0]
    C = conv_w.shape[1]
    tc = _tile(C, 512)
    tr = _tile(L, 512, CONV_HALO)
    off = col0 // tc

    def body(x_ref, w_ref, b_ref, o_ref):
        i = pl.program_id(1)
        xe = _conv_rows(x_ref, i, tr, L)
        c = _conv_eval(_conv_taps(xe), w_ref, b_ref)[CONV_HALO:CONV_HALO + tr]
        o_ref[...] = _silu(c).astype(BF16)

    return pl.pallas_call(
        body, grid=(C // tc, L // tr),
        in_specs=[pl.BlockSpec((L, tc), lambda j, i: (0, off + j)), pl.BlockSpec((CONV_W, tc), lambda j, i: (0, j)),
                  pl.BlockSpec((1, tc), lambda j, i: (0, j))],
        out_specs=pl.BlockSpec((tr, tc), lambda j, i: (i, j)),
        out_shape=jax.ShapeDtypeStruct((L, C), BF16), compiler_params=_cp(("parallel", "arbitrary")),
        name="conv_fwd")(zx, conv_w, conv_b)


def conv_bwd(zx, col0, conv_w, conv_b, g, dzx, name):
    L = zx.shape[0]
    C = conv_w.shape[1]
    tc = _tile(C, 512)
    tr = _tile(L, 512, CONV_HALO)
    off = col0 // tc
    H = CONV_HALO

    def body(x_ref, g_ref, w_ref, b_ref, buf_ref, dx_ref, dw_ref, db_ref):
        i = pl.program_id(1)
        xe = _conv_rows(x_ref, i, tr, L)
        ge = _conv_rows(g_ref, i, tr, L)
        taps = _conv_taps(xe)
        dc = ge * _dsilu(_conv_eval(taps, w_ref, b_ref))
        dx = w_ref[CONV_W - 1:CONV_W, :] * dc
        for k in range(CONV_W - 1):
            dx = dx + w_ref[k:k + 1, :] * _shift_rows(dc, -(CONV_W - 1 - k))
        dx_ref[...] = dx[H:H + tr].astype(BF16)
        dcc = dc[H:H + tr]
        rows = [jnp.sum(dcc * taps[k][H:H + tr], axis=0, keepdims=True) for k in range(CONV_W)]
        dwv = jnp.concatenate(rows + [jnp.zeros((8 - CONV_W, tc), F32)], axis=0)
        dbv = jnp.sum(dcc, axis=0, keepdims=True)

        @pl.when(i == 0)
        def _():
            dw_ref[...] = dwv
            db_ref[...] = dbv

        @pl.when(i > 0)
        def _():
            dw_ref[...] += dwv
            db_ref[...] += dbv

    dx, dw, db = pl.pallas_call(
        body, grid=(C // tc, L // tr),
        in_specs=[pl.BlockSpec((L, tc), lambda j, i: (0, off + j)), pl.BlockSpec((L, tc), lambda j, i: (0, j)),
                  pl.BlockSpec((CONV_W, tc), lambda j, i: (0, j)), pl.BlockSpec((1, tc), lambda j, i: (0, j)), _ANY],
        out_specs=[pl.BlockSpec((tr, tc), lambda j, i: (i, off + j)), pl.BlockSpec((8, tc), lambda j, i: (0, j)),
                   pl.BlockSpec((1, tc), lambda j, i: (0, j))],
        out_shape=[jax.ShapeDtypeStruct(dzx.shape, BF16), jax.ShapeDtypeStruct((8, C), F32),
                   jax.ShapeDtypeStruct((1, C), F32)],
        input_output_aliases={4: 0},
        compiler_params=_cp(("parallel", "arbitrary")), name=name)(zx, g, conv_w, conv_b, dzx)
    return dx, dw[:CONV_W], db


_NN = (((1,), (0,)), ((), ()))


def _pieces(x, n):
    out, r = [], x
    for _ in range(n):
        p = r.astype(BF16)
        out.append(p)
        r = r - p.astype(F32)
    return out


def _dot01(a, b01, n, dims=_NN):
    b = b01.astype(BF16)
    return functools.reduce(lambda u, v: u + v,
                            [lax.dot_general(p, b, dims, preferred_element_type=F32) for p in _pieces(a, n)])


def _dot01_left(a01, b, n, dims=_NN):
    a = a01.astype(BF16)
    return functools.reduce(lambda u, v: u + v,
                            [lax.dot_general(a, p, dims, preferred_element_type=F32) for p in _pieces(b, n)])


def _ssd_common(dtp_ref, dtpT_ref, bias_ref, biasT_ref, alog_ref, alogT_ref, b_ref, c_ref):
    Q = SSD_Q
    dt = _softplus(dtp_ref[...] + bias_ref[...])
    A = -jnp.exp(alog_ref[...])
    row = lax.broadcasted_iota(jnp.int32, (Q, Q), 0)
    col = lax.broadcasted_iota(jnp.int32, (Q, Q), 1)
    causal = row >= col
    tril = causal.astype(F32)
    Kh = dt.shape[1]
    acum = _dot01_left(tril, dt * A, 3)
    eye = (lax.broadcasted_iota(jnp.int32, (Kh, Kh), 0) == lax.broadcasted_iota(jnp.int32, (Kh, Kh), 1)).astype(F32)
    acumT = _dot01_left(eye, acum, 3, dims=(((1,), (1,)), ((), ())))
    Bm = b_ref[...]
    Cm = c_ref[...]
    cb = lax.dot_general(Cm, Bm, (((1,), (1,)), ((), ())), preferred_element_type=F32)
    return dt, A, causal, row, col, acum, acumT, Bm, Cm, cb


def _ssd_in_specs(Q, GP, N, Kh, DI, cmap):
    nb0 = DI // N
    vec = pl.BlockSpec((None, 1, Kh), lambda g, c: (g, 0, 0))
    vecT = pl.BlockSpec((None, Kh, 1), lambda g, c: (g, 0, 0))
    return [pl.BlockSpec((Q, GP), lambda g, c: (cmap(c), g)),
            pl.BlockSpec((Q, N), lambda g, c: (cmap(c), nb0 + g)),
            pl.BlockSpec((Q, N), lambda g, c: (cmap(c), nb0 + SSD_G + g)),
            pl.BlockSpec((None, Q, Kh), lambda g, c: (g, cmap(c), 0)),
            pl.BlockSpec((None, Kh, Q), lambda g, c: (g, 0, cmap(c))),
            vec, vecT, vec, vecT, vec, vecT]


def _hi(a, b01):
    return _dot01(a, b01, 2)


def _headsum(a, b01):
    return _dot01(a, b01, 1)


def _ssd_heads(dskT_ref, acum, acumT, dt, Kh):
    Q, P, N = SSD_Q, SSD_P, SSD_N
    GP = Kh * P
    sh_p = P.bit_length() - 1
    seg = lambda shape, dim: lax.shift_right_logical(lax.broadcasted_iota(jnp.int32, shape, dim), sh_p)
    E = (seg((Kh, GP), 1) == lax.broadcasted_iota(jnp.int32, (Kh, GP), 0)).astype(F32)
    ET = (seg((GP, Kh), 0) == lax.broadcasted_iota(jnp.int32, (GP, Kh), 1)).astype(F32)
    a_last = acum[Q - 1:Q, :]
    tail = jnp.exp(a_last - acum)
    eLT = jnp.exp(acumT[:, Q - 1:Q])
    rowseg = seg((GP, N), 0)
    eL_b = jnp.zeros((GP, N), F32)
    for k in range(Kh):
        eL_b = jnp.where(rowseg == k, eLT[k:k + 1, :], eL_b)
    return dict(
        E=E, ET=ET, a_last=a_last, tail=tail, eL_b=eL_b,
        dt_all=_hi(dt, E), ea_all=_headsum(jnp.exp(acum), E), tail_all=_headsum(tail, E),
        dsk_all=jnp.sum(E * dskT_ref[...], axis=0, keepdims=True))


def _head_chunks(GP):
    CW = min(GP, 128)
    return CW, CW // SSD_P, GP // CW


def _head_mask(Q, CW, kk):
    lane = lax.broadcasted_iota(jnp.int32, (Q, CW), 1)
    return jnp.logical_and(lane >= kk * SSD_P, lane < (kk + 1) * SSD_P)


def ssd_fwd(xbc, dtp_g, dtp_gT, bias_g, bias_gT, alog_g, alog_gT, dsk_g, dsk_gT, zx, norm_g, DI):
    L = xbc.shape[0]
    Q, P, N, G = SSD_Q, SSD_P, SSD_N, SSD_G
    GP = DI // G
    Kh = GP // P
    nc = L // Q

    CW, hpc, nch = _head_chunks(GP)
    nt = (((1,), (1,)), ((), ()))
    tn = (((0,), (0,)), ((), ()))

    def body(xs_ref, b_ref, c_ref, dtp_ref, dtpT_ref, bias_ref, biasT_ref, alog_ref, alogT_ref, dsk_ref, dskT_ref,
             z_ref, ng_ref, y_ref, st_ref, yn_ref, state):
        @pl.when(pl.program_id(1) == 0)
        def _():
            state[...] = jnp.zeros(state.shape, F32)

        st_ref[...] = state[...]
        dt, A, causal, row, col, acum, acumT, Bm, Cm, cb = _ssd_common(
            dtp_ref, dtpT_ref, bias_ref, biasT_ref, alog_ref, alogT_ref, b_ref, c_ref)
        hd = _ssd_heads(dskT_ref, acum, acumT, dt, Kh)
        xs = xs_ref[...].astype(F32)
        xdt_all = xs * hd["dt_all"]
        S_all = state[...]
        y_all = (lax.dot_general(Cm, S_all.astype(BF16), nt, preferred_element_type=F32) * hd["ea_all"]
                 + xs * hd["dsk_all"])
        state[...] = S_all * hd["eL_b"] + lax.dot_general(
            (xdt_all * hd["tail_all"]).astype(BF16), Bm, tn, preferred_element_type=F32)
        for ch in range(nch):
            cs = slice(ch * CW, (ch + 1) * CW)
            xc = xdt_all[:, cs]
            acc = y_all[:, cs]
            for kk in range(hpc):
                k = ch * hpc + kk
                decay = jnp.exp(jnp.where(causal, acum[:, k:k + 1] - acumT[k:k + 1, :], -jnp.inf))
                xk = xc if hpc == 1 else jnp.where(_head_mask(Q, CW, kk), xc, 0.0)
                acc = acc + jnp.dot((cb * decay).astype(BF16), xk.astype(BF16), preferred_element_type=F32)
            y_ref[:, cs] = acc.astype(BF16)
        y2 = y_ref[...].astype(F32) * _silu(z_ref[...].astype(F32))
        rr = lax.rsqrt(jnp.mean(y2 * y2, axis=-1, keepdims=True) + RMS_EPS)
        yn_ref[...] = (y2 * rr * ng_ref[...]).astype(BF16)

    tile = pl.BlockSpec((Q, GP), lambda g, c: (c, g))
    return pl.pallas_call(
        body, grid=(G, nc),
        in_specs=_ssd_in_specs(Q, GP, N, Kh, DI, lambda c: c) + [tile, pl.BlockSpec((1, GP), lambda g, c: (0, g))],
        out_specs=[tile, pl.BlockSpec((None, None, GP, N), lambda g, c: (c, g, 0, 0)), tile],
        out_shape=[jax.ShapeDtypeStruct((L, DI), BF16), jax.ShapeDtypeStruct((nc, G, GP, N), F32),
                   jax.ShapeDtypeStruct((L, DI), BF16)],
        scratch_shapes=[pltpu.VMEM((GP, N), F32)], compiler_params=_cp(("parallel", "arbitrary")),
        name="ssd_fwd")(xbc, xbc, xbc, dtp_g, dtp_gT, bias_g, bias_gT, alog_g, alog_gT, dsk_g, dsk_gT, zx, norm_g)


def ssd_bwd(xbc, dtp_g, dtp_gT, bias_g, bias_gT, alog_g, alog_gT, dsk_g, dsk_gT, states, dyn, y, zx, norm_g, DI):
    L = xbc.shape[0]
    Q, P, N, G = SSD_Q, SSD_P, SSD_N, SSD_G
    GP = DI // G
    Kh = GP // P
    nc = L // Q
    rev = lambda c: nc - 1 - c

    CW, hpc, nch = _head_chunks(GP)

    def body(xs_ref, b_ref, c_ref, dtp_ref, dtpT_ref, bias_ref, biasT_ref, alog_ref, alogT_ref, dsk_ref, dskT_ref,
             st_ref, dyn_ref, y_ref, z_ref, ng_ref,
             dxs_ref, dB_ref, dC_ref, ddtp_ref, dbias_ref, dalog_ref, dD_ref, dz_ref, dng_ref, dstate):
        ci = pl.program_id(1)

        @pl.when(ci == 0)
        def _():
            dstate[...] = jnp.zeros(dstate.shape, F32)

        dt, A, causal, row, col, acum, acumT, Bm, Cm, cb = _ssd_common(
            dtp_ref, dtpT_ref, bias_ref, biasT_ref, alog_ref, alogT_ref, b_ref, c_ref)
        tn = (((0,), (0,)), ((), ()))
        nt = (((1,), (1,)), ((), ()))
        hd = _ssd_heads(dskT_ref, acum, acumT, dt, Kh)
        ET, tail = hd["ET"], hd["tail"]
        cbT = lax.dot_general(Bm, Cm, nt, preferred_element_type=F32)
        causalT = row <= col
        xs = xs_ref[...].astype(F32)
        xdt_all = xs * hd["dt_all"]
        yv = y_ref[...].astype(F32)
        zv = z_ref[...].astype(F32)
        dynv = dyn_ref[...].astype(F32)
        sz = _silu(zv)
        y2 = yv * sz
        rr = lax.rsqrt(jnp.mean(y2 * y2, axis=-1, keepdims=True) + RMS_EPS)
        yh = y2 * rr
        dyh = dynv * ng_ref[...]
        dy2 = rr * (dyh - yh * jnp.mean(dyh * yh, axis=-1, keepdims=True))
        dz_ref[...] = (dy2 * yv * _dsilu(zv)).astype(BF16)
        dng_v = jnp.sum(dynv * yh, axis=0, keepdims=True)
        dyb = (dy2 * sz).astype(BF16)
        dy_all = dyb.astype(F32)
        S_all = st_ref[...]
        S_b = S_all.astype(BF16)
        dS_all = dstate[...]
        dS_b = dS_all.astype(BF16)
        CS_all = lax.dot_general(Cm, S_b, nt, preferred_element_type=F32)
        dyE_b = (dy_all * hd["ea_all"]).astype(BF16)
        dC_acc = jnp.dot(dyE_b, S_b, preferred_element_type=F32)
        dS_y = lax.dot_general(dyE_b, Cm, tn, preferred_element_type=F32)
        BdS_all = lax.dot_general(Bm, dS_b, nt, preferred_element_type=F32)
        dB_acc = jnp.dot((xdt_all * hd["tail_all"]).astype(BF16), dS_b, preferred_element_type=F32)
        dtail = _headsum(xdt_all * BdS_all, ET)
        da_cols = _headsum(dy_all * CS_all * hd["ea_all"], ET) - dtail * tail
        dss = _dot01_left(jnp.ones((8, N), F32), _dot01_left(hd["E"], dS_all * S_all, 2), 2, dims=nt)
        da_last = dss[0:1] * jnp.exp(hd["a_last"]) + jnp.sum(dtail * tail, axis=0, keepdims=True)
        rowi = lax.broadcasted_iota(jnp.int32, (Q, Kh), 0)
        da_cols = da_cols + jnp.where(rowi == Q - 1, da_last, 0.0)
        dstate[...] = hd["eL_b"] * dS_all + dS_y
        sum_mg = jnp.zeros((Q, Q), F32)
        ddt_x = jnp.zeros((Q, Kh), F32)
        da_rows = jnp.zeros((Kh, Q), F32)
        lane_k = lax.broadcasted_iota(jnp.int32, (Q, Kh), 1)
        sub_k = lax.broadcasted_iota(jnp.int32, (Kh, Q), 0)
        for ch in range(nch):
            cs = slice(ch * CW, (ch + 1) * CW)
            dyc = dyb[:, cs]
            xc_b = xdt_all[:, cs].astype(BF16)
            acc = hd["tail_all"][:, cs] * BdS_all[:, cs]
            for kk in range(hpc):
                k = ch * hpc + kk
                a_b = jnp.broadcast_to(acum[:, k:k + 1], (Q, Q))
                a_r = acumT[k:k + 1, :]
                decay = jnp.exp(jnp.where(causal, a_b - a_r, -jnp.inf))
                decayT = jnp.exp(jnp.where(causalT, a_r - a_b, -jnp.inf))
                dyk = dyc if hpc == 1 else jnp.where(_head_mask(Q, CW, kk), dyc, jnp.zeros_like(dyc))
                mg = decay * lax.dot_general(dyk, xc_b, nt, preferred_element_type=F32)
                sum_mg = sum_mg + mg
                w = mg * cb
                da_cols = da_cols + jnp.where(lane_k == k, jnp.sum(w, axis=1, keepdims=True), 0.0)
                da_rows = da_rows + jnp.where(sub_k == k, jnp.sum(w, axis=0, keepdims=True), 0.0)
                acc = acc + jnp.dot((decayT * cbT).astype(BF16), dyk, preferred_element_type=F32)
            dxs_ref[:, cs] = (acc * hd["dt_all"][:, cs] + dy_all[:, cs] * hd["dsk_all"][:, cs]).astype(BF16)
            ddt_x = ddt_x + _headsum(acc * xs[:, cs], ET[cs, :])
        eye_q = (row == col).astype(F32)
        da_cols = da_cols - _dot01_left(eye_q, da_rows, 3, dims=nt)
        dD_row = jnp.sum(_headsum(dy_all * xs, ET), axis=0, keepdims=True)
        sum_mg_b = sum_mg.astype(BF16)
        dB_ref[...] = (dB_acc + lax.dot_general(sum_mg_b, Cm, tn, preferred_element_type=F32)).astype(BF16)
        dC_ref[...] = (dC_acc + jnp.dot(sum_mg_b, Bm, preferred_element_type=F32)).astype(BF16)
        triu = (row <= col).astype(F32)
        ddtA = _dot01_left(triu, da_cols, 3)
        ddt = ddt_x + ddtA * A
        dpre = ddt * _sigmoid(dtp_ref[...] + bias_ref[...])
        ddtp_ref[...] = dpre
        dbias_v = jnp.sum(dpre, axis=0, keepdims=True)
        dalog_v = jnp.sum(ddtA * dt, axis=0, keepdims=True) * A

        @pl.when(ci == 0)
        def _():
            dbias_ref[...] = dbias_v
            dalog_ref[...] = dalog_v
            dD_ref[...] = dD_row
            dng_ref[...] = dng_v

        @pl.when(ci > 0)
        def _():
            dbias_ref[...] += dbias_v
            dalog_ref[...] += dalog_v
            dD_ref[...] += dD_row
            dng_ref[...] += dng_v

    vec_o = pl.BlockSpec((None, 1, Kh), lambda g, c: (g, 0, 0))
    tile = pl.BlockSpec((Q, GP), lambda g, c: (rev(c), g))
    return pl.pallas_call(
        body, grid=(G, nc),
        in_specs=_ssd_in_specs(Q, GP, N, Kh, DI, rev)
        + [pl.BlockSpec((None, None, GP, N), lambda g, c: (rev(c), g, 0, 0)), tile, tile, tile,
           pl.BlockSpec((1, GP), lambda g, c: (0, g))],
        out_specs=[tile, pl.BlockSpec((Q, N), lambda g, c: (rev(c), g)), pl.BlockSpec((Q, N), lambda g, c: (rev(c), g)),
                   pl.BlockSpec((None, Q, Kh), lambda g, c: (g, rev(c), 0)), vec_o, vec_o, vec_o,
                   tile, pl.BlockSpec((1, GP), lambda g, c: (0, g))],
        out_shape=[jax.ShapeDtypeStruct((L, DI), BF16), jax.ShapeDtypeStruct((L, G * N), BF16),
                   jax.ShapeDtypeStruct((L, G * N), BF16), jax.ShapeDtypeStruct((G, L, Kh), F32)]
        + [jax.ShapeDtypeStruct((G, 1, Kh), F32)] * 3
        + [jax.ShapeDtypeStruct(zx.shape, BF16), jax.ShapeDtypeStruct((1, DI), F32)],
        scratch_shapes=[pltpu.VMEM((GP, N), F32)], compiler_params=_cp(("parallel", "arbitrary")),
        name="ssd_bwd")(xbc, xbc, xbc, dtp_g, dtp_gT, bias_g, bias_gT, alog_g, alog_gT, dsk_g, dsk_gT, states,
                        dyn, y, zx, norm_g)


def _alibi_slope(gi, h):
    n = len(DIL_PATTERNS) * DIL_H
    return float(2.0 ** (-8.0 * (gi * DIL_H + h + 1) / n))


def _attn_masks():
    qi = lax.broadcasted_iota(jnp.int32, (DIL_BLK, DIL_BLK), 0)
    kj = lax.broadcasted_iota(jnp.int32, (DIL_BLK, DIL_BLK), 1)
    dcur = (qi - kj).astype(F32)
    return dcur, qi >= kj, dcur + float(DIL_BLK), kj >= qi


def attn_fwd(q3, kv3, gi):
    window, d = DIL_PATTERNS[gi]
    assert window // d == DIL_BLK
    HW = DIL_H * DIL_E
    M = q3.shape[1]
    nb = M // DIL_BLK
    scale = DIL_E ** -0.5
    nt = (((1,), (1,)), ((), ()))

    def body(q_ref, kp_ref, kc_ref, vp_ref, vc_ref, o_ref, lse_ref):
        n = pl.program_id(1)
        dcur, vcur, dprev, vprev0 = _attn_masks()
        dist = jnp.concatenate([dprev, dcur], axis=1)
        valid = jnp.concatenate([jnp.logical_and(vprev0, n > 0), vcur], axis=1)
        lane = lax.broadcasted_iota(jnp.int32, (DIL_BLK, 128), 1)
        lse_acc = jnp.zeros((DIL_BLK, 128), F32)
        for h in range(DIL_H):
            hs = slice(h * DIL_E, (h + 1) * DIL_E)
            sl = _alibi_slope(gi, h) * d
            kcat = jnp.concatenate([kp_ref[:, hs], kc_ref[:, hs]], axis=0)
            vcat = jnp.concatenate([vp_ref[:, hs], vc_ref[:, hs]], axis=0)
            s = lax.dot_general(q_ref[:, hs], kcat, nt, preferred_element_type=F32) * scale - sl * dist
            s = jnp.where(valid, s, -jnp.inf)
            m = jnp.max(s, axis=-1, keepdims=True)
            p = jnp.exp(s - m)
            den = jnp.sum(p, axis=-1, keepdims=True)
            o = jnp.dot(p.astype(BF16), vcat, preferred_element_type=F32) / den
            o_ref[:, hs] = o.astype(BF16)
            lse_acc = jnp.where(lane == h, m + jnp.log(den), lse_acc)
        lse_ref[...] = lse_acc

    blk = (None, DIL_BLK, HW)
    prev = lambda n: jnp.maximum(n - 1, 0)
    return pl.pallas_call(
        body, grid=(d, nb),
        in_specs=[pl.BlockSpec(blk, lambda r, n: (r, n, 0)),
                  pl.BlockSpec(blk, lambda r, n: (r, prev(n), 0)), pl.BlockSpec(blk, lambda r, n: (r, n, 0)),
                  pl.BlockSpec(blk, lambda r, n: (r, prev(n), 1)), pl.BlockSpec(blk, lambda r, n: (r, n, 1))],
        out_specs=[pl.BlockSpec(blk, lambda r, n: (r, n, 0)), pl.BlockSpec((None, DIL_BLK, 128), lambda r, n: (r, n, 0))],
        out_shape=[jax.ShapeDtypeStruct((d, M, HW), BF16), jax.ShapeDtypeStruct((d, M, 128), F32)],
        compiler_params=_cp(("parallel", "parallel")), name=f"attn_fwd_{gi}")(q3, kv3, kv3, kv3, kv3)


def attn_bwd(q3, kv3, do3, lse3, dpr3, gi):
    window, d = DIL_PATTERNS[gi]
    HW = DIL_H * DIL_E
    M = q3.shape[1]
    L = M * d
    nb = M // DIL_BLK
    scale = DIL_E ** -0.5
    nt = (((1,), (1,)), ((), ()))
    tn = (((0,), (0,)), ((), ()))

    def body(q0_ref, q1_ref, k_ref, v_ref, do0_ref, do1_ref, l0_ref, l1_ref, r0_ref, r1_ref,
             dq_ref, dk_ref, dv_ref, carry):
        n = pl.program_id(1)

        @pl.when(n == 0)
        def _():
            carry[...] = jnp.zeros(carry.shape, F32)

        dcur, vcur, dprev, vprev0 = _attn_masks()
        dist = jnp.concatenate([dcur, dprev], axis=0)
        valid = jnp.concatenate([vcur, jnp.logical_and(vprev0, n < nb - 1)], axis=0)
        B = DIL_BLK
        for h in range(DIL_H):
            hs = slice(h * DIL_E, (h + 1) * DIL_E)
            sl = _alibi_slope(gi, h) * d
            kh = k_ref[:, hs]
            vh = v_ref[:, hs]
            qcat = jnp.concatenate([q0_ref[:, hs], q1_ref[:, hs]], axis=0)
            docat = jnp.concatenate([do0_ref[:, hs], do1_ref[:, hs]], axis=0)
            lcat = jnp.concatenate([l0_ref[:, h:h + 1], l1_ref[:, h:h + 1]], axis=0)
            rcat = jnp.concatenate([r0_ref[:, h:h + 1], r1_ref[:, h:h + 1]], axis=0)
            s = lax.dot_general(qcat, kh, nt, preferred_element_type=F32) * scale - sl * dist
            p = jnp.exp(jnp.where(valid, s - lcat, -jnp.inf))
            ds = p * (lax.dot_general(docat, vh, nt, preferred_element_type=F32) - rcat)
            ds_b = (ds * scale).astype(BF16)
            dv_ref[:, hs] = lax.dot_general(p.astype(BF16), docat, tn, preferred_element_type=F32).astype(BF16)
            dk_ref[:, hs] = lax.dot_general(ds_b, qcat, tn, preferred_element_type=F32).astype(BF16)
            dqc = jnp.dot(ds_b, kh, preferred_element_type=F32)
            dq_ref[:, hs] = (carry[:, hs] + dqc[:B]).astype(BF16)
            carry[:, hs] = dqc[B:]

    blk = (None, DIL_BLK, HW)
    sblk = (None, DIL_BLK, 128)
    oblk = (DIL_BLK, HW)
    nxt = lambda n: jnp.minimum(n + 1, nb - 1)
    here = lambda c: (lambda r, n: (r, n, c))
    ahead = lambda c: (lambda r, n: (r, nxt(n), c))
    outs = pl.pallas_call(
        body, grid=(d, nb),
        in_specs=[pl.BlockSpec(blk, here(0)), pl.BlockSpec(blk, ahead(0)),
                  pl.BlockSpec(blk, here(0)), pl.BlockSpec(blk, here(1)),
                  pl.BlockSpec(blk, here(0)), pl.BlockSpec(blk, ahead(0)),
                  pl.BlockSpec(sblk, here(0)), pl.BlockSpec(sblk, ahead(0)),
                  pl.BlockSpec(sblk, here(0)), pl.BlockSpec(sblk, ahead(0))],
        out_specs=[pl.BlockSpec(oblk, lambda r, n: (n, r))] * 3,
        out_shape=[jax.ShapeDtypeStruct((M, d * HW), BF16)] * 3,
        scratch_shapes=[pltpu.VMEM(oblk, F32)], compiler_params=_cp(("parallel", "arbitrary")),
        name=f"attn_bwd_{gi}")(q3, q3, kv3, kv3, do3, do3, lse3, lse3, dpr3, dpr3)
    return [t.reshape(L, HW) for t in outs]


def _merge_weights(l_tiles, h):
    ls = [t[:, h:h + 1] for t in l_tiles]
    mx = functools.reduce(jnp.maximum, ls)
    es = [jnp.exp(l - mx) for l in ls]
    den = functools.reduce(lambda a, b: a + b, es)
    return [e / den for e in es]


def _dil_specs(tr, arrs):
    return [pl.BlockSpec((a.shape[0], tr // a.shape[0], a.shape[2]), lambda i: (0, i, 0)) for a in arrs]


def _dil_scratch(tr, arrs):
    return [pltpu.VMEM((a.shape[2] // 128, tr, 128), F32) for a in arrs if a.shape[0] > 1]


def _undilate(refs3, scrs, tr):
    out, k = [], 0
    for ref in refs3:
        d, _, W = ref.shape
        if d == 1:
            out.append(lambda c, ref=ref: ref[0, :, c * 128:(c + 1) * 128])
            continue
        scr = scrs[k]
        k += 1
        for r in range(d):
            for c in range(W // 128):
                scr.at[c][pl.ds(r, tr // d, stride=d), :] = ref[r, :, c * 128:(c + 1) * 128].astype(F32)
        out.append(lambda c, scr=scr: scr[c])
    return out


def merge_fwd(os3, lses3, z):
    HW = os3[0].shape[2]
    L = os3[0].shape[0] * os3[0].shape[1]
    tr = _tile(L, 256, 16)
    ng = len(os3)
    n_scr = len(_dil_scratch(tr, os3))

    def body(*refs):
        z_ref, out_ref = refs[2 * ng], refs[2 * ng + 1]
        scrs = refs[2 * ng + 2:]
        o_get = _undilate(refs[:ng], scrs[:n_scr], tr)
        l_tiles = [g(0) for g in _undilate(refs[ng:2 * ng], scrs[n_scr:], tr)]
        for h in range(DIL_H):
            hs = slice(h * DIL_E, (h + 1) * DIL_E)
            ws = _merge_weights(l_tiles, h)
            om = functools.reduce(lambda a, b: a + b, [w * o(h).astype(F32) for w, o in zip(ws, o_get)])
            out_ref[:, hs] = (om * _silu(z_ref[:, hs].astype(F32))).astype(BF16)

    return pl.pallas_call(
        body, grid=(L // tr,),
        in_specs=_dil_specs(tr, os3) + _dil_specs(tr, lses3) + _row_specs(tr, [HW]),
        out_specs=_row_specs(tr, [HW])[0], out_shape=jax.ShapeDtypeStruct((L, HW), BF16),
        scratch_shapes=_dil_scratch(tr, os3) + _dil_scratch(tr, lses3),
        compiler_params=_cp(("parallel",)), name="merge_fwd")(*os3, *lses3, z)


def merge_bwd(dgated, os3, lses3, z):
    HW = os3[0].shape[2]
    L = os3[0].shape[0] * os3[0].shape[1]
    tr = _tile(L, 256, 16)
    ng = len(os3)
    n_scr = len(_dil_scratch(tr, os3))

    def body(*refs):
        dg_ref = refs[0]
        z_ref = refs[1 + 2 * ng]
        outs = refs[2 + 2 * ng:2 + 2 * ng + 2 * ng + 1]
        scrs = refs[2 + 2 * ng + 2 * ng + 1:]
        do_out, dpr_out, dz_ref = outs[:ng], outs[ng:2 * ng], outs[2 * ng]
        o_get = _undilate(refs[1:1 + ng], scrs[:n_scr], tr)
        l_tiles = [g(0) for g in _undilate(refs[1 + ng:1 + 2 * ng], scrs[n_scr:2 * n_scr], tr)]
        stage = scrs[2 * n_scr:]
        do_stage, dpr_stage, k = [], [], 0
        for g in range(ng):
            if do_out[g].shape[0] == 1:
                do_stage.append(None)
                dpr_stage.append(None)
            else:
                do_stage.append(stage[2 * k])
                dpr_stage.append(stage[2 * k + 1])
                k += 1
        lane = lax.broadcasted_iota(jnp.int32, (tr, 128), 1)
        accs = [jnp.zeros((tr, 128), F32) for _ in range(ng)]
        for h in range(DIL_H):
            hs = slice(h * DIL_E, (h + 1) * DIL_E)
            ws = _merge_weights(l_tiles, h)
            ov = [o(h).astype(F32) for o in o_get]
            om = functools.reduce(lambda a, b: a + b, [w * o for w, o in zip(ws, ov)])
            zv = z_ref[:, hs].astype(F32)
            dgv = dg_ref[:, hs].astype(F32)
            dom = dgv * _silu(zv)
            dz_ref[:, hs] = (dgv * om * _dsilu(zv)).astype(BF16)
            dws = [jnp.sum(dom * o, axis=-1, keepdims=True) for o in ov]
            dwbar = functools.reduce(lambda a, b: a + b, [w * dw for w, dw in zip(ws, dws)])
            for g in range(ng):
                if do_stage[g] is None:
                    do_out[g][0, :, hs] = (ws[g] * dom).astype(BF16)
                else:
                    do_stage[g][h] = ws[g] * dom
                accs[g] = jnp.where(lane == h, ws[g] * dwbar, accs[g])
        for g in range(ng):
            d = do_out[g].shape[0]
            if d == 1:
                dpr_out[g][0] = accs[g]
                continue
            dpr_stage[g][0] = accs[g]
            for r in range(d):
                dpr_out[g][r] = dpr_stage[g].at[0][pl.ds(r, tr // d, stride=d), :]
                for c in range(HW // 128):
                    do_out[g][r, :, c * 128:(c + 1) * 128] = do_stage[g].at[c][pl.ds(r, tr // d, stride=d), :].astype(BF16)

    stage_shapes = []
    for o3 in os3:
        if o3.shape[0] > 1:
            stage_shapes += [pltpu.VMEM((HW // 128, tr, 128), F32), pltpu.VMEM((1, tr, 128), F32)]
    outs = pl.pallas_call(
        body, grid=(L // tr,),
        in_specs=_row_specs(tr, [HW]) + _dil_specs(tr, os3) + _dil_specs(tr, lses3) + _row_specs(tr, [HW]),
        out_specs=_dil_specs(tr, os3) + _dil_specs(tr, lses3) + _row_specs(tr, [HW]),
        out_shape=[jax.ShapeDtypeStruct(o.shape, BF16) for o in os3] + [jax.ShapeDtypeStruct(l.shape, F32) for l in lses3]
        + [jax.ShapeDtypeStruct((L, HW), BF16)],
        scratch_shapes=_dil_scratch(tr, os3) + _dil_scratch(tr, lses3) + stage_shapes,
        compiler_params=_cp(("parallel",)), name="merge_bwd")(dgated, *os3, *lses3, z)
    return outs[:ng], outs[ng:2 * ng], outs[2 * ng]


def ada_fwd(c8, ada_w):
    nl, D, Ws = ada_w.shape
    tn = _tile(Ws, 512)

    def body(c_ref, w_ref, o_ref):
        o_ref[...] = jnp.dot(_silu(c_ref[...]), w_ref[...], precision=lax.Precision.HIGHEST,
                             preferred_element_type=F32)

    return pl.pallas_call(
        body, grid=(nl, Ws // tn),
        in_specs=[pl.BlockSpec((N_DEV, D), lambda l, j: (0, 0)), pl.BlockSpec((None, D, tn), lambda l, j: (l, 0, j))],
        out_specs=pl.BlockSpec((None, N_DEV, tn), lambda l, j: (l, 0, j)),
        out_shape=jax.ShapeDtypeStruct((nl, N_DEV, Ws), F32), compiler_params=_cp(("parallel", "parallel")),
        name="ada_fwd")(c8, ada_w)


def ada_wgrad(c8t, dmod):
    nl, _, Ws = dmod.shape
    D = c8t.shape[0]
    tm = _tile(D, 512, 8)

    def body(c_ref, d_ref, o_ref):
        sc = _silu(c_ref[...])
        acc = sc[:, 0:1] * d_ref[0:1, :]
        for e in range(1, N_DEV):
            acc = acc + sc[:, e:e + 1] * d_ref[e:e + 1, :]
        o_ref[...] = acc

    return pl.pallas_call(
        body, grid=(nl, D // tm),
        in_specs=[pl.BlockSpec((tm, N_DEV), lambda l, i: (i, 0)), pl.BlockSpec((None, N_DEV, Ws), lambda l, i: (l, 0, 0))],
        out_specs=pl.BlockSpec((None, tm, Ws), lambda l, i: (l, i, 0)),
        out_shape=jax.ShapeDtypeStruct((nl, D, Ws), F32), compiler_params=_cp(("parallel", "parallel")),
        name="ada_wgrad")(c8t, dmod)


def adamw(w, g, m, v, name):
    R, C = w.shape
    tr = _tile(R, 256, 8)
    c1 = 1.0 - ADAM_B1 ** ADAM_STEP
    c2 = 1.0 - ADAM_B2 ** ADAM_STEP

    def body(w_ref, g_ref, m_ref, v_ref, d_ref, nm_ref, nv_ref):
        gv = g_ref[...]
        nm = ADAM_B1 * m_ref[...] + (1.0 - ADAM_B1) * gv
        nv = ADAM_B2 * v_ref[...] + (1.0 - ADAM_B2) * (gv * gv)
        nm_ref[...] = nm
        nv_ref[...] = nv
        d_ref[...] = -ADAM_LR * ((nm / c1) / (jnp.sqrt(nv / c2) + ADAM_EPS) + ADAM_WD * w_ref[...])

    return pl.pallas_call(
        body, grid=(R // tr,), in_specs=_row_specs(tr, [C] * 4), out_specs=_row_specs(tr, [C] * 3),
        out_shape=[jax.ShapeDtypeStruct((R, C), F32)] * 3, compiler_params=_cp(("parallel",)), name=name)(w, g, m, v)


def sum_leading(t, name, out_dtype=F32):
    S, R, C = t.shape
    tr = _tile(R, 256, 16)

    def body(t_ref, o_ref):
        acc = t_ref[0].astype(F32)
        for s in range(1, S):
            acc = acc + t_ref[s].astype(F32)
        o_ref[...] = acc.astype(out_dtype)

    return pl.pallas_call(
        body, grid=(R // tr,), in_specs=[pl.BlockSpec((S, tr, C), lambda i: (0, i, 0))],
        out_specs=pl.BlockSpec((tr, C), lambda i: (i, 0)), out_shape=jax.ShapeDtypeStruct((R, C), out_dtype),
        compiler_params=_cp(("parallel",)), name=name)(t)


def add_half(g, a, core, name, by_cols=False):
    S, R, C = g.shape

    def body(core_ref, g_ref, a_ref, o_ref):
        o_ref[...] = (g_ref[...].astype(F32) + a_ref[...].astype(F32)).astype(BF16)

    if by_cols:
        hc = C // 2
        tr = _tile(R, 512, 16)
        return pl.pallas_call(
            body,
            grid_spec=pltpu.PrefetchScalarGridSpec(
                num_scalar_prefetch=1, grid=(S, R // tr),
                in_specs=[pl.BlockSpec((None, tr, hc), lambda s, i, core_ref: (s, i, core_ref[0])),
                          pl.BlockSpec((None, tr, hc), lambda s, i, core_ref: (s, i, 0))],
                out_specs=pl.BlockSpec((None, tr, hc), lambda s, i, core_ref: (s, i, 0))),
            out_shape=jax.ShapeDtypeStruct((S, R, hc), BF16), compiler_params=_cp(("parallel", "parallel")),
            name=name)(core, g, a)
    h = R // 2
    tr = _tile(h, 256, 16)
    nb = h // tr

    return pl.pallas_call(
        body,
        grid_spec=pltpu.PrefetchScalarGridSpec(
            num_scalar_prefetch=1, grid=(S, nb),
            in_specs=[pl.BlockSpec((None, tr, C), lambda s, i, core_ref: (s, core_ref[0] * nb + i, 0)),
                      pl.BlockSpec((None, tr, C), lambda s, i, core_ref: (s, i, 0))],
            out_specs=pl.BlockSpec((None, tr, C), lambda s, i, core_ref: (s, i, 0))),
        out_shape=jax.ShapeDtypeStruct((S, h, C), BF16), compiler_params=_cp(("parallel", "parallel")),
        name=name)(core, g, a)


def sum_partials(own, landed, chip, name):
    _, h, C = own.shape
    tr = _tile(h, 512, 16)

    def body(chip_ref, own_ref, l_ref, o_ref):
        acc = own_ref[...].astype(F32)
        for j in range(3):
            acc = acc + l_ref[j].astype(F32)
        o_ref[...] = acc

    return pl.pallas_call(
        body,
        grid_spec=pltpu.PrefetchScalarGridSpec(
            num_scalar_prefetch=1, grid=(h // tr,),
            in_specs=[pl.BlockSpec((None, tr, C), lambda i, chip_ref: (chip_ref[0], i, 0)),
                      pl.BlockSpec((3, tr, C), lambda i, chip_ref: (0, i, 0))],
            out_specs=pl.BlockSpec((tr, C), lambda i, chip_ref: (i, 0))),
        out_shape=jax.ShapeDtypeStruct((h, C), F32), compiler_params=_cp(("parallel",)), name=name)(chip, own, landed)


def adamw_halves(w, g_mine, g_theirs, m, v, core, name, after=None):
    R, C = w.shape
    h = R // 2
    tr = _tile(h, 256, 8)
    nbh = h // tr
    c1 = 1.0 - ADAM_B1 ** ADAM_STEP
    c2 = 1.0 - ADAM_B2 ** ADAM_STEP
    extra = [] if after is None else [after]

    def body(core_ref, w_ref, gm_ref, gt_ref, m_ref, v_ref, *rest):
        g_ref, d_ref, nm_ref, nv_ref = rest[len(extra):]
        mine = (pl.program_id(0) // nbh) == core_ref[0]
        gv = jnp.where(mine, gm_ref[...], gt_ref[...])
        g_ref[...] = gv
        nm = ADAM_B1 * m_ref[...] + (1.0 - ADAM_B1) * gv
        nv = ADAM_B2 * v_ref[...] + (1.0 - ADAM_B2) * (gv * gv)
        nm_ref[...] = nm
        nv_ref[...] = nv
        d_ref[...] = -ADAM_LR * ((nm / c1) / (jnp.sqrt(nv / c2) + ADAM_EPS) + ADAM_WD * w_ref[...])

    full = pl.BlockSpec((tr, C), lambda i, core_ref: (i, 0))
    halfspec = pl.BlockSpec((tr, C), lambda i, core_ref: (i % nbh, 0))
    return pl.pallas_call(
        body,
        grid_spec=pltpu.PrefetchScalarGridSpec(
            num_scalar_prefetch=1, grid=(2 * nbh,),
            in_specs=[full, halfspec, halfspec, full, full] + [_ANY] * len(extra), out_specs=[full] * 4),
        out_shape=[jax.ShapeDtypeStruct((R, C), F32)] * 4, compiler_params=_cp(("parallel",)),
        name=name)(core, w, g_mine, g_theirs, m, v, *extra)


_ANY = pl.BlockSpec(memory_space=pl.ANY)


def _place():
    x, y, c = lax.axis_index("x"), lax.axis_index("y"), lax.axis_index("c")
    chips = [(1 - x, y), (x, 1 - y), (1 - x, 1 - y)]
    return x, y, c, chips


def allgather_small(v, name):
    R, W = v.shape

    def body(x_ref, out_ref, send_sems, recv_sems, local_sem):
        x, y, c, chips = _place()
        me, sibling = (x, y, c), (x, y, 1 - c)

        def rows(px, py, pc):
            return out_ref.at[pl.ds((4 * px + 2 * py + pc) * R, R), :]

        def copy(k, block, to, src=None):
            return pltpu.make_async_remote_copy(
                src_ref=rows(*block) if src is None else src, dst_ref=rows(*block),
                send_sem=send_sems.at[k], recv_sem=recv_sems.at[k], device_id=to, device_id_type=MESH)

        mine = pltpu.make_async_copy(x_ref, rows(*me), local_sem)
        mine.start()
        first = [copy(0, me, sibling, src=x_ref)]
        first += [copy(1 + j, me, (*chip, c), src=x_ref) for j, chip in enumerate(chips)]
        for cp in first:
            cp.start()
        passed = [copy(4 + j, (*chip, c), sibling) for j, chip in enumerate(chips)]
        for j, chip in enumerate(chips):
            copy(1 + j, (*chip, c), me).wait_recv()
            passed[j].start()
        copy(0, sibling, me).wait_recv()
        for j, chip in enumerate(chips):
            copy(4 + j, (*chip, 1 - c), me).wait_recv()
        for cp in first + passed:
            cp.wait_send()
        mine.wait()

    return pl.pallas_call(
        body, out_shape=jax.ShapeDtypeStruct((N_DEV * R, W), v.dtype),
        in_specs=[pl.BlockSpec(memory_space=pltpu.VMEM)], out_specs=pl.BlockSpec(memory_space=pltpu.VMEM),
        scratch_shapes=[pltpu.SemaphoreType.DMA((7,)), pltpu.SemaphoreType.DMA((7,)), pltpu.SemaphoreType.DMA],
        name=name)(v)


def allgather_routed(shard, name):
    R, C = shard.shape
    hc = C // 2
    ra = (R // 2) // 16 * 16

    def body(in_ref, out_ref, send_sems, recv_sems):
        x, y, c, _ = _place()
        xn, yn = (1 - x, y, c), (x, 1 - y, c)
        sibling = (x, y, 1 - c)
        p, pxn, pyn, pdg = 2 * x + y, 2 * (1 - x) + y, 2 * x + (1 - y), 2 * (1 - x) + (1 - y)
        rows_a, rows_b, rows_all = pl.ds(0, ra), pl.ds(ra, R - ra), pl.ds(0, R)

        def win(ref, rows, core):
            return ref.at[rows, pl.ds(pl.multiple_of(core * hc, 128), hc)]

        def copy(k, chip_id, rows, core, to, src=None):
            blk = win(out_ref.at[chip_id], rows, core)
            return pltpu.make_async_remote_copy(
                src_ref=blk if src is None else src, dst_ref=blk, send_sem=send_sems.at[k], recv_sem=recv_sems.at[k],
                device_id=to, device_id_type=MESH)

        own = [copy(0, p, rows_a, c, xn, src=win(in_ref, rows_a, c)), copy(1, p, rows_b, c, xn, src=win(in_ref, rows_b, c)),
               copy(2, p, rows_b, c, yn, src=win(in_ref, rows_b, c)), copy(3, p, rows_a, c, yn, src=win(in_ref, rows_a, c))]
        for cp in own:
            cp.start()
        copy(0, pxn, rows_a, c, xn).wait_recv()
        fwd_a = copy(4, pxn, rows_a, c, yn)
        fwd_a.start()
        copy(2, pyn, rows_b, c, yn).wait_recv()
        fwd_b = copy(5, pyn, rows_b, c, xn)
        fwd_b.start()
        copy(1, pxn, rows_b, c, xn).wait_recv()
        copy(3, pyn, rows_a, c, yn).wait_recv()
        passed = [copy(6, pxn, rows_all, c, sibling), copy(7, pyn, rows_all, c, sibling)]
        for cp in passed:
            cp.start()
        copy(4, pdg, rows_a, c, yn).wait_recv()
        passed.append(copy(8, pdg, rows_a, c, sibling))
        passed[-1].start()
        copy(5, pdg, rows_b, c, xn).wait_recv()
        passed.append(copy(9, pdg, rows_b, c, sibling))
        passed[-1].start()
        for k, (chip_id, rows) in enumerate([(pxn, rows_all), (pyn, rows_all), (pdg, rows_a), (pdg, rows_b)]):
            copy(6 + k, chip_id, rows, 1 - c, sibling).wait_recv()
        for cp in own + [fwd_a, fwd_b] + passed:
            cp.wait_send()

    out = pl.pallas_call(
        body, out_shape=jax.ShapeDtypeStruct((N_CHIPS, R, C), shard.dtype), in_specs=[_ANY], out_specs=_ANY,
        scratch_shapes=[pltpu.SemaphoreType.DMA((10,)), pltpu.SemaphoreType.DMA((10,))], name=name)(shard)
    chip = 2 * lax.axis_index("x") + lax.axis_index("y")
    return lax.dynamic_update_index_in_dim(out, shard, chip, 0)


_HBM = pl.BlockSpec(memory_space=pltpu.HBM)
_SEM = pl.BlockSpec(memory_space=pltpu.SEMAPHORE)
_EFFECT = pltpu.SideEffectType.DATAFLOW_SIDE_EFFECTING


def _chip_copies(kind, srcs, lands, send_sems, recv_sems):
    x, y, c, chips = _place()
    p = 2 * x + y
    cps = []
    if kind == "join":
        return [pltpu.make_async_remote_copy(
            src_ref=srcs[i], dst_ref=lands[i], send_sem=send_sems.at[3 * i], recv_sem=recv_sems.at[3 * i],
            device_id=(x, y, 1 - c), device_id_type=MESH) for i in range(len(srcs))]
    if kind == "sibling":
        for i in range(len(srcs)):
            h = srcs[i].shape[1] // 2
            cps.append(pltpu.make_async_remote_copy(
                src_ref=srcs[i].at[:, pl.ds((1 - c) * h, h), :], dst_ref=lands[i], send_sem=send_sems.at[3 * i],
                recv_sem=recv_sems.at[3 * i], device_id=(x, y, 1 - c), device_id_type=MESH))
        return cps
    for i in range(len(srcs)):
        for j, (cx, cy) in enumerate(chips):
            if kind == "gather":
                src, dst = srcs[i].at[c], lands[i].at[p, c]
            else:
                src, dst = srcs[i].at[2 * cx + cy], lands[i].at[j]
            cps.append(pltpu.make_async_remote_copy(
                src_ref=src, dst_ref=dst, send_sem=send_sems.at[3 * i + j], recv_sem=recv_sems.at[3 * i + j],
                device_id=(cx, cy, c), device_id_type=MESH))
    return cps


def split_start(kind, srcs, land_shapes, after, name, land_dtype=BF16):
    n = len(srcs)

    def body(*refs):
        src_refs, land_refs = refs[:n], refs[n:2 * n]
        send_sems, recv_sems = refs[2 * n + 1], refs[2 * n + 2]
        token = refs[-1]
        for cp in _chip_copies(kind, src_refs, land_refs, send_sems, recv_sems):
            cp.start()
        token[...] = jnp.zeros_like(token)

    lands = [pltpu.with_memory_space_constraint(lax.empty(s, land_dtype), pltpu.HBM) for s in land_shapes]
    outs = pl.pallas_call(
        body, name=name,
        out_shape=(pltpu.SemaphoreType.DMA((3 * n,)), pltpu.SemaphoreType.DMA((3 * n,)),
                   *[pltpu.HBM(s.shape, s.dtype) for s in srcs], *[pltpu.HBM(s, land_dtype) for s in land_shapes],
                   jax.ShapeDtypeStruct((8, 128), F32)),
        in_specs=[_HBM] * (2 * n) + [_ANY],
        out_specs=(_SEM, _SEM, *([_HBM] * (2 * n)), pl.BlockSpec(memory_space=pltpu.VMEM)),
        input_output_aliases={i: 2 + i for i in range(2 * n)},
        compiler_params=pltpu.CompilerParams(has_side_effects=_EFFECT),
    )(*[pltpu.with_memory_space_constraint(s, pltpu.HBM) for s in srcs], *lands, after)
    return outs[0], outs[1], outs[2:2 + n], outs[2 + n:2 + 2 * n], outs[-1]


def split_wait(kind, send_sems, recv_sems, srcs, lands, after, name):
    n = len(srcs)

    def body(*refs):
        src_refs, land_refs = refs[:n], refs[n:2 * n]
        ssem, rsem = refs[2 * n], refs[2 * n + 1]
        for cp in _chip_copies(kind, src_refs, land_refs, ssem, rsem):
            cp.wait_send()
            cp.wait_recv()

    outs = pl.pallas_call(
        body, name=name,
        out_shape=[pltpu.HBM(s.shape, s.dtype) for s in srcs] + [pltpu.HBM(s.shape, s.dtype) for s in lands],
        in_specs=[_HBM] * (2 * n) + [_SEM, _SEM, _ANY], out_specs=[_HBM] * (2 * n),
        input_output_aliases={i: i for i in range(2 * n)},
        compiler_params=pltpu.CompilerParams(has_side_effects=_EFFECT),
    )(*srcs, *lands, send_sems, recv_sems, after)
    return outs[:n], outs[n:]


def pass_to_sibling(lands):
    n = len(lands)

    def body(*refs):
        ins, outs = refs[:n], refs[n:2 * n]
        send_sems, recv_sems = refs[2 * n:]
        x, y, c, chips = _place()
        cps = []
        for i in range(n):
            for j, (cx, cy) in enumerate(chips):
                blk = outs[i].at[2 * cx + cy, c]
                cps.append(pltpu.make_async_remote_copy(
                    src_ref=ins[i].at[2 * cx + cy, c], dst_ref=blk, send_sem=send_sems.at[3 * i + j],
                    recv_sem=recv_sems.at[3 * i + j], device_id=(x, y, 1 - c), device_id_type=MESH))
        for cp in cps:
            cp.start()
        for cp in cps:
            cp.wait()

    return pl.pallas_call(
        body, out_shape=[jax.ShapeDtypeStruct(t.shape, t.dtype) for t in lands], in_specs=[_ANY] * n,
        out_specs=[_ANY] * n, input_output_aliases={i: i for i in range(n)},
        scratch_shapes=[pltpu.SemaphoreType.DMA((3 * n,)), pltpu.SemaphoreType.DMA((3 * n,))],
        name="ag_pass_to_sibling")(*lands)


def _all8_copies(src, land, send_sems, recv_sems):
    x, y, c, _ = _place()
    me = 4 * x + 2 * y + c
    cps = []
    for k, (fx, fy, fc) in enumerate([(a, b, d) for a in (0, 1) for b in (0, 1) for d in (0, 1)][1:]):
        peer = (1 - x if fx else x, 1 - y if fy else y, 1 - c if fc else c)
        cps.append(pltpu.make_async_remote_copy(
            src_ref=src, dst_ref=land.at[me], send_sem=send_sems.at[k], recv_sem=recv_sems.at[k],
            device_id=peer, device_id_type=MESH))
    return cps


def gather8_start(v, after, name):
    R, W = v.shape

    def body(v_ref, land_ref, aft_ref, send_sems, recv_sems, v_thru, land_thru, token):
        for cp in _all8_copies(v_ref, land_ref, send_sems, recv_sems):
            cp.start()
        token[...] = jnp.zeros_like(token)

    land = pltpu.with_memory_space_constraint(lax.empty((N_DEV, R, W), v.dtype), pltpu.HBM)
    outs = pl.pallas_call(
        body, name=name,
        out_shape=(pltpu.SemaphoreType.DMA((N_DEV - 1,)), pltpu.SemaphoreType.DMA((N_DEV - 1,)),
                   pltpu.HBM(v.shape, v.dtype), pltpu.HBM((N_DEV, R, W), v.dtype), jax.ShapeDtypeStruct((8, 128), F32)),
        in_specs=[_HBM, _HBM, _ANY],
        out_specs=(_SEM, _SEM, _HBM, _HBM, pl.BlockSpec(memory_space=pltpu.VMEM)),
        input_output_aliases={0: 2, 1: 3},
        compiler_params=pltpu.CompilerParams(has_side_effects=_EFFECT),
    )(pltpu.with_memory_space_constraint(v, pltpu.HBM), land, after)
    return outs


def gather8_wait(send_sems, recv_sems, v, land, after, name):
    def body(v_ref, land_ref, ssem, rsem, aft_ref, v_dead, land_out):
        for cp in _all8_copies(v_ref, land_ref, ssem, rsem):
            cp.wait_send()
            cp.wait_recv()

    return pl.pallas_call(
        body, name=name, out_shape=[pltpu.HBM(v.shape, v.dtype), pltpu.HBM(land.shape, land.dtype)],
        in_specs=[_HBM, _HBM, _SEM, _SEM, _ANY], out_specs=[_HBM, _HBM], input_output_aliases={0: 0, 1: 1},
        compiler_params=pltpu.CompilerParams(has_side_effects=_EFFECT),
    )(v, land, send_sems, recv_sems, after)[1]


def _pass_copies(bufs, send_sems, recv_sems):
    x, y, c, chips = _place()
    cps = []
    for i in range(len(bufs)):
        for j, (cx, cy) in enumerate(chips):
            blk = bufs[i].at[2 * cx + cy, c]
            cps.append(pltpu.make_async_remote_copy(
                src_ref=blk, dst_ref=blk, send_sem=send_sems.at[3 * i + j], recv_sem=recv_sems.at[3 * i + j],
                device_id=(x, y, 1 - c), device_id_type=MESH))
    return cps


def pass_start(bufs, after, name):
    n = len(bufs)

    def body(*refs):
        send_sems, recv_sems = refs[n + 1], refs[n + 2]
        for cp in _pass_copies(refs[:n], send_sems, recv_sems):
            cp.start()
        refs[-1][...] = jnp.zeros_like(refs[-1])

    outs = pl.pallas_call(
        body, name=name,
        out_shape=(pltpu.SemaphoreType.DMA((3 * n,)), pltpu.SemaphoreType.DMA((3 * n,)),
                   *[pltpu.HBM(b.shape, b.dtype) for b in bufs], jax.ShapeDtypeStruct((8, 128), F32)),
        in_specs=[_HBM] * n + [_ANY],
        out_specs=(_SEM, _SEM, *([_HBM] * n), pl.BlockSpec(memory_space=pltpu.VMEM)),
        input_output_aliases={i: 2 + i for i in range(n)},
        compiler_params=pltpu.CompilerParams(has_side_effects=_EFFECT),
    )(*[pltpu.with_memory_space_constraint(b, pltpu.HBM) for b in bufs], after)
    return outs[0], outs[1], outs[2:2 + n], outs[-1]


def pass_wait(send_sems, recv_sems, bufs, after, name):
    n = len(bufs)

    def body(*refs):
        for cp in _pass_copies(refs[:n], refs[n], refs[n + 1]):
            cp.wait_send()
            cp.wait_recv()

    return pl.pallas_call(
        body, name=name, out_shape=[pltpu.HBM(b.shape, b.dtype) for b in bufs],
        in_specs=[_HBM] * n + [_SEM, _SEM, _ANY], out_specs=[_HBM] * n,
        input_output_aliases={i: i for i in range(n)},
        compiler_params=pltpu.CompilerParams(has_side_effects=_EFFECT),
    )(*bufs, send_sems, recv_sems, after)


def exchange_halves_to_sibling(gs, name, by_cols=False):
    n = len(gs)

    def body(*refs):
        ins, outs = refs[:n], refs[n:2 * n]
        send_sems, recv_sems = refs[2 * n:]
        x, y, c, _ = _place()
        cps = []
        for i in range(n):
            if by_cols:
                hc = ins[i].shape[2] // 2
                src = ins[i].at[:, :, pl.ds(pl.multiple_of((1 - c) * hc, 128), hc)]
            else:
                h = ins[i].shape[1] // 2
                src = ins[i].at[:, pl.ds((1 - c) * h, h), :]
            cps.append(pltpu.make_async_remote_copy(
                src_ref=src, dst_ref=outs[i],
                send_sem=send_sems.at[i], recv_sem=recv_sems.at[i], device_id=(x, y, 1 - c), device_id_type=MESH))
        for cp in cps:
            cp.start()
        for cp in cps:
            cp.wait()

    halve = (lambda s: (s[0], s[1], s[2] // 2)) if by_cols else (lambda s: (s[0], s[1] // 2, s[2]))
    return pl.pallas_call(
        body, out_shape=[jax.ShapeDtypeStruct(halve(g.shape), g.dtype) for g in gs],
        in_specs=[_ANY] * n, out_specs=[_ANY] * n,
        scratch_shapes=[pltpu.SemaphoreType.DMA((n,)), pltpu.SemaphoreType.DMA((n,))],
        name=name)(*gs)


def _pack(parts, row_mult=8):
    flat = jnp.concatenate([p.reshape(-1).astype(F32) for p in parts])
    unit = row_mult * 128
    n = -(-flat.shape[0] // unit) * unit
    return jnp.pad(flat, (0, n - flat.shape[0])).reshape(n // 128, 128)


def _unpack(flat, shapes):
    out, off = [], 0
    for s in shapes:
        n = int(np.prod(s))
        out.append(flat[off:off + n].reshape(s))
        off += n
    return out


def _gather_packed(parts, name):
    packed = _pack(parts)
    g = allgather_small(packed, name).reshape(N_DEV, -1)
    return _unpack_rows(g, [p.shape for p in parts])


def _unpack_rows(g, shapes):
    out, off = [], 0
    for s in shapes:
        n = int(np.prod(s))
        out.append(g[:, off:off + n].reshape((g.shape[0],) + tuple(s)))
        off += n
    return out


def _by_chip(t, axis):
    return jnp.concatenate([t[2 * p] for p in range(N_CHIPS)], axis=axis)


def kernel(x, c, ada_w, ada_b, ln_g, ln_b, a_in_w, a_conv_w, a_conv_b, a_dt_bias, a_A_log, a_D, a_norm_g, a_out_w, kv_w, b_in_w, b_out_w, loss_target, m_ada_w, m_ada_b, m_ln_g, m_ln_b, m_a_in_w, m_a_conv_w, m_a_conv_b, m_a_dt_bias, m_a_A_log, m_a_D, m_a_norm_g, m_a_out_w, m_kv_w, m_b_in_w, m_b_out_w, v_ada_w, v_ada_b, v_ln_g, v_ln_b, v_a_in_w, v_a_conv_w, v_a_conv_b, v_a_dt_bias, v_a_A_log, v_a_D, v_a_norm_g, v_a_out_w, v_kv_w, v_b_in_w, v_b_out_w):
    ax, ay, ac = lax.axis_index("x"), lax.axis_index("y"), lax.axis_index("c")
    chip = 2 * ax + ay
    dev = 4 * ax + 2 * ay + ac
    xin = x[0]
    tgt = loss_target[0]
    L, D = xin.shape
    G, P = SSD_G, SSD_P
    H = a_dt_bias.shape[1]
    Kh = H // G
    DI = H * P
    CONVD = a_conv_b.shape[1] * N_CHIPS
    HW = DIL_H * DIL_E
    Ws = ada_w.shape[2]

    w_in_g = allgather_routed(jnp.transpose(a_in_w[0]).astype(BF16), "allgather_w_in")
    later = [a_out_w[0].astype(BF16), kv_w.astype(BF16), b_in_w[0].astype(BF16), b_out_w[0].astype(BF16)]
    later_split = [s.reshape(2, s.shape[0] // 2, s.shape[1]) for s in later]
    ag_ssem, ag_rsem, ag_srcs, ag_lands, ag_token = split_start(
        "gather", later_split, [(N_CHIPS,) + s.shape for s in later_split], w_in_g, "ag_later_start")
    w_in_t = w_in_g.reshape(-1, D)
    w_dt_t = jnp.pad(w_in_t[DI + CONVD:], ((0, 128 - H), (0, 0)))

    c8, cw8, cb8, ng8 = _gather_packed([c[0], a_conv_w[0], a_conv_b[0], a_norm_g[0]], "allgather_small_params")
    conv_w = _by_chip(cw8, 1)
    conv_b = _by_chip(cb8, 0).reshape(1, CONVD)
    norm_g = _by_chip(ng8, 0).reshape(1, DI)

    mod_s = ada_fwd(c8, ada_w)
    (mod8,) = _gather_packed([mod_s], "allgather_small_mod")
    mods = _by_chip(mod8, 2)
    mod = lax.dynamic_index_in_dim(mods, dev, axis=1, keepdims=False) + ada_b
    shift = [mod[l:l + 1, :D] for l in range(DEPTH)]
    scale = [mod[l:l + 1, D:2 * D] for l in range(DEPTH)]
    gate = [mod[l:l + 1, 2 * D:] for l in range(DEPTH)]
    lg = [ln_g[l:l + 1] for l in range(DEPTH)]
    lb = [ln_b[l:l + 1] for l in range(DEPTH)]

    h0 = modulate(xin, scale[0] + ag_token[0:1, 0:1], shift[0], "modulate0")
    zx = mm_nt(h0, w_in_t, BF16, "mm_in_zx", kw_rows=DI + CONVD)
    dtp = mm_nt(h0, w_dt_t, F32, "mm_in_dt")
    xbc = conv_fwd(zx, DI, conv_w, conv_b)
    dtp_g = jnp.transpose(dtp[:, :H].reshape(L, G, Kh), (1, 0, 2))
    dtp_gT = jnp.transpose(dtp_g, (0, 2, 1))
    vecs = [a_dt_bias.reshape(G, 1, Kh), a_dt_bias.reshape(G, Kh, 1), a_A_log.reshape(G, 1, Kh),
            a_A_log.reshape(G, Kh, 1), a_D.reshape(G, 1, Kh), a_D.reshape(G, Kh, 1)]
    y_ssd, states, yn = ssd_fwd(xbc, dtp_g, dtp_gT, *vecs, zx, norm_g, DI)
    later_split, ag_lands = split_wait("gather", ag_ssem, ag_rsem, ag_srcs, ag_lands, yn, "ag_later_wait")
    (land_out,) = pass_to_sibling(ag_lands[:1])
    ps_ssem, ps_rsem, lands_b, ps_token = pass_start(ag_lands[1:], land_out, "ag_pass_start")

    def place_own(o, s, full):
        return lax.dynamic_update_index_in_dim(o, s, chip, 0).reshape((N_CHIPS,) + full.shape)

    w_out_g = place_own(land_out, later_split[0], later[0])
    ymix0 = mm_nn(yn, w_out_g.reshape(-1, D), F32, "mm_out_a", after=ps_token)
    x1, x1b, h1 = ln_mid(xin, ymix0, gate[0], lg[0], lb[0], scale[1], shift[1])
    lands_b = pass_wait(ps_ssem, ps_rsem, lands_b, x1b, "ag_pass_wait")
    w_kv_g, w_bin_g, w_bout_g = [place_own(o, s, full) for o, s, full in zip(lands_b, later_split[1:], later[1:])]

    n_grp = len(DIL_PATTERNS)
    cb = HW // 512
    assert w_bin_g.shape[2] == HW
    kv3 = [mm_cols_dilated(x1b, w_kv_g, [g * cb + t for t in range(cb)] + [(n_grp + g) * cb + t for t in range(cb)],
                           DIL_PATTERNS[g][1], f"mm_kv_{g}") for g in range(n_grp)]
    q3 = [mm_cols_dilated(h1, w_bin_g, [g], DIL_PATTERNS[g][1], f"mm_q_{g}", tn=HW) for g in range(n_grp)]
    z_b = mm_nn(h1, w_bin_g[n_grp], BF16, "mm_z_b")
    os_, lses = [], []
    for gi in range(len(DIL_PATTERNS)):
        o, lse = attn_fwd(q3[gi], kv3[gi], gi)
        os_.append(o)
        lses.append(lse)
    om = merge_fwd(os_, lses, z_b)
    ymix1 = mm_nn(om, w_bout_g, F32, "mm_out_b", stack="col")
    dres2, dy2, dg1, db1, dgate1, sq = ln_final_fwd_bwd(x1, ymix1, gate[1], lg[1], lb[1], tgt)
    loss_part = 0.5 * jnp.sum(sq) / D

    g_bout = mm_tn(om, dy2, BF16, "mm_gw_out_b", stack="col")
    dgated = mm_nt(dy2, w_bout_g, BF16, "mm_gx_out_b", stack="col")
    dos, dprs, dz_b = merge_bwd(dgated, os_, lses, z_b)
    dqs, dks, dvs = [], [], []
    for gi in range(len(DIL_PATTERNS)):
        dq, dk, dv = attn_bwd(q3[gi], kv3[gi], dos[gi], lses[gi], dprs[gi], gi)
        dqs.append(dq)
        dks.append(dk)
        dvs.append(dv)
    dqz = jnp.concatenate(dqs + [dz_b], axis=1)
    dkv = jnp.concatenate(dks + dvs, axis=1)
    g_bin = mm_tn(h1, dqz, BF16, "mm_gw_in_b", stack="col")
    dh1 = mm_nt(dqz, w_bin_g, BF16, "mm_gx_in_b", stack="col")
    g_kv = mm_tn(x1b, dkv, BF16, "mm_gw_kv", stack="col")

    core = ac.astype(jnp.int32).reshape(1)
    chip_i = chip.astype(jnp.int32).reshape(1)

    def begin_exchange(gs, tag):
        shapes = [(g.shape[0], g.shape[1] // 2, g.shape[2]) for g in gs]
        return split_start("sibling", gs, shapes, gs[0], "rs_x%s_start" % tag)

    def begin_scatter(gs, nms, tag, exchange=None, after=None, by_cols=False):
        if exchange is None:
            sib = exchange_halves_to_sibling(gs, "rs_sibling_exchange_" + tag, by_cols=by_cols)
        else:
            gs, sib = split_wait("sibling", exchange[0], exchange[1], exchange[2], exchange[3], after,
                                 "rs_x%s_wait" % tag)
        parts = [add_half(g, a, core, "rs_add_" + nm, by_cols=by_cols) for g, a, nm in zip(gs, sib, nms)]
        return split_start("scatter", parts, [(3,) + t.shape[1:] for t in parts], parts[0], "rs_%s_start" % tag)

    def sum_scattered(handles, after, tag):
        nms, owns, landed = [], [], []
        for k, (handle, hn) in enumerate(handles):
            parts, lands = split_wait("scatter", handle[0], handle[1], handle[2], handle[3], after,
                                      "rs_%s%d_wait" % (tag, k))
            nms += hn
            owns += list(parts)
            landed += list(lands)
        return nms, [sum_partials(own, t, chip_i, "rs_sum_" + nm) for own, t, nm in zip(owns, landed, nms)]

    def begin_join(halves, tag):
        return split_start("join", halves, [t.shape for t in halves], halves[0], "rs_j%s_start" % tag, land_dtype=F32)

    def end_join(handle, after, tag):
        return split_wait("join", handle[0], handle[1], handle[2], handle[3], after, "rs_j%s_wait" % tag)

    names_b = ["kv", "in_b", "out_b"]
    ex_b = begin_exchange([g_kv, g_bin, g_bout], "b")
    dx1_kv = mm_nt(dkv, w_kv_g, BF16, "mm_gx_kv", stack="col", after=ex_b[4])
    rs_b = begin_scatter(None, names_b, "b", exchange=ex_b, after=dx1_kv)

    dres1, dy1, dg0, db0, dgate0, dscale1, dshift1 = mod_ln_bwd(
        dres2, dh1, dx1_kv, x1, scale[1], xin, ymix0, gate[0] + rs_b[4][0:1, 0:1], lg[0])
    g_out = mm_tn(yn, dy1, BF16, "mm_gw_out_a", stack="row")
    ex_a1 = begin_exchange([g_out], "a1")
    dyn = mm_nt(dy1, w_out_g, BF16, "mm_gx_out_a", stack="row", after=ex_a1[4])
    rs_a1 = begin_scatter(None, ["out_a"], "a1", exchange=ex_a1, after=dyn)
    dxs, dB, dC, ddtp_g, dbias_g, dalog_g, dD_g, dz_a, dnorm_g = ssd_bwd(
        xbc, dtp_g, dtp_gT, *vecs, states, dyn, y_ssd, zx, norm_g + rs_a1[4][0:1, 0:1], DI)
    dzx, dws, dbs, lo = dz_a, [], [], 0
    for tag, gpart in (("xs", dxs), ("b", dB), ("c", dC)):
        hi = lo + gpart.shape[1]
        dzx, dw_p, db_p = conv_bwd(zx, DI + lo, conv_w[:, lo:hi], conv_b[:, lo:hi], gpart, dzx, "conv_bwd_" + tag)
        dws.append(dw_p)
        dbs.append(db_p)
        lo = hi
    dconv_w = jnp.concatenate(dws, axis=1)
    dconv_b = jnp.concatenate(dbs, axis=1)
    ddtp = jnp.pad(jnp.transpose(ddtp_g, (1, 0, 2)).reshape(L, H), ((0, 0), (0, 128 - H)))
    g_inT = mm_tn(dzx, h0, BF16, "mm_gw_in_zx", m_rows=DI + CONVD + H)
    g_dtT = mm_tn(ddtp, h0, BF16, "mm_gw_in_dt")
    g_inT = lax.dynamic_update_slice(g_inT, g_dtT[:H], (DI + CONVD, 0))
    rs_a2 = begin_scatter([g_inT.reshape(N_CHIPS, -1, D)], ["in_a"], "a2", by_cols=True)
    dh0 = mm_nn(dzx, w_in_t, BF16, "mm_gx_in_zx", after=rs_a2[4])
    dh0_dt = mm_nn(ddtp, w_dt_t, F32, "mm_gx_in_dt")
    grad_x, dscale0, dshift0 = mod_bwd(dres1, dh0, dh0_dt, xin, scale[0] + rs_a2[4][0:1, 0:1], "mod_bwd0")
    nms_b, halves_b = sum_scattered([(rs_b, names_b)], grad_x, "b")
    join_b = begin_join(halves_b, "b")
    nms_a, halves_a = sum_scattered([(rs_a1, ["out_a"]), (rs_a2, ["in_a"])], join_b[4], "a")
    g_halves = dict(zip(nms_b, zip(*end_join(join_b, halves_a[0], "b"))))
    join_a = begin_join(halves_a, "a")

    def step_halves(w, m, v, nm, after=None):
        shp = w.shape
        mine, theirs_ = g_halves[nm]
        outs4 = adamw_halves(w.reshape(-1, shp[-1]), mine, theirs_, m.reshape(-1, shp[-1]), v.reshape(-1, shp[-1]),
                             core, "adamw_" + nm, after=after)
        return tuple(t.reshape(shp) for t in outs4)

    big = {
        "kv_w": step_halves(kv_w, m_kv_w, v_kv_w, "kv", after=join_a[4]),
        "b_in_w": step_halves(b_in_w, m_b_in_w, v_b_in_w, "in_b"),
        "b_out_w": step_halves(b_out_w, m_b_out_w, v_b_out_w, "out_b"),
    }
    g_halves.update(dict(zip(nms_a, zip(*end_join(join_a, big["kv_w"][1], "a")))))
    g_halves["in_a"] = tuple(jnp.transpose(t) for t in g_halves["in_a"])

    dmod = jnp.concatenate([jnp.concatenate([dshift0, dscale0, dgate0], axis=1),
                            jnp.concatenate([dshift1, dscale1, dgate1], axis=1)], axis=0)
    small_parts = [jnp.concatenate([dg0, dg1], axis=0), jnp.concatenate([db0, db1], axis=0),
                   dbias_g.reshape(1, H), dalog_g.reshape(1, H), dD_g.reshape(1, H),
                   dconv_w, dconv_b, dnorm_g, loss_part.reshape(1, 1)]
    small_shapes = [p.shape for p in small_parts]
    packed = jnp.concatenate([_pack([dmod]), _pack(small_parts)], axis=0)
    n_mod_rows = _pack([dmod]).shape[0]
    sg_ssem, sg_rsem, sg_src, sg_land, sg_token = gather8_start(packed, g_halves["in_a"][1], "small_grads_start")
    big["a_in_w"] = step_halves(a_in_w, m_a_in_w, v_a_in_w, "in_a", after=sg_token)
    big["a_out_w"] = step_halves(a_out_w, m_a_out_w, v_a_out_w, "out_a")
    sg_land = gather8_wait(sg_ssem, sg_rsem, sg_src, sg_land, big["a_in_w"][1], "small_grads_wait")
    gathered = lax.dynamic_update_index_in_dim(sg_land, packed, dev, 0)
    dmod8 = gathered[:, :n_mod_rows].reshape(N_DEV, -1)[:, :2 * 3 * D].reshape(N_DEV, DEPTH, 3 * D)
    summed = sum_leading(gathered, "sum_small")
    g_ada_b = summed[:n_mod_rows].reshape(-1)[:2 * 3 * D].reshape(DEPTH, 3 * D)
    (g_ln_g, g_ln_b, g_dt_bias, g_a_log, g_dsk, g_conv_w, g_conv_b, g_norm_g, loss_all) = _unpack(
        summed[n_mod_rows:].reshape(-1), small_shapes)
    loss = loss_all.reshape(())
    Cs = CONVD // N_CHIPS
    g_conv_w_s = lax.dynamic_slice_in_dim(g_conv_w, chip * Cs, Cs, axis=1)
    g_conv_b_s = lax.dynamic_slice_in_dim(g_conv_b, chip * Cs, Cs, axis=1)
    g_norm_g_s = lax.dynamic_slice_in_dim(g_norm_g, chip * (DI // N_CHIPS), DI // N_CHIPS, axis=1)
    dmod_s = jnp.transpose(lax.dynamic_slice_in_dim(dmod8, chip * Ws, Ws, axis=2), (1, 0, 2))

    def step2d(w, g, m, v, nm):
        shp = w.shape
        d_, m_, v_ = adamw(w.reshape(-1, shp[-1]), g.reshape(-1, shp[-1]), m.reshape(-1, shp[-1]),
                           v.reshape(-1, shp[-1]), "adamw_" + nm)
        return g.reshape(shp), d_.reshape(shp), m_.reshape(shp), v_.reshape(shp)

    big["ada_w"] = step2d(ada_w, ada_wgrad(jnp.transpose(c8), dmod_s), m_ada_w, v_ada_w, "ada_w")
    small_names = ["ada_b", "ln_g", "ln_b", "a_conv_w", "a_conv_b", "a_dt_bias", "a_A_log", "a_D", "a_norm_g"]
    small_w = [ada_b, ln_g, ln_b, a_conv_w, a_conv_b, a_dt_bias, a_A_log, a_D, a_norm_g]
    small_m = [m_ada_b, m_ln_g, m_ln_b, m_a_conv_w, m_a_conv_b, m_a_dt_bias, m_a_A_log, m_a_D, m_a_norm_g]
    small_v = [v_ada_b, v_ln_g, v_ln_b, v_a_conv_w, v_a_conv_b, v_a_dt_bias, v_a_A_log, v_a_D, v_a_norm_g]
    small_g = [g_ada_b, g_ln_g, g_ln_b, g_conv_w_s, g_conv_b_s, g_dt_bias, g_a_log, g_dsk, g_norm_g_s]
    shapes = [w.shape for w in small_w]
    small_g = [g.reshape(s) for g, s in zip(small_g, shapes)]
    d_p, m_p, v_p = adamw(_pack(small_w), _pack(small_g), _pack(small_m), _pack(small_v), "adamw_small")
    small = {}
    for nm, g, d_, m_, v_ in zip(small_names, small_g, _unpack(d_p.reshape(-1), shapes), _unpack(m_p.reshape(-1), shapes),
                                 _unpack(v_p.reshape(-1), shapes)):
        small[nm] = (g, d_, m_, v_)
    allw = {**big, **small}
    order = ["ada_w", "ada_b", "ln_g", "ln_b", "a_in_w", "a_conv_w", "a_conv_b", "a_dt_bias", "a_A_log", "a_D",
             "a_norm_g", "a_out_w", "kv_w", "b_in_w", "b_out_w"]
    outs = [loss, grad_x.reshape(x.shape)]
    for k in range(4):
        outs += [allw[n][k] for n in order]
    return tuple(outs)
```

```python
import functools

import jax
import jax.numpy as jnp
import numpy as np
from jax import lax
from jax.experimental import pallas as pl
from jax.experimental.pallas import tpu as pltpu

F32 = jnp.float32
BF16 = jnp.bfloat16
MESH = pl.DeviceIdType.MESH

DEPTH = 2
ALPHA = (2 * DEPTH) ** 0.25
LN_EPS = 1e-5
RMS_EPS = 1e-5
SSD_P = 64
SSD_N = 128
SSD_Q = 256
SSD_G = 8
CONV_W = 4
DIL_PATTERNS = ((128, 1), (512, 4), (2048, 16))
DIL_H = 8
DIL_E = 128
DIL_BLK = 128
ADAM_LR, ADAM_B1, ADAM_B2, ADAM_EPS, ADAM_WD, ADAM_STEP = 0.001, 0.9, 0.999, 1e-08, 0.01, 10

VMEM_LIMIT = 56 * 1024 * 1024
N_CHIPS = 4
N_DEV = 8


def _tile(dim, target, mult=128):
    if dim <= target:
        return dim
    t = (target // mult) * mult
    while t >= mult:
        if dim % t == 0:
            return t
        t -= mult
    return dim


def _cp(sem):
    return pltpu.CompilerParams(dimension_semantics=sem, vmem_limit_bytes=VMEM_LIMIT)


def _sigmoid(x):
    return 1.0 / (1.0 + jnp.exp(-x))


def _silu(x):
    return x * _sigmoid(x)


def _dsilu(x):
    s = _sigmoid(x)
    return s * (1.0 + x * (1.0 - s))


def _softplus(x):
    return jnp.maximum(x, 0.0) + jnp.log(1.0 + jnp.exp(-jnp.abs(x)))


def _mm_call(a, b, out_shape, grid, a_spec, b_spec, o_spec, acc_shape, dims, name, after=None):
    nk = grid[2]
    extra = [] if after is None else [after]

    def prod(a_ref, b_ref):
        return lax.dot_general(a_ref[...].astype(BF16), b_ref[...].astype(BF16), (dims, ((), ())),
                               preferred_element_type=F32)

    def body_single(a_ref, b_ref, *rest):
        o_ref = rest[len(extra)]
        o_ref[...] = prod(a_ref, b_ref).astype(o_ref.dtype)

    def body_multi(a_ref, b_ref, *rest):
        o_ref, acc_ref = rest[len(extra):]
        k = pl.program_id(2)

        @pl.when(k == 0)
        def _():
            acc_ref[...] = prod(a_ref, b_ref)

        @pl.when(jnp.logical_and(k > 0, k < nk - 1))
        def _():
            acc_ref[...] += prod(a_ref, b_ref)

        @pl.when(k == nk - 1)
        def _():
            o_ref[...] = (acc_ref[...] + prod(a_ref, b_ref)).astype(o_ref.dtype)

    return pl.pallas_call(
        body_single if nk == 1 else body_multi, grid=grid, in_specs=[a_spec, b_spec] + [_ANY] * len(extra),
        out_specs=o_spec, out_shape=out_shape, scratch_shapes=[] if nk == 1 else [pltpu.VMEM(acc_shape, F32)],
        compiler_params=_cp(("parallel", "parallel", "arbitrary")), name=name)(a, b, *extra)


def mm_nn(a, b, out_dtype, name, stack=None, tm=1024, tn=1024, tk=2048, n_cols=None, after=None):
    M, K = a.shape
    if stack is None:
        N = b.shape[1] if n_cols is None else n_cols
        tn, tk = _tile(N, tn), _tile(K, tk)
        b_spec = pl.BlockSpec((tk, tn), lambda i, j, k: (k, j))
    elif stack == "col":
        S, _, Ns = b.shape
        N = S * Ns
        tn, tk = _tile(Ns, tn), _tile(K, tk)
        npb = Ns // tn
        b_spec = pl.BlockSpec((None, tk, tn), lambda i, j, k: (j // npb, k, j % npb))
    else:
        S, Ks, N = b.shape
        tn, tk = _tile(N, tn), _tile(Ks, tk)
        kpb = Ks // tk
        b_spec = pl.BlockSpec((None, tk, tn), lambda i, j, k: (k // kpb, k % kpb, j))
    tm = _tile(M, tm)
    return _mm_call(a, b, jax.ShapeDtypeStruct((M, N), out_dtype), (M // tm, N // tn, K // tk),
                    pl.BlockSpec((tm, tk), lambda i, j, k: (i, k)), b_spec,
                    pl.BlockSpec((tm, tn), lambda i, j, k: (i, j)), (tm, tn), ((1,), (0,)), name, after=after)


def mm_cols_dilated(a, b, gcols, d, name, tm=1024, tn=512):
    L, K = a.shape
    S, _, Ns = b.shape
    tm, tn = _tile(L, tm), _tile(Ns, tn)
    npb = Ns // tn
    nj = len(gcols)
    rows = tm // d

    def body(cols_ref, a_ref, b_ref, o_ref, *scr):
        prod = jnp.dot(a_ref[...], b_ref[...], preferred_element_type=F32)
        if d == 1:
            o_ref[0] = prod.astype(BF16)
        else:
            for c in range(tn // 128):
                scr[0][c] = prod[:, c * 128:(c + 1) * 128]
            for r in range(d):
                for c in range(tn // 128):
                    o_ref[r, :, c * 128:(c + 1) * 128] = scr[0].at[c][pl.ds(r, rows, stride=d), :].astype(BF16)

    return pl.pallas_call(
        body,
        grid_spec=pltpu.PrefetchScalarGridSpec(
            num_scalar_prefetch=1, grid=(L // tm, nj),
            in_specs=[pl.BlockSpec((tm, K), lambda i, j, c: (i, 0)),
                      pl.BlockSpec((None, K, tn), lambda i, j, c: (c[j] // npb, 0, c[j] % npb))],
            out_specs=pl.BlockSpec((d, rows, tn), lambda i, j, c: (0, i, j)),
            scratch_shapes=[] if d == 1 else [pltpu.VMEM((tn // 128, tm, 128), F32)]),
        out_shape=jax.ShapeDtypeStruct((d, L // d, nj * tn), BF16),
        compiler_params=_cp(("parallel", "arbitrary")), name=name)(jnp.asarray(gcols, jnp.int32), a, b)


def mm_nt(a, b, out_dtype, name, stack=None, tm=1024, tn=1024, tk=2048, after=None, kw_rows=None):
    M, C = a.shape
    if stack is None:
        Kw = b.shape[0] if kw_rows is None else kw_rows
        tn, tk = _tile(Kw, tn), _tile(C, tk)
        b_spec = pl.BlockSpec((tn, tk), lambda i, j, k: (j, k))
    elif stack == "col":
        S, Kw, Cs = b.shape
        tn, tk = _tile(Kw, tn), _tile(Cs, tk)
        cpb = Cs // tk
        b_spec = pl.BlockSpec((None, tn, tk), lambda i, j, k: (k // cpb, j, k % cpb))
    else:
        S, Ks, _ = b.shape
        Kw = S * Ks
        tn, tk = _tile(Ks, tn), _tile(C, tk)
        jpb = Ks // tn
        b_spec = pl.BlockSpec((None, tn, tk), lambda i, j, k: (j // jpb, j % jpb, k))
    tm = _tile(M, tm)
    return _mm_call(a, b, jax.ShapeDtypeStruct((M, Kw), out_dtype), (M // tm, Kw // tn, C // tk),
                    pl.BlockSpec((tm, tk), lambda i, j, k: (i, k)), b_spec,
                    pl.BlockSpec((tm, tn), lambda i, j, k: (i, j)), (tm, tn), ((1,), (1,)), name, after=after)


def mm_tn(a, b, out_dtype, name, stack=None, n_stack=N_CHIPS, tm=1024, tn=1024, tk=2048, m_rows=None):
    L, M = a.shape
    N = b.shape[1]
    tk = _tile(L, tk)
    if stack is None:
        tm, tn = _tile(M, tm), _tile(N, tn)
        o_spec = pl.BlockSpec((tm, tn), lambda i, j, k: (i, j))
        out_shape = (M if m_rows is None else m_rows, N)
    elif stack == "col":
        Ns = N // n_stack
        tm, tn = _tile(M, tm), _tile(Ns, tn)
        npb = Ns // tn
        o_spec = pl.BlockSpec((None, tm, tn), lambda i, j, k: (j // npb, i, j % npb))
        out_shape = (n_stack, M, Ns)
    else:
        Ms = M // n_stack
        tm, tn = _tile(Ms, tm), _tile(N, tn)
        mpb = Ms // tm
        o_spec = pl.BlockSpec((None, tm, tn), lambda i, j, k: (i // mpb, i % mpb, j))
        out_shape = (n_stack, Ms, N)
    return _mm_call(a, b, jax.ShapeDtypeStruct(out_shape, out_dtype), (M // tm, N // tn, L // tk),
                    pl.BlockSpec((tk, tm), lambda i, j, k: (k, i)), pl.BlockSpec((tk, tn), lambda i, j, k: (k, j)),
                    o_spec, (tm, tn), ((0,), (0,)), name)


def _row_specs(tr, widths):
    return [pl.BlockSpec((tr, w), lambda i: (i, 0)) for w in widths]


def _vec_spec(w):
    return pl.BlockSpec((1, w), lambda i: (0, 0))


def _acc_rows(ref, val, i):
    s = jnp.sum(val, axis=0, keepdims=True)

    @pl.when(i == 0)
    def _():
        ref[...] = s

    @pl.when(i > 0)
    def _():
        ref[...] += s


def modulate(x, scale, shift, name):
    L, D = x.shape
    tr = _tile(L, 256, 16)

    def body(x_ref, sc_ref, sh_ref, h_ref):
        h_ref[...] = (x_ref[...] * (1.0 + sc_ref[...]) + sh_ref[...]).astype(BF16)

    return pl.pallas_call(
        body, grid=(L // tr,), in_specs=_row_specs(tr, [D]) + [_vec_spec(D)] * 2, out_specs=_row_specs(tr, [D])[0],
        out_shape=jax.ShapeDtypeStruct((L, D), BF16), compiler_params=_cp(("parallel",)), name=name)(x, scale, shift)


def _ln_core(x, y, gate, g, b):
    u = ALPHA * x + (1.0 + gate) * y
    mu = jnp.mean(u, axis=-1, keepdims=True)
    d = u - mu
    var = jnp.mean(d * d, axis=-1, keepdims=True)
    rstd = lax.rsqrt(var + LN_EPS)
    xhat = d * rstd
    return xhat * g + b, xhat, rstd


def ln_mid(x, y, gate, g, b, scale, shift):
    L, D = x.shape
    tr = _tile(L, 256, 16)

    def body(x_ref, y_ref, gate_ref, g_ref, b_ref, sc_ref, sh_ref, x1_ref, x1b_ref, h_ref):
        x1, _, _ = _ln_core(x_ref[...], y_ref[...], gate_ref[...], g_ref[...], b_ref[...])
        x1_ref[...] = x1
        x1b_ref[...] = x1.astype(BF16)
        h_ref[...] = (x1 * (1.0 + sc_ref[...]) + sh_ref[...]).astype(BF16)

    return pl.pallas_call(
        body, grid=(L // tr,), in_specs=_row_specs(tr, [D, D]) + [_vec_spec(D)] * 5,
        out_specs=_row_specs(tr, [D, D, D]),
        out_shape=[jax.ShapeDtypeStruct((L, D), F32), jax.ShapeDtypeStruct((L, D), BF16),
                   jax.ShapeDtypeStruct((L, D), BF16)],
        compiler_params=_cp(("parallel",)), name="ln_mid")(x, y, gate, g, b, scale, shift)


def _ln_bwd_rows(dout_v, xhat, rstd, g):
    dxh = dout_v * g
    m1 = jnp.mean(dxh, axis=-1, keepdims=True)
    m2 = jnp.mean(dxh * xhat, axis=-1, keepdims=True)
    return rstd * (dxh - m1 - xhat * m2)


def ln_final_fwd_bwd(x, y, gate, g, b, target):
    L, D = x.shape
    tr = _tile(L, 256, 16)

    def body(x_ref, y_ref, gate_ref, g_ref, b_ref, t_ref, dres_ref, dy_ref, dg_ref, db_ref, dgate_ref, sq_ref):
        i = pl.program_id(0)
        yv = y_ref[...]
        out, xhat, rstd = _ln_core(x_ref[...], yv, gate_ref[...], g_ref[...], b_ref[...])
        err = out - t_ref[...]
        dout_v = err * (1.0 / D)
        du = _ln_bwd_rows(dout_v, xhat, rstd, g_ref[...])
        dres_ref[...] = ALPHA * du
        dy_ref[...] = ((1.0 + gate_ref[...]) * du).astype(BF16)
        _acc_rows(dg_ref, dout_v * xhat, i)
        _acc_rows(db_ref, dout_v, i)
        _acc_rows(dgate_ref, du * yv, i)
        _acc_rows(sq_ref, err * err, i)

    return pl.pallas_call(
        body, grid=(L // tr,), in_specs=_row_specs(tr, [D, D]) + [_vec_spec(D)] * 3 + _row_specs(tr, [D]),
        out_specs=_row_specs(tr, [D, D]) + [_vec_spec(D)] * 4,
        out_shape=[jax.ShapeDtypeStruct((L, D), F32), jax.ShapeDtypeStruct((L, D), BF16)]
        + [jax.ShapeDtypeStruct((1, D), F32)] * 4,
        compiler_params=_cp(("arbitrary",)), name="ln_final_fwd_bwd")(x, y, gate, g, b, target)


def mod_bwd(dres, dh, dh2, xin, scale, name):
    L, D = xin.shape
    tr = _tile(L, 256, 16)

    def body(dres_ref, dh_ref, dh2_ref, x_ref, sc_ref, dx_ref, dsc_ref, dsh_ref):
        i = pl.program_id(0)
        dh_v = dh_ref[...].astype(F32) + dh2_ref[...].astype(F32)
        dx_ref[...] = dres_ref[...] + dh_v * (1.0 + sc_ref[...])
        _acc_rows(dsc_ref, dh_v * x_ref[...], i)
        _acc_rows(dsh_ref, dh_v, i)

    return pl.pallas_call(
        body, grid=(L // tr,), in_specs=_row_specs(tr, [D, D, D, D]) + [_vec_spec(D)],
        out_specs=_row_specs(tr, [D]) + [_vec_spec(D)] * 2,
        out_shape=[jax.ShapeDtypeStruct((L, D), F32)] + [jax.ShapeDtypeStruct((1, D), F32)] * 2,
        compiler_params=_cp(("arbitrary",)), name=name)(dres, dh, dh2, xin, scale)


def mod_ln_bwd(dres_in, dh, dskip, xmid, scale, x, y, gate, g):
    L, D = x.shape
    tr = _tile(L, 256, 16)

    def body(dres_ref, dh_ref, dskip_ref, xm_ref, sc_ref, x_ref, y_ref, gate_ref, g_ref,
             dres_out, dy_ref, dg_ref, db_ref, dgate_ref, dsc_ref, dsh_ref):
        i = pl.program_id(0)
        dh_v = dh_ref[...].astype(F32)
        dout_v = dres_ref[...] + dskip_ref[...].astype(F32) + dh_v * (1.0 + sc_ref[...])
        _acc_rows(dsc_ref, dh_v * xm_ref[...], i)
        _acc_rows(dsh_ref, dh_v, i)
        yv = y_ref[...]
        _, xhat, rstd = _ln_core(x_ref[...], yv, gate_ref[...], g_ref[...], 0.0)
        du = _ln_bwd_rows(dout_v, xhat, rstd, g_ref[...])
        dres_out[...] = ALPHA * du
        dy_ref[...] = ((1.0 + gate_ref[...]) * du).astype(BF16)
        _acc_rows(dg_ref, dout_v * xhat, i)
        _acc_rows(db_ref, dout_v, i)
        _acc_rows(dgate_ref, du * yv, i)

    return pl.pallas_call(
        body, grid=(L // tr,),
        in_specs=_row_specs(tr, [D] * 4) + [_vec_spec(D)] + _row_specs(tr, [D, D]) + [_vec_spec(D)] * 2,
        out_specs=_row_specs(tr, [D, D]) + [_vec_spec(D)] * 5,
        out_shape=[jax.ShapeDtypeStruct((L, D), F32), jax.ShapeDtypeStruct((L, D), BF16)]
        + [jax.ShapeDtypeStruct((1, D), F32)] * 5,
        compiler_params=_cp(("arbitrary",)), name="mod_ln_bwd")(dres_in, dh, dskip, xmid, scale, x, y, gate, g)


CONV_HALO = 16


def _conv_rows(x_ref, i, tr, L):
    nblk = L // tr
    s = pl.multiple_of(i * tr, CONV_HALO)
    cur = x_ref[pl.ds(s, tr), :].astype(F32)
    sp = pl.multiple_of(jnp.maximum(i * tr - CONV_HALO, 0), CONV_HALO)
    sn = pl.multiple_of(jnp.minimum(i * tr + tr, L - CONV_HALO), CONV_HALO)
    prev = x_ref[pl.ds(sp, CONV_HALO), :].astype(F32) * (i > 0).astype(F32)
    nxt = x_ref[pl.ds(sn, CONV_HALO), :].astype(F32) * (i < nblk - 1).astype(F32)
    return jnp.concatenate([prev, cur, nxt], axis=0)


def _shift_rows(v, j):
    n = v.shape[0]
    return v if j % n == 0 else pltpu.roll(v, j % n, 0)


def _conv_taps(xe):
    return [_shift_rows(xe, CONV_W - 1 - k) for k in range(CONV_W)]


def _conv_eval(taps, w_ref, b_ref):
    c = b_ref[...] + w_ref[0:1, :] * taps[0]
    for k in range(1, CONV_W):
        c = c + w_ref[k:k + 1, :] * taps[k]
    return c


def conv_fwd(zx, col0, conv_w, conv_b):
    L = zx.shape[0]
    C = conv_w.shape[1]
    tc = _tile(C, 512)
    tr = _tile(L, 512, CONV_HALO)
    off = col0 // tc

    def body(x_ref, w_ref, b_ref, o_ref):
        i = pl.program_id(1)
        xe = _conv_rows(x_ref, i, tr, L)
        c = _conv_eval(_conv_taps(xe), w_ref, b_ref)[CONV_HALO:CONV_HALO + tr]
        o_ref[...] = _silu(c).astype(BF16)

    return pl.pallas_call(
        body, grid=(C // tc, L // tr),
        in_specs=[pl.BlockSpec((L, tc), lambda j, i: (0, off + j)), pl.BlockSpec((CONV_W, tc), lambda j, i: (0, j)),
                  pl.BlockSpec((1, tc), lambda j, i: (0, j))],
        out_specs=pl.BlockSpec((tr, tc), lambda j, i: (i, j)),
        out_shape=jax.ShapeDtypeStruct((L, C), BF16), compiler_params=_cp(("parallel", "arbitrary")),
        name="conv_fwd")(zx, conv_w, conv_b)


def conv_bwd(zx, col0, conv_w, conv_b, g, dzx, name):
    L = zx.shape[0]
    C = conv_w.shape[1]
    tc = _tile(C, 512)
    tr = _tile(L, 512, CONV_HALO)
    off = col0 // tc
    H = CONV_HALO

    def body(x_ref, g_ref, w_ref, b_ref, buf_ref, dx_ref, dw_ref, db_ref):
        i = pl.program_id(1)
        xe = _conv_rows(x_ref, i, tr, L)
        ge = _conv_rows(g_ref, i, tr, L)
        taps = _conv_taps(xe)
        dc = ge * _dsilu(_conv_eval(taps, w_ref, b_ref))
        dx = w_ref[CONV_W - 1:CONV_W, :] * dc
        for k in range(CONV_W - 1):
            dx = dx + w_ref[k:k + 1, :] * _shift_rows(dc, -(CONV_W - 1 - k))
        dx_ref[...] = dx[H:H + tr].astype(BF16)
        dcc = dc[H:H + tr]
        rows = [jnp.sum(dcc * taps[k][H:H + tr], axis=0, keepdims=True) for k in range(CONV_W)]
        dwv = jnp.concatenate(rows + [jnp.zeros((8 - CONV_W, tc), F32)], axis=0)
        dbv = jnp.sum(dcc, axis=0, keepdims=True)

        @pl.when(i == 0)
        def _():
            dw_ref[...] = dwv
            db_ref[...] = dbv

        @pl.when(i > 0)
        def _():
            dw_ref[...] += dwv
            db_ref[...] += dbv

    dx, dw, db = pl.pallas_call(
        body, grid=(C // tc, L // tr),
        in_specs=[pl.BlockSpec((L, tc), lambda j, i: (0, off + j)), pl.BlockSpec((L, tc), lambda j, i: (0, j)),
                  pl.BlockSpec((CONV_W, tc), lambda j, i: (0, j)), pl.BlockSpec((1, tc), lambda j, i: (0, j)), _ANY],
        out_specs=[pl.BlockSpec((tr, tc), lambda j, i: (i, off + j)), pl.BlockSpec((8, tc), lambda j, i: (0, j)),
                   pl.BlockSpec((1, tc), lambda j, i: (0, j))],
        out_shape=[jax.ShapeDtypeStruct(dzx.shape, BF16), jax.ShapeDtypeStruct((8, C), F32),
                   jax.ShapeDtypeStruct((1, C), F32)],
        input_output_aliases={4: 0},
        compiler_params=_cp(("parallel", "arbitrary")), name=name)(zx, g, conv_w, conv_b, dzx)
    return dx, dw[:CONV_W], db


_NN = (((1,), (0,)), ((), ()))


def _pieces(x, n):
    out, r = [], x
    for _ in range(n):
        p = r.astype(BF16)
        out.append(p)
        r = r - p.astype(F32)
    return out


def _dot01(a, b01, n, dims=_NN):
    b = b01.astype(BF16)
    return functools.reduce(lambda u, v: u + v,
                            [lax.dot_general(p, b, dims, preferred_element_type=F32) for p in _pieces(a, n)])


def _dot01_left(a01, b, n, dims=_NN):
    a = a01.astype(BF16)
    return functools.reduce(lambda u, v: u + v,
                            [lax.dot_general(a, p, dims, preferred_element_type=F32) for p in _pieces(b, n)])


def _ssd_common(dtp_ref, dtpT_ref, bias_ref, biasT_ref, alog_ref, alogT_ref, b_ref, c_ref):
    Q = SSD_Q
    dt = _softplus(dtp_ref[...] + bias_ref[...])
    A = -jnp.exp(alog_ref[...])
    row = lax.broadcasted_iota(jnp.int32, (Q, Q), 0)
    col = lax.broadcasted_iota(jnp.int32, (Q, Q), 1)
    causal = row >= col
    tril = causal.astype(F32)
    Kh = dt.shape[1]
    acum = _dot01_left(tril, dt * A, 3)
    eye = (lax.broadcasted_iota(jnp.int32, (Kh, Kh), 0) == lax.broadcasted_iota(jnp.int32, (Kh, Kh), 1)).astype(F32)
    acumT = _dot01_left(eye, acum, 3, dims=(((1,), (1,)), ((), ())))
    Bm = b_ref[...]
    Cm = c_ref[...]
    cb = lax.dot_general(Cm, Bm, (((1,), (1,)), ((), ())), preferred_element_type=F32)
    return dt, A, causal, row, col, acum, acumT, Bm, Cm, cb


def _ssd_in_specs(Q, GP, N, Kh, DI, cmap):
    nb0 = DI // N
    vec = pl.BlockSpec((None, 1, Kh), lambda g, c: (g, 0, 0))
    vecT = pl.BlockSpec((None, Kh, 1), lambda g, c: (g, 0, 0))
    return [pl.BlockSpec((Q, GP), lambda g, c: (cmap(c), g)),
            pl.BlockSpec((Q, N), lambda g, c: (cmap(c), nb0 + g)),
            pl.BlockSpec((Q, N), lambda g, c: (cmap(c), nb0 + SSD_G + g)),
            pl.BlockSpec((None, Q, Kh), lambda g, c: (g, cmap(c), 0)),
            pl.BlockSpec((None, Kh, Q), lambda g, c: (g, 0, cmap(c))),
            vec, vecT, vec, vecT, vec, vecT]


def _hi(a, b01):
    return _dot01(a, b01, 2)


def _headsum(a, b01):
    return _dot01(a, b01, 1)


def _ssd_heads(dskT_ref, acum, acumT, dt, Kh):
    Q, P, N = SSD_Q, SSD_P, SSD_N
    GP = Kh * P
    sh_p = P.bit_length() - 1
    seg = lambda shape, dim: lax.shift_right_logical(lax.broadcasted_iota(jnp.int32, shape, dim), sh_p)
    E = (seg((Kh, GP), 1) == lax.broadcasted_iota(jnp.int32, (Kh, GP), 0)).astype(F32)
    ET = (seg((GP, Kh), 0) == lax.broadcasted_iota(jnp.int32, (GP, Kh), 1)).astype(F32)
    a_last = acum[Q - 1:Q, :]
    tail = jnp.exp(a_last - acum)
    eLT = jnp.exp(acumT[:, Q - 1:Q])
    rowseg = seg((GP, N), 0)
    eL_b = jnp.zeros((GP, N), F32)
    for k in range(Kh):
        eL_b = jnp.where(rowseg == k, eLT[k:k + 1, :], eL_b)
    return dict(
        E=E, ET=ET, a_last=a_last, tail=tail, eL_b=eL_b,
        dt_all=_hi(dt, E), ea_all=_headsum(jnp.exp(acum), E), tail_all=_headsum(tail, E),
        dsk_all=jnp.sum(E * dskT_ref[...], axis=0, keepdims=True))


def _head_chunks(GP):
    CW = min(GP, 128)
    return CW, CW // SSD_P, GP // CW


def _head_mask(Q, CW, kk):
    lane = lax.broadcasted_iota(jnp.int32, (Q, CW), 1)
    return jnp.logical_and(lane >= kk * SSD_P, lane < (kk + 1) * SSD_P)


def ssd_fwd(xbc, dtp_g, dtp_gT, bias_g, bias_gT, alog_g, alog_gT, dsk_g, dsk_gT, zx, norm_g, DI):
    L = xbc.shape[0]
    Q, P, N, G = SSD_Q, SSD_P, SSD_N, SSD_G
    GP = DI // G
    Kh = GP // P
    nc = L // Q

    CW, hpc, nch = _head_chunks(GP)
    nt = (((1,), (1,)), ((), ()))
    tn = (((0,), (0,)), ((), ()))

    def body(xs_ref, b_ref, c_ref, dtp_ref, dtpT_ref, bias_ref, biasT_ref, alog_ref, alogT_ref, dsk_ref, dskT_ref,
             z_ref, ng_ref, y_ref, st_ref, yn_ref, state):
        @pl.when(pl.program_id(1) == 0)
        def _():
            state[...] = jnp.zeros(state.shape, F32)

        st_ref[...] = state[...]
        dt, A, causal, row, col, acum, acumT, Bm, Cm, cb = _ssd_common(
            dtp_ref, dtpT_ref, bias_ref, biasT_ref, alog_ref, alogT_ref, b_ref, c_ref)
        hd = _ssd_heads(dskT_ref, acum, acumT, dt, Kh)
        xs = xs_ref[...].astype(F32)
        xdt_all = xs * hd["dt_all"]
        S_all = state[...]
        y_all = (lax.dot_general(Cm, S_all.astype(BF16), nt, preferred_element_type=F32) * hd["ea_all"]
                 + xs * hd["dsk_all"])
        state[...] = S_all * hd["eL_b"] + lax.dot_general(
            (xdt_all * hd["tail_all"]).astype(BF16), Bm, tn, preferred_element_type=F32)
        for ch in range(nch):
            cs = slice(ch * CW, (ch + 1) * CW)
            xc = xdt_all[:, cs]
            acc = y_all[:, cs]
            for kk in range(hpc):
                k = ch * hpc + kk
                decay = jnp.exp(jnp.where(causal, acum[:, k:k + 1] - acumT[k:k + 1, :], -jnp.inf))
                xk = xc if hpc == 1 else jnp.where(_head_mask(Q, CW, kk), xc, 0.0)
                acc = acc + jnp.dot((cb * decay).astype(BF16), xk.astype(BF16), preferred_element_type=F32)
            y_ref[:, cs] = acc.astype(BF16)
        y2 = y_ref[...].astype(F32) * _silu(z_ref[...].astype(F32))
        rr = lax.rsqrt(jnp.mean(y2 * y2, axis=-1, keepdims=True) + RMS_EPS)
        yn_ref[...] = (y2 * rr * ng_ref[...]).astype(BF16)

    tile = pl.BlockSpec((Q, GP), lambda g, c: (c, g))
    return pl.pallas_call(
        body, grid=(G, nc),
        in_specs=_ssd_in_specs(Q, GP, N, Kh, DI, lambda c: c) + [tile, pl.BlockSpec((1, GP), lambda g, c: (0, g))],
        out_specs=[tile, pl.BlockSpec((None, None, GP, N), lambda g, c: (c, g, 0, 0)), tile],
        out_shape=[jax.ShapeDtypeStruct((L, DI), BF16), jax.ShapeDtypeStruct((nc, G, GP, N), F32),
                   jax.ShapeDtypeStruct((L, DI), BF16)],
        scratch_shapes=[pltpu.VMEM((GP, N), F32)], compiler_params=_cp(("parallel", "arbitrary")),
        name="ssd_fwd")(xbc, xbc, xbc, dtp_g, dtp_gT, bias_g, bias_gT, alog_g, alog_gT, dsk_g, dsk_gT, zx, norm_g)


def ssd_bwd(xbc, dtp_g, dtp_gT, bias_g, bias_gT, alog_g, alog_gT, dsk_g, dsk_gT, states, dyn, y, zx, norm_g, DI):
    L = xbc.shape[0]
    Q, P, N, G = SSD_Q, SSD_P, SSD_N, SSD_G
    GP = DI // G
    Kh = GP // P
    nc = L // Q
    rev = lambda c: nc - 1 - c

    CW, hpc, nch = _head_chunks(GP)

    def body(xs_ref, b_ref, c_ref, dtp_ref, dtpT_ref, bias_ref, biasT_ref, alog_ref, alogT_ref, dsk_ref, dskT_ref,
             st_ref, dyn_ref, y_ref, z_ref, ng_ref,
             dxs_ref, dB_ref, dC_ref, ddtp_ref, dbias_ref, dalog_ref, dD_ref, dz_ref, dng_ref, dstate):
        ci = pl.program_id(1)

        @pl.when(ci == 0)
        def _():
            dstate[...] = jnp.zeros(dstate.shape, F32)

        dt, A, causal, row, col, acum, acumT, Bm, Cm, cb = _ssd_common(
            dtp_ref, dtpT_ref, bias_ref, biasT_ref, alog_ref, alogT_ref, b_ref, c_ref)
        tn = (((0,), (0,)), ((), ()))
        nt = (((1,), (1,)), ((), ()))
        hd = _ssd_heads(dskT_ref, acum, acumT, dt, Kh)
        ET, tail = hd["ET"], hd["tail"]
        cbT = lax.dot_general(Bm, Cm, nt, preferred_element_type=F32)
        causalT = row <= col
        xs = xs_ref[...].astype(F32)
        xdt_all = xs * hd["dt_all"]
        yv = y_ref[...].astype(F32)
        zv = z_ref[...].astype(F32)
        dynv = dyn_ref[...].astype(F32)
        sz = _silu(zv)
        y2 = yv * sz
        rr = lax.rsqrt(jnp.mean(y2 * y2, axis=-1, keepdims=True) + RMS_EPS)
        yh = y2 * rr
        dyh = dynv * ng_ref[...]
        dy2 = rr * (dyh - yh * jnp.mean(dyh * yh, axis=-1, keepdims=True))
        dz_ref[...] = (dy2 * yv * _dsilu(zv)).astype(BF16)
        dng_v = jnp.sum(dynv * yh, axis=0, keepdims=True)
        dyb = (dy2 * sz).astype(BF16)
        dy_all = dyb.astype(F32)
        S_all = st_ref[...]
        S_b = S_all.astype(BF16)
        dS_all = dstate[...]
        dS_b = dS_all.astype(BF16)
        CS_all = lax.dot_general(Cm, S_b, nt, preferred_element_type=F32)
        dyE_b = (dy_all * hd["ea_all"]).astype(BF16)
        dC_acc = jnp.dot(dyE_b, S_b, preferred_element_type=F32)
        dS_y = lax.dot_general(dyE_b, Cm, tn, preferred_element_type=F32)
        BdS_all = lax.dot_general(Bm, dS_b, nt, preferred_element_type=F32)
        dB_acc = jnp.dot((xdt_all * hd["tail_all"]).astype(BF16), dS_b, preferred_element_type=F32)
        dtail = _headsum(xdt_all * BdS_all, ET)
        da_cols = _headsum(dy_all * CS_all * hd["ea_all"], ET) - dtail * tail
        dss = _dot01_left(jnp.ones((8, N), F32), _dot01_left(hd["E"], dS_all * S_all, 2), 2, dims=nt)
        da_last = dss[0:1] * jnp.exp(hd["a_last"]) + jnp.sum(dtail * tail, axis=0, keepdims=True)
        rowi = lax.broadcasted_iota(jnp.int32, (Q, Kh), 0)
        da_cols = da_cols + jnp.where(rowi == Q - 1, da_last, 0.0)
        dstate[...] = hd["eL_b"] * dS_all + dS_y
        sum_mg = jnp.zeros((Q, Q), F32)
        ddt_x = jnp.zeros((Q, Kh), F32)
        da_rows = jnp.zeros((Kh, Q), F32)
        lane_k = lax.broadcasted_iota(jnp.int32, (Q, Kh), 1)
        sub_k = lax.broadcasted_iota(jnp.int32, (Kh, Q), 0)
        for ch in range(nch):
            cs = slice(ch * CW, (ch + 1) * CW)
            dyc = dyb[:, cs]
            xc_b = xdt_all[:, cs].astype(BF16)
            acc = hd["tail_all"][:, cs] * BdS_all[:, cs]
            for kk in range(hpc):
                k = ch * hpc + kk
                a_b = jnp.broadcast_to(acum[:, k:k + 1], (Q, Q))
                a_r = acumT[k:k + 1, :]
                decay = jnp.exp(jnp.where(causal, a_b - a_r, -jnp.inf))
                decayT = jnp.exp(jnp.where(causalT, a_r - a_b, -jnp.inf))
                dyk = dyc if hpc == 1 else jnp.where(_head_mask(Q, CW, kk), dyc, jnp.zeros_like(dyc))
                mg = decay * lax.dot_general(dyk, xc_b, nt, preferred_element_type=F32)
                sum_mg = sum_mg + mg
                w = mg * cb
                da_cols = da_cols + jnp.where(lane_k == k, jnp.sum(w, axis=1, keepdims=True), 0.0)
                da_rows = da_rows + jnp.where(sub_k == k, jnp.sum(w, axis=0, keepdims=True), 0.0)
                acc = acc + jnp.dot((decayT * cbT).astype(BF16), dyk, preferred_element_type=F32)
            dxs_ref[:, cs] = (acc * hd["dt_all"][:, cs] + dy_all[:, cs] * hd["dsk_all"][:, cs]).astype(BF16)
            ddt_x = ddt_x + _headsum(acc * xs[:, cs], ET[cs, :])
        eye_q = (row == col).astype(F32)
        da_cols = da_cols - _dot01_left(eye_q, da_rows, 3, dims=nt)
        dD_row = jnp.sum(_headsum(dy_all * xs, ET), axis=0, keepdims=True)
        sum_mg_b = sum_mg.astype(BF16)
        dB_ref[...] = (dB_acc + lax.dot_general(sum_mg_b, Cm, tn, preferred_element_type=F32)).astype(BF16)
        dC_ref[...] = (dC_acc + jnp.dot(sum_mg_b, Bm, preferred_element_type=F32)).astype(BF16)
        triu = (row <= col).astype(F32)
        ddtA = _dot01_left(triu, da_cols, 3)
        ddt = ddt_x + ddtA * A
        dpre = ddt * _sigmoid(dtp_ref[...] + bias_ref[...])
        ddtp_ref[...] = dpre
        dbias_v = jnp.sum(dpre, axis=0, keepdims=True)
        dalog_v = jnp.sum(ddtA * dt, axis=0, keepdims=True) * A

        @pl.when(ci == 0)
        def _():
            dbias_ref[...] = dbias_v
            dalog_ref[...] = dalog_v
            dD_ref[...] = dD_row
            dng_ref[...] = dng_v

        @pl.when(ci > 0)
        def _():
            dbias_ref[...] += dbias_v
            dalog_ref[...] += dalog_v
            dD_ref[...] += dD_row
            dng_ref[...] += dng_v

    vec_o = pl.BlockSpec((None, 1, Kh), lambda g, c: (g, 0, 0))
    tile = pl.BlockSpec((Q, GP), lambda g, c: (rev(c), g))
    return pl.pallas_call(
        body, grid=(G, nc),
        in_specs=_ssd_in_specs(Q, GP, N, Kh, DI, rev)
        + [pl.BlockSpec((None, None, GP, N), lambda g, c: (rev(c), g, 0, 0)), tile, tile, tile,
           pl.BlockSpec((1, GP), lambda g, c: (0, g))],
        out_specs=[tile, pl.BlockSpec((Q, N), lambda g, c: (rev(c), g)), pl.BlockSpec((Q, N), lambda g, c: (rev(c), g)),
                   pl.BlockSpec((None, Q, Kh), lambda g, c: (g, rev(c), 0)), vec_o, vec_o, vec_o,
                   tile, pl.BlockSpec((1, GP), lambda g, c: (0, g))],
        out_shape=[jax.ShapeDtypeStruct((L, DI), BF16), jax.ShapeDtypeStruct((L, G * N), BF16),
                   jax.ShapeDtypeStruct((L, G * N), BF16), jax.ShapeDtypeStruct((G, L, Kh), F32)]
        + [jax.ShapeDtypeStruct((G, 1, Kh), F32)] * 3
        + [jax.ShapeDtypeStruct(zx.shape, BF16), jax.ShapeDtypeStruct((1, DI), F32)],
        scratch_shapes=[pltpu.VMEM((GP, N), F32)], compiler_params=_cp(("parallel", "arbitrary")),
        name="ssd_bwd")(xbc, xbc, xbc, dtp_g, dtp_gT, bias_g, bias_gT, alog_g, alog_gT, dsk_g, dsk_gT, states,
                        dyn, y, zx, norm_g)


def _alibi_slope(gi, h):
    n = len(DIL_PATTERNS) * DIL_H
    return float(2.0 ** (-8.0 * (gi * DIL_H + h + 1) / n))


def _attn_masks():
    qi = lax.broadcasted_iota(jnp.int32, (DIL_BLK, DIL_BLK), 0)
    kj = lax.broadcasted_iota(jnp.int32, (DIL_BLK, DIL_BLK), 1)
    dcur = (qi - kj).astype(F32)
    return dcur, qi >= kj, dcur + float(DIL_BLK), kj >= qi


def attn_fwd(q3, kv3, gi):
    window, d = DIL_PATTERNS[gi]
    assert window // d == DIL_BLK
    HW = DIL_H * DIL_E
    M = q3.shape[1]
    nb = M // DIL_BLK
    scale = DIL_E ** -0.5
    nt = (((1,), (1,)), ((), ()))

    def body(q_ref, kp_ref, kc_ref, vp_ref, vc_ref, o_ref, lse_ref):
        n = pl.program_id(1)
        dcur, vcur, dprev, vprev0 = _attn_masks()
        dist = jnp.concatenate([dprev, dcur], axis=1)
        valid = jnp.concatenate([jnp.logical_and(vprev0, n > 0), vcur], axis=1)
        lane = lax.broadcasted_iota(jnp.int32, (DIL_BLK, 128), 1)
        lse_acc = jnp.zeros((DIL_BLK, 128), F32)
        for h in range(DIL_H):
            hs = slice(h * DIL_E, (h + 1) * DIL_E)
            sl = _alibi_slope(gi, h) * d
            kcat = jnp.concatenate([kp_ref[:, hs], kc_ref[:, hs]], axis=0)
            vcat = jnp.concatenate([vp_ref[:, hs], vc_ref[:, hs]], axis=0)
            s = lax.dot_general(q_ref[:, hs], kcat, nt, preferred_element_type=F32) * scale - sl * dist
            s = jnp.where(valid, s, -jnp.inf)
            m = jnp.max(s, axis=-1, keepdims=True)
            p = jnp.exp(s - m)
            den = jnp.sum(p, axis=-1, keepdims=True)
            o = jnp.dot(p.astype(BF16), vcat, preferred_element_type=F32) / den
            o_ref[:, hs] = o.astype(BF16)
            lse_acc = jnp.where(lane == h, m + jnp.log(den), lse_acc)
        lse_ref[...] = lse_acc

    blk = (None, DIL_BLK, HW)
    prev = lambda n: jnp.maximum(n - 1, 0)
    return pl.pallas_call(
        body, grid=(d, nb),
        in_specs=[pl.BlockSpec(blk, lambda r, n: (r, n, 0)),
                  pl.BlockSpec(blk, lambda r, n: (r, prev(n), 0)), pl.BlockSpec(blk, lambda r, n: (r, n, 0)),
                  pl.BlockSpec(blk, lambda r, n: (r, prev(n), 1)), pl.BlockSpec(blk, lambda r, n: (r, n, 1))],
        out_specs=[pl.BlockSpec(blk, lambda r, n: (r, n, 0)), pl.BlockSpec((None, DIL_BLK, 128), lambda r, n: (r, n, 0))],
        out_shape=[jax.ShapeDtypeStruct((d, M, HW), BF16), jax.ShapeDtypeStruct((d, M, 128), F32)],
        compiler_params=_cp(("parallel", "parallel")), name=f"attn_fwd_{gi}")(q3, kv3, kv3, kv3, kv3)


def attn_bwd(q3, kv3, do3, lse3, dpr3, gi):
    window, d = DIL_PATTERNS[gi]
    HW = DIL_H * DIL_E
    M = q3.shape[1]
    L = M * d
    nb = M // DIL_BLK
    scale = DIL_E ** -0.5
    nt = (((1,), (1,)), ((), ()))
    tn = (((0,), (0,)), ((), ()))

    def body(q0_ref, q1_ref, k_ref, v_ref, do0_ref, do1_ref, l0_ref, l1_ref, r0_ref, r1_ref,
             dq_ref, dk_ref, dv_ref, carry):
        n = pl.program_id(1)

        @pl.when(n == 0)
        def _():
            carry[...] = jnp.zeros(carry.shape, F32)

        dcur, vcur, dprev, vprev0 = _attn_masks()
        dist = jnp.concatenate([dcur, dprev], axis=0)
        valid = jnp.concatenate([vcur, jnp.logical_and(vprev0, n < nb - 1)], axis=0)
        B = DIL_BLK
        for h in range(DIL_H):
            hs = slice(h * DIL_E, (h + 1) * DIL_E)
            sl = _alibi_slope(gi, h) * d
            kh = k_ref[:, hs]
            vh = v_ref[:, hs]
            qcat = jnp.concatenate([q0_ref[:, hs], q1_ref[:, hs]], axis=0)
            docat = jnp.concatenate([do0_ref[:, hs], do1_ref[:, hs]], axis=0)
            lcat = jnp.concatenate([l0_ref[:, h:h + 1], l1_ref[:, h:h + 1]], axis=0)
            rcat = jnp.concatenate([r0_ref[:, h:h + 1], r1_ref[:, h:h + 1]], axis=0)
            s = lax.dot_general(qcat, kh, nt, preferred_element_type=F32) * scale - sl * dist
            p = jnp.exp(jnp.where(valid, s - lcat, -jnp.inf))
            ds = p * (lax.dot_general(docat, vh, nt, preferred_element_type=F32) - rcat)
            ds_b = (ds * scale).astype(BF16)
            dv_ref[:, hs] = lax.dot_general(p.astype(BF16), docat, tn, preferred_element_type=F32).astype(BF16)
            dk_ref[:, hs] = lax.dot_general(ds_b, qcat, tn, preferred_element_type=F32).astype(BF16)
            dqc = jnp.dot(ds_b, kh, preferred_element_type=F32)
            dq_ref[:, hs] = (carry[:, hs] + dqc[:B]).astype(BF16)
            carry[:, hs] = dqc[B:]

    blk = (None, DIL_BLK, HW)
    sblk = (None, DIL_BLK, 128)
    oblk = (DIL_BLK, HW)
    nxt = lambda n: jnp.minimum(n + 1, nb - 1)
    here = lambda c: (lambda r, n: (r, n, c))
    ahead = lambda c: (lambda r, n: (r, nxt(n), c))
    outs = pl.pallas_call(
        body, grid=(d, nb),
        in_specs=[pl.BlockSpec(blk, here(0)), pl.BlockSpec(blk, ahead(0)),
                  pl.BlockSpec(blk, here(0)), pl.BlockSpec(blk, here(1)),
                  pl.BlockSpec(blk, here(0)), pl.BlockSpec(blk, ahead(0)),
                  pl.BlockSpec(sblk, here(0)), pl.BlockSpec(sblk, ahead(0)),
                  pl.BlockSpec(sblk, here(0)), pl.BlockSpec(sblk, ahead(0))],
        out_specs=[pl.BlockSpec(oblk, lambda r, n: (n, r))] * 3,
        out_shape=[jax.ShapeDtypeStruct((M, d * HW), BF16)] * 3,
        scratch_shapes=[pltpu.VMEM(oblk, F32)], compiler_params=_cp(("parallel", "arbitrary")),
        name=f"attn_bwd_{gi}")(q3, q3, kv3, kv3, do3, do3, lse3, lse3, dpr3, dpr3)
    return [t.reshape(L, HW) for t in outs]


def _merge_weights(l_tiles, h):
    ls = [t[:, h:h + 1] for t in l_tiles]
    mx = functools.reduce(jnp.maximum, ls)
    es = [jnp.exp(l - mx) for l in ls]
    den = functools.reduce(lambda a, b: a + b, es)
    return [e / den for e in es]


def _dil_specs(tr, arrs):
    return [pl.BlockSpec((a.shape[0], tr // a.shape[0], a.shape[2]), lambda i: (0, i, 0)) for a in arrs]


def _dil_scratch(tr, arrs):
    return [pltpu.VMEM((a.shape[2] // 128, tr, 128), F32) for a in arrs if a.shape[0] > 1]


def _undilate(refs3, scrs, tr):
    out, k = [], 0
    for ref in refs3:
        d, _, W = ref.shape
        if d == 1:
            out.append(lambda c, ref=ref: ref[0, :, c * 128:(c + 1) * 128])
            continue
        scr = scrs[k]
        k += 1
        for r in range(d):
            for c in range(W // 128):
                scr.at[c][pl.ds(r, tr // d, stride=d), :] = ref[r, :, c * 128:(c + 1) * 128].astype(F32)
        out.append(lambda c, scr=scr: scr[c])
    return out


def merge_fwd(os3, lses3, z):
    HW = os3[0].shape[2]
    L = os3[0].shape[0] * os3[0].shape[1]
    tr = _tile(L, 256, 16)
    ng = len(os3)
    n_scr = len(_dil_scratch(tr, os3))

    def body(*refs):
        z_ref, out_ref = refs[2 * ng], refs[2 * ng + 1]
        scrs = refs[2 * ng + 2:]
        o_get = _undilate(refs[:ng], scrs[:n_scr], tr)
        l_tiles = [g(0) for g in _undilate(refs[ng:2 * ng], scrs[n_scr:], tr)]
        for h in range(DIL_H):
            hs = slice(h * DIL_E, (h + 1) * DIL_E)
            ws = _merge_weights(l_tiles, h)
            om = functools.reduce(lambda a, b: a + b, [w * o(h).astype(F32) for w, o in zip(ws, o_get)])
            out_ref[:, hs] = (om * _silu(z_ref[:, hs].astype(F32))).astype(BF16)

    return pl.pallas_call(
        body, grid=(L // tr,),
        in_specs=_dil_specs(tr, os3) + _dil_specs(tr, lses3) + _row_specs(tr, [HW]),
        out_specs=_row_specs(tr, [HW])[0], out_shape=jax.ShapeDtypeStruct((L, HW), BF16),
        scratch_shapes=_dil_scratch(tr, os3) + _dil_scratch(tr, lses3),
        compiler_params=_cp(("parallel",)), name="merge_fwd")(*os3, *lses3, z)


def merge_bwd(dgated, os3, lses3, z):
    HW = os3[0].shape[2]
    L = os3[0].shape[0] * os3[0].shape[1]
    tr = _tile(L, 256, 16)
    ng = len(os3)
    n_scr = len(_dil_scratch(tr, os3))

    def body(*refs):
        dg_ref = refs[0]
        z_ref = refs[1 + 2 * ng]
        outs = refs[2 + 2 * ng:2 + 2 * ng + 2 * ng + 1]
        scrs = refs[2 + 2 * ng + 2 * ng + 1:]
        do_out, dpr_out, dz_ref = outs[:ng], outs[ng:2 * ng], outs[2 * ng]
        o_get = _undilate(refs[1:1 + ng], scrs[:n_scr], tr)
        l_tiles = [g(0) for g in _undilate(refs[1 + ng:1 + 2 * ng], scrs[n_scr:2 * n_scr], tr)]
        stage = scrs[2 * n_scr:]
        do_stage, dpr_stage, k = [], [], 0
        for g in range(ng):
            if do_out[g].shape[0] == 1:
                do_stage.append(None)
                dpr_stage.append(None)
            else:
                do_stage.append(stage[2 * k])
                dpr_stage.append(stage[2 * k + 1])
                k += 1
        lane = lax.broadcasted_iota(jnp.int32, (tr, 128), 1)
        accs = [jnp.zeros((tr, 128), F32) for _ in range(ng)]
        for h in range(DIL_H):
            hs = slice(h * DIL_E, (h + 1) * DIL_E)
            ws = _merge_weights(l_tiles, h)
            ov = [o(h).astype(F32) for o in o_get]
            om = functools.reduce(lambda a, b: a + b, [w * o for w, o in zip(ws, ov)])
            zv = z_ref[:, hs].astype(F32)
            dgv = dg_ref[:, hs].astype(F32)
            dom = dgv * _silu(zv)
            dz_ref[:, hs] = (dgv * om * _dsilu(zv)).astype(BF16)
            dws = [jnp.sum(dom * o, axis=-1, keepdims=True) for o in ov]
            dwbar = functools.reduce(lambda a, b: a + b, [w * dw for w, dw in zip(ws, dws)])
            for g in range(ng):
                if do_stage[g] is None:
                    do_out[g][0, :, hs] = (ws[g] * dom).astype(BF16)
                else:
                    do_stage[g][h] = ws[g] * dom
                accs[g] = jnp.where(lane == h, ws[g] * dwbar, accs[g])
        for g in range(ng):
            d = do_out[g].shape[0]
            if d == 1:
                dpr_out[g][0] = accs[g]
                continue
            dpr_stage[g][0] = accs[g]
            for r in range(d):
                dpr_out[g][r] = dpr_stage[g].at[0][pl.ds(r, tr // d, stride=d), :]
                for c in range(HW // 128):
                    do_out[g][r, :, c * 128:(c + 1) * 128] = do_stage[g].at[c][pl.ds(r, tr // d, stride=d), :].astype(BF16)

    stage_shapes = []
    for o3 in os3:
        if o3.shape[0] > 1:
            stage_shapes += [pltpu.VMEM((HW // 128, tr, 128), F32), pltpu.VMEM((1, tr, 128), F32)]
    outs = pl.pallas_call(
        body, grid=(L // tr,),
        in_specs=_row_specs(tr, [HW]) + _dil_specs(tr, os3) + _dil_specs(tr, lses3) + _row_specs(tr, [HW]),
        out_specs=_dil_specs(tr, os3) + _dil_specs(tr, lses3) + _row_specs(tr, [HW]),
        out_shape=[jax.ShapeDtypeStruct(o.shape, BF16) for o in os3] + [jax.ShapeDtypeStruct(l.shape, F32) for l in lses3]
        + [jax.ShapeDtypeStruct((L, HW), BF16)],
        scratch_shapes=_dil_scratch(tr, os3) + _dil_scratch(tr, lses3) + stage_shapes,
        compiler_params=_cp(("parallel",)), name="merge_bwd")(dgated, *os3, *lses3, z)
    return outs[:ng], outs[ng:2 * ng], outs[2 * ng]


def ada_fwd(c8, ada_w):
    nl, D, Ws = ada_w.shape
    tn = _tile(Ws, 512)

    def body(c_ref, w_ref, o_ref):
        o_ref[...] = jnp.dot(_silu(c_ref[...]), w_ref[...], precision=lax.Precision.HIGHEST,
                             preferred_element_type=F32)

    return pl.pallas_call(
        body, grid=(nl, Ws // tn),
        in_specs=[pl.BlockSpec((N_DEV, D), lambda l, j: (0, 0)), pl.BlockSpec((None, D, tn), lambda l, j: (l, 0, j))],
        out_specs=pl.BlockSpec((None, N_DEV, tn), lambda l, j: (l, 0, j)),
        out_shape=jax.ShapeDtypeStruct((nl, N_DEV, Ws), F32), compiler_params=_cp(("parallel", "parallel")),
        name="ada_fwd")(c8, ada_w)


def ada_wgrad(c8t, dmod):
    nl, _, Ws = dmod.shape
    D = c8t.shape[0]
    tm = _tile(D, 512, 8)

    def body(c_ref, d_ref, o_ref):
        sc = _silu(c_ref[...])
        acc = sc[:, 0:1] * d_ref[0:1, :]
        for e in range(1, N_DEV):
            acc = acc + sc[:, e:e + 1] * d_ref[e:e + 1, :]
        o_ref[...] = acc

    return pl.pallas_call(
        body, grid=(nl, D // tm),
        in_specs=[pl.BlockSpec((tm, N_DEV), lambda l, i: (i, 0)), pl.BlockSpec((None, N_DEV, Ws), lambda l, i: (l, 0, 0))],
        out_specs=pl.BlockSpec((None, tm, Ws), lambda l, i: (l, i, 0)),
        out_shape=jax.ShapeDtypeStruct((nl, D, Ws), F32), compiler_params=_cp(("parallel", "parallel")),
        name="ada_wgrad")(c8t, dmod)


def adamw(w, g, m, v, name):
    R, C = w.shape
    tr = _tile(R, 256, 8)
    c1 = 1.0 - ADAM_B1 ** ADAM_STEP
    c2 = 1.0 - ADAM_B2 ** ADAM_STEP

    def body(w_ref, g_ref, m_ref, v_ref, d_ref, nm_ref, nv_ref):
        gv = g_ref[...]
        nm = ADAM_B1 * m_ref[...] + (1.0 - ADAM_B1) * gv
        nv = ADAM_B2 * v_ref[...] + (1.0 - ADAM_B2) * (gv * gv)
        nm_ref[...] = nm
        nv_ref[...] = nv
        d_ref[...] = -ADAM_LR * ((nm / c1) / (jnp.sqrt(nv / c2) + ADAM_EPS) + ADAM_WD * w_ref[...])

    return pl.pallas_call(
        body, grid=(R // tr,), in_specs=_row_specs(tr, [C] * 4), out_specs=_row_specs(tr, [C] * 3),
        out_shape=[jax.ShapeDtypeStruct((R, C), F32)] * 3, compiler_params=_cp(("parallel",)), name=name)(w, g, m, v)


def sum_leading(t, name, out_dtype=F32):
    S, R, C = t.shape
    tr = _tile(R, 256, 16)

    def body(t_ref, o_ref):
        acc = t_ref[0].astype(F32)
        for s in range(1, S):
            acc = acc + t_ref[s].astype(F32)
        o_ref[...] = acc.astype(out_dtype)

    return pl.pallas_call(
        body, grid=(R // tr,), in_specs=[pl.BlockSpec((S, tr, C), lambda i: (0, i, 0))],
        out_specs=pl.BlockSpec((tr, C), lambda i: (i, 0)), out_shape=jax.ShapeDtypeStruct((R, C), out_dtype),
        compiler_params=_cp(("parallel",)), name=name)(t)


def add_half(g, a, core, name, by_cols=False):
    S, R, C = g.shape

    def body(core_ref, g_ref, a_ref, o_ref):
        o_ref[...] = (g_ref[...].astype(F32) + a_ref[...].astype(F32)).astype(BF16)

    if by_cols:
        hc = C // 2
        tr = _tile(R, 512, 16)
        return pl.pallas_call(
            body,
            grid_spec=pltpu.PrefetchScalarGridSpec(
                num_scalar_prefetch=1, grid=(S, R // tr),
                in_specs=[pl.BlockSpec((None, tr, hc), lambda s, i, core_ref: (s, i, core_ref[0])),
                          pl.BlockSpec((None, tr, hc), lambda s, i, core_ref: (s, i, 0))],
                out_specs=pl.BlockSpec((None, tr, hc), lambda s, i, core_ref: (s, i, 0))),
            out_shape=jax.ShapeDtypeStruct((S, R, hc), BF16), compiler_params=_cp(("parallel", "parallel")),
            name=name)(core, g, a)
    h = R // 2
    tr = _tile(h, 256, 16)
    nb = h // tr

    return pl.pallas_call(
        body,
        grid_spec=pltpu.PrefetchScalarGridSpec(
            num_scalar_prefetch=1, grid=(S, nb),
            in_specs=[pl.BlockSpec((None, tr, C), lambda s, i, core_ref: (s, core_ref[0] * nb + i, 0)),
                      pl.BlockSpec((None, tr, C), lambda s, i, core_ref: (s, i, 0))],
            out_specs=pl.BlockSpec((None, tr, C), lambda s, i, core_ref: (s, i, 0))),
        out_shape=jax.ShapeDtypeStruct((S, h, C), BF16), compiler_params=_cp(("parallel", "parallel")),
        name=name)(core, g, a)


def sum_partials(own, landed, chip, name):
    _, h, C = own.shape
    tr = _tile(h, 512, 16)

    def body(chip_ref, own_ref, l_ref, o_ref):
        acc = own_ref[...].astype(F32)
        for j in range(3):
            acc = acc + l_ref[j].astype(F32)
        o_ref[...] = acc

    return pl.pallas_call(
        body,
        grid_spec=pltpu.PrefetchScalarGridSpec(
            num_scalar_prefetch=1, grid=(h // tr,),
            in_specs=[pl.BlockSpec((None, tr, C), lambda i, chip_ref: (chip_ref[0], i, 0)),
                      pl.BlockSpec((3, tr, C), lambda i, chip_ref: (0, i, 0))],
            out_specs=pl.BlockSpec((tr, C), lambda i, chip_ref: (i, 0))),
        out_shape=jax.ShapeDtypeStruct((h, C), F32), compiler_params=_cp(("parallel",)), name=name)(chip, own, landed)


def adamw_halves(w, g_mine, g_theirs, m, v, core, name, after=None):
    R, C = w.shape
    h = R // 2
    tr = _tile(h, 256, 8)
    nbh = h // tr
    c1 = 1.0 - ADAM_B1 ** ADAM_STEP
    c2 = 1.0 - ADAM_B2 ** ADAM_STEP
    extra = [] if after is None else [after]

    def body(core_ref, w_ref, gm_ref, gt_ref, m_ref, v_ref, *rest):
        g_ref, d_ref, nm_ref, nv_ref = rest[len(extra):]
        mine = (pl.program_id(0) // nbh) == core_ref[0]
        gv = jnp.where(mine, gm_ref[...], gt_ref[...])
        g_ref[...] = gv
        nm = ADAM_B1 * m_ref[...] + (1.0 - ADAM_B1) * gv
        nv = ADAM_B2 * v_ref[...] + (1.0 - ADAM_B2) * (gv * gv)
        nm_ref[...] = nm
        nv_ref[...] = nv
        d_ref[...] = -ADAM_LR * ((nm / c1) / (jnp.sqrt(nv / c2) + ADAM_EPS) + ADAM_WD * w_ref[...])

    full = pl.BlockSpec((tr, C), lambda i, core_ref: (i, 0))

    def mine_map(i, core_ref):
        return (jnp.where(i // nbh == core_ref[0], i % nbh, (1 - core_ref[0]) * (nbh - 1)), 0)

    def theirs_map(i, core_ref):
        return (jnp.where(i // nbh == core_ref[0], core_ref[0] * (nbh - 1), i % nbh), 0)

    return pl.pallas_call(
        body,
        grid_spec=pltpu.PrefetchScalarGridSpec(
            num_scalar_prefetch=1, grid=(2 * nbh,),
            in_specs=[full, pl.BlockSpec((tr, C), mine_map), pl.BlockSpec((tr, C), theirs_map), full, full]
            + [_ANY] * len(extra), out_specs=[full] * 4),
        out_shape=[jax.ShapeDtypeStruct((R, C), F32)] * 4, compiler_params=_cp(("parallel",)),
        name=name)(core, w, g_mine, g_theirs, m, v, *extra)


_ANY = pl.BlockSpec(memory_space=pl.ANY)


def _place():
    x, y, c = lax.axis_index("x"), lax.axis_index("y"), lax.axis_index("c")
    chips = [(1 - x, y), (x, 1 - y), (1 - x, 1 - y)]
    return x, y, c, chips


def allgather_small(v, name):
    R, W = v.shape

    def body(x_ref, out_ref, send_sems, recv_sems, local_sem):
        x, y, c, chips = _place()
        me, sibling = (x, y, c), (x, y, 1 - c)

        def rows(px, py, pc):
            return out_ref.at[pl.ds((4 * px + 2 * py + pc) * R, R), :]

        def copy(k, block, to, src=None):
            return pltpu.make_async_remote_copy(
                src_ref=rows(*block) if src is None else src, dst_ref=rows(*block),
                send_sem=send_sems.at[k], recv_sem=recv_sems.at[k], device_id=to, device_id_type=MESH)

        mine = pltpu.make_async_copy(x_ref, rows(*me), local_sem)
        mine.start()
        first = [copy(0, me, sibling, src=x_ref)]
        first += [copy(1 + j, me, (*chip, c), src=x_ref) for j, chip in enumerate(chips)]
        for cp in first:
            cp.start()
        passed = [copy(4 + j, (*chip, c), sibling) for j, chip in enumerate(chips)]
        for j, chip in enumerate(chips):
            copy(1 + j, (*chip, c), me).wait_recv()
            passed[j].start()
        copy(0, sibling, me).wait_recv()
        for j, chip in enumerate(chips):
            copy(4 + j, (*chip, 1 - c), me).wait_recv()
        for cp in first + passed:
            cp.wait_send()
        mine.wait()

    return pl.pallas_call(
        body, out_shape=jax.ShapeDtypeStruct((N_DEV * R, W), v.dtype),
        in_specs=[pl.BlockSpec(memory_space=pltpu.VMEM)], out_specs=pl.BlockSpec(memory_space=pltpu.VMEM),
        scratch_shapes=[pltpu.SemaphoreType.DMA((7,)), pltpu.SemaphoreType.DMA((7,)), pltpu.SemaphoreType.DMA],
        name=name)(v)


def allgather_routed(shard, name):
    R, C = shard.shape
    hc = C // 2
    ra = (R // 2) // 16 * 16

    def body(in_ref, out_ref, send_sems, recv_sems):
        x, y, c, _ = _place()
        xn, yn = (1 - x, y, c), (x, 1 - y, c)
        sibling = (x, y, 1 - c)
        p, pxn, pyn, pdg = 2 * x + y, 2 * (1 - x) + y, 2 * x + (1 - y), 2 * (1 - x) + (1 - y)
        rows_a, rows_b, rows_all = pl.ds(0, ra), pl.ds(ra, R - ra), pl.ds(0, R)

        def win(ref, rows, core):
            return ref.at[rows, pl.ds(pl.multiple_of(core * hc, 128), hc)]

        def copy(k, chip_id, rows, core, to, src=None):
            blk = win(out_ref.at[chip_id], rows, core)
            return pltpu.make_async_remote_copy(
                src_ref=blk if src is None else src, dst_ref=blk, send_sem=send_sems.at[k], recv_sem=recv_sems.at[k],
                device_id=to, device_id_type=MESH)

        own = [copy(0, p, rows_a, c, xn, src=win(in_ref, rows_a, c)), copy(1, p, rows_b, c, xn, src=win(in_ref, rows_b, c)),
               copy(2, p, rows_b, c, yn, src=win(in_ref, rows_b, c)), copy(3, p, rows_a, c, yn, src=win(in_ref, rows_a, c))]
        for cp in own:
            cp.start()
        copy(0, pxn, rows_a, c, xn).wait_recv()
        fwd_a = copy(4, pxn, rows_a, c, yn)
        fwd_a.start()
        copy(2, pyn, rows_b, c, yn).wait_recv()
        fwd_b = copy(5, pyn, rows_b, c, xn)
        fwd_b.start()
        copy(1, pxn, rows_b, c, xn).wait_recv()
        copy(3, pyn, rows_a, c, yn).wait_recv()
        passed = [copy(6, pxn, rows_all, c, sibling), copy(7, pyn, rows_all, c, sibling)]
        for cp in passed:
            cp.start()
        copy(4, pdg, rows_a, c, yn).wait_recv()
        passed.append(copy(8, pdg, rows_a, c, sibling))
        passed[-1].start()
        copy(5, pdg, rows_b, c, xn).wait_recv()
        passed.append(copy(9, pdg, rows_b, c, sibling))
        passed[-1].start()
        for k, (chip_id, rows) in enumerate([(pxn, rows_all), (pyn, rows_all), (pdg, rows_a), (pdg, rows_b)]):
            copy(6 + k, chip_id, rows, 1 - c, sibling).wait_recv()
        for cp in own + [fwd_a, fwd_b] + passed:
            cp.wait_send()

    out = pl.pallas_call(
        body, out_shape=jax.ShapeDtypeStruct((N_CHIPS, R, C), shard.dtype), in_specs=[_ANY], out_specs=_ANY,
        scratch_shapes=[pltpu.SemaphoreType.DMA((10,)), pltpu.SemaphoreType.DMA((10,))], name=name)(shard)
    chip = 2 * lax.axis_index("x") + lax.axis_index("y")
    return lax.dynamic_update_index_in_dim(out, shard, chip, 0)


_HBM = pl.BlockSpec(memory_space=pltpu.HBM)
_SEM = pl.BlockSpec(memory_space=pltpu.SEMAPHORE)
_EFFECT = pltpu.SideEffectType.DATAFLOW_SIDE_EFFECTING


def _chip_copies(kind, srcs, lands, send_sems, recv_sems):
    x, y, c, chips = _place()
    p = 2 * x + y
    cps = []
    if kind == "join":
        return [pltpu.make_async_remote_copy(
            src_ref=srcs[i], dst_ref=lands[i], send_sem=send_sems.at[3 * i], recv_sem=recv_sems.at[3 * i],
            device_id=(x, y, 1 - c), device_id_type=MESH) for i in range(len(srcs))]
    if kind == "sibling":
        for i in range(len(srcs)):
            h = srcs[i].shape[1] // 2
            cps.append(pltpu.make_async_remote_copy(
                src_ref=srcs[i].at[:, pl.ds((1 - c) * h, h), :], dst_ref=lands[i], send_sem=send_sems.at[3 * i],
                recv_sem=recv_sems.at[3 * i], device_id=(x, y, 1 - c), device_id_type=MESH))
        return cps
    for i in range(len(srcs)):
        for j, (cx, cy) in enumerate(chips):
            if kind == "gather":
                src, dst = srcs[i].at[c], lands[i].at[p, c]
            else:
                src, dst = srcs[i].at[2 * cx + cy], lands[i].at[j]
            cps.append(pltpu.make_async_remote_copy(
                src_ref=src, dst_ref=dst, send_sem=send_sems.at[3 * i + j], recv_sem=recv_sems.at[3 * i + j],
                device_id=(cx, cy, c), device_id_type=MESH))
    return cps


def split_start(kind, srcs, land_shapes, after, name, land_dtype=BF16):
    n = len(srcs)

    def body(*refs):
        src_refs, land_refs = refs[:n], refs[n:2 * n]
        send_sems, recv_sems = refs[2 * n + 1], refs[2 * n + 2]
        token = refs[-1]
        for cp in _chip_copies(kind, src_refs, land_refs, send_sems, recv_sems):
            cp.start()
        token[...] = jnp.zeros_like(token)

    lands = [pltpu.with_memory_space_constraint(lax.empty(s, land_dtype), pltpu.HBM) for s in land_shapes]
    outs = pl.pallas_call(
        body, name=name,
        out_shape=(pltpu.SemaphoreType.DMA((3 * n,)), pltpu.SemaphoreType.DMA((3 * n,)),
                   *[pltpu.HBM(s.shape, s.dtype) for s in srcs], *[pltpu.HBM(s, land_dtype) for s in land_shapes],
                   jax.ShapeDtypeStruct((8, 128), F32)),
        in_specs=[_HBM] * (2 * n) + [_ANY],
        out_specs=(_SEM, _SEM, *([_HBM] * (2 * n)), pl.BlockSpec(memory_space=pltpu.VMEM)),
        input_output_aliases={i: 2 + i for i in range(2 * n)},
        compiler_params=pltpu.CompilerParams(has_side_effects=_EFFECT),
    )(*[pltpu.with_memory_space_constraint(s, pltpu.HBM) for s in srcs], *lands, after)
    return outs[0], outs[1], outs[2:2 + n], outs[2 + n:2 + 2 * n], outs[-1]


def split_wait(kind, send_sems, recv_sems, srcs, lands, after, name):
    n = len(srcs)

    def body(*refs):
        src_refs, land_refs = refs[:n], refs[n:2 * n]
        ssem, rsem = refs[2 * n], refs[2 * n + 1]
        for cp in _chip_copies(kind, src_refs, land_refs, ssem, rsem):
            cp.wait_send()
            cp.wait_recv()

    outs = pl.pallas_call(
        body, name=name,
        out_shape=[pltpu.HBM(s.shape, s.dtype) for s in srcs] + [pltpu.HBM(s.shape, s.dtype) for s in lands],
        in_specs=[_HBM] * (2 * n) + [_SEM, _SEM, _ANY], out_specs=[_HBM] * (2 * n),
        input_output_aliases={i: i for i in range(2 * n)},
        compiler_params=pltpu.CompilerParams(has_side_effects=_EFFECT),
    )(*srcs, *lands, send_sems, recv_sems, after)
    return outs[:n], outs[n:]


def pass_to_sibling(lands):
    n = len(lands)

    def body(*refs):
        ins, outs = refs[:n], refs[n:2 * n]
        send_sems, recv_sems = refs[2 * n:]
        x, y, c, chips = _place()
        cps = []
        for i in range(n):
            for j, (cx, cy) in enumerate(chips):
                blk = outs[i].at[2 * cx + cy, c]
                cps.append(pltpu.make_async_remote_copy(
                    src_ref=ins[i].at[2 * cx + cy, c], dst_ref=blk, send_sem=send_sems.at[3 * i + j],
                    recv_sem=recv_sems.at[3 * i + j], device_id=(x, y, 1 - c), device_id_type=MESH))
        for cp in cps:
            cp.start()
        for cp in cps:
            cp.wait()

    return pl.pallas_call(
        body, out_shape=[jax.ShapeDtypeStruct(t.shape, t.dtype) for t in lands], in_specs=[_ANY] * n,
        out_specs=[_ANY] * n, input_output_aliases={i: i for i in range(n)},
        scratch_shapes=[pltpu.SemaphoreType.DMA((3 * n,)), pltpu.SemaphoreType.DMA((3 * n,))],
        name="ag_pass_to_sibling")(*lands)


def _all8_copies(src, land, send_sems, recv_sems):
    x, y, c, _ = _place()
    me = 4 * x + 2 * y + c
    cps = []
    for k, (fx, fy, fc) in enumerate([(a, b, d) for a in (0, 1) for b in (0, 1) for d in (0, 1)][1:]):
        peer = (1 - x if fx else x, 1 - y if fy else y, 1 - c if fc else c)
        cps.append(pltpu.make_async_remote_copy(
            src_ref=src, dst_ref=land.at[me], send_sem=send_sems.at[k], recv_sem=recv_sems.at[k],
            device_id=peer, device_id_type=MESH))
    return cps


def gather8_start(v, after, name):
    R, W = v.shape

    def body(v_ref, land_ref, aft_ref, send_sems, recv_sems, v_thru, land_thru, token):
        for cp in _all8_copies(v_ref, land_ref, send_sems, recv_sems):
            cp.start()
        token[...] = jnp.zeros_like(token)

    land = pltpu.with_memory_space_constraint(lax.empty((N_DEV, R, W), v.dtype), pltpu.HBM)
    outs = pl.pallas_call(
        body, name=name,
        out_shape=(pltpu.SemaphoreType.DMA((N_DEV - 1,)), pltpu.SemaphoreType.DMA((N_DEV - 1,)),
                   pltpu.HBM(v.shape, v.dtype), pltpu.HBM((N_DEV, R, W), v.dtype), jax.ShapeDtypeStruct((8, 128), F32)),
        in_specs=[_HBM, _HBM, _ANY],
        out_specs=(_SEM, _SEM, _HBM, _HBM, pl.BlockSpec(memory_space=pltpu.VMEM)),
        input_output_aliases={0: 2, 1: 3},
        compiler_params=pltpu.CompilerParams(has_side_effects=_EFFECT),
    )(pltpu.with_memory_space_constraint(v, pltpu.HBM), land, after)
    return outs


def gather8_wait(send_sems, recv_sems, v, land, after, name):
    def body(v_ref, land_ref, ssem, rsem, aft_ref, v_dead, land_out):
        for cp in _all8_copies(v_ref, land_ref, ssem, rsem):
            cp.wait_send()
            cp.wait_recv()

    return pl.pallas_call(
        body, name=name, out_shape=[pltpu.HBM(v.shape, v.dtype), pltpu.HBM(land.shape, land.dtype)],
        in_specs=[_HBM, _HBM, _SEM, _SEM, _ANY], out_specs=[_HBM, _HBM], input_output_aliases={0: 0, 1: 1},
        compiler_params=pltpu.CompilerParams(has_side_effects=_EFFECT),
    )(v, land, send_sems, recv_sems, after)[1]


def _pass_copies(bufs, send_sems, recv_sems):
    x, y, c, chips = _place()
    cps = []
    for i in range(len(bufs)):
        for j, (cx, cy) in enumerate(chips):
            blk = bufs[i].at[2 * cx + cy, c]
            cps.append(pltpu.make_async_remote_copy(
                src_ref=blk, dst_ref=blk, send_sem=send_sems.at[3 * i + j], recv_sem=recv_sems.at[3 * i + j],
                device_id=(x, y, 1 - c), device_id_type=MESH))
    return cps


def pass_start(bufs, after, name):
    n = len(bufs)

    def body(*refs):
        send_sems, recv_sems = refs[n + 1], refs[n + 2]
        for cp in _pass_copies(refs[:n], send_sems, recv_sems):
            cp.start()
        refs[-1][...] = jnp.zeros_like(refs[-1])

    outs = pl.pallas_call(
        body, name=name,
        out_shape=(pltpu.SemaphoreType.DMA((3 * n,)), pltpu.SemaphoreType.DMA((3 * n,)),
                   *[pltpu.HBM(b.shape, b.dtype) for b in bufs], jax.ShapeDtypeStruct((8, 128), F32)),
        in_specs=[_HBM] * n + [_ANY],
        out_specs=(_SEM, _SEM, *([_HBM] * n), pl.BlockSpec(memory_space=pltpu.VMEM)),
        input_output_aliases={i: 2 + i for i in range(n)},
        compiler_params=pltpu.CompilerParams(has_side_effects=_EFFECT),
    )(*[pltpu.with_memory_space_constraint(b, pltpu.HBM) for b in bufs], after)
    return outs[0], outs[1], outs[2:2 + n], outs[-1]


def pass_wait(send_sems, recv_sems, bufs, after, name):
    n = len(bufs)

    def body(*refs):
        for cp in _pass_copies(refs[:n], refs[n], refs[n + 1]):
            cp.wait_send()
            cp.wait_recv()

    return pl.pallas_call(
        body, name=name, out_shape=[pltpu.HBM(b.shape, b.dtype) for b in bufs],
        in_specs=[_HBM] * n + [_SEM, _SEM, _ANY], out_specs=[_HBM] * n,
        input_output_aliases={i: i for i in range(n)},
        compiler_params=pltpu.CompilerParams(has_side_effects=_EFFECT),
    )(*bufs, send_sems, recv_sems, after)


def exchange_halves_to_sibling(gs, name, by_cols=False):
    n = len(gs)

    def body(*refs):
        ins, outs = refs[:n], refs[n:2 * n]
        send_sems, recv_sems = refs[2 * n:]
        x, y, c, _ = _place()
        cps = []
        for i in range(n):
            if by_cols:
                hc = ins[i].shape[2] // 2
                src = ins[i].at[:, :, pl.ds(pl.multiple_of((1 - c) * hc, 128), hc)]
            else:
                h = ins[i].shape[1] // 2
                src = ins[i].at[:, pl.ds((1 - c) * h, h), :]
            cps.append(pltpu.make_async_remote_copy(
                src_ref=src, dst_ref=outs[i],
                send_sem=send_sems.at[i], recv_sem=recv_sems.at[i], device_id=(x, y, 1 - c), device_id_type=MESH))
        for cp in cps:
            cp.start()
        for cp in cps:
            cp.wait()

    halve = (lambda s: (s[0], s[1], s[2] // 2)) if by_cols else (lambda s: (s[0], s[1] // 2, s[2]))
    return pl.pallas_call(
        body, out_shape=[jax.ShapeDtypeStruct(halve(g.shape), g.dtype) for g in gs],
        in_specs=[_ANY] * n, out_specs=[_ANY] * n,
        scratch_shapes=[pltpu.SemaphoreType.DMA((n,)), pltpu.SemaphoreType.DMA((n,))],
        name=name)(*gs)


def _pack(parts, row_mult=8):
    flat = jnp.concatenate([p.reshape(-1).astype(F32) for p in parts])
    unit = row_mult * 128
    n = -(-flat.shape[0] // unit) * unit
    return jnp.pad(flat, (0, n - flat.shape[0])).reshape(n // 128, 128)


def _unpack(flat, shapes):
    out, off = [], 0
    for s in shapes:
        n = int(np.prod(s))
        out.append(flat[off:off + n].reshape(s))
        off += n
    return out


def _gather_packed(parts, name):
    packed = _pack(parts)
    g = allgather_small(packed, name).reshape(N_DEV, -1)
    return _unpack_rows(g, [p.shape for p in parts])


def _unpack_rows(g, shapes):
    out, off = [], 0
    for s in shapes:
        n = int(np.prod(s))
        out.append(g[:, off:off + n].reshape((g.shape[0],) + tuple(s)))
        off += n
    return out


def _by_chip(t, axis):
    return jnp.concatenate([t[2 * p] for p in range(N_CHIPS)], axis=axis)


def kernel(x, c, ada_w, ada_b, ln_g, ln_b, a_in_w, a_conv_w, a_conv_b, a_dt_bias, a_A_log, a_D, a_norm_g, a_out_w, kv_w, b_in_w, b_out_w, loss_target, m_ada_w, m_ada_b, m_ln_g, m_ln_b, m_a_in_w, m_a_conv_w, m_a_conv_b, m_a_dt_bias, m_a_A_log, m_a_D, m_a_norm_g, m_a_out_w, m_kv_w, m_b_in_w, m_b_out_w, v_ada_w, v_ada_b, v_ln_g, v_ln_b, v_a_in_w, v_a_conv_w, v_a_conv_b, v_a_dt_bias, v_a_A_log, v_a_D, v_a_norm_g, v_a_out_w, v_kv_w, v_b_in_w, v_b_out_w):
    ax, ay, ac = lax.axis_index("x"), lax.axis_index("y"), lax.axis_index("c")
    chip = 2 * ax + ay
    dev = 4 * ax + 2 * ay + ac
    xin = x[0]
    tgt = loss_target[0]
    L, D = xin.shape
    G, P = SSD_G, SSD_P
    H = a_dt_bias.shape[1]
    Kh = H // G
    DI = H * P
    CONVD = a_conv_b.shape[1] * N_CHIPS
    HW = DIL_H * DIL_E
    Ws = ada_w.shape[2]

    w_in_g = allgather_routed(jnp.transpose(a_in_w[0]).astype(BF16), "allgather_w_in")
    later = [a_out_w[0].astype(BF16), kv_w.astype(BF16), b_in_w[0].astype(BF16), b_out_w[0].astype(BF16)]
    later_split = [s.reshape(2, s.shape[0] // 2, s.shape[1]) for s in later]
    ag_ssem, ag_rsem, ag_srcs, ag_lands, ag_token = split_start(
        "gather", later_split, [(N_CHIPS,) + s.shape for s in later_split], w_in_g, "ag_later_start")
    w_in_t = w_in_g.reshape(-1, D)
    w_dt_t = jnp.pad(w_in_t[DI + CONVD:], ((0, 128 - H), (0, 0)))

    c8, cw8, cb8, ng8 = _gather_packed([c[0], a_conv_w[0], a_conv_b[0], a_norm_g[0]], "allgather_small_params")
    conv_w = _by_chip(cw8, 1)
    conv_b = _by_chip(cb8, 0).reshape(1, CONVD)
    norm_g = _by_chip(ng8, 0).reshape(1, DI)

    mod_s = ada_fwd(c8, ada_w)
    (mod8,) = _gather_packed([mod_s], "allgather_small_mod")
    mods = _by_chip(mod8, 2)
    mod = lax.dynamic_index_in_dim(mods, dev, axis=1, keepdims=False) + ada_b
    shift = [mod[l:l + 1, :D] for l in range(DEPTH)]
    scale = [mod[l:l + 1, D:2 * D] for l in range(DEPTH)]
    gate = [mod[l:l + 1, 2 * D:] for l in range(DEPTH)]
    lg = [ln_g[l:l + 1] for l in range(DEPTH)]
    lb = [ln_b[l:l + 1] for l in range(DEPTH)]

    h0 = modulate(xin, scale[0] + ag_token[0:1, 0:1], shift[0], "modulate0")
    zx = mm_nt(h0, w_in_t, BF16, "mm_in_zx", kw_rows=DI + CONVD)
    dtp = mm_nt(h0, w_dt_t, F32, "mm_in_dt")
    xbc = conv_fwd(zx, DI, conv_w, conv_b)
    dtp_g = jnp.transpose(dtp[:, :H].reshape(L, G, Kh), (1, 0, 2))
    dtp_gT = jnp.transpose(dtp_g, (0, 2, 1))
    vecs = [a_dt_bias.reshape(G, 1, Kh), a_dt_bias.reshape(G, Kh, 1), a_A_log.reshape(G, 1, Kh),
            a_A_log.reshape(G, Kh, 1), a_D.reshape(G, 1, Kh), a_D.reshape(G, Kh, 1)]
    y_ssd, states, yn = ssd_fwd(xbc, dtp_g, dtp_gT, *vecs, zx, norm_g, DI)
    later_split, ag_lands = split_wait("gather", ag_ssem, ag_rsem, ag_srcs, ag_lands, yn, "ag_later_wait")
    (land_out,) = pass_to_sibling(ag_lands[:1])
    ps_ssem, ps_rsem, lands_b, ps_token = pass_start(ag_lands[1:], land_out, "ag_pass_start")

    def place_own(o, s, full):
        return lax.dynamic_update_index_in_dim(o, s, chip, 0).reshape((N_CHIPS,) + full.shape)

    w_out_g = place_own(land_out, later_split[0], later[0])
    ymix0 = mm_nn(yn, w_out_g.reshape(-1, D), F32, "mm_out_a", after=ps_token)
    x1, x1b, h1 = ln_mid(xin, ymix0, gate[0], lg[0], lb[0], scale[1], shift[1])
    lands_b = pass_wait(ps_ssem, ps_rsem, lands_b, x1b, "ag_pass_wait")
    w_kv_g, w_bin_g, w_bout_g = [place_own(o, s, full) for o, s, full in zip(lands_b, later_split[1:], later[1:])]

    n_grp = len(DIL_PATTERNS)
    cb = HW // 512
    assert w_bin_g.shape[2] == HW
    kv3 = [mm_cols_dilated(x1b, w_kv_g, [g * cb + t for t in range(cb)] + [(n_grp + g) * cb + t for t in range(cb)],
                           DIL_PATTERNS[g][1], f"mm_kv_{g}") for g in range(n_grp)]
    q3 = [mm_cols_dilated(h1, w_bin_g, [g], DIL_PATTERNS[g][1], f"mm_q_{g}", tn=HW) for g in range(n_grp)]
    z_b = mm_nn(h1, w_bin_g[n_grp], BF16, "mm_z_b")
    os_, lses = [], []
    for gi in range(len(DIL_PATTERNS)):
        o, lse = attn_fwd(q3[gi], kv3[gi], gi)
        os_.append(o)
        lses.append(lse)
    om = merge_fwd(os_, lses, z_b)
    ymix1 = mm_nn(om, w_bout_g, F32, "mm_out_b", stack="col")
    dres2, dy2, dg1, db1, dgate1, sq = ln_final_fwd_bwd(x1, ymix1, gate[1], lg[1], lb[1], tgt)
    loss_part = 0.5 * jnp.sum(sq) / D

    g_bout = mm_tn(om, dy2, BF16, "mm_gw_out_b", stack="col")
    dgated = mm_nt(dy2, w_bout_g, BF16, "mm_gx_out_b", stack="col")
    dos, dprs, dz_b = merge_bwd(dgated, os_, lses, z_b)
    dqs, dks, dvs = [], [], []
    for gi in range(len(DIL_PATTERNS)):
        dq, dk, dv = attn_bwd(q3[gi], kv3[gi], dos[gi], lses[gi], dprs[gi], gi)
        dqs.append(dq)
        dks.append(dk)
        dvs.append(dv)
    dqz = jnp.concatenate(dqs + [dz_b], axis=1)
    dkv = jnp.concatenate(dks + dvs, axis=1)
    g_bin = mm_tn(h1, dqz, BF16, "mm_gw_in_b", stack="col")
    dh1 = mm_nt(dqz, w_bin_g, BF16, "mm_gx_in_b", stack="col")
    g_kv = mm_tn(x1b, dkv, BF16, "mm_gw_kv", stack="col")

    core = ac.astype(jnp.int32).reshape(1)
    chip_i = chip.astype(jnp.int32).reshape(1)

    def begin_exchange(gs, tag):
        shapes = [(g.shape[0], g.shape[1] // 2, g.shape[2]) for g in gs]
        return split_start("sibling", gs, shapes, gs[0], "rs_x%s_start" % tag)

    def begin_scatter(gs, nms, tag, exchange=None, after=None, by_cols=False):
        if exchange is None:
            sib = exchange_halves_to_sibling(gs, "rs_sibling_exchange_" + tag, by_cols=by_cols)
        else:
            gs, sib = split_wait("sibling", exchange[0], exchange[1], exchange[2], exchange[3], after,
                                 "rs_x%s_wait" % tag)
        parts = [add_half(g, a, core, "rs_add_" + nm, by_cols=by_cols) for g, a, nm in zip(gs, sib, nms)]
        return split_start("scatter", parts, [(3,) + t.shape[1:] for t in parts], parts[0], "rs_%s_start" % tag)

    def sum_scattered(handles, after, tag):
        nms, owns, landed = [], [], []
        for k, (handle, hn) in enumerate(handles):
            parts, lands = split_wait("scatter", handle[0], handle[1], handle[2], handle[3], after,
                                      "rs_%s%d_wait" % (tag, k))
            nms += hn
            owns += list(parts)
            landed += list(lands)
        return nms, [sum_partials(own, t, chip_i, "rs_sum_" + nm) for own, t, nm in zip(owns, landed, nms)]

    def begin_join(halves, tag):
        return split_start("join", halves, [t.shape for t in halves], halves[0], "rs_j%s_start" % tag, land_dtype=F32)

    def end_join(handle, after, tag):
        return split_wait("join", handle[0], handle[1], handle[2], handle[3], after, "rs_j%s_wait" % tag)

    names_b = ["kv", "in_b", "out_b"]
    ex_b = begin_exchange([g_kv, g_bin, g_bout], "b")
    dx1_kv = mm_nt(dkv, w_kv_g, BF16, "mm_gx_kv", stack="col", after=ex_b[4])
    rs_b = begin_scatter(None, names_b, "b", exchange=ex_b, after=dx1_kv)

    dres1, dy1, dg0, db0, dgate0, dscale1, dshift1 = mod_ln_bwd(
        dres2, dh1, dx1_kv, x1, scale[1], xin, ymix0, gate[0] + rs_b[4][0:1, 0:1], lg[0])
    g_out = mm_tn(yn, dy1, BF16, "mm_gw_out_a", stack="row")
    ex_a1 = begin_exchange([g_out], "a1")
    dyn = mm_nt(dy1, w_out_g, BF16, "mm_gx_out_a", stack="row", after=ex_a1[4])
    rs_a1 = begin_scatter(None, ["out_a"], "a1", exchange=ex_a1, after=dyn)
    dxs, dB, dC, ddtp_g, dbias_g, dalog_g, dD_g, dz_a, dnorm_g = ssd_bwd(
        xbc, dtp_g, dtp_gT, *vecs, states, dyn, y_ssd, zx, norm_g + rs_a1[4][0:1, 0:1], DI)
    dzx, dws, dbs, lo = dz_a, [], [], 0
    for tag, gpart in (("xs", dxs), ("b", dB), ("c", dC)):
        hi = lo + gpart.shape[1]
        dzx, dw_p, db_p = conv_bwd(zx, DI + lo, conv_w[:, lo:hi], conv_b[:, lo:hi], gpart, dzx, "conv_bwd_" + tag)
        dws.append(dw_p)
        dbs.append(db_p)
        lo = hi
    dconv_w = jnp.concatenate(dws, axis=1)
    dconv_b = jnp.concatenate(dbs, axis=1)
    ddtp = jnp.pad(jnp.transpose(ddtp_g, (1, 0, 2)).reshape(L, H), ((0, 0), (0, 128 - H)))
    g_inT = mm_tn(dzx, h0, BF16, "mm_gw_in_zx", m_rows=DI + CONVD + H)
    g_dtT = mm_tn(ddtp, h0, BF16, "mm_gw_in_dt")
    g_inT = lax.dynamic_update_slice(g_inT, g_dtT[:H], (DI + CONVD, 0))
    rs_a2 = begin_scatter([g_inT.reshape(N_CHIPS, -1, D)], ["in_a"], "a2", by_cols=True)
    dh0 = mm_nn(dzx, w_in_t, BF16, "mm_gx_in_zx", after=rs_a2[4])
    dh0_dt = mm_nn(ddtp, w_dt_t, F32, "mm_gx_in_dt")
    grad_x, dscale0, dshift0 = mod_bwd(dres1, dh0, dh0_dt, xin, scale[0] + rs_a2[4][0:1, 0:1], "mod_bwd0")
    nms_b, halves_b = sum_scattered([(rs_b, names_b)], grad_x, "b")
    join_b = begin_join(halves_b, "b")
    nms_a, halves_a = sum_scattered([(rs_a1, ["out_a"]), (rs_a2, ["in_a"])], join_b[4], "a")
    g_halves = dict(zip(nms_b, zip(*end_join(join_b, halves_a[0], "b"))))
    join_a = begin_join(halves_a, "a")

    def step_halves(w, m, v, nm, after=None):
        shp = w.shape
        mine, theirs_ = g_halves[nm]
        outs4 = adamw_halves(w.reshape(-1, shp[-1]), mine, theirs_, m.reshape(-1, shp[-1]), v.reshape(-1, shp[-1]),
                             core, "adamw_" + nm, after=after)
        return tuple(t.reshape(shp) for t in outs4)

    big = {
        "kv_w": step_halves(kv_w, m_kv_w, v_kv_w, "kv", after=join_a[4]),
        "b_in_w": step_halves(b_in_w, m_b_in_w, v_b_in_w, "in_b"),
        "b_out_w": step_halves(b_out_w, m_b_out_w, v_b_out_w, "out_b"),
    }
    g_halves.update(dict(zip(nms_a, zip(*end_join(join_a, big["kv_w"][1], "a")))))
    g_halves["in_a"] = tuple(jnp.transpose(t) for t in g_halves["in_a"])

    dmod = jnp.concatenate([jnp.concatenate([dshift0, dscale0, dgate0], axis=1),
                            jnp.concatenate([dshift1, dscale1, dgate1], axis=1)], axis=0)
    small_parts = [jnp.concatenate([dg0, dg1], axis=0), jnp.concatenate([db0, db1], axis=0),
                   dbias_g.reshape(1, H), dalog_g.reshape(1, H), dD_g.reshape(1, H),
                   dconv_w, dconv_b, dnorm_g, loss_part.reshape(1, 1)]
    small_shapes = [p.shape for p in small_parts]
    packed = jnp.concatenate([_pack([dmod]), _pack(small_parts)], axis=0)
    n_mod_rows = _pack([dmod]).shape[0]
    sg_ssem, sg_rsem, sg_src, sg_land, sg_token = gather8_start(packed, g_halves["in_a"][1], "small_grads_start")
    big["a_in_w"] = step_halves(a_in_w, m_a_in_w, v_a_in_w, "in_a", after=sg_token)
    big["a_out_w"] = step_halves(a_out_w, m_a_out_w, v_a_out_w, "out_a")
    sg_land = gather8_wait(sg_ssem, sg_rsem, sg_src, sg_land, big["a_in_w"][1], "small_grads_wait")
    gathered = lax.dynamic_update_index_in_dim(sg_land, packed, dev, 0)
    dmod8 = gathered[:, :n_mod_rows].reshape(N_DEV, -1)[:, :2 * 3 * D].reshape(N_DEV, DEPTH, 3 * D)
    summed = sum_leading(gathered, "sum_small")
    g_ada_b = summed[:n_mod_rows].reshape(-1)[:2 * 3 * D].reshape(DEPTH, 3 * D)
    (g_ln_g, g_ln_b, g_dt_bias, g_a_log, g_dsk, g_conv_w, g_conv_b, g_norm_g, loss_all) = _unpack(
        summed[n_mod_rows:].reshape(-1), small_shapes)
    loss = loss_all.reshape(())
    Cs = CONVD // N_CHIPS
    g_conv_w_s = lax.dynamic_slice_in_dim(g_conv_w, chip * Cs, Cs, axis=1)
    g_conv_b_s = lax.dynamic_slice_in_dim(g_conv_b, chip * Cs, Cs, axis=1)
    g_norm_g_s = lax.dynamic_slice_in_dim(g_norm_g, chip * (DI // N_CHIPS), DI // N_CHIPS, axis=1)
    dmod_s = jnp.transpose(lax.dynamic_slice_in_dim(dmod8, chip * Ws, Ws, axis=2), (1, 0, 2))

    def step2d(w, g, m, v, nm):
        shp = w.shape
        d_, m_, v_ = adamw(w.reshape(-1, shp[-1]), g.reshape(-1, shp[-1]), m.reshape(-1, shp[-1]),
                           v.reshape(-1, shp[-1]), "adamw_" + nm)
        return g.reshape(shp), d_.reshape(shp), m_.reshape(shp), v_.reshape(shp)

    big["ada_w"] = step2d(ada_w, ada_wgrad(jnp.transpose(c8), dmod_s), m_ada_w, v_ada_w, "ada_w")
    small_names = ["ada_b", "ln_g", "ln_b", "a_conv_w", "a_conv_b", "a_dt_bias", "a_A_log", "a_D", "a_norm_g"]
    small_w = [ada_b, ln_g, ln_b, a_conv_w, a_conv_b, a_dt_bias, a_A_log, a_D, a_norm_g]
    small_m = [m_ada_b, m_ln_g, m_ln_b, m_a_conv_w, m_a_conv_b, m_a_dt_bias, m_a_A_log, m_a_D, m_a_norm_g]
    small_v = [v_ada_b, v_ln_g, v_ln_b, v_a_conv_w, v_a_conv_b, v_a_dt_bias, v_a_A_log, v_a_D, v_a_norm_g]
    small_g = [g_ada_b, g_ln_g, g_ln_b, g_conv_w_s, g_conv_b_s, g_dt_bias, g_a_log, g_dsk, g_norm_g_s]
    shapes = [w.shape for w in small_w]
    small_g = [g.reshape(s) for g, s in zip(small_g, shapes)]
    d_p, m_p, v_p = adamw(_pack(small_w), _pack(small_g), _pack(small_m), _pack(small_v), "adamw_small")
    small = {}
    for nm, g, d_, m_, v_ in zip(small_names, small_g, _unpack(d_p.reshape(-1), shapes), _unpack(m_p.reshape(-1), shapes),
                                 _unpack(v_p.reshape(-1), shapes)):
        small[nm] = (g, d_, m_, v_)
    allw = {**big, **small}
    order = ["ada_w", "ada_b", "ln_g", "ln_b", "a_in_w", "a_conv_w", "a_conv_b", "a_dt_bias", "a_A_log", "a_D",
             "a_norm_g", "a_out_w", "kv_w", "b_in_w", "b_out_w"]
    outs = [loss, grad_x.reshape(x.shape)]
    for k in range(4):
        outs += [allw[n][k] for n in order]
    return tuple(outs)
```
